```python
import jax, jax.numpy as jnp
from jax import lax
import numpy as np

D_MODEL = 1024
BATCH = 8
SEQ = 4096
DEPTH = 2

HEAD_DIM = 64
N_Q_HEADS = 8
N_KV_HEADS = 2
Q_PER_KV = N_Q_HEADS // N_KV_HEADS
ATTN_WIDTH = N_Q_HEADS * HEAD_DIM
KV_WIDTH = N_KV_HEADS * HEAD_DIM
WINDOW = 128
ATTN_BLOCK = 128
ROPE_DIM = HEAD_DIM // 4
ROPE_THETA = 500000.0
N_SG_GROUPS = 8
SG_GROUP_DIM = 64
SG_WIDTH = N_SG_GROUPS * SG_GROUP_DIM
SG_CHUNK = 128
EVEN_GATE_WIDTH = ATTN_WIDTH + SG_WIDTH
EVEN_SPLITS = tuple(int(s) for s in np.cumsum([ATTN_WIDTH, KV_WIDTH, KV_WIDTH, SG_WIDTH, SG_WIDTH]))
EVEN_IN_WIDTH = EVEN_SPLITS[-1] + EVEN_GATE_WIDTH
RNN_WIDTH = D_MODEL
RNN_HEADS = 8
RNN_HEAD_DIM = RNN_WIDTH // RNN_HEADS
CONV_WIDTH = 4
CONV_PAD = (2, 1)
RG_LRU_C = 8.0
ODD_IN_WIDTH = 2 * RNN_WIDTH
N_EVEN = (DEPTH + 1) // 2
N_ODD = DEPTH // 2
DEEPNORM_ALPHA = (2 * DEPTH) ** 0.25
DEEPNORM_BETA = (8 * DEPTH) ** -0.25
LN_EPS = 1e-5
NEG_INF = -1e30

kernel_name = "hybrid_swa_gmlp_rglru_deepnorm_encoder"


def layer_norm(x, g, b):
    xf = x.astype(jnp.float32)
    mu = xf.mean(-1, keepdims=True)
    var = jnp.square(xf - mu).mean(-1, keepdims=True)
    y = (xf - mu) * lax.rsqrt(var + LN_EPS)
    return (y * g.astype(jnp.float32) + b.astype(jnp.float32)).astype(x.dtype)


def partial_rotary(t, positions):
    half = ROPE_DIM // 2
    inv_freq = jnp.power(jnp.float32(ROPE_THETA), -jnp.arange(half, dtype=jnp.float32) / half)
    ang = positions.astype(jnp.float32)[:, :, None, None] * inv_freq
    cos, sin = jnp.cos(ang), jnp.sin(ang)
    tr = t[..., :ROPE_DIM].astype(jnp.float32)
    t1, t2 = tr[..., :half], tr[..., half:]
    rot = jnp.concatenate([t1 * cos - t2 * sin, t2 * cos + t1 * sin], axis=-1)
    return jnp.concatenate([rot.astype(t.dtype), t[..., ROPE_DIM:]], axis=-1)


def windowed_gqa_sink(q, k, v, sink):
    B, S = q.shape[0], q.shape[1]
    nb = S // ATTN_BLOCK
    qb = q.reshape(B, nb, ATTN_BLOCK, N_KV_HEADS, Q_PER_KV, HEAD_DIM)

    def band(t):
        tp = jnp.pad(t, ((0, 0), (ATTN_BLOCK, ATTN_BLOCK), (0, 0), (0, 0)))
        parts = [tp[:, o * ATTN_BLOCK:o * ATTN_BLOCK + S].reshape(B, nb, ATTN_BLOCK, N_KV_HEADS, HEAD_DIM)
                 for o in range(3)]
        return jnp.concatenate(parts, axis=2)

    kb, vb = band(k), band(v)
    s = jnp.einsum('bnqhgd,bnkhd->bnhgqk', qb, kb).astype(jnp.float32) * (HEAD_DIM ** -0.5)
    qi = jnp.arange(ATTN_BLOCK)[:, None]
    kj = jnp.arange(3 * ATTN_BLOCK)[None, :]
    blk = jnp.arange(nb)[:, None, None]
    k_abs = blk * ATTN_BLOCK - ATTN_BLOCK + kj
    valid = (jnp.abs(kj - ATTN_BLOCK - qi) <= WINDOW) & (k_abs >= 0) & (k_abs < S)
    s = jnp.where(valid[None, :, None, None], s, NEG_INF)
    sink_l = sink.astype(jnp.float32).reshape(N_KV_HEADS, Q_PER_KV)[None, None, :, :, None, None]
    m = jnp.maximum(s.max(-1, keepdims=True), sink_l)
    p = jnp.exp(s - m)
    denom = p.sum(-1, keepdims=True) + jnp.exp(sink_l - m)
    o = jnp.einsum('bnhgqk,bnkhd->bnqhgd', (p / denom).astype(v.dtype), vb)
    return o.reshape(B, S, ATTN_WIDTH)


def chunked_spatial_gating(u, v, ln_g, ln_b, w_s, b_s):
    B, S = u.shape[0], u.shape[1]
    nc = S // SG_CHUNK
    vg = v.reshape(B, S, N_SG_GROUPS, SG_GROUP_DIM)
    vg = layer_norm(vg, ln_g.reshape(N_SG_GROUPS, SG_GROUP_DIM), ln_b.reshape(N_SG_GROUPS, SG_GROUP_DIM))
    vc = vg.reshape(B, nc, SG_CHUNK, N_SG_GROUPS, SG_GROUP_DIM)
    sv = jnp.einsum('gpq,bcqgd->bcpgd', w_s, vc) + b_s.T[None, None, :, :, None]
    return u * sv.reshape(B, S, SG_WIDTH)


def centred_depthwise_conv(x, w, b):
    y = lax.conv_general_dilated(x, w[:, None, :], window_strides=(1,), padding=[CONV_PAD],
                                 dimension_numbers=('NWC', 'WIO', 'NWC'),
                                 feature_group_count=x.shape[-1])
    return y + b


def rg_lru(x, w_a, b_a, w_x, b_x, lam, reverse):
    B, S = x.shape[0], x.shape[1]
    xh = x.reshape(B, S, RNN_HEADS, RNN_HEAD_DIM)
    pre_r = jnp.einsum('bshi,hij->bshj', xh, w_a).reshape(B, S, RNN_WIDTH) + b_a
    pre_i = jnp.einsum('bshi,hij->bshj', xh, w_x).reshape(B, S, RNN_WIDTH) + b_x
    rec_gate = jax.nn.sigmoid(pre_r.astype(jnp.float32))
    in_gate = jax.nn.sigmoid(pre_i.astype(jnp.float32))
    log_a = -RG_LRU_C * rec_gate * jax.nn.softplus(-lam.astype(jnp.float32))
    a = jnp.exp(log_a)
    bterm = jnp.sqrt(-jnp.expm1(2.0 * log_a)) * in_gate * x.astype(jnp.float32)

    def combine(lhs, rhs):
        a1, b1 = lhs
        a2, b2 = rhs
        return a1 * a2, a2 * b1 + b2

    _, h = lax.associative_scan(combine, (a, bterm), axis=1, reverse=reverse)
    return h


def even_mixer(h, positions, w_in, w_out, sink, sg_ln_g, sg_ln_b, sg_w, sg_b):
    B, S = h.shape[0], h.shape[1]
    q, k, v, su, sv, g = jnp.split(h @ w_in, EVEN_SPLITS, axis=-1)
    q = partial_rotary(q.reshape(B, S, N_Q_HEADS, HEAD_DIM), positions)
    k = partial_rotary(k.reshape(B, S, N_KV_HEADS, HEAD_DIM), positions)
    v = v.reshape(B, S, N_KV_HEADS, HEAD_DIM)
    y_attn = windowed_gqa_sink(q, k, v, sink)
    y_sg = chunked_spatial_gating(su, sv, sg_ln_g, sg_ln_b, sg_w, sg_b)
    y = jnp.concatenate([y_attn, y_sg], axis=-1) * jax.nn.silu(g)
    return y @ w_out


def odd_mixer(h, w_in, conv_w, conv_b, w_a, b_a, w_x, b_x, lam, w_out):
    xr, g = jnp.split(h @ w_in, 2, axis=-1)
    xr = centred_depthwise_conv(xr, conv_w, conv_b)
    y = (rg_lru(xr, w_a[0], b_a[0], w_x[0], b_x[0], lam[0], reverse=False)
         + rg_lru(xr, w_a[1], b_a[1], w_x[1], b_x[1], lam[1], reverse=True))
    y = y.astype(h.dtype) * jax.nn.silu(g)
    return y @ w_out


def _fwd_setup_inputs(seed: int = 0) -> dict:
    key = jax.random.key(seed)
    ks = jax.random.split(key, 24)
    f32 = jnp.float32
    nrm = lambda k, shape, s: jax.random.normal(k, shape, f32) * s
    a_c = jax.random.uniform(ks[22], (N_ODD, 2, RNN_WIDTH), f32, minval=0.9, maxval=0.999)
    p = a_c ** (1.0 / RG_LRU_C)
    lam = jnp.log(p) - jnp.log1p(-p)
    return {
        "x": nrm(ks[0], (BATCH, SEQ, D_MODEL), 1.0),
        "c": nrm(ks[1], (BATCH, D_MODEL), 1.0),
        "positions": jnp.broadcast_to(jnp.arange(SEQ, dtype=jnp.int32), (BATCH, SEQ)),
        "ada_w": nrm(ks[2], (DEPTH, D_MODEL, 3 * D_MODEL), D_MODEL ** -0.5),
        "ada_b": nrm(ks[3], (DEPTH, 3 * D_MODEL), 0.01),
        "ln_g": 1.0 + nrm(ks[4], (DEPTH, D_MODEL), 0.02),
        "ln_b": nrm(ks[5], (DEPTH, D_MODEL), 0.02),
        "ev_w_in": nrm(ks[6], (N_EVEN, D_MODEL, EVEN_IN_WIDTH), D_MODEL ** -0.5),
        "ev_w_out": nrm(ks[7], (N_EVEN, EVEN_GATE_WIDTH, D_MODEL), DEEPNORM_BETA * EVEN_GATE_WIDTH ** -0.5),
        "ev_sink": nrm(ks[8], (N_EVEN, N_Q_HEADS), 1.0),
        "ev_sg_ln_g": 1.0 + nrm(ks[9], (N_EVEN, SG_WIDTH), 0.02),
        "ev_sg_ln_b": nrm(ks[10], (N_EVEN, SG_WIDTH), 0.02),
        "ev_sg_w": nrm(ks[11], (N_EVEN, N_SG_GROUPS, SG_CHUNK, SG_CHUNK), SG_CHUNK ** -0.5),
        "ev_sg_b": 1.0 + nrm(ks[12], (N_EVEN, N_SG_GROUPS, SG_CHUNK), 0.1),
        "od_w_in": nrm(ks[13], (N_ODD, D_MODEL, ODD_IN_WIDTH), D_MODEL ** -0.5),
        "od_conv_w": nrm(ks[14], (N_ODD, CONV_WIDTH, RNN_WIDTH), CONV_WIDTH ** -0.5),
        "od_conv_b": nrm(ks[15], (N_ODD, RNN_WIDTH), 0.01),
        "od_w_a": nrm(ks[16], (N_ODD, 2, RNN_HEADS, RNN_HEAD_DIM, RNN_HEAD_DIM), RNN_HEAD_DIM ** -0.5),
        "od_b_a": nrm(ks[17], (N_ODD, 2, RNN_WIDTH), 0.01),
        "od_w_x": nrm(ks[18], (N_ODD, 2, RNN_HEADS, RNN_HEAD_DIM, RNN_HEAD_DIM), RNN_HEAD_DIM ** -0.5),
        "od_b_x": nrm(ks[19], (N_ODD, 2, RNN_WIDTH), 0.01),
        "od_lam": lam,
        "od_w_out": nrm(ks[20], (N_ODD, RNN_WIDTH, D_MODEL), DEEPNORM_BETA * RNN_WIDTH ** -0.5),
    }


def _fwd_reference(x, c, positions, ada_w, ada_b, ln_g, ln_b, ev_w_in, ev_w_out, ev_sink, ev_sg_ln_g, ev_sg_ln_b,
              ev_sg_w, ev_sg_b, od_w_in, od_conv_w, od_conv_b, od_w_a, od_b_a, od_w_x, od_b_x, od_lam, od_w_out):
    cond = jax.nn.silu(c)
    for layer in range(DEPTH):
        mod = cond @ ada_w[layer] + ada_b[layer]
        shift, scale, gate = jnp.split(mod, 3, axis=-1)
        h = x * (1.0 + scale[:, None, :]) + shift[:, None, :]
        j = layer // 2
        if layer % 2 == 0:
            y = even_mixer(h, positions, ev_w_in[j], ev_w_out[j], ev_sink[j], ev_sg_ln_g[j], ev_sg_ln_b[j],
                           ev_sg_w[j], ev_sg_b[j])
        else:
            y = odd_mixer(h, od_w_in[j], od_conv_w[j], od_conv_b[j], od_w_a[j], od_b_a[j], od_w_x[j],
                          od_b_x[j], od_lam[j], od_w_out[j])
        x = layer_norm(DEEPNORM_ALPHA * x + gate[:, None, :] * y, ln_g[layer], ln_b[layer])
    return x


import jax as _jax
import jax.numpy as _jnp

TWIN_FORMAT = 'train_step'
FWD_PARAMS = ['x', 'c', 'positions', 'ada_w', 'ada_b', 'ln_g', 'ln_b', 'ev_w_in', 'ev_w_out', 'ev_sink', 'ev_sg_ln_g', 'ev_sg_ln_b', 'ev_sg_w', 'ev_sg_b', 'od_w_in', 'od_conv_w', 'od_conv_b', 'od_w_a', 'od_b_a', 'od_w_x', 'od_b_x', 'od_lam', 'od_w_out']
TWIN_WEIGHTS = ['ada_w', 'ada_b', 'ln_g', 'ln_b', 'ev_w_in', 'ev_w_out', 'ev_sink', 'ev_sg_ln_g', 'ev_sg_ln_b', 'ev_sg_w', 'ev_sg_b', 'od_w_in', 'od_conv_w', 'od_conv_b', 'od_w_a', 'od_b_a', 'od_w_x', 'od_b_x', 'od_lam', 'od_w_out']
TWIN_DIFF_INPUT = 'x'
TWIN_INPUTS = ['x', 'c', 'positions', 'ada_w', 'ada_b', 'ln_g', 'ln_b', 'ev_w_in', 'ev_w_out', 'ev_sink', 'ev_sg_ln_g', 'ev_sg_ln_b', 'ev_sg_w', 'ev_sg_b', 'od_w_in', 'od_conv_w', 'od_conv_b', 'od_w_a', 'od_b_a', 'od_w_x', 'od_b_x', 'od_lam', 'od_w_out', 'loss_target', 'm_ada_w', 'm_ada_b', 'm_ln_g', 'm_ln_b', 'm_ev_w_in', 'm_ev_w_out', 'm_ev_sink', 'm_ev_sg_ln_g', 'm_ev_sg_ln_b', 'm_ev_sg_w', 'm_ev_sg_b', 'm_od_w_in', 'm_od_conv_w', 'm_od_conv_b', 'm_od_w_a', 'm_od_b_a', 'm_od_w_x', 'm_od_b_x', 'm_od_lam', 'm_od_w_out', 'v_ada_w', 'v_ada_b', 'v_ln_g', 'v_ln_b', 'v_ev_w_in', 'v_ev_w_out', 'v_ev_sink', 'v_ev_sg_ln_g', 'v_ev_sg_ln_b', 'v_ev_sg_w', 'v_ev_sg_b', 'v_od_w_in', 'v_od_conv_w', 'v_od_conv_b', 'v_od_w_a', 'v_od_b_a', 'v_od_w_x', 'v_od_b_x', 'v_od_lam', 'v_od_w_out']
TWIN_OUTPUTS = ['loss', 'grad_x', 'grad_ada_w', 'grad_ada_b', 'grad_ln_g', 'grad_ln_b', 'grad_ev_w_in', 'grad_ev_w_out', 'grad_ev_sink', 'grad_ev_sg_ln_g', 'grad_ev_sg_ln_b', 'grad_ev_sg_w', 'grad_ev_sg_b', 'grad_od_w_in', 'grad_od_conv_w', 'grad_od_conv_b', 'grad_od_w_a', 'grad_od_b_a', 'grad_od_w_x', 'grad_od_b_x', 'grad_od_lam', 'grad_od_w_out', 'delta_ada_w', 'delta_ada_b', 'delta_ln_g', 'delta_ln_b', 'delta_ev_w_in', 'delta_ev_w_out', 'delta_ev_sink', 'delta_ev_sg_ln_g', 'delta_ev_sg_ln_b', 'delta_ev_sg_w', 'delta_ev_sg_b', 'delta_od_w_in', 'delta_od_conv_w', 'delta_od_conv_b', 'delta_od_w_a', 'delta_od_b_a', 'delta_od_w_x', 'delta_od_b_x', 'delta_od_lam', 'delta_od_w_out', 'new_m_ada_w', 'new_m_ada_b', 'new_m_ln_g', 'new_m_ln_b', 'new_m_ev_w_in', 'new_m_ev_w_out', 'new_m_ev_sink', 'new_m_ev_sg_ln_g', 'new_m_ev_sg_ln_b', 'new_m_ev_sg_w', 'new_m_ev_sg_b', 'new_m_od_w_in', 'new_m_od_conv_w', 'new_m_od_conv_b', 'new_m_od_w_a', 'new_m_od_b_a', 'new_m_od_w_x', 'new_m_od_b_x', 'new_m_od_lam', 'new_m_od_w_out', 'new_v_ada_w', 'new_v_ada_b', 'new_v_ln_g', 'new_v_ln_b', 'new_v_ev_w_in', 'new_v_ev_w_out', 'new_v_ev_sink', 'new_v_ev_sg_ln_g', 'new_v_ev_sg_ln_b', 'new_v_ev_sg_w', 'new_v_ev_sg_b', 'new_v_od_w_in', 'new_v_od_conv_w', 'new_v_od_conv_b', 'new_v_od_w_a', 'new_v_od_b_a', 'new_v_od_w_x', 'new_v_od_b_x', 'new_v_od_lam', 'new_v_od_w_out']
TWIN_LEAF_KINDS = {'loss': 'loss', 'grad_x': 'grad_x', 'grad_ada_w': 'grad_w', 'grad_ada_b': 'grad_w', 'grad_ln_g': 'grad_w', 'grad_ln_b': 'grad_w', 'grad_ev_w_in': 'grad_w', 'grad_ev_w_out': 'grad_w', 'grad_ev_sink': 'grad_w', 'grad_ev_sg_ln_g': 'grad_w', 'grad_ev_sg_ln_b': 'grad_w', 'grad_ev_sg_w': 'grad_w', 'grad_ev_sg_b': 'grad_w', 'grad_od_w_in': 'grad_w', 'grad_od_conv_w': 'grad_w', 'grad_od_conv_b': 'grad_w', 'grad_od_w_a': 'grad_w', 'grad_od_b_a': 'grad_w', 'grad_od_w_x': 'grad_w', 'grad_od_b_x': 'grad_w', 'grad_od_lam': 'grad_w', 'grad_od_w_out': 'grad_w', 'delta_ada_w': 'delta_w', 'delta_ada_b': 'delta_w', 'delta_ln_g': 'delta_w', 'delta_ln_b': 'delta_w', 'delta_ev_w_in': 'delta_w', 'delta_ev_w_out': 'delta_w', 'delta_ev_sink': 'delta_w', 'delta_ev_sg_ln_g': 'delta_w', 'delta_ev_sg_ln_b': 'delta_w', 'delta_ev_sg_w': 'delta_w', 'delta_ev_sg_b': 'delta_w', 'delta_od_w_in': 'delta_w', 'delta_od_conv_w': 'delta_w', 'delta_od_conv_b': 'delta_w', 'delta_od_w_a': 'delta_w', 'delta_od_b_a': 'delta_w', 'delta_od_w_x': 'delta_w', 'delta_od_b_x': 'delta_w', 'delta_od_lam': 'delta_w', 'delta_od_w_out': 'delta_w', 'new_m_ada_w': 'new_m', 'new_m_ada_b': 'new_m', 'new_m_ln_g': 'new_m', 'new_m_ln_b': 'new_m', 'new_m_ev_w_in': 'new_m', 'new_m_ev_w_out': 'new_m', 'new_m_ev_sink': 'new_m', 'new_m_ev_sg_ln_g': 'new_m', 'new_m_ev_sg_ln_b': 'new_m', 'new_m_ev_sg_w': 'new_m', 'new_m_ev_sg_b': 'new_m', 'new_m_od_w_in': 'new_m', 'new_m_od_conv_w': 'new_m', 'new_m_od_conv_b': 'new_m', 'new_m_od_w_a': 'new_m', 'new_m_od_b_a': 'new_m', 'new_m_od_w_x': 'new_m', 'new_m_od_b_x': 'new_m', 'new_m_od_lam': 'new_m', 'new_m_od_w_out': 'new_m', 'new_v_ada_w': 'new_v', 'new_v_ada_b': 'new_v', 'new_v_ln_g': 'new_v', 'new_v_ln_b': 'new_v', 'new_v_ev_w_in': 'new_v', 'new_v_ev_w_out': 'new_v', 'new_v_ev_sink': 'new_v', 'new_v_ev_sg_ln_g': 'new_v', 'new_v_ev_sg_ln_b': 'new_v', 'new_v_ev_sg_w': 'new_v', 'new_v_ev_sg_b': 'new_v', 'new_v_od_w_in': 'new_v', 'new_v_od_conv_w': 'new_v', 'new_v_od_conv_b': 'new_v', 'new_v_od_w_a': 'new_v', 'new_v_od_b_a': 'new_v', 'new_v_od_w_x': 'new_v', 'new_v_od_b_x': 'new_v', 'new_v_od_lam': 'new_v', 'new_v_od_w_out': 'new_v'}


def _forward(args):
    return _fwd_reference(*[args[k] for k in FWD_PARAMS])


def _output_shape():
    def fwd():
        inp = _fwd_setup_inputs(0)
        return _fwd_reference(*[inp[k] for k in FWD_PARAMS])
    out = _jax.eval_shape(fwd)
    return out.shape, out.dtype

N_MICROBATCH = 1
ADAM_LR = 0.001
ADAM_B1 = 0.9
ADAM_B2 = 0.999
ADAM_EPS = 1e-08
ADAM_WD = 0.01
ADAM_STEP = 10
PER_EXAMPLE_BATCH_AXIS = {'x': 0, 'c': 0, 'positions': 0, 'loss_target': 0}
SHARED_INPUTS = []
_WEIGHT_DTYPES = {'ada_w': _jnp.float32, 'ada_b': _jnp.float32, 'ln_g': _jnp.float32, 'ln_b': _jnp.float32, 'ev_w_in': _jnp.float32, 'ev_w_out': _jnp.float32, 'ev_sink': _jnp.float32, 'ev_sg_ln_g': _jnp.float32, 'ev_sg_ln_b': _jnp.float32, 'ev_sg_w': _jnp.float32, 'ev_sg_b': _jnp.float32, 'od_w_in': _jnp.float32, 'od_conv_w': _jnp.float32, 'od_conv_b': _jnp.float32, 'od_w_a': _jnp.float32, 'od_b_a': _jnp.float32, 'od_w_x': _jnp.float32, 'od_b_x': _jnp.float32, 'od_lam': _jnp.float32, 'od_w_out': _jnp.float32}
MOMENT_SCALE = {'ada_w': 1.563952e-01, 'ada_b': 3.025074e-01, 'ln_g': 2.330419e+01, 'ln_b': 3.246534e+00, 'ev_w_in': 4.253754e-02, 'ev_w_out': 9.269709e-02, 'ev_sink': 8.399156e-04, 'ev_sg_ln_g': 3.676146e-02, 'ev_sg_ln_b': 3.660531e-02, 'ev_sg_w': 2.794685e-02, 'ev_sg_b': 2.908469e-02, 'od_w_in': 1.857681e-01, 'od_conv_w': 1.805382e-01, 'od_conv_b': 3.569696e-01, 'od_w_a': 1.257853e-02, 'od_b_a': 1.806370e-02, 'od_w_x': 2.611066e-02, 'od_b_x': 3.624696e-02, 'od_lam': 4.993719e-02, 'od_w_out': 4.030294e-01}


def _to_microbatches(a, axis):
    t = _jnp.moveaxis(a, axis, 0)
    t = t.reshape((N_MICROBATCH, t.shape[0] // N_MICROBATCH) + t.shape[1:])
    return _jnp.moveaxis(t, 1, axis + 1)


def setup_inputs(seed: int = 0) -> dict:
    inp = _fwd_setup_inputs(seed)
    key = _jax.random.fold_in(_jax.random.key(seed), 7919)
    shape, _ = _output_shape()
    out = dict(inp)
    out["loss_target"] = _jax.random.normal(_jax.random.fold_in(key, 0), shape, _jnp.float32)
    for i, name in enumerate(TWIN_WEIGHTS):
        w = inp[name].astype(_jnp.float32)
        if MOMENT_SCALE is None:
            s = _jnp.sqrt(_jnp.mean(_jnp.square(w)) + 1e-30)
        else:
            s = MOMENT_SCALE[name]
        km, kv = _jax.random.split(_jax.random.fold_in(key, i + 1))
        out[name] = w
        out["m_" + name] = s * _jax.random.normal(km, w.shape, _jnp.float32)
        out["v_" + name] = (s * s) * _jax.random.uniform(kv, w.shape, _jnp.float32, 0.5, 1.5)
    if N_MICROBATCH > 1:
        for name, axis in PER_EXAMPLE_BATCH_AXIS.items():
            out[name] = _to_microbatches(out[name], axis)
    return {'x': out['x'], 'c': out['c'], 'positions': out['positions'], 'ada_w': out['ada_w'], 'ada_b': out['ada_b'], 'ln_g': out['ln_g'], 'ln_b': out['ln_b'], 'ev_w_in': out['ev_w_in'], 'ev_w_out': out['ev_w_out'], 'ev_sink': out['ev_sink'], 'ev_sg_ln_g': out['ev_sg_ln_g'], 'ev_sg_ln_b': out['ev_sg_ln_b'], 'ev_sg_w': out['ev_sg_w'], 'ev_sg_b': out['ev_sg_b'], 'od_w_in': out['od_w_in'], 'od_conv_w': out['od_conv_w'], 'od_conv_b': out['od_conv_b'], 'od_w_a': out['od_w_a'], 'od_b_a': out['od_b_a'], 'od_w_x': out['od_w_x'], 'od_b_x': out['od_b_x'], 'od_lam': out['od_lam'], 'od_w_out': out['od_w_out'], 'loss_target': out['loss_target'], 'm_ada_w': out['m_ada_w'], 'm_ada_b': out['m_ada_b'], 'm_ln_g': out['m_ln_g'], 'm_ln_b': out['m_ln_b'], 'm_ev_w_in': out['m_ev_w_in'], 'm_ev_w_out': out['m_ev_w_out'], 'm_ev_sink': out['m_ev_sink'], 'm_ev_sg_ln_g': out['m_ev_sg_ln_g'], 'm_ev_sg_ln_b': out['m_ev_sg_ln_b'], 'm_ev_sg_w': out['m_ev_sg_w'], 'm_ev_sg_b': out['m_ev_sg_b'], 'm_od_w_in': out['m_od_w_in'], 'm_od_conv_w': out['m_od_conv_w'], 'm_od_conv_b': out['m_od_conv_b'], 'm_od_w_a': out['m_od_w_a'], 'm_od_b_a': out['m_od_b_a'], 'm_od_w_x': out['m_od_w_x'], 'm_od_b_x': out['m_od_b_x'], 'm_od_lam': out['m_od_lam'], 'm_od_w_out': out['m_od_w_out'], 'v_ada_w': out['v_ada_w'], 'v_ada_b': out['v_ada_b'], 'v_ln_g': out['v_ln_g'], 'v_ln_b': out['v_ln_b'], 'v_ev_w_in': out['v_ev_w_in'], 'v_ev_w_out': out['v_ev_w_out'], 'v_ev_sink': out['v_ev_sink'], 'v_ev_sg_ln_g': out['v_ev_sg_ln_g'], 'v_ev_sg_ln_b': out['v_ev_sg_ln_b'], 'v_ev_sg_w': out['v_ev_sg_w'], 'v_ev_sg_b': out['v_ev_sg_b'], 'v_od_w_in': out['v_od_w_in'], 'v_od_conv_w': out['v_od_conv_w'], 'v_od_conv_b': out['v_od_conv_b'], 'v_od_w_a': out['v_od_w_a'], 'v_od_b_a': out['v_od_b_a'], 'v_od_w_x': out['v_od_w_x'], 'v_od_b_x': out['v_od_b_x'], 'v_od_lam': out['v_od_lam'], 'v_od_w_out': out['v_od_w_out']}


def _loss(weights, diff, rest, loss_target):
    with _jax.named_scope("forward"):
        args = {**rest, TWIN_DIFF_INPUT: diff, **{k: w.astype(_WEIGHT_DTYPES[k]) for k, w in weights.items()}}
        y = _forward(args)
    with _jax.named_scope("loss_head"):
        err = _jnp.square(y.astype(_jnp.float32) - loss_target)
        return 0.5 * _jnp.sum(_jnp.mean(err, axis=-1)) if err.ndim else 0.5 * err


def _adamw(w, g, m, v):
    m = ADAM_B1 * m + (1.0 - ADAM_B1) * g
    v = ADAM_B2 * v + (1.0 - ADAM_B2) * _jnp.square(g)
    m_hat = m / (1.0 - ADAM_B1 ** ADAM_STEP)
    v_hat = v / (1.0 - ADAM_B2 ** ADAM_STEP)
    delta = -ADAM_LR * (m_hat / (_jnp.sqrt(v_hat) + ADAM_EPS) + ADAM_WD * w)
    return delta, m, v


def reference(x, c, positions, ada_w, ada_b, ln_g, ln_b, ev_w_in, ev_w_out, ev_sink, ev_sg_ln_g, ev_sg_ln_b, ev_sg_w, ev_sg_b, od_w_in, od_conv_w, od_conv_b, od_w_a, od_b_a, od_w_x, od_b_x, od_lam, od_w_out, loss_target, m_ada_w, m_ada_b, m_ln_g, m_ln_b, m_ev_w_in, m_ev_w_out, m_ev_sink, m_ev_sg_ln_g, m_ev_sg_ln_b, m_ev_sg_w, m_ev_sg_b, m_od_w_in, m_od_conv_w, m_od_conv_b, m_od_w_a, m_od_b_a, m_od_w_x, m_od_b_x, m_od_lam, m_od_w_out, v_ada_w, v_ada_b, v_ln_g, v_ln_b, v_ev_w_in, v_ev_w_out, v_ev_sink, v_ev_sg_ln_g, v_ev_sg_ln_b, v_ev_sg_w, v_ev_sg_b, v_od_w_in, v_od_conv_w, v_od_conv_b, v_od_w_a, v_od_b_a, v_od_w_x, v_od_b_x, v_od_lam, v_od_w_out):
    given = dict(x=x, c=c, positions=positions, ada_w=ada_w, ada_b=ada_b, ln_g=ln_g, ln_b=ln_b, ev_w_in=ev_w_in, ev_w_out=ev_w_out, ev_sink=ev_sink, ev_sg_ln_g=ev_sg_ln_g, ev_sg_ln_b=ev_sg_ln_b, ev_sg_w=ev_sg_w, ev_sg_b=ev_sg_b, od_w_in=od_w_in, od_conv_w=od_conv_w, od_conv_b=od_conv_b, od_w_a=od_w_a, od_b_a=od_b_a, od_w_x=od_w_x, od_b_x=od_b_x, od_lam=od_lam, od_w_out=od_w_out, loss_target=loss_target, m_ada_w=m_ada_w, m_ada_b=m_ada_b, m_ln_g=m_ln_g, m_ln_b=m_ln_b, m_ev_w_in=m_ev_w_in, m_ev_w_out=m_ev_w_out, m_ev_sink=m_ev_sink, m_ev_sg_ln_g=m_ev_sg_ln_g, m_ev_sg_ln_b=m_ev_sg_ln_b, m_ev_sg_w=m_ev_sg_w, m_ev_sg_b=m_ev_sg_b, m_od_w_in=m_od_w_in, m_od_conv_w=m_od_conv_w, m_od_conv_b=m_od_conv_b, m_od_w_a=m_od_w_a, m_od_b_a=m_od_b_a, m_od_w_x=m_od_w_x, m_od_b_x=m_od_b_x, m_od_lam=m_od_lam, m_od_w_out=m_od_w_out, v_ada_w=v_ada_w, v_ada_b=v_ada_b, v_ln_g=v_ln_g, v_ln_b=v_ln_b, v_ev_w_in=v_ev_w_in, v_ev_w_out=v_ev_w_out, v_ev_sink=v_ev_sink, v_ev_sg_ln_g=v_ev_sg_ln_g, v_ev_sg_ln_b=v_ev_sg_ln_b, v_ev_sg_w=v_ev_sg_w, v_ev_sg_b=v_ev_sg_b, v_od_w_in=v_od_w_in, v_od_conv_w=v_od_conv_w, v_od_conv_b=v_od_conv_b, v_od_w_a=v_od_w_a, v_od_b_a=v_od_b_a, v_od_w_x=v_od_w_x, v_od_b_x=v_od_b_x, v_od_lam=v_od_lam, v_od_w_out=v_od_w_out)
    weights = {n: given[n] for n in TWIN_WEIGHTS}
    shared = {n: given[n] for n in SHARED_INPUTS}
    per_example = {n: given[n] for n in ['x', 'c', 'positions']}
    grad_fn = _jax.value_and_grad(_loss, argnums=(0, 1))

    def one_microbatch(ex, loss_target):
        ex = dict(ex)
        diff = ex.pop(TWIN_DIFF_INPUT)
        return grad_fn(weights, diff, {**shared, **ex}, loss_target)

    if N_MICROBATCH == 1:
        loss, (grad_w, grad_x) = one_microbatch(per_example, given["loss_target"])
    else:
        def body(carry, xs):
            loss_sum, grad_sum = carry
            l_k, (gw_k, gx_k) = one_microbatch(xs[0], xs[1])
            with _jax.named_scope("update"):
                return (loss_sum + l_k, _jax.tree.map(_jnp.add, grad_sum, gw_k)), gx_k

        init = (_jnp.zeros((), _jnp.float32), _jax.tree.map(_jnp.zeros_like, weights))
        (loss, grad_w), grad_x = _jax.lax.scan(body, init, (per_example, given["loss_target"]))
    with _jax.named_scope("update"):
        delta_w, new_m, new_v = {}, {}, {}
        for n in TWIN_WEIGHTS:
            delta_w[n], new_m[n], new_v[n] = _adamw(weights[n], grad_w[n], given["m_" + n], given["v_" + n])
    return (loss, grad_x, *[grad_w[n] for n in TWIN_WEIGHTS], *[delta_w[n] for n in TWIN_WEIGHTS],
            *[new_m[n] for n in TWIN_WEIGHTS], *[new_v[n] for n in TWIN_WEIGHTS])
```

```python
import functools

import jax
import jax.numpy as jnp
from jax import lax
from jax.experimental import pallas as pl
from jax.experimental.pallas import tpu as pltpu

F32 = jnp.float32
BF16 = jnp.bfloat16

N_DEV = 8
D = 1024
N_HEADS = 8
HEAD_DIM = 64
KV_WIDTH = 128
ATTN_W = 512
SG_W = 512
SG_GROUPS = 8
SG_DIM = 64
BLK = 128
EV_IN = 2816
OD_IN = 2048
RNN_HEADS = 8
RNN_HD = 128
ALPHA = 4.0 ** 0.25
LN_EPS = 1e-5
NEG_INF = -1e30
RG_C = 8.0
ROPE_THETA = 500000.0
LR, B1, B2, EPS, WD, STEP = 0.001, 0.9, 0.999, 1e-08, 0.01, 10

LANE = 128
SUBLANE = 8
TM = 256
TS = 256
VMEM_LIMIT = 56 * 1024 * 1024

MESH = pl.DeviceIdType.MESH


def _pallas(body, **kw):
    return pl.pallas_call(body, **kw)


def _params(sem, vmem=VMEM_LIMIT):
    return pltpu.CompilerParams(dimension_semantics=sem, vmem_limit_bytes=vmem)


def _sigmoid(x):
    return 1.0 / (1.0 + jnp.exp(-x))


def _silu_and_grad(x):
    s = _sigmoid(x)
    return x * s, s * (1.0 + x * (1.0 - s))


def _dot(a, b):
    return jnp.dot(a.astype(BF16), b.astype(BF16), preferred_element_type=F32)


def _dot_nt(a, b):
    return lax.dot_general(a.astype(BF16), b.astype(BF16), (((1,), (1,)), ((), ())), preferred_element_type=F32)


def _dot_tn(a, b):
    return lax.dot_general(a.astype(BF16), b.astype(BF16), (((0,), (0,)), ((), ())), preferred_element_type=F32)


def _ln_fwd(z, g, b):
    mu = jnp.mean(z, axis=-1, keepdims=True)
    zc = z - mu
    var = jnp.mean(zc * zc, axis=-1, keepdims=True)
    rstd = lax.rsqrt(var + LN_EPS)
    xhat = zc * rstd
    return xhat * g + b, xhat, rstd


def _ln_bwd(dy, xhat, rstd, g):
    dxh = dy * g
    m1 = jnp.mean(dxh, axis=-1, keepdims=True)
    m2 = jnp.mean(dxh * xhat, axis=-1, keepdims=True)
    return rstd * (dxh - m1 - xhat * m2)


def _rowsum(v):
    return jnp.sum(v, axis=0, keepdims=True)


def _rope_fwd(t, c, s1, s2):
    return t * c + pltpu.roll(t, LANE - 8, 1) * s1 + pltpu.roll(t, 8, 1) * s2


def _rope_bwd(d, c, s1, s2):
    return d * c + pltpu.roll(d * s1, 8, 1) + pltpu.roll(d * s2, LANE - 8, 1)


def _neg_expm1(x):
    poly = -x * (1.0 + x * (0.5 + x * (1.0 / 6.0 + x * (1.0 / 24.0))))
    return jnp.where(x > -0.03, poly, 1.0 - jnp.exp(x))


def _adam(w, g, m, v):
    m2 = B1 * m + (1.0 - B1) * g
    v2 = B2 * v + (1.0 - B2) * (g * g)
    m_hat = m2 / (1.0 - B1 ** STEP)
    v_hat = v2 / (1.0 - B2 ** STEP)
    delta = -LR * (m_hat / (jnp.sqrt(v_hat) + EPS) + WD * w)
    return delta, m2, v2


def _tile(rows, width):
    return pl.BlockSpec((rows, width), lambda i: (i, 0))


def _full(shape):
    zeros = (0,) * len(shape)
    return pl.BlockSpec(shape, lambda i: zeros)


def _rev_tile(rows, width, n, reverse):
    if reverse:
        return pl.BlockSpec((rows, width), lambda i: (n - 1 - i, 0))
    return pl.BlockSpec((rows, width), lambda i: (i, 0))


def _halo_specs(rows, width, n, total_rows, reverse):
    per = rows // SUBLANE
    last = total_rows // SUBLANE - 1

    def tile_of(i):
        return (n - 1 - i) if reverse else i

    prev = pl.BlockSpec((SUBLANE, width), lambda i: (jnp.maximum(tile_of(i) * per - 1, 0), 0))
    nxt = pl.BlockSpec((SUBLANE, width), lambda i: (jnp.minimum((tile_of(i) + 1) * per, last), 0))
    return prev, nxt


def _my_pos():
    return lax.axis_index("x"), lax.axis_index("y"), lax.axis_index("c")


def _slot(px, py, pc):
    return 4 * px + 2 * py + pc


def _all_gather(arrs, name):
    n = len(arrs)

    def body(*refs):
        ins, outs = refs[:n], refs[n:2 * n]
        send_sems, recv_sems, local_sems = refs[2 * n:]
        x, y, c = _my_pos()
        me, sibling = (x, y, c), (x, y, 1 - c)
        chips = [(1 - x, y), (x, 1 - y), (1 - x, 1 - y)]

        def copy(a, k, block, to, src=None):
            dst = outs[a].at[_slot(*block)]
            return pltpu.make_async_remote_copy(
                src_ref=dst if src is None else src, dst_ref=dst,
                send_sem=send_sems.at[a * 7 + k], recv_sem=recv_sems.at[a * 7 + k],
                device_id=to, device_id_type=MESH)

        local, first = [], []
        for a in range(n):
            lc = pltpu.make_async_copy(ins[a], outs[a].at[_slot(*me)], local_sems.at[a])
            lc.start()
            local.append(lc)
            first.append(copy(a, 0, me, sibling, src=ins[a]))
            first += [copy(a, 1 + j, me, (*chip, c), src=ins[a]) for j, chip in enumerate(chips)]
        for cp in first:
            cp.start()
        passed = []
        for j, chip in enumerate(chips):
            for a in range(n):
                copy(a, 1 + j, (*chip, c), me).wait_recv()
                fw = copy(a, 4 + j, (*chip, c), sibling)
                fw.start()
                passed.append(fw)
        for a in range(n):
            copy(a, 0, sibling, me).wait_recv()
            for j, chip in enumerate(chips):
                copy(a, 4 + j, (*chip, 1 - c), me).wait_recv()
        for cp in first + passed:
            cp.wait_send()
        for lc in local:
            lc.wait()

    any_spec = pl.BlockSpec(memory_space=pl.ANY)
    return _pallas(
        body, name=name,
        out_shape=[jax.ShapeDtypeStruct((N_DEV,) + a.shape, a.dtype) for a in arrs],
        in_specs=[any_spec] * n, out_specs=[any_spec] * n,
        scratch_shapes=[pltpu.SemaphoreType.DMA((7 * n,)), pltpu.SemaphoreType.DMA((7 * n,)),
                        pltpu.SemaphoreType.DMA((n,))],
    )(*arrs)


def _all_to_all(arrs, name):
    n = len(arrs)

    def body(*refs):
        ins, outs = refs[:n], refs[n:2 * n]
        send_sems, recv_sems, local_sems = refs[2 * n:]
        x, y, c = _my_pos()
        mine = _slot(x, y, c)
        copies = []
        for a in range(n):
            lc = pltpu.make_async_copy(ins[a].at[mine], outs[a].at[mine], local_sems.at[a])
            lc.start()
            copies.append(lc)
        for k in range(1, N_DEV):
            px = (1 - x) if (k & 4) else x
            py = (1 - y) if (k & 2) else y
            pc = (1 - c) if (k & 1) else c
            for a in range(n):
                cp = pltpu.make_async_remote_copy(
                    src_ref=ins[a].at[_slot(px, py, pc)], dst_ref=outs[a].at[mine],
                    send_sem=send_sems.at[a * 7 + k - 1], recv_sem=recv_sems.at[a * 7 + k - 1],
                    device_id=(px, py, pc), device_id_type=MESH)
                cp.start()
                copies.append(cp)
        for cp in copies:
            cp.wait()

    any_spec = pl.BlockSpec(memory_space=pl.ANY)
    return _pallas(
        body, name=name,
        out_shape=[jax.ShapeDtypeStruct(a.shape, a.dtype) for a in arrs],
        in_specs=[any_spec] * n, out_specs=[any_spec] * n,
        scratch_shapes=[pltpu.SemaphoreType.DMA((7 * n,)), pltpu.SemaphoreType.DMA((7 * n,)),
                        pltpu.SemaphoreType.DMA((n,))],
    )(*arrs)


def _mod_part(c_all, ada_w):
    cols = ada_w.shape[2]

    def body(c_ref, w_ref, o_ref):
        cv = c_ref[...]
        cond = cv * _sigmoid(cv)
        for l in range(2):
            o_ref[l] = _dot(cond, w_ref[l])

    return _pallas(
        body, name="mod_part", grid=(1,),
        in_specs=[_full((N_DEV, D)), _full((2, D, cols))],
        out_specs=_full((2, N_DEV, cols)),
        out_shape=jax.ShapeDtypeStruct((2, N_DEV, cols), F32),
        compiler_params=_params(("arbitrary",)),
    )(c_all, ada_w)


def _ada_update(c_all, dmod_cols, dmod_all, ada_w, m_w, v_w, ada_b, m_b, v_b):
    cols = ada_w.shape[2]
    nb = ada_b.shape[1]

    def body(c_ref, dmc_ref, dma_ref, w_ref, mw_ref, vw_ref, b_ref, mb_ref, vb_ref,
             gw_ref, dw_ref, nmw_ref, nvw_ref, gb_ref, db_ref, nmb_ref, nvb_ref):
        cv = c_ref[...]
        cond = cv * _sigmoid(cv)
        for l in range(2):
            g = _dot_tn(cond, dmc_ref[l])
            gw_ref[l] = g
            dlt, m2, v2 = _adam(w_ref[l], g, mw_ref[l], vw_ref[l])
            dw_ref[l] = dlt
            nmw_ref[l] = m2
            nvw_ref[l] = v2
        gb = dma_ref[0]
        for i in range(1, N_DEV):
            gb = gb + dma_ref[i]
        gb_ref[...] = gb
        dlt, m2, v2 = _adam(b_ref[...], gb, mb_ref[...], vb_ref[...])
        db_ref[...] = dlt
        nmb_ref[...] = m2
        nvb_ref[...] = v2

    wspec = _full((2, D, cols))
    bspec = _full((2, nb))
    wshape = jax.ShapeDtypeStruct((2, D, cols), F32)
    bshape = jax.ShapeDtypeStruct((2, nb), F32)
    return _pallas(
        body, name="ada_update", grid=(1,),
        in_specs=[_full((N_DEV, D)), _full((2, N_DEV, cols)), _full((N_DEV, 2, nb)),
                  wspec, wspec, wspec, bspec, bspec, bspec],
        out_specs=[wspec] * 4 + [bspec] * 4,
        out_shape=[wshape] * 4 + [bshape] * 4,
        compiler_params=_params(("arbitrary",)),
    )(c_all, dmod_cols, dmod_all, ada_w, m_w, v_w, ada_b, m_b, v_b)


def _ev_in(x, mod, w_in, rc, rs1, rs2):
    T = x.shape[0]

    def body(x_ref, mod_ref, w_ref, c_ref, s1_ref, s2_ref, q_ref, kv_ref, su_ref, sv_ref, g_ref):
        h = x_ref[...] * (1.0 + mod_ref[1:2, :]) + mod_ref[0:1, :]
        p = _dot(h, w_ref[...])
        c, s1, s2 = c_ref[...], s1_ref[...], s2_ref[...]
        for j in range(ATTN_W // LANE):
            q_ref[:, j * LANE:(j + 1) * LANE] = _rope_fwd(p[:, j * LANE:(j + 1) * LANE], c, s1, s2).astype(BF16)
        kv_ref[:, 0:LANE] = _rope_fwd(p[:, 512:640], c, s1, s2).astype(BF16)
        kv_ref[:, LANE:2 * LANE] = p[:, 640:768].astype(BF16)
        su_ref[...] = p[:, 768:1280].astype(BF16)
        sv_ref[...] = p[:, 1280:1792].astype(BF16)
        g_ref[...] = p[:, 1792:2816].astype(BF16)

    sh = lambda w: jax.ShapeDtypeStruct((T, w), BF16)
    return _pallas(
        body, name="ev_in", grid=(T // TM,),
        in_specs=[_tile(TM, D), _full((3, D)), _full((D, EV_IN)), _tile(TM, LANE), _tile(TM, LANE), _tile(TM, LANE)],
        out_specs=[_tile(TM, ATTN_W), _tile(TM, 2 * KV_WIDTH), _tile(TM, SG_W), _tile(TM, SG_W), _tile(TM, D)],
        out_shape=[sh(ATTN_W), sh(2 * KV_WIDTH), sh(SG_W), sh(SG_W), sh(D)],
        compiler_params=_params(("parallel",)),
    )(x, mod, w_in, rc, rs1, rs2)


def _band_mask(n, T):
    qi = lax.broadcasted_iota(jnp.int32, (BLK, 3 * BLK), 0)
    kj = lax.broadcasted_iota(jnp.int32, (BLK, 3 * BLK), 1)
    k_abs = n * BLK - BLK + kj
    return (jnp.abs(kj - BLK - qi) <= BLK) & (k_abs >= 0) & (k_abs < T)


def _sg_norm(sv, g, lng, lnb):
    vg = sv[:, g * SG_DIM:(g + 1) * SG_DIM]
    y, xhat, rstd = _ln_fwd(vg, lng[:, g * SG_DIM:(g + 1) * SG_DIM], lnb[:, g * SG_DIM:(g + 1) * SG_DIM])
    return y, xhat, rstd


def _mix0_fwd(q, kvp, su, sv, g0, sink, sg_lng, sg_lnb, sg_w, sg_bt):
    T = q.shape[0]

    def body(q_ref, kv_ref, su_ref, sv_ref, g_ref, sink_ref, lng_ref, lnb_ref, w_ref, bt_ref,
             ycat_ref, y0_ref, lse_ref):
        n = pl.program_id(0)
        kv = kv_ref[pl.ds(pl.multiple_of(n * BLK, BLK), 3 * BLK), :]
        valid = _band_mask(n, T)
        qb = q_ref[...]
        outs = []
        for h in range(N_HEADS):
            kh = (h // 4) * HEAD_DIM
            s = _dot_nt(qb[:, h * HEAD_DIM:(h + 1) * HEAD_DIM], kv[:, kh:kh + HEAD_DIM]) * (HEAD_DIM ** -0.5)
            s = jnp.where(valid, s, NEG_INF)
            sk = sink_ref[0:1, h:h + 1]
            m = jnp.maximum(jnp.max(s, axis=-1, keepdims=True), sk)
            p = jnp.exp(s - m)
            denom = jnp.sum(p, axis=-1, keepdims=True) + jnp.exp(sk - m)
            outs.append(_dot(p / denom, kv[:, KV_WIDTH + kh:KV_WIDTH + kh + HEAD_DIM]))
            lse_ref[:, h:h + 1] = m + jnp.log(denom)
        svf = sv_ref[...].astype(F32)
        suf = su_ref[...].astype(F32)
        lng, lnb = lng_ref[...], lnb_ref[...]
        for g in range(SG_GROUPS):
            vgn, _, _ = _sg_norm(svf, g, lng, lnb)
            svm = _dot(w_ref[g], vgn) + bt_ref[:, g:g + 1]
            outs.append(suf[:, g * SG_DIM:(g + 1) * SG_DIM] * svm)
        ycat = jnp.concatenate(outs, axis=-1)
        gf = g_ref[...].astype(F32)
        ycat_ref[...] = ycat.astype(BF16)
        y0_ref[...] = (ycat * (gf * _sigmoid(gf))).astype(BF16)

    return _pallas(
        body, name="mix0_fwd", grid=(T // BLK,),
        in_specs=[_tile(BLK, ATTN_W), _full((T + 2 * BLK, 2 * KV_WIDTH)), _tile(BLK, SG_W), _tile(BLK, SG_W),
                  _tile(BLK, D), _full((1, N_HEADS)), _full((1, SG_W)), _full((1, SG_W)),
                  _full((SG_GROUPS, BLK, BLK)), _full((BLK, SG_GROUPS))],
        out_specs=[_tile(BLK, D), _tile(BLK, D), _tile(BLK, N_HEADS)],
        out_shape=[jax.ShapeDtypeStruct((T, D), BF16), jax.ShapeDtypeStruct((T, D), BF16),
                   jax.ShapeDtypeStruct((T, N_HEADS), F32)],
        compiler_params=_params(("parallel",)),
    )(q, kvp, su, sv, g0, sink, sg_lng, sg_lnb, sg_w, sg_bt)


def _ev_out(y0, w_out, x, mod, lnp):
    T = x.shape[0]

    def body(y_ref, w_ref, x_ref, mod_ref, ln_ref, out_ref, z_ref, x1_ref):
        out = _dot(y_ref[...], w_ref[...])
        z = ALPHA * x_ref[...] + mod_ref[2:3, :] * out
        x1, _, _ = _ln_fwd(z, ln_ref[0:1, :], ln_ref[1:2, :])
        out_ref[...] = out.astype(BF16)
        z_ref[...] = z
        x1_ref[...] = x1

    return _pallas(
        body, name="ev_out", grid=(T // TM,),
        in_specs=[_tile(TM, D), _full((D, D)), _tile(TM, D), _full((3, D)), _full((2, D))],
        out_specs=[_tile(TM, D)] * 3,
        out_shape=[jax.ShapeDtypeStruct((T, D), BF16), jax.ShapeDtypeStruct((T, D), F32),
                   jax.ShapeDtypeStruct((T, D), F32)],
        compiler_params=_params(("parallel",)),
    )(y0, w_out, x, mod, lnp)


def _od_in(x1, mod, w_in):
    T = x1.shape[0]

    def body(x_ref, mod_ref, w_ref, xr_ref, g_ref):
        h = x_ref[...] * (1.0 + mod_ref[1:2, :]) + mod_ref[0:1, :]
        p = _dot(h, w_ref[...])
        xr_ref[...] = p[:, :D]
        g_ref[...] = p[:, D:].astype(BF16)

    return _pallas(
        body, name="od_in", grid=(T // TM,),
        in_specs=[_tile(TM, D), _full((3, D)), _full((D, OD_IN))],
        out_specs=[_tile(TM, D), _tile(TM, D)],
        out_shape=[jax.ShapeDtypeStruct((T, D), F32), jax.ShapeDtypeStruct((T, D), BF16)],
        compiler_params=_params(("parallel",)),
    )(x1, mod, w_in)


def _ext_rows(prev_ref, cur, next_ref, j, n):
    prev = jnp.where(j > 0, prev_ref[...], 0.0)
    nxt = jnp.where(j < n - 1, next_ref[...], 0.0)
    return jnp.concatenate([prev, cur, nxt], axis=0)


def _shift_rows(ext, off, rows):
    total = ext.shape[0]
    if off == 0:
        return ext[SUBLANE:SUBLANE + rows, :]
    return pltpu.roll(ext, (-off) % total, 0)[SUBLANE:SUBLANE + rows, :]


def _conv_fwd(ext, cw, cb, rows):
    xc = cb
    for k in range(4):
        xc = xc + cw[k:k + 1, :] * _shift_rows(ext, k - 2, rows)
    return xc


def _gates(xc, wa_ref, wx_ref, ba, bx, lam):
    pr, pi = [], []
    for h in range(RNN_HEADS):
        xh = xc[:, h * RNN_HD:(h + 1) * RNN_HD].astype(BF16)
        pr.append(_dot(xh, wa_ref[h]))
        pi.append(_dot(xh, wx_ref[h]))
    r = _sigmoid(jnp.concatenate(pr, axis=-1) + ba)
    ig = _sigmoid(jnp.concatenate(pi, axis=-1) + bx)
    sp = jnp.maximum(-lam, 0.0) + jnp.log(1.0 + jnp.exp(-jnp.abs(lam)))
    log_a = -RG_C * r * sp
    a = jnp.exp(log_a)
    s = jnp.sqrt(_neg_expm1(2.0 * log_a))
    return r, ig, sp, a, s


def _scan_tile(a_ref, b_ref, o_ref, carry_ref, rows, reverse):
    ridx = lax.broadcasted_iota(jnp.int32, (SUBLANE, D), 0)
    groups = rows // SUBLANE

    def group(gi, h):
        g = (groups - 1 - gi) if reverse else gi
        off = pl.multiple_of(g * SUBLANE, SUBLANE)
        a = a_ref[pl.ds(off, SUBLANE), :]
        b = b_ref[pl.ds(off, SUBLANE), :]
        for sh in (1, 2, 4):
            if reverse:
                keep = ridx < SUBLANE - sh
                a_p = jnp.where(keep, pltpu.roll(a, SUBLANE - sh, 0), 1.0)
                b_p = jnp.where(keep, pltpu.roll(b, SUBLANE - sh, 0), 0.0)
            else:
                keep = ridx >= sh
                a_p = jnp.where(keep, pltpu.roll(a, sh, 0), 1.0)
                b_p = jnp.where(keep, pltpu.roll(b, sh, 0), 0.0)
            b = b + a * b_p
            a = a * a_p
        hh = b + a * h
        o_ref[pl.ds(off, SUBLANE), :] = hh
        return hh[0:1, :] if reverse else hh[SUBLANE - 1:SUBLANE, :]

    carry_ref[...] = lax.fori_loop(0, groups, group, carry_ref[...])


def _rglru_fwd(xr, cw, cb, wa, wx, ba, bx, lam, reverse, name):
    T = xr.shape[0]
    n = T // TS
    prev_spec, next_spec = _halo_specs(TS, D, n, T, reverse)

    def body(prev_ref, cur_ref, next_ref, cw_ref, cb_ref, wa_ref, wx_ref, ba_ref, bx_ref, lam_ref,
             h_ref, a_s, b_s, carry):
        i = pl.program_id(0)
        j = (n - 1 - i) if reverse else i

        @pl.when(i == 0)
        def _():
            carry[...] = jnp.zeros_like(carry)

        ext = _ext_rows(prev_ref, cur_ref[...], next_ref, j, n)
        xc = _conv_fwd(ext, cw_ref[...], cb_ref[...], TS)
        _, ig, _, a, s = _gates(xc, wa_ref, wx_ref, ba_ref[...], bx_ref[...], lam_ref[...])
        a_s[...] = a
        b_s[...] = s * ig * xc
        _scan_tile(a_s, b_s, h_ref, carry, TS, reverse)

    wspec = _full((RNN_HEADS, RNN_HD, RNN_HD))
    return _pallas(
        body, name=name, grid=(n,),
        in_specs=[prev_spec, _rev_tile(TS, D, n, reverse), next_spec, _full((4, D)), _full((1, D)),
                  wspec, wspec, _full((1, D)), _full((1, D)), _full((1, D))],
        out_specs=_rev_tile(TS, D, n, reverse),
        out_shape=jax.ShapeDtypeStruct((T, D), F32),
        scratch_shapes=[pltpu.VMEM((TS, D), F32), pltpu.VMEM((TS, D), F32), pltpu.VMEM((1, D), F32)],
        compiler_params=_params(("arbitrary",)),
    )(xr, xr, xr, cw, cb, wa, wx, ba, bx, lam)


def _od_out(hf, hb, g1, w_out, x1, tgt, mod, lnp):
    T = x1.shape[0]

    def body(hf_ref, hb_ref, g_ref, w_ref, x_ref, t_ref, mod_ref, ln_ref,
             loss_ref, dh_ref, dg_ref, dx_ref, dw_ref, vec_ref):
        i = pl.program_id(0)

        @pl.when(i == 0)
        def _():
            loss_ref[...] = jnp.zeros_like(loss_ref)
            dw_ref[...] = jnp.zeros_like(dw_ref)
            vec_ref[...] = jnp.zeros_like(vec_ref)

        hs = hf_ref[...] + hb_ref[...]
        sg, dsg = _silu_and_grad(g_ref[...].astype(F32))
        yr = (hs * sg).astype(BF16)
        w = w_ref[...]
        out = _dot(yr, w)
        gate = mod_ref[2:3, :]
        z = ALPHA * x_ref[...] + gate * out
        lng = ln_ref[0:1, :]
        x2, xhat, rstd = _ln_fwd(z, lng, ln_ref[1:2, :])
        diff = x2 - t_ref[...]
        loss_ref[...] += 0.5 * jnp.sum(diff * diff) * (1.0 / D)
        dx2 = diff * (1.0 / D)
        dz = _ln_bwd(dx2, xhat, rstd, lng)
        vec_ref[0:1, :] += _rowsum(dx2 * xhat)
        vec_ref[1:2, :] += _rowsum(dx2)
        vec_ref[2:3, :] += _rowsum(dz * out)
        dout = (dz * gate).astype(BF16)
        dyr = _dot_nt(dout, w)
        dw_ref[...] += _dot_tn(yr, dout)
        dh_ref[...] = dyr * sg
        dg_ref[...] = (dyr * hs * dsg).astype(BF16)
        dx_ref[...] = ALPHA * dz

    return _pallas(
        body, name="od_out", grid=(T // TM,),
        in_specs=[_tile(TM, D), _tile(TM, D), _tile(TM, D), _full((D, D)), _tile(TM, D), _tile(TM, D),
                  _full((3, D)), _full((2, D))],
        out_specs=[_full((1, LANE)), _tile(TM, D), _tile(TM, D), _tile(TM, D), _full((D, D)), _full((SUBLANE, D))],
        out_shape=[jax.ShapeDtypeStruct((1, LANE), F32), jax.ShapeDtypeStruct((T, D), F32),
                   jax.ShapeDtypeStruct((T, D), BF16), jax.ShapeDtypeStruct((T, D), F32),
                   jax.ShapeDtypeStruct((D, D), F32), jax.ShapeDtypeStruct((SUBLANE, D), F32)],
        compiler_params=_params(("arbitrary",)),
    )(hf, hb, g1, w_out, x1, tgt, mod, lnp)


def _rglru_bwd(xr, dh, h, cw, cb, wa, wx, ba, bx, lam, reverse, name):
    T = xr.shape[0]
    n = T // TS
    adj_rev = not reverse
    xprev_spec, xnext_spec = _halo_specs(TS, D, n, T, adj_rev)
    hprev_spec, hnext_spec = _halo_specs(TS, D, n, T, adj_rev)
    h_halo_spec = hnext_spec if reverse else hprev_spec

    def body(xprev_ref, xcur_ref, xnext_ref, dh_ref, h_ref, hh_ref, cw_ref, cb_ref, wa_ref, wx_ref,
             ba_ref, bx_ref, lam_ref, dxc_ref, dwa_ref, dwx_ref, vec_ref, a_s, b_s, l_s, carry, a_edge):
        i = pl.program_id(0)
        j = (n - 1 - i) if adj_rev else i

        @pl.when(i == 0)
        def _():
            carry[...] = jnp.zeros_like(carry)
            a_edge[...] = jnp.zeros_like(a_edge)
            dwa_ref[...] = jnp.zeros_like(dwa_ref)
            dwx_ref[...] = jnp.zeros_like(dwx_ref)
            vec_ref[...] = jnp.zeros_like(vec_ref)

        ext = _ext_rows(xprev_ref, xcur_ref[...], xnext_ref, j, n)
        xc = _conv_fwd(ext, cw_ref[...], cb_ref[...], TS)
        lam = lam_ref[...]
        r, ig, sp, a, s = _gates(xc, wa_ref, wx_ref, ba_ref[...], bx_ref[...], lam)

        rows = lax.broadcasted_iota(jnp.int32, (TS, D), 0)
        hcur = h_ref[...]
        if reverse:
            a_sh = jnp.where(rows == 0, a_edge[...], pltpu.roll(a, 1, 0))
            halo = jnp.where(j < n - 1, hh_ref[0:1, :], 0.0)
            h_nb = jnp.where(rows == TS - 1, halo, pltpu.roll(hcur, TS - 1, 0))
        else:
            a_sh = jnp.where(rows == TS - 1, a_edge[...], pltpu.roll(a, TS - 1, 0))
            halo = jnp.where(j > 0, hh_ref[SUBLANE - 1:SUBLANE, :], 0.0)
            h_nb = jnp.where(rows == 0, halo, pltpu.roll(hcur, 1, 0))
        a_s[...] = a_sh
        b_s[...] = dh_ref[...]
        _scan_tile(a_s, b_s, l_s, carry, TS, adj_rev)
        a_edge[...] = a[TS - 1:TS, :] if reverse else a[0:1, :]

        lm = l_s[...]
        da = lm * h_nb
        di = lm * s * xc
        dxc = lm * s * ig
        ds = lm * ig * xc
        dlog_a = a * da - ds * (a * a) / s
        dr = (-RG_C) * sp * dlog_a
        dsp = _rowsum((-RG_C) * r * dlog_a)
        dpr = dr * r * (1.0 - r)
        dpi = di * ig * (1.0 - ig)
        vec_ref[0:1, :] += _rowsum(dpr)
        vec_ref[1:2, :] += _rowsum(dpi)
        vec_ref[2:3, :] += dsp * (-_sigmoid(-lam))
        parts = []
        for hd in range(RNN_HEADS):
            sl = slice(hd * RNN_HD, (hd + 1) * RNN_HD)
            xh = xc[:, sl].astype(BF16)
            dprh = dpr[:, sl].astype(BF16)
            dpih = dpi[:, sl].astype(BF16)
            parts.append(_dot_nt(dprh, wa_ref[hd]) + _dot_nt(dpih, wx_ref[hd]))
            dwa_ref[hd] += _dot_tn(xh, dprh)
            dwx_ref[hd] += _dot_tn(xh, dpih)
        dxc_ref[...] = dxc + jnp.concatenate(parts, axis=-1)

    wspec = _full((RNN_HEADS, RNN_HD, RNN_HD))
    cur = _rev_tile(TS, D, n, adj_rev)
    return _pallas(
        body, name=name, grid=(n,),
        in_specs=[xprev_spec, cur, xnext_spec, cur, cur, h_halo_spec, _full((4, D)), _full((1, D)),
                  wspec, wspec, _full((1, D)), _full((1, D)), _full((1, D))],
        out_specs=[cur, wspec, wspec, _full((SUBLANE, D))],
        out_shape=[jax.ShapeDtypeStruct((T, D), F32),
                   jax.ShapeDtypeStruct((RNN_HEADS, RNN_HD, RNN_HD), F32),
                   jax.ShapeDtypeStruct((RNN_HEADS, RNN_HD, RNN_HD), F32),
                   jax.ShapeDtypeStruct((SUBLANE, D), F32)],
        scratch_shapes=[pltpu.VMEM((TS, D), F32), pltpu.VMEM((TS, D), F32), pltpu.VMEM((TS, D), F32),
                        pltpu.VMEM((1, D), F32), pltpu.VMEM((1, D), F32)],
        compiler_params=_params(("arbitrary",)),
    )(xr, xr, xr, dh, h, h, cw, cb, wa, wx, ba, bx, lam)


def _od_in_bwd(dxcf, dxcb, xr, dg1, x1, dx1p, mod, w_in, cw):
    T = x1.shape[0]
    n = T // TM
    prev_spec, next_spec = _halo_specs(TM, D, n, T, False)

    def body(fp_ref, fc_ref, fn_ref, bp_ref, bc_ref, bn_ref, xp_ref, xc_ref, xn_ref, dg_ref, x1_ref, dxp_ref,
             mod_ref, w_ref, cw_ref, dx_ref, dw_ref, vec_ref):
        i = pl.program_id(0)

        @pl.when(i == 0)
        def _():
            dw_ref[...] = jnp.zeros_like(dw_ref)
            vec_ref[...] = jnp.zeros_like(vec_ref)

        dcur = fc_ref[...] + bc_ref[...]
        dprev = jnp.where(i > 0, fp_ref[...] + bp_ref[...], 0.0)
        dnext = jnp.where(i < n - 1, fn_ref[...] + bn_ref[...], 0.0)
        dext = jnp.concatenate([dprev, dcur, dnext], axis=0)
        xext = _ext_rows(xp_ref, xc_ref[...], xn_ref, i, n)
        cw_v = cw_ref[...]
        dxr = None
        for k in range(4):
            term = cw_v[k:k + 1, :] * _shift_rows(dext, 2 - k, TM)
            dxr = term if dxr is None else dxr + term
            vec_ref[k:k + 1, :] += _rowsum(dcur * _shift_rows(xext, k - 2, TM))
        vec_ref[4:5, :] += _rowsum(dcur)
        dp = jnp.concatenate([dxr.astype(BF16), dg_ref[...]], axis=-1)
        x1v = x1_ref[...]
        scale1 = 1.0 + mod_ref[1:2, :]
        h1 = (x1v * scale1 + mod_ref[0:1, :]).astype(BF16)
        dh1 = _dot_nt(dp, w_ref[...])
        dw_ref[...] += _dot_tn(h1, dp)
        dx_ref[...] = dxp_ref[...] + dh1 * scale1
        vec_ref[5:6, :] += _rowsum(dh1)
        vec_ref[6:7, :] += _rowsum(dh1 * x1v)

    t = _tile(TM, D)
    return _pallas(
        body, name="od_in_bwd", grid=(n,),
        in_specs=[prev_spec, t, next_spec, prev_spec, t, next_spec, prev_spec, t, next_spec, t, t, t,
                  _full((3, D)), _full((D, OD_IN)), _full((4, D))],
        out_specs=[t, _full((D, OD_IN)), _full((SUBLANE, D))],
        out_shape=[jax.ShapeDtypeStruct((T, D), F32), jax.ShapeDtypeStruct((D, OD_IN), F32),
                   jax.ShapeDtypeStruct((SUBLANE, D), F32)],
        compiler_params=_params(("arbitrary",)),
    )(dxcf, dxcf, dxcf, dxcb, dxcb, dxcb, xr, xr, xr, dg1, x1, dx1p, mod, w_in, cw)


def _ev_out_bwd(dx1, z0, out0, y0, ycat, g0, w_out, mod, lnp):
    T = dx1.shape[0]

    def body(dx_ref, z_ref, out_ref, y0_ref, yc_ref, g_ref, w_ref, mod_ref, ln_ref,
             dxp_ref, dyc_ref, dg_ref, dw_ref, vec_ref):
        i = pl.program_id(0)

        @pl.when(i == 0)
        def _():
            dw_ref[...] = jnp.zeros_like(dw_ref)
            vec_ref[...] = jnp.zeros_like(vec_ref)

        lng = ln_ref[0:1, :]
        _, xhat, rstd = _ln_fwd(z_ref[...], lng, ln_ref[1:2, :])
        dy = dx_ref[...]
        dz = _ln_bwd(dy, xhat, rstd, lng)
        vec_ref[0:1, :] += _rowsum(dy * xhat)
        vec_ref[1:2, :] += _rowsum(dy)
        vec_ref[2:3, :] += _rowsum(dz * out_ref[...].astype(F32))
        dout = (dz * mod_ref[2:3, :]).astype(BF16)
        dy0 = _dot_nt(dout, w_ref[...])
        dw_ref[...] += _dot_tn(y0_ref[...], dout)
        sg, dsg = _silu_and_grad(g_ref[...].astype(F32))
        dyc_ref[...] = (dy0 * sg).astype(BF16)
        dg_ref[...] = (dy0 * yc_ref[...].astype(F32) * dsg).astype(BF16)
        dxp_ref[...] = ALPHA * dz

    t = _tile(TM, D)
    return _pallas(
        body, name="ev_out_bwd", grid=(T // TM,),
        in_specs=[t, t, t, t, t, t, _full((D, D)), _full((3, D)), _full((2, D))],
        out_specs=[t, t, t, _full((D, D)), _full((SUBLANE, D))],
        out_shape=[jax.ShapeDtypeStruct((T, D), F32), jax.ShapeDtypeStruct((T, D), BF16),
                   jax.ShapeDtypeStruct((T, D), BF16), jax.ShapeDtypeStruct((D, D), F32),
                   jax.ShapeDtypeStruct((SUBLANE, D), F32)],
        compiler_params=_params(("arbitrary",)),
    )(dx1, z0, out0, y0, ycat, g0, w_out, mod, lnp)


def _mix0_bwd(q, kvp, lse, dyc, ycat, su, sv, sink, sg_lng, sg_lnb, sg_w, sg_bt, rc, rs1, rs2):
    T = q.shape[0]

    def body(q_ref, kv_ref, lse_ref, dyc_ref, yc_ref, su_ref, sv_ref, sink_ref, lng_ref, lnb_ref, w_ref, bt_ref,
             c_ref, s1_ref, s2_ref,
             dq_ref, dkv_ref, dsu_ref, dsv_ref, dw_ref, dbt_ref, vec_ref, dsink_ref):
        n = pl.program_id(0)

        @pl.when(n == 0)
        def _():
            dkv_ref[...] = jnp.zeros_like(dkv_ref)
            dw_ref[...] = jnp.zeros_like(dw_ref)
            dbt_ref[...] = jnp.zeros_like(dbt_ref)
            vec_ref[...] = jnp.zeros_like(vec_ref)
            dsink_ref[...] = jnp.zeros_like(dsink_ref)

        band = pl.ds(pl.multiple_of(n * BLK, BLK), 3 * BLK)
        kv = kv_ref[band, :]
        valid = _band_mask(n, T)
        qb = q_ref[...]
        dyc = dyc_ref[...]
        ycat = yc_ref[...]
        scale = HEAD_DIM ** -0.5
        dq_parts = []
        for kvh in range(2):
            kh = kvh * HEAD_DIM
            kk = kv[:, kh:kh + HEAD_DIM]
            vv = kv[:, KV_WIDTH + kh:KV_WIDTH + kh + HEAD_DIM]
            dk = jnp.zeros((3 * BLK, HEAD_DIM), F32)
            dv = jnp.zeros((3 * BLK, HEAD_DIM), F32)
            for h in range(4 * kvh, 4 * kvh + 4):
                sl = slice(h * HEAD_DIM, (h + 1) * HEAD_DIM)
                qh = qb[:, sl]
                do = dyc[:, sl]
                delta = jnp.sum(do.astype(F32) * ycat[:, sl].astype(F32), axis=-1, keepdims=True)
                lse = lse_ref[:, h:h + 1]
                s = jnp.where(valid, _dot_nt(qh, kk) * scale, NEG_INF)
                p = jnp.exp(s - lse)
                dp = _dot_nt(do, vv)
                ds = (p * (dp - delta)).astype(BF16)
                psink = jnp.exp(sink_ref[0:1, h:h + 1] - lse)
                dsink_ref[0:1, h:h + 1] += -jnp.sum(psink * delta, axis=0, keepdims=True)
                dq_parts.append(_dot(ds, kk) * scale)
                dk = dk + _dot_tn(ds, qh) * scale
                dv = dv + _dot_tn(p, do)
            dkv_ref[band, kh:kh + HEAD_DIM] += dk
            dkv_ref[band, KV_WIDTH + kh:KV_WIDTH + kh + HEAD_DIM] += dv
        c, s1, s2 = c_ref[...], s1_ref[...], s2_ref[...]
        for jb in range(ATTN_W // LANE):
            blk = jnp.concatenate(dq_parts[2 * jb:2 * jb + 2], axis=-1)
            dq_ref[:, jb * LANE:(jb + 1) * LANE] = _rope_bwd(blk, c, s1, s2).astype(BF16)

        svf = sv_ref[...].astype(F32)
        suf = su_ref[...].astype(F32)
        lng, lnb = lng_ref[...], lnb_ref[...]
        dsu_parts, dsv_parts, dlng_parts, dlnb_parts = [], [], [], []
        for g in range(SG_GROUPS):
            sl = slice(g * SG_DIM, (g + 1) * SG_DIM)
            vgn, xhat, rstd = _sg_norm(svf, g, lng, lnb)
            vgb = vgn.astype(BF16)
            wg = w_ref[g]
            svm = _dot(wg, vgb) + bt_ref[:, g:g + 1]
            dy = dyc[:, ATTN_W + g * SG_DIM:ATTN_W + (g + 1) * SG_DIM].astype(F32)
            dsu_parts.append(dy * svm)
            dsvm = dy * suf[:, sl]
            dbt_ref[:, g:g + 1] += jnp.sum(dsvm, axis=-1, keepdims=True)
            dsvb = dsvm.astype(BF16)
            dw_ref[g] += _dot_nt(dsvb, vgb)
            dvgn = _dot_tn(wg, dsvb)
            dlng_parts.append(_rowsum(dvgn * xhat))
            dlnb_parts.append(_rowsum(dvgn))
            dsv_parts.append(_ln_bwd(dvgn, xhat, rstd, lng[:, sl]))
        dsu_ref[...] = jnp.concatenate(dsu_parts, axis=-1).astype(BF16)
        dsv_ref[...] = jnp.concatenate(dsv_parts, axis=-1).astype(BF16)
        vec_ref[0:1, :] += jnp.concatenate(dlng_parts, axis=-1)
        vec_ref[1:2, :] += jnp.concatenate(dlnb_parts, axis=-1)

    return _pallas(
        body, name="mix0_bwd", grid=(T // BLK,),
        in_specs=[_tile(BLK, ATTN_W), _full((T + 2 * BLK, 2 * KV_WIDTH)), _tile(BLK, N_HEADS), _tile(BLK, D),
                  _tile(BLK, D), _tile(BLK, SG_W), _tile(BLK, SG_W), _full((1, N_HEADS)), _full((1, SG_W)),
                  _full((1, SG_W)), _full((SG_GROUPS, BLK, BLK)), _full((BLK, SG_GROUPS)),
                  _tile(BLK, LANE), _tile(BLK, LANE), _tile(BLK, LANE)],
        out_specs=[_tile(BLK, ATTN_W), _full((T + 2 * BLK, 2 * KV_WIDTH)), _tile(BLK, SG_W), _tile(BLK, SG_W),
                   _full((SG_GROUPS, BLK, BLK)), _full((BLK, SG_GROUPS)), _full((SUBLANE, SG_W)),
                   _full((1, N_HEADS))],
        out_shape=[jax.ShapeDtypeStruct((T, ATTN_W), BF16), jax.ShapeDtypeStruct((T + 2 * BLK, 2 * KV_WIDTH), F32),
                   jax.ShapeDtypeStruct((T, SG_W), BF16), jax.ShapeDtypeStruct((T, SG_W), BF16),
                   jax.ShapeDtypeStruct((SG_GROUPS, BLK, BLK), F32), jax.ShapeDtypeStruct((BLK, SG_GROUPS), F32),
                   jax.ShapeDtypeStruct((SUBLANE, SG_W), F32), jax.ShapeDtypeStruct((1, N_HEADS), F32)],
        compiler_params=_params(("arbitrary",)),
    )(q, kvp, lse, dyc, ycat, su, sv, sink, sg_lng, sg_lnb, sg_w, sg_bt, rc, rs1, rs2)


def _ev_in_bwd(dq, dkv, dsu, dsv, dg0, x, dxp, mod, w_in, rc, rs1, rs2):
    T = x.shape[0]

    def body(dq_ref, dkv_ref, dsu_ref, dsv_ref, dg_ref, x_ref, dxp_ref, mod_ref, w_ref, c_ref, s1_ref, s2_ref,
             dx_ref, dw_ref, vec_ref):
        i = pl.program_id(0)

        @pl.when(i == 0)
        def _():
            dw_ref[...] = jnp.zeros_like(dw_ref)
            vec_ref[...] = jnp.zeros_like(vec_ref)

        dkv = dkv_ref[...]
        dk = _rope_bwd(dkv[:, :KV_WIDTH], c_ref[...], s1_ref[...], s2_ref[...]).astype(BF16)
        dp = jnp.concatenate([dq_ref[...], dk, dkv[:, KV_WIDTH:].astype(BF16), dsu_ref[...], dsv_ref[...],
                              dg_ref[...]], axis=-1)
        xv = x_ref[...]
        scale0 = 1.0 + mod_ref[1:2, :]
        h0 = (xv * scale0 + mod_ref[0:1, :]).astype(BF16)
        dh0 = _dot_nt(dp, w_ref[...])
        dw_ref[...] += _dot_tn(h0, dp)
        dx_ref[...] = dxp_ref[...] + dh0 * scale0
        vec_ref[0:1, :] += _rowsum(dh0)
        vec_ref[1:2, :] += _rowsum(dh0 * xv)

    t = _tile(TM, D)
    return _pallas(
        body, name="ev_in_bwd", grid=(T // TM,),
        in_specs=[_tile(TM, ATTN_W), _tile(TM, 2 * KV_WIDTH), _tile(TM, SG_W), _tile(TM, SG_W), t, t, t,
                  _full((3, D)), _full((D, EV_IN)), _tile(TM, LANE), _tile(TM, LANE), _tile(TM, LANE)],
        out_specs=[t, _full((D, EV_IN)), _full((SUBLANE, D))],
        out_shape=[jax.ShapeDtypeStruct((T, D), F32), jax.ShapeDtypeStruct((D, EV_IN), F32),
                   jax.ShapeDtypeStruct((SUBLANE, D), F32)],
        compiler_params=_params(("arbitrary",)),
    )(dq, dkv, dsu, dsv, dg0, x, dxp, mod, w_in, rc, rs1, rs2)


def _sum_slots(land_ref):
    g = land_ref[0]
    for i in range(1, N_DEV):
        g = g + land_ref[i]
    return g


def _reduce_adam(land, w, m, v, name):
    R, C = w.shape
    rb = R
    for cand in (128, 64, 32, 16, 8):
        if R % cand == 0:
            rb = cand
            break

    def body(l_ref, w_ref, m_ref, v_ref, g_ref, d_ref, nm_ref, nv_ref):
        g = _sum_slots(l_ref)
        g_ref[...] = g
        dlt, m2, v2 = _adam(w_ref[...], g, m_ref[...], v_ref[...])
        d_ref[...] = dlt
        nm_ref[...] = m2
        nv_ref[...] = v2

    t = pl.BlockSpec((rb, C), lambda i: (i, 0))
    shp = jax.ShapeDtypeStruct((R, C), F32)
    return _pallas(
        body, name=name, grid=(R // rb,),
        in_specs=[pl.BlockSpec((N_DEV, rb, C), lambda i: (0, i, 0)), t, t, t],
        out_specs=[t] * 4, out_shape=[shp] * 4,
        compiler_params=_params(("parallel",)),
    )(land, w, m, v)


def _reduce_only(land, name):
    _, R, C = land.shape

    def body(l_ref, g_ref):
        g_ref[...] = _sum_slots(l_ref)

    return _pallas(
        body, name=name, grid=(1,),
        in_specs=[_full((N_DEV, R, C))], out_specs=_full((R, C)),
        out_shape=jax.ShapeDtypeStruct((R, C), F32),
        compiler_params=_params(("arbitrary",)),
    )(land)


def _adam_only(g, w, m, v, name):
    R, C = w.shape
    rb = R
    for cand in (128, 64, 32, 16, 8):
        if R % cand == 0:
            rb = cand
            break

    def body(g_ref, w_ref, m_ref, v_ref, d_ref, nm_ref, nv_ref):
        dlt, m2, v2 = _adam(w_ref[...], g_ref[...], m_ref[...], v_ref[...])
        d_ref[...] = dlt
        nm_ref[...] = m2
        nv_ref[...] = v2

    t = pl.BlockSpec((rb, C), lambda i: (i, 0))
    shp = jax.ShapeDtypeStruct((R, C), F32)
    return _pallas(
        body, name=name, grid=(R // rb,),
        in_specs=[t] * 4, out_specs=[t] * 3, out_shape=[shp] * 3,
        compiler_params=_params(("parallel",)),
    )(g, w, m, v)


REP_ROWS = 704
REP_LAYOUT = (
    ("ln_g", 2), ("ln_b", 2), ("ev_sg_ln_g", 1), ("ev_sg_ln_b", 1), ("ev_sink", 1), ("ev_sg_b", 1),
    ("ev_sg_w", 128), ("od_w_a", 256), ("od_w_x", 256))


def _pack_rep(parts):
    rows = []
    for name, nrows in REP_LAYOUT:
        flat = parts[name].reshape(-1)
        flat = jnp.pad(flat, (0, nrows * D - flat.shape[0]))
        rows.append(flat.reshape(nrows, D))
    used = sum(r for _, r in REP_LAYOUT)
    rows.append(jnp.zeros((REP_ROWS - used, D), F32))
    return jnp.concatenate(rows, axis=0)


def _unpack_rep(buf, shapes):
    out, r0 = {}, 0
    for name, nrows in REP_LAYOUT:
        size = 1
        for s in shapes[name]:
            size *= s
        out[name] = buf[r0:r0 + nrows].reshape(-1)[:size].reshape(shapes[name])
        r0 += nrows
    return out


VEC_ROWS = 16
VEC_LAYOUT = (("od_conv_w", 4), ("od_conv_b", 1), ("od_b_a", 2), ("od_b_x", 2), ("od_lam", 2))


def _pack_vec(parts):
    rows = [parts[name].reshape(nrows, -1) for name, nrows in VEC_LAYOUT]
    used = sum(r for _, r in VEC_LAYOUT)
    rows.append(jnp.zeros((VEC_ROWS - used, rows[0].shape[1]), F32))
    return jnp.concatenate(rows, axis=0)


def _unpack_vec(buf, shapes):
    out, r0 = {}, 0
    for name, nrows in VEC_LAYOUT:
        out[name] = buf[r0:r0 + nrows].reshape(shapes[name])
        r0 += nrows
    return out


def _to_slabs(full, cols_per):
    R = full.shape[0]
    return full.reshape(R, N_DEV, cols_per).transpose(1, 0, 2)


def _from_slabs(slabs):
    n, R, cp = slabs.shape
    return slabs.transpose(1, 0, 2).reshape(R, n * cp)


def kernel(x, c, positions, ada_w, ada_b, ln_g, ln_b, ev_w_in, ev_w_out, ev_sink, ev_sg_ln_g, ev_sg_ln_b, ev_sg_w, ev_sg_b, od_w_in, od_conv_w, od_conv_b, od_w_a, od_b_a, od_w_x, od_b_x, od_lam, od_w_out, loss_target, m_ada_w, m_ada_b, m_ln_g, m_ln_b, m_ev_w_in, m_ev_w_out, m_ev_sink, m_ev_sg_ln_g, m_ev_sg_ln_b, m_ev_sg_w, m_ev_sg_b, m_od_w_in, m_od_conv_w, m_od_conv_b, m_od_w_a, m_od_b_a, m_od_w_x, m_od_b_x, m_od_lam, m_od_w_out, v_ada_w, v_ada_b, v_ln_g, v_ln_b, v_ev_w_in, v_ev_w_out, v_ev_sink, v_ev_sg_ln_g, v_ev_sg_ln_b, v_ev_sg_w, v_ev_sg_b, v_od_w_in, v_od_conv_w, v_od_conv_b, v_od_w_a, v_od_b_a, v_od_w_x, v_od_b_x, v_od_lam, v_od_w_out):
    T = x.shape[1]
    me = _slot(*_my_pos())
    xs = x.reshape(T, D)
    tgt = loss_target.reshape(T, D)

    vec_w = _pack_vec(dict(od_conv_w=od_conv_w[0], od_conv_b=od_conv_b, od_b_a=od_b_a[0], od_b_x=od_b_x[0],
                           od_lam=od_lam[0]))
    c_all, g_ev_in, g_ev_out, g_od_in, g_od_out, g_vec = _all_gather(
        [c, ev_w_in[0].astype(BF16), ev_w_out[0].astype(BF16), od_w_in[0].astype(BF16),
         od_w_out[0].astype(BF16), vec_w], "ag_params")
    c_all = c_all.reshape(N_DEV, D)
    w_ev_in = _from_slabs(g_ev_in)
    w_ev_out = g_ev_out.reshape(D, D)
    w_od_in = _from_slabs(g_od_in)
    w_od_out = g_od_out.reshape(D, D)
    vec_full = _from_slabs(g_vec)
    cw, cb = vec_full[0:4], vec_full[4:5]
    ba, bx, lam = vec_full[5:7], vec_full[7:9], vec_full[9:11]

    mod_part = _mod_part(c_all, ada_w)
    (mod_all,) = _all_gather([mod_part], "ag_mod")
    mod_mine = lax.dynamic_index_in_dim(mod_all, me, axis=2, keepdims=False)
    mod = mod_mine.transpose(1, 0, 2).reshape(2, 3 * D) + ada_b
    mod0 = mod[0].reshape(3, D)
    mod1 = mod[1].reshape(3, D)

    half = 8
    inv_freq = jnp.power(jnp.float32(ROPE_THETA), -jnp.arange(half, dtype=F32) / half)
    ang = positions.reshape(T).astype(F32)[:, None] * inv_freq
    cos_t = jnp.tile(jnp.cos(ang), (1, LANE // half))
    sin_t = jnp.tile(jnp.sin(ang), (1, LANE // half))
    l64 = jnp.arange(LANE) % HEAD_DIM
    rc = jnp.where(l64 < 2 * half, cos_t, 1.0)
    rs1 = jnp.where(l64 < half, -sin_t, 0.0)
    rs2 = jnp.where((l64 >= half) & (l64 < 2 * half), sin_t, 0.0)

    ln0 = jnp.stack([ln_g[0], ln_b[0]])
    ln1 = jnp.stack([ln_g[1], ln_b[1]])
    sg_lng = ev_sg_ln_g
    sg_lnb = ev_sg_ln_b
    sg_w = ev_sg_w[0].astype(BF16)
    sg_bt = ev_sg_b[0].T
    wa = od_w_a[0].astype(BF16)
    wx = od_w_x[0].astype(BF16)

    q, kv, su, sv, g0 = _ev_in(xs, mod0, w_ev_in, rc, rs1, rs2)
    kvp = jnp.pad(kv, ((BLK, BLK), (0, 0)))
    ycat, y0, lse = _mix0_fwd(q, kvp, su, sv, g0, ev_sink, sg_lng, sg_lnb, sg_w, sg_bt)
    out0, z0, x1 = _ev_out(y0, w_ev_out, xs, mod0, ln0)
    xr, g1 = _od_in(x1, mod1, w_od_in)
    hf = _rglru_fwd(xr, cw, cb, wa[0], wx[0], ba[0:1], bx[0:1], lam[0:1], False, "rglru_fwd_f")
    hb = _rglru_fwd(xr, cw, cb, wa[1], wx[1], ba[1:2], bx[1:2], lam[1:2], True, "rglru_fwd_b")
    loss_v, dh, dg1, dx1p, d_od_out, vec_a = _od_out(hf, hb, g1, w_od_out, x1, tgt, mod1, ln1)

    dxcf, dwa_f, dwx_f, vec_f = _rglru_bwd(xr, dh, hf, cw, cb, wa[0], wx[0], ba[0:1], bx[0:1], lam[0:1],
                                           False, "rglru_bwd_f")
    dxcb, dwa_b, dwx_b, vec_b = _rglru_bwd(xr, dh, hb, cw, cb, wa[1], wx[1], ba[1:2], bx[1:2], lam[1:2],
                                           True, "rglru_bwd_b")
    dx1, d_od_in, vec_c = _od_in_bwd(dxcf, dxcb, xr, dg1, x1, dx1p, mod1, w_od_in, cw)
    dxp, dyc, dg0, d_ev_out, vec_d = _ev_out_bwd(dx1, z0, out0, y0, ycat, g0, w_ev_out, mod0, ln0)
    dq, dkv, dsu, dsv, d_sg_w, d_sg_bt, vec_e, d_sink = _mix0_bwd(
        q, kvp, lse, dyc, ycat, su, sv, ev_sink, sg_lng, sg_lnb, sg_w, sg_bt, rc, rs1, rs2)
    grad_x, d_ev_in, vec_g = _ev_in_bwd(dq, dkv[BLK:BLK + T], dsu, dsv, dg0, xs, dxp, mod0, w_ev_in, rc, rs1, rs2)

    loss = lax.psum(loss_v[0, 0], ("x", "y", "c"))

    dmod = jnp.stack([jnp.concatenate([vec_g[0], vec_g[1], vec_d[2]]),
                      jnp.concatenate([vec_c[5], vec_c[6], vec_a[2]])])
    (dmod_all,) = _all_gather([dmod], "ag_dmod")
    cols = ada_w.shape[2]
    dmod_cols = lax.dynamic_slice_in_dim(dmod_all, me * cols, cols, axis=2).transpose(1, 0, 2)
    (g_ada_w, d_ada_w, nm_ada_w, nv_ada_w, g_ada_b, d_ada_b, nm_ada_b, nv_ada_b) = _ada_update(
        c_all, dmod_cols, dmod_all, ada_w, m_ada_w, v_ada_w, ada_b, m_ada_b, v_ada_b)

    rep_g = _pack_rep(dict(
        ln_g=jnp.stack([vec_d[0], vec_a[0]]), ln_b=jnp.stack([vec_d[1], vec_a[1]]),
        ev_sg_ln_g=vec_e[0], ev_sg_ln_b=vec_e[1], ev_sink=d_sink, ev_sg_b=d_sg_bt.T, ev_sg_w=d_sg_w,
        od_w_a=jnp.stack([dwa_f, dwa_b]), od_w_x=jnp.stack([dwx_f, dwx_b])))
    vec_grads = _pack_vec(dict(
        od_conv_w=vec_c[0:4], od_conv_b=vec_c[4:5], od_b_a=jnp.stack([vec_f[0], vec_b[0]]),
        od_b_x=jnp.stack([vec_f[1], vec_b[1]]), od_lam=jnp.stack([vec_f[2], vec_b[2]])))
    l_ev_in, l_ev_out, l_od_in, l_od_out, l_vec, l_rep = _all_to_all(
        [_to_slabs(d_ev_in, EV_IN // N_DEV), d_ev_out.reshape(N_DEV, D // N_DEV, D),
         _to_slabs(d_od_in, OD_IN // N_DEV), d_od_out.reshape(N_DEV, D // N_DEV, D),
         _to_slabs(vec_grads, D // N_DEV), rep_g.reshape(N_DEV, REP_ROWS // N_DEV, D)], "rs_grads")

    r_ev_in = _reduce_adam(l_ev_in, ev_w_in[0], m_ev_w_in[0], v_ev_w_in[0], "adam_ev_in")
    r_ev_out = _reduce_adam(l_ev_out, ev_w_out[0], m_ev_w_out[0], v_ev_w_out[0], "adam_ev_out")
    r_od_in = _reduce_adam(l_od_in, od_w_in[0], m_od_w_in[0], v_od_w_in[0], "adam_od_in")
    r_od_out = _reduce_adam(l_od_out, od_w_out[0], m_od_w_out[0], v_od_w_out[0], "adam_od_out")
    vec_m = _pack_vec(dict(od_conv_w=m_od_conv_w[0], od_conv_b=m_od_conv_b, od_b_a=m_od_b_a[0],
                           od_b_x=m_od_b_x[0], od_lam=m_od_lam[0]))
    vec_v = _pack_vec(dict(od_conv_w=v_od_conv_w[0], od_conv_b=v_od_conv_b, od_b_a=v_od_b_a[0],
                           od_b_x=v_od_b_x[0], od_lam=v_od_lam[0]))
    r_vec = _reduce_adam(l_vec, vec_w, vec_m, vec_v, "adam_vec")

    rep_slab = _reduce_only(l_rep, "reduce_rep")
    (rep_all,) = _all_gather([rep_slab], "ag_rep")
    rep_grad = rep_all.reshape(REP_ROWS, D)
    rep_names = [nm for nm, _ in REP_LAYOUT]
    given = dict(ln_g=(ln_g, m_ln_g, v_ln_g), ln_b=(ln_b, m_ln_b, v_ln_b),
                 ev_sg_ln_g=(ev_sg_ln_g, m_ev_sg_ln_g, v_ev_sg_ln_g),
                 ev_sg_ln_b=(ev_sg_ln_b, m_ev_sg_ln_b, v_ev_sg_ln_b),
                 ev_sink=(ev_sink, m_ev_sink, v_ev_sink), ev_sg_b=(ev_sg_b, m_ev_sg_b, v_ev_sg_b),
                 ev_sg_w=(ev_sg_w, m_ev_sg_w, v_ev_sg_w), od_w_a=(od_w_a, m_od_w_a, v_od_w_a),
                 od_w_x=(od_w_x, m_od_w_x, v_od_w_x))
    rep_w = _pack_rep({k: given[k][0] for k in rep_names})
    rep_m = _pack_rep({k: given[k][1] for k in rep_names})
    rep_v = _pack_rep({k: given[k][2] for k in rep_names})
    rep_d, rep_nm, rep_nv = _adam_only(rep_grad, rep_w, rep_m, rep_v, "adam_rep")
    rep_shapes = {k: given[k][0].shape for k in rep_names}
    rep_out = [_unpack_rep(b, rep_shapes) for b in (rep_grad, rep_d, rep_nm, rep_nv)]

    vec_shapes = dict(od_conv_w=od_conv_w.shape, od_conv_b=od_conv_b.shape, od_b_a=od_b_a.shape,
                      od_b_x=od_b_x.shape, od_lam=od_lam.shape)
    vec_out = [_unpack_vec(b, vec_shapes) for b in r_vec]

    big = dict(ev_w_in=[a[None] for a in r_ev_in], ev_w_out=[a[None] for a in r_ev_out],
               od_w_in=[a[None] for a in r_od_in], od_w_out=[a[None] for a in r_od_out],
               ada_w=[g_ada_w, d_ada_w, nm_ada_w, nv_ada_w], ada_b=[g_ada_b, d_ada_b, nm_ada_b, nv_ada_b])

    order = ["ada_w", "ada_b", "ln_g", "ln_b", "ev_w_in", "ev_w_out", "ev_sink", "ev_sg_ln_g", "ev_sg_ln_b",
             "ev_sg_w", "ev_sg_b", "od_w_in", "od_conv_w", "od_conv_b", "od_w_a", "od_b_a", "od_w_x", "od_b_x",
             "od_lam", "od_w_out"]

    def pick(kind, name):
        if name in big:
            return big[name][kind]
        if name in vec_shapes:
            return vec_out[kind][name]
        return rep_out[kind][name]

    outs = [loss, grad_x.reshape(1, T, D)]
    for kind in range(4):
        outs += [pick(kind, name) for name in order]
    return tuple(outs)
```

```python
import functools

import jax
import jax.numpy as jnp
from jax import lax
from jax.experimental import pallas as pl
from jax.experimental.pallas import tpu as pltpu

F32 = jnp.float32
BF16 = jnp.bfloat16

N_DEV = 8
D = 1024
N_HEADS = 8
HEAD_DIM = 64
KV_WIDTH = 128
ATTN_W = 512
SG_W = 512
SG_GROUPS = 8
SG_DIM = 64
BLK = 128
EV_IN = 2816
OD_IN = 2048
RNN_HEADS = 8
RNN_HD = 128
ALPHA = 4.0 ** 0.25
LN_EPS = 1e-5
NEG_INF = -1e30
RG_C = 8.0
ROPE_THETA = 500000.0
LR, B1, B2, EPS, WD, STEP = 0.001, 0.9, 0.999, 1e-08, 0.01, 10

LANE = 128
SUBLANE = 8
TM = 256
TS = 256
VMEM_LIMIT = 56 * 1024 * 1024

MESH = pl.DeviceIdType.MESH


def _pallas(body, **kw):
    return pl.pallas_call(body, **kw)


def _params(sem, vmem=VMEM_LIMIT):
    return pltpu.CompilerParams(dimension_semantics=sem, vmem_limit_bytes=vmem)


def _sigmoid(x):
    return 1.0 / (1.0 + jnp.exp(-x))


def _silu_and_grad(x):
    s = _sigmoid(x)
    return x * s, s * (1.0 + x * (1.0 - s))


def _dot(a, b):
    return jnp.dot(a.astype(BF16), b.astype(BF16), preferred_element_type=F32)


def _dot_nt(a, b):
    return lax.dot_general(a.astype(BF16), b.astype(BF16), (((1,), (1,)), ((), ())), preferred_element_type=F32)


def _dot_tn(a, b):
    return lax.dot_general(a.astype(BF16), b.astype(BF16), (((0,), (0,)), ((), ())), preferred_element_type=F32)


def _ln_fwd(z, g, b):
    mu = jnp.mean(z, axis=-1, keepdims=True)
    zc = z - mu
    var = jnp.mean(zc * zc, axis=-1, keepdims=True)
    rstd = lax.rsqrt(var + LN_EPS)
    xhat = zc * rstd
    return xhat * g + b, xhat, rstd


def _ln_bwd(dy, xhat, rstd, g):
    dxh = dy * g
    m1 = jnp.mean(dxh, axis=-1, keepdims=True)
    m2 = jnp.mean(dxh * xhat, axis=-1, keepdims=True)
    return rstd * (dxh - m1 - xhat * m2)


def _rowsum(v):
    return jnp.sum(v, axis=0, keepdims=True)


def _rope_fwd(t, c, s1, s2):
    return t * c + pltpu.roll(t, LANE - 8, 1) * s1 + pltpu.roll(t, 8, 1) * s2


def _rope_bwd(d, c, s1, s2):
    return d * c + pltpu.roll(d * s1, 8, 1) + pltpu.roll(d * s2, LANE - 8, 1)


def _neg_expm1(x):
    poly = -x * (1.0 + x * (0.5 + x * (1.0 / 6.0 + x * (1.0 / 24.0))))
    return jnp.where(x > -0.03, poly, 1.0 - jnp.exp(x))


def _adam(w, g, m, v):
    m2 = B1 * m + (1.0 - B1) * g
    v2 = B2 * v + (1.0 - B2) * (g * g)
    m_hat = m2 / (1.0 - B1 ** STEP)
    v_hat = v2 / (1.0 - B2 ** STEP)
    delta = -LR * (m_hat / (jnp.sqrt(v_hat) + EPS) + WD * w)
    return delta, m2, v2


def _tile(rows, width):
    return pl.BlockSpec((rows, width), lambda i: (i, 0))


def _full(shape):
    zeros = (0,) * len(shape)
    return pl.BlockSpec(shape, lambda i: zeros)


def _rev_tile(rows, width, n, reverse):
    if reverse:
        return pl.BlockSpec((rows, width), lambda i: (n - 1 - i, 0))
    return pl.BlockSpec((rows, width), lambda i: (i, 0))


def _halo_specs(rows, width, n, total_rows, reverse):
    per = rows // SUBLANE
    last = total_rows // SUBLANE - 1

    def tile_of(i):
        return (n - 1 - i) if reverse else i

    prev = pl.BlockSpec((SUBLANE, width), lambda i: (jnp.maximum(tile_of(i) * per - 1, 0), 0))
    nxt = pl.BlockSpec((SUBLANE, width), lambda i: (jnp.minimum((tile_of(i) + 1) * per, last), 0))
    return prev, nxt


def _my_pos():
    return lax.axis_index("x"), lax.axis_index("y"), lax.axis_index("c")


def _slot(px, py, pc):
    return 4 * px + 2 * py + pc


def _all_gather(arrs, name):
    n = len(arrs)

    def body(*refs):
        ins, outs = refs[:n], refs[n:2 * n]
        send_sems, recv_sems, local_sems = refs[2 * n:]
        x, y, c = _my_pos()
        me, sibling = (x, y, c), (x, y, 1 - c)
        chips = [(1 - x, y), (x, 1 - y), (1 - x, 1 - y)]

        def copy(a, k, block, to, src=None):
            dst = outs[a].at[_slot(*block)]
            return pltpu.make_async_remote_copy(
                src_ref=dst if src is None else src, dst_ref=dst,
                send_sem=send_sems.at[a * 7 + k], recv_sem=recv_sems.at[a * 7 + k],
                device_id=to, device_id_type=MESH)

        local, first = [], []
        for a in range(n):
            lc = pltpu.make_async_copy(ins[a], outs[a].at[_slot(*me)], local_sems.at[a])
            lc.start()
            local.append(lc)
            first.append(copy(a, 0, me, sibling, src=ins[a]))
            first += [copy(a, 1 + j, me, (*chip, c), src=ins[a]) for j, chip in enumerate(chips)]
        for cp in first:
            cp.start()
        passed = []
        for j, chip in enumerate(chips):
            for a in range(n):
                copy(a, 1 + j, (*chip, c), me).wait_recv()
                fw = copy(a, 4 + j, (*chip, c), sibling)
                fw.start()
                passed.append(fw)
        for a in range(n):
            copy(a, 0, sibling, me).wait_recv()
            for j, chip in enumerate(chips):
                copy(a, 4 + j, (*chip, 1 - c), me).wait_recv()
        for cp in first + passed:
            cp.wait_send()
        for lc in local:
            lc.wait()

    any_spec = pl.BlockSpec(memory_space=pl.ANY)
    return _pallas(
        body, name=name,
        out_shape=[jax.ShapeDtypeStruct((N_DEV,) + a.shape, a.dtype) for a in arrs],
        in_specs=[any_spec] * n, out_specs=[any_spec] * n,
        scratch_shapes=[pltpu.SemaphoreType.DMA((7 * n,)), pltpu.SemaphoreType.DMA((7 * n,)),
                        pltpu.SemaphoreType.DMA((n,))],
    )(*arrs)


def _all_to_all(arrs, name):
    n = len(arrs)

    def body(*refs):
        ins, outs = refs[:n], refs[n:2 * n]
        send_sems, recv_sems, local_sems = refs[2 * n:]
        x, y, c = _my_pos()
        mine = _slot(x, y, c)
        copies = []
        for a in range(n):
            lc = pltpu.make_async_copy(ins[a].at[mine], outs[a].at[mine], local_sems.at[a])
            lc.start()
            copies.append(lc)
        for k in range(1, N_DEV):
            px = (1 - x) if (k & 4) else x
            py = (1 - y) if (k & 2) else y
            pc = (1 - c) if (k & 1) else c
            for a in range(n):
                cp = pltpu.make_async_remote_copy(
                    src_ref=ins[a].at[_slot(px, py, pc)], dst_ref=outs[a].at[mine],
                    send_sem=send_sems.at[a * 7 + k - 1], recv_sem=recv_sems.at[a * 7 + k - 1],
                    device_id=(px, py, pc), device_id_type=MESH)
                cp.start()
                copies.append(cp)
        for cp in copies:
            cp.wait()

    any_spec = pl.BlockSpec(memory_space=pl.ANY)
    return _pallas(
        body, name=name,
        out_shape=[jax.ShapeDtypeStruct(a.shape, a.dtype) for a in arrs],
        in_specs=[any_spec] * n, out_specs=[any_spec] * n,
        scratch_shapes=[pltpu.SemaphoreType.DMA((7 * n,)), pltpu.SemaphoreType.DMA((7 * n,)),
                        pltpu.SemaphoreType.DMA((n,))],
    )(*arrs)


class _GatherComm:
    has_mid = True

    def __init__(self, arrs):
        self.arrs = list(arrs)
        self.n = len(self.arrs)

    def out_shapes(self):
        return [jax.ShapeDtypeStruct((N_DEV,) + a.shape, a.dtype) for a in self.arrs]

    def sems(self):
        return [pltpu.SemaphoreType.DMA((7 * self.n,)), pltpu.SemaphoreType.DMA((7 * self.n,)),
                pltpu.SemaphoreType.DMA((self.n,))]

    def _parts(self, ins, outs, sems):
        send_sems, recv_sems, local_sems = sems
        x, y, c = _my_pos()
        me, sibling = (x, y, c), (x, y, 1 - c)
        chips = [(1 - x, y), (x, 1 - y), (1 - x, 1 - y)]

        def copy(a, k, block, to, src=None):
            dst = outs[a].at[_slot(*block)]
            return pltpu.make_async_remote_copy(
                src_ref=dst if src is None else src, dst_ref=dst,
                send_sem=send_sems.at[a * 7 + k], recv_sem=recv_sems.at[a * 7 + k],
                device_id=to, device_id_type=MESH)

        local = [pltpu.make_async_copy(ins[a], outs[a].at[_slot(*me)], local_sems.at[a]) for a in range(self.n)]
        first = []
        for a in range(self.n):
            first.append(copy(a, 0, me, sibling, src=ins[a]))
            first += [copy(a, 1 + j, me, (*chip, c), src=ins[a]) for j, chip in enumerate(chips)]
        ici_in = [copy(a, 1 + j, (*chip, c), me) for j, chip in enumerate(chips) for a in range(self.n)]
        passed = [copy(a, 4 + j, (*chip, c), sibling) for j, chip in enumerate(chips) for a in range(self.n)]
        d2d_in = []
        for a in range(self.n):
            d2d_in.append(copy(a, 0, sibling, me))
            d2d_in += [copy(a, 4 + j, (*chip, 1 - c), me) for j, chip in enumerate(chips)]
        return local, first, ici_in, passed, d2d_in

    def start(self, ins, outs, sems):
        local, first, _, _, _ = self._parts(ins, outs, sems)
        for cp in local + first:
            cp.start()

    def mid(self, ins, outs, sems):
        _, _, ici_in, passed, _ = self._parts(ins, outs, sems)
        for arrived, fw in zip(ici_in, passed):
            arrived.wait_recv()
            fw.start()

    def finish(self, ins, outs, sems):
        local, first, _, passed, d2d_in = self._parts(ins, outs, sems)
        for cp in d2d_in:
            cp.wait_recv()
        for cp in first + passed:
            cp.wait_send()
        for cp in local:
            cp.wait()


class _ExchangeComm:
    has_mid = False

    def __init__(self, arrs):
        self.arrs = list(arrs)
        self.n = len(self.arrs)

    def out_shapes(self):
        return [jax.ShapeDtypeStruct(a.shape, a.dtype) for a in self.arrs]

    def sems(self):
        return [pltpu.SemaphoreType.DMA((7 * self.n,)), pltpu.SemaphoreType.DMA((7 * self.n,)),
                pltpu.SemaphoreType.DMA((self.n,))]

    def _copies(self, ins, outs, sems):
        send_sems, recv_sems, local_sems = sems
        x, y, c = _my_pos()
        mine = _slot(x, y, c)
        copies = [pltpu.make_async_copy(ins[a].at[mine], outs[a].at[mine], local_sems.at[a]) for a in range(self.n)]
        for k in range(1, N_DEV):
            px = (1 - x) if (k & 4) else x
            py = (1 - y) if (k & 2) else y
            pc = (1 - c) if (k & 1) else c
            for a in range(self.n):
                copies.append(pltpu.make_async_remote_copy(
                    src_ref=ins[a].at[_slot(px, py, pc)], dst_ref=outs[a].at[mine],
                    send_sem=send_sems.at[a * 7 + k - 1], recv_sem=recv_sems.at[a * 7 + k - 1],
                    device_id=(px, py, pc), device_id_type=MESH))
        return copies

    def start(self, ins, outs, sems):
        for cp in self._copies(ins, outs, sems):
            cp.start()

    def finish(self, ins, outs, sems):
        for cp in self._copies(ins, outs, sems):
            cp.wait()


def _fused_call(body, comm, operands, *, name, grid, in_specs, out_specs, out_shape, scratch_shapes=(),
                semantics=("arbitrary",)):
    n_in, n_out, n_scr = len(in_specs), len(out_specs), len(scratch_shapes)
    if comm is None:
        res = _pallas(body, name=name, grid=grid, in_specs=list(in_specs), out_specs=list(out_specs),
                      out_shape=list(out_shape), scratch_shapes=list(scratch_shapes),
                      compiler_params=_params(semantics))(*operands)
        return list(res), []
    k = comm.n
    steps = grid[0]

    def wrapped(*refs):
        ins, cins = refs[:n_in], refs[n_in:n_in + k]
        outs = refs[n_in + k:n_in + k + n_out]
        couts = refs[n_in + k + n_out:n_in + 2 * k + n_out]
        rest = refs[n_in + 2 * k + n_out:]
        scratch, sems = rest[:n_scr], rest[n_scr:]
        i = pl.program_id(0)

        @pl.when(i == 0)
        def _():
            comm.start(cins, couts, sems)

        body(*ins, *outs, *scratch)

        if comm.has_mid:
            @pl.when(i == steps // 2)
            def _():
                comm.mid(cins, couts, sems)

        @pl.when(i == steps - 1)
        def _():
            comm.finish(cins, couts, sems)

    any_spec = pl.BlockSpec(memory_space=pl.ANY)
    res = _pallas(wrapped, name=name, grid=grid, in_specs=list(in_specs) + [any_spec] * k,
                  out_specs=list(out_specs) + [any_spec] * k, out_shape=list(out_shape) + comm.out_shapes(),
                  scratch_shapes=list(scratch_shapes) + comm.sems(),
                  compiler_params=_params(("arbitrary",)))(*operands, *comm.arrs)
    return list(res[:n_out]), list(res[n_out:])


def _mod_part(c_all, ada_w):
    cols = ada_w.shape[2]

    def body(c_ref, w_ref, o_ref):
        cv = c_ref[...]
        cond = cv * _sigmoid(cv)
        for l in range(2):
            o_ref[l] = _dot(cond, w_ref[l])

    return _pallas(
        body, name="mod_part", grid=(1,),
        in_specs=[_full((N_DEV, D)), _full((2, D, cols))],
        out_specs=_full((2, N_DEV, cols)),
        out_shape=jax.ShapeDtypeStruct((2, N_DEV, cols), F32),
        compiler_params=_params(("arbitrary",)),
    )(c_all, ada_w)


def _ada_update(c_all, dmod_cols, dmod_all, ada_w, m_w, v_w, ada_b, m_b, v_b):
    cols = ada_w.shape[2]
    nb = ada_b.shape[1]

    def body(c_ref, dmc_ref, dma_ref, w_ref, mw_ref, vw_ref, b_ref, mb_ref, vb_ref,
             gw_ref, dw_ref, nmw_ref, nvw_ref, gb_ref, db_ref, nmb_ref, nvb_ref):
        cv = c_ref[...]
        cond = cv * _sigmoid(cv)
        for l in range(2):
            g = _dot_tn(cond, dmc_ref[l])
            gw_ref[l] = g
            dlt, m2, v2 = _adam(w_ref[l], g, mw_ref[l], vw_ref[l])
            dw_ref[l] = dlt
            nmw_ref[l] = m2
            nvw_ref[l] = v2
        gb = dma_ref[0]
        for i in range(1, N_DEV):
            gb = gb + dma_ref[i]
        gb_ref[...] = gb
        dlt, m2, v2 = _adam(b_ref[...], gb, mb_ref[...], vb_ref[...])
        db_ref[...] = dlt
        nmb_ref[...] = m2
        nvb_ref[...] = v2

    wspec = _full((2, D, cols))
    bspec = _full((2, nb))
    wshape = jax.ShapeDtypeStruct((2, D, cols), F32)
    bshape = jax.ShapeDtypeStruct((2, nb), F32)
    return _pallas(
        body, name="ada_update", grid=(1,),
        in_specs=[_full((N_DEV, D)), _full((2, N_DEV, cols)), _full((N_DEV, 2, nb)),
                  wspec, wspec, wspec, bspec, bspec, bspec],
        out_specs=[wspec] * 4 + [bspec] * 4,
        out_shape=[wshape] * 4 + [bshape] * 4,
        compiler_params=_params(("arbitrary",)),
    )(c_all, dmod_cols, dmod_all, ada_w, m_w, v_w, ada_b, m_b, v_b)


def _ev_in(x, mod, w_in, rc, rs1, rs2, comm=None):
    T = x.shape[0]

    def body(x_ref, mod_ref, w_ref, c_ref, s1_ref, s2_ref, q_ref, kv_ref, su_ref, sv_ref, g_ref):
        h = x_ref[...] * (1.0 + mod_ref[1:2, :]) + mod_ref[0:1, :]
        p = _dot(h, w_ref[...])
        c, s1, s2 = c_ref[...], s1_ref[...], s2_ref[...]
        for j in range(ATTN_W // LANE):
            q_ref[:, j * LANE:(j + 1) * LANE] = _rope_fwd(p[:, j * LANE:(j + 1) * LANE], c, s1, s2).astype(BF16)
        kv_ref[:, 0:LANE] = _rope_fwd(p[:, 512:640], c, s1, s2).astype(BF16)
        kv_ref[:, LANE:2 * LANE] = p[:, 640:768].astype(BF16)
        su_ref[...] = p[:, 768:1280].astype(BF16)
        sv_ref[...] = p[:, 1280:1792].astype(BF16)
        g_ref[...] = p[:, 1792:2816].astype(BF16)

    sh = lambda w: jax.ShapeDtypeStruct((T, w), BF16)
    return _fused_call(
        body, comm, (x, mod, w_in, rc, rs1, rs2), name="ev_in", grid=(T // TM,),
        in_specs=[_tile(TM, D), _full((3, D)), _full((D, EV_IN)), _tile(TM, LANE), _tile(TM, LANE), _tile(TM, LANE)],
        out_specs=[_tile(TM, ATTN_W), _tile(TM, 2 * KV_WIDTH), _tile(TM, SG_W), _tile(TM, SG_W), _tile(TM, D)],
        out_shape=[sh(ATTN_W), sh(2 * KV_WIDTH), sh(SG_W), sh(SG_W), sh(D)], semantics=("parallel",))


def _band_mask(n, T):
    qi = lax.broadcasted_iota(jnp.int32, (BLK, 3 * BLK), 0)
    kj = lax.broadcasted_iota(jnp.int32, (BLK, 3 * BLK), 1)
    k_abs = n * BLK - BLK + kj
    return (jnp.abs(kj - BLK - qi) <= BLK) & (k_abs >= 0) & (k_abs < T)


def _sg_norm(sv, g, lng, lnb):
    vg = sv[:, g * SG_DIM:(g + 1) * SG_DIM]
    y, xhat, rstd = _ln_fwd(vg, lng[:, g * SG_DIM:(g + 1) * SG_DIM], lnb[:, g * SG_DIM:(g + 1) * SG_DIM])
    return y, xhat, rstd


def _mix0_fwd(q, kvp, su, sv, g0, sink, sg_lng, sg_lnb, sg_w, sg_bt, comm=None):
    T = q.shape[0]

    def body(q_ref, kv_ref, su_ref, sv_ref, g_ref, sink_ref, lng_ref, lnb_ref, w_ref, bt_ref,
             ycat_ref, y0_ref, lse_ref):
        n = pl.program_id(0)
        kv = kv_ref[pl.ds(pl.multiple_of(n * BLK, BLK), 3 * BLK), :]
        valid = _band_mask(n, T)
        qb = q_ref[...]
        outs = []
        for h in range(N_HEADS):
            kh = (h // 4) * HEAD_DIM
            s = _dot_nt(qb[:, h * HEAD_DIM:(h + 1) * HEAD_DIM], kv[:, kh:kh + HEAD_DIM]) * (HEAD_DIM ** -0.5)
            s = jnp.where(valid, s, NEG_INF)
            sk = sink_ref[0:1, h:h + 1]
            m = jnp.maximum(jnp.max(s, axis=-1, keepdims=True), sk)
            p = jnp.exp(s - m)
            denom = jnp.sum(p, axis=-1, keepdims=True) + jnp.exp(sk - m)
            outs.append(_dot(p / denom, kv[:, KV_WIDTH + kh:KV_WIDTH + kh + HEAD_DIM]))
            lse_ref[:, h:h + 1] = m + jnp.log(denom)
        svf = sv_ref[...].astype(F32)
        suf = su_ref[...].astype(F32)
        lng, lnb = lng_ref[...], lnb_ref[...]
        for g in range(SG_GROUPS):
            vgn, _, _ = _sg_norm(svf, g, lng, lnb)
            svm = _dot(w_ref[g], vgn) + bt_ref[:, g:g + 1]
            outs.append(suf[:, g * SG_DIM:(g + 1) * SG_DIM] * svm)
        ycat = jnp.concatenate(outs, axis=-1)
        gf = g_ref[...].astype(F32)
        ycat_ref[...] = ycat.astype(BF16)
        y0_ref[...] = (ycat * (gf * _sigmoid(gf))).astype(BF16)

    return _fused_call(
        body, comm, (q, kvp, su, sv, g0, sink, sg_lng, sg_lnb, sg_w, sg_bt), name="mix0_fwd", grid=(T // BLK,),
        in_specs=[_tile(BLK, ATTN_W), _full((T + 2 * BLK, 2 * KV_WIDTH)), _tile(BLK, SG_W), _tile(BLK, SG_W),
                  _tile(BLK, D), _full((1, N_HEADS)), _full((1, SG_W)), _full((1, SG_W)),
                  _full((SG_GROUPS, BLK, BLK)), _full((BLK, SG_GROUPS))],
        out_specs=[_tile(BLK, D), _tile(BLK, D), _tile(BLK, N_HEADS)],
        out_shape=[jax.ShapeDtypeStruct((T, D), BF16), jax.ShapeDtypeStruct((T, D), BF16),
                   jax.ShapeDtypeStruct((T, N_HEADS), F32)], semantics=("parallel",))


def _ev_out(y0, w_out, x, mod, lnp):
    T = x.shape[0]

    def body(y_ref, w_ref, x_ref, mod_ref, ln_ref, out_ref, z_ref, x1_ref):
        out = _dot(y_ref[...], w_ref[...])
        z = ALPHA * x_ref[...] + mod_ref[2:3, :] * out
        x1, _, _ = _ln_fwd(z, ln_ref[0:1, :], ln_ref[1:2, :])
        out_ref[...] = out.astype(BF16)
        z_ref[...] = z
        x1_ref[...] = x1

    return _pallas(
        body, name="ev_out", grid=(T // TM,),
        in_specs=[_tile(TM, D), _full((D, D)), _tile(TM, D), _full((3, D)), _full((2, D))],
        out_specs=[_tile(TM, D)] * 3,
        out_shape=[jax.ShapeDtypeStruct((T, D), BF16), jax.ShapeDtypeStruct((T, D), F32),
                   jax.ShapeDtypeStruct((T, D), F32)],
        compiler_params=_params(("parallel",)),
    )(y0, w_out, x, mod, lnp)


def _od_in(x1, mod, w_in):
    T = x1.shape[0]

    def body(x_ref, mod_ref, w_ref, xr_ref, g_ref):
        h = x_ref[...] * (1.0 + mod_ref[1:2, :]) + mod_ref[0:1, :]
        p = _dot(h, w_ref[...])
        xr_ref[...] = p[:, :D]
        g_ref[...] = p[:, D:].astype(BF16)

    return _pallas(
        body, name="od_in", grid=(T // TM,),
        in_specs=[_tile(TM, D), _full((3, D)), _full((D, OD_IN))],
        out_specs=[_tile(TM, D), _tile(TM, D)],
        out_shape=[jax.ShapeDtypeStruct((T, D), F32), jax.ShapeDtypeStruct((T, D), BF16)],
        compiler_params=_params(("parallel",)),
    )(x1, mod, w_in)


def _ext_rows(prev_ref, cur, next_ref, j, n):
    prev = jnp.where(j > 0, prev_ref[...], 0.0)
    nxt = jnp.where(j < n - 1, next_ref[...], 0.0)
    return jnp.concatenate([prev, cur, nxt], axis=0)


def _shift_rows(ext, off, rows):
    total = ext.shape[0]
    if off == 0:
        return ext[SUBLANE:SUBLANE + rows, :]
    return pltpu.roll(ext, (-off) % total, 0)[SUBLANE:SUBLANE + rows, :]


def _conv_fwd(ext, cw, cb, rows):
    xc = cb
    for k in range(4):
        xc = xc + cw[k:k + 1, :] * _shift_rows(ext, k - 2, rows)
    return xc


def _gates(xc, wa_ref, wx_ref, ba, bx, lam):
    pr, pi = [], []
    for h in range(RNN_HEADS):
        xh = xc[:, h * RNN_HD:(h + 1) * RNN_HD].astype(BF16)
        pr.append(_dot(xh, wa_ref[h]))
        pi.append(_dot(xh, wx_ref[h]))
    r = _sigmoid(jnp.concatenate(pr, axis=-1) + ba)
    ig = _sigmoid(jnp.concatenate(pi, axis=-1) + bx)
    sp = jnp.maximum(-lam, 0.0) + jnp.log(1.0 + jnp.exp(-jnp.abs(lam)))
    log_a = -RG_C * r * sp
    a = jnp.exp(log_a)
    s = jnp.sqrt(_neg_expm1(2.0 * log_a))
    return r, ig, sp, a, s


def _scan_tile(a_ref, b_ref, o_ref, carry_ref, rows, reverse):
    ridx = lax.broadcasted_iota(jnp.int32, (SUBLANE, D), 0)
    groups = rows // SUBLANE

    def group(gi, h):
        g = (groups - 1 - gi) if reverse else gi
        off = pl.multiple_of(g * SUBLANE, SUBLANE)
        a = a_ref[pl.ds(off, SUBLANE), :]
        b = b_ref[pl.ds(off, SUBLANE), :]
        for sh in (1, 2, 4):
            if reverse:
                keep = ridx < SUBLANE - sh
                a_p = jnp.where(keep, pltpu.roll(a, SUBLANE - sh, 0), 1.0)
                b_p = jnp.where(keep, pltpu.roll(b, SUBLANE - sh, 0), 0.0)
            else:
                keep = ridx >= sh
                a_p = jnp.where(keep, pltpu.roll(a, sh, 0), 1.0)
                b_p = jnp.where(keep, pltpu.roll(b, sh, 0), 0.0)
            b = b + a * b_p
            a = a * a_p
        hh = b + a * h
        o_ref[pl.ds(off, SUBLANE), :] = hh
        return hh[0:1, :] if reverse else hh[SUBLANE - 1:SUBLANE, :]

    carry_ref[...] = lax.fori_loop(0, groups, group, carry_ref[...])


def _rglru_fwd(xr, cw, cb, wa, wx, ba, bx, lam, reverse, name):
    T = xr.shape[0]
    n = T // TS
    prev_spec, next_spec = _halo_specs(TS, D, n, T, reverse)

    def body(prev_ref, cur_ref, next_ref, cw_ref, cb_ref, wa_ref, wx_ref, ba_ref, bx_ref, lam_ref,
             h_ref, a_s, b_s, carry):
        i = pl.program_id(0)
        j = (n - 1 - i) if reverse else i

        @pl.when(i == 0)
        def _():
            carry[...] = jnp.zeros_like(carry)

        ext = _ext_rows(prev_ref, cur_ref[...], next_ref, j, n)
        xc = _conv_fwd(ext, cw_ref[...], cb_ref[...], TS)
        _, ig, _, a, s = _gates(xc, wa_ref, wx_ref, ba_ref[...], bx_ref[...], lam_ref[...])
        a_s[...] = a
        b_s[...] = s * ig * xc
        _scan_tile(a_s, b_s, h_ref, carry, TS, reverse)

    wspec = _full((RNN_HEADS, RNN_HD, RNN_HD))
    return _pallas(
        body, name=name, grid=(n,),
        in_specs=[prev_spec, _rev_tile(TS, D, n, reverse), next_spec, _full((4, D)), _full((1, D)),
                  wspec, wspec, _full((1, D)), _full((1, D)), _full((1, D))],
        out_specs=_rev_tile(TS, D, n, reverse),
        out_shape=jax.ShapeDtypeStruct((T, D), F32),
        scratch_shapes=[pltpu.VMEM((TS, D), F32), pltpu.VMEM((TS, D), F32), pltpu.VMEM((1, D), F32)],
        compiler_params=_params(("arbitrary",)),
    )(xr, xr, xr, cw, cb, wa, wx, ba, bx, lam)


def _od_out(hf, hb, g1, w_out, x1, tgt, mod, lnp):
    T = x1.shape[0]

    def body(hf_ref, hb_ref, g_ref, w_ref, x_ref, t_ref, mod_ref, ln_ref,
             loss_ref, dh_ref, dg_ref, dx_ref, dw_ref, vec_ref):
        i = pl.program_id(0)

        @pl.when(i == 0)
        def _():
            loss_ref[...] = jnp.zeros_like(loss_ref)
            dw_ref[...] = jnp.zeros_like(dw_ref)
            vec_ref[...] = jnp.zeros_like(vec_ref)

        hs = hf_ref[...] + hb_ref[...]
        sg, dsg = _silu_and_grad(g_ref[...].astype(F32))
        yr = (hs * sg).astype(BF16)
        w = w_ref[...]
        out = _dot(yr, w)
        gate = mod_ref[2:3, :]
        z = ALPHA * x_ref[...] + gate * out
        lng = ln_ref[0:1, :]
        x2, xhat, rstd = _ln_fwd(z, lng, ln_ref[1:2, :])
        diff = x2 - t_ref[...]
        loss_ref[...] += 0.5 * jnp.sum(diff * diff) * (1.0 / D)
        dx2 = diff * (1.0 / D)
        dz = _ln_bwd(dx2, xhat, rstd, lng)
        vec_ref[0:1, :] += _rowsum(dx2 * xhat)
        vec_ref[1:2, :] += _rowsum(dx2)
        vec_ref[2:3, :] += _rowsum(dz * out)
        dout = (dz * gate).astype(BF16)
        dyr = _dot_nt(dout, w)
        dw_ref[...] += _dot_tn(yr, dout)
        dh_ref[...] = dyr * sg
        dg_ref[...] = (dyr * hs * dsg).astype(BF16)
        dx_ref[...] = ALPHA * dz

    return _pallas(
        body, name="od_out", grid=(T // TM,),
        in_specs=[_tile(TM, D), _tile(TM, D), _tile(TM, D), _full((D, D)), _tile(TM, D), _tile(TM, D),
                  _full((3, D)), _full((2, D))],
        out_specs=[_full((1, LANE)), _tile(TM, D), _tile(TM, D), _tile(TM, D), _full((D, D)), _full((SUBLANE, D))],
        out_shape=[jax.ShapeDtypeStruct((1, LANE), F32), jax.ShapeDtypeStruct((T, D), F32),
                   jax.ShapeDtypeStruct((T, D), BF16), jax.ShapeDtypeStruct((T, D), F32),
                   jax.ShapeDtypeStruct((D, D), F32), jax.ShapeDtypeStruct((SUBLANE, D), F32)],
        compiler_params=_params(("arbitrary",)),
    )(hf, hb, g1, w_out, x1, tgt, mod, lnp)


def _rglru_bwd(xr, dh, h, cw, cb, wa, wx, ba, bx, lam, reverse, name, comm=None):
    T = xr.shape[0]
    n = T // TS
    adj_rev = not reverse
    xprev_spec, xnext_spec = _halo_specs(TS, D, n, T, adj_rev)
    hprev_spec, hnext_spec = _halo_specs(TS, D, n, T, adj_rev)
    h_halo_spec = hnext_spec if reverse else hprev_spec

    def body(xprev_ref, xcur_ref, xnext_ref, dh_ref, h_ref, hh_ref, cw_ref, cb_ref, wa_ref, wx_ref,
             ba_ref, bx_ref, lam_ref, dxc_ref, dwa_ref, dwx_ref, vec_ref, a_s, b_s, l_s, carry, a_edge):
        i = pl.program_id(0)
        j = (n - 1 - i) if adj_rev else i

        @pl.when(i == 0)
        def _():
            carry[...] = jnp.zeros_like(carry)
            a_edge[...] = jnp.zeros_like(a_edge)
            dwa_ref[...] = jnp.zeros_like(dwa_ref)
            dwx_ref[...] = jnp.zeros_like(dwx_ref)
            vec_ref[...] = jnp.zeros_like(vec_ref)

        ext = _ext_rows(xprev_ref, xcur_ref[...], xnext_ref, j, n)
        xc = _conv_fwd(ext, cw_ref[...], cb_ref[...], TS)
        lam = lam_ref[...]
        r, ig, sp, a, s = _gates(xc, wa_ref, wx_ref, ba_ref[...], bx_ref[...], lam)

        rows = lax.broadcasted_iota(jnp.int32, (TS, D), 0)
        hcur = h_ref[...]
        if reverse:
            a_sh = jnp.where(rows == 0, a_edge[...], pltpu.roll(a, 1, 0))
            halo = jnp.where(j < n - 1, hh_ref[0:1, :], 0.0)
            h_nb = jnp.where(rows == TS - 1, halo, pltpu.roll(hcur, TS - 1, 0))
        else:
            a_sh = jnp.where(rows == TS - 1, a_edge[...], pltpu.roll(a, TS - 1, 0))
            halo = jnp.where(j > 0, hh_ref[SUBLANE - 1:SUBLANE, :], 0.0)
            h_nb = jnp.where(rows == 0, halo, pltpu.roll(hcur, 1, 0))
        a_s[...] = a_sh
        b_s[...] = dh_ref[...]
        _scan_tile(a_s, b_s, l_s, carry, TS, adj_rev)
        a_edge[...] = a[TS - 1:TS, :] if reverse else a[0:1, :]

        lm = l_s[...]
        da = lm * h_nb
        di = lm * s * xc
        dxc = lm * s * ig
        ds = lm * ig * xc
        dlog_a = a * da - ds * (a * a) / s
        dr = (-RG_C) * sp * dlog_a
        dsp = _rowsum((-RG_C) * r * dlog_a)
        dpr = dr * r * (1.0 - r)
        dpi = di * ig * (1.0 - ig)
        vec_ref[0:1, :] += _rowsum(dpr)
        vec_ref[1:2, :] += _rowsum(dpi)
        vec_ref[2:3, :] += dsp * (-_sigmoid(-lam))
        parts = []
        for hd in range(RNN_HEADS):
            sl = slice(hd * RNN_HD, (hd + 1) * RNN_HD)
            xh = xc[:, sl].astype(BF16)
            dprh = dpr[:, sl].astype(BF16)
            dpih = dpi[:, sl].astype(BF16)
            parts.append(_dot_nt(dprh, wa_ref[hd]) + _dot_nt(dpih, wx_ref[hd]))
            dwa_ref[hd] += _dot_tn(xh, dprh)
            dwx_ref[hd] += _dot_tn(xh, dpih)
        dxc_ref[...] = dxc + jnp.concatenate(parts, axis=-1)

    wspec = _full((RNN_HEADS, RNN_HD, RNN_HD))
    cur = _rev_tile(TS, D, n, adj_rev)
    return _fused_call(
        body, comm, (xr, xr, xr, dh, h, h, cw, cb, wa, wx, ba, bx, lam), name=name, grid=(n,),
        in_specs=[xprev_spec, cur, xnext_spec, cur, cur, h_halo_spec, _full((4, D)), _full((1, D)),
                  wspec, wspec, _full((1, D)), _full((1, D)), _full((1, D))],
        out_specs=[cur, wspec, wspec, _full((SUBLANE, D))],
        out_shape=[jax.ShapeDtypeStruct((T, D), F32),
                   jax.ShapeDtypeStruct((RNN_HEADS, RNN_HD, RNN_HD), F32),
                   jax.ShapeDtypeStruct((RNN_HEADS, RNN_HD, RNN_HD), F32),
                   jax.ShapeDtypeStruct((SUBLANE, D), F32)],
        scratch_shapes=[pltpu.VMEM((TS, D), F32), pltpu.VMEM((TS, D), F32), pltpu.VMEM((TS, D), F32),
                        pltpu.VMEM((1, D), F32), pltpu.VMEM((1, D), F32)])


def _od_in_bwd(dxcf, dxcb, xr, dg1, x1, dx1p, mod, w_in, cw):
    T = x1.shape[0]
    n = T // TM
    prev_spec, next_spec = _halo_specs(TM, D, n, T, False)

    def body(fp_ref, fc_ref, fn_ref, bp_ref, bc_ref, bn_ref, xp_ref, xc_ref, xn_ref, dg_ref, x1_ref, dxp_ref,
             mod_ref, w_ref, cw_ref, dx_ref, dw_ref, vec_ref):
        i = pl.program_id(0)

        @pl.when(i == 0)
        def _():
            dw_ref[...] = jnp.zeros_like(dw_ref)
            vec_ref[...] = jnp.zeros_like(vec_ref)

        dcur = fc_ref[...] + bc_ref[...]
        dprev = jnp.where(i > 0, fp_ref[...] + bp_ref[...], 0.0)
        dnext = jnp.where(i < n - 1, fn_ref[...] + bn_ref[...], 0.0)
        dext = jnp.concatenate([dprev, dcur, dnext], axis=0)
        xext = _ext_rows(xp_ref, xc_ref[...], xn_ref, i, n)
        cw_v = cw_ref[...]
        dxr = None
        for k in range(4):
            term = cw_v[k:k + 1, :] * _shift_rows(dext, 2 - k, TM)
            dxr = term if dxr is None else dxr + term
            vec_ref[k:k + 1, :] += _rowsum(dcur * _shift_rows(xext, k - 2, TM))
        vec_ref[4:5, :] += _rowsum(dcur)
        dp = jnp.concatenate([dxr.astype(BF16), dg_ref[...]], axis=-1)
        x1v = x1_ref[...]
        scale1 = 1.0 + mod_ref[1:2, :]
        h1 = (x1v * scale1 + mod_ref[0:1, :]).astype(BF16)
        dh1 = _dot_nt(dp, w_ref[...])
        dw_ref[...] += _dot_tn(h1, dp)
        dx_ref[...] = dxp_ref[...] + dh1 * scale1
        vec_ref[5:6, :] += _rowsum(dh1)
        vec_ref[6:7, :] += _rowsum(dh1 * x1v)

    t = _tile(TM, D)
    return _pallas(
        body, name="od_in_bwd", grid=(n,),
        in_specs=[prev_spec, t, next_spec, prev_spec, t, next_spec, prev_spec, t, next_spec, t, t, t,
                  _full((3, D)), _full((D, OD_IN)), _full((4, D))],
        out_specs=[t, _full((D, OD_IN)), _full((SUBLANE, D))],
        out_shape=[jax.ShapeDtypeStruct((T, D), F32), jax.ShapeDtypeStruct((D, OD_IN), F32),
                   jax.ShapeDtypeStruct((SUBLANE, D), F32)],
        compiler_params=_params(("arbitrary",)),
    )(dxcf, dxcf, dxcf, dxcb, dxcb, dxcb, xr, xr, xr, dg1, x1, dx1p, mod, w_in, cw)


def _ev_out_bwd(dx1, z0, out0, y0, ycat, g0, w_out, mod, lnp):
    T = dx1.shape[0]

    def body(dx_ref, z_ref, out_ref, y0_ref, yc_ref, g_ref, w_ref, mod_ref, ln_ref,
             dxp_ref, dyc_ref, dg_ref, dw_ref, vec_ref):
        i = pl.program_id(0)

        @pl.when(i == 0)
        def _():
            dw_ref[...] = jnp.zeros_like(dw_ref)
            vec_ref[...] = jnp.zeros_like(vec_ref)

        lng = ln_ref[0:1, :]
        _, xhat, rstd = _ln_fwd(z_ref[...], lng, ln_ref[1:2, :])
        dy = dx_ref[...]
        dz = _ln_bwd(dy, xhat, rstd, lng)
        vec_ref[0:1, :] += _rowsum(dy * xhat)
        vec_ref[1:2, :] += _rowsum(dy)
        vec_ref[2:3, :] += _rowsum(dz * out_ref[...].astype(F32))
        dout = (dz * mod_ref[2:3, :]).astype(BF16)
        dy0 = _dot_nt(dout, w_ref[...])
        dw_ref[...] += _dot_tn(y0_ref[...], dout)
        sg, dsg = _silu_and_grad(g_ref[...].astype(F32))
        dyc_ref[...] = (dy0 * sg).astype(BF16)
        dg_ref[...] = (dy0 * yc_ref[...].astype(F32) * dsg).astype(BF16)
        dxp_ref[...] = ALPHA * dz

    t = _tile(TM, D)
    return _pallas(
        body, name="ev_out_bwd", grid=(T // TM,),
        in_specs=[t, t, t, t, t, t, _full((D, D)), _full((3, D)), _full((2, D))],
        out_specs=[t, t, t, _full((D, D)), _full((SUBLANE, D))],
        out_shape=[jax.ShapeDtypeStruct((T, D), F32), jax.ShapeDtypeStruct((T, D), BF16),
                   jax.ShapeDtypeStruct((T, D), BF16), jax.ShapeDtypeStruct((D, D), F32),
                   jax.ShapeDtypeStruct((SUBLANE, D), F32)],
        compiler_params=_params(("arbitrary",)),
    )(dx1, z0, out0, y0, ycat, g0, w_out, mod, lnp)


def _mix0_bwd(q, kvp, lse, dyc, ycat, su, sv, sink, sg_lng, sg_lnb, sg_w, sg_bt, rc, rs1, rs2, comm=None):
    T = q.shape[0]

    def body(q_ref, kv_ref, lse_ref, dyc_ref, yc_ref, su_ref, sv_ref, sink_ref, lng_ref, lnb_ref, w_ref, bt_ref,
             c_ref, s1_ref, s2_ref,
             dq_ref, dkv_ref, dsu_ref, dsv_ref, dw_ref, dbt_ref, vec_ref, dsink_ref):
        n = pl.program_id(0)

        @pl.when(n == 0)
        def _():
            dkv_ref[...] = jnp.zeros_like(dkv_ref)
            dw_ref[...] = jnp.zeros_like(dw_ref)
            dbt_ref[...] = jnp.zeros_like(dbt_ref)
            vec_ref[...] = jnp.zeros_like(vec_ref)
            dsink_ref[...] = jnp.zeros_like(dsink_ref)

        band = pl.ds(pl.multiple_of(n * BLK, BLK), 3 * BLK)
        kv = kv_ref[band, :]
        valid = _band_mask(n, T)
        qb = q_ref[...]
        dyc = dyc_ref[...]
        ycat = yc_ref[...]
        scale = HEAD_DIM ** -0.5
        dq_parts = []
        for kvh in range(2):
            kh = kvh * HEAD_DIM
            kk = kv[:, kh:kh + HEAD_DIM]
            vv = kv[:, KV_WIDTH + kh:KV_WIDTH + kh + HEAD_DIM]
            dk = jnp.zeros((3 * BLK, HEAD_DIM), F32)
            dv = jnp.zeros((3 * BLK, HEAD_DIM), F32)
            for h in range(4 * kvh, 4 * kvh + 4):
                sl = slice(h * HEAD_DIM, (h + 1) * HEAD_DIM)
                qh = qb[:, sl]
                do = dyc[:, sl]
                delta = jnp.sum(do.astype(F32) * ycat[:, sl].astype(F32), axis=-1, keepdims=True)
                lse = lse_ref[:, h:h + 1]
                s = jnp.where(valid, _dot_nt(qh, kk) * scale, NEG_INF)
                p = jnp.exp(s - lse)
                dp = _dot_nt(do, vv)
                ds = (p * (dp - delta)).astype(BF16)
                psink = jnp.exp(sink_ref[0:1, h:h + 1] - lse)
                dsink_ref[0:1, h:h + 1] += -jnp.sum(psink * delta, axis=0, keepdims=True)
                dq_parts.append(_dot(ds, kk) * scale)
                dk = dk + _dot_tn(ds, qh) * scale
                dv = dv + _dot_tn(p, do)
            dkv_ref[band, kh:kh + HEAD_DIM] += dk
            dkv_ref[band, KV_WIDTH + kh:KV_WIDTH + kh + HEAD_DIM] += dv
        c, s1, s2 = c_ref[...], s1_ref[...], s2_ref[...]
        for jb in range(ATTN_W // LANE):
            blk = jnp.concatenate(dq_parts[2 * jb:2 * jb + 2], axis=-1)
            dq_ref[:, jb * LANE:(jb + 1) * LANE] = _rope_bwd(blk, c, s1, s2).astype(BF16)

        svf = sv_ref[...].astype(F32)
        suf = su_ref[...].astype(F32)
        lng, lnb = lng_ref[...], lnb_ref[...]
        dsu_parts, dsv_parts, dlng_parts, dlnb_parts = [], [], [], []
        for g in range(SG_GROUPS):
            sl = slice(g * SG_DIM, (g + 1) * SG_DIM)
            vgn, xhat, rstd = _sg_norm(svf, g, lng, lnb)
            vgb = vgn.astype(BF16)
            wg = w_ref[g]
            svm = _dot(wg, vgb) + bt_ref[:, g:g + 1]
            dy = dyc[:, ATTN_W + g * SG_DIM:ATTN_W + (g + 1) * SG_DIM].astype(F32)
            dsu_parts.append(dy * svm)
            dsvm = dy * suf[:, sl]
            dbt_ref[:, g:g + 1] += jnp.sum(dsvm, axis=-1, keepdims=True)
            dsvb = dsvm.astype(BF16)
            dw_ref[g] += _dot_nt(dsvb, vgb)
            dvgn = _dot_tn(wg, dsvb)
            dlng_parts.append(_rowsum(dvgn * xhat))
            dlnb_parts.append(_rowsum(dvgn))
            dsv_parts.append(_ln_bwd(dvgn, xhat, rstd, lng[:, sl]))
        dsu_ref[...] = jnp.concatenate(dsu_parts, axis=-1).astype(BF16)
        dsv_ref[...] = jnp.concatenate(dsv_parts, axis=-1).astype(BF16)
        vec_ref[0:1, :] += jnp.concatenate(dlng_parts, axis=-1)
        vec_ref[1:2, :] += jnp.concatenate(dlnb_parts, axis=-1)

    return _fused_call(
        body, comm, (q, kvp, lse, dyc, ycat, su, sv, sink, sg_lng, sg_lnb, sg_w, sg_bt, rc, rs1, rs2),
        name="mix0_bwd", grid=(T // BLK,),
        in_specs=[_tile(BLK, ATTN_W), _full((T + 2 * BLK, 2 * KV_WIDTH)), _tile(BLK, N_HEADS), _tile(BLK, D),
                  _tile(BLK, D), _tile(BLK, SG_W), _tile(BLK, SG_W), _full((1, N_HEADS)), _full((1, SG_W)),
                  _full((1, SG_W)), _full((SG_GROUPS, BLK, BLK)), _full((BLK, SG_GROUPS)),
                  _tile(BLK, LANE), _tile(BLK, LANE), _tile(BLK, LANE)],
        out_specs=[_tile(BLK, ATTN_W), _full((T + 2 * BLK, 2 * KV_WIDTH)), _tile(BLK, SG_W), _tile(BLK, SG_W),
                   _full((SG_GROUPS, BLK, BLK)), _full((BLK, SG_GROUPS)), _full((SUBLANE, SG_W)),
                   _full((1, N_HEADS))],
        out_shape=[jax.ShapeDtypeStruct((T, ATTN_W), BF16), jax.ShapeDtypeStruct((T + 2 * BLK, 2 * KV_WIDTH), F32),
                   jax.ShapeDtypeStruct((T, SG_W), BF16), jax.ShapeDtypeStruct((T, SG_W), BF16),
                   jax.ShapeDtypeStruct((SG_GROUPS, BLK, BLK), F32), jax.ShapeDtypeStruct((BLK, SG_GROUPS), F32),
                   jax.ShapeDtypeStruct((SUBLANE, SG_W), F32), jax.ShapeDtypeStruct((1, N_HEADS), F32)])


def _ev_in_bwd(dq, dkv, dsu, dsv, dg0, x, dxp, mod, w_in, rc, rs1, rs2, comm=None):
    T = x.shape[0]

    def body(dq_ref, dkv_ref, dsu_ref, dsv_ref, dg_ref, x_ref, dxp_ref, mod_ref, w_ref, c_ref, s1_ref, s2_ref,
             dx_ref, dw_ref, vec_ref):
        i = pl.program_id(0)

        @pl.when(i == 0)
        def _():
            dw_ref[...] = jnp.zeros_like(dw_ref)
            vec_ref[...] = jnp.zeros_like(vec_ref)

        dkv = dkv_ref[...]
        dk = _rope_bwd(dkv[:, :KV_WIDTH], c_ref[...], s1_ref[...], s2_ref[...]).astype(BF16)
        dp = jnp.concatenate([dq_ref[...], dk, dkv[:, KV_WIDTH:].astype(BF16), dsu_ref[...], dsv_ref[...],
                              dg_ref[...]], axis=-1)
        xv = x_ref[...]
        scale0 = 1.0 + mod_ref[1:2, :]
        h0 = (xv * scale0 + mod_ref[0:1, :]).astype(BF16)
        dh0 = _dot_nt(dp, w_ref[...])
        dw_ref[...] += _dot_tn(h0, dp)
        dx_ref[...] = dxp_ref[...] + dh0 * scale0
        vec_ref[0:1, :] += _rowsum(dh0)
        vec_ref[1:2, :] += _rowsum(dh0 * xv)

    t = _tile(TM, D)
    return _fused_call(
        body, comm, (dq, dkv, dsu, dsv, dg0, x, dxp, mod, w_in, rc, rs1, rs2), name="ev_in_bwd", grid=(T // TM,),
        in_specs=[_tile(TM, ATTN_W), _tile(TM, 2 * KV_WIDTH), _tile(TM, SG_W), _tile(TM, SG_W), t, t, t,
                  _full((3, D)), _full((D, EV_IN)), _tile(TM, LANE), _tile(TM, LANE), _tile(TM, LANE)],
        out_specs=[t, _full((D, EV_IN)), _full((SUBLANE, D))],
        out_shape=[jax.ShapeDtypeStruct((T, D), F32), jax.ShapeDtypeStruct((D, EV_IN), F32),
                   jax.ShapeDtypeStruct((SUBLANE, D), F32)])


def _sum_slots(land_ref):
    g = land_ref[0].astype(F32)
    for i in range(1, N_DEV):
        g = g + land_ref[i].astype(F32)
    return g


def _reduce_adam(land, w, m, v, name):
    R, C = w.shape
    rb = R
    for cand in (128, 64, 32, 16, 8):
        if R % cand == 0:
            rb = cand
            break

    def body(l_ref, w_ref, m_ref, v_ref, g_ref, d_ref, nm_ref, nv_ref):
        g = _sum_slots(l_ref)
        g_ref[...] = g
        dlt, m2, v2 = _adam(w_ref[...], g, m_ref[...], v_ref[...])
        d_ref[...] = dlt
        nm_ref[...] = m2
        nv_ref[...] = v2

    t = pl.BlockSpec((rb, C), lambda i: (i, 0))
    shp = jax.ShapeDtypeStruct((R, C), F32)
    return _pallas(
        body, name=name, grid=(R // rb,),
        in_specs=[pl.BlockSpec((N_DEV, rb, C), lambda i: (0, i, 0)), t, t, t],
        out_specs=[t] * 4, out_shape=[shp] * 4,
        compiler_params=_params(("parallel",)),
    )(land, w, m, v)


def _reduce_only(land, name):
    _, R, C = land.shape

    def body(l_ref, g_ref):
        g_ref[...] = _sum_slots(l_ref)

    return _pallas(
        body, name=name, grid=(1,),
        in_specs=[_full((N_DEV, R, C))], out_specs=_full((R, C)),
        out_shape=jax.ShapeDtypeStruct((R, C), F32),
        compiler_params=_params(("arbitrary",)),
    )(land)


def _adam_only(g, w, m, v, name):
    R, C = w.shape
    rb = R
    for cand in (128, 64, 32, 16, 8):
        if R % cand == 0:
            rb = cand
            break

    def body(g_ref, w_ref, m_ref, v_ref, d_ref, nm_ref, nv_ref):
        dlt, m2, v2 = _adam(w_ref[...], g_ref[...], m_ref[...], v_ref[...])
        d_ref[...] = dlt
        nm_ref[...] = m2
        nv_ref[...] = v2

    t = pl.BlockSpec((rb, C), lambda i: (i, 0))
    shp = jax.ShapeDtypeStruct((R, C), F32)
    return _pallas(
        body, name=name, grid=(R // rb,),
        in_specs=[t] * 4, out_specs=[t] * 3, out_shape=[shp] * 3,
        compiler_params=_params(("parallel",)),
    )(g, w, m, v)


REP_ROWS = 704
REP_LAYOUT = (
    ("ln_g", 2), ("ln_b", 2), ("ev_sg_ln_g", 1), ("ev_sg_ln_b", 1), ("ev_sink", 1), ("ev_sg_b", 1),
    ("ev_sg_w", 128), ("od_w_a", 256), ("od_w_x", 256))


def _pack_rep(parts):
    rows = []
    for name, nrows in REP_LAYOUT:
        flat = parts[name].reshape(-1)
        flat = jnp.pad(flat, (0, nrows * D - flat.shape[0]))
        rows.append(flat.reshape(nrows, D))
    used = sum(r for _, r in REP_LAYOUT)
    rows.append(jnp.zeros((REP_ROWS - used, D), F32))
    return jnp.concatenate(rows, axis=0)


def _unpack_rep(buf, shapes):
    out, r0 = {}, 0
    for name, nrows in REP_LAYOUT:
        size = 1
        for s in shapes[name]:
            size *= s
        out[name] = buf[r0:r0 + nrows].reshape(-1)[:size].reshape(shapes[name])
        r0 += nrows
    return out


VEC_ROWS = 16
VEC_LAYOUT = (("od_conv_w", 4), ("od_conv_b", 1), ("od_b_a", 2), ("od_b_x", 2), ("od_lam", 2))


def _pack_vec(parts):
    rows = [parts[name].reshape(nrows, -1) for name, nrows in VEC_LAYOUT]
    used = sum(r for _, r in VEC_LAYOUT)
    rows.append(jnp.zeros((VEC_ROWS - used, rows[0].shape[1]), F32))
    return jnp.concatenate(rows, axis=0)


def _unpack_vec(buf, shapes):
    out, r0 = {}, 0
    for name, nrows in VEC_LAYOUT:
        out[name] = buf[r0:r0 + nrows].reshape(shapes[name])
        r0 += nrows
    return out


def _to_slabs(full, cols_per):
    R = full.shape[0]
    return full.reshape(R, N_DEV, cols_per).transpose(1, 0, 2)


def _from_slabs(slabs):
    n, R, cp = slabs.shape
    return slabs.transpose(1, 0, 2).reshape(R, n * cp)


def kernel(x, c, positions, ada_w, ada_b, ln_g, ln_b, ev_w_in, ev_w_out, ev_sink, ev_sg_ln_g, ev_sg_ln_b, ev_sg_w, ev_sg_b, od_w_in, od_conv_w, od_conv_b, od_w_a, od_b_a, od_w_x, od_b_x, od_lam, od_w_out, loss_target, m_ada_w, m_ada_b, m_ln_g, m_ln_b, m_ev_w_in, m_ev_w_out, m_ev_sink, m_ev_sg_ln_g, m_ev_sg_ln_b, m_ev_sg_w, m_ev_sg_b, m_od_w_in, m_od_conv_w, m_od_conv_b, m_od_w_a, m_od_b_a, m_od_w_x, m_od_b_x, m_od_lam, m_od_w_out, v_ada_w, v_ada_b, v_ln_g, v_ln_b, v_ev_w_in, v_ev_w_out, v_ev_sink, v_ev_sg_ln_g, v_ev_sg_ln_b, v_ev_sg_w, v_ev_sg_b, v_od_w_in, v_od_conv_w, v_od_conv_b, v_od_w_a, v_od_b_a, v_od_w_x, v_od_b_x, v_od_lam, v_od_w_out):
    T = x.shape[1]
    me = _slot(*_my_pos())
    xs = x.reshape(T, D)
    tgt = loss_target.reshape(T, D)

    vec_w = _pack_vec(dict(od_conv_w=od_conv_w[0], od_conv_b=od_conv_b, od_b_a=od_b_a[0], od_b_x=od_b_x[0],
                           od_lam=od_lam[0]))
    c_all, g_ev_in, g_vec = _all_gather([c, ev_w_in[0].astype(BF16), vec_w], "ag_params")
    c_all = c_all.reshape(N_DEV, D)
    w_ev_in = _from_slabs(g_ev_in)
    vec_full = _from_slabs(g_vec)
    cw, cb = vec_full[0:4], vec_full[4:5]
    ba, bx, lam = vec_full[5:7], vec_full[7:9], vec_full[9:11]

    mod_part = _mod_part(c_all, ada_w)
    (mod_all,) = _all_gather([mod_part], "ag_mod")
    mod_mine = lax.dynamic_index_in_dim(mod_all, me, axis=2, keepdims=False)
    mod = mod_mine.transpose(1, 0, 2).reshape(2, 3 * D) + ada_b
    mod0 = mod[0].reshape(3, D)
    mod1 = mod[1].reshape(3, D)

    half = 8
    inv_freq = jnp.power(jnp.float32(ROPE_THETA), -jnp.arange(half, dtype=F32) / half)
    ang = positions.reshape(T).astype(F32)[:, None] * inv_freq
    cos_t = jnp.tile(jnp.cos(ang), (1, LANE // half))
    sin_t = jnp.tile(jnp.sin(ang), (1, LANE // half))
    l64 = jnp.arange(LANE) % HEAD_DIM
    rc = jnp.where(l64 < 2 * half, cos_t, 1.0)
    rs1 = jnp.where(l64 < half, -sin_t, 0.0)
    rs2 = jnp.where((l64 >= half) & (l64 < 2 * half), sin_t, 0.0)

    ln0 = jnp.stack([ln_g[0], ln_b[0]])
    ln1 = jnp.stack([ln_g[1], ln_b[1]])
    sg_lng = ev_sg_ln_g
    sg_lnb = ev_sg_ln_b
    sg_w = ev_sg_w[0].astype(BF16)
    sg_bt = ev_sg_b[0].T
    wa = od_w_a[0].astype(BF16)
    wx = od_w_x[0].astype(BF16)

    (q, kv, su, sv, g0), (g_ev_out,) = _ev_in(xs, mod0, w_ev_in, rc, rs1, rs2,
                                              _GatherComm([ev_w_out[0].astype(BF16)]))
    w_ev_out = g_ev_out.reshape(D, D)
    kvp = jnp.pad(kv, ((BLK, BLK), (0, 0)))
    (ycat, y0, lse), (g_od_in, g_od_out) = _mix0_fwd(
        q, kvp, su, sv, g0, ev_sink, sg_lng, sg_lnb, sg_w, sg_bt,
        _GatherComm([od_w_in[0].astype(BF16), od_w_out[0].astype(BF16)]))
    w_od_in = _from_slabs(g_od_in)
    w_od_out = g_od_out.reshape(D, D)
    out0, z0, x1 = _ev_out(y0, w_ev_out, xs, mod0, ln0)
    xr, g1 = _od_in(x1, mod1, w_od_in)
    hf = _rglru_fwd(xr, cw, cb, wa[0], wx[0], ba[0:1], bx[0:1], lam[0:1], False, "rglru_fwd_f")
    hb = _rglru_fwd(xr, cw, cb, wa[1], wx[1], ba[1:2], bx[1:2], lam[1:2], True, "rglru_fwd_b")
    loss_v, dh, dg1, dx1p, d_od_out, vec_a = _od_out(hf, hb, g1, w_od_out, x1, tgt, mod1, ln1)

    (dxcf, dwa_f, dwx_f, vec_f), (l_od_out,) = _rglru_bwd(
        xr, dh, hf, cw, cb, wa[0], wx[0], ba[0:1], bx[0:1], lam[0:1], False, "rglru_bwd_f",
        _ExchangeComm([d_od_out.astype(BF16).reshape(N_DEV, D // N_DEV, D)]))
    (dxcb, dwa_b, dwx_b, vec_b), _ = _rglru_bwd(xr, dh, hb, cw, cb, wa[1], wx[1], ba[1:2], bx[1:2], lam[1:2],
                                                True, "rglru_bwd_b")
    dx1, d_od_in, vec_c = _od_in_bwd(dxcf, dxcb, xr, dg1, x1, dx1p, mod1, w_od_in, cw)
    dxp, dyc, dg0, d_ev_out, vec_d = _ev_out_bwd(dx1, z0, out0, y0, ycat, g0, w_ev_out, mod0, ln0)
    (dq, dkv, dsu, dsv, d_sg_w, d_sg_bt, vec_e, d_sink), (l_od_in,) = _mix0_bwd(
        q, kvp, lse, dyc, ycat, su, sv, ev_sink, sg_lng, sg_lnb, sg_w, sg_bt, rc, rs1, rs2,
        _ExchangeComm([_to_slabs(d_od_in.astype(BF16), OD_IN // N_DEV)]))
    (grad_x, d_ev_in, vec_g), (l_ev_out,) = _ev_in_bwd(
        dq, dkv[BLK:BLK + T], dsu, dsv, dg0, xs, dxp, mod0, w_ev_in, rc, rs1, rs2,
        _ExchangeComm([d_ev_out.astype(BF16).reshape(N_DEV, D // N_DEV, D)]))

    loss = lax.psum(loss_v[0, 0], ("x", "y", "c"))

    dmod = jnp.stack([jnp.concatenate([vec_g[0], vec_g[1], vec_d[2]]),
                      jnp.concatenate([vec_c[5], vec_c[6], vec_a[2]])])
    (dmod_all,) = _all_gather([dmod], "ag_dmod")
    cols = ada_w.shape[2]
    dmod_cols = lax.dynamic_slice_in_dim(dmod_all, me * cols, cols, axis=2).transpose(1, 0, 2)
    (g_ada_w, d_ada_w, nm_ada_w, nv_ada_w, g_ada_b, d_ada_b, nm_ada_b, nv_ada_b) = _ada_update(
        c_all, dmod_cols, dmod_all, ada_w, m_ada_w, v_ada_w, ada_b, m_ada_b, v_ada_b)

    rep_g = _pack_rep(dict(
        ln_g=jnp.stack([vec_d[0], vec_a[0]]), ln_b=jnp.stack([vec_d[1], vec_a[1]]),
        ev_sg_ln_g=vec_e[0], ev_sg_ln_b=vec_e[1], ev_sink=d_sink, ev_sg_b=d_sg_bt.T, ev_sg_w=d_sg_w,
        od_w_a=jnp.stack([dwa_f, dwa_b]), od_w_x=jnp.stack([dwx_f, dwx_b])))
    vec_grads = _pack_vec(dict(
        od_conv_w=vec_c[0:4], od_conv_b=vec_c[4:5], od_b_a=jnp.stack([vec_f[0], vec_b[0]]),
        od_b_x=jnp.stack([vec_f[1], vec_b[1]]), od_lam=jnp.stack([vec_f[2], vec_b[2]])))
    l_ev_in, l_vec, l_rep = _all_to_all(
        [_to_slabs(d_ev_in.astype(BF16), EV_IN // N_DEV), _to_slabs(vec_grads, D // N_DEV),
         rep_g.reshape(N_DEV, REP_ROWS // N_DEV, D)], "rs_grads")

    r_ev_in = _reduce_adam(l_ev_in, ev_w_in[0], m_ev_w_in[0], v_ev_w_in[0], "adam_ev_in")
    r_ev_out = _reduce_adam(l_ev_out, ev_w_out[0], m_ev_w_out[0], v_ev_w_out[0], "adam_ev_out")
    r_od_in = _reduce_adam(l_od_in, od_w_in[0], m_od_w_in[0], v_od_w_in[0], "adam_od_in")
    r_od_out = _reduce_adam(l_od_out, od_w_out[0], m_od_w_out[0], v_od_w_out[0], "adam_od_out")
    vec_m = _pack_vec(dict(od_conv_w=m_od_conv_w[0], od_conv_b=m_od_conv_b, od_b_a=m_od_b_a[0],
                           od_b_x=m_od_b_x[0], od_lam=m_od_lam[0]))
    vec_v = _pack_vec(dict(od_conv_w=v_od_conv_w[0], od_conv_b=v_od_conv_b, od_b_a=v_od_b_a[0],
                           od_b_x=v_od_b_x[0], od_lam=v_od_lam[0]))
    r_vec = _reduce_adam(l_vec, vec_w, vec_m, vec_v, "adam_vec")

    rep_slab = _reduce_only(l_rep, "reduce_rep")
    (rep_all,) = _all_gather([rep_slab], "ag_rep")
    rep_grad = rep_all.reshape(REP_ROWS, D)
    rep_names = [nm for nm, _ in REP_LAYOUT]
    given = dict(ln_g=(ln_g, m_ln_g, v_ln_g), ln_b=(ln_b, m_ln_b, v_ln_b),
                 ev_sg_ln_g=(ev_sg_ln_g, m_ev_sg_ln_g, v_ev_sg_ln_g),
                 ev_sg_ln_b=(ev_sg_ln_b, m_ev_sg_ln_b, v_ev_sg_ln_b),
                 ev_sink=(ev_sink, m_ev_sink, v_ev_sink), ev_sg_b=(ev_sg_b, m_ev_sg_b, v_ev_sg_b),
                 ev_sg_w=(ev_sg_w, m_ev_sg_w, v_ev_sg_w), od_w_a=(od_w_a, m_od_w_a, v_od_w_a),
                 od_w_x=(od_w_x, m_od_w_x, v_od_w_x))
    rep_w = _pack_rep({k: given[k][0] for k in rep_names})
    rep_m = _pack_rep({k: given[k][1] for k in rep_names})
    rep_v = _pack_rep({k: given[k][2] for k in rep_names})
    rep_d, rep_nm, rep_nv = _adam_only(rep_grad, rep_w, rep_m, rep_v, "adam_rep")
    rep_shapes = {k: given[k][0].shape for k in rep_names}
    rep_out = [_unpack_rep(b, rep_shapes) for b in (rep_grad, rep_d, rep_nm, rep_nv)]

    vec_shapes = dict(od_conv_w=od_conv_w.shape, od_conv_b=od_conv_b.shape, od_b_a=od_b_a.shape,
                      od_b_x=od_b_x.shape, od_lam=od_lam.shape)
    vec_out = [_unpack_vec(b, vec_shapes) for b in r_vec]

    big = dict(ev_w_in=[a[None] for a in r_ev_in], ev_w_out=[a[None] for a in r_ev_out],
               od_w_in=[a[None] for a in r_od_in], od_w_out=[a[None] for a in r_od_out],
               ada_w=[g_ada_w, d_ada_w, nm_ada_w, nv_ada_w], ada_b=[g_ada_b, d_ada_b, nm_ada_b, nv_ada_b])

    order = ["ada_w", "ada_b", "ln_g", "ln_b", "ev_w_in", "ev_w_out", "ev_sink", "ev_sg_ln_g", "ev_sg_ln_b",
             "ev_sg_w", "ev_sg_b", "od_w_in", "od_conv_w", "od_conv_b", "od_w_a", "od_b_a", "od_w_x", "od_b_x",
             "od_lam", "od_w_out"]

    def pick(kind, name):
        if name in big:
            return big[name][kind]
        if name in vec_shapes:
            return vec_out[kind][name]
        return rep_out[kind][name]

    outs = [loss, grad_x.reshape(1, T, D)]
    for kind in range(4):
        outs += [pick(kind, name) for name in order]
    return tuple(outs)
```

```python
import functools

import jax
import jax.numpy as jnp
from jax import lax
from jax.experimental import pallas as pl
from jax.experimental.pallas import tpu as pltpu

F32 = jnp.float32
BF16 = jnp.bfloat16

N_DEV = 8
D = 1024
N_HEADS = 8
HEAD_DIM = 64
KV_WIDTH = 128
ATTN_W = 512
SG_W = 512
SG_GROUPS = 8
SG_DIM = 64
BLK = 128
KVX_W = 1024
EV_IN = 2816
OD_IN = 2048
RNN_HEADS = 8
RNN_HD = 128
ALPHA = 4.0 ** 0.25
LN_EPS = 1e-5
NEG_INF = -1e30
RG_C = 8.0
ROPE_THETA = 500000.0
LR, B1, B2, EPS, WD, STEP = 0.001, 0.9, 0.999, 1e-08, 0.01, 10

LANE = 128
SUBLANE = 8
TM = 256
TS = 256
VMEM_LIMIT = 56 * 1024 * 1024

MESH = pl.DeviceIdType.MESH


def _pallas(body, **kw):
    return pl.pallas_call(body, **kw)


def _params(sem, vmem=VMEM_LIMIT):
    return pltpu.CompilerParams(dimension_semantics=sem, vmem_limit_bytes=vmem)


def _sigmoid(x):
    return 1.0 / (1.0 + jnp.exp(-x))


def _silu_and_grad(x):
    s = _sigmoid(x)
    return x * s, s * (1.0 + x * (1.0 - s))


def _dot(a, b):
    return jnp.dot(a.astype(BF16), b.astype(BF16), preferred_element_type=F32)


def _dot_nt(a, b):
    return lax.dot_general(a.astype(BF16), b.astype(BF16), (((1,), (1,)), ((), ())), preferred_element_type=F32)


def _dot_tn(a, b):
    return lax.dot_general(a.astype(BF16), b.astype(BF16), (((0,), (0,)), ((), ())), preferred_element_type=F32)


def _ln_fwd(z, g, b):
    mu = jnp.mean(z, axis=-1, keepdims=True)
    zc = z - mu
    var = jnp.mean(zc * zc, axis=-1, keepdims=True)
    rstd = lax.rsqrt(var + LN_EPS)
    xhat = zc * rstd
    return xhat * g + b, xhat, rstd


def _ln_bwd(dy, xhat, rstd, g):
    dxh = dy * g
    m1 = jnp.mean(dxh, axis=-1, keepdims=True)
    m2 = jnp.mean(dxh * xhat, axis=-1, keepdims=True)
    return rstd * (dxh - m1 - xhat * m2)


def _rowsum(v):
    return jnp.sum(v, axis=0, keepdims=True)


def _rope_fwd(t, c, s1, s2):
    return t * c + pltpu.roll(t, LANE - 8, 1) * s1 + pltpu.roll(t, 8, 1) * s2


def _rope_bwd(d, c, s1, s2):
    return d * c + pltpu.roll(d * s1, 8, 1) + pltpu.roll(d * s2, LANE - 8, 1)


def _neg_expm1(x):
    poly = -x * (1.0 + x * (0.5 + x * (1.0 / 6.0 + x * (1.0 / 24.0))))
    return jnp.where(x > -0.03, poly, 1.0 - jnp.exp(x))


def _adam(w, g, m, v):
    m2 = B1 * m + (1.0 - B1) * g
    v2 = B2 * v + (1.0 - B2) * (g * g)
    m_hat = m2 / (1.0 - B1 ** STEP)
    v_hat = v2 / (1.0 - B2 ** STEP)
    delta = -LR * (m_hat / (jnp.sqrt(v_hat) + EPS) + WD * w)
    return delta, m2, v2


def _tile(rows, width):
    return pl.BlockSpec((rows, width), lambda i: (i, 0))


def _full(shape):
    zeros = (0,) * len(shape)
    return pl.BlockSpec(shape, lambda i: zeros)


def _rev_tile(rows, width, n, reverse):
    if reverse:
        return pl.BlockSpec((rows, width), lambda i: (n - 1 - i, 0))
    return pl.BlockSpec((rows, width), lambda i: (i, 0))


def _halo_specs(rows, width, n, total_rows, reverse):
    per = rows // SUBLANE
    last = total_rows // SUBLANE - 1

    def tile_of(i):
        return (n - 1 - i) if reverse else i

    prev = pl.BlockSpec((SUBLANE, width), lambda i: (jnp.maximum(tile_of(i) * per - 1, 0), 0))
    nxt = pl.BlockSpec((SUBLANE, width), lambda i: (jnp.minimum((tile_of(i) + 1) * per, last), 0))
    return prev, nxt


def _my_pos():
    return lax.axis_index("x"), lax.axis_index("y"), lax.axis_index("c")


def _slot(px, py, pc):
    return 4 * px + 2 * py + pc


def _all_gather(arrs, name):
    n = len(arrs)

    def body(*refs):
        ins, outs = refs[:n], refs[n:2 * n]
        send_sems, recv_sems, local_sems = refs[2 * n:]
        x, y, c = _my_pos()
        me, sibling = (x, y, c), (x, y, 1 - c)
        chips = [(1 - x, y), (x, 1 - y), (1 - x, 1 - y)]

        def copy(a, k, block, to, src=None):
            dst = outs[a].at[_slot(*block)]
            return pltpu.make_async_remote_copy(
                src_ref=dst if src is None else src, dst_ref=dst,
                send_sem=send_sems.at[a * 7 + k], recv_sem=recv_sems.at[a * 7 + k],
                device_id=to, device_id_type=MESH)

        local, first = [], []
        for a in range(n):
            lc = pltpu.make_async_copy(ins[a], outs[a].at[_slot(*me)], local_sems.at[a])
            lc.start()
            local.append(lc)
            first.append(copy(a, 0, me, sibling, src=ins[a]))
            first += [copy(a, 1 + j, me, (*chip, c), src=ins[a]) for j, chip in enumerate(chips)]
        for cp in first:
            cp.start()
        passed = []
        for j, chip in enumerate(chips):
            for a in range(n):
                copy(a, 1 + j, (*chip, c), me).wait_recv()
                fw = copy(a, 4 + j, (*chip, c), sibling)
                fw.start()
                passed.append(fw)
        for a in range(n):
            copy(a, 0, sibling, me).wait_recv()
            for j, chip in enumerate(chips):
                copy(a, 4 + j, (*chip, 1 - c), me).wait_recv()
        for cp in first + passed:
            cp.wait_send()
        for lc in local:
            lc.wait()

    any_spec = pl.BlockSpec(memory_space=pl.ANY)
    return _pallas(
        body, name=name,
        out_shape=[jax.ShapeDtypeStruct((N_DEV,) + a.shape, a.dtype) for a in arrs],
        in_specs=[any_spec] * n, out_specs=[any_spec] * n,
        scratch_shapes=[pltpu.SemaphoreType.DMA((7 * n,)), pltpu.SemaphoreType.DMA((7 * n,)),
                        pltpu.SemaphoreType.DMA((n,))],
    )(*arrs)


def _all_to_all(arrs, name):
    n = len(arrs)

    def body(*refs):
        ins, outs = refs[:n], refs[n:2 * n]
        send_sems, recv_sems, local_sems = refs[2 * n:]
        x, y, c = _my_pos()
        mine = _slot(x, y, c)
        copies = []
        for a in range(n):
            lc = pltpu.make_async_copy(ins[a].at[mine], outs[a].at[mine], local_sems.at[a])
            lc.start()
            copies.append(lc)
        for k in range(1, N_DEV):
            px = (1 - x) if (k & 4) else x
            py = (1 - y) if (k & 2) else y
            pc = (1 - c) if (k & 1) else c
            for a in range(n):
                cp = pltpu.make_async_remote_copy(
                    src_ref=ins[a].at[_slot(px, py, pc)], dst_ref=outs[a].at[mine],
                    send_sem=send_sems.at[a * 7 + k - 1], recv_sem=recv_sems.at[a * 7 + k - 1],
                    device_id=(px, py, pc), device_id_type=MESH)
                cp.start()
                copies.append(cp)
        for cp in copies:
            cp.wait()

    any_spec = pl.BlockSpec(memory_space=pl.ANY)
    return _pallas(
        body, name=name,
        out_shape=[jax.ShapeDtypeStruct(a.shape, a.dtype) for a in arrs],
        in_specs=[any_spec] * n, out_specs=[any_spec] * n,
        scratch_shapes=[pltpu.SemaphoreType.DMA((7 * n,)), pltpu.SemaphoreType.DMA((7 * n,)),
                        pltpu.SemaphoreType.DMA((n,))],
    )(*arrs)


class _GatherComm:
    has_mid = True

    def __init__(self, arrs):
        self.arrs = list(arrs)
        self.n = len(self.arrs)

    def out_shapes(self):
        return [jax.ShapeDtypeStruct((N_DEV,) + a.shape, a.dtype) for a in self.arrs]

    def sems(self):
        return [pltpu.SemaphoreType.DMA((7 * self.n,)), pltpu.SemaphoreType.DMA((7 * self.n,)),
                pltpu.SemaphoreType.DMA((self.n,))]

    def _parts(self, ins, outs, sems):
        send_sems, recv_sems, local_sems = sems
        x, y, c = _my_pos()
        me, sibling = (x, y, c), (x, y, 1 - c)
        chips = [(1 - x, y), (x, 1 - y), (1 - x, 1 - y)]

        def copy(a, k, block, to, src=None):
            dst = outs[a].at[_slot(*block)]
            return pltpu.make_async_remote_copy(
                src_ref=dst if src is None else src, dst_ref=dst,
                send_sem=send_sems.at[a * 7 + k], recv_sem=recv_sems.at[a * 7 + k],
                device_id=to, device_id_type=MESH)

        local = [pltpu.make_async_copy(ins[a], outs[a].at[_slot(*me)], local_sems.at[a]) for a in range(self.n)]
        first = []
        for a in range(self.n):
            first.append(copy(a, 0, me, sibling, src=ins[a]))
            first += [copy(a, 1 + j, me, (*chip, c), src=ins[a]) for j, chip in enumerate(chips)]
        ici_in = [copy(a, 1 + j, (*chip, c), me) for j, chip in enumerate(chips) for a in range(self.n)]
        passed = [copy(a, 4 + j, (*chip, c), sibling) for j, chip in enumerate(chips) for a in range(self.n)]
        d2d_in = []
        for a in range(self.n):
            d2d_in.append(copy(a, 0, sibling, me))
            d2d_in += [copy(a, 4 + j, (*chip, 1 - c), me) for j, chip in enumerate(chips)]
        return local, first, ici_in, passed, d2d_in

    def start(self, ins, outs, sems):
        local, first, _, _, _ = self._parts(ins, outs, sems)
        for cp in local + first:
            cp.start()

    def mid(self, ins, outs, sems):
        _, _, ici_in, passed, _ = self._parts(ins, outs, sems)
        for arrived, fw in zip(ici_in, passed):
            arrived.wait_recv()
            fw.start()

    def finish(self, ins, outs, sems):
        local, first, _, passed, d2d_in = self._parts(ins, outs, sems)
        for cp in d2d_in:
            cp.wait_recv()
        for cp in first + passed:
            cp.wait_send()
        for cp in local:
            cp.wait()


class _ExchangeComm:
    has_mid = False

    def __init__(self, arrs):
        self.arrs = list(arrs)
        self.n = len(self.arrs)

    def out_shapes(self):
        return [jax.ShapeDtypeStruct(a.shape, a.dtype) for a in self.arrs]

    def sems(self):
        return [pltpu.SemaphoreType.DMA((7 * self.n,)), pltpu.SemaphoreType.DMA((7 * self.n,)),
                pltpu.SemaphoreType.DMA((self.n,))]

    def _copies(self, ins, outs, sems):
        send_sems, recv_sems, local_sems = sems
        x, y, c = _my_pos()
        mine = _slot(x, y, c)
        copies = [pltpu.make_async_copy(ins[a].at[mine], outs[a].at[mine], local_sems.at[a]) for a in range(self.n)]
        for k in range(1, N_DEV):
            px = (1 - x) if (k & 4) else x
            py = (1 - y) if (k & 2) else y
            pc = (1 - c) if (k & 1) else c
            for a in range(self.n):
                copies.append(pltpu.make_async_remote_copy(
                    src_ref=ins[a].at[_slot(px, py, pc)], dst_ref=outs[a].at[mine],
                    send_sem=send_sems.at[a * 7 + k - 1], recv_sem=recv_sems.at[a * 7 + k - 1],
                    device_id=(px, py, pc), device_id_type=MESH))
        return copies

    def start(self, ins, outs, sems):
        for cp in self._copies(ins, outs, sems):
            cp.start()

    def finish(self, ins, outs, sems):
        for cp in self._copies(ins, outs, sems):
            cp.wait()


def _fused_call(body, comm, operands, *, name, grid, in_specs, out_specs, out_shape, scratch_shapes=(),
                semantics=("arbitrary",)):
    n_in, n_out, n_scr = len(in_specs), len(out_specs), len(scratch_shapes)
    if comm is None:
        res = _pallas(body, name=name, grid=grid, in_specs=list(in_specs), out_specs=list(out_specs),
                      out_shape=list(out_shape), scratch_shapes=list(scratch_shapes),
                      compiler_params=_params(semantics))(*operands)
        return list(res), []
    k = comm.n
    steps = grid[0]

    def wrapped(*refs):
        ins, cins = refs[:n_in], refs[n_in:n_in + k]
        outs = refs[n_in + k:n_in + k + n_out]
        couts = refs[n_in + k + n_out:n_in + 2 * k + n_out]
        rest = refs[n_in + 2 * k + n_out:]
        scratch, sems = rest[:n_scr], rest[n_scr:]
        i = pl.program_id(0)

        @pl.when(i == 0)
        def _():
            comm.start(cins, couts, sems)

        body(*ins, *outs, *scratch)

        if comm.has_mid:
            @pl.when(i == steps // 2)
            def _():
                comm.mid(cins, couts, sems)

        @pl.when(i == steps - 1)
        def _():
            comm.finish(cins, couts, sems)

    any_spec = pl.BlockSpec(memory_space=pl.ANY)
    res = _pallas(wrapped, name=name, grid=grid, in_specs=list(in_specs) + [any_spec] * k,
                  out_specs=list(out_specs) + [any_spec] * k, out_shape=list(out_shape) + comm.out_shapes(),
                  scratch_shapes=list(scratch_shapes) + comm.sems(),
                  compiler_params=_params(("arbitrary",)))(*operands, *comm.arrs)
    return list(res[:n_out]), list(res[n_out:])


def _mod_part(c_all, ada_w):
    cols = ada_w.shape[2]

    def body(c_ref, w_ref, o_ref):
        cv = c_ref[...]
        cond = cv * _sigmoid(cv)
        for l in range(2):
            o_ref[l] = _dot(cond, w_ref[l])

    return _pallas(
        body, name="mod_part", grid=(1,),
        in_specs=[_full((N_DEV, D)), _full((2, D, cols))],
        out_specs=_full((2, N_DEV, cols)),
        out_shape=jax.ShapeDtypeStruct((2, N_DEV, cols), F32),
        compiler_params=_params(("arbitrary",)),
    )(c_all, ada_w)


def _ada_update(c_all, dmod_cols, dmod_all, ada_w, m_w, v_w, ada_b, m_b, v_b):
    cols = ada_w.shape[2]
    nb = ada_b.shape[1]

    def body(c_ref, dmc_ref, dma_ref, w_ref, mw_ref, vw_ref, b_ref, mb_ref, vb_ref,
             gw_ref, dw_ref, nmw_ref, nvw_ref, gb_ref, db_ref, nmb_ref, nvb_ref):
        cv = c_ref[...]
        cond = cv * _sigmoid(cv)
        for l in range(2):
            g = _dot_tn(cond, dmc_ref[l])
            gw_ref[l] = g
            dlt, m2, v2 = _adam(w_ref[l], g, mw_ref[l], vw_ref[l])
            dw_ref[l] = dlt
            nmw_ref[l] = m2
            nvw_ref[l] = v2
        gb = dma_ref[0]
        for i in range(1, N_DEV):
            gb = gb + dma_ref[i]
        gb_ref[...] = gb
        dlt, m2, v2 = _adam(b_ref[...], gb, mb_ref[...], vb_ref[...])
        db_ref[...] = dlt
        nmb_ref[...] = m2
        nvb_ref[...] = v2

    wspec = _full((2, D, cols))
    bspec = _full((2, nb))
    wshape = jax.ShapeDtypeStruct((2, D, cols), F32)
    bshape = jax.ShapeDtypeStruct((2, nb), F32)
    return _pallas(
        body, name="ada_update", grid=(1,),
        in_specs=[_full((N_DEV, D)), _full((2, N_DEV, cols)), _full((N_DEV, 2, nb)),
                  wspec, wspec, wspec, bspec, bspec, bspec],
        out_specs=[wspec] * 4 + [bspec] * 4,
        out_shape=[wshape] * 4 + [bshape] * 4,
        compiler_params=_params(("arbitrary",)),
    )(c_all, dmod_cols, dmod_all, ada_w, m_w, v_w, ada_b, m_b, v_b)


def _ev_in(x, mod, w_in, rc, rs1, rs2, comm=None):
    T = x.shape[0]

    def body(x_ref, mod_ref, w_ref, c_ref, s1_ref, s2_ref, q_ref, kv_ref, su_ref, sv_ref, g_ref):
        h = x_ref[...] * (1.0 + mod_ref[1:2, :]) + mod_ref[0:1, :]
        p = _dot(h, w_ref[...])
        c, s1, s2 = c_ref[...], s1_ref[...], s2_ref[...]
        for j in range(ATTN_W // LANE):
            qr = _rope_fwd(p[:, j * LANE:(j + 1) * LANE], c, s1, s2)
            q_ref[:, j * LANE:(j + 1) * LANE] = (qr * (HEAD_DIM ** -0.5)).astype(BF16)
        low = lax.broadcasted_iota(jnp.int32, (TM, LANE), 1) < HEAD_DIM
        for j, val in enumerate((_rope_fwd(p[:, 512:640], c, s1, s2), p[:, 640:768])):
            swapped = pltpu.roll(val, HEAD_DIM, 1)
            tiles = (jnp.where(low, val, 0.0), jnp.where(low, 0.0, swapped),
                     jnp.where(low, swapped, 0.0), jnp.where(low, 0.0, val))
            for k, tile in enumerate(tiles):
                kv_ref[:, (4 * j + k) * LANE:(4 * j + k + 1) * LANE] = tile.astype(BF16)
        su_ref[...] = p[:, 768:1280].astype(BF16)
        sv_ref[...] = p[:, 1280:1792].astype(BF16)
        g_ref[...] = p[:, 1792:2816].astype(BF16)

    sh = lambda w: jax.ShapeDtypeStruct((T, w), BF16)
    return _fused_call(
        body, comm, (x, mod, w_in, rc, rs1, rs2), name="ev_in", grid=(T // TM,),
        in_specs=[_tile(TM, D), _full((3, D)), _full((D, EV_IN)), _tile(TM, LANE), _tile(TM, LANE), _tile(TM, LANE)],
        out_specs=[_tile(TM, ATTN_W), _tile(TM, KVX_W), _tile(TM, SG_W), _tile(TM, SG_W), _tile(TM, D)],
        out_shape=[sh(ATTN_W), sh(KVX_W), sh(SG_W), sh(SG_W), sh(D)], semantics=("parallel",))


def _band_specs(width, nb):
    return [pl.BlockSpec((BLK, width), lambda n: (jnp.maximum(n - 1, 0), 0)),
            pl.BlockSpec((BLK, width), lambda n: (n, 0)),
            pl.BlockSpec((BLK, width), lambda n: (jnp.minimum(n + 1, nb - 1), 0))]


def _band_bias(bias_ref, n, nb):
    rows = lax.broadcasted_iota(jnp.int32, (3 * BLK, 1), 0)
    outside = ((rows < BLK) & (n == 0)) | ((rows >= 2 * BLK) & (n == nb - 1))
    return bias_ref[...] + jnp.where(outside, NEG_INF, 0.0)


def _split_bf16(v):
    hi = v.astype(BF16)
    return hi, (v - hi.astype(F32)).astype(BF16)


def _group_mean(v, a_ref, exact_bf16=False):
    hi, lo = _split_bf16(v)
    a = a_ref[...]
    out = []
    for t in range(SG_W // LANE):
        sl = slice(t * LANE, (t + 1) * LANE)
        r = jnp.dot(hi[:, sl], a, preferred_element_type=F32)
        if not exact_bf16:
            r = r + jnp.dot(lo[:, sl], a, preferred_element_type=F32)
        out.append(r)
    return jnp.concatenate(out, axis=-1)


def _sg_core(sv_ref, lng, lnb, a_ref, w_ref, bfull_ref):
    svf = sv_ref[...].astype(F32)
    xc = svf - _group_mean(svf, a_ref, exact_bf16=True)
    rstd = lax.rsqrt(_group_mean(xc * xc, a_ref) + LN_EPS)
    xhat = xc * rstd
    vb = (xhat * lng + lnb).astype(BF16)
    low = lax.broadcasted_iota(jnp.int32, (BLK, LANE), 1) < SG_DIM
    tiles = []
    for t in range(SG_W // LANE):
        v2 = vb[:, t * LANE:(t + 1) * LANE]
        r0 = jnp.dot(w_ref[2 * t], v2, preferred_element_type=F32)
        r1 = jnp.dot(w_ref[2 * t + 1], v2, preferred_element_type=F32)
        tiles.append(jnp.where(low, r0, r1))
    svm = jnp.concatenate(tiles, axis=-1) + bfull_ref[...]
    return xhat, rstd, vb, svm


def _mix0_fwd(q, kvx, su, sv, g0, sink_l, bias, a128, sg_lng, sg_lnb, sg_w, sg_bfull, comm=None):
    T = q.shape[0]
    nb = T // BLK

    def body(q_ref, kp_ref, kc_ref, kn_ref, su_ref, sv_ref, g_ref, sink_ref, bias_ref, a_ref, lng_ref, lnb_ref,
             w_ref, bfull_ref, ycat_ref, y0_ref, lse_ref):
        n = pl.program_id(0)
        bias = _band_bias(bias_ref, n, nb)
        kvx = jnp.concatenate([kp_ref[...], kc_ref[...], kn_ref[...]], axis=0)
        tiles = []
        for t in range(ATTN_W // LANE):
            qt = q_ref[:, t * LANE:(t + 1) * LANE]
            acc = None
            for par in range(2):
                h = 2 * t + par
                kt = 2 * (h // 4) + par
                ke = kvx[:, kt * LANE:(kt + 1) * LANE]
                ve = kvx[:, (4 + kt) * LANE:(5 + kt) * LANE]
                st = _dot_nt(ke, qt) + bias
                sk = sink_ref[:, h * LANE:(h + 1) * LANE]
                m = jnp.maximum(jnp.max(st, axis=0, keepdims=True), sk)
                p = jnp.exp(st - m)
                denom = jnp.sum(p, axis=0, keepdims=True) + jnp.exp(sk - m)
                contrib = _dot_tn(p * (1.0 / denom), ve)
                acc = contrib if acc is None else acc + contrib
                lse_ref[0, :, h * LANE:(h + 1) * LANE] = m + jnp.log(denom)
            tiles.append(acc)
        _, _, _, svm = _sg_core(sv_ref, lng_ref[...], lnb_ref[...], a_ref, w_ref, bfull_ref)
        tiles.append(su_ref[...].astype(F32) * svm)
        ycat = jnp.concatenate(tiles, axis=-1)
        gf = g_ref[...].astype(F32)
        ycat_ref[...] = ycat.astype(BF16)
        y0_ref[...] = (ycat * (gf * _sigmoid(gf))).astype(BF16)

    return _fused_call(
        body, comm, (q, kvx, kvx, kvx, su, sv, g0, sink_l, bias, a128, sg_lng, sg_lnb, sg_w, sg_bfull),
        name="mix0_fwd", grid=(nb,),
        in_specs=[_tile(BLK, ATTN_W)] + _band_specs(KVX_W, nb) + [
            _tile(BLK, SG_W), _tile(BLK, SG_W), _tile(BLK, D), _full((1, N_HEADS * LANE)), _full((3 * BLK, LANE)),
            _full((LANE, LANE)), _full((1, SG_W)), _full((1, SG_W)), _full((SG_GROUPS, BLK, BLK)),
            _full((BLK, SG_W))],
        out_specs=[_tile(BLK, D), _tile(BLK, D), pl.BlockSpec((1, 1, N_HEADS * LANE), lambda n: (n, 0, 0))],
        out_shape=[jax.ShapeDtypeStruct((T, D), BF16), jax.ShapeDtypeStruct((T, D), BF16),
                   jax.ShapeDtypeStruct((nb, 1, N_HEADS * LANE), F32)], semantics=("parallel",))


def _ev_out(y0, w_out, x, mod, lnp):
    T = x.shape[0]

    def body(y_ref, w_ref, x_ref, mod_ref, ln_ref, out_ref, z_ref, x1_ref):
        out = _dot(y_ref[...], w_ref[...])
        z = ALPHA * x_ref[...] + mod_ref[2:3, :] * out
        x1, _, _ = _ln_fwd(z, ln_ref[0:1, :], ln_ref[1:2, :])
        out_ref[...] = out.astype(BF16)
        z_ref[...] = z
        x1_ref[...] = x1

    return _pallas(
        body, name="ev_out", grid=(T // TM,),
        in_specs=[_tile(TM, D), _full((D, D)), _tile(TM, D), _full((3, D)), _full((2, D))],
        out_specs=[_tile(TM, D)] * 3,
        out_shape=[jax.ShapeDtypeStruct((T, D), BF16), jax.ShapeDtypeStruct((T, D), F32),
                   jax.ShapeDtypeStruct((T, D), F32)],
        compiler_params=_params(("parallel",)),
    )(y0, w_out, x, mod, lnp)


def _od_in(x1, mod, w_in):
    T = x1.shape[0]

    def body(x_ref, mod_ref, w_ref, xr_ref, g_ref):
        h = x_ref[...] * (1.0 + mod_ref[1:2, :]) + mod_ref[0:1, :]
        p = _dot(h, w_ref[...])
        xr_ref[...] = p[:, :D]
        g_ref[...] = p[:, D:].astype(BF16)

    return _pallas(
        body, name="od_in", grid=(T // TM,),
        in_specs=[_tile(TM, D), _full((3, D)), _full((D, OD_IN))],
        out_specs=[_tile(TM, D), _tile(TM, D)],
        out_shape=[jax.ShapeDtypeStruct((T, D), F32), jax.ShapeDtypeStruct((T, D), BF16)],
        compiler_params=_params(("parallel",)),
    )(x1, mod, w_in)


def _ext_rows(prev_ref, cur, next_ref, j, n):
    prev = jnp.where(j > 0, prev_ref[...], 0.0)
    nxt = jnp.where(j < n - 1, next_ref[...], 0.0)
    return jnp.concatenate([prev, cur, nxt], axis=0)


def _shift_rows(ext, off, rows):
    total = ext.shape[0]
    if off == 0:
        return ext[SUBLANE:SUBLANE + rows, :]
    return pltpu.roll(ext, (-off) % total, 0)[SUBLANE:SUBLANE + rows, :]


def _conv_fwd(ext, cw, cb, rows):
    xc = cb
    for k in range(4):
        xc = xc + cw[k:k + 1, :] * _shift_rows(ext, k - 2, rows)
    return xc


def _gates(xc, wa_ref, wx_ref, ba, bx, lam):
    pr, pi = [], []
    for h in range(RNN_HEADS):
        xh = xc[:, h * RNN_HD:(h + 1) * RNN_HD].astype(BF16)
        pr.append(_dot(xh, wa_ref[h]))
        pi.append(_dot(xh, wx_ref[h]))
    r = _sigmoid(jnp.concatenate(pr, axis=-1) + ba)
    ig = _sigmoid(jnp.concatenate(pi, axis=-1) + bx)
    sp = jnp.maximum(-lam, 0.0) + jnp.log(1.0 + jnp.exp(-jnp.abs(lam)))
    log_a = -RG_C * r * sp
    a = jnp.exp(log_a)
    s = jnp.sqrt(_neg_expm1(2.0 * log_a))
    return r, ig, sp, a, s


def _scan_tile(a_ref, b_ref, o_ref, carry_ref, rows, reverse):
    ridx = lax.broadcasted_iota(jnp.int32, (SUBLANE, D), 0)
    groups = rows // SUBLANE

    def group(gi, h):
        g = (groups - 1 - gi) if reverse else gi
        off = pl.multiple_of(g * SUBLANE, SUBLANE)
        a = a_ref[pl.ds(off, SUBLANE), :]
        b = b_ref[pl.ds(off, SUBLANE), :]
        for sh in (1, 2, 4):
            if reverse:
                keep = ridx < SUBLANE - sh
                a_p = jnp.where(keep, pltpu.roll(a, SUBLANE - sh, 0), 1.0)
                b_p = jnp.where(keep, pltpu.roll(b, SUBLANE - sh, 0), 0.0)
            else:
                keep = ridx >= sh
                a_p = jnp.where(keep, pltpu.roll(a, sh, 0), 1.0)
                b_p = jnp.where(keep, pltpu.roll(b, sh, 0), 0.0)
            b = b + a * b_p
            a = a * a_p
        hh = b + a * h
        o_ref[pl.ds(off, SUBLANE), :] = hh
        return hh[0:1, :] if reverse else hh[SUBLANE - 1:SUBLANE, :]

    carry_ref[...] = lax.fori_loop(0, groups, group, carry_ref[...])


def _rglru_fwd(xr, cw, cb, wa, wx, ba, bx, lam, reverse, name):
    T = xr.shape[0]
    n = T // TS
    prev_spec, next_spec = _halo_specs(TS, D, n, T, reverse)

    def body(prev_ref, cur_ref, next_ref, cw_ref, cb_ref, wa_ref, wx_ref, ba_ref, bx_ref, lam_ref,
             h_ref, a_s, b_s, carry):
        i = pl.program_id(0)
        j = (n - 1 - i) if reverse else i

        @pl.when(i == 0)
        def _():
            carry[...] = jnp.zeros_like(carry)

        ext = _ext_rows(prev_ref, cur_ref[...], next_ref, j, n)
        xc = _conv_fwd(ext, cw_ref[...], cb_ref[...], TS)
        _, ig, _, a, s = _gates(xc, wa_ref, wx_ref, ba_ref[...], bx_ref[...], lam_ref[...])
        a_s[...] = a
        b_s[...] = s * ig * xc
        _scan_tile(a_s, b_s, h_ref, carry, TS, reverse)

    wspec = _full((RNN_HEADS, RNN_HD, RNN_HD))
    return _pallas(
        body, name=name, grid=(n,),
        in_specs=[prev_spec, _rev_tile(TS, D, n, reverse), next_spec, _full((4, D)), _full((1, D)),
                  wspec, wspec, _full((1, D)), _full((1, D)), _full((1, D))],
        out_specs=_rev_tile(TS, D, n, reverse),
        out_shape=jax.ShapeDtypeStruct((T, D), F32),
        scratch_shapes=[pltpu.VMEM((TS, D), F32), pltpu.VMEM((TS, D), F32), pltpu.VMEM((1, D), F32)],
        compiler_params=_params(("arbitrary",)),
    )(xr, xr, xr, cw, cb, wa, wx, ba, bx, lam)


def _od_out(hf, hb, g1, w_out, x1, tgt, mod, lnp):
    T = x1.shape[0]

    def body(hf_ref, hb_ref, g_ref, w_ref, x_ref, t_ref, mod_ref, ln_ref,
             loss_ref, dh_ref, dg_ref, dx_ref, dw_ref, vec_ref):
        i = pl.program_id(0)

        @pl.when(i == 0)
        def _():
            loss_ref[...] = jnp.zeros_like(loss_ref)
            dw_ref[...] = jnp.zeros_like(dw_ref)
            vec_ref[...] = jnp.zeros_like(vec_ref)

        hs = hf_ref[...] + hb_ref[...]
        sg, dsg = _silu_and_grad(g_ref[...].astype(F32))
        yr = (hs * sg).astype(BF16)
        w = w_ref[...]
        out = _dot(yr, w)
        gate = mod_ref[2:3, :]
        z = ALPHA * x_ref[...] + gate * out
        lng = ln_ref[0:1, :]
        x2, xhat, rstd = _ln_fwd(z, lng, ln_ref[1:2, :])
        diff = x2 - t_ref[...]
        loss_ref[...] += 0.5 * jnp.sum(diff * diff) * (1.0 / D)
        dx2 = diff * (1.0 / D)
        dz = _ln_bwd(dx2, xhat, rstd, lng)
        vec_ref[0:1, :] += _rowsum(dx2 * xhat)
        vec_ref[1:2, :] += _rowsum(dx2)
        vec_ref[2:3, :] += _rowsum(dz * out)
        dout = (dz * gate).astype(BF16)
        dyr = _dot_nt(dout, w)
        dw_ref[...] += _dot_tn(yr, dout)
        dh_ref[...] = dyr * sg
        dg_ref[...] = (dyr * hs * dsg).astype(BF16)
        dx_ref[...] = ALPHA * dz

    return _pallas(
        body, name="od_out", grid=(T // TM,),
        in_specs=[_tile(TM, D), _tile(TM, D), _tile(TM, D), _full((D, D)), _tile(TM, D), _tile(TM, D),
                  _full((3, D)), _full((2, D))],
        out_specs=[_full((1, LANE)), _tile(TM, D), _tile(TM, D), _tile(TM, D), _full((D, D)), _full((SUBLANE, D))],
        out_shape=[jax.ShapeDtypeStruct((1, LANE), F32), jax.ShapeDtypeStruct((T, D), F32),
                   jax.ShapeDtypeStruct((T, D), BF16), jax.ShapeDtypeStruct((T, D), F32),
                   jax.ShapeDtypeStruct((D, D), F32), jax.ShapeDtypeStruct((SUBLANE, D), F32)],
        compiler_params=_params(("arbitrary",)),
    )(hf, hb, g1, w_out, x1, tgt, mod, lnp)


def _rglru_bwd(xr, dh, h, cw, cb, wa, wx, ba, bx, lam, reverse, name, comm=None):
    T = xr.shape[0]
    n = T // TS
    adj_rev = not reverse
    xprev_spec, xnext_spec = _halo_specs(TS, D, n, T, adj_rev)
    hprev_spec, hnext_spec = _halo_specs(TS, D, n, T, adj_rev)
    h_halo_spec = hnext_spec if reverse else hprev_spec

    def body(xprev_ref, xcur_ref, xnext_ref, dh_ref, h_ref, hh_ref, cw_ref, cb_ref, wa_ref, wx_ref,
             ba_ref, bx_ref, lam_ref, dxc_ref, dwa_ref, dwx_ref, vec_ref, a_s, b_s, l_s, carry, a_edge):
        i = pl.program_id(0)
        j = (n - 1 - i) if adj_rev else i

        @pl.when(i == 0)
        def _():
            carry[...] = jnp.zeros_like(carry)
            a_edge[...] = jnp.zeros_like(a_edge)
            dwa_ref[...] = jnp.zeros_like(dwa_ref)
            dwx_ref[...] = jnp.zeros_like(dwx_ref)
            vec_ref[...] = jnp.zeros_like(vec_ref)

        ext = _ext_rows(xprev_ref, xcur_ref[...], xnext_ref, j, n)
        xc = _conv_fwd(ext, cw_ref[...], cb_ref[...], TS)
        lam = lam_ref[...]
        r, ig, sp, a, s = _gates(xc, wa_ref, wx_ref, ba_ref[...], bx_ref[...], lam)

        rows = lax.broadcasted_iota(jnp.int32, (TS, D), 0)
        hcur = h_ref[...]
        if reverse:
            a_sh = jnp.where(rows == 0, a_edge[...], pltpu.roll(a, 1, 0))
            halo = jnp.where(j < n - 1, hh_ref[0:1, :], 0.0)
            h_nb = jnp.where(rows == TS - 1, halo, pltpu.roll(hcur, TS - 1, 0))
        else:
            a_sh = jnp.where(rows == TS - 1, a_edge[...], pltpu.roll(a, TS - 1, 0))
            halo = jnp.where(j > 0, hh_ref[SUBLANE - 1:SUBLANE, :], 0.0)
            h_nb = jnp.where(rows == 0, halo, pltpu.roll(hcur, 1, 0))
        a_s[...] = a_sh
        b_s[...] = dh_ref[...]
        _scan_tile(a_s, b_s, l_s, carry, TS, adj_rev)
        a_edge[...] = a[TS - 1:TS, :] if reverse else a[0:1, :]

        lm = l_s[...]
        da = lm * h_nb
        di = lm * s * xc
        dxc = lm * s * ig
        ds = lm * ig * xc
        dlog_a = a * da - ds * (a * a) / s
        dr = (-RG_C) * sp * dlog_a
        dsp = _rowsum((-RG_C) * r * dlog_a)
        dpr = dr * r * (1.0 - r)
        dpi = di * ig * (1.0 - ig)
        vec_ref[0:1, :] += _rowsum(dpr)
        vec_ref[1:2, :] += _rowsum(dpi)
        vec_ref[2:3, :] += dsp * (-_sigmoid(-lam))
        parts = []
        for hd in range(RNN_HEADS):
            sl = slice(hd * RNN_HD, (hd + 1) * RNN_HD)
            xh = xc[:, sl].astype(BF16)
            dprh = dpr[:, sl].astype(BF16)
            dpih = dpi[:, sl].astype(BF16)
            parts.append(_dot_nt(dprh, wa_ref[hd]) + _dot_nt(dpih, wx_ref[hd]))
            dwa_ref[hd] += _dot_tn(xh, dprh)
            dwx_ref[hd] += _dot_tn(xh, dpih)
        dxc_ref[...] = dxc + jnp.concatenate(parts, axis=-1)

    wspec = _full((RNN_HEADS, RNN_HD, RNN_HD))
    cur = _rev_tile(TS, D, n, adj_rev)
    return _fused_call(
        body, comm, (xr, xr, xr, dh, h, h, cw, cb, wa, wx, ba, bx, lam), name=name, grid=(n,),
        in_specs=[xprev_spec, cur, xnext_spec, cur, cur, h_halo_spec, _full((4, D)), _full((1, D)),
                  wspec, wspec, _full((1, D)), _full((1, D)), _full((1, D))],
        out_specs=[cur, wspec, wspec, _full((SUBLANE, D))],
        out_shape=[jax.ShapeDtypeStruct((T, D), F32),
                   jax.ShapeDtypeStruct((RNN_HEADS, RNN_HD, RNN_HD), F32),
                   jax.ShapeDtypeStruct((RNN_HEADS, RNN_HD, RNN_HD), F32),
                   jax.ShapeDtypeStruct((SUBLANE, D), F32)],
        scratch_shapes=[pltpu.VMEM((TS, D), F32), pltpu.VMEM((TS, D), F32), pltpu.VMEM((TS, D), F32),
                        pltpu.VMEM((1, D), F32), pltpu.VMEM((1, D), F32)])


def _od_in_bwd(dxcf, dxcb, xr, dg1, x1, dx1p, mod, w_in, cw):
    T = x1.shape[0]
    n = T // TM
    prev_spec, next_spec = _halo_specs(TM, D, n, T, False)

    def body(fp_ref, fc_ref, fn_ref, bp_ref, bc_ref, bn_ref, xp_ref, xc_ref, xn_ref, dg_ref, x1_ref, dxp_ref,
             mod_ref, w_ref, cw_ref, dx_ref, dw_ref, vec_ref):
        i = pl.program_id(0)

        @pl.when(i == 0)
        def _():
            dw_ref[...] = jnp.zeros_like(dw_ref)
            vec_ref[...] = jnp.zeros_like(vec_ref)

        dcur = fc_ref[...] + bc_ref[...]
        dprev = jnp.where(i > 0, fp_ref[...] + bp_ref[...], 0.0)
        dnext = jnp.where(i < n - 1, fn_ref[...] + bn_ref[...], 0.0)
        dext = jnp.concatenate([dprev, dcur, dnext], axis=0)
        xext = _ext_rows(xp_ref, xc_ref[...], xn_ref, i, n)
        cw_v = cw_ref[...]
        dxr = None
        for k in range(4):
            term = cw_v[k:k + 1, :] * _shift_rows(dext, 2 - k, TM)
            dxr = term if dxr is None else dxr + term
            vec_ref[k:k + 1, :] += _rowsum(dcur * _shift_rows(xext, k - 2, TM))
        vec_ref[4:5, :] += _rowsum(dcur)
        dp = jnp.concatenate([dxr.astype(BF16), dg_ref[...]], axis=-1)
        x1v = x1_ref[...]
        scale1 = 1.0 + mod_ref[1:2, :]
        h1 = (x1v * scale1 + mod_ref[0:1, :]).astype(BF16)
        dh1 = _dot_nt(dp, w_ref[...])
        dw_ref[...] += _dot_tn(h1, dp)
        dx_ref[...] = dxp_ref[...] + dh1 * scale1
        vec_ref[5:6, :] += _rowsum(dh1)
        vec_ref[6:7, :] += _rowsum(dh1 * x1v)

    t = _tile(TM, D)
    return _pallas(
        body, name="od_in_bwd", grid=(n,),
        in_specs=[prev_spec, t, next_spec, prev_spec, t, next_spec, prev_spec, t, next_spec, t, t, t,
                  _full((3, D)), _full((D, OD_IN)), _full((4, D))],
        out_specs=[t, _full((D, OD_IN)), _full((SUBLANE, D))],
        out_shape=[jax.ShapeDtypeStruct((T, D), F32), jax.ShapeDtypeStruct((D, OD_IN), F32),
                   jax.ShapeDtypeStruct((SUBLANE, D), F32)],
        compiler_params=_params(("arbitrary",)),
    )(dxcf, dxcf, dxcf, dxcb, dxcb, dxcb, xr, xr, xr, dg1, x1, dx1p, mod, w_in, cw)


def _ev_out_bwd(dx1, z0, out0, y0, ycat, g0, w_out, mod, lnp):
    T = dx1.shape[0]

    def body(dx_ref, z_ref, out_ref, y0_ref, yc_ref, g_ref, w_ref, mod_ref, ln_ref,
             dxp_ref, dyc_ref, dg_ref, dw_ref, vec_ref):
        i = pl.program_id(0)

        @pl.when(i == 0)
        def _():
            dw_ref[...] = jnp.zeros_like(dw_ref)
            vec_ref[...] = jnp.zeros_like(vec_ref)

        lng = ln_ref[0:1, :]
        _, xhat, rstd = _ln_fwd(z_ref[...], lng, ln_ref[1:2, :])
        dy = dx_ref[...]
        dz = _ln_bwd(dy, xhat, rstd, lng)
        vec_ref[0:1, :] += _rowsum(dy * xhat)
        vec_ref[1:2, :] += _rowsum(dy)
        vec_ref[2:3, :] += _rowsum(dz * out_ref[...].astype(F32))
        dout = (dz * mod_ref[2:3, :]).astype(BF16)
        dy0 = _dot_nt(dout, w_ref[...])
        dw_ref[...] += _dot_tn(y0_ref[...], dout)
        sg, dsg = _silu_and_grad(g_ref[...].astype(F32))
        dyc_ref[...] = (dy0 * sg).astype(BF16)
        dg_ref[...] = (dy0 * yc_ref[...].astype(F32) * dsg).astype(BF16)
        dxp_ref[...] = ALPHA * dz

    t = _tile(TM, D)
    return _pallas(
        body, name="ev_out_bwd", grid=(T // TM,),
        in_specs=[t, t, t, t, t, t, _full((D, D)), _full((3, D)), _full((2, D))],
        out_specs=[t, t, t, _full((D, D)), _full((SUBLANE, D))],
        out_shape=[jax.ShapeDtypeStruct((T, D), F32), jax.ShapeDtypeStruct((T, D), BF16),
                   jax.ShapeDtypeStruct((T, D), BF16), jax.ShapeDtypeStruct((D, D), F32),
                   jax.ShapeDtypeStruct((SUBLANE, D), F32)],
        compiler_params=_params(("arbitrary",)),
    )(dx1, z0, out0, y0, ycat, g0, w_out, mod, lnp)


def _mix0_bwd(q, kvx, lse, dyc, ycat, su, sv, sink_l, bias, a128, gsum, sel, sg_lng, sg_lnb, sg_w, sg_bfull,
              rc, rs1, rs2, comm=None):
    T = q.shape[0]
    nb = T // BLK

    def body(q_ref, kp_ref, kc_ref, kn_ref, lse_ref, dyc_ref, yc_ref, su_ref, sv_ref, sink_ref, bias_ref, a_ref,
             gsum_ref, sel_ref, lng_ref, lnb_ref, w_ref, bfull_ref, c_ref, s1_ref, s2_ref,
             dq_ref, dkv_ref, dsu_ref, dsv_ref, dw_ref, dbt_ref, vec_ref, dsink_ref):
        n = pl.program_id(0)

        @pl.when(n == 0)
        def _():
            dkv_ref[...] = jnp.zeros_like(dkv_ref)
            dw_ref[...] = jnp.zeros_like(dw_ref)
            dbt_ref[...] = jnp.zeros_like(dbt_ref)
            vec_ref[...] = jnp.zeros_like(vec_ref)
            dsink_ref[...] = jnp.zeros_like(dsink_ref)

        band = pl.ds(pl.multiple_of(n * BLK, BLK), 3 * BLK)
        bias = _band_bias(bias_ref, n, nb)
        kvx = jnp.concatenate([kp_ref[...], kc_ref[...], kn_ref[...]], axis=0)
        low = lax.broadcasted_iota(jnp.int32, (BLK, LANE), 1) < HEAD_DIM
        sel = sel_ref[...]
        c, s1, s2 = c_ref[...], s1_ref[...], s2_ref[...]
        for kvh in range(2):
            dkx = jnp.zeros((3 * BLK, LANE), F32)
            dvx = jnp.zeros((3 * BLK, LANE), F32)
            for t in range(2 * kvh, 2 * kvh + 2):
                tl = slice(t * LANE, (t + 1) * LANE)
                qt = q_ref[:, tl]
                do = dyc_ref[:, tl]
                p_hi, p_lo = _split_bf16(do.astype(F32) * yc_ref[:, tl].astype(F32))
                deltas = _dot_nt(sel, p_hi) + _dot_nt(sel, p_lo)
                dq_acc = None
                for par in range(2):
                    h = 2 * t + par
                    hl = slice(h * LANE, (h + 1) * LANE)
                    kt = 2 * kvh + par
                    ke = kvx[:, kt * LANE:(kt + 1) * LANE]
                    ve = kvx[:, (4 + kt) * LANE:(5 + kt) * LANE]
                    lse = lse_ref[0, :, hl]
                    delta = deltas[par:par + 1, :]
                    pt = jnp.exp(_dot_nt(ke, qt) + bias - lse)
                    dst = (pt * (_dot_nt(ve, do) - delta)).astype(BF16)
                    dsink_ref[:, hl] += jnp.exp(sink_ref[:, hl] - lse) * delta
                    part = _dot_tn(dst, ke)
                    dq_acc = part if dq_acc is None else dq_acc + part
                    mine = low if par == 0 else jnp.logical_not(low)
                    dkx = dkx + jnp.dot(dst, jnp.where(mine, qt, jnp.zeros_like(qt)), preferred_element_type=F32)
                    dvx = dvx + jnp.dot(pt.astype(BF16), jnp.where(mine, do, jnp.zeros_like(do)),
                                        preferred_element_type=F32)
                dq_ref[:, tl] = _rope_bwd(dq_acc * (HEAD_DIM ** -0.5), c, s1, s2).astype(BF16)
            dkv_ref[band, kvh * LANE:(kvh + 1) * LANE] += dkx
            dkv_ref[band, (2 + kvh) * LANE:(3 + kvh) * LANE] += dvx

        lng = lng_ref[...]
        xhat, rstd, vb, svm = _sg_core(sv_ref, lng, lnb_ref[...], a_ref, w_ref, bfull_ref)
        dy = dyc_ref[:, ATTN_W:].astype(F32)
        dsu_ref[...] = (dy * svm).astype(BF16)
        dsvm = dy * su_ref[...].astype(F32)
        d_hi, d_lo = _split_bf16(dsvm)
        gsum = gsum_ref[...]
        dbt_ref[...] += jnp.dot(d_hi, gsum, preferred_element_type=F32) + jnp.dot(d_lo, gsum,
                                                                                 preferred_element_type=F32)
        tiles = []
        for t in range(SG_W // LANE):
            tl = slice(t * LANE, (t + 1) * LANE)
            dt, v2 = d_hi[:, tl], vb[:, tl]
            dw_ref[2 * t] += _dot_nt(jnp.where(low, dt, jnp.zeros_like(dt)), v2)
            dw_ref[2 * t + 1] += _dot_nt(jnp.where(low, jnp.zeros_like(dt), dt), v2)
            tiles.append(jnp.where(low, _dot_tn(w_ref[2 * t], dt), _dot_tn(w_ref[2 * t + 1], dt)))
        dvgn = jnp.concatenate(tiles, axis=-1)
        vec_ref[0:1, :] += _rowsum(dvgn * xhat)
        vec_ref[1:2, :] += _rowsum(dvgn)
        dxh = dvgn * lng
        m1 = _group_mean(dxh, a_ref)
        m2 = _group_mean(dxh * xhat, a_ref)
        dsv_ref[...] = (rstd * (dxh - m1 - xhat * m2)).astype(BF16)

    return _fused_call(
        body, comm, (q, kvx, kvx, kvx, lse, dyc, ycat, su, sv, sink_l, bias, a128, gsum, sel, sg_lng, sg_lnb, sg_w,
                     sg_bfull, rc, rs1, rs2),
        name="mix0_bwd", grid=(nb,),
        in_specs=[_tile(BLK, ATTN_W)] + _band_specs(KVX_W, nb) + [
            pl.BlockSpec((1, 1, N_HEADS * LANE), lambda n: (n, 0, 0)), _tile(BLK, D), _tile(BLK, D),
            _tile(BLK, SG_W), _tile(BLK, SG_W), _full((1, N_HEADS * LANE)), _full((3 * BLK, LANE)),
            _full((LANE, LANE)), _full((SG_W, LANE)), _full((SUBLANE, LANE)), _full((1, SG_W)), _full((1, SG_W)),
            _full((SG_GROUPS, BLK, BLK)), _full((BLK, SG_W)), _tile(BLK, LANE), _tile(BLK, LANE), _tile(BLK, LANE)],
        out_specs=[_tile(BLK, ATTN_W), _full((T + 2 * BLK, 4 * LANE)), _tile(BLK, SG_W), _tile(BLK, SG_W),
                   _full((SG_GROUPS, BLK, BLK)), _full((BLK, LANE)), _full((SUBLANE, SG_W)),
                   _full((1, N_HEADS * LANE))],
        out_shape=[jax.ShapeDtypeStruct((T, ATTN_W), BF16), jax.ShapeDtypeStruct((T + 2 * BLK, 4 * LANE), F32),
                   jax.ShapeDtypeStruct((T, SG_W), BF16), jax.ShapeDtypeStruct((T, SG_W), BF16),
                   jax.ShapeDtypeStruct((SG_GROUPS, BLK, BLK), F32), jax.ShapeDtypeStruct((BLK, LANE), F32),
                   jax.ShapeDtypeStruct((SUBLANE, SG_W), F32), jax.ShapeDtypeStruct((1, N_HEADS * LANE), F32)])


def _ev_in_bwd(dq, dkv, dsu, dsv, dg0, x, dxp, mod, w_in, rc, rs1, rs2, comm=None):
    T = x.shape[0]

    def body(dq_ref, dkv_ref, dsu_ref, dsv_ref, dg_ref, x_ref, dxp_ref, mod_ref, w_ref, c_ref, s1_ref, s2_ref,
             dx_ref, dw_ref, vec_ref):
        i = pl.program_id(0)

        @pl.when(i == 0)
        def _():
            dw_ref[...] = jnp.zeros_like(dw_ref)
            vec_ref[...] = jnp.zeros_like(vec_ref)

        low = lax.broadcasted_iota(jnp.int32, (TM, LANE), 1) < HEAD_DIM

        def fold(j):
            t0 = dkv_ref[:, (2 * j) * LANE:(2 * j + 1) * LANE]
            t1 = dkv_ref[:, (2 * j + 1) * LANE:(2 * j + 2) * LANE]
            return jnp.where(low, t0 + pltpu.roll(t0, HEAD_DIM, 1), t1 + pltpu.roll(t1, HEAD_DIM, 1))

        dk = _rope_bwd(fold(0), c_ref[...], s1_ref[...], s2_ref[...]).astype(BF16)
        dp = jnp.concatenate([dq_ref[...], dk, fold(1).astype(BF16), dsu_ref[...], dsv_ref[...],
                              dg_ref[...]], axis=-1)
        xv = x_ref[...]
        scale0 = 1.0 + mod_ref[1:2, :]
        h0 = (xv * scale0 + mod_ref[0:1, :]).astype(BF16)
        dh0 = _dot_nt(dp, w_ref[...])
        dw_ref[...] += _dot_tn(h0, dp)
        dx_ref[...] = dxp_ref[...] + dh0 * scale0
        vec_ref[0:1, :] += _rowsum(dh0)
        vec_ref[1:2, :] += _rowsum(dh0 * xv)

    t = _tile(TM, D)
    return _fused_call(
        body, comm, (dq, dkv, dsu, dsv, dg0, x, dxp, mod, w_in, rc, rs1, rs2), name="ev_in_bwd", grid=(T // TM,),
        in_specs=[_tile(TM, ATTN_W), _tile(TM, 4 * LANE), _tile(TM, SG_W), _tile(TM, SG_W), t, t, t,
                  _full((3, D)), _full((D, EV_IN)), _tile(TM, LANE), _tile(TM, LANE), _tile(TM, LANE)],
        out_specs=[t, _full((D, EV_IN)), _full((SUBLANE, D))],
        out_shape=[jax.ShapeDtypeStruct((T, D), F32), jax.ShapeDtypeStruct((D, EV_IN), F32),
                   jax.ShapeDtypeStruct((SUBLANE, D), F32)])


def _sum_slots(land_ref):
    g = land_ref[0].astype(F32)
    for i in range(1, N_DEV):
        g = g + land_ref[i].astype(F32)
    return g


def _reduce_adam(land, w, m, v, name):
    R, C = w.shape
    rb = R
    for cand in (128, 64, 32, 16, 8):
        if R % cand == 0:
            rb = cand
            break

    def body(l_ref, w_ref, m_ref, v_ref, g_ref, d_ref, nm_ref, nv_ref):
        g = _sum_slots(l_ref)
        g_ref[...] = g
        dlt, m2, v2 = _adam(w_ref[...], g, m_ref[...], v_ref[...])
        d_ref[...] = dlt
        nm_ref[...] = m2
        nv_ref[...] = v2

    t = pl.BlockSpec((rb, C), lambda i: (i, 0))
    shp = jax.ShapeDtypeStruct((R, C), F32)
    return _pallas(
        body, name=name, grid=(R // rb,),
        in_specs=[pl.BlockSpec((N_DEV, rb, C), lambda i: (0, i, 0)), t, t, t],
        out_specs=[t] * 4, out_shape=[shp] * 4,
        compiler_params=_params(("parallel",)),
    )(land, w, m, v)


def _reduce_only(land, name):
    _, R, C = land.shape

    def body(l_ref, g_ref):
        g_ref[...] = _sum_slots(l_ref)

    return _pallas(
        body, name=name, grid=(1,),
        in_specs=[_full((N_DEV, R, C))], out_specs=_full((R, C)),
        out_shape=jax.ShapeDtypeStruct((R, C), F32),
        compiler_params=_params(("arbitrary",)),
    )(land)


def _adam_only(g, w, m, v, name):
    R, C = w.shape
    rb = R
    for cand in (128, 64, 32, 16, 8):
        if R % cand == 0:
            rb = cand
            break

    def body(g_ref, w_ref, m_ref, v_ref, d_ref, nm_ref, nv_ref):
        dlt, m2, v2 = _adam(w_ref[...], g_ref[...], m_ref[...], v_ref[...])
        d_ref[...] = dlt
        nm_ref[...] = m2
        nv_ref[...] = v2

    t = pl.BlockSpec((rb, C), lambda i: (i, 0))
    shp = jax.ShapeDtypeStruct((R, C), F32)
    return _pallas(
        body, name=name, grid=(R // rb,),
        in_specs=[t] * 4, out_specs=[t] * 3, out_shape=[shp] * 3,
        compiler_params=_params(("parallel",)),
    )(g, w, m, v)


REP_ROWS = 704
REP_LAYOUT = (
    ("ln_g", 2), ("ln_b", 2), ("ev_sg_ln_g", 1), ("ev_sg_ln_b", 1), ("ev_sink", 1), ("ev_sg_b", 1),
    ("ev_sg_w", 128), ("od_w_a", 256), ("od_w_x", 256))


def _pack_rep(parts):
    rows = []
    for name, nrows in REP_LAYOUT:
        flat = parts[name].reshape(-1)
        flat = jnp.pad(flat, (0, nrows * D - flat.shape[0]))
        rows.append(flat.reshape(nrows, D))
    used = sum(r for _, r in REP_LAYOUT)
    rows.append(jnp.zeros((REP_ROWS - used, D), F32))
    return jnp.concatenate(rows, axis=0)


def _unpack_rep(buf, shapes):
    out, r0 = {}, 0
    for name, nrows in REP_LAYOUT:
        size = 1
        for s in shapes[name]:
            size *= s
        out[name] = buf[r0:r0 + nrows].reshape(-1)[:size].reshape(shapes[name])
        r0 += nrows
    return out


VEC_ROWS = 16
VEC_LAYOUT = (("od_conv_w", 4), ("od_conv_b", 1), ("od_b_a", 2), ("od_b_x", 2), ("od_lam", 2))


def _pack_vec(parts):
    rows = [parts[name].reshape(nrows, -1) for name, nrows in VEC_LAYOUT]
    used = sum(r for _, r in VEC_LAYOUT)
    rows.append(jnp.zeros((VEC_ROWS - used, rows[0].shape[1]), F32))
    return jnp.concatenate(rows, axis=0)


def _unpack_vec(buf, shapes):
    out, r0 = {}, 0
    for name, nrows in VEC_LAYOUT:
        out[name] = buf[r0:r0 + nrows].reshape(shapes[name])
        r0 += nrows
    return out


def _to_slabs(full, cols_per):
    R = full.shape[0]
    return full.reshape(R, N_DEV, cols_per).transpose(1, 0, 2)


def _from_slabs(slabs):
    n, R, cp = slabs.shape
    return slabs.transpose(1, 0, 2).reshape(R, n * cp)


def kernel(x, c, positions, ada_w, ada_b, ln_g, ln_b, ev_w_in, ev_w_out, ev_sink, ev_sg_ln_g, ev_sg_ln_b, ev_sg_w, ev_sg_b, od_w_in, od_conv_w, od_conv_b, od_w_a, od_b_a, od_w_x, od_b_x, od_lam, od_w_out, loss_target, m_ada_w, m_ada_b, m_ln_g, m_ln_b, m_ev_w_in, m_ev_w_out, m_ev_sink, m_ev_sg_ln_g, m_ev_sg_ln_b, m_ev_sg_w, m_ev_sg_b, m_od_w_in, m_od_conv_w, m_od_conv_b, m_od_w_a, m_od_b_a, m_od_w_x, m_od_b_x, m_od_lam, m_od_w_out, v_ada_w, v_ada_b, v_ln_g, v_ln_b, v_ev_w_in, v_ev_w_out, v_ev_sink, v_ev_sg_ln_g, v_ev_sg_ln_b, v_ev_sg_w, v_ev_sg_b, v_od_w_in, v_od_conv_w, v_od_conv_b, v_od_w_a, v_od_b_a, v_od_w_x, v_od_b_x, v_od_lam, v_od_w_out):
    T = x.shape[1]
    me = _slot(*_my_pos())
    xs = x.reshape(T, D)
    tgt = loss_target.reshape(T, D)

    vec_w = _pack_vec(dict(od_conv_w=od_conv_w[0], od_conv_b=od_conv_b, od_b_a=od_b_a[0], od_b_x=od_b_x[0],
                           od_lam=od_lam[0]))
    c_all, g_ev_in, g_vec = _all_gather([c, ev_w_in[0].astype(BF16), vec_w], "ag_params")
    c_all = c_all.reshape(N_DEV, D)
    w_ev_in = _from_slabs(g_ev_in)
    vec_full = _from_slabs(g_vec)
    cw, cb = vec_full[0:4], vec_full[4:5]
    ba, bx, lam = vec_full[5:7], vec_full[7:9], vec_full[9:11]

    mod_part = _mod_part(c_all, ada_w)
    (mod_all,) = _all_gather([mod_part], "ag_mod")
    mod_mine = lax.dynamic_index_in_dim(mod_all, me, axis=2, keepdims=False)
    mod = mod_mine.transpose(1, 0, 2).reshape(2, 3 * D) + ada_b
    mod0 = mod[0].reshape(3, D)
    mod1 = mod[1].reshape(3, D)

    half = 8
    inv_freq = jnp.power(jnp.float32(ROPE_THETA), -jnp.arange(half, dtype=F32) / half)
    ang = positions.reshape(T).astype(F32)[:, None] * inv_freq
    cos_t = jnp.tile(jnp.cos(ang), (1, LANE // half))
    sin_t = jnp.tile(jnp.sin(ang), (1, LANE // half))
    l64 = jnp.arange(LANE) % HEAD_DIM
    rc = jnp.where(l64 < 2 * half, cos_t, 1.0)
    rs1 = jnp.where(l64 < half, -sin_t, 0.0)
    rs2 = jnp.where((l64 >= half) & (l64 < 2 * half), sin_t, 0.0)

    ln0 = jnp.stack([ln_g[0], ln_b[0]])
    ln1 = jnp.stack([ln_g[1], ln_b[1]])
    sg_lng = ev_sg_ln_g
    sg_lnb = ev_sg_ln_b
    sg_w = ev_sg_w[0].astype(BF16)
    sg_bfull = jnp.repeat(ev_sg_b[0].T, SG_DIM, axis=1)
    sink_l = jnp.repeat(ev_sink, LANE, axis=1)
    kj = jnp.arange(3 * BLK)[:, None]
    qi = jnp.arange(BLK)[None, :]
    band_bias = jnp.where(jnp.abs(kj - BLK - qi) <= BLK, 0.0, NEG_INF).astype(F32)
    lanes = jnp.arange(LANE)
    a128 = jnp.where(lanes[:, None] // SG_DIM == lanes[None, :] // SG_DIM, 1.0 / SG_DIM, 0.0).astype(BF16)
    gsum = (jnp.arange(SG_W)[:, None] // SG_DIM == lanes[None, :]).astype(BF16)
    sel = (jnp.arange(SUBLANE)[:, None] == lanes[None, :] // HEAD_DIM).astype(BF16)
    wa = od_w_a[0].astype(BF16)
    wx = od_w_x[0].astype(BF16)

    (q, kvx, su, sv, g0), (g_ev_out,) = _ev_in(xs, mod0, w_ev_in, rc, rs1, rs2,
                                               _GatherComm([ev_w_out[0].astype(BF16)]))
    w_ev_out = g_ev_out.reshape(D, D)
    (ycat, y0, lse), (g_od_in, g_od_out) = _mix0_fwd(
        q, kvx, su, sv, g0, sink_l, band_bias, a128, sg_lng, sg_lnb, sg_w, sg_bfull,
        _GatherComm([od_w_in[0].astype(BF16), od_w_out[0].astype(BF16)]))
    w_od_in = _from_slabs(g_od_in)
    w_od_out = g_od_out.reshape(D, D)
    out0, z0, x1 = _ev_out(y0, w_ev_out, xs, mod0, ln0)
    xr, g1 = _od_in(x1, mod1, w_od_in)
    hf = _rglru_fwd(xr, cw, cb, wa[0], wx[0], ba[0:1], bx[0:1], lam[0:1], False, "rglru_fwd_f")
    hb = _rglru_fwd(xr, cw, cb, wa[1], wx[1], ba[1:2], bx[1:2], lam[1:2], True, "rglru_fwd_b")
    loss_v, dh, dg1, dx1p, d_od_out, vec_a = _od_out(hf, hb, g1, w_od_out, x1, tgt, mod1, ln1)

    (dxcf, dwa_f, dwx_f, vec_f), (l_od_out,) = _rglru_bwd(
        xr, dh, hf, cw, cb, wa[0], wx[0], ba[0:1], bx[0:1], lam[0:1], False, "rglru_bwd_f",
        _ExchangeComm([d_od_out.astype(BF16).reshape(N_DEV, D // N_DEV, D)]))
    (dxcb, dwa_b, dwx_b, vec_b), _ = _rglru_bwd(xr, dh, hb, cw, cb, wa[1], wx[1], ba[1:2], bx[1:2], lam[1:2],
                                                True, "rglru_bwd_b")
    dx1, d_od_in, vec_c = _od_in_bwd(dxcf, dxcb, xr, dg1, x1, dx1p, mod1, w_od_in, cw)
    dxp, dyc, dg0, d_ev_out, vec_d = _ev_out_bwd(dx1, z0, out0, y0, ycat, g0, w_ev_out, mod0, ln0)
    (dq, dkv, dsu, dsv, d_sg_w, d_sg_bt, vec_e, d_sink_l), (l_od_in,) = _mix0_bwd(
        q, kvx, lse, dyc, ycat, su, sv, sink_l, band_bias, a128, gsum, sel, sg_lng, sg_lnb, sg_w, sg_bfull,
        rc, rs1, rs2, _ExchangeComm([_to_slabs(d_od_in.astype(BF16), OD_IN // N_DEV)]))
    d_sink = -jnp.sum(d_sink_l.reshape(N_HEADS, LANE), axis=1).reshape(1, N_HEADS)
    (grad_x, d_ev_in, vec_g), (l_ev_out,) = _ev_in_bwd(
        dq, dkv[BLK:BLK + T], dsu, dsv, dg0, xs, dxp, mod0, w_ev_in, rc, rs1, rs2,
        _ExchangeComm([d_ev_out.astype(BF16).reshape(N_DEV, D // N_DEV, D)]))

    loss = lax.psum(loss_v[0, 0], ("x", "y", "c"))

    dmod = jnp.stack([jnp.concatenate([vec_g[0], vec_g[1], vec_d[2]]),
                      jnp.concatenate([vec_c[5], vec_c[6], vec_a[2]])])
    (dmod_all,) = _all_gather([dmod], "ag_dmod")
    cols = ada_w.shape[2]
    dmod_cols = lax.dynamic_slice_in_dim(dmod_all, me * cols, cols, axis=2).transpose(1, 0, 2)
    (g_ada_w, d_ada_w, nm_ada_w, nv_ada_w, g_ada_b, d_ada_b, nm_ada_b, nv_ada_b) = _ada_update(
        c_all, dmod_cols, dmod_all, ada_w, m_ada_w, v_ada_w, ada_b, m_ada_b, v_ada_b)

    rep_g = _pack_rep(dict(
        ln_g=jnp.stack([vec_d[0], vec_a[0]]), ln_b=jnp.stack([vec_d[1], vec_a[1]]),
        ev_sg_ln_g=vec_e[0], ev_sg_ln_b=vec_e[1], ev_sink=d_sink, ev_sg_b=d_sg_bt[:, :SG_GROUPS].T, ev_sg_w=d_sg_w,
        od_w_a=jnp.stack([dwa_f, dwa_b]), od_w_x=jnp.stack([dwx_f, dwx_b])))
    vec_grads = _pack_vec(dict(
        od_conv_w=vec_c[0:4], od_conv_b=vec_c[4:5], od_b_a=jnp.stack([vec_f[0], vec_b[0]]),
        od_b_x=jnp.stack([vec_f[1], vec_b[1]]), od_lam=jnp.stack([vec_f[2], vec_b[2]])))
    l_ev_in, l_vec, l_rep = _all_to_all(
        [_to_slabs(d_ev_in.astype(BF16), EV_IN // N_DEV), _to_slabs(vec_grads, D // N_DEV),
         rep_g.reshape(N_DEV, REP_ROWS // N_DEV, D)], "rs_grads")

    r_ev_in = _reduce_adam(l_ev_in, ev_w_in[0], m_ev_w_in[0], v_ev_w_in[0], "adam_ev_in")
    r_ev_out = _reduce_adam(l_ev_out, ev_w_out[0], m_ev_w_out[0], v_ev_w_out[0], "adam_ev_out")
    r_od_in = _reduce_adam(l_od_in, od_w_in[0], m_od_w_in[0], v_od_w_in[0], "adam_od_in")
    r_od_out = _reduce_adam(l_od_out, od_w_out[0], m_od_w_out[0], v_od_w_out[0], "adam_od_out")
    vec_m = _pack_vec(dict(od_conv_w=m_od_conv_w[0], od_conv_b=m_od_conv_b, od_b_a=m_od_b_a[0],
                           od_b_x=m_od_b_x[0], od_lam=m_od_lam[0]))
    vec_v = _pack_vec(dict(od_conv_w=v_od_conv_w[0], od_conv_b=v_od_conv_b, od_b_a=v_od_b_a[0],
                           od_b_x=v_od_b_x[0], od_lam=v_od_lam[0]))
    r_vec = _reduce_adam(l_vec, vec_w, vec_m, vec_v, "adam_vec")

    rep_slab = _reduce_only(l_rep, "reduce_rep")
    (rep_all,) = _all_gather([rep_slab], "ag_rep")
    rep_grad = rep_all.reshape(REP_ROWS, D)
    rep_names = [nm for nm, _ in REP_LAYOUT]
    given = dict(ln_g=(ln_g, m_ln_g, v_ln_g), ln_b=(ln_b, m_ln_b, v_ln_b),
                 ev_sg_ln_g=(ev_sg_ln_g, m_ev_sg_ln_g, v_ev_sg_ln_g),
                 ev_sg_ln_b=(ev_sg_ln_b, m_ev_sg_ln_b, v_ev_sg_ln_b),
                 ev_sink=(ev_sink, m_ev_sink, v_ev_sink), ev_sg_b=(ev_sg_b, m_ev_sg_b, v_ev_sg_b),
                 ev_sg_w=(ev_sg_w, m_ev_sg_w, v_ev_sg_w), od_w_a=(od_w_a, m_od_w_a, v_od_w_a),
                 od_w_x=(od_w_x, m_od_w_x, v_od_w_x))
    rep_w = _pack_rep({k: given[k][0] for k in rep_names})
    rep_m = _pack_rep({k: given[k][1] for k in rep_names})
    rep_v = _pack_rep({k: given[k][2] for k in rep_names})
    rep_d, rep_nm, rep_nv = _adam_only(rep_grad, rep_w, rep_m, rep_v, "adam_rep")
    rep_shapes = {k: given[k][0].shape for k in rep_names}
    rep_out = [_unpack_rep(b, rep_shapes) for b in (rep_grad, rep_d, rep_nm, rep_nv)]

    vec_shapes = dict(od_conv_w=od_conv_w.shape, od_conv_b=od_conv_b.shape, od_b_a=od_b_a.shape,
                      od_b_x=od_b_x.shape, od_lam=od_lam.shape)
    vec_out = [_unpack_vec(b, vec_shapes) for b in r_vec]

    big = dict(ev_w_in=[a[None] for a in r_ev_in], ev_w_out=[a[None] for a in r_ev_out],
               od_w_in=[a[None] for a in r_od_in], od_w_out=[a[None] for a in r_od_out],
               ada_w=[g_ada_w, d_ada_w, nm_ada_w, nv_ada_w], ada_b=[g_ada_b, d_ada_b, nm_ada_b, nv_ada_b])

    order = ["ada_w", "ada_b", "ln_g", "ln_b", "ev_w_in", "ev_w_out", "ev_sink", "ev_sg_ln_g", "ev_sg_ln_b",
             "ev_sg_w", "ev_sg_b", "od_w_in", "od_conv_w", "od_conv_b", "od_w_a", "od_b_a", "od_w_x", "od_b_x",
             "od_lam", "od_w_out"]

    def pick(kind, name):
        if name in big:
            return big[name][kind]
        if name in vec_shapes:
            return vec_out[kind][name]
        return rep_out[kind][name]

    outs = [loss, grad_x.reshape(1, T, D)]
    for kind in range(4):
        outs += [pick(kind, name) for name in order]
    return tuple(outs)
```

```python
import functools

import jax
import jax.numpy as jnp
from jax import lax
from jax.experimental import pallas as pl
from jax.experimental.pallas import tpu as pltpu

F32 = jnp.float32
BF16 = jnp.bfloat16

N_DEV = 8
D = 1024
N_HEADS = 8
HEAD_DIM = 64
KV_WIDTH = 128
ATTN_W = 512
SG_W = 512
SG_GROUPS = 8
SG_DIM = 64
BLK = 128
KVX_W = 1024
EV_IN = 2816
OD_IN = 2048
RNN_HEADS = 8
RNN_HD = 128
ALPHA = 4.0 ** 0.25
LN_EPS = 1e-5
NEG_INF = -1e30
RG_C = 8.0
ROPE_THETA = 500000.0
LR, B1, B2, EPS, WD, STEP = 0.001, 0.9, 0.999, 1e-08, 0.01, 10

LANE = 128
SUBLANE = 8
TM = 256
TS = 256
VMEM_LIMIT = 56 * 1024 * 1024

MESH = pl.DeviceIdType.MESH


def _pallas(body, **kw):
    return pl.pallas_call(body, **kw)


def _params(sem, vmem=VMEM_LIMIT):
    return pltpu.CompilerParams(dimension_semantics=sem, vmem_limit_bytes=vmem)


def _sigmoid(x):
    return 1.0 / (1.0 + jnp.exp(-x))


def _silu_and_grad(x):
    s = _sigmoid(x)
    return x * s, s * (1.0 + x * (1.0 - s))


def _dot(a, b):
    return jnp.dot(a.astype(BF16), b.astype(BF16), preferred_element_type=F32)


def _dot_nt(a, b):
    return lax.dot_general(a.astype(BF16), b.astype(BF16), (((1,), (1,)), ((), ())), preferred_element_type=F32)


def _dot_tn(a, b):
    return lax.dot_general(a.astype(BF16), b.astype(BF16), (((0,), (0,)), ((), ())), preferred_element_type=F32)


def _ln_fwd(z, g, b):
    mu = jnp.mean(z, axis=-1, keepdims=True)
    zc = z - mu
    var = jnp.mean(zc * zc, axis=-1, keepdims=True)
    rstd = lax.rsqrt(var + LN_EPS)
    xhat = zc * rstd
    return xhat * g + b, xhat, rstd


def _ln_bwd(dy, xhat, rstd, g):
    dxh = dy * g
    m1 = jnp.mean(dxh, axis=-1, keepdims=True)
    m2 = jnp.mean(dxh * xhat, axis=-1, keepdims=True)
    return rstd * (dxh - m1 - xhat * m2)


def _rowsum(v):
    return jnp.sum(v, axis=0, keepdims=True)


def _rope_fwd(t, c, s1, s2):
    return t * c + pltpu.roll(t, LANE - 8, 1) * s1 + pltpu.roll(t, 8, 1) * s2


def _rope_bwd(d, c, s1, s2):
    return d * c + pltpu.roll(d * s1, 8, 1) + pltpu.roll(d * s2, LANE - 8, 1)


def _neg_expm1(x):
    poly = -x * (1.0 + x * (0.5 + x * (1.0 / 6.0 + x * (1.0 / 24.0))))
    return jnp.where(x > -0.03, poly, 1.0 - jnp.exp(x))


def _adam(w, g, m, v):
    m2 = B1 * m + (1.0 - B1) * g
    v2 = B2 * v + (1.0 - B2) * (g * g)
    m_hat = m2 / (1.0 - B1 ** STEP)
    v_hat = v2 / (1.0 - B2 ** STEP)
    delta = -LR * (m_hat / (jnp.sqrt(v_hat) + EPS) + WD * w)
    return delta, m2, v2


def _tile(rows, width):
    return pl.BlockSpec((rows, width), lambda i: (i, 0))


def _full(shape):
    zeros = (0,) * len(shape)
    return pl.BlockSpec(shape, lambda i: zeros)


def _rev_tile(rows, width, n, reverse):
    if reverse:
        return pl.BlockSpec((rows, width), lambda i: (n - 1 - i, 0))
    return pl.BlockSpec((rows, width), lambda i: (i, 0))


def _halo_specs(rows, width, n, total_rows, reverse):
    per = rows // SUBLANE
    last = total_rows // SUBLANE - 1

    def tile_of(i):
        return (n - 1 - i) if reverse else i

    prev = pl.BlockSpec((SUBLANE, width), lambda i: (jnp.maximum(tile_of(i) * per - 1, 0), 0))
    nxt = pl.BlockSpec((SUBLANE, width), lambda i: (jnp.minimum((tile_of(i) + 1) * per, last), 0))
    return prev, nxt


def _my_pos():
    return lax.axis_index("x"), lax.axis_index("y"), lax.axis_index("c")


def _slot(px, py, pc):
    return 4 * px + 2 * py + pc


def _all_gather(arrs, name):
    n = len(arrs)

    def body(*refs):
        ins, outs = refs[:n], refs[n:2 * n]
        send_sems, recv_sems, local_sems = refs[2 * n:]
        x, y, c = _my_pos()
        me, sibling = (x, y, c), (x, y, 1 - c)
        chips = [(1 - x, y), (x, 1 - y), (1 - x, 1 - y)]

        def copy(a, k, block, to, src=None):
            dst = outs[a].at[_slot(*block)]
            return pltpu.make_async_remote_copy(
                src_ref=dst if src is None else src, dst_ref=dst,
                send_sem=send_sems.at[a * 7 + k], recv_sem=recv_sems.at[a * 7 + k],
                device_id=to, device_id_type=MESH)

        local, first = [], []
        for a in range(n):
            lc = pltpu.make_async_copy(ins[a], outs[a].at[_slot(*me)], local_sems.at[a])
            lc.start()
            local.append(lc)
            first.append(copy(a, 0, me, sibling, src=ins[a]))
            first += [copy(a, 1 + j, me, (*chip, c), src=ins[a]) for j, chip in enumerate(chips)]
        for cp in first:
            cp.start()
        passed = []
        for j, chip in enumerate(chips):
            for a in range(n):
                copy(a, 1 + j, (*chip, c), me).wait_recv()
                fw = copy(a, 4 + j, (*chip, c), sibling)
                fw.start()
                passed.append(fw)
        for a in range(n):
            copy(a, 0, sibling, me).wait_recv()
            for j, chip in enumerate(chips):
                copy(a, 4 + j, (*chip, 1 - c), me).wait_recv()
        for cp in first + passed:
            cp.wait_send()
        for lc in local:
            lc.wait()

    any_spec = pl.BlockSpec(memory_space=pl.ANY)
    return _pallas(
        body, name=name,
        out_shape=[jax.ShapeDtypeStruct((N_DEV,) + a.shape, a.dtype) for a in arrs],
        in_specs=[any_spec] * n, out_specs=[any_spec] * n,
        scratch_shapes=[pltpu.SemaphoreType.DMA((7 * n,)), pltpu.SemaphoreType.DMA((7 * n,)),
                        pltpu.SemaphoreType.DMA((n,))],
    )(*arrs)


def _all_to_all(arrs, name):
    n = len(arrs)

    def body(*refs):
        ins, outs = refs[:n], refs[n:2 * n]
        send_sems, recv_sems, local_sems = refs[2 * n:]
        x, y, c = _my_pos()
        mine = _slot(x, y, c)
        copies = []
        for a in range(n):
            lc = pltpu.make_async_copy(ins[a].at[mine], outs[a].at[mine], local_sems.at[a])
            lc.start()
            copies.append(lc)
        for k in range(1, N_DEV):
            px = (1 - x) if (k & 4) else x
            py = (1 - y) if (k & 2) else y
            pc = (1 - c) if (k & 1) else c
            for a in range(n):
                cp = pltpu.make_async_remote_copy(
                    src_ref=ins[a].at[_slot(px, py, pc)], dst_ref=outs[a].at[mine],
                    send_sem=send_sems.at[a * 7 + k - 1], recv_sem=recv_sems.at[a * 7 + k - 1],
                    device_id=(px, py, pc), device_id_type=MESH)
                cp.start()
                copies.append(cp)
        for cp in copies:
            cp.wait()

    any_spec = pl.BlockSpec(memory_space=pl.ANY)
    return _pallas(
        body, name=name,
        out_shape=[jax.ShapeDtypeStruct(a.shape, a.dtype) for a in arrs],
        in_specs=[any_spec] * n, out_specs=[any_spec] * n,
        scratch_shapes=[pltpu.SemaphoreType.DMA((7 * n,)), pltpu.SemaphoreType.DMA((7 * n,)),
                        pltpu.SemaphoreType.DMA((n,))],
    )(*arrs)


class _GatherComm:
    has_mid = True

    def __init__(self, arrs):
        self.arrs = list(arrs)
        self.n = len(self.arrs)

    def out_shapes(self):
        return [jax.ShapeDtypeStruct((N_DEV,) + a.shape, a.dtype) for a in self.arrs]

    def sems(self):
        return [pltpu.SemaphoreType.DMA((7 * self.n,)), pltpu.SemaphoreType.DMA((7 * self.n,)),
                pltpu.SemaphoreType.DMA((self.n,))]

    def _parts(self, ins, outs, sems):
        send_sems, recv_sems, local_sems = sems
        x, y, c = _my_pos()
        me, sibling = (x, y, c), (x, y, 1 - c)
        chips = [(1 - x, y), (x, 1 - y), (1 - x, 1 - y)]

        def copy(a, k, block, to, src=None):
            dst = outs[a].at[_slot(*block)]
            return pltpu.make_async_remote_copy(
                src_ref=dst if src is None else src, dst_ref=dst,
                send_sem=send_sems.at[a * 7 + k], recv_sem=recv_sems.at[a * 7 + k],
                device_id=to, device_id_type=MESH)

        local = [pltpu.make_async_copy(ins[a], outs[a].at[_slot(*me)], local_sems.at[a]) for a in range(self.n)]
        first = []
        for a in range(self.n):
            first.append(copy(a, 0, me, sibling, src=ins[a]))
            first += [copy(a, 1 + j, me, (*chip, c), src=ins[a]) for j, chip in enumerate(chips)]
        ici_in = [copy(a, 1 + j, (*chip, c), me) for j, chip in enumerate(chips) for a in range(self.n)]
        passed = [copy(a, 4 + j, (*chip, c), sibling) for j, chip in enumerate(chips) for a in range(self.n)]
        d2d_in = []
        for a in range(self.n):
            d2d_in.append(copy(a, 0, sibling, me))
            d2d_in += [copy(a, 4 + j, (*chip, 1 - c), me) for j, chip in enumerate(chips)]
        return local, first, ici_in, passed, d2d_in

    def start(self, ins, outs, sems):
        local, first, _, _, _ = self._parts(ins, outs, sems)
        for cp in local + first:
            cp.start()

    def mid(self, ins, outs, sems):
        _, _, ici_in, passed, _ = self._parts(ins, outs, sems)
        for arrived, fw in zip(ici_in, passed):
            arrived.wait_recv()
            fw.start()

    def finish(self, ins, outs, sems):
        local, first, _, passed, d2d_in = self._parts(ins, outs, sems)
        for cp in d2d_in:
            cp.wait_recv()
        for cp in first + passed:
            cp.wait_send()
        for cp in local:
            cp.wait()


class _ExchangeComm:
    has_mid = False

    def __init__(self, arrs):
        self.arrs = list(arrs)
        self.n = len(self.arrs)

    def out_shapes(self):
        return [jax.ShapeDtypeStruct(a.shape, a.dtype) for a in self.arrs]

    def sems(self):
        return [pltpu.SemaphoreType.DMA((7 * self.n,)), pltpu.SemaphoreType.DMA((7 * self.n,)),
                pltpu.SemaphoreType.DMA((self.n,))]

    def _copies(self, ins, outs, sems):
        send_sems, recv_sems, local_sems = sems
        x, y, c = _my_pos()
        mine = _slot(x, y, c)
        copies = [pltpu.make_async_copy(ins[a].at[mine], outs[a].at[mine], local_sems.at[a]) for a in range(self.n)]
        for k in range(1, N_DEV):
            px = (1 - x) if (k & 4) else x
            py = (1 - y) if (k & 2) else y
            pc = (1 - c) if (k & 1) else c
            for a in range(self.n):
                copies.append(pltpu.make_async_remote_copy(
                    src_ref=ins[a].at[_slot(px, py, pc)], dst_ref=outs[a].at[mine],
                    send_sem=send_sems.at[a * 7 + k - 1], recv_sem=recv_sems.at[a * 7 + k - 1],
                    device_id=(px, py, pc), device_id_type=MESH))
        return copies

    def start(self, ins, outs, sems):
        for cp in self._copies(ins, outs, sems):
            cp.start()

    def finish(self, ins, outs, sems):
        for cp in self._copies(ins, outs, sems):
            cp.wait()


def _fused_call(body, comm, operands, *, name, grid, in_specs, out_specs, out_shape, scratch_shapes=(),
                semantics=("arbitrary",)):
    n_in, n_out, n_scr = len(in_specs), len(out_specs), len(scratch_shapes)
    if comm is None:
        res = _pallas(body, name=name, grid=grid, in_specs=list(in_specs), out_specs=list(out_specs),
                      out_shape=list(out_shape), scratch_shapes=list(scratch_shapes),
                      compiler_params=_params(semantics))(*operands)
        return list(res), []
    k = comm.n
    steps = grid[0]

    def wrapped(*refs):
        ins, cins = refs[:n_in], refs[n_in:n_in + k]
        outs = refs[n_in + k:n_in + k + n_out]
        couts = refs[n_in + k + n_out:n_in + 2 * k + n_out]
        rest = refs[n_in + 2 * k + n_out:]
        scratch, sems = rest[:n_scr], rest[n_scr:]
        i = pl.program_id(0)

        @pl.when(i == 0)
        def _():
            comm.start(cins, couts, sems)

        body(*ins, *outs, *scratch)

        if comm.has_mid:
            @pl.when(i == steps // 2)
            def _():
                comm.mid(cins, couts, sems)

        @pl.when(i == steps - 1)
        def _():
            comm.finish(cins, couts, sems)

    any_spec = pl.BlockSpec(memory_space=pl.ANY)
    res = _pallas(wrapped, name=name, grid=grid, in_specs=list(in_specs) + [any_spec] * k,
                  out_specs=list(out_specs) + [any_spec] * k, out_shape=list(out_shape) + comm.out_shapes(),
                  scratch_shapes=list(scratch_shapes) + comm.sems(),
                  compiler_params=_params(("arbitrary",)))(*operands, *comm.arrs)
    return list(res[:n_out]), list(res[n_out:])


def _mod_part(c_all, ada_w):
    cols = ada_w.shape[2]

    def body(c_ref, w_ref, o_ref):
        cv = c_ref[...]
        cond = cv * _sigmoid(cv)
        for l in range(2):
            o_ref[l] = _dot(cond, w_ref[l])

    return _pallas(
        body, name="mod_part", grid=(1,),
        in_specs=[_full((N_DEV, D)), _full((2, D, cols))],
        out_specs=_full((2, N_DEV, cols)),
        out_shape=jax.ShapeDtypeStruct((2, N_DEV, cols), F32),
        compiler_params=_params(("arbitrary",)),
    )(c_all, ada_w)


def _ada_update(c_all, dmod_cols, dmod_all, ada_w, m_w, v_w, ada_b, m_b, v_b):
    cols = ada_w.shape[2]
    nb = ada_b.shape[1]

    def body(c_ref, dmc_ref, dma_ref, w_ref, mw_ref, vw_ref, b_ref, mb_ref, vb_ref,
             gw_ref, dw_ref, nmw_ref, nvw_ref, gb_ref, db_ref, nmb_ref, nvb_ref):
        cv = c_ref[...]
        cond = cv * _sigmoid(cv)
        for l in range(2):
            g = _dot_tn(cond, dmc_ref[l])
            gw_ref[l] = g
            dlt, m2, v2 = _adam(w_ref[l], g, mw_ref[l], vw_ref[l])
            dw_ref[l] = dlt
            nmw_ref[l] = m2
            nvw_ref[l] = v2
        gb = dma_ref[0]
        for i in range(1, N_DEV):
            gb = gb + dma_ref[i]
        gb_ref[...] = gb
        dlt, m2, v2 = _adam(b_ref[...], gb, mb_ref[...], vb_ref[...])
        db_ref[...] = dlt
        nmb_ref[...] = m2
        nvb_ref[...] = v2

    wspec = _full((2, D, cols))
    bspec = _full((2, nb))
    wshape = jax.ShapeDtypeStruct((2, D, cols), F32)
    bshape = jax.ShapeDtypeStruct((2, nb), F32)
    return _pallas(
        body, name="ada_update", grid=(1,),
        in_specs=[_full((N_DEV, D)), _full((2, N_DEV, cols)), _full((N_DEV, 2, nb)),
                  wspec, wspec, wspec, bspec, bspec, bspec],
        out_specs=[wspec] * 4 + [bspec] * 4,
        out_shape=[wshape] * 4 + [bshape] * 4,
        compiler_params=_params(("arbitrary",)),
    )(c_all, dmod_cols, dmod_all, ada_w, m_w, v_w, ada_b, m_b, v_b)


def _ev_in(x, mod, w_in, rc, rs1, rs2, comm=None):
    T = x.shape[0]

    def body(x_ref, mod_ref, w_ref, c_ref, s1_ref, s2_ref, q_ref, kv_ref, su_ref, sv_ref, g_ref):
        h = x_ref[...] * (1.0 + mod_ref[1:2, :]) + mod_ref[0:1, :]
        p = _dot(h, w_ref[...])
        c, s1, s2 = c_ref[...], s1_ref[...], s2_ref[...]
        for j in range(ATTN_W // LANE):
            qr = _rope_fwd(p[:, j * LANE:(j + 1) * LANE], c, s1, s2)
            q_ref[:, j * LANE:(j + 1) * LANE] = (qr * (HEAD_DIM ** -0.5)).astype(BF16)
        low = lax.broadcasted_iota(jnp.int32, (TM, LANE), 1) < HEAD_DIM
        for j, val in enumerate((_rope_fwd(p[:, 512:640], c, s1, s2), p[:, 640:768])):
            swapped = pltpu.roll(val, HEAD_DIM, 1)
            tiles = (jnp.where(low, val, 0.0), jnp.where(low, 0.0, swapped),
                     jnp.where(low, swapped, 0.0), jnp.where(low, 0.0, val))
            for k, tile in enumerate(tiles):
                kv_ref[:, (4 * j + k) * LANE:(4 * j + k + 1) * LANE] = tile.astype(BF16)
        su_ref[...] = p[:, 768:1280].astype(BF16)
        sv_ref[...] = p[:, 1280:1792].astype(BF16)
        g_ref[...] = p[:, 1792:2816].astype(BF16)

    sh = lambda w: jax.ShapeDtypeStruct((T, w), BF16)
    return _fused_call(
        body, comm, (x, mod, w_in, rc, rs1, rs2), name="ev_in", grid=(T // TM,),
        in_specs=[_tile(TM, D), _full((3, D)), _full((D, EV_IN)), _tile(TM, LANE), _tile(TM, LANE), _tile(TM, LANE)],
        out_specs=[_tile(TM, ATTN_W), _tile(TM, KVX_W), _tile(TM, SG_W), _tile(TM, SG_W), _tile(TM, D)],
        out_shape=[sh(ATTN_W), sh(KVX_W), sh(SG_W), sh(SG_W), sh(D)], semantics=("parallel",))


def _band_specs(width, nb):
    return [pl.BlockSpec((BLK, width), lambda n: (jnp.maximum(n - 1, 0), 0)),
            pl.BlockSpec((BLK, width), lambda n: (n, 0)),
            pl.BlockSpec((BLK, width), lambda n: (jnp.minimum(n + 1, nb - 1), 0))]


def _band_bias(bias_ref, n, nb):
    rows = lax.broadcasted_iota(jnp.int32, (3 * BLK, 1), 0)
    outside = ((rows < BLK) & (n == 0)) | ((rows >= 2 * BLK) & (n == nb - 1))
    return bias_ref[...] + jnp.where(outside, NEG_INF, 0.0)


def _split_bf16(v):
    hi = v.astype(BF16)
    return hi, (v - hi.astype(F32)).astype(BF16)


def _group_mean(v, a_ref, exact_bf16=False):
    hi, lo = _split_bf16(v)
    a = a_ref[...]
    out = []
    for t in range(SG_W // LANE):
        sl = slice(t * LANE, (t + 1) * LANE)
        r = jnp.dot(hi[:, sl], a, preferred_element_type=F32)
        if not exact_bf16:
            r = r + jnp.dot(lo[:, sl], a, preferred_element_type=F32)
        out.append(r)
    return jnp.concatenate(out, axis=-1)


def _sg_core(sv_ref, lng, lnb, a_ref, w_ref, bfull_ref):
    svf = sv_ref[...].astype(F32)
    xc = svf - _group_mean(svf, a_ref, exact_bf16=True)
    rstd = lax.rsqrt(_group_mean(xc * xc, a_ref) + LN_EPS)
    xhat = xc * rstd
    vb = (xhat * lng + lnb).astype(BF16)
    low = lax.broadcasted_iota(jnp.int32, (BLK, LANE), 1) < SG_DIM
    tiles = []
    for t in range(SG_W // LANE):
        v2 = vb[:, t * LANE:(t + 1) * LANE]
        r0 = jnp.dot(w_ref[2 * t], v2, preferred_element_type=F32)
        r1 = jnp.dot(w_ref[2 * t + 1], v2, preferred_element_type=F32)
        tiles.append(jnp.where(low, r0, r1))
    svm = jnp.concatenate(tiles, axis=-1) + bfull_ref[...]
    return xhat, rstd, vb, svm


def _mix0_fwd(q, kvx, su, sv, g0, sink_l, bias, a128, sg_lng, sg_lnb, sg_w, sg_bfull, comm=None):
    T = q.shape[0]
    nb = T // BLK

    def body(q_ref, kp_ref, kc_ref, kn_ref, su_ref, sv_ref, g_ref, sink_ref, bias_ref, a_ref, lng_ref, lnb_ref,
             w_ref, bfull_ref, ycat_ref, y0_ref, lse_ref):
        n = pl.program_id(0)
        bias = _band_bias(bias_ref, n, nb)
        kvx = jnp.concatenate([kp_ref[...], kc_ref[...], kn_ref[...]], axis=0)
        tiles = []
        for t in range(ATTN_W // LANE):
            qt = q_ref[:, t * LANE:(t + 1) * LANE]
            acc = None
            for par in range(2):
                h = 2 * t + par
                kt = 2 * (h // 4) + par
                ke = kvx[:, kt * LANE:(kt + 1) * LANE]
                ve = kvx[:, (4 + kt) * LANE:(5 + kt) * LANE]
                st = _dot_nt(ke, qt) + bias
                sk = sink_ref[:, h * LANE:(h + 1) * LANE]
                m = jnp.maximum(jnp.max(st, axis=0, keepdims=True), sk)
                p = jnp.exp(st - m)
                denom = jnp.sum(p, axis=0, keepdims=True) + jnp.exp(sk - m)
                contrib = _dot_tn(p * (1.0 / denom), ve)
                acc = contrib if acc is None else acc + contrib
                lse_ref[0, :, h * LANE:(h + 1) * LANE] = m + jnp.log(denom)
            tiles.append(acc)
        _, _, _, svm = _sg_core(sv_ref, lng_ref[...], lnb_ref[...], a_ref, w_ref, bfull_ref)
        tiles.append(su_ref[...].astype(F32) * svm)
        ycat = jnp.concatenate(tiles, axis=-1)
        gf = g_ref[...].astype(F32)
        ycat_ref[...] = ycat.astype(BF16)
        y0_ref[...] = (ycat * (gf * _sigmoid(gf))).astype(BF16)

    return _fused_call(
        body, comm, (q, kvx, kvx, kvx, su, sv, g0, sink_l, bias, a128, sg_lng, sg_lnb, sg_w, sg_bfull),
        name="mix0_fwd", grid=(nb,),
        in_specs=[_tile(BLK, ATTN_W)] + _band_specs(KVX_W, nb) + [
            _tile(BLK, SG_W), _tile(BLK, SG_W), _tile(BLK, D), _full((1, N_HEADS * LANE)), _full((3 * BLK, LANE)),
            _full((LANE, LANE)), _full((1, SG_W)), _full((1, SG_W)), _full((SG_GROUPS, BLK, BLK)),
            _full((BLK, SG_W))],
        out_specs=[_tile(BLK, D), _tile(BLK, D), pl.BlockSpec((1, 1, N_HEADS * LANE), lambda n: (n, 0, 0))],
        out_shape=[jax.ShapeDtypeStruct((T, D), BF16), jax.ShapeDtypeStruct((T, D), BF16),
                   jax.ShapeDtypeStruct((nb, 1, N_HEADS * LANE), F32)], semantics=("parallel",))


def _ev_out(y0, w_out, x, mod, lnp):
    T = x.shape[0]

    def body(y_ref, w_ref, x_ref, mod_ref, ln_ref, out_ref, z_ref, x1_ref):
        out = _dot(y_ref[...], w_ref[...])
        z = ALPHA * x_ref[...] + mod_ref[2:3, :] * out
        x1, _, _ = _ln_fwd(z, ln_ref[0:1, :], ln_ref[1:2, :])
        out_ref[...] = out.astype(BF16)
        z_ref[...] = z
        x1_ref[...] = x1

    return _pallas(
        body, name="ev_out", grid=(T // TM,),
        in_specs=[_tile(TM, D), _full((D, D)), _tile(TM, D), _full((3, D)), _full((2, D))],
        out_specs=[_tile(TM, D)] * 3,
        out_shape=[jax.ShapeDtypeStruct((T, D), BF16), jax.ShapeDtypeStruct((T, D), F32),
                   jax.ShapeDtypeStruct((T, D), F32)],
        compiler_params=_params(("parallel",)),
    )(y0, w_out, x, mod, lnp)


def _od_in(x1, mod, w_in):
    T = x1.shape[0]

    def body(x_ref, mod_ref, w_ref, xr_ref, g_ref):
        h = x_ref[...] * (1.0 + mod_ref[1:2, :]) + mod_ref[0:1, :]
        p = _dot(h, w_ref[...])
        xr_ref[...] = p[:, :D]
        g_ref[...] = p[:, D:].astype(BF16)

    return _pallas(
        body, name="od_in", grid=(T // TM,),
        in_specs=[_tile(TM, D), _full((3, D)), _full((D, OD_IN))],
        out_specs=[_tile(TM, D), _tile(TM, D)],
        out_shape=[jax.ShapeDtypeStruct((T, D), F32), jax.ShapeDtypeStruct((T, D), BF16)],
        compiler_params=_params(("parallel",)),
    )(x1, mod, w_in)


def _ext_rows(prev_ref, cur, next_ref, j, n):
    prev = jnp.where(j > 0, prev_ref[...], 0.0)
    nxt = jnp.where(j < n - 1, next_ref[...], 0.0)
    return jnp.concatenate([prev, cur, nxt], axis=0)


def _shift_rows(ext, off, rows):
    total = ext.shape[0]
    if off == 0:
        return ext[SUBLANE:SUBLANE + rows, :]
    return pltpu.roll(ext, (-off) % total, 0)[SUBLANE:SUBLANE + rows, :]


def _conv_fwd(ext, cw, cb, rows):
    xc = cb
    for k in range(4):
        xc = xc + cw[k:k + 1, :] * _shift_rows(ext, k - 2, rows)
    return xc


def _gates(xc, wa_ref, wx_ref, ba, bx, lam):
    pr, pi = [], []
    for h in range(RNN_HEADS):
        xh = xc[:, h * RNN_HD:(h + 1) * RNN_HD].astype(BF16)
        pr.append(_dot(xh, wa_ref[h]))
        pi.append(_dot(xh, wx_ref[h]))
    r = _sigmoid(jnp.concatenate(pr, axis=-1) + ba)
    ig = _sigmoid(jnp.concatenate(pi, axis=-1) + bx)
    sp = jnp.maximum(-lam, 0.0) + jnp.log(1.0 + jnp.exp(-jnp.abs(lam)))
    log_a = -RG_C * r * sp
    a = jnp.exp(log_a)
    s = jnp.sqrt(_neg_expm1(2.0 * log_a))
    return r, ig, sp, a, s


def _scan_tile(a_ref, b_ref, o_ref, carry_ref, rows, reverse):
    ridx = lax.broadcasted_iota(jnp.int32, (SUBLANE, D), 0)
    groups = rows // SUBLANE

    def group(gi, h):
        g = (groups - 1 - gi) if reverse else gi
        off = pl.multiple_of(g * SUBLANE, SUBLANE)
        a = a_ref[pl.ds(off, SUBLANE), :]
        b = b_ref[pl.ds(off, SUBLANE), :]
        for sh in (1, 2, 4):
            if reverse:
                keep = ridx < SUBLANE - sh
                a_p = jnp.where(keep, pltpu.roll(a, SUBLANE - sh, 0), 1.0)
                b_p = jnp.where(keep, pltpu.roll(b, SUBLANE - sh, 0), 0.0)
            else:
                keep = ridx >= sh
                a_p = jnp.where(keep, pltpu.roll(a, sh, 0), 1.0)
                b_p = jnp.where(keep, pltpu.roll(b, sh, 0), 0.0)
            b = b + a * b_p
            a = a * a_p
        hh = b + a * h
        o_ref[pl.ds(off, SUBLANE), :] = hh
        return hh[0:1, :] if reverse else hh[SUBLANE - 1:SUBLANE, :]

    carry_ref[...] = lax.fori_loop(0, groups, group, carry_ref[...])


def _rglru_fwd(xr, cw, cb, wa, wx, ba, bx, lam, reverse, name):
    T = xr.shape[0]
    n = T // TS
    prev_spec, next_spec = _halo_specs(TS, D, n, T, reverse)

    def body(prev_ref, cur_ref, next_ref, cw_ref, cb_ref, wa_ref, wx_ref, ba_ref, bx_ref, lam_ref,
             h_ref, a_s, b_s, carry):
        i = pl.program_id(0)
        j = (n - 1 - i) if reverse else i

        @pl.when(i == 0)
        def _():
            carry[...] = jnp.zeros_like(carry)

        ext = _ext_rows(prev_ref, cur_ref[...], next_ref, j, n)
        xc = _conv_fwd(ext, cw_ref[...], cb_ref[...], TS)
        _, ig, _, a, s = _gates(xc, wa_ref, wx_ref, ba_ref[...], bx_ref[...], lam_ref[...])
        a_s[...] = a
        b_s[...] = s * ig * xc
        _scan_tile(a_s, b_s, h_ref, carry, TS, reverse)

    wspec = _full((RNN_HEADS, RNN_HD, RNN_HD))
    return _pallas(
        body, name=name, grid=(n,),
        in_specs=[prev_spec, _rev_tile(TS, D, n, reverse), next_spec, _full((4, D)), _full((1, D)),
                  wspec, wspec, _full((1, D)), _full((1, D)), _full((1, D))],
        out_specs=_rev_tile(TS, D, n, reverse),
        out_shape=jax.ShapeDtypeStruct((T, D), F32),
        scratch_shapes=[pltpu.VMEM((TS, D), F32), pltpu.VMEM((TS, D), F32), pltpu.VMEM((1, D), F32)],
        compiler_params=_params(("arbitrary",)),
    )(xr, xr, xr, cw, cb, wa, wx, ba, bx, lam)


def _od_out(hf, hb, g1, w_out, x1, tgt, mod, lnp):
    T = x1.shape[0]

    def body(hf_ref, hb_ref, g_ref, w_ref, x_ref, t_ref, mod_ref, ln_ref,
             dh_ref, dg_ref, dx_ref, dwb_ref, vec_ref, dw_ref):
        i = pl.program_id(0)

        @pl.when(i == 0)
        def _():
            dw_ref[...] = jnp.zeros_like(dw_ref)
            vec_ref[...] = jnp.zeros_like(vec_ref)

        hs = hf_ref[...] + hb_ref[...]
        sg, dsg = _silu_and_grad(g_ref[...].astype(F32))
        yr = (hs * sg).astype(BF16)
        w = w_ref[...]
        out = _dot(yr, w)
        gate = mod_ref[2:3, :]
        z = ALPHA * x_ref[...] + gate * out
        lng = ln_ref[0:1, :]
        x2, xhat, rstd = _ln_fwd(z, lng, ln_ref[1:2, :])
        diff = x2 - t_ref[...]
        vec_ref[3:4, 0:LANE] += 0.5 * jnp.sum(diff * diff) * (1.0 / D)
        dx2 = diff * (1.0 / D)
        dz = _ln_bwd(dx2, xhat, rstd, lng)
        vec_ref[0:1, :] += _rowsum(dx2 * xhat)
        vec_ref[1:2, :] += _rowsum(dx2)
        vec_ref[2:3, :] += _rowsum(dz * out)
        dout = (dz * gate).astype(BF16)
        dyr = _dot_nt(dout, w)
        dw_ref[...] += _dot_tn(yr, dout)
        dh_ref[...] = dyr * sg
        dg_ref[...] = (dyr * hs * dsg).astype(BF16)
        dx_ref[...] = ALPHA * dz

        @pl.when(i == T // TM - 1)
        def _():
            dwb_ref[...] = dw_ref[...].astype(BF16)

    return _pallas(
        body, name="od_out", grid=(T // TM,),
        in_specs=[_tile(TM, D), _tile(TM, D), _tile(TM, D), _full((D, D)), _tile(TM, D), _tile(TM, D),
                  _full((3, D)), _full((2, D))],
        out_specs=[_tile(TM, D), _tile(TM, D), _tile(TM, D), _full((D, D)), _full((SUBLANE, D))],
        out_shape=[jax.ShapeDtypeStruct((T, D), F32), jax.ShapeDtypeStruct((T, D), BF16),
                   jax.ShapeDtypeStruct((T, D), F32), jax.ShapeDtypeStruct((D, D), BF16),
                   jax.ShapeDtypeStruct((SUBLANE, D), F32)],
        scratch_shapes=[pltpu.VMEM((D, D), F32)],
        compiler_params=_params(("arbitrary",)),
    )(hf, hb, g1, w_out, x1, tgt, mod, lnp)


def _rglru_bwd(xr, dh, h, cw, cb, wa, wx, ba, bx, lam, reverse, name, comm=None):
    T = xr.shape[0]
    n = T // TS
    adj_rev = not reverse
    xprev_spec, xnext_spec = _halo_specs(TS, D, n, T, adj_rev)
    hprev_spec, hnext_spec = _halo_specs(TS, D, n, T, adj_rev)
    h_halo_spec = hnext_spec if reverse else hprev_spec

    def body(xprev_ref, xcur_ref, xnext_ref, dh_ref, h_ref, hh_ref, cw_ref, cb_ref, wa_ref, wx_ref,
             ba_ref, bx_ref, lam_ref, dxc_ref, dwa_ref, dwx_ref, vec_ref, a_s, b_s, l_s, carry, a_edge):
        i = pl.program_id(0)
        j = (n - 1 - i) if adj_rev else i

        @pl.when(i == 0)
        def _():
            carry[...] = jnp.zeros_like(carry)
            a_edge[...] = jnp.zeros_like(a_edge)
            dwa_ref[...] = jnp.zeros_like(dwa_ref)
            dwx_ref[...] = jnp.zeros_like(dwx_ref)
            vec_ref[...] = jnp.zeros_like(vec_ref)

        ext = _ext_rows(xprev_ref, xcur_ref[...], xnext_ref, j, n)
        xc = _conv_fwd(ext, cw_ref[...], cb_ref[...], TS)
        lam = lam_ref[...]
        r, ig, sp, a, s = _gates(xc, wa_ref, wx_ref, ba_ref[...], bx_ref[...], lam)

        rows = lax.broadcasted_iota(jnp.int32, (TS, D), 0)
        hcur = h_ref[...]
        if reverse:
            a_sh = jnp.where(rows == 0, a_edge[...], pltpu.roll(a, 1, 0))
            halo = jnp.where(j < n - 1, hh_ref[0:1, :], 0.0)
            h_nb = jnp.where(rows == TS - 1, halo, pltpu.roll(hcur, TS - 1, 0))
        else:
            a_sh = jnp.where(rows == TS - 1, a_edge[...], pltpu.roll(a, TS - 1, 0))
            halo = jnp.where(j > 0, hh_ref[SUBLANE - 1:SUBLANE, :], 0.0)
            h_nb = jnp.where(rows == 0, halo, pltpu.roll(hcur, 1, 0))
        a_s[...] = a_sh
        b_s[...] = dh_ref[...]
        _scan_tile(a_s, b_s, l_s, carry, TS, adj_rev)
        a_edge[...] = a[TS - 1:TS, :] if reverse else a[0:1, :]

        lm = l_s[...]
        da = lm * h_nb
        di = lm * s * xc
        dxc = lm * s * ig
        ds = lm * ig * xc
        dlog_a = a * da - ds * (a * a) / s
        dr = (-RG_C) * sp * dlog_a
        dsp = _rowsum((-RG_C) * r * dlog_a)
        dpr = dr * r * (1.0 - r)
        dpi = di * ig * (1.0 - ig)
        vec_ref[0:1, :] += _rowsum(dpr)
        vec_ref[1:2, :] += _rowsum(dpi)
        vec_ref[2:3, :] += dsp * (-_sigmoid(-lam))
        parts = []
        for hd in range(RNN_HEADS):
            sl = slice(hd * RNN_HD, (hd + 1) * RNN_HD)
            xh = xc[:, sl].astype(BF16)
            dprh = dpr[:, sl].astype(BF16)
            dpih = dpi[:, sl].astype(BF16)
            parts.append(_dot_nt(dprh, wa_ref[hd]) + _dot_nt(dpih, wx_ref[hd]))
            dwa_ref[hd] += _dot_tn(xh, dprh)
            dwx_ref[hd] += _dot_tn(xh, dpih)
        dxc_ref[...] = dxc + jnp.concatenate(parts, axis=-1)

    wspec = _full((RNN_HEADS, RNN_HD, RNN_HD))
    cur = _rev_tile(TS, D, n, adj_rev)
    return _fused_call(
        body, comm, (xr, xr, xr, dh, h, h, cw, cb, wa, wx, ba, bx, lam), name=name, grid=(n,),
        in_specs=[xprev_spec, cur, xnext_spec, cur, cur, h_halo_spec, _full((4, D)), _full((1, D)),
                  wspec, wspec, _full((1, D)), _full((1, D)), _full((1, D))],
        out_specs=[cur, wspec, wspec, _full((SUBLANE, D))],
        out_shape=[jax.ShapeDtypeStruct((T, D), F32),
                   jax.ShapeDtypeStruct((RNN_HEADS, RNN_HD, RNN_HD), F32),
                   jax.ShapeDtypeStruct((RNN_HEADS, RNN_HD, RNN_HD), F32),
                   jax.ShapeDtypeStruct((SUBLANE, D), F32)],
        scratch_shapes=[pltpu.VMEM((TS, D), F32), pltpu.VMEM((TS, D), F32), pltpu.VMEM((TS, D), F32),
                        pltpu.VMEM((1, D), F32), pltpu.VMEM((1, D), F32)])


def _od_in_bwd(dxcf, dxcb, xr, dg1, x1, dx1p, mod, w_in, cw, comm=None):
    T = x1.shape[0]
    n = T // TM
    slab = OD_IN // N_DEV
    prev_spec, next_spec = _halo_specs(TM, D, n, T, False)

    def body(fp_ref, fc_ref, fn_ref, bp_ref, bc_ref, bn_ref, xp_ref, xc_ref, xn_ref, dg_ref, x1_ref, dxp_ref,
             mod_ref, w_ref, cw_ref, dx_ref, dwb_ref, vec_ref, dw_ref):
        i = pl.program_id(0)

        @pl.when(i == 0)
        def _():
            dw_ref[...] = jnp.zeros_like(dw_ref)
            vec_ref[...] = jnp.zeros_like(vec_ref)

        dcur = fc_ref[...] + bc_ref[...]
        dprev = jnp.where(i > 0, fp_ref[...] + bp_ref[...], 0.0)
        dnext = jnp.where(i < n - 1, fn_ref[...] + bn_ref[...], 0.0)
        dext = jnp.concatenate([dprev, dcur, dnext], axis=0)
        xext = _ext_rows(xp_ref, xc_ref[...], xn_ref, i, n)
        cw_v = cw_ref[...]
        dxr = None
        for k in range(4):
            term = cw_v[k:k + 1, :] * _shift_rows(dext, 2 - k, TM)
            dxr = term if dxr is None else dxr + term
            vec_ref[k:k + 1, :] += _rowsum(dcur * _shift_rows(xext, k - 2, TM))
        vec_ref[4:5, :] += _rowsum(dcur)
        dp = jnp.concatenate([dxr.astype(BF16), dg_ref[...]], axis=-1)
        x1v = x1_ref[...]
        scale1 = 1.0 + mod_ref[1:2, :]
        h1 = (x1v * scale1 + mod_ref[0:1, :]).astype(BF16)
        dh1 = _dot_nt(dp, w_ref[...])
        dw_ref[...] += _dot_tn(h1, dp)
        dx_ref[...] = dxp_ref[...] + dh1 * scale1
        vec_ref[5:6, :] += _rowsum(dh1)
        vec_ref[6:7, :] += _rowsum(dh1 * x1v)

        @pl.when(i == n - 1)
        def _():
            for j in range(N_DEV):
                dwb_ref[j] = dw_ref[:, j * slab:(j + 1) * slab].astype(BF16)

    t = _tile(TM, D)
    return _fused_call(
        body, comm, (dxcf, dxcf, dxcf, dxcb, dxcb, dxcb, xr, xr, xr, dg1, x1, dx1p, mod, w_in, cw),
        name="od_in_bwd", grid=(n,),
        in_specs=[prev_spec, t, next_spec, prev_spec, t, next_spec, prev_spec, t, next_spec, t, t, t,
                  _full((3, D)), _full((D, OD_IN)), _full((4, D))],
        out_specs=[t, _full((N_DEV, D, slab)), _full((SUBLANE, D))],
        out_shape=[jax.ShapeDtypeStruct((T, D), F32), jax.ShapeDtypeStruct((N_DEV, D, slab), BF16),
                   jax.ShapeDtypeStruct((SUBLANE, D), F32)],
        scratch_shapes=[pltpu.VMEM((D, OD_IN), F32)])


def _ev_out_bwd(dx1, z0, out0, y0, ycat, g0, w_out, mod, lnp):
    T = dx1.shape[0]

    def body(dx_ref, z_ref, out_ref, y0_ref, yc_ref, g_ref, w_ref, mod_ref, ln_ref,
             dxp_ref, dyc_ref, dg_ref, dwb_ref, vec_ref, dw_ref):
        i = pl.program_id(0)

        @pl.when(i == 0)
        def _():
            dw_ref[...] = jnp.zeros_like(dw_ref)
            vec_ref[...] = jnp.zeros_like(vec_ref)

        lng = ln_ref[0:1, :]
        _, xhat, rstd = _ln_fwd(z_ref[...], lng, ln_ref[1:2, :])
        dy = dx_ref[...]
        dz = _ln_bwd(dy, xhat, rstd, lng)
        vec_ref[0:1, :] += _rowsum(dy * xhat)
        vec_ref[1:2, :] += _rowsum(dy)
        vec_ref[2:3, :] += _rowsum(dz * out_ref[...].astype(F32))
        dout = (dz * mod_ref[2:3, :]).astype(BF16)
        dy0 = _dot_nt(dout, w_ref[...])
        dw_ref[...] += _dot_tn(y0_ref[...], dout)
        sg, dsg = _silu_and_grad(g_ref[...].astype(F32))
        dyc_ref[...] = (dy0 * sg).astype(BF16)
        dg_ref[...] = (dy0 * yc_ref[...].astype(F32) * dsg).astype(BF16)
        dxp_ref[...] = ALPHA * dz

        @pl.when(i == T // TM - 1)
        def _():
            dwb_ref[...] = dw_ref[...].astype(BF16)

    t = _tile(TM, D)
    return _pallas(
        body, name="ev_out_bwd", grid=(T // TM,),
        in_specs=[t, t, t, t, t, t, _full((D, D)), _full((3, D)), _full((2, D))],
        out_specs=[t, t, t, _full((D, D)), _full((SUBLANE, D))],
        out_shape=[jax.ShapeDtypeStruct((T, D), F32), jax.ShapeDtypeStruct((T, D), BF16),
                   jax.ShapeDtypeStruct((T, D), BF16), jax.ShapeDtypeStruct((D, D), BF16),
                   jax.ShapeDtypeStruct((SUBLANE, D), F32)],
        scratch_shapes=[pltpu.VMEM((D, D), F32)],
        compiler_params=_params(("arbitrary",)),
    )(dx1, z0, out0, y0, ycat, g0, w_out, mod, lnp)


def _mix0_bwd(q, kvx, lse, dyc, ycat, su, sv, sink_l, bias, a128, gsum, sel, sg_lng, sg_lnb, sg_w, sg_bfull,
              rc, rs1, rs2, comm=None):
    T = q.shape[0]
    nb = T // BLK

    def body(q_ref, kp_ref, kc_ref, kn_ref, lse_ref, dyc_ref, yc_ref, su_ref, sv_ref, sink_ref, bias_ref, a_ref,
             gsum_ref, sel_ref, lng_ref, lnb_ref, w_ref, bfull_ref, c_ref, s1_ref, s2_ref,
             dq_ref, dkv_ref, dsu_ref, dsv_ref, dw_ref, dbt_ref, vec_ref, dsink_ref):
        n = pl.program_id(0)

        @pl.when(n == 0)
        def _():
            dkv_ref[...] = jnp.zeros_like(dkv_ref)
            dw_ref[...] = jnp.zeros_like(dw_ref)
            dbt_ref[...] = jnp.zeros_like(dbt_ref)
            vec_ref[...] = jnp.zeros_like(vec_ref)
            dsink_ref[...] = jnp.zeros_like(dsink_ref)

        band = pl.ds(pl.multiple_of(n * BLK + (TM - BLK), BLK), 3 * BLK)
        bias = _band_bias(bias_ref, n, nb)
        kvx = jnp.concatenate([kp_ref[...], kc_ref[...], kn_ref[...]], axis=0)
        low = lax.broadcasted_iota(jnp.int32, (BLK, LANE), 1) < HEAD_DIM
        sel = sel_ref[...]
        c, s1, s2 = c_ref[...], s1_ref[...], s2_ref[...]
        for kvh in range(2):
            dkx = jnp.zeros((3 * BLK, LANE), F32)
            dvx = jnp.zeros((3 * BLK, LANE), F32)
            for t in range(2 * kvh, 2 * kvh + 2):
                tl = slice(t * LANE, (t + 1) * LANE)
                qt = q_ref[:, tl]
                do = dyc_ref[:, tl]
                p_hi, p_lo = _split_bf16(do.astype(F32) * yc_ref[:, tl].astype(F32))
                deltas = _dot_nt(sel, p_hi) + _dot_nt(sel, p_lo)
                dq_acc = None
                for par in range(2):
                    h = 2 * t + par
                    hl = slice(h * LANE, (h + 1) * LANE)
                    kt = 2 * kvh + par
                    ke = kvx[:, kt * LANE:(kt + 1) * LANE]
                    ve = kvx[:, (4 + kt) * LANE:(5 + kt) * LANE]
                    lse = lse_ref[0, :, hl]
                    delta = deltas[par:par + 1, :]
                    pt = jnp.exp(_dot_nt(ke, qt) + bias - lse)
                    dst = (pt * (_dot_nt(ve, do) - delta)).astype(BF16)
                    dsink_ref[:, hl] += jnp.exp(sink_ref[:, hl] - lse) * delta
                    part = _dot_tn(dst, ke)
                    dq_acc = part if dq_acc is None else dq_acc + part
                    mine = low if par == 0 else jnp.logical_not(low)
                    dkx = dkx + jnp.dot(dst, jnp.where(mine, qt, jnp.zeros_like(qt)), preferred_element_type=F32)
                    dvx = dvx + jnp.dot(pt.astype(BF16), jnp.where(mine, do, jnp.zeros_like(do)),
                                        preferred_element_type=F32)
                dq_ref[:, tl] = _rope_bwd(dq_acc * (HEAD_DIM ** -0.5), c, s1, s2).astype(BF16)
            dkv_ref[band, kvh * LANE:(kvh + 1) * LANE] += dkx
            dkv_ref[band, (2 + kvh) * LANE:(3 + kvh) * LANE] += dvx

        lng = lng_ref[...]
        xhat, rstd, vb, svm = _sg_core(sv_ref, lng, lnb_ref[...], a_ref, w_ref, bfull_ref)
        dy = dyc_ref[:, ATTN_W:].astype(F32)
        dsu_ref[...] = (dy * svm).astype(BF16)
        dsvm = dy * su_ref[...].astype(F32)
        d_hi, d_lo = _split_bf16(dsvm)
        gsum = gsum_ref[...]
        dbt_ref[...] += jnp.dot(d_hi, gsum, preferred_element_type=F32) + jnp.dot(d_lo, gsum,
                                                                                 preferred_element_type=F32)
        tiles = []
        for t in range(SG_W // LANE):
            tl = slice(t * LANE, (t + 1) * LANE)
            dt, v2 = d_hi[:, tl], vb[:, tl]
            dw_ref[2 * t] += _dot_nt(jnp.where(low, dt, jnp.zeros_like(dt)), v2)
            dw_ref[2 * t + 1] += _dot_nt(jnp.where(low, jnp.zeros_like(dt), dt), v2)
            tiles.append(jnp.where(low, _dot_tn(w_ref[2 * t], dt), _dot_tn(w_ref[2 * t + 1], dt)))
        dvgn = jnp.concatenate(tiles, axis=-1)
        vec_ref[0:1, :] += _rowsum(dvgn * xhat)
        vec_ref[1:2, :] += _rowsum(dvgn)
        dxh = dvgn * lng
        m1 = _group_mean(dxh, a_ref)
        m2 = _group_mean(dxh * xhat, a_ref)
        dsv_ref[...] = (rstd * (dxh - m1 - xhat * m2)).astype(BF16)

    return _fused_call(
        body, comm, (q, kvx, kvx, kvx, lse, dyc, ycat, su, sv, sink_l, bias, a128, gsum, sel, sg_lng, sg_lnb, sg_w,
                     sg_bfull, rc, rs1, rs2),
        name="mix0_bwd", grid=(nb,),
        in_specs=[_tile(BLK, ATTN_W)] + _band_specs(KVX_W, nb) + [
            pl.BlockSpec((1, 1, N_HEADS * LANE), lambda n: (n, 0, 0)), _tile(BLK, D), _tile(BLK, D),
            _tile(BLK, SG_W), _tile(BLK, SG_W), _full((1, N_HEADS * LANE)), _full((3 * BLK, LANE)),
            _full((LANE, LANE)), _full((SG_W, LANE)), _full((SUBLANE, LANE)), _full((1, SG_W)), _full((1, SG_W)),
            _full((SG_GROUPS, BLK, BLK)), _full((BLK, SG_W)), _tile(BLK, LANE), _tile(BLK, LANE), _tile(BLK, LANE)],
        out_specs=[_tile(BLK, ATTN_W), _full((T + 2 * TM, 4 * LANE)), _tile(BLK, SG_W), _tile(BLK, SG_W),
                   _full((SG_GROUPS, BLK, BLK)), _full((BLK, LANE)), _full((SUBLANE, SG_W)),
                   _full((1, N_HEADS * LANE))],
        out_shape=[jax.ShapeDtypeStruct((T, ATTN_W), BF16), jax.ShapeDtypeStruct((T + 2 * TM, 4 * LANE), F32),
                   jax.ShapeDtypeStruct((T, SG_W), BF16), jax.ShapeDtypeStruct((T, SG_W), BF16),
                   jax.ShapeDtypeStruct((SG_GROUPS, BLK, BLK), F32), jax.ShapeDtypeStruct((BLK, LANE), F32),
                   jax.ShapeDtypeStruct((SUBLANE, SG_W), F32), jax.ShapeDtypeStruct((1, N_HEADS * LANE), F32)])


def _ev_in_bwd(dq, dkv, dsu, dsv, dg0, x, dxp, mod, w_in, rc, rs1, rs2, comm=None):
    T = x.shape[0]

    def body(dq_ref, dkv_ref, dsu_ref, dsv_ref, dg_ref, x_ref, dxp_ref, mod_ref, w_ref, c_ref, s1_ref, s2_ref,
             dx_ref, dwb_ref, vec_ref, dw_ref):
        i = pl.program_id(0)

        @pl.when(i == 0)
        def _():
            dw_ref[...] = jnp.zeros_like(dw_ref)
            vec_ref[...] = jnp.zeros_like(vec_ref)

        low = lax.broadcasted_iota(jnp.int32, (TM, LANE), 1) < HEAD_DIM

        def fold(j):
            t0 = dkv_ref[:, (2 * j) * LANE:(2 * j + 1) * LANE]
            t1 = dkv_ref[:, (2 * j + 1) * LANE:(2 * j + 2) * LANE]
            return jnp.where(low, t0 + pltpu.roll(t0, HEAD_DIM, 1), t1 + pltpu.roll(t1, HEAD_DIM, 1))

        dk = _rope_bwd(fold(0), c_ref[...], s1_ref[...], s2_ref[...]).astype(BF16)
        dp = jnp.concatenate([dq_ref[...], dk, fold(1).astype(BF16), dsu_ref[...], dsv_ref[...],
                              dg_ref[...]], axis=-1)
        xv = x_ref[...]
        scale0 = 1.0 + mod_ref[1:2, :]
        h0 = (xv * scale0 + mod_ref[0:1, :]).astype(BF16)
        dh0 = _dot_nt(dp, w_ref[...])
        dw_ref[...] += _dot_tn(h0, dp)
        dx_ref[...] = dxp_ref[...] + dh0 * scale0
        vec_ref[0:1, :] += _rowsum(dh0)
        vec_ref[1:2, :] += _rowsum(dh0 * xv)

        @pl.when(i == T // TM - 1)
        def _():
            dwb_ref[...] = dw_ref[...].astype(BF16)

    t = _tile(TM, D)
    return _fused_call(
        body, comm, (dq, dkv, dsu, dsv, dg0, x, dxp, mod, w_in, rc, rs1, rs2), name="ev_in_bwd", grid=(T // TM,),
        in_specs=[_tile(TM, ATTN_W), pl.BlockSpec((TM, 4 * LANE), lambda i: (i + 1, 0)), _tile(TM, SG_W),
                  _tile(TM, SG_W), t, t, t,
                  _full((3, D)), _full((D, EV_IN)), _tile(TM, LANE), _tile(TM, LANE), _tile(TM, LANE)],
        out_specs=[t, _full((D, EV_IN)), _full((SUBLANE, D))],
        out_shape=[jax.ShapeDtypeStruct((T, D), F32), jax.ShapeDtypeStruct((D, EV_IN), BF16),
                   jax.ShapeDtypeStruct((SUBLANE, D), F32)],
        scratch_shapes=[pltpu.VMEM((D, EV_IN), F32)])


def _sum_slots(land_ref):
    g = land_ref[0].astype(F32)
    for i in range(1, N_DEV):
        g = g + land_ref[i].astype(F32)
    return g


def _reduce_adam(land, w, m, v, name):
    R, C = w.shape
    rb = R
    for cand in (128, 64, 32, 16, 8):
        if R % cand == 0:
            rb = cand
            break

    def body(l_ref, w_ref, m_ref, v_ref, g_ref, d_ref, nm_ref, nv_ref):
        g = _sum_slots(l_ref)
        g_ref[...] = g
        dlt, m2, v2 = _adam(w_ref[...], g, m_ref[...], v_ref[...])
        d_ref[...] = dlt
        nm_ref[...] = m2
        nv_ref[...] = v2

    t = pl.BlockSpec((rb, C), lambda i: (i, 0))
    shp = jax.ShapeDtypeStruct((R, C), F32)
    return _pallas(
        body, name=name, grid=(R // rb,),
        in_specs=[pl.BlockSpec((N_DEV, rb, C), lambda i: (0, i, 0)), t, t, t],
        out_specs=[t] * 4, out_shape=[shp] * 4,
        compiler_params=_params(("parallel",)),
    )(land, w, m, v)


def _comm_only(comms, name):
    total = sum(cm.n for cm in comms)

    def body(*refs):
        ins, outs, sems = refs[:total], refs[total:2 * total], refs[2 * total:]
        groups, pos = [], 0
        for idx, cm in enumerate(comms):
            groups.append((cm, ins[pos:pos + cm.n], outs[pos:pos + cm.n], sems[3 * idx:3 * idx + 3]))
            pos += cm.n
        for cm, i_, o_, s_ in groups:
            cm.start(i_, o_, s_)
        for cm, i_, o_, s_ in groups:
            if cm.has_mid:
                cm.mid(i_, o_, s_)
        for cm, i_, o_, s_ in groups:
            cm.finish(i_, o_, s_)

    any_spec = pl.BlockSpec(memory_space=pl.ANY)
    res = _pallas(
        body, name=name,
        out_shape=[s for cm in comms for s in cm.out_shapes()],
        in_specs=[any_spec] * total, out_specs=[any_spec] * total,
        scratch_shapes=[s for cm in comms for s in cm.sems()],
    )(*[a for cm in comms for a in cm.arrs])
    out, pos = [], 0
    for cm in comms:
        out.append(list(res[pos:pos + cm.n]))
        pos += cm.n
    return out


def _slots_adam(land, w, m, v, name):
    lead = w.shape[1] if w.ndim == 5 else 1
    inner = w.shape[-3:]
    zeros3 = (0, 0, 0)
    if w.ndim == 5:
        lspec = pl.BlockSpec((N_DEV, 1) + inner, lambda i: (0, i) + zeros3)
        wspec = pl.BlockSpec((1, 1) + inner, lambda i: (0, i) + zeros3)
    else:
        lspec = pl.BlockSpec((N_DEV,) + inner, lambda i: (0,) + zeros3)
        wspec = pl.BlockSpec((1,) + inner, lambda i: (0,) + zeros3)

    def body(l_ref, w_ref, m_ref, v_ref, g_ref, d_ref, nm_ref, nv_ref):
        at = (0, 0) if w.ndim == 5 else (0,)
        g = l_ref[(0,) + at[1:]].astype(F32)
        for i in range(1, N_DEV):
            g = g + l_ref[(i,) + at[1:]].astype(F32)
        dlt, m2, v2 = _adam(w_ref[at], g, m_ref[at], v_ref[at])
        g_ref[at] = g
        d_ref[at] = dlt
        nm_ref[at] = m2
        nv_ref[at] = v2

    shp = jax.ShapeDtypeStruct(w.shape, F32)
    return _pallas(
        body, name=name, grid=(lead,),
        in_specs=[lspec, wspec, wspec, wspec], out_specs=[wspec] * 4, out_shape=[shp] * 4,
        compiler_params=_params(("parallel",)),
    )(land, w, m, v)


SMALL_PARAMS = ("ln_g", "ln_b", "ev_sg_ln_g", "ev_sg_ln_b", "ev_sink", "ev_sg_b",
                "od_conv_w", "od_conv_b", "od_b_a", "od_b_x", "od_lam")


def _small_update(ga, gc, gd, gf, gb, ge, gsink, gbt, params):
    names = list(SMALL_PARAMS)
    flat = [a for nm in names for a in params[nm]]
    n_g = 8

    def body(*refs):
        ga_ref, gc_ref, gd_ref, gf_ref, gb_ref, ge_ref, gs_ref, gbt_ref = refs[:n_g]
        prm = refs[n_g:n_g + 3 * len(names)]
        loss_ref = refs[n_g + 3 * len(names)]
        outs = refs[n_g + 3 * len(names) + 1:]

        def ssum(ref):
            acc = ref[0]
            for i in range(1, N_DEV):
                acc = acc + ref[i]
            return acc

        a, cc, dd, ff, bb, ee = ssum(ga_ref), ssum(gc_ref), ssum(gd_ref), ssum(gf_ref), ssum(gb_ref), ssum(ge_ref)
        loss_ref[...] = a[3:4, 0:LANE]
        me = _slot(*_my_pos())

        def mine(rows):
            acc = jnp.zeros((rows.shape[0], LANE), F32)
            for j in range(N_DEV):
                acc = acc + jnp.where(me == j, rows[:, j * LANE:(j + 1) * LANE], 0.0)
            return acc

        sink_terms = ssum(gs_ref)
        lane8 = lax.broadcasted_iota(jnp.int32, (1, N_HEADS), 1)
        g_sink = jnp.zeros((1, N_HEADS), F32)
        for h in range(N_HEADS):
            tot = -jnp.sum(sink_terms[:, h * LANE:(h + 1) * LANE], axis=1, keepdims=True)
            g_sink = jnp.where(lane8 == h, tot, g_sink)
        grads = dict(
            ln_g=jnp.concatenate([dd[0:1], a[0:1]], axis=0), ln_b=jnp.concatenate([dd[1:2], a[1:2]], axis=0),
            ev_sg_ln_g=ee[0:1], ev_sg_ln_b=ee[1:2], ev_sink=g_sink,
            ev_sg_b=jnp.transpose(ssum(gbt_ref))[0:SG_GROUPS, :],
            od_conv_w=mine(cc[0:4]), od_conv_b=mine(cc[4:5]),
            od_b_a=mine(jnp.concatenate([ff[0:1], bb[0:1]], axis=0)),
            od_b_x=mine(jnp.concatenate([ff[1:2], bb[1:2]], axis=0)),
            od_lam=mine(jnp.concatenate([ff[2:3], bb[2:3]], axis=0)))
        for k, nm in enumerate(names):
            w_ref, m_ref, v_ref = prm[3 * k:3 * k + 3]
            at = (0,) if len(w_ref.shape) == 3 else ()
            g = grads[nm]
            dlt, m2, v2 = _adam(w_ref[at] if at else w_ref[...], g, m_ref[at] if at else m_ref[...],
                                v_ref[at] if at else v_ref[...])
            for o_ref, val in zip(outs[4 * k:4 * k + 4], (g, dlt, m2, v2)):
                if at:
                    o_ref[at] = val
                else:
                    o_ref[...] = val

    gathered = [ga, gc, gd, gf, gb, ge, gsink, gbt]
    out_shape = [jax.ShapeDtypeStruct((1, LANE), F32)]
    for nm in names:
        out_shape += [jax.ShapeDtypeStruct(params[nm][0].shape, F32)] * 4
    return _pallas(
        body, name="small_update", grid=(1,),
        in_specs=[_full(a.shape) for a in gathered + flat],
        out_specs=[_full(s.shape) for s in out_shape], out_shape=out_shape,
        compiler_params=_params(("arbitrary",)),
    )(*gathered, *flat)


VEC_ROWS = 16
VEC_LAYOUT = (("od_conv_w", 4), ("od_conv_b", 1), ("od_b_a", 2), ("od_b_x", 2), ("od_lam", 2))


def _pack_vec(parts):
    rows = [parts[name].reshape(nrows, -1) for name, nrows in VEC_LAYOUT]
    used = sum(r for _, r in VEC_LAYOUT)
    rows.append(jnp.zeros((VEC_ROWS - used, rows[0].shape[1]), F32))
    return jnp.concatenate(rows, axis=0)


def _to_slabs(full, cols_per):
    R = full.shape[0]
    return full.reshape(R, N_DEV, cols_per).transpose(1, 0, 2)


def _from_slabs(slabs):
    n, R, cp = slabs.shape
    return slabs.transpose(1, 0, 2).reshape(R, n * cp)


def kernel(x, c, positions, ada_w, ada_b, ln_g, ln_b, ev_w_in, ev_w_out, ev_sink, ev_sg_ln_g, ev_sg_ln_b, ev_sg_w, ev_sg_b, od_w_in, od_conv_w, od_conv_b, od_w_a, od_b_a, od_w_x, od_b_x, od_lam, od_w_out, loss_target, m_ada_w, m_ada_b, m_ln_g, m_ln_b, m_ev_w_in, m_ev_w_out, m_ev_sink, m_ev_sg_ln_g, m_ev_sg_ln_b, m_ev_sg_w, m_ev_sg_b, m_od_w_in, m_od_conv_w, m_od_conv_b, m_od_w_a, m_od_b_a, m_od_w_x, m_od_b_x, m_od_lam, m_od_w_out, v_ada_w, v_ada_b, v_ln_g, v_ln_b, v_ev_w_in, v_ev_w_out, v_ev_sink, v_ev_sg_ln_g, v_ev_sg_ln_b, v_ev_sg_w, v_ev_sg_b, v_od_w_in, v_od_conv_w, v_od_conv_b, v_od_w_a, v_od_b_a, v_od_w_x, v_od_b_x, v_od_lam, v_od_w_out):
    T = x.shape[1]
    me = _slot(*_my_pos())
    xs = x.reshape(T, D)
    tgt = loss_target.reshape(T, D)

    vec_w = _pack_vec(dict(od_conv_w=od_conv_w[0], od_conv_b=od_conv_b, od_b_a=od_b_a[0], od_b_x=od_b_x[0],
                           od_lam=od_lam[0]))
    c_all, g_ev_in, g_vec = _all_gather([c, ev_w_in[0].astype(BF16), vec_w], "ag_params")
    c_all = c_all.reshape(N_DEV, D)
    w_ev_in = _from_slabs(g_ev_in)
    vec_full = _from_slabs(g_vec)
    cw, cb = vec_full[0:4], vec_full[4:5]
    ba, bx, lam = vec_full[5:7], vec_full[7:9], vec_full[9:11]

    mod_part = _mod_part(c_all, ada_w)
    (mod_all,) = _all_gather([mod_part], "ag_mod")
    mod_mine = lax.dynamic_index_in_dim(mod_all, me, axis=2, keepdims=False)
    mod = mod_mine.transpose(1, 0, 2).reshape(2, 3 * D) + ada_b
    mod0 = mod[0].reshape(3, D)
    mod1 = mod[1].reshape(3, D)

    half = 8
    inv_freq = jnp.power(jnp.float32(ROPE_THETA), -jnp.arange(half, dtype=F32) / half)
    ang = positions.reshape(T).astype(F32)[:, None] * inv_freq
    cos_t = jnp.tile(jnp.cos(ang), (1, LANE // half))
    sin_t = jnp.tile(jnp.sin(ang), (1, LANE // half))
    l64 = jnp.arange(LANE) % HEAD_DIM
    rc = jnp.where(l64 < 2 * half, cos_t, 1.0)
    rs1 = jnp.where(l64 < half, -sin_t, 0.0)
    rs2 = jnp.where((l64 >= half) & (l64 < 2 * half), sin_t, 0.0)

    ln0 = jnp.stack([ln_g[0], ln_b[0]])
    ln1 = jnp.stack([ln_g[1], ln_b[1]])
    sg_lng = ev_sg_ln_g
    sg_lnb = ev_sg_ln_b
    sg_w = ev_sg_w[0].astype(BF16)
    sg_bfull = jnp.repeat(ev_sg_b[0].T, SG_DIM, axis=1)
    sink_l = jnp.repeat(ev_sink, LANE, axis=1)
    kj = jnp.arange(3 * BLK)[:, None]
    qi = jnp.arange(BLK)[None, :]
    band_bias = jnp.where(jnp.abs(kj - BLK - qi) <= BLK, 0.0, NEG_INF).astype(F32)
    lanes = jnp.arange(LANE)
    a128 = jnp.where(lanes[:, None] // SG_DIM == lanes[None, :] // SG_DIM, 1.0 / SG_DIM, 0.0).astype(BF16)
    gsum = (jnp.arange(SG_W)[:, None] // SG_DIM == lanes[None, :]).astype(BF16)
    sel = (jnp.arange(SUBLANE)[:, None] == lanes[None, :] // HEAD_DIM).astype(BF16)
    wa = od_w_a[0].astype(BF16)
    wx = od_w_x[0].astype(BF16)

    (q, kvx, su, sv, g0), (g_ev_out,) = _ev_in(xs, mod0, w_ev_in, rc, rs1, rs2,
                                               _GatherComm([ev_w_out[0].astype(BF16)]))
    w_ev_out = g_ev_out.reshape(D, D)
    (ycat, y0, lse), (g_od_in, g_od_out) = _mix0_fwd(
        q, kvx, su, sv, g0, sink_l, band_bias, a128, sg_lng, sg_lnb, sg_w, sg_bfull,
        _GatherComm([od_w_in[0].astype(BF16), od_w_out[0].astype(BF16)]))
    w_od_in = _from_slabs(g_od_in)
    w_od_out = g_od_out.reshape(D, D)
    out0, z0, x1 = _ev_out(y0, w_ev_out, xs, mod0, ln0)
    xr, g1 = _od_in(x1, mod1, w_od_in)
    hf = _rglru_fwd(xr, cw, cb, wa[0], wx[0], ba[0:1], bx[0:1], lam[0:1], False, "rglru_fwd_f")
    hb = _rglru_fwd(xr, cw, cb, wa[1], wx[1], ba[1:2], bx[1:2], lam[1:2], True, "rglru_fwd_b")
    dh, dg1, dx1p, d_od_out, vec_a = _od_out(hf, hb, g1, w_od_out, x1, tgt, mod1, ln1)

    (dxcf, dwa_f, dwx_f, vec_f), (l_od_out,) = _rglru_bwd(
        xr, dh, hf, cw, cb, wa[0], wx[0], ba[0:1], bx[0:1], lam[0:1], False, "rglru_bwd_f",
        _ExchangeComm([d_od_out.reshape(N_DEV, D // N_DEV, D)]))
    (dxcb, dwa_b, dwx_b, vec_b), _ = _rglru_bwd(xr, dh, hb, cw, cb, wa[1], wx[1], ba[1:2], bx[1:2], lam[1:2],
                                                True, "rglru_bwd_b")
    (dx1, d_od_in, vec_c), (a_wa, a_wx) = _od_in_bwd(
        dxcf, dxcb, xr, dg1, x1, dx1p, mod1, w_od_in, cw,
        _GatherComm([jnp.stack([dwa_f, dwa_b]).astype(BF16), jnp.stack([dwx_f, dwx_b]).astype(BF16)]))
    dxp, dyc, dg0, d_ev_out, vec_d = _ev_out_bwd(dx1, z0, out0, y0, ycat, g0, w_ev_out, mod0, ln0)
    (dq, dkv, dsu, dsv, d_sg_w, d_sg_bt, vec_e, d_sink_l), (l_od_in, l_ev_out) = _mix0_bwd(
        q, kvx, lse, dyc, ycat, su, sv, sink_l, band_bias, a128, gsum, sel, sg_lng, sg_lnb, sg_w, sg_bfull,
        rc, rs1, rs2, _ExchangeComm([d_od_in, d_ev_out.reshape(N_DEV, D // N_DEV, D)]))
    (grad_x, d_ev_in, vec_g), _ = _ev_in_bwd(dq, dkv, dsu, dsv, dg0, xs, dxp, mod0, w_ev_in, rc, rs1, rs2)

    (l_ev_in,), (ga, gc, gd, gf, gb, gg, ge, gsink, gbt, a_sgw) = _comm_only(
        [_ExchangeComm([_to_slabs(d_ev_in, EV_IN // N_DEV)]),
         _GatherComm([vec_a, vec_c, vec_d, vec_f, vec_b, vec_g, vec_e, d_sink_l, d_sg_bt, d_sg_w.astype(BF16)])],
        "tail_exchange")

    dmod_all = jnp.stack([jnp.concatenate([gg[:, 0], gg[:, 1], gd[:, 2]], axis=-1),
                          jnp.concatenate([gc[:, 5], gc[:, 6], ga[:, 2]], axis=-1)], axis=1)
    cols = ada_w.shape[2]
    dmod_cols = lax.dynamic_slice_in_dim(dmod_all, me * cols, cols, axis=2).transpose(1, 0, 2)
    (g_ada_w, d_ada_w, nm_ada_w, nv_ada_w, g_ada_b, d_ada_b, nm_ada_b, nv_ada_b) = _ada_update(
        c_all, dmod_cols, dmod_all, ada_w, m_ada_w, v_ada_w, ada_b, m_ada_b, v_ada_b)

    res = dict(ada_w=[g_ada_w, d_ada_w, nm_ada_w, nv_ada_w], ada_b=[g_ada_b, d_ada_b, nm_ada_b, nv_ada_b])
    for name, land, w, m, v in (("ev_w_in", l_ev_in, ev_w_in, m_ev_w_in, v_ev_w_in),
                                ("ev_w_out", l_ev_out, ev_w_out, m_ev_w_out, v_ev_w_out),
                                ("od_w_in", l_od_in, od_w_in, m_od_w_in, v_od_w_in),
                                ("od_w_out", l_od_out, od_w_out, m_od_w_out, v_od_w_out)):
        res[name] = [a[None] for a in _reduce_adam(land, w[0], m[0], v[0], "adam_" + name)]
    res["od_w_a"] = _slots_adam(a_wa, od_w_a, m_od_w_a, v_od_w_a, "adam_od_w_a")
    res["od_w_x"] = _slots_adam(a_wx, od_w_x, m_od_w_x, v_od_w_x, "adam_od_w_x")
    res["ev_sg_w"] = _slots_adam(a_sgw, ev_sg_w, m_ev_sg_w, v_ev_sg_w, "adam_ev_sg_w")
    small = dict(ln_g=(ln_g, m_ln_g, v_ln_g), ln_b=(ln_b, m_ln_b, v_ln_b),
                 ev_sg_ln_g=(ev_sg_ln_g, m_ev_sg_ln_g, v_ev_sg_ln_g),
                 ev_sg_ln_b=(ev_sg_ln_b, m_ev_sg_ln_b, v_ev_sg_ln_b),
                 ev_sink=(ev_sink, m_ev_sink, v_ev_sink), ev_sg_b=(ev_sg_b, m_ev_sg_b, v_ev_sg_b),
                 od_conv_w=(od_conv_w, m_od_conv_w, v_od_conv_w), od_conv_b=(od_conv_b, m_od_conv_b, v_od_conv_b),
                 od_b_a=(od_b_a, m_od_b_a, v_od_b_a), od_b_x=(od_b_x, m_od_b_x, v_od_b_x),
                 od_lam=(od_lam, m_od_lam, v_od_lam))
    small_out = _small_update(ga, gc, gd, gf, gb, ge, gsink, gbt, small)
    loss = small_out[0][0, 0]
    for k, name in enumerate(SMALL_PARAMS):
        res[name] = small_out[1 + 4 * k:5 + 4 * k]

    order = ["ada_w", "ada_b", "ln_g", "ln_b", "ev_w_in", "ev_w_out", "ev_sink", "ev_sg_ln_g", "ev_sg_ln_b",
             "ev_sg_w", "ev_sg_b", "od_w_in", "od_conv_w", "od_conv_b", "od_w_a", "od_b_a", "od_w_x", "od_b_x",
             "od_lam", "od_w_out"]
    outs = [loss, grad_x.reshape(1, T, D)]
    for kind in range(4):
        outs += [res[name][kind] for name in order]
    return tuple(outs)
```

```python
import functools

import jax
import jax.numpy as jnp
from jax import lax
from jax.experimental import pallas as pl
from jax.experimental.pallas import tpu as pltpu

F32 = jnp.float32
BF16 = jnp.bfloat16

N_DEV = 8
D = 1024
N_HEADS = 8
HEAD_DIM = 64
KV_WIDTH = 128
ATTN_W = 512
SG_W = 512
SG_GROUPS = 8
SG_DIM = 64
BLK = 128
KVX_W = 1024
EV_IN = 2816
OD_IN = 2048
RNN_HEADS = 8
RNN_HD = 128
ALPHA = 4.0 ** 0.25
LN_EPS = 1e-5
NEG_INF = -1e30
RG_C = 8.0
ROPE_THETA = 500000.0
LR, B1, B2, EPS, WD, STEP = 0.001, 0.9, 0.999, 1e-08, 0.01, 10

LANE = 128
SUBLANE = 8
TM = 256
TS = 256
VMEM_LIMIT = 56 * 1024 * 1024

MESH = pl.DeviceIdType.MESH


def _pallas(body, **kw):
    return pl.pallas_call(body, **kw)


def _params(sem, vmem=VMEM_LIMIT):
    return pltpu.CompilerParams(dimension_semantics=sem, vmem_limit_bytes=vmem)


def _sigmoid(x):
    return 0.5 * jnp.tanh(0.5 * x) + 0.5


def _silu_and_grad(x):
    s = _sigmoid(x)
    return x * s, s * (1.0 + x * (1.0 - s))


def _dot(a, b):
    return jnp.dot(a.astype(BF16), b.astype(BF16), preferred_element_type=F32)


def _dot_nt(a, b):
    return lax.dot_general(a.astype(BF16), b.astype(BF16), (((1,), (1,)), ((), ())), preferred_element_type=F32)


def _dot_tn(a, b):
    return lax.dot_general(a.astype(BF16), b.astype(BF16), (((0,), (0,)), ((), ())), preferred_element_type=F32)


def _ln_fwd(z, g, b):
    mu = jnp.mean(z, axis=-1, keepdims=True)
    zc = z - mu
    var = jnp.mean(zc * zc, axis=-1, keepdims=True)
    rstd = lax.rsqrt(var + LN_EPS)
    xhat = zc * rstd
    return xhat * g + b, xhat, rstd


def _ln_bwd(dy, xhat, rstd, g):
    dxh = dy * g
    m1 = jnp.mean(dxh, axis=-1, keepdims=True)
    m2 = jnp.mean(dxh * xhat, axis=-1, keepdims=True)
    return rstd * (dxh - m1 - xhat * m2)


def _rowsum(v):
    return jnp.sum(v, axis=0, keepdims=True)


def _rope_fwd(t, c, s1, s2):
    return t * c + pltpu.roll(t, LANE - 8, 1) * s1 + pltpu.roll(t, 8, 1) * s2


def _rope_bwd(d, c, s1, s2):
    return d * c + pltpu.roll(d * s1, 8, 1) + pltpu.roll(d * s2, LANE - 8, 1)


def _adam(w, g, m, v):
    m2 = B1 * m + (1.0 - B1) * g
    v2 = B2 * v + (1.0 - B2) * (g * g)
    m_hat = m2 / (1.0 - B1 ** STEP)
    v_hat = v2 / (1.0 - B2 ** STEP)
    delta = -LR * (m_hat / (jnp.sqrt(v_hat) + EPS) + WD * w)
    return delta, m2, v2


def _tile(rows, width):
    return pl.BlockSpec((rows, width), lambda i: (i, 0))


def _full(shape):
    zeros = (0,) * len(shape)
    return pl.BlockSpec(shape, lambda i: zeros)


def _rev_tile(rows, width, n, reverse):
    if reverse:
        return pl.BlockSpec((rows, width), lambda i: (n - 1 - i, 0))
    return pl.BlockSpec((rows, width), lambda i: (i, 0))


def _halo_specs(rows, width, n, total_rows, reverse):
    per = rows // SUBLANE
    last = total_rows // SUBLANE - 1

    def tile_of(i):
        return (n - 1 - i) if reverse else i

    prev = pl.BlockSpec((SUBLANE, width), lambda i: (jnp.maximum(tile_of(i) * per - 1, 0), 0))
    nxt = pl.BlockSpec((SUBLANE, width), lambda i: (jnp.minimum((tile_of(i) + 1) * per, last), 0))
    return prev, nxt


def _my_pos():
    return lax.axis_index("x"), lax.axis_index("y"), lax.axis_index("c")


def _slot(px, py, pc):
    return 4 * px + 2 * py + pc


def _all_gather(arrs, name):
    n = len(arrs)

    def body(*refs):
        ins, outs = refs[:n], refs[n:2 * n]
        send_sems, recv_sems, local_sems = refs[2 * n:]
        x, y, c = _my_pos()
        me, sibling = (x, y, c), (x, y, 1 - c)
        chips = [(1 - x, y), (x, 1 - y), (1 - x, 1 - y)]

        def copy(a, k, block, to, src=None):
            dst = outs[a].at[_slot(*block)]
            return pltpu.make_async_remote_copy(
                src_ref=dst if src is None else src, dst_ref=dst,
                send_sem=send_sems.at[a * 7 + k], recv_sem=recv_sems.at[a * 7 + k],
                device_id=to, device_id_type=MESH)

        local, first = [], []
        for a in range(n):
            lc = pltpu.make_async_copy(ins[a], outs[a].at[_slot(*me)], local_sems.at[a])
            lc.start()
            local.append(lc)
            first.append(copy(a, 0, me, sibling, src=ins[a]))
            first += [copy(a, 1 + j, me, (*chip, c), src=ins[a]) for j, chip in enumerate(chips)]
        for cp in first:
            cp.start()
        passed = []
        for j, chip in enumerate(chips):
            for a in range(n):
                copy(a, 1 + j, (*chip, c), me).wait_recv()
                fw = copy(a, 4 + j, (*chip, c), sibling)
                fw.start()
                passed.append(fw)
        for a in range(n):
            copy(a, 0, sibling, me).wait_recv()
            for j, chip in enumerate(chips):
                copy(a, 4 + j, (*chip, 1 - c), me).wait_recv()
        for cp in first + passed:
            cp.wait_send()
        for lc in local:
            lc.wait()

    any_spec = pl.BlockSpec(memory_space=pl.ANY)
    return _pallas(
        body, name=name,
        out_shape=[jax.ShapeDtypeStruct((N_DEV,) + a.shape, a.dtype) for a in arrs],
        in_specs=[any_spec] * n, out_specs=[any_spec] * n,
        scratch_shapes=[pltpu.SemaphoreType.DMA((7 * n,)), pltpu.SemaphoreType.DMA((7 * n,)),
                        pltpu.SemaphoreType.DMA((n,))],
    )(*arrs)


def _all_to_all(arrs, name):
    n = len(arrs)

    def body(*refs):
        ins, outs = refs[:n], refs[n:2 * n]
        send_sems, recv_sems, local_sems = refs[2 * n:]
        x, y, c = _my_pos()
        mine = _slot(x, y, c)
        copies = []
        for a in range(n):
            lc = pltpu.make_async_copy(ins[a].at[mine], outs[a].at[mine], local_sems.at[a])
            lc.start()
            copies.append(lc)
        for k in range(1, N_DEV):
            px = (1 - x) if (k & 4) else x
            py = (1 - y) if (k & 2) else y
            pc = (1 - c) if (k & 1) else c
            for a in range(n):
                cp = pltpu.make_async_remote_copy(
                    src_ref=ins[a].at[_slot(px, py, pc)], dst_ref=outs[a].at[mine],
                    send_sem=send_sems.at[a * 7 + k - 1], recv_sem=recv_sems.at[a * 7 + k - 1],
                    device_id=(px, py, pc), device_id_type=MESH)
                cp.start()
                copies.append(cp)
        for cp in copies:
            cp.wait()

    any_spec = pl.BlockSpec(memory_space=pl.ANY)
    return _pallas(
        body, name=name,
        out_shape=[jax.ShapeDtypeStruct(a.shape, a.dtype) for a in arrs],
        in_specs=[any_spec] * n, out_specs=[any_spec] * n,
        scratch_shapes=[pltpu.SemaphoreType.DMA((7 * n,)), pltpu.SemaphoreType.DMA((7 * n,)),
                        pltpu.SemaphoreType.DMA((n,))],
    )(*arrs)


class _GatherComm:
    has_mid = True

    def __init__(self, arrs, mid_frac=0.5):
        self.arrs = list(arrs)
        self.n = len(self.arrs)
        self.mid_frac = mid_frac

    def out_shapes(self):
        return [jax.ShapeDtypeStruct((N_DEV,) + a.shape, a.dtype) for a in self.arrs]

    def sems(self):
        return [pltpu.SemaphoreType.DMA((7 * self.n,)), pltpu.SemaphoreType.DMA((7 * self.n,)),
                pltpu.SemaphoreType.DMA((self.n,))]

    def _parts(self, ins, outs, sems):
        send_sems, recv_sems, local_sems = sems
        x, y, c = _my_pos()
        me, sibling = (x, y, c), (x, y, 1 - c)
        chips = [(1 - x, y), (x, 1 - y), (1 - x, 1 - y)]

        def copy(a, k, block, to, src=None):
            dst = outs[a].at[_slot(*block)]
            return pltpu.make_async_remote_copy(
                src_ref=dst if src is None else src, dst_ref=dst,
                send_sem=send_sems.at[a * 7 + k], recv_sem=recv_sems.at[a * 7 + k],
                device_id=to, device_id_type=MESH)

        local = [pltpu.make_async_copy(ins[a], outs[a].at[_slot(*me)], local_sems.at[a]) for a in range(self.n)]
        first = []
        for a in range(self.n):
            first.append(copy(a, 0, me, sibling, src=ins[a]))
            first += [copy(a, 1 + j, me, (*chip, c), src=ins[a]) for j, chip in enumerate(chips)]
        ici_in = [copy(a, 1 + j, (*chip, c), me) for j, chip in enumerate(chips) for a in range(self.n)]
        passed = [copy(a, 4 + j, (*chip, c), sibling) for j, chip in enumerate(chips) for a in range(self.n)]
        d2d_in = []
        for a in range(self.n):
            d2d_in.append(copy(a, 0, sibling, me))
            d2d_in += [copy(a, 4 + j, (*chip, 1 - c), me) for j, chip in enumerate(chips)]
        return local, first, ici_in, passed, d2d_in

    def start(self, ins, outs, sems):
        local, first, _, _, _ = self._parts(ins, outs, sems)
        for cp in local + first:
            cp.start()

    def mid(self, ins, outs, sems):
        _, _, ici_in, passed, _ = self._parts(ins, outs, sems)
        for arrived, fw in zip(ici_in, passed):
            arrived.wait_recv()
            fw.start()

    def finish(self, ins, outs, sems):
        local, first, _, passed, d2d_in = self._parts(ins, outs, sems)
        for cp in d2d_in:
            cp.wait_recv()
        for cp in first + passed:
            cp.wait_send()
        for cp in local:
            cp.wait()


class _ExchangeComm:
    has_mid = False

    def __init__(self, arrs):
        self.arrs = list(arrs)
        self.n = len(self.arrs)

    def out_shapes(self):
        return [jax.ShapeDtypeStruct(a.shape, a.dtype) for a in self.arrs]

    def sems(self):
        return [pltpu.SemaphoreType.DMA((7 * self.n,)), pltpu.SemaphoreType.DMA((7 * self.n,)),
                pltpu.SemaphoreType.DMA((self.n,))]

    def _copies(self, ins, outs, sems):
        send_sems, recv_sems, local_sems = sems
        x, y, c = _my_pos()
        mine = _slot(x, y, c)
        copies = [pltpu.make_async_copy(ins[a].at[mine], outs[a].at[mine], local_sems.at[a]) for a in range(self.n)]
        for k in range(1, N_DEV):
            px = (1 - x) if (k & 4) else x
            py = (1 - y) if (k & 2) else y
            pc = (1 - c) if (k & 1) else c
            for a in range(self.n):
                copies.append(pltpu.make_async_remote_copy(
                    src_ref=ins[a].at[_slot(px, py, pc)], dst_ref=outs[a].at[mine],
                    send_sem=send_sems.at[a * 7 + k - 1], recv_sem=recv_sems.at[a * 7 + k - 1],
                    device_id=(px, py, pc), device_id_type=MESH))
        return copies

    def start(self, ins, outs, sems):
        for cp in self._copies(ins, outs, sems):
            cp.start()

    def finish(self, ins, outs, sems):
        for cp in self._copies(ins, outs, sems):
            cp.wait()


def _fused_call(body, comm, operands, *, name, grid, in_specs, out_specs, out_shape, scratch_shapes=(),
                semantics=("arbitrary",)):
    n_in, n_out, n_scr = len(in_specs), len(out_specs), len(scratch_shapes)
    if comm is None:
        res = _pallas(body, name=name, grid=grid, in_specs=list(in_specs), out_specs=list(out_specs),
                      out_shape=list(out_shape), scratch_shapes=list(scratch_shapes),
                      compiler_params=_params(semantics))(*operands)
        return list(res), []
    k = comm.n
    steps = grid[0]

    def wrapped(*refs):
        ins, cins = refs[:n_in], refs[n_in:n_in + k]
        outs = refs[n_in + k:n_in + k + n_out]
        couts = refs[n_in + k + n_out:n_in + 2 * k + n_out]
        rest = refs[n_in + 2 * k + n_out:]
        scratch, sems = rest[:n_scr], rest[n_scr:]
        i = pl.program_id(0)

        @pl.when(i == 0)
        def _():
            comm.start(cins, couts, sems)

        body(*ins, *outs, *scratch)

        if comm.has_mid:
            @pl.when(i == int(steps * comm.mid_frac))
            def _():
                comm.mid(cins, couts, sems)

        @pl.when(i == steps - 1)
        def _():
            comm.finish(cins, couts, sems)

    any_spec = pl.BlockSpec(memory_space=pl.ANY)
    res = _pallas(wrapped, name=name, grid=grid, in_specs=list(in_specs) + [any_spec] * k,
                  out_specs=list(out_specs) + [any_spec] * k, out_shape=list(out_shape) + comm.out_shapes(),
                  scratch_shapes=list(scratch_shapes) + comm.sems(),
                  compiler_params=_params(("arbitrary",)))(*operands, *comm.arrs)
    return list(res[:n_out]), list(res[n_out:])


def _mod_part(c_all, ada_w):
    cols = ada_w.shape[2]

    def body(c_ref, w_ref, o_ref):
        cv = c_ref[...]
        cond = cv * _sigmoid(cv)
        for l in range(2):
            o_ref[l] = _dot(cond, w_ref[l])

    return _pallas(
        body, name="mod_part", grid=(1,),
        in_specs=[_full((N_DEV, D)), _full((2, D, cols))],
        out_specs=_full((2, N_DEV, cols)),
        out_shape=jax.ShapeDtypeStruct((2, N_DEV, cols), F32),
        compiler_params=_params(("arbitrary",)),
    )(c_all, ada_w)


def _ada_update(c_all, dmod_cols, dmod_all, ada_w, m_w, v_w, ada_b, m_b, v_b):
    cols = ada_w.shape[2]
    nb = ada_b.shape[1]

    def body(c_ref, dmc_ref, dma_ref, w_ref, mw_ref, vw_ref, b_ref, mb_ref, vb_ref,
             gw_ref, dw_ref, nmw_ref, nvw_ref, gb_ref, db_ref, nmb_ref, nvb_ref):
        cv = c_ref[...]
        cond = cv * _sigmoid(cv)
        for l in range(2):
            g = _dot_tn(cond, dmc_ref[l])
            gw_ref[l] = g
            dlt, m2, v2 = _adam(w_ref[l], g, mw_ref[l], vw_ref[l])
            dw_ref[l] = dlt
            nmw_ref[l] = m2
            nvw_ref[l] = v2
        gb = dma_ref[0]
        for i in range(1, N_DEV):
            gb = gb + dma_ref[i]
        gb_ref[...] = gb
        dlt, m2, v2 = _adam(b_ref[...], gb, mb_ref[...], vb_ref[...])
        db_ref[...] = dlt
        nmb_ref[...] = m2
        nvb_ref[...] = v2

    wspec = _full((2, D, cols))
    bspec = _full((2, nb))
    wshape = jax.ShapeDtypeStruct((2, D, cols), F32)
    bshape = jax.ShapeDtypeStruct((2, nb), F32)
    return _pallas(
        body, name="ada_update", grid=(1,),
        in_specs=[_full((N_DEV, D)), _full((2, N_DEV, cols)), _full((N_DEV, 2, nb)),
                  wspec, wspec, wspec, bspec, bspec, bspec],
        out_specs=[wspec] * 4 + [bspec] * 4,
        out_shape=[wshape] * 4 + [bshape] * 4,
        compiler_params=_params(("arbitrary",)),
    )(c_all, dmod_cols, dmod_all, ada_w, m_w, v_w, ada_b, m_b, v_b)


def _ev_in(x, mod, w_in, rc, rs1, rs2, comm=None):
    T = x.shape[0]

    def body(x_ref, mod_ref, w_ref, c_ref, s1_ref, s2_ref, q_ref, kv_ref, su_ref, sv_ref, g_ref):
        h = x_ref[...] * (1.0 + mod_ref[1:2, :]) + mod_ref[0:1, :]
        p = _dot(h, w_ref[...])
        c, s1, s2 = c_ref[...], s1_ref[...], s2_ref[...]
        for j in range(ATTN_W // LANE):
            qr = _rope_fwd(p[:, j * LANE:(j + 1) * LANE], c, s1, s2)
            q_ref[:, j * LANE:(j + 1) * LANE] = (qr * (HEAD_DIM ** -0.5)).astype(BF16)
        low = lax.broadcasted_iota(jnp.int32, (TM, LANE), 1) < HEAD_DIM
        for j, val in enumerate((_rope_fwd(p[:, 512:640], c, s1, s2), p[:, 640:768])):
            swapped = pltpu.roll(val, HEAD_DIM, 1)
            tiles = (jnp.where(low, val, 0.0), jnp.where(low, 0.0, swapped),
                     jnp.where(low, swapped, 0.0), jnp.where(low, 0.0, val))
            for k, tile in enumerate(tiles):
                kv_ref[:, (4 * j + k) * LANE:(4 * j + k + 1) * LANE] = tile.astype(BF16)
        su_ref[...] = p[:, 768:1280].astype(BF16)
        sv_ref[...] = p[:, 1280:1792].astype(BF16)
        g_ref[...] = p[:, 1792:2816].astype(BF16)

    sh = lambda w: jax.ShapeDtypeStruct((T, w), BF16)
    return _fused_call(
        body, comm, (x, mod, w_in, rc, rs1, rs2), name="ev_in", grid=(T // TM,),
        in_specs=[_tile(TM, D), _full((3, D)), _full((D, EV_IN)), _tile(TM, LANE), _tile(TM, LANE), _tile(TM, LANE)],
        out_specs=[_tile(TM, ATTN_W), _tile(TM, KVX_W), _tile(TM, SG_W), _tile(TM, SG_W), _tile(TM, D)],
        out_shape=[sh(ATTN_W), sh(KVX_W), sh(SG_W), sh(SG_W), sh(D)], semantics=("parallel",))


def _band_specs(width, nb):
    return [pl.BlockSpec((BLK, width), lambda n: (jnp.maximum(n - 1, 0), 0)),
            pl.BlockSpec((BLK, width), lambda n: (n, 0)),
            pl.BlockSpec((BLK, width), lambda n: (jnp.minimum(n + 1, nb - 1), 0))]


def _band_bias(bias_ref, n, nb):
    rows = lax.broadcasted_iota(jnp.int32, (3 * BLK, 1), 0)
    outside = ((rows < BLK) & (n == 0)) | ((rows >= 2 * BLK) & (n == nb - 1))
    return bias_ref[...] + jnp.where(outside, NEG_INF, 0.0)


def _split_bf16(v):
    hi = v.astype(BF16)
    return hi, (v - hi.astype(F32)).astype(BF16)


def _group_mean(v, a_ref, exact_bf16=False):
    hi, lo = _split_bf16(v)
    a = a_ref[...]
    out = []
    for t in range(SG_W // LANE):
        sl = slice(t * LANE, (t + 1) * LANE)
        r = jnp.dot(hi[:, sl], a, preferred_element_type=F32)
        if not exact_bf16:
            r = r + jnp.dot(lo[:, sl], a, preferred_element_type=F32)
        out.append(r)
    return jnp.concatenate(out, axis=-1)


def _sg_core(sv_ref, lng, lnb, a_ref, w_ref, bfull_ref):
    svf = sv_ref[...].astype(F32)
    xc = svf - _group_mean(svf, a_ref, exact_bf16=True)
    rstd = lax.rsqrt(_group_mean(xc * xc, a_ref) + LN_EPS)
    xhat = xc * rstd
    vb = (xhat * lng + lnb).astype(BF16)
    low = lax.broadcasted_iota(jnp.int32, (BLK, LANE), 1) < SG_DIM
    tiles = []
    for t in range(SG_W // LANE):
        v2 = vb[:, t * LANE:(t + 1) * LANE]
        r0 = jnp.dot(w_ref[2 * t], v2, preferred_element_type=F32)
        r1 = jnp.dot(w_ref[2 * t + 1], v2, preferred_element_type=F32)
        tiles.append(jnp.where(low, r0, r1))
    svm = jnp.concatenate(tiles, axis=-1) + bfull_ref[...]
    return xhat, rstd, vb, svm


def _mix0_fwd(q, kvx, su, sv, g0, sink_l, bias, a128, sg_lng, sg_lnb, sg_w, sg_bfull, comm=None):
    T = q.shape[0]
    nb = T // BLK

    def body(q_ref, kp_ref, kc_ref, kn_ref, su_ref, sv_ref, g_ref, sink_ref, bias_ref, a_ref, lng_ref, lnb_ref,
             w_ref, bfull_ref, ycat_ref, y0_ref, lse_ref):
        n = pl.program_id(0)
        bias = _band_bias(bias_ref, n, nb)
        kvx = jnp.concatenate([kp_ref[...], kc_ref[...], kn_ref[...]], axis=0)
        tiles = []
        for t in range(ATTN_W // LANE):
            qt = q_ref[:, t * LANE:(t + 1) * LANE]
            acc = None
            for par in range(2):
                h = 2 * t + par
                kt = 2 * (h // 4) + par
                ke = kvx[:, kt * LANE:(kt + 1) * LANE]
                ve = kvx[:, (4 + kt) * LANE:(5 + kt) * LANE]
                st = _dot_nt(ke, qt) + bias
                sk = sink_ref[:, h * LANE:(h + 1) * LANE]
                m = jnp.maximum(jnp.max(st, axis=0, keepdims=True), sk)
                p = jnp.exp(st - m)
                denom = jnp.sum(p, axis=0, keepdims=True) + jnp.exp(sk - m)
                contrib = _dot_tn(p * (1.0 / denom), ve)
                acc = contrib if acc is None else acc + contrib
                lse_ref[0, :, h * LANE:(h + 1) * LANE] = m + jnp.log(denom)
            tiles.append(acc)
        _, _, _, svm = _sg_core(sv_ref, lng_ref[...], lnb_ref[...], a_ref, w_ref, bfull_ref)
        tiles.append(su_ref[...].astype(F32) * svm)
        ycat = jnp.concatenate(tiles, axis=-1)
        gf = g_ref[...].astype(F32)
        ycat_ref[...] = ycat.astype(BF16)
        y0_ref[...] = (ycat * (gf * _sigmoid(gf))).astype(BF16)

    return _fused_call(
        body, comm, (q, kvx, kvx, kvx, su, sv, g0, sink_l, bias, a128, sg_lng, sg_lnb, sg_w, sg_bfull),
        name="mix0_fwd", grid=(nb,),
        in_specs=[_tile(BLK, ATTN_W)] + _band_specs(KVX_W, nb) + [
            _tile(BLK, SG_W), _tile(BLK, SG_W), _tile(BLK, D), _full((1, N_HEADS * LANE)), _full((3 * BLK, LANE)),
            _full((LANE, LANE)), _full((1, SG_W)), _full((1, SG_W)), _full((SG_GROUPS, BLK, BLK)),
            _full((BLK, SG_W))],
        out_specs=[_tile(BLK, D), _tile(BLK, D), pl.BlockSpec((1, 1, N_HEADS * LANE), lambda n: (n, 0, 0))],
        out_shape=[jax.ShapeDtypeStruct((T, D), BF16), jax.ShapeDtypeStruct((T, D), BF16),
                   jax.ShapeDtypeStruct((nb, 1, N_HEADS * LANE), F32)], semantics=("parallel",))


def _ev_out(y0, w_out, x, mod, lnp):
    T = x.shape[0]

    def body(y_ref, w_ref, x_ref, mod_ref, ln_ref, out_ref, z_ref, x1_ref):
        out = _dot(y_ref[...], w_ref[...])
        z = ALPHA * x_ref[...] + mod_ref[2:3, :] * out
        x1, _, _ = _ln_fwd(z, ln_ref[0:1, :], ln_ref[1:2, :])
        out_ref[...] = out.astype(BF16)
        z_ref[...] = z
        x1_ref[...] = x1

    return _pallas(
        body, name="ev_out", grid=(T // TM,),
        in_specs=[_tile(TM, D), _full((D, D)), _tile(TM, D), _full((3, D)), _full((2, D))],
        out_specs=[_tile(TM, D)] * 3,
        out_shape=[jax.ShapeDtypeStruct((T, D), BF16), jax.ShapeDtypeStruct((T, D), F32),
                   jax.ShapeDtypeStruct((T, D), F32)],
        compiler_params=_params(("parallel",)),
    )(y0, w_out, x, mod, lnp)


def _od_in(x1, mod, w_in):
    T = x1.shape[0]

    def body(x_ref, mod_ref, w_ref, xr_ref, g_ref):
        h = x_ref[...] * (1.0 + mod_ref[1:2, :]) + mod_ref[0:1, :]
        p = _dot(h, w_ref[...])
        xr_ref[...] = p[:, :D]
        g_ref[...] = p[:, D:].astype(BF16)

    return _pallas(
        body, name="od_in", grid=(T // TM,),
        in_specs=[_tile(TM, D), _full((3, D)), _full((D, OD_IN))],
        out_specs=[_tile(TM, D), _tile(TM, D)],
        out_shape=[jax.ShapeDtypeStruct((T, D), F32), jax.ShapeDtypeStruct((T, D), BF16)],
        compiler_params=_params(("parallel",)),
    )(x1, mod, w_in)


def _ext_rows(prev_ref, cur, next_ref, j, n):
    prev = jnp.where(j > 0, prev_ref[...], 0.0)
    nxt = jnp.where(j < n - 1, next_ref[...], 0.0)
    return jnp.concatenate([prev, cur, nxt], axis=0)


def _shift_rows(ext, off, rows):
    total = ext.shape[0]
    if off == 0:
        return ext[SUBLANE:SUBLANE + rows, :]
    return pltpu.roll(ext, (-off) % total, 0)[SUBLANE:SUBLANE + rows, :]


def _conv_fwd(ext, cw, cb, rows):
    xc = cb
    for k in range(4):
        xc = xc + cw[k:k + 1, :] * _shift_rows(ext, k - 2, rows)
    return xc


def _gates(xc, wa_ref, wx_ref, ba, bx, lam):
    pr, pi = [], []
    for h in range(RNN_HEADS):
        xh = xc[:, h * RNN_HD:(h + 1) * RNN_HD].astype(BF16)
        pr.append(_dot(xh, wa_ref[h]))
        pi.append(_dot(xh, wx_ref[h]))
    r = _sigmoid(jnp.concatenate(pr, axis=-1) + ba)
    ig = _sigmoid(jnp.concatenate(pi, axis=-1) + bx)
    sp = jnp.maximum(-lam, 0.0) + jnp.log(1.0 + jnp.exp(-jnp.abs(lam)))
    neg_log_a = RG_C * r * sp
    a = jnp.exp(-neg_log_a)
    s2 = (1.0 + a * a) * jnp.tanh(neg_log_a)
    inv_s = lax.rsqrt(jnp.maximum(s2, 1e-30))
    return r, ig, sp, a, s2 * inv_s, inv_s


def _scan_tile(a_ref, b_ref, o_ref, carry_ref, rows, reverse):
    ridx = lax.broadcasted_iota(jnp.int32, (SUBLANE, D), 0)
    groups = rows // SUBLANE

    def group(gi, h):
        g = (groups - 1 - gi) if reverse else gi
        off = pl.multiple_of(g * SUBLANE, SUBLANE)
        a = a_ref[pl.ds(off, SUBLANE), :]
        b = b_ref[pl.ds(off, SUBLANE), :]
        for sh in (1, 2, 4):
            if reverse:
                keep = ridx < SUBLANE - sh
                a_p = jnp.where(keep, pltpu.roll(a, SUBLANE - sh, 0), 1.0)
                b_p = jnp.where(keep, pltpu.roll(b, SUBLANE - sh, 0), 0.0)
            else:
                keep = ridx >= sh
                a_p = jnp.where(keep, pltpu.roll(a, sh, 0), 1.0)
                b_p = jnp.where(keep, pltpu.roll(b, sh, 0), 0.0)
            b = b + a * b_p
            a = a * a_p
        hh = b + a * h
        o_ref[pl.ds(off, SUBLANE), :] = hh
        return hh[0:1, :] if reverse else hh[SUBLANE - 1:SUBLANE, :]

    carry_ref[...] = lax.fori_loop(0, groups, group, carry_ref[...])


def _rglru_fwd(xr, cw, cb, wa, wx, ba, bx, lam, reverse, name):
    T = xr.shape[0]
    n = T // TS
    prev_spec, next_spec = _halo_specs(TS, D, n, T, reverse)

    def body(prev_ref, cur_ref, next_ref, cw_ref, cb_ref, wa_ref, wx_ref, ba_ref, bx_ref, lam_ref,
             h_ref, a_s, b_s, carry):
        i = pl.program_id(0)
        j = (n - 1 - i) if reverse else i

        @pl.when(i == 0)
        def _():
            carry[...] = jnp.zeros_like(carry)

        ext = _ext_rows(prev_ref, cur_ref[...], next_ref, j, n)
        xc = _conv_fwd(ext, cw_ref[...], cb_ref[...], TS)
        _, ig, _, a, s, _ = _gates(xc, wa_ref, wx_ref, ba_ref[...], bx_ref[...], lam_ref[...])
        a_s[...] = a
        b_s[...] = s * ig * xc
        _scan_tile(a_s, b_s, h_ref, carry, TS, reverse)

    wspec = _full((RNN_HEADS, RNN_HD, RNN_HD))
    return _pallas(
        body, name=name, grid=(n,),
        in_specs=[prev_spec, _rev_tile(TS, D, n, reverse), next_spec, _full((4, D)), _full((1, D)),
                  wspec, wspec, _full((1, D)), _full((1, D)), _full((1, D))],
        out_specs=_rev_tile(TS, D, n, reverse),
        out_shape=jax.ShapeDtypeStruct((T, D), F32),
        scratch_shapes=[pltpu.VMEM((TS, D), F32), pltpu.VMEM((TS, D), F32), pltpu.VMEM((1, D), F32)],
        compiler_params=_params(("arbitrary",)),
    )(xr, xr, xr, cw, cb, wa, wx, ba, bx, lam)


def _od_out(hf, hb, g1, w_out, x1, tgt, mod, lnp):
    T = x1.shape[0]

    def body(hf_ref, hb_ref, g_ref, w_ref, x_ref, t_ref, mod_ref, ln_ref,
             dh_ref, dg_ref, dx_ref, dwb_ref, vec_ref, dw_ref):
        i = pl.program_id(0)

        @pl.when(i == 0)
        def _():
            dw_ref[...] = jnp.zeros_like(dw_ref)
            vec_ref[...] = jnp.zeros_like(vec_ref)

        hs = hf_ref[...] + hb_ref[...]
        sg, dsg = _silu_and_grad(g_ref[...].astype(F32))
        yr = (hs * sg).astype(BF16)
        w = w_ref[...]
        out = _dot(yr, w)
        gate = mod_ref[2:3, :]
        z = ALPHA * x_ref[...] + gate * out
        lng = ln_ref[0:1, :]
        x2, xhat, rstd = _ln_fwd(z, lng, ln_ref[1:2, :])
        diff = x2 - t_ref[...]
        vec_ref[3:4, 0:LANE] += 0.5 * jnp.sum(diff * diff) * (1.0 / D)
        dx2 = diff * (1.0 / D)
        dz = _ln_bwd(dx2, xhat, rstd, lng)
        vec_ref[0:1, :] += _rowsum(dx2 * xhat)
        vec_ref[1:2, :] += _rowsum(dx2)
        vec_ref[2:3, :] += _rowsum(dz * out)
        dout = (dz * gate).astype(BF16)
        dyr = _dot_nt(dout, w)
        dw_ref[...] += _dot_tn(yr, dout)
        dh_ref[...] = dyr * sg
        dg_ref[...] = (dyr * hs * dsg).astype(BF16)
        dx_ref[...] = ALPHA * dz

        @pl.when(i == T // TM - 1)
        def _():
            dwb_ref[...] = dw_ref[...].astype(BF16)

    return _pallas(
        body, name="od_out", grid=(T // TM,),
        in_specs=[_tile(TM, D), _tile(TM, D), _tile(TM, D), _full((D, D)), _tile(TM, D), _tile(TM, D),
                  _full((3, D)), _full((2, D))],
        out_specs=[_tile(TM, D), _tile(TM, D), _tile(TM, D), _full((D, D)), _full((SUBLANE, D))],
        out_shape=[jax.ShapeDtypeStruct((T, D), F32), jax.ShapeDtypeStruct((T, D), BF16),
                   jax.ShapeDtypeStruct((T, D), F32), jax.ShapeDtypeStruct((D, D), BF16),
                   jax.ShapeDtypeStruct((SUBLANE, D), F32)],
        scratch_shapes=[pltpu.VMEM((D, D), F32)],
        compiler_params=_params(("arbitrary",)),
    )(hf, hb, g1, w_out, x1, tgt, mod, lnp)


def _rglru_bwd(xr, dh, h, cw, cb, wa, wx, ba, bx, lam, reverse, name, comm=None):
    T = xr.shape[0]
    n = T // TS
    adj_rev = not reverse
    xprev_spec, xnext_spec = _halo_specs(TS, D, n, T, adj_rev)
    hprev_spec, hnext_spec = _halo_specs(TS, D, n, T, adj_rev)
    h_halo_spec = hnext_spec if reverse else hprev_spec

    def body(xprev_ref, xcur_ref, xnext_ref, dh_ref, h_ref, hh_ref, cw_ref, cb_ref, wa_ref, wx_ref,
             ba_ref, bx_ref, lam_ref, dxc_ref, dwa_ref, dwx_ref, vec_ref, a_s, b_s, l_s, carry, a_edge):
        i = pl.program_id(0)
        j = (n - 1 - i) if adj_rev else i

        @pl.when(i == 0)
        def _():
            carry[...] = jnp.zeros_like(carry)
            a_edge[...] = jnp.zeros_like(a_edge)
            dwa_ref[...] = jnp.zeros_like(dwa_ref)
            dwx_ref[...] = jnp.zeros_like(dwx_ref)
            vec_ref[...] = jnp.zeros_like(vec_ref)

        ext = _ext_rows(xprev_ref, xcur_ref[...], xnext_ref, j, n)
        xc = _conv_fwd(ext, cw_ref[...], cb_ref[...], TS)
        lam = lam_ref[...]
        r, ig, sp, a, s, inv_s = _gates(xc, wa_ref, wx_ref, ba_ref[...], bx_ref[...], lam)

        rows = lax.broadcasted_iota(jnp.int32, (TS, D), 0)
        hcur = h_ref[...]
        if reverse:
            a_sh = jnp.where(rows == 0, a_edge[...], pltpu.roll(a, 1, 0))
            halo = jnp.where(j < n - 1, hh_ref[0:1, :], 0.0)
            h_nb = jnp.where(rows == TS - 1, halo, pltpu.roll(hcur, TS - 1, 0))
        else:
            a_sh = jnp.where(rows == TS - 1, a_edge[...], pltpu.roll(a, TS - 1, 0))
            halo = jnp.where(j > 0, hh_ref[SUBLANE - 1:SUBLANE, :], 0.0)
            h_nb = jnp.where(rows == 0, halo, pltpu.roll(hcur, 1, 0))
        a_s[...] = a_sh
        b_s[...] = dh_ref[...]
        _scan_tile(a_s, b_s, l_s, carry, TS, adj_rev)
        a_edge[...] = a[TS - 1:TS, :] if reverse else a[0:1, :]

        lm = l_s[...]
        da = lm * h_nb
        di = lm * s * xc
        dxc = lm * s * ig
        ds = lm * ig * xc
        dlog_a = a * (da - ds * a * inv_s)
        dr = (-RG_C) * sp * dlog_a
        dsp = _rowsum((-RG_C) * r * dlog_a)
        dpr = dr * r * (1.0 - r)
        dpi = di * ig * (1.0 - ig)
        vec_ref[0:1, :] += _rowsum(dpr)
        vec_ref[1:2, :] += _rowsum(dpi)
        vec_ref[2:3, :] += dsp * (-_sigmoid(-lam))
        parts = []
        for hd in range(RNN_HEADS):
            sl = slice(hd * RNN_HD, (hd + 1) * RNN_HD)
            xh = xc[:, sl].astype(BF16)
            dprh = dpr[:, sl].astype(BF16)
            dpih = dpi[:, sl].astype(BF16)
            parts.append(_dot_nt(dprh, wa_ref[hd]) + _dot_nt(dpih, wx_ref[hd]))
            dwa_ref[hd] += _dot_tn(xh, dprh)
            dwx_ref[hd] += _dot_tn(xh, dpih)
        dxc_ref[...] = dxc + jnp.concatenate(parts, axis=-1)

    wspec = _full((RNN_HEADS, RNN_HD, RNN_HD))
    cur = _rev_tile(TS, D, n, adj_rev)
    return _fused_call(
        body, comm, (xr, xr, xr, dh, h, h, cw, cb, wa, wx, ba, bx, lam), name=name, grid=(n,),
        in_specs=[xprev_spec, cur, xnext_spec, cur, cur, h_halo_spec, _full((4, D)), _full((1, D)),
                  wspec, wspec, _full((1, D)), _full((1, D)), _full((1, D))],
        out_specs=[cur, wspec, wspec, _full((SUBLANE, D))],
        out_shape=[jax.ShapeDtypeStruct((T, D), F32),
                   jax.ShapeDtypeStruct((RNN_HEADS, RNN_HD, RNN_HD), F32),
                   jax.ShapeDtypeStruct((RNN_HEADS, RNN_HD, RNN_HD), F32),
                   jax.ShapeDtypeStruct((SUBLANE, D), F32)],
        scratch_shapes=[pltpu.VMEM((TS, D), F32)] * 3 + [pltpu.VMEM((1, D), F32)] * 2)


def _od_in_bwd(dxcf, dxcb, xr, dg1, x1, dx1p, mod, w_in, cw, comm=None):
    T = x1.shape[0]
    n = T // TM
    slab = OD_IN // N_DEV
    prev_spec, next_spec = _halo_specs(TM, D, n, T, False)

    def body(fp_ref, fc_ref, fn_ref, bp_ref, bc_ref, bn_ref, xp_ref, xc_ref, xn_ref, dg_ref, x1_ref, dxp_ref,
             mod_ref, w_ref, cw_ref, dx_ref, dwb_ref, vec_ref, dw_ref):
        i = pl.program_id(0)

        @pl.when(i == 0)
        def _():
            dw_ref[...] = jnp.zeros_like(dw_ref)
            vec_ref[...] = jnp.zeros_like(vec_ref)

        dcur = fc_ref[...] + bc_ref[...]
        dprev = jnp.where(i > 0, fp_ref[...] + bp_ref[...], 0.0)
        dnext = jnp.where(i < n - 1, fn_ref[...] + bn_ref[...], 0.0)
        dext = jnp.concatenate([dprev, dcur, dnext], axis=0)
        xext = _ext_rows(xp_ref, xc_ref[...], xn_ref, i, n)
        cw_v = cw_ref[...]
        dxr = None
        for k in range(4):
            term = cw_v[k:k + 1, :] * _shift_rows(dext, 2 - k, TM)
            dxr = term if dxr is None else dxr + term
            vec_ref[k:k + 1, :] += _rowsum(dcur * _shift_rows(xext, k - 2, TM))
        vec_ref[4:5, :] += _rowsum(dcur)
        dp = jnp.concatenate([dxr.astype(BF16), dg_ref[...]], axis=-1)
        x1v = x1_ref[...]
        scale1 = 1.0 + mod_ref[1:2, :]
        h1 = (x1v * scale1 + mod_ref[0:1, :]).astype(BF16)
        dh1 = _dot_nt(dp, w_ref[...])
        dw_ref[...] += _dot_tn(h1, dp)
        dx_ref[...] = dxp_ref[...] + dh1 * scale1
        vec_ref[5:6, :] += _rowsum(dh1)
        vec_ref[6:7, :] += _rowsum(dh1 * x1v)

        @pl.when(i == n - 1)
        def _():
            for j in range(N_DEV):
                dwb_ref[j] = dw_ref[:, j * slab:(j + 1) * slab].astype(BF16)

    t = _tile(TM, D)
    return _fused_call(
        body, comm, (dxcf, dxcf, dxcf, dxcb, dxcb, dxcb, xr, xr, xr, dg1, x1, dx1p, mod, w_in, cw),
        name="od_in_bwd", grid=(n,),
        in_specs=[prev_spec, t, next_spec, prev_spec, t, next_spec, prev_spec, t, next_spec, t, t, t,
                  _full((3, D)), _full((D, OD_IN)), _full((4, D))],
        out_specs=[t, _full((N_DEV, D, slab)), _full((SUBLANE, D))],
        out_shape=[jax.ShapeDtypeStruct((T, D), F32), jax.ShapeDtypeStruct((N_DEV, D, slab), BF16),
                   jax.ShapeDtypeStruct((SUBLANE, D), F32)],
        scratch_shapes=[pltpu.VMEM((D, OD_IN), F32)])


def _ev_out_bwd(dx1, z0, out0, y0, ycat, g0, w_out, mod, lnp):
    T = dx1.shape[0]

    def body(dx_ref, z_ref, out_ref, y0_ref, yc_ref, g_ref, w_ref, mod_ref, ln_ref,
             dxp_ref, dyc_ref, dg_ref, dwb_ref, vec_ref, dw_ref):
        i = pl.program_id(0)

        @pl.when(i == 0)
        def _():
            dw_ref[...] = jnp.zeros_like(dw_ref)
            vec_ref[...] = jnp.zeros_like(vec_ref)

        lng = ln_ref[0:1, :]
        _, xhat, rstd = _ln_fwd(z_ref[...], lng, ln_ref[1:2, :])
        dy = dx_ref[...]
        dz = _ln_bwd(dy, xhat, rstd, lng)
        vec_ref[0:1, :] += _rowsum(dy * xhat)
        vec_ref[1:2, :] += _rowsum(dy)
        vec_ref[2:3, :] += _rowsum(dz * out_ref[...].astype(F32))
        dout = (dz * mod_ref[2:3, :]).astype(BF16)
        dy0 = _dot_nt(dout, w_ref[...])
        dw_ref[...] += _dot_tn(y0_ref[...], dout)
        sg, dsg = _silu_and_grad(g_ref[...].astype(F32))
        dyc_ref[...] = (dy0 * sg).astype(BF16)
        dg_ref[...] = (dy0 * yc_ref[...].astype(F32) * dsg).astype(BF16)
        dxp_ref[...] = ALPHA * dz

        @pl.when(i == T // TM - 1)
        def _():
            dwb_ref[...] = dw_ref[...].astype(BF16)

    t = _tile(TM, D)
    return _pallas(
        body, name="ev_out_bwd", grid=(T // TM,),
        in_specs=[t, t, t, t, t, t, _full((D, D)), _full((3, D)), _full((2, D))],
        out_specs=[t, t, t, _full((D, D)), _full((SUBLANE, D))],
        out_shape=[jax.ShapeDtypeStruct((T, D), F32), jax.ShapeDtypeStruct((T, D), BF16),
                   jax.ShapeDtypeStruct((T, D), BF16), jax.ShapeDtypeStruct((D, D), BF16),
                   jax.ShapeDtypeStruct((SUBLANE, D), F32)],
        scratch_shapes=[pltpu.VMEM((D, D), F32)],
        compiler_params=_params(("arbitrary",)),
    )(dx1, z0, out0, y0, ycat, g0, w_out, mod, lnp)


def _mix0_bwd(q, kvx, lse, dyc, ycat, su, sv, sink_l, bias, a128, gsum, sel, sg_lng, sg_lnb, sg_w, sg_bfull,
              rc, rs1, rs2, comm=None):
    T = q.shape[0]
    nb = T // BLK

    def body(q_ref, kp_ref, kc_ref, kn_ref, lse_ref, dyc_ref, yc_ref, su_ref, sv_ref, sink_ref, bias_ref, a_ref,
             gsum_ref, sel_ref, lng_ref, lnb_ref, w_ref, bfull_ref, c_ref, s1_ref, s2_ref,
             dq_ref, dkv_ref, dsu_ref, dsv_ref, dw_ref, dbt_ref, vec_ref, dsink_ref):
        n = pl.program_id(0)

        @pl.when(n == 0)
        def _():
            dkv_ref[...] = jnp.zeros_like(dkv_ref)
            dw_ref[...] = jnp.zeros_like(dw_ref)
            dbt_ref[...] = jnp.zeros_like(dbt_ref)
            vec_ref[...] = jnp.zeros_like(vec_ref)
            dsink_ref[...] = jnp.zeros_like(dsink_ref)

        band = pl.ds(pl.multiple_of(n * BLK + (TM - BLK), BLK), 3 * BLK)
        bias = _band_bias(bias_ref, n, nb)
        kvx = jnp.concatenate([kp_ref[...], kc_ref[...], kn_ref[...]], axis=0)
        low = lax.broadcasted_iota(jnp.int32, (BLK, LANE), 1) < HEAD_DIM
        sel = sel_ref[...]
        c, s1, s2 = c_ref[...], s1_ref[...], s2_ref[...]
        for kvh in range(2):
            dkx = jnp.zeros((3 * BLK, LANE), F32)
            dvx = jnp.zeros((3 * BLK, LANE), F32)
            for t in range(2 * kvh, 2 * kvh + 2):
                tl = slice(t * LANE, (t + 1) * LANE)
                qt = q_ref[:, tl]
                do = dyc_ref[:, tl]
                p_hi, p_lo = _split_bf16(do.astype(F32) * yc_ref[:, tl].astype(F32))
                deltas = _dot_nt(sel, p_hi) + _dot_nt(sel, p_lo)
                dq_acc = None
                for par in range(2):
                    h = 2 * t + par
                    hl = slice(h * LANE, (h + 1) * LANE)
                    kt = 2 * kvh + par
                    ke = kvx[:, kt * LANE:(kt + 1) * LANE]
                    ve = kvx[:, (4 + kt) * LANE:(5 + kt) * LANE]
                    lse = lse_ref[0, :, hl]
                    delta = deltas[par:par + 1, :]
                    pt = jnp.exp(_dot_nt(ke, qt) + bias - lse)
                    dst = (pt * (_dot_nt(ve, do) - delta)).astype(BF16)
                    dsink_ref[:, hl] += jnp.exp(sink_ref[:, hl] - lse) * delta
                    part = _dot_tn(dst, ke)
                    dq_acc = part if dq_acc is None else dq_acc + part
                    mine = low if par == 0 else jnp.logical_not(low)
                    dkx = dkx + jnp.dot(dst, jnp.where(mine, qt, jnp.zeros_like(qt)), preferred_element_type=F32)
                    dvx = dvx + jnp.dot(pt.astype(BF16), jnp.where(mine, do, jnp.zeros_like(do)),
                                        preferred_element_type=F32)
                dq_ref[:, tl] = _rope_bwd(dq_acc * (HEAD_DIM ** -0.5), c, s1, s2).astype(BF16)
            dkv_ref[band, kvh * LANE:(kvh + 1) * LANE] += dkx
            dkv_ref[band, (2 + kvh) * LANE:(3 + kvh) * LANE] += dvx

        lng = lng_ref[...]
        xhat, rstd, vb, svm = _sg_core(sv_ref, lng, lnb_ref[...], a_ref, w_ref, bfull_ref)
        dy = dyc_ref[:, ATTN_W:].astype(F32)
        dsu_ref[...] = (dy * svm).astype(BF16)
        dsvm = dy * su_ref[...].astype(F32)
        d_hi, d_lo = _split_bf16(dsvm)
        gsum = gsum_ref[...]
        dbt_ref[...] += jnp.dot(d_hi, gsum, preferred_element_type=F32) + jnp.dot(d_lo, gsum,
                                                                                 preferred_element_type=F32)
        tiles = []
        for t in range(SG_W // LANE):
            tl = slice(t * LANE, (t + 1) * LANE)
            dt, v2 = d_hi[:, tl], vb[:, tl]
            dw_ref[2 * t] += _dot_nt(jnp.where(low, dt, jnp.zeros_like(dt)), v2)
            dw_ref[2 * t + 1] += _dot_nt(jnp.where(low, jnp.zeros_like(dt), dt), v2)
            tiles.append(jnp.where(low, _dot_tn(w_ref[2 * t], dt), _dot_tn(w_ref[2 * t + 1], dt)))
        dvgn = jnp.concatenate(tiles, axis=-1)
        vec_ref[0:1, :] += _rowsum(dvgn * xhat)
        vec_ref[1:2, :] += _rowsum(dvgn)
        dxh = dvgn * lng
        m1 = _group_mean(dxh, a_ref)
        m2 = _group_mean(dxh * xhat, a_ref)
        dsv_ref[...] = (rstd * (dxh - m1 - xhat * m2)).astype(BF16)

    return _fused_call(
        body, comm, (q, kvx, kvx, kvx, lse, dyc, ycat, su, sv, sink_l, bias, a128, gsum, sel, sg_lng, sg_lnb, sg_w,
                     sg_bfull, rc, rs1, rs2),
        name="mix0_bwd", grid=(nb,),
        in_specs=[_tile(BLK, ATTN_W)] + _band_specs(KVX_W, nb) + [
            pl.BlockSpec((1, 1, N_HEADS * LANE), lambda n: (n, 0, 0)), _tile(BLK, D), _tile(BLK, D),
            _tile(BLK, SG_W), _tile(BLK, SG_W), _full((1, N_HEADS * LANE)), _full((3 * BLK, LANE)),
            _full((LANE, LANE)), _full((SG_W, LANE)), _full((SUBLANE, LANE)), _full((1, SG_W)), _full((1, SG_W)),
            _full((SG_GROUPS, BLK, BLK)), _full((BLK, SG_W)), _tile(BLK, LANE), _tile(BLK, LANE), _tile(BLK, LANE)],
        out_specs=[_tile(BLK, ATTN_W), _full((T + 2 * TM, 4 * LANE)), _tile(BLK, SG_W), _tile(BLK, SG_W),
                   _full((SG_GROUPS, BLK, BLK)), _full((BLK, LANE)), _full((SUBLANE, SG_W)),
                   _full((1, N_HEADS * LANE))],
        out_shape=[jax.ShapeDtypeStruct((T, ATTN_W), BF16), jax.ShapeDtypeStruct((T + 2 * TM, 4 * LANE), F32),
                   jax.ShapeDtypeStruct((T, SG_W), BF16), jax.ShapeDtypeStruct((T, SG_W), BF16),
                   jax.ShapeDtypeStruct((SG_GROUPS, BLK, BLK), F32), jax.ShapeDtypeStruct((BLK, LANE), F32),
                   jax.ShapeDtypeStruct((SUBLANE, SG_W), F32), jax.ShapeDtypeStruct((1, N_HEADS * LANE), F32)])


def _ev_in_bwd(dq, dkv, dsu, dsv, dg0, x, dxp, mod, w_in, rc, rs1, rs2, comm=None):
    T = x.shape[0]

    def body(dq_ref, dkv_ref, dsu_ref, dsv_ref, dg_ref, x_ref, dxp_ref, mod_ref, w_ref, c_ref, s1_ref, s2_ref,
             dx_ref, dwb_ref, vec_ref, dw_ref):
        i = pl.program_id(0)

        @pl.when(i == 0)
        def _():
            dw_ref[...] = jnp.zeros_like(dw_ref)
            vec_ref[...] = jnp.zeros_like(vec_ref)

        low = lax.broadcasted_iota(jnp.int32, (TM, LANE), 1) < HEAD_DIM

        def fold(j):
            t0 = dkv_ref[:, (2 * j) * LANE:(2 * j + 1) * LANE]
            t1 = dkv_ref[:, (2 * j + 1) * LANE:(2 * j + 2) * LANE]
            return jnp.where(low, t0 + pltpu.roll(t0, HEAD_DIM, 1), t1 + pltpu.roll(t1, HEAD_DIM, 1))

        dk = _rope_bwd(fold(0), c_ref[...], s1_ref[...], s2_ref[...]).astype(BF16)
        dp = jnp.concatenate([dq_ref[...], dk, fold(1).astype(BF16), dsu_ref[...], dsv_ref[...],
                              dg_ref[...]], axis=-1)
        xv = x_ref[...]
        scale0 = 1.0 + mod_ref[1:2, :]
        h0 = (xv * scale0 + mod_ref[0:1, :]).astype(BF16)
        dh0 = _dot_nt(dp, w_ref[...])
        dw_ref[...] += _dot_tn(h0, dp)
        dx_ref[...] = dxp_ref[...] + dh0 * scale0
        vec_ref[0:1, :] += _rowsum(dh0)
        vec_ref[1:2, :] += _rowsum(dh0 * xv)

        @pl.when(i == T // TM - 1)
        def _():
            dwb_ref[...] = dw_ref[...].astype(BF16)

    t = _tile(TM, D)
    return _fused_call(
        body, comm, (dq, dkv, dsu, dsv, dg0, x, dxp, mod, w_in, rc, rs1, rs2), name="ev_in_bwd", grid=(T // TM,),
        in_specs=[_tile(TM, ATTN_W), pl.BlockSpec((TM, 4 * LANE), lambda i: (i + 1, 0)), _tile(TM, SG_W),
                  _tile(TM, SG_W), t, t, t,
                  _full((3, D)), _full((D, EV_IN)), _tile(TM, LANE), _tile(TM, LANE), _tile(TM, LANE)],
        out_specs=[t, _full((D, EV_IN)), _full((SUBLANE, D))],
        out_shape=[jax.ShapeDtypeStruct((T, D), F32), jax.ShapeDtypeStruct((D, EV_IN), BF16),
                   jax.ShapeDtypeStruct((SUBLANE, D), F32)],
        scratch_shapes=[pltpu.VMEM((D, EV_IN), F32)])


def _sum_slots(land_ref):
    g = land_ref[0].astype(F32)
    for i in range(1, N_DEV):
        g = g + land_ref[i].astype(F32)
    return g


def _reduce_adam(land, w, m, v, name):
    R, C = w.shape
    rb = R
    for cand in (128, 64, 32, 16, 8):
        if R % cand == 0:
            rb = cand
            break

    def body(l_ref, w_ref, m_ref, v_ref, g_ref, d_ref, nm_ref, nv_ref):
        g = _sum_slots(l_ref)
        g_ref[...] = g
        dlt, m2, v2 = _adam(w_ref[...], g, m_ref[...], v_ref[...])
        d_ref[...] = dlt
        nm_ref[...] = m2
        nv_ref[...] = v2

    t = pl.BlockSpec((rb, C), lambda i: (i, 0))
    shp = jax.ShapeDtypeStruct((R, C), F32)
    return _pallas(
        body, name=name, grid=(R // rb,),
        in_specs=[pl.BlockSpec((N_DEV, rb, C), lambda i: (0, i, 0)), t, t, t],
        out_specs=[t] * 4, out_shape=[shp] * 4,
        compiler_params=_params(("parallel",)),
    )(land, w, m, v)


def _comm_only(comms, name):
    total = sum(cm.n for cm in comms)

    def body(*refs):
        ins, outs, sems = refs[:total], refs[total:2 * total], refs[2 * total:]
        groups, pos = [], 0
        for idx, cm in enumerate(comms):
            groups.append((cm, ins[pos:pos + cm.n], outs[pos:pos + cm.n], sems[3 * idx:3 * idx + 3]))
            pos += cm.n
        for cm, i_, o_, s_ in groups:
            cm.start(i_, o_, s_)
        for cm, i_, o_, s_ in groups:
            if cm.has_mid:
                cm.mid(i_, o_, s_)
        for cm, i_, o_, s_ in groups:
            cm.finish(i_, o_, s_)

    any_spec = pl.BlockSpec(memory_space=pl.ANY)
    res = _pallas(
        body, name=name,
        out_shape=[s for cm in comms for s in cm.out_shapes()],
        in_specs=[any_spec] * total, out_specs=[any_spec] * total,
        scratch_shapes=[s for cm in comms for s in cm.sems()],
    )(*[a for cm in comms for a in cm.arrs])
    out, pos = [], 0
    for cm in comms:
        out.append(list(res[pos:pos + cm.n]))
        pos += cm.n
    return out


def _slots_adam(land, w, m, v, name):
    lead = w.shape[1] if w.ndim == 5 else 1
    inner = w.shape[-3:]
    zeros3 = (0, 0, 0)
    if w.ndim == 5:
        lspec = pl.BlockSpec((N_DEV, 1) + inner, lambda i: (0, i) + zeros3)
        wspec = pl.BlockSpec((1, 1) + inner, lambda i: (0, i) + zeros3)
    else:
        lspec = pl.BlockSpec((N_DEV,) + inner, lambda i: (0,) + zeros3)
        wspec = pl.BlockSpec((1,) + inner, lambda i: (0,) + zeros3)

    def body(l_ref, w_ref, m_ref, v_ref, g_ref, d_ref, nm_ref, nv_ref):
        at = (0, 0) if w.ndim == 5 else (0,)
        g = l_ref[(0,) + at[1:]].astype(F32)
        for i in range(1, N_DEV):
            g = g + l_ref[(i,) + at[1:]].astype(F32)
        dlt, m2, v2 = _adam(w_ref[at], g, m_ref[at], v_ref[at])
        g_ref[at] = g
        d_ref[at] = dlt
        nm_ref[at] = m2
        nv_ref[at] = v2

    shp = jax.ShapeDtypeStruct(w.shape, F32)
    return _pallas(
        body, name=name, grid=(lead,),
        in_specs=[lspec, wspec, wspec, wspec], out_specs=[wspec] * 4, out_shape=[shp] * 4,
        compiler_params=_params(("parallel",)),
    )(land, w, m, v)


SMALL_PARAMS = ("ln_g", "ln_b", "ev_sg_ln_g", "ev_sg_ln_b", "ev_sink", "ev_sg_b",
                "od_conv_w", "od_conv_b", "od_b_a", "od_b_x", "od_lam")


def _small_update(ga, gc, gd, gf, gb, ge, gsink, gbt, params):
    names = list(SMALL_PARAMS)
    flat = [a for nm in names for a in params[nm]]
    n_g = 8

    def body(*refs):
        ga_ref, gc_ref, gd_ref, gf_ref, gb_ref, ge_ref, gs_ref, gbt_ref = refs[:n_g]
        prm = refs[n_g:n_g + 3 * len(names)]
        loss_ref = refs[n_g + 3 * len(names)]
        outs = refs[n_g + 3 * len(names) + 1:]

        def ssum(ref):
            acc = ref[0]
            for i in range(1, N_DEV):
                acc = acc + ref[i]
            return acc

        a, cc, dd, ff, bb, ee = ssum(ga_ref), ssum(gc_ref), ssum(gd_ref), ssum(gf_ref), ssum(gb_ref), ssum(ge_ref)
        loss_ref[...] = a[3:4, 0:LANE]
        me = _slot(*_my_pos())

        def mine(rows):
            acc = jnp.zeros((rows.shape[0], LANE), F32)
            for j in range(N_DEV):
                acc = acc + jnp.where(me == j, rows[:, j * LANE:(j + 1) * LANE], 0.0)
            return acc

        sink_terms = ssum(gs_ref)
        lane8 = lax.broadcasted_iota(jnp.int32, (1, N_HEADS), 1)
        g_sink = jnp.zeros((1, N_HEADS), F32)
        for h in range(N_HEADS):
            tot = -jnp.sum(sink_terms[:, h * LANE:(h + 1) * LANE], axis=1, keepdims=True)
            g_sink = jnp.where(lane8 == h, tot, g_sink)
        grads = dict(
            ln_g=jnp.concatenate([dd[0:1], a[0:1]], axis=0), ln_b=jnp.concatenate([dd[1:2], a[1:2]], axis=0),
            ev_sg_ln_g=ee[0:1], ev_sg_ln_b=ee[1:2], ev_sink=g_sink,
            ev_sg_b=jnp.transpose(ssum(gbt_ref))[0:SG_GROUPS, :],
            od_conv_w=mine(cc[0:4]), od_conv_b=mine(cc[4:5]),
            od_b_a=mine(jnp.concatenate([ff[0:1], bb[0:1]], axis=0)),
            od_b_x=mine(jnp.concatenate([ff[1:2], bb[1:2]], axis=0)),
            od_lam=mine(jnp.concatenate([ff[2:3], bb[2:3]], axis=0)))
        for k, nm in enumerate(names):
            w_ref, m_ref, v_ref = prm[3 * k:3 * k + 3]
            at = (0,) if len(w_ref.shape) == 3 else ()
            g = grads[nm]
            dlt, m2, v2 = _adam(w_ref[at] if at else w_ref[...], g, m_ref[at] if at else m_ref[...],
                                v_ref[at] if at else v_ref[...])
            for o_ref, val in zip(outs[4 * k:4 * k + 4], (g, dlt, m2, v2)):
                if at:
                    o_ref[at] = val
                else:
                    o_ref[...] = val

    gathered = [ga, gc, gd, gf, gb, ge, gsink, gbt]
    out_shape = [jax.ShapeDtypeStruct((1, LANE), F32)]
    for nm in names:
        out_shape += [jax.ShapeDtypeStruct(params[nm][0].shape, F32)] * 4
    return _pallas(
        body, name="small_update", grid=(1,),
        in_specs=[_full(a.shape) for a in gathered + flat],
        out_specs=[_full(s.shape) for s in out_shape], out_shape=out_shape,
        compiler_params=_params(("arbitrary",)),
    )(*gathered, *flat)


VEC_ROWS = 16
VEC_LAYOUT = (("od_conv_w", 4), ("od_conv_b", 1), ("od_b_a", 2), ("od_b_x", 2), ("od_lam", 2))


def _pack_vec(parts):
    rows = [parts[name].reshape(nrows, -1) for name, nrows in VEC_LAYOUT]
    used = sum(r for _, r in VEC_LAYOUT)
    rows.append(jnp.zeros((VEC_ROWS - used, rows[0].shape[1]), F32))
    return jnp.concatenate(rows, axis=0)


def _to_slabs(full, cols_per):
    R = full.shape[0]
    return full.reshape(R, N_DEV, cols_per).transpose(1, 0, 2)


def _from_slabs(slabs):
    n, R, cp = slabs.shape
    return slabs.transpose(1, 0, 2).reshape(R, n * cp)


def kernel(x, c, positions, ada_w, ada_b, ln_g, ln_b, ev_w_in, ev_w_out, ev_sink, ev_sg_ln_g, ev_sg_ln_b, ev_sg_w, ev_sg_b, od_w_in, od_conv_w, od_conv_b, od_w_a, od_b_a, od_w_x, od_b_x, od_lam, od_w_out, loss_target, m_ada_w, m_ada_b, m_ln_g, m_ln_b, m_ev_w_in, m_ev_w_out, m_ev_sink, m_ev_sg_ln_g, m_ev_sg_ln_b, m_ev_sg_w, m_ev_sg_b, m_od_w_in, m_od_conv_w, m_od_conv_b, m_od_w_a, m_od_b_a, m_od_w_x, m_od_b_x, m_od_lam, m_od_w_out, v_ada_w, v_ada_b, v_ln_g, v_ln_b, v_ev_w_in, v_ev_w_out, v_ev_sink, v_ev_sg_ln_g, v_ev_sg_ln_b, v_ev_sg_w, v_ev_sg_b, v_od_w_in, v_od_conv_w, v_od_conv_b, v_od_w_a, v_od_b_a, v_od_w_x, v_od_b_x, v_od_lam, v_od_w_out):
    T = x.shape[1]
    me = _slot(*_my_pos())
    xs = x.reshape(T, D)
    tgt = loss_target.reshape(T, D)

    vec_w = _pack_vec(dict(od_conv_w=od_conv_w[0], od_conv_b=od_conv_b, od_b_a=od_b_a[0], od_b_x=od_b_x[0],
                           od_lam=od_lam[0]))
    c_all, g_ev_in, g_vec = _all_gather([c, ev_w_in[0].astype(BF16), vec_w], "ag_params")
    c_all = c_all.reshape(N_DEV, D)
    w_ev_in = _from_slabs(g_ev_in)
    vec_full = _from_slabs(g_vec)
    cw, cb = vec_full[0:4], vec_full[4:5]
    ba, bx, lam = vec_full[5:7], vec_full[7:9], vec_full[9:11]

    mod_part = _mod_part(c_all, ada_w)
    (mod_all,) = _all_gather([mod_part], "ag_mod")
    mod_mine = lax.dynamic_index_in_dim(mod_all, me, axis=2, keepdims=False)
    mod = mod_mine.transpose(1, 0, 2).reshape(2, 3 * D) + ada_b
    mod0 = mod[0].reshape(3, D)
    mod1 = mod[1].reshape(3, D)

    half = 8
    inv_freq = jnp.power(jnp.float32(ROPE_THETA), -jnp.arange(half, dtype=F32) / half)
    ang = positions.reshape(T).astype(F32)[:, None] * inv_freq
    cos_t = jnp.tile(jnp.cos(ang), (1, LANE // half))
    sin_t = jnp.tile(jnp.sin(ang), (1, LANE // half))
    l64 = jnp.arange(LANE) % HEAD_DIM
    rc = jnp.where(l64 < 2 * half, cos_t, 1.0)
    rs1 = jnp.where(l64 < half, -sin_t, 0.0)
    rs2 = jnp.where((l64 >= half) & (l64 < 2 * half), sin_t, 0.0)

    ln0 = jnp.stack([ln_g[0], ln_b[0]])
    ln1 = jnp.stack([ln_g[1], ln_b[1]])
    sg_lng = ev_sg_ln_g
    sg_lnb = ev_sg_ln_b
    sg_w = ev_sg_w[0].astype(BF16)
    sg_bfull = jnp.repeat(ev_sg_b[0].T, SG_DIM, axis=1)
    sink_l = jnp.repeat(ev_sink, LANE, axis=1)
    kj = jnp.arange(3 * BLK)[:, None]
    qi = jnp.arange(BLK)[None, :]
    band_bias = jnp.where(jnp.abs(kj - BLK - qi) <= BLK, 0.0, NEG_INF).astype(F32)
    lanes = jnp.arange(LANE)
    a128 = jnp.where(lanes[:, None] // SG_DIM == lanes[None, :] // SG_DIM, 1.0 / SG_DIM, 0.0).astype(BF16)
    gsum = (jnp.arange(SG_W)[:, None] // SG_DIM == lanes[None, :]).astype(BF16)
    sel = (jnp.arange(SUBLANE)[:, None] == lanes[None, :] // HEAD_DIM).astype(BF16)
    wa = od_w_a[0].astype(BF16)
    wx = od_w_x[0].astype(BF16)

    (q, kvx, su, sv, g0), (g_ev_out,) = _ev_in(xs, mod0, w_ev_in, rc, rs1, rs2,
                                               _GatherComm([ev_w_out[0].astype(BF16)]))
    w_ev_out = g_ev_out.reshape(D, D)
    (ycat, y0, lse), (g_od_in, g_od_out) = _mix0_fwd(
        q, kvx, su, sv, g0, sink_l, band_bias, a128, sg_lng, sg_lnb, sg_w, sg_bfull,
        _GatherComm([od_w_in[0].astype(BF16), od_w_out[0].astype(BF16)]))
    w_od_in = _from_slabs(g_od_in)
    w_od_out = g_od_out.reshape(D, D)
    out0, z0, x1 = _ev_out(y0, w_ev_out, xs, mod0, ln0)
    xr, g1 = _od_in(x1, mod1, w_od_in)
    hf = _rglru_fwd(xr, cw, cb, wa[0], wx[0], ba[0:1], bx[0:1], lam[0:1], False, "rglru_fwd_f")
    hb = _rglru_fwd(xr, cw, cb, wa[1], wx[1], ba[1:2], bx[1:2], lam[1:2], True, "rglru_fwd_b")
    dh, dg1, dx1p, d_od_out, vec_a = _od_out(hf, hb, g1, w_od_out, x1, tgt, mod1, ln1)

    (dxcf, dwa_f, dwx_f, vec_f), (l_od_out,) = _rglru_bwd(
        xr, dh, hf, cw, cb, wa[0], wx[0], ba[0:1], bx[0:1], lam[0:1], False, "rglru_bwd_f",
        _ExchangeComm([d_od_out.reshape(N_DEV, D // N_DEV, D)]))
    (dxcb, dwa_b, dwx_b, vec_b), _ = _rglru_bwd(xr, dh, hb, cw, cb, wa[1], wx[1], ba[1:2], bx[1:2], lam[1:2],
                                                True, "rglru_bwd_b")
    (dx1, d_od_in, vec_c), (a_wa, a_wx) = _od_in_bwd(
        dxcf, dxcb, xr, dg1, x1, dx1p, mod1, w_od_in, cw,
        _GatherComm([jnp.stack([dwa_f, dwa_b]).astype(BF16), jnp.stack([dwx_f, dwx_b]).astype(BF16)],
                    mid_frac=0.75))
    dxp, dyc, dg0, d_ev_out, vec_d = _ev_out_bwd(dx1, z0, out0, y0, ycat, g0, w_ev_out, mod0, ln0)
    (dq, dkv, dsu, dsv, d_sg_w, d_sg_bt, vec_e, d_sink_l), (l_od_in, l_ev_out) = _mix0_bwd(
        q, kvx, lse, dyc, ycat, su, sv, sink_l, band_bias, a128, gsum, sel, sg_lng, sg_lnb, sg_w, sg_bfull,
        rc, rs1, rs2, _ExchangeComm([d_od_in, d_ev_out.reshape(N_DEV, D // N_DEV, D)]))
    (grad_x, d_ev_in, vec_g), _ = _ev_in_bwd(dq, dkv, dsu, dsv, dg0, xs, dxp, mod0, w_ev_in, rc, rs1, rs2)

    (l_ev_in,), (ga, gc, gd, gf, gb, gg, ge, gsink, gbt, a_sgw) = _comm_only(
        [_ExchangeComm([_to_slabs(d_ev_in, EV_IN // N_DEV)]),
         _GatherComm([vec_a, vec_c, vec_d, vec_f, vec_b, vec_g, vec_e, d_sink_l, d_sg_bt, d_sg_w.astype(BF16)])],
        "tail_exchange")

    dmod_all = jnp.stack([jnp.concatenate([gg[:, 0], gg[:, 1], gd[:, 2]], axis=-1),
                          jnp.concatenate([gc[:, 5], gc[:, 6], ga[:, 2]], axis=-1)], axis=1)
    cols = ada_w.shape[2]
    dmod_cols = lax.dynamic_slice_in_dim(dmod_all, me * cols, cols, axis=2).transpose(1, 0, 2)
    (g_ada_w, d_ada_w, nm_ada_w, nv_ada_w, g_ada_b, d_ada_b, nm_ada_b, nv_ada_b) = _ada_update(
        c_all, dmod_cols, dmod_all, ada_w, m_ada_w, v_ada_w, ada_b, m_ada_b, v_ada_b)

    res = dict(ada_w=[g_ada_w, d_ada_w, nm_ada_w, nv_ada_w], ada_b=[g_ada_b, d_ada_b, nm_ada_b, nv_ada_b])
    for name, land, w, m, v in (("ev_w_in", l_ev_in, ev_w_in, m_ev_w_in, v_ev_w_in),
                                ("ev_w_out", l_ev_out, ev_w_out, m_ev_w_out, v_ev_w_out),
                                ("od_w_in", l_od_in, od_w_in, m_od_w_in, v_od_w_in),
                                ("od_w_out", l_od_out, od_w_out, m_od_w_out, v_od_w_out)):
        res[name] = [a[None] for a in _reduce_adam(land, w[0], m[0], v[0], "adam_" + name)]
    res["od_w_a"] = _slots_adam(a_wa, od_w_a, m_od_w_a, v_od_w_a, "adam_od_w_a")
    res["od_w_x"] = _slots_adam(a_wx, od_w_x, m_od_w_x, v_od_w_x, "adam_od_w_x")
    res["ev_sg_w"] = _slots_adam(a_sgw, ev_sg_w, m_ev_sg_w, v_ev_sg_w, "adam_ev_sg_w")
    small = dict(ln_g=(ln_g, m_ln_g, v_ln_g), ln_b=(ln_b, m_ln_b, v_ln_b),
                 ev_sg_ln_g=(ev_sg_ln_g, m_ev_sg_ln_g, v_ev_sg_ln_g),
                 ev_sg_ln_b=(ev_sg_ln_b, m_ev_sg_ln_b, v_ev_sg_ln_b),
                 ev_sink=(ev_sink, m_ev_sink, v_ev_sink), ev_sg_b=(ev_sg_b, m_ev_sg_b, v_ev_sg_b),
                 od_conv_w=(od_conv_w, m_od_conv_w, v_od_conv_w), od_conv_b=(od_conv_b, m_od_conv_b, v_od_conv_b),
                 od_b_a=(od_b_a, m_od_b_a, v_od_b_a), od_b_x=(od_b_x, m_od_b_x, v_od_b_x),
                 od_lam=(od_lam, m_od_lam, v_od_lam))
    small_out = _small_update(ga, gc, gd, gf, gb, ge, gsink, gbt, small)
    loss = small_out[0][0, 0]
    for k, name in enumerate(SMALL_PARAMS):
        res[name] = small_out[1 + 4 * k:5 + 4 * k]

    order = ["ada_w", "ada_b", "ln_g", "ln_b", "ev_w_in", "ev_w_out", "ev_sink", "ev_sg_ln_g", "ev_sg_ln_b",
             "ev_sg_w", "ev_sg_b", "od_w_in", "od_conv_w", "od_conv_b", "od_w_a", "od_b_a", "od_w_x", "od_b_x",
             "od_lam", "od_w_out"]
    outs = [loss, grad_x.reshape(1, T, D)]
    for kind in range(4):
        outs += [res[name][kind] for name in order]
    return tuple(outs)
```

```python
import functools

import jax
import jax.numpy as jnp
from jax import lax
from jax.experimental import pallas as pl
from jax.experimental.pallas import tpu as pltpu

F32 = jnp.float32
BF16 = jnp.bfloat16

N_DEV = 8
D = 1024
N_HEADS = 8
HEAD_DIM = 64
KV_WIDTH = 128
ATTN_W = 512
SG_W = 512
SG_GROUPS = 8
SG_DIM = 64
BLK = 128
KVX_W = 1024
EV_IN = 2816
OD_IN = 2048
RNN_HEADS = 8
RNN_HD = 128
ALPHA = 4.0 ** 0.25
LN_EPS = 1e-5
NEG_INF = -1e30
RG_C = 8.0
ROPE_THETA = 500000.0
LR, B1, B2, EPS, WD, STEP = 0.001, 0.9, 0.999, 1e-08, 0.01, 10

LANE = 128
SUBLANE = 8
TM = 256
TS = 256
VMEM_LIMIT = 56 * 1024 * 1024

MESH = pl.DeviceIdType.MESH


def _pallas(body, **kw):
    return pl.pallas_call(body, **kw)


def _params(sem, vmem=VMEM_LIMIT):
    return pltpu.CompilerParams(dimension_semantics=sem, vmem_limit_bytes=vmem)


def _sigmoid(x):
    return 0.5 * jnp.tanh(0.5 * x) + 0.5


def _silu_and_grad(x):
    s = _sigmoid(x)
    return x * s, s * (1.0 + x * (1.0 - s))


def _dot(a, b):
    return jnp.dot(a.astype(BF16), b.astype(BF16), preferred_element_type=F32)


def _dot_nt(a, b):
    return lax.dot_general(a.astype(BF16), b.astype(BF16), (((1,), (1,)), ((), ())), preferred_element_type=F32)


def _dot_tn(a, b):
    return lax.dot_general(a.astype(BF16), b.astype(BF16), (((0,), (0,)), ((), ())), preferred_element_type=F32)


def _ln_fwd(z, g, b):
    mu = jnp.mean(z, axis=-1, keepdims=True)
    zc = z - mu
    var = jnp.mean(zc * zc, axis=-1, keepdims=True)
    rstd = lax.rsqrt(var + LN_EPS)
    xhat = zc * rstd
    return xhat * g + b, xhat, rstd


def _ln_bwd(dy, xhat, rstd, g):
    dxh = dy * g
    m1 = jnp.mean(dxh, axis=-1, keepdims=True)
    m2 = jnp.mean(dxh * xhat, axis=-1, keepdims=True)
    return rstd * (dxh - m1 - xhat * m2)


def _rowsum(v):
    return jnp.sum(v, axis=0, keepdims=True)


def _rope_fwd(t, c, s1, s2):
    return t * c + pltpu.roll(t, LANE - 8, 1) * s1 + pltpu.roll(t, 8, 1) * s2


def _rope_bwd(d, c, s1, s2):
    return d * c + pltpu.roll(d * s1, 8, 1) + pltpu.roll(d * s2, LANE - 8, 1)


def _adam(w, g, m, v):
    m2 = B1 * m + (1.0 - B1) * g
    v2 = B2 * v + (1.0 - B2) * (g * g)
    m_hat = m2 / (1.0 - B1 ** STEP)
    v_hat = v2 / (1.0 - B2 ** STEP)
    delta = -LR * (m_hat / (jnp.sqrt(v_hat) + EPS) + WD * w)
    return delta, m2, v2


def _tile(rows, width):
    return pl.BlockSpec((rows, width), lambda i: (i, 0))


def _full(shape):
    zeros = (0,) * len(shape)
    return pl.BlockSpec(shape, lambda i: zeros)


def _rev_tile(rows, width, n, reverse):
    if reverse:
        return pl.BlockSpec((rows, width), lambda i: (n - 1 - i, 0))
    return pl.BlockSpec((rows, width), lambda i: (i, 0))


def _halo_specs(rows, width, n, total_rows, reverse):
    per = rows // SUBLANE
    last = total_rows // SUBLANE - 1

    def tile_of(i):
        return (n - 1 - i) if reverse else i

    prev = pl.BlockSpec((SUBLANE, width), lambda i: (jnp.maximum(tile_of(i) * per - 1, 0), 0))
    nxt = pl.BlockSpec((SUBLANE, width), lambda i: (jnp.minimum((tile_of(i) + 1) * per, last), 0))
    return prev, nxt


def _my_pos():
    return lax.axis_index("x"), lax.axis_index("y"), lax.axis_index("c")


def _slot(px, py, pc):
    return 4 * px + 2 * py + pc


def _all_gather(arrs, name):
    n = len(arrs)

    def body(*refs):
        ins, outs = refs[:n], refs[n:2 * n]
        send_sems, recv_sems, local_sems = refs[2 * n:]
        x, y, c = _my_pos()
        me, sibling = (x, y, c), (x, y, 1 - c)
        chips = [(1 - x, y), (x, 1 - y), (1 - x, 1 - y)]

        def copy(a, k, block, to, src=None):
            dst = outs[a].at[_slot(*block)]
            return pltpu.make_async_remote_copy(
                src_ref=dst if src is None else src, dst_ref=dst,
                send_sem=send_sems.at[a * 7 + k], recv_sem=recv_sems.at[a * 7 + k],
                device_id=to, device_id_type=MESH)

        local, first = [], []
        for a in range(n):
            lc = pltpu.make_async_copy(ins[a], outs[a].at[_slot(*me)], local_sems.at[a])
            lc.start()
            local.append(lc)
            first.append(copy(a, 0, me, sibling, src=ins[a]))
            first += [copy(a, 1 + j, me, (*chip, c), src=ins[a]) for j, chip in enumerate(chips)]
        for cp in first:
            cp.start()
        passed = []
        for j, chip in enumerate(chips):
            for a in range(n):
                copy(a, 1 + j, (*chip, c), me).wait_recv()
                fw = copy(a, 4 + j, (*chip, c), sibling)
                fw.start()
                passed.append(fw)
        for a in range(n):
            copy(a, 0, sibling, me).wait_recv()
            for j, chip in enumerate(chips):
                copy(a, 4 + j, (*chip, 1 - c), me).wait_recv()
        for cp in first + passed:
            cp.wait_send()
        for lc in local:
            lc.wait()

    any_spec = pl.BlockSpec(memory_space=pl.ANY)
    return _pallas(
        body, name=name,
        out_shape=[jax.ShapeDtypeStruct((N_DEV,) + a.shape, a.dtype) for a in arrs],
        in_specs=[any_spec] * n, out_specs=[any_spec] * n,
        scratch_shapes=[pltpu.SemaphoreType.DMA((7 * n,)), pltpu.SemaphoreType.DMA((7 * n,)),
                        pltpu.SemaphoreType.DMA((n,))],
    )(*arrs)


class _GatherComm:
    has_mid = True

    def __init__(self, arrs, mid_frac=0.5):
        self.arrs = list(arrs)
        self.n = len(self.arrs)
        self.mid_frac = mid_frac

    def out_shapes(self):
        return [jax.ShapeDtypeStruct((N_DEV,) + a.shape, a.dtype) for a in self.arrs]

    def sems(self):
        return [pltpu.SemaphoreType.DMA((7 * self.n,)), pltpu.SemaphoreType.DMA((7 * self.n,)),
                pltpu.SemaphoreType.DMA((self.n,))]

    def _parts(self, ins, outs, sems):
        send_sems, recv_sems, local_sems = sems
        x, y, c = _my_pos()
        me, sibling = (x, y, c), (x, y, 1 - c)
        chips = [(1 - x, y), (x, 1 - y), (1 - x, 1 - y)]

        def copy(a, k, block, to, src=None):
            dst = outs[a].at[_slot(*block)]
            return pltpu.make_async_remote_copy(
                src_ref=dst if src is None else src, dst_ref=dst,
                send_sem=send_sems.at[a * 7 + k], recv_sem=recv_sems.at[a * 7 + k],
                device_id=to, device_id_type=MESH)

        local = [pltpu.make_async_copy(ins[a], outs[a].at[_slot(*me)], local_sems.at[a]) for a in range(self.n)]
        first = []
        for a in range(self.n):
            first.append(copy(a, 0, me, sibling, src=ins[a]))
            first += [copy(a, 1 + j, me, (*chip, c), src=ins[a]) for j, chip in enumerate(chips)]
        ici_in = [copy(a, 1 + j, (*chip, c), me) for j, chip in enumerate(chips) for a in range(self.n)]
        passed = [copy(a, 4 + j, (*chip, c), sibling) for j, chip in enumerate(chips) for a in range(self.n)]
        d2d_in = []
        for a in range(self.n):
            d2d_in.append(copy(a, 0, sibling, me))
            d2d_in += [copy(a, 4 + j, (*chip, 1 - c), me) for j, chip in enumerate(chips)]
        return local, first, ici_in, passed, d2d_in

    def start(self, ins, outs, sems):
        local, first, _, _, _ = self._parts(ins, outs, sems)
        for cp in local + first:
            cp.start()

    def mid(self, ins, outs, sems):
        _, _, ici_in, passed, _ = self._parts(ins, outs, sems)
        for arrived, fw in zip(ici_in, passed):
            arrived.wait_recv()
            fw.start()

    def finish(self, ins, outs, sems):
        local, first, _, passed, d2d_in = self._parts(ins, outs, sems)
        for cp in d2d_in:
            cp.wait_recv()
        for cp in first + passed:
            cp.wait_send()
        for cp in local:
            cp.wait()


class _ExchangeComm:
    has_mid = False

    def __init__(self, arrs):
        self.arrs = list(arrs)
        self.n = len(self.arrs)

    def out_shapes(self):
        return [jax.ShapeDtypeStruct(a.shape, a.dtype) for a in self.arrs]

    def sems(self):
        return [pltpu.SemaphoreType.DMA((7 * self.n,)), pltpu.SemaphoreType.DMA((7 * self.n,)),
                pltpu.SemaphoreType.DMA((self.n,))]

    def _copies(self, ins, outs, sems):
        send_sems, recv_sems, local_sems = sems
        x, y, c = _my_pos()
        mine = _slot(x, y, c)
        copies = [pltpu.make_async_copy(ins[a].at[mine], outs[a].at[mine], local_sems.at[a]) for a in range(self.n)]
        for k in range(1, N_DEV):
            px = (1 - x) if (k & 4) else x
            py = (1 - y) if (k & 2) else y
            pc = (1 - c) if (k & 1) else c
            for a in range(self.n):
                copies.append(pltpu.make_async_remote_copy(
                    src_ref=ins[a].at[_slot(px, py, pc)], dst_ref=outs[a].at[mine],
                    send_sem=send_sems.at[a * 7 + k - 1], recv_sem=recv_sems.at[a * 7 + k - 1],
                    device_id=(px, py, pc), device_id_type=MESH))
        return copies

    def start(self, ins, outs, sems):
        for cp in self._copies(ins, outs, sems):
            cp.start()

    def finish(self, ins, outs, sems):
        for cp in self._copies(ins, outs, sems):
            cp.wait()


def _fused_call(body, comm, operands, *, name, grid, in_specs, out_specs, out_shape, scratch_shapes=(),
                semantics=("arbitrary",)):
    n_in, n_out, n_scr = len(in_specs), len(out_specs), len(scratch_shapes)
    if comm is None:
        res = _pallas(body, name=name, grid=grid, in_specs=list(in_specs), out_specs=list(out_specs),
                      out_shape=list(out_shape), scratch_shapes=list(scratch_shapes),
                      compiler_params=_params(semantics))(*operands)
        return list(res), []
    k = comm.n
    steps = grid[0]

    def wrapped(*refs):
        ins, cins = refs[:n_in], refs[n_in:n_in + k]
        outs = refs[n_in + k:n_in + k + n_out]
        couts = refs[n_in + k + n_out:n_in + 2 * k + n_out]
        rest = refs[n_in + 2 * k + n_out:]
        scratch, sems = rest[:n_scr], rest[n_scr:]
        i = pl.program_id(0)

        @pl.when(i == 0)
        def _():
            comm.start(cins, couts, sems)

        body(*ins, *outs, *scratch)

        if comm.has_mid:
            @pl.when(i == int(steps * comm.mid_frac))
            def _():
                comm.mid(cins, couts, sems)

        @pl.when(i == steps - 1)
        def _():
            comm.finish(cins, couts, sems)

    any_spec = pl.BlockSpec(memory_space=pl.ANY)
    res = _pallas(wrapped, name=name, grid=grid, in_specs=list(in_specs) + [any_spec] * k,
                  out_specs=list(out_specs) + [any_spec] * k, out_shape=list(out_shape) + comm.out_shapes(),
                  scratch_shapes=list(scratch_shapes) + comm.sems(),
                  compiler_params=_params(("arbitrary",)))(*operands, *comm.arrs)
    return list(res[:n_out]), list(res[n_out:])


def _mod_part(c_all, ada_w):
    cols = ada_w.shape[2]

    def body(c_ref, w_ref, o_ref):
        cv = c_ref[...]
        cond = cv * _sigmoid(cv)
        for l in range(2):
            o_ref[l] = _dot(cond, w_ref[l])

    return _pallas(
        body, name="mod_part", grid=(1,),
        in_specs=[_full((N_DEV, D)), _full((2, D, cols))],
        out_specs=_full((2, N_DEV, cols)),
        out_shape=jax.ShapeDtypeStruct((2, N_DEV, cols), F32),
        compiler_params=_params(("arbitrary",)),
    )(c_all, ada_w)


def _ada_update(c_all, dmod_cols, dmod_all, ada_w, m_w, v_w, ada_b, m_b, v_b):
    cols = ada_w.shape[2]
    nb = ada_b.shape[1]

    def body(c_ref, dmc_ref, dma_ref, w_ref, mw_ref, vw_ref, b_ref, mb_ref, vb_ref,
             gw_ref, dw_ref, nmw_ref, nvw_ref, gb_ref, db_ref, nmb_ref, nvb_ref):
        cv = c_ref[...]
        cond = cv * _sigmoid(cv)
        for l in range(2):
            g = _dot_tn(cond, dmc_ref[l])
            gw_ref[l] = g
            dlt, m2, v2 = _adam(w_ref[l], g, mw_ref[l], vw_ref[l])
            dw_ref[l] = dlt
            nmw_ref[l] = m2
            nvw_ref[l] = v2
        gb = dma_ref[0]
        for i in range(1, N_DEV):
            gb = gb + dma_ref[i]
        gb_ref[...] = gb
        dlt, m2, v2 = _adam(b_ref[...], gb, mb_ref[...], vb_ref[...])
        db_ref[...] = dlt
        nmb_ref[...] = m2
        nvb_ref[...] = v2

    wspec = _full((2, D, cols))
    bspec = _full((2, nb))
    wshape = jax.ShapeDtypeStruct((2, D, cols), F32)
    bshape = jax.ShapeDtypeStruct((2, nb), F32)
    return _pallas(
        body, name="ada_update", grid=(1,),
        in_specs=[_full((N_DEV, D)), _full((2, N_DEV, cols)), _full((N_DEV, 2, nb)),
                  wspec, wspec, wspec, bspec, bspec, bspec],
        out_specs=[wspec] * 4 + [bspec] * 4,
        out_shape=[wshape] * 4 + [bshape] * 4,
        compiler_params=_params(("arbitrary",)),
    )(c_all, dmod_cols, dmod_all, ada_w, m_w, v_w, ada_b, m_b, v_b)


def _ev_in(x, mod, w_in, rc, rs1, rs2, comm=None):
    T = x.shape[0]

    def body(x_ref, mod_ref, w_ref, c_ref, s1_ref, s2_ref, q_ref, kv_ref, su_ref, sv_ref, g_ref):
        h = x_ref[...] * (1.0 + mod_ref[1:2, :]) + mod_ref[0:1, :]
        p = _dot(h, w_ref[...])
        c, s1, s2 = c_ref[...], s1_ref[...], s2_ref[...]
        for j in range(ATTN_W // LANE):
            qr = _rope_fwd(p[:, j * LANE:(j + 1) * LANE], c, s1, s2)
            q_ref[:, j * LANE:(j + 1) * LANE] = (qr * (HEAD_DIM ** -0.5)).astype(BF16)
        low = lax.broadcasted_iota(jnp.int32, (TM, LANE), 1) < HEAD_DIM
        for j, val in enumerate((_rope_fwd(p[:, 512:640], c, s1, s2), p[:, 640:768])):
            swapped = pltpu.roll(val, HEAD_DIM, 1)
            tiles = (jnp.where(low, val, 0.0), jnp.where(low, 0.0, swapped),
                     jnp.where(low, swapped, 0.0), jnp.where(low, 0.0, val))
            for k, tile in enumerate(tiles):
                kv_ref[:, (4 * j + k) * LANE:(4 * j + k + 1) * LANE] = tile.astype(BF16)
        su_ref[...] = p[:, 768:1280].astype(BF16)
        sv_ref[...] = p[:, 1280:1792].astype(BF16)
        g_ref[...] = p[:, 1792:2816].astype(BF16)

    sh = lambda w: jax.ShapeDtypeStruct((T, w), BF16)
    return _fused_call(
        body, comm, (x, mod, w_in, rc, rs1, rs2), name="ev_in", grid=(T // TM,),
        in_specs=[_tile(TM, D), _full((3, D)), _full((D, EV_IN)), _tile(TM, LANE), _tile(TM, LANE), _tile(TM, LANE)],
        out_specs=[_tile(TM, ATTN_W), _tile(TM, KVX_W), _tile(TM, SG_W), _tile(TM, SG_W), _tile(TM, D)],
        out_shape=[sh(ATTN_W), sh(KVX_W), sh(SG_W), sh(SG_W), sh(D)], semantics=("parallel",))


def _band_specs(width, nb):
    return [pl.BlockSpec((BLK, width), lambda n: (jnp.maximum(n - 1, 0), 0)),
            pl.BlockSpec((BLK, width), lambda n: (n, 0)),
            pl.BlockSpec((BLK, width), lambda n: (jnp.minimum(n + 1, nb - 1), 0))]


def _band_bias(bias_ref, n, nb):
    rows = lax.broadcasted_iota(jnp.int32, (3 * BLK, 1), 0)
    outside = ((rows < BLK) & (n == 0)) | ((rows >= 2 * BLK) & (n == nb - 1))
    return bias_ref[...] + jnp.where(outside, NEG_INF, 0.0)


def _split_bf16(v):
    hi = v.astype(BF16)
    return hi, (v - hi.astype(F32)).astype(BF16)


def _group_mean(v, a_ref, exact_bf16=False):
    hi, lo = _split_bf16(v)
    a = a_ref[...]
    out = []
    for t in range(SG_W // LANE):
        sl = slice(t * LANE, (t + 1) * LANE)
        r = jnp.dot(hi[:, sl], a, preferred_element_type=F32)
        if not exact_bf16:
            r = r + jnp.dot(lo[:, sl], a, preferred_element_type=F32)
        out.append(r)
    return jnp.concatenate(out, axis=-1)


def _sg_core(sv_ref, lng, lnb, a_ref, w_ref, bfull_ref):
    svf = sv_ref[...].astype(F32)
    xc = svf - _group_mean(svf, a_ref, exact_bf16=True)
    rstd = lax.rsqrt(_group_mean(xc * xc, a_ref) + LN_EPS)
    xhat = xc * rstd
    vb = (xhat * lng + lnb).astype(BF16)
    low = lax.broadcasted_iota(jnp.int32, (BLK, LANE), 1) < SG_DIM
    tiles = []
    for t in range(SG_W // LANE):
        v2 = vb[:, t * LANE:(t + 1) * LANE]
        r0 = jnp.dot(w_ref[2 * t], v2, preferred_element_type=F32)
        r1 = jnp.dot(w_ref[2 * t + 1], v2, preferred_element_type=F32)
        tiles.append(jnp.where(low, r0, r1))
    svm = jnp.concatenate(tiles, axis=-1) + bfull_ref[...]
    return xhat, rstd, vb, svm


def _mix0_fwd(q, kvx, su, sv, g0, sink_l, bias, a128, sg_lng, sg_lnb, sg_w, sg_bfull, comm=None):
    T = q.shape[0]
    nb = T // BLK

    def body(q_ref, kp_ref, kc_ref, kn_ref, su_ref, sv_ref, g_ref, sink_ref, bias_ref, a_ref, lng_ref, lnb_ref,
             w_ref, bfull_ref, ycat_ref, y0_ref, lse_ref):
        n = pl.program_id(0)
        bias = _band_bias(bias_ref, n, nb)
        kvx = jnp.concatenate([kp_ref[...], kc_ref[...], kn_ref[...]], axis=0)
        tiles = []
        for t in range(ATTN_W // LANE):
            qt = q_ref[:, t * LANE:(t + 1) * LANE]
            acc = None
            for par in range(2):
                h = 2 * t + par
                kt = 2 * (h // 4) + par
                ke = kvx[:, kt * LANE:(kt + 1) * LANE]
                ve = kvx[:, (4 + kt) * LANE:(5 + kt) * LANE]
                st = _dot_nt(ke, qt) + bias
                sk = sink_ref[:, h * LANE:(h + 1) * LANE]
                m = jnp.maximum(jnp.max(st, axis=0, keepdims=True), sk)
                p = jnp.exp(st - m)
                denom = jnp.sum(p, axis=0, keepdims=True) + jnp.exp(sk - m)
                contrib = _dot_tn(p * (1.0 / denom), ve)
                acc = contrib if acc is None else acc + contrib
                lse_ref[0, :, h * LANE:(h + 1) * LANE] = m + jnp.log(denom)
            tiles.append(acc)
        _, _, _, svm = _sg_core(sv_ref, lng_ref[...], lnb_ref[...], a_ref, w_ref, bfull_ref)
        tiles.append(su_ref[...].astype(F32) * svm)
        ycat = jnp.concatenate(tiles, axis=-1)
        gf = g_ref[...].astype(F32)
        ycat_ref[...] = ycat.astype(BF16)
        y0_ref[...] = (ycat * (gf * _sigmoid(gf))).astype(BF16)

    return _fused_call(
        body, comm, (q, kvx, kvx, kvx, su, sv, g0, sink_l, bias, a128, sg_lng, sg_lnb, sg_w, sg_bfull),
        name="mix0_fwd", grid=(nb,),
        in_specs=[_tile(BLK, ATTN_W)] + _band_specs(KVX_W, nb) + [
            _tile(BLK, SG_W), _tile(BLK, SG_W), _tile(BLK, D), _full((1, N_HEADS * LANE)), _full((3 * BLK, LANE)),
            _full((LANE, LANE)), _full((1, SG_W)), _full((1, SG_W)), _full((SG_GROUPS, BLK, BLK)),
            _full((BLK, SG_W))],
        out_specs=[_tile(BLK, D), _tile(BLK, D), pl.BlockSpec((1, 1, N_HEADS * LANE), lambda n: (n, 0, 0))],
        out_shape=[jax.ShapeDtypeStruct((T, D), BF16), jax.ShapeDtypeStruct((T, D), BF16),
                   jax.ShapeDtypeStruct((nb, 1, N_HEADS * LANE), F32)], semantics=("parallel",))


def _ev_out(y0, w_out, x, mod, lnp):
    T = x.shape[0]

    def body(y_ref, w_ref, x_ref, mod_ref, ln_ref, out_ref, z_ref, x1_ref):
        out = _dot(y_ref[...], w_ref[...])
        z = ALPHA * x_ref[...] + mod_ref[2:3, :] * out
        x1, _, _ = _ln_fwd(z, ln_ref[0:1, :], ln_ref[1:2, :])
        out_ref[...] = out.astype(BF16)
        z_ref[...] = z
        x1_ref[...] = x1

    return _pallas(
        body, name="ev_out", grid=(T // TM,),
        in_specs=[_tile(TM, D), _full((D, D)), _tile(TM, D), _full((3, D)), _full((2, D))],
        out_specs=[_tile(TM, D)] * 3,
        out_shape=[jax.ShapeDtypeStruct((T, D), BF16), jax.ShapeDtypeStruct((T, D), F32),
                   jax.ShapeDtypeStruct((T, D), F32)],
        compiler_params=_params(("parallel",)),
    )(y0, w_out, x, mod, lnp)


def _od_in(x1, mod, w_in):
    T = x1.shape[0]

    def body(x_ref, mod_ref, w_ref, xr_ref, g_ref):
        h = x_ref[...] * (1.0 + mod_ref[1:2, :]) + mod_ref[0:1, :]
        p = _dot(h, w_ref[...])
        xr_ref[...] = p[:, :D]
        g_ref[...] = p[:, D:].astype(BF16)

    return _pallas(
        body, name="od_in", grid=(T // TM,),
        in_specs=[_tile(TM, D), _full((3, D)), _full((D, OD_IN))],
        out_specs=[_tile(TM, D), _tile(TM, D)],
        out_shape=[jax.ShapeDtypeStruct((T, D), F32), jax.ShapeDtypeStruct((T, D), BF16)],
        compiler_params=_params(("parallel",)),
    )(x1, mod, w_in)


def _ext_rows(prev_ref, cur, next_ref, j, n):
    prev = jnp.where(j > 0, prev_ref[...], 0.0)
    nxt = jnp.where(j < n - 1, next_ref[...], 0.0)
    return jnp.concatenate([prev, cur, nxt], axis=0)


def _shift_rows(ext, off, rows):
    total = ext.shape[0]
    if off == 0:
        return ext[SUBLANE:SUBLANE + rows, :]
    return pltpu.roll(ext, (-off) % total, 0)[SUBLANE:SUBLANE + rows, :]


def _conv_fwd(ext, cw, cb, rows):
    xc = cb
    for k in range(4):
        xc = xc + cw[k:k + 1, :] * _shift_rows(ext, k - 2, rows)
    return xc


def _gates(xc, wa_ref, wx_ref, ba, bx, lam):
    pr, pi = [], []
    for h in range(RNN_HEADS):
        xh = xc[:, h * RNN_HD:(h + 1) * RNN_HD].astype(BF16)
        pr.append(_dot(xh, wa_ref[h]))
        pi.append(_dot(xh, wx_ref[h]))
    r = _sigmoid(jnp.concatenate(pr, axis=-1) + ba)
    ig = _sigmoid(jnp.concatenate(pi, axis=-1) + bx)
    sp = jnp.maximum(-lam, 0.0) + jnp.log(1.0 + jnp.exp(-jnp.abs(lam)))
    neg_log_a = RG_C * r * sp
    a = jnp.exp(-neg_log_a)
    s2 = (1.0 + a * a) * jnp.tanh(neg_log_a)
    inv_s = lax.rsqrt(jnp.maximum(s2, 1e-30))
    return r, ig, sp, a, s2 * inv_s, inv_s


def _scan_tile(a_ref, b_ref, o_ref, carry_ref, rows, reverse):
    ridx = lax.broadcasted_iota(jnp.int32, (SUBLANE, D), 0)
    groups = rows // SUBLANE

    def group(gi, h):
        g = (groups - 1 - gi) if reverse else gi
        off = pl.multiple_of(g * SUBLANE, SUBLANE)
        a = a_ref[pl.ds(off, SUBLANE), :]
        b = b_ref[pl.ds(off, SUBLANE), :]
        for sh in (1, 2, 4):
            if reverse:
                keep = ridx < SUBLANE - sh
                a_p = jnp.where(keep, pltpu.roll(a, SUBLANE - sh, 0), 1.0)
                b_p = jnp.where(keep, pltpu.roll(b, SUBLANE - sh, 0), 0.0)
            else:
                keep = ridx >= sh
                a_p = jnp.where(keep, pltpu.roll(a, sh, 0), 1.0)
                b_p = jnp.where(keep, pltpu.roll(b, sh, 0), 0.0)
            b = b + a * b_p
            a = a * a_p
        hh = b + a * h
        o_ref[pl.ds(off, SUBLANE), :] = hh
        return hh[0:1, :] if reverse else hh[SUBLANE - 1:SUBLANE, :]

    carry_ref[...] = lax.fori_loop(0, groups, group, carry_ref[...])


def _rglru_fwd(xr, cw, cb, wa, wx, ba, bx, lam, reverse, name):
    T = xr.shape[0]
    n = T // TS
    prev_spec, next_spec = _halo_specs(TS, D, n, T, reverse)

    def body(prev_ref, cur_ref, next_ref, cw_ref, cb_ref, wa_ref, wx_ref, ba_ref, bx_ref, lam_ref,
             h_ref, a_s, b_s, carry):
        i = pl.program_id(0)
        j = (n - 1 - i) if reverse else i

        @pl.when(i == 0)
        def _():
            carry[...] = jnp.zeros_like(carry)

        ext = _ext_rows(prev_ref, cur_ref[...], next_ref, j, n)
        xc = _conv_fwd(ext, cw_ref[...], cb_ref[...], TS)
        _, ig, _, a, s, _ = _gates(xc, wa_ref, wx_ref, ba_ref[...], bx_ref[...], lam_ref[...])
        a_s[...] = a
        b_s[...] = s * ig * xc
        _scan_tile(a_s, b_s, h_ref, carry, TS, reverse)

    wspec = _full((RNN_HEADS, RNN_HD, RNN_HD))
    return _pallas(
        body, name=name, grid=(n,),
        in_specs=[prev_spec, _rev_tile(TS, D, n, reverse), next_spec, _full((4, D)), _full((1, D)),
                  wspec, wspec, _full((1, D)), _full((1, D)), _full((1, D))],
        out_specs=_rev_tile(TS, D, n, reverse),
        out_shape=jax.ShapeDtypeStruct((T, D), F32),
        scratch_shapes=[pltpu.VMEM((TS, D), F32), pltpu.VMEM((TS, D), F32), pltpu.VMEM((1, D), F32)],
        compiler_params=_params(("arbitrary",)),
    )(xr, xr, xr, cw, cb, wa, wx, ba, bx, lam)


def _od_out(hf, hb, g1, w_out, x1, tgt, mod, lnp):
    T = x1.shape[0]

    def body(hf_ref, hb_ref, g_ref, w_ref, x_ref, t_ref, mod_ref, ln_ref,
             dh_ref, dg_ref, dx_ref, dwb_ref, vec_ref, dw_ref):
        i = pl.program_id(0)

        @pl.when(i == 0)
        def _():
            dw_ref[...] = jnp.zeros_like(dw_ref)
            vec_ref[...] = jnp.zeros_like(vec_ref)

        hs = hf_ref[...] + hb_ref[...]
        sg, dsg = _silu_and_grad(g_ref[...].astype(F32))
        yr = (hs * sg).astype(BF16)
        w = w_ref[...]
        out = _dot(yr, w)
        gate = mod_ref[2:3, :]
        z = ALPHA * x_ref[...] + gate * out
        lng = ln_ref[0:1, :]
        x2, xhat, rstd = _ln_fwd(z, lng, ln_ref[1:2, :])
        diff = x2 - t_ref[...]
        vec_ref[3:4, 0:LANE] += 0.5 * jnp.sum(diff * diff) * (1.0 / D)
        dx2 = diff * (1.0 / D)
        dz = _ln_bwd(dx2, xhat, rstd, lng)
        vec_ref[0:1, :] += _rowsum(dx2 * xhat)
        vec_ref[1:2, :] += _rowsum(dx2)
        vec_ref[2:3, :] += _rowsum(dz * out)
        dout = (dz * gate).astype(BF16)
        dyr = _dot_nt(dout, w)
        dw_ref[...] += _dot_tn(yr, dout)
        dh_ref[...] = dyr * sg
        dg_ref[...] = (dyr * hs * dsg).astype(BF16)
        dx_ref[...] = ALPHA * dz

        @pl.when(i == T // TM - 1)
        def _():
            dwb_ref[...] = dw_ref[...].astype(BF16)

    return _pallas(
        body, name="od_out", grid=(T // TM,),
        in_specs=[_tile(TM, D), _tile(TM, D), _tile(TM, D), _full((D, D)), _tile(TM, D), _tile(TM, D),
                  _full((3, D)), _full((2, D))],
        out_specs=[_tile(TM, D), _tile(TM, D), _tile(TM, D), _full((D, D)), _full((SUBLANE, D))],
        out_shape=[jax.ShapeDtypeStruct((T, D), F32), jax.ShapeDtypeStruct((T, D), BF16),
                   jax.ShapeDtypeStruct((T, D), F32), jax.ShapeDtypeStruct((D, D), BF16),
                   jax.ShapeDtypeStruct((SUBLANE, D), F32)],
        scratch_shapes=[pltpu.VMEM((D, D), F32)],
        compiler_params=_params(("arbitrary",)),
    )(hf, hb, g1, w_out, x1, tgt, mod, lnp)


def _rglru_bwd(xr, dh, h, cw, cb, wa, wx, ba, bx, lam, reverse, name, comm=None):
    T = xr.shape[0]
    n = T // TS
    adj_rev = not reverse
    xprev_spec, xnext_spec = _halo_specs(TS, D, n, T, adj_rev)
    hprev_spec, hnext_spec = _halo_specs(TS, D, n, T, adj_rev)
    h_halo_spec = hnext_spec if reverse else hprev_spec

    def body(xprev_ref, xcur_ref, xnext_ref, dh_ref, h_ref, hh_ref, cw_ref, cb_ref, wa_ref, wx_ref,
             ba_ref, bx_ref, lam_ref, dxc_ref, dwa_ref, dwx_ref, vec_ref, a_s, b_s, l_s, carry, a_edge):
        i = pl.program_id(0)
        j = (n - 1 - i) if adj_rev else i

        @pl.when(i == 0)
        def _():
            carry[...] = jnp.zeros_like(carry)
            a_edge[...] = jnp.zeros_like(a_edge)
            dwa_ref[...] = jnp.zeros_like(dwa_ref)
            dwx_ref[...] = jnp.zeros_like(dwx_ref)
            vec_ref[...] = jnp.zeros_like(vec_ref)

        ext = _ext_rows(xprev_ref, xcur_ref[...], xnext_ref, j, n)
        xc = _conv_fwd(ext, cw_ref[...], cb_ref[...], TS)
        lam = lam_ref[...]
        r, ig, sp, a, s, inv_s = _gates(xc, wa_ref, wx_ref, ba_ref[...], bx_ref[...], lam)

        rows = lax.broadcasted_iota(jnp.int32, (TS, D), 0)
        hcur = h_ref[...]
        if reverse:
            a_sh = jnp.where(rows == 0, a_edge[...], pltpu.roll(a, 1, 0))
            halo = jnp.where(j < n - 1, hh_ref[0:1, :], 0.0)
            h_nb = jnp.where(rows == TS - 1, halo, pltpu.roll(hcur, TS - 1, 0))
        else:
            a_sh = jnp.where(rows == TS - 1, a_edge[...], pltpu.roll(a, TS - 1, 0))
            halo = jnp.where(j > 0, hh_ref[SUBLANE - 1:SUBLANE, :], 0.0)
            h_nb = jnp.where(rows == 0, halo, pltpu.roll(hcur, 1, 0))
        a_s[...] = a_sh
        b_s[...] = dh_ref[...]
        _scan_tile(a_s, b_s, l_s, carry, TS, adj_rev)
        a_edge[...] = a[TS - 1:TS, :] if reverse else a[0:1, :]

        lm = l_s[...]
        da = lm * h_nb
        di = lm * s * xc
        dxc = lm * s * ig
        ds = lm * ig * xc
        dlog_a = a * (da - ds * a * inv_s)
        dr = (-RG_C) * sp * dlog_a
        dsp = _rowsum((-RG_C) * r * dlog_a)
        dpr = dr * r * (1.0 - r)
        dpi = di * ig * (1.0 - ig)
        vec_ref[0:1, :] += _rowsum(dpr)
        vec_ref[1:2, :] += _rowsum(dpi)
        vec_ref[2:3, :] += dsp * (-_sigmoid(-lam))
        parts = []
        for hd in range(RNN_HEADS):
            sl = slice(hd * RNN_HD, (hd + 1) * RNN_HD)
            xh = xc[:, sl].astype(BF16)
            dprh = dpr[:, sl].astype(BF16)
            dpih = dpi[:, sl].astype(BF16)
            parts.append(_dot_nt(dprh, wa_ref[hd]) + _dot_nt(dpih, wx_ref[hd]))
            dwa_ref[hd] += _dot_tn(xh, dprh)
            dwx_ref[hd] += _dot_tn(xh, dpih)
        dxc_ref[...] = dxc + jnp.concatenate(parts, axis=-1)

    wspec = _full((RNN_HEADS, RNN_HD, RNN_HD))
    cur = _rev_tile(TS, D, n, adj_rev)
    return _fused_call(
        body, comm, (xr, xr, xr, dh, h, h, cw, cb, wa, wx, ba, bx, lam), name=name, grid=(n,),
        in_specs=[xprev_spec, cur, xnext_spec, cur, cur, h_halo_spec, _full((4, D)), _full((1, D)),
                  wspec, wspec, _full((1, D)), _full((1, D)), _full((1, D))],
        out_specs=[cur, wspec, wspec, _full((SUBLANE, D))],
        out_shape=[jax.ShapeDtypeStruct((T, D), F32),
                   jax.ShapeDtypeStruct((RNN_HEADS, RNN_HD, RNN_HD), F32),
                   jax.ShapeDtypeStruct((RNN_HEADS, RNN_HD, RNN_HD), F32),
                   jax.ShapeDtypeStruct((SUBLANE, D), F32)],
        scratch_shapes=[pltpu.VMEM((TS, D), F32)] * 3 + [pltpu.VMEM((1, D), F32)] * 2)


def _od_in_bwd(dxcf, dxcb, xr, dg1, x1, dx1p, mod, w_in, cw, comm=None):
    T = x1.shape[0]
    n = T // TM
    slab = OD_IN // N_DEV
    prev_spec, next_spec = _halo_specs(TM, D, n, T, False)

    def body(fp_ref, fc_ref, fn_ref, bp_ref, bc_ref, bn_ref, xp_ref, xc_ref, xn_ref, dg_ref, x1_ref, dxp_ref,
             mod_ref, w_ref, cw_ref, dx_ref, dwb_ref, vec_ref, dw_ref):
        i = pl.program_id(0)

        @pl.when(i == 0)
        def _():
            dw_ref[...] = jnp.zeros_like(dw_ref)
            vec_ref[...] = jnp.zeros_like(vec_ref)

        dcur = fc_ref[...] + bc_ref[...]
        dprev = jnp.where(i > 0, fp_ref[...] + bp_ref[...], 0.0)
        dnext = jnp.where(i < n - 1, fn_ref[...] + bn_ref[...], 0.0)
        dext = jnp.concatenate([dprev, dcur, dnext], axis=0)
        xext = _ext_rows(xp_ref, xc_ref[...], xn_ref, i, n)
        cw_v = cw_ref[...]
        dxr = None
        for k in range(4):
            term = cw_v[k:k + 1, :] * _shift_rows(dext, 2 - k, TM)
            dxr = term if dxr is None else dxr + term
            vec_ref[k:k + 1, :] += _rowsum(dcur * _shift_rows(xext, k - 2, TM))
        vec_ref[4:5, :] += _rowsum(dcur)
        dp = jnp.concatenate([dxr.astype(BF16), dg_ref[...]], axis=-1)
        x1v = x1_ref[...]
        scale1 = 1.0 + mod_ref[1:2, :]
        h1 = (x1v * scale1 + mod_ref[0:1, :]).astype(BF16)
        dh1 = _dot_nt(dp, w_ref[...])
        dw_ref[...] += _dot_tn(h1, dp)
        dx_ref[...] = dxp_ref[...] + dh1 * scale1
        vec_ref[5:6, :] += _rowsum(dh1)
        vec_ref[6:7, :] += _rowsum(dh1 * x1v)

        @pl.when(i == n - 1)
        def _():
            for j in range(N_DEV):
                dwb_ref[j] = dw_ref[:, j * slab:(j + 1) * slab].astype(BF16)

    t = _tile(TM, D)
    return _fused_call(
        body, comm, (dxcf, dxcf, dxcf, dxcb, dxcb, dxcb, xr, xr, xr, dg1, x1, dx1p, mod, w_in, cw),
        name="od_in_bwd", grid=(n,),
        in_specs=[prev_spec, t, next_spec, prev_spec, t, next_spec, prev_spec, t, next_spec, t, t, t,
                  _full((3, D)), _full((D, OD_IN)), _full((4, D))],
        out_specs=[t, _full((N_DEV, D, slab)), _full((SUBLANE, D))],
        out_shape=[jax.ShapeDtypeStruct((T, D), F32), jax.ShapeDtypeStruct((N_DEV, D, slab), BF16),
                   jax.ShapeDtypeStruct((SUBLANE, D), F32)],
        scratch_shapes=[pltpu.VMEM((D, OD_IN), F32)])


def _ev_out_bwd(dx1, z0, out0, y0, ycat, g0, w_out, mod, lnp):
    T = dx1.shape[0]

    def body(dx_ref, z_ref, out_ref, y0_ref, yc_ref, g_ref, w_ref, mod_ref, ln_ref,
             dxp_ref, dyc_ref, dg_ref, dwb_ref, vec_ref, dw_ref):
        i = pl.program_id(0)

        @pl.when(i == 0)
        def _():
            dw_ref[...] = jnp.zeros_like(dw_ref)
            vec_ref[...] = jnp.zeros_like(vec_ref)

        lng = ln_ref[0:1, :]
        _, xhat, rstd = _ln_fwd(z_ref[...], lng, ln_ref[1:2, :])
        dy = dx_ref[...]
        dz = _ln_bwd(dy, xhat, rstd, lng)
        vec_ref[0:1, :] += _rowsum(dy * xhat)
        vec_ref[1:2, :] += _rowsum(dy)
        vec_ref[2:3, :] += _rowsum(dz * out_ref[...].astype(F32))
        dout = (dz * mod_ref[2:3, :]).astype(BF16)
        dy0 = _dot_nt(dout, w_ref[...])
        dw_ref[...] += _dot_tn(y0_ref[...], dout)
        sg, dsg = _silu_and_grad(g_ref[...].astype(F32))
        dyc_ref[...] = (dy0 * sg).astype(BF16)
        dg_ref[...] = (dy0 * yc_ref[...].astype(F32) * dsg).astype(BF16)
        dxp_ref[...] = ALPHA * dz

        @pl.when(i == T // TM - 1)
        def _():
            dwb_ref[...] = dw_ref[...].astype(BF16)

    t = _tile(TM, D)
    return _pallas(
        body, name="ev_out_bwd", grid=(T // TM,),
        in_specs=[t, t, t, t, t, t, _full((D, D)), _full((3, D)), _full((2, D))],
        out_specs=[t, t, t, _full((D, D)), _full((SUBLANE, D))],
        out_shape=[jax.ShapeDtypeStruct((T, D), F32), jax.ShapeDtypeStruct((T, D), BF16),
                   jax.ShapeDtypeStruct((T, D), BF16), jax.ShapeDtypeStruct((D, D), BF16),
                   jax.ShapeDtypeStruct((SUBLANE, D), F32)],
        scratch_shapes=[pltpu.VMEM((D, D), F32)],
        compiler_params=_params(("arbitrary",)),
    )(dx1, z0, out0, y0, ycat, g0, w_out, mod, lnp)


def _mix0_bwd(q, kvx, lse, dyc, ycat, su, sv, sink_l, bias, a128, gsum, sel, sg_lng, sg_lnb, sg_w, sg_bfull,
              rc, rs1, rs2, comm=None):
    T = q.shape[0]
    nb = T // BLK

    def body(q_ref, kp_ref, kc_ref, kn_ref, lse_ref, dyc_ref, yc_ref, su_ref, sv_ref, sink_ref, bias_ref, a_ref,
             gsum_ref, sel_ref, lng_ref, lnb_ref, w_ref, bfull_ref, c_ref, s1_ref, s2_ref,
             dq_ref, dkv_ref, dsu_ref, dsv_ref, dw_ref, dbt_ref, vec_ref, dsink_ref):
        n = pl.program_id(0)

        @pl.when(n == 0)
        def _():
            dkv_ref[...] = jnp.zeros_like(dkv_ref)
            dw_ref[...] = jnp.zeros_like(dw_ref)
            dbt_ref[...] = jnp.zeros_like(dbt_ref)
            vec_ref[...] = jnp.zeros_like(vec_ref)
            dsink_ref[...] = jnp.zeros_like(dsink_ref)

        band = pl.ds(pl.multiple_of(n * BLK + (TM - BLK), BLK), 3 * BLK)
        bias = _band_bias(bias_ref, n, nb)
        kvx = jnp.concatenate([kp_ref[...], kc_ref[...], kn_ref[...]], axis=0)
        low = lax.broadcasted_iota(jnp.int32, (BLK, LANE), 1) < HEAD_DIM
        sel = sel_ref[...]
        c, s1, s2 = c_ref[...], s1_ref[...], s2_ref[...]
        for kvh in range(2):
            dkx = jnp.zeros((3 * BLK, LANE), F32)
            dvx = jnp.zeros((3 * BLK, LANE), F32)
            for t in range(2 * kvh, 2 * kvh + 2):
                tl = slice(t * LANE, (t + 1) * LANE)
                qt = q_ref[:, tl]
                do = dyc_ref[:, tl]
                p_hi, p_lo = _split_bf16(do.astype(F32) * yc_ref[:, tl].astype(F32))
                deltas = _dot_nt(sel, p_hi) + _dot_nt(sel, p_lo)
                dq_acc = None
                for par in range(2):
                    h = 2 * t + par
                    hl = slice(h * LANE, (h + 1) * LANE)
                    kt = 2 * kvh + par
                    ke = kvx[:, kt * LANE:(kt + 1) * LANE]
                    ve = kvx[:, (4 + kt) * LANE:(5 + kt) * LANE]
                    lse = lse_ref[0, :, hl]
                    delta = deltas[par:par + 1, :]
                    pt = jnp.exp(_dot_nt(ke, qt) + bias - lse)
                    dst = (pt * (_dot_nt(ve, do) - delta)).astype(BF16)
                    dsink_ref[:, hl] += jnp.exp(sink_ref[:, hl] - lse) * delta
                    part = _dot_tn(dst, ke)
                    dq_acc = part if dq_acc is None else dq_acc + part
                    mine = low if par == 0 else jnp.logical_not(low)
                    dkx = dkx + jnp.dot(dst, jnp.where(mine, qt, jnp.zeros_like(qt)), preferred_element_type=F32)
                    dvx = dvx + jnp.dot(pt.astype(BF16), jnp.where(mine, do, jnp.zeros_like(do)),
                                        preferred_element_type=F32)
                dq_ref[:, tl] = _rope_bwd(dq_acc * (HEAD_DIM ** -0.5), c, s1, s2).astype(BF16)
            dkv_ref[band, kvh * LANE:(kvh + 1) * LANE] += dkx
            dkv_ref[band, (2 + kvh) * LANE:(3 + kvh) * LANE] += dvx

        lng = lng_ref[...]
        xhat, rstd, vb, svm = _sg_core(sv_ref, lng, lnb_ref[...], a_ref, w_ref, bfull_ref)
        dy = dyc_ref[:, ATTN_W:].astype(F32)
        dsu_ref[...] = (dy * svm).astype(BF16)
        dsvm = dy * su_ref[...].astype(F32)
        d_hi, d_lo = _split_bf16(dsvm)
        gsum = gsum_ref[...]
        dbt_ref[...] += jnp.dot(d_hi, gsum, preferred_element_type=F32) + jnp.dot(d_lo, gsum,
                                                                                 preferred_element_type=F32)
        tiles = []
        for t in range(SG_W // LANE):
            tl = slice(t * LANE, (t + 1) * LANE)
            dt, v2 = d_hi[:, tl], vb[:, tl]
            dw_ref[2 * t] += _dot_nt(jnp.where(low, dt, jnp.zeros_like(dt)), v2)
            dw_ref[2 * t + 1] += _dot_nt(jnp.where(low, jnp.zeros_like(dt), dt), v2)
            tiles.append(jnp.where(low, _dot_tn(w_ref[2 * t], dt), _dot_tn(w_ref[2 * t + 1], dt)))
        dvgn = jnp.concatenate(tiles, axis=-1)
        vec_ref[0:1, :] += _rowsum(dvgn * xhat)
        vec_ref[1:2, :] += _rowsum(dvgn)
        dxh = dvgn * lng
        m1 = _group_mean(dxh, a_ref)
        m2 = _group_mean(dxh * xhat, a_ref)
        dsv_ref[...] = (rstd * (dxh - m1 - xhat * m2)).astype(BF16)

    return _fused_call(
        body, comm, (q, kvx, kvx, kvx, lse, dyc, ycat, su, sv, sink_l, bias, a128, gsum, sel, sg_lng, sg_lnb, sg_w,
                     sg_bfull, rc, rs1, rs2),
        name="mix0_bwd", grid=(nb,),
        in_specs=[_tile(BLK, ATTN_W)] + _band_specs(KVX_W, nb) + [
            pl.BlockSpec((1, 1, N_HEADS * LANE), lambda n: (n, 0, 0)), _tile(BLK, D), _tile(BLK, D),
            _tile(BLK, SG_W), _tile(BLK, SG_W), _full((1, N_HEADS * LANE)), _full((3 * BLK, LANE)),
            _full((LANE, LANE)), _full((SG_W, LANE)), _full((SUBLANE, LANE)), _full((1, SG_W)), _full((1, SG_W)),
            _full((SG_GROUPS, BLK, BLK)), _full((BLK, SG_W)), _tile(BLK, LANE), _tile(BLK, LANE), _tile(BLK, LANE)],
        out_specs=[_tile(BLK, ATTN_W), _full((T + 2 * TM, 4 * LANE)), _tile(BLK, SG_W), _tile(BLK, SG_W),
                   _full((SG_GROUPS, BLK, BLK)), _full((BLK, LANE)), _full((SUBLANE, SG_W)),
                   _full((1, N_HEADS * LANE))],
        out_shape=[jax.ShapeDtypeStruct((T, ATTN_W), BF16), jax.ShapeDtypeStruct((T + 2 * TM, 4 * LANE), F32),
                   jax.ShapeDtypeStruct((T, SG_W), BF16), jax.ShapeDtypeStruct((T, SG_W), BF16),
                   jax.ShapeDtypeStruct((SG_GROUPS, BLK, BLK), F32), jax.ShapeDtypeStruct((BLK, LANE), F32),
                   jax.ShapeDtypeStruct((SUBLANE, SG_W), F32), jax.ShapeDtypeStruct((1, N_HEADS * LANE), F32)])


def _ev_in_bwd(dq, dkv, dsu, dsv, dg0, x, dxp, mod, w_in, rc, rs1, rs2, comm=None):
    T = x.shape[0]

    def body(dq_ref, dkv_ref, dsu_ref, dsv_ref, dg_ref, x_ref, dxp_ref, mod_ref, w_ref, c_ref, s1_ref, s2_ref,
             dx_ref, dwb_ref, vec_ref, dw_ref):
        i = pl.program_id(0)

        @pl.when(i == 0)
        def _():
            dw_ref[...] = jnp.zeros_like(dw_ref)
            vec_ref[...] = jnp.zeros_like(vec_ref)

        low = lax.broadcasted_iota(jnp.int32, (TM, LANE), 1) < HEAD_DIM

        def fold(j):
            t0 = dkv_ref[:, (2 * j) * LANE:(2 * j + 1) * LANE]
            t1 = dkv_ref[:, (2 * j + 1) * LANE:(2 * j + 2) * LANE]
            return jnp.where(low, t0 + pltpu.roll(t0, HEAD_DIM, 1), t1 + pltpu.roll(t1, HEAD_DIM, 1))

        dk = _rope_bwd(fold(0), c_ref[...], s1_ref[...], s2_ref[...]).astype(BF16)
        dp = jnp.concatenate([dq_ref[...], dk, fold(1).astype(BF16), dsu_ref[...], dsv_ref[...],
                              dg_ref[...]], axis=-1)
        xv = x_ref[...]
        scale0 = 1.0 + mod_ref[1:2, :]
        h0 = (xv * scale0 + mod_ref[0:1, :]).astype(BF16)
        dh0 = _dot_nt(dp, w_ref[...])
        dw_ref[...] += _dot_tn(h0, dp)
        dx_ref[...] = dxp_ref[...] + dh0 * scale0
        vec_ref[0:1, :] += _rowsum(dh0)
        vec_ref[1:2, :] += _rowsum(dh0 * xv)

        @pl.when(i == T // TM - 1)
        def _():
            dwb_ref[...] = dw_ref[...].astype(BF16)

    t = _tile(TM, D)
    return _fused_call(
        body, comm, (dq, dkv, dsu, dsv, dg0, x, dxp, mod, w_in, rc, rs1, rs2), name="ev_in_bwd", grid=(T // TM,),
        in_specs=[_tile(TM, ATTN_W), pl.BlockSpec((TM, 4 * LANE), lambda i: (i + 1, 0)), _tile(TM, SG_W),
                  _tile(TM, SG_W), t, t, t,
                  _full((3, D)), _full((D, EV_IN)), _tile(TM, LANE), _tile(TM, LANE), _tile(TM, LANE)],
        out_specs=[t, _full((D, EV_IN)), _full((SUBLANE, D))],
        out_shape=[jax.ShapeDtypeStruct((T, D), F32), jax.ShapeDtypeStruct((D, EV_IN), BF16),
                   jax.ShapeDtypeStruct((SUBLANE, D), F32)],
        scratch_shapes=[pltpu.VMEM((D, EV_IN), F32)])


def _sum_slots(land_ref):
    g = land_ref[0].astype(F32)
    for i in range(1, land_ref.shape[0]):
        g = g + land_ref[i].astype(F32)
    return g


def _reduce_adam(land, w, m, v, name):
    R, C = w.shape
    rb = R
    for cand in (128, 64, 32, 16, 8):
        if R % cand == 0:
            rb = cand
            break

    def body(l_ref, w_ref, m_ref, v_ref, g_ref, d_ref, nm_ref, nv_ref):
        g = _sum_slots(l_ref)
        g_ref[...] = g
        dlt, m2, v2 = _adam(w_ref[...], g, m_ref[...], v_ref[...])
        d_ref[...] = dlt
        nm_ref[...] = m2
        nv_ref[...] = v2

    t = pl.BlockSpec((rb, C), lambda i: (i, 0))
    shp = jax.ShapeDtypeStruct((R, C), F32)
    return _pallas(
        body, name=name, grid=(R // rb,),
        in_specs=[pl.BlockSpec((land.shape[0], rb, C), lambda i: (0, i, 0)), t, t, t],
        out_specs=[t] * 4, out_shape=[shp] * 4,
        compiler_params=_params(("parallel",)),
    )(land, w, m, v)


def _tail_exchange(slabs, small):
    _, R, C = slabs.shape
    n_chips = N_DEV // 2
    gather = _GatherComm(small)
    ns = gather.n

    def body(*refs):
        slab_ref = refs[0]
        g_ins = refs[1:1 + ns]
        land_ref = refs[1 + ns]
        g_outs = refs[2 + ns:2 + 2 * ns]
        stage, part, s1_send, s1_recv, s2_send, s2_recv = refs[2 + 2 * ns:8 + 2 * ns]
        g_sems = refs[8 + 2 * ns:]
        x, y, c = _my_pos()
        chip = 2 * x + y
        gather.start(g_ins, g_outs, g_sems)

        swaps = [pltpu.make_async_remote_copy(
            src_ref=slab_ref.at[2 * k + (1 - c)], dst_ref=stage.at[k], send_sem=s1_send.at[k],
            recv_sem=s1_recv.at[k], device_id=(x, y, 1 - c), device_id_type=MESH) for k in range(n_chips)]
        for cp in swaps:
            cp.start()
        for cp in swaps:
            cp.wait()
        for k in range(n_chips):
            part[k] = (slab_ref[2 * k + c].astype(F32) + stage[k].astype(F32)).astype(BF16)

        gather.mid(g_ins, g_outs, g_sems)

        sends = []
        for r in range(1, n_chips):
            px = (1 - x) if (r & 2) else x
            py = (1 - y) if (r & 1) else y
            sends.append(pltpu.make_async_remote_copy(
                src_ref=part.at[2 * px + py], dst_ref=land_ref.at[chip], send_sem=s2_send.at[r - 1],
                recv_sem=s2_recv.at[r - 1], device_id=(px, py, c), device_id_type=MESH))
        for cp in sends:
            cp.start()
        land_ref[chip] = part[chip]
        for cp in sends:
            cp.wait()
        gather.finish(g_ins, g_outs, g_sems)

    any_spec = pl.BlockSpec(memory_space=pl.ANY)
    vmem_spec = pl.BlockSpec(memory_space=pltpu.VMEM)
    res = _pallas(
        body, name="tail_exchange",
        out_shape=[jax.ShapeDtypeStruct((n_chips, R, C), BF16)] + gather.out_shapes(),
        in_specs=[vmem_spec] + [any_spec] * ns, out_specs=[vmem_spec] + [any_spec] * ns,
        scratch_shapes=[pltpu.VMEM((n_chips, R, C), BF16), pltpu.VMEM((n_chips, R, C), BF16),
                        pltpu.SemaphoreType.DMA((n_chips,)), pltpu.SemaphoreType.DMA((n_chips,)),
                        pltpu.SemaphoreType.DMA((n_chips - 1,)), pltpu.SemaphoreType.DMA((n_chips - 1,))]
        + gather.sems(),
        compiler_params=pltpu.CompilerParams(vmem_limit_bytes=VMEM_LIMIT),
    )(slabs, *gather.arrs)
    return res[0], list(res[1:])


def _slots_adam(land, w, m, v, name):
    lead = w.shape[1] if w.ndim == 5 else 1
    inner = w.shape[-3:]
    zeros3 = (0, 0, 0)
    if w.ndim == 5:
        lspec = pl.BlockSpec((N_DEV, 1) + inner, lambda i: (0, i) + zeros3)
        wspec = pl.BlockSpec((1, 1) + inner, lambda i: (0, i) + zeros3)
    else:
        lspec = pl.BlockSpec((N_DEV,) + inner, lambda i: (0,) + zeros3)
        wspec = pl.BlockSpec((1,) + inner, lambda i: (0,) + zeros3)

    def body(l_ref, w_ref, m_ref, v_ref, g_ref, d_ref, nm_ref, nv_ref):
        at = (0, 0) if w.ndim == 5 else (0,)
        g = l_ref[(0,) + at[1:]].astype(F32)
        for i in range(1, N_DEV):
            g = g + l_ref[(i,) + at[1:]].astype(F32)
        dlt, m2, v2 = _adam(w_ref[at], g, m_ref[at], v_ref[at])
        g_ref[at] = g
        d_ref[at] = dlt
        nm_ref[at] = m2
        nv_ref[at] = v2

    shp = jax.ShapeDtypeStruct(w.shape, F32)
    return _pallas(
        body, name=name, grid=(lead,),
        in_specs=[lspec, wspec, wspec, wspec], out_specs=[wspec] * 4, out_shape=[shp] * 4,
        compiler_params=_params(("parallel",)),
    )(land, w, m, v)


SMALL_PARAMS = ("ln_g", "ln_b", "ev_sg_ln_g", "ev_sg_ln_b", "ev_sink", "ev_sg_b",
                "od_conv_w", "od_conv_b", "od_b_a", "od_b_x", "od_lam")


def _small_update(ga, gc, gd, gf, gb, ge, gsink, gbt, params):
    names = list(SMALL_PARAMS)
    flat = [a for nm in names for a in params[nm]]
    n_g = 8

    def body(*refs):
        ga_ref, gc_ref, gd_ref, gf_ref, gb_ref, ge_ref, gs_ref, gbt_ref = refs[:n_g]
        prm = refs[n_g:n_g + 3 * len(names)]
        loss_ref = refs[n_g + 3 * len(names)]
        outs = refs[n_g + 3 * len(names) + 1:]

        def ssum(ref):
            acc = ref[0]
            for i in range(1, N_DEV):
                acc = acc + ref[i]
            return acc

        a, cc, dd, ff, bb, ee = ssum(ga_ref), ssum(gc_ref), ssum(gd_ref), ssum(gf_ref), ssum(gb_ref), ssum(ge_ref)
        loss_ref[...] = a[3:4, 0:LANE]
        me = _slot(*_my_pos())

        def mine(rows):
            acc = jnp.zeros((rows.shape[0], LANE), F32)
            for j in range(N_DEV):
                acc = acc + jnp.where(me == j, rows[:, j * LANE:(j + 1) * LANE], 0.0)
            return acc

        sink_terms = ssum(gs_ref)
        lane8 = lax.broadcasted_iota(jnp.int32, (1, N_HEADS), 1)
        g_sink = jnp.zeros((1, N_HEADS), F32)
        for h in range(N_HEADS):
            tot = -jnp.sum(sink_terms[:, h * LANE:(h + 1) * LANE], axis=1, keepdims=True)
            g_sink = jnp.where(lane8 == h, tot, g_sink)
        grads = dict(
            ln_g=jnp.concatenate([dd[0:1], a[0:1]], axis=0), ln_b=jnp.concatenate([dd[1:2], a[1:2]], axis=0),
            ev_sg_ln_g=ee[0:1], ev_sg_ln_b=ee[1:2], ev_sink=g_sink,
            ev_sg_b=jnp.transpose(ssum(gbt_ref))[0:SG_GROUPS, :],
            od_conv_w=mine(cc[0:4]), od_conv_b=mine(cc[4:5]),
            od_b_a=mine(jnp.concatenate([ff[0:1], bb[0:1]], axis=0)),
            od_b_x=mine(jnp.concatenate([ff[1:2], bb[1:2]], axis=0)),
            od_lam=mine(jnp.concatenate([ff[2:3], bb[2:3]], axis=0)))
        for k, nm in enumerate(names):
            w_ref, m_ref, v_ref = prm[3 * k:3 * k + 3]
            at = (0,) if len(w_ref.shape) == 3 else ()
            g = grads[nm]
            dlt, m2, v2 = _adam(w_ref[at] if at else w_ref[...], g, m_ref[at] if at else m_ref[...],
                                v_ref[at] if at else v_ref[...])
            for o_ref, val in zip(outs[4 * k:4 * k + 4], (g, dlt, m2, v2)):
                if at:
                    o_ref[at] = val
                else:
                    o_ref[...] = val

    gathered = [ga, gc, gd, gf, gb, ge, gsink, gbt]
    out_shape = [jax.ShapeDtypeStruct((1, LANE), F32)]
    for nm in names:
        out_shape += [jax.ShapeDtypeStruct(params[nm][0].shape, F32)] * 4
    return _pallas(
        body, name="small_update", grid=(1,),
        in_specs=[_full(a.shape) for a in gathered + flat],
        out_specs=[_full(s.shape) for s in out_shape], out_shape=out_shape,
        compiler_params=_params(("arbitrary",)),
    )(*gathered, *flat)


VEC_ROWS = 16
VEC_LAYOUT = (("od_conv_w", 4), ("od_conv_b", 1), ("od_b_a", 2), ("od_b_x", 2), ("od_lam", 2))


def _pack_vec(parts):
    rows = [parts[name].reshape(nrows, -1) for name, nrows in VEC_LAYOUT]
    used = sum(r for _, r in VEC_LAYOUT)
    rows.append(jnp.zeros((VEC_ROWS - used, rows[0].shape[1]), F32))
    return jnp.concatenate(rows, axis=0)


def _to_slabs(full, cols_per):
    R = full.shape[0]
    return full.reshape(R, N_DEV, cols_per).transpose(1, 0, 2)


def _from_slabs(slabs):
    n, R, cp = slabs.shape
    return slabs.transpose(1, 0, 2).reshape(R, n * cp)


def kernel(x, c, positions, ada_w, ada_b, ln_g, ln_b, ev_w_in, ev_w_out, ev_sink, ev_sg_ln_g, ev_sg_ln_b, ev_sg_w, ev_sg_b, od_w_in, od_conv_w, od_conv_b, od_w_a, od_b_a, od_w_x, od_b_x, od_lam, od_w_out, loss_target, m_ada_w, m_ada_b, m_ln_g, m_ln_b, m_ev_w_in, m_ev_w_out, m_ev_sink, m_ev_sg_ln_g, m_ev_sg_ln_b, m_ev_sg_w, m_ev_sg_b, m_od_w_in, m_od_conv_w, m_od_conv_b, m_od_w_a, m_od_b_a, m_od_w_x, m_od_b_x, m_od_lam, m_od_w_out, v_ada_w, v_ada_b, v_ln_g, v_ln_b, v_ev_w_in, v_ev_w_out, v_ev_sink, v_ev_sg_ln_g, v_ev_sg_ln_b, v_ev_sg_w, v_ev_sg_b, v_od_w_in, v_od_conv_w, v_od_conv_b, v_od_w_a, v_od_b_a, v_od_w_x, v_od_b_x, v_od_lam, v_od_w_out):
    T = x.shape[1]
    me = _slot(*_my_pos())
    xs = x.reshape(T, D)
    tgt = loss_target.reshape(T, D)

    vec_w = _pack_vec(dict(od_conv_w=od_conv_w[0], od_conv_b=od_conv_b, od_b_a=od_b_a[0], od_b_x=od_b_x[0],
                           od_lam=od_lam[0]))
    c_all, g_ev_in, g_vec = _all_gather([c, ev_w_in[0].astype(BF16), vec_w], "ag_params")
    c_all = c_all.reshape(N_DEV, D)
    w_ev_in = _from_slabs(g_ev_in)
    vec_full = _from_slabs(g_vec)
    cw, cb = vec_full[0:4], vec_full[4:5]
    ba, bx, lam = vec_full[5:7], vec_full[7:9], vec_full[9:11]

    mod_part = _mod_part(c_all, ada_w)
    (mod_all,) = _all_gather([mod_part], "ag_mod")
    mod_mine = lax.dynamic_index_in_dim(mod_all, me, axis=2, keepdims=False)
    mod = mod_mine.transpose(1, 0, 2).reshape(2, 3 * D) + ada_b
    mod0 = mod[0].reshape(3, D)
    mod1 = mod[1].reshape(3, D)

    half = 8
    inv_freq = jnp.power(jnp.float32(ROPE_THETA), -jnp.arange(half, dtype=F32) / half)
    ang = positions.reshape(T).astype(F32)[:, None] * inv_freq
    cos_t = jnp.tile(jnp.cos(ang), (1, LANE // half))
    sin_t = jnp.tile(jnp.sin(ang), (1, LANE // half))
    l64 = jnp.arange(LANE) % HEAD_DIM
    rc = jnp.where(l64 < 2 * half, cos_t, 1.0)
    rs1 = jnp.where(l64 < half, -sin_t, 0.0)
    rs2 = jnp.where((l64 >= half) & (l64 < 2 * half), sin_t, 0.0)

    ln0 = jnp.stack([ln_g[0], ln_b[0]])
    ln1 = jnp.stack([ln_g[1], ln_b[1]])
    sg_lng = ev_sg_ln_g
    sg_lnb = ev_sg_ln_b
    sg_w = ev_sg_w[0].astype(BF16)
    sg_bfull = jnp.repeat(ev_sg_b[0].T, SG_DIM, axis=1)
    sink_l = jnp.repeat(ev_sink, LANE, axis=1)
    kj = jnp.arange(3 * BLK)[:, None]
    qi = jnp.arange(BLK)[None, :]
    band_bias = jnp.where(jnp.abs(kj - BLK - qi) <= BLK, 0.0, NEG_INF).astype(F32)
    lanes = jnp.arange(LANE)
    a128 = jnp.where(lanes[:, None] // SG_DIM == lanes[None, :] // SG_DIM, 1.0 / SG_DIM, 0.0).astype(BF16)
    gsum = (jnp.arange(SG_W)[:, None] // SG_DIM == lanes[None, :]).astype(BF16)
    sel = (jnp.arange(SUBLANE)[:, None] == lanes[None, :] // HEAD_DIM).astype(BF16)
    wa = od_w_a[0].astype(BF16)
    wx = od_w_x[0].astype(BF16)

    (q, kvx, su, sv, g0), (g_ev_out,) = _ev_in(xs, mod0, w_ev_in, rc, rs1, rs2,
                                               _GatherComm([ev_w_out[0].astype(BF16)]))
    w_ev_out = g_ev_out.reshape(D, D)
    (ycat, y0, lse), (g_od_in, g_od_out) = _mix0_fwd(
        q, kvx, su, sv, g0, sink_l, band_bias, a128, sg_lng, sg_lnb, sg_w, sg_bfull,
        _GatherComm([od_w_in[0].astype(BF16), od_w_out[0].astype(BF16)]))
    w_od_in = _from_slabs(g_od_in)
    w_od_out = g_od_out.reshape(D, D)
    out0, z0, x1 = _ev_out(y0, w_ev_out, xs, mod0, ln0)
    xr, g1 = _od_in(x1, mod1, w_od_in)
    hf = _rglru_fwd(xr, cw, cb, wa[0], wx[0], ba[0:1], bx[0:1], lam[0:1], False, "rglru_fwd_f")
    hb = _rglru_fwd(xr, cw, cb, wa[1], wx[1], ba[1:2], bx[1:2], lam[1:2], True, "rglru_fwd_b")
    dh, dg1, dx1p, d_od_out, vec_a = _od_out(hf, hb, g1, w_od_out, x1, tgt, mod1, ln1)

    (dxcf, dwa_f, dwx_f, vec_f), (l_od_out,) = _rglru_bwd(
        xr, dh, hf, cw, cb, wa[0], wx[0], ba[0:1], bx[0:1], lam[0:1], False, "rglru_bwd_f",
        _ExchangeComm([d_od_out.reshape(N_DEV, D // N_DEV, D)]))
    (dxcb, dwa_b, dwx_b, vec_b), _ = _rglru_bwd(xr, dh, hb, cw, cb, wa[1], wx[1], ba[1:2], bx[1:2], lam[1:2],
                                                True, "rglru_bwd_b")
    (dx1, d_od_in, vec_c), (a_wa, a_wx) = _od_in_bwd(
        dxcf, dxcb, xr, dg1, x1, dx1p, mod1, w_od_in, cw,
        _GatherComm([jnp.stack([dwa_f, dwa_b]).astype(BF16), jnp.stack([dwx_f, dwx_b]).astype(BF16)],
                    mid_frac=0.75))
    dxp, dyc, dg0, d_ev_out, vec_d = _ev_out_bwd(dx1, z0, out0, y0, ycat, g0, w_ev_out, mod0, ln0)
    (dq, dkv, dsu, dsv, d_sg_w, d_sg_bt, vec_e, d_sink_l), (l_od_in, l_ev_out) = _mix0_bwd(
        q, kvx, lse, dyc, ycat, su, sv, sink_l, band_bias, a128, gsum, sel, sg_lng, sg_lnb, sg_w, sg_bfull,
        rc, rs1, rs2, _ExchangeComm([d_od_in, d_ev_out.reshape(N_DEV, D // N_DEV, D)]))
    (grad_x, d_ev_in, vec_g), _ = _ev_in_bwd(dq, dkv, dsu, dsv, dg0, xs, dxp, mod0, w_ev_in, rc, rs1, rs2)

    l_ev_in, (ga, gc, gd, gf, gb, gg, ge, gsink, gbt, a_sgw) = _tail_exchange(
        _to_slabs(d_ev_in, EV_IN // N_DEV),
        [vec_a, vec_c, vec_d, vec_f, vec_b, vec_g, vec_e, d_sink_l, d_sg_bt, d_sg_w.astype(BF16)])

    dmod_all = jnp.stack([jnp.concatenate([gg[:, 0], gg[:, 1], gd[:, 2]], axis=-1),
                          jnp.concatenate([gc[:, 5], gc[:, 6], ga[:, 2]], axis=-1)], axis=1)
    cols = ada_w.shape[2]
    dmod_cols = lax.dynamic_slice_in_dim(dmod_all, me * cols, cols, axis=2).transpose(1, 0, 2)
    (g_ada_w, d_ada_w, nm_ada_w, nv_ada_w, g_ada_b, d_ada_b, nm_ada_b, nv_ada_b) = _ada_update(
        c_all, dmod_cols, dmod_all, ada_w, m_ada_w, v_ada_w, ada_b, m_ada_b, v_ada_b)

    res = dict(ada_w=[g_ada_w, d_ada_w, nm_ada_w, nv_ada_w], ada_b=[g_ada_b, d_ada_b, nm_ada_b, nv_ada_b])
    for name, land, w, m, v in (("ev_w_in", l_ev_in, ev_w_in, m_ev_w_in, v_ev_w_in),
                                ("ev_w_out", l_ev_out, ev_w_out, m_ev_w_out, v_ev_w_out),
                                ("od_w_in", l_od_in, od_w_in, m_od_w_in, v_od_w_in),
                                ("od_w_out", l_od_out, od_w_out, m_od_w_out, v_od_w_out)):
        res[name] = [a[None] for a in _reduce_adam(land, w[0], m[0], v[0], "adam_" + name)]
    res["od_w_a"] = _slots_adam(a_wa, od_w_a, m_od_w_a, v_od_w_a, "adam_od_w_a")
    res["od_w_x"] = _slots_adam(a_wx, od_w_x, m_od_w_x, v_od_w_x, "adam_od_w_x")
    res["ev_sg_w"] = _slots_adam(a_sgw, ev_sg_w, m_ev_sg_w, v_ev_sg_w, "adam_ev_sg_w")
    small = dict(ln_g=(ln_g, m_ln_g, v_ln_g), ln_b=(ln_b, m_ln_b, v_ln_b),
                 ev_sg_ln_g=(ev_sg_ln_g, m_ev_sg_ln_g, v_ev_sg_ln_g),
                 ev_sg_ln_b=(ev_sg_ln_b, m_ev_sg_ln_b, v_ev_sg_ln_b),
                 ev_sink=(ev_sink, m_ev_sink, v_ev_sink), ev_sg_b=(ev_sg_b, m_ev_sg_b, v_ev_sg_b),
                 od_conv_w=(od_conv_w, m_od_conv_w, v_od_conv_w), od_conv_b=(od_conv_b, m_od_conv_b, v_od_conv_b),
                 od_b_a=(od_b_a, m_od_b_a, v_od_b_a), od_b_x=(od_b_x, m_od_b_x, v_od_b_x),
                 od_lam=(od_lam, m_od_lam, v_od_lam))
    small_out = _small_update(ga, gc, gd, gf, gb, ge, gsink, gbt, small)
    loss = small_out[0][0, 0]
    for k, name in enumerate(SMALL_PARAMS):
        res[name] = small_out[1 + 4 * k:5 + 4 * k]

    order = ["ada_w", "ada_b", "ln_g", "ln_b", "ev_w_in", "ev_w_out", "ev_sink", "ev_sg_ln_g", "ev_sg_ln_b",
             "ev_sg_w", "ev_sg_b", "od_w_in", "od_conv_w", "od_conv_b", "od_w_a", "od_b_a", "od_w_x", "od_b_x",
             "od_lam", "od_w_out"]
    outs = [loss, grad_x.reshape(1, T, D)]
    for kind in range(4):
        outs += [res[name][kind] for name in order]
    return tuple(outs)
```

```python
import functools

import jax
import jax.numpy as jnp
from jax import lax
from jax.experimental import pallas as pl
from jax.experimental.pallas import tpu as pltpu

F32 = jnp.float32
BF16 = jnp.bfloat16

N_DEV = 8
D = 1024
N_HEADS = 8
HEAD_DIM = 64
KV_WIDTH = 128
ATTN_W = 512
SG_W = 512
SG_GROUPS = 8
SG_DIM = 64
BLK = 128
KVX_W = 1024
EV_IN = 2816
OD_IN = 2048
RNN_HEADS = 8
RNN_HD = 128
ALPHA = 4.0 ** 0.25
LN_EPS = 1e-5
NEG_INF = -1e30
RG_C = 8.0
ROPE_THETA = 500000.0
LR, B1, B2, EPS, WD, STEP = 0.001, 0.9, 0.999, 1e-08, 0.01, 10

LANE = 128
SUBLANE = 8
TM = 256
TMF = 512
TS = 256
VMEM_LIMIT = 56 * 1024 * 1024

MESH = pl.DeviceIdType.MESH


def _pallas(body, **kw):
    return pl.pallas_call(body, **kw)


def _params(sem, vmem=VMEM_LIMIT):
    return pltpu.CompilerParams(dimension_semantics=sem, vmem_limit_bytes=vmem)


def _sigmoid(x):
    return 0.5 * jnp.tanh(0.5 * x) + 0.5


def _silu_and_grad(x):
    s = _sigmoid(x)
    return x * s, s * (1.0 + x * (1.0 - s))


def _dot(a, b):
    return jnp.dot(a.astype(BF16), b.astype(BF16), preferred_element_type=F32)


def _dot_nt(a, b):
    return lax.dot_general(a.astype(BF16), b.astype(BF16), (((1,), (1,)), ((), ())), preferred_element_type=F32)


def _dot_tn(a, b):
    return lax.dot_general(a.astype(BF16), b.astype(BF16), (((0,), (0,)), ((), ())), preferred_element_type=F32)


def _ln_fwd(z, g, b):
    mu = jnp.mean(z, axis=-1, keepdims=True)
    zc = z - mu
    var = jnp.mean(zc * zc, axis=-1, keepdims=True)
    rstd = lax.rsqrt(var + LN_EPS)
    xhat = zc * rstd
    return xhat * g + b, xhat, rstd


def _ln_bwd(dy, xhat, rstd, g):
    dxh = dy * g
    m1 = jnp.mean(dxh, axis=-1, keepdims=True)
    m2 = jnp.mean(dxh * xhat, axis=-1, keepdims=True)
    return rstd * (dxh - m1 - xhat * m2)


def _rowsum(v):
    return jnp.sum(v, axis=0, keepdims=True)


def _rope_fwd(t, c, s1, s2):
    return t * c + pltpu.roll(t, LANE - 8, 1) * s1 + pltpu.roll(t, 8, 1) * s2


def _rope_bwd(d, c, s1, s2):
    return d * c + pltpu.roll(d * s1, 8, 1) + pltpu.roll(d * s2, LANE - 8, 1)


def _adam(w, g, m, v):
    m2 = B1 * m + (1.0 - B1) * g
    v2 = B2 * v + (1.0 - B2) * (g * g)
    m_hat = m2 / (1.0 - B1 ** STEP)
    v_hat = v2 / (1.0 - B2 ** STEP)
    delta = -LR * (m_hat / (jnp.sqrt(v_hat) + EPS) + WD * w)
    return delta, m2, v2


def _tile(rows, width):
    return pl.BlockSpec((rows, width), lambda i: (i, 0))


def _full(shape):
    zeros = (0,) * len(shape)
    return pl.BlockSpec(shape, lambda i: zeros)


def _rev_tile(rows, width, n, reverse):
    if reverse:
        return pl.BlockSpec((rows, width), lambda i: (n - 1 - i, 0))
    return pl.BlockSpec((rows, width), lambda i: (i, 0))


def _halo_specs(rows, width, n, total_rows, reverse):
    per = rows // SUBLANE
    last = total_rows // SUBLANE - 1

    def tile_of(i):
        return (n - 1 - i) if reverse else i

    prev = pl.BlockSpec((SUBLANE, width), lambda i: (jnp.maximum(tile_of(i) * per - 1, 0), 0))
    nxt = pl.BlockSpec((SUBLANE, width), lambda i: (jnp.minimum((tile_of(i) + 1) * per, last), 0))
    return prev, nxt


def _my_pos():
    return lax.axis_index("x"), lax.axis_index("y"), lax.axis_index("c")


def _slot(px, py, pc):
    return 4 * px + 2 * py + pc


def _all_gather(arrs, name):
    n = len(arrs)

    def body(*refs):
        ins, outs = refs[:n], refs[n:2 * n]
        send_sems, recv_sems, local_sems = refs[2 * n:]
        x, y, c = _my_pos()
        me, sibling = (x, y, c), (x, y, 1 - c)
        chips = [(1 - x, y), (x, 1 - y), (1 - x, 1 - y)]

        def copy(a, k, block, to, src=None):
            dst = outs[a].at[_slot(*block)]
            return pltpu.make_async_remote_copy(
                src_ref=dst if src is None else src, dst_ref=dst,
                send_sem=send_sems.at[a * 7 + k], recv_sem=recv_sems.at[a * 7 + k],
                device_id=to, device_id_type=MESH)

        local, first = [], []
        for a in range(n):
            lc = pltpu.make_async_copy(ins[a], outs[a].at[_slot(*me)], local_sems.at[a])
            lc.start()
            local.append(lc)
            first.append(copy(a, 0, me, sibling, src=ins[a]))
            first += [copy(a, 1 + j, me, (*chip, c), src=ins[a]) for j, chip in enumerate(chips)]
        for cp in first:
            cp.start()
        passed = []
        for j, chip in enumerate(chips):
            for a in range(n):
                copy(a, 1 + j, (*chip, c), me).wait_recv()
                fw = copy(a, 4 + j, (*chip, c), sibling)
                fw.start()
                passed.append(fw)
        for a in range(n):
            copy(a, 0, sibling, me).wait_recv()
            for j, chip in enumerate(chips):
                copy(a, 4 + j, (*chip, 1 - c), me).wait_recv()
        for cp in first + passed:
            cp.wait_send()
        for lc in local:
            lc.wait()

    any_spec = pl.BlockSpec(memory_space=pl.ANY)
    return _pallas(
        body, name=name,
        out_shape=[jax.ShapeDtypeStruct((N_DEV,) + a.shape, a.dtype) for a in arrs],
        in_specs=[any_spec] * n, out_specs=[any_spec] * n,
        scratch_shapes=[pltpu.SemaphoreType.DMA((7 * n,)), pltpu.SemaphoreType.DMA((7 * n,)),
                        pltpu.SemaphoreType.DMA((n,))],
    )(*arrs)


class _GatherComm:
    has_mid = True

    def __init__(self, arrs, mid_frac=0.5):
        self.arrs = list(arrs)
        self.n = len(self.arrs)
        self.mid_frac = mid_frac

    def out_shapes(self):
        return [jax.ShapeDtypeStruct((N_DEV,) + a.shape, a.dtype) for a in self.arrs]

    def sems(self):
        return [pltpu.SemaphoreType.DMA((7 * self.n,)), pltpu.SemaphoreType.DMA((7 * self.n,)),
                pltpu.SemaphoreType.DMA((self.n,))]

    def _parts(self, ins, outs, sems):
        send_sems, recv_sems, local_sems = sems
        x, y, c = _my_pos()
        me, sibling = (x, y, c), (x, y, 1 - c)
        chips = [(1 - x, y), (x, 1 - y), (1 - x, 1 - y)]

        def copy(a, k, block, to, src=None):
            dst = outs[a].at[_slot(*block)]
            return pltpu.make_async_remote_copy(
                src_ref=dst if src is None else src, dst_ref=dst,
                send_sem=send_sems.at[a * 7 + k], recv_sem=recv_sems.at[a * 7 + k],
                device_id=to, device_id_type=MESH)

        local = [pltpu.make_async_copy(ins[a], outs[a].at[_slot(*me)], local_sems.at[a]) for a in range(self.n)]
        first = []
        for a in range(self.n):
            first.append(copy(a, 0, me, sibling, src=ins[a]))
            first += [copy(a, 1 + j, me, (*chip, c), src=ins[a]) for j, chip in enumerate(chips)]
        ici_in = [copy(a, 1 + j, (*chip, c), me) for j, chip in enumerate(chips) for a in range(self.n)]
        passed = [copy(a, 4 + j, (*chip, c), sibling) for j, chip in enumerate(chips) for a in range(self.n)]
        d2d_in = []
        for a in range(self.n):
            d2d_in.append(copy(a, 0, sibling, me))
            d2d_in += [copy(a, 4 + j, (*chip, 1 - c), me) for j, chip in enumerate(chips)]
        return local, first, ici_in, passed, d2d_in

    def start(self, ins, outs, sems):
        local, first, _, _, _ = self._parts(ins, outs, sems)
        for cp in local + first:
            cp.start()

    def mid(self, ins, outs, sems):
        _, _, ici_in, passed, _ = self._parts(ins, outs, sems)
        for arrived, fw in zip(ici_in, passed):
            arrived.wait_recv()
            fw.start()

    def finish(self, ins, outs, sems):
        local, first, _, passed, d2d_in = self._parts(ins, outs, sems)
        for cp in d2d_in:
            cp.wait_recv()
        for cp in first + passed:
            cp.wait_send()
        for cp in local:
            cp.wait()


class _ExchangeComm:
    has_mid = False

    def __init__(self, arrs):
        self.arrs = list(arrs)
        self.n = len(self.arrs)

    def out_shapes(self):
        return [jax.ShapeDtypeStruct(a.shape, a.dtype) for a in self.arrs]

    def sems(self):
        return [pltpu.SemaphoreType.DMA((7 * self.n,)), pltpu.SemaphoreType.DMA((7 * self.n,)),
                pltpu.SemaphoreType.DMA((self.n,))]

    def _copies(self, ins, outs, sems):
        send_sems, recv_sems, local_sems = sems
        x, y, c = _my_pos()
        mine = _slot(x, y, c)
        copies = [pltpu.make_async_copy(ins[a].at[mine], outs[a].at[mine], local_sems.at[a]) for a in range(self.n)]
        for k in range(1, N_DEV):
            px = (1 - x) if (k & 4) else x
            py = (1 - y) if (k & 2) else y
            pc = (1 - c) if (k & 1) else c
            for a in range(self.n):
                copies.append(pltpu.make_async_remote_copy(
                    src_ref=ins[a].at[_slot(px, py, pc)], dst_ref=outs[a].at[mine],
                    send_sem=send_sems.at[a * 7 + k - 1], recv_sem=recv_sems.at[a * 7 + k - 1],
                    device_id=(px, py, pc), device_id_type=MESH))
        return copies

    def start(self, ins, outs, sems):
        for cp in self._copies(ins, outs, sems):
            cp.start()

    def finish(self, ins, outs, sems):
        for cp in self._copies(ins, outs, sems):
            cp.wait()


def _fused_call(body, comm, operands, *, name, grid, in_specs, out_specs, out_shape, scratch_shapes=(),
                semantics=("arbitrary",)):
    n_in, n_out, n_scr = len(in_specs), len(out_specs), len(scratch_shapes)
    if comm is None:
        res = _pallas(body, name=name, grid=grid, in_specs=list(in_specs), out_specs=list(out_specs),
                      out_shape=list(out_shape), scratch_shapes=list(scratch_shapes),
                      compiler_params=_params(semantics))(*operands)
        return list(res), []
    k = comm.n
    steps = grid[0]

    def wrapped(*refs):
        ins, cins = refs[:n_in], refs[n_in:n_in + k]
        outs = refs[n_in + k:n_in + k + n_out]
        couts = refs[n_in + k + n_out:n_in + 2 * k + n_out]
        rest = refs[n_in + 2 * k + n_out:]
        scratch, sems = rest[:n_scr], rest[n_scr:]
        i = pl.program_id(0)

        @pl.when(i == 0)
        def _():
            comm.start(cins, couts, sems)

        body(*ins, *outs, *scratch)

        if comm.has_mid:
            @pl.when(i == int(steps * comm.mid_frac))
            def _():
                comm.mid(cins, couts, sems)

        @pl.when(i == steps - 1)
        def _():
            comm.finish(cins, couts, sems)

    any_spec = pl.BlockSpec(memory_space=pl.ANY)
    res = _pallas(wrapped, name=name, grid=grid, in_specs=list(in_specs) + [any_spec] * k,
                  out_specs=list(out_specs) + [any_spec] * k, out_shape=list(out_shape) + comm.out_shapes(),
                  scratch_shapes=list(scratch_shapes) + comm.sems(),
                  compiler_params=_params(("arbitrary",)))(*operands, *comm.arrs)
    return list(res[:n_out]), list(res[n_out:])


def _mod_part(c_all, ada_w):
    cols = ada_w.shape[2]

    def body(c_ref, w_ref, o_ref):
        cv = c_ref[...]
        cond = cv * _sigmoid(cv)
        for l in range(2):
            o_ref[l] = _dot(cond, w_ref[l])

    return _pallas(
        body, name="mod_part", grid=(1,),
        in_specs=[_full((N_DEV, D)), _full((2, D, cols))],
        out_specs=_full((2, N_DEV, cols)),
        out_shape=jax.ShapeDtypeStruct((2, N_DEV, cols), F32),
        compiler_params=_params(("arbitrary",)),
    )(c_all, ada_w)


def _ada_update(c_all, dmod_cols, dmod_all, ada_w, m_w, v_w, ada_b, m_b, v_b):
    cols = ada_w.shape[2]
    nb = ada_b.shape[1]

    def body(c_ref, dmc_ref, dma_ref, w_ref, mw_ref, vw_ref, b_ref, mb_ref, vb_ref,
             gw_ref, dw_ref, nmw_ref, nvw_ref, gb_ref, db_ref, nmb_ref, nvb_ref):
        cv = c_ref[...]
        cond = cv * _sigmoid(cv)
        for l in range(2):
            g = _dot_tn(cond, dmc_ref[l])
            gw_ref[l] = g
            dlt, m2, v2 = _adam(w_ref[l], g, mw_ref[l], vw_ref[l])
            dw_ref[l] = dlt
            nmw_ref[l] = m2
            nvw_ref[l] = v2
        gb = dma_ref[0]
        for i in range(1, N_DEV):
            gb = gb + dma_ref[i]
        gb_ref[...] = gb
        dlt, m2, v2 = _adam(b_ref[...], gb, mb_ref[...], vb_ref[...])
        db_ref[...] = dlt
        nmb_ref[...] = m2
        nvb_ref[...] = v2

    wspec = _full((2, D, cols))
    bspec = _full((2, nb))
    wshape = jax.ShapeDtypeStruct((2, D, cols), F32)
    bshape = jax.ShapeDtypeStruct((2, nb), F32)
    return _pallas(
        body, name="ada_update", grid=(1,),
        in_specs=[_full((N_DEV, D)), _full((2, N_DEV, cols)), _full((N_DEV, 2, nb)),
                  wspec, wspec, wspec, bspec, bspec, bspec],
        out_specs=[wspec] * 4 + [bspec] * 4,
        out_shape=[wshape] * 4 + [bshape] * 4,
        compiler_params=_params(("arbitrary",)),
    )(c_all, dmod_cols, dmod_all, ada_w, m_w, v_w, ada_b, m_b, v_b)


def _ev_in(x, mod, w_in, rc, rs1, rs2, comm=None):
    T = x.shape[0]

    def body(x_ref, mod_ref, w_ref, c_ref, s1_ref, s2_ref, q_ref, kv_ref, su_ref, sv_ref, g_ref):
        h = x_ref[...] * (1.0 + mod_ref[1:2, :]) + mod_ref[0:1, :]
        p = _dot(h, w_ref[...])
        c, s1, s2 = c_ref[...], s1_ref[...], s2_ref[...]
        for j in range(ATTN_W // LANE):
            qr = _rope_fwd(p[:, j * LANE:(j + 1) * LANE], c, s1, s2)
            q_ref[:, j * LANE:(j + 1) * LANE] = (qr * (HEAD_DIM ** -0.5)).astype(BF16)
        low = lax.broadcasted_iota(jnp.int32, (TMF, LANE), 1) < HEAD_DIM
        for j, val in enumerate((_rope_fwd(p[:, 512:640], c, s1, s2), p[:, 640:768])):
            swapped = pltpu.roll(val, HEAD_DIM, 1)
            tiles = (jnp.where(low, val, 0.0), jnp.where(low, 0.0, swapped),
                     jnp.where(low, swapped, 0.0), jnp.where(low, 0.0, val))
            for k, tile in enumerate(tiles):
                kv_ref[:, (4 * j + k) * LANE:(4 * j + k + 1) * LANE] = tile.astype(BF16)
        su_ref[...] = p[:, 768:1280].astype(BF16)
        sv_ref[...] = p[:, 1280:1792].astype(BF16)
        g_ref[...] = p[:, 1792:2816].astype(BF16)

    sh = lambda w: jax.ShapeDtypeStruct((T, w), BF16)
    return _fused_call(
        body, comm, (x, mod, w_in, rc, rs1, rs2), name="ev_in", grid=(T // TMF,),
        in_specs=[_tile(TMF, D), _full((3, D)), _full((D, EV_IN)), _tile(TMF, LANE), _tile(TMF, LANE),
                  _tile(TMF, LANE)],
        out_specs=[_tile(TMF, ATTN_W), _tile(TMF, KVX_W), _tile(TMF, SG_W), _tile(TMF, SG_W), _tile(TMF, D)],
        out_shape=[sh(ATTN_W), sh(KVX_W), sh(SG_W), sh(SG_W), sh(D)], semantics=("parallel",))


def _band_specs(width, nb):
    return [pl.BlockSpec((BLK, width), lambda n: (jnp.maximum(n - 1, 0), 0)),
            pl.BlockSpec((BLK, width), lambda n: (n, 0)),
            pl.BlockSpec((BLK, width), lambda n: (jnp.minimum(n + 1, nb - 1), 0))]


def _band_bias(bias_ref, n, nb):
    rows = lax.broadcasted_iota(jnp.int32, (3 * BLK, 1), 0)
    outside = ((rows < BLK) & (n == 0)) | ((rows >= 2 * BLK) & (n == nb - 1))
    return bias_ref[...] + jnp.where(outside, NEG_INF, 0.0)


def _split_bf16(v):
    hi = v.astype(BF16)
    return hi, (v - hi.astype(F32)).astype(BF16)


def _group_mean(v, a_ref, exact_bf16=False):
    hi, lo = _split_bf16(v)
    a = a_ref[...]
    out = []
    for t in range(SG_W // LANE):
        sl = slice(t * LANE, (t + 1) * LANE)
        r = jnp.dot(hi[:, sl], a, preferred_element_type=F32)
        if not exact_bf16:
            r = r + jnp.dot(lo[:, sl], a, preferred_element_type=F32)
        out.append(r)
    return jnp.concatenate(out, axis=-1)


def _sg_core(sv_ref, lng, lnb, a_ref, w_ref, bfull_ref):
    svf = sv_ref[...].astype(F32)
    xc = svf - _group_mean(svf, a_ref, exact_bf16=True)
    rstd = lax.rsqrt(_group_mean(xc * xc, a_ref) + LN_EPS)
    xhat = xc * rstd
    vb = (xhat * lng + lnb).astype(BF16)
    low = lax.broadcasted_iota(jnp.int32, (BLK, LANE), 1) < SG_DIM
    tiles = []
    for t in range(SG_W // LANE):
        v2 = vb[:, t * LANE:(t + 1) * LANE]
        r0 = jnp.dot(w_ref[2 * t], v2, preferred_element_type=F32)
        r1 = jnp.dot(w_ref[2 * t + 1], v2, preferred_element_type=F32)
        tiles.append(jnp.where(low, r0, r1))
    svm = jnp.concatenate(tiles, axis=-1) + bfull_ref[...]
    return xhat, rstd, vb, svm


def _mix0_fwd(q, kvx, su, sv, g0, sink_l, bias, a128, sg_lng, sg_lnb, sg_w, sg_bfull, comm=None):
    T = q.shape[0]
    nb = T // BLK

    def body(q_ref, kp_ref, kc_ref, kn_ref, su_ref, sv_ref, g_ref, sink_ref, bias_ref, a_ref, lng_ref, lnb_ref,
             w_ref, bfull_ref, ycat_ref, y0_ref, lse_ref):
        n = pl.program_id(0)
        bias = _band_bias(bias_ref, n, nb)
        kvx = jnp.concatenate([kp_ref[...], kc_ref[...], kn_ref[...]], axis=0)
        tiles = []
        for t in range(ATTN_W // LANE):
            qt = q_ref[:, t * LANE:(t + 1) * LANE]
            acc = None
            for par in range(2):
                h = 2 * t + par
                kt = 2 * (h // 4) + par
                ke = kvx[:, kt * LANE:(kt + 1) * LANE]
                ve = kvx[:, (4 + kt) * LANE:(5 + kt) * LANE]
                st = _dot_nt(ke, qt) + bias
                sk = sink_ref[:, h * LANE:(h + 1) * LANE]
                m = jnp.maximum(jnp.max(st, axis=0, keepdims=True), sk)
                p = jnp.exp(st - m)
                denom = jnp.sum(p, axis=0, keepdims=True) + jnp.exp(sk - m)
                contrib = _dot_tn(p * (1.0 / denom), ve)
                acc = contrib if acc is None else acc + contrib
                lse_ref[0, :, h * LANE:(h + 1) * LANE] = m + jnp.log(denom)
            tiles.append(acc)
        _, _, _, svm = _sg_core(sv_ref, lng_ref[...], lnb_ref[...], a_ref, w_ref, bfull_ref)
        tiles.append(su_ref[...].astype(F32) * svm)
        ycat = jnp.concatenate(tiles, axis=-1)
        gf = g_ref[...].astype(F32)
        ycat_ref[...] = ycat.astype(BF16)
        y0_ref[...] = (ycat * (gf * _sigmoid(gf))).astype(BF16)

    return _fused_call(
        body, comm, (q, kvx, kvx, kvx, su, sv, g0, sink_l, bias, a128, sg_lng, sg_lnb, sg_w, sg_bfull),
        name="mix0_fwd", grid=(nb,),
        in_specs=[_tile(BLK, ATTN_W)] + _band_specs(KVX_W, nb) + [
            _tile(BLK, SG_W), _tile(BLK, SG_W), _tile(BLK, D), _full((1, N_HEADS * LANE)), _full((3 * BLK, LANE)),
            _full((LANE, LANE)), _full((1, SG_W)), _full((1, SG_W)), _full((SG_GROUPS, BLK, BLK)),
            _full((BLK, SG_W))],
        out_specs=[_tile(BLK, D), _tile(BLK, D), pl.BlockSpec((1, 1, N_HEADS * LANE), lambda n: (n, 0, 0))],
        out_shape=[jax.ShapeDtypeStruct((T, D), BF16), jax.ShapeDtypeStruct((T, D), BF16),
                   jax.ShapeDtypeStruct((nb, 1, N_HEADS * LANE), F32)], semantics=("parallel",))


def _ev_out(y0, w_out, x, mod, lnp):
    T = x.shape[0]

    def body(y_ref, w_ref, x_ref, mod_ref, ln_ref, out_ref, z_ref, x1_ref):
        out = _dot(y_ref[...], w_ref[...])
        z = ALPHA * x_ref[...] + mod_ref[2:3, :] * out
        x1, _, _ = _ln_fwd(z, ln_ref[0:1, :], ln_ref[1:2, :])
        out_ref[...] = out.astype(BF16)
        z_ref[...] = z
        x1_ref[...] = x1

    return _pallas(
        body, name="ev_out", grid=(T // TMF,),
        in_specs=[_tile(TMF, D), _full((D, D)), _tile(TMF, D), _full((3, D)), _full((2, D))],
        out_specs=[_tile(TMF, D)] * 3,
        out_shape=[jax.ShapeDtypeStruct((T, D), BF16), jax.ShapeDtypeStruct((T, D), F32),
                   jax.ShapeDtypeStruct((T, D), F32)],
        compiler_params=_params(("parallel",)),
    )(y0, w_out, x, mod, lnp)


def _od_in(x1, mod, w_in):
    T = x1.shape[0]

    def body(x_ref, mod_ref, w_ref, xr_ref, g_ref):
        h = x_ref[...] * (1.0 + mod_ref[1:2, :]) + mod_ref[0:1, :]
        p = _dot(h, w_ref[...])
        xr_ref[...] = p[:, :D]
        g_ref[...] = p[:, D:].astype(BF16)

    return _pallas(
        body, name="od_in", grid=(T // TMF,),
        in_specs=[_tile(TMF, D), _full((3, D)), _full((D, OD_IN))],
        out_specs=[_tile(TMF, D), _tile(TMF, D)],
        out_shape=[jax.ShapeDtypeStruct((T, D), F32), jax.ShapeDtypeStruct((T, D), BF16)],
        compiler_params=_params(("parallel",)),
    )(x1, mod, w_in)


def _ext_rows(prev_ref, cur, next_ref, j, n):
    prev = jnp.where(j > 0, prev_ref[...], 0.0)
    nxt = jnp.where(j < n - 1, next_ref[...], 0.0)
    return jnp.concatenate([prev, cur, nxt], axis=0)


def _shift_rows(ext, off, rows):
    total = ext.shape[0]
    if off == 0:
        return ext[SUBLANE:SUBLANE + rows, :]
    return pltpu.roll(ext, (-off) % total, 0)[SUBLANE:SUBLANE + rows, :]


def _conv_fwd(ext, cw, cb, rows):
    xc = cb
    for k in range(4):
        xc = xc + cw[k:k + 1, :] * _shift_rows(ext, k - 2, rows)
    return xc


def _gates(xc, wa_ref, wx_ref, ba, bx, lam):
    pr, pi = [], []
    for h in range(RNN_HEADS):
        xh = xc[:, h * RNN_HD:(h + 1) * RNN_HD].astype(BF16)
        pr.append(_dot(xh, wa_ref[h]))
        pi.append(_dot(xh, wx_ref[h]))
    r = _sigmoid(jnp.concatenate(pr, axis=-1) + ba)
    ig = _sigmoid(jnp.concatenate(pi, axis=-1) + bx)
    sp = jnp.maximum(-lam, 0.0) + jnp.log(1.0 + jnp.exp(-jnp.abs(lam)))
    neg_log_a = RG_C * r * sp
    a = jnp.exp(-neg_log_a)
    s2 = (1.0 + a * a) * jnp.tanh(neg_log_a)
    inv_s = lax.rsqrt(jnp.maximum(s2, 1e-30))
    return r, ig, sp, a, s2 * inv_s, inv_s


def _scan_tile(a_ref, b_ref, o_ref, carry_ref, rows, reverse):
    ridx = lax.broadcasted_iota(jnp.int32, (SUBLANE, D), 0)
    groups = rows // SUBLANE

    def group(gi, h):
        g = (groups - 1 - gi) if reverse else gi
        off = pl.multiple_of(g * SUBLANE, SUBLANE)
        a = a_ref[pl.ds(off, SUBLANE), :]
        b = b_ref[pl.ds(off, SUBLANE), :]
        for sh in (1, 2, 4):
            if reverse:
                keep = ridx < SUBLANE - sh
                a_p = jnp.where(keep, pltpu.roll(a, SUBLANE - sh, 0), 1.0)
                b_p = jnp.where(keep, pltpu.roll(b, SUBLANE - sh, 0), 0.0)
            else:
                keep = ridx >= sh
                a_p = jnp.where(keep, pltpu.roll(a, sh, 0), 1.0)
                b_p = jnp.where(keep, pltpu.roll(b, sh, 0), 0.0)
            b = b + a * b_p
            a = a * a_p
        hh = b + a * h
        o_ref[pl.ds(off, SUBLANE), :] = hh
        return hh[0:1, :] if reverse else hh[SUBLANE - 1:SUBLANE, :]

    carry_ref[...] = lax.fori_loop(0, groups, group, carry_ref[...])


def _rglru_fwd(xr, cw, cb, wa, wx, ba, bx, lam, reverse, name):
    T = xr.shape[0]
    n = T // TS
    prev_spec, next_spec = _halo_specs(TS, D, n, T, reverse)

    def body(prev_ref, cur_ref, next_ref, cw_ref, cb_ref, wa_ref, wx_ref, ba_ref, bx_ref, lam_ref,
             h_ref, a_ref, s_ref, r_ref, ig_ref, xc_ref, b_s, carry):
        i = pl.program_id(0)
        j = (n - 1 - i) if reverse else i

        @pl.when(i == 0)
        def _():
            carry[...] = jnp.zeros_like(carry)

        ext = _ext_rows(prev_ref, cur_ref[...], next_ref, j, n)
        xc = _conv_fwd(ext, cw_ref[...], cb_ref[...], TS)
        r, ig, _, a, s, _ = _gates(xc, wa_ref, wx_ref, ba_ref[...], bx_ref[...], lam_ref[...])
        s_ref[...] = s
        r_ref[...] = r.astype(BF16)
        ig_ref[...] = ig.astype(BF16)
        xc_ref[...] = xc.astype(BF16)
        a_ref[...] = a
        b_s[...] = s * ig * xc
        _scan_tile(a_ref, b_s, h_ref, carry, TS, reverse)

    wspec = _full((RNN_HEADS, RNN_HD, RNN_HD))
    cur = _rev_tile(TS, D, n, reverse)
    f32 = jax.ShapeDtypeStruct((T, D), F32)
    b16 = jax.ShapeDtypeStruct((T, D), BF16)
    return _pallas(
        body, name=name, grid=(n,),
        in_specs=[prev_spec, cur, next_spec, _full((4, D)), _full((1, D)),
                  wspec, wspec, _full((1, D)), _full((1, D)), _full((1, D))],
        out_specs=[cur] * 6,
        out_shape=[f32, f32, f32, b16, b16, b16],
        scratch_shapes=[pltpu.VMEM((TS, D), F32), pltpu.VMEM((1, D), F32)],
        compiler_params=_params(("arbitrary",)),
    )(xr, xr, xr, cw, cb, wa, wx, ba, bx, lam)


def _od_out(hf, hb, g1, w_out, x1, tgt, mod, lnp):
    T = x1.shape[0]

    def body(hf_ref, hb_ref, g_ref, w_ref, x_ref, t_ref, mod_ref, ln_ref,
             dh_ref, dg_ref, dx_ref, dwb_ref, vec_ref, dw_ref):
        i = pl.program_id(0)

        @pl.when(i == 0)
        def _():
            dw_ref[...] = jnp.zeros_like(dw_ref)
            vec_ref[...] = jnp.zeros_like(vec_ref)

        hs = hf_ref[...] + hb_ref[...]
        sg, dsg = _silu_and_grad(g_ref[...].astype(F32))
        yr = (hs * sg).astype(BF16)
        w = w_ref[...]
        out = _dot(yr, w)
        gate = mod_ref[2:3, :]
        z = ALPHA * x_ref[...] + gate * out
        lng = ln_ref[0:1, :]
        x2, xhat, rstd = _ln_fwd(z, lng, ln_ref[1:2, :])
        diff = x2 - t_ref[...]
        vec_ref[3:4, 0:LANE] += 0.5 * jnp.sum(diff * diff) * (1.0 / D)
        dx2 = diff * (1.0 / D)
        dz = _ln_bwd(dx2, xhat, rstd, lng)
        vec_ref[0:1, :] += _rowsum(dx2 * xhat)
        vec_ref[1:2, :] += _rowsum(dx2)
        vec_ref[2:3, :] += _rowsum(dz * out)
        dout = (dz * gate).astype(BF16)
        dyr = _dot_nt(dout, w)
        dw_ref[...] += _dot_tn(yr, dout)
        dh_ref[...] = dyr * sg
        dg_ref[...] = (dyr * hs * dsg).astype(BF16)
        dx_ref[...] = ALPHA * dz

        @pl.when(i == T // TM - 1)
        def _():
            dwb_ref[...] = dw_ref[...].astype(BF16)

    return _pallas(
        body, name="od_out", grid=(T // TM,),
        in_specs=[_tile(TM, D), _tile(TM, D), _tile(TM, D), _full((D, D)), _tile(TM, D), _tile(TM, D),
                  _full((3, D)), _full((2, D))],
        out_specs=[_tile(TM, D), _tile(TM, D), _tile(TM, D), _full((D, D)), _full((SUBLANE, D))],
        out_shape=[jax.ShapeDtypeStruct((T, D), F32), jax.ShapeDtypeStruct((T, D), BF16),
                   jax.ShapeDtypeStruct((T, D), F32), jax.ShapeDtypeStruct((D, D), BF16),
                   jax.ShapeDtypeStruct((SUBLANE, D), F32)],
        scratch_shapes=[pltpu.VMEM((D, D), F32)],
        compiler_params=_params(("arbitrary",)),
    )(hf, hb, g1, w_out, x1, tgt, mod, lnp)


def _rglru_bwd(fwd, dh, wa, wx, lam, reverse, name, comm=None):
    h, a_all, s_all, r_all, ig_all, xc_all = fwd
    T = h.shape[0]
    n = T // TS
    adj_rev = not reverse
    hprev_spec, hnext_spec = _halo_specs(TS, D, n, T, adj_rev)
    h_halo_spec = hnext_spec if reverse else hprev_spec

    def body(dh_ref, h_ref, hh_ref, a_ref, s_ref, r_ref, ig_ref, xc_ref, wa_ref, wx_ref, lam_ref,
             dxc_ref, dwa_ref, dwx_ref, vec_ref, a_s, l_s, carry, a_edge):
        i = pl.program_id(0)
        j = (n - 1 - i) if adj_rev else i

        @pl.when(i == 0)
        def _():
            carry[...] = jnp.zeros_like(carry)
            a_edge[...] = jnp.zeros_like(a_edge)
            dwa_ref[...] = jnp.zeros_like(dwa_ref)
            dwx_ref[...] = jnp.zeros_like(dwx_ref)
            vec_ref[...] = jnp.zeros_like(vec_ref)

        lam = lam_ref[...]
        sp = jnp.maximum(-lam, 0.0) + jnp.log(1.0 + jnp.exp(-jnp.abs(lam)))
        a, s = a_ref[...], s_ref[...]
        inv_s = lax.rsqrt(jnp.maximum(s * s, 1e-30))
        r, ig = r_ref[...].astype(F32), ig_ref[...].astype(F32)
        xcb = xc_ref[...]
        xc = xcb.astype(F32)

        rows = lax.broadcasted_iota(jnp.int32, (TS, D), 0)
        hcur = h_ref[...]
        if reverse:
            a_sh = jnp.where(rows == 0, a_edge[...], pltpu.roll(a, 1, 0))
            halo = jnp.where(j < n - 1, hh_ref[0:1, :], 0.0)
            h_nb = jnp.where(rows == TS - 1, halo, pltpu.roll(hcur, TS - 1, 0))
        else:
            a_sh = jnp.where(rows == TS - 1, a_edge[...], pltpu.roll(a, TS - 1, 0))
            halo = jnp.where(j > 0, hh_ref[SUBLANE - 1:SUBLANE, :], 0.0)
            h_nb = jnp.where(rows == 0, halo, pltpu.roll(hcur, 1, 0))
        a_s[...] = a_sh
        _scan_tile(a_s, dh_ref, l_s, carry, TS, adj_rev)
        a_edge[...] = a[TS - 1:TS, :] if reverse else a[0:1, :]

        lm = l_s[...]
        da = lm * h_nb
        di = lm * s * xc
        dxc = lm * s * ig
        ds = lm * ig * xc
        dlog_a = a * (da - ds * a * inv_s)
        dr = (-RG_C) * sp * dlog_a
        dsp = _rowsum((-RG_C) * r * dlog_a)
        dpr = dr * r * (1.0 - r)
        dpi = di * ig * (1.0 - ig)
        vec_ref[0:1, :] += _rowsum(dpr)
        vec_ref[1:2, :] += _rowsum(dpi)
        vec_ref[2:3, :] += dsp * (-_sigmoid(-lam))
        parts = []
        for hd in range(RNN_HEADS):
            sl = slice(hd * RNN_HD, (hd + 1) * RNN_HD)
            xh = xcb[:, sl]
            dprh = dpr[:, sl].astype(BF16)
            dpih = dpi[:, sl].astype(BF16)
            parts.append(_dot_nt(dprh, wa_ref[hd]) + _dot_nt(dpih, wx_ref[hd]))
            dwa_ref[hd] += _dot_tn(xh, dprh)
            dwx_ref[hd] += _dot_tn(xh, dpih)
        dxc_ref[...] = dxc + jnp.concatenate(parts, axis=-1)

    wspec = _full((RNN_HEADS, RNN_HD, RNN_HD))
    cur = _rev_tile(TS, D, n, adj_rev)
    return _fused_call(
        body, comm, (dh, h, h, a_all, s_all, r_all, ig_all, xc_all, wa, wx, lam), name=name, grid=(n,),
        in_specs=[cur, cur, h_halo_spec, cur, cur, cur, cur, cur, wspec, wspec, _full((1, D))],
        out_specs=[cur, wspec, wspec, _full((SUBLANE, D))],
        out_shape=[jax.ShapeDtypeStruct((T, D), F32),
                   jax.ShapeDtypeStruct((RNN_HEADS, RNN_HD, RNN_HD), F32),
                   jax.ShapeDtypeStruct((RNN_HEADS, RNN_HD, RNN_HD), F32),
                   jax.ShapeDtypeStruct((SUBLANE, D), F32)],
        scratch_shapes=[pltpu.VMEM((TS, D), F32)] * 2 + [pltpu.VMEM((1, D), F32)] * 2)


def _od_in_bwd(dxcf, dxcb, xr, dg1, x1, dx1p, mod, w_in, cw, comm=None):
    T = x1.shape[0]
    n = T // TM
    slab = OD_IN // N_DEV
    prev_spec, next_spec = _halo_specs(TM, D, n, T, False)

    def body(fp_ref, fc_ref, fn_ref, bp_ref, bc_ref, bn_ref, xp_ref, xc_ref, xn_ref, dg_ref, x1_ref, dxp_ref,
             mod_ref, w_ref, cw_ref, dx_ref, dwb_ref, vec_ref, dw_ref):
        i = pl.program_id(0)

        @pl.when(i == 0)
        def _():
            dw_ref[...] = jnp.zeros_like(dw_ref)
            vec_ref[...] = jnp.zeros_like(vec_ref)

        dcur = fc_ref[...] + bc_ref[...]
        dprev = jnp.where(i > 0, fp_ref[...] + bp_ref[...], 0.0)
        dnext = jnp.where(i < n - 1, fn_ref[...] + bn_ref[...], 0.0)
        dext = jnp.concatenate([dprev, dcur, dnext], axis=0)
        xext = _ext_rows(xp_ref, xc_ref[...], xn_ref, i, n)
        cw_v = cw_ref[...]
        dxr = None
        for k in range(4):
            term = cw_v[k:k + 1, :] * _shift_rows(dext, 2 - k, TM)
            dxr = term if dxr is None else dxr + term
            vec_ref[k:k + 1, :] += _rowsum(dcur * _shift_rows(xext, k - 2, TM))
        vec_ref[4:5, :] += _rowsum(dcur)
        dp = jnp.concatenate([dxr.astype(BF16), dg_ref[...]], axis=-1)
        x1v = x1_ref[...]
        scale1 = 1.0 + mod_ref[1:2, :]
        h1 = (x1v * scale1 + mod_ref[0:1, :]).astype(BF16)
        dh1 = _dot_nt(dp, w_ref[...])
        dw_ref[...] += _dot_tn(h1, dp)
        dx_ref[...] = dxp_ref[...] + dh1 * scale1
        vec_ref[5:6, :] += _rowsum(dh1)
        vec_ref[6:7, :] += _rowsum(dh1 * x1v)

        @pl.when(i == n - 1)
        def _():
            for j in range(N_DEV):
                dwb_ref[j] = dw_ref[:, j * slab:(j + 1) * slab].astype(BF16)

    t = _tile(TM, D)
    return _fused_call(
        body, comm, (dxcf, dxcf, dxcf, dxcb, dxcb, dxcb, xr, xr, xr, dg1, x1, dx1p, mod, w_in, cw),
        name="od_in_bwd", grid=(n,),
        in_specs=[prev_spec, t, next_spec, prev_spec, t, next_spec, prev_spec, t, next_spec, t, t, t,
                  _full((3, D)), _full((D, OD_IN)), _full((4, D))],
        out_specs=[t, _full((N_DEV, D, slab)), _full((SUBLANE, D))],
        out_shape=[jax.ShapeDtypeStruct((T, D), F32), jax.ShapeDtypeStruct((N_DEV, D, slab), BF16),
                   jax.ShapeDtypeStruct((SUBLANE, D), F32)],
        scratch_shapes=[pltpu.VMEM((D, OD_IN), F32)])


def _ev_out_bwd(dx1, z0, out0, y0, ycat, g0, w_out, mod, lnp):
    T = dx1.shape[0]

    def body(dx_ref, z_ref, out_ref, y0_ref, yc_ref, g_ref, w_ref, mod_ref, ln_ref,
             dxp_ref, dyc_ref, dg_ref, dwb_ref, vec_ref, dw_ref):
        i = pl.program_id(0)

        @pl.when(i == 0)
        def _():
            dw_ref[...] = jnp.zeros_like(dw_ref)
            vec_ref[...] = jnp.zeros_like(vec_ref)

        lng = ln_ref[0:1, :]
        _, xhat, rstd = _ln_fwd(z_ref[...], lng, ln_ref[1:2, :])
        dy = dx_ref[...]
        dz = _ln_bwd(dy, xhat, rstd, lng)
        vec_ref[0:1, :] += _rowsum(dy * xhat)
        vec_ref[1:2, :] += _rowsum(dy)
        vec_ref[2:3, :] += _rowsum(dz * out_ref[...].astype(F32))
        dout = (dz * mod_ref[2:3, :]).astype(BF16)
        dy0 = _dot_nt(dout, w_ref[...])
        dw_ref[...] += _dot_tn(y0_ref[...], dout)
        sg, dsg = _silu_and_grad(g_ref[...].astype(F32))
        dyc_ref[...] = (dy0 * sg).astype(BF16)
        dg_ref[...] = (dy0 * yc_ref[...].astype(F32) * dsg).astype(BF16)
        dxp_ref[...] = ALPHA * dz

        @pl.when(i == T // TM - 1)
        def _():
            dwb_ref[...] = dw_ref[...].astype(BF16)

    t = _tile(TM, D)
    return _pallas(
        body, name="ev_out_bwd", grid=(T // TM,),
        in_specs=[t, t, t, t, t, t, _full((D, D)), _full((3, D)), _full((2, D))],
        out_specs=[t, t, t, _full((D, D)), _full((SUBLANE, D))],
        out_shape=[jax.ShapeDtypeStruct((T, D), F32), jax.ShapeDtypeStruct((T, D), BF16),
                   jax.ShapeDtypeStruct((T, D), BF16), jax.ShapeDtypeStruct((D, D), BF16),
                   jax.ShapeDtypeStruct((SUBLANE, D), F32)],
        scratch_shapes=[pltpu.VMEM((D, D), F32)],
        compiler_params=_params(("arbitrary",)),
    )(dx1, z0, out0, y0, ycat, g0, w_out, mod, lnp)


def _mix0_bwd(q, kvx, lse, dyc, ycat, su, sv, sink_l, bias, a128, gsum, sel, sg_lng, sg_lnb, sg_w, sg_bfull,
              rc, rs1, rs2, comm=None):
    T = q.shape[0]
    nb = T // BLK

    def body(q_ref, kp_ref, kc_ref, kn_ref, lse_ref, dyc_ref, yc_ref, su_ref, sv_ref, sink_ref, bias_ref, a_ref,
             gsum_ref, sel_ref, lng_ref, lnb_ref, w_ref, bfull_ref, c_ref, s1_ref, s2_ref,
             dq_ref, dkv_ref, dsu_ref, dsv_ref, dw_ref, dbt_ref, vec_ref, dsink_ref):
        n = pl.program_id(0)

        @pl.when(n == 0)
        def _():
            dkv_ref[...] = jnp.zeros_like(dkv_ref)
            dw_ref[...] = jnp.zeros_like(dw_ref)
            dbt_ref[...] = jnp.zeros_like(dbt_ref)
            vec_ref[...] = jnp.zeros_like(vec_ref)
            dsink_ref[...] = jnp.zeros_like(dsink_ref)

        band = pl.ds(pl.multiple_of(n * BLK + (TM - BLK), BLK), 3 * BLK)
        bias = _band_bias(bias_ref, n, nb)
        kvx = jnp.concatenate([kp_ref[...], kc_ref[...], kn_ref[...]], axis=0)
        low = lax.broadcasted_iota(jnp.int32, (BLK, LANE), 1) < HEAD_DIM
        sel = sel_ref[...]
        c, s1, s2 = c_ref[...], s1_ref[...], s2_ref[...]
        for kvh in range(2):
            dkx = jnp.zeros((3 * BLK, LANE), F32)
            dvx = jnp.zeros((3 * BLK, LANE), F32)
            for t in range(2 * kvh, 2 * kvh + 2):
                tl = slice(t * LANE, (t + 1) * LANE)
                qt = q_ref[:, tl]
                do = dyc_ref[:, tl]
                p_hi, p_lo = _split_bf16(do.astype(F32) * yc_ref[:, tl].astype(F32))
                deltas = _dot_nt(sel, p_hi) + _dot_nt(sel, p_lo)
                dq_acc = None
                for par in range(2):
                    h = 2 * t + par
                    hl = slice(h * LANE, (h + 1) * LANE)
                    kt = 2 * kvh + par
                    ke = kvx[:, kt * LANE:(kt + 1) * LANE]
                    ve = kvx[:, (4 + kt) * LANE:(5 + kt) * LANE]
                    lse = lse_ref[0, :, hl]
                    delta = deltas[par:par + 1, :]
                    pt = jnp.exp(_dot_nt(ke, qt) + bias - lse)
                    dst = (pt * (_dot_nt(ve, do) - delta)).astype(BF16)
                    dsink_ref[:, hl] += jnp.exp(sink_ref[:, hl] - lse) * delta
                    part = _dot_tn(dst, ke)
                    dq_acc = part if dq_acc is None else dq_acc + part
                    mine = low if par == 0 else jnp.logical_not(low)
                    dkx = dkx + jnp.dot(dst, jnp.where(mine, qt, jnp.zeros_like(qt)), preferred_element_type=F32)
                    dvx = dvx + jnp.dot(pt.astype(BF16), jnp.where(mine, do, jnp.zeros_like(do)),
                                        preferred_element_type=F32)
                dq_ref[:, tl] = _rope_bwd(dq_acc * (HEAD_DIM ** -0.5), c, s1, s2).astype(BF16)
            dkv_ref[band, kvh * LANE:(kvh + 1) * LANE] += dkx
            dkv_ref[band, (2 + kvh) * LANE:(3 + kvh) * LANE] += dvx

        lng = lng_ref[...]
        xhat, rstd, vb, svm = _sg_core(sv_ref, lng, lnb_ref[...], a_ref, w_ref, bfull_ref)
        dy = dyc_ref[:, ATTN_W:].astype(F32)
        dsu_ref[...] = (dy * svm).astype(BF16)
        dsvm = dy * su_ref[...].astype(F32)
        d_hi, d_lo = _split_bf16(dsvm)
        gsum = gsum_ref[...]
        dbt_ref[...] += jnp.dot(d_hi, gsum, preferred_element_type=F32) + jnp.dot(d_lo, gsum,
                                                                                 preferred_element_type=F32)
        tiles = []
        for t in range(SG_W // LANE):
            tl = slice(t * LANE, (t + 1) * LANE)
            dt, v2 = d_hi[:, tl], vb[:, tl]
            dw_ref[2 * t] += _dot_nt(jnp.where(low, dt, jnp.zeros_like(dt)), v2)
            dw_ref[2 * t + 1] += _dot_nt(jnp.where(low, jnp.zeros_like(dt), dt), v2)
            tiles.append(jnp.where(low, _dot_tn(w_ref[2 * t], dt), _dot_tn(w_ref[2 * t + 1], dt)))
        dvgn = jnp.concatenate(tiles, axis=-1)
        vec_ref[0:1, :] += _rowsum(dvgn * xhat)
        vec_ref[1:2, :] += _rowsum(dvgn)
        dxh = dvgn * lng
        m1 = _group_mean(dxh, a_ref)
        m2 = _group_mean(dxh * xhat, a_ref)
        dsv_ref[...] = (rstd * (dxh - m1 - xhat * m2)).astype(BF16)

    return _fused_call(
        body, comm, (q, kvx, kvx, kvx, lse, dyc, ycat, su, sv, sink_l, bias, a128, gsum, sel, sg_lng, sg_lnb, sg_w,
                     sg_bfull, rc, rs1, rs2),
        name="mix0_bwd", grid=(nb,),
        in_specs=[_tile(BLK, ATTN_W)] + _band_specs(KVX_W, nb) + [
            pl.BlockSpec((1, 1, N_HEADS * LANE), lambda n: (n, 0, 0)), _tile(BLK, D), _tile(BLK, D),
            _tile(BLK, SG_W), _tile(BLK, SG_W), _full((1, N_HEADS * LANE)), _full((3 * BLK, LANE)),
            _full((LANE, LANE)), _full((SG_W, LANE)), _full((SUBLANE, LANE)), _full((1, SG_W)), _full((1, SG_W)),
            _full((SG_GROUPS, BLK, BLK)), _full((BLK, SG_W)), _tile(BLK, LANE), _tile(BLK, LANE), _tile(BLK, LANE)],
        out_specs=[_tile(BLK, ATTN_W), _full((T + 2 * TM, 4 * LANE)), _tile(BLK, SG_W), _tile(BLK, SG_W),
                   _full((SG_GROUPS, BLK, BLK)), _full((BLK, LANE)), _full((SUBLANE, SG_W)),
                   _full((1, N_HEADS * LANE))],
        out_shape=[jax.ShapeDtypeStruct((T, ATTN_W), BF16), jax.ShapeDtypeStruct((T + 2 * TM, 4 * LANE), F32),
                   jax.ShapeDtypeStruct((T, SG_W), BF16), jax.ShapeDtypeStruct((T, SG_W), BF16),
                   jax.ShapeDtypeStruct((SG_GROUPS, BLK, BLK), F32), jax.ShapeDtypeStruct((BLK, LANE), F32),
                   jax.ShapeDtypeStruct((SUBLANE, SG_W), F32), jax.ShapeDtypeStruct((1, N_HEADS * LANE), F32)])


def _ev_in_bwd(dq, dkv, dsu, dsv, dg0, x, dxp, mod, w_in, rc, rs1, rs2, comm=None):
    T = x.shape[0]

    def body(dq_ref, dkv_ref, dsu_ref, dsv_ref, dg_ref, x_ref, dxp_ref, mod_ref, w_ref, c_ref, s1_ref, s2_ref,
             dx_ref, dwb_ref, vec_ref, dw_ref):
        i = pl.program_id(0)

        @pl.when(i == 0)
        def _():
            dw_ref[...] = jnp.zeros_like(dw_ref)
            vec_ref[...] = jnp.zeros_like(vec_ref)

        low = lax.broadcasted_iota(jnp.int32, (TM, LANE), 1) < HEAD_DIM

        def fold(j):
            t0 = dkv_ref[:, (2 * j) * LANE:(2 * j + 1) * LANE]
            t1 = dkv_ref[:, (2 * j + 1) * LANE:(2 * j + 2) * LANE]
            return jnp.where(low, t0 + pltpu.roll(t0, HEAD_DIM, 1), t1 + pltpu.roll(t1, HEAD_DIM, 1))

        dk = _rope_bwd(fold(0), c_ref[...], s1_ref[...], s2_ref[...]).astype(BF16)
        dp = jnp.concatenate([dq_ref[...], dk, fold(1).astype(BF16), dsu_ref[...], dsv_ref[...],
                              dg_ref[...]], axis=-1)
        xv = x_ref[...]
        scale0 = 1.0 + mod_ref[1:2, :]
        h0 = (xv * scale0 + mod_ref[0:1, :]).astype(BF16)
        dh0 = _dot_nt(dp, w_ref[...])
        dw_ref[...] += _dot_tn(h0, dp)
        dx_ref[...] = dxp_ref[...] + dh0 * scale0
        vec_ref[0:1, :] += _rowsum(dh0)
        vec_ref[1:2, :] += _rowsum(dh0 * xv)

        @pl.when(i == T // TM - 1)
        def _():
            dwb_ref[...] = dw_ref[...].astype(BF16)

    t = _tile(TM, D)
    return _fused_call(
        body, comm, (dq, dkv, dsu, dsv, dg0, x, dxp, mod, w_in, rc, rs1, rs2), name="ev_in_bwd", grid=(T // TM,),
        in_specs=[_tile(TM, ATTN_W), pl.BlockSpec((TM, 4 * LANE), lambda i: (i + 1, 0)), _tile(TM, SG_W),
                  _tile(TM, SG_W), t, t, t,
                  _full((3, D)), _full((D, EV_IN)), _tile(TM, LANE), _tile(TM, LANE), _tile(TM, LANE)],
        out_specs=[t, _full((D, EV_IN)), _full((SUBLANE, D))],
        out_shape=[jax.ShapeDtypeStruct((T, D), F32), jax.ShapeDtypeStruct((D, EV_IN), BF16),
                   jax.ShapeDtypeStruct((SUBLANE, D), F32)],
        scratch_shapes=[pltpu.VMEM((D, EV_IN), F32)])


def _sum_slots(land_ref):
    g = land_ref[0].astype(F32)
    for i in range(1, land_ref.shape[0]):
        g = g + land_ref[i].astype(F32)
    return g


def _reduce_adam(land, w, m, v, name):
    R, C = w.shape
    rb = R
    for cand in (128, 64, 32, 16, 8):
        if R % cand == 0:
            rb = cand
            break

    def body(l_ref, w_ref, m_ref, v_ref, g_ref, d_ref, nm_ref, nv_ref):
        g = _sum_slots(l_ref)
        g_ref[...] = g
        dlt, m2, v2 = _adam(w_ref[...], g, m_ref[...], v_ref[...])
        d_ref[...] = dlt
        nm_ref[...] = m2
        nv_ref[...] = v2

    t = pl.BlockSpec((rb, C), lambda i: (i, 0))
    shp = jax.ShapeDtypeStruct((R, C), F32)
    return _pallas(
        body, name=name, grid=(R // rb,),
        in_specs=[pl.BlockSpec((land.shape[0], rb, C), lambda i: (0, i, 0)), t, t, t],
        out_specs=[t] * 4, out_shape=[shp] * 4,
        compiler_params=_params(("parallel",)),
    )(land, w, m, v)


def _tail_exchange(slabs, small):
    _, R, C = slabs.shape
    n_chips = N_DEV // 2
    gather = _GatherComm(small)
    ns = gather.n

    def body(*refs):
        slab_ref = refs[0]
        g_ins = refs[1:1 + ns]
        land_ref = refs[1 + ns]
        g_outs = refs[2 + ns:2 + 2 * ns]
        stage, part, s1_send, s1_recv, s2_send, s2_recv = refs[2 + 2 * ns:8 + 2 * ns]
        g_sems = refs[8 + 2 * ns:]
        x, y, c = _my_pos()
        chip = 2 * x + y
        gather.start(g_ins, g_outs, g_sems)

        swaps = [pltpu.make_async_remote_copy(
            src_ref=slab_ref.at[2 * k + (1 - c)], dst_ref=stage.at[k], send_sem=s1_send.at[k],
            recv_sem=s1_recv.at[k], device_id=(x, y, 1 - c), device_id_type=MESH) for k in range(n_chips)]
        for cp in swaps:
            cp.start()
        for cp in swaps:
            cp.wait()
        for k in range(n_chips):
            part[k] = (slab_ref[2 * k + c].astype(F32) + stage[k].astype(F32)).astype(BF16)

        gather.mid(g_ins, g_outs, g_sems)

        sends = []
        for r in range(1, n_chips):
            px = (1 - x) if (r & 2) else x
            py = (1 - y) if (r & 1) else y
            sends.append(pltpu.make_async_remote_copy(
                src_ref=part.at[2 * px + py], dst_ref=land_ref.at[chip], send_sem=s2_send.at[r - 1],
                recv_sem=s2_recv.at[r - 1], device_id=(px, py, c), device_id_type=MESH))
        for cp in sends:
            cp.start()
        land_ref[chip] = part[chip]
        for cp in sends:
            cp.wait()
        gather.finish(g_ins, g_outs, g_sems)

    any_spec = pl.BlockSpec(memory_space=pl.ANY)
    vmem_spec = pl.BlockSpec(memory_space=pltpu.VMEM)
    res = _pallas(
        body, name="tail_exchange",
        out_shape=[jax.ShapeDtypeStruct((n_chips, R, C), BF16)] + gather.out_shapes(),
        in_specs=[vmem_spec] + [any_spec] * ns, out_specs=[vmem_spec] + [any_spec] * ns,
        scratch_shapes=[pltpu.VMEM((n_chips, R, C), BF16), pltpu.VMEM((n_chips, R, C), BF16),
                        pltpu.SemaphoreType.DMA((n_chips,)), pltpu.SemaphoreType.DMA((n_chips,)),
                        pltpu.SemaphoreType.DMA((n_chips - 1,)), pltpu.SemaphoreType.DMA((n_chips - 1,))]
        + gather.sems(),
        compiler_params=pltpu.CompilerParams(vmem_limit_bytes=VMEM_LIMIT),
    )(slabs, *gather.arrs)
    return res[0], list(res[1:])


def _slots_adam(land, w, m, v, name):
    lead = w.shape[1] if w.ndim == 5 else 1
    inner = w.shape[-3:]
    zeros3 = (0, 0, 0)
    if w.ndim == 5:
        lspec = pl.BlockSpec((N_DEV, 1) + inner, lambda i: (0, i) + zeros3)
        wspec = pl.BlockSpec((1, 1) + inner, lambda i: (0, i) + zeros3)
    else:
        lspec = pl.BlockSpec((N_DEV,) + inner, lambda i: (0,) + zeros3)
        wspec = pl.BlockSpec((1,) + inner, lambda i: (0,) + zeros3)

    def body(l_ref, w_ref, m_ref, v_ref, g_ref, d_ref, nm_ref, nv_ref):
        at = (0, 0) if w.ndim == 5 else (0,)
        g = l_ref[(0,) + at[1:]].astype(F32)
        for i in range(1, N_DEV):
            g = g + l_ref[(i,) + at[1:]].astype(F32)
        dlt, m2, v2 = _adam(w_ref[at], g, m_ref[at], v_ref[at])
        g_ref[at] = g
        d_ref[at] = dlt
        nm_ref[at] = m2
        nv_ref[at] = v2

    shp = jax.ShapeDtypeStruct(w.shape, F32)
    return _pallas(
        body, name=name, grid=(lead,),
        in_specs=[lspec, wspec, wspec, wspec], out_specs=[wspec] * 4, out_shape=[shp] * 4,
        compiler_params=_params(("parallel",)),
    )(land, w, m, v)


SMALL_PARAMS = ("ln_g", "ln_b", "ev_sg_ln_g", "ev_sg_ln_b", "ev_sink", "ev_sg_b",
                "od_conv_w", "od_conv_b", "od_b_a", "od_b_x", "od_lam")


def _small_update(ga, gc, gd, gf, gb, ge, gsink, gbt, params):
    names = list(SMALL_PARAMS)
    flat = [a for nm in names for a in params[nm]]
    n_g = 8

    def body(*refs):
        ga_ref, gc_ref, gd_ref, gf_ref, gb_ref, ge_ref, gs_ref, gbt_ref = refs[:n_g]
        prm = refs[n_g:n_g + 3 * len(names)]
        loss_ref = refs[n_g + 3 * len(names)]
        outs = refs[n_g + 3 * len(names) + 1:]

        def ssum(ref):
            acc = ref[0]
            for i in range(1, N_DEV):
                acc = acc + ref[i]
            return acc

        a, cc, dd, ff, bb, ee = ssum(ga_ref), ssum(gc_ref), ssum(gd_ref), ssum(gf_ref), ssum(gb_ref), ssum(ge_ref)
        loss_ref[...] = a[3:4, 0:LANE]
        me = _slot(*_my_pos())

        def mine(rows):
            acc = jnp.zeros((rows.shape[0], LANE), F32)
            for j in range(N_DEV):
                acc = acc + jnp.where(me == j, rows[:, j * LANE:(j + 1) * LANE], 0.0)
            return acc

        sink_terms = ssum(gs_ref)
        lane8 = lax.broadcasted_iota(jnp.int32, (1, N_HEADS), 1)
        g_sink = jnp.zeros((1, N_HEADS), F32)
        for h in range(N_HEADS):
            tot = -jnp.sum(sink_terms[:, h * LANE:(h + 1) * LANE], axis=1, keepdims=True)
            g_sink = jnp.where(lane8 == h, tot, g_sink)
        grads = dict(
            ln_g=jnp.concatenate([dd[0:1], a[0:1]], axis=0), ln_b=jnp.concatenate([dd[1:2], a[1:2]], axis=0),
            ev_sg_ln_g=ee[0:1], ev_sg_ln_b=ee[1:2], ev_sink=g_sink,
            ev_sg_b=jnp.transpose(ssum(gbt_ref))[0:SG_GROUPS, :],
            od_conv_w=mine(cc[0:4]), od_conv_b=mine(cc[4:5]),
            od_b_a=mine(jnp.concatenate([ff[0:1], bb[0:1]], axis=0)),
            od_b_x=mine(jnp.concatenate([ff[1:2], bb[1:2]], axis=0)),
            od_lam=mine(jnp.concatenate([ff[2:3], bb[2:3]], axis=0)))
        for k, nm in enumerate(names):
            w_ref, m_ref, v_ref = prm[3 * k:3 * k + 3]
            at = (0,) if len(w_ref.shape) == 3 else ()
            g = grads[nm]
            dlt, m2, v2 = _adam(w_ref[at] if at else w_ref[...], g, m_ref[at] if at else m_ref[...],
                                v_ref[at] if at else v_ref[...])
            for o_ref, val in zip(outs[4 * k:4 * k + 4], (g, dlt, m2, v2)):
                if at:
                    o_ref[at] = val
                else:
                    o_ref[...] = val

    gathered = [ga, gc, gd, gf, gb, ge, gsink, gbt]
    out_shape = [jax.ShapeDtypeStruct((1, LANE), F32)]
    for nm in names:
        out_shape += [jax.ShapeDtypeStruct(params[nm][0].shape, F32)] * 4
    return _pallas(
        body, name="small_update", grid=(1,),
        in_specs=[_full(a.shape) for a in gathered + flat],
        out_specs=[_full(s.shape) for s in out_shape], out_shape=out_shape,
        compiler_params=_params(("arbitrary",)),
    )(*gathered, *flat)


VEC_ROWS = 16
VEC_LAYOUT = (("od_conv_w", 4), ("od_conv_b", 1), ("od_b_a", 2), ("od_b_x", 2), ("od_lam", 2))


def _pack_vec(parts):
    rows = [parts[name].reshape(nrows, -1) for name, nrows in VEC_LAYOUT]
    used = sum(r for _, r in VEC_LAYOUT)
    rows.append(jnp.zeros((VEC_ROWS - used, rows[0].shape[1]), F32))
    return jnp.concatenate(rows, axis=0)


def _to_slabs(full, cols_per):
    R = full.shape[0]
    return full.reshape(R, N_DEV, cols_per).transpose(1, 0, 2)


def _from_slabs(slabs):
    n, R, cp = slabs.shape
    return slabs.transpose(1, 0, 2).reshape(R, n * cp)


def kernel(x, c, positions, ada_w, ada_b, ln_g, ln_b, ev_w_in, ev_w_out, ev_sink, ev_sg_ln_g, ev_sg_ln_b, ev_sg_w, ev_sg_b, od_w_in, od_conv_w, od_conv_b, od_w_a, od_b_a, od_w_x, od_b_x, od_lam, od_w_out, loss_target, m_ada_w, m_ada_b, m_ln_g, m_ln_b, m_ev_w_in, m_ev_w_out, m_ev_sink, m_ev_sg_ln_g, m_ev_sg_ln_b, m_ev_sg_w, m_ev_sg_b, m_od_w_in, m_od_conv_w, m_od_conv_b, m_od_w_a, m_od_b_a, m_od_w_x, m_od_b_x, m_od_lam, m_od_w_out, v_ada_w, v_ada_b, v_ln_g, v_ln_b, v_ev_w_in, v_ev_w_out, v_ev_sink, v_ev_sg_ln_g, v_ev_sg_ln_b, v_ev_sg_w, v_ev_sg_b, v_od_w_in, v_od_conv_w, v_od_conv_b, v_od_w_a, v_od_b_a, v_od_w_x, v_od_b_x, v_od_lam, v_od_w_out):
    T = x.shape[1]
    me = _slot(*_my_pos())
    xs = x.reshape(T, D)
    tgt = loss_target.reshape(T, D)

    vec_w = _pack_vec(dict(od_conv_w=od_conv_w[0], od_conv_b=od_conv_b, od_b_a=od_b_a[0], od_b_x=od_b_x[0],
                           od_lam=od_lam[0]))
    c_all, g_ev_in, g_vec = _all_gather([c, ev_w_in[0].astype(BF16), vec_w], "ag_params")
    c_all = c_all.reshape(N_DEV, D)
    w_ev_in = _from_slabs(g_ev_in)
    vec_full = _from_slabs(g_vec)
    cw, cb = vec_full[0:4], vec_full[4:5]
    ba, bx, lam = vec_full[5:7], vec_full[7:9], vec_full[9:11]

    mod_part = _mod_part(c_all, ada_w)
    (mod_all,) = _all_gather([mod_part], "ag_mod")
    mod_mine = lax.dynamic_index_in_dim(mod_all, me, axis=2, keepdims=False)
    mod = mod_mine.transpose(1, 0, 2).reshape(2, 3 * D) + ada_b
    mod0 = mod[0].reshape(3, D)
    mod1 = mod[1].reshape(3, D)

    half = 8
    inv_freq = jnp.power(jnp.float32(ROPE_THETA), -jnp.arange(half, dtype=F32) / half)
    ang = positions.reshape(T).astype(F32)[:, None] * inv_freq
    cos_t = jnp.tile(jnp.cos(ang), (1, LANE // half))
    sin_t = jnp.tile(jnp.sin(ang), (1, LANE // half))
    l64 = jnp.arange(LANE) % HEAD_DIM
    rc = jnp.where(l64 < 2 * half, cos_t, 1.0)
    rs1 = jnp.where(l64 < half, -sin_t, 0.0)
    rs2 = jnp.where((l64 >= half) & (l64 < 2 * half), sin_t, 0.0)

    ln0 = jnp.stack([ln_g[0], ln_b[0]])
    ln1 = jnp.stack([ln_g[1], ln_b[1]])
    sg_lng = ev_sg_ln_g
    sg_lnb = ev_sg_ln_b
    sg_w = ev_sg_w[0].astype(BF16)
    sg_bfull = jnp.repeat(ev_sg_b[0].T, SG_DIM, axis=1)
    sink_l = jnp.repeat(ev_sink, LANE, axis=1)
    kj = jnp.arange(3 * BLK)[:, None]
    qi = jnp.arange(BLK)[None, :]
    band_bias = jnp.where(jnp.abs(kj - BLK - qi) <= BLK, 0.0, NEG_INF).astype(F32)
    lanes = jnp.arange(LANE)
    a128 = jnp.where(lanes[:, None] // SG_DIM == lanes[None, :] // SG_DIM, 1.0 / SG_DIM, 0.0).astype(BF16)
    gsum = (jnp.arange(SG_W)[:, None] // SG_DIM == lanes[None, :]).astype(BF16)
    sel = (jnp.arange(SUBLANE)[:, None] == lanes[None, :] // HEAD_DIM).astype(BF16)
    wa = od_w_a[0].astype(BF16)
    wx = od_w_x[0].astype(BF16)

    (q, kvx, su, sv, g0), (g_ev_out,) = _ev_in(xs, mod0, w_ev_in, rc, rs1, rs2,
                                               _GatherComm([ev_w_out[0].astype(BF16)]))
    w_ev_out = g_ev_out.reshape(D, D)
    (ycat, y0, lse), (g_od_in, g_od_out) = _mix0_fwd(
        q, kvx, su, sv, g0, sink_l, band_bias, a128, sg_lng, sg_lnb, sg_w, sg_bfull,
        _GatherComm([od_w_in[0].astype(BF16), od_w_out[0].astype(BF16)]))
    w_od_in = _from_slabs(g_od_in)
    w_od_out = g_od_out.reshape(D, D)
    out0, z0, x1 = _ev_out(y0, w_ev_out, xs, mod0, ln0)
    xr, g1 = _od_in(x1, mod1, w_od_in)
    fwd_f = _rglru_fwd(xr, cw, cb, wa[0], wx[0], ba[0:1], bx[0:1], lam[0:1], False, "rglru_fwd_f")
    fwd_b = _rglru_fwd(xr, cw, cb, wa[1], wx[1], ba[1:2], bx[1:2], lam[1:2], True, "rglru_fwd_b")
    dh, dg1, dx1p, d_od_out, vec_a = _od_out(fwd_f[0], fwd_b[0], g1, w_od_out, x1, tgt, mod1, ln1)

    (dxcf, dwa_f, dwx_f, vec_f), (l_od_out,) = _rglru_bwd(
        fwd_f, dh, wa[0], wx[0], lam[0:1], False, "rglru_bwd_f",
        _ExchangeComm([d_od_out.reshape(N_DEV, D // N_DEV, D)]))
    (dxcb, dwa_b, dwx_b, vec_b), _ = _rglru_bwd(fwd_b, dh, wa[1], wx[1], lam[1:2], True, "rglru_bwd_b")
    (dx1, d_od_in, vec_c), (a_wa, a_wx) = _od_in_bwd(
        dxcf, dxcb, xr, dg1, x1, dx1p, mod1, w_od_in, cw,
        _GatherComm([jnp.stack([dwa_f, dwa_b]).astype(BF16), jnp.stack([dwx_f, dwx_b]).astype(BF16)],
                    mid_frac=0.75))
    dxp, dyc, dg0, d_ev_out, vec_d = _ev_out_bwd(dx1, z0, out0, y0, ycat, g0, w_ev_out, mod0, ln0)
    (dq, dkv, dsu, dsv, d_sg_w, d_sg_bt, vec_e, d_sink_l), (l_od_in, l_ev_out) = _mix0_bwd(
        q, kvx, lse, dyc, ycat, su, sv, sink_l, band_bias, a128, gsum, sel, sg_lng, sg_lnb, sg_w, sg_bfull,
        rc, rs1, rs2, _ExchangeComm([d_od_in, d_ev_out.reshape(N_DEV, D // N_DEV, D)]))
    (grad_x, d_ev_in, vec_g), _ = _ev_in_bwd(dq, dkv, dsu, dsv, dg0, xs, dxp, mod0, w_ev_in, rc, rs1, rs2)

    l_ev_in, (ga, gc, gd, gf, gb, gg, ge, gsink, gbt, a_sgw) = _tail_exchange(
        _to_slabs(d_ev_in, EV_IN // N_DEV),
        [vec_a, vec_c, vec_d, vec_f, vec_b, vec_g, vec_e, d_sink_l, d_sg_bt, d_sg_w.astype(BF16)])

    dmod_all = jnp.stack([jnp.concatenate([gg[:, 0], gg[:, 1], gd[:, 2]], axis=-1),
                          jnp.concatenate([gc[:, 5], gc[:, 6], ga[:, 2]], axis=-1)], axis=1)
    cols = ada_w.shape[2]
    dmod_cols = lax.dynamic_slice_in_dim(dmod_all, me * cols, cols, axis=2).transpose(1, 0, 2)
    (g_ada_w, d_ada_w, nm_ada_w, nv_ada_w, g_ada_b, d_ada_b, nm_ada_b, nv_ada_b) = _ada_update(
        c_all, dmod_cols, dmod_all, ada_w, m_ada_w, v_ada_w, ada_b, m_ada_b, v_ada_b)

    res = dict(ada_w=[g_ada_w, d_ada_w, nm_ada_w, nv_ada_w], ada_b=[g_ada_b, d_ada_b, nm_ada_b, nv_ada_b])
    for name, land, w, m, v in (("ev_w_in", l_ev_in, ev_w_in, m_ev_w_in, v_ev_w_in),
                                ("ev_w_out", l_ev_out, ev_w_out, m_ev_w_out, v_ev_w_out),
                                ("od_w_in", l_od_in, od_w_in, m_od_w_in, v_od_w_in),
                                ("od_w_out", l_od_out, od_w_out, m_od_w_out, v_od_w_out)):
        res[name] = [a[None] for a in _reduce_adam(land, w[0], m[0], v[0], "adam_" + name)]
    res["od_w_a"] = _slots_adam(a_wa, od_w_a, m_od_w_a, v_od_w_a, "adam_od_w_a")
    res["od_w_x"] = _slots_adam(a_wx, od_w_x, m_od_w_x, v_od_w_x, "adam_od_w_x")
    res["ev_sg_w"] = _slots_adam(a_sgw, ev_sg_w, m_ev_sg_w, v_ev_sg_w, "adam_ev_sg_w")
    small = dict(ln_g=(ln_g, m_ln_g, v_ln_g), ln_b=(ln_b, m_ln_b, v_ln_b),
                 ev_sg_ln_g=(ev_sg_ln_g, m_ev_sg_ln_g, v_ev_sg_ln_g),
                 ev_sg_ln_b=(ev_sg_ln_b, m_ev_sg_ln_b, v_ev_sg_ln_b),
                 ev_sink=(ev_sink, m_ev_sink, v_ev_sink), ev_sg_b=(ev_sg_b, m_ev_sg_b, v_ev_sg_b),
                 od_conv_w=(od_conv_w, m_od_conv_w, v_od_conv_w), od_conv_b=(od_conv_b, m_od_conv_b, v_od_conv_b),
                 od_b_a=(od_b_a, m_od_b_a, v_od_b_a), od_b_x=(od_b_x, m_od_b_x, v_od_b_x),
                 od_lam=(od_lam, m_od_lam, v_od_lam))
    small_out = _small_update(ga, gc, gd, gf, gb, ge, gsink, gbt, small)
    loss = small_out[0][0, 0]
    for k, name in enumerate(SMALL_PARAMS):
        res[name] = small_out[1 + 4 * k:5 + 4 * k]

    order = ["ada_w", "ada_b", "ln_g", "ln_b", "ev_w_in", "ev_w_out", "ev_sink", "ev_sg_ln_g", "ev_sg_ln_b",
             "ev_sg_w", "ev_sg_b", "od_w_in", "od_conv_w", "od_conv_b", "od_w_a", "od_b_a", "od_w_x", "od_b_x",
             "od_lam", "od_w_out"]
    outs = [loss, grad_x.reshape(1, T, D)]
    for kind in range(4):
        outs += [res[name][kind] for name in order]
    return tuple(outs)
```

```python
import functools

import jax
import jax.numpy as jnp
from jax import lax
from jax.experimental import pallas as pl
from jax.experimental.pallas import tpu as pltpu

F32 = jnp.float32
BF16 = jnp.bfloat16

N_DEV = 8
D = 1024
N_HEADS = 8
HEAD_DIM = 64
KV_WIDTH = 128
ATTN_W = 512
SG_W = 512
SG_GROUPS = 8
SG_DIM = 64
BLK = 128
KVX_W = 1024
EV_IN = 2816
OD_IN = 2048
RNN_HEADS = 8
RNN_HD = 128
ALPHA = 4.0 ** 0.25
LN_EPS = 1e-5
NEG_INF = -1e30
RG_C = 8.0
ROPE_THETA = 500000.0
LR, B1, B2, EPS, WD, STEP = 0.001, 0.9, 0.999, 1e-08, 0.01, 10

LANE = 128
SUBLANE = 8
TM = 256
TMF = 512
TS = 256
VMEM_LIMIT = 56 * 1024 * 1024

MESH = pl.DeviceIdType.MESH


def _pallas(body, **kw):
    return pl.pallas_call(body, **kw)


def _params(sem, vmem=VMEM_LIMIT):
    return pltpu.CompilerParams(dimension_semantics=sem, vmem_limit_bytes=vmem)


def _sigmoid(x):
    return 0.5 * jnp.tanh(0.5 * x) + 0.5


def _silu_and_grad(x):
    s = _sigmoid(x)
    return x * s, s * (1.0 + x * (1.0 - s))


def _dot(a, b):
    return jnp.dot(a.astype(BF16), b.astype(BF16), preferred_element_type=F32)


def _dot_nt(a, b):
    return lax.dot_general(a.astype(BF16), b.astype(BF16), (((1,), (1,)), ((), ())), preferred_element_type=F32)


def _dot_tn(a, b):
    return lax.dot_general(a.astype(BF16), b.astype(BF16), (((0,), (0,)), ((), ())), preferred_element_type=F32)


def _ln_fwd(z, g, b):
    mu = jnp.mean(z, axis=-1, keepdims=True)
    zc = z - mu
    var = jnp.mean(zc * zc, axis=-1, keepdims=True)
    rstd = lax.rsqrt(var + LN_EPS)
    xhat = zc * rstd
    return xhat * g + b, xhat, rstd


def _ln_bwd(dy, xhat, rstd, g):
    dxh = dy * g
    m1 = jnp.mean(dxh, axis=-1, keepdims=True)
    m2 = jnp.mean(dxh * xhat, axis=-1, keepdims=True)
    return rstd * (dxh - m1 - xhat * m2)


def _rowsum(v):
    return jnp.sum(v, axis=0, keepdims=True)


def _rope_fwd(t, c, s1, s2):
    return t * c + pltpu.roll(t, LANE - 8, 1) * s1 + pltpu.roll(t, 8, 1) * s2


def _rope_bwd(d, c, s1, s2):
    return d * c + pltpu.roll(d * s1, 8, 1) + pltpu.roll(d * s2, LANE - 8, 1)


def _adam(w, g, m, v):
    m2 = B1 * m + (1.0 - B1) * g
    v2 = B2 * v + (1.0 - B2) * (g * g)
    m_hat = m2 / (1.0 - B1 ** STEP)
    v_hat = v2 / (1.0 - B2 ** STEP)
    delta = -LR * (m_hat / (jnp.sqrt(v_hat) + EPS) + WD * w)
    return delta, m2, v2


def _tile(rows, width):
    return pl.BlockSpec((rows, width), lambda i: (i, 0))


def _full(shape):
    zeros = (0,) * len(shape)
    return pl.BlockSpec(shape, lambda i: zeros)


def _rev_tile(rows, width, n, reverse):
    if reverse:
        return pl.BlockSpec((rows, width), lambda i: (n - 1 - i, 0))
    return pl.BlockSpec((rows, width), lambda i: (i, 0))


def _halo_specs(rows, width, n, total_rows, reverse):
    per = rows // SUBLANE
    last = total_rows // SUBLANE - 1

    def tile_of(i):
        return (n - 1 - i) if reverse else i

    prev = pl.BlockSpec((SUBLANE, width), lambda i: (jnp.maximum(tile_of(i) * per - 1, 0), 0))
    nxt = pl.BlockSpec((SUBLANE, width), lambda i: (jnp.minimum((tile_of(i) + 1) * per, last), 0))
    return prev, nxt


def _my_pos():
    return lax.axis_index("x"), lax.axis_index("y"), lax.axis_index("c")


def _slot(px, py, pc):
    return 4 * px + 2 * py + pc


def _all_gather(arrs, name):
    n = len(arrs)

    def body(*refs):
        ins, outs = refs[:n], refs[n:2 * n]
        send_sems, recv_sems, local_sems = refs[2 * n:]
        x, y, c = _my_pos()
        me, sibling = (x, y, c), (x, y, 1 - c)
        chips = [(1 - x, y), (x, 1 - y), (1 - x, 1 - y)]

        def copy(a, k, block, to, src=None):
            dst = outs[a].at[_slot(*block)]
            return pltpu.make_async_remote_copy(
                src_ref=dst if src is None else src, dst_ref=dst,
                send_sem=send_sems.at[a * 7 + k], recv_sem=recv_sems.at[a * 7 + k],
                device_id=to, device_id_type=MESH)

        local, first = [], []
        for a in range(n):
            lc = pltpu.make_async_copy(ins[a], outs[a].at[_slot(*me)], local_sems.at[a])
            lc.start()
            local.append(lc)
            first.append(copy(a, 0, me, sibling, src=ins[a]))
            first += [copy(a, 1 + j, me, (*chip, c), src=ins[a]) for j, chip in enumerate(chips)]
        for cp in first:
            cp.start()
        passed = []
        for j, chip in enumerate(chips):
            for a in range(n):
                copy(a, 1 + j, (*chip, c), me).wait_recv()
                fw = copy(a, 4 + j, (*chip, c), sibling)
                fw.start()
                passed.append(fw)
        for a in range(n):
            copy(a, 0, sibling, me).wait_recv()
            for j, chip in enumerate(chips):
                copy(a, 4 + j, (*chip, 1 - c), me).wait_recv()
        for cp in first + passed:
            cp.wait_send()
        for lc in local:
            lc.wait()

    any_spec = pl.BlockSpec(memory_space=pl.ANY)
    return _pallas(
        body, name=name,
        out_shape=[jax.ShapeDtypeStruct((N_DEV,) + a.shape, a.dtype) for a in arrs],
        in_specs=[any_spec] * n, out_specs=[any_spec] * n,
        scratch_shapes=[pltpu.SemaphoreType.DMA((7 * n,)), pltpu.SemaphoreType.DMA((7 * n,)),
                        pltpu.SemaphoreType.DMA((n,))],
    )(*arrs)


class _GatherComm:
    has_mid = True

    def __init__(self, arrs, mid_frac=0.5):
        self.arrs = list(arrs)
        self.n = len(self.arrs)
        self.mid_frac = mid_frac

    def out_shapes(self):
        return [jax.ShapeDtypeStruct((N_DEV,) + a.shape, a.dtype) for a in self.arrs]

    def sems(self):
        return [pltpu.SemaphoreType.DMA((7 * self.n,)), pltpu.SemaphoreType.DMA((7 * self.n,)),
                pltpu.SemaphoreType.DMA((self.n,))]

    def _parts(self, ins, outs, sems):
        send_sems, recv_sems, local_sems = sems
        x, y, c = _my_pos()
        me, sibling = (x, y, c), (x, y, 1 - c)
        chips = [(1 - x, y), (x, 1 - y), (1 - x, 1 - y)]

        def copy(a, k, block, to, src=None):
            dst = outs[a].at[_slot(*block)]
            return pltpu.make_async_remote_copy(
                src_ref=dst if src is None else src, dst_ref=dst,
                send_sem=send_sems.at[a * 7 + k], recv_sem=recv_sems.at[a * 7 + k],
                device_id=to, device_id_type=MESH)

        local = [pltpu.make_async_copy(ins[a], outs[a].at[_slot(*me)], local_sems.at[a]) for a in range(self.n)]
        first = []
        for a in range(self.n):
            first.append(copy(a, 0, me, sibling, src=ins[a]))
            first += [copy(a, 1 + j, me, (*chip, c), src=ins[a]) for j, chip in enumerate(chips)]
        ici_in = [copy(a, 1 + j, (*chip, c), me) for j, chip in enumerate(chips) for a in range(self.n)]
        passed = [copy(a, 4 + j, (*chip, c), sibling) for j, chip in enumerate(chips) for a in range(self.n)]
        d2d_in = []
        for a in range(self.n):
            d2d_in.append(copy(a, 0, sibling, me))
            d2d_in += [copy(a, 4 + j, (*chip, 1 - c), me) for j, chip in enumerate(chips)]
        return local, first, ici_in, passed, d2d_in

    def start(self, ins, outs, sems):
        local, first, _, _, _ = self._parts(ins, outs, sems)
        for cp in local + first:
            cp.start()

    def mid(self, ins, outs, sems):
        _, _, ici_in, passed, _ = self._parts(ins, outs, sems)
        for arrived, fw in zip(ici_in, passed):
            arrived.wait_recv()
            fw.start()

    def finish(self, ins, outs, sems):
        local, first, _, passed, d2d_in = self._parts(ins, outs, sems)
        for cp in d2d_in:
            cp.wait_recv()
        for cp in first + passed:
            cp.wait_send()
        for cp in local:
            cp.wait()


class _ExchangeComm:
    has_mid = False

    def __init__(self, arrs):
        self.arrs = list(arrs)
        self.n = len(self.arrs)

    def out_shapes(self):
        return [jax.ShapeDtypeStruct(a.shape, a.dtype) for a in self.arrs]

    def sems(self):
        return [pltpu.SemaphoreType.DMA((7 * self.n,)), pltpu.SemaphoreType.DMA((7 * self.n,)),
                pltpu.SemaphoreType.DMA((self.n,))]

    def _copies(self, ins, outs, sems):
        send_sems, recv_sems, local_sems = sems
        x, y, c = _my_pos()
        mine = _slot(x, y, c)
        copies = [pltpu.make_async_copy(ins[a].at[mine], outs[a].at[mine], local_sems.at[a]) for a in range(self.n)]
        for k in range(1, N_DEV):
            px = (1 - x) if (k & 4) else x
            py = (1 - y) if (k & 2) else y
            pc = (1 - c) if (k & 1) else c
            for a in range(self.n):
                copies.append(pltpu.make_async_remote_copy(
                    src_ref=ins[a].at[_slot(px, py, pc)], dst_ref=outs[a].at[mine],
                    send_sem=send_sems.at[a * 7 + k - 1], recv_sem=recv_sems.at[a * 7 + k - 1],
                    device_id=(px, py, pc), device_id_type=MESH))
        return copies

    def start(self, ins, outs, sems):
        for cp in self._copies(ins, outs, sems):
            cp.start()

    def finish(self, ins, outs, sems):
        for cp in self._copies(ins, outs, sems):
            cp.wait()


def _fused_call(body, comm, operands, *, name, grid, in_specs, out_specs, out_shape, scratch_shapes=(),
                semantics=("arbitrary",)):
    n_in, n_out, n_scr = len(in_specs), len(out_specs), len(scratch_shapes)
    if comm is None:
        res = _pallas(body, name=name, grid=grid, in_specs=list(in_specs), out_specs=list(out_specs),
                      out_shape=list(out_shape), scratch_shapes=list(scratch_shapes),
                      compiler_params=_params(semantics))(*operands)
        return list(res), []
    k = comm.n
    steps = grid[0]

    def wrapped(*refs):
        ins, cins = refs[:n_in], refs[n_in:n_in + k]
        outs = refs[n_in + k:n_in + k + n_out]
        couts = refs[n_in + k + n_out:n_in + 2 * k + n_out]
        rest = refs[n_in + 2 * k + n_out:]
        scratch, sems = rest[:n_scr], rest[n_scr:]
        i = pl.program_id(0)

        @pl.when(i == 0)
        def _():
            comm.start(cins, couts, sems)

        body(*ins, *outs, *scratch)

        if comm.has_mid:
            @pl.when(i == int(steps * comm.mid_frac))
            def _():
                comm.mid(cins, couts, sems)

        @pl.when(i == steps - 1)
        def _():
            comm.finish(cins, couts, sems)

    any_spec = pl.BlockSpec(memory_space=pl.ANY)
    res = _pallas(wrapped, name=name, grid=grid, in_specs=list(in_specs) + [any_spec] * k,
                  out_specs=list(out_specs) + [any_spec] * k, out_shape=list(out_shape) + comm.out_shapes(),
                  scratch_shapes=list(scratch_shapes) + comm.sems(),
                  compiler_params=_params(("arbitrary",)))(*operands, *comm.arrs)
    return list(res[:n_out]), list(res[n_out:])


def _mod_part(c_all, ada_w):
    cols = ada_w.shape[2]

    def body(c_ref, w_ref, o_ref):
        cv = c_ref[...]
        cond = cv * _sigmoid(cv)
        for l in range(2):
            o_ref[l] = _dot(cond, w_ref[l])

    return _pallas(
        body, name="mod_part", grid=(1,),
        in_specs=[_full((N_DEV, D)), _full((2, D, cols))],
        out_specs=_full((2, N_DEV, cols)),
        out_shape=jax.ShapeDtypeStruct((2, N_DEV, cols), F32),
        compiler_params=_params(("arbitrary",)),
    )(c_all, ada_w)


def _ada_update(c_all, dmod_cols, dmod_all, ada_w, m_w, v_w, ada_b, m_b, v_b):
    cols = ada_w.shape[2]
    nb = ada_b.shape[1]

    def body(c_ref, dmc_ref, dma_ref, w_ref, mw_ref, vw_ref, b_ref, mb_ref, vb_ref,
             gw_ref, dw_ref, nmw_ref, nvw_ref, gb_ref, db_ref, nmb_ref, nvb_ref):
        cv = c_ref[...]
        cond = cv * _sigmoid(cv)
        for l in range(2):
            g = _dot_tn(cond, dmc_ref[l])
            gw_ref[l] = g
            dlt, m2, v2 = _adam(w_ref[l], g, mw_ref[l], vw_ref[l])
            dw_ref[l] = dlt
            nmw_ref[l] = m2
            nvw_ref[l] = v2
        gb = dma_ref[0]
        for i in range(1, N_DEV):
            gb = gb + dma_ref[i]
        gb_ref[...] = gb
        dlt, m2, v2 = _adam(b_ref[...], gb, mb_ref[...], vb_ref[...])
        db_ref[...] = dlt
        nmb_ref[...] = m2
        nvb_ref[...] = v2

    wspec = _full((2, D, cols))
    bspec = _full((2, nb))
    wshape = jax.ShapeDtypeStruct((2, D, cols), F32)
    bshape = jax.ShapeDtypeStruct((2, nb), F32)
    return _pallas(
        body, name="ada_update", grid=(1,),
        in_specs=[_full((N_DEV, D)), _full((2, N_DEV, cols)), _full((N_DEV, 2, nb)),
                  wspec, wspec, wspec, bspec, bspec, bspec],
        out_specs=[wspec] * 4 + [bspec] * 4,
        out_shape=[wshape] * 4 + [bshape] * 4,
        compiler_params=_params(("arbitrary",)),
    )(c_all, dmod_cols, dmod_all, ada_w, m_w, v_w, ada_b, m_b, v_b)


def _ev_in(x, mod, w_in, rc, rs1, rs2, comm=None):
    T = x.shape[0]

    def body(x_ref, mod_ref, w_ref, c_ref, s1_ref, s2_ref, q_ref, kv_ref, su_ref, sv_ref, g_ref):
        h = x_ref[...] * (1.0 + mod_ref[1:2, :]) + mod_ref[0:1, :]
        p = _dot(h, w_ref[...])
        c, s1, s2 = c_ref[...], s1_ref[...], s2_ref[...]
        for j in range(ATTN_W // LANE):
            qr = _rope_fwd(p[:, j * LANE:(j + 1) * LANE], c, s1, s2)
            q_ref[:, j * LANE:(j + 1) * LANE] = (qr * (HEAD_DIM ** -0.5)).astype(BF16)
        low = lax.broadcasted_iota(jnp.int32, (TMF, LANE), 1) < HEAD_DIM
        for j, val in enumerate((_rope_fwd(p[:, 512:640], c, s1, s2), p[:, 640:768])):
            swapped = pltpu.roll(val, HEAD_DIM, 1)
            tiles = (jnp.where(low, val, 0.0), jnp.where(low, 0.0, swapped),
                     jnp.where(low, swapped, 0.0), jnp.where(low, 0.0, val))
            for k, tile in enumerate(tiles):
                kv_ref[:, (4 * j + k) * LANE:(4 * j + k + 1) * LANE] = tile.astype(BF16)
        su_ref[...] = p[:, 768:1280].astype(BF16)
        sv_ref[...] = p[:, 1280:1792].astype(BF16)
        g_ref[...] = p[:, 1792:2816].astype(BF16)

    sh = lambda w: jax.ShapeDtypeStruct((T, w), BF16)
    return _fused_call(
        body, comm, (x, mod, w_in, rc, rs1, rs2), name="ev_in", grid=(T // TMF,),
        in_specs=[_tile(TMF, D), _full((3, D)), _full((D, EV_IN)), _tile(TMF, LANE), _tile(TMF, LANE),
                  _tile(TMF, LANE)],
        out_specs=[_tile(TMF, ATTN_W), _tile(TMF, KVX_W), _tile(TMF, SG_W), _tile(TMF, SG_W), _tile(TMF, D)],
        out_shape=[sh(ATTN_W), sh(KVX_W), sh(SG_W), sh(SG_W), sh(D)], semantics=("parallel",))


def _band_specs(width, nb):
    return [pl.BlockSpec((BLK, width), lambda n: (jnp.maximum(n - 1, 0), 0)),
            pl.BlockSpec((BLK, width), lambda n: (n, 0)),
            pl.BlockSpec((BLK, width), lambda n: (jnp.minimum(n + 1, nb - 1), 0))]


def _band_bias(bias_ref, n, nb):
    rows = lax.broadcasted_iota(jnp.int32, (3 * BLK, 1), 0)
    outside = ((rows < BLK) & (n == 0)) | ((rows >= 2 * BLK) & (n == nb - 1))
    return bias_ref[...] + jnp.where(outside, NEG_INF, 0.0)


def _lane_tile(ref, t):
    return ref[:, t * LANE:(t + 1) * LANE]


def _split_bf16(v):
    hi = v.astype(BF16)
    return hi, (v - hi.astype(F32)).astype(BF16)


def _group_mean(v, a_ref, exact_bf16=False):
    hi, lo = _split_bf16(v)
    a = a_ref[...]
    out = []
    for t in range(SG_W // (2 * LANE)):
        sl = slice(t * 2 * LANE, (t + 1) * 2 * LANE)
        r = jnp.dot(hi[:, sl], a, preferred_element_type=F32)
        if not exact_bf16:
            r = r + jnp.dot(lo[:, sl], a, preferred_element_type=F32)
        out.append(r)
    return jnp.concatenate(out, axis=-1)


def _sg_core(sv_ref, lng, lnb, a_ref, w_ref, bfull_ref):
    svf = sv_ref[...].astype(F32)
    xc = svf - _group_mean(svf, a_ref, exact_bf16=True)
    rstd = lax.rsqrt(_group_mean(xc * xc, a_ref) + LN_EPS)
    xhat = xc * rstd
    vb = (xhat * lng + lnb).astype(BF16)
    low = lax.broadcasted_iota(jnp.int32, (BLK, LANE), 1) < SG_DIM
    tiles = []
    for t in range(SG_W // LANE):
        v2 = vb[:, t * LANE:(t + 1) * LANE]
        r0 = jnp.dot(w_ref[2 * t], v2, preferred_element_type=F32)
        r1 = jnp.dot(w_ref[2 * t + 1], v2, preferred_element_type=F32)
        tiles.append(jnp.where(low, r0, r1))
    svm = jnp.concatenate(tiles, axis=-1) + bfull_ref[...]
    return xhat, rstd, vb, svm


def _mix0_fwd(q, kvx, su, sv, g0, sink_l, bias, a128, sg_lng, sg_lnb, sg_w, sg_bfull, comm=None):
    T = q.shape[0]
    nb = T // BLK

    def body(q_ref, kp_ref, kc_ref, kn_ref, su_ref, sv_ref, g_ref, sink_ref, bias_ref, a_ref, lng_ref, lnb_ref,
             w_ref, bfull_ref, ycat_ref, y0_ref, lse_ref):
        n = pl.program_id(0)
        bias = _band_bias(bias_ref, n, nb)
        kvx = jnp.concatenate([kp_ref[...], kc_ref[...], kn_ref[...]], axis=0)
        tiles = []
        for t in range(ATTN_W // LANE):
            qt = _lane_tile(q_ref, t)
            acc = None
            for par in range(2):
                h = 2 * t + par
                kt = 2 * (h // 4) + par
                ke = kvx[:, kt * LANE:(kt + 1) * LANE]
                ve = kvx[:, (4 + kt) * LANE:(5 + kt) * LANE]
                st = _dot_nt(ke, qt) + bias
                sk = _lane_tile(sink_ref, h)
                m = jnp.maximum(jnp.max(st, axis=0, keepdims=True), sk)
                p = jnp.exp(st - m)
                denom = jnp.sum(p, axis=0, keepdims=True) + jnp.exp(sk - m)
                contrib = _dot_tn(p * (1.0 / denom), ve)
                acc = contrib if acc is None else acc + contrib
                lse_ref[0, :, h * LANE:(h + 1) * LANE] = m + jnp.log(denom)
            tiles.append(acc)
        _, _, _, svm = _sg_core(sv_ref, lng_ref[...], lnb_ref[...], a_ref, w_ref, bfull_ref)
        tiles.append(su_ref[...].astype(F32) * svm)
        ycat = jnp.concatenate(tiles, axis=-1)
        gf = g_ref[...].astype(F32)
        ycat_ref[...] = ycat.astype(BF16)
        y0_ref[...] = (ycat * (gf * _sigmoid(gf))).astype(BF16)

    return _fused_call(
        body, comm, (q, kvx, kvx, kvx, su, sv, g0, sink_l, bias, a128, sg_lng, sg_lnb, sg_w, sg_bfull),
        name="mix0_fwd", grid=(nb,),
        in_specs=[_tile(BLK, ATTN_W)] + _band_specs(KVX_W, nb) + [
            _tile(BLK, SG_W), _tile(BLK, SG_W), _tile(BLK, D), _full((1, N_HEADS * LANE)), _full((3 * BLK, LANE)),
            _full((2 * LANE, 2 * LANE)),_full((1, SG_W)), _full((1, SG_W)), _full((SG_GROUPS, BLK, BLK)),
            _full((BLK, SG_W))],
        out_specs=[_tile(BLK, D), _tile(BLK, D), pl.BlockSpec((1, 1, N_HEADS * LANE), lambda n: (n, 0, 0))],
        out_shape=[jax.ShapeDtypeStruct((T, D), BF16), jax.ShapeDtypeStruct((T, D), BF16),
                   jax.ShapeDtypeStruct((nb, 1, N_HEADS * LANE), F32)], semantics=("parallel",))


def _ev_out(y0, w_out, x, mod, lnp):
    T = x.shape[0]

    def body(y_ref, w_ref, x_ref, mod_ref, ln_ref, out_ref, z_ref, x1_ref):
        out = _dot(y_ref[...], w_ref[...])
        z = ALPHA * x_ref[...] + mod_ref[2:3, :] * out
        x1, _, _ = _ln_fwd(z, ln_ref[0:1, :], ln_ref[1:2, :])
        out_ref[...] = out.astype(BF16)
        z_ref[...] = z
        x1_ref[...] = x1

    return _pallas(
        body, name="ev_out", grid=(T // TMF,),
        in_specs=[_tile(TMF, D), _full((D, D)), _tile(TMF, D), _full((3, D)), _full((2, D))],
        out_specs=[_tile(TMF, D)] * 3,
        out_shape=[jax.ShapeDtypeStruct((T, D), BF16), jax.ShapeDtypeStruct((T, D), F32),
                   jax.ShapeDtypeStruct((T, D), F32)],
        compiler_params=_params(("parallel",)),
    )(y0, w_out, x, mod, lnp)


def _od_in(x1, mod, w_in):
    T = x1.shape[0]

    def body(x_ref, mod_ref, w_ref, xr_ref, g_ref):
        h = x_ref[...] * (1.0 + mod_ref[1:2, :]) + mod_ref[0:1, :]
        p = _dot(h, w_ref[...])
        xr_ref[...] = p[:, :D]
        g_ref[...] = p[:, D:].astype(BF16)

    return _pallas(
        body, name="od_in", grid=(T // TMF,),
        in_specs=[_tile(TMF, D), _full((3, D)), _full((D, OD_IN))],
        out_specs=[_tile(TMF, D), _tile(TMF, D)],
        out_shape=[jax.ShapeDtypeStruct((T, D), F32), jax.ShapeDtypeStruct((T, D), BF16)],
        compiler_params=_params(("parallel",)),
    )(x1, mod, w_in)


def _ext_rows(prev_ref, cur, next_ref, j, n):
    prev = jnp.where(j > 0, prev_ref[...], 0.0)
    nxt = jnp.where(j < n - 1, next_ref[...], 0.0)
    return jnp.concatenate([prev, cur, nxt], axis=0)


def _shift_rows(ext, off, rows):
    total = ext.shape[0]
    if off == 0:
        return ext[SUBLANE:SUBLANE + rows, :]
    return pltpu.roll(ext, (-off) % total, 0)[SUBLANE:SUBLANE + rows, :]


def _conv_fwd(ext, cw, cb, rows):
    xc = cb
    for k in range(4):
        xc = xc + cw[k:k + 1, :] * _shift_rows(ext, k - 2, rows)
    return xc


def _gates(xc, wa_ref, wx_ref, ba, bx, lam):
    pr, pi = [], []
    for h in range(RNN_HEADS):
        xh = xc[:, h * RNN_HD:(h + 1) * RNN_HD].astype(BF16)
        pr.append(_dot(xh, wa_ref[h]))
        pi.append(_dot(xh, wx_ref[h]))
    r = _sigmoid(jnp.concatenate(pr, axis=-1) + ba)
    ig = _sigmoid(jnp.concatenate(pi, axis=-1) + bx)
    sp = jnp.maximum(-lam, 0.0) + jnp.log(1.0 + jnp.exp(-jnp.abs(lam)))
    neg_log_a = RG_C * r * sp
    a = jnp.exp(-neg_log_a)
    s2 = (1.0 + a * a) * jnp.tanh(neg_log_a)
    inv_s = lax.rsqrt(jnp.maximum(s2, 1e-30))
    return r, ig, sp, a, s2 * inv_s, inv_s


def _scan_tile(a_ref, b_ref, o_ref, carry_ref, rows, reverse):
    ridx = lax.broadcasted_iota(jnp.int32, (SUBLANE, D), 0)
    groups = rows // SUBLANE

    def group(gi, h):
        g = (groups - 1 - gi) if reverse else gi
        off = pl.multiple_of(g * SUBLANE, SUBLANE)
        a = a_ref[pl.ds(off, SUBLANE), :]
        b = b_ref[pl.ds(off, SUBLANE), :]
        for sh in (1, 2, 4):
            if reverse:
                keep = ridx < SUBLANE - sh
                a_p = jnp.where(keep, pltpu.roll(a, SUBLANE - sh, 0), 1.0)
                b_p = jnp.where(keep, pltpu.roll(b, SUBLANE - sh, 0), 0.0)
            else:
                keep = ridx >= sh
                a_p = jnp.where(keep, pltpu.roll(a, sh, 0), 1.0)
                b_p = jnp.where(keep, pltpu.roll(b, sh, 0), 0.0)
            b = b + a * b_p
            a = a * a_p
        hh = b + a * h
        o_ref[pl.ds(off, SUBLANE), :] = hh
        return hh[0:1, :] if reverse else hh[SUBLANE - 1:SUBLANE, :]

    carry_ref[...] = lax.fori_loop(0, groups, group, carry_ref[...])


def _rglru_fwd(xr, cw, cb, wa, wx, ba, bx, lam, reverse, name):
    T = xr.shape[0]
    n = T // TS
    prev_spec, next_spec = _halo_specs(TS, D, n, T, reverse)

    def body(prev_ref, cur_ref, next_ref, cw_ref, cb_ref, wa_ref, wx_ref, ba_ref, bx_ref, lam_ref,
             h_ref, a_ref, s_ref, r_ref, ig_ref, xc_ref, b_s, carry):
        i = pl.program_id(0)
        j = (n - 1 - i) if reverse else i

        @pl.when(i == 0)
        def _():
            carry[...] = jnp.zeros_like(carry)

        ext = _ext_rows(prev_ref, cur_ref[...], next_ref, j, n)
        xc = _conv_fwd(ext, cw_ref[...], cb_ref[...], TS)
        r, ig, _, a, s, _ = _gates(xc, wa_ref, wx_ref, ba_ref[...], bx_ref[...], lam_ref[...])
        s_ref[...] = s
        r_ref[...] = r.astype(BF16)
        ig_ref[...] = ig.astype(BF16)
        xc_ref[...] = xc.astype(BF16)
        a_ref[...] = a
        b_s[...] = s * ig * xc
        _scan_tile(a_ref, b_s, h_ref, carry, TS, reverse)

    wspec = _full((RNN_HEADS, RNN_HD, RNN_HD))
    cur = _rev_tile(TS, D, n, reverse)
    f32 = jax.ShapeDtypeStruct((T, D), F32)
    b16 = jax.ShapeDtypeStruct((T, D), BF16)
    return _pallas(
        body, name=name, grid=(n,),
        in_specs=[prev_spec, cur, next_spec, _full((4, D)), _full((1, D)),
                  wspec, wspec, _full((1, D)), _full((1, D)), _full((1, D))],
        out_specs=[cur] * 6,
        out_shape=[f32, f32, f32, b16, b16, b16],
        scratch_shapes=[pltpu.VMEM((TS, D), F32), pltpu.VMEM((1, D), F32)],
        compiler_params=_params(("arbitrary",)),
    )(xr, xr, xr, cw, cb, wa, wx, ba, bx, lam)


def _od_out(hf, hb, g1, w_out, x1, tgt, mod, lnp):
    T = x1.shape[0]

    def body(hf_ref, hb_ref, g_ref, w_ref, x_ref, t_ref, mod_ref, ln_ref,
             dh_ref, dg_ref, dx_ref, dwb_ref, vec_ref, dw_ref):
        i = pl.program_id(0)

        @pl.when(i == 0)
        def _():
            dw_ref[...] = jnp.zeros_like(dw_ref)
            vec_ref[...] = jnp.zeros_like(vec_ref)

        hs = hf_ref[...] + hb_ref[...]
        sg, dsg = _silu_and_grad(g_ref[...].astype(F32))
        yr = (hs * sg).astype(BF16)
        w = w_ref[...]
        out = _dot(yr, w)
        gate = mod_ref[2:3, :]
        z = ALPHA * x_ref[...] + gate * out
        lng = ln_ref[0:1, :]
        x2, xhat, rstd = _ln_fwd(z, lng, ln_ref[1:2, :])
        diff = x2 - t_ref[...]
        vec_ref[3:4, 0:LANE] += 0.5 * jnp.sum(diff * diff) * (1.0 / D)
        dx2 = diff * (1.0 / D)
        dz = _ln_bwd(dx2, xhat, rstd, lng)
        vec_ref[0:1, :] += _rowsum(dx2 * xhat)
        vec_ref[1:2, :] += _rowsum(dx2)
        vec_ref[2:3, :] += _rowsum(dz * out)
        dout = (dz * gate).astype(BF16)
        dyr = _dot_nt(dout, w)
        dw_ref[...] += _dot_tn(yr, dout)
        dh_ref[...] = dyr * sg
        dg_ref[...] = (dyr * hs * dsg).astype(BF16)
        dx_ref[...] = ALPHA * dz

        @pl.when(i == T // TM - 1)
        def _():
            dwb_ref[...] = dw_ref[...].astype(BF16)

    return _pallas(
        body, name="od_out", grid=(T // TM,),
        in_specs=[_tile(TM, D), _tile(TM, D), _tile(TM, D), _full((D, D)), _tile(TM, D), _tile(TM, D),
                  _full((3, D)), _full((2, D))],
        out_specs=[_tile(TM, D), _tile(TM, D), _tile(TM, D), _full((D, D)), _full((SUBLANE, D))],
        out_shape=[jax.ShapeDtypeStruct((T, D), F32), jax.ShapeDtypeStruct((T, D), BF16),
                   jax.ShapeDtypeStruct((T, D), F32), jax.ShapeDtypeStruct((D, D), BF16),
                   jax.ShapeDtypeStruct((SUBLANE, D), F32)],
        scratch_shapes=[pltpu.VMEM((D, D), F32)],
        compiler_params=_params(("arbitrary",)),
    )(hf, hb, g1, w_out, x1, tgt, mod, lnp)


def _rglru_bwd(fwd, dh, wa, wx, lam, reverse, name, comm=None):
    h, a_all, s_all, r_all, ig_all, xc_all = fwd
    T = h.shape[0]
    n = T // TS
    adj_rev = not reverse
    hprev_spec, hnext_spec = _halo_specs(TS, D, n, T, adj_rev)
    h_halo_spec = hnext_spec if reverse else hprev_spec

    def body(dh_ref, h_ref, hh_ref, a_ref, s_ref, r_ref, ig_ref, xc_ref, wa_ref, wx_ref, lam_ref,
             dxc_ref, dwa_ref, dwx_ref, vec_ref, a_s, l_s, carry, a_edge):
        i = pl.program_id(0)
        j = (n - 1 - i) if adj_rev else i

        @pl.when(i == 0)
        def _():
            carry[...] = jnp.zeros_like(carry)
            a_edge[...] = jnp.zeros_like(a_edge)
            dwa_ref[...] = jnp.zeros_like(dwa_ref)
            dwx_ref[...] = jnp.zeros_like(dwx_ref)
            vec_ref[...] = jnp.zeros_like(vec_ref)

        lam = lam_ref[...]
        sp = jnp.maximum(-lam, 0.0) + jnp.log(1.0 + jnp.exp(-jnp.abs(lam)))
        a, s = a_ref[...], s_ref[...]
        inv_s = lax.rsqrt(jnp.maximum(s * s, 1e-30))
        r, ig = r_ref[...].astype(F32), ig_ref[...].astype(F32)
        xcb = xc_ref[...]
        xc = xcb.astype(F32)

        rows = lax.broadcasted_iota(jnp.int32, (TS, D), 0)
        hcur = h_ref[...]
        if reverse:
            a_sh = jnp.where(rows == 0, a_edge[...], pltpu.roll(a, 1, 0))
            halo = jnp.where(j < n - 1, hh_ref[0:1, :], 0.0)
            h_nb = jnp.where(rows == TS - 1, halo, pltpu.roll(hcur, TS - 1, 0))
        else:
            a_sh = jnp.where(rows == TS - 1, a_edge[...], pltpu.roll(a, TS - 1, 0))
            halo = jnp.where(j > 0, hh_ref[SUBLANE - 1:SUBLANE, :], 0.0)
            h_nb = jnp.where(rows == 0, halo, pltpu.roll(hcur, 1, 0))
        a_s[...] = a_sh
        _scan_tile(a_s, dh_ref, l_s, carry, TS, adj_rev)
        a_edge[...] = a[TS - 1:TS, :] if reverse else a[0:1, :]

        lm = l_s[...]
        da = lm * h_nb
        di = lm * s * xc
        dxc = lm * s * ig
        ds = lm * ig * xc
        dlog_a = a * (da - ds * a * inv_s)
        dr = (-RG_C) * sp * dlog_a
        dsp = _rowsum((-RG_C) * r * dlog_a)
        dpr = dr * r * (1.0 - r)
        dpi = di * ig * (1.0 - ig)
        vec_ref[0:1, :] += _rowsum(dpr)
        vec_ref[1:2, :] += _rowsum(dpi)
        vec_ref[2:3, :] += dsp * (-_sigmoid(-lam))
        parts = []
        for hd in range(RNN_HEADS):
            sl = slice(hd * RNN_HD, (hd + 1) * RNN_HD)
            xh = xcb[:, sl]
            dprh = dpr[:, sl].astype(BF16)
            dpih = dpi[:, sl].astype(BF16)
            parts.append(_dot_nt(dprh, wa_ref[hd]) + _dot_nt(dpih, wx_ref[hd]))
            dwa_ref[hd] += _dot_tn(xh, dprh)
            dwx_ref[hd] += _dot_tn(xh, dpih)
        dxc_ref[...] = dxc + jnp.concatenate(parts, axis=-1)

    wspec = _full((RNN_HEADS, RNN_HD, RNN_HD))
    cur = _rev_tile(TS, D, n, adj_rev)
    return _fused_call(
        body, comm, (dh, h, h, a_all, s_all, r_all, ig_all, xc_all, wa, wx, lam), name=name, grid=(n,),
        in_specs=[cur, cur, h_halo_spec, cur, cur, cur, cur, cur, wspec, wspec, _full((1, D))],
        out_specs=[cur, wspec, wspec, _full((SUBLANE, D))],
        out_shape=[jax.ShapeDtypeStruct((T, D), F32),
                   jax.ShapeDtypeStruct((RNN_HEADS, RNN_HD, RNN_HD), F32),
                   jax.ShapeDtypeStruct((RNN_HEADS, RNN_HD, RNN_HD), F32),
                   jax.ShapeDtypeStruct((SUBLANE, D), F32)],
        scratch_shapes=[pltpu.VMEM((TS, D), F32)] * 2 + [pltpu.VMEM((1, D), F32)] * 2)


def _od_in_bwd(dxcf, dxcb, xr, dg1, x1, dx1p, mod, w_in, cw, comm=None):
    T = x1.shape[0]
    n = T // TM
    slab = OD_IN // N_DEV
    prev_spec, next_spec = _halo_specs(TM, D, n, T, False)

    def body(fp_ref, fc_ref, fn_ref, bp_ref, bc_ref, bn_ref, xp_ref, xc_ref, xn_ref, dg_ref, x1_ref, dxp_ref,
             mod_ref, w_ref, cw_ref, dx_ref, dwb_ref, vec_ref, dw_ref):
        i = pl.program_id(0)

        @pl.when(i == 0)
        def _():
            dw_ref[...] = jnp.zeros_like(dw_ref)
            vec_ref[...] = jnp.zeros_like(vec_ref)

        dcur = fc_ref[...] + bc_ref[...]
        dprev = jnp.where(i > 0, fp_ref[...] + bp_ref[...], 0.0)
        dnext = jnp.where(i < n - 1, fn_ref[...] + bn_ref[...], 0.0)
        dext = jnp.concatenate([dprev, dcur, dnext], axis=0)
        xext = _ext_rows(xp_ref, xc_ref[...], xn_ref, i, n)
        cw_v = cw_ref[...]
        dxr = None
        for k in range(4):
            term = cw_v[k:k + 1, :] * _shift_rows(dext, 2 - k, TM)
            dxr = term if dxr is None else dxr + term
            vec_ref[k:k + 1, :] += _rowsum(dcur * _shift_rows(xext, k - 2, TM))
        vec_ref[4:5, :] += _rowsum(dcur)
        dp = jnp.concatenate([dxr.astype(BF16), dg_ref[...]], axis=-1)
        x1v = x1_ref[...]
        scale1 = 1.0 + mod_ref[1:2, :]
        h1 = (x1v * scale1 + mod_ref[0:1, :]).astype(BF16)
        dh1 = _dot_nt(dp, w_ref[...])
        dw_ref[...] += _dot_tn(h1, dp)
        dx_ref[...] = dxp_ref[...] + dh1 * scale1
        vec_ref[5:6, :] += _rowsum(dh1)
        vec_ref[6:7, :] += _rowsum(dh1 * x1v)

        @pl.when(i == n - 1)
        def _():
            for j in range(N_DEV):
                dwb_ref[j] = dw_ref[:, j * slab:(j + 1) * slab].astype(BF16)

    t = _tile(TM, D)
    return _fused_call(
        body, comm, (dxcf, dxcf, dxcf, dxcb, dxcb, dxcb, xr, xr, xr, dg1, x1, dx1p, mod, w_in, cw),
        name="od_in_bwd", grid=(n,),
        in_specs=[prev_spec, t, next_spec, prev_spec, t, next_spec, prev_spec, t, next_spec, t, t, t,
                  _full((3, D)), _full((D, OD_IN)), _full((4, D))],
        out_specs=[t, _full((N_DEV, D, slab)), _full((SUBLANE, D))],
        out_shape=[jax.ShapeDtypeStruct((T, D), F32), jax.ShapeDtypeStruct((N_DEV, D, slab), BF16),
                   jax.ShapeDtypeStruct((SUBLANE, D), F32)],
        scratch_shapes=[pltpu.VMEM((D, OD_IN), F32)])


def _ev_out_bwd(dx1, z0, out0, y0, ycat, g0, w_out, mod, lnp):
    T = dx1.shape[0]

    def body(dx_ref, z_ref, out_ref, y0_ref, yc_ref, g_ref, w_ref, mod_ref, ln_ref,
             dxp_ref, dyc_ref, dg_ref, dwb_ref, vec_ref, dw_ref):
        i = pl.program_id(0)

        @pl.when(i == 0)
        def _():
            dw_ref[...] = jnp.zeros_like(dw_ref)
            vec_ref[...] = jnp.zeros_like(vec_ref)

        lng = ln_ref[0:1, :]
        _, xhat, rstd = _ln_fwd(z_ref[...], lng, ln_ref[1:2, :])
        dy = dx_ref[...]
        dz = _ln_bwd(dy, xhat, rstd, lng)
        vec_ref[0:1, :] += _rowsum(dy * xhat)
        vec_ref[1:2, :] += _rowsum(dy)
        vec_ref[2:3, :] += _rowsum(dz * out_ref[...].astype(F32))
        dout = (dz * mod_ref[2:3, :]).astype(BF16)
        dy0 = _dot_nt(dout, w_ref[...])
        dw_ref[...] += _dot_tn(y0_ref[...], dout)
        sg, dsg = _silu_and_grad(g_ref[...].astype(F32))
        dyc_ref[...] = (dy0 * sg).astype(BF16)
        dg_ref[...] = (dy0 * yc_ref[...].astype(F32) * dsg).astype(BF16)
        dxp_ref[...] = ALPHA * dz

        @pl.when(i == T // TM - 1)
        def _():
            dwb_ref[...] = dw_ref[...].astype(BF16)

    t = _tile(TM, D)
    return _pallas(
        body, name="ev_out_bwd", grid=(T // TM,),
        in_specs=[t, t, t, t, t, t, _full((D, D)), _full((3, D)), _full((2, D))],
        out_specs=[t, t, t, _full((D, D)), _full((SUBLANE, D))],
        out_shape=[jax.ShapeDtypeStruct((T, D), F32), jax.ShapeDtypeStruct((T, D), BF16),
                   jax.ShapeDtypeStruct((T, D), BF16), jax.ShapeDtypeStruct((D, D), BF16),
                   jax.ShapeDtypeStruct((SUBLANE, D), F32)],
        scratch_shapes=[pltpu.VMEM((D, D), F32)],
        compiler_params=_params(("arbitrary",)),
    )(dx1, z0, out0, y0, ycat, g0, w_out, mod, lnp)


def _mix0_bwd(q, kvx, lse, dyc, ycat, su, sv, sink_l, bias, a128, gsum, sel, sg_lng, sg_lnb, sg_w, sg_bfull,
              rc, rs1, rs2, comm=None):
    T = q.shape[0]
    nb = T // BLK

    def body(q_ref, kp_ref, kc_ref, kn_ref, lse_ref, dyc_ref, yc_ref, su_ref, sv_ref, sink_ref, bias_ref, a_ref,
             gsum_ref, sel_ref, lng_ref, lnb_ref, w_ref, bfull_ref, c_ref, s1_ref, s2_ref,
             dq_ref, dkv_ref, dsu_ref, dsv_ref, dw_ref, dbt_ref, vec_ref, dsink_ref):
        n = pl.program_id(0)

        @pl.when(n == 0)
        def _():
            dkv_ref[...] = jnp.zeros_like(dkv_ref)
            dw_ref[...] = jnp.zeros_like(dw_ref)
            dbt_ref[...] = jnp.zeros_like(dbt_ref)
            vec_ref[...] = jnp.zeros_like(vec_ref)
            dsink_ref[...] = jnp.zeros_like(dsink_ref)

        band = pl.ds(pl.multiple_of(n * BLK + (TM - BLK), BLK), 3 * BLK)
        bias = _band_bias(bias_ref, n, nb)
        kvx = jnp.concatenate([kp_ref[...], kc_ref[...], kn_ref[...]], axis=0)
        bias2 = jnp.concatenate([bias, bias], axis=1)
        low = lax.broadcasted_iota(jnp.int32, (BLK, LANE), 1) < HEAD_DIM
        low2 = lax.broadcasted_iota(jnp.int32, (2 * BLK, LANE), 1) < HEAD_DIM
        sel = sel_ref[...]
        c, s1, s2 = c_ref[...], s1_ref[...], s2_ref[...]
        for kvh in range(2):
            t0, t1 = 2 * kvh, 2 * kvh + 1
            q2 = jnp.concatenate([_lane_tile(q_ref, t0), _lane_tile(q_ref, t1)], axis=0)
            do2 = jnp.concatenate([_lane_tile(dyc_ref, t0), _lane_tile(dyc_ref, t1)], axis=0)
            yc2 = jnp.concatenate([_lane_tile(yc_ref, t0), _lane_tile(yc_ref, t1)], axis=0)
            p_hi, p_lo = _split_bf16(do2.astype(F32) * yc2.astype(F32))
            deltas = _dot_nt(sel, p_hi) + _dot_nt(sel, p_lo)
            dkx = jnp.zeros((3 * BLK, LANE), F32)
            dvx = jnp.zeros((3 * BLK, LANE), F32)
            dq_acc = None
            for par in range(2):
                heads = (4 * kvh + par, 4 * kvh + 2 + par)
                kt = 2 * kvh + par
                ke = kvx[:, kt * LANE:(kt + 1) * LANE]
                ve = kvx[:, (4 + kt) * LANE:(5 + kt) * LANE]
                lse = jnp.concatenate([lse_ref[0, :, h * LANE:(h + 1) * LANE] for h in heads], axis=1)
                sk = jnp.concatenate([_lane_tile(sink_ref, h) for h in heads], axis=1)
                delta = deltas[par:par + 1, :]
                pt = jnp.exp(_dot_nt(ke, q2) + bias2 - lse)
                dst = (pt * (_dot_nt(ve, do2) - delta)).astype(BF16)
                sink_terms = jnp.exp(sk - lse) * delta
                for k, h in enumerate(heads):
                    dsink_ref[:, h * LANE:(h + 1) * LANE] += sink_terms[:, k * LANE:(k + 1) * LANE]
                part = _dot_tn(dst, ke)
                dq_acc = part if dq_acc is None else dq_acc + part
                mine = low2 if par == 0 else jnp.logical_not(low2)
                dkx = dkx + jnp.dot(dst, jnp.where(mine, q2, jnp.zeros_like(q2)), preferred_element_type=F32)
                dvx = dvx + jnp.dot(pt.astype(BF16), jnp.where(mine, do2, jnp.zeros_like(do2)),
                                    preferred_element_type=F32)
            for k, t in enumerate((t0, t1)):
                dq_t = dq_acc[k * BLK:(k + 1) * BLK] * (HEAD_DIM ** -0.5)
                dq_ref[:, t * LANE:(t + 1) * LANE] = _rope_bwd(dq_t, c, s1, s2).astype(BF16)
            dkv_ref[band, kvh * LANE:(kvh + 1) * LANE] += dkx
            dkv_ref[band, (2 + kvh) * LANE:(3 + kvh) * LANE] += dvx

        lng = lng_ref[...]
        xhat, rstd, vb, svm = _sg_core(sv_ref, lng, lnb_ref[...], a_ref, w_ref, bfull_ref)
        dy = dyc_ref[:, ATTN_W:].astype(F32)
        dsu_ref[...] = (dy * svm).astype(BF16)
        dsvm = dy * su_ref[...].astype(F32)
        d_hi, d_lo = _split_bf16(dsvm)
        gsum = gsum_ref[...]
        dbt_ref[...] += jnp.dot(d_hi, gsum, preferred_element_type=F32) + jnp.dot(d_lo, gsum,
                                                                                 preferred_element_type=F32)
        tiles = []
        for t in range(SG_W // LANE):
            tl = slice(t * LANE, (t + 1) * LANE)
            dt, v2 = d_hi[:, tl], vb[:, tl]
            dw_ref[2 * t] += _dot_nt(jnp.where(low, dt, jnp.zeros_like(dt)), v2)
            dw_ref[2 * t + 1] += _dot_nt(jnp.where(low, jnp.zeros_like(dt), dt), v2)
            tiles.append(jnp.where(low, _dot_tn(w_ref[2 * t], dt), _dot_tn(w_ref[2 * t + 1], dt)))
        dvgn = jnp.concatenate(tiles, axis=-1)
        vec_ref[0:1, :] += _rowsum(dvgn * xhat)
        vec_ref[1:2, :] += _rowsum(dvgn)
        dxh = dvgn * lng
        m1 = _group_mean(dxh, a_ref)
        m2 = _group_mean(dxh * xhat, a_ref)
        dsv_ref[...] = (rstd * (dxh - m1 - xhat * m2)).astype(BF16)

    return _fused_call(
        body, comm, (q, kvx, kvx, kvx, lse, dyc, ycat, su, sv, sink_l, bias, a128, gsum, sel, sg_lng, sg_lnb, sg_w,
                     sg_bfull, rc, rs1, rs2),
        name="mix0_bwd", grid=(nb,),
        in_specs=[_tile(BLK, ATTN_W)] + _band_specs(KVX_W, nb) + [
            pl.BlockSpec((1, 1, N_HEADS * LANE), lambda n: (n, 0, 0)), _tile(BLK, D), _tile(BLK, D),
            _tile(BLK, SG_W), _tile(BLK, SG_W), _full((1, N_HEADS * LANE)), _full((3 * BLK, LANE)),
            _full((2 * LANE, 2 * LANE)),_full((SG_W, LANE)), _full((SUBLANE, LANE)), _full((1, SG_W)), _full((1, SG_W)),
            _full((SG_GROUPS, BLK, BLK)), _full((BLK, SG_W)), _tile(BLK, LANE), _tile(BLK, LANE), _tile(BLK, LANE)],
        out_specs=[_tile(BLK, ATTN_W), _full((T + 2 * TM, 4 * LANE)), _tile(BLK, SG_W), _tile(BLK, SG_W),
                   _full((SG_GROUPS, BLK, BLK)), _full((BLK, LANE)), _full((SUBLANE, SG_W)),
                   _full((1, N_HEADS * LANE))],
        out_shape=[jax.ShapeDtypeStruct((T, ATTN_W), BF16), jax.ShapeDtypeStruct((T + 2 * TM, 4 * LANE), F32),
                   jax.ShapeDtypeStruct((T, SG_W), BF16), jax.ShapeDtypeStruct((T, SG_W), BF16),
                   jax.ShapeDtypeStruct((SG_GROUPS, BLK, BLK), F32), jax.ShapeDtypeStruct((BLK, LANE), F32),
                   jax.ShapeDtypeStruct((SUBLANE, SG_W), F32), jax.ShapeDtypeStruct((1, N_HEADS * LANE), F32)])


def _ev_in_bwd(dq, dkv, dsu, dsv, dg0, x, dxp, mod, w_in, rc, rs1, rs2, comm=None):
    T = x.shape[0]

    def body(dq_ref, dkv_ref, dsu_ref, dsv_ref, dg_ref, x_ref, dxp_ref, mod_ref, w_ref, c_ref, s1_ref, s2_ref,
             dx_ref, dwb_ref, vec_ref, dw_ref):
        i = pl.program_id(0)

        @pl.when(i == 0)
        def _():
            dw_ref[...] = jnp.zeros_like(dw_ref)
            vec_ref[...] = jnp.zeros_like(vec_ref)

        low = lax.broadcasted_iota(jnp.int32, (TM, LANE), 1) < HEAD_DIM

        def fold(j):
            t0 = dkv_ref[:, (2 * j) * LANE:(2 * j + 1) * LANE]
            t1 = dkv_ref[:, (2 * j + 1) * LANE:(2 * j + 2) * LANE]
            return jnp.where(low, t0 + pltpu.roll(t0, HEAD_DIM, 1), t1 + pltpu.roll(t1, HEAD_DIM, 1))

        dk = _rope_bwd(fold(0), c_ref[...], s1_ref[...], s2_ref[...]).astype(BF16)
        dp = jnp.concatenate([dq_ref[...], dk, fold(1).astype(BF16), dsu_ref[...], dsv_ref[...],
                              dg_ref[...]], axis=-1)
        xv = x_ref[...]
        scale0 = 1.0 + mod_ref[1:2, :]
        h0 = (xv * scale0 + mod_ref[0:1, :]).astype(BF16)
        dh0 = _dot_nt(dp, w_ref[...])
        dw_ref[...] += _dot_tn(h0, dp)
        dx_ref[...] = dxp_ref[...] + dh0 * scale0
        vec_ref[0:1, :] += _rowsum(dh0)
        vec_ref[1:2, :] += _rowsum(dh0 * xv)

        @pl.when(i == T // TM - 1)
        def _():
            dwb_ref[...] = dw_ref[...].astype(BF16)

    t = _tile(TM, D)
    return _fused_call(
        body, comm, (dq, dkv, dsu, dsv, dg0, x, dxp, mod, w_in, rc, rs1, rs2), name="ev_in_bwd", grid=(T // TM,),
        in_specs=[_tile(TM, ATTN_W), pl.BlockSpec((TM, 4 * LANE), lambda i: (i + 1, 0)), _tile(TM, SG_W),
                  _tile(TM, SG_W), t, t, t,
                  _full((3, D)), _full((D, EV_IN)), _tile(TM, LANE), _tile(TM, LANE), _tile(TM, LANE)],
        out_specs=[t, _full((D, EV_IN)), _full((SUBLANE, D))],
        out_shape=[jax.ShapeDtypeStruct((T, D), F32), jax.ShapeDtypeStruct((D, EV_IN), BF16),
                   jax.ShapeDtypeStruct((SUBLANE, D), F32)],
        scratch_shapes=[pltpu.VMEM((D, EV_IN), F32)])


def _sum_slots(land_ref):
    g = land_ref[0].astype(F32)
    for i in range(1, land_ref.shape[0]):
        g = g + land_ref[i].astype(F32)
    return g


def _reduce_adam(land, w, m, v, name):
    R, C = w.shape
    rb = R
    for cand in (512, 256, 128, 64, 32, 16, 8):
        if R % cand == 0:
            rb = cand
            break

    def body(l_ref, w_ref, m_ref, v_ref, g_ref, d_ref, nm_ref, nv_ref):
        g = _sum_slots(l_ref)
        g_ref[...] = g
        dlt, m2, v2 = _adam(w_ref[...], g, m_ref[...], v_ref[...])
        d_ref[...] = dlt
        nm_ref[...] = m2
        nv_ref[...] = v2

    t = pl.BlockSpec((rb, C), lambda i: (i, 0))
    shp = jax.ShapeDtypeStruct((R, C), F32)
    return _pallas(
        body, name=name, grid=(R // rb,),
        in_specs=[pl.BlockSpec((land.shape[0], rb, C), lambda i: (0, i, 0)), t, t, t],
        out_specs=[t] * 4, out_shape=[shp] * 4,
        compiler_params=_params(("parallel",)),
    )(land, w, m, v)


def _tail_exchange(slabs, small):
    _, R, C = slabs.shape
    n_chips = N_DEV // 2
    gather = _GatherComm(small)
    ns = gather.n

    def body(*refs):
        slab_ref = refs[0]
        g_ins = refs[1:1 + ns]
        land_ref = refs[1 + ns]
        g_outs = refs[2 + ns:2 + 2 * ns]
        stage, part, s1_send, s1_recv, s2_send, s2_recv = refs[2 + 2 * ns:8 + 2 * ns]
        g_sems = refs[8 + 2 * ns:]
        x, y, c = _my_pos()
        chip = 2 * x + y
        gather.start(g_ins, g_outs, g_sems)

        swaps = [pltpu.make_async_remote_copy(
            src_ref=slab_ref.at[2 * k + (1 - c)], dst_ref=stage.at[k], send_sem=s1_send.at[k],
            recv_sem=s1_recv.at[k], device_id=(x, y, 1 - c), device_id_type=MESH) for k in range(n_chips)]
        for cp in swaps:
            cp.start()
        for cp in swaps:
            cp.wait()
        for k in range(n_chips):
            part[k] = (slab_ref[2 * k + c].astype(F32) + stage[k].astype(F32)).astype(BF16)

        gather.mid(g_ins, g_outs, g_sems)

        sends = []
        for r in range(1, n_chips):
            px = (1 - x) if (r & 2) else x
            py = (1 - y) if (r & 1) else y
            sends.append(pltpu.make_async_remote_copy(
                src_ref=part.at[2 * px + py], dst_ref=land_ref.at[chip], send_sem=s2_send.at[r - 1],
                recv_sem=s2_recv.at[r - 1], device_id=(px, py, c), device_id_type=MESH))
        for cp in sends:
            cp.start()
        land_ref[chip] = part[chip]
        for cp in sends:
            cp.wait()
        gather.finish(g_ins, g_outs, g_sems)

    any_spec = pl.BlockSpec(memory_space=pl.ANY)
    vmem_spec = pl.BlockSpec(memory_space=pltpu.VMEM)
    res = _pallas(
        body, name="tail_exchange",
        out_shape=[jax.ShapeDtypeStruct((n_chips, R, C), BF16)] + gather.out_shapes(),
        in_specs=[vmem_spec] + [any_spec] * ns, out_specs=[vmem_spec] + [any_spec] * ns,
        scratch_shapes=[pltpu.VMEM((n_chips, R, C), BF16), pltpu.VMEM((n_chips, R, C), BF16),
                        pltpu.SemaphoreType.DMA((n_chips,)), pltpu.SemaphoreType.DMA((n_chips,)),
                        pltpu.SemaphoreType.DMA((n_chips - 1,)), pltpu.SemaphoreType.DMA((n_chips - 1,))]
        + gather.sems(),
        compiler_params=pltpu.CompilerParams(vmem_limit_bytes=VMEM_LIMIT),
    )(slabs, *gather.arrs)
    return res[0], list(res[1:])


def _slots_adam(land, w, m, v, name):
    lead = w.shape[1] if w.ndim == 5 else 1
    inner = w.shape[-3:]
    zeros3 = (0, 0, 0)
    if w.ndim == 5:
        lspec = pl.BlockSpec((N_DEV, 1) + inner, lambda i: (0, i) + zeros3)
        wspec = pl.BlockSpec((1, 1) + inner, lambda i: (0, i) + zeros3)
    else:
        lspec = pl.BlockSpec((N_DEV,) + inner, lambda i: (0,) + zeros3)
        wspec = pl.BlockSpec((1,) + inner, lambda i: (0,) + zeros3)

    def body(l_ref, w_ref, m_ref, v_ref, g_ref, d_ref, nm_ref, nv_ref):
        at = (0, 0) if w.ndim == 5 else (0,)
        g = l_ref[(0,) + at[1:]].astype(F32)
        for i in range(1, N_DEV):
            g = g + l_ref[(i,) + at[1:]].astype(F32)
        dlt, m2, v2 = _adam(w_ref[at], g, m_ref[at], v_ref[at])
        g_ref[at] = g
        d_ref[at] = dlt
        nm_ref[at] = m2
        nv_ref[at] = v2

    shp = jax.ShapeDtypeStruct(w.shape, F32)
    return _pallas(
        body, name=name, grid=(lead,),
        in_specs=[lspec, wspec, wspec, wspec], out_specs=[wspec] * 4, out_shape=[shp] * 4,
        compiler_params=_params(("parallel",)),
    )(land, w, m, v)


SMALL_PARAMS = ("ln_g", "ln_b", "ev_sg_ln_g", "ev_sg_ln_b", "ev_sink", "ev_sg_b",
                "od_conv_w", "od_conv_b", "od_b_a", "od_b_x", "od_lam")


def _small_update(ga, gc, gd, gf, gb, ge, gsink, gbt, params):
    names = list(SMALL_PARAMS)
    flat = [a for nm in names for a in params[nm]]
    n_g = 8

    def body(*refs):
        ga_ref, gc_ref, gd_ref, gf_ref, gb_ref, ge_ref, gs_ref, gbt_ref = refs[:n_g]
        prm = refs[n_g:n_g + 3 * len(names)]
        loss_ref = refs[n_g + 3 * len(names)]
        outs = refs[n_g + 3 * len(names) + 1:]

        def ssum(ref):
            acc = ref[0]
            for i in range(1, N_DEV):
                acc = acc + ref[i]
            return acc

        a, cc, dd, ff, bb, ee = ssum(ga_ref), ssum(gc_ref), ssum(gd_ref), ssum(gf_ref), ssum(gb_ref), ssum(ge_ref)
        loss_ref[...] = a[3:4, 0:LANE]
        me = _slot(*_my_pos())

        def mine(rows):
            acc = jnp.zeros((rows.shape[0], LANE), F32)
            for j in range(N_DEV):
                acc = acc + jnp.where(me == j, rows[:, j * LANE:(j + 1) * LANE], 0.0)
            return acc

        sink_terms = ssum(gs_ref)
        lane8 = lax.broadcasted_iota(jnp.int32, (1, N_HEADS), 1)
        g_sink = jnp.zeros((1, N_HEADS), F32)
        for h in range(N_HEADS):
            tot = -jnp.sum(sink_terms[:, h * LANE:(h + 1) * LANE], axis=1, keepdims=True)
            g_sink = jnp.where(lane8 == h, tot, g_sink)
        grads = dict(
            ln_g=jnp.concatenate([dd[0:1], a[0:1]], axis=0), ln_b=jnp.concatenate([dd[1:2], a[1:2]], axis=0),
            ev_sg_ln_g=ee[0:1], ev_sg_ln_b=ee[1:2], ev_sink=g_sink,
            ev_sg_b=jnp.transpose(ssum(gbt_ref))[0:SG_GROUPS, :],
            od_conv_w=mine(cc[0:4]), od_conv_b=mine(cc[4:5]),
            od_b_a=mine(jnp.concatenate([ff[0:1], bb[0:1]], axis=0)),
            od_b_x=mine(jnp.concatenate([ff[1:2], bb[1:2]], axis=0)),
            od_lam=mine(jnp.concatenate([ff[2:3], bb[2:3]], axis=0)))
        for k, nm in enumerate(names):
            w_ref, m_ref, v_ref = prm[3 * k:3 * k + 3]
            at = (0,) if len(w_ref.shape) == 3 else ()
            g = grads[nm]
            dlt, m2, v2 = _adam(w_ref[at] if at else w_ref[...], g, m_ref[at] if at else m_ref[...],
                                v_ref[at] if at else v_ref[...])
            for o_ref, val in zip(outs[4 * k:4 * k + 4], (g, dlt, m2, v2)):
                if at:
                    o_ref[at] = val
                else:
                    o_ref[...] = val

    gathered = [ga, gc, gd, gf, gb, ge, gsink, gbt]
    out_shape = [jax.ShapeDtypeStruct((1, LANE), F32)]
    for nm in names:
        out_shape += [jax.ShapeDtypeStruct(params[nm][0].shape, F32)] * 4
    return _pallas(
        body, name="small_update", grid=(1,),
        in_specs=[_full(a.shape) for a in gathered + flat],
        out_specs=[_full(s.shape) for s in out_shape], out_shape=out_shape,
        compiler_params=_params(("arbitrary",)),
    )(*gathered, *flat)


VEC_ROWS = 16
VEC_LAYOUT = (("od_conv_w", 4), ("od_conv_b", 1), ("od_b_a", 2), ("od_b_x", 2), ("od_lam", 2))


def _pack_vec(parts):
    rows = [parts[name].reshape(nrows, -1) for name, nrows in VEC_LAYOUT]
    used = sum(r for _, r in VEC_LAYOUT)
    rows.append(jnp.zeros((VEC_ROWS - used, rows[0].shape[1]), F32))
    return jnp.concatenate(rows, axis=0)


def _to_slabs(full, cols_per):
    R = full.shape[0]
    return full.reshape(R, N_DEV, cols_per).transpose(1, 0, 2)


def _from_slabs(slabs):
    n, R, cp = slabs.shape
    return slabs.transpose(1, 0, 2).reshape(R, n * cp)


def kernel(x, c, positions, ada_w, ada_b, ln_g, ln_b, ev_w_in, ev_w_out, ev_sink, ev_sg_ln_g, ev_sg_ln_b, ev_sg_w, ev_sg_b, od_w_in, od_conv_w, od_conv_b, od_w_a, od_b_a, od_w_x, od_b_x, od_lam, od_w_out, loss_target, m_ada_w, m_ada_b, m_ln_g, m_ln_b, m_ev_w_in, m_ev_w_out, m_ev_sink, m_ev_sg_ln_g, m_ev_sg_ln_b, m_ev_sg_w, m_ev_sg_b, m_od_w_in, m_od_conv_w, m_od_conv_b, m_od_w_a, m_od_b_a, m_od_w_x, m_od_b_x, m_od_lam, m_od_w_out, v_ada_w, v_ada_b, v_ln_g, v_ln_b, v_ev_w_in, v_ev_w_out, v_ev_sink, v_ev_sg_ln_g, v_ev_sg_ln_b, v_ev_sg_w, v_ev_sg_b, v_od_w_in, v_od_conv_w, v_od_conv_b, v_od_w_a, v_od_b_a, v_od_w_x, v_od_b_x, v_od_lam, v_od_w_out):
    T = x.shape[1]
    me = _slot(*_my_pos())
    xs = x.reshape(T, D)
    tgt = loss_target.reshape(T, D)

    vec_w = _pack_vec(dict(od_conv_w=od_conv_w[0], od_conv_b=od_conv_b, od_b_a=od_b_a[0], od_b_x=od_b_x[0],
                           od_lam=od_lam[0]))
    c_all, g_ev_in, g_vec = _all_gather([c, ev_w_in[0].astype(BF16), vec_w], "ag_params")
    c_all = c_all.reshape(N_DEV, D)
    w_ev_in = _from_slabs(g_ev_in)
    vec_full = _from_slabs(g_vec)
    cw, cb = vec_full[0:4], vec_full[4:5]
    ba, bx, lam = vec_full[5:7], vec_full[7:9], vec_full[9:11]

    mod_part = _mod_part(c_all, ada_w)
    (mod_all,) = _all_gather([mod_part], "ag_mod")
    mod_mine = lax.dynamic_index_in_dim(mod_all, me, axis=2, keepdims=False)
    mod = mod_mine.transpose(1, 0, 2).reshape(2, 3 * D) + ada_b
    mod0 = mod[0].reshape(3, D)
    mod1 = mod[1].reshape(3, D)

    half = 8
    inv_freq = jnp.power(jnp.float32(ROPE_THETA), -jnp.arange(half, dtype=F32) / half)
    ang = positions.reshape(T).astype(F32)[:, None] * inv_freq
    cos_t = jnp.tile(jnp.cos(ang), (1, LANE // half))
    sin_t = jnp.tile(jnp.sin(ang), (1, LANE // half))
    l64 = jnp.arange(LANE) % HEAD_DIM
    rc = jnp.where(l64 < 2 * half, cos_t, 1.0)
    rs1 = jnp.where(l64 < half, -sin_t, 0.0)
    rs2 = jnp.where((l64 >= half) & (l64 < 2 * half), sin_t, 0.0)

    ln0 = jnp.stack([ln_g[0], ln_b[0]])
    ln1 = jnp.stack([ln_g[1], ln_b[1]])
    sg_lng = ev_sg_ln_g
    sg_lnb = ev_sg_ln_b
    sg_w = ev_sg_w[0].astype(BF16)
    sg_bfull = jnp.repeat(ev_sg_b[0].T, SG_DIM, axis=1)
    sink_l = jnp.repeat(ev_sink, LANE, axis=1)
    kj = jnp.arange(3 * BLK)[:, None]
    qi = jnp.arange(BLK)[None, :]
    band_bias = jnp.where(jnp.abs(kj - BLK - qi) <= BLK, 0.0, NEG_INF).astype(F32)
    lanes = jnp.arange(LANE)
    lanes2 = jnp.arange(2 * LANE)
    a128 = jnp.where(lanes2[:, None] // SG_DIM == lanes2[None, :] // SG_DIM, 1.0 / SG_DIM, 0.0).astype(BF16)
    gsum = (jnp.arange(SG_W)[:, None] // SG_DIM == lanes[None, :]).astype(BF16)
    sel = (jnp.arange(SUBLANE)[:, None] == lanes[None, :] // HEAD_DIM).astype(BF16)
    wa = od_w_a[0].astype(BF16)
    wx = od_w_x[0].astype(BF16)

    (q, kvx, su, sv, g0), (g_ev_out,) = _ev_in(xs, mod0, w_ev_in, rc, rs1, rs2,
                                               _GatherComm([ev_w_out[0].astype(BF16)]))
    w_ev_out = g_ev_out.reshape(D, D)
    (ycat, y0, lse), (g_od_in, g_od_out) = _mix0_fwd(
        q, kvx, su, sv, g0, sink_l, band_bias, a128, sg_lng, sg_lnb, sg_w, sg_bfull,
        _GatherComm([od_w_in[0].astype(BF16), od_w_out[0].astype(BF16)]))
    w_od_in = _from_slabs(g_od_in)
    w_od_out = g_od_out.reshape(D, D)
    out0, z0, x1 = _ev_out(y0, w_ev_out, xs, mod0, ln0)
    xr, g1 = _od_in(x1, mod1, w_od_in)
    fwd_f = _rglru_fwd(xr, cw, cb, wa[0], wx[0], ba[0:1], bx[0:1], lam[0:1], False, "rglru_fwd_f")
    fwd_b = _rglru_fwd(xr, cw, cb, wa[1], wx[1], ba[1:2], bx[1:2], lam[1:2], True, "rglru_fwd_b")
    dh, dg1, dx1p, d_od_out, vec_a = _od_out(fwd_f[0], fwd_b[0], g1, w_od_out, x1, tgt, mod1, ln1)

    (dxcf, dwa_f, dwx_f, vec_f), (l_od_out,) = _rglru_bwd(
        fwd_f, dh, wa[0], wx[0], lam[0:1], False, "rglru_bwd_f",
        _ExchangeComm([d_od_out.reshape(N_DEV, D // N_DEV, D)]))
    (dxcb, dwa_b, dwx_b, vec_b), _ = _rglru_bwd(fwd_b, dh, wa[1], wx[1], lam[1:2], True, "rglru_bwd_b")
    (dx1, d_od_in, vec_c), (a_wa, a_wx) = _od_in_bwd(
        dxcf, dxcb, xr, dg1, x1, dx1p, mod1, w_od_in, cw,
        _GatherComm([jnp.stack([dwa_f, dwa_b]).astype(BF16), jnp.stack([dwx_f, dwx_b]).astype(BF16)],
                    mid_frac=0.75))
    dxp, dyc, dg0, d_ev_out, vec_d = _ev_out_bwd(dx1, z0, out0, y0, ycat, g0, w_ev_out, mod0, ln0)
    (dq, dkv, dsu, dsv, d_sg_w, d_sg_bt, vec_e, d_sink_l), (l_od_in, l_ev_out) = _mix0_bwd(
        q, kvx, lse, dyc, ycat, su, sv, sink_l, band_bias, a128, gsum, sel, sg_lng, sg_lnb, sg_w, sg_bfull,
        rc, rs1, rs2, _ExchangeComm([d_od_in, d_ev_out.reshape(N_DEV, D // N_DEV, D)]))
    (grad_x, d_ev_in, vec_g), _ = _ev_in_bwd(dq, dkv, dsu, dsv, dg0, xs, dxp, mod0, w_ev_in, rc, rs1, rs2)

    l_ev_in, (ga, gc, gd, gf, gb, gg, ge, gsink, gbt, a_sgw) = _tail_exchange(
        _to_slabs(d_ev_in, EV_IN // N_DEV),
        [vec_a, vec_c, vec_d, vec_f, vec_b, vec_g, vec_e, d_sink_l, d_sg_bt, d_sg_w.astype(BF16)])

    dmod_all = jnp.stack([jnp.concatenate([gg[:, 0], gg[:, 1], gd[:, 2]], axis=-1),
                          jnp.concatenate([gc[:, 5], gc[:, 6], ga[:, 2]], axis=-1)], axis=1)
    cols = ada_w.shape[2]
    dmod_cols = lax.dynamic_slice_in_dim(dmod_all, me * cols, cols, axis=2).transpose(1, 0, 2)
    (g_ada_w, d_ada_w, nm_ada_w, nv_ada_w, g_ada_b, d_ada_b, nm_ada_b, nv_ada_b) = _ada_update(
        c_all, dmod_cols, dmod_all, ada_w, m_ada_w, v_ada_w, ada_b, m_ada_b, v_ada_b)

    res = dict(ada_w=[g_ada_w, d_ada_w, nm_ada_w, nv_ada_w], ada_b=[g_ada_b, d_ada_b, nm_ada_b, nv_ada_b])
    for name, land, w, m, v in (("ev_w_in", l_ev_in, ev_w_in, m_ev_w_in, v_ev_w_in),
                                ("ev_w_out", l_ev_out, ev_w_out, m_ev_w_out, v_ev_w_out),
                                ("od_w_in", l_od_in, od_w_in, m_od_w_in, v_od_w_in),
                                ("od_w_out", l_od_out, od_w_out, m_od_w_out, v_od_w_out)):
        res[name] = [a[None] for a in _reduce_adam(land, w[0], m[0], v[0], "adam_" + name)]
    res["od_w_a"] = _slots_adam(a_wa, od_w_a, m_od_w_a, v_od_w_a, "adam_od_w_a")
    res["od_w_x"] = _slots_adam(a_wx, od_w_x, m_od_w_x, v_od_w_x, "adam_od_w_x")
    res["ev_sg_w"] = _slots_adam(a_sgw, ev_sg_w, m_ev_sg_w, v_ev_sg_w, "adam_ev_sg_w")
    small = dict(ln_g=(ln_g, m_ln_g, v_ln_g), ln_b=(ln_b, m_ln_b, v_ln_b),
                 ev_sg_ln_g=(ev_sg_ln_g, m_ev_sg_ln_g, v_ev_sg_ln_g),
                 ev_sg_ln_b=(ev_sg_ln_b, m_ev_sg_ln_b, v_ev_sg_ln_b),
                 ev_sink=(ev_sink, m_ev_sink, v_ev_sink), ev_sg_b=(ev_sg_b, m_ev_sg_b, v_ev_sg_b),
                 od_conv_w=(od_conv_w, m_od_conv_w, v_od_conv_w), od_conv_b=(od_conv_b, m_od_conv_b, v_od_conv_b),
                 od_b_a=(od_b_a, m_od_b_a, v_od_b_a), od_b_x=(od_b_x, m_od_b_x, v_od_b_x),
                 od_lam=(od_lam, m_od_lam, v_od_lam))
    small_out = _small_update(ga, gc, gd, gf, gb, ge, gsink, gbt, small)
    loss = small_out[0][0, 0]
    for k, name in enumerate(SMALL_PARAMS):
        res[name] = small_out[1 + 4 * k:5 + 4 * k]

    order = ["ada_w", "ada_b", "ln_g", "ln_b", "ev_w_in", "ev_w_out", "ev_sink", "ev_sg_ln_g", "ev_sg_ln_b",
             "ev_sg_w", "ev_sg_b", "od_w_in", "od_conv_w", "od_conv_b", "od_w_a", "od_b_a", "od_w_x", "od_b_x",
             "od_lam", "od_w_out"]
    outs = [loss, grad_x.reshape(1, T, D)]
    for kind in range(4):
        outs += [res[name][kind] for name in order]
    return tuple(outs)
```

```python
import functools

import jax
import jax.numpy as jnp
from jax import lax
from jax.experimental import pallas as pl
from jax.experimental.pallas import tpu as pltpu

F32 = jnp.float32
BF16 = jnp.bfloat16

N_DEV = 8
D = 1024
N_HEADS = 8
HEAD_DIM = 64
KV_WIDTH = 128
ATTN_W = 512
SG_W = 512
SG_GROUPS = 8
SG_DIM = 64
BLK = 128
KVX_W = 1024
EV_IN = 2816
OD_IN = 2048
RNN_HEADS = 8
RNN_HD = 128
ALPHA = 4.0 ** 0.25
LN_EPS = 1e-5
NEG_INF = -1e30
RG_C = 8.0
ROPE_THETA = 500000.0
LR, B1, B2, EPS, WD, STEP = 0.001, 0.9, 0.999, 1e-08, 0.01, 10

LANE = 128
SUBLANE = 8
TM = 256
TMF = 512
TS = 256
VMEM_LIMIT = 56 * 1024 * 1024

MESH = pl.DeviceIdType.MESH


def _pallas(body, **kw):
    return pl.pallas_call(body, **kw)


def _params(sem, vmem=VMEM_LIMIT):
    return pltpu.CompilerParams(dimension_semantics=sem, vmem_limit_bytes=vmem)


def _sigmoid(x):
    return 0.5 * jnp.tanh(0.5 * x) + 0.5


def _silu_and_grad(x):
    s = _sigmoid(x)
    return x * s, s * (1.0 + x * (1.0 - s))


def _dot(a, b):
    return jnp.dot(a.astype(BF16), b.astype(BF16), preferred_element_type=F32)


def _dot_nt(a, b):
    return lax.dot_general(a.astype(BF16), b.astype(BF16), (((1,), (1,)), ((), ())), preferred_element_type=F32)


def _dot_tn(a, b):
    return lax.dot_general(a.astype(BF16), b.astype(BF16), (((0,), (0,)), ((), ())), preferred_element_type=F32)


def _ln_fwd(z, g, b):
    mu = jnp.mean(z, axis=-1, keepdims=True)
    zc = z - mu
    var = jnp.mean(zc * zc, axis=-1, keepdims=True)
    rstd = lax.rsqrt(var + LN_EPS)
    xhat = zc * rstd
    return xhat * g + b, xhat, rstd


def _ln_bwd(dy, xhat, rstd, g):
    dxh = dy * g
    m1 = jnp.mean(dxh, axis=-1, keepdims=True)
    m2 = jnp.mean(dxh * xhat, axis=-1, keepdims=True)
    return rstd * (dxh - m1 - xhat * m2)


def _rowsum(v):
    return jnp.sum(v, axis=0, keepdims=True)


def _rope_fwd(t, c, s1, s2):
    return t * c + pltpu.roll(t, LANE - 8, 1) * s1 + pltpu.roll(t, 8, 1) * s2


def _rope_bwd(d, c, s1, s2):
    return d * c + pltpu.roll(d * s1, 8, 1) + pltpu.roll(d * s2, LANE - 8, 1)


def _adam(w, g, m, v):
    m2 = B1 * m + (1.0 - B1) * g
    v2 = B2 * v + (1.0 - B2) * (g * g)
    m_hat = m2 / (1.0 - B1 ** STEP)
    v_hat = v2 / (1.0 - B2 ** STEP)
    delta = -LR * (m_hat / (jnp.sqrt(v_hat) + EPS) + WD * w)
    return delta, m2, v2


def _tile(rows, width):
    return pl.BlockSpec((rows, width), lambda i: (i, 0))


def _full(shape):
    zeros = (0,) * len(shape)
    return pl.BlockSpec(shape, lambda i: zeros)


def _rev_tile(rows, width, n, reverse):
    if reverse:
        return pl.BlockSpec((rows, width), lambda i: (n - 1 - i, 0))
    return pl.BlockSpec((rows, width), lambda i: (i, 0))


def _halo_specs(rows, width, n, total_rows, reverse):
    per = rows // SUBLANE
    last = total_rows // SUBLANE - 1

    def tile_of(i):
        return (n - 1 - i) if reverse else i

    prev = pl.BlockSpec((SUBLANE, width), lambda i: (jnp.maximum(tile_of(i) * per - 1, 0), 0))
    nxt = pl.BlockSpec((SUBLANE, width), lambda i: (jnp.minimum((tile_of(i) + 1) * per, last), 0))
    return prev, nxt


def _my_pos():
    return lax.axis_index("x"), lax.axis_index("y"), lax.axis_index("c")


def _slot(px, py, pc):
    return 4 * px + 2 * py + pc


def _all_gather(arrs, name):
    n = len(arrs)

    def body(*refs):
        ins, outs = refs[:n], refs[n:2 * n]
        send_sems, recv_sems, local_sems = refs[2 * n:]
        x, y, c = _my_pos()
        me, sibling = (x, y, c), (x, y, 1 - c)
        chips = [(1 - x, y), (x, 1 - y), (1 - x, 1 - y)]

        def copy(a, k, block, to, src=None):
            dst = outs[a].at[_slot(*block)]
            return pltpu.make_async_remote_copy(
                src_ref=dst if src is None else src, dst_ref=dst,
                send_sem=send_sems.at[a * 7 + k], recv_sem=recv_sems.at[a * 7 + k],
                device_id=to, device_id_type=MESH)

        local, first = [], []
        for a in range(n):
            lc = pltpu.make_async_copy(ins[a], outs[a].at[_slot(*me)], local_sems.at[a])
            lc.start()
            local.append(lc)
            first.append(copy(a, 0, me, sibling, src=ins[a]))
            first += [copy(a, 1 + j, me, (*chip, c), src=ins[a]) for j, chip in enumerate(chips)]
        for cp in first:
            cp.start()
        passed = []
        for j, chip in enumerate(chips):
            for a in range(n):
                copy(a, 1 + j, (*chip, c), me).wait_recv()
                fw = copy(a, 4 + j, (*chip, c), sibling)
                fw.start()
                passed.append(fw)
        for a in range(n):
            copy(a, 0, sibling, me).wait_recv()
            for j, chip in enumerate(chips):
                copy(a, 4 + j, (*chip, 1 - c), me).wait_recv()
        for cp in first + passed:
            cp.wait_send()
        for lc in local:
            lc.wait()

    any_spec = pl.BlockSpec(memory_space=pl.ANY)
    return _pallas(
        body, name=name,
        out_shape=[jax.ShapeDtypeStruct((N_DEV,) + a.shape, a.dtype) for a in arrs],
        in_specs=[any_spec] * n, out_specs=[any_spec] * n,
        scratch_shapes=[pltpu.SemaphoreType.DMA((7 * n,)), pltpu.SemaphoreType.DMA((7 * n,)),
                        pltpu.SemaphoreType.DMA((n,))],
    )(*arrs)


class _GatherComm:
    has_mid = True

    def __init__(self, arrs, mid_frac=0.5):
        self.arrs = list(arrs)
        self.n = len(self.arrs)
        self.mid_frac = mid_frac

    def out_shapes(self):
        return [jax.ShapeDtypeStruct((N_DEV,) + a.shape, a.dtype) for a in self.arrs]

    def sems(self):
        return [pltpu.SemaphoreType.DMA((7 * self.n,)), pltpu.SemaphoreType.DMA((7 * self.n,)),
                pltpu.SemaphoreType.DMA((self.n,))]

    def _parts(self, ins, outs, sems):
        send_sems, recv_sems, local_sems = sems
        x, y, c = _my_pos()
        me, sibling = (x, y, c), (x, y, 1 - c)
        chips = [(1 - x, y), (x, 1 - y), (1 - x, 1 - y)]

        def copy(a, k, block, to, src=None):
            dst = outs[a].at[_slot(*block)]
            return pltpu.make_async_remote_copy(
                src_ref=dst if src is None else src, dst_ref=dst,
                send_sem=send_sems.at[a * 7 + k], recv_sem=recv_sems.at[a * 7 + k],
                device_id=to, device_id_type=MESH)

        local = [pltpu.make_async_copy(ins[a], outs[a].at[_slot(*me)], local_sems.at[a]) for a in range(self.n)]
        first = []
        for a in range(self.n):
            first.append(copy(a, 0, me, sibling, src=ins[a]))
            first += [copy(a, 1 + j, me, (*chip, c), src=ins[a]) for j, chip in enumerate(chips)]
        ici_in = [copy(a, 1 + j, (*chip, c), me) for j, chip in enumerate(chips) for a in range(self.n)]
        passed = [copy(a, 4 + j, (*chip, c), sibling) for j, chip in enumerate(chips) for a in range(self.n)]
        d2d_in = []
        for a in range(self.n):
            d2d_in.append(copy(a, 0, sibling, me))
            d2d_in += [copy(a, 4 + j, (*chip, 1 - c), me) for j, chip in enumerate(chips)]
        return local, first, ici_in, passed, d2d_in

    def start(self, ins, outs, sems):
        local, first, _, _, _ = self._parts(ins, outs, sems)
        for cp in local + first:
            cp.start()

    def mid(self, ins, outs, sems):
        _, _, ici_in, passed, _ = self._parts(ins, outs, sems)
        for arrived, fw in zip(ici_in, passed):
            arrived.wait_recv()
            fw.start()

    def finish(self, ins, outs, sems):
        local, first, _, passed, d2d_in = self._parts(ins, outs, sems)
        for cp in d2d_in:
            cp.wait_recv()
        for cp in first + passed:
            cp.wait_send()
        for cp in local:
            cp.wait()


class _ExchangeComm:
    has_mid = False

    def __init__(self, arrs):
        self.arrs = list(arrs)
        self.n = len(self.arrs)

    def out_shapes(self):
        return [jax.ShapeDtypeStruct(a.shape, a.dtype) for a in self.arrs]

    def sems(self):
        return [pltpu.SemaphoreType.DMA((7 * self.n,)), pltpu.SemaphoreType.DMA((7 * self.n,)),
                pltpu.SemaphoreType.DMA((self.n,))]

    def _copies(self, ins, outs, sems):
        send_sems, recv_sems, local_sems = sems
        x, y, c = _my_pos()
        mine = _slot(x, y, c)
        copies = [pltpu.make_async_copy(ins[a].at[mine], outs[a].at[mine], local_sems.at[a]) for a in range(self.n)]
        for k in range(1, N_DEV):
            px = (1 - x) if (k & 4) else x
            py = (1 - y) if (k & 2) else y
            pc = (1 - c) if (k & 1) else c
            for a in range(self.n):
                copies.append(pltpu.make_async_remote_copy(
                    src_ref=ins[a].at[_slot(px, py, pc)], dst_ref=outs[a].at[mine],
                    send_sem=send_sems.at[a * 7 + k - 1], recv_sem=recv_sems.at[a * 7 + k - 1],
                    device_id=(px, py, pc), device_id_type=MESH))
        return copies

    def start(self, ins, outs, sems):
        for cp in self._copies(ins, outs, sems):
            cp.start()

    def finish(self, ins, outs, sems):
        for cp in self._copies(ins, outs, sems):
            cp.wait()


def _fused_call(body, comm, operands, *, name, grid, in_specs, out_specs, out_shape, scratch_shapes=(),
                semantics=("arbitrary",)):
    n_in, n_out, n_scr = len(in_specs), len(out_specs), len(scratch_shapes)
    if comm is None:
        res = _pallas(body, name=name, grid=grid, in_specs=list(in_specs), out_specs=list(out_specs),
                      out_shape=list(out_shape), scratch_shapes=list(scratch_shapes),
                      compiler_params=_params(semantics))(*operands)
        return list(res), []
    k = comm.n
    steps = grid[0]

    def wrapped(*refs):
        ins, cins = refs[:n_in], refs[n_in:n_in + k]
        outs = refs[n_in + k:n_in + k + n_out]
        couts = refs[n_in + k + n_out:n_in + 2 * k + n_out]
        rest = refs[n_in + 2 * k + n_out:]
        scratch, sems = rest[:n_scr], rest[n_scr:]
        i = pl.program_id(0)

        @pl.when(i == 0)
        def _():
            comm.start(cins, couts, sems)

        body(*ins, *outs, *scratch)

        if comm.has_mid:
            @pl.when(i == int(steps * comm.mid_frac))
            def _():
                comm.mid(cins, couts, sems)

        @pl.when(i == steps - 1)
        def _():
            comm.finish(cins, couts, sems)

    any_spec = pl.BlockSpec(memory_space=pl.ANY)
    res = _pallas(wrapped, name=name, grid=grid, in_specs=list(in_specs) + [any_spec] * k,
                  out_specs=list(out_specs) + [any_spec] * k, out_shape=list(out_shape) + comm.out_shapes(),
                  scratch_shapes=list(scratch_shapes) + comm.sems(),
                  compiler_params=_params(("arbitrary",)))(*operands, *comm.arrs)
    return list(res[:n_out]), list(res[n_out:])


def _mod_part(c_all, ada_w):
    cols = ada_w.shape[2]

    def body(c_ref, w_ref, o_ref):
        cv = c_ref[...]
        cond = cv * _sigmoid(cv)
        for l in range(2):
            o_ref[l] = _dot(cond, w_ref[l])

    return _pallas(
        body, name="mod_part", grid=(1,),
        in_specs=[_full((N_DEV, D)), _full((2, D, cols))],
        out_specs=_full((2, N_DEV, cols)),
        out_shape=jax.ShapeDtypeStruct((2, N_DEV, cols), F32),
        compiler_params=_params(("arbitrary",)),
    )(c_all, ada_w)


def _ada_update(c_all, dmod_cols, dmod_all, ada_w, m_w, v_w, ada_b, m_b, v_b):
    cols = ada_w.shape[2]
    nb = ada_b.shape[1]

    def body(c_ref, dmc_ref, dma_ref, w_ref, mw_ref, vw_ref, b_ref, mb_ref, vb_ref,
             gw_ref, dw_ref, nmw_ref, nvw_ref, gb_ref, db_ref, nmb_ref, nvb_ref):
        cv = c_ref[...]
        cond = cv * _sigmoid(cv)
        for l in range(2):
            g = _dot_tn(cond, dmc_ref[l])
            gw_ref[l] = g
            dlt, m2, v2 = _adam(w_ref[l], g, mw_ref[l], vw_ref[l])
            dw_ref[l] = dlt
            nmw_ref[l] = m2
            nvw_ref[l] = v2
        gb = dma_ref[0]
        for i in range(1, N_DEV):
            gb = gb + dma_ref[i]
        gb_ref[...] = gb
        dlt, m2, v2 = _adam(b_ref[...], gb, mb_ref[...], vb_ref[...])
        db_ref[...] = dlt
        nmb_ref[...] = m2
        nvb_ref[...] = v2

    wspec = _full((2, D, cols))
    bspec = _full((2, nb))
    wshape = jax.ShapeDtypeStruct((2, D, cols), F32)
    bshape = jax.ShapeDtypeStruct((2, nb), F32)
    return _pallas(
        body, name="ada_update", grid=(1,),
        in_specs=[_full((N_DEV, D)), _full((2, N_DEV, cols)), _full((N_DEV, 2, nb)),
                  wspec, wspec, wspec, bspec, bspec, bspec],
        out_specs=[wspec] * 4 + [bspec] * 4,
        out_shape=[wshape] * 4 + [bshape] * 4,
        compiler_params=_params(("arbitrary",)),
    )(c_all, dmod_cols, dmod_all, ada_w, m_w, v_w, ada_b, m_b, v_b)


def _ev_in(x, mod, w_in, rc, rs1, rs2, comm=None):
    T = x.shape[0]

    def body(x_ref, mod_ref, w_ref, c_ref, s1_ref, s2_ref, q_ref, kv_ref, su_ref, sv_ref, g_ref):
        h = x_ref[...] * (1.0 + mod_ref[1:2, :]) + mod_ref[0:1, :]
        p = _dot_nt(h, w_ref[...])
        c, s1, s2 = c_ref[...], s1_ref[...], s2_ref[...]
        for j in range(ATTN_W // LANE):
            qr = _rope_fwd(p[:, j * LANE:(j + 1) * LANE], c, s1, s2)
            q_ref[:, j * LANE:(j + 1) * LANE] = (qr * (HEAD_DIM ** -0.5)).astype(BF16)
        low = lax.broadcasted_iota(jnp.int32, (TMF, LANE), 1) < HEAD_DIM
        for j, val in enumerate((_rope_fwd(p[:, 512:640], c, s1, s2), p[:, 640:768])):
            swapped = pltpu.roll(val, HEAD_DIM, 1)
            tiles = (jnp.where(low, val, 0.0), jnp.where(low, 0.0, swapped),
                     jnp.where(low, swapped, 0.0), jnp.where(low, 0.0, val))
            for k, tile in enumerate(tiles):
                kv_ref[:, (4 * j + k) * LANE:(4 * j + k + 1) * LANE] = tile.astype(BF16)
        su_ref[...] = p[:, 768:1280].astype(BF16)
        sv_ref[...] = p[:, 1280:1792].astype(BF16)
        g_ref[...] = p[:, 1792:2816].astype(BF16)

    sh = lambda w: jax.ShapeDtypeStruct((T, w), BF16)
    return _fused_call(
        body, comm, (x, mod, w_in, rc, rs1, rs2), name="ev_in", grid=(T // TMF,),
        in_specs=[_tile(TMF, D), _full((3, D)), _full((EV_IN, D)), _tile(TMF, LANE), _tile(TMF, LANE),
                  _tile(TMF, LANE)],
        out_specs=[_tile(TMF, ATTN_W), _tile(TMF, KVX_W), _tile(TMF, SG_W), _tile(TMF, SG_W), _tile(TMF, D)],
        out_shape=[sh(ATTN_W), sh(KVX_W), sh(SG_W), sh(SG_W), sh(D)], semantics=("parallel",))


def _band_specs(width, nb):
    return [pl.BlockSpec((BLK, width), lambda n: (jnp.maximum(n - 1, 0), 0)),
            pl.BlockSpec((BLK, width), lambda n: (n, 0)),
            pl.BlockSpec((BLK, width), lambda n: (jnp.minimum(n + 1, nb - 1), 0))]


def _band_bias(bias_ref, n, nb):
    rows = lax.broadcasted_iota(jnp.int32, (3 * BLK, 1), 0)
    outside = ((rows < BLK) & (n == 0)) | ((rows >= 2 * BLK) & (n == nb - 1))
    return bias_ref[...] + jnp.where(outside, NEG_INF, 0.0)


def _lane_tile(ref, t):
    return ref[:, t * LANE:(t + 1) * LANE]


def _split_bf16(v):
    hi = v.astype(BF16)
    return hi, (v - hi.astype(F32)).astype(BF16)


def _group_mean(v, a_ref, exact_bf16=False):
    hi, lo = _split_bf16(v)
    a = a_ref[...]
    out = []
    for t in range(SG_W // (2 * LANE)):
        sl = slice(t * 2 * LANE, (t + 1) * 2 * LANE)
        r = jnp.dot(hi[:, sl], a, preferred_element_type=F32)
        if not exact_bf16:
            r = r + jnp.dot(lo[:, sl], a, preferred_element_type=F32)
        out.append(r)
    return jnp.concatenate(out, axis=-1)


def _sg_core(sv_ref, lng, lnb, a_ref, w_ref, bfull_ref):
    svf = sv_ref[...].astype(F32)
    xc = svf - _group_mean(svf, a_ref, exact_bf16=True)
    rstd = lax.rsqrt(_group_mean(xc * xc, a_ref) + LN_EPS)
    xhat = xc * rstd
    vb = (xhat * lng + lnb).astype(BF16)
    low = lax.broadcasted_iota(jnp.int32, (BLK, LANE), 1) < SG_DIM
    tiles = []
    for t in range(SG_W // LANE):
        v2 = vb[:, t * LANE:(t + 1) * LANE]
        r0 = jnp.dot(w_ref[2 * t], v2, preferred_element_type=F32)
        r1 = jnp.dot(w_ref[2 * t + 1], v2, preferred_element_type=F32)
        tiles.append(jnp.where(low, r0, r1))
    svm = jnp.concatenate(tiles, axis=-1) + bfull_ref[...]
    return xhat, rstd, vb, svm


def _mix0_fwd(q, kvx, su, sv, g0, sink_l, bias, a128, sg_lng, sg_lnb, sg_w, sg_bfull, comm=None):
    T = q.shape[0]
    nb = T // BLK

    def body(q_ref, kp_ref, kc_ref, kn_ref, su_ref, sv_ref, g_ref, sink_ref, bias_ref, a_ref, lng_ref, lnb_ref,
             w_ref, bfull_ref, ycat_ref, y0_ref, lse_ref):
        n = pl.program_id(0)
        bias = _band_bias(bias_ref, n, nb)
        kvx = jnp.concatenate([kp_ref[...], kc_ref[...], kn_ref[...]], axis=0)
        tiles = []
        for t in range(ATTN_W // LANE):
            qt = _lane_tile(q_ref, t)
            acc = None
            for par in range(2):
                h = 2 * t + par
                kt = 2 * (h // 4) + par
                ke = kvx[:, kt * LANE:(kt + 1) * LANE]
                ve = kvx[:, (4 + kt) * LANE:(5 + kt) * LANE]
                st = _dot_nt(ke, qt) + bias
                sk = _lane_tile(sink_ref, h)
                m = jnp.maximum(jnp.max(st, axis=0, keepdims=True), sk)
                p = jnp.exp(st - m)
                denom = jnp.sum(p, axis=0, keepdims=True) + jnp.exp(sk - m)
                contrib = _dot_tn(p * (1.0 / denom), ve)
                acc = contrib if acc is None else acc + contrib
                lse_ref[0, :, h * LANE:(h + 1) * LANE] = m + jnp.log(denom)
            tiles.append(acc)
        _, _, _, svm = _sg_core(sv_ref, lng_ref[...], lnb_ref[...], a_ref, w_ref, bfull_ref)
        tiles.append(su_ref[...].astype(F32) * svm)
        ycat = jnp.concatenate(tiles, axis=-1)
        gf = g_ref[...].astype(F32)
        ycat_ref[...] = ycat.astype(BF16)
        y0_ref[...] = (ycat * (gf * _sigmoid(gf))).astype(BF16)

    return _fused_call(
        body, comm, (q, kvx, kvx, kvx, su, sv, g0, sink_l, bias, a128, sg_lng, sg_lnb, sg_w, sg_bfull),
        name="mix0_fwd", grid=(nb,),
        in_specs=[_tile(BLK, ATTN_W)] + _band_specs(KVX_W, nb) + [
            _tile(BLK, SG_W), _tile(BLK, SG_W), _tile(BLK, D), _full((1, N_HEADS * LANE)), _full((3 * BLK, LANE)),
            _full((2 * LANE, 2 * LANE)),_full((1, SG_W)), _full((1, SG_W)), _full((SG_GROUPS, BLK, BLK)),
            _full((BLK, SG_W))],
        out_specs=[_tile(BLK, D), _tile(BLK, D), pl.BlockSpec((1, 1, N_HEADS * LANE), lambda n: (n, 0, 0))],
        out_shape=[jax.ShapeDtypeStruct((T, D), BF16), jax.ShapeDtypeStruct((T, D), BF16),
                   jax.ShapeDtypeStruct((nb, 1, N_HEADS * LANE), F32)], semantics=("parallel",))


def _ev_out(y0, w_out, x, mod, lnp):
    T = x.shape[0]

    def body(y_ref, w_ref, x_ref, mod_ref, ln_ref, out_ref, z_ref, x1_ref):
        out = _dot(y_ref[...], w_ref[...])
        z = ALPHA * x_ref[...] + mod_ref[2:3, :] * out
        x1, _, _ = _ln_fwd(z, ln_ref[0:1, :], ln_ref[1:2, :])
        out_ref[...] = out.astype(BF16)
        z_ref[...] = z
        x1_ref[...] = x1

    return _pallas(
        body, name="ev_out", grid=(T // TMF,),
        in_specs=[_tile(TMF, D), _full((D, D)), _tile(TMF, D), _full((3, D)), _full((2, D))],
        out_specs=[_tile(TMF, D)] * 3,
        out_shape=[jax.ShapeDtypeStruct((T, D), BF16), jax.ShapeDtypeStruct((T, D), F32),
                   jax.ShapeDtypeStruct((T, D), F32)],
        compiler_params=_params(("parallel",)),
    )(y0, w_out, x, mod, lnp)


def _od_in(x1, mod, w_in):
    T = x1.shape[0]

    def body(x_ref, mod_ref, w_ref, xr_ref, g_ref):
        h = x_ref[...] * (1.0 + mod_ref[1:2, :]) + mod_ref[0:1, :]
        p = _dot(h, w_ref[...])
        xr_ref[...] = p[:, :D]
        g_ref[...] = p[:, D:].astype(BF16)

    return _pallas(
        body, name="od_in", grid=(T // TMF,),
        in_specs=[_tile(TMF, D), _full((3, D)), _full((D, OD_IN))],
        out_specs=[_tile(TMF, D), _tile(TMF, D)],
        out_shape=[jax.ShapeDtypeStruct((T, D), F32), jax.ShapeDtypeStruct((T, D), BF16)],
        compiler_params=_params(("parallel",)),
    )(x1, mod, w_in)


def _ext_rows(prev_ref, cur, next_ref, j, n):
    prev = jnp.where(j > 0, prev_ref[...], 0.0)
    nxt = jnp.where(j < n - 1, next_ref[...], 0.0)
    return jnp.concatenate([prev, cur, nxt], axis=0)


def _shift_rows(ext, off, rows):
    total = ext.shape[0]
    if off == 0:
        return ext[SUBLANE:SUBLANE + rows, :]
    return pltpu.roll(ext, (-off) % total, 0)[SUBLANE:SUBLANE + rows, :]


def _conv_fwd(ext, cw, cb, rows):
    xc = cb
    for k in range(4):
        xc = xc + cw[k:k + 1, :] * _shift_rows(ext, k - 2, rows)
    return xc


def _gates(xc, wa_ref, wx_ref, ba, bx, lam):
    pr, pi = [], []
    for h in range(RNN_HEADS):
        xh = xc[:, h * RNN_HD:(h + 1) * RNN_HD].astype(BF16)
        pr.append(_dot(xh, wa_ref[h]))
        pi.append(_dot(xh, wx_ref[h]))
    r = _sigmoid(jnp.concatenate(pr, axis=-1) + ba)
    ig = _sigmoid(jnp.concatenate(pi, axis=-1) + bx)
    sp = jnp.maximum(-lam, 0.0) + jnp.log(1.0 + jnp.exp(-jnp.abs(lam)))
    neg_log_a = RG_C * r * sp
    a = jnp.exp(-neg_log_a)
    s2 = (1.0 + a * a) * jnp.tanh(neg_log_a)
    inv_s = lax.rsqrt(jnp.maximum(s2, 1e-30))
    return r, ig, sp, a, s2 * inv_s, inv_s


def _scan_tile(a_ref, b_ref, o_ref, carry_ref, rows, reverse):
    ridx = lax.broadcasted_iota(jnp.int32, (SUBLANE, D), 0)
    groups = rows // SUBLANE

    def group(gi, h):
        g = (groups - 1 - gi) if reverse else gi
        off = pl.multiple_of(g * SUBLANE, SUBLANE)
        a = a_ref[pl.ds(off, SUBLANE), :]
        b = b_ref[pl.ds(off, SUBLANE), :]
        for sh in (1, 2, 4):
            if reverse:
                keep = ridx < SUBLANE - sh
                a_p = jnp.where(keep, pltpu.roll(a, SUBLANE - sh, 0), 1.0)
                b_p = jnp.where(keep, pltpu.roll(b, SUBLANE - sh, 0), 0.0)
            else:
                keep = ridx >= sh
                a_p = jnp.where(keep, pltpu.roll(a, sh, 0), 1.0)
                b_p = jnp.where(keep, pltpu.roll(b, sh, 0), 0.0)
            b = b + a * b_p
            a = a * a_p
        hh = b + a * h
        o_ref[pl.ds(off, SUBLANE), :] = hh
        return hh[0:1, :] if reverse else hh[SUBLANE - 1:SUBLANE, :]

    carry_ref[...] = lax.fori_loop(0, groups, group, carry_ref[...])


def _rglru_fwd(xr, cw, cb, wa, wx, ba, bx, lam, reverse, name):
    T = xr.shape[0]
    n = T // TS
    prev_spec, next_spec = _halo_specs(TS, D, n, T, reverse)

    def body(prev_ref, cur_ref, next_ref, cw_ref, cb_ref, wa_ref, wx_ref, ba_ref, bx_ref, lam_ref,
             h_ref, a_ref, s_ref, r_ref, ig_ref, xc_ref, b_s, carry):
        i = pl.program_id(0)
        j = (n - 1 - i) if reverse else i

        @pl.when(i == 0)
        def _():
            carry[...] = jnp.zeros_like(carry)

        ext = _ext_rows(prev_ref, cur_ref[...], next_ref, j, n)
        xc = _conv_fwd(ext, cw_ref[...], cb_ref[...], TS)
        r, ig, _, a, s, _ = _gates(xc, wa_ref, wx_ref, ba_ref[...], bx_ref[...], lam_ref[...])
        s_ref[...] = s
        r_ref[...] = r.astype(BF16)
        ig_ref[...] = ig.astype(BF16)
        xc_ref[...] = xc.astype(BF16)
        a_ref[...] = a
        b_s[...] = s * ig * xc
        _scan_tile(a_ref, b_s, h_ref, carry, TS, reverse)

    wspec = _full((RNN_HEADS, RNN_HD, RNN_HD))
    cur = _rev_tile(TS, D, n, reverse)
    f32 = jax.ShapeDtypeStruct((T, D), F32)
    b16 = jax.ShapeDtypeStruct((T, D), BF16)
    return _pallas(
        body, name=name, grid=(n,),
        in_specs=[prev_spec, cur, next_spec, _full((4, D)), _full((1, D)),
                  wspec, wspec, _full((1, D)), _full((1, D)), _full((1, D))],
        out_specs=[cur] * 6,
        out_shape=[f32, f32, f32, b16, b16, b16],
        scratch_shapes=[pltpu.VMEM((TS, D), F32), pltpu.VMEM((1, D), F32)],
        compiler_params=_params(("arbitrary",)),
    )(xr, xr, xr, cw, cb, wa, wx, ba, bx, lam)


def _od_out(hf, hb, g1, w_out, x1, tgt, mod, lnp):
    T = x1.shape[0]

    def body(hf_ref, hb_ref, g_ref, w_ref, x_ref, t_ref, mod_ref, ln_ref,
             dh_ref, dg_ref, dx_ref, dwb_ref, vec_ref, dw_ref):
        i = pl.program_id(0)

        @pl.when(i == 0)
        def _():
            dw_ref[...] = jnp.zeros_like(dw_ref)
            vec_ref[...] = jnp.zeros_like(vec_ref)

        hs = hf_ref[...] + hb_ref[...]
        sg, dsg = _silu_and_grad(g_ref[...].astype(F32))
        yr = (hs * sg).astype(BF16)
        w = w_ref[...]
        out = _dot(yr, w)
        gate = mod_ref[2:3, :]
        z = ALPHA * x_ref[...] + gate * out
        lng = ln_ref[0:1, :]
        x2, xhat, rstd = _ln_fwd(z, lng, ln_ref[1:2, :])
        diff = x2 - t_ref[...]
        vec_ref[3:4, 0:LANE] += 0.5 * jnp.sum(diff * diff) * (1.0 / D)
        dx2 = diff * (1.0 / D)
        dz = _ln_bwd(dx2, xhat, rstd, lng)
        vec_ref[0:1, :] += _rowsum(dx2 * xhat)
        vec_ref[1:2, :] += _rowsum(dx2)
        vec_ref[2:3, :] += _rowsum(dz * out)
        dout = (dz * gate).astype(BF16)
        dyr = _dot_nt(dout, w)
        dw_ref[...] += _dot_tn(yr, dout)
        dh_ref[...] = dyr * sg
        dg_ref[...] = (dyr * hs * dsg).astype(BF16)
        dx_ref[...] = ALPHA * dz

        @pl.when(i == T // TM - 1)
        def _():
            dwb_ref[...] = dw_ref[...].astype(BF16)

    return _pallas(
        body, name="od_out", grid=(T // TM,),
        in_specs=[_tile(TM, D), _tile(TM, D), _tile(TM, D), _full((D, D)), _tile(TM, D), _tile(TM, D),
                  _full((3, D)), _full((2, D))],
        out_specs=[_tile(TM, D), _tile(TM, D), _tile(TM, D), _full((D, D)), _full((SUBLANE, D))],
        out_shape=[jax.ShapeDtypeStruct((T, D), F32), jax.ShapeDtypeStruct((T, D), BF16),
                   jax.ShapeDtypeStruct((T, D), F32), jax.ShapeDtypeStruct((D, D), BF16),
                   jax.ShapeDtypeStruct((SUBLANE, D), F32)],
        scratch_shapes=[pltpu.VMEM((D, D), F32)],
        compiler_params=_params(("arbitrary",)),
    )(hf, hb, g1, w_out, x1, tgt, mod, lnp)


def _rglru_bwd(fwd, dh, wa, wx, lam, reverse, name, comm=None):
    h, a_all, s_all, r_all, ig_all, xc_all = fwd
    T = h.shape[0]
    n = T // TS
    adj_rev = not reverse
    hprev_spec, hnext_spec = _halo_specs(TS, D, n, T, adj_rev)
    h_halo_spec = hnext_spec if reverse else hprev_spec

    def body(dh_ref, h_ref, hh_ref, a_ref, s_ref, r_ref, ig_ref, xc_ref, wa_ref, wx_ref, lam_ref,
             dxc_ref, dwa_ref, dwx_ref, vec_ref, a_s, l_s, carry, a_edge):
        i = pl.program_id(0)
        j = (n - 1 - i) if adj_rev else i

        @pl.when(i == 0)
        def _():
            carry[...] = jnp.zeros_like(carry)
            a_edge[...] = jnp.zeros_like(a_edge)
            dwa_ref[...] = jnp.zeros_like(dwa_ref)
            dwx_ref[...] = jnp.zeros_like(dwx_ref)
            vec_ref[...] = jnp.zeros_like(vec_ref)

        lam = lam_ref[...]
        sp = jnp.maximum(-lam, 0.0) + jnp.log(1.0 + jnp.exp(-jnp.abs(lam)))
        a, s = a_ref[...], s_ref[...]
        inv_s = lax.rsqrt(jnp.maximum(s * s, 1e-30))
        r, ig = r_ref[...].astype(F32), ig_ref[...].astype(F32)
        xcb = xc_ref[...]
        xc = xcb.astype(F32)

        rows = lax.broadcasted_iota(jnp.int32, (TS, D), 0)
        hcur = h_ref[...]
        if reverse:
            a_sh = jnp.where(rows == 0, a_edge[...], pltpu.roll(a, 1, 0))
            halo = jnp.where(j < n - 1, hh_ref[0:1, :], 0.0)
            h_nb = jnp.where(rows == TS - 1, halo, pltpu.roll(hcur, TS - 1, 0))
        else:
            a_sh = jnp.where(rows == TS - 1, a_edge[...], pltpu.roll(a, TS - 1, 0))
            halo = jnp.where(j > 0, hh_ref[SUBLANE - 1:SUBLANE, :], 0.0)
            h_nb = jnp.where(rows == 0, halo, pltpu.roll(hcur, 1, 0))
        a_s[...] = a_sh
        _scan_tile(a_s, dh_ref, l_s, carry, TS, adj_rev)
        a_edge[...] = a[TS - 1:TS, :] if reverse else a[0:1, :]

        lm = l_s[...]
        da = lm * h_nb
        di = lm * s * xc
        dxc = lm * s * ig
        ds = lm * ig * xc
        dlog_a = a * (da - ds * a * inv_s)
        dr = (-RG_C) * sp * dlog_a
        dsp = _rowsum((-RG_C) * r * dlog_a)
        dpr = dr * r * (1.0 - r)
        dpi = di * ig * (1.0 - ig)
        vec_ref[0:1, :] += _rowsum(dpr)
        vec_ref[1:2, :] += _rowsum(dpi)
        vec_ref[2:3, :] += dsp * (-_sigmoid(-lam))
        parts = []
        for hd in range(RNN_HEADS):
            sl = slice(hd * RNN_HD, (hd + 1) * RNN_HD)
            xh = xcb[:, sl]
            dprh = dpr[:, sl].astype(BF16)
            dpih = dpi[:, sl].astype(BF16)
            parts.append(_dot_nt(dprh, wa_ref[hd]) + _dot_nt(dpih, wx_ref[hd]))
            dwa_ref[hd] += _dot_tn(xh, dprh)
            dwx_ref[hd] += _dot_tn(xh, dpih)
        dxc_ref[...] = dxc + jnp.concatenate(parts, axis=-1)

    wspec = _full((RNN_HEADS, RNN_HD, RNN_HD))
    cur = _rev_tile(TS, D, n, adj_rev)
    return _fused_call(
        body, comm, (dh, h, h, a_all, s_all, r_all, ig_all, xc_all, wa, wx, lam), name=name, grid=(n,),
        in_specs=[cur, cur, h_halo_spec, cur, cur, cur, cur, cur, wspec, wspec, _full((1, D))],
        out_specs=[cur, wspec, wspec, _full((SUBLANE, D))],
        out_shape=[jax.ShapeDtypeStruct((T, D), F32),
                   jax.ShapeDtypeStruct((RNN_HEADS, RNN_HD, RNN_HD), F32),
                   jax.ShapeDtypeStruct((RNN_HEADS, RNN_HD, RNN_HD), F32),
                   jax.ShapeDtypeStruct((SUBLANE, D), F32)],
        scratch_shapes=[pltpu.VMEM((TS, D), F32)] * 2 + [pltpu.VMEM((1, D), F32)] * 2)


def _od_in_bwd(dxcf, dxcb, xr, dg1, x1, dx1p, mod, w_in, cw, comm=None):
    T = x1.shape[0]
    n = T // TM
    slab = OD_IN // N_DEV
    prev_spec, next_spec = _halo_specs(TM, D, n, T, False)

    def body(fp_ref, fc_ref, fn_ref, bp_ref, bc_ref, bn_ref, xp_ref, xc_ref, xn_ref, dg_ref, x1_ref, dxp_ref,
             mod_ref, w_ref, cw_ref, dx_ref, dwb_ref, vec_ref, dw_ref):
        i = pl.program_id(0)

        @pl.when(i == 0)
        def _():
            dw_ref[...] = jnp.zeros_like(dw_ref)
            vec_ref[...] = jnp.zeros_like(vec_ref)

        dcur = fc_ref[...] + bc_ref[...]
        dprev = jnp.where(i > 0, fp_ref[...] + bp_ref[...], 0.0)
        dnext = jnp.where(i < n - 1, fn_ref[...] + bn_ref[...], 0.0)
        dext = jnp.concatenate([dprev, dcur, dnext], axis=0)
        xext = _ext_rows(xp_ref, xc_ref[...], xn_ref, i, n)
        cw_v = cw_ref[...]
        dxr = None
        for k in range(4):
            term = cw_v[k:k + 1, :] * _shift_rows(dext, 2 - k, TM)
            dxr = term if dxr is None else dxr + term
            vec_ref[k:k + 1, :] += _rowsum(dcur * _shift_rows(xext, k - 2, TM))
        vec_ref[4:5, :] += _rowsum(dcur)
        dp = jnp.concatenate([dxr.astype(BF16), dg_ref[...]], axis=-1)
        x1v = x1_ref[...]
        scale1 = 1.0 + mod_ref[1:2, :]
        h1 = (x1v * scale1 + mod_ref[0:1, :]).astype(BF16)
        dh1 = _dot_nt(dp, w_ref[...])
        dw_ref[...] += _dot_tn(h1, dp)
        dx_ref[...] = dxp_ref[...] + dh1 * scale1
        vec_ref[5:6, :] += _rowsum(dh1)
        vec_ref[6:7, :] += _rowsum(dh1 * x1v)

        @pl.when(i == n - 1)
        def _():
            for j in range(N_DEV):
                dwb_ref[j] = dw_ref[:, j * slab:(j + 1) * slab].astype(BF16)

    t = _tile(TM, D)
    return _fused_call(
        body, comm, (dxcf, dxcf, dxcf, dxcb, dxcb, dxcb, xr, xr, xr, dg1, x1, dx1p, mod, w_in, cw),
        name="od_in_bwd", grid=(n,),
        in_specs=[prev_spec, t, next_spec, prev_spec, t, next_spec, prev_spec, t, next_spec, t, t, t,
                  _full((3, D)), _full((D, OD_IN)), _full((4, D))],
        out_specs=[t, _full((N_DEV, D, slab)), _full((SUBLANE, D))],
        out_shape=[jax.ShapeDtypeStruct((T, D), F32), jax.ShapeDtypeStruct((N_DEV, D, slab), BF16),
                   jax.ShapeDtypeStruct((SUBLANE, D), F32)],
        scratch_shapes=[pltpu.VMEM((D, OD_IN), F32)])


def _ev_out_bwd(dx1, z0, out0, y0, ycat, g0, w_out, mod, lnp):
    T = dx1.shape[0]

    def body(dx_ref, z_ref, out_ref, y0_ref, yc_ref, g_ref, w_ref, mod_ref, ln_ref,
             dxp_ref, dyc_ref, dg_ref, dwb_ref, vec_ref, dw_ref):
        i = pl.program_id(0)

        @pl.when(i == 0)
        def _():
            dw_ref[...] = jnp.zeros_like(dw_ref)
            vec_ref[...] = jnp.zeros_like(vec_ref)

        lng = ln_ref[0:1, :]
        _, xhat, rstd = _ln_fwd(z_ref[...], lng, ln_ref[1:2, :])
        dy = dx_ref[...]
        dz = _ln_bwd(dy, xhat, rstd, lng)
        vec_ref[0:1, :] += _rowsum(dy * xhat)
        vec_ref[1:2, :] += _rowsum(dy)
        vec_ref[2:3, :] += _rowsum(dz * out_ref[...].astype(F32))
        dout = (dz * mod_ref[2:3, :]).astype(BF16)
        dy0 = _dot_nt(dout, w_ref[...])
        dw_ref[...] += _dot_tn(y0_ref[...], dout)
        sg, dsg = _silu_and_grad(g_ref[...].astype(F32))
        dyc_ref[...] = (dy0 * sg).astype(BF16)
        dg_ref[...] = (dy0 * yc_ref[...].astype(F32) * dsg).astype(BF16)
        dxp_ref[...] = ALPHA * dz

        @pl.when(i == T // TM - 1)
        def _():
            dwb_ref[...] = dw_ref[...].astype(BF16)

    t = _tile(TM, D)
    return _pallas(
        body, name="ev_out_bwd", grid=(T // TM,),
        in_specs=[t, t, t, t, t, t, _full((D, D)), _full((3, D)), _full((2, D))],
        out_specs=[t, t, t, _full((D, D)), _full((SUBLANE, D))],
        out_shape=[jax.ShapeDtypeStruct((T, D), F32), jax.ShapeDtypeStruct((T, D), BF16),
                   jax.ShapeDtypeStruct((T, D), BF16), jax.ShapeDtypeStruct((D, D), BF16),
                   jax.ShapeDtypeStruct((SUBLANE, D), F32)],
        scratch_shapes=[pltpu.VMEM((D, D), F32)],
        compiler_params=_params(("arbitrary",)),
    )(dx1, z0, out0, y0, ycat, g0, w_out, mod, lnp)


def _mix0_bwd(q, kvx, lse, dyc, ycat, su, sv, sink_l, bias, a128, gsum, sel, sg_lng, sg_lnb, sg_w, sg_bfull,
              rc, rs1, rs2, comm=None):
    T = q.shape[0]
    nb = T // BLK

    def body(q_ref, kp_ref, kc_ref, kn_ref, lse_ref, dyc_ref, yc_ref, su_ref, sv_ref, sink_ref, bias_ref, a_ref,
             gsum_ref, sel_ref, lng_ref, lnb_ref, w_ref, bfull_ref, c_ref, s1_ref, s2_ref,
             dq_ref, dkv_ref, dsu_ref, dsv_ref, dw_ref, dbt_ref, vec_ref, dsink_ref):
        n = pl.program_id(0)

        @pl.when(n == 0)
        def _():
            dkv_ref[...] = jnp.zeros_like(dkv_ref)
            dw_ref[...] = jnp.zeros_like(dw_ref)
            dbt_ref[...] = jnp.zeros_like(dbt_ref)
            vec_ref[...] = jnp.zeros_like(vec_ref)
            dsink_ref[...] = jnp.zeros_like(dsink_ref)

        band = pl.ds(pl.multiple_of(n * BLK + (TM - BLK), BLK), 3 * BLK)
        bias = _band_bias(bias_ref, n, nb)
        kvx = jnp.concatenate([kp_ref[...], kc_ref[...], kn_ref[...]], axis=0)
        bias2 = jnp.concatenate([bias, bias], axis=1)
        low = lax.broadcasted_iota(jnp.int32, (BLK, LANE), 1) < HEAD_DIM
        low2 = lax.broadcasted_iota(jnp.int32, (2 * BLK, LANE), 1) < HEAD_DIM
        sel = sel_ref[...]
        c, s1, s2 = c_ref[...], s1_ref[...], s2_ref[...]
        for kvh in range(2):
            t0, t1 = 2 * kvh, 2 * kvh + 1
            q2 = jnp.concatenate([_lane_tile(q_ref, t0), _lane_tile(q_ref, t1)], axis=0)
            do2 = jnp.concatenate([_lane_tile(dyc_ref, t0), _lane_tile(dyc_ref, t1)], axis=0)
            yc2 = jnp.concatenate([_lane_tile(yc_ref, t0), _lane_tile(yc_ref, t1)], axis=0)
            p_hi, p_lo = _split_bf16(do2.astype(F32) * yc2.astype(F32))
            deltas = _dot_nt(sel, p_hi) + _dot_nt(sel, p_lo)
            dkx = jnp.zeros((3 * BLK, LANE), F32)
            dvx = jnp.zeros((3 * BLK, LANE), F32)
            dq_acc = None
            for par in range(2):
                heads = (4 * kvh + par, 4 * kvh + 2 + par)
                kt = 2 * kvh + par
                ke = kvx[:, kt * LANE:(kt + 1) * LANE]
                ve = kvx[:, (4 + kt) * LANE:(5 + kt) * LANE]
                lse = jnp.concatenate([lse_ref[0, :, h * LANE:(h + 1) * LANE] for h in heads], axis=1)
                sk = jnp.concatenate([_lane_tile(sink_ref, h) for h in heads], axis=1)
                delta = deltas[par:par + 1, :]
                pt = jnp.exp(_dot_nt(ke, q2) + bias2 - lse)
                dst = (pt * (_dot_nt(ve, do2) - delta)).astype(BF16)
                sink_terms = jnp.exp(sk - lse) * delta
                for k, h in enumerate(heads):
                    dsink_ref[:, h * LANE:(h + 1) * LANE] += sink_terms[:, k * LANE:(k + 1) * LANE]
                part = _dot_tn(dst, ke)
                dq_acc = part if dq_acc is None else dq_acc + part
                mine = low2 if par == 0 else jnp.logical_not(low2)
                dkx = dkx + jnp.dot(dst, jnp.where(mine, q2, jnp.zeros_like(q2)), preferred_element_type=F32)
                dvx = dvx + jnp.dot(pt.astype(BF16), jnp.where(mine, do2, jnp.zeros_like(do2)),
                                    preferred_element_type=F32)
            for k, t in enumerate((t0, t1)):
                dq_t = dq_acc[k * BLK:(k + 1) * BLK] * (HEAD_DIM ** -0.5)
                dq_ref[:, t * LANE:(t + 1) * LANE] = _rope_bwd(dq_t, c, s1, s2).astype(BF16)
            dkv_ref[band, kvh * LANE:(kvh + 1) * LANE] += dkx
            dkv_ref[band, (2 + kvh) * LANE:(3 + kvh) * LANE] += dvx

        lng = lng_ref[...]
        xhat, rstd, vb, svm = _sg_core(sv_ref, lng, lnb_ref[...], a_ref, w_ref, bfull_ref)
        dy = dyc_ref[:, ATTN_W:].astype(F32)
        dsu_ref[...] = (dy * svm).astype(BF16)
        dsvm = dy * su_ref[...].astype(F32)
        d_hi, d_lo = _split_bf16(dsvm)
        gsum = gsum_ref[...]
        dbt_ref[...] += jnp.dot(d_hi, gsum, preferred_element_type=F32) + jnp.dot(d_lo, gsum,
                                                                                 preferred_element_type=F32)
        tiles = []
        for t in range(SG_W // LANE):
            tl = slice(t * LANE, (t + 1) * LANE)
            dt, v2 = d_hi[:, tl], vb[:, tl]
            dw_ref[2 * t] += _dot_nt(jnp.where(low, dt, jnp.zeros_like(dt)), v2)
            dw_ref[2 * t + 1] += _dot_nt(jnp.where(low, jnp.zeros_like(dt), dt), v2)
            tiles.append(jnp.where(low, _dot_tn(w_ref[2 * t], dt), _dot_tn(w_ref[2 * t + 1], dt)))
        dvgn = jnp.concatenate(tiles, axis=-1)
        vec_ref[0:1, :] += _rowsum(dvgn * xhat)
        vec_ref[1:2, :] += _rowsum(dvgn)
        dxh = dvgn * lng
        m1 = _group_mean(dxh, a_ref)
        m2 = _group_mean(dxh * xhat, a_ref)
        dsv_ref[...] = (rstd * (dxh - m1 - xhat * m2)).astype(BF16)

    return _fused_call(
        body, comm, (q, kvx, kvx, kvx, lse, dyc, ycat, su, sv, sink_l, bias, a128, gsum, sel, sg_lng, sg_lnb, sg_w,
                     sg_bfull, rc, rs1, rs2),
        name="mix0_bwd", grid=(nb,),
        in_specs=[_tile(BLK, ATTN_W)] + _band_specs(KVX_W, nb) + [
            pl.BlockSpec((1, 1, N_HEADS * LANE), lambda n: (n, 0, 0)), _tile(BLK, D), _tile(BLK, D),
            _tile(BLK, SG_W), _tile(BLK, SG_W), _full((1, N_HEADS * LANE)), _full((3 * BLK, LANE)),
            _full((2 * LANE, 2 * LANE)),_full((SG_W, LANE)), _full((SUBLANE, LANE)), _full((1, SG_W)), _full((1, SG_W)),
            _full((SG_GROUPS, BLK, BLK)), _full((BLK, SG_W)), _tile(BLK, LANE), _tile(BLK, LANE), _tile(BLK, LANE)],
        out_specs=[_tile(BLK, ATTN_W), _full((T + 2 * TM, 4 * LANE)), _tile(BLK, SG_W), _tile(BLK, SG_W),
                   _full((SG_GROUPS, BLK, BLK)), _full((BLK, LANE)), _full((SUBLANE, SG_W)),
                   _full((1, N_HEADS * LANE))],
        out_shape=[jax.ShapeDtypeStruct((T, ATTN_W), BF16), jax.ShapeDtypeStruct((T + 2 * TM, 4 * LANE), F32),
                   jax.ShapeDtypeStruct((T, SG_W), BF16), jax.ShapeDtypeStruct((T, SG_W), BF16),
                   jax.ShapeDtypeStruct((SG_GROUPS, BLK, BLK), F32), jax.ShapeDtypeStruct((BLK, LANE), F32),
                   jax.ShapeDtypeStruct((SUBLANE, SG_W), F32), jax.ShapeDtypeStruct((1, N_HEADS * LANE), F32)])


def _ev_in_bwd(dq, dkv, dsu, dsv, dg0, x, dxp, mod, w_in, rc, rs1, rs2, comm=None):
    T = x.shape[0]

    def body(dq_ref, dkv_ref, dsu_ref, dsv_ref, dg_ref, x_ref, dxp_ref, mod_ref, w_ref, c_ref, s1_ref, s2_ref,
             dx_ref, dwb_ref, vec_ref, dw_ref):
        i = pl.program_id(0)

        @pl.when(i == 0)
        def _():
            dw_ref[...] = jnp.zeros_like(dw_ref)
            vec_ref[...] = jnp.zeros_like(vec_ref)

        low = lax.broadcasted_iota(jnp.int32, (TM, LANE), 1) < HEAD_DIM

        def fold(j):
            t0 = dkv_ref[:, (2 * j) * LANE:(2 * j + 1) * LANE]
            t1 = dkv_ref[:, (2 * j + 1) * LANE:(2 * j + 2) * LANE]
            return jnp.where(low, t0 + pltpu.roll(t0, HEAD_DIM, 1), t1 + pltpu.roll(t1, HEAD_DIM, 1))

        dk = _rope_bwd(fold(0), c_ref[...], s1_ref[...], s2_ref[...]).astype(BF16)
        dp = jnp.concatenate([dq_ref[...], dk, fold(1).astype(BF16), dsu_ref[...], dsv_ref[...],
                              dg_ref[...]], axis=-1)
        xv = x_ref[...]
        scale0 = 1.0 + mod_ref[1:2, :]
        h0 = (xv * scale0 + mod_ref[0:1, :]).astype(BF16)
        dh0 = _dot(dp, w_ref[...])
        dw_ref[...] += _dot_tn(dp, h0)
        dx_ref[...] = dxp_ref[...] + dh0 * scale0
        vec_ref[0:1, :] += _rowsum(dh0)
        vec_ref[1:2, :] += _rowsum(dh0 * xv)

        @pl.when(i == T // TM - 1)
        def _():
            dwb_ref[...] = dw_ref[...].astype(BF16)

    t = _tile(TM, D)
    return _fused_call(
        body, comm, (dq, dkv, dsu, dsv, dg0, x, dxp, mod, w_in, rc, rs1, rs2), name="ev_in_bwd", grid=(T // TM,),
        in_specs=[_tile(TM, ATTN_W), pl.BlockSpec((TM, 4 * LANE), lambda i: (i + 1, 0)), _tile(TM, SG_W),
                  _tile(TM, SG_W), t, t, t,
                  _full((3, D)), _full((EV_IN, D)), _tile(TM, LANE), _tile(TM, LANE), _tile(TM, LANE)],
        out_specs=[t, _full((EV_IN, D)), _full((SUBLANE, D))],
        out_shape=[jax.ShapeDtypeStruct((T, D), F32), jax.ShapeDtypeStruct((EV_IN, D), BF16),
                   jax.ShapeDtypeStruct((SUBLANE, D), F32)],
        scratch_shapes=[pltpu.VMEM((EV_IN, D), F32)])


def _sum_slots(land_ref):
    g = land_ref[0].astype(F32)
    for i in range(1, land_ref.shape[0]):
        g = g + land_ref[i].astype(F32)
    return g


def _reduce_adam(land, w, m, v, name):
    R, C = w.shape
    rb = R
    if R > 512:
        for cand in (512, 256, 128, 64, 32, 16, 8):
            if R % cand == 0:
                rb = cand
                break

    def body(l_ref, w_ref, m_ref, v_ref, g_ref, d_ref, nm_ref, nv_ref):
        g = _sum_slots(l_ref)
        g_ref[...] = g
        dlt, m2, v2 = _adam(w_ref[...], g, m_ref[...], v_ref[...])
        d_ref[...] = dlt
        nm_ref[...] = m2
        nv_ref[...] = v2

    t = pl.BlockSpec((rb, C), lambda i: (i, 0))
    shp = jax.ShapeDtypeStruct((R, C), F32)
    return _pallas(
        body, name=name, grid=(R // rb,),
        in_specs=[pl.BlockSpec((land.shape[0], rb, C), lambda i: (0, i, 0)), t, t, t],
        out_specs=[t] * 4, out_shape=[shp] * 4,
        compiler_params=_params(("parallel",)),
    )(land, w, m, v)


def _tail_exchange(slabs, small):
    _, R, C = slabs.shape
    n_chips = N_DEV // 2
    gather = _GatherComm(small)
    ns = gather.n

    def body(*refs):
        slab_ref = refs[0]
        g_ins = refs[1:1 + ns]
        land_ref = refs[1 + ns]
        g_outs = refs[2 + ns:2 + 2 * ns]
        stage, part, s1_send, s1_recv, s2_send, s2_recv = refs[2 + 2 * ns:8 + 2 * ns]
        g_sems = refs[8 + 2 * ns:]
        x, y, c = _my_pos()
        chip = 2 * x + y
        gather.start(g_ins, g_outs, g_sems)

        swaps = [pltpu.make_async_remote_copy(
            src_ref=slab_ref.at[2 * k + (1 - c)], dst_ref=stage.at[k], send_sem=s1_send.at[k],
            recv_sem=s1_recv.at[k], device_id=(x, y, 1 - c), device_id_type=MESH) for k in range(n_chips)]
        for cp in swaps:
            cp.start()
        for cp in swaps:
            cp.wait()
        for k in range(n_chips):
            part[k] = (slab_ref[2 * k + c].astype(F32) + stage[k].astype(F32)).astype(BF16)

        gather.mid(g_ins, g_outs, g_sems)

        sends = []
        for r in range(1, n_chips):
            px = (1 - x) if (r & 2) else x
            py = (1 - y) if (r & 1) else y
            sends.append(pltpu.make_async_remote_copy(
                src_ref=part.at[2 * px + py], dst_ref=land_ref.at[chip], send_sem=s2_send.at[r - 1],
                recv_sem=s2_recv.at[r - 1], device_id=(px, py, c), device_id_type=MESH))
        for cp in sends:
            cp.start()
        land_ref[chip] = part[chip]
        for cp in sends:
            cp.wait()
        gather.finish(g_ins, g_outs, g_sems)

    any_spec = pl.BlockSpec(memory_space=pl.ANY)
    vmem_spec = pl.BlockSpec(memory_space=pltpu.VMEM)
    res = _pallas(
        body, name="tail_exchange",
        out_shape=[jax.ShapeDtypeStruct((n_chips, R, C), BF16)] + gather.out_shapes(),
        in_specs=[vmem_spec] + [any_spec] * ns, out_specs=[vmem_spec] + [any_spec] * ns,
        scratch_shapes=[pltpu.VMEM((n_chips, R, C), BF16), pltpu.VMEM((n_chips, R, C), BF16),
                        pltpu.SemaphoreType.DMA((n_chips,)), pltpu.SemaphoreType.DMA((n_chips,)),
                        pltpu.SemaphoreType.DMA((n_chips - 1,)), pltpu.SemaphoreType.DMA((n_chips - 1,))]
        + gather.sems(),
        compiler_params=pltpu.CompilerParams(vmem_limit_bytes=VMEM_LIMIT),
    )(slabs, *gather.arrs)
    return res[0], list(res[1:])


def _slots_adam(land, w, m, v, name):
    lead = w.shape[1] if w.ndim == 5 else 1
    inner = w.shape[-3:]
    zeros3 = (0, 0, 0)
    if w.ndim == 5:
        lspec = pl.BlockSpec((N_DEV, 1) + inner, lambda i: (0, i) + zeros3)
        wspec = pl.BlockSpec((1, 1) + inner, lambda i: (0, i) + zeros3)
    else:
        lspec = pl.BlockSpec((N_DEV,) + inner, lambda i: (0,) + zeros3)
        wspec = pl.BlockSpec((1,) + inner, lambda i: (0,) + zeros3)

    def body(l_ref, w_ref, m_ref, v_ref, g_ref, d_ref, nm_ref, nv_ref):
        at = (0, 0) if w.ndim == 5 else (0,)
        g = l_ref[(0,) + at[1:]].astype(F32)
        for i in range(1, N_DEV):
            g = g + l_ref[(i,) + at[1:]].astype(F32)
        dlt, m2, v2 = _adam(w_ref[at], g, m_ref[at], v_ref[at])
        g_ref[at] = g
        d_ref[at] = dlt
        nm_ref[at] = m2
        nv_ref[at] = v2

    shp = jax.ShapeDtypeStruct(w.shape, F32)
    return _pallas(
        body, name=name, grid=(lead,),
        in_specs=[lspec, wspec, wspec, wspec], out_specs=[wspec] * 4, out_shape=[shp] * 4,
        compiler_params=_params(("parallel",)),
    )(land, w, m, v)


SMALL_PARAMS = ("ln_g", "ln_b", "ev_sg_ln_g", "ev_sg_ln_b", "ev_sink", "ev_sg_b",
                "od_conv_w", "od_conv_b", "od_b_a", "od_b_x", "od_lam")


def _small_update(ga, gc, gd, gf, gb, ge, gsink, gbt, params):
    names = list(SMALL_PARAMS)
    flat = [a for nm in names for a in params[nm]]
    n_g = 8

    def body(*refs):
        ga_ref, gc_ref, gd_ref, gf_ref, gb_ref, ge_ref, gs_ref, gbt_ref = refs[:n_g]
        prm = refs[n_g:n_g + 3 * len(names)]
        loss_ref = refs[n_g + 3 * len(names)]
        outs = refs[n_g + 3 * len(names) + 1:]

        def ssum(ref):
            acc = ref[0]
            for i in range(1, N_DEV):
                acc = acc + ref[i]
            return acc

        a, cc, dd, ff, bb, ee = ssum(ga_ref), ssum(gc_ref), ssum(gd_ref), ssum(gf_ref), ssum(gb_ref), ssum(ge_ref)
        loss_ref[...] = a[3:4, 0:LANE]
        me = _slot(*_my_pos())

        def mine(rows):
            acc = jnp.zeros((rows.shape[0], LANE), F32)
            for j in range(N_DEV):
                acc = acc + jnp.where(me == j, rows[:, j * LANE:(j + 1) * LANE], 0.0)
            return acc

        sink_terms = ssum(gs_ref)
        lane8 = lax.broadcasted_iota(jnp.int32, (1, N_HEADS), 1)
        g_sink = jnp.zeros((1, N_HEADS), F32)
        for h in range(N_HEADS):
            tot = -jnp.sum(sink_terms[:, h * LANE:(h + 1) * LANE], axis=1, keepdims=True)
            g_sink = jnp.where(lane8 == h, tot, g_sink)
        grads = dict(
            ln_g=jnp.concatenate([dd[0:1], a[0:1]], axis=0), ln_b=jnp.concatenate([dd[1:2], a[1:2]], axis=0),
            ev_sg_ln_g=ee[0:1], ev_sg_ln_b=ee[1:2], ev_sink=g_sink,
            ev_sg_b=jnp.transpose(ssum(gbt_ref))[0:SG_GROUPS, :],
            od_conv_w=mine(cc[0:4]), od_conv_b=mine(cc[4:5]),
            od_b_a=mine(jnp.concatenate([ff[0:1], bb[0:1]], axis=0)),
            od_b_x=mine(jnp.concatenate([ff[1:2], bb[1:2]], axis=0)),
            od_lam=mine(jnp.concatenate([ff[2:3], bb[2:3]], axis=0)))
        for k, nm in enumerate(names):
            w_ref, m_ref, v_ref = prm[3 * k:3 * k + 3]
            at = (0,) if len(w_ref.shape) == 3 else ()
            g = grads[nm]
            dlt, m2, v2 = _adam(w_ref[at] if at else w_ref[...], g, m_ref[at] if at else m_ref[...],
                                v_ref[at] if at else v_ref[...])
            for o_ref, val in zip(outs[4 * k:4 * k + 4], (g, dlt, m2, v2)):
                if at:
                    o_ref[at] = val
                else:
                    o_ref[...] = val

    gathered = [ga, gc, gd, gf, gb, ge, gsink, gbt]
    out_shape = [jax.ShapeDtypeStruct((1, LANE), F32)]
    for nm in names:
        out_shape += [jax.ShapeDtypeStruct(params[nm][0].shape, F32)] * 4
    return _pallas(
        body, name="small_update", grid=(1,),
        in_specs=[_full(a.shape) for a in gathered + flat],
        out_specs=[_full(s.shape) for s in out_shape], out_shape=out_shape,
        compiler_params=_params(("arbitrary",)),
    )(*gathered, *flat)


VEC_ROWS = 16
VEC_LAYOUT = (("od_conv_w", 4), ("od_conv_b", 1), ("od_b_a", 2), ("od_b_x", 2), ("od_lam", 2))


def _pack_vec(parts):
    rows = [parts[name].reshape(nrows, -1) for name, nrows in VEC_LAYOUT]
    used = sum(r for _, r in VEC_LAYOUT)
    rows.append(jnp.zeros((VEC_ROWS - used, rows[0].shape[1]), F32))
    return jnp.concatenate(rows, axis=0)


def _to_slabs(full, cols_per):
    R = full.shape[0]
    return full.reshape(R, N_DEV, cols_per).transpose(1, 0, 2)


def _from_slabs(slabs):
    n, R, cp = slabs.shape
    return slabs.transpose(1, 0, 2).reshape(R, n * cp)


def kernel(x, c, positions, ada_w, ada_b, ln_g, ln_b, ev_w_in, ev_w_out, ev_sink, ev_sg_ln_g, ev_sg_ln_b, ev_sg_w, ev_sg_b, od_w_in, od_conv_w, od_conv_b, od_w_a, od_b_a, od_w_x, od_b_x, od_lam, od_w_out, loss_target, m_ada_w, m_ada_b, m_ln_g, m_ln_b, m_ev_w_in, m_ev_w_out, m_ev_sink, m_ev_sg_ln_g, m_ev_sg_ln_b, m_ev_sg_w, m_ev_sg_b, m_od_w_in, m_od_conv_w, m_od_conv_b, m_od_w_a, m_od_b_a, m_od_w_x, m_od_b_x, m_od_lam, m_od_w_out, v_ada_w, v_ada_b, v_ln_g, v_ln_b, v_ev_w_in, v_ev_w_out, v_ev_sink, v_ev_sg_ln_g, v_ev_sg_ln_b, v_ev_sg_w, v_ev_sg_b, v_od_w_in, v_od_conv_w, v_od_conv_b, v_od_w_a, v_od_b_a, v_od_w_x, v_od_b_x, v_od_lam, v_od_w_out):
    T = x.shape[1]
    me = _slot(*_my_pos())
    xs = x.reshape(T, D)
    tgt = loss_target.reshape(T, D)

    vec_w = _pack_vec(dict(od_conv_w=od_conv_w[0], od_conv_b=od_conv_b, od_b_a=od_b_a[0], od_b_x=od_b_x[0],
                           od_lam=od_lam[0]))
    c_all, g_ev_in, g_vec = _all_gather([c, ev_w_in[0].T.astype(BF16), vec_w], "ag_params")
    c_all = c_all.reshape(N_DEV, D)
    w_ev_in = g_ev_in.reshape(EV_IN, D)
    vec_full = _from_slabs(g_vec)
    cw, cb = vec_full[0:4], vec_full[4:5]
    ba, bx, lam = vec_full[5:7], vec_full[7:9], vec_full[9:11]

    mod_part = _mod_part(c_all, ada_w)
    (mod_all,) = _all_gather([mod_part], "ag_mod")
    mod_mine = lax.dynamic_index_in_dim(mod_all, me, axis=2, keepdims=False)
    mod = mod_mine.transpose(1, 0, 2).reshape(2, 3 * D) + ada_b
    mod0 = mod[0].reshape(3, D)
    mod1 = mod[1].reshape(3, D)

    half = 8
    inv_freq = jnp.power(jnp.float32(ROPE_THETA), -jnp.arange(half, dtype=F32) / half)
    ang = positions.reshape(T).astype(F32)[:, None] * inv_freq
    cos_t = jnp.tile(jnp.cos(ang), (1, LANE // half))
    sin_t = jnp.tile(jnp.sin(ang), (1, LANE // half))
    l64 = jnp.arange(LANE) % HEAD_DIM
    rc = jnp.where(l64 < 2 * half, cos_t, 1.0)
    rs1 = jnp.where(l64 < half, -sin_t, 0.0)
    rs2 = jnp.where((l64 >= half) & (l64 < 2 * half), sin_t, 0.0)

    ln0 = jnp.stack([ln_g[0], ln_b[0]])
    ln1 = jnp.stack([ln_g[1], ln_b[1]])
    sg_lng = ev_sg_ln_g
    sg_lnb = ev_sg_ln_b
    sg_w = ev_sg_w[0].astype(BF16)
    sg_bfull = jnp.repeat(ev_sg_b[0].T, SG_DIM, axis=1)
    sink_l = jnp.repeat(ev_sink, LANE, axis=1)
    kj = jnp.arange(3 * BLK)[:, None]
    qi = jnp.arange(BLK)[None, :]
    band_bias = jnp.where(jnp.abs(kj - BLK - qi) <= BLK, 0.0, NEG_INF).astype(F32)
    lanes = jnp.arange(LANE)
    lanes2 = jnp.arange(2 * LANE)
    a128 = jnp.where(lanes2[:, None] // SG_DIM == lanes2[None, :] // SG_DIM, 1.0 / SG_DIM, 0.0).astype(BF16)
    gsum = (jnp.arange(SG_W)[:, None] // SG_DIM == lanes[None, :]).astype(BF16)
    sel = (jnp.arange(SUBLANE)[:, None] == lanes[None, :] // HEAD_DIM).astype(BF16)
    wa = od_w_a[0].astype(BF16)
    wx = od_w_x[0].astype(BF16)

    (q, kvx, su, sv, g0), (g_ev_out,) = _ev_in(xs, mod0, w_ev_in, rc, rs1, rs2,
                                               _GatherComm([ev_w_out[0].astype(BF16)]))
    w_ev_out = g_ev_out.reshape(D, D)
    (ycat, y0, lse), (g_od_in, g_od_out) = _mix0_fwd(
        q, kvx, su, sv, g0, sink_l, band_bias, a128, sg_lng, sg_lnb, sg_w, sg_bfull,
        _GatherComm([od_w_in[0].astype(BF16), od_w_out[0].astype(BF16)]))
    w_od_in = _from_slabs(g_od_in)
    w_od_out = g_od_out.reshape(D, D)
    out0, z0, x1 = _ev_out(y0, w_ev_out, xs, mod0, ln0)
    xr, g1 = _od_in(x1, mod1, w_od_in)
    fwd_f = _rglru_fwd(xr, cw, cb, wa[0], wx[0], ba[0:1], bx[0:1], lam[0:1], False, "rglru_fwd_f")
    fwd_b = _rglru_fwd(xr, cw, cb, wa[1], wx[1], ba[1:2], bx[1:2], lam[1:2], True, "rglru_fwd_b")
    dh, dg1, dx1p, d_od_out, vec_a = _od_out(fwd_f[0], fwd_b[0], g1, w_od_out, x1, tgt, mod1, ln1)

    (dxcf, dwa_f, dwx_f, vec_f), (l_od_out,) = _rglru_bwd(
        fwd_f, dh, wa[0], wx[0], lam[0:1], False, "rglru_bwd_f",
        _ExchangeComm([d_od_out.reshape(N_DEV, D // N_DEV, D)]))
    (dxcb, dwa_b, dwx_b, vec_b), _ = _rglru_bwd(fwd_b, dh, wa[1], wx[1], lam[1:2], True, "rglru_bwd_b")
    (dx1, d_od_in, vec_c), (a_wa, a_wx) = _od_in_bwd(
        dxcf, dxcb, xr, dg1, x1, dx1p, mod1, w_od_in, cw,
        _GatherComm([jnp.stack([dwa_f, dwa_b]).astype(BF16), jnp.stack([dwx_f, dwx_b]).astype(BF16)],
                    mid_frac=0.75))
    dxp, dyc, dg0, d_ev_out, vec_d = _ev_out_bwd(dx1, z0, out0, y0, ycat, g0, w_ev_out, mod0, ln0)
    (dq, dkv, dsu, dsv, d_sg_w, d_sg_bt, vec_e, d_sink_l), (l_od_in, l_ev_out) = _mix0_bwd(
        q, kvx, lse, dyc, ycat, su, sv, sink_l, band_bias, a128, gsum, sel, sg_lng, sg_lnb, sg_w, sg_bfull,
        rc, rs1, rs2, _ExchangeComm([d_od_in, d_ev_out.reshape(N_DEV, D // N_DEV, D)]))
    (grad_x, d_ev_in, vec_g), _ = _ev_in_bwd(dq, dkv, dsu, dsv, dg0, xs, dxp, mod0, w_ev_in, rc, rs1, rs2)

    l_ev_in, (ga, gc, gd, gf, gb, gg, ge, gsink, gbt, a_sgw) = _tail_exchange(
        d_ev_in.reshape(N_DEV, EV_IN // N_DEV, D),
        [vec_a, vec_c, vec_d, vec_f, vec_b, vec_g, vec_e, d_sink_l, d_sg_bt, d_sg_w.astype(BF16)])

    dmod_all = jnp.stack([jnp.concatenate([gg[:, 0], gg[:, 1], gd[:, 2]], axis=-1),
                          jnp.concatenate([gc[:, 5], gc[:, 6], ga[:, 2]], axis=-1)], axis=1)
    cols = ada_w.shape[2]
    dmod_cols = lax.dynamic_slice_in_dim(dmod_all, me * cols, cols, axis=2).transpose(1, 0, 2)
    (g_ada_w, d_ada_w, nm_ada_w, nv_ada_w, g_ada_b, d_ada_b, nm_ada_b, nv_ada_b) = _ada_update(
        c_all, dmod_cols, dmod_all, ada_w, m_ada_w, v_ada_w, ada_b, m_ada_b, v_ada_b)

    res = dict(ada_w=[g_ada_w, d_ada_w, nm_ada_w, nv_ada_w], ada_b=[g_ada_b, d_ada_b, nm_ada_b, nv_ada_b])
    res["ev_w_in"] = [a.T[None] for a in _reduce_adam(l_ev_in, ev_w_in[0].T, m_ev_w_in[0].T, v_ev_w_in[0].T,
                                                      "adam_ev_w_in")]
    for name, land, w, m, v in (("ev_w_out", l_ev_out, ev_w_out, m_ev_w_out, v_ev_w_out),
                                ("od_w_in", l_od_in, od_w_in, m_od_w_in, v_od_w_in),
                                ("od_w_out", l_od_out, od_w_out, m_od_w_out, v_od_w_out)):
        res[name] = [a[None] for a in _reduce_adam(land, w[0], m[0], v[0], "adam_" + name)]
    res["od_w_a"] = _slots_adam(a_wa, od_w_a, m_od_w_a, v_od_w_a, "adam_od_w_a")
    res["od_w_x"] = _slots_adam(a_wx, od_w_x, m_od_w_x, v_od_w_x, "adam_od_w_x")
    res["ev_sg_w"] = _slots_adam(a_sgw, ev_sg_w, m_ev_sg_w, v_ev_sg_w, "adam_ev_sg_w")
    small = dict(ln_g=(ln_g, m_ln_g, v_ln_g), ln_b=(ln_b, m_ln_b, v_ln_b),
                 ev_sg_ln_g=(ev_sg_ln_g, m_ev_sg_ln_g, v_ev_sg_ln_g),
                 ev_sg_ln_b=(ev_sg_ln_b, m_ev_sg_ln_b, v_ev_sg_ln_b),
                 ev_sink=(ev_sink, m_ev_sink, v_ev_sink), ev_sg_b=(ev_sg_b, m_ev_sg_b, v_ev_sg_b),
                 od_conv_w=(od_conv_w, m_od_conv_w, v_od_conv_w), od_conv_b=(od_conv_b, m_od_conv_b, v_od_conv_b),
                 od_b_a=(od_b_a, m_od_b_a, v_od_b_a), od_b_x=(od_b_x, m_od_b_x, v_od_b_x),
                 od_lam=(od_lam, m_od_lam, v_od_lam))
    small_out = _small_update(ga, gc, gd, gf, gb, ge, gsink, gbt, small)
    loss = small_out[0][0, 0]
    for k, name in enumerate(SMALL_PARAMS):
        res[name] = small_out[1 + 4 * k:5 + 4 * k]

    order = ["ada_w", "ada_b", "ln_g", "ln_b", "ev_w_in", "ev_w_out", "ev_sink", "ev_sg_ln_g", "ev_sg_ln_b",
             "ev_sg_w", "ev_sg_b", "od_w_in", "od_conv_w", "od_conv_b", "od_w_a", "od_b_a", "od_w_x", "od_b_x",
             "od_lam", "od_w_out"]
    outs = [loss, grad_x.reshape(1, T, D)]
    for kind in range(4):
        outs += [res[name][kind] for name in order]
    return tuple(outs)
```

```python
import functools

import jax
import jax.numpy as jnp
from jax import lax
from jax.experimental import pallas as pl
from jax.experimental.pallas import tpu as pltpu

F32 = jnp.float32
BF16 = jnp.bfloat16

N_DEV = 8
D = 1024
N_HEADS = 8
HEAD_DIM = 64
KV_WIDTH = 128
ATTN_W = 512
SG_W = 512
SG_GROUPS = 8
SG_DIM = 64
BLK = 128
KVX_W = 1024
EV_IN = 2816
OD_IN = 2048
RNN_HEADS = 8
RNN_HD = 128
ALPHA = 4.0 ** 0.25
LN_EPS = 1e-5
NEG_INF = -1e30
RG_C = 8.0
ROPE_THETA = 500000.0
LR, B1, B2, EPS, WD, STEP = 0.001, 0.9, 0.999, 1e-08, 0.01, 10

LANE = 128
SUBLANE = 8
TM = 256
TMF = 512
TS = 256
VMEM_LIMIT = 56 * 1024 * 1024

MESH = pl.DeviceIdType.MESH


def _pallas(body, **kw):
    return pl.pallas_call(body, **kw)


def _params(sem, vmem=VMEM_LIMIT):
    return pltpu.CompilerParams(dimension_semantics=sem, vmem_limit_bytes=vmem)


def _sigmoid(x):
    return 0.5 * jnp.tanh(0.5 * x) + 0.5


def _silu_and_grad(x):
    s = _sigmoid(x)
    return x * s, s * (1.0 + x * (1.0 - s))


def _dot(a, b):
    return jnp.dot(a.astype(BF16), b.astype(BF16), preferred_element_type=F32)


def _dot_nt(a, b):
    return lax.dot_general(a.astype(BF16), b.astype(BF16), (((1,), (1,)), ((), ())), preferred_element_type=F32)


def _dot_tn(a, b):
    return lax.dot_general(a.astype(BF16), b.astype(BF16), (((0,), (0,)), ((), ())), preferred_element_type=F32)


def _ln_fwd(z, g, b):
    mu = jnp.mean(z, axis=-1, keepdims=True)
    zc = z - mu
    var = jnp.mean(zc * zc, axis=-1, keepdims=True)
    rstd = lax.rsqrt(var + LN_EPS)
    xhat = zc * rstd
    return xhat * g + b, xhat, rstd


def _ln_bwd(dy, xhat, rstd, g):
    dxh = dy * g
    m1 = jnp.mean(dxh, axis=-1, keepdims=True)
    m2 = jnp.mean(dxh * xhat, axis=-1, keepdims=True)
    return rstd * (dxh - m1 - xhat * m2)


def _rowsum(v):
    return jnp.sum(v, axis=0, keepdims=True)


def _rope_fwd(t, c, s1, s2):
    return t * c + pltpu.roll(t, LANE - 8, 1) * s1 + pltpu.roll(t, 8, 1) * s2


def _rope_bwd(d, c, s1, s2):
    return d * c + pltpu.roll(d * s1, 8, 1) + pltpu.roll(d * s2, LANE - 8, 1)


def _adam(w, g, m, v):
    m2 = B1 * m + (1.0 - B1) * g
    v2 = B2 * v + (1.0 - B2) * (g * g)
    m_hat = m2 / (1.0 - B1 ** STEP)
    v_hat = v2 / (1.0 - B2 ** STEP)
    delta = -LR * (m_hat / (jnp.sqrt(v_hat) + EPS) + WD * w)
    return delta, m2, v2


def _tile(rows, width):
    return pl.BlockSpec((rows, width), lambda i: (i, 0))


def _full(shape):
    zeros = (0,) * len(shape)
    return pl.BlockSpec(shape, lambda i: zeros)


def _rev_tile(rows, width, n, reverse):
    if reverse:
        return pl.BlockSpec((rows, width), lambda i: (n - 1 - i, 0))
    return pl.BlockSpec((rows, width), lambda i: (i, 0))


def _halo_specs(rows, width, n, total_rows, reverse):
    per = rows // SUBLANE
    last = total_rows // SUBLANE - 1

    def tile_of(i):
        return (n - 1 - i) if reverse else i

    prev = pl.BlockSpec((SUBLANE, width), lambda i: (jnp.maximum(tile_of(i) * per - 1, 0), 0))
    nxt = pl.BlockSpec((SUBLANE, width), lambda i: (jnp.minimum((tile_of(i) + 1) * per, last), 0))
    return prev, nxt


def _my_pos():
    return lax.axis_index("x"), lax.axis_index("y"), lax.axis_index("c")


def _slot(px, py, pc):
    return 4 * px + 2 * py + pc


class _GatherComm:
    has_mid = True

    def __init__(self, arrs, mid_frac=0.5):
        self.arrs = list(arrs)
        self.n = len(self.arrs)
        self.mid_frac = mid_frac

    def out_shapes(self):
        return [jax.ShapeDtypeStruct((N_DEV,) + a.shape, a.dtype) for a in self.arrs]

    def sems(self):
        return [pltpu.SemaphoreType.DMA((7 * self.n,)), pltpu.SemaphoreType.DMA((7 * self.n,)),
                pltpu.SemaphoreType.DMA((self.n,))]

    def _parts(self, ins, outs, sems):
        send_sems, recv_sems, local_sems = sems
        x, y, c = _my_pos()
        me, sibling = (x, y, c), (x, y, 1 - c)
        chips = [(1 - x, y), (x, 1 - y), (1 - x, 1 - y)]

        def copy(a, k, block, to, src=None):
            dst = outs[a].at[_slot(*block)]
            return pltpu.make_async_remote_copy(
                src_ref=dst if src is None else src, dst_ref=dst,
                send_sem=send_sems.at[a * 7 + k], recv_sem=recv_sems.at[a * 7 + k],
                device_id=to, device_id_type=MESH)

        local = [pltpu.make_async_copy(ins[a], outs[a].at[_slot(*me)], local_sems.at[a]) for a in range(self.n)]
        first = []
        for a in range(self.n):
            first.append(copy(a, 0, me, sibling, src=ins[a]))
            first += [copy(a, 1 + j, me, (*chip, c), src=ins[a]) for j, chip in enumerate(chips)]
        ici_in = [copy(a, 1 + j, (*chip, c), me) for j, chip in enumerate(chips) for a in range(self.n)]
        passed = [copy(a, 4 + j, (*chip, c), sibling) for j, chip in enumerate(chips) for a in range(self.n)]
        d2d_in = []
        for a in range(self.n):
            d2d_in.append(copy(a, 0, sibling, me))
            d2d_in += [copy(a, 4 + j, (*chip, 1 - c), me) for j, chip in enumerate(chips)]
        return local, first, ici_in, passed, d2d_in

    def start(self, ins, outs, sems):
        local, first, _, _, _ = self._parts(ins, outs, sems)
        for cp in local + first:
            cp.start()

    def mid(self, ins, outs, sems):
        _, _, ici_in, passed, _ = self._parts(ins, outs, sems)
        for arrived, fw in zip(ici_in, passed):
            arrived.wait_recv()
            fw.start()

    def finish(self, ins, outs, sems):
        local, first, _, passed, d2d_in = self._parts(ins, outs, sems)
        for cp in d2d_in:
            cp.wait_recv()
        for cp in first + passed:
            cp.wait_send()
        for cp in local:
            cp.wait()


class _ExchangeComm:
    has_mid = False

    def __init__(self, arrs):
        self.arrs = list(arrs)
        self.n = len(self.arrs)

    def out_shapes(self):
        return [jax.ShapeDtypeStruct(a.shape, a.dtype) for a in self.arrs]

    def sems(self):
        return [pltpu.SemaphoreType.DMA((7 * self.n,)), pltpu.SemaphoreType.DMA((7 * self.n,)),
                pltpu.SemaphoreType.DMA((self.n,))]

    def _copies(self, ins, outs, sems):
        send_sems, recv_sems, local_sems = sems
        x, y, c = _my_pos()
        mine = _slot(x, y, c)
        copies = [pltpu.make_async_copy(ins[a].at[mine], outs[a].at[mine], local_sems.at[a]) for a in range(self.n)]
        for k in range(1, N_DEV):
            px = (1 - x) if (k & 4) else x
            py = (1 - y) if (k & 2) else y
            pc = (1 - c) if (k & 1) else c
            for a in range(self.n):
                copies.append(pltpu.make_async_remote_copy(
                    src_ref=ins[a].at[_slot(px, py, pc)], dst_ref=outs[a].at[mine],
                    send_sem=send_sems.at[a * 7 + k - 1], recv_sem=recv_sems.at[a * 7 + k - 1],
                    device_id=(px, py, pc), device_id_type=MESH))
        return copies

    def start(self, ins, outs, sems):
        for cp in self._copies(ins, outs, sems):
            cp.start()

    def finish(self, ins, outs, sems):
        for cp in self._copies(ins, outs, sems):
            cp.wait()


def _fused_call(body, comm, operands, *, name, grid, in_specs, out_specs, out_shape, scratch_shapes=(),
                semantics=("arbitrary",)):
    n_in, n_out, n_scr = len(in_specs), len(out_specs), len(scratch_shapes)
    if comm is None:
        res = _pallas(body, name=name, grid=grid, in_specs=list(in_specs), out_specs=list(out_specs),
                      out_shape=list(out_shape), scratch_shapes=list(scratch_shapes),
                      compiler_params=_params(semantics))(*operands)
        return list(res), []
    k = comm.n
    steps = grid[0]

    def wrapped(*refs):
        ins, cins = refs[:n_in], refs[n_in:n_in + k]
        outs = refs[n_in + k:n_in + k + n_out]
        couts = refs[n_in + k + n_out:n_in + 2 * k + n_out]
        rest = refs[n_in + 2 * k + n_out:]
        scratch, sems = rest[:n_scr], rest[n_scr:]
        i = pl.program_id(0)

        @pl.when(i == 0)
        def _():
            comm.start(cins, couts, sems)

        body(*ins, *outs, *scratch)

        if comm.has_mid:
            @pl.when(i == int(steps * comm.mid_frac))
            def _():
                comm.mid(cins, couts, sems)

        @pl.when(i == steps - 1)
        def _():
            comm.finish(cins, couts, sems)

    any_spec = pl.BlockSpec(memory_space=pl.ANY)
    res = _pallas(wrapped, name=name, grid=grid, in_specs=list(in_specs) + [any_spec] * k,
                  out_specs=list(out_specs) + [any_spec] * k, out_shape=list(out_shape) + comm.out_shapes(),
                  scratch_shapes=list(scratch_shapes) + comm.sems(),
                  compiler_params=_params(("arbitrary",)))(*operands, *comm.arrs)
    return list(res[:n_out]), list(res[n_out:])


def _head_gather(c, ada_w, big):
    cols = ada_w.shape[2]
    g_c, g_big = _GatherComm([c]), _GatherComm(big)
    g_mod = _GatherComm([jax.ShapeDtypeStruct((2, N_DEV, cols), F32)])
    nb = g_big.n

    def body(*refs):
        c_ref, w_ref = refs[0], refs[1]
        big_in = refs[2:2 + nb]
        c_all_ref, mod_all_ref = refs[2 + nb], refs[3 + nb]
        big_out = refs[4 + nb:4 + 2 * nb]
        part_ref = refs[4 + 2 * nb]
        sems = refs[5 + 2 * nb:]
        s_c, s_mod, s_big = sems[0:3], sems[3:6], sems[6:9]
        g_c.start([c_ref], [c_all_ref], s_c)
        g_big.start(big_in, big_out, s_big)
        g_c.mid([c_ref], [c_all_ref], s_c)
        g_c.finish([c_ref], [c_all_ref], s_c)
        cv = c_all_ref[:, 0, :]
        cond = cv * _sigmoid(cv)
        for l in range(2):
            part_ref[l] = _dot(cond, w_ref[l])
        g_mod.start([part_ref], [mod_all_ref], s_mod)
        g_mod.mid([part_ref], [mod_all_ref], s_mod)
        g_mod.finish([part_ref], [mod_all_ref], s_mod)
        g_big.mid(big_in, big_out, s_big)
        g_big.finish(big_in, big_out, s_big)

    any_spec = pl.BlockSpec(memory_space=pl.ANY)
    vmem_spec = pl.BlockSpec(memory_space=pltpu.VMEM)
    res = _pallas(
        body, name="head_gather",
        out_shape=g_c.out_shapes() + g_mod.out_shapes() + g_big.out_shapes(),
        in_specs=[vmem_spec, vmem_spec] + [any_spec] * nb,
        out_specs=[vmem_spec, vmem_spec] + [any_spec] * nb,
        scratch_shapes=[pltpu.VMEM((2, N_DEV, cols), F32)] + g_c.sems() + g_mod.sems() + g_big.sems(),
        compiler_params=pltpu.CompilerParams(vmem_limit_bytes=VMEM_LIMIT),
    )(c, ada_w, *big)
    return res[0], res[1], list(res[2:])


def _ada_update(c_all, dmod_cols, dmod_all, ada_w, m_w, v_w, ada_b, m_b, v_b):
    cols = ada_w.shape[2]
    nb = ada_b.shape[1]

    def body(c_ref, dmc_ref, dma_ref, w_ref, mw_ref, vw_ref, b_ref, mb_ref, vb_ref,
             gw_ref, dw_ref, nmw_ref, nvw_ref, gb_ref, db_ref, nmb_ref, nvb_ref):
        cv = c_ref[...]
        cond = cv * _sigmoid(cv)
        for l in range(2):
            g = _dot_tn(cond, dmc_ref[l])
            gw_ref[l] = g
            dlt, m2, v2 = _adam(w_ref[l], g, mw_ref[l], vw_ref[l])
            dw_ref[l] = dlt
            nmw_ref[l] = m2
            nvw_ref[l] = v2
        gb = dma_ref[0]
        for i in range(1, N_DEV):
            gb = gb + dma_ref[i]
        gb_ref[...] = gb
        dlt, m2, v2 = _adam(b_ref[...], gb, mb_ref[...], vb_ref[...])
        db_ref[...] = dlt
        nmb_ref[...] = m2
        nvb_ref[...] = v2

    wspec = _full((2, D, cols))
    bspec = _full((2, nb))
    wshape = jax.ShapeDtypeStruct((2, D, cols), F32)
    bshape = jax.ShapeDtypeStruct((2, nb), F32)
    return _pallas(
        body, name="ada_update", grid=(1,),
        in_specs=[_full((N_DEV, D)), _full((2, N_DEV, cols)), _full((N_DEV, 2, nb)),
                  wspec, wspec, wspec, bspec, bspec, bspec],
        out_specs=[wspec] * 4 + [bspec] * 4,
        out_shape=[wshape] * 4 + [bshape] * 4,
        compiler_params=_params(("arbitrary",)),
    )(c_all, dmod_cols, dmod_all, ada_w, m_w, v_w, ada_b, m_b, v_b)


def _ev_in(x, mod, w_in, rc, rs1, rs2, comm=None):
    T = x.shape[0]

    def body(x_ref, mod_ref, w_ref, c_ref, s1_ref, s2_ref, q_ref, kv_ref, su_ref, sv_ref, g_ref):
        h = x_ref[...] * (1.0 + mod_ref[1:2, :]) + mod_ref[0:1, :]
        p = _dot_nt(h, w_ref[...])
        c, s1, s2 = c_ref[...], s1_ref[...], s2_ref[...]
        for j in range(ATTN_W // LANE):
            qr = _rope_fwd(p[:, j * LANE:(j + 1) * LANE], c, s1, s2)
            q_ref[:, j * LANE:(j + 1) * LANE] = (qr * (HEAD_DIM ** -0.5)).astype(BF16)
        low = lax.broadcasted_iota(jnp.int32, (TMF, LANE), 1) < HEAD_DIM
        for j, val in enumerate((_rope_fwd(p[:, 512:640], c, s1, s2), p[:, 640:768])):
            swapped = pltpu.roll(val, HEAD_DIM, 1)
            tiles = (jnp.where(low, val, 0.0), jnp.where(low, 0.0, swapped),
                     jnp.where(low, swapped, 0.0), jnp.where(low, 0.0, val))
            for k, tile in enumerate(tiles):
                kv_ref[:, (4 * j + k) * LANE:(4 * j + k + 1) * LANE] = tile.astype(BF16)
        su_ref[...] = p[:, 768:1280].astype(BF16)
        sv_ref[...] = p[:, 1280:1792].astype(BF16)
        g_ref[...] = p[:, 1792:2816].astype(BF16)

    sh = lambda w: jax.ShapeDtypeStruct((T, w), BF16)
    return _fused_call(
        body, comm, (x, mod, w_in, rc, rs1, rs2), name="ev_in", grid=(T // TMF,),
        in_specs=[_tile(TMF, D), _full((3, D)), _full((EV_IN, D)), _tile(TMF, LANE), _tile(TMF, LANE),
                  _tile(TMF, LANE)],
        out_specs=[_tile(TMF, ATTN_W), _tile(TMF, KVX_W), _tile(TMF, SG_W), _tile(TMF, SG_W), _tile(TMF, D)],
        out_shape=[sh(ATTN_W), sh(KVX_W), sh(SG_W), sh(SG_W), sh(D)], semantics=("parallel",))


def _band_specs(width, nb):
    return [pl.BlockSpec((BLK, width), lambda n: (jnp.maximum(n - 1, 0), 0)),
            pl.BlockSpec((BLK, width), lambda n: (n, 0)),
            pl.BlockSpec((BLK, width), lambda n: (jnp.minimum(n + 1, nb - 1), 0))]


def _band_bias(bias_ref, n, nb):
    rows = lax.broadcasted_iota(jnp.int32, (3 * BLK, 1), 0)
    outside = ((rows < BLK) & (n == 0)) | ((rows >= 2 * BLK) & (n == nb - 1))
    return bias_ref[...] + jnp.where(outside, NEG_INF, 0.0)


def _lane_tile(ref, t):
    return ref[:, t * LANE:(t + 1) * LANE]


def _split_bf16(v):
    hi = v.astype(BF16)
    return hi, (v - hi.astype(F32)).astype(BF16)


def _group_mean(v, a_ref, exact_bf16=False):
    hi, lo = _split_bf16(v)
    a = a_ref[...]
    out = []
    for t in range(SG_W // (2 * LANE)):
        sl = slice(t * 2 * LANE, (t + 1) * 2 * LANE)
        r = jnp.dot(hi[:, sl], a, preferred_element_type=F32)
        if not exact_bf16:
            r = r + jnp.dot(lo[:, sl], a, preferred_element_type=F32)
        out.append(r)
    return jnp.concatenate(out, axis=-1)


def _sg_core(sv_ref, lng, lnb, a_ref, w_ref, bfull_ref):
    svf = sv_ref[...].astype(F32)
    xc = svf - _group_mean(svf, a_ref, exact_bf16=True)
    rstd = lax.rsqrt(_group_mean(xc * xc, a_ref) + LN_EPS)
    xhat = xc * rstd
    vb = (xhat * lng + lnb).astype(BF16)
    low = lax.broadcasted_iota(jnp.int32, (BLK, LANE), 1) < SG_DIM
    tiles = []
    for t in range(SG_W // LANE):
        v2 = vb[:, t * LANE:(t + 1) * LANE]
        r0 = jnp.dot(w_ref[2 * t], v2, preferred_element_type=F32)
        r1 = jnp.dot(w_ref[2 * t + 1], v2, preferred_element_type=F32)
        tiles.append(jnp.where(low, r0, r1))
    svm = jnp.concatenate(tiles, axis=-1) + bfull_ref[...]
    return xhat, rstd, vb, svm


def _mix0_fwd(q, kvx, su, sv, g0, sink_l, bias, a128, sg_lng, sg_lnb, sg_w, sg_bfull, comm=None):
    T = q.shape[0]
    nb = T // BLK

    def body(q_ref, kp_ref, kc_ref, kn_ref, su_ref, sv_ref, g_ref, sink_ref, bias_ref, a_ref, lng_ref, lnb_ref,
             w_ref, bfull_ref, ycat_ref, y0_ref, lse_ref):
        n = pl.program_id(0)
        bias = _band_bias(bias_ref, n, nb)
        kvx = jnp.concatenate([kp_ref[...], kc_ref[...], kn_ref[...]], axis=0)
        tiles = []
        for t in range(ATTN_W // LANE):
            qt = _lane_tile(q_ref, t)
            acc = None
            for par in range(2):
                h = 2 * t + par
                kt = 2 * (h // 4) + par
                ke = kvx[:, kt * LANE:(kt + 1) * LANE]
                ve = kvx[:, (4 + kt) * LANE:(5 + kt) * LANE]
                st = _dot_nt(ke, qt) + bias
                sk = _lane_tile(sink_ref, h)
                m = jnp.maximum(jnp.max(st, axis=0, keepdims=True), sk)
                p = jnp.exp(st - m)
                denom = jnp.sum(p, axis=0, keepdims=True) + jnp.exp(sk - m)
                contrib = _dot_tn(p * (1.0 / denom), ve)
                acc = contrib if acc is None else acc + contrib
                lse_ref[0, :, h * LANE:(h + 1) * LANE] = m + jnp.log(denom)
            tiles.append(acc)
        _, _, _, svm = _sg_core(sv_ref, lng_ref[...], lnb_ref[...], a_ref, w_ref, bfull_ref)
        tiles.append(su_ref[...].astype(F32) * svm)
        ycat = jnp.concatenate(tiles, axis=-1)
        gf = g_ref[...].astype(F32)
        ycat_ref[...] = ycat.astype(BF16)
        y0_ref[...] = (ycat * (gf * _sigmoid(gf))).astype(BF16)

    return _fused_call(
        body, comm, (q, kvx, kvx, kvx, su, sv, g0, sink_l, bias, a128, sg_lng, sg_lnb, sg_w, sg_bfull),
        name="mix0_fwd", grid=(nb,),
        in_specs=[_tile(BLK, ATTN_W)] + _band_specs(KVX_W, nb) + [
            _tile(BLK, SG_W), _tile(BLK, SG_W), _tile(BLK, D), _full((1, N_HEADS * LANE)), _full((3 * BLK, LANE)),
            _full((2 * LANE, 2 * LANE)),_full((1, SG_W)), _full((1, SG_W)), _full((SG_GROUPS, BLK, BLK)),
            _full((BLK, SG_W))],
        out_specs=[_tile(BLK, D), _tile(BLK, D), pl.BlockSpec((1, 1, N_HEADS * LANE), lambda n: (n, 0, 0))],
        out_shape=[jax.ShapeDtypeStruct((T, D), BF16), jax.ShapeDtypeStruct((T, D), BF16),
                   jax.ShapeDtypeStruct((nb, 1, N_HEADS * LANE), F32)], semantics=("parallel",))


def _ev_out(y0, w_out, x, mod, lnp):
    T = x.shape[0]

    def body(y_ref, w_ref, x_ref, mod_ref, ln_ref, out_ref, z_ref, x1_ref):
        out = _dot(y_ref[...], w_ref[...])
        z = ALPHA * x_ref[...] + mod_ref[2:3, :] * out
        x1, _, _ = _ln_fwd(z, ln_ref[0:1, :], ln_ref[1:2, :])
        out_ref[...] = out.astype(BF16)
        z_ref[...] = z
        x1_ref[...] = x1

    return _pallas(
        body, name="ev_out", grid=(T // TMF,),
        in_specs=[_tile(TMF, D), _full((D, D)), _tile(TMF, D), _full((3, D)), _full((2, D))],
        out_specs=[_tile(TMF, D)] * 3,
        out_shape=[jax.ShapeDtypeStruct((T, D), BF16), jax.ShapeDtypeStruct((T, D), F32),
                   jax.ShapeDtypeStruct((T, D), F32)],
        compiler_params=_params(("parallel",)),
    )(y0, w_out, x, mod, lnp)


def _od_in(x1, mod, w_in):
    T = x1.shape[0]

    def body(x_ref, mod_ref, w_ref, xr_ref, g_ref):
        h = x_ref[...] * (1.0 + mod_ref[1:2, :]) + mod_ref[0:1, :]
        p = _dot(h, w_ref[...])
        xr_ref[...] = p[:, :D]
        g_ref[...] = p[:, D:].astype(BF16)

    return _pallas(
        body, name="od_in", grid=(T // TMF,),
        in_specs=[_tile(TMF, D), _full((3, D)), _full((D, OD_IN))],
        out_specs=[_tile(TMF, D), _tile(TMF, D)],
        out_shape=[jax.ShapeDtypeStruct((T, D), F32), jax.ShapeDtypeStruct((T, D), BF16)],
        compiler_params=_params(("parallel",)),
    )(x1, mod, w_in)


def _ext_rows(prev_ref, cur, next_ref, j, n):
    prev = jnp.where(j > 0, prev_ref[...], 0.0)
    nxt = jnp.where(j < n - 1, next_ref[...], 0.0)
    return jnp.concatenate([prev, cur, nxt], axis=0)


def _shift_rows(ext, off, rows):
    total = ext.shape[0]
    if off == 0:
        return ext[SUBLANE:SUBLANE + rows, :]
    return pltpu.roll(ext, (-off) % total, 0)[SUBLANE:SUBLANE + rows, :]


def _conv_fwd(ext, cw, cb, rows):
    xc = cb
    for k in range(4):
        xc = xc + cw[k:k + 1, :] * _shift_rows(ext, k - 2, rows)
    return xc


def _gates(xc, wa_ref, wx_ref, ba, bx, lam):
    pr, pi = [], []
    for h in range(RNN_HEADS):
        xh = xc[:, h * RNN_HD:(h + 1) * RNN_HD].astype(BF16)
        pr.append(_dot(xh, wa_ref[h]))
        pi.append(_dot(xh, wx_ref[h]))
    r = _sigmoid(jnp.concatenate(pr, axis=-1) + ba)
    ig = _sigmoid(jnp.concatenate(pi, axis=-1) + bx)
    sp = jnp.maximum(-lam, 0.0) + jnp.log(1.0 + jnp.exp(-jnp.abs(lam)))
    neg_log_a = RG_C * r * sp
    a = jnp.exp(-neg_log_a)
    s2 = (1.0 + a * a) * jnp.tanh(neg_log_a)
    inv_s = lax.rsqrt(jnp.maximum(s2, 1e-30))
    return r, ig, sp, a, s2 * inv_s, inv_s


def _scan_tile(a_ref, b_ref, o_ref, carry_ref, rows, reverse):
    ridx = lax.broadcasted_iota(jnp.int32, (SUBLANE, D), 0)
    groups = rows // SUBLANE

    def group(gi, h):
        g = (groups - 1 - gi) if reverse else gi
        off = pl.multiple_of(g * SUBLANE, SUBLANE)
        a = a_ref[pl.ds(off, SUBLANE), :]
        b = b_ref[pl.ds(off, SUBLANE), :]
        for sh in (1, 2, 4):
            if reverse:
                keep = ridx < SUBLANE - sh
                a_p = jnp.where(keep, pltpu.roll(a, SUBLANE - sh, 0), 1.0)
                b_p = jnp.where(keep, pltpu.roll(b, SUBLANE - sh, 0), 0.0)
            else:
                keep = ridx >= sh
                a_p = jnp.where(keep, pltpu.roll(a, sh, 0), 1.0)
                b_p = jnp.where(keep, pltpu.roll(b, sh, 0), 0.0)
            b = b + a * b_p
            a = a * a_p
        hh = b + a * h
        o_ref[pl.ds(off, SUBLANE), :] = hh
        return hh[0:1, :] if reverse else hh[SUBLANE - 1:SUBLANE, :]

    carry_ref[...] = lax.fori_loop(0, groups, group, carry_ref[...])


def _rglru_fwd(xr, cw, cb, wa, wx, ba, bx, lam, reverse, name):
    T = xr.shape[0]
    n = T // TS
    prev_spec, next_spec = _halo_specs(TS, D, n, T, reverse)

    def body(prev_ref, cur_ref, next_ref, cw_ref, cb_ref, wa_ref, wx_ref, ba_ref, bx_ref, lam_ref,
             h_ref, a_ref, s_ref, r_ref, ig_ref, xc_ref, b_s, carry):
        i = pl.program_id(0)
        j = (n - 1 - i) if reverse else i

        @pl.when(i == 0)
        def _():
            carry[...] = jnp.zeros_like(carry)

        ext = _ext_rows(prev_ref, cur_ref[...], next_ref, j, n)
        xc = _conv_fwd(ext, cw_ref[...], cb_ref[...], TS)
        r, ig, _, a, s, _ = _gates(xc, wa_ref, wx_ref, ba_ref[...], bx_ref[...], lam_ref[...])
        s_ref[...] = s
        r_ref[...] = r.astype(BF16)
        ig_ref[...] = ig.astype(BF16)
        xc_ref[...] = xc.astype(BF16)
        a_ref[...] = a
        b_s[...] = s * ig * xc
        _scan_tile(a_ref, b_s, h_ref, carry, TS, reverse)

    wspec = _full((RNN_HEADS, RNN_HD, RNN_HD))
    cur = _rev_tile(TS, D, n, reverse)
    f32 = jax.ShapeDtypeStruct((T, D), F32)
    b16 = jax.ShapeDtypeStruct((T, D), BF16)
    return _pallas(
        body, name=name, grid=(n,),
        in_specs=[prev_spec, cur, next_spec, _full((4, D)), _full((1, D)),
                  wspec, wspec, _full((1, D)), _full((1, D)), _full((1, D))],
        out_specs=[cur] * 6,
        out_shape=[f32, f32, f32, b16, b16, b16],
        scratch_shapes=[pltpu.VMEM((TS, D), F32), pltpu.VMEM((1, D), F32)],
        compiler_params=_params(("arbitrary",)),
    )(xr, xr, xr, cw, cb, wa, wx, ba, bx, lam)


def _od_out(hf, hb, g1, w_out, x1, tgt, mod, lnp):
    T = x1.shape[0]

    def body(hf_ref, hb_ref, g_ref, w_ref, x_ref, t_ref, mod_ref, ln_ref,
             dh_ref, dg_ref, dx_ref, dwb_ref, vec_ref, dw_ref):
        i = pl.program_id(0)

        @pl.when(i == 0)
        def _():
            dw_ref[...] = jnp.zeros_like(dw_ref)
            vec_ref[...] = jnp.zeros_like(vec_ref)

        hs = hf_ref[...] + hb_ref[...]
        sg, dsg = _silu_and_grad(g_ref[...].astype(F32))
        yr = (hs * sg).astype(BF16)
        w = w_ref[...]
        out = _dot(yr, w)
        gate = mod_ref[2:3, :]
        z = ALPHA * x_ref[...] + gate * out
        lng = ln_ref[0:1, :]
        x2, xhat, rstd = _ln_fwd(z, lng, ln_ref[1:2, :])
        diff = x2 - t_ref[...]
        vec_ref[3:4, 0:LANE] += 0.5 * jnp.sum(diff * diff) * (1.0 / D)
        dx2 = diff * (1.0 / D)
        dz = _ln_bwd(dx2, xhat, rstd, lng)
        vec_ref[0:1, :] += _rowsum(dx2 * xhat)
        vec_ref[1:2, :] += _rowsum(dx2)
        vec_ref[2:3, :] += _rowsum(dz * out)
        dout = (dz * gate).astype(BF16)
        dyr = _dot_nt(dout, w)
        dw_ref[...] += _dot_tn(yr, dout)
        dh_ref[...] = dyr * sg
        dg_ref[...] = (dyr * hs * dsg).astype(BF16)
        dx_ref[...] = ALPHA * dz

        @pl.when(i == T // TM - 1)
        def _():
            dwb_ref[...] = dw_ref[...].astype(BF16)

    return _pallas(
        body, name="od_out", grid=(T // TM,),
        in_specs=[_tile(TM, D), _tile(TM, D), _tile(TM, D), _full((D, D)), _tile(TM, D), _tile(TM, D),
                  _full((3, D)), _full((2, D))],
        out_specs=[_tile(TM, D), _tile(TM, D), _tile(TM, D), _full((D, D)), _full((SUBLANE, D))],
        out_shape=[jax.ShapeDtypeStruct((T, D), F32), jax.ShapeDtypeStruct((T, D), BF16),
                   jax.ShapeDtypeStruct((T, D), F32), jax.ShapeDtypeStruct((D, D), BF16),
                   jax.ShapeDtypeStruct((SUBLANE, D), F32)],
        scratch_shapes=[pltpu.VMEM((D, D), F32)],
        compiler_params=_params(("arbitrary",)),
    )(hf, hb, g1, w_out, x1, tgt, mod, lnp)


def _rglru_bwd(fwd, dh, wa, wx, lam, reverse, name, comm=None):
    h, a_all, s_all, r_all, ig_all, xc_all = fwd
    T = h.shape[0]
    n = T // TS
    adj_rev = not reverse
    hprev_spec, hnext_spec = _halo_specs(TS, D, n, T, adj_rev)
    h_halo_spec = hnext_spec if reverse else hprev_spec

    def body(dh_ref, h_ref, hh_ref, a_ref, s_ref, r_ref, ig_ref, xc_ref, wa_ref, wx_ref, lam_ref,
             dxc_ref, dwa_ref, dwx_ref, vec_ref, a_s, l_s, carry, a_edge):
        i = pl.program_id(0)
        j = (n - 1 - i) if adj_rev else i

        @pl.when(i == 0)
        def _():
            carry[...] = jnp.zeros_like(carry)
            a_edge[...] = jnp.zeros_like(a_edge)
            dwa_ref[...] = jnp.zeros_like(dwa_ref)
            dwx_ref[...] = jnp.zeros_like(dwx_ref)
            vec_ref[...] = jnp.zeros_like(vec_ref)

        lam = lam_ref[...]
        sp = jnp.maximum(-lam, 0.0) + jnp.log(1.0 + jnp.exp(-jnp.abs(lam)))
        a, s = a_ref[...], s_ref[...]
        inv_s = lax.rsqrt(jnp.maximum(s * s, 1e-30))
        r, ig = r_ref[...].astype(F32), ig_ref[...].astype(F32)
        xcb = xc_ref[...]
        xc = xcb.astype(F32)

        rows = lax.broadcasted_iota(jnp.int32, (TS, D), 0)
        hcur = h_ref[...]
        if reverse:
            a_sh = jnp.where(rows == 0, a_edge[...], pltpu.roll(a, 1, 0))
            halo = jnp.where(j < n - 1, hh_ref[0:1, :], 0.0)
            h_nb = jnp.where(rows == TS - 1, halo, pltpu.roll(hcur, TS - 1, 0))
        else:
            a_sh = jnp.where(rows == TS - 1, a_edge[...], pltpu.roll(a, TS - 1, 0))
            halo = jnp.where(j > 0, hh_ref[SUBLANE - 1:SUBLANE, :], 0.0)
            h_nb = jnp.where(rows == 0, halo, pltpu.roll(hcur, 1, 0))
        a_s[...] = a_sh
        _scan_tile(a_s, dh_ref, l_s, carry, TS, adj_rev)
        a_edge[...] = a[TS - 1:TS, :] if reverse else a[0:1, :]

        lm = l_s[...]
        da = lm * h_nb
        di = lm * s * xc
        dxc = lm * s * ig
        ds = lm * ig * xc
        dlog_a = a * (da - ds * a * inv_s)
        dr = (-RG_C) * sp * dlog_a
        dsp = _rowsum((-RG_C) * r * dlog_a)
        dpr = dr * r * (1.0 - r)
        dpi = di * ig * (1.0 - ig)
        vec_ref[0:1, :] += _rowsum(dpr)
        vec_ref[1:2, :] += _rowsum(dpi)
        vec_ref[2:3, :] += dsp * (-_sigmoid(-lam))
        parts = []
        for hd in range(RNN_HEADS):
            sl = slice(hd * RNN_HD, (hd + 1) * RNN_HD)
            xh = xcb[:, sl]
            dprh = dpr[:, sl].astype(BF16)
            dpih = dpi[:, sl].astype(BF16)
            parts.append(_dot_nt(dprh, wa_ref[hd]) + _dot_nt(dpih, wx_ref[hd]))
            dwa_ref[hd] += _dot_tn(xh, dprh)
            dwx_ref[hd] += _dot_tn(xh, dpih)
        dxc_ref[...] = dxc + jnp.concatenate(parts, axis=-1)

    wspec = _full((RNN_HEADS, RNN_HD, RNN_HD))
    cur = _rev_tile(TS, D, n, adj_rev)
    return _fused_call(
        body, comm, (dh, h, h, a_all, s_all, r_all, ig_all, xc_all, wa, wx, lam), name=name, grid=(n,),
        in_specs=[cur, cur, h_halo_spec, cur, cur, cur, cur, cur, wspec, wspec, _full((1, D))],
        out_specs=[cur, wspec, wspec, _full((SUBLANE, D))],
        out_shape=[jax.ShapeDtypeStruct((T, D), F32),
                   jax.ShapeDtypeStruct((RNN_HEADS, RNN_HD, RNN_HD), F32),
                   jax.ShapeDtypeStruct((RNN_HEADS, RNN_HD, RNN_HD), F32),
                   jax.ShapeDtypeStruct((SUBLANE, D), F32)],
        scratch_shapes=[pltpu.VMEM((TS, D), F32)] * 2 + [pltpu.VMEM((1, D), F32)] * 2)


def _od_in_bwd(dxcf, dxcb, xr, dg1, x1, dx1p, mod, w_in, cw, comm=None):
    T = x1.shape[0]
    n = T // TM
    slab = OD_IN // N_DEV
    prev_spec, next_spec = _halo_specs(TM, D, n, T, False)

    def body(fp_ref, fc_ref, fn_ref, bp_ref, bc_ref, bn_ref, xp_ref, xc_ref, xn_ref, dg_ref, x1_ref, dxp_ref,
             mod_ref, w_ref, cw_ref, dx_ref, dwb_ref, vec_ref, dw_ref):
        i = pl.program_id(0)

        @pl.when(i == 0)
        def _():
            dw_ref[...] = jnp.zeros_like(dw_ref)
            vec_ref[...] = jnp.zeros_like(vec_ref)

        dcur = fc_ref[...] + bc_ref[...]
        dprev = jnp.where(i > 0, fp_ref[...] + bp_ref[...], 0.0)
        dnext = jnp.where(i < n - 1, fn_ref[...] + bn_ref[...], 0.0)
        dext = jnp.concatenate([dprev, dcur, dnext], axis=0)
        xext = _ext_rows(xp_ref, xc_ref[...], xn_ref, i, n)
        cw_v = cw_ref[...]
        dxr = None
        for k in range(4):
            term = cw_v[k:k + 1, :] * _shift_rows(dext, 2 - k, TM)
            dxr = term if dxr is None else dxr + term
            vec_ref[k:k + 1, :] += _rowsum(dcur * _shift_rows(xext, k - 2, TM))
        vec_ref[4:5, :] += _rowsum(dcur)
        dp = jnp.concatenate([dxr.astype(BF16), dg_ref[...]], axis=-1)
        x1v = x1_ref[...]
        scale1 = 1.0 + mod_ref[1:2, :]
        h1 = (x1v * scale1 + mod_ref[0:1, :]).astype(BF16)
        dh1 = _dot_nt(dp, w_ref[...])
        dw_ref[...] += _dot_tn(h1, dp)
        dx_ref[...] = dxp_ref[...] + dh1 * scale1
        vec_ref[5:6, :] += _rowsum(dh1)
        vec_ref[6:7, :] += _rowsum(dh1 * x1v)

        @pl.when(i == n - 1)
        def _():
            for j in range(N_DEV):
                dwb_ref[j] = dw_ref[:, j * slab:(j + 1) * slab].astype(BF16)

    t = _tile(TM, D)
    return _fused_call(
        body, comm, (dxcf, dxcf, dxcf, dxcb, dxcb, dxcb, xr, xr, xr, dg1, x1, dx1p, mod, w_in, cw),
        name="od_in_bwd", grid=(n,),
        in_specs=[prev_spec, t, next_spec, prev_spec, t, next_spec, prev_spec, t, next_spec, t, t, t,
                  _full((3, D)), _full((D, OD_IN)), _full((4, D))],
        out_specs=[t, _full((N_DEV, D, slab)), _full((SUBLANE, D))],
        out_shape=[jax.ShapeDtypeStruct((T, D), F32), jax.ShapeDtypeStruct((N_DEV, D, slab), BF16),
                   jax.ShapeDtypeStruct((SUBLANE, D), F32)],
        scratch_shapes=[pltpu.VMEM((D, OD_IN), F32)])


def _ev_out_bwd(dx1, z0, out0, y0, ycat, g0, w_out, mod, lnp):
    T = dx1.shape[0]

    def body(dx_ref, z_ref, out_ref, y0_ref, yc_ref, g_ref, w_ref, mod_ref, ln_ref,
             dxp_ref, dyc_ref, dg_ref, dwb_ref, vec_ref, dw_ref):
        i = pl.program_id(0)

        @pl.when(i == 0)
        def _():
            dw_ref[...] = jnp.zeros_like(dw_ref)
            vec_ref[...] = jnp.zeros_like(vec_ref)

        lng = ln_ref[0:1, :]
        _, xhat, rstd = _ln_fwd(z_ref[...], lng, ln_ref[1:2, :])
        dy = dx_ref[...]
        dz = _ln_bwd(dy, xhat, rstd, lng)
        vec_ref[0:1, :] += _rowsum(dy * xhat)
        vec_ref[1:2, :] += _rowsum(dy)
        vec_ref[2:3, :] += _rowsum(dz * out_ref[...].astype(F32))
        dout = (dz * mod_ref[2:3, :]).astype(BF16)
        dy0 = _dot_nt(dout, w_ref[...])
        dw_ref[...] += _dot_tn(y0_ref[...], dout)
        sg, dsg = _silu_and_grad(g_ref[...].astype(F32))
        dyc_ref[...] = (dy0 * sg).astype(BF16)
        dg_ref[...] = (dy0 * yc_ref[...].astype(F32) * dsg).astype(BF16)
        dxp_ref[...] = ALPHA * dz

        @pl.when(i == T // TM - 1)
        def _():
            dwb_ref[...] = dw_ref[...].astype(BF16)

    t = _tile(TM, D)
    return _pallas(
        body, name="ev_out_bwd", grid=(T // TM,),
        in_specs=[t, t, t, t, t, t, _full((D, D)), _full((3, D)), _full((2, D))],
        out_specs=[t, t, t, _full((D, D)), _full((SUBLANE, D))],
        out_shape=[jax.ShapeDtypeStruct((T, D), F32), jax.ShapeDtypeStruct((T, D), BF16),
                   jax.ShapeDtypeStruct((T, D), BF16), jax.ShapeDtypeStruct((D, D), BF16),
                   jax.ShapeDtypeStruct((SUBLANE, D), F32)],
        scratch_shapes=[pltpu.VMEM((D, D), F32)],
        compiler_params=_params(("arbitrary",)),
    )(dx1, z0, out0, y0, ycat, g0, w_out, mod, lnp)


def _mix0_bwd(q, kvx, lse, dyc, ycat, su, sv, sink_l, bias, a128, gsum, sel, sg_lng, sg_lnb, sg_w, sg_bfull,
              rc, rs1, rs2, comm=None):
    T = q.shape[0]
    nb = T // BLK

    def body(q_ref, kp_ref, kc_ref, kn_ref, lse_ref, dyc_ref, yc_ref, su_ref, sv_ref, sink_ref, bias_ref, a_ref,
             gsum_ref, sel_ref, lng_ref, lnb_ref, w_ref, bfull_ref, c_ref, s1_ref, s2_ref,
             dq_ref, dkv_ref, dsu_ref, dsv_ref, dw_ref, dbt_ref, vec_ref, dsink_ref):
        n = pl.program_id(0)

        @pl.when(n == 0)
        def _():
            dkv_ref[...] = jnp.zeros_like(dkv_ref)
            dw_ref[...] = jnp.zeros_like(dw_ref)
            dbt_ref[...] = jnp.zeros_like(dbt_ref)
            vec_ref[...] = jnp.zeros_like(vec_ref)
            dsink_ref[...] = jnp.zeros_like(dsink_ref)

        band = pl.ds(pl.multiple_of(n * BLK + (TM - BLK), BLK), 3 * BLK)
        bias = _band_bias(bias_ref, n, nb)
        kvx = jnp.concatenate([kp_ref[...], kc_ref[...], kn_ref[...]], axis=0)
        bias2 = jnp.concatenate([bias, bias], axis=1)
        low = lax.broadcasted_iota(jnp.int32, (BLK, LANE), 1) < HEAD_DIM
        low2 = lax.broadcasted_iota(jnp.int32, (2 * BLK, LANE), 1) < HEAD_DIM
        sel = sel_ref[...]
        c, s1, s2 = c_ref[...], s1_ref[...], s2_ref[...]
        for kvh in range(2):
            t0, t1 = 2 * kvh, 2 * kvh + 1
            q2 = jnp.concatenate([_lane_tile(q_ref, t0), _lane_tile(q_ref, t1)], axis=0)
            do2 = jnp.concatenate([_lane_tile(dyc_ref, t0), _lane_tile(dyc_ref, t1)], axis=0)
            yc2 = jnp.concatenate([_lane_tile(yc_ref, t0), _lane_tile(yc_ref, t1)], axis=0)
            p_hi, p_lo = _split_bf16(do2.astype(F32) * yc2.astype(F32))
            deltas = _dot_nt(sel, p_hi) + _dot_nt(sel, p_lo)
            dkx = jnp.zeros((3 * BLK, LANE), F32)
            dvx = jnp.zeros((3 * BLK, LANE), F32)
            dq_acc = None
            for par in range(2):
                heads = (4 * kvh + par, 4 * kvh + 2 + par)
                kt = 2 * kvh + par
                ke = kvx[:, kt * LANE:(kt + 1) * LANE]
                ve = kvx[:, (4 + kt) * LANE:(5 + kt) * LANE]
                lse = jnp.concatenate([lse_ref[0, :, h * LANE:(h + 1) * LANE] for h in heads], axis=1)
                sk = jnp.concatenate([_lane_tile(sink_ref, h) for h in heads], axis=1)
                delta = deltas[par:par + 1, :]
                pt = jnp.exp(_dot_nt(ke, q2) + bias2 - lse)
                dst = (pt * (_dot_nt(ve, do2) - delta)).astype(BF16)
                sink_terms = jnp.exp(sk - lse) * delta
                for k, h in enumerate(heads):
                    dsink_ref[:, h * LANE:(h + 1) * LANE] += sink_terms[:, k * LANE:(k + 1) * LANE]
                part = _dot_tn(dst, ke)
                dq_acc = part if dq_acc is None else dq_acc + part
                mine = low2 if par == 0 else jnp.logical_not(low2)
                dkx = dkx + jnp.dot(dst, jnp.where(mine, q2, jnp.zeros_like(q2)), preferred_element_type=F32)
                dvx = dvx + jnp.dot(pt.astype(BF16), jnp.where(mine, do2, jnp.zeros_like(do2)),
                                    preferred_element_type=F32)
            for k, t in enumerate((t0, t1)):
                dq_t = dq_acc[k * BLK:(k + 1) * BLK] * (HEAD_DIM ** -0.5)
                dq_ref[:, t * LANE:(t + 1) * LANE] = _rope_bwd(dq_t, c, s1, s2).astype(BF16)
            dkv_ref[band, kvh * LANE:(kvh + 1) * LANE] += dkx
            dkv_ref[band, (2 + kvh) * LANE:(3 + kvh) * LANE] += dvx

        lng = lng_ref[...]
        xhat, rstd, vb, svm = _sg_core(sv_ref, lng, lnb_ref[...], a_ref, w_ref, bfull_ref)
        dy = dyc_ref[:, ATTN_W:].astype(F32)
        dsu_ref[...] = (dy * svm).astype(BF16)
        dsvm = dy * su_ref[...].astype(F32)
        d_hi, d_lo = _split_bf16(dsvm)
        gsum = gsum_ref[...]
        dbt_ref[...] += jnp.dot(d_hi, gsum, preferred_element_type=F32) + jnp.dot(d_lo, gsum,
                                                                                 preferred_element_type=F32)
        tiles = []
        for t in range(SG_W // LANE):
            tl = slice(t * LANE, (t + 1) * LANE)
            dt, v2 = d_hi[:, tl], vb[:, tl]
            dw_ref[2 * t] += _dot_nt(jnp.where(low, dt, jnp.zeros_like(dt)), v2)
            dw_ref[2 * t + 1] += _dot_nt(jnp.where(low, jnp.zeros_like(dt), dt), v2)
            tiles.append(jnp.where(low, _dot_tn(w_ref[2 * t], dt), _dot_tn(w_ref[2 * t + 1], dt)))
        dvgn = jnp.concatenate(tiles, axis=-1)
        vec_ref[0:1, :] += _rowsum(dvgn * xhat)
        vec_ref[1:2, :] += _rowsum(dvgn)
        dxh = dvgn * lng
        m1 = _group_mean(dxh, a_ref)
        m2 = _group_mean(dxh * xhat, a_ref)
        dsv_ref[...] = (rstd * (dxh - m1 - xhat * m2)).astype(BF16)

    return _fused_call(
        body, comm, (q, kvx, kvx, kvx, lse, dyc, ycat, su, sv, sink_l, bias, a128, gsum, sel, sg_lng, sg_lnb, sg_w,
                     sg_bfull, rc, rs1, rs2),
        name="mix0_bwd", grid=(nb,),
        in_specs=[_tile(BLK, ATTN_W)] + _band_specs(KVX_W, nb) + [
            pl.BlockSpec((1, 1, N_HEADS * LANE), lambda n: (n, 0, 0)), _tile(BLK, D), _tile(BLK, D),
            _tile(BLK, SG_W), _tile(BLK, SG_W), _full((1, N_HEADS * LANE)), _full((3 * BLK, LANE)),
            _full((2 * LANE, 2 * LANE)),_full((SG_W, LANE)), _full((SUBLANE, LANE)), _full((1, SG_W)), _full((1, SG_W)),
            _full((SG_GROUPS, BLK, BLK)), _full((BLK, SG_W)), _tile(BLK, LANE), _tile(BLK, LANE), _tile(BLK, LANE)],
        out_specs=[_tile(BLK, ATTN_W), _full((T + 2 * TM, 4 * LANE)), _tile(BLK, SG_W), _tile(BLK, SG_W),
                   _full((SG_GROUPS, BLK, BLK)), _full((BLK, LANE)), _full((SUBLANE, SG_W)),
                   _full((1, N_HEADS * LANE))],
        out_shape=[jax.ShapeDtypeStruct((T, ATTN_W), BF16), jax.ShapeDtypeStruct((T + 2 * TM, 4 * LANE), F32),
                   jax.ShapeDtypeStruct((T, SG_W), BF16), jax.ShapeDtypeStruct((T, SG_W), BF16),
                   jax.ShapeDtypeStruct((SG_GROUPS, BLK, BLK), F32), jax.ShapeDtypeStruct((BLK, LANE), F32),
                   jax.ShapeDtypeStruct((SUBLANE, SG_W), F32), jax.ShapeDtypeStruct((1, N_HEADS * LANE), F32)])


def _ev_in_bwd(dq, dkv, dsu, dsv, dg0, x, dxp, mod, w_in, rc, rs1, rs2, comm=None):
    T = x.shape[0]

    def body(dq_ref, dkv_ref, dsu_ref, dsv_ref, dg_ref, x_ref, dxp_ref, mod_ref, w_ref, c_ref, s1_ref, s2_ref,
             dx_ref, dwb_ref, vec_ref, dw_ref):
        i = pl.program_id(0)

        @pl.when(i == 0)
        def _():
            dw_ref[...] = jnp.zeros_like(dw_ref)
            vec_ref[...] = jnp.zeros_like(vec_ref)

        low = lax.broadcasted_iota(jnp.int32, (TM, LANE), 1) < HEAD_DIM

        def fold(j):
            t0 = dkv_ref[:, (2 * j) * LANE:(2 * j + 1) * LANE]
            t1 = dkv_ref[:, (2 * j + 1) * LANE:(2 * j + 2) * LANE]
            return jnp.where(low, t0 + pltpu.roll(t0, HEAD_DIM, 1), t1 + pltpu.roll(t1, HEAD_DIM, 1))

        dk = _rope_bwd(fold(0), c_ref[...], s1_ref[...], s2_ref[...]).astype(BF16)
        dp = jnp.concatenate([dq_ref[...], dk, fold(1).astype(BF16), dsu_ref[...], dsv_ref[...],
                              dg_ref[...]], axis=-1)
        xv = x_ref[...]
        scale0 = 1.0 + mod_ref[1:2, :]
        h0 = (xv * scale0 + mod_ref[0:1, :]).astype(BF16)
        dh0 = _dot(dp, w_ref[...])
        dw_ref[...] += _dot_tn(dp, h0)
        dx_ref[...] = dxp_ref[...] + dh0 * scale0
        vec_ref[0:1, :] += _rowsum(dh0)
        vec_ref[1:2, :] += _rowsum(dh0 * xv)

        @pl.when(i == T // TM - 1)
        def _():
            dwb_ref[...] = dw_ref[...].astype(BF16)

    t = _tile(TM, D)
    return _fused_call(
        body, comm, (dq, dkv, dsu, dsv, dg0, x, dxp, mod, w_in, rc, rs1, rs2), name="ev_in_bwd", grid=(T // TM,),
        in_specs=[_tile(TM, ATTN_W), pl.BlockSpec((TM, 4 * LANE), lambda i: (i + 1, 0)), _tile(TM, SG_W),
                  _tile(TM, SG_W), t, t, t,
                  _full((3, D)), _full((EV_IN, D)), _tile(TM, LANE), _tile(TM, LANE), _tile(TM, LANE)],
        out_specs=[t, _full((EV_IN, D)), _full((SUBLANE, D))],
        out_shape=[jax.ShapeDtypeStruct((T, D), F32), jax.ShapeDtypeStruct((EV_IN, D), BF16),
                   jax.ShapeDtypeStruct((SUBLANE, D), F32)],
        scratch_shapes=[pltpu.VMEM((EV_IN, D), F32)])


def _sum_slots(land_ref):
    g = land_ref[0].astype(F32)
    for i in range(1, land_ref.shape[0]):
        g = g + land_ref[i].astype(F32)
    return g


def _reduce_adam(land, w, m, v, name):
    R, C = w.shape
    rb = R
    if R > 512:
        for cand in (512, 256, 128, 64, 32, 16, 8):
            if R % cand == 0:
                rb = cand
                break

    def body(l_ref, w_ref, m_ref, v_ref, g_ref, d_ref, nm_ref, nv_ref):
        g = _sum_slots(l_ref)
        g_ref[...] = g
        dlt, m2, v2 = _adam(w_ref[...], g, m_ref[...], v_ref[...])
        d_ref[...] = dlt
        nm_ref[...] = m2
        nv_ref[...] = v2

    t = pl.BlockSpec((rb, C), lambda i: (i, 0))
    shp = jax.ShapeDtypeStruct((R, C), F32)
    return _pallas(
        body, name=name, grid=(R // rb,),
        in_specs=[pl.BlockSpec((land.shape[0], rb, C), lambda i: (0, i, 0)), t, t, t],
        out_specs=[t] * 4, out_shape=[shp] * 4,
        compiler_params=_params(("parallel",)),
    )(land, w, m, v)


def _tail_exchange(slabs, small):
    _, R, C = slabs.shape
    n_chips = N_DEV // 2
    gather = _GatherComm(small)
    ns = gather.n

    def body(*refs):
        slab_ref = refs[0]
        g_ins = refs[1:1 + ns]
        land_ref = refs[1 + ns]
        g_outs = refs[2 + ns:2 + 2 * ns]
        stage, part, s1_send, s1_recv, s2_send, s2_recv = refs[2 + 2 * ns:8 + 2 * ns]
        g_sems = refs[8 + 2 * ns:]
        x, y, c = _my_pos()
        chip = 2 * x + y
        gather.start(g_ins, g_outs, g_sems)

        swaps = [pltpu.make_async_remote_copy(
            src_ref=slab_ref.at[2 * k + (1 - c)], dst_ref=stage.at[k], send_sem=s1_send.at[k],
            recv_sem=s1_recv.at[k], device_id=(x, y, 1 - c), device_id_type=MESH) for k in range(n_chips)]
        for cp in swaps:
            cp.start()
        for cp in swaps:
            cp.wait()
        for k in range(n_chips):
            part[k] = (slab_ref[2 * k + c].astype(F32) + stage[k].astype(F32)).astype(BF16)

        gather.mid(g_ins, g_outs, g_sems)

        sends = []
        for r in range(1, n_chips):
            px = (1 - x) if (r & 2) else x
            py = (1 - y) if (r & 1) else y
            sends.append(pltpu.make_async_remote_copy(
                src_ref=part.at[2 * px + py], dst_ref=land_ref.at[chip], send_sem=s2_send.at[r - 1],
                recv_sem=s2_recv.at[r - 1], device_id=(px, py, c), device_id_type=MESH))
        for cp in sends:
            cp.start()
        land_ref[chip] = part[chip]
        for cp in sends:
            cp.wait()
        gather.finish(g_ins, g_outs, g_sems)

    any_spec = pl.BlockSpec(memory_space=pl.ANY)
    vmem_spec = pl.BlockSpec(memory_space=pltpu.VMEM)
    res = _pallas(
        body, name="tail_exchange",
        out_shape=[jax.ShapeDtypeStruct((n_chips, R, C), BF16)] + gather.out_shapes(),
        in_specs=[vmem_spec] + [any_spec] * ns, out_specs=[vmem_spec] + [any_spec] * ns,
        scratch_shapes=[pltpu.VMEM((n_chips, R, C), BF16), pltpu.VMEM((n_chips, R, C), BF16),
                        pltpu.SemaphoreType.DMA((n_chips,)), pltpu.SemaphoreType.DMA((n_chips,)),
                        pltpu.SemaphoreType.DMA((n_chips - 1,)), pltpu.SemaphoreType.DMA((n_chips - 1,))]
        + gather.sems(),
        compiler_params=pltpu.CompilerParams(vmem_limit_bytes=VMEM_LIMIT),
    )(slabs, *gather.arrs)
    return res[0], list(res[1:])


def _slots_adam(land, w, m, v, name):
    lead = w.shape[1] if w.ndim == 5 else 1
    inner = w.shape[-3:]
    zeros3 = (0, 0, 0)
    if w.ndim == 5:
        lspec = pl.BlockSpec((N_DEV, 1) + inner, lambda i: (0, i) + zeros3)
        wspec = pl.BlockSpec((1, 1) + inner, lambda i: (0, i) + zeros3)
    else:
        lspec = pl.BlockSpec((N_DEV,) + inner, lambda i: (0,) + zeros3)
        wspec = pl.BlockSpec((1,) + inner, lambda i: (0,) + zeros3)

    def body(l_ref, w_ref, m_ref, v_ref, g_ref, d_ref, nm_ref, nv_ref):
        at = (0, 0) if w.ndim == 5 else (0,)
        g = l_ref[(0,) + at[1:]].astype(F32)
        for i in range(1, N_DEV):
            g = g + l_ref[(i,) + at[1:]].astype(F32)
        dlt, m2, v2 = _adam(w_ref[at], g, m_ref[at], v_ref[at])
        g_ref[at] = g
        d_ref[at] = dlt
        nm_ref[at] = m2
        nv_ref[at] = v2

    shp = jax.ShapeDtypeStruct(w.shape, F32)
    return _pallas(
        body, name=name, grid=(lead,),
        in_specs=[lspec, wspec, wspec, wspec], out_specs=[wspec] * 4, out_shape=[shp] * 4,
        compiler_params=_params(("parallel",)),
    )(land, w, m, v)


SMALL_PARAMS = ("ln_g", "ln_b", "ev_sg_ln_g", "ev_sg_ln_b", "ev_sink", "ev_sg_b",
                "od_conv_w", "od_conv_b", "od_b_a", "od_b_x", "od_lam")


def _small_update(ga, gc, gd, gf, gb, ge, gsink, gbt, params):
    names = list(SMALL_PARAMS)
    flat = [a for nm in names for a in params[nm]]
    n_g = 8

    def body(*refs):
        ga_ref, gc_ref, gd_ref, gf_ref, gb_ref, ge_ref, gs_ref, gbt_ref = refs[:n_g]
        prm = refs[n_g:n_g + 3 * len(names)]
        loss_ref = refs[n_g + 3 * len(names)]
        outs = refs[n_g + 3 * len(names) + 1:]

        def ssum(ref):
            acc = ref[0]
            for i in range(1, N_DEV):
                acc = acc + ref[i]
            return acc

        a, cc, dd, ff, bb, ee = ssum(ga_ref), ssum(gc_ref), ssum(gd_ref), ssum(gf_ref), ssum(gb_ref), ssum(ge_ref)
        loss_ref[...] = a[3:4, 0:LANE]
        me = _slot(*_my_pos())

        def mine(rows):
            acc = jnp.zeros((rows.shape[0], LANE), F32)
            for j in range(N_DEV):
                acc = acc + jnp.where(me == j, rows[:, j * LANE:(j + 1) * LANE], 0.0)
            return acc

        sink_terms = ssum(gs_ref)
        lane8 = lax.broadcasted_iota(jnp.int32, (1, N_HEADS), 1)
        g_sink = jnp.zeros((1, N_HEADS), F32)
        for h in range(N_HEADS):
            tot = -jnp.sum(sink_terms[:, h * LANE:(h + 1) * LANE], axis=1, keepdims=True)
            g_sink = jnp.where(lane8 == h, tot, g_sink)
        grads = dict(
            ln_g=jnp.concatenate([dd[0:1], a[0:1]], axis=0), ln_b=jnp.concatenate([dd[1:2], a[1:2]], axis=0),
            ev_sg_ln_g=ee[0:1], ev_sg_ln_b=ee[1:2], ev_sink=g_sink,
            ev_sg_b=jnp.transpose(ssum(gbt_ref))[0:SG_GROUPS, :],
            od_conv_w=mine(cc[0:4]), od_conv_b=mine(cc[4:5]),
            od_b_a=mine(jnp.concatenate([ff[0:1], bb[0:1]], axis=0)),
            od_b_x=mine(jnp.concatenate([ff[1:2], bb[1:2]], axis=0)),
            od_lam=mine(jnp.concatenate([ff[2:3], bb[2:3]], axis=0)))
        for k, nm in enumerate(names):
            w_ref, m_ref, v_ref = prm[3 * k:3 * k + 3]
            at = (0,) if len(w_ref.shape) == 3 else ()
            g = grads[nm]
            dlt, m2, v2 = _adam(w_ref[at] if at else w_ref[...], g, m_ref[at] if at else m_ref[...],
                                v_ref[at] if at else v_ref[...])
            for o_ref, val in zip(outs[4 * k:4 * k + 4], (g, dlt, m2, v2)):
                if at:
                    o_ref[at] = val
                else:
                    o_ref[...] = val

    gathered = [ga, gc, gd, gf, gb, ge, gsink, gbt]
    out_shape = [jax.ShapeDtypeStruct((1, LANE), F32)]
    for nm in names:
        out_shape += [jax.ShapeDtypeStruct(params[nm][0].shape, F32)] * 4
    return _pallas(
        body, name="small_update", grid=(1,),
        in_specs=[_full(a.shape) for a in gathered + flat],
        out_specs=[_full(s.shape) for s in out_shape], out_shape=out_shape,
        compiler_params=_params(("arbitrary",)),
    )(*gathered, *flat)


VEC_ROWS = 16
VEC_LAYOUT = (("od_conv_w", 4), ("od_conv_b", 1), ("od_b_a", 2), ("od_b_x", 2), ("od_lam", 2))


def _pack_vec(parts):
    rows = [parts[name].reshape(nrows, -1) for name, nrows in VEC_LAYOUT]
    used = sum(r for _, r in VEC_LAYOUT)
    rows.append(jnp.zeros((VEC_ROWS - used, rows[0].shape[1]), F32))
    return jnp.concatenate(rows, axis=0)


def _to_slabs(full, cols_per):
    R = full.shape[0]
    return full.reshape(R, N_DEV, cols_per).transpose(1, 0, 2)


def _from_slabs(slabs):
    n, R, cp = slabs.shape
    return slabs.transpose(1, 0, 2).reshape(R, n * cp)


def kernel(x, c, positions, ada_w, ada_b, ln_g, ln_b, ev_w_in, ev_w_out, ev_sink, ev_sg_ln_g, ev_sg_ln_b, ev_sg_w, ev_sg_b, od_w_in, od_conv_w, od_conv_b, od_w_a, od_b_a, od_w_x, od_b_x, od_lam, od_w_out, loss_target, m_ada_w, m_ada_b, m_ln_g, m_ln_b, m_ev_w_in, m_ev_w_out, m_ev_sink, m_ev_sg_ln_g, m_ev_sg_ln_b, m_ev_sg_w, m_ev_sg_b, m_od_w_in, m_od_conv_w, m_od_conv_b, m_od_w_a, m_od_b_a, m_od_w_x, m_od_b_x, m_od_lam, m_od_w_out, v_ada_w, v_ada_b, v_ln_g, v_ln_b, v_ev_w_in, v_ev_w_out, v_ev_sink, v_ev_sg_ln_g, v_ev_sg_ln_b, v_ev_sg_w, v_ev_sg_b, v_od_w_in, v_od_conv_w, v_od_conv_b, v_od_w_a, v_od_b_a, v_od_w_x, v_od_b_x, v_od_lam, v_od_w_out):
    T = x.shape[1]
    me = _slot(*_my_pos())
    xs = x.reshape(T, D)
    tgt = loss_target.reshape(T, D)

    vec_w = _pack_vec(dict(od_conv_w=od_conv_w[0], od_conv_b=od_conv_b, od_b_a=od_b_a[0], od_b_x=od_b_x[0],
                           od_lam=od_lam[0]))
    c_all, mod_all, (g_ev_in, g_vec) = _head_gather(c, ada_w, [ev_w_in[0].T.astype(BF16), vec_w])
    c_all = c_all.reshape(N_DEV, D)
    w_ev_in = g_ev_in.reshape(EV_IN, D)
    vec_full = _from_slabs(g_vec)
    cw, cb = vec_full[0:4], vec_full[4:5]
    ba, bx, lam = vec_full[5:7], vec_full[7:9], vec_full[9:11]
    mod_mine = lax.dynamic_index_in_dim(mod_all, me, axis=2, keepdims=False)
    mod = mod_mine.transpose(1, 0, 2).reshape(2, 3 * D) + ada_b
    mod0 = mod[0].reshape(3, D)
    mod1 = mod[1].reshape(3, D)

    half = 8
    inv_freq = jnp.power(jnp.float32(ROPE_THETA), -jnp.arange(half, dtype=F32) / half)
    ang = positions.reshape(T).astype(F32)[:, None] * inv_freq
    cos_t = jnp.tile(jnp.cos(ang), (1, LANE // half))
    sin_t = jnp.tile(jnp.sin(ang), (1, LANE // half))
    l64 = jnp.arange(LANE) % HEAD_DIM
    rc = jnp.where(l64 < 2 * half, cos_t, 1.0)
    rs1 = jnp.where(l64 < half, -sin_t, 0.0)
    rs2 = jnp.where((l64 >= half) & (l64 < 2 * half), sin_t, 0.0)

    ln0 = jnp.stack([ln_g[0], ln_b[0]])
    ln1 = jnp.stack([ln_g[1], ln_b[1]])
    sg_lng = ev_sg_ln_g
    sg_lnb = ev_sg_ln_b
    sg_w = ev_sg_w[0].astype(BF16)
    sg_bfull = jnp.repeat(ev_sg_b[0].T, SG_DIM, axis=1)
    sink_l = jnp.repeat(ev_sink, LANE, axis=1)
    kj = jnp.arange(3 * BLK)[:, None]
    qi = jnp.arange(BLK)[None, :]
    band_bias = jnp.where(jnp.abs(kj - BLK - qi) <= BLK, 0.0, NEG_INF).astype(F32)
    lanes = jnp.arange(LANE)
    lanes2 = jnp.arange(2 * LANE)
    a128 = jnp.where(lanes2[:, None] // SG_DIM == lanes2[None, :] // SG_DIM, 1.0 / SG_DIM, 0.0).astype(BF16)
    gsum = (jnp.arange(SG_W)[:, None] // SG_DIM == lanes[None, :]).astype(BF16)
    sel = (jnp.arange(SUBLANE)[:, None] == lanes[None, :] // HEAD_DIM).astype(BF16)
    wa = od_w_a[0].astype(BF16)
    wx = od_w_x[0].astype(BF16)

    (q, kvx, su, sv, g0), _ = _ev_in(xs, mod0, w_ev_in, rc, rs1, rs2)
    (ycat, y0, lse), (g_ev_out, g_od_in, g_od_out) = _mix0_fwd(
        q, kvx, su, sv, g0, sink_l, band_bias, a128, sg_lng, sg_lnb, sg_w, sg_bfull,
        _GatherComm([ev_w_out[0].astype(BF16), od_w_in[0].astype(BF16), od_w_out[0].astype(BF16)]))
    w_ev_out = g_ev_out.reshape(D, D)
    w_od_in = _from_slabs(g_od_in)
    w_od_out = g_od_out.reshape(D, D)
    out0, z0, x1 = _ev_out(y0, w_ev_out, xs, mod0, ln0)
    xr, g1 = _od_in(x1, mod1, w_od_in)
    fwd_f = _rglru_fwd(xr, cw, cb, wa[0], wx[0], ba[0:1], bx[0:1], lam[0:1], False, "rglru_fwd_f")
    fwd_b = _rglru_fwd(xr, cw, cb, wa[1], wx[1], ba[1:2], bx[1:2], lam[1:2], True, "rglru_fwd_b")
    dh, dg1, dx1p, d_od_out, vec_a = _od_out(fwd_f[0], fwd_b[0], g1, w_od_out, x1, tgt, mod1, ln1)

    (dxcf, dwa_f, dwx_f, vec_f), (l_od_out,) = _rglru_bwd(
        fwd_f, dh, wa[0], wx[0], lam[0:1], False, "rglru_bwd_f",
        _ExchangeComm([d_od_out.reshape(N_DEV, D // N_DEV, D)]))
    (dxcb, dwa_b, dwx_b, vec_b), _ = _rglru_bwd(fwd_b, dh, wa[1], wx[1], lam[1:2], True, "rglru_bwd_b")
    (dx1, d_od_in, vec_c), (a_wa, a_wx) = _od_in_bwd(
        dxcf, dxcb, xr, dg1, x1, dx1p, mod1, w_od_in, cw,
        _GatherComm([jnp.stack([dwa_f, dwa_b]).astype(BF16), jnp.stack([dwx_f, dwx_b]).astype(BF16)],
                    mid_frac=0.75))
    dxp, dyc, dg0, d_ev_out, vec_d = _ev_out_bwd(dx1, z0, out0, y0, ycat, g0, w_ev_out, mod0, ln0)
    (dq, dkv, dsu, dsv, d_sg_w, d_sg_bt, vec_e, d_sink_l), (l_od_in, l_ev_out) = _mix0_bwd(
        q, kvx, lse, dyc, ycat, su, sv, sink_l, band_bias, a128, gsum, sel, sg_lng, sg_lnb, sg_w, sg_bfull,
        rc, rs1, rs2, _ExchangeComm([d_od_in, d_ev_out.reshape(N_DEV, D // N_DEV, D)]))
    (grad_x, d_ev_in, vec_g), _ = _ev_in_bwd(dq, dkv, dsu, dsv, dg0, xs, dxp, mod0, w_ev_in, rc, rs1, rs2)

    l_ev_in, (ga, gc, gd, gf, gb, gg, ge, gsink, gbt, a_sgw) = _tail_exchange(
        d_ev_in.reshape(N_DEV, EV_IN // N_DEV, D),
        [vec_a, vec_c, vec_d, vec_f, vec_b, vec_g, vec_e, d_sink_l, d_sg_bt, d_sg_w.astype(BF16)])

    dmod_all = jnp.stack([jnp.concatenate([gg[:, 0], gg[:, 1], gd[:, 2]], axis=-1),
                          jnp.concatenate([gc[:, 5], gc[:, 6], ga[:, 2]], axis=-1)], axis=1)
    cols = ada_w.shape[2]
    dmod_cols = lax.dynamic_slice_in_dim(dmod_all, me * cols, cols, axis=2).transpose(1, 0, 2)
    (g_ada_w, d_ada_w, nm_ada_w, nv_ada_w, g_ada_b, d_ada_b, nm_ada_b, nv_ada_b) = _ada_update(
        c_all, dmod_cols, dmod_all, ada_w, m_ada_w, v_ada_w, ada_b, m_ada_b, v_ada_b)

    res = dict(ada_w=[g_ada_w, d_ada_w, nm_ada_w, nv_ada_w], ada_b=[g_ada_b, d_ada_b, nm_ada_b, nv_ada_b])
    res["ev_w_in"] = [a.T[None] for a in _reduce_adam(l_ev_in, ev_w_in[0].T, m_ev_w_in[0].T, v_ev_w_in[0].T,
                                                      "adam_ev_w_in")]
    for name, land, w, m, v in (("ev_w_out", l_ev_out, ev_w_out, m_ev_w_out, v_ev_w_out),
                                ("od_w_in", l_od_in, od_w_in, m_od_w_in, v_od_w_in),
                                ("od_w_out", l_od_out, od_w_out, m_od_w_out, v_od_w_out)):
        res[name] = [a[None] for a in _reduce_adam(land, w[0], m[0], v[0], "adam_" + name)]
    res["od_w_a"] = _slots_adam(a_wa, od_w_a, m_od_w_a, v_od_w_a, "adam_od_w_a")
    res["od_w_x"] = _slots_adam(a_wx, od_w_x, m_od_w_x, v_od_w_x, "adam_od_w_x")
    res["ev_sg_w"] = _slots_adam(a_sgw, ev_sg_w, m_ev_sg_w, v_ev_sg_w, "adam_ev_sg_w")
    small = dict(ln_g=(ln_g, m_ln_g, v_ln_g), ln_b=(ln_b, m_ln_b, v_ln_b),
                 ev_sg_ln_g=(ev_sg_ln_g, m_ev_sg_ln_g, v_ev_sg_ln_g),
                 ev_sg_ln_b=(ev_sg_ln_b, m_ev_sg_ln_b, v_ev_sg_ln_b),
                 ev_sink=(ev_sink, m_ev_sink, v_ev_sink), ev_sg_b=(ev_sg_b, m_ev_sg_b, v_ev_sg_b),
                 od_conv_w=(od_conv_w, m_od_conv_w, v_od_conv_w), od_conv_b=(od_conv_b, m_od_conv_b, v_od_conv_b),
                 od_b_a=(od_b_a, m_od_b_a, v_od_b_a), od_b_x=(od_b_x, m_od_b_x, v_od_b_x),
                 od_lam=(od_lam, m_od_lam, v_od_lam))
    small_out = _small_update(ga, gc, gd, gf, gb, ge, gsink, gbt, small)
    loss = small_out[0][0, 0]
    for k, name in enumerate(SMALL_PARAMS):
        res[name] = small_out[1 + 4 * k:5 + 4 * k]

    order = ["ada_w", "ada_b", "ln_g", "ln_b", "ev_w_in", "ev_w_out", "ev_sink", "ev_sg_ln_g", "ev_sg_ln_b",
             "ev_sg_w", "ev_sg_b", "od_w_in", "od_conv_w", "od_conv_b", "od_w_a", "od_b_a", "od_w_x", "od_b_x",
             "od_lam", "od_w_out"]
    outs = [loss, grad_x.reshape(1, T, D)]
    for kind in range(4):
        outs += [res[name][kind] for name in order]
    return tuple(outs)
```

```python
import functools

import jax
import jax.numpy as jnp
from jax import lax
from jax.experimental import pallas as pl
from jax.experimental.pallas import tpu as pltpu

F32 = jnp.float32
BF16 = jnp.bfloat16

N_DEV = 8
D = 1024
N_HEADS = 8
HEAD_DIM = 64
KV_WIDTH = 128
ATTN_W = 512
SG_W = 512
SG_GROUPS = 8
SG_DIM = 64
BLK = 128
KVX_W = 1024
EV_IN = 2816
OD_IN = 2048
RNN_HEADS = 8
RNN_HD = 128
ALPHA = 4.0 ** 0.25
LN_EPS = 1e-5
NEG_INF = -1e30
RG_C = 8.0
ROPE_THETA = 500000.0
LR, B1, B2, EPS, WD, STEP = 0.001, 0.9, 0.999, 1e-08, 0.01, 10

LANE = 128
SUBLANE = 8
TM = 256
TMF = 512
TS = 256
VMEM_LIMIT = 56 * 1024 * 1024

MESH = pl.DeviceIdType.MESH


def _pallas(body, **kw):
    return pl.pallas_call(body, **kw)


def _params(sem, vmem=VMEM_LIMIT):
    return pltpu.CompilerParams(dimension_semantics=sem, vmem_limit_bytes=vmem)


def _sigmoid(x):
    return 0.5 * jnp.tanh(0.5 * x) + 0.5


def _silu_and_grad(x):
    s = _sigmoid(x)
    return x * s, s * (1.0 + x * (1.0 - s))


def _dot(a, b):
    return jnp.dot(a.astype(BF16), b.astype(BF16), preferred_element_type=F32)


def _dot_nt(a, b):
    return lax.dot_general(a.astype(BF16), b.astype(BF16), (((1,), (1,)), ((), ())), preferred_element_type=F32)


def _dot_tn(a, b):
    return lax.dot_general(a.astype(BF16), b.astype(BF16), (((0,), (0,)), ((), ())), preferred_element_type=F32)


def _ln_fwd(z, g, b):
    mu = jnp.mean(z, axis=-1, keepdims=True)
    zc = z - mu
    var = jnp.mean(zc * zc, axis=-1, keepdims=True)
    rstd = lax.rsqrt(var + LN_EPS)
    xhat = zc * rstd
    return xhat * g + b, xhat, rstd


def _ln_bwd(dy, xhat, rstd, g):
    dxh = dy * g
    m1 = jnp.mean(dxh, axis=-1, keepdims=True)
    m2 = jnp.mean(dxh * xhat, axis=-1, keepdims=True)
    return rstd * (dxh - m1 - xhat * m2)


def _rowsum(v):
    return jnp.sum(v, axis=0, keepdims=True)


def _rope_fwd(t, c, s1, s2):
    return t * c + pltpu.roll(t, LANE - 8, 1) * s1 + pltpu.roll(t, 8, 1) * s2


def _rope_bwd(d, c, s1, s2):
    return d * c + pltpu.roll(d * s1, 8, 1) + pltpu.roll(d * s2, LANE - 8, 1)


def _adam(w, g, m, v):
    m2 = B1 * m + (1.0 - B1) * g
    v2 = B2 * v + (1.0 - B2) * (g * g)
    m_hat = m2 / (1.0 - B1 ** STEP)
    v_hat = v2 / (1.0 - B2 ** STEP)
    delta = -LR * (m_hat / (jnp.sqrt(v_hat) + EPS) + WD * w)
    return delta, m2, v2


def _tile(rows, width):
    return pl.BlockSpec((rows, width), lambda i: (i, 0))


def _full(shape):
    zeros = (0,) * len(shape)
    return pl.BlockSpec(shape, lambda i: zeros)


def _rev_tile(rows, width, n, reverse):
    if reverse:
        return pl.BlockSpec((rows, width), lambda i: (n - 1 - i, 0))
    return pl.BlockSpec((rows, width), lambda i: (i, 0))


def _halo_specs(rows, width, n, total_rows, reverse):
    per = rows // SUBLANE
    last = total_rows // SUBLANE - 1

    def tile_of(i):
        return (n - 1 - i) if reverse else i

    prev = pl.BlockSpec((SUBLANE, width), lambda i: (jnp.maximum(tile_of(i) * per - 1, 0), 0))
    nxt = pl.BlockSpec((SUBLANE, width), lambda i: (jnp.minimum((tile_of(i) + 1) * per, last), 0))
    return prev, nxt


def _my_pos():
    return lax.axis_index("x"), lax.axis_index("y"), lax.axis_index("c")


def _slot(px, py, pc):
    return 4 * px + 2 * py + pc


class _GatherComm:
    has_mid = True

    def __init__(self, arrs, mid_frac=0.5):
        self.arrs = list(arrs)
        self.n = len(self.arrs)
        self.mid_frac = mid_frac

    def out_shapes(self):
        return [jax.ShapeDtypeStruct((N_DEV,) + a.shape, a.dtype) for a in self.arrs]

    def sems(self):
        return [pltpu.SemaphoreType.DMA((7 * self.n,)), pltpu.SemaphoreType.DMA((7 * self.n,)),
                pltpu.SemaphoreType.DMA((self.n,))]

    def _parts(self, ins, outs, sems):
        send_sems, recv_sems, local_sems = sems
        x, y, c = _my_pos()
        me, sibling = (x, y, c), (x, y, 1 - c)
        chips = [(1 - x, y), (x, 1 - y), (1 - x, 1 - y)]

        def copy(a, k, block, to, src=None):
            dst = outs[a].at[_slot(*block)]
            return pltpu.make_async_remote_copy(
                src_ref=dst if src is None else src, dst_ref=dst,
                send_sem=send_sems.at[a * 7 + k], recv_sem=recv_sems.at[a * 7 + k],
                device_id=to, device_id_type=MESH)

        local = [pltpu.make_async_copy(ins[a], outs[a].at[_slot(*me)], local_sems.at[a]) for a in range(self.n)]
        first = []
        for a in range(self.n):
            first.append(copy(a, 0, me, sibling, src=ins[a]))
            first += [copy(a, 1 + j, me, (*chip, c), src=ins[a]) for j, chip in enumerate(chips)]
        ici_in = [copy(a, 1 + j, (*chip, c), me) for j, chip in enumerate(chips) for a in range(self.n)]
        passed = [copy(a, 4 + j, (*chip, c), sibling) for j, chip in enumerate(chips) for a in range(self.n)]
        d2d_in = []
        for a in range(self.n):
            d2d_in.append(copy(a, 0, sibling, me))
            d2d_in += [copy(a, 4 + j, (*chip, 1 - c), me) for j, chip in enumerate(chips)]
        return local, first, ici_in, passed, d2d_in

    def start(self, ins, outs, sems):
        local, first, _, _, _ = self._parts(ins, outs, sems)
        for cp in local + first:
            cp.start()

    def mid(self, ins, outs, sems):
        _, _, ici_in, passed, _ = self._parts(ins, outs, sems)
        for arrived, fw in zip(ici_in, passed):
            arrived.wait_recv()
            fw.start()

    def finish(self, ins, outs, sems):
        local, first, _, passed, d2d_in = self._parts(ins, outs, sems)
        for cp in d2d_in:
            cp.wait_recv()
        for cp in first + passed:
            cp.wait_send()
        for cp in local:
            cp.wait()


class _ExchangeComm:
    has_mid = False

    def __init__(self, arrs):
        self.arrs = list(arrs)
        self.n = len(self.arrs)

    def out_shapes(self):
        return [jax.ShapeDtypeStruct(a.shape, a.dtype) for a in self.arrs]

    def sems(self):
        return [pltpu.SemaphoreType.DMA((7 * self.n,)), pltpu.SemaphoreType.DMA((7 * self.n,)),
                pltpu.SemaphoreType.DMA((self.n,))]

    def _copies(self, ins, outs, sems):
        send_sems, recv_sems, local_sems = sems
        x, y, c = _my_pos()
        mine = _slot(x, y, c)
        copies = [pltpu.make_async_copy(ins[a].at[mine], outs[a].at[mine], local_sems.at[a]) for a in range(self.n)]
        for k in range(1, N_DEV):
            px = (1 - x) if (k & 4) else x
            py = (1 - y) if (k & 2) else y
            pc = (1 - c) if (k & 1) else c
            for a in range(self.n):
                copies.append(pltpu.make_async_remote_copy(
                    src_ref=ins[a].at[_slot(px, py, pc)], dst_ref=outs[a].at[mine],
                    send_sem=send_sems.at[a * 7 + k - 1], recv_sem=recv_sems.at[a * 7 + k - 1],
                    device_id=(px, py, pc), device_id_type=MESH))
        return copies

    def start(self, ins, outs, sems):
        for cp in self._copies(ins, outs, sems):
            cp.start()

    def finish(self, ins, outs, sems):
        for cp in self._copies(ins, outs, sems):
            cp.wait()


def _fused_call(body, comm, operands, *, name, grid, in_specs, out_specs, out_shape, scratch_shapes=(),
                semantics=("arbitrary",)):
    n_in, n_out, n_scr = len(in_specs), len(out_specs), len(scratch_shapes)
    if comm is None:
        res = _pallas(body, name=name, grid=grid, in_specs=list(in_specs), out_specs=list(out_specs),
                      out_shape=list(out_shape), scratch_shapes=list(scratch_shapes),
                      compiler_params=_params(semantics))(*operands)
        return list(res), []
    k = comm.n
    steps = grid[0]

    def wrapped(*refs):
        ins, cins = refs[:n_in], refs[n_in:n_in + k]
        outs = refs[n_in + k:n_in + k + n_out]
        couts = refs[n_in + k + n_out:n_in + 2 * k + n_out]
        rest = refs[n_in + 2 * k + n_out:]
        scratch, sems = rest[:n_scr], rest[n_scr:]
        i = pl.program_id(0)

        @pl.when(i == 0)
        def _():
            comm.start(cins, couts, sems)

        body(*ins, *outs, *scratch)

        if comm.has_mid:
            @pl.when(i == int(steps * comm.mid_frac))
            def _():
                comm.mid(cins, couts, sems)

        @pl.when(i == steps - 1)
        def _():
            comm.finish(cins, couts, sems)

    any_spec = pl.BlockSpec(memory_space=pl.ANY)
    res = _pallas(wrapped, name=name, grid=grid, in_specs=list(in_specs) + [any_spec] * k,
                  out_specs=list(out_specs) + [any_spec] * k, out_shape=list(out_shape) + comm.out_shapes(),
                  scratch_shapes=list(scratch_shapes) + comm.sems(),
                  compiler_params=_params(("arbitrary",)))(*operands, *comm.arrs)
    return list(res[:n_out]), list(res[n_out:])


def _head_gather(c, ada_w, big):
    cols = ada_w.shape[2]
    g_c, g_big = _GatherComm([c]), _GatherComm(big)
    g_mod = _GatherComm([jax.ShapeDtypeStruct((2, N_DEV, cols), F32)])
    nb = g_big.n

    def body(*refs):
        c_ref, w_ref = refs[0], refs[1]
        big_in = refs[2:2 + nb]
        c_all_ref, mod_all_ref = refs[2 + nb], refs[3 + nb]
        big_out = refs[4 + nb:4 + 2 * nb]
        part_ref = refs[4 + 2 * nb]
        sems = refs[5 + 2 * nb:]
        s_c, s_mod, s_big = sems[0:3], sems[3:6], sems[6:9]
        g_c.start([c_ref], [c_all_ref], s_c)
        g_big.start(big_in, big_out, s_big)
        g_c.mid([c_ref], [c_all_ref], s_c)
        g_c.finish([c_ref], [c_all_ref], s_c)
        cv = c_all_ref[:, 0, :]
        cond = cv * _sigmoid(cv)
        for l in range(2):
            part_ref[l] = _dot(cond, w_ref[l])
        g_mod.start([part_ref], [mod_all_ref], s_mod)
        g_mod.mid([part_ref], [mod_all_ref], s_mod)
        g_mod.finish([part_ref], [mod_all_ref], s_mod)
        g_big.mid(big_in, big_out, s_big)
        g_big.finish(big_in, big_out, s_big)

    any_spec = pl.BlockSpec(memory_space=pl.ANY)
    vmem_spec = pl.BlockSpec(memory_space=pltpu.VMEM)
    res = _pallas(
        body, name="head_gather",
        out_shape=g_c.out_shapes() + g_mod.out_shapes() + g_big.out_shapes(),
        in_specs=[vmem_spec, vmem_spec] + [any_spec] * nb,
        out_specs=[vmem_spec, vmem_spec] + [any_spec] * nb,
        scratch_shapes=[pltpu.VMEM((2, N_DEV, cols), F32)] + g_c.sems() + g_mod.sems() + g_big.sems(),
        compiler_params=pltpu.CompilerParams(vmem_limit_bytes=VMEM_LIMIT),
    )(c, ada_w, *big)
    return res[0], res[1], list(res[2:])


def _ada_update(c_all, dmod_cols, dmod_all, ada_w, m_w, v_w, ada_b, m_b, v_b):
    cols = ada_w.shape[2]
    nb = ada_b.shape[1]

    def body(c_ref, dmc_ref, dma_ref, w_ref, mw_ref, vw_ref, b_ref, mb_ref, vb_ref,
             gw_ref, dw_ref, nmw_ref, nvw_ref, gb_ref, db_ref, nmb_ref, nvb_ref):
        cv = c_ref[...]
        cond = cv * _sigmoid(cv)
        for l in range(2):
            g = _dot_tn(cond, dmc_ref[l])
            gw_ref[l] = g
            dlt, m2, v2 = _adam(w_ref[l], g, mw_ref[l], vw_ref[l])
            dw_ref[l] = dlt
            nmw_ref[l] = m2
            nvw_ref[l] = v2
        gb = dma_ref[0]
        for i in range(1, N_DEV):
            gb = gb + dma_ref[i]
        gb_ref[...] = gb
        dlt, m2, v2 = _adam(b_ref[...], gb, mb_ref[...], vb_ref[...])
        db_ref[...] = dlt
        nmb_ref[...] = m2
        nvb_ref[...] = v2

    wspec = _full((2, D, cols))
    bspec = _full((2, nb))
    wshape = jax.ShapeDtypeStruct((2, D, cols), F32)
    bshape = jax.ShapeDtypeStruct((2, nb), F32)
    return _pallas(
        body, name="ada_update", grid=(1,),
        in_specs=[_full((N_DEV, D)), _full((2, N_DEV, cols)), _full((N_DEV, 2, nb)),
                  wspec, wspec, wspec, bspec, bspec, bspec],
        out_specs=[wspec] * 4 + [bspec] * 4,
        out_shape=[wshape] * 4 + [bshape] * 4,
        compiler_params=_params(("arbitrary",)),
    )(c_all, dmod_cols, dmod_all, ada_w, m_w, v_w, ada_b, m_b, v_b)


def _ev_in(x, mod, w_in, rc, rs1, rs2, comm=None):
    T = x.shape[0]

    def body(x_ref, mod_ref, w_ref, c_ref, s1_ref, s2_ref, q_ref, kv_ref, su_ref, sv_ref, g_ref):
        h = x_ref[...] * (1.0 + mod_ref[1:2, :]) + mod_ref[0:1, :]
        p = _dot_nt(h, w_ref[...])
        c, s1, s2 = c_ref[...], s1_ref[...], s2_ref[...]
        for j in range(ATTN_W // LANE):
            qr = _rope_fwd(p[:, j * LANE:(j + 1) * LANE], c, s1, s2)
            q_ref[:, j * LANE:(j + 1) * LANE] = (qr * (HEAD_DIM ** -0.5)).astype(BF16)
        low = lax.broadcasted_iota(jnp.int32, (TMF, LANE), 1) < HEAD_DIM
        for j, val in enumerate((_rope_fwd(p[:, 512:640], c, s1, s2), p[:, 640:768])):
            swapped = pltpu.roll(val, HEAD_DIM, 1)
            tiles = (jnp.where(low, val, 0.0), jnp.where(low, 0.0, swapped),
                     jnp.where(low, swapped, 0.0), jnp.where(low, 0.0, val))
            for k, tile in enumerate(tiles):
                kv_ref[:, (4 * j + k) * LANE:(4 * j + k + 1) * LANE] = tile.astype(BF16)
        su_ref[...] = p[:, 768:1280].astype(BF16)
        sv_ref[...] = p[:, 1280:1792].astype(BF16)
        g_ref[...] = p[:, 1792:2816].astype(BF16)

    sh = lambda w: jax.ShapeDtypeStruct((T, w), BF16)
    return _fused_call(
        body, comm, (x, mod, w_in, rc, rs1, rs2), name="ev_in", grid=(T // TMF,),
        in_specs=[_tile(TMF, D), _full((3, D)), _full((EV_IN, D)), _tile(TMF, LANE), _tile(TMF, LANE),
                  _tile(TMF, LANE)],
        out_specs=[_tile(TMF, ATTN_W), _tile(TMF, KVX_W), _tile(TMF, SG_W), _tile(TMF, SG_W), _tile(TMF, D)],
        out_shape=[sh(ATTN_W), sh(KVX_W), sh(SG_W), sh(SG_W), sh(D)], semantics=("parallel",))


def _band_specs(width, nb):
    return [pl.BlockSpec((BLK, width), lambda n: (jnp.maximum(n - 1, 0), 0)),
            pl.BlockSpec((BLK, width), lambda n: (n, 0)),
            pl.BlockSpec((BLK, width), lambda n: (jnp.minimum(n + 1, nb - 1), 0))]


def _band_bias(bias_ref, n, nb):
    rows = lax.broadcasted_iota(jnp.int32, (3 * BLK, 1), 0)
    outside = ((rows < BLK) & (n == 0)) | ((rows >= 2 * BLK) & (n == nb - 1))
    return bias_ref[...] + jnp.where(outside, NEG_INF, 0.0)


def _lane_tile(ref, t):
    return ref[:, t * LANE:(t + 1) * LANE]


def _split_bf16(v):
    hi = v.astype(BF16)
    return hi, (v - hi.astype(F32)).astype(BF16)


def _group_mean(v, a_ref, exact_bf16=False):
    hi, lo = _split_bf16(v)
    a = a_ref[...]
    out = []
    for t in range(SG_W // (2 * LANE)):
        sl = slice(t * 2 * LANE, (t + 1) * 2 * LANE)
        r = jnp.dot(hi[:, sl], a, preferred_element_type=F32)
        if not exact_bf16:
            r = r + jnp.dot(lo[:, sl], a, preferred_element_type=F32)
        out.append(r)
    return jnp.concatenate(out, axis=-1)


def _sg_core(sv_ref, lng, lnb, a_ref, w_ref, bfull_ref):
    svf = sv_ref[...].astype(F32)
    xc = svf - _group_mean(svf, a_ref, exact_bf16=True)
    rstd = lax.rsqrt(_group_mean(xc * xc, a_ref) + LN_EPS)
    xhat = xc * rstd
    vb = (xhat * lng + lnb).astype(BF16)
    low = lax.broadcasted_iota(jnp.int32, (BLK, LANE), 1) < SG_DIM
    tiles = []
    for t in range(SG_W // LANE):
        v2 = vb[:, t * LANE:(t + 1) * LANE]
        r0 = jnp.dot(w_ref[2 * t], v2, preferred_element_type=F32)
        r1 = jnp.dot(w_ref[2 * t + 1], v2, preferred_element_type=F32)
        tiles.append(jnp.where(low, r0, r1))
    svm = jnp.concatenate(tiles, axis=-1) + bfull_ref[...]
    return xhat, rstd, vb, svm


def _mix0_fwd(q, kvx, su, sv, g0, sink_l, bias, a128, sg_lng, sg_lnb, sg_w, sg_bfull, comm=None):
    T = q.shape[0]
    nb = T // BLK

    def body(q_ref, kp_ref, kc_ref, kn_ref, su_ref, sv_ref, g_ref, sink_ref, bias_ref, a_ref, lng_ref, lnb_ref,
             w_ref, bfull_ref, ycat_ref, y0_ref, lse_ref):
        n = pl.program_id(0)
        bias = _band_bias(bias_ref, n, nb)
        kvx = jnp.concatenate([kp_ref[...], kc_ref[...], kn_ref[...]], axis=0)
        tiles = []
        for t in range(ATTN_W // LANE):
            qt = _lane_tile(q_ref, t)
            acc = None
            for par in range(2):
                h = 2 * t + par
                kt = 2 * (h // 4) + par
                ke = kvx[:, kt * LANE:(kt + 1) * LANE]
                ve = kvx[:, (4 + kt) * LANE:(5 + kt) * LANE]
                st = _dot_nt(ke, qt) + bias
                sk = _lane_tile(sink_ref, h)
                m = jnp.maximum(jnp.max(st, axis=0, keepdims=True), sk)
                p = jnp.exp(st - m)
                denom = jnp.sum(p, axis=0, keepdims=True) + jnp.exp(sk - m)
                contrib = _dot_tn(p * (1.0 / denom), ve)
                acc = contrib if acc is None else acc + contrib
                lse_ref[0, :, h * LANE:(h + 1) * LANE] = m + jnp.log(denom)
            tiles.append(acc)
        _, _, _, svm = _sg_core(sv_ref, lng_ref[...], lnb_ref[...], a_ref, w_ref, bfull_ref)
        tiles.append(su_ref[...].astype(F32) * svm)
        ycat = jnp.concatenate(tiles, axis=-1)
        gf = g_ref[...].astype(F32)
        ycat_ref[...] = ycat.astype(BF16)
        y0_ref[...] = (ycat * (gf * _sigmoid(gf))).astype(BF16)

    return _fused_call(
        body, comm, (q, kvx, kvx, kvx, su, sv, g0, sink_l, bias, a128, sg_lng, sg_lnb, sg_w, sg_bfull),
        name="mix0_fwd", grid=(nb,),
        in_specs=[_tile(BLK, ATTN_W)] + _band_specs(KVX_W, nb) + [
            _tile(BLK, SG_W), _tile(BLK, SG_W), _tile(BLK, D), _full((1, N_HEADS * LANE)), _full((3 * BLK, LANE)),
            _full((2 * LANE, 2 * LANE)),_full((1, SG_W)), _full((1, SG_W)), _full((SG_GROUPS, BLK, BLK)),
            _full((BLK, SG_W))],
        out_specs=[_tile(BLK, D), _tile(BLK, D), pl.BlockSpec((1, 1, N_HEADS * LANE), lambda n: (n, 0, 0))],
        out_shape=[jax.ShapeDtypeStruct((T, D), BF16), jax.ShapeDtypeStruct((T, D), BF16),
                   jax.ShapeDtypeStruct((nb, 1, N_HEADS * LANE), F32)], semantics=("parallel",))


def _ev_out(y0, w_out, x, mod, lnp):
    T = x.shape[0]

    def body(y_ref, w_ref, x_ref, mod_ref, ln_ref, out_ref, z_ref, x1_ref):
        out = _dot(y_ref[...], w_ref[...])
        z = ALPHA * x_ref[...] + mod_ref[2:3, :] * out
        x1, _, _ = _ln_fwd(z, ln_ref[0:1, :], ln_ref[1:2, :])
        out_ref[...] = out.astype(BF16)
        z_ref[...] = z
        x1_ref[...] = x1

    return _pallas(
        body, name="ev_out", grid=(T // TMF,),
        in_specs=[_tile(TMF, D), _full((D, D)), _tile(TMF, D), _full((3, D)), _full((2, D))],
        out_specs=[_tile(TMF, D)] * 3,
        out_shape=[jax.ShapeDtypeStruct((T, D), BF16), jax.ShapeDtypeStruct((T, D), F32),
                   jax.ShapeDtypeStruct((T, D), F32)],
        compiler_params=_params(("parallel",)),
    )(y0, w_out, x, mod, lnp)


def _od_in(x1, mod, w_in):
    T = x1.shape[0]

    def body(x_ref, mod_ref, w_ref, xr_ref, g_ref):
        h = x_ref[...] * (1.0 + mod_ref[1:2, :]) + mod_ref[0:1, :]
        p = _dot(h, w_ref[...])
        xr_ref[...] = p[:, :D]
        g_ref[...] = p[:, D:].astype(BF16)

    return _pallas(
        body, name="od_in", grid=(T // TMF,),
        in_specs=[_tile(TMF, D), _full((3, D)), _full((D, OD_IN))],
        out_specs=[_tile(TMF, D), _tile(TMF, D)],
        out_shape=[jax.ShapeDtypeStruct((T, D), F32), jax.ShapeDtypeStruct((T, D), BF16)],
        compiler_params=_params(("parallel",)),
    )(x1, mod, w_in)


def _ext_rows(prev_ref, cur, next_ref, j, n):
    prev = jnp.where(j > 0, prev_ref[...], 0.0)
    nxt = jnp.where(j < n - 1, next_ref[...], 0.0)
    return jnp.concatenate([prev, cur, nxt], axis=0)


def _shift_rows(ext, off, rows):
    total = ext.shape[0]
    if off == 0:
        return ext[SUBLANE:SUBLANE + rows, :]
    return pltpu.roll(ext, (-off) % total, 0)[SUBLANE:SUBLANE + rows, :]


def _conv_fwd(ext, cw, cb, rows):
    xc = cb
    for k in range(4):
        xc = xc + cw[k:k + 1, :] * _shift_rows(ext, k - 2, rows)
    return xc


def _gates(xc, wa_ref, wx_ref, ba, bx, lam):
    pr, pi = [], []
    for h in range(RNN_HEADS):
        xh = xc[:, h * RNN_HD:(h + 1) * RNN_HD].astype(BF16)
        pr.append(_dot(xh, wa_ref[h]))
        pi.append(_dot(xh, wx_ref[h]))
    r = _sigmoid(jnp.concatenate(pr, axis=-1) + ba)
    ig = _sigmoid(jnp.concatenate(pi, axis=-1) + bx)
    sp = jnp.maximum(-lam, 0.0) + jnp.log(1.0 + jnp.exp(-jnp.abs(lam)))
    neg_log_a = RG_C * r * sp
    a = jnp.exp(-neg_log_a)
    s2 = (1.0 + a * a) * jnp.tanh(neg_log_a)
    inv_s = lax.rsqrt(jnp.maximum(s2, 1e-30))
    return r, ig, sp, a, s2 * inv_s, inv_s


def _scan_tile(a_ref, b_ref, o_ref, carry_ref, rows, reverse):
    ridx = lax.broadcasted_iota(jnp.int32, (SUBLANE, D), 0)
    groups = rows // SUBLANE

    def group(gi, h):
        g = (groups - 1 - gi) if reverse else gi
        off = pl.multiple_of(g * SUBLANE, SUBLANE)
        a = a_ref[pl.ds(off, SUBLANE), :]
        b = b_ref[pl.ds(off, SUBLANE), :]
        for sh in (1, 2, 4):
            if reverse:
                keep = ridx < SUBLANE - sh
                a_p = jnp.where(keep, pltpu.roll(a, SUBLANE - sh, 0), 1.0)
                b_p = jnp.where(keep, pltpu.roll(b, SUBLANE - sh, 0), 0.0)
            else:
                keep = ridx >= sh
                a_p = jnp.where(keep, pltpu.roll(a, sh, 0), 1.0)
                b_p = jnp.where(keep, pltpu.roll(b, sh, 0), 0.0)
            b = b + a * b_p
            a = a * a_p
        hh = b + a * h
        o_ref[pl.ds(off, SUBLANE), :] = hh
        return hh[0:1, :] if reverse else hh[SUBLANE - 1:SUBLANE, :]

    carry_ref[...] = lax.fori_loop(0, groups, group, carry_ref[...])


def _rglru_fwd(xr, cw, cb, wa, wx, ba, bx, lam, reverse, name):
    T = xr.shape[0]
    n = T // TS
    prev_spec, next_spec = _halo_specs(TS, D, n, T, reverse)

    def body(prev_ref, cur_ref, next_ref, cw_ref, cb_ref, wa_ref, wx_ref, ba_ref, bx_ref, lam_ref,
             h_ref, a_ref, s_ref, r_ref, ig_ref, xc_ref, b_s, carry):
        i = pl.program_id(0)
        j = (n - 1 - i) if reverse else i

        @pl.when(i == 0)
        def _():
            carry[...] = jnp.zeros_like(carry)

        ext = _ext_rows(prev_ref, cur_ref[...], next_ref, j, n)
        xc = _conv_fwd(ext, cw_ref[...], cb_ref[...], TS)
        r, ig, _, a, s, _ = _gates(xc, wa_ref, wx_ref, ba_ref[...], bx_ref[...], lam_ref[...])
        s_ref[...] = s
        r_ref[...] = r.astype(BF16)
        ig_ref[...] = ig.astype(BF16)
        xc_ref[...] = xc.astype(BF16)
        a_ref[...] = a
        b_s[...] = s * ig * xc
        _scan_tile(a_ref, b_s, h_ref, carry, TS, reverse)

    wspec = _full((RNN_HEADS, RNN_HD, RNN_HD))
    cur = _rev_tile(TS, D, n, reverse)
    f32 = jax.ShapeDtypeStruct((T, D), F32)
    b16 = jax.ShapeDtypeStruct((T, D), BF16)
    return _pallas(
        body, name=name, grid=(n,),
        in_specs=[prev_spec, cur, next_spec, _full((4, D)), _full((1, D)),
                  wspec, wspec, _full((1, D)), _full((1, D)), _full((1, D))],
        out_specs=[cur] * 6,
        out_shape=[f32, f32, f32, b16, b16, b16],
        scratch_shapes=[pltpu.VMEM((TS, D), F32), pltpu.VMEM((1, D), F32)],
        compiler_params=_params(("arbitrary",)),
    )(xr, xr, xr, cw, cb, wa, wx, ba, bx, lam)


def _od_out(hf, hb, g1, w_out, x1, tgt, mod, lnp):
    T = x1.shape[0]

    def body(hf_ref, hb_ref, g_ref, w_ref, x_ref, t_ref, mod_ref, ln_ref,
             dh_ref, dg_ref, dx_ref, dwb_ref, vec_ref, dw_ref):
        i = pl.program_id(0)

        @pl.when(i == 0)
        def _():
            dw_ref[...] = jnp.zeros_like(dw_ref)
            vec_ref[...] = jnp.zeros_like(vec_ref)

        hs = hf_ref[...] + hb_ref[...]
        sg, dsg = _silu_and_grad(g_ref[...].astype(F32))
        yr = (hs * sg).astype(BF16)
        w = w_ref[...]
        out = _dot(yr, w)
        gate = mod_ref[2:3, :]
        z = ALPHA * x_ref[...] + gate * out
        lng = ln_ref[0:1, :]
        x2, xhat, rstd = _ln_fwd(z, lng, ln_ref[1:2, :])
        diff = x2 - t_ref[...]
        vec_ref[3:4, 0:LANE] += 0.5 * jnp.sum(diff * diff) * (1.0 / D)
        dx2 = diff * (1.0 / D)
        dz = _ln_bwd(dx2, xhat, rstd, lng)
        vec_ref[0:1, :] += _rowsum(dx2 * xhat)
        vec_ref[1:2, :] += _rowsum(dx2)
        vec_ref[2:3, :] += _rowsum(dz * out)
        dout = (dz * gate).astype(BF16)
        dyr = _dot_nt(dout, w)
        dw_ref[...] += _dot_tn(yr, dout)
        dh_ref[...] = dyr * sg
        dg_ref[...] = (dyr * hs * dsg).astype(BF16)
        dx_ref[...] = ALPHA * dz

        @pl.when(i == T // TM - 1)
        def _():
            dwb_ref[...] = dw_ref[...].astype(BF16)

    return _pallas(
        body, name="od_out", grid=(T // TM,),
        in_specs=[_tile(TM, D), _tile(TM, D), _tile(TM, D), _full((D, D)), _tile(TM, D), _tile(TM, D),
                  _full((3, D)), _full((2, D))],
        out_specs=[_tile(TM, D), _tile(TM, D), _tile(TM, D), _full((D, D)), _full((SUBLANE, D))],
        out_shape=[jax.ShapeDtypeStruct((T, D), F32), jax.ShapeDtypeStruct((T, D), BF16),
                   jax.ShapeDtypeStruct((T, D), F32), jax.ShapeDtypeStruct((D, D), BF16),
                   jax.ShapeDtypeStruct((SUBLANE, D), F32)],
        scratch_shapes=[pltpu.VMEM((D, D), F32)],
        compiler_params=_params(("arbitrary",)),
    )(hf, hb, g1, w_out, x1, tgt, mod, lnp)


def _rglru_bwd(fwd, dh, wa, wx, lam, reverse, name, comm=None):
    h, a_all, s_all, r_all, ig_all, xc_all = fwd
    T = h.shape[0]
    n = T // TS
    adj_rev = not reverse
    hprev_spec, hnext_spec = _halo_specs(TS, D, n, T, adj_rev)
    h_halo_spec = hnext_spec if reverse else hprev_spec

    def body(dh_ref, h_ref, hh_ref, a_ref, s_ref, r_ref, ig_ref, xc_ref, wa_ref, wx_ref, lam_ref,
             dxc_ref, dwa_ref, dwx_ref, vec_ref, a_s, l_s, carry, a_edge):
        i = pl.program_id(0)
        j = (n - 1 - i) if adj_rev else i

        @pl.when(i == 0)
        def _():
            carry[...] = jnp.zeros_like(carry)
            a_edge[...] = jnp.zeros_like(a_edge)
            dwa_ref[...] = jnp.zeros_like(dwa_ref)
            dwx_ref[...] = jnp.zeros_like(dwx_ref)
            vec_ref[...] = jnp.zeros_like(vec_ref)

        lam = lam_ref[...]
        sp = jnp.maximum(-lam, 0.0) + jnp.log(1.0 + jnp.exp(-jnp.abs(lam)))
        a, s = a_ref[...], s_ref[...]
        inv_s = lax.rsqrt(jnp.maximum(s * s, 1e-30))
        r, ig = r_ref[...].astype(F32), ig_ref[...].astype(F32)
        xcb = xc_ref[...]
        xc = xcb.astype(F32)

        rows = lax.broadcasted_iota(jnp.int32, (TS, D), 0)
        hcur = h_ref[...]
        if reverse:
            a_sh = jnp.where(rows == 0, a_edge[...], pltpu.roll(a, 1, 0))
            halo = jnp.where(j < n - 1, hh_ref[0:1, :], 0.0)
            h_nb = jnp.where(rows == TS - 1, halo, pltpu.roll(hcur, TS - 1, 0))
        else:
            a_sh = jnp.where(rows == TS - 1, a_edge[...], pltpu.roll(a, TS - 1, 0))
            halo = jnp.where(j > 0, hh_ref[SUBLANE - 1:SUBLANE, :], 0.0)
            h_nb = jnp.where(rows == 0, halo, pltpu.roll(hcur, 1, 0))
        a_s[...] = a_sh
        _scan_tile(a_s, dh_ref, l_s, carry, TS, adj_rev)
        a_edge[...] = a[TS - 1:TS, :] if reverse else a[0:1, :]

        lm = l_s[...]
        da = lm * h_nb
        di = lm * s * xc
        dxc = lm * s * ig
        ds = lm * ig * xc
        dlog_a = a * (da - ds * a * inv_s)
        dr = (-RG_C) * sp * dlog_a
        dsp = _rowsum((-RG_C) * r * dlog_a)
        dpr = dr * r * (1.0 - r)
        dpi = di * ig * (1.0 - ig)
        vec_ref[0:1, :] += _rowsum(dpr)
        vec_ref[1:2, :] += _rowsum(dpi)
        vec_ref[2:3, :] += dsp * (-_sigmoid(-lam))
        parts = []
        for hd in range(RNN_HEADS):
            sl = slice(hd * RNN_HD, (hd + 1) * RNN_HD)
            xh = xcb[:, sl]
            dprh = dpr[:, sl].astype(BF16)
            dpih = dpi[:, sl].astype(BF16)
            parts.append(_dot_nt(dprh, wa_ref[hd]) + _dot_nt(dpih, wx_ref[hd]))
            dwa_ref[hd] += _dot_tn(xh, dprh)
            dwx_ref[hd] += _dot_tn(xh, dpih)
        dxc_ref[...] = dxc + jnp.concatenate(parts, axis=-1)

    wspec = _full((RNN_HEADS, RNN_HD, RNN_HD))
    cur = _rev_tile(TS, D, n, adj_rev)
    return _fused_call(
        body, comm, (dh, h, h, a_all, s_all, r_all, ig_all, xc_all, wa, wx, lam), name=name, grid=(n,),
        in_specs=[cur, cur, h_halo_spec, cur, cur, cur, cur, cur, wspec, wspec, _full((1, D))],
        out_specs=[cur, wspec, wspec, _full((SUBLANE, D))],
        out_shape=[jax.ShapeDtypeStruct((T, D), F32),
                   jax.ShapeDtypeStruct((RNN_HEADS, RNN_HD, RNN_HD), F32),
                   jax.ShapeDtypeStruct((RNN_HEADS, RNN_HD, RNN_HD), F32),
                   jax.ShapeDtypeStruct((SUBLANE, D), F32)],
        scratch_shapes=[pltpu.VMEM((TS, D), F32)] * 2 + [pltpu.VMEM((1, D), F32)] * 2)


def _od_in_bwd(dxcf, dxcb, xr, dg1, x1, dx1p, mod, w_in, cw, comm=None):
    T = x1.shape[0]
    n = T // TM
    slab = OD_IN // N_DEV
    prev_spec, next_spec = _halo_specs(TM, D, n, T, False)

    def body(fp_ref, fc_ref, fn_ref, bp_ref, bc_ref, bn_ref, xr_ref, dg_ref, x1_ref, dxp_ref,
             mod_ref, w_ref, cw_ref, dx_ref, dwb_ref, vec_ref, dw_ref):
        i = pl.program_id(0)

        @pl.when(i == 0)
        def _():
            dw_ref[...] = jnp.zeros_like(dw_ref)
            vec_ref[...] = jnp.zeros_like(vec_ref)

        dcur = fc_ref[...] + bc_ref[...]
        dprev = jnp.where(i > 0, fp_ref[...] + bp_ref[...], 0.0)
        dnext = jnp.where(i < n - 1, fn_ref[...] + bn_ref[...], 0.0)
        dext = jnp.concatenate([dprev, dcur, dnext], axis=0)
        xr_v = xr_ref[...]
        cw_v = cw_ref[...]
        dxr = None
        for k in range(4):
            shifted = _shift_rows(dext, 2 - k, TM)
            term = cw_v[k:k + 1, :] * shifted
            dxr = term if dxr is None else dxr + term
            vec_ref[k:k + 1, :] += _rowsum(shifted * xr_v)
        vec_ref[4:5, :] += _rowsum(dcur)
        dp = jnp.concatenate([dxr.astype(BF16), dg_ref[...]], axis=-1)
        x1v = x1_ref[...]
        scale1 = 1.0 + mod_ref[1:2, :]
        h1 = (x1v * scale1 + mod_ref[0:1, :]).astype(BF16)
        dh1 = _dot_nt(dp, w_ref[...])
        dw_ref[...] += _dot_tn(h1, dp)
        dx_ref[...] = dxp_ref[...] + dh1 * scale1
        vec_ref[5:6, :] += _rowsum(dh1)
        vec_ref[6:7, :] += _rowsum(dh1 * x1v)

        @pl.when(i == n - 1)
        def _():
            for j in range(N_DEV):
                dwb_ref[j] = dw_ref[:, j * slab:(j + 1) * slab].astype(BF16)

    t = _tile(TM, D)
    return _fused_call(
        body, comm, (dxcf, dxcf, dxcf, dxcb, dxcb, dxcb, xr, dg1, x1, dx1p, mod, w_in, cw),
        name="od_in_bwd", grid=(n,),
        in_specs=[prev_spec, t, next_spec, prev_spec, t, next_spec, t, t, t, t,
                  _full((3, D)), _full((D, OD_IN)), _full((4, D))],
        out_specs=[t, _full((N_DEV, D, slab)), _full((SUBLANE, D))],
        out_shape=[jax.ShapeDtypeStruct((T, D), F32), jax.ShapeDtypeStruct((N_DEV, D, slab), BF16),
                   jax.ShapeDtypeStruct((SUBLANE, D), F32)],
        scratch_shapes=[pltpu.VMEM((D, OD_IN), F32)])


def _ev_out_bwd(dx1, z0, out0, y0, ycat, g0, w_out, mod, lnp):
    T = dx1.shape[0]

    def body(dx_ref, z_ref, out_ref, y0_ref, yc_ref, g_ref, w_ref, mod_ref, ln_ref,
             dxp_ref, dyc_ref, dg_ref, dwb_ref, vec_ref, dw_ref):
        i = pl.program_id(0)

        @pl.when(i == 0)
        def _():
            dw_ref[...] = jnp.zeros_like(dw_ref)
            vec_ref[...] = jnp.zeros_like(vec_ref)

        lng = ln_ref[0:1, :]
        _, xhat, rstd = _ln_fwd(z_ref[...], lng, ln_ref[1:2, :])
        dy = dx_ref[...]
        dz = _ln_bwd(dy, xhat, rstd, lng)
        vec_ref[0:1, :] += _rowsum(dy * xhat)
        vec_ref[1:2, :] += _rowsum(dy)
        vec_ref[2:3, :] += _rowsum(dz * out_ref[...].astype(F32))
        dout = (dz * mod_ref[2:3, :]).astype(BF16)
        dy0 = _dot_nt(dout, w_ref[...])
        dw_ref[...] += _dot_tn(y0_ref[...], dout)
        sg, dsg = _silu_and_grad(g_ref[...].astype(F32))
        dyc_ref[...] = (dy0 * sg).astype(BF16)
        dg_ref[...] = (dy0 * yc_ref[...].astype(F32) * dsg).astype(BF16)
        dxp_ref[...] = ALPHA * dz

        @pl.when(i == T // TM - 1)
        def _():
            dwb_ref[...] = dw_ref[...].astype(BF16)

    t = _tile(TM, D)
    return _pallas(
        body, name="ev_out_bwd", grid=(T // TM,),
        in_specs=[t, t, t, t, t, t, _full((D, D)), _full((3, D)), _full((2, D))],
        out_specs=[t, t, t, _full((D, D)), _full((SUBLANE, D))],
        out_shape=[jax.ShapeDtypeStruct((T, D), F32), jax.ShapeDtypeStruct((T, D), BF16),
                   jax.ShapeDtypeStruct((T, D), BF16), jax.ShapeDtypeStruct((D, D), BF16),
                   jax.ShapeDtypeStruct((SUBLANE, D), F32)],
        scratch_shapes=[pltpu.VMEM((D, D), F32)],
        compiler_params=_params(("arbitrary",)),
    )(dx1, z0, out0, y0, ycat, g0, w_out, mod, lnp)


def _mix0_bwd(q, kvx, lse, dyc, ycat, su, sv, sink_l, bias, a128, gsum, sel, sg_lng, sg_lnb, sg_w, sg_bfull,
              rc, rs1, rs2, comm=None):
    T = q.shape[0]
    nb = T // BLK

    def body(q_ref, kp_ref, kc_ref, kn_ref, lse_ref, dyc_ref, yc_ref, su_ref, sv_ref, sink_ref, bias_ref, a_ref,
             gsum_ref, sel_ref, lng_ref, lnb_ref, w_ref, bfull_ref, c_ref, s1_ref, s2_ref,
             dq_ref, dkv_ref, dsu_ref, dsv_ref, dw_ref, dbt_ref, vec_ref, dsink_ref):
        n = pl.program_id(0)

        @pl.when(n == 0)
        def _():
            dkv_ref[...] = jnp.zeros_like(dkv_ref)
            dw_ref[...] = jnp.zeros_like(dw_ref)
            dbt_ref[...] = jnp.zeros_like(dbt_ref)
            vec_ref[...] = jnp.zeros_like(vec_ref)
            dsink_ref[...] = jnp.zeros_like(dsink_ref)

        band = pl.ds(pl.multiple_of(n * BLK + (TM - BLK), BLK), 3 * BLK)
        bias = _band_bias(bias_ref, n, nb)
        kvx = jnp.concatenate([kp_ref[...], kc_ref[...], kn_ref[...]], axis=0)
        bias2 = jnp.concatenate([bias, bias], axis=1)
        low = lax.broadcasted_iota(jnp.int32, (BLK, LANE), 1) < HEAD_DIM
        low2 = lax.broadcasted_iota(jnp.int32, (2 * BLK, LANE), 1) < HEAD_DIM
        sel = sel_ref[...]
        c, s1, s2 = c_ref[...], s1_ref[...], s2_ref[...]
        for kvh in range(2):
            t0, t1 = 2 * kvh, 2 * kvh + 1
            q2 = jnp.concatenate([_lane_tile(q_ref, t0), _lane_tile(q_ref, t1)], axis=0)
            do2 = jnp.concatenate([_lane_tile(dyc_ref, t0), _lane_tile(dyc_ref, t1)], axis=0)
            yc2 = jnp.concatenate([_lane_tile(yc_ref, t0), _lane_tile(yc_ref, t1)], axis=0)
            p_hi, p_lo = _split_bf16(do2.astype(F32) * yc2.astype(F32))
            deltas = _dot_nt(sel, p_hi) + _dot_nt(sel, p_lo)
            dkx = jnp.zeros((3 * BLK, LANE), F32)
            dvx = jnp.zeros((3 * BLK, LANE), F32)
            dq_acc = None
            for par in range(2):
                heads = (4 * kvh + par, 4 * kvh + 2 + par)
                kt = 2 * kvh + par
                ke = kvx[:, kt * LANE:(kt + 1) * LANE]
                ve = kvx[:, (4 + kt) * LANE:(5 + kt) * LANE]
                lse = jnp.concatenate([lse_ref[0, :, h * LANE:(h + 1) * LANE] for h in heads], axis=1)
                sk = jnp.concatenate([_lane_tile(sink_ref, h) for h in heads], axis=1)
                delta = deltas[par:par + 1, :]
                pt = jnp.exp(_dot_nt(ke, q2) + bias2 - lse)
                dst = (pt * (_dot_nt(ve, do2) - delta)).astype(BF16)
                sink_terms = jnp.exp(sk - lse) * delta
                for k, h in enumerate(heads):
                    dsink_ref[:, h * LANE:(h + 1) * LANE] += sink_terms[:, k * LANE:(k + 1) * LANE]
                part = _dot_tn(dst, ke)
                dq_acc = part if dq_acc is None else dq_acc + part
                mine = low2 if par == 0 else jnp.logical_not(low2)
                dkx = dkx + jnp.dot(dst, jnp.where(mine, q2, jnp.zeros_like(q2)), preferred_element_type=F32)
                dvx = dvx + jnp.dot(pt.astype(BF16), jnp.where(mine, do2, jnp.zeros_like(do2)),
                                    preferred_element_type=F32)
            for k, t in enumerate((t0, t1)):
                dq_t = dq_acc[k * BLK:(k + 1) * BLK] * (HEAD_DIM ** -0.5)
                dq_ref[:, t * LANE:(t + 1) * LANE] = _rope_bwd(dq_t, c, s1, s2).astype(BF16)
            dkv_ref[band, kvh * LANE:(kvh + 1) * LANE] += dkx
            dkv_ref[band, (2 + kvh) * LANE:(3 + kvh) * LANE] += dvx

        lng = lng_ref[...]
        xhat, rstd, vb, svm = _sg_core(sv_ref, lng, lnb_ref[...], a_ref, w_ref, bfull_ref)
        dy = dyc_ref[:, ATTN_W:].astype(F32)
        dsu_ref[...] = (dy * svm).astype(BF16)
        dsvm = dy * su_ref[...].astype(F32)
        d_hi, d_lo = _split_bf16(dsvm)
        gsum = gsum_ref[...]
        dbt_ref[...] += jnp.dot(d_hi, gsum, preferred_element_type=F32) + jnp.dot(d_lo, gsum,
                                                                                 preferred_element_type=F32)
        tiles = []
        for t in range(SG_W // LANE):
            tl = slice(t * LANE, (t + 1) * LANE)
            dt, v2 = d_hi[:, tl], vb[:, tl]
            dw_ref[2 * t] += _dot_nt(jnp.where(low, dt, jnp.zeros_like(dt)), v2)
            dw_ref[2 * t + 1] += _dot_nt(jnp.where(low, jnp.zeros_like(dt), dt), v2)
            tiles.append(jnp.where(low, _dot_tn(w_ref[2 * t], dt), _dot_tn(w_ref[2 * t + 1], dt)))
        dvgn = jnp.concatenate(tiles, axis=-1)
        vec_ref[0:1, :] += _rowsum(dvgn * xhat)
        vec_ref[1:2, :] += _rowsum(dvgn)
        dxh = dvgn * lng
        m1 = _group_mean(dxh, a_ref)
        m2 = _group_mean(dxh * xhat, a_ref)
        dsv_ref[...] = (rstd * (dxh - m1 - xhat * m2)).astype(BF16)

    return _fused_call(
        body, comm, (q, kvx, kvx, kvx, lse, dyc, ycat, su, sv, sink_l, bias, a128, gsum, sel, sg_lng, sg_lnb, sg_w,
                     sg_bfull, rc, rs1, rs2),
        name="mix0_bwd", grid=(nb,),
        in_specs=[_tile(BLK, ATTN_W)] + _band_specs(KVX_W, nb) + [
            pl.BlockSpec((1, 1, N_HEADS * LANE), lambda n: (n, 0, 0)), _tile(BLK, D), _tile(BLK, D),
            _tile(BLK, SG_W), _tile(BLK, SG_W), _full((1, N_HEADS * LANE)), _full((3 * BLK, LANE)),
            _full((2 * LANE, 2 * LANE)),_full((SG_W, LANE)), _full((SUBLANE, LANE)), _full((1, SG_W)), _full((1, SG_W)),
            _full((SG_GROUPS, BLK, BLK)), _full((BLK, SG_W)), _tile(BLK, LANE), _tile(BLK, LANE), _tile(BLK, LANE)],
        out_specs=[_tile(BLK, ATTN_W), _full((T + 2 * TM, 4 * LANE)), _tile(BLK, SG_W), _tile(BLK, SG_W),
                   _full((SG_GROUPS, BLK, BLK)), _full((BLK, LANE)), _full((SUBLANE, SG_W)),
                   _full((1, N_HEADS * LANE))],
        out_shape=[jax.ShapeDtypeStruct((T, ATTN_W), BF16), jax.ShapeDtypeStruct((T + 2 * TM, 4 * LANE), F32),
                   jax.ShapeDtypeStruct((T, SG_W), BF16), jax.ShapeDtypeStruct((T, SG_W), BF16),
                   jax.ShapeDtypeStruct((SG_GROUPS, BLK, BLK), F32), jax.ShapeDtypeStruct((BLK, LANE), F32),
                   jax.ShapeDtypeStruct((SUBLANE, SG_W), F32), jax.ShapeDtypeStruct((1, N_HEADS * LANE), F32)])


def _ev_in_bwd(dq, dkv, dsu, dsv, dg0, x, dxp, mod, w_in, rc, rs1, rs2, comm=None):
    T = x.shape[0]

    def body(dq_ref, dkv_ref, dsu_ref, dsv_ref, dg_ref, x_ref, dxp_ref, mod_ref, w_ref, c_ref, s1_ref, s2_ref,
             dx_ref, dwb_ref, vec_ref, dw_ref):
        i = pl.program_id(0)

        @pl.when(i == 0)
        def _():
            dw_ref[...] = jnp.zeros_like(dw_ref)
            vec_ref[...] = jnp.zeros_like(vec_ref)

        low = lax.broadcasted_iota(jnp.int32, (TM, LANE), 1) < HEAD_DIM

        def fold(j):
            t0 = dkv_ref[:, (2 * j) * LANE:(2 * j + 1) * LANE]
            t1 = dkv_ref[:, (2 * j + 1) * LANE:(2 * j + 2) * LANE]
            return jnp.where(low, t0 + pltpu.roll(t0, HEAD_DIM, 1), t1 + pltpu.roll(t1, HEAD_DIM, 1))

        dk = _rope_bwd(fold(0), c_ref[...], s1_ref[...], s2_ref[...]).astype(BF16)
        dp = jnp.concatenate([dq_ref[...], dk, fold(1).astype(BF16), dsu_ref[...], dsv_ref[...],
                              dg_ref[...]], axis=-1)
        xv = x_ref[...]
        scale0 = 1.0 + mod_ref[1:2, :]
        h0 = (xv * scale0 + mod_ref[0:1, :]).astype(BF16)
        dh0 = _dot(dp, w_ref[...])
        dw_ref[...] += _dot_tn(dp, h0)
        dx_ref[...] = dxp_ref[...] + dh0 * scale0
        vec_ref[0:1, :] += _rowsum(dh0)
        vec_ref[1:2, :] += _rowsum(dh0 * xv)

        @pl.when(i == T // TM - 1)
        def _():
            dwb_ref[...] = dw_ref[...].astype(BF16)

    t = _tile(TM, D)
    return _fused_call(
        body, comm, (dq, dkv, dsu, dsv, dg0, x, dxp, mod, w_in, rc, rs1, rs2), name="ev_in_bwd", grid=(T // TM,),
        in_specs=[_tile(TM, ATTN_W), pl.BlockSpec((TM, 4 * LANE), lambda i: (i + 1, 0)), _tile(TM, SG_W),
                  _tile(TM, SG_W), t, t, t,
                  _full((3, D)), _full((EV_IN, D)), _tile(TM, LANE), _tile(TM, LANE), _tile(TM, LANE)],
        out_specs=[t, _full((EV_IN, D)), _full((SUBLANE, D))],
        out_shape=[jax.ShapeDtypeStruct((T, D), F32), jax.ShapeDtypeStruct((EV_IN, D), BF16),
                   jax.ShapeDtypeStruct((SUBLANE, D), F32)],
        scratch_shapes=[pltpu.VMEM((EV_IN, D), F32)])


def _sum_slots(land_ref):
    g = land_ref[0].astype(F32)
    for i in range(1, land_ref.shape[0]):
        g = g + land_ref[i].astype(F32)
    return g


def _reduce_adam(items, name):
    R, C = items[0][1].shape
    rb = R
    if R > 512:
        for cand in (512, 256, 128, 64, 32, 16, 8):
            if R % cand == 0:
                rb = cand
                break
    n = len(items)

    def body(*refs):
        for k in range(n):
            l_ref, w_ref, m_ref, v_ref = refs[4 * k:4 * k + 4]
            g_ref, d_ref, nm_ref, nv_ref = refs[4 * n + 4 * k:4 * n + 4 * k + 4]
            g = _sum_slots(l_ref)
            g_ref[...] = g
            dlt, m2, v2 = _adam(w_ref[...], g, m_ref[...], v_ref[...])
            d_ref[...] = dlt
            nm_ref[...] = m2
            nv_ref[...] = v2

    t = pl.BlockSpec((rb, C), lambda i: (i, 0))
    shp = jax.ShapeDtypeStruct((R, C), F32)
    in_specs, operands = [], []
    for land, w, m, v in items:
        in_specs += [pl.BlockSpec((land.shape[0], rb, C), lambda i: (0, i, 0)), t, t, t]
        operands += [land, w, m, v]
    res = _pallas(
        body, name=name, grid=(R // rb,),
        in_specs=in_specs, out_specs=[t] * (4 * n), out_shape=[shp] * (4 * n),
        compiler_params=_params(("parallel",)),
    )(*operands)
    return [list(res[4 * k:4 * k + 4]) for k in range(n)]


def _tail_exchange(slabs, small):
    _, R, C = slabs.shape
    n_chips = N_DEV // 2
    gather = _GatherComm(small)
    ns = gather.n

    def body(*refs):
        slab_ref = refs[0]
        g_ins = refs[1:1 + ns]
        land_ref = refs[1 + ns]
        g_outs = refs[2 + ns:2 + 2 * ns]
        stage, part, s1_send, s1_recv, s2_send, s2_recv = refs[2 + 2 * ns:8 + 2 * ns]
        g_sems = refs[8 + 2 * ns:]
        x, y, c = _my_pos()
        chip = 2 * x + y
        gather.start(g_ins, g_outs, g_sems)

        swaps = [pltpu.make_async_remote_copy(
            src_ref=slab_ref.at[2 * k + (1 - c)], dst_ref=stage.at[k], send_sem=s1_send.at[k],
            recv_sem=s1_recv.at[k], device_id=(x, y, 1 - c), device_id_type=MESH) for k in range(n_chips)]
        for cp in swaps:
            cp.start()
        for cp in swaps:
            cp.wait()
        for k in range(n_chips):
            part[k] = (slab_ref[2 * k + c].astype(F32) + stage[k].astype(F32)).astype(BF16)

        gather.mid(g_ins, g_outs, g_sems)

        sends = []
        for r in range(1, n_chips):
            px = (1 - x) if (r & 2) else x
            py = (1 - y) if (r & 1) else y
            sends.append(pltpu.make_async_remote_copy(
                src_ref=part.at[2 * px + py], dst_ref=land_ref.at[chip], send_sem=s2_send.at[r - 1],
                recv_sem=s2_recv.at[r - 1], device_id=(px, py, c), device_id_type=MESH))
        for cp in sends:
            cp.start()
        land_ref[chip] = part[chip]
        for cp in sends:
            cp.wait()
        gather.finish(g_ins, g_outs, g_sems)

    any_spec = pl.BlockSpec(memory_space=pl.ANY)
    vmem_spec = pl.BlockSpec(memory_space=pltpu.VMEM)
    res = _pallas(
        body, name="tail_exchange",
        out_shape=[jax.ShapeDtypeStruct((n_chips, R, C), BF16)] + gather.out_shapes(),
        in_specs=[vmem_spec] + [any_spec] * ns, out_specs=[vmem_spec] + [any_spec] * ns,
        scratch_shapes=[pltpu.VMEM((n_chips, R, C), BF16), pltpu.VMEM((n_chips, R, C), BF16),
                        pltpu.SemaphoreType.DMA((n_chips,)), pltpu.SemaphoreType.DMA((n_chips,)),
                        pltpu.SemaphoreType.DMA((n_chips - 1,)), pltpu.SemaphoreType.DMA((n_chips - 1,))]
        + gather.sems(),
        compiler_params=pltpu.CompilerParams(vmem_limit_bytes=VMEM_LIMIT),
    )(slabs, *gather.arrs)
    return res[0], list(res[1:])


def _slots_adam(items, name):
    zeros3 = (0, 0, 0)
    in_specs, out_specs, out_shape, operands = [], [], [], []
    for land, w, m, v in items:
        inner = w.shape[-3:]
        if w.ndim == 5:
            lspec = pl.BlockSpec((N_DEV, 1) + inner, lambda i: (0, i) + zeros3)
            wspec = pl.BlockSpec((1, 1) + inner, lambda i: (0, i) + zeros3)
        else:
            lspec = pl.BlockSpec((N_DEV,) + inner, lambda i: (0,) + zeros3)
            wspec = pl.BlockSpec((1,) + inner, lambda i: (0,) + zeros3)
        in_specs += [lspec, wspec, wspec, wspec]
        out_specs += [wspec] * 4
        out_shape += [jax.ShapeDtypeStruct(w.shape, F32)] * 4
        operands += [land, w, m, v]
    n = len(items)

    def body(*refs):
        for k, (_, w, _, _) in enumerate(items):
            l_ref, w_ref, m_ref, v_ref = refs[4 * k:4 * k + 4]
            outs = refs[4 * n + 4 * k:4 * n + 4 * k + 4]
            at = (0, 0) if w.ndim == 5 else (0,)

            def update(l_ref=l_ref, w_ref=w_ref, m_ref=m_ref, v_ref=v_ref, outs=outs, at=at):
                g = l_ref[(0,) + at[1:]].astype(F32)
                for i in range(1, N_DEV):
                    g = g + l_ref[(i,) + at[1:]].astype(F32)
                dlt, m2, v2 = _adam(w_ref[at], g, m_ref[at], v_ref[at])
                for o_ref, val in zip(outs, (g, dlt, m2, v2)):
                    o_ref[at] = val

            if w.ndim == 5:
                update()
            else:
                pl.when(pl.program_id(0) == 0)(update)

    res = _pallas(
        body, name=name, grid=(2,),
        in_specs=in_specs, out_specs=out_specs, out_shape=out_shape,
        compiler_params=_params(("arbitrary",)),
    )(*operands)
    return [list(res[4 * k:4 * k + 4]) for k in range(n)]


SMALL_PARAMS = ("ln_g", "ln_b", "ev_sg_ln_g", "ev_sg_ln_b", "ev_sink", "ev_sg_b",
                "od_conv_w", "od_conv_b", "od_b_a", "od_b_x", "od_lam")


def _small_update(ga, gc, gd, gf, gb, ge, gsink, gbt, params):
    names = list(SMALL_PARAMS)
    flat = [a for nm in names for a in params[nm]]
    n_g = 8

    def body(*refs):
        ga_ref, gc_ref, gd_ref, gf_ref, gb_ref, ge_ref, gs_ref, gbt_ref = refs[:n_g]
        prm = refs[n_g:n_g + 3 * len(names)]
        loss_ref = refs[n_g + 3 * len(names)]
        outs = refs[n_g + 3 * len(names) + 1:]

        def ssum(ref):
            acc = ref[0]
            for i in range(1, N_DEV):
                acc = acc + ref[i]
            return acc

        a, cc, dd, ff, bb, ee = ssum(ga_ref), ssum(gc_ref), ssum(gd_ref), ssum(gf_ref), ssum(gb_ref), ssum(ge_ref)
        loss_ref[...] = a[3:4, 0:LANE]
        me = _slot(*_my_pos())

        def mine(rows):
            acc = jnp.zeros((rows.shape[0], LANE), F32)
            for j in range(N_DEV):
                acc = acc + jnp.where(me == j, rows[:, j * LANE:(j + 1) * LANE], 0.0)
            return acc

        sink_terms = ssum(gs_ref)
        lane8 = lax.broadcasted_iota(jnp.int32, (1, N_HEADS), 1)
        g_sink = jnp.zeros((1, N_HEADS), F32)
        for h in range(N_HEADS):
            tot = -jnp.sum(sink_terms[:, h * LANE:(h + 1) * LANE], axis=1, keepdims=True)
            g_sink = jnp.where(lane8 == h, tot, g_sink)
        grads = dict(
            ln_g=jnp.concatenate([dd[0:1], a[0:1]], axis=0), ln_b=jnp.concatenate([dd[1:2], a[1:2]], axis=0),
            ev_sg_ln_g=ee[0:1], ev_sg_ln_b=ee[1:2], ev_sink=g_sink,
            ev_sg_b=jnp.transpose(ssum(gbt_ref))[0:SG_GROUPS, :],
            od_conv_w=mine(cc[0:4]), od_conv_b=mine(cc[4:5]),
            od_b_a=mine(jnp.concatenate([ff[0:1], bb[0:1]], axis=0)),
            od_b_x=mine(jnp.concatenate([ff[1:2], bb[1:2]], axis=0)),
            od_lam=mine(jnp.concatenate([ff[2:3], bb[2:3]], axis=0)))
        for k, nm in enumerate(names):
            w_ref, m_ref, v_ref = prm[3 * k:3 * k + 3]
            at = (0,) if len(w_ref.shape) == 3 else ()
            g = grads[nm]
            dlt, m2, v2 = _adam(w_ref[at] if at else w_ref[...], g, m_ref[at] if at else m_ref[...],
                                v_ref[at] if at else v_ref[...])
            for o_ref, val in zip(outs[4 * k:4 * k + 4], (g, dlt, m2, v2)):
                if at:
                    o_ref[at] = val
                else:
                    o_ref[...] = val

    gathered = [ga, gc, gd, gf, gb, ge, gsink, gbt]
    out_shape = [jax.ShapeDtypeStruct((1, LANE), F32)]
    for nm in names:
        out_shape += [jax.ShapeDtypeStruct(params[nm][0].shape, F32)] * 4
    return _pallas(
        body, name="small_update", grid=(1,),
        in_specs=[_full(a.shape) for a in gathered + flat],
        out_specs=[_full(s.shape) for s in out_shape], out_shape=out_shape,
        compiler_params=_params(("arbitrary",)),
    )(*gathered, *flat)


VEC_ROWS = 16
VEC_LAYOUT = (("od_conv_w", 4), ("od_conv_b", 1), ("od_b_a", 2), ("od_b_x", 2), ("od_lam", 2))


def _pack_vec(parts):
    rows = [parts[name].reshape(nrows, -1) for name, nrows in VEC_LAYOUT]
    used = sum(r for _, r in VEC_LAYOUT)
    rows.append(jnp.zeros((VEC_ROWS - used, rows[0].shape[1]), F32))
    return jnp.concatenate(rows, axis=0)


def _to_slabs(full, cols_per):
    R = full.shape[0]
    return full.reshape(R, N_DEV, cols_per).transpose(1, 0, 2)


def _from_slabs(slabs):
    n, R, cp = slabs.shape
    return slabs.transpose(1, 0, 2).reshape(R, n * cp)


def kernel(x, c, positions, ada_w, ada_b, ln_g, ln_b, ev_w_in, ev_w_out, ev_sink, ev_sg_ln_g, ev_sg_ln_b, ev_sg_w, ev_sg_b, od_w_in, od_conv_w, od_conv_b, od_w_a, od_b_a, od_w_x, od_b_x, od_lam, od_w_out, loss_target, m_ada_w, m_ada_b, m_ln_g, m_ln_b, m_ev_w_in, m_ev_w_out, m_ev_sink, m_ev_sg_ln_g, m_ev_sg_ln_b, m_ev_sg_w, m_ev_sg_b, m_od_w_in, m_od_conv_w, m_od_conv_b, m_od_w_a, m_od_b_a, m_od_w_x, m_od_b_x, m_od_lam, m_od_w_out, v_ada_w, v_ada_b, v_ln_g, v_ln_b, v_ev_w_in, v_ev_w_out, v_ev_sink, v_ev_sg_ln_g, v_ev_sg_ln_b, v_ev_sg_w, v_ev_sg_b, v_od_w_in, v_od_conv_w, v_od_conv_b, v_od_w_a, v_od_b_a, v_od_w_x, v_od_b_x, v_od_lam, v_od_w_out):
    T = x.shape[1]
    me = _slot(*_my_pos())
    xs = x.reshape(T, D)
    tgt = loss_target.reshape(T, D)

    vec_w = _pack_vec(dict(od_conv_w=od_conv_w[0], od_conv_b=od_conv_b, od_b_a=od_b_a[0], od_b_x=od_b_x[0],
                           od_lam=od_lam[0]))
    c_all, mod_all, (g_ev_in, g_vec) = _head_gather(c, ada_w, [ev_w_in[0].T.astype(BF16), vec_w])
    c_all = c_all.reshape(N_DEV, D)
    w_ev_in = g_ev_in.reshape(EV_IN, D)
    vec_full = _from_slabs(g_vec)
    cw, cb = vec_full[0:4], vec_full[4:5]
    ba, bx, lam = vec_full[5:7], vec_full[7:9], vec_full[9:11]
    mod_mine = lax.dynamic_index_in_dim(mod_all, me, axis=2, keepdims=False)
    mod = mod_mine.transpose(1, 0, 2).reshape(2, 3 * D) + ada_b
    mod0 = mod[0].reshape(3, D)
    mod1 = mod[1].reshape(3, D)

    half = 8
    inv_freq = jnp.power(jnp.float32(ROPE_THETA), -jnp.arange(half, dtype=F32) / half)
    ang = positions.reshape(T).astype(F32)[:, None] * inv_freq
    cos_t = jnp.tile(jnp.cos(ang), (1, LANE // half))
    sin_t = jnp.tile(jnp.sin(ang), (1, LANE // half))
    l64 = jnp.arange(LANE) % HEAD_DIM
    rc = jnp.where(l64 < 2 * half, cos_t, 1.0)
    rs1 = jnp.where(l64 < half, -sin_t, 0.0)
    rs2 = jnp.where((l64 >= half) & (l64 < 2 * half), sin_t, 0.0)

    ln0 = jnp.stack([ln_g[0], ln_b[0]])
    ln1 = jnp.stack([ln_g[1], ln_b[1]])
    sg_lng = ev_sg_ln_g
    sg_lnb = ev_sg_ln_b
    sg_w = ev_sg_w[0].astype(BF16)
    sg_bfull = jnp.repeat(ev_sg_b[0].T, SG_DIM, axis=1)
    sink_l = jnp.repeat(ev_sink, LANE, axis=1)
    kj = jnp.arange(3 * BLK)[:, None]
    qi = jnp.arange(BLK)[None, :]
    band_bias = jnp.where(jnp.abs(kj - BLK - qi) <= BLK, 0.0, NEG_INF).astype(F32)
    lanes = jnp.arange(LANE)
    lanes2 = jnp.arange(2 * LANE)
    a128 = jnp.where(lanes2[:, None] // SG_DIM == lanes2[None, :] // SG_DIM, 1.0 / SG_DIM, 0.0).astype(BF16)
    gsum = (jnp.arange(SG_W)[:, None] // SG_DIM == lanes[None, :]).astype(BF16)
    sel = (jnp.arange(SUBLANE)[:, None] == lanes[None, :] // HEAD_DIM).astype(BF16)
    wa = od_w_a[0].astype(BF16)
    wx = od_w_x[0].astype(BF16)

    (q, kvx, su, sv, g0), _ = _ev_in(xs, mod0, w_ev_in, rc, rs1, rs2)
    (ycat, y0, lse), (g_ev_out, g_od_in, g_od_out) = _mix0_fwd(
        q, kvx, su, sv, g0, sink_l, band_bias, a128, sg_lng, sg_lnb, sg_w, sg_bfull,
        _GatherComm([ev_w_out[0].astype(BF16), od_w_in[0].astype(BF16), od_w_out[0].astype(BF16)], mid_frac=0.75))
    w_ev_out = g_ev_out.reshape(D, D)
    w_od_in = _from_slabs(g_od_in)
    w_od_out = g_od_out.reshape(D, D)
    out0, z0, x1 = _ev_out(y0, w_ev_out, xs, mod0, ln0)
    xr, g1 = _od_in(x1, mod1, w_od_in)
    fwd_f = _rglru_fwd(xr, cw, cb, wa[0], wx[0], ba[0:1], bx[0:1], lam[0:1], False, "rglru_fwd_f")
    fwd_b = _rglru_fwd(xr, cw, cb, wa[1], wx[1], ba[1:2], bx[1:2], lam[1:2], True, "rglru_fwd_b")
    dh, dg1, dx1p, d_od_out, vec_a = _od_out(fwd_f[0], fwd_b[0], g1, w_od_out, x1, tgt, mod1, ln1)

    (dxcf, dwa_f, dwx_f, vec_f), (l_od_out,) = _rglru_bwd(
        fwd_f, dh, wa[0], wx[0], lam[0:1], False, "rglru_bwd_f",
        _ExchangeComm([d_od_out.reshape(N_DEV, D // N_DEV, D)]))
    (dxcb, dwa_b, dwx_b, vec_b), _ = _rglru_bwd(fwd_b, dh, wa[1], wx[1], lam[1:2], True, "rglru_bwd_b")
    (dx1, d_od_in, vec_c), (a_wa, a_wx) = _od_in_bwd(
        dxcf, dxcb, xr, dg1, x1, dx1p, mod1, w_od_in, cw,
        _GatherComm([jnp.stack([dwa_f, dwa_b]).astype(BF16), jnp.stack([dwx_f, dwx_b]).astype(BF16)],
                    mid_frac=0.75))
    dxp, dyc, dg0, d_ev_out, vec_d = _ev_out_bwd(dx1, z0, out0, y0, ycat, g0, w_ev_out, mod0, ln0)
    (dq, dkv, dsu, dsv, d_sg_w, d_sg_bt, vec_e, d_sink_l), (l_od_in, l_ev_out) = _mix0_bwd(
        q, kvx, lse, dyc, ycat, su, sv, sink_l, band_bias, a128, gsum, sel, sg_lng, sg_lnb, sg_w, sg_bfull,
        rc, rs1, rs2, _ExchangeComm([d_od_in, d_ev_out.reshape(N_DEV, D // N_DEV, D)]))
    (grad_x, d_ev_in, vec_g), _ = _ev_in_bwd(dq, dkv, dsu, dsv, dg0, xs, dxp, mod0, w_ev_in, rc, rs1, rs2)

    l_ev_in, (ga, gc, gd, gf, gb, gg, ge, gsink, gbt, a_sgw) = _tail_exchange(
        d_ev_in.reshape(N_DEV, EV_IN // N_DEV, D),
        [vec_a, vec_c, vec_d, vec_f, vec_b, vec_g, vec_e, d_sink_l, d_sg_bt, d_sg_w.astype(BF16)])

    dmod_all = jnp.stack([jnp.concatenate([gg[:, 0], gg[:, 1], gd[:, 2]], axis=-1),
                          jnp.concatenate([gc[:, 5], gc[:, 6], ga[:, 2]], axis=-1)], axis=1)
    cols = ada_w.shape[2]
    dmod_cols = lax.dynamic_slice_in_dim(dmod_all, me * cols, cols, axis=2).transpose(1, 0, 2)
    (g_ada_w, d_ada_w, nm_ada_w, nv_ada_w, g_ada_b, d_ada_b, nm_ada_b, nv_ada_b) = _ada_update(
        c_all, dmod_cols, dmod_all, ada_w, m_ada_w, v_ada_w, ada_b, m_ada_b, v_ada_b)

    res = dict(ada_w=[g_ada_w, d_ada_w, nm_ada_w, nv_ada_w], ada_b=[g_ada_b, d_ada_b, nm_ada_b, nv_ada_b])
    (r_ev_in,) = _reduce_adam([(l_ev_in, ev_w_in[0].T, m_ev_w_in[0].T, v_ev_w_in[0].T)], "adam_ev_w_in")
    res["ev_w_in"] = [a.T[None] for a in r_ev_in]
    (r_od_in,) = _reduce_adam([(l_od_in, od_w_in[0], m_od_w_in[0], v_od_w_in[0])], "adam_od_w_in")
    r_ev_out, r_od_out = _reduce_adam([(l_ev_out, ev_w_out[0], m_ev_w_out[0], v_ev_w_out[0]),
                                       (l_od_out, od_w_out[0], m_od_w_out[0], v_od_w_out[0])], "adam_w_out")
    for name, r in (("od_w_in", r_od_in), ("ev_w_out", r_ev_out), ("od_w_out", r_od_out)):
        res[name] = [a[None] for a in r]
    res["od_w_a"], res["od_w_x"], res["ev_sg_w"] = _slots_adam(
        [(a_wa, od_w_a, m_od_w_a, v_od_w_a), (a_wx, od_w_x, m_od_w_x, v_od_w_x),
         (a_sgw, ev_sg_w, m_ev_sg_w, v_ev_sg_w)], "adam_gates")
    small = dict(ln_g=(ln_g, m_ln_g, v_ln_g), ln_b=(ln_b, m_ln_b, v_ln_b),
                 ev_sg_ln_g=(ev_sg_ln_g, m_ev_sg_ln_g, v_ev_sg_ln_g),
                 ev_sg_ln_b=(ev_sg_ln_b, m_ev_sg_ln_b, v_ev_sg_ln_b),
                 ev_sink=(ev_sink, m_ev_sink, v_ev_sink), ev_sg_b=(ev_sg_b, m_ev_sg_b, v_ev_sg_b),
                 od_conv_w=(od_conv_w, m_od_conv_w, v_od_conv_w), od_conv_b=(od_conv_b, m_od_conv_b, v_od_conv_b),
                 od_b_a=(od_b_a, m_od_b_a, v_od_b_a), od_b_x=(od_b_x, m_od_b_x, v_od_b_x),
                 od_lam=(od_lam, m_od_lam, v_od_lam))
    small_out = _small_update(ga, gc, gd, gf, gb, ge, gsink, gbt, small)
    loss = small_out[0][0, 0]
    for k, name in enumerate(SMALL_PARAMS):
        res[name] = small_out[1 + 4 * k:5 + 4 * k]

    order = ["ada_w", "ada_b", "ln_g", "ln_b", "ev_w_in", "ev_w_out", "ev_sink", "ev_sg_ln_g", "ev_sg_ln_b",
             "ev_sg_w", "ev_sg_b", "od_w_in", "od_conv_w", "od_conv_b", "od_w_a", "od_b_a", "od_w_x", "od_b_x",
             "od_lam", "od_w_out"]
    outs = [loss, grad_x.reshape(1, T, D)]
    for kind in range(4):
        outs += [res[name][kind] for name in order]
    return tuple(outs)
```

```python
import functools

import jax
import jax.numpy as jnp
from jax import lax
from jax.experimental import pallas as pl
from jax.experimental.pallas import tpu as pltpu

F32 = jnp.float32
BF16 = jnp.bfloat16

N_DEV = 8
D = 1024
N_HEADS = 8
HEAD_DIM = 64
KV_WIDTH = 128
ATTN_W = 512
SG_W = 512
SG_GROUPS = 8
SG_DIM = 64
BLK = 128
KVX_W = 1024
EV_IN = 2816
OD_IN = 2048
RNN_HEADS = 8
RNN_HD = 128
ALPHA = 4.0 ** 0.25
LN_EPS = 1e-5
NEG_INF = -1e30
RG_C = 8.0
ROPE_THETA = 500000.0
LR, B1, B2, EPS, WD, STEP = 0.001, 0.9, 0.999, 1e-08, 0.01, 10

LANE = 128
SUBLANE = 8
TM = 256
TMF = 512
TS = 256
VMEM_LIMIT = 56 * 1024 * 1024

MESH = pl.DeviceIdType.MESH


def _pallas(body, **kw):
    return pl.pallas_call(body, **kw)


def _params(sem, vmem=VMEM_LIMIT):
    return pltpu.CompilerParams(dimension_semantics=sem, vmem_limit_bytes=vmem)


def _sigmoid(x):
    return 0.5 * jnp.tanh(0.5 * x) + 0.5


def _silu_and_grad(x):
    s = _sigmoid(x)
    return x * s, s * (1.0 + x * (1.0 - s))


def _dot(a, b):
    return jnp.dot(a.astype(BF16), b.astype(BF16), preferred_element_type=F32)


def _dot_nt(a, b):
    return lax.dot_general(a.astype(BF16), b.astype(BF16), (((1,), (1,)), ((), ())), preferred_element_type=F32)


def _dot_tn(a, b):
    return lax.dot_general(a.astype(BF16), b.astype(BF16), (((0,), (0,)), ((), ())), preferred_element_type=F32)


def _ln_fwd(z, g, b):
    mu = jnp.mean(z, axis=-1, keepdims=True)
    zc = z - mu
    var = jnp.mean(zc * zc, axis=-1, keepdims=True)
    rstd = lax.rsqrt(var + LN_EPS)
    xhat = zc * rstd
    return xhat * g + b, xhat, rstd


def _ln_bwd(dy, xhat, rstd, g):
    dxh = dy * g
    m1 = jnp.mean(dxh, axis=-1, keepdims=True)
    m2 = jnp.mean(dxh * xhat, axis=-1, keepdims=True)
    return rstd * (dxh - m1 - xhat * m2)


def _rowsum(v):
    return jnp.sum(v, axis=0, keepdims=True)


def _rope_fwd(t, c, s1, s2):
    return t * c + pltpu.roll(t, LANE - 8, 1) * s1 + pltpu.roll(t, 8, 1) * s2


def _rope_bwd(d, c, s1, s2):
    return d * c + pltpu.roll(d * s1, 8, 1) + pltpu.roll(d * s2, LANE - 8, 1)


def _adam(w, g, m, v):
    m2 = B1 * m + (1.0 - B1) * g
    v2 = B2 * v + (1.0 - B2) * (g * g)
    m_hat = m2 / (1.0 - B1 ** STEP)
    v_hat = v2 / (1.0 - B2 ** STEP)
    delta = -LR * (m_hat / (jnp.sqrt(v_hat) + EPS) + WD * w)
    return delta, m2, v2


def _tile(rows, width):
    return pl.BlockSpec((rows, width), lambda i: (i, 0))


def _full(shape):
    zeros = (0,) * len(shape)
    return pl.BlockSpec(shape, lambda i: zeros)


def _rev_tile(rows, width, n, reverse):
    if reverse:
        return pl.BlockSpec((rows, width), lambda i: (n - 1 - i, 0))
    return pl.BlockSpec((rows, width), lambda i: (i, 0))


def _halo_specs(rows, width, n, total_rows, reverse):
    per = rows // SUBLANE
    last = total_rows // SUBLANE - 1

    def tile_of(i):
        return (n - 1 - i) if reverse else i

    prev = pl.BlockSpec((SUBLANE, width), lambda i: (jnp.maximum(tile_of(i) * per - 1, 0), 0))
    nxt = pl.BlockSpec((SUBLANE, width), lambda i: (jnp.minimum((tile_of(i) + 1) * per, last), 0))
    return prev, nxt


def _my_pos():
    return lax.axis_index("x"), lax.axis_index("y"), lax.axis_index("c")


def _slot(px, py, pc):
    return 4 * px + 2 * py + pc


class _GatherComm:
    has_mid = True

    def __init__(self, arrs, mid_frac=0.5):
        self.arrs = list(arrs)
        self.n = len(self.arrs)
        self.mid_frac = mid_frac

    def out_shapes(self):
        return [jax.ShapeDtypeStruct((N_DEV,) + a.shape, a.dtype) for a in self.arrs]

    def sems(self):
        return [pltpu.SemaphoreType.DMA((7 * self.n,)), pltpu.SemaphoreType.DMA((7 * self.n,)),
                pltpu.SemaphoreType.DMA((self.n,))]

    def _parts(self, ins, outs, sems):
        send_sems, recv_sems, local_sems = sems
        x, y, c = _my_pos()
        me, sibling = (x, y, c), (x, y, 1 - c)
        chips = [(1 - x, y), (x, 1 - y), (1 - x, 1 - y)]

        def copy(a, k, block, to, src=None):
            dst = outs[a].at[_slot(*block)]
            return pltpu.make_async_remote_copy(
                src_ref=dst if src is None else src, dst_ref=dst,
                send_sem=send_sems.at[a * 7 + k], recv_sem=recv_sems.at[a * 7 + k],
                device_id=to, device_id_type=MESH)

        local = [pltpu.make_async_copy(ins[a], outs[a].at[_slot(*me)], local_sems.at[a]) for a in range(self.n)]
        first = []
        for a in range(self.n):
            first.append(copy(a, 0, me, sibling, src=ins[a]))
            first += [copy(a, 1 + j, me, (*chip, c), src=ins[a]) for j, chip in enumerate(chips)]
        ici_in = [copy(a, 1 + j, (*chip, c), me) for j, chip in enumerate(chips) for a in range(self.n)]
        passed = [copy(a, 4 + j, (*chip, c), sibling) for j, chip in enumerate(chips) for a in range(self.n)]
        d2d_in = []
        for a in range(self.n):
            d2d_in.append(copy(a, 0, sibling, me))
            d2d_in += [copy(a, 4 + j, (*chip, 1 - c), me) for j, chip in enumerate(chips)]
        return local, first, ici_in, passed, d2d_in

    def start(self, ins, outs, sems):
        local, first, _, _, _ = self._parts(ins, outs, sems)
        for cp in local + first:
            cp.start()

    def mid(self, ins, outs, sems):
        _, _, ici_in, passed, _ = self._parts(ins, outs, sems)
        for arrived, fw in zip(ici_in, passed):
            arrived.wait_recv()
            fw.start()

    def finish(self, ins, outs, sems):
        local, first, _, passed, d2d_in = self._parts(ins, outs, sems)
        for cp in d2d_in:
            cp.wait_recv()
        for cp in first + passed:
            cp.wait_send()
        for cp in local:
            cp.wait()


class _ExchangeComm:
    has_mid = False

    def __init__(self, arrs):
        self.arrs = list(arrs)
        self.n = len(self.arrs)

    def out_shapes(self):
        return [jax.ShapeDtypeStruct(a.shape, a.dtype) for a in self.arrs]

    def sems(self):
        return [pltpu.SemaphoreType.DMA((7 * self.n,)), pltpu.SemaphoreType.DMA((7 * self.n,)),
                pltpu.SemaphoreType.DMA((self.n,))]

    def _copies(self, ins, outs, sems):
        send_sems, recv_sems, local_sems = sems
        x, y, c = _my_pos()
        mine = _slot(x, y, c)
        copies = [pltpu.make_async_copy(ins[a].at[mine], outs[a].at[mine], local_sems.at[a]) for a in range(self.n)]
        for k in range(1, N_DEV):
            px = (1 - x) if (k & 4) else x
            py = (1 - y) if (k & 2) else y
            pc = (1 - c) if (k & 1) else c
            for a in range(self.n):
                copies.append(pltpu.make_async_remote_copy(
                    src_ref=ins[a].at[_slot(px, py, pc)], dst_ref=outs[a].at[mine],
                    send_sem=send_sems.at[a * 7 + k - 1], recv_sem=recv_sems.at[a * 7 + k - 1],
                    device_id=(px, py, pc), device_id_type=MESH))
        return copies

    def start(self, ins, outs, sems):
        for cp in self._copies(ins, outs, sems):
            cp.start()

    def finish(self, ins, outs, sems):
        for cp in self._copies(ins, outs, sems):
            cp.wait()


def _fused_call(body, comm, operands, *, name, grid, in_specs, out_specs, out_shape, scratch_shapes=(),
                semantics=("arbitrary",)):
    n_in, n_out, n_scr = len(in_specs), len(out_specs), len(scratch_shapes)
    if comm is None:
        res = _pallas(body, name=name, grid=grid, in_specs=list(in_specs), out_specs=list(out_specs),
                      out_shape=list(out_shape), scratch_shapes=list(scratch_shapes),
                      compiler_params=_params(semantics))(*operands)
        return list(res), []
    k = comm.n
    steps = grid[0]

    def wrapped(*refs):
        ins, cins = refs[:n_in], refs[n_in:n_in + k]
        outs = refs[n_in + k:n_in + k + n_out]
        couts = refs[n_in + k + n_out:n_in + 2 * k + n_out]
        rest = refs[n_in + 2 * k + n_out:]
        scratch, sems = rest[:n_scr], rest[n_scr:]
        i = pl.program_id(0)

        @pl.when(i == 0)
        def _():
            comm.start(cins, couts, sems)

        body(*ins, *outs, *scratch)

        if comm.has_mid:
            @pl.when(i == int(steps * comm.mid_frac))
            def _():
                comm.mid(cins, couts, sems)

        @pl.when(i == steps - 1)
        def _():
            comm.finish(cins, couts, sems)

    any_spec = pl.BlockSpec(memory_space=pl.ANY)
    res = _pallas(wrapped, name=name, grid=grid, in_specs=list(in_specs) + [any_spec] * k,
                  out_specs=list(out_specs) + [any_spec] * k, out_shape=list(out_shape) + comm.out_shapes(),
                  scratch_shapes=list(scratch_shapes) + comm.sems(),
                  compiler_params=_params(("arbitrary",)))(*operands, *comm.arrs)
    return list(res[:n_out]), list(res[n_out:])


def _head_gather(c, ada_w, big, to_cast, vec_parts):
    cols = ada_w.shape[2]
    g_c, g_big = _GatherComm([c]), _GatherComm(big)
    g_mod = _GatherComm([jax.ShapeDtypeStruct((2, N_DEV, cols), F32)])
    g_vec = _GatherComm([jax.ShapeDtypeStruct((VEC_ROWS, LANE), F32)])
    nb, nc, nv = g_big.n, len(to_cast), len(vec_parts)

    def body(*refs):
        c_ref, w_ref = refs[0], refs[1]
        vec_in = refs[2:2 + nv]
        cast_in = refs[2 + nv:2 + nv + nc]
        big_in = refs[2 + nv + nc:2 + nv + nc + nb]
        outs = refs[2 + nv + nc + nb:]
        c_all_ref, mod_all_ref, vec_all_ref = outs[0], outs[1], outs[2]
        cast_out = outs[3:3 + nc]
        big_out = outs[3 + nc:3 + nc + nb]
        part_ref, pack_ref = outs[3 + nc + nb], outs[4 + nc + nb]
        sems = outs[5 + nc + nb:]
        s_c, s_mod, s_big, s_vec = sems[0:3], sems[3:6], sems[6:9], sems[9:12]
        g_c.start([c_ref], [c_all_ref], s_c)
        g_big.start(big_in, big_out, s_big)
        pack_ref[...] = jnp.zeros_like(pack_ref)
        row = 0
        for ref, (_, nrows) in zip(vec_in, VEC_LAYOUT):
            pack_ref[row:row + nrows, :] = ref[0] if len(ref.shape) == 3 else ref[...]
            row += nrows
        g_vec.start([pack_ref], [vec_all_ref], s_vec)
        g_c.mid([c_ref], [c_all_ref], s_c)
        g_c.finish([c_ref], [c_all_ref], s_c)
        cv = c_all_ref[:, 0, :]
        cond = cv * _sigmoid(cv)
        for l in range(2):
            part_ref[l] = _dot(cond, w_ref[l])
        g_mod.start([part_ref], [mod_all_ref], s_mod)
        for src, dst in zip(cast_in, cast_out):
            dst[...] = src[...].astype(BF16)
        for g, ins, outs_, sm in ((g_vec, [pack_ref], [vec_all_ref], s_vec), (g_mod, [part_ref], [mod_all_ref], s_mod),
                                  (g_big, big_in, big_out, s_big)):
            g.mid(ins, outs_, sm)
            g.finish(ins, outs_, sm)

    any_spec = pl.BlockSpec(memory_space=pl.ANY)
    vmem_spec = pl.BlockSpec(memory_space=pltpu.VMEM)
    res = _pallas(
        body, name="head_gather",
        out_shape=(g_c.out_shapes() + g_mod.out_shapes() + g_vec.out_shapes()
                   + [jax.ShapeDtypeStruct(a.shape, BF16) for a in to_cast] + g_big.out_shapes()),
        in_specs=[vmem_spec] * (2 + nv + nc) + [any_spec] * nb,
        out_specs=[vmem_spec] * (3 + nc) + [any_spec] * nb,
        scratch_shapes=[pltpu.VMEM((2, N_DEV, cols), F32), pltpu.VMEM((VEC_ROWS, LANE), F32)]
        + g_c.sems() + g_mod.sems() + g_big.sems() + g_vec.sems(),
        compiler_params=pltpu.CompilerParams(vmem_limit_bytes=VMEM_LIMIT),
    )(c, ada_w, *vec_parts, *to_cast, *big)
    return res[0], res[1], res[2], list(res[3 + nc:]), list(res[3:3 + nc])


def _ada_update(c_all, dmod_cols, dmod_all, ada_w, m_w, v_w, ada_b, m_b, v_b):
    cols = ada_w.shape[2]
    nb = ada_b.shape[1]

    def body(c_ref, dmc_ref, dma_ref, w_ref, mw_ref, vw_ref, b_ref, mb_ref, vb_ref,
             gw_ref, dw_ref, nmw_ref, nvw_ref, gb_ref, db_ref, nmb_ref, nvb_ref):
        cv = c_ref[...]
        cond = cv * _sigmoid(cv)
        for l in range(2):
            g = _dot_tn(cond, dmc_ref[l])
            gw_ref[l] = g
            dlt, m2, v2 = _adam(w_ref[l], g, mw_ref[l], vw_ref[l])
            dw_ref[l] = dlt
            nmw_ref[l] = m2
            nvw_ref[l] = v2
        gb = dma_ref[0]
        for i in range(1, N_DEV):
            gb = gb + dma_ref[i]
        gb_ref[...] = gb
        dlt, m2, v2 = _adam(b_ref[...], gb, mb_ref[...], vb_ref[...])
        db_ref[...] = dlt
        nmb_ref[...] = m2
        nvb_ref[...] = v2

    wspec = _full((2, D, cols))
    bspec = _full((2, nb))
    wshape = jax.ShapeDtypeStruct((2, D, cols), F32)
    bshape = jax.ShapeDtypeStruct((2, nb), F32)
    return _pallas(
        body, name="ada_update", grid=(1,),
        in_specs=[_full((N_DEV, D)), _full((2, N_DEV, cols)), _full((N_DEV, 2, nb)),
                  wspec, wspec, wspec, bspec, bspec, bspec],
        out_specs=[wspec] * 4 + [bspec] * 4,
        out_shape=[wshape] * 4 + [bshape] * 4,
        compiler_params=_params(("arbitrary",)),
    )(c_all, dmod_cols, dmod_all, ada_w, m_w, v_w, ada_b, m_b, v_b)


def _ev_in(x, mod, w_in, rc, rs1, rs2, comm=None):
    T = x.shape[0]

    def body(x_ref, mod_ref, w_ref, c_ref, s1_ref, s2_ref, q_ref, kv_ref, su_ref, sv_ref, g_ref):
        h = x_ref[...] * (1.0 + mod_ref[1:2, :]) + mod_ref[0:1, :]
        p = _dot_nt(h, w_ref[...])
        c, s1, s2 = c_ref[...], s1_ref[...], s2_ref[...]
        for j in range(ATTN_W // LANE):
            qr = _rope_fwd(p[:, j * LANE:(j + 1) * LANE], c, s1, s2)
            q_ref[:, j * LANE:(j + 1) * LANE] = (qr * (HEAD_DIM ** -0.5)).astype(BF16)
        low = lax.broadcasted_iota(jnp.int32, (TMF, LANE), 1) < HEAD_DIM
        for j, val in enumerate((_rope_fwd(p[:, 512:640], c, s1, s2), p[:, 640:768])):
            swapped = pltpu.roll(val, HEAD_DIM, 1)
            tiles = (jnp.where(low, val, 0.0), jnp.where(low, 0.0, swapped),
                     jnp.where(low, swapped, 0.0), jnp.where(low, 0.0, val))
            for k, tile in enumerate(tiles):
                kv_ref[:, (4 * j + k) * LANE:(4 * j + k + 1) * LANE] = tile.astype(BF16)
        su_ref[...] = p[:, 768:1280].astype(BF16)
        sv_ref[...] = p[:, 1280:1792].astype(BF16)
        g_ref[...] = p[:, 1792:2816].astype(BF16)

    sh = lambda w: jax.ShapeDtypeStruct((T, w), BF16)
    return _fused_call(
        body, comm, (x, mod, w_in, rc, rs1, rs2), name="ev_in", grid=(T // TMF,),
        in_specs=[_tile(TMF, D), _full((3, D)), _full((EV_IN, D)), _tile(TMF, LANE), _tile(TMF, LANE),
                  _tile(TMF, LANE)],
        out_specs=[_tile(TMF, ATTN_W), _tile(TMF, KVX_W), _tile(TMF, SG_W), _tile(TMF, SG_W), _tile(TMF, D)],
        out_shape=[sh(ATTN_W), sh(KVX_W), sh(SG_W), sh(SG_W), sh(D)], semantics=("parallel",))


def _band_specs(width, nb):
    return [pl.BlockSpec((BLK, width), lambda n: (jnp.maximum(n - 1, 0), 0)),
            pl.BlockSpec((BLK, width), lambda n: (n, 0)),
            pl.BlockSpec((BLK, width), lambda n: (jnp.minimum(n + 1, nb - 1), 0))]


def _band_bias(bias_ref, n, nb):
    rows = lax.broadcasted_iota(jnp.int32, (3 * BLK, 1), 0)
    outside = ((rows < BLK) & (n == 0)) | ((rows >= 2 * BLK) & (n == nb - 1))
    return bias_ref[...] + jnp.where(outside, NEG_INF, 0.0)


def _lane_tile(ref, t):
    return ref[:, t * LANE:(t + 1) * LANE]


def _split_bf16(v):
    hi = v.astype(BF16)
    return hi, (v - hi.astype(F32)).astype(BF16)


def _group_mean(v, a_ref, exact_bf16=False):
    hi, lo = _split_bf16(v)
    a = a_ref[...]
    out = []
    for t in range(SG_W // (2 * LANE)):
        sl = slice(t * 2 * LANE, (t + 1) * 2 * LANE)
        r = jnp.dot(hi[:, sl], a, preferred_element_type=F32)
        if not exact_bf16:
            r = r + jnp.dot(lo[:, sl], a, preferred_element_type=F32)
        out.append(r)
    return jnp.concatenate(out, axis=-1)


def _sg_core(sv_ref, lng, lnb, a_ref, w_ref, bfull_ref):
    svf = sv_ref[...].astype(F32)
    xc = svf - _group_mean(svf, a_ref, exact_bf16=True)
    rstd = lax.rsqrt(_group_mean(xc * xc, a_ref) + LN_EPS)
    xhat = xc * rstd
    vb = (xhat * lng + lnb).astype(BF16)
    low = lax.broadcasted_iota(jnp.int32, (BLK, LANE), 1) < SG_DIM
    tiles = []
    for t in range(SG_W // LANE):
        v2 = vb[:, t * LANE:(t + 1) * LANE]
        r0 = jnp.dot(w_ref[2 * t], v2, preferred_element_type=F32)
        r1 = jnp.dot(w_ref[2 * t + 1], v2, preferred_element_type=F32)
        tiles.append(jnp.where(low, r0, r1))
    svm = jnp.concatenate(tiles, axis=-1) + bfull_ref[...]
    return xhat, rstd, vb, svm


def _mix0_fwd(q, kvx, su, sv, g0, sink_l, bias, a128, sg_lng, sg_lnb, sg_w, sg_bfull, comm=None):
    T = q.shape[0]
    nb = T // BLK

    def body(q_ref, kp_ref, kc_ref, kn_ref, su_ref, sv_ref, g_ref, sink_ref, bias_ref, a_ref, lng_ref, lnb_ref,
             w_ref, bfull_ref, ycat_ref, y0_ref, lse_ref):
        n = pl.program_id(0)
        bias = _band_bias(bias_ref, n, nb)
        kvx = jnp.concatenate([kp_ref[...], kc_ref[...], kn_ref[...]], axis=0)
        tiles = []
        for t in range(ATTN_W // LANE):
            qt = _lane_tile(q_ref, t)
            acc = None
            for par in range(2):
                h = 2 * t + par
                kt = 2 * (h // 4) + par
                ke = kvx[:, kt * LANE:(kt + 1) * LANE]
                ve = kvx[:, (4 + kt) * LANE:(5 + kt) * LANE]
                st = _dot_nt(ke, qt) + bias
                sk = _lane_tile(sink_ref, h)
                m = jnp.maximum(jnp.max(st, axis=0, keepdims=True), sk)
                p = jnp.exp(st - m)
                denom = jnp.sum(p, axis=0, keepdims=True) + jnp.exp(sk - m)
                contrib = _dot_tn(p * (1.0 / denom), ve)
                acc = contrib if acc is None else acc + contrib
                lse_ref[0, :, h * LANE:(h + 1) * LANE] = m + jnp.log(denom)
            tiles.append(acc)
        _, _, _, svm = _sg_core(sv_ref, lng_ref[...], lnb_ref[...], a_ref, w_ref, bfull_ref)
        tiles.append(su_ref[...].astype(F32) * svm)
        ycat = jnp.concatenate(tiles, axis=-1)
        gf = g_ref[...].astype(F32)
        ycat_ref[...] = ycat.astype(BF16)
        y0_ref[...] = (ycat * (gf * _sigmoid(gf))).astype(BF16)

    return _fused_call(
        body, comm, (q, kvx, kvx, kvx, su, sv, g0, sink_l, bias, a128, sg_lng, sg_lnb, sg_w, sg_bfull),
        name="mix0_fwd", grid=(nb,),
        in_specs=[_tile(BLK, ATTN_W)] + _band_specs(KVX_W, nb) + [
            _tile(BLK, SG_W), _tile(BLK, SG_W), _tile(BLK, D), _full((1, N_HEADS * LANE)), _full((3 * BLK, LANE)),
            _full((2 * LANE, 2 * LANE)),_full((1, SG_W)), _full((1, SG_W)), _full((SG_GROUPS, BLK, BLK)),
            _full((BLK, SG_W))],
        out_specs=[_tile(BLK, D), _tile(BLK, D), pl.BlockSpec((1, 1, N_HEADS * LANE), lambda n: (n, 0, 0))],
        out_shape=[jax.ShapeDtypeStruct((T, D), BF16), jax.ShapeDtypeStruct((T, D), BF16),
                   jax.ShapeDtypeStruct((nb, 1, N_HEADS * LANE), F32)], semantics=("parallel",))


def _ev_out(y0, w_out, x, mod, lnp):
    T = x.shape[0]

    def body(y_ref, w_ref, x_ref, mod_ref, ln_ref, out_ref, z_ref, x1_ref):
        out = _dot(y_ref[...], w_ref[...])
        z = ALPHA * x_ref[...] + mod_ref[2:3, :] * out
        x1, _, _ = _ln_fwd(z, ln_ref[0:1, :], ln_ref[1:2, :])
        out_ref[...] = out.astype(BF16)
        z_ref[...] = z
        x1_ref[...] = x1

    return _pallas(
        body, name="ev_out", grid=(T // TMF,),
        in_specs=[_tile(TMF, D), _full((D, D)), _tile(TMF, D), _full((3, D)), _full((2, D))],
        out_specs=[_tile(TMF, D)] * 3,
        out_shape=[jax.ShapeDtypeStruct((T, D), BF16), jax.ShapeDtypeStruct((T, D), F32),
                   jax.ShapeDtypeStruct((T, D), F32)],
        compiler_params=_params(("parallel",)),
    )(y0, w_out, x, mod, lnp)


def _od_in(x1, mod, w_in):
    T = x1.shape[0]

    def body(x_ref, mod_ref, w_ref, xr_ref, g_ref):
        h = x_ref[...] * (1.0 + mod_ref[1:2, :]) + mod_ref[0:1, :]
        p = _dot(h, w_ref[...])
        xr_ref[...] = p[:, :D]
        g_ref[...] = p[:, D:].astype(BF16)

    return _pallas(
        body, name="od_in", grid=(T // TMF,),
        in_specs=[_tile(TMF, D), _full((3, D)), _full((D, OD_IN))],
        out_specs=[_tile(TMF, D), _tile(TMF, D)],
        out_shape=[jax.ShapeDtypeStruct((T, D), F32), jax.ShapeDtypeStruct((T, D), BF16)],
        compiler_params=_params(("parallel",)),
    )(x1, mod, w_in)


def _ext_rows(prev_ref, cur, next_ref, j, n):
    prev = jnp.where(j > 0, prev_ref[...], 0.0)
    nxt = jnp.where(j < n - 1, next_ref[...], 0.0)
    return jnp.concatenate([prev, cur, nxt], axis=0)


def _shift_rows(ext, off, rows):
    total = ext.shape[0]
    if off == 0:
        return ext[SUBLANE:SUBLANE + rows, :]
    return pltpu.roll(ext, (-off) % total, 0)[SUBLANE:SUBLANE + rows, :]


def _conv_fwd(ext, cw, cb, rows):
    xc = cb
    for k in range(4):
        xc = xc + cw[k:k + 1, :] * _shift_rows(ext, k - 2, rows)
    return xc


def _gates(xc, wa_ref, wx_ref, ba, bx, lam):
    pr, pi = [], []
    for h in range(RNN_HEADS):
        xh = xc[:, h * RNN_HD:(h + 1) * RNN_HD].astype(BF16)
        pr.append(_dot(xh, wa_ref[h]))
        pi.append(_dot(xh, wx_ref[h]))
    r = _sigmoid(jnp.concatenate(pr, axis=-1) + ba)
    ig = _sigmoid(jnp.concatenate(pi, axis=-1) + bx)
    sp = jnp.maximum(-lam, 0.0) + jnp.log(1.0 + jnp.exp(-jnp.abs(lam)))
    neg_log_a = RG_C * r * sp
    a = jnp.exp(-neg_log_a)
    s2 = (1.0 + a * a) * jnp.tanh(neg_log_a)
    inv_s = lax.rsqrt(jnp.maximum(s2, 1e-30))
    return r, ig, sp, a, s2 * inv_s, inv_s


def _scan_tile(a_ref, b_ref, o_ref, carry_ref, rows, reverse):
    ridx = lax.broadcasted_iota(jnp.int32, (SUBLANE, D), 0)
    groups = rows // SUBLANE

    def group(gi, h):
        g = (groups - 1 - gi) if reverse else gi
        off = pl.multiple_of(g * SUBLANE, SUBLANE)
        a = a_ref[pl.ds(off, SUBLANE), :]
        b = b_ref[pl.ds(off, SUBLANE), :]
        for sh in (1, 2, 4):
            if reverse:
                keep = ridx < SUBLANE - sh
                a_p = jnp.where(keep, pltpu.roll(a, SUBLANE - sh, 0), 1.0)
                b_p = jnp.where(keep, pltpu.roll(b, SUBLANE - sh, 0), 0.0)
            else:
                keep = ridx >= sh
                a_p = jnp.where(keep, pltpu.roll(a, sh, 0), 1.0)
                b_p = jnp.where(keep, pltpu.roll(b, sh, 0), 0.0)
            b = b + a * b_p
            a = a * a_p
        hh = b + a * h
        o_ref[pl.ds(off, SUBLANE), :] = hh
        return hh[0:1, :] if reverse else hh[SUBLANE - 1:SUBLANE, :]

    carry_ref[...] = lax.fori_loop(0, groups, group, carry_ref[...])


def _rglru_fwd(xr, cw, cb, wa, wx, ba, bx, lam, reverse, name):
    T = xr.shape[0]
    n = T // TS
    prev_spec, next_spec = _halo_specs(TS, D, n, T, reverse)

    def body(prev_ref, cur_ref, next_ref, cw_ref, cb_ref, wa_ref, wx_ref, ba_ref, bx_ref, lam_ref,
             h_ref, a_ref, s_ref, r_ref, ig_ref, xc_ref, b_s, carry):
        i = pl.program_id(0)
        j = (n - 1 - i) if reverse else i

        @pl.when(i == 0)
        def _():
            carry[...] = jnp.zeros_like(carry)

        ext = _ext_rows(prev_ref, cur_ref[...], next_ref, j, n)
        xc = _conv_fwd(ext, cw_ref[...], cb_ref[...], TS)
        r, ig, _, a, s, _ = _gates(xc, wa_ref, wx_ref, ba_ref[...], bx_ref[...], lam_ref[...])
        s_ref[...] = s
        r_ref[...] = r.astype(BF16)
        ig_ref[...] = ig.astype(BF16)
        xc_ref[...] = xc.astype(BF16)
        a_ref[...] = a
        b_s[...] = s * ig * xc
        _scan_tile(a_ref, b_s, h_ref, carry, TS, reverse)

    wspec = _full((RNN_HEADS, RNN_HD, RNN_HD))
    cur = _rev_tile(TS, D, n, reverse)
    f32 = jax.ShapeDtypeStruct((T, D), F32)
    b16 = jax.ShapeDtypeStruct((T, D), BF16)
    return _pallas(
        body, name=name, grid=(n,),
        in_specs=[prev_spec, cur, next_spec, _full((4, D)), _full((1, D)),
                  wspec, wspec, _full((1, D)), _full((1, D)), _full((1, D))],
        out_specs=[cur] * 6,
        out_shape=[f32, f32, f32, b16, b16, b16],
        scratch_shapes=[pltpu.VMEM((TS, D), F32), pltpu.VMEM((1, D), F32)],
        compiler_params=_params(("arbitrary",)),
    )(xr, xr, xr, cw, cb, wa, wx, ba, bx, lam)


def _od_out(hf, hb, g1, w_out, x1, tgt, mod, lnp):
    T = x1.shape[0]

    def body(hf_ref, hb_ref, g_ref, w_ref, x_ref, t_ref, mod_ref, ln_ref,
             dh_ref, dg_ref, dx_ref, dwb_ref, vec_ref, dw_ref):
        i = pl.program_id(0)

        @pl.when(i == 0)
        def _():
            dw_ref[...] = jnp.zeros_like(dw_ref)
            vec_ref[...] = jnp.zeros_like(vec_ref)

        hs = hf_ref[...] + hb_ref[...]
        sg, dsg = _silu_and_grad(g_ref[...].astype(F32))
        yr = (hs * sg).astype(BF16)
        w = w_ref[...]
        out = _dot(yr, w)
        gate = mod_ref[2:3, :]
        z = ALPHA * x_ref[...] + gate * out
        lng = ln_ref[0:1, :]
        x2, xhat, rstd = _ln_fwd(z, lng, ln_ref[1:2, :])
        diff = x2 - t_ref[...]
        vec_ref[3:4, 0:LANE] += 0.5 * jnp.sum(diff * diff) * (1.0 / D)
        dx2 = diff * (1.0 / D)
        dz = _ln_bwd(dx2, xhat, rstd, lng)
        vec_ref[0:1, :] += _rowsum(dx2 * xhat)
        vec_ref[1:2, :] += _rowsum(dx2)
        vec_ref[2:3, :] += _rowsum(dz * out)
        dout = (dz * gate).astype(BF16)
        dyr = _dot_nt(dout, w)
        dw_ref[...] += _dot_tn(yr, dout)
        dh_ref[...] = dyr * sg
        dg_ref[...] = (dyr * hs * dsg).astype(BF16)
        dx_ref[...] = ALPHA * dz

        @pl.when(i == T // TM - 1)
        def _():
            dwb_ref[...] = dw_ref[...].astype(BF16)

    return _pallas(
        body, name="od_out", grid=(T // TM,),
        in_specs=[_tile(TM, D), _tile(TM, D), _tile(TM, D), _full((D, D)), _tile(TM, D), _tile(TM, D),
                  _full((3, D)), _full((2, D))],
        out_specs=[_tile(TM, D), _tile(TM, D), _tile(TM, D), _full((D, D)), _full((SUBLANE, D))],
        out_shape=[jax.ShapeDtypeStruct((T, D), F32), jax.ShapeDtypeStruct((T, D), BF16),
                   jax.ShapeDtypeStruct((T, D), F32), jax.ShapeDtypeStruct((D, D), BF16),
                   jax.ShapeDtypeStruct((SUBLANE, D), F32)],
        scratch_shapes=[pltpu.VMEM((D, D), F32)],
        compiler_params=_params(("arbitrary",)),
    )(hf, hb, g1, w_out, x1, tgt, mod, lnp)


def _rglru_bwd(fwd, dh, wa, wx, lam, reverse, name, comm=None):
    h, a_all, s_all, r_all, ig_all, xc_all = fwd
    T = h.shape[0]
    n = T // TS
    adj_rev = not reverse
    hprev_spec, hnext_spec = _halo_specs(TS, D, n, T, adj_rev)
    h_halo_spec = hnext_spec if reverse else hprev_spec

    def body(dh_ref, h_ref, hh_ref, a_ref, s_ref, r_ref, ig_ref, xc_ref, wa_ref, wx_ref, lam_ref,
             dxc_ref, dwa_ref, dwx_ref, vec_ref, a_s, l_s, carry, a_edge):
        i = pl.program_id(0)
        j = (n - 1 - i) if adj_rev else i

        @pl.when(i == 0)
        def _():
            carry[...] = jnp.zeros_like(carry)
            a_edge[...] = jnp.zeros_like(a_edge)
            dwa_ref[...] = jnp.zeros_like(dwa_ref)
            dwx_ref[...] = jnp.zeros_like(dwx_ref)
            vec_ref[...] = jnp.zeros_like(vec_ref)

        lam = lam_ref[...]
        sp = jnp.maximum(-lam, 0.0) + jnp.log(1.0 + jnp.exp(-jnp.abs(lam)))
        a, s = a_ref[...], s_ref[...]
        inv_s = lax.rsqrt(jnp.maximum(s * s, 1e-30))
        r, ig = r_ref[...].astype(F32), ig_ref[...].astype(F32)
        xcb = xc_ref[...]
        xc = xcb.astype(F32)

        rows = lax.broadcasted_iota(jnp.int32, (TS, D), 0)
        hcur = h_ref[...]
        if reverse:
            a_sh = jnp.where(rows == 0, a_edge[...], pltpu.roll(a, 1, 0))
            halo = jnp.where(j < n - 1, hh_ref[0:1, :], 0.0)
            h_nb = jnp.where(rows == TS - 1, halo, pltpu.roll(hcur, TS - 1, 0))
        else:
            a_sh = jnp.where(rows == TS - 1, a_edge[...], pltpu.roll(a, TS - 1, 0))
            halo = jnp.where(j > 0, hh_ref[SUBLANE - 1:SUBLANE, :], 0.0)
            h_nb = jnp.where(rows == 0, halo, pltpu.roll(hcur, 1, 0))
        a_s[...] = a_sh
        _scan_tile(a_s, dh_ref, l_s, carry, TS, adj_rev)
        a_edge[...] = a[TS - 1:TS, :] if reverse else a[0:1, :]

        lm = l_s[...]
        da = lm * h_nb
        di = lm * s * xc
        dxc = lm * s * ig
        ds = lm * ig * xc
        dlog_a = a * (da - ds * a * inv_s)
        dr = (-RG_C) * sp * dlog_a
        dsp = _rowsum((-RG_C) * r * dlog_a)
        dpr = dr * r * (1.0 - r)
        dpi = di * ig * (1.0 - ig)
        vec_ref[0:1, :] += _rowsum(dpr)
        vec_ref[1:2, :] += _rowsum(dpi)
        vec_ref[2:3, :] += dsp * (-_sigmoid(-lam))
        parts = []
        for hd in range(RNN_HEADS):
            sl = slice(hd * RNN_HD, (hd + 1) * RNN_HD)
            xh = xcb[:, sl]
            dprh = dpr[:, sl].astype(BF16)
            dpih = dpi[:, sl].astype(BF16)
            parts.append(_dot_nt(dprh, wa_ref[hd]) + _dot_nt(dpih, wx_ref[hd]))
            dwa_ref[hd] += _dot_tn(xh, dprh)
            dwx_ref[hd] += _dot_tn(xh, dpih)
        dxc_ref[...] = dxc + jnp.concatenate(parts, axis=-1)

    wspec = _full((RNN_HEADS, RNN_HD, RNN_HD))
    cur = _rev_tile(TS, D, n, adj_rev)
    return _fused_call(
        body, comm, (dh, h, h, a_all, s_all, r_all, ig_all, xc_all, wa, wx, lam), name=name, grid=(n,),
        in_specs=[cur, cur, h_halo_spec, cur, cur, cur, cur, cur, wspec, wspec, _full((1, D))],
        out_specs=[cur, wspec, wspec, _full((SUBLANE, D))],
        out_shape=[jax.ShapeDtypeStruct((T, D), F32),
                   jax.ShapeDtypeStruct((RNN_HEADS, RNN_HD, RNN_HD), F32),
                   jax.ShapeDtypeStruct((RNN_HEADS, RNN_HD, RNN_HD), F32),
                   jax.ShapeDtypeStruct((SUBLANE, D), F32)],
        scratch_shapes=[pltpu.VMEM((TS, D), F32)] * 2 + [pltpu.VMEM((1, D), F32)] * 2)


def _od_in_bwd(dxcf, dxcb, xr, dg1, x1, dx1p, mod, w_in, cw, comm=None):
    T = x1.shape[0]
    n = T // TM
    slab = OD_IN // N_DEV
    prev_spec, next_spec = _halo_specs(TM, D, n, T, False)

    def body(fp_ref, fc_ref, fn_ref, bp_ref, bc_ref, bn_ref, xr_ref, dg_ref, x1_ref, dxp_ref,
             mod_ref, w_ref, cw_ref, dx_ref, dwb_ref, vec_ref, dw_ref):
        i = pl.program_id(0)

        @pl.when(i == 0)
        def _():
            dw_ref[...] = jnp.zeros_like(dw_ref)
            vec_ref[...] = jnp.zeros_like(vec_ref)

        dcur = fc_ref[...] + bc_ref[...]
        dprev = jnp.where(i > 0, fp_ref[...] + bp_ref[...], 0.0)
        dnext = jnp.where(i < n - 1, fn_ref[...] + bn_ref[...], 0.0)
        dext = jnp.concatenate([dprev, dcur, dnext], axis=0)
        xr_v = xr_ref[...]
        cw_v = cw_ref[...]
        dxr = None
        for k in range(4):
            shifted = _shift_rows(dext, 2 - k, TM)
            term = cw_v[k:k + 1, :] * shifted
            dxr = term if dxr is None else dxr + term
            vec_ref[k:k + 1, :] += _rowsum(shifted * xr_v)
        vec_ref[4:5, :] += _rowsum(dcur)
        dp = jnp.concatenate([dxr.astype(BF16), dg_ref[...]], axis=-1)
        x1v = x1_ref[...]
        scale1 = 1.0 + mod_ref[1:2, :]
        h1 = (x1v * scale1 + mod_ref[0:1, :]).astype(BF16)
        dh1 = _dot_nt(dp, w_ref[...])
        dw_ref[...] += _dot_tn(h1, dp)
        dx_ref[...] = dxp_ref[...] + dh1 * scale1
        vec_ref[5:6, :] += _rowsum(dh1)
        vec_ref[6:7, :] += _rowsum(dh1 * x1v)

        @pl.when(i == n - 1)
        def _():
            for j in range(N_DEV):
                dwb_ref[j] = dw_ref[:, j * slab:(j + 1) * slab].astype(BF16)

    t = _tile(TM, D)
    return _fused_call(
        body, comm, (dxcf, dxcf, dxcf, dxcb, dxcb, dxcb, xr, dg1, x1, dx1p, mod, w_in, cw),
        name="od_in_bwd", grid=(n,),
        in_specs=[prev_spec, t, next_spec, prev_spec, t, next_spec, t, t, t, t,
                  _full((3, D)), _full((D, OD_IN)), _full((4, D))],
        out_specs=[t, _full((N_DEV, D, slab)), _full((SUBLANE, D))],
        out_shape=[jax.ShapeDtypeStruct((T, D), F32), jax.ShapeDtypeStruct((N_DEV, D, slab), BF16),
                   jax.ShapeDtypeStruct((SUBLANE, D), F32)],
        scratch_shapes=[pltpu.VMEM((D, OD_IN), F32)])


def _ev_out_bwd(dx1, z0, out0, y0, ycat, g0, w_out, mod, lnp):
    T = dx1.shape[0]

    def body(dx_ref, z_ref, out_ref, y0_ref, yc_ref, g_ref, w_ref, mod_ref, ln_ref,
             dxp_ref, dyc_ref, dg_ref, dwb_ref, vec_ref, dw_ref):
        i = pl.program_id(0)

        @pl.when(i == 0)
        def _():
            dw_ref[...] = jnp.zeros_like(dw_ref)
            vec_ref[...] = jnp.zeros_like(vec_ref)

        lng = ln_ref[0:1, :]
        _, xhat, rstd = _ln_fwd(z_ref[...], lng, ln_ref[1:2, :])
        dy = dx_ref[...]
        dz = _ln_bwd(dy, xhat, rstd, lng)
        vec_ref[0:1, :] += _rowsum(dy * xhat)
        vec_ref[1:2, :] += _rowsum(dy)
        vec_ref[2:3, :] += _rowsum(dz * out_ref[...].astype(F32))
        dout = (dz * mod_ref[2:3, :]).astype(BF16)
        dy0 = _dot_nt(dout, w_ref[...])
        dw_ref[...] += _dot_tn(y0_ref[...], dout)
        sg, dsg = _silu_and_grad(g_ref[...].astype(F32))
        dyc_ref[...] = (dy0 * sg).astype(BF16)
        dg_ref[...] = (dy0 * yc_ref[...].astype(F32) * dsg).astype(BF16)
        dxp_ref[...] = ALPHA * dz

        @pl.when(i == T // TM - 1)
        def _():
            dwb_ref[...] = dw_ref[...].astype(BF16)

    t = _tile(TM, D)
    return _pallas(
        body, name="ev_out_bwd", grid=(T // TM,),
        in_specs=[t, t, t, t, t, t, _full((D, D)), _full((3, D)), _full((2, D))],
        out_specs=[t, t, t, _full((D, D)), _full((SUBLANE, D))],
        out_shape=[jax.ShapeDtypeStruct((T, D), F32), jax.ShapeDtypeStruct((T, D), BF16),
                   jax.ShapeDtypeStruct((T, D), BF16), jax.ShapeDtypeStruct((D, D), BF16),
                   jax.ShapeDtypeStruct((SUBLANE, D), F32)],
        scratch_shapes=[pltpu.VMEM((D, D), F32)],
        compiler_params=_params(("arbitrary",)),
    )(dx1, z0, out0, y0, ycat, g0, w_out, mod, lnp)


def _mix0_bwd(q, kvx, lse, dyc, ycat, su, sv, sink_l, bias, a128, gsum, sel, sg_lng, sg_lnb, sg_w, sg_bfull,
              rc, rs1, rs2, comm=None):
    T = q.shape[0]
    nb = T // BLK

    def body(q_ref, kp_ref, kc_ref, kn_ref, lse_ref, dyc_ref, yc_ref, su_ref, sv_ref, sink_ref, bias_ref, a_ref,
             gsum_ref, sel_ref, lng_ref, lnb_ref, w_ref, bfull_ref, c_ref, s1_ref, s2_ref,
             dq_ref, dkv_ref, dsu_ref, dsv_ref, dw_ref, dbt_ref, vec_ref, dsink_ref):
        n = pl.program_id(0)

        @pl.when(n == 0)
        def _():
            dkv_ref[...] = jnp.zeros_like(dkv_ref)
            dw_ref[...] = jnp.zeros_like(dw_ref)
            dbt_ref[...] = jnp.zeros_like(dbt_ref)
            vec_ref[...] = jnp.zeros_like(vec_ref)
            dsink_ref[...] = jnp.zeros_like(dsink_ref)

        band = pl.ds(pl.multiple_of(n * BLK + (TM - BLK), BLK), 3 * BLK)
        bias = _band_bias(bias_ref, n, nb)
        kvx = jnp.concatenate([kp_ref[...], kc_ref[...], kn_ref[...]], axis=0)
        bias2 = jnp.concatenate([bias, bias], axis=1)
        low = lax.broadcasted_iota(jnp.int32, (BLK, LANE), 1) < HEAD_DIM
        low2 = lax.broadcasted_iota(jnp.int32, (2 * BLK, LANE), 1) < HEAD_DIM
        sel = sel_ref[...]
        c, s1, s2 = c_ref[...], s1_ref[...], s2_ref[...]
        for kvh in range(2):
            t0, t1 = 2 * kvh, 2 * kvh + 1
            q2 = jnp.concatenate([_lane_tile(q_ref, t0), _lane_tile(q_ref, t1)], axis=0)
            do2 = jnp.concatenate([_lane_tile(dyc_ref, t0), _lane_tile(dyc_ref, t1)], axis=0)
            yc2 = jnp.concatenate([_lane_tile(yc_ref, t0), _lane_tile(yc_ref, t1)], axis=0)
            p_hi, p_lo = _split_bf16(do2.astype(F32) * yc2.astype(F32))
            deltas = _dot_nt(sel, p_hi) + _dot_nt(sel, p_lo)
            dkx = jnp.zeros((3 * BLK, LANE), F32)
            dvx = jnp.zeros((3 * BLK, LANE), F32)
            dq_acc = None
            for par in range(2):
                heads = (4 * kvh + par, 4 * kvh + 2 + par)
                kt = 2 * kvh + par
                ke = kvx[:, kt * LANE:(kt + 1) * LANE]
                ve = kvx[:, (4 + kt) * LANE:(5 + kt) * LANE]
                lse = jnp.concatenate([lse_ref[0, :, h * LANE:(h + 1) * LANE] for h in heads], axis=1)
                sk = jnp.concatenate([_lane_tile(sink_ref, h) for h in heads], axis=1)
                delta = deltas[par:par + 1, :]
                pt = jnp.exp(_dot_nt(ke, q2) + bias2 - lse)
                dst = (pt * (_dot_nt(ve, do2) - delta)).astype(BF16)
                sink_terms = jnp.exp(sk - lse) * delta
                for k, h in enumerate(heads):
                    dsink_ref[:, h * LANE:(h + 1) * LANE] += sink_terms[:, k * LANE:(k + 1) * LANE]
                part = _dot_tn(dst, ke)
                dq_acc = part if dq_acc is None else dq_acc + part
                mine = low2 if par == 0 else jnp.logical_not(low2)
                dkx = dkx + jnp.dot(dst, jnp.where(mine, q2, jnp.zeros_like(q2)), preferred_element_type=F32)
                dvx = dvx + jnp.dot(pt.astype(BF16), jnp.where(mine, do2, jnp.zeros_like(do2)),
                                    preferred_element_type=F32)
            for k, t in enumerate((t0, t1)):
                dq_t = dq_acc[k * BLK:(k + 1) * BLK] * (HEAD_DIM ** -0.5)
                dq_ref[:, t * LANE:(t + 1) * LANE] = _rope_bwd(dq_t, c, s1, s2).astype(BF16)
            dkv_ref[band, kvh * LANE:(kvh + 1) * LANE] += dkx
            dkv_ref[band, (2 + kvh) * LANE:(3 + kvh) * LANE] += dvx

        lng = lng_ref[...]
        xhat, rstd, vb, svm = _sg_core(sv_ref, lng, lnb_ref[...], a_ref, w_ref, bfull_ref)
        dy = dyc_ref[:, ATTN_W:].astype(F32)
        dsu_ref[...] = (dy * svm).astype(BF16)
        dsvm = dy * su_ref[...].astype(F32)
        d_hi, d_lo = _split_bf16(dsvm)
        gsum = gsum_ref[...]
        dbt_ref[...] += jnp.dot(d_hi, gsum, preferred_element_type=F32) + jnp.dot(d_lo, gsum,
                                                                                 preferred_element_type=F32)
        tiles = []
        for t in range(SG_W // LANE):
            tl = slice(t * LANE, (t + 1) * LANE)
            dt, v2 = d_hi[:, tl], vb[:, tl]
            dw_ref[2 * t] += _dot_nt(jnp.where(low, dt, jnp.zeros_like(dt)), v2)
            dw_ref[2 * t + 1] += _dot_nt(jnp.where(low, jnp.zeros_like(dt), dt), v2)
            tiles.append(jnp.where(low, _dot_tn(w_ref[2 * t], dt), _dot_tn(w_ref[2 * t + 1], dt)))
        dvgn = jnp.concatenate(tiles, axis=-1)
        vec_ref[0:1, :] += _rowsum(dvgn * xhat)
        vec_ref[1:2, :] += _rowsum(dvgn)
        dxh = dvgn * lng
        m1 = _group_mean(dxh, a_ref)
        m2 = _group_mean(dxh * xhat, a_ref)
        dsv_ref[...] = (rstd * (dxh - m1 - xhat * m2)).astype(BF16)

    return _fused_call(
        body, comm, (q, kvx, kvx, kvx, lse, dyc, ycat, su, sv, sink_l, bias, a128, gsum, sel, sg_lng, sg_lnb, sg_w,
                     sg_bfull, rc, rs1, rs2),
        name="mix0_bwd", grid=(nb,),
        in_specs=[_tile(BLK, ATTN_W)] + _band_specs(KVX_W, nb) + [
            pl.BlockSpec((1, 1, N_HEADS * LANE), lambda n: (n, 0, 0)), _tile(BLK, D), _tile(BLK, D),
            _tile(BLK, SG_W), _tile(BLK, SG_W), _full((1, N_HEADS * LANE)), _full((3 * BLK, LANE)),
            _full((2 * LANE, 2 * LANE)),_full((SG_W, LANE)), _full((SUBLANE, LANE)), _full((1, SG_W)), _full((1, SG_W)),
            _full((SG_GROUPS, BLK, BLK)), _full((BLK, SG_W)), _tile(BLK, LANE), _tile(BLK, LANE), _tile(BLK, LANE)],
        out_specs=[_tile(BLK, ATTN_W), _full((T + 2 * TM, 4 * LANE)), _tile(BLK, SG_W), _tile(BLK, SG_W),
                   _full((SG_GROUPS, BLK, BLK)), _full((BLK, LANE)), _full((SUBLANE, SG_W)),
                   _full((1, N_HEADS * LANE))],
        out_shape=[jax.ShapeDtypeStruct((T, ATTN_W), BF16), jax.ShapeDtypeStruct((T + 2 * TM, 4 * LANE), F32),
                   jax.ShapeDtypeStruct((T, SG_W), BF16), jax.ShapeDtypeStruct((T, SG_W), BF16),
                   jax.ShapeDtypeStruct((SG_GROUPS, BLK, BLK), F32), jax.ShapeDtypeStruct((BLK, LANE), F32),
                   jax.ShapeDtypeStruct((SUBLANE, SG_W), F32), jax.ShapeDtypeStruct((1, N_HEADS * LANE), F32)])


def _ev_in_bwd(dq, dkv, dsu, dsv, dg0, x, dxp, mod, w_in, rc, rs1, rs2, comm=None):
    T = x.shape[0]

    def body(dq_ref, dkv_ref, dsu_ref, dsv_ref, dg_ref, x_ref, dxp_ref, mod_ref, w_ref, c_ref, s1_ref, s2_ref,
             dx_ref, dwb_ref, vec_ref, dw_ref):
        i = pl.program_id(0)

        @pl.when(i == 0)
        def _():
            dw_ref[...] = jnp.zeros_like(dw_ref)
            vec_ref[...] = jnp.zeros_like(vec_ref)

        low = lax.broadcasted_iota(jnp.int32, (TM, LANE), 1) < HEAD_DIM

        def fold(j):
            t0 = dkv_ref[:, (2 * j) * LANE:(2 * j + 1) * LANE]
            t1 = dkv_ref[:, (2 * j + 1) * LANE:(2 * j + 2) * LANE]
            return jnp.where(low, t0 + pltpu.roll(t0, HEAD_DIM, 1), t1 + pltpu.roll(t1, HEAD_DIM, 1))

        dk = _rope_bwd(fold(0), c_ref[...], s1_ref[...], s2_ref[...]).astype(BF16)
        dp = jnp.concatenate([dq_ref[...], dk, fold(1).astype(BF16), dsu_ref[...], dsv_ref[...],
                              dg_ref[...]], axis=-1)
        xv = x_ref[...]
        scale0 = 1.0 + mod_ref[1:2, :]
        h0 = (xv * scale0 + mod_ref[0:1, :]).astype(BF16)
        dh0 = _dot(dp, w_ref[...])
        dw_ref[...] += _dot_tn(dp, h0)
        dx_ref[...] = dxp_ref[...] + dh0 * scale0
        vec_ref[0:1, :] += _rowsum(dh0)
        vec_ref[1:2, :] += _rowsum(dh0 * xv)

        @pl.when(i == T // TM - 1)
        def _():
            dwb_ref[...] = dw_ref[...].astype(BF16)

    t = _tile(TM, D)
    return _fused_call(
        body, comm, (dq, dkv, dsu, dsv, dg0, x, dxp, mod, w_in, rc, rs1, rs2), name="ev_in_bwd", grid=(T // TM,),
        in_specs=[_tile(TM, ATTN_W), pl.BlockSpec((TM, 4 * LANE), lambda i: (i + 1, 0)), _tile(TM, SG_W),
                  _tile(TM, SG_W), t, t, t,
                  _full((3, D)), _full((EV_IN, D)), _tile(TM, LANE), _tile(TM, LANE), _tile(TM, LANE)],
        out_specs=[t, _full((EV_IN, D)), _full((SUBLANE, D))],
        out_shape=[jax.ShapeDtypeStruct((T, D), F32), jax.ShapeDtypeStruct((EV_IN, D), BF16),
                   jax.ShapeDtypeStruct((SUBLANE, D), F32)],
        scratch_shapes=[pltpu.VMEM((EV_IN, D), F32)])


def _sum_slots(land_ref):
    g = land_ref[0].astype(F32)
    for i in range(1, land_ref.shape[0]):
        g = g + land_ref[i].astype(F32)
    return g


def _reduce_adam(items, name):
    R, C = items[0][1].shape
    rb = R
    if R > 512:
        for cand in (512, 256, 128, 64, 32, 16, 8):
            if R % cand == 0:
                rb = cand
                break
    n = len(items)

    def body(*refs):
        for k in range(n):
            l_ref, w_ref, m_ref, v_ref = refs[4 * k:4 * k + 4]
            g_ref, d_ref, nm_ref, nv_ref = refs[4 * n + 4 * k:4 * n + 4 * k + 4]
            g = _sum_slots(l_ref)
            g_ref[...] = g
            dlt, m2, v2 = _adam(w_ref[...], g, m_ref[...], v_ref[...])
            d_ref[...] = dlt
            nm_ref[...] = m2
            nv_ref[...] = v2

    t = pl.BlockSpec((rb, C), lambda i: (i, 0))
    shp = jax.ShapeDtypeStruct((R, C), F32)
    in_specs, operands = [], []
    for land, w, m, v in items:
        in_specs += [pl.BlockSpec((land.shape[0], rb, C), lambda i: (0, i, 0)), t, t, t]
        operands += [land, w, m, v]
    res = _pallas(
        body, name=name, grid=(R // rb,),
        in_specs=in_specs, out_specs=[t] * (4 * n), out_shape=[shp] * (4 * n),
        compiler_params=_params(("parallel",)),
    )(*operands)
    return [list(res[4 * k:4 * k + 4]) for k in range(n)]


def _tail_exchange(slabs, small):
    _, R, C = slabs.shape
    n_chips = N_DEV // 2
    gather = _GatherComm(small)
    ns = gather.n

    def body(*refs):
        slab_ref = refs[0]
        g_ins = refs[1:1 + ns]
        land_ref = refs[1 + ns]
        g_outs = refs[2 + ns:2 + 2 * ns]
        stage, part, s1_send, s1_recv, s2_send, s2_recv = refs[2 + 2 * ns:8 + 2 * ns]
        g_sems = refs[8 + 2 * ns:]
        x, y, c = _my_pos()
        chip = 2 * x + y
        gather.start(g_ins, g_outs, g_sems)

        swaps = [pltpu.make_async_remote_copy(
            src_ref=slab_ref.at[2 * k + (1 - c)], dst_ref=stage.at[k], send_sem=s1_send.at[k],
            recv_sem=s1_recv.at[k], device_id=(x, y, 1 - c), device_id_type=MESH) for k in range(n_chips)]
        for cp in swaps:
            cp.start()
        for cp in swaps:
            cp.wait()
        for k in range(n_chips):
            part[k] = (slab_ref[2 * k + c].astype(F32) + stage[k].astype(F32)).astype(BF16)

        gather.mid(g_ins, g_outs, g_sems)

        sends = []
        for r in range(1, n_chips):
            px = (1 - x) if (r & 2) else x
            py = (1 - y) if (r & 1) else y
            sends.append(pltpu.make_async_remote_copy(
                src_ref=part.at[2 * px + py], dst_ref=land_ref.at[chip], send_sem=s2_send.at[r - 1],
                recv_sem=s2_recv.at[r - 1], device_id=(px, py, c), device_id_type=MESH))
        for cp in sends:
            cp.start()
        land_ref[chip] = part[chip]
        for cp in sends:
            cp.wait()
        gather.finish(g_ins, g_outs, g_sems)

    any_spec = pl.BlockSpec(memory_space=pl.ANY)
    vmem_spec = pl.BlockSpec(memory_space=pltpu.VMEM)
    res = _pallas(
        body, name="tail_exchange",
        out_shape=[jax.ShapeDtypeStruct((n_chips, R, C), BF16)] + gather.out_shapes(),
        in_specs=[vmem_spec] + [any_spec] * ns, out_specs=[vmem_spec] + [any_spec] * ns,
        scratch_shapes=[pltpu.VMEM((n_chips, R, C), BF16), pltpu.VMEM((n_chips, R, C), BF16),
                        pltpu.SemaphoreType.DMA((n_chips,)), pltpu.SemaphoreType.DMA((n_chips,)),
                        pltpu.SemaphoreType.DMA((n_chips - 1,)), pltpu.SemaphoreType.DMA((n_chips - 1,))]
        + gather.sems(),
        compiler_params=pltpu.CompilerParams(vmem_limit_bytes=VMEM_LIMIT),
    )(slabs, *gather.arrs)
    return res[0], list(res[1:])


def _slots_adam(items, name):
    zeros3 = (0, 0, 0)
    in_specs, out_specs, out_shape, operands = [], [], [], []
    for land, w, m, v in items:
        inner = w.shape[-3:]
        if w.ndim == 5:
            lspec = pl.BlockSpec((N_DEV, 1) + inner, lambda i: (0, i) + zeros3)
            wspec = pl.BlockSpec((1, 1) + inner, lambda i: (0, i) + zeros3)
        else:
            lspec = pl.BlockSpec((N_DEV,) + inner, lambda i: (0,) + zeros3)
            wspec = pl.BlockSpec((1,) + inner, lambda i: (0,) + zeros3)
        in_specs += [lspec, wspec, wspec, wspec]
        out_specs += [wspec] * 4
        out_shape += [jax.ShapeDtypeStruct(w.shape, F32)] * 4
        operands += [land, w, m, v]
    n = len(items)

    def body(*refs):
        for k, (_, w, _, _) in enumerate(items):
            l_ref, w_ref, m_ref, v_ref = refs[4 * k:4 * k + 4]
            outs = refs[4 * n + 4 * k:4 * n + 4 * k + 4]
            at = (0, 0) if w.ndim == 5 else (0,)

            def update(l_ref=l_ref, w_ref=w_ref, m_ref=m_ref, v_ref=v_ref, outs=outs, at=at):
                g = l_ref[(0,) + at[1:]].astype(F32)
                for i in range(1, N_DEV):
                    g = g + l_ref[(i,) + at[1:]].astype(F32)
                dlt, m2, v2 = _adam(w_ref[at], g, m_ref[at], v_ref[at])
                for o_ref, val in zip(outs, (g, dlt, m2, v2)):
                    o_ref[at] = val

            if w.ndim == 5:
                update()
            else:
                pl.when(pl.program_id(0) == 0)(update)

    res = _pallas(
        body, name=name, grid=(2,),
        in_specs=in_specs, out_specs=out_specs, out_shape=out_shape,
        compiler_params=_params(("arbitrary",)),
    )(*operands)
    return [list(res[4 * k:4 * k + 4]) for k in range(n)]


SMALL_PARAMS = ("ln_g", "ln_b", "ev_sg_ln_g", "ev_sg_ln_b", "ev_sink", "ev_sg_b",
                "od_conv_w", "od_conv_b", "od_b_a", "od_b_x", "od_lam")


def _small_update(ga, gc, gd, gf, gb, ge, gsink, gbt, params):
    names = list(SMALL_PARAMS)
    flat = [a for nm in names for a in params[nm]]
    n_g = 8

    def body(*refs):
        ga_ref, gc_ref, gd_ref, gf_ref, gb_ref, ge_ref, gs_ref, gbt_ref = refs[:n_g]
        prm = refs[n_g:n_g + 3 * len(names)]
        loss_ref = refs[n_g + 3 * len(names)]
        outs = refs[n_g + 3 * len(names) + 1:]

        def ssum(ref):
            acc = ref[0]
            for i in range(1, N_DEV):
                acc = acc + ref[i]
            return acc

        a, cc, dd, ff, bb, ee = ssum(ga_ref), ssum(gc_ref), ssum(gd_ref), ssum(gf_ref), ssum(gb_ref), ssum(ge_ref)
        loss_ref[...] = a[3:4, 0:LANE]
        me = _slot(*_my_pos())

        def mine(rows):
            acc = jnp.zeros((rows.shape[0], LANE), F32)
            for j in range(N_DEV):
                acc = acc + jnp.where(me == j, rows[:, j * LANE:(j + 1) * LANE], 0.0)
            return acc

        sink_terms = ssum(gs_ref)
        lane8 = lax.broadcasted_iota(jnp.int32, (1, N_HEADS), 1)
        g_sink = jnp.zeros((1, N_HEADS), F32)
        for h in range(N_HEADS):
            tot = -jnp.sum(sink_terms[:, h * LANE:(h + 1) * LANE], axis=1, keepdims=True)
            g_sink = jnp.where(lane8 == h, tot, g_sink)
        grads = dict(
            ln_g=jnp.concatenate([dd[0:1], a[0:1]], axis=0), ln_b=jnp.concatenate([dd[1:2], a[1:2]], axis=0),
            ev_sg_ln_g=ee[0:1], ev_sg_ln_b=ee[1:2], ev_sink=g_sink,
            ev_sg_b=jnp.transpose(ssum(gbt_ref))[0:SG_GROUPS, :],
            od_conv_w=mine(cc[0:4]), od_conv_b=mine(cc[4:5]),
            od_b_a=mine(jnp.concatenate([ff[0:1], bb[0:1]], axis=0)),
            od_b_x=mine(jnp.concatenate([ff[1:2], bb[1:2]], axis=0)),
            od_lam=mine(jnp.concatenate([ff[2:3], bb[2:3]], axis=0)))
        for k, nm in enumerate(names):
            w_ref, m_ref, v_ref = prm[3 * k:3 * k + 3]
            at = (0,) if len(w_ref.shape) == 3 else ()
            g = grads[nm]
            dlt, m2, v2 = _adam(w_ref[at] if at else w_ref[...], g, m_ref[at] if at else m_ref[...],
                                v_ref[at] if at else v_ref[...])
            for o_ref, val in zip(outs[4 * k:4 * k + 4], (g, dlt, m2, v2)):
                if at:
                    o_ref[at] = val
                else:
                    o_ref[...] = val

    gathered = [ga, gc, gd, gf, gb, ge, gsink, gbt]
    out_shape = [jax.ShapeDtypeStruct((1, LANE), F32)]
    for nm in names:
        out_shape += [jax.ShapeDtypeStruct(params[nm][0].shape, F32)] * 4
    return _pallas(
        body, name="small_update", grid=(1,),
        in_specs=[_full(a.shape) for a in gathered + flat],
        out_specs=[_full(s.shape) for s in out_shape], out_shape=out_shape,
        compiler_params=_params(("arbitrary",)),
    )(*gathered, *flat)


VEC_ROWS = 16
VEC_LAYOUT = (("od_conv_w", 4), ("od_conv_b", 1), ("od_b_a", 2), ("od_b_x", 2), ("od_lam", 2))


def _to_slabs(full, cols_per):
    R = full.shape[0]
    return full.reshape(R, N_DEV, cols_per).transpose(1, 0, 2)


def _from_slabs(slabs):
    n, R, cp = slabs.shape
    return slabs.transpose(1, 0, 2).reshape(R, n * cp)


def kernel(x, c, positions, ada_w, ada_b, ln_g, ln_b, ev_w_in, ev_w_out, ev_sink, ev_sg_ln_g, ev_sg_ln_b, ev_sg_w, ev_sg_b, od_w_in, od_conv_w, od_conv_b, od_w_a, od_b_a, od_w_x, od_b_x, od_lam, od_w_out, loss_target, m_ada_w, m_ada_b, m_ln_g, m_ln_b, m_ev_w_in, m_ev_w_out, m_ev_sink, m_ev_sg_ln_g, m_ev_sg_ln_b, m_ev_sg_w, m_ev_sg_b, m_od_w_in, m_od_conv_w, m_od_conv_b, m_od_w_a, m_od_b_a, m_od_w_x, m_od_b_x, m_od_lam, m_od_w_out, v_ada_w, v_ada_b, v_ln_g, v_ln_b, v_ev_w_in, v_ev_w_out, v_ev_sink, v_ev_sg_ln_g, v_ev_sg_ln_b, v_ev_sg_w, v_ev_sg_b, v_od_w_in, v_od_conv_w, v_od_conv_b, v_od_w_a, v_od_b_a, v_od_w_x, v_od_b_x, v_od_lam, v_od_w_out):
    T = x.shape[1]
    me = _slot(*_my_pos())
    xs = x.reshape(T, D)
    tgt = loss_target.reshape(T, D)

    c_all, mod_all, g_vec, (g_ev_in,), (s_ev_out, s_od_in, s_od_out, sg_w, wa, wx) = _head_gather(
        c, ada_w, [ev_w_in[0].T.astype(BF16)],
        [ev_w_out[0], od_w_in[0], od_w_out[0], ev_sg_w[0], od_w_a[0], od_w_x[0]],
        [od_conv_w, od_conv_b, od_b_a, od_b_x, od_lam])
    c_all = c_all.reshape(N_DEV, D)
    w_ev_in = g_ev_in.reshape(EV_IN, D)
    vec_full = _from_slabs(g_vec)
    cw, cb = vec_full[0:4], vec_full[4:5]
    ba, bx, lam = vec_full[5:7], vec_full[7:9], vec_full[9:11]
    mod_mine = lax.dynamic_index_in_dim(mod_all, me, axis=2, keepdims=False)
    mod = mod_mine.transpose(1, 0, 2).reshape(2, 3 * D) + ada_b
    mod0 = mod[0].reshape(3, D)
    mod1 = mod[1].reshape(3, D)

    half = 8
    inv_freq = jnp.power(jnp.float32(ROPE_THETA), -jnp.arange(half, dtype=F32) / half)
    ang = positions.reshape(T).astype(F32)[:, None] * inv_freq
    cos_t = jnp.tile(jnp.cos(ang), (1, LANE // half))
    sin_t = jnp.tile(jnp.sin(ang), (1, LANE // half))
    l64 = jnp.arange(LANE) % HEAD_DIM
    rc = jnp.where(l64 < 2 * half, cos_t, 1.0)
    rs1 = jnp.where(l64 < half, -sin_t, 0.0)
    rs2 = jnp.where((l64 >= half) & (l64 < 2 * half), sin_t, 0.0)

    ln0 = jnp.stack([ln_g[0], ln_b[0]])
    ln1 = jnp.stack([ln_g[1], ln_b[1]])
    sg_lng = ev_sg_ln_g
    sg_lnb = ev_sg_ln_b
    sg_bfull = jnp.repeat(ev_sg_b[0].T, SG_DIM, axis=1)
    sink_l = jnp.repeat(ev_sink, LANE, axis=1)
    kj = jnp.arange(3 * BLK)[:, None]
    qi = jnp.arange(BLK)[None, :]
    band_bias = jnp.where(jnp.abs(kj - BLK - qi) <= BLK, 0.0, NEG_INF).astype(F32)
    lanes = jnp.arange(LANE)
    lanes2 = jnp.arange(2 * LANE)
    a128 = jnp.where(lanes2[:, None] // SG_DIM == lanes2[None, :] // SG_DIM, 1.0 / SG_DIM, 0.0).astype(BF16)
    gsum = (jnp.arange(SG_W)[:, None] // SG_DIM == lanes[None, :]).astype(BF16)
    sel = (jnp.arange(SUBLANE)[:, None] == lanes[None, :] // HEAD_DIM).astype(BF16)

    (q, kvx, su, sv, g0), _ = _ev_in(xs, mod0, w_ev_in, rc, rs1, rs2)
    (ycat, y0, lse), (g_ev_out, g_od_in, g_od_out) = _mix0_fwd(
        q, kvx, su, sv, g0, sink_l, band_bias, a128, sg_lng, sg_lnb, sg_w, sg_bfull,
        _GatherComm([s_ev_out, s_od_in, s_od_out], mid_frac=0.75))
    w_ev_out = g_ev_out.reshape(D, D)
    w_od_in = _from_slabs(g_od_in)
    w_od_out = g_od_out.reshape(D, D)
    out0, z0, x1 = _ev_out(y0, w_ev_out, xs, mod0, ln0)
    xr, g1 = _od_in(x1, mod1, w_od_in)
    fwd_f = _rglru_fwd(xr, cw, cb, wa[0], wx[0], ba[0:1], bx[0:1], lam[0:1], False, "rglru_fwd_f")
    fwd_b = _rglru_fwd(xr, cw, cb, wa[1], wx[1], ba[1:2], bx[1:2], lam[1:2], True, "rglru_fwd_b")
    dh, dg1, dx1p, d_od_out, vec_a = _od_out(fwd_f[0], fwd_b[0], g1, w_od_out, x1, tgt, mod1, ln1)

    (dxcf, dwa_f, dwx_f, vec_f), (l_od_out,) = _rglru_bwd(
        fwd_f, dh, wa[0], wx[0], lam[0:1], False, "rglru_bwd_f",
        _ExchangeComm([d_od_out.reshape(N_DEV, D // N_DEV, D)]))
    (dxcb, dwa_b, dwx_b, vec_b), _ = _rglru_bwd(fwd_b, dh, wa[1], wx[1], lam[1:2], True, "rglru_bwd_b")
    (dx1, d_od_in, vec_c), (a_wa, a_wx) = _od_in_bwd(
        dxcf, dxcb, xr, dg1, x1, dx1p, mod1, w_od_in, cw,
        _GatherComm([jnp.stack([dwa_f, dwa_b]).astype(BF16), jnp.stack([dwx_f, dwx_b]).astype(BF16)],
                    mid_frac=0.75))
    dxp, dyc, dg0, d_ev_out, vec_d = _ev_out_bwd(dx1, z0, out0, y0, ycat, g0, w_ev_out, mod0, ln0)
    (dq, dkv, dsu, dsv, d_sg_w, d_sg_bt, vec_e, d_sink_l), (l_od_in, l_ev_out) = _mix0_bwd(
        q, kvx, lse, dyc, ycat, su, sv, sink_l, band_bias, a128, gsum, sel, sg_lng, sg_lnb, sg_w, sg_bfull,
        rc, rs1, rs2, _ExchangeComm([d_od_in, d_ev_out.reshape(N_DEV, D // N_DEV, D)]))
    (grad_x, d_ev_in, vec_g), _ = _ev_in_bwd(dq, dkv, dsu, dsv, dg0, xs, dxp, mod0, w_ev_in, rc, rs1, rs2)

    l_ev_in, (ga, gc, gd, gf, gb, gg, ge, gsink, gbt, a_sgw) = _tail_exchange(
        d_ev_in.reshape(N_DEV, EV_IN // N_DEV, D),
        [vec_a, vec_c, vec_d, vec_f, vec_b, vec_g, vec_e, d_sink_l, d_sg_bt, d_sg_w.astype(BF16)])

    dmod_all = jnp.stack([jnp.concatenate([gg[:, 0], gg[:, 1], gd[:, 2]], axis=-1),
                          jnp.concatenate([gc[:, 5], gc[:, 6], ga[:, 2]], axis=-1)], axis=1)
    cols = ada_w.shape[2]
    dmod_cols = lax.dynamic_slice_in_dim(dmod_all, me * cols, cols, axis=2).transpose(1, 0, 2)
    (g_ada_w, d_ada_w, nm_ada_w, nv_ada_w, g_ada_b, d_ada_b, nm_ada_b, nv_ada_b) = _ada_update(
        c_all, dmod_cols, dmod_all, ada_w, m_ada_w, v_ada_w, ada_b, m_ada_b, v_ada_b)

    res = dict(ada_w=[g_ada_w, d_ada_w, nm_ada_w, nv_ada_w], ada_b=[g_ada_b, d_ada_b, nm_ada_b, nv_ada_b])
    (r_ev_in,) = _reduce_adam([(l_ev_in, ev_w_in[0].T, m_ev_w_in[0].T, v_ev_w_in[0].T)], "adam_ev_w_in")
    res["ev_w_in"] = [a.T[None] for a in r_ev_in]
    (r_od_in,) = _reduce_adam([(l_od_in, od_w_in[0], m_od_w_in[0], v_od_w_in[0])], "adam_od_w_in")
    r_ev_out, r_od_out = _reduce_adam([(l_ev_out, ev_w_out[0], m_ev_w_out[0], v_ev_w_out[0]),
                                       (l_od_out, od_w_out[0], m_od_w_out[0], v_od_w_out[0])], "adam_w_out")
    for name, r in (("od_w_in", r_od_in), ("ev_w_out", r_ev_out), ("od_w_out", r_od_out)):
        res[name] = [a[None] for a in r]
    res["od_w_a"], res["od_w_x"], res["ev_sg_w"] = _slots_adam(
        [(a_wa, od_w_a, m_od_w_a, v_od_w_a), (a_wx, od_w_x, m_od_w_x, v_od_w_x),
         (a_sgw, ev_sg_w, m_ev_sg_w, v_ev_sg_w)], "adam_gates")
    small = dict(ln_g=(ln_g, m_ln_g, v_ln_g), ln_b=(ln_b, m_ln_b, v_ln_b),
                 ev_sg_ln_g=(ev_sg_ln_g, m_ev_sg_ln_g, v_ev_sg_ln_g),
                 ev_sg_ln_b=(ev_sg_ln_b, m_ev_sg_ln_b, v_ev_sg_ln_b),
                 ev_sink=(ev_sink, m_ev_sink, v_ev_sink), ev_sg_b=(ev_sg_b, m_ev_sg_b, v_ev_sg_b),
                 od_conv_w=(od_conv_w, m_od_conv_w, v_od_conv_w), od_conv_b=(od_conv_b, m_od_conv_b, v_od_conv_b),
                 od_b_a=(od_b_a, m_od_b_a, v_od_b_a), od_b_x=(od_b_x, m_od_b_x, v_od_b_x),
                 od_lam=(od_lam, m_od_lam, v_od_lam))
    small_out = _small_update(ga, gc, gd, gf, gb, ge, gsink, gbt, small)
    loss = small_out[0][0, 0]
    for k, name in enumerate(SMALL_PARAMS):
        res[name] = small_out[1 + 4 * k:5 + 4 * k]

    order = ["ada_w", "ada_b", "ln_g", "ln_b", "ev_w_in", "ev_w_out", "ev_sink", "ev_sg_ln_g", "ev_sg_ln_b",
             "ev_sg_w", "ev_sg_b", "od_w_in", "od_conv_w", "od_conv_b", "od_w_a", "od_b_a", "od_w_x", "od_b_x",
             "od_lam", "od_w_out"]
    outs = [loss, grad_x.reshape(1, T, D)]
    for kind in range(4):
        outs += [res[name][kind] for name in order]
    return tuple(outs)
```

```python
import functools

import jax
import jax.numpy as jnp
from jax import lax
from jax.experimental import pallas as pl
from jax.experimental.pallas import tpu as pltpu

F32 = jnp.float32
BF16 = jnp.bfloat16

N_DEV = 8
D = 1024
N_HEADS = 8
HEAD_DIM = 64
KV_WIDTH = 128
ATTN_W = 512
SG_W = 512
SG_GROUPS = 8
SG_DIM = 64
BLK = 128
KVX_W = 1024
EV_IN = 2816
OD_IN = 2048
RNN_HEADS = 8
RNN_HD = 128
ALPHA = 4.0 ** 0.25
LN_EPS = 1e-5
NEG_INF = -1e30
RG_C = 8.0
ROPE_THETA = 500000.0
LR, B1, B2, EPS, WD, STEP = 0.001, 0.9, 0.999, 1e-08, 0.01, 10

LANE = 128
SUBLANE = 8
TM = 256
TMF = 512
TMO = 512
TS = 256
VMEM_LIMIT = 56 * 1024 * 1024

MESH = pl.DeviceIdType.MESH


def _pallas(body, **kw):
    return pl.pallas_call(body, **kw)


def _params(sem, vmem=VMEM_LIMIT):
    return pltpu.CompilerParams(dimension_semantics=sem, vmem_limit_bytes=vmem)


def _sigmoid(x):
    return 0.5 * jnp.tanh(0.5 * x) + 0.5


def _silu_and_grad(x):
    s = _sigmoid(x)
    return x * s, s * (1.0 + x * (1.0 - s))


def _dot(a, b):
    return jnp.dot(a.astype(BF16), b.astype(BF16), preferred_element_type=F32)


def _dot_nt(a, b):
    return lax.dot_general(a.astype(BF16), b.astype(BF16), (((1,), (1,)), ((), ())), preferred_element_type=F32)


def _dot_tn(a, b):
    return lax.dot_general(a.astype(BF16), b.astype(BF16), (((0,), (0,)), ((), ())), preferred_element_type=F32)


def _ln_fwd(z, g, b):
    mu = jnp.mean(z, axis=-1, keepdims=True)
    zc = z - mu
    var = jnp.mean(zc * zc, axis=-1, keepdims=True)
    rstd = lax.rsqrt(var + LN_EPS)
    xhat = zc * rstd
    return xhat * g + b, xhat, rstd


def _ln_bwd(dy, xhat, rstd, g):
    dxh = dy * g
    m1 = jnp.mean(dxh, axis=-1, keepdims=True)
    m2 = jnp.mean(dxh * xhat, axis=-1, keepdims=True)
    return rstd * (dxh - m1 - xhat * m2)


def _rowsum(v):
    return jnp.sum(v, axis=0, keepdims=True)


def _rope_fwd(t, c, s1, s2):
    return t * c + pltpu.roll(t, LANE - 8, 1) * s1 + pltpu.roll(t, 8, 1) * s2


def _rope_bwd(d, c, s1, s2):
    return d * c + pltpu.roll(d * s1, 8, 1) + pltpu.roll(d * s2, LANE - 8, 1)


def _adam(w, g, m, v):
    m2 = B1 * m + (1.0 - B1) * g
    v2 = B2 * v + (1.0 - B2) * (g * g)
    m_hat = m2 / (1.0 - B1 ** STEP)
    v_hat = v2 / (1.0 - B2 ** STEP)
    delta = -LR * (m_hat / (jnp.sqrt(v_hat) + EPS) + WD * w)
    return delta, m2, v2


def _tile(rows, width):
    return pl.BlockSpec((rows, width), lambda i: (i, 0))


def _full(shape):
    zeros = (0,) * len(shape)
    return pl.BlockSpec(shape, lambda i: zeros)


def _rev_tile(rows, width, n, reverse):
    if reverse:
        return pl.BlockSpec((rows, width), lambda i: (n - 1 - i, 0))
    return pl.BlockSpec((rows, width), lambda i: (i, 0))


def _halo_specs(rows, width, n, total_rows, reverse):
    per = rows // SUBLANE
    last = total_rows // SUBLANE - 1

    def tile_of(i):
        return (n - 1 - i) if reverse else i

    prev = pl.BlockSpec((SUBLANE, width), lambda i: (jnp.maximum(tile_of(i) * per - 1, 0), 0))
    nxt = pl.BlockSpec((SUBLANE, width), lambda i: (jnp.minimum((tile_of(i) + 1) * per, last), 0))
    return prev, nxt


def _my_pos():
    return lax.axis_index("x"), lax.axis_index("y"), lax.axis_index("c")


def _slot(px, py, pc):
    return 4 * px + 2 * py + pc


class _GatherComm:
    has_mid = True

    def __init__(self, arrs, mid_frac=0.5):
        self.arrs = list(arrs)
        self.n = len(self.arrs)
        self.mid_frac = mid_frac

    def out_shapes(self):
        return [jax.ShapeDtypeStruct((N_DEV,) + a.shape, a.dtype) for a in self.arrs]

    def sems(self):
        return [pltpu.SemaphoreType.DMA((7 * self.n,)), pltpu.SemaphoreType.DMA((7 * self.n,)),
                pltpu.SemaphoreType.DMA((self.n,))]

    def _parts(self, ins, outs, sems):
        send_sems, recv_sems, local_sems = sems
        x, y, c = _my_pos()
        me, sibling = (x, y, c), (x, y, 1 - c)
        chips = [(1 - x, y), (x, 1 - y), (1 - x, 1 - y)]

        def copy(a, k, block, to, src=None):
            dst = outs[a].at[_slot(*block)]
            return pltpu.make_async_remote_copy(
                src_ref=dst if src is None else src, dst_ref=dst,
                send_sem=send_sems.at[a * 7 + k], recv_sem=recv_sems.at[a * 7 + k],
                device_id=to, device_id_type=MESH)

        local = [pltpu.make_async_copy(ins[a], outs[a].at[_slot(*me)], local_sems.at[a]) for a in range(self.n)]
        first = []
        for a in range(self.n):
            first.append(copy(a, 0, me, sibling, src=ins[a]))
            first += [copy(a, 1 + j, me, (*chip, c), src=ins[a]) for j, chip in enumerate(chips)]
        ici_in = [copy(a, 1 + j, (*chip, c), me) for j, chip in enumerate(chips) for a in range(self.n)]
        passed = [copy(a, 4 + j, (*chip, c), sibling) for j, chip in enumerate(chips) for a in range(self.n)]
        d2d_in = []
        for a in range(self.n):
            d2d_in.append(copy(a, 0, sibling, me))
            d2d_in += [copy(a, 4 + j, (*chip, 1 - c), me) for j, chip in enumerate(chips)]
        return local, first, ici_in, passed, d2d_in

    def start(self, ins, outs, sems):
        local, first, _, _, _ = self._parts(ins, outs, sems)
        for cp in local + first:
            cp.start()

    def mid(self, ins, outs, sems):
        _, _, ici_in, passed, _ = self._parts(ins, outs, sems)
        for arrived, fw in zip(ici_in, passed):
            arrived.wait_recv()
            fw.start()

    def finish(self, ins, outs, sems):
        local, first, _, passed, d2d_in = self._parts(ins, outs, sems)
        for cp in d2d_in:
            cp.wait_recv()
        for cp in first + passed:
            cp.wait_send()
        for cp in local:
            cp.wait()


class _ExchangeComm:
    has_mid = False

    def __init__(self, arrs):
        self.arrs = list(arrs)
        self.n = len(self.arrs)

    def out_shapes(self):
        return [jax.ShapeDtypeStruct(a.shape, a.dtype) for a in self.arrs]

    def sems(self):
        return [pltpu.SemaphoreType.DMA((7 * self.n,)), pltpu.SemaphoreType.DMA((7 * self.n,)),
                pltpu.SemaphoreType.DMA((self.n,))]

    def _copies(self, ins, outs, sems):
        send_sems, recv_sems, local_sems = sems
        x, y, c = _my_pos()
        mine = _slot(x, y, c)
        copies = [pltpu.make_async_copy(ins[a].at[mine], outs[a].at[mine], local_sems.at[a]) for a in range(self.n)]
        for k in range(1, N_DEV):
            px = (1 - x) if (k & 4) else x
            py = (1 - y) if (k & 2) else y
            pc = (1 - c) if (k & 1) else c
            for a in range(self.n):
                copies.append(pltpu.make_async_remote_copy(
                    src_ref=ins[a].at[_slot(px, py, pc)], dst_ref=outs[a].at[mine],
                    send_sem=send_sems.at[a * 7 + k - 1], recv_sem=recv_sems.at[a * 7 + k - 1],
                    device_id=(px, py, pc), device_id_type=MESH))
        return copies

    def start(self, ins, outs, sems):
        for cp in self._copies(ins, outs, sems):
            cp.start()

    def finish(self, ins, outs, sems):
        for cp in self._copies(ins, outs, sems):
            cp.wait()


def _fused_call(body, comm, operands, *, name, grid, in_specs, out_specs, out_shape, scratch_shapes=(),
                semantics=("arbitrary",)):
    n_in, n_out, n_scr = len(in_specs), len(out_specs), len(scratch_shapes)
    if comm is None:
        res = _pallas(body, name=name, grid=grid, in_specs=list(in_specs), out_specs=list(out_specs),
                      out_shape=list(out_shape), scratch_shapes=list(scratch_shapes),
                      compiler_params=_params(semantics))(*operands)
        return list(res), []
    k = comm.n
    steps = grid[0]

    def wrapped(*refs):
        ins, cins = refs[:n_in], refs[n_in:n_in + k]
        outs = refs[n_in + k:n_in + k + n_out]
        couts = refs[n_in + k + n_out:n_in + 2 * k + n_out]
        rest = refs[n_in + 2 * k + n_out:]
        scratch, sems = rest[:n_scr], rest[n_scr:]
        i = pl.program_id(0)

        @pl.when(i == 0)
        def _():
            comm.start(cins, couts, sems)

        body(*ins, *outs, *scratch)

        if comm.has_mid:
            @pl.when(i == int(steps * comm.mid_frac))
            def _():
                comm.mid(cins, couts, sems)

        @pl.when(i == steps - 1)
        def _():
            comm.finish(cins, couts, sems)

    any_spec = pl.BlockSpec(memory_space=pl.ANY)
    res = _pallas(wrapped, name=name, grid=grid, in_specs=list(in_specs) + [any_spec] * k,
                  out_specs=list(out_specs) + [any_spec] * k, out_shape=list(out_shape) + comm.out_shapes(),
                  scratch_shapes=list(scratch_shapes) + comm.sems(),
                  compiler_params=_params(("arbitrary",)))(*operands, *comm.arrs)
    return list(res[:n_out]), list(res[n_out:])


def _head_gather(c, ada_w, big, to_cast, vec_parts):
    cols = ada_w.shape[2]
    g_c, g_big = _GatherComm([c]), _GatherComm(big)
    g_mod = _GatherComm([jax.ShapeDtypeStruct((2, N_DEV, cols), F32)])
    g_vec = _GatherComm([jax.ShapeDtypeStruct((VEC_ROWS, LANE), F32)])
    nb, nc, nv = g_big.n, len(to_cast), len(vec_parts)

    def body(*refs):
        c_ref, w_ref = refs[0], refs[1]
        vec_in = refs[2:2 + nv]
        cast_in = refs[2 + nv:2 + nv + nc]
        big_in = refs[2 + nv + nc:2 + nv + nc + nb]
        outs = refs[2 + nv + nc + nb:]
        c_all_ref, mod_all_ref, vec_all_ref = outs[0], outs[1], outs[2]
        cast_out = outs[3:3 + nc]
        big_out = outs[3 + nc:3 + nc + nb]
        part_ref, pack_ref = outs[3 + nc + nb], outs[4 + nc + nb]
        sems = outs[5 + nc + nb:]
        s_c, s_mod, s_big, s_vec = sems[0:3], sems[3:6], sems[6:9], sems[9:12]
        g_c.start([c_ref], [c_all_ref], s_c)
        g_big.start(big_in, big_out, s_big)
        pack_ref[...] = jnp.zeros_like(pack_ref)
        row = 0
        for ref, (_, nrows) in zip(vec_in, VEC_LAYOUT):
            pack_ref[row:row + nrows, :] = ref[0] if len(ref.shape) == 3 else ref[...]
            row += nrows
        g_vec.start([pack_ref], [vec_all_ref], s_vec)
        g_c.mid([c_ref], [c_all_ref], s_c)
        g_c.finish([c_ref], [c_all_ref], s_c)
        cv = c_all_ref[:, 0, :]
        cond = cv * _sigmoid(cv)
        for l in range(2):
            part_ref[l] = _dot(cond, w_ref[l])
        g_mod.start([part_ref], [mod_all_ref], s_mod)
        for src, dst in zip(cast_in, cast_out):
            dst[...] = src[...].astype(BF16)
        for g, ins, outs_, sm in ((g_vec, [pack_ref], [vec_all_ref], s_vec), (g_mod, [part_ref], [mod_all_ref], s_mod),
                                  (g_big, big_in, big_out, s_big)):
            g.mid(ins, outs_, sm)
            g.finish(ins, outs_, sm)

    any_spec = pl.BlockSpec(memory_space=pl.ANY)
    vmem_spec = pl.BlockSpec(memory_space=pltpu.VMEM)
    res = _pallas(
        body, name="head_gather",
        out_shape=(g_c.out_shapes() + g_mod.out_shapes() + g_vec.out_shapes()
                   + [jax.ShapeDtypeStruct(a.shape, BF16) for a in to_cast] + g_big.out_shapes()),
        in_specs=[vmem_spec] * (2 + nv + nc) + [any_spec] * nb,
        out_specs=[vmem_spec] * (3 + nc) + [any_spec] * nb,
        scratch_shapes=[pltpu.VMEM((2, N_DEV, cols), F32), pltpu.VMEM((VEC_ROWS, LANE), F32)]
        + g_c.sems() + g_mod.sems() + g_big.sems() + g_vec.sems(),
        compiler_params=pltpu.CompilerParams(vmem_limit_bytes=VMEM_LIMIT),
    )(c, ada_w, *vec_parts, *to_cast, *big)
    return res[0], res[1], res[2], list(res[3 + nc:]), list(res[3:3 + nc])


def _ada_update(c_all, dmod_cols, dmod_all, ada_w, m_w, v_w, ada_b, m_b, v_b):
    cols = ada_w.shape[2]
    nb = ada_b.shape[1]

    def body(c_ref, dmc_ref, dma_ref, w_ref, mw_ref, vw_ref, b_ref, mb_ref, vb_ref,
             gw_ref, dw_ref, nmw_ref, nvw_ref, gb_ref, db_ref, nmb_ref, nvb_ref):
        cv = c_ref[...]
        cond = cv * _sigmoid(cv)
        for l in range(2):
            g = _dot_tn(cond, dmc_ref[l])
            gw_ref[l] = g
            dlt, m2, v2 = _adam(w_ref[l], g, mw_ref[l], vw_ref[l])
            dw_ref[l] = dlt
            nmw_ref[l] = m2
            nvw_ref[l] = v2
        gb = dma_ref[0]
        for i in range(1, N_DEV):
            gb = gb + dma_ref[i]
        gb_ref[...] = gb
        dlt, m2, v2 = _adam(b_ref[...], gb, mb_ref[...], vb_ref[...])
        db_ref[...] = dlt
        nmb_ref[...] = m2
        nvb_ref[...] = v2

    wspec = _full((2, D, cols))
    bspec = _full((2, nb))
    wshape = jax.ShapeDtypeStruct((2, D, cols), F32)
    bshape = jax.ShapeDtypeStruct((2, nb), F32)
    return _pallas(
        body, name="ada_update", grid=(1,),
        in_specs=[_full((N_DEV, D)), _full((2, N_DEV, cols)), _full((N_DEV, 2, nb)),
                  wspec, wspec, wspec, bspec, bspec, bspec],
        out_specs=[wspec] * 4 + [bspec] * 4,
        out_shape=[wshape] * 4 + [bshape] * 4,
        compiler_params=_params(("arbitrary",)),
    )(c_all, dmod_cols, dmod_all, ada_w, m_w, v_w, ada_b, m_b, v_b)


def _ev_in(x, mod, w_in, rc, rs1, rs2, comm=None):
    T = x.shape[0]

    def body(x_ref, mod_ref, w_ref, c_ref, s1_ref, s2_ref, q_ref, kv_ref, su_ref, sv_ref, g_ref):
        h = x_ref[...] * (1.0 + mod_ref[1:2, :]) + mod_ref[0:1, :]
        p = _dot_nt(h, w_ref[...])
        c, s1, s2 = c_ref[...], s1_ref[...], s2_ref[...]
        for j in range(ATTN_W // LANE):
            qr = _rope_fwd(p[:, j * LANE:(j + 1) * LANE], c, s1, s2)
            q_ref[:, j * LANE:(j + 1) * LANE] = (qr * (HEAD_DIM ** -0.5)).astype(BF16)
        low = lax.broadcasted_iota(jnp.int32, (TMF, LANE), 1) < HEAD_DIM
        for j, val in enumerate((_rope_fwd(p[:, 512:640], c, s1, s2), p[:, 640:768])):
            swapped = pltpu.roll(val, HEAD_DIM, 1)
            tiles = (jnp.where(low, val, 0.0), jnp.where(low, 0.0, swapped),
                     jnp.where(low, swapped, 0.0), jnp.where(low, 0.0, val))
            for k, tile in enumerate(tiles):
                kv_ref[:, (4 * j + k) * LANE:(4 * j + k + 1) * LANE] = tile.astype(BF16)
        su_ref[...] = p[:, 768:1280].astype(BF16)
        sv_ref[...] = p[:, 1280:1792].astype(BF16)
        g_ref[...] = p[:, 1792:2816].astype(BF16)

    sh = lambda w: jax.ShapeDtypeStruct((T, w), BF16)
    return _fused_call(
        body, comm, (x, mod, w_in, rc, rs1, rs2), name="ev_in", grid=(T // TMF,),
        in_specs=[_tile(TMF, D), _full((3, D)), _full((EV_IN, D)), _tile(TMF, LANE), _tile(TMF, LANE),
                  _tile(TMF, LANE)],
        out_specs=[_tile(TMF, ATTN_W), _tile(TMF, KVX_W), _tile(TMF, SG_W), _tile(TMF, SG_W), _tile(TMF, D)],
        out_shape=[sh(ATTN_W), sh(KVX_W), sh(SG_W), sh(SG_W), sh(D)], semantics=("parallel",))


def _band_specs(width, nb):
    return [pl.BlockSpec((BLK, width), lambda n: (jnp.maximum(n - 1, 0), 0)),
            pl.BlockSpec((BLK, width), lambda n: (n, 0)),
            pl.BlockSpec((BLK, width), lambda n: (jnp.minimum(n + 1, nb - 1), 0))]


def _band_bias(bias_ref, n, nb):
    rows = lax.broadcasted_iota(jnp.int32, (3 * BLK, 1), 0)
    outside = ((rows < BLK) & (n == 0)) | ((rows >= 2 * BLK) & (n == nb - 1))
    return bias_ref[...] + jnp.where(outside, NEG_INF, 0.0)


def _lane_tile(ref, t):
    return ref[:, t * LANE:(t + 1) * LANE]


def _split_bf16(v):
    hi = v.astype(BF16)
    return hi, (v - hi.astype(F32)).astype(BF16)


def _group_mean(v, a_ref, exact_bf16=False):
    hi, lo = _split_bf16(v)
    a = a_ref[...]
    out = []
    for t in range(SG_W // (2 * LANE)):
        sl = slice(t * 2 * LANE, (t + 1) * 2 * LANE)
        r = jnp.dot(hi[:, sl], a, preferred_element_type=F32)
        if not exact_bf16:
            r = r + jnp.dot(lo[:, sl], a, preferred_element_type=F32)
        out.append(r)
    return jnp.concatenate(out, axis=-1)


def _sg_core(sv_ref, lng, lnb, a_ref, w_ref, bfull_ref):
    svf = sv_ref[...].astype(F32)
    xc = svf - _group_mean(svf, a_ref, exact_bf16=True)
    rstd = lax.rsqrt(_group_mean(xc * xc, a_ref) + LN_EPS)
    xhat = xc * rstd
    vb = (xhat * lng + lnb).astype(BF16)
    low = lax.broadcasted_iota(jnp.int32, (BLK, LANE), 1) < SG_DIM
    tiles = []
    for t in range(SG_W // LANE):
        v2 = vb[:, t * LANE:(t + 1) * LANE]
        r0 = jnp.dot(w_ref[2 * t], v2, preferred_element_type=F32)
        r1 = jnp.dot(w_ref[2 * t + 1], v2, preferred_element_type=F32)
        tiles.append(jnp.where(low, r0, r1))
    svm = jnp.concatenate(tiles, axis=-1) + bfull_ref[...]
    return xhat, rstd, vb, svm


def _mix0_fwd(q, kvx, su, sv, g0, sink_l, bias, a128, sg_lng, sg_lnb, sg_w, sg_bfull, comm=None):
    T = q.shape[0]
    nb = T // BLK

    def body(q_ref, kp_ref, kc_ref, kn_ref, su_ref, sv_ref, g_ref, sink_ref, bias_ref, a_ref, lng_ref, lnb_ref,
             w_ref, bfull_ref, ycat_ref, y0_ref, lse_ref):
        n = pl.program_id(0)
        bias = _band_bias(bias_ref, n, nb)
        kvx = jnp.concatenate([kp_ref[...], kc_ref[...], kn_ref[...]], axis=0)
        tiles = []
        for t in range(ATTN_W // LANE):
            qt = _lane_tile(q_ref, t)
            acc = None
            for par in range(2):
                h = 2 * t + par
                kt = 2 * (h // 4) + par
                ke = kvx[:, kt * LANE:(kt + 1) * LANE]
                ve = kvx[:, (4 + kt) * LANE:(5 + kt) * LANE]
                st = _dot_nt(ke, qt) + bias
                sk = _lane_tile(sink_ref, h)
                m = jnp.maximum(jnp.max(st, axis=0, keepdims=True), sk)
                p = jnp.exp(st - m)
                denom = jnp.sum(p, axis=0, keepdims=True) + jnp.exp(sk - m)
                contrib = _dot_tn(p * (1.0 / denom), ve)
                acc = contrib if acc is None else acc + contrib
                lse_ref[0, :, h * LANE:(h + 1) * LANE] = m + jnp.log(denom)
            tiles.append(acc)
        _, _, _, svm = _sg_core(sv_ref, lng_ref[...], lnb_ref[...], a_ref, w_ref, bfull_ref)
        tiles.append(su_ref[...].astype(F32) * svm)
        ycat = jnp.concatenate(tiles, axis=-1)
        gf = g_ref[...].astype(F32)
        ycat_ref[...] = ycat.astype(BF16)
        y0_ref[...] = (ycat * (gf * _sigmoid(gf))).astype(BF16)

    return _fused_call(
        body, comm, (q, kvx, kvx, kvx, su, sv, g0, sink_l, bias, a128, sg_lng, sg_lnb, sg_w, sg_bfull),
        name="mix0_fwd", grid=(nb,),
        in_specs=[_tile(BLK, ATTN_W)] + _band_specs(KVX_W, nb) + [
            _tile(BLK, SG_W), _tile(BLK, SG_W), _tile(BLK, D), _full((1, N_HEADS * LANE)), _full((3 * BLK, LANE)),
            _full((2 * LANE, 2 * LANE)),_full((1, SG_W)), _full((1, SG_W)), _full((SG_GROUPS, BLK, BLK)),
            _full((BLK, SG_W))],
        out_specs=[_tile(BLK, D), _tile(BLK, D), pl.BlockSpec((1, 1, N_HEADS * LANE), lambda n: (n, 0, 0))],
        out_shape=[jax.ShapeDtypeStruct((T, D), BF16), jax.ShapeDtypeStruct((T, D), BF16),
                   jax.ShapeDtypeStruct((nb, 1, N_HEADS * LANE), F32)], semantics=("parallel",))


def _ev_out(y0, w_out, x, mod, lnp):
    T = x.shape[0]

    def body(y_ref, w_ref, x_ref, mod_ref, ln_ref, out_ref, z_ref, x1_ref):
        out = _dot(y_ref[...], w_ref[...])
        z = ALPHA * x_ref[...] + mod_ref[2:3, :] * out
        x1, _, _ = _ln_fwd(z, ln_ref[0:1, :], ln_ref[1:2, :])
        out_ref[...] = out.astype(BF16)
        z_ref[...] = z
        x1_ref[...] = x1

    return _pallas(
        body, name="ev_out", grid=(T // TMF,),
        in_specs=[_tile(TMF, D), _full((D, D)), _tile(TMF, D), _full((3, D)), _full((2, D))],
        out_specs=[_tile(TMF, D)] * 3,
        out_shape=[jax.ShapeDtypeStruct((T, D), BF16), jax.ShapeDtypeStruct((T, D), F32),
                   jax.ShapeDtypeStruct((T, D), F32)],
        compiler_params=_params(("parallel",)),
    )(y0, w_out, x, mod, lnp)


def _od_in(x1, mod, w_in):
    T = x1.shape[0]

    def body(x_ref, mod_ref, w_ref, xr_ref, g_ref):
        h = x_ref[...] * (1.0 + mod_ref[1:2, :]) + mod_ref[0:1, :]
        p = _dot(h, w_ref[...])
        xr_ref[...] = p[:, :D]
        g_ref[...] = p[:, D:].astype(BF16)

    return _pallas(
        body, name="od_in", grid=(T // TMF,),
        in_specs=[_tile(TMF, D), _full((3, D)), _full((D, OD_IN))],
        out_specs=[_tile(TMF, D), _tile(TMF, D)],
        out_shape=[jax.ShapeDtypeStruct((T, D), F32), jax.ShapeDtypeStruct((T, D), BF16)],
        compiler_params=_params(("parallel",)),
    )(x1, mod, w_in)


def _ext_rows(prev_ref, cur, next_ref, j, n):
    prev = jnp.where(j > 0, prev_ref[...], 0.0)
    nxt = jnp.where(j < n - 1, next_ref[...], 0.0)
    return jnp.concatenate([prev, cur, nxt], axis=0)


def _shift_rows(ext, off, rows):
    total = ext.shape[0]
    if off == 0:
        return ext[SUBLANE:SUBLANE + rows, :]
    return pltpu.roll(ext, (-off) % total, 0)[SUBLANE:SUBLANE + rows, :]


def _conv_fwd(ext, cw, cb, rows):
    xc = cb
    for k in range(4):
        xc = xc + cw[k:k + 1, :] * _shift_rows(ext, k - 2, rows)
    return xc


def _gates(xc, wa_ref, wx_ref, ba, bx, lam):
    pr, pi = [], []
    for h in range(RNN_HEADS):
        xh = xc[:, h * RNN_HD:(h + 1) * RNN_HD].astype(BF16)
        pr.append(_dot(xh, wa_ref[h]))
        pi.append(_dot(xh, wx_ref[h]))
    r = _sigmoid(jnp.concatenate(pr, axis=-1) + ba)
    ig = _sigmoid(jnp.concatenate(pi, axis=-1) + bx)
    sp = jnp.maximum(-lam, 0.0) + jnp.log(1.0 + jnp.exp(-jnp.abs(lam)))
    neg_log_a = RG_C * r * sp
    a = jnp.exp(-neg_log_a)
    s2 = (1.0 + a * a) * jnp.tanh(neg_log_a)
    inv_s = lax.rsqrt(jnp.maximum(s2, 1e-30))
    return r, ig, sp, a, s2 * inv_s, inv_s


def _scan_tile(a_ref, b_ref, o_ref, carry_ref, rows, reverse):
    ridx = lax.broadcasted_iota(jnp.int32, (SUBLANE, D), 0)
    groups = rows // SUBLANE

    def group(gi, h):
        g = (groups - 1 - gi) if reverse else gi
        off = pl.multiple_of(g * SUBLANE, SUBLANE)
        a = a_ref[pl.ds(off, SUBLANE), :]
        b = b_ref[pl.ds(off, SUBLANE), :]
        for sh in (1, 2, 4):
            if reverse:
                keep = ridx < SUBLANE - sh
                a_p = jnp.where(keep, pltpu.roll(a, SUBLANE - sh, 0), 1.0)
                b_p = jnp.where(keep, pltpu.roll(b, SUBLANE - sh, 0), 0.0)
            else:
                keep = ridx >= sh
                a_p = jnp.where(keep, pltpu.roll(a, sh, 0), 1.0)
                b_p = jnp.where(keep, pltpu.roll(b, sh, 0), 0.0)
            b = b + a * b_p
            a = a * a_p
        hh = b + a * h
        o_ref[pl.ds(off, SUBLANE), :] = hh
        return hh[0:1, :] if reverse else hh[SUBLANE - 1:SUBLANE, :]

    carry_ref[...] = lax.fori_loop(0, groups, group, carry_ref[...])


def _rglru_fwd(xr, cw, cb, wa, wx, ba, bx, lam, reverse, name):
    T = xr.shape[0]
    n = T // TS
    prev_spec, next_spec = _halo_specs(TS, D, n, T, reverse)

    def body(prev_ref, cur_ref, next_ref, cw_ref, cb_ref, wa_ref, wx_ref, ba_ref, bx_ref, lam_ref,
             h_ref, a_ref, s_ref, r_ref, ig_ref, xc_ref, b_s, carry):
        i = pl.program_id(0)
        j = (n - 1 - i) if reverse else i

        @pl.when(i == 0)
        def _():
            carry[...] = jnp.zeros_like(carry)

        ext = _ext_rows(prev_ref, cur_ref[...], next_ref, j, n)
        xc = _conv_fwd(ext, cw_ref[...], cb_ref[...], TS)
        r, ig, _, a, s, _ = _gates(xc, wa_ref, wx_ref, ba_ref[...], bx_ref[...], lam_ref[...])
        s_ref[...] = s
        r_ref[...] = r.astype(BF16)
        ig_ref[...] = ig.astype(BF16)
        xc_ref[...] = xc.astype(BF16)
        a_ref[...] = a
        b_s[...] = s * ig * xc
        _scan_tile(a_ref, b_s, h_ref, carry, TS, reverse)

    wspec = _full((RNN_HEADS, RNN_HD, RNN_HD))
    cur = _rev_tile(TS, D, n, reverse)
    f32 = jax.ShapeDtypeStruct((T, D), F32)
    b16 = jax.ShapeDtypeStruct((T, D), BF16)
    return _pallas(
        body, name=name, grid=(n,),
        in_specs=[prev_spec, cur, next_spec, _full((4, D)), _full((1, D)),
                  wspec, wspec, _full((1, D)), _full((1, D)), _full((1, D))],
        out_specs=[cur] * 6,
        out_shape=[f32, f32, f32, b16, b16, b16],
        scratch_shapes=[pltpu.VMEM((TS, D), F32), pltpu.VMEM((1, D), F32)],
        compiler_params=_params(("arbitrary",)),
    )(xr, xr, xr, cw, cb, wa, wx, ba, bx, lam)


def _od_out(hf, hb, g1, w_out, x1, tgt, mod, lnp):
    T = x1.shape[0]

    def body(hf_ref, hb_ref, g_ref, w_ref, x_ref, t_ref, mod_ref, ln_ref,
             dh_ref, dg_ref, dx_ref, dwb_ref, vec_ref, dw_ref):
        i = pl.program_id(0)

        @pl.when(i == 0)
        def _():
            dw_ref[...] = jnp.zeros_like(dw_ref)
            vec_ref[...] = jnp.zeros_like(vec_ref)

        hs = hf_ref[...] + hb_ref[...]
        sg, dsg = _silu_and_grad(g_ref[...].astype(F32))
        yr = (hs * sg).astype(BF16)
        w = w_ref[...]
        out = _dot(yr, w)
        gate = mod_ref[2:3, :]
        z = ALPHA * x_ref[...] + gate * out
        lng = ln_ref[0:1, :]
        x2, xhat, rstd = _ln_fwd(z, lng, ln_ref[1:2, :])
        diff = x2 - t_ref[...]
        vec_ref[3:4, 0:LANE] += 0.5 * jnp.sum(diff * diff) * (1.0 / D)
        dx2 = diff * (1.0 / D)
        dz = _ln_bwd(dx2, xhat, rstd, lng)
        vec_ref[0:1, :] += _rowsum(dx2 * xhat)
        vec_ref[1:2, :] += _rowsum(dx2)
        vec_ref[2:3, :] += _rowsum(dz * out)
        dout = (dz * gate).astype(BF16)
        dyr = _dot_nt(dout, w)
        dw_ref[...] += _dot_tn(yr, dout)
        dh_ref[...] = dyr * sg
        dg_ref[...] = (dyr * hs * dsg).astype(BF16)
        dx_ref[...] = ALPHA * dz

        @pl.when(i == T // TMO - 1)
        def _():
            dwb_ref[...] = dw_ref[...].astype(BF16)

    return _pallas(
        body, name="od_out", grid=(T // TMO,),
        in_specs=[_tile(TMO, D), _tile(TMO, D), _tile(TMO, D), _full((D, D)), _tile(TMO, D), _tile(TMO, D),
                  _full((3, D)), _full((2, D))],
        out_specs=[_tile(TMO, D), _tile(TMO, D), _tile(TMO, D), _full((D, D)), _full((SUBLANE, D))],
        out_shape=[jax.ShapeDtypeStruct((T, D), F32), jax.ShapeDtypeStruct((T, D), BF16),
                   jax.ShapeDtypeStruct((T, D), F32), jax.ShapeDtypeStruct((D, D), BF16),
                   jax.ShapeDtypeStruct((SUBLANE, D), F32)],
        scratch_shapes=[pltpu.VMEM((D, D), F32)],
        compiler_params=_params(("arbitrary",)),
    )(hf, hb, g1, w_out, x1, tgt, mod, lnp)


def _rglru_bwd(fwd, dh, wa, wx, lam, reverse, name, comm=None):
    h, a_all, s_all, r_all, ig_all, xc_all = fwd
    T = h.shape[0]
    n = T // TS
    adj_rev = not reverse
    hprev_spec, hnext_spec = _halo_specs(TS, D, n, T, adj_rev)
    h_halo_spec = hnext_spec if reverse else hprev_spec

    def body(dh_ref, h_ref, hh_ref, a_ref, s_ref, r_ref, ig_ref, xc_ref, wa_ref, wx_ref, lam_ref,
             dxc_ref, dwa_ref, dwx_ref, vec_ref, a_s, l_s, carry, a_edge):
        i = pl.program_id(0)
        j = (n - 1 - i) if adj_rev else i

        @pl.when(i == 0)
        def _():
            carry[...] = jnp.zeros_like(carry)
            a_edge[...] = jnp.zeros_like(a_edge)
            dwa_ref[...] = jnp.zeros_like(dwa_ref)
            dwx_ref[...] = jnp.zeros_like(dwx_ref)
            vec_ref[...] = jnp.zeros_like(vec_ref)

        lam = lam_ref[...]
        sp = jnp.maximum(-lam, 0.0) + jnp.log(1.0 + jnp.exp(-jnp.abs(lam)))
        a, s = a_ref[...], s_ref[...]
        inv_s = lax.rsqrt(jnp.maximum(s * s, 1e-30))
        r, ig = r_ref[...].astype(F32), ig_ref[...].astype(F32)
        xcb = xc_ref[...]
        xc = xcb.astype(F32)

        rows = lax.broadcasted_iota(jnp.int32, (TS, D), 0)
        hcur = h_ref[...]
        if reverse:
            a_sh = jnp.where(rows == 0, a_edge[...], pltpu.roll(a, 1, 0))
            halo = jnp.where(j < n - 1, hh_ref[0:1, :], 0.0)
            h_nb = jnp.where(rows == TS - 1, halo, pltpu.roll(hcur, TS - 1, 0))
        else:
            a_sh = jnp.where(rows == TS - 1, a_edge[...], pltpu.roll(a, TS - 1, 0))
            halo = jnp.where(j > 0, hh_ref[SUBLANE - 1:SUBLANE, :], 0.0)
            h_nb = jnp.where(rows == 0, halo, pltpu.roll(hcur, 1, 0))
        a_s[...] = a_sh
        _scan_tile(a_s, dh_ref, l_s, carry, TS, adj_rev)
        a_edge[...] = a[TS - 1:TS, :] if reverse else a[0:1, :]

        lm = l_s[...]
        da = lm * h_nb
        di = lm * s * xc
        dxc = lm * s * ig
        ds = lm * ig * xc
        dlog_a = a * (da - ds * a * inv_s)
        dr = (-RG_C) * sp * dlog_a
        dsp = _rowsum((-RG_C) * r * dlog_a)
        dpr = dr * r * (1.0 - r)
        dpi = di * ig * (1.0 - ig)
        vec_ref[0:1, :] += _rowsum(dpr)
        vec_ref[1:2, :] += _rowsum(dpi)
        vec_ref[2:3, :] += dsp * (-_sigmoid(-lam))
        parts = []
        for hd in range(RNN_HEADS):
            sl = slice(hd * RNN_HD, (hd + 1) * RNN_HD)
            xh = xcb[:, sl]
            dprh = dpr[:, sl].astype(BF16)
            dpih = dpi[:, sl].astype(BF16)
            parts.append(_dot_nt(dprh, wa_ref[hd]) + _dot_nt(dpih, wx_ref[hd]))
            dwa_ref[hd] += _dot_tn(xh, dprh)
            dwx_ref[hd] += _dot_tn(xh, dpih)
        dxc_ref[...] = dxc + jnp.concatenate(parts, axis=-1)

    wspec = _full((RNN_HEADS, RNN_HD, RNN_HD))
    cur = _rev_tile(TS, D, n, adj_rev)
    return _fused_call(
        body, comm, (dh, h, h, a_all, s_all, r_all, ig_all, xc_all, wa, wx, lam), name=name, grid=(n,),
        in_specs=[cur, cur, h_halo_spec, cur, cur, cur, cur, cur, wspec, wspec, _full((1, D))],
        out_specs=[cur, wspec, wspec, _full((SUBLANE, D))],
        out_shape=[jax.ShapeDtypeStruct((T, D), F32),
                   jax.ShapeDtypeStruct((RNN_HEADS, RNN_HD, RNN_HD), F32),
                   jax.ShapeDtypeStruct((RNN_HEADS, RNN_HD, RNN_HD), F32),
                   jax.ShapeDtypeStruct((SUBLANE, D), F32)],
        scratch_shapes=[pltpu.VMEM((TS, D), F32)] * 2 + [pltpu.VMEM((1, D), F32)] * 2)


def _od_in_bwd(dxcf, dxcb, xr, dg1, x1, dx1p, mod, w_in, cw, comm=None):
    T = x1.shape[0]
    n = T // TMO
    slab = OD_IN // N_DEV
    prev_spec, next_spec = _halo_specs(TMO, D, n, T, False)

    def body(fp_ref, fc_ref, fn_ref, bp_ref, bc_ref, bn_ref, xr_ref, dg_ref, x1_ref, dxp_ref,
             mod_ref, w_ref, cw_ref, dx_ref, dwb_ref, vec_ref, dw_ref):
        i = pl.program_id(0)

        @pl.when(i == 0)
        def _():
            dw_ref[...] = jnp.zeros_like(dw_ref)
            vec_ref[...] = jnp.zeros_like(vec_ref)

        dcur = fc_ref[...] + bc_ref[...]
        dprev = jnp.where(i > 0, fp_ref[...] + bp_ref[...], 0.0)
        dnext = jnp.where(i < n - 1, fn_ref[...] + bn_ref[...], 0.0)
        dext = jnp.concatenate([dprev, dcur, dnext], axis=0)
        xr_v = xr_ref[...]
        cw_v = cw_ref[...]
        dxr = None
        for k in range(4):
            shifted = _shift_rows(dext, 2 - k, TMO)
            term = cw_v[k:k + 1, :] * shifted
            dxr = term if dxr is None else dxr + term
            vec_ref[k:k + 1, :] += _rowsum(shifted * xr_v)
        vec_ref[4:5, :] += _rowsum(dcur)
        dp = jnp.concatenate([dxr.astype(BF16), dg_ref[...]], axis=-1)
        x1v = x1_ref[...]
        scale1 = 1.0 + mod_ref[1:2, :]
        h1 = (x1v * scale1 + mod_ref[0:1, :]).astype(BF16)
        dh1 = _dot_nt(dp, w_ref[...])
        dw_ref[...] += _dot_tn(h1, dp)
        dx_ref[...] = dxp_ref[...] + dh1 * scale1
        vec_ref[5:6, :] += _rowsum(dh1)
        vec_ref[6:7, :] += _rowsum(dh1 * x1v)

        @pl.when(i == n - 1)
        def _():
            for j in range(N_DEV):
                dwb_ref[j] = dw_ref[:, j * slab:(j + 1) * slab].astype(BF16)

    t = _tile(TMO, D)
    return _fused_call(
        body, comm, (dxcf, dxcf, dxcf, dxcb, dxcb, dxcb, xr, dg1, x1, dx1p, mod, w_in, cw),
        name="od_in_bwd", grid=(n,),
        in_specs=[prev_spec, t, next_spec, prev_spec, t, next_spec, t, t, t, t,
                  _full((3, D)), _full((D, OD_IN)), _full((4, D))],
        out_specs=[t, _full((N_DEV, D, slab)), _full((SUBLANE, D))],
        out_shape=[jax.ShapeDtypeStruct((T, D), F32), jax.ShapeDtypeStruct((N_DEV, D, slab), BF16),
                   jax.ShapeDtypeStruct((SUBLANE, D), F32)],
        scratch_shapes=[pltpu.VMEM((D, OD_IN), F32)])


def _ev_out_bwd(dx1, z0, out0, y0, ycat, g0, w_out, mod, lnp):
    T = dx1.shape[0]

    def body(dx_ref, z_ref, out_ref, y0_ref, yc_ref, g_ref, w_ref, mod_ref, ln_ref,
             dxp_ref, dyc_ref, dg_ref, dwb_ref, vec_ref, dw_ref):
        i = pl.program_id(0)

        @pl.when(i == 0)
        def _():
            dw_ref[...] = jnp.zeros_like(dw_ref)
            vec_ref[...] = jnp.zeros_like(vec_ref)

        lng = ln_ref[0:1, :]
        _, xhat, rstd = _ln_fwd(z_ref[...], lng, ln_ref[1:2, :])
        dy = dx_ref[...]
        dz = _ln_bwd(dy, xhat, rstd, lng)
        vec_ref[0:1, :] += _rowsum(dy * xhat)
        vec_ref[1:2, :] += _rowsum(dy)
        vec_ref[2:3, :] += _rowsum(dz * out_ref[...].astype(F32))
        dout = (dz * mod_ref[2:3, :]).astype(BF16)
        dy0 = _dot_nt(dout, w_ref[...])
        dw_ref[...] += _dot_tn(y0_ref[...], dout)
        sg, dsg = _silu_and_grad(g_ref[...].astype(F32))
        dyc_ref[...] = (dy0 * sg).astype(BF16)
        dg_ref[...] = (dy0 * yc_ref[...].astype(F32) * dsg).astype(BF16)
        dxp_ref[...] = ALPHA * dz

        @pl.when(i == T // TMO - 1)
        def _():
            dwb_ref[...] = dw_ref[...].astype(BF16)

    t = _tile(TMO, D)
    return _pallas(
        body, name="ev_out_bwd", grid=(T // TMO,),
        in_specs=[t, t, t, t, t, t, _full((D, D)), _full((3, D)), _full((2, D))],
        out_specs=[t, t, t, _full((D, D)), _full((SUBLANE, D))],
        out_shape=[jax.ShapeDtypeStruct((T, D), F32), jax.ShapeDtypeStruct((T, D), BF16),
                   jax.ShapeDtypeStruct((T, D), BF16), jax.ShapeDtypeStruct((D, D), BF16),
                   jax.ShapeDtypeStruct((SUBLANE, D), F32)],
        scratch_shapes=[pltpu.VMEM((D, D), F32)],
        compiler_params=_params(("arbitrary",)),
    )(dx1, z0, out0, y0, ycat, g0, w_out, mod, lnp)


def _mix0_bwd(q, kvx, lse, dyc, ycat, su, sv, sink_l, bias, a128, gsum, sel, sg_lng, sg_lnb, sg_w, sg_bfull,
              rc, rs1, rs2, comm=None):
    T = q.shape[0]
    nb = T // BLK

    def body(q_ref, kp_ref, kc_ref, kn_ref, lse_ref, dyc_ref, yc_ref, su_ref, sv_ref, sink_ref, bias_ref, a_ref,
             gsum_ref, sel_ref, lng_ref, lnb_ref, w_ref, bfull_ref, c_ref, s1_ref, s2_ref,
             dq_ref, dkv_ref, dsu_ref, dsv_ref, dw_ref, dbt_ref, vec_ref, dsink_ref):
        n = pl.program_id(0)

        @pl.when(n == 0)
        def _():
            dkv_ref[...] = jnp.zeros_like(dkv_ref)
            dw_ref[...] = jnp.zeros_like(dw_ref)
            dbt_ref[...] = jnp.zeros_like(dbt_ref)
            vec_ref[...] = jnp.zeros_like(vec_ref)
            dsink_ref[...] = jnp.zeros_like(dsink_ref)

        band = pl.ds(pl.multiple_of(n * BLK + (TM - BLK), BLK), 3 * BLK)
        bias = _band_bias(bias_ref, n, nb)
        kvx = jnp.concatenate([kp_ref[...], kc_ref[...], kn_ref[...]], axis=0)
        bias2 = jnp.concatenate([bias, bias], axis=1)
        low = lax.broadcasted_iota(jnp.int32, (BLK, LANE), 1) < HEAD_DIM
        low2 = lax.broadcasted_iota(jnp.int32, (2 * BLK, LANE), 1) < HEAD_DIM
        sel = sel_ref[...]
        c, s1, s2 = c_ref[...], s1_ref[...], s2_ref[...]
        for kvh in range(2):
            t0, t1 = 2 * kvh, 2 * kvh + 1
            q2 = jnp.concatenate([_lane_tile(q_ref, t0), _lane_tile(q_ref, t1)], axis=0)
            do2 = jnp.concatenate([_lane_tile(dyc_ref, t0), _lane_tile(dyc_ref, t1)], axis=0)
            yc2 = jnp.concatenate([_lane_tile(yc_ref, t0), _lane_tile(yc_ref, t1)], axis=0)
            p_hi, p_lo = _split_bf16(do2.astype(F32) * yc2.astype(F32))
            deltas = _dot_nt(sel, p_hi) + _dot_nt(sel, p_lo)
            dkx = jnp.zeros((3 * BLK, LANE), F32)
            dvx = jnp.zeros((3 * BLK, LANE), F32)
            dq_acc = None
            for par in range(2):
                heads = (4 * kvh + par, 4 * kvh + 2 + par)
                kt = 2 * kvh + par
                ke = kvx[:, kt * LANE:(kt + 1) * LANE]
                ve = kvx[:, (4 + kt) * LANE:(5 + kt) * LANE]
                lse = jnp.concatenate([lse_ref[0, :, h * LANE:(h + 1) * LANE] for h in heads], axis=1)
                sk = jnp.concatenate([_lane_tile(sink_ref, h) for h in heads], axis=1)
                delta = deltas[par:par + 1, :]
                pt = jnp.exp(_dot_nt(ke, q2) + bias2 - lse)
                dst = (pt * (_dot_nt(ve, do2) - delta)).astype(BF16)
                sink_terms = jnp.exp(sk - lse) * delta
                for k, h in enumerate(heads):
                    dsink_ref[:, h * LANE:(h + 1) * LANE] += sink_terms[:, k * LANE:(k + 1) * LANE]
                part = _dot_tn(dst, ke)
                dq_acc = part if dq_acc is None else dq_acc + part
                mine = low2 if par == 0 else jnp.logical_not(low2)
                dkx = dkx + jnp.dot(dst, jnp.where(mine, q2, jnp.zeros_like(q2)), preferred_element_type=F32)
                dvx = dvx + jnp.dot(pt.astype(BF16), jnp.where(mine, do2, jnp.zeros_like(do2)),
                                    preferred_element_type=F32)
            for k, t in enumerate((t0, t1)):
                dq_t = dq_acc[k * BLK:(k + 1) * BLK] * (HEAD_DIM ** -0.5)
                dq_ref[:, t * LANE:(t + 1) * LANE] = _rope_bwd(dq_t, c, s1, s2).astype(BF16)
            dkv_ref[band, kvh * LANE:(kvh + 1) * LANE] += dkx
            dkv_ref[band, (2 + kvh) * LANE:(3 + kvh) * LANE] += dvx

        lng = lng_ref[...]
        xhat, rstd, vb, svm = _sg_core(sv_ref, lng, lnb_ref[...], a_ref, w_ref, bfull_ref)
        dy = dyc_ref[:, ATTN_W:].astype(F32)
        dsu_ref[...] = (dy * svm).astype(BF16)
        dsvm = dy * su_ref[...].astype(F32)
        d_hi, d_lo = _split_bf16(dsvm)
        gsum = gsum_ref[...]
        dbt_ref[...] += jnp.dot(d_hi, gsum, preferred_element_type=F32) + jnp.dot(d_lo, gsum,
                                                                                 preferred_element_type=F32)
        tiles = []
        for t in range(SG_W // LANE):
            tl = slice(t * LANE, (t + 1) * LANE)
            dt, v2 = d_hi[:, tl], vb[:, tl]
            dw_ref[2 * t] += _dot_nt(jnp.where(low, dt, jnp.zeros_like(dt)), v2)
            dw_ref[2 * t + 1] += _dot_nt(jnp.where(low, jnp.zeros_like(dt), dt), v2)
            tiles.append(jnp.where(low, _dot_tn(w_ref[2 * t], dt), _dot_tn(w_ref[2 * t + 1], dt)))
        dvgn = jnp.concatenate(tiles, axis=-1)
        vec_ref[0:1, :] += _rowsum(dvgn * xhat)
        vec_ref[1:2, :] += _rowsum(dvgn)
        dxh = dvgn * lng
        m1 = _group_mean(dxh, a_ref)
        m2 = _group_mean(dxh * xhat, a_ref)
        dsv_ref[...] = (rstd * (dxh - m1 - xhat * m2)).astype(BF16)

    return _fused_call(
        body, comm, (q, kvx, kvx, kvx, lse, dyc, ycat, su, sv, sink_l, bias, a128, gsum, sel, sg_lng, sg_lnb, sg_w,
                     sg_bfull, rc, rs1, rs2),
        name="mix0_bwd", grid=(nb,),
        in_specs=[_tile(BLK, ATTN_W)] + _band_specs(KVX_W, nb) + [
            pl.BlockSpec((1, 1, N_HEADS * LANE), lambda n: (n, 0, 0)), _tile(BLK, D), _tile(BLK, D),
            _tile(BLK, SG_W), _tile(BLK, SG_W), _full((1, N_HEADS * LANE)), _full((3 * BLK, LANE)),
            _full((2 * LANE, 2 * LANE)),_full((SG_W, LANE)), _full((SUBLANE, LANE)), _full((1, SG_W)), _full((1, SG_W)),
            _full((SG_GROUPS, BLK, BLK)), _full((BLK, SG_W)), _tile(BLK, LANE), _tile(BLK, LANE), _tile(BLK, LANE)],
        out_specs=[_tile(BLK, ATTN_W), _full((T + 2 * TM, 4 * LANE)), _tile(BLK, SG_W), _tile(BLK, SG_W),
                   _full((SG_GROUPS, BLK, BLK)), _full((BLK, LANE)), _full((SUBLANE, SG_W)),
                   _full((1, N_HEADS * LANE))],
        out_shape=[jax.ShapeDtypeStruct((T, ATTN_W), BF16), jax.ShapeDtypeStruct((T + 2 * TM, 4 * LANE), F32),
                   jax.ShapeDtypeStruct((T, SG_W), BF16), jax.ShapeDtypeStruct((T, SG_W), BF16),
                   jax.ShapeDtypeStruct((SG_GROUPS, BLK, BLK), F32), jax.ShapeDtypeStruct((BLK, LANE), F32),
                   jax.ShapeDtypeStruct((SUBLANE, SG_W), F32), jax.ShapeDtypeStruct((1, N_HEADS * LANE), F32)])


def _ev_in_bwd(dq, dkv, dsu, dsv, dg0, x, dxp, mod, w_in, rc, rs1, rs2, comm=None):
    T = x.shape[0]

    def body(dq_ref, dkv_ref, dsu_ref, dsv_ref, dg_ref, x_ref, dxp_ref, mod_ref, w_ref, c_ref, s1_ref, s2_ref,
             dx_ref, dwb_ref, vec_ref, dw_ref):
        i = pl.program_id(0)

        @pl.when(i == 0)
        def _():
            dw_ref[...] = jnp.zeros_like(dw_ref)
            vec_ref[...] = jnp.zeros_like(vec_ref)

        low = lax.broadcasted_iota(jnp.int32, (TM, LANE), 1) < HEAD_DIM

        def fold(j):
            t0 = dkv_ref[:, (2 * j) * LANE:(2 * j + 1) * LANE]
            t1 = dkv_ref[:, (2 * j + 1) * LANE:(2 * j + 2) * LANE]
            return jnp.where(low, t0 + pltpu.roll(t0, HEAD_DIM, 1), t1 + pltpu.roll(t1, HEAD_DIM, 1))

        dk = _rope_bwd(fold(0), c_ref[...], s1_ref[...], s2_ref[...]).astype(BF16)
        dp = jnp.concatenate([dq_ref[...], dk, fold(1).astype(BF16), dsu_ref[...], dsv_ref[...],
                              dg_ref[...]], axis=-1)
        xv = x_ref[...]
        scale0 = 1.0 + mod_ref[1:2, :]
        h0 = (xv * scale0 + mod_ref[0:1, :]).astype(BF16)
        dh0 = _dot(dp, w_ref[...])
        dw_ref[...] += _dot_tn(dp, h0)
        dx_ref[...] = dxp_ref[...] + dh0 * scale0
        vec_ref[0:1, :] += _rowsum(dh0)
        vec_ref[1:2, :] += _rowsum(dh0 * xv)

        @pl.when(i == T // TM - 1)
        def _():
            dwb_ref[...] = dw_ref[...].astype(BF16)

    t = _tile(TM, D)
    return _fused_call(
        body, comm, (dq, dkv, dsu, dsv, dg0, x, dxp, mod, w_in, rc, rs1, rs2), name="ev_in_bwd", grid=(T // TM,),
        in_specs=[_tile(TM, ATTN_W), pl.BlockSpec((TM, 4 * LANE), lambda i: (i + 1, 0)), _tile(TM, SG_W),
                  _tile(TM, SG_W), t, t, t,
                  _full((3, D)), _full((EV_IN, D)), _tile(TM, LANE), _tile(TM, LANE), _tile(TM, LANE)],
        out_specs=[t, _full((EV_IN, D)), _full((SUBLANE, D))],
        out_shape=[jax.ShapeDtypeStruct((T, D), F32), jax.ShapeDtypeStruct((EV_IN, D), BF16),
                   jax.ShapeDtypeStruct((SUBLANE, D), F32)],
        scratch_shapes=[pltpu.VMEM((EV_IN, D), F32)])


def _sum_slots(land_ref):
    g = land_ref[0].astype(F32)
    for i in range(1, land_ref.shape[0]):
        g = g + land_ref[i].astype(F32)
    return g


def _reduce_adam(items, name):
    R, C = items[0][1].shape
    rb = R
    if R > 512:
        for cand in (512, 256, 128, 64, 32, 16, 8):
            if R % cand == 0:
                rb = cand
                break
    n = len(items)

    def body(*refs):
        for k in range(n):
            l_ref, w_ref, m_ref, v_ref = refs[4 * k:4 * k + 4]
            g_ref, d_ref, nm_ref, nv_ref = refs[4 * n + 4 * k:4 * n + 4 * k + 4]
            g = _sum_slots(l_ref)
            g_ref[...] = g
            dlt, m2, v2 = _adam(w_ref[...], g, m_ref[...], v_ref[...])
            d_ref[...] = dlt
            nm_ref[...] = m2
            nv_ref[...] = v2

    t = pl.BlockSpec((rb, C), lambda i: (i, 0))
    shp = jax.ShapeDtypeStruct((R, C), F32)
    in_specs, operands = [], []
    for land, w, m, v in items:
        in_specs += [pl.BlockSpec((land.shape[0], rb, C), lambda i: (0, i, 0)), t, t, t]
        operands += [land, w, m, v]
    res = _pallas(
        body, name=name, grid=(R // rb,),
        in_specs=in_specs, out_specs=[t] * (4 * n), out_shape=[shp] * (4 * n),
        compiler_params=_params(("parallel",)),
    )(*operands)
    return [list(res[4 * k:4 * k + 4]) for k in range(n)]


def _tail_exchange(slabs, small):
    _, R, C = slabs.shape
    n_chips = N_DEV // 2
    gather = _GatherComm(small)
    ns = gather.n

    def body(*refs):
        slab_ref = refs[0]
        g_ins = refs[1:1 + ns]
        land_ref = refs[1 + ns]
        g_outs = refs[2 + ns:2 + 2 * ns]
        stage, part, s1_send, s1_recv, s2_send, s2_recv = refs[2 + 2 * ns:8 + 2 * ns]
        g_sems = refs[8 + 2 * ns:]
        x, y, c = _my_pos()
        chip = 2 * x + y
        gather.start(g_ins, g_outs, g_sems)

        swaps = [pltpu.make_async_remote_copy(
            src_ref=slab_ref.at[2 * k + (1 - c)], dst_ref=stage.at[k], send_sem=s1_send.at[k],
            recv_sem=s1_recv.at[k], device_id=(x, y, 1 - c), device_id_type=MESH) for k in range(n_chips)]
        for cp in swaps:
            cp.start()
        for cp in swaps:
            cp.wait()
        for k in range(n_chips):
            part[k] = (slab_ref[2 * k + c].astype(F32) + stage[k].astype(F32)).astype(BF16)

        gather.mid(g_ins, g_outs, g_sems)

        sends = []
        for r in range(1, n_chips):
            px = (1 - x) if (r & 2) else x
            py = (1 - y) if (r & 1) else y
            sends.append(pltpu.make_async_remote_copy(
                src_ref=part.at[2 * px + py], dst_ref=land_ref.at[chip], send_sem=s2_send.at[r - 1],
                recv_sem=s2_recv.at[r - 1], device_id=(px, py, c), device_id_type=MESH))
        for cp in sends:
            cp.start()
        land_ref[chip] = part[chip]
        for cp in sends:
            cp.wait()
        gather.finish(g_ins, g_outs, g_sems)

    any_spec = pl.BlockSpec(memory_space=pl.ANY)
    vmem_spec = pl.BlockSpec(memory_space=pltpu.VMEM)
    res = _pallas(
        body, name="tail_exchange",
        out_shape=[jax.ShapeDtypeStruct((n_chips, R, C), BF16)] + gather.out_shapes(),
        in_specs=[vmem_spec] + [any_spec] * ns, out_specs=[vmem_spec] + [any_spec] * ns,
        scratch_shapes=[pltpu.VMEM((n_chips, R, C), BF16), pltpu.VMEM((n_chips, R, C), BF16),
                        pltpu.SemaphoreType.DMA((n_chips,)), pltpu.SemaphoreType.DMA((n_chips,)),
                        pltpu.SemaphoreType.DMA((n_chips - 1,)), pltpu.SemaphoreType.DMA((n_chips - 1,))]
        + gather.sems(),
        compiler_params=pltpu.CompilerParams(vmem_limit_bytes=VMEM_LIMIT),
    )(slabs, *gather.arrs)
    return res[0], list(res[1:])


def _slots_adam(items, name):
    zeros3 = (0, 0, 0)
    in_specs, out_specs, out_shape, operands = [], [], [], []
    for land, w, m, v in items:
        inner = w.shape[-3:]
        if w.ndim == 5:
            lspec = pl.BlockSpec((N_DEV, 1) + inner, lambda i: (0, i) + zeros3)
            wspec = pl.BlockSpec((1, 1) + inner, lambda i: (0, i) + zeros3)
        else:
            lspec = pl.BlockSpec((N_DEV,) + inner, lambda i: (0,) + zeros3)
            wspec = pl.BlockSpec((1,) + inner, lambda i: (0,) + zeros3)
        in_specs += [lspec, wspec, wspec, wspec]
        out_specs += [wspec] * 4
        out_shape += [jax.ShapeDtypeStruct(w.shape, F32)] * 4
        operands += [land, w, m, v]
    n = len(items)

    def body(*refs):
        for k, (_, w, _, _) in enumerate(items):
            l_ref, w_ref, m_ref, v_ref = refs[4 * k:4 * k + 4]
            outs = refs[4 * n + 4 * k:4 * n + 4 * k + 4]
            at = (0, 0) if w.ndim == 5 else (0,)

            def update(l_ref=l_ref, w_ref=w_ref, m_ref=m_ref, v_ref=v_ref, outs=outs, at=at):
                g = l_ref[(0,) + at[1:]].astype(F32)
                for i in range(1, N_DEV):
                    g = g + l_ref[(i,) + at[1:]].astype(F32)
                dlt, m2, v2 = _adam(w_ref[at], g, m_ref[at], v_ref[at])
                for o_ref, val in zip(outs, (g, dlt, m2, v2)):
                    o_ref[at] = val

            if w.ndim == 5:
                update()
            else:
                pl.when(pl.program_id(0) == 0)(update)

    res = _pallas(
        body, name=name, grid=(2,),
        in_specs=in_specs, out_specs=out_specs, out_shape=out_shape,
        compiler_params=_params(("arbitrary",)),
    )(*operands)
    return [list(res[4 * k:4 * k + 4]) for k in range(n)]


SMALL_PARAMS = ("ln_g", "ln_b", "ev_sg_ln_g", "ev_sg_ln_b", "ev_sink", "ev_sg_b",
                "od_conv_w", "od_conv_b", "od_b_a", "od_b_x", "od_lam")


def _small_update(ga, gc, gd, gf, gb, ge, gsink, gbt, params):
    names = list(SMALL_PARAMS)
    flat = [a for nm in names for a in params[nm]]
    n_g = 8

    def body(*refs):
        ga_ref, gc_ref, gd_ref, gf_ref, gb_ref, ge_ref, gs_ref, gbt_ref = refs[:n_g]
        prm = refs[n_g:n_g + 3 * len(names)]
        loss_ref = refs[n_g + 3 * len(names)]
        outs = refs[n_g + 3 * len(names) + 1:]

        def ssum(ref):
            acc = ref[0]
            for i in range(1, N_DEV):
                acc = acc + ref[i]
            return acc

        a, cc, dd, ff, bb, ee = ssum(ga_ref), ssum(gc_ref), ssum(gd_ref), ssum(gf_ref), ssum(gb_ref), ssum(ge_ref)
        loss_ref[...] = a[3:4, 0:LANE]
        me = _slot(*_my_pos())

        def mine(rows):
            acc = jnp.zeros((rows.shape[0], LANE), F32)
            for j in range(N_DEV):
                acc = acc + jnp.where(me == j, rows[:, j * LANE:(j + 1) * LANE], 0.0)
            return acc

        sink_terms = ssum(gs_ref)
        lane8 = lax.broadcasted_iota(jnp.int32, (1, N_HEADS), 1)
        g_sink = jnp.zeros((1, N_HEADS), F32)
        for h in range(N_HEADS):
            tot = -jnp.sum(sink_terms[:, h * LANE:(h + 1) * LANE], axis=1, keepdims=True)
            g_sink = jnp.where(lane8 == h, tot, g_sink)
        grads = dict(
            ln_g=jnp.concatenate([dd[0:1], a[0:1]], axis=0), ln_b=jnp.concatenate([dd[1:2], a[1:2]], axis=0),
            ev_sg_ln_g=ee[0:1], ev_sg_ln_b=ee[1:2], ev_sink=g_sink,
            ev_sg_b=jnp.transpose(ssum(gbt_ref))[0:SG_GROUPS, :],
            od_conv_w=mine(cc[0:4]), od_conv_b=mine(cc[4:5]),
            od_b_a=mine(jnp.concatenate([ff[0:1], bb[0:1]], axis=0)),
            od_b_x=mine(jnp.concatenate([ff[1:2], bb[1:2]], axis=0)),
            od_lam=mine(jnp.concatenate([ff[2:3], bb[2:3]], axis=0)))
        for k, nm in enumerate(names):
            w_ref, m_ref, v_ref = prm[3 * k:3 * k + 3]
            at = (0,) if len(w_ref.shape) == 3 else ()
            g = grads[nm]
            dlt, m2, v2 = _adam(w_ref[at] if at else w_ref[...], g, m_ref[at] if at else m_ref[...],
                                v_ref[at] if at else v_ref[...])
            for o_ref, val in zip(outs[4 * k:4 * k + 4], (g, dlt, m2, v2)):
                if at:
                    o_ref[at] = val
                else:
                    o_ref[...] = val

    gathered = [ga, gc, gd, gf, gb, ge, gsink, gbt]
    out_shape = [jax.ShapeDtypeStruct((1, LANE), F32)]
    for nm in names:
        out_shape += [jax.ShapeDtypeStruct(params[nm][0].shape, F32)] * 4
    return _pallas(
        body, name="small_update", grid=(1,),
        in_specs=[_full(a.shape) for a in gathered + flat],
        out_specs=[_full(s.shape) for s in out_shape], out_shape=out_shape,
        compiler_params=_params(("arbitrary",)),
    )(*gathered, *flat)


VEC_ROWS = 16
VEC_LAYOUT = (("od_conv_w", 4), ("od_conv_b", 1), ("od_b_a", 2), ("od_b_x", 2), ("od_lam", 2))


def _to_slabs(full, cols_per):
    R = full.shape[0]
    return full.reshape(R, N_DEV, cols_per).transpose(1, 0, 2)


def _from_slabs(slabs):
    n, R, cp = slabs.shape
    return slabs.transpose(1, 0, 2).reshape(R, n * cp)


def kernel(x, c, positions, ada_w, ada_b, ln_g, ln_b, ev_w_in, ev_w_out, ev_sink, ev_sg_ln_g, ev_sg_ln_b, ev_sg_w, ev_sg_b, od_w_in, od_conv_w, od_conv_b, od_w_a, od_b_a, od_w_x, od_b_x, od_lam, od_w_out, loss_target, m_ada_w, m_ada_b, m_ln_g, m_ln_b, m_ev_w_in, m_ev_w_out, m_ev_sink, m_ev_sg_ln_g, m_ev_sg_ln_b, m_ev_sg_w, m_ev_sg_b, m_od_w_in, m_od_conv_w, m_od_conv_b, m_od_w_a, m_od_b_a, m_od_w_x, m_od_b_x, m_od_lam, m_od_w_out, v_ada_w, v_ada_b, v_ln_g, v_ln_b, v_ev_w_in, v_ev_w_out, v_ev_sink, v_ev_sg_ln_g, v_ev_sg_ln_b, v_ev_sg_w, v_ev_sg_b, v_od_w_in, v_od_conv_w, v_od_conv_b, v_od_w_a, v_od_b_a, v_od_w_x, v_od_b_x, v_od_lam, v_od_w_out):
    T = x.shape[1]
    me = _slot(*_my_pos())
    xs = x.reshape(T, D)
    tgt = loss_target.reshape(T, D)

    c_all, mod_all, g_vec, (g_ev_in,), (s_ev_out, s_od_in, s_od_out, sg_w, wa, wx) = _head_gather(
        c, ada_w, [ev_w_in[0].T.astype(BF16)],
        [ev_w_out[0], od_w_in[0], od_w_out[0], ev_sg_w[0], od_w_a[0], od_w_x[0]],
        [od_conv_w, od_conv_b, od_b_a, od_b_x, od_lam])
    c_all = c_all.reshape(N_DEV, D)
    w_ev_in = g_ev_in.reshape(EV_IN, D)
    vec_full = _from_slabs(g_vec)
    cw, cb = vec_full[0:4], vec_full[4:5]
    ba, bx, lam = vec_full[5:7], vec_full[7:9], vec_full[9:11]
    mod_mine = lax.dynamic_index_in_dim(mod_all, me, axis=2, keepdims=False)
    mod = mod_mine.transpose(1, 0, 2).reshape(2, 3 * D) + ada_b
    mod0 = mod[0].reshape(3, D)
    mod1 = mod[1].reshape(3, D)

    half = 8
    inv_freq = jnp.power(jnp.float32(ROPE_THETA), -jnp.arange(half, dtype=F32) / half)
    ang = positions.reshape(T).astype(F32)[:, None] * inv_freq
    cos_t = jnp.tile(jnp.cos(ang), (1, LANE // half))
    sin_t = jnp.tile(jnp.sin(ang), (1, LANE // half))
    l64 = jnp.arange(LANE) % HEAD_DIM
    rc = jnp.where(l64 < 2 * half, cos_t, 1.0)
    rs1 = jnp.where(l64 < half, -sin_t, 0.0)
    rs2 = jnp.where((l64 >= half) & (l64 < 2 * half), sin_t, 0.0)

    ln0 = jnp.stack([ln_g[0], ln_b[0]])
    ln1 = jnp.stack([ln_g[1], ln_b[1]])
    sg_lng = ev_sg_ln_g
    sg_lnb = ev_sg_ln_b
    sg_bfull = jnp.repeat(ev_sg_b[0].T, SG_DIM, axis=1)
    sink_l = jnp.repeat(ev_sink, LANE, axis=1)
    kj = jnp.arange(3 * BLK)[:, None]
    qi = jnp.arange(BLK)[None, :]
    band_bias = jnp.where(jnp.abs(kj - BLK - qi) <= BLK, 0.0, NEG_INF).astype(F32)
    lanes = jnp.arange(LANE)
    lanes2 = jnp.arange(2 * LANE)
    a128 = jnp.where(lanes2[:, None] // SG_DIM == lanes2[None, :] // SG_DIM, 1.0 / SG_DIM, 0.0).astype(BF16)
    gsum = (jnp.arange(SG_W)[:, None] // SG_DIM == lanes[None, :]).astype(BF16)
    sel = (jnp.arange(SUBLANE)[:, None] == lanes[None, :] // HEAD_DIM).astype(BF16)

    (q, kvx, su, sv, g0), _ = _ev_in(xs, mod0, w_ev_in, rc, rs1, rs2)
    (ycat, y0, lse), (g_ev_out, g_od_in, g_od_out) = _mix0_fwd(
        q, kvx, su, sv, g0, sink_l, band_bias, a128, sg_lng, sg_lnb, sg_w, sg_bfull,
        _GatherComm([s_ev_out, s_od_in, s_od_out], mid_frac=0.75))
    w_ev_out = g_ev_out.reshape(D, D)
    w_od_in = _from_slabs(g_od_in)
    w_od_out = g_od_out.reshape(D, D)
    out0, z0, x1 = _ev_out(y0, w_ev_out, xs, mod0, ln0)
    xr, g1 = _od_in(x1, mod1, w_od_in)
    fwd_f = _rglru_fwd(xr, cw, cb, wa[0], wx[0], ba[0:1], bx[0:1], lam[0:1], False, "rglru_fwd_f")
    fwd_b = _rglru_fwd(xr, cw, cb, wa[1], wx[1], ba[1:2], bx[1:2], lam[1:2], True, "rglru_fwd_b")
    dh, dg1, dx1p, d_od_out, vec_a = _od_out(fwd_f[0], fwd_b[0], g1, w_od_out, x1, tgt, mod1, ln1)

    (dxcf, dwa_f, dwx_f, vec_f), (l_od_out,) = _rglru_bwd(
        fwd_f, dh, wa[0], wx[0], lam[0:1], False, "rglru_bwd_f",
        _ExchangeComm([d_od_out.reshape(N_DEV, D // N_DEV, D)]))
    (dxcb, dwa_b, dwx_b, vec_b), _ = _rglru_bwd(fwd_b, dh, wa[1], wx[1], lam[1:2], True, "rglru_bwd_b")
    (dx1, d_od_in, vec_c), (a_wa, a_wx) = _od_in_bwd(
        dxcf, dxcb, xr, dg1, x1, dx1p, mod1, w_od_in, cw,
        _GatherComm([jnp.stack([dwa_f, dwa_b]).astype(BF16), jnp.stack([dwx_f, dwx_b]).astype(BF16)],
                    mid_frac=0.75))
    dxp, dyc, dg0, d_ev_out, vec_d = _ev_out_bwd(dx1, z0, out0, y0, ycat, g0, w_ev_out, mod0, ln0)
    (dq, dkv, dsu, dsv, d_sg_w, d_sg_bt, vec_e, d_sink_l), (l_od_in, l_ev_out) = _mix0_bwd(
        q, kvx, lse, dyc, ycat, su, sv, sink_l, band_bias, a128, gsum, sel, sg_lng, sg_lnb, sg_w, sg_bfull,
        rc, rs1, rs2, _ExchangeComm([d_od_in, d_ev_out.reshape(N_DEV, D // N_DEV, D)]))
    (grad_x, d_ev_in, vec_g), _ = _ev_in_bwd(dq, dkv, dsu, dsv, dg0, xs, dxp, mod0, w_ev_in, rc, rs1, rs2)

    l_ev_in, (ga, gc, gd, gf, gb, gg, ge, gsink, gbt, a_sgw) = _tail_exchange(
        d_ev_in.reshape(N_DEV, EV_IN // N_DEV, D),
        [vec_a, vec_c, vec_d, vec_f, vec_b, vec_g, vec_e, d_sink_l, d_sg_bt, d_sg_w.astype(BF16)])

    dmod_all = jnp.stack([jnp.concatenate([gg[:, 0], gg[:, 1], gd[:, 2]], axis=-1),
                          jnp.concatenate([gc[:, 5], gc[:, 6], ga[:, 2]], axis=-1)], axis=1)
    cols = ada_w.shape[2]
    dmod_cols = lax.dynamic_slice_in_dim(dmod_all, me * cols, cols, axis=2).transpose(1, 0, 2)
    (g_ada_w, d_ada_w, nm_ada_w, nv_ada_w, g_ada_b, d_ada_b, nm_ada_b, nv_ada_b) = _ada_update(
        c_all, dmod_cols, dmod_all, ada_w, m_ada_w, v_ada_w, ada_b, m_ada_b, v_ada_b)

    res = dict(ada_w=[g_ada_w, d_ada_w, nm_ada_w, nv_ada_w], ada_b=[g_ada_b, d_ada_b, nm_ada_b, nv_ada_b])
    (r_ev_in,) = _reduce_adam([(l_ev_in, ev_w_in[0].T, m_ev_w_in[0].T, v_ev_w_in[0].T)], "adam_ev_w_in")
    res["ev_w_in"] = [a.T[None] for a in r_ev_in]
    (r_od_in,) = _reduce_adam([(l_od_in, od_w_in[0], m_od_w_in[0], v_od_w_in[0])], "adam_od_w_in")
    r_ev_out, r_od_out = _reduce_adam([(l_ev_out, ev_w_out[0], m_ev_w_out[0], v_ev_w_out[0]),
                                       (l_od_out, od_w_out[0], m_od_w_out[0], v_od_w_out[0])], "adam_w_out")
    for name, r in (("od_w_in", r_od_in), ("ev_w_out", r_ev_out), ("od_w_out", r_od_out)):
        res[name] = [a[None] for a in r]
    res["od_w_a"], res["od_w_x"], res["ev_sg_w"] = _slots_adam(
        [(a_wa, od_w_a, m_od_w_a, v_od_w_a), (a_wx, od_w_x, m_od_w_x, v_od_w_x),
         (a_sgw, ev_sg_w, m_ev_sg_w, v_ev_sg_w)], "adam_gates")
    small = dict(ln_g=(ln_g, m_ln_g, v_ln_g), ln_b=(ln_b, m_ln_b, v_ln_b),
                 ev_sg_ln_g=(ev_sg_ln_g, m_ev_sg_ln_g, v_ev_sg_ln_g),
                 ev_sg_ln_b=(ev_sg_ln_b, m_ev_sg_ln_b, v_ev_sg_ln_b),
                 ev_sink=(ev_sink, m_ev_sink, v_ev_sink), ev_sg_b=(ev_sg_b, m_ev_sg_b, v_ev_sg_b),
                 od_conv_w=(od_conv_w, m_od_conv_w, v_od_conv_w), od_conv_b=(od_conv_b, m_od_conv_b, v_od_conv_b),
                 od_b_a=(od_b_a, m_od_b_a, v_od_b_a), od_b_x=(od_b_x, m_od_b_x, v_od_b_x),
                 od_lam=(od_lam, m_od_lam, v_od_lam))
    small_out = _small_update(ga, gc, gd, gf, gb, ge, gsink, gbt, small)
    loss = small_out[0][0, 0]
    for k, name in enumerate(SMALL_PARAMS):
        res[name] = small_out[1 + 4 * k:5 + 4 * k]

    order = ["ada_w", "ada_b", "ln_g", "ln_b", "ev_w_in", "ev_w_out", "ev_sink", "ev_sg_ln_g", "ev_sg_ln_b",
             "ev_sg_w", "ev_sg_b", "od_w_in", "od_conv_w", "od_conv_b", "od_w_a", "od_b_a", "od_w_x", "od_b_x",
             "od_lam", "od_w_out"]
    outs = [loss, grad_x.reshape(1, T, D)]
    for kind in range(4):
        outs += [res[name][kind] for name in order]
    return tuple(outs)
```

```python
import functools

import jax
import jax.numpy as jnp
from jax import lax
from jax.experimental import pallas as pl
from jax.experimental.pallas import tpu as pltpu

F32 = jnp.float32
BF16 = jnp.bfloat16

N_DEV = 8
D = 1024
N_HEADS = 8
HEAD_DIM = 64
KV_WIDTH = 128
ATTN_W = 512
SG_W = 512
SG_GROUPS = 8
SG_DIM = 64
BLK = 128
KVX_W = 1024
EV_IN = 2816
OD_IN = 2048
RNN_HEADS = 8
RNN_HD = 128
ALPHA = 4.0 ** 0.25
LN_EPS = 1e-5
NEG_INF = -1e30
RG_C = 8.0
ROPE_THETA = 500000.0
LR, B1, B2, EPS, WD, STEP = 0.001, 0.9, 0.999, 1e-08, 0.01, 10

LANE = 128
SUBLANE = 8
TM = 256
TMF = 512
TMO = 512
TS = 256
VMEM_LIMIT = 56 * 1024 * 1024

MESH = pl.DeviceIdType.MESH


def _pallas(body, **kw):
    return pl.pallas_call(body, **kw)


def _params(sem, vmem=VMEM_LIMIT):
    return pltpu.CompilerParams(dimension_semantics=sem, vmem_limit_bytes=vmem)


def _sigmoid(x):
    return 0.5 * jnp.tanh(0.5 * x) + 0.5


def _silu_and_grad(x):
    s = _sigmoid(x)
    return x * s, s * (1.0 + x * (1.0 - s))


def _dot(a, b):
    return jnp.dot(a.astype(BF16), b.astype(BF16), preferred_element_type=F32)


def _dot_nt(a, b):
    return lax.dot_general(a.astype(BF16), b.astype(BF16), (((1,), (1,)), ((), ())), preferred_element_type=F32)


def _dot_tn(a, b):
    return lax.dot_general(a.astype(BF16), b.astype(BF16), (((0,), (0,)), ((), ())), preferred_element_type=F32)


def _ln_fwd(z, g, b):
    mu = jnp.mean(z, axis=-1, keepdims=True)
    zc = z - mu
    var = jnp.mean(zc * zc, axis=-1, keepdims=True)
    rstd = lax.rsqrt(var + LN_EPS)
    xhat = zc * rstd
    return xhat * g + b, xhat, rstd


def _ln_bwd(dy, xhat, rstd, g):
    dxh = dy * g
    m1 = jnp.mean(dxh, axis=-1, keepdims=True)
    m2 = jnp.mean(dxh * xhat, axis=-1, keepdims=True)
    return rstd * (dxh - m1 - xhat * m2)


def _rowsum(v):
    return jnp.sum(v, axis=0, keepdims=True)


def _rope_fwd(t, c, s1, s2):
    return t * c + pltpu.roll(t, LANE - 8, 1) * s1 + pltpu.roll(t, 8, 1) * s2


def _rope_bwd(d, c, s1, s2):
    return d * c + pltpu.roll(d * s1, 8, 1) + pltpu.roll(d * s2, LANE - 8, 1)


def _adam(w, g, m, v):
    m2 = B1 * m + (1.0 - B1) * g
    v2 = B2 * v + (1.0 - B2) * (g * g)
    m_hat = m2 / (1.0 - B1 ** STEP)
    v_hat = v2 / (1.0 - B2 ** STEP)
    delta = -LR * (m_hat / (jnp.sqrt(v_hat) + EPS) + WD * w)
    return delta, m2, v2


def _tile(rows, width):
    return pl.BlockSpec((rows, width), lambda i: (i, 0))


def _full(shape):
    zeros = (0,) * len(shape)
    return pl.BlockSpec(shape, lambda i: zeros)


def _rev_tile(rows, width, n, reverse):
    if reverse:
        return pl.BlockSpec((rows, width), lambda i: (n - 1 - i, 0))
    return pl.BlockSpec((rows, width), lambda i: (i, 0))


def _halo_specs(rows, width, n, total_rows, reverse):
    per = rows // SUBLANE
    last = total_rows // SUBLANE - 1

    def tile_of(i):
        return (n - 1 - i) if reverse else i

    prev = pl.BlockSpec((SUBLANE, width), lambda i: (jnp.maximum(tile_of(i) * per - 1, 0), 0))
    nxt = pl.BlockSpec((SUBLANE, width), lambda i: (jnp.minimum((tile_of(i) + 1) * per, last), 0))
    return prev, nxt


def _my_pos():
    return lax.axis_index("x"), lax.axis_index("y"), lax.axis_index("c")


def _slot(px, py, pc):
    return 4 * px + 2 * py + pc


class _GatherComm:
    has_mid = True

    def __init__(self, arrs, mid_frac=0.5):
        self.arrs = list(arrs)
        self.n = len(self.arrs)
        self.mid_frac = mid_frac

    def out_shapes(self):
        return [jax.ShapeDtypeStruct((N_DEV,) + a.shape, a.dtype) for a in self.arrs]

    def sems(self):
        return [pltpu.SemaphoreType.DMA((7 * self.n,)), pltpu.SemaphoreType.DMA((7 * self.n,)),
                pltpu.SemaphoreType.DMA((self.n,))]

    def _parts(self, ins, outs, sems):
        send_sems, recv_sems, local_sems = sems
        x, y, c = _my_pos()
        me, sibling = (x, y, c), (x, y, 1 - c)
        chips = [(1 - x, y), (x, 1 - y), (1 - x, 1 - y)]

        def copy(a, k, block, to, src=None):
            dst = outs[a].at[_slot(*block)]
            return pltpu.make_async_remote_copy(
                src_ref=dst if src is None else src, dst_ref=dst,
                send_sem=send_sems.at[a * 7 + k], recv_sem=recv_sems.at[a * 7 + k],
                device_id=to, device_id_type=MESH)

        local = [pltpu.make_async_copy(ins[a], outs[a].at[_slot(*me)], local_sems.at[a]) for a in range(self.n)]
        first = []
        for a in range(self.n):
            first.append(copy(a, 0, me, sibling, src=ins[a]))
            first += [copy(a, 1 + j, me, (*chip, c), src=ins[a]) for j, chip in enumerate(chips)]
        ici_in = [copy(a, 1 + j, (*chip, c), me) for j, chip in enumerate(chips) for a in range(self.n)]
        passed = [copy(a, 4 + j, (*chip, c), sibling) for j, chip in enumerate(chips) for a in range(self.n)]
        d2d_in = []
        for a in range(self.n):
            d2d_in.append(copy(a, 0, sibling, me))
            d2d_in += [copy(a, 4 + j, (*chip, 1 - c), me) for j, chip in enumerate(chips)]
        return local, first, ici_in, passed, d2d_in

    def start(self, ins, outs, sems):
        local, first, _, _, _ = self._parts(ins, outs, sems)
        for cp in local + first:
            cp.start()

    def mid(self, ins, outs, sems):
        _, _, ici_in, passed, _ = self._parts(ins, outs, sems)
        for arrived, fw in zip(ici_in, passed):
            arrived.wait_recv()
            fw.start()

    def finish(self, ins, outs, sems):
        local, first, _, passed, d2d_in = self._parts(ins, outs, sems)
        for cp in d2d_in:
            cp.wait_recv()
        for cp in first + passed:
            cp.wait_send()
        for cp in local:
            cp.wait()


class _ExchangeComm:
    has_mid = False

    def __init__(self, arrs):
        self.arrs = list(arrs)
        self.n = len(self.arrs)

    def out_shapes(self):
        return [jax.ShapeDtypeStruct(a.shape, a.dtype) for a in self.arrs]

    def sems(self):
        return [pltpu.SemaphoreType.DMA((7 * self.n,)), pltpu.SemaphoreType.DMA((7 * self.n,)),
                pltpu.SemaphoreType.DMA((self.n,))]

    def _copies(self, ins, outs, sems):
        send_sems, recv_sems, local_sems = sems
        x, y, c = _my_pos()
        mine = _slot(x, y, c)
        copies = [pltpu.make_async_copy(ins[a].at[mine], outs[a].at[mine], local_sems.at[a]) for a in range(self.n)]
        for k in range(1, N_DEV):
            px = (1 - x) if (k & 4) else x
            py = (1 - y) if (k & 2) else y
            pc = (1 - c) if (k & 1) else c
            for a in range(self.n):
                copies.append(pltpu.make_async_remote_copy(
                    src_ref=ins[a].at[_slot(px, py, pc)], dst_ref=outs[a].at[mine],
                    send_sem=send_sems.at[a * 7 + k - 1], recv_sem=recv_sems.at[a * 7 + k - 1],
                    device_id=(px, py, pc), device_id_type=MESH))
        return copies

    def start(self, ins, outs, sems):
        for cp in self._copies(ins, outs, sems):
            cp.start()

    def finish(self, ins, outs, sems):
        for cp in self._copies(ins, outs, sems):
            cp.wait()


def _fused_call(body, comm, operands, *, name, grid, in_specs, out_specs, out_shape, scratch_shapes=(),
                semantics=("arbitrary",)):
    n_in, n_out, n_scr = len(in_specs), len(out_specs), len(scratch_shapes)
    if comm is None:
        res = _pallas(body, name=name, grid=grid, in_specs=list(in_specs), out_specs=list(out_specs),
                      out_shape=list(out_shape), scratch_shapes=list(scratch_shapes),
                      compiler_params=_params(semantics))(*operands)
        return list(res), []
    k = comm.n
    steps = grid[0]

    def wrapped(*refs):
        ins, cins = refs[:n_in], refs[n_in:n_in + k]
        outs = refs[n_in + k:n_in + k + n_out]
        couts = refs[n_in + k + n_out:n_in + 2 * k + n_out]
        rest = refs[n_in + 2 * k + n_out:]
        scratch, sems = rest[:n_scr], rest[n_scr:]
        i = pl.program_id(0)

        @pl.when(i == 0)
        def _():
            comm.start(cins, couts, sems)

        body(*ins, *outs, *scratch)

        if comm.has_mid:
            @pl.when(i == int(steps * comm.mid_frac))
            def _():
                comm.mid(cins, couts, sems)

        @pl.when(i == steps - 1)
        def _():
            comm.finish(cins, couts, sems)

    any_spec = pl.BlockSpec(memory_space=pl.ANY)
    res = _pallas(wrapped, name=name, grid=grid, in_specs=list(in_specs) + [any_spec] * k,
                  out_specs=list(out_specs) + [any_spec] * k, out_shape=list(out_shape) + comm.out_shapes(),
                  scratch_shapes=list(scratch_shapes) + comm.sems(),
                  compiler_params=_params(("arbitrary",)))(*operands, *comm.arrs)
    return list(res[:n_out]), list(res[n_out:])


def _head_gather(c, ada_w, big, to_cast, vec_parts):
    cols = ada_w.shape[2]
    g_c, g_big = _GatherComm([c]), _GatherComm(big)
    g_mod = _GatherComm([jax.ShapeDtypeStruct((2, N_DEV, cols), F32)])
    g_vec = _GatherComm([jax.ShapeDtypeStruct((VEC_ROWS, LANE), F32)])
    nb, nc, nv = g_big.n, len(to_cast), len(vec_parts)

    def body(*refs):
        c_ref, w_ref = refs[0], refs[1]
        vec_in = refs[2:2 + nv]
        cast_in = refs[2 + nv:2 + nv + nc]
        big_in = refs[2 + nv + nc:2 + nv + nc + nb]
        outs = refs[2 + nv + nc + nb:]
        c_all_ref, mod_all_ref, vec_all_ref = outs[0], outs[1], outs[2]
        cast_out = outs[3:3 + nc]
        big_out = outs[3 + nc:3 + nc + nb]
        part_ref, pack_ref = outs[3 + nc + nb], outs[4 + nc + nb]
        sems = outs[5 + nc + nb:]
        s_c, s_mod, s_big, s_vec = sems[0:3], sems[3:6], sems[6:9], sems[9:12]
        g_c.start([c_ref], [c_all_ref], s_c)
        g_big.start(big_in, big_out, s_big)
        pack_ref[...] = jnp.zeros_like(pack_ref)
        row = 0
        for ref, (_, nrows) in zip(vec_in, VEC_LAYOUT):
            pack_ref[row:row + nrows, :] = ref[0] if len(ref.shape) == 3 else ref[...]
            row += nrows
        g_vec.start([pack_ref], [vec_all_ref], s_vec)
        g_c.mid([c_ref], [c_all_ref], s_c)
        g_c.finish([c_ref], [c_all_ref], s_c)
        cv = c_all_ref[:, 0, :]
        cond = cv * _sigmoid(cv)
        for l in range(2):
            part_ref[l] = _dot(cond, w_ref[l])
        g_mod.start([part_ref], [mod_all_ref], s_mod)
        for src, dst in zip(cast_in, cast_out):
            dst[...] = src[...].astype(BF16)
        for g, ins, outs_, sm in ((g_vec, [pack_ref], [vec_all_ref], s_vec), (g_mod, [part_ref], [mod_all_ref], s_mod),
                                  (g_big, big_in, big_out, s_big)):
            g.mid(ins, outs_, sm)
            g.finish(ins, outs_, sm)

    any_spec = pl.BlockSpec(memory_space=pl.ANY)
    vmem_spec = pl.BlockSpec(memory_space=pltpu.VMEM)
    res = _pallas(
        body, name="head_gather",
        out_shape=(g_c.out_shapes() + g_mod.out_shapes() + g_vec.out_shapes()
                   + [jax.ShapeDtypeStruct(a.shape, BF16) for a in to_cast] + g_big.out_shapes()),
        in_specs=[vmem_spec] * (2 + nv + nc) + [any_spec] * nb,
        out_specs=[vmem_spec] * (3 + nc) + [any_spec] * nb,
        scratch_shapes=[pltpu.VMEM((2, N_DEV, cols), F32), pltpu.VMEM((VEC_ROWS, LANE), F32)]
        + g_c.sems() + g_mod.sems() + g_big.sems() + g_vec.sems(),
        compiler_params=pltpu.CompilerParams(vmem_limit_bytes=VMEM_LIMIT),
    )(c, ada_w, *vec_parts, *to_cast, *big)
    return res[0], res[1], res[2], list(res[3 + nc:]), list(res[3:3 + nc])


def _ada_update(c_all, dmod_cols, dmod_all, ada_w, m_w, v_w, ada_b, m_b, v_b):
    cols = ada_w.shape[2]
    nb = ada_b.shape[1]

    def body(c_ref, dmc_ref, dma_ref, w_ref, mw_ref, vw_ref, b_ref, mb_ref, vb_ref,
             gw_ref, dw_ref, nmw_ref, nvw_ref, gb_ref, db_ref, nmb_ref, nvb_ref):
        cv = c_ref[...]
        cond = cv * _sigmoid(cv)
        for l in range(2):
            g = _dot_tn(cond, dmc_ref[l])
            gw_ref[l] = g
            dlt, m2, v2 = _adam(w_ref[l], g, mw_ref[l], vw_ref[l])
            dw_ref[l] = dlt
            nmw_ref[l] = m2
            nvw_ref[l] = v2
        gb = dma_ref[0]
        for i in range(1, N_DEV):
            gb = gb + dma_ref[i]
        gb_ref[...] = gb
        dlt, m2, v2 = _adam(b_ref[...], gb, mb_ref[...], vb_ref[...])
        db_ref[...] = dlt
        nmb_ref[...] = m2
        nvb_ref[...] = v2

    wspec = _full((2, D, cols))
    bspec = _full((2, nb))
    wshape = jax.ShapeDtypeStruct((2, D, cols), F32)
    bshape = jax.ShapeDtypeStruct((2, nb), F32)
    return _pallas(
        body, name="ada_update", grid=(1,),
        in_specs=[_full((N_DEV, D)), _full((2, N_DEV, cols)), _full((N_DEV, 2, nb)),
                  wspec, wspec, wspec, bspec, bspec, bspec],
        out_specs=[wspec] * 4 + [bspec] * 4,
        out_shape=[wshape] * 4 + [bshape] * 4,
        compiler_params=_params(("arbitrary",)),
    )(c_all, dmod_cols, dmod_all, ada_w, m_w, v_w, ada_b, m_b, v_b)


def _ev_in(x, mod, w_in, rc, rs1, rs2, comm=None):
    T = x.shape[0]

    def body(x_ref, mod_ref, w_ref, c_ref, s1_ref, s2_ref, q_ref, kv_ref, su_ref, sv_ref, g_ref):
        h = x_ref[...] * (1.0 + mod_ref[1:2, :]) + mod_ref[0:1, :]
        p = _dot_nt(h, w_ref[...])
        c, s1, s2 = c_ref[...], s1_ref[...], s2_ref[...]
        for j in range(ATTN_W // LANE):
            qr = _rope_fwd(p[:, j * LANE:(j + 1) * LANE], c, s1, s2)
            q_ref[:, j * LANE:(j + 1) * LANE] = (qr * (HEAD_DIM ** -0.5)).astype(BF16)
        low = lax.broadcasted_iota(jnp.int32, (TMF, LANE), 1) < HEAD_DIM
        for j, val in enumerate((_rope_fwd(p[:, 512:640], c, s1, s2), p[:, 640:768])):
            swapped = pltpu.roll(val, HEAD_DIM, 1)
            tiles = (jnp.where(low, val, 0.0), jnp.where(low, 0.0, swapped),
                     jnp.where(low, swapped, 0.0), jnp.where(low, 0.0, val))
            for k, tile in enumerate(tiles):
                kv_ref[:, (4 * j + k) * LANE:(4 * j + k + 1) * LANE] = tile.astype(BF16)
        su_ref[...] = p[:, 768:1280].astype(BF16)
        sv_ref[...] = p[:, 1280:1792].astype(BF16)
        g_ref[...] = p[:, 1792:2816].astype(BF16)

    sh = lambda w: jax.ShapeDtypeStruct((T, w), BF16)
    return _fused_call(
        body, comm, (x, mod, w_in, rc, rs1, rs2), name="ev_in", grid=(T // TMF,),
        in_specs=[_tile(TMF, D), _full((3, D)), _full((EV_IN, D)), _tile(TMF, LANE), _tile(TMF, LANE),
                  _tile(TMF, LANE)],
        out_specs=[_tile(TMF, ATTN_W), _tile(TMF, KVX_W), _tile(TMF, SG_W), _tile(TMF, SG_W), _tile(TMF, D)],
        out_shape=[sh(ATTN_W), sh(KVX_W), sh(SG_W), sh(SG_W), sh(D)], semantics=("parallel",))


def _band_specs(width, nb):
    return [pl.BlockSpec((BLK, width), lambda n: (jnp.maximum(n - 1, 0), 0)),
            pl.BlockSpec((BLK, width), lambda n: (n, 0)),
            pl.BlockSpec((BLK, width), lambda n: (jnp.minimum(n + 1, nb - 1), 0))]


def _band_bias(bias_ref, n, nb):
    rows = lax.broadcasted_iota(jnp.int32, (3 * BLK, 1), 0)
    outside = ((rows < BLK) & (n == 0)) | ((rows >= 2 * BLK) & (n == nb - 1))
    return bias_ref[...] + jnp.where(outside, NEG_INF, 0.0)


def _lane_tile(ref, t):
    return ref[:, t * LANE:(t + 1) * LANE]


def _split_bf16(v):
    hi = v.astype(BF16)
    return hi, (v - hi.astype(F32)).astype(BF16)


def _group_mean(v, a_ref, exact_bf16=False):
    hi, lo = _split_bf16(v)
    a = a_ref[...]
    out = []
    for t in range(SG_W // (2 * LANE)):
        sl = slice(t * 2 * LANE, (t + 1) * 2 * LANE)
        r = jnp.dot(hi[:, sl], a, preferred_element_type=F32)
        if not exact_bf16:
            r = r + jnp.dot(lo[:, sl], a, preferred_element_type=F32)
        out.append(r)
    return jnp.concatenate(out, axis=-1)


def _sg_core(sv_ref, lng, lnb, a_ref, w_ref, bfull_ref):
    svf = sv_ref[...].astype(F32)
    xc = svf - _group_mean(svf, a_ref, exact_bf16=True)
    rstd = lax.rsqrt(_group_mean(xc * xc, a_ref) + LN_EPS)
    xhat = xc * rstd
    vb = (xhat * lng + lnb).astype(BF16)
    low = lax.broadcasted_iota(jnp.int32, (BLK, LANE), 1) < SG_DIM
    tiles = []
    for t in range(SG_W // LANE):
        v2 = vb[:, t * LANE:(t + 1) * LANE]
        r0 = jnp.dot(w_ref[2 * t], v2, preferred_element_type=F32)
        r1 = jnp.dot(w_ref[2 * t + 1], v2, preferred_element_type=F32)
        tiles.append(jnp.where(low, r0, r1))
    svm = jnp.concatenate(tiles, axis=-1) + bfull_ref[...]
    return xhat, rstd, vb, svm


def _mix0_fwd(q, kvx, su, sv, g0, sink_l, bias, a128, sg_lng, sg_lnb, sg_w, sg_bfull, comm=None):
    T = q.shape[0]
    nb = T // BLK

    def body(q_ref, kp_ref, kc_ref, kn_ref, su_ref, sv_ref, g_ref, sink_ref, bias_ref, a_ref, lng_ref, lnb_ref,
             w_ref, bfull_ref, ycat_ref, y0_ref, lse_ref):
        n = pl.program_id(0)
        bias = _band_bias(bias_ref, n, nb)
        kvx = jnp.concatenate([kp_ref[...], kc_ref[...], kn_ref[...]], axis=0)
        tiles = []
        for t in range(ATTN_W // LANE):
            qt = _lane_tile(q_ref, t)
            acc = None
            for par in range(2):
                h = 2 * t + par
                kt = 2 * (h // 4) + par
                ke = kvx[:, kt * LANE:(kt + 1) * LANE]
                ve = kvx[:, (4 + kt) * LANE:(5 + kt) * LANE]
                st = _dot_nt(ke, qt) + bias
                sk = _lane_tile(sink_ref, h)
                m = jnp.maximum(jnp.max(st, axis=0, keepdims=True), sk)
                p = jnp.exp(st - m)
                denom = jnp.sum(p, axis=0, keepdims=True) + jnp.exp(sk - m)
                contrib = _dot_tn(p * (1.0 / denom), ve)
                acc = contrib if acc is None else acc + contrib
                lse_ref[0, :, h * LANE:(h + 1) * LANE] = m + jnp.log(denom)
            tiles.append(acc)
        _, _, _, svm = _sg_core(sv_ref, lng_ref[...], lnb_ref[...], a_ref, w_ref, bfull_ref)
        tiles.append(su_ref[...].astype(F32) * svm)
        ycat = jnp.concatenate(tiles, axis=-1)
        gf = g_ref[...].astype(F32)
        ycat_ref[...] = ycat.astype(BF16)
        y0_ref[...] = (ycat * (gf * _sigmoid(gf))).astype(BF16)

    return _fused_call(
        body, comm, (q, kvx, kvx, kvx, su, sv, g0, sink_l, bias, a128, sg_lng, sg_lnb, sg_w, sg_bfull),
        name="mix0_fwd", grid=(nb,),
        in_specs=[_tile(BLK, ATTN_W)] + _band_specs(KVX_W, nb) + [
            _tile(BLK, SG_W), _tile(BLK, SG_W), _tile(BLK, D), _full((1, N_HEADS * LANE)), _full((3 * BLK, LANE)),
            _full((2 * LANE, 2 * LANE)),_full((1, SG_W)), _full((1, SG_W)), _full((SG_GROUPS, BLK, BLK)),
            _full((BLK, SG_W))],
        out_specs=[_tile(BLK, D), _tile(BLK, D), pl.BlockSpec((1, 1, N_HEADS * LANE), lambda n: (n, 0, 0))],
        out_shape=[jax.ShapeDtypeStruct((T, D), BF16), jax.ShapeDtypeStruct((T, D), BF16),
                   jax.ShapeDtypeStruct((nb, 1, N_HEADS * LANE), F32)], semantics=("parallel",))


def _ev_out(y0, w_out, x, mod, lnp):
    T = x.shape[0]

    def body(y_ref, w_ref, x_ref, mod_ref, ln_ref, out_ref, z_ref, x1_ref):
        out = _dot(y_ref[...], w_ref[...])
        z = ALPHA * x_ref[...] + mod_ref[2:3, :] * out
        x1, _, _ = _ln_fwd(z, ln_ref[0:1, :], ln_ref[1:2, :])
        out_ref[...] = out.astype(BF16)
        z_ref[...] = z
        x1_ref[...] = x1

    return _pallas(
        body, name="ev_out", grid=(T // TMF,),
        in_specs=[_tile(TMF, D), _full((D, D)), _tile(TMF, D), _full((3, D)), _full((2, D))],
        out_specs=[_tile(TMF, D)] * 3,
        out_shape=[jax.ShapeDtypeStruct((T, D), BF16), jax.ShapeDtypeStruct((T, D), F32),
                   jax.ShapeDtypeStruct((T, D), F32)],
        compiler_params=_params(("parallel",)),
    )(y0, w_out, x, mod, lnp)


def _od_in(x1, mod, w_in):
    T = x1.shape[0]

    def body(x_ref, mod_ref, w_ref, xr_ref, g_ref):
        h = x_ref[...] * (1.0 + mod_ref[1:2, :]) + mod_ref[0:1, :]
        p = _dot(h, w_ref[...])
        xr_ref[...] = p[:, :D]
        g_ref[...] = p[:, D:].astype(BF16)

    return _pallas(
        body, name="od_in", grid=(T // TMF,),
        in_specs=[_tile(TMF, D), _full((3, D)), _full((D, OD_IN))],
        out_specs=[_tile(TMF, D), _tile(TMF, D)],
        out_shape=[jax.ShapeDtypeStruct((T, D), F32), jax.ShapeDtypeStruct((T, D), BF16)],
        compiler_params=_params(("parallel",)),
    )(x1, mod, w_in)


def _ext_rows(prev_ref, cur, next_ref, j, n):
    prev = jnp.where(j > 0, prev_ref[...], 0.0)
    nxt = jnp.where(j < n - 1, next_ref[...], 0.0)
    return jnp.concatenate([prev, cur, nxt], axis=0)


def _shift_rows(ext, off, rows):
    total = ext.shape[0]
    if off == 0:
        return ext[SUBLANE:SUBLANE + rows, :]
    return pltpu.roll(ext, (-off) % total, 0)[SUBLANE:SUBLANE + rows, :]


def _conv_fwd(ext, cw, cb, rows):
    xc = cb
    for k in range(4):
        xc = xc + cw[k:k + 1, :] * _shift_rows(ext, k - 2, rows)
    return xc


def _gates(xc, wa_ref, wx_ref, ba, bx, lam):
    pr, pi = [], []
    for h in range(RNN_HEADS):
        xh = xc[:, h * RNN_HD:(h + 1) * RNN_HD].astype(BF16)
        pr.append(_dot(xh, wa_ref[h]))
        pi.append(_dot(xh, wx_ref[h]))
    r = _sigmoid(jnp.concatenate(pr, axis=-1) + ba)
    ig = _sigmoid(jnp.concatenate(pi, axis=-1) + bx)
    sp = jnp.maximum(-lam, 0.0) + jnp.log(1.0 + jnp.exp(-jnp.abs(lam)))
    neg_log_a = RG_C * r * sp
    a = jnp.exp(-neg_log_a)
    s2 = (1.0 + a * a) * jnp.tanh(neg_log_a)
    inv_s = lax.rsqrt(jnp.maximum(s2, 1e-30))
    return r, ig, sp, a, s2 * inv_s, inv_s


def _scan_tile(a_ref, b_ref, o_ref, carry_ref, rows, reverse):
    ridx = lax.broadcasted_iota(jnp.int32, (SUBLANE, D), 0)
    groups = rows // SUBLANE

    def group(gi, h):
        g = (groups - 1 - gi) if reverse else gi
        off = pl.multiple_of(g * SUBLANE, SUBLANE)
        a = a_ref[pl.ds(off, SUBLANE), :]
        b = b_ref[pl.ds(off, SUBLANE), :]
        for sh in (1, 2, 4):
            if reverse:
                keep = ridx < SUBLANE - sh
                a_p = jnp.where(keep, pltpu.roll(a, SUBLANE - sh, 0), 1.0)
                b_p = jnp.where(keep, pltpu.roll(b, SUBLANE - sh, 0), 0.0)
            else:
                keep = ridx >= sh
                a_p = jnp.where(keep, pltpu.roll(a, sh, 0), 1.0)
                b_p = jnp.where(keep, pltpu.roll(b, sh, 0), 0.0)
            b = b + a * b_p
            a = a * a_p
        hh = b + a * h
        o_ref[pl.ds(off, SUBLANE), :] = hh
        return hh[0:1, :] if reverse else hh[SUBLANE - 1:SUBLANE, :]

    carry_ref[...] = lax.fori_loop(0, groups, group, carry_ref[...])


def _rglru_fwd(xr, cw, cb, wa, wx, ba, bx, lam, reverse, name):
    T = xr.shape[0]
    n = T // TS
    prev_spec, next_spec = _halo_specs(TS, D, n, T, reverse)

    def body(prev_ref, cur_ref, next_ref, cw_ref, cb_ref, wa_ref, wx_ref, ba_ref, bx_ref, lam_ref,
             h_ref, a_ref, s_ref, r_ref, ig_ref, xc_ref, b_s, carry):
        i = pl.program_id(0)
        j = (n - 1 - i) if reverse else i

        @pl.when(i == 0)
        def _():
            carry[...] = jnp.zeros_like(carry)

        ext = _ext_rows(prev_ref, cur_ref[...], next_ref, j, n)
        xc = _conv_fwd(ext, cw_ref[...], cb_ref[...], TS)
        r, ig, _, a, s, _ = _gates(xc, wa_ref, wx_ref, ba_ref[...], bx_ref[...], lam_ref[...])
        s_ref[...] = s
        r_ref[...] = r.astype(BF16)
        ig_ref[...] = ig.astype(BF16)
        xc_ref[...] = xc.astype(BF16)
        a_ref[...] = a
        b_s[...] = s * ig * xc
        _scan_tile(a_ref, b_s, h_ref, carry, TS, reverse)

    wspec = _full((RNN_HEADS, RNN_HD, RNN_HD))
    cur = _rev_tile(TS, D, n, reverse)
    f32 = jax.ShapeDtypeStruct((T, D), F32)
    b16 = jax.ShapeDtypeStruct((T, D), BF16)
    return _pallas(
        body, name=name, grid=(n,),
        in_specs=[prev_spec, cur, next_spec, _full((4, D)), _full((1, D)),
                  wspec, wspec, _full((1, D)), _full((1, D)), _full((1, D))],
        out_specs=[cur] * 6,
        out_shape=[f32, f32, f32, b16, b16, b16],
        scratch_shapes=[pltpu.VMEM((TS, D), F32), pltpu.VMEM((1, D), F32)],
        compiler_params=_params(("arbitrary",)),
    )(xr, xr, xr, cw, cb, wa, wx, ba, bx, lam)


def _od_out(hf, hb, g1, w_out, x1, tgt, mod, lnp):
    T = x1.shape[0]

    def body(hf_ref, hb_ref, g_ref, w_ref, x_ref, t_ref, mod_ref, ln_ref,
             dh_ref, dg_ref, dx_ref, dwb_ref, vec_ref, dw_ref):
        i = pl.program_id(0)

        @pl.when(i == 0)
        def _():
            dw_ref[...] = jnp.zeros_like(dw_ref)
            vec_ref[...] = jnp.zeros_like(vec_ref)

        hs = hf_ref[...] + hb_ref[...]
        sg, dsg = _silu_and_grad(g_ref[...].astype(F32))
        yr = (hs * sg).astype(BF16)
        w = w_ref[...]
        out = _dot(yr, w)
        gate = mod_ref[2:3, :]
        z = ALPHA * x_ref[...] + gate * out
        lng = ln_ref[0:1, :]
        x2, xhat, rstd = _ln_fwd(z, lng, ln_ref[1:2, :])
        diff = x2 - t_ref[...]
        vec_ref[3:4, 0:LANE] += 0.5 * jnp.sum(diff * diff) * (1.0 / D)
        dx2 = diff * (1.0 / D)
        dz = _ln_bwd(dx2, xhat, rstd, lng)
        vec_ref[0:1, :] += _rowsum(dx2 * xhat)
        vec_ref[1:2, :] += _rowsum(dx2)
        vec_ref[2:3, :] += _rowsum(dz * out)
        dout = (dz * gate).astype(BF16)
        dyr = _dot_nt(dout, w)
        dw_ref[...] += _dot_tn(yr, dout)
        dh_ref[...] = dyr * sg
        dg_ref[...] = (dyr * hs * dsg).astype(BF16)
        dx_ref[...] = ALPHA * dz

        @pl.when(i == T // TMO - 1)
        def _():
            dwb_ref[...] = dw_ref[...].astype(BF16)

    return _pallas(
        body, name="od_out", grid=(T // TMO,),
        in_specs=[_tile(TMO, D), _tile(TMO, D), _tile(TMO, D), _full((D, D)), _tile(TMO, D), _tile(TMO, D),
                  _full((3, D)), _full((2, D))],
        out_specs=[_tile(TMO, D), _tile(TMO, D), _tile(TMO, D), _full((D, D)), _full((SUBLANE, D))],
        out_shape=[jax.ShapeDtypeStruct((T, D), F32), jax.ShapeDtypeStruct((T, D), BF16),
                   jax.ShapeDtypeStruct((T, D), F32), jax.ShapeDtypeStruct((D, D), BF16),
                   jax.ShapeDtypeStruct((SUBLANE, D), F32)],
        scratch_shapes=[pltpu.VMEM((D, D), F32)],
        compiler_params=_params(("arbitrary",)),
    )(hf, hb, g1, w_out, x1, tgt, mod, lnp)


def _rglru_bwd(fwd, dh, wa, wx, lam, reverse, name, comm=None):
    h, a_all, s_all, r_all, ig_all, xc_all = fwd
    T = h.shape[0]
    n = T // TS
    adj_rev = not reverse
    hprev_spec, hnext_spec = _halo_specs(TS, D, n, T, adj_rev)
    h_halo_spec = hnext_spec if reverse else hprev_spec

    def body(dh_ref, h_ref, hh_ref, a_ref, s_ref, r_ref, ig_ref, xc_ref, wa_ref, wx_ref, lam_ref,
             dxc_ref, dwa_ref, dwx_ref, vec_ref, a_s, l_s, carry, a_edge):
        i = pl.program_id(0)
        j = (n - 1 - i) if adj_rev else i

        @pl.when(i == 0)
        def _():
            carry[...] = jnp.zeros_like(carry)
            a_edge[...] = jnp.zeros_like(a_edge)
            dwa_ref[...] = jnp.zeros_like(dwa_ref)
            dwx_ref[...] = jnp.zeros_like(dwx_ref)
            vec_ref[...] = jnp.zeros_like(vec_ref)

        lam = lam_ref[...]
        sp = jnp.maximum(-lam, 0.0) + jnp.log(1.0 + jnp.exp(-jnp.abs(lam)))
        a, s = a_ref[...], s_ref[...]
        inv_s = lax.rsqrt(jnp.maximum(s * s, 1e-30))
        r, ig = r_ref[...].astype(F32), ig_ref[...].astype(F32)
        xcb = xc_ref[...]
        xc = xcb.astype(F32)

        rows = lax.broadcasted_iota(jnp.int32, (TS, D), 0)
        hcur = h_ref[...]
        if reverse:
            a_sh = jnp.where(rows == 0, a_edge[...], pltpu.roll(a, 1, 0))
            halo = jnp.where(j < n - 1, hh_ref[0:1, :], 0.0)
            h_nb = jnp.where(rows == TS - 1, halo, pltpu.roll(hcur, TS - 1, 0))
        else:
            a_sh = jnp.where(rows == TS - 1, a_edge[...], pltpu.roll(a, TS - 1, 0))
            halo = jnp.where(j > 0, hh_ref[SUBLANE - 1:SUBLANE, :], 0.0)
            h_nb = jnp.where(rows == 0, halo, pltpu.roll(hcur, 1, 0))
        a_s[...] = a_sh
        _scan_tile(a_s, dh_ref, l_s, carry, TS, adj_rev)
        a_edge[...] = a[TS - 1:TS, :] if reverse else a[0:1, :]

        lm = l_s[...]
        da = lm * h_nb
        di = lm * s * xc
        dxc = lm * s * ig
        ds = lm * ig * xc
        dlog_a = a * (da - ds * a * inv_s)
        dr = (-RG_C) * sp * dlog_a
        dsp = _rowsum((-RG_C) * r * dlog_a)
        dpr = dr * r * (1.0 - r)
        dpi = di * ig * (1.0 - ig)
        vec_ref[0:1, :] += _rowsum(dpr)
        vec_ref[1:2, :] += _rowsum(dpi)
        vec_ref[2:3, :] += dsp * (-_sigmoid(-lam))
        parts = []
        for hd in range(RNN_HEADS):
            sl = slice(hd * RNN_HD, (hd + 1) * RNN_HD)
            xh = xcb[:, sl]
            dprh = dpr[:, sl].astype(BF16)
            dpih = dpi[:, sl].astype(BF16)
            parts.append(_dot_nt(dprh, wa_ref[hd]) + _dot_nt(dpih, wx_ref[hd]))
            dwa_ref[hd] += _dot_tn(xh, dprh)
            dwx_ref[hd] += _dot_tn(xh, dpih)
        dxc_ref[...] = dxc + jnp.concatenate(parts, axis=-1)

    wspec = _full((RNN_HEADS, RNN_HD, RNN_HD))
    cur = _rev_tile(TS, D, n, adj_rev)
    return _fused_call(
        body, comm, (dh, h, h, a_all, s_all, r_all, ig_all, xc_all, wa, wx, lam), name=name, grid=(n,),
        in_specs=[cur, cur, h_halo_spec, cur, cur, cur, cur, cur, wspec, wspec, _full((1, D))],
        out_specs=[cur, wspec, wspec, _full((SUBLANE, D))],
        out_shape=[jax.ShapeDtypeStruct((T, D), F32),
                   jax.ShapeDtypeStruct((RNN_HEADS, RNN_HD, RNN_HD), F32),
                   jax.ShapeDtypeStruct((RNN_HEADS, RNN_HD, RNN_HD), F32),
                   jax.ShapeDtypeStruct((SUBLANE, D), F32)],
        scratch_shapes=[pltpu.VMEM((TS, D), F32)] * 2 + [pltpu.VMEM((1, D), F32)] * 2)


def _od_in_bwd(dxcf, dxcb, xr, dg1, x1, dx1p, mod, w_in, cw, comm=None):
    T = x1.shape[0]
    n = T // TMO
    slab = OD_IN // N_DEV
    prev_spec, next_spec = _halo_specs(TMO, D, n, T, False)

    def body(fp_ref, fc_ref, fn_ref, bp_ref, bc_ref, bn_ref, xr_ref, dg_ref, x1_ref, dxp_ref,
             mod_ref, w_ref, cw_ref, dx_ref, dwb_ref, vec_ref, dw_ref):
        i = pl.program_id(0)

        @pl.when(i == 0)
        def _():
            dw_ref[...] = jnp.zeros_like(dw_ref)
            vec_ref[...] = jnp.zeros_like(vec_ref)

        dcur = fc_ref[...] + bc_ref[...]
        dprev = jnp.where(i > 0, fp_ref[...] + bp_ref[...], 0.0)
        dnext = jnp.where(i < n - 1, fn_ref[...] + bn_ref[...], 0.0)
        dext = jnp.concatenate([dprev, dcur, dnext], axis=0)
        xr_v = xr_ref[...]
        cw_v = cw_ref[...]
        dxr = None
        for k in range(4):
            shifted = _shift_rows(dext, 2 - k, TMO)
            term = cw_v[k:k + 1, :] * shifted
            dxr = term if dxr is None else dxr + term
            vec_ref[k:k + 1, :] += _rowsum(shifted * xr_v)
        vec_ref[4:5, :] += _rowsum(dcur)
        dp = jnp.concatenate([dxr.astype(BF16), dg_ref[...]], axis=-1)
        x1v = x1_ref[...]
        scale1 = 1.0 + mod_ref[1:2, :]
        h1 = (x1v * scale1 + mod_ref[0:1, :]).astype(BF16)
        dh1 = _dot_nt(dp, w_ref[...])
        dw_ref[...] += _dot_tn(h1, dp)
        dx_ref[...] = dxp_ref[...] + dh1 * scale1
        vec_ref[5:6, :] += _rowsum(dh1)
        vec_ref[6:7, :] += _rowsum(dh1 * x1v)

        @pl.when(i == n - 1)
        def _():
            for j in range(N_DEV):
                dwb_ref[j] = dw_ref[:, j * slab:(j + 1) * slab].astype(BF16)

    t = _tile(TMO, D)
    return _fused_call(
        body, comm, (dxcf, dxcf, dxcf, dxcb, dxcb, dxcb, xr, dg1, x1, dx1p, mod, w_in, cw),
        name="od_in_bwd", grid=(n,),
        in_specs=[prev_spec, t, next_spec, prev_spec, t, next_spec, t, t, t, t,
                  _full((3, D)), _full((D, OD_IN)), _full((4, D))],
        out_specs=[t, _full((N_DEV, D, slab)), _full((SUBLANE, D))],
        out_shape=[jax.ShapeDtypeStruct((T, D), F32), jax.ShapeDtypeStruct((N_DEV, D, slab), BF16),
                   jax.ShapeDtypeStruct((SUBLANE, D), F32)],
        scratch_shapes=[pltpu.VMEM((D, OD_IN), F32)])


def _ev_out_bwd(dx1, z0, out0, y0, ycat, g0, w_out, mod, lnp, comm=None):
    T = dx1.shape[0]

    def body(dx_ref, z_ref, out_ref, y0_ref, yc_ref, g_ref, w_ref, mod_ref, ln_ref,
             dxp_ref, dyc_ref, dg_ref, dwb_ref, vec_ref, dw_ref):
        i = pl.program_id(0)

        @pl.when(i == 0)
        def _():
            dw_ref[...] = jnp.zeros_like(dw_ref)
            vec_ref[...] = jnp.zeros_like(vec_ref)

        lng = ln_ref[0:1, :]
        _, xhat, rstd = _ln_fwd(z_ref[...], lng, ln_ref[1:2, :])
        dy = dx_ref[...]
        dz = _ln_bwd(dy, xhat, rstd, lng)
        vec_ref[0:1, :] += _rowsum(dy * xhat)
        vec_ref[1:2, :] += _rowsum(dy)
        vec_ref[2:3, :] += _rowsum(dz * out_ref[...].astype(F32))
        dout = (dz * mod_ref[2:3, :]).astype(BF16)
        dy0 = _dot_nt(dout, w_ref[...])
        dw_ref[...] += _dot_tn(y0_ref[...], dout)
        sg, dsg = _silu_and_grad(g_ref[...].astype(F32))
        dyc_ref[...] = (dy0 * sg).astype(BF16)
        dg_ref[...] = (dy0 * yc_ref[...].astype(F32) * dsg).astype(BF16)
        dxp_ref[...] = ALPHA * dz

        @pl.when(i == T // TMO - 1)
        def _():
            dwb_ref[...] = dw_ref[...].astype(BF16)

    t = _tile(TMO, D)
    return _fused_call(
        body, comm, (dx1, z0, out0, y0, ycat, g0, w_out, mod, lnp), name="ev_out_bwd", grid=(T // TMO,),
        in_specs=[t, t, t, t, t, t, _full((D, D)), _full((3, D)), _full((2, D))],
        out_specs=[t, t, t, _full((D, D)), _full((SUBLANE, D))],
        out_shape=[jax.ShapeDtypeStruct((T, D), F32), jax.ShapeDtypeStruct((T, D), BF16),
                   jax.ShapeDtypeStruct((T, D), BF16), jax.ShapeDtypeStruct((D, D), BF16),
                   jax.ShapeDtypeStruct((SUBLANE, D), F32)],
        scratch_shapes=[pltpu.VMEM((D, D), F32)])


def _mix0_bwd(q, kvx, lse, dyc, ycat, su, sv, sink_l, bias, a128, gsum, sel, sg_lng, sg_lnb, sg_w, sg_bfull,
              rc, rs1, rs2, comm=None):
    T = q.shape[0]
    nb = T // BLK

    def body(q_ref, kp_ref, kc_ref, kn_ref, lse_ref, dyc_ref, yc_ref, su_ref, sv_ref, sink_ref, bias_ref, a_ref,
             gsum_ref, sel_ref, lng_ref, lnb_ref, w_ref, bfull_ref, c_ref, s1_ref, s2_ref,
             dq_ref, dkv_ref, dsu_ref, dsv_ref, dw_ref, dbt_ref, vec_ref, dsink_ref):
        n = pl.program_id(0)

        @pl.when(n == 0)
        def _():
            dkv_ref[...] = jnp.zeros_like(dkv_ref)
            dw_ref[...] = jnp.zeros_like(dw_ref)
            dbt_ref[...] = jnp.zeros_like(dbt_ref)
            vec_ref[...] = jnp.zeros_like(vec_ref)
            dsink_ref[...] = jnp.zeros_like(dsink_ref)

        band = pl.ds(pl.multiple_of(n * BLK + (TM - BLK), BLK), 3 * BLK)
        bias = _band_bias(bias_ref, n, nb)
        kvx = jnp.concatenate([kp_ref[...], kc_ref[...], kn_ref[...]], axis=0)
        bias2 = jnp.concatenate([bias, bias], axis=1)
        low = lax.broadcasted_iota(jnp.int32, (BLK, LANE), 1) < HEAD_DIM
        low2 = lax.broadcasted_iota(jnp.int32, (2 * BLK, LANE), 1) < HEAD_DIM
        sel = sel_ref[...]
        c, s1, s2 = c_ref[...], s1_ref[...], s2_ref[...]
        for kvh in range(2):
            t0, t1 = 2 * kvh, 2 * kvh + 1
            q2 = jnp.concatenate([_lane_tile(q_ref, t0), _lane_tile(q_ref, t1)], axis=0)
            do2 = jnp.concatenate([_lane_tile(dyc_ref, t0), _lane_tile(dyc_ref, t1)], axis=0)
            yc2 = jnp.concatenate([_lane_tile(yc_ref, t0), _lane_tile(yc_ref, t1)], axis=0)
            p_hi, p_lo = _split_bf16(do2.astype(F32) * yc2.astype(F32))
            deltas = _dot_nt(sel, p_hi) + _dot_nt(sel, p_lo)
            dkx = jnp.zeros((3 * BLK, LANE), F32)
            dvx = jnp.zeros((3 * BLK, LANE), F32)
            dq_acc = None
            for par in range(2):
                heads = (4 * kvh + par, 4 * kvh + 2 + par)
                kt = 2 * kvh + par
                ke = kvx[:, kt * LANE:(kt + 1) * LANE]
                ve = kvx[:, (4 + kt) * LANE:(5 + kt) * LANE]
                lse = jnp.concatenate([lse_ref[0, :, h * LANE:(h + 1) * LANE] for h in heads], axis=1)
                sk = jnp.concatenate([_lane_tile(sink_ref, h) for h in heads], axis=1)
                delta = deltas[par:par + 1, :]
                pt = jnp.exp(_dot_nt(ke, q2) + bias2 - lse)
                dst = (pt * (_dot_nt(ve, do2) - delta)).astype(BF16)
                sink_terms = jnp.exp(sk - lse) * delta
                for k, h in enumerate(heads):
                    dsink_ref[:, h * LANE:(h + 1) * LANE] += sink_terms[:, k * LANE:(k + 1) * LANE]
                part = _dot_tn(dst, ke)
                dq_acc = part if dq_acc is None else dq_acc + part
                mine = low2 if par == 0 else jnp.logical_not(low2)
                dkx = dkx + jnp.dot(dst, jnp.where(mine, q2, jnp.zeros_like(q2)), preferred_element_type=F32)
                dvx = dvx + jnp.dot(pt.astype(BF16), jnp.where(mine, do2, jnp.zeros_like(do2)),
                                    preferred_element_type=F32)
            for k, t in enumerate((t0, t1)):
                dq_t = dq_acc[k * BLK:(k + 1) * BLK] * (HEAD_DIM ** -0.5)
                dq_ref[:, t * LANE:(t + 1) * LANE] = _rope_bwd(dq_t, c, s1, s2).astype(BF16)
            dkv_ref[band, kvh * LANE:(kvh + 1) * LANE] += dkx
            dkv_ref[band, (2 + kvh) * LANE:(3 + kvh) * LANE] += dvx

        lng = lng_ref[...]
        xhat, rstd, vb, svm = _sg_core(sv_ref, lng, lnb_ref[...], a_ref, w_ref, bfull_ref)
        dy = dyc_ref[:, ATTN_W:].astype(F32)
        dsu_ref[...] = (dy * svm).astype(BF16)
        dsvm = dy * su_ref[...].astype(F32)
        d_hi, d_lo = _split_bf16(dsvm)
        gsum = gsum_ref[...]
        dbt_ref[...] += jnp.dot(d_hi, gsum, preferred_element_type=F32) + jnp.dot(d_lo, gsum,
                                                                                 preferred_element_type=F32)
        tiles = []
        for t in range(SG_W // LANE):
            tl = slice(t * LANE, (t + 1) * LANE)
            dt, v2 = d_hi[:, tl], vb[:, tl]
            dw_ref[2 * t] += _dot_nt(jnp.where(low, dt, jnp.zeros_like(dt)), v2)
            dw_ref[2 * t + 1] += _dot_nt(jnp.where(low, jnp.zeros_like(dt), dt), v2)
            tiles.append(jnp.where(low, _dot_tn(w_ref[2 * t], dt), _dot_tn(w_ref[2 * t + 1], dt)))
        dvgn = jnp.concatenate(tiles, axis=-1)
        vec_ref[0:1, :] += _rowsum(dvgn * xhat)
        vec_ref[1:2, :] += _rowsum(dvgn)
        dxh = dvgn * lng
        m1 = _group_mean(dxh, a_ref)
        m2 = _group_mean(dxh * xhat, a_ref)
        dsv_ref[...] = (rstd * (dxh - m1 - xhat * m2)).astype(BF16)

    return _fused_call(
        body, comm, (q, kvx, kvx, kvx, lse, dyc, ycat, su, sv, sink_l, bias, a128, gsum, sel, sg_lng, sg_lnb, sg_w,
                     sg_bfull, rc, rs1, rs2),
        name="mix0_bwd", grid=(nb,),
        in_specs=[_tile(BLK, ATTN_W)] + _band_specs(KVX_W, nb) + [
            pl.BlockSpec((1, 1, N_HEADS * LANE), lambda n: (n, 0, 0)), _tile(BLK, D), _tile(BLK, D),
            _tile(BLK, SG_W), _tile(BLK, SG_W), _full((1, N_HEADS * LANE)), _full((3 * BLK, LANE)),
            _full((2 * LANE, 2 * LANE)),_full((SG_W, LANE)), _full((SUBLANE, LANE)), _full((1, SG_W)), _full((1, SG_W)),
            _full((SG_GROUPS, BLK, BLK)), _full((BLK, SG_W)), _tile(BLK, LANE), _tile(BLK, LANE), _tile(BLK, LANE)],
        out_specs=[_tile(BLK, ATTN_W), _full((T + 2 * TM, 4 * LANE)), _tile(BLK, SG_W), _tile(BLK, SG_W),
                   _full((SG_GROUPS, BLK, BLK)), _full((BLK, LANE)), _full((SUBLANE, SG_W)),
                   _full((1, N_HEADS * LANE))],
        out_shape=[jax.ShapeDtypeStruct((T, ATTN_W), BF16), jax.ShapeDtypeStruct((T + 2 * TM, 4 * LANE), F32),
                   jax.ShapeDtypeStruct((T, SG_W), BF16), jax.ShapeDtypeStruct((T, SG_W), BF16),
                   jax.ShapeDtypeStruct((SG_GROUPS, BLK, BLK), F32), jax.ShapeDtypeStruct((BLK, LANE), F32),
                   jax.ShapeDtypeStruct((SUBLANE, SG_W), F32), jax.ShapeDtypeStruct((1, N_HEADS * LANE), F32)])


def _ev_in_bwd(dq, dkv, dsu, dsv, dg0, x, dxp, mod, w_in, rc, rs1, rs2, comm=None):
    T = x.shape[0]

    def body(dq_ref, dkv_ref, dsu_ref, dsv_ref, dg_ref, x_ref, dxp_ref, mod_ref, w_ref, c_ref, s1_ref, s2_ref,
             dx_ref, dwb_ref, vec_ref, dw_ref):
        i = pl.program_id(0)

        @pl.when(i == 0)
        def _():
            dw_ref[...] = jnp.zeros_like(dw_ref)
            vec_ref[...] = jnp.zeros_like(vec_ref)

        low = lax.broadcasted_iota(jnp.int32, (TM, LANE), 1) < HEAD_DIM

        def fold(j):
            t0 = dkv_ref[:, (2 * j) * LANE:(2 * j + 1) * LANE]
            t1 = dkv_ref[:, (2 * j + 1) * LANE:(2 * j + 2) * LANE]
            return jnp.where(low, t0 + pltpu.roll(t0, HEAD_DIM, 1), t1 + pltpu.roll(t1, HEAD_DIM, 1))

        dk = _rope_bwd(fold(0), c_ref[...], s1_ref[...], s2_ref[...]).astype(BF16)
        dp = jnp.concatenate([dq_ref[...], dk, fold(1).astype(BF16), dsu_ref[...], dsv_ref[...],
                              dg_ref[...]], axis=-1)
        xv = x_ref[...]
        scale0 = 1.0 + mod_ref[1:2, :]
        h0 = (xv * scale0 + mod_ref[0:1, :]).astype(BF16)
        dh0 = _dot(dp, w_ref[...])
        dw_ref[...] += _dot_tn(dp, h0)
        dx_ref[...] = dxp_ref[...] + dh0 * scale0
        vec_ref[0:1, :] += _rowsum(dh0)
        vec_ref[1:2, :] += _rowsum(dh0 * xv)

        @pl.when(i == T // TM - 1)
        def _():
            dwb_ref[...] = dw_ref[...].astype(BF16)

    t = _tile(TM, D)
    return _fused_call(
        body, comm, (dq, dkv, dsu, dsv, dg0, x, dxp, mod, w_in, rc, rs1, rs2), name="ev_in_bwd", grid=(T // TM,),
        in_specs=[_tile(TM, ATTN_W), pl.BlockSpec((TM, 4 * LANE), lambda i: (i + 1, 0)), _tile(TM, SG_W),
                  _tile(TM, SG_W), t, t, t,
                  _full((3, D)), _full((EV_IN, D)), _tile(TM, LANE), _tile(TM, LANE), _tile(TM, LANE)],
        out_specs=[t, _full((EV_IN, D)), _full((SUBLANE, D))],
        out_shape=[jax.ShapeDtypeStruct((T, D), F32), jax.ShapeDtypeStruct((EV_IN, D), BF16),
                   jax.ShapeDtypeStruct((SUBLANE, D), F32)],
        scratch_shapes=[pltpu.VMEM((EV_IN, D), F32)])


def _sum_slots(land_ref):
    g = land_ref[0].astype(F32)
    for i in range(1, land_ref.shape[0]):
        g = g + land_ref[i].astype(F32)
    return g


def _reduce_adam(items, name):
    R, C = items[0][1].shape
    rb = R
    if R > 512:
        for cand in (512, 256, 128, 64, 32, 16, 8):
            if R % cand == 0:
                rb = cand
                break
    n = len(items)

    def body(*refs):
        for k in range(n):
            l_ref, w_ref, m_ref, v_ref = refs[4 * k:4 * k + 4]
            g_ref, d_ref, nm_ref, nv_ref = refs[4 * n + 4 * k:4 * n + 4 * k + 4]
            g = _sum_slots(l_ref)
            g_ref[...] = g
            dlt, m2, v2 = _adam(w_ref[...], g, m_ref[...], v_ref[...])
            d_ref[...] = dlt
            nm_ref[...] = m2
            nv_ref[...] = v2

    t = pl.BlockSpec((rb, C), lambda i: (i, 0))
    shp = jax.ShapeDtypeStruct((R, C), F32)
    in_specs, operands = [], []
    for land, w, m, v in items:
        in_specs += [pl.BlockSpec((land.shape[0], rb, C), lambda i: (0, i, 0)), t, t, t]
        operands += [land, w, m, v]
    res = _pallas(
        body, name=name, grid=(R // rb,),
        in_specs=in_specs, out_specs=[t] * (4 * n), out_shape=[shp] * (4 * n),
        compiler_params=_params(("parallel",)),
    )(*operands)
    return [list(res[4 * k:4 * k + 4]) for k in range(n)]


def _tail_exchange(slabs, small):
    _, R, C = slabs.shape
    n_chips = N_DEV // 2
    gather = _GatherComm(small)
    ns = gather.n

    def body(*refs):
        slab_ref = refs[0]
        g_ins = refs[1:1 + ns]
        land_ref = refs[1 + ns]
        g_outs = refs[2 + ns:2 + 2 * ns]
        stage, part, s1_send, s1_recv, s2_send, s2_recv = refs[2 + 2 * ns:8 + 2 * ns]
        g_sems = refs[8 + 2 * ns:]
        x, y, c = _my_pos()
        chip = 2 * x + y
        gather.start(g_ins, g_outs, g_sems)

        swaps = [pltpu.make_async_remote_copy(
            src_ref=slab_ref.at[2 * k + (1 - c)], dst_ref=stage.at[k], send_sem=s1_send.at[k],
            recv_sem=s1_recv.at[k], device_id=(x, y, 1 - c), device_id_type=MESH) for k in range(n_chips)]
        for cp in swaps:
            cp.start()
        for cp in swaps:
            cp.wait()
        for k in range(n_chips):
            part[k] = (slab_ref[2 * k + c].astype(F32) + stage[k].astype(F32)).astype(BF16)

        gather.mid(g_ins, g_outs, g_sems)

        sends = []
        for r in range(1, n_chips):
            px = (1 - x) if (r & 2) else x
            py = (1 - y) if (r & 1) else y
            sends.append(pltpu.make_async_remote_copy(
                src_ref=part.at[2 * px + py], dst_ref=land_ref.at[chip], send_sem=s2_send.at[r - 1],
                recv_sem=s2_recv.at[r - 1], device_id=(px, py, c), device_id_type=MESH))
        for cp in sends:
            cp.start()
        land_ref[chip] = part[chip]
        for cp in sends:
            cp.wait()
        gather.finish(g_ins, g_outs, g_sems)

    any_spec = pl.BlockSpec(memory_space=pl.ANY)
    vmem_spec = pl.BlockSpec(memory_space=pltpu.VMEM)
    res = _pallas(
        body, name="tail_exchange",
        out_shape=[jax.ShapeDtypeStruct((n_chips, R, C), BF16)] + gather.out_shapes(),
        in_specs=[vmem_spec] + [any_spec] * ns, out_specs=[vmem_spec] + [any_spec] * ns,
        scratch_shapes=[pltpu.VMEM((n_chips, R, C), BF16), pltpu.VMEM((n_chips, R, C), BF16),
                        pltpu.SemaphoreType.DMA((n_chips,)), pltpu.SemaphoreType.DMA((n_chips,)),
                        pltpu.SemaphoreType.DMA((n_chips - 1,)), pltpu.SemaphoreType.DMA((n_chips - 1,))]
        + gather.sems(),
        compiler_params=pltpu.CompilerParams(vmem_limit_bytes=VMEM_LIMIT),
    )(slabs, *gather.arrs)
    return res[0], list(res[1:])


def _slots_adam(items, name):
    zeros3 = (0, 0, 0)
    in_specs, out_specs, out_shape, operands = [], [], [], []
    for land, w, m, v in items:
        inner = w.shape[-3:]
        if w.ndim == 5:
            lspec = pl.BlockSpec((N_DEV, 1) + inner, lambda i: (0, i) + zeros3)
            wspec = pl.BlockSpec((1, 1) + inner, lambda i: (0, i) + zeros3)
        else:
            lspec = pl.BlockSpec((N_DEV,) + inner, lambda i: (0,) + zeros3)
            wspec = pl.BlockSpec((1,) + inner, lambda i: (0,) + zeros3)
        in_specs += [lspec, wspec, wspec, wspec]
        out_specs += [wspec] * 4
        out_shape += [jax.ShapeDtypeStruct(w.shape, F32)] * 4
        operands += [land, w, m, v]
    n = len(items)

    def body(*refs):
        for k, (_, w, _, _) in enumerate(items):
            l_ref, w_ref, m_ref, v_ref = refs[4 * k:4 * k + 4]
            outs = refs[4 * n + 4 * k:4 * n + 4 * k + 4]
            at = (0, 0) if w.ndim == 5 else (0,)

            def update(l_ref=l_ref, w_ref=w_ref, m_ref=m_ref, v_ref=v_ref, outs=outs, at=at):
                g = l_ref[(0,) + at[1:]].astype(F32)
                for i in range(1, N_DEV):
                    g = g + l_ref[(i,) + at[1:]].astype(F32)
                dlt, m2, v2 = _adam(w_ref[at], g, m_ref[at], v_ref[at])
                for o_ref, val in zip(outs, (g, dlt, m2, v2)):
                    o_ref[at] = val

            if w.ndim == 5:
                update()
            else:
                pl.when(pl.program_id(0) == 0)(update)

    res = _pallas(
        body, name=name, grid=(2,),
        in_specs=in_specs, out_specs=out_specs, out_shape=out_shape,
        compiler_params=_params(("arbitrary",)),
    )(*operands)
    return [list(res[4 * k:4 * k + 4]) for k in range(n)]


SMALL_PARAMS = ("ln_g", "ln_b", "ev_sg_ln_g", "ev_sg_ln_b", "ev_sink", "ev_sg_b",
                "od_conv_w", "od_conv_b", "od_b_a", "od_b_x", "od_lam")


def _small_update(ga, gc, gd, gf, gb, ge, gsink, gbt, params):
    names = list(SMALL_PARAMS)
    flat = [a for nm in names for a in params[nm]]
    n_g = 8

    def body(*refs):
        ga_ref, gc_ref, gd_ref, gf_ref, gb_ref, ge_ref, gs_ref, gbt_ref = refs[:n_g]
        prm = refs[n_g:n_g + 3 * len(names)]
        loss_ref = refs[n_g + 3 * len(names)]
        outs = refs[n_g + 3 * len(names) + 1:]

        def ssum(ref):
            acc = ref[0]
            for i in range(1, N_DEV):
                acc = acc + ref[i]
            return acc

        a, cc, dd, ff, bb, ee = ssum(ga_ref), ssum(gc_ref), ssum(gd_ref), ssum(gf_ref), ssum(gb_ref), ssum(ge_ref)
        loss_ref[...] = a[3:4, 0:LANE]
        me = _slot(*_my_pos())

        def mine(rows):
            acc = jnp.zeros((rows.shape[0], LANE), F32)
            for j in range(N_DEV):
                acc = acc + jnp.where(me == j, rows[:, j * LANE:(j + 1) * LANE], 0.0)
            return acc

        sink_terms = ssum(gs_ref)
        lane8 = lax.broadcasted_iota(jnp.int32, (1, N_HEADS), 1)
        g_sink = jnp.zeros((1, N_HEADS), F32)
        for h in range(N_HEADS):
            tot = -jnp.sum(sink_terms[:, h * LANE:(h + 1) * LANE], axis=1, keepdims=True)
            g_sink = jnp.where(lane8 == h, tot, g_sink)
        grads = dict(
            ln_g=jnp.concatenate([dd[0:1], a[0:1]], axis=0), ln_b=jnp.concatenate([dd[1:2], a[1:2]], axis=0),
            ev_sg_ln_g=ee[0:1], ev_sg_ln_b=ee[1:2], ev_sink=g_sink,
            ev_sg_b=jnp.transpose(ssum(gbt_ref))[0:SG_GROUPS, :],
            od_conv_w=mine(cc[0:4]), od_conv_b=mine(cc[4:5]),
            od_b_a=mine(jnp.concatenate([ff[0:1], bb[0:1]], axis=0)),
            od_b_x=mine(jnp.concatenate([ff[1:2], bb[1:2]], axis=0)),
            od_lam=mine(jnp.concatenate([ff[2:3], bb[2:3]], axis=0)))
        for k, nm in enumerate(names):
            w_ref, m_ref, v_ref = prm[3 * k:3 * k + 3]
            at = (0,) if len(w_ref.shape) == 3 else ()
            g = grads[nm]
            dlt, m2, v2 = _adam(w_ref[at] if at else w_ref[...], g, m_ref[at] if at else m_ref[...],
                                v_ref[at] if at else v_ref[...])
            for o_ref, val in zip(outs[4 * k:4 * k + 4], (g, dlt, m2, v2)):
                if at:
                    o_ref[at] = val
                else:
                    o_ref[...] = val

    gathered = [ga, gc, gd, gf, gb, ge, gsink, gbt]
    out_shape = [jax.ShapeDtypeStruct((1, LANE), F32)]
    for nm in names:
        out_shape += [jax.ShapeDtypeStruct(params[nm][0].shape, F32)] * 4
    return _pallas(
        body, name="small_update", grid=(1,),
        in_specs=[_full(a.shape) for a in gathered + flat],
        out_specs=[_full(s.shape) for s in out_shape], out_shape=out_shape,
        compiler_params=_params(("arbitrary",)),
    )(*gathered, *flat)


VEC_ROWS = 16
VEC_LAYOUT = (("od_conv_w", 4), ("od_conv_b", 1), ("od_b_a", 2), ("od_b_x", 2), ("od_lam", 2))


def _to_slabs(full, cols_per):
    R = full.shape[0]
    return full.reshape(R, N_DEV, cols_per).transpose(1, 0, 2)


def _from_slabs(slabs):
    n, R, cp = slabs.shape
    return slabs.transpose(1, 0, 2).reshape(R, n * cp)


def kernel(x, c, positions, ada_w, ada_b, ln_g, ln_b, ev_w_in, ev_w_out, ev_sink, ev_sg_ln_g, ev_sg_ln_b, ev_sg_w, ev_sg_b, od_w_in, od_conv_w, od_conv_b, od_w_a, od_b_a, od_w_x, od_b_x, od_lam, od_w_out, loss_target, m_ada_w, m_ada_b, m_ln_g, m_ln_b, m_ev_w_in, m_ev_w_out, m_ev_sink, m_ev_sg_ln_g, m_ev_sg_ln_b, m_ev_sg_w, m_ev_sg_b, m_od_w_in, m_od_conv_w, m_od_conv_b, m_od_w_a, m_od_b_a, m_od_w_x, m_od_b_x, m_od_lam, m_od_w_out, v_ada_w, v_ada_b, v_ln_g, v_ln_b, v_ev_w_in, v_ev_w_out, v_ev_sink, v_ev_sg_ln_g, v_ev_sg_ln_b, v_ev_sg_w, v_ev_sg_b, v_od_w_in, v_od_conv_w, v_od_conv_b, v_od_w_a, v_od_b_a, v_od_w_x, v_od_b_x, v_od_lam, v_od_w_out):
    T = x.shape[1]
    me = _slot(*_my_pos())
    xs = x.reshape(T, D)
    tgt = loss_target.reshape(T, D)

    c_all, mod_all, g_vec, (g_ev_in,), (s_ev_out, s_od_in, s_od_out, sg_w, wa, wx) = _head_gather(
        c, ada_w, [ev_w_in[0].T.astype(BF16)],
        [ev_w_out[0], od_w_in[0], od_w_out[0], ev_sg_w[0], od_w_a[0], od_w_x[0]],
        [od_conv_w, od_conv_b, od_b_a, od_b_x, od_lam])
    c_all = c_all.reshape(N_DEV, D)
    w_ev_in = g_ev_in.reshape(EV_IN, D)
    vec_full = _from_slabs(g_vec)
    cw, cb = vec_full[0:4], vec_full[4:5]
    ba, bx, lam = vec_full[5:7], vec_full[7:9], vec_full[9:11]
    mod_mine = lax.dynamic_index_in_dim(mod_all, me, axis=2, keepdims=False)
    mod = mod_mine.transpose(1, 0, 2).reshape(2, 3 * D) + ada_b
    mod0 = mod[0].reshape(3, D)
    mod1 = mod[1].reshape(3, D)

    half = 8
    inv_freq = jnp.power(jnp.float32(ROPE_THETA), -jnp.arange(half, dtype=F32) / half)
    ang = positions.reshape(T).astype(F32)[:, None] * inv_freq
    cos_t = jnp.tile(jnp.cos(ang), (1, LANE // half))
    sin_t = jnp.tile(jnp.sin(ang), (1, LANE // half))
    l64 = jnp.arange(LANE) % HEAD_DIM
    rc = jnp.where(l64 < 2 * half, cos_t, 1.0)
    rs1 = jnp.where(l64 < half, -sin_t, 0.0)
    rs2 = jnp.where((l64 >= half) & (l64 < 2 * half), sin_t, 0.0)

    ln0 = jnp.stack([ln_g[0], ln_b[0]])
    ln1 = jnp.stack([ln_g[1], ln_b[1]])
    sg_lng = ev_sg_ln_g
    sg_lnb = ev_sg_ln_b
    sg_bfull = jnp.repeat(ev_sg_b[0].T, SG_DIM, axis=1)
    sink_l = jnp.repeat(ev_sink, LANE, axis=1)
    kj = jnp.arange(3 * BLK)[:, None]
    qi = jnp.arange(BLK)[None, :]
    band_bias = jnp.where(jnp.abs(kj - BLK - qi) <= BLK, 0.0, NEG_INF).astype(F32)
    lanes = jnp.arange(LANE)
    lanes2 = jnp.arange(2 * LANE)
    a128 = jnp.where(lanes2[:, None] // SG_DIM == lanes2[None, :] // SG_DIM, 1.0 / SG_DIM, 0.0).astype(BF16)
    gsum = (jnp.arange(SG_W)[:, None] // SG_DIM == lanes[None, :]).astype(BF16)
    sel = (jnp.arange(SUBLANE)[:, None] == lanes[None, :] // HEAD_DIM).astype(BF16)

    (q, kvx, su, sv, g0), _ = _ev_in(xs, mod0, w_ev_in, rc, rs1, rs2)
    (ycat, y0, lse), (g_ev_out, g_od_in, g_od_out) = _mix0_fwd(
        q, kvx, su, sv, g0, sink_l, band_bias, a128, sg_lng, sg_lnb, sg_w, sg_bfull,
        _GatherComm([s_ev_out, s_od_in, s_od_out], mid_frac=0.75))
    w_ev_out = g_ev_out.reshape(D, D)
    w_od_in = _from_slabs(g_od_in)
    w_od_out = g_od_out.reshape(D, D)
    out0, z0, x1 = _ev_out(y0, w_ev_out, xs, mod0, ln0)
    xr, g1 = _od_in(x1, mod1, w_od_in)
    fwd_f = _rglru_fwd(xr, cw, cb, wa[0], wx[0], ba[0:1], bx[0:1], lam[0:1], False, "rglru_fwd_f")
    fwd_b = _rglru_fwd(xr, cw, cb, wa[1], wx[1], ba[1:2], bx[1:2], lam[1:2], True, "rglru_fwd_b")
    dh, dg1, dx1p, d_od_out, vec_a = _od_out(fwd_f[0], fwd_b[0], g1, w_od_out, x1, tgt, mod1, ln1)

    (dxcf, dwa_f, dwx_f, vec_f), (l_od_out,) = _rglru_bwd(
        fwd_f, dh, wa[0], wx[0], lam[0:1], False, "rglru_bwd_f",
        _ExchangeComm([d_od_out.reshape(N_DEV, D // N_DEV, D)]))
    (dxcb, dwa_b, dwx_b, vec_b), _ = _rglru_bwd(fwd_b, dh, wa[1], wx[1], lam[1:2], True, "rglru_bwd_b")
    (dx1, d_od_in, vec_c), (a_wa,) = _od_in_bwd(
        dxcf, dxcb, xr, dg1, x1, dx1p, mod1, w_od_in, cw,
        _GatherComm([jnp.stack([dwa_f, dwa_b]).astype(BF16)], mid_frac=0.75))
    (dxp, dyc, dg0, d_ev_out, vec_d), (a_wx,) = _ev_out_bwd(
        dx1, z0, out0, y0, ycat, g0, w_ev_out, mod0, ln0,
        _GatherComm([jnp.stack([dwx_f, dwx_b]).astype(BF16)], mid_frac=0.75))
    (dq, dkv, dsu, dsv, d_sg_w, d_sg_bt, vec_e, d_sink_l), (l_od_in, l_ev_out) = _mix0_bwd(
        q, kvx, lse, dyc, ycat, su, sv, sink_l, band_bias, a128, gsum, sel, sg_lng, sg_lnb, sg_w, sg_bfull,
        rc, rs1, rs2, _ExchangeComm([d_od_in, d_ev_out.reshape(N_DEV, D // N_DEV, D)]))
    (grad_x, d_ev_in, vec_g), _ = _ev_in_bwd(dq, dkv, dsu, dsv, dg0, xs, dxp, mod0, w_ev_in, rc, rs1, rs2)

    l_ev_in, (ga, gc, gd, gf, gb, gg, ge, gsink, gbt, a_sgw) = _tail_exchange(
        d_ev_in.reshape(N_DEV, EV_IN // N_DEV, D),
        [vec_a, vec_c, vec_d, vec_f, vec_b, vec_g, vec_e, d_sink_l, d_sg_bt, d_sg_w.astype(BF16)])

    dmod_all = jnp.stack([jnp.concatenate([gg[:, 0], gg[:, 1], gd[:, 2]], axis=-1),
                          jnp.concatenate([gc[:, 5], gc[:, 6], ga[:, 2]], axis=-1)], axis=1)
    cols = ada_w.shape[2]
    dmod_cols = lax.dynamic_slice_in_dim(dmod_all, me * cols, cols, axis=2).transpose(1, 0, 2)
    (g_ada_w, d_ada_w, nm_ada_w, nv_ada_w, g_ada_b, d_ada_b, nm_ada_b, nv_ada_b) = _ada_update(
        c_all, dmod_cols, dmod_all, ada_w, m_ada_w, v_ada_w, ada_b, m_ada_b, v_ada_b)

    res = dict(ada_w=[g_ada_w, d_ada_w, nm_ada_w, nv_ada_w], ada_b=[g_ada_b, d_ada_b, nm_ada_b, nv_ada_b])
    (r_ev_in,) = _reduce_adam([(l_ev_in, ev_w_in[0].T, m_ev_w_in[0].T, v_ev_w_in[0].T)], "adam_ev_w_in")
    res["ev_w_in"] = [a.T[None] for a in r_ev_in]
    (r_od_in,) = _reduce_adam([(l_od_in, od_w_in[0], m_od_w_in[0], v_od_w_in[0])], "adam_od_w_in")
    r_ev_out, r_od_out = _reduce_adam([(l_ev_out, ev_w_out[0], m_ev_w_out[0], v_ev_w_out[0]),
                                       (l_od_out, od_w_out[0], m_od_w_out[0], v_od_w_out[0])], "adam_w_out")
    for name, r in (("od_w_in", r_od_in), ("ev_w_out", r_ev_out), ("od_w_out", r_od_out)):
        res[name] = [a[None] for a in r]
    res["od_w_a"], res["od_w_x"], res["ev_sg_w"] = _slots_adam(
        [(a_wa, od_w_a, m_od_w_a, v_od_w_a), (a_wx, od_w_x, m_od_w_x, v_od_w_x),
         (a_sgw, ev_sg_w, m_ev_sg_w, v_ev_sg_w)], "adam_gates")
    small = dict(ln_g=(ln_g, m_ln_g, v_ln_g), ln_b=(ln_b, m_ln_b, v_ln_b),
                 ev_sg_ln_g=(ev_sg_ln_g, m_ev_sg_ln_g, v_ev_sg_ln_g),
                 ev_sg_ln_b=(ev_sg_ln_b, m_ev_sg_ln_b, v_ev_sg_ln_b),
                 ev_sink=(ev_sink, m_ev_sink, v_ev_sink), ev_sg_b=(ev_sg_b, m_ev_sg_b, v_ev_sg_b),
                 od_conv_w=(od_conv_w, m_od_conv_w, v_od_conv_w), od_conv_b=(od_conv_b, m_od_conv_b, v_od_conv_b),
                 od_b_a=(od_b_a, m_od_b_a, v_od_b_a), od_b_x=(od_b_x, m_od_b_x, v_od_b_x),
                 od_lam=(od_lam, m_od_lam, v_od_lam))
    small_out = _small_update(ga, gc, gd, gf, gb, ge, gsink, gbt, small)
    loss = small_out[0][0, 0]
    for k, name in enumerate(SMALL_PARAMS):
        res[name] = small_out[1 + 4 * k:5 + 4 * k]

    order = ["ada_w", "ada_b", "ln_g", "ln_b", "ev_w_in", "ev_w_out", "ev_sink", "ev_sg_ln_g", "ev_sg_ln_b",
             "ev_sg_w", "ev_sg_b", "od_w_in", "od_conv_w", "od_conv_b", "od_w_a", "od_b_a", "od_w_x", "od_b_x",
             "od_lam", "od_w_out"]
    outs = [loss, grad_x.reshape(1, T, D)]
    for kind in range(4):
        outs += [res[name][kind] for name in order]
    return tuple(outs)
```

```python
import functools

import jax
import jax.numpy as jnp
from jax import lax
from jax.experimental import pallas as pl
from jax.experimental.pallas import tpu as pltpu

F32 = jnp.float32
BF16 = jnp.bfloat16

N_DEV = 8
D = 1024
N_HEADS = 8
HEAD_DIM = 64
KV_WIDTH = 128
ATTN_W = 512
SG_W = 512
SG_GROUPS = 8
SG_DIM = 64
BLK = 128
KVX_W = 1024
EV_IN = 2816
OD_IN = 2048
RNN_HEADS = 8
RNN_HD = 128
ALPHA = 4.0 ** 0.25
LN_EPS = 1e-5
NEG_INF = -1e30
RG_C = 8.0
ROPE_THETA = 500000.0
LR, B1, B2, EPS, WD, STEP = 0.001, 0.9, 0.999, 1e-08, 0.01, 10

LANE = 128
SUBLANE = 8
TM = 256
TMF = 512
TMO = 512
TS = 256
VMEM_LIMIT = 56 * 1024 * 1024

MESH = pl.DeviceIdType.MESH


def _pallas(body, **kw):
    return pl.pallas_call(body, **kw)


def _params(sem, vmem=VMEM_LIMIT):
    return pltpu.CompilerParams(dimension_semantics=sem, vmem_limit_bytes=vmem)


def _sigmoid(x):
    return 0.5 * jnp.tanh(0.5 * x) + 0.5


def _silu_and_grad(x):
    s = _sigmoid(x)
    return x * s, s * (1.0 + x * (1.0 - s))


def _dot(a, b):
    return jnp.dot(a.astype(BF16), b.astype(BF16), preferred_element_type=F32)


def _dot_nt(a, b):
    return lax.dot_general(a.astype(BF16), b.astype(BF16), (((1,), (1,)), ((), ())), preferred_element_type=F32)


def _dot_tn(a, b):
    return lax.dot_general(a.astype(BF16), b.astype(BF16), (((0,), (0,)), ((), ())), preferred_element_type=F32)


def _ln_fwd(z, g, b):
    mu = jnp.mean(z, axis=-1, keepdims=True)
    zc = z - mu
    var = jnp.mean(zc * zc, axis=-1, keepdims=True)
    rstd = lax.rsqrt(var + LN_EPS)
    xhat = zc * rstd
    return xhat * g + b, xhat, rstd


def _ln_bwd(dy, xhat, rstd, g):
    dxh = dy * g
    m1 = jnp.mean(dxh, axis=-1, keepdims=True)
    m2 = jnp.mean(dxh * xhat, axis=-1, keepdims=True)
    return rstd * (dxh - m1 - xhat * m2)


def _rowsum(v):
    return jnp.sum(v, axis=0, keepdims=True)


def _rope_fwd(t, c, s1, s2):
    return t * c + pltpu.roll(t, LANE - 8, 1) * s1 + pltpu.roll(t, 8, 1) * s2


def _rope_bwd(d, c, s1, s2):
    return d * c + pltpu.roll(d * s1, 8, 1) + pltpu.roll(d * s2, LANE - 8, 1)


def _adam(w, g, m, v):
    m2 = B1 * m + (1.0 - B1) * g
    v2 = B2 * v + (1.0 - B2) * (g * g)
    m_hat = m2 / (1.0 - B1 ** STEP)
    v_hat = v2 / (1.0 - B2 ** STEP)
    delta = -LR * (m_hat / (jnp.sqrt(v_hat) + EPS) + WD * w)
    return delta, m2, v2


def _tile(rows, width):
    return pl.BlockSpec((rows, width), lambda i: (i, 0))


def _full(shape):
    zeros = (0,) * len(shape)
    return pl.BlockSpec(shape, lambda i: zeros)


def _rev_tile(rows, width, n, reverse):
    if reverse:
        return pl.BlockSpec((rows, width), lambda i: (n - 1 - i, 0))
    return pl.BlockSpec((rows, width), lambda i: (i, 0))


def _halo_specs(rows, width, n, total_rows, reverse):
    per = rows // SUBLANE
    last = total_rows // SUBLANE - 1

    def tile_of(i):
        return (n - 1 - i) if reverse else i

    prev = pl.BlockSpec((SUBLANE, width), lambda i: (jnp.maximum(tile_of(i) * per - 1, 0), 0))
    nxt = pl.BlockSpec((SUBLANE, width), lambda i: (jnp.minimum((tile_of(i) + 1) * per, last), 0))
    return prev, nxt


def _my_pos():
    return lax.axis_index("x"), lax.axis_index("y"), lax.axis_index("c")


def _slot(px, py, pc):
    return 4 * px + 2 * py + pc


class _GatherComm:
    has_mid = True

    def __init__(self, arrs, mid_frac=0.5):
        self.arrs = list(arrs)
        self.n = len(self.arrs)
        self.mid_frac = mid_frac

    def out_shapes(self):
        return [jax.ShapeDtypeStruct((N_DEV,) + a.shape, a.dtype) for a in self.arrs]

    def sems(self):
        return [pltpu.SemaphoreType.DMA((7 * self.n,)), pltpu.SemaphoreType.DMA((7 * self.n,)),
                pltpu.SemaphoreType.DMA((self.n,))]

    def _parts(self, ins, outs, sems):
        send_sems, recv_sems, local_sems = sems
        x, y, c = _my_pos()
        me, sibling = (x, y, c), (x, y, 1 - c)
        chips = [(1 - x, y), (x, 1 - y), (1 - x, 1 - y)]

        def copy(a, k, block, to, src=None):
            dst = outs[a].at[_slot(*block)]
            return pltpu.make_async_remote_copy(
                src_ref=dst if src is None else src, dst_ref=dst,
                send_sem=send_sems.at[a * 7 + k], recv_sem=recv_sems.at[a * 7 + k],
                device_id=to, device_id_type=MESH)

        local = [pltpu.make_async_copy(ins[a], outs[a].at[_slot(*me)], local_sems.at[a]) for a in range(self.n)]
        first = []
        for a in range(self.n):
            first.append(copy(a, 0, me, sibling, src=ins[a]))
            first += [copy(a, 1 + j, me, (*chip, c), src=ins[a]) for j, chip in enumerate(chips)]
        ici_in = [copy(a, 1 + j, (*chip, c), me) for j, chip in enumerate(chips) for a in range(self.n)]
        passed = [copy(a, 4 + j, (*chip, c), sibling) for j, chip in enumerate(chips) for a in range(self.n)]
        d2d_in = []
        for a in range(self.n):
            d2d_in.append(copy(a, 0, sibling, me))
            d2d_in += [copy(a, 4 + j, (*chip, 1 - c), me) for j, chip in enumerate(chips)]
        return local, first, ici_in, passed, d2d_in

    def start(self, ins, outs, sems):
        local, first, _, _, _ = self._parts(ins, outs, sems)
        for cp in local + first:
            cp.start()

    def mid(self, ins, outs, sems):
        _, _, ici_in, passed, _ = self._parts(ins, outs, sems)
        for arrived, fw in zip(ici_in, passed):
            arrived.wait_recv()
            fw.start()

    def finish(self, ins, outs, sems):
        local, first, _, passed, d2d_in = self._parts(ins, outs, sems)
        for cp in d2d_in:
            cp.wait_recv()
        for cp in first + passed:
            cp.wait_send()
        for cp in local:
            cp.wait()


class _ExchangeComm:
    has_mid = False

    def __init__(self, arrs):
        self.arrs = list(arrs)
        self.n = len(self.arrs)

    def out_shapes(self):
        return [jax.ShapeDtypeStruct(a.shape, a.dtype) for a in self.arrs]

    def sems(self):
        return [pltpu.SemaphoreType.DMA((7 * self.n,)), pltpu.SemaphoreType.DMA((7 * self.n,)),
                pltpu.SemaphoreType.DMA((self.n,))]

    def _copies(self, ins, outs, sems):
        send_sems, recv_sems, local_sems = sems
        x, y, c = _my_pos()
        mine = _slot(x, y, c)
        copies = [pltpu.make_async_copy(ins[a].at[mine], outs[a].at[mine], local_sems.at[a]) for a in range(self.n)]
        for k in range(1, N_DEV):
            px = (1 - x) if (k & 4) else x
            py = (1 - y) if (k & 2) else y
            pc = (1 - c) if (k & 1) else c
            for a in range(self.n):
                copies.append(pltpu.make_async_remote_copy(
                    src_ref=ins[a].at[_slot(px, py, pc)], dst_ref=outs[a].at[mine],
                    send_sem=send_sems.at[a * 7 + k - 1], recv_sem=recv_sems.at[a * 7 + k - 1],
                    device_id=(px, py, pc), device_id_type=MESH))
        return copies

    def start(self, ins, outs, sems):
        for cp in self._copies(ins, outs, sems):
            cp.start()

    def finish(self, ins, outs, sems):
        for cp in self._copies(ins, outs, sems):
            cp.wait()


def _fused_call(body, comm, operands, *, name, grid, in_specs, out_specs, out_shape, scratch_shapes=(),
                semantics=("arbitrary",)):
    n_in, n_out, n_scr = len(in_specs), len(out_specs), len(scratch_shapes)
    if comm is None:
        res = _pallas(body, name=name, grid=grid, in_specs=list(in_specs), out_specs=list(out_specs),
                      out_shape=list(out_shape), scratch_shapes=list(scratch_shapes),
                      compiler_params=_params(semantics))(*operands)
        return list(res), []
    k = comm.n
    steps = grid[0]

    def wrapped(*refs):
        ins, cins = refs[:n_in], refs[n_in:n_in + k]
        outs = refs[n_in + k:n_in + k + n_out]
        couts = refs[n_in + k + n_out:n_in + 2 * k + n_out]
        rest = refs[n_in + 2 * k + n_out:]
        scratch, sems = rest[:n_scr], rest[n_scr:]
        i = pl.program_id(0)

        @pl.when(i == 0)
        def _():
            comm.start(cins, couts, sems)

        body(*ins, *outs, *scratch)

        if comm.has_mid:
            @pl.when(i == int(steps * comm.mid_frac))
            def _():
                comm.mid(cins, couts, sems)

        @pl.when(i == steps - 1)
        def _():
            comm.finish(cins, couts, sems)

    any_spec = pl.BlockSpec(memory_space=pl.ANY)
    res = _pallas(wrapped, name=name, grid=grid, in_specs=list(in_specs) + [any_spec] * k,
                  out_specs=list(out_specs) + [any_spec] * k, out_shape=list(out_shape) + comm.out_shapes(),
                  scratch_shapes=list(scratch_shapes) + comm.sems(),
                  compiler_params=_params(("arbitrary",)))(*operands, *comm.arrs)
    return list(res[:n_out]), list(res[n_out:])


def _head_gather(c, ada_w, big, to_cast, vec_parts):
    cols = ada_w.shape[2]
    g_c, g_big = _GatherComm([c]), _GatherComm(big)
    g_mod = _GatherComm([jax.ShapeDtypeStruct((2, N_DEV, cols), F32)])
    g_vec = _GatherComm([jax.ShapeDtypeStruct((VEC_ROWS, LANE), F32)])
    nb, nc, nv = g_big.n, len(to_cast), len(vec_parts)

    def body(*refs):
        c_ref, w_ref = refs[0], refs[1]
        vec_in = refs[2:2 + nv]
        cast_in = refs[2 + nv:2 + nv + nc]
        big_in = refs[2 + nv + nc:2 + nv + nc + nb]
        outs = refs[2 + nv + nc + nb:]
        c_all_ref, mod_all_ref, vec_all_ref = outs[0], outs[1], outs[2]
        cast_out = outs[3:3 + nc]
        big_out = outs[3 + nc:3 + nc + nb]
        part_ref, pack_ref = outs[3 + nc + nb], outs[4 + nc + nb]
        sems = outs[5 + nc + nb:]
        s_c, s_mod, s_big, s_vec = sems[0:3], sems[3:6], sems[6:9], sems[9:12]
        g_c.start([c_ref], [c_all_ref], s_c)
        g_big.start(big_in, big_out, s_big)
        pack_ref[...] = jnp.zeros_like(pack_ref)
        row = 0
        for ref, (_, nrows) in zip(vec_in, VEC_LAYOUT):
            pack_ref[row:row + nrows, :] = ref[0] if len(ref.shape) == 3 else ref[...]
            row += nrows
        g_vec.start([pack_ref], [vec_all_ref], s_vec)
        g_c.mid([c_ref], [c_all_ref], s_c)
        g_c.finish([c_ref], [c_all_ref], s_c)
        cv = c_all_ref[:, 0, :]
        cond = cv * _sigmoid(cv)
        for l in range(2):
            part_ref[l] = _dot(cond, w_ref[l])
        g_mod.start([part_ref], [mod_all_ref], s_mod)
        for src, dst in zip(cast_in, cast_out):
            dst[...] = src[...].astype(BF16)
        for g, ins, outs_, sm in ((g_vec, [pack_ref], [vec_all_ref], s_vec), (g_mod, [part_ref], [mod_all_ref], s_mod),
                                  (g_big, big_in, big_out, s_big)):
            g.mid(ins, outs_, sm)
            g.finish(ins, outs_, sm)

    any_spec = pl.BlockSpec(memory_space=pl.ANY)
    vmem_spec = pl.BlockSpec(memory_space=pltpu.VMEM)
    res = _pallas(
        body, name="head_gather",
        out_shape=(g_c.out_shapes() + g_mod.out_shapes() + g_vec.out_shapes()
                   + [jax.ShapeDtypeStruct(a.shape, BF16) for a in to_cast] + g_big.out_shapes()),
        in_specs=[vmem_spec] * (2 + nv + nc) + [any_spec] * nb,
        out_specs=[vmem_spec] * (3 + nc) + [any_spec] * nb,
        scratch_shapes=[pltpu.VMEM((2, N_DEV, cols), F32), pltpu.VMEM((VEC_ROWS, LANE), F32)]
        + g_c.sems() + g_mod.sems() + g_big.sems() + g_vec.sems(),
        compiler_params=pltpu.CompilerParams(vmem_limit_bytes=VMEM_LIMIT),
    )(c, ada_w, *vec_parts, *to_cast, *big)
    return res[0], res[1], res[2], list(res[3 + nc:]), list(res[3:3 + nc])


def _ada_update(c_all, dmod_cols, dmod_all, ada_w, m_w, v_w, ada_b, m_b, v_b):
    cols = ada_w.shape[2]
    nb = ada_b.shape[1]

    def body(c_ref, dmc_ref, dma_ref, w_ref, mw_ref, vw_ref, b_ref, mb_ref, vb_ref,
             gw_ref, dw_ref, nmw_ref, nvw_ref, gb_ref, db_ref, nmb_ref, nvb_ref):
        cv = c_ref[...]
        cond = cv * _sigmoid(cv)
        for l in range(2):
            g = _dot_tn(cond, dmc_ref[l])
            gw_ref[l] = g
            dlt, m2, v2 = _adam(w_ref[l], g, mw_ref[l], vw_ref[l])
            dw_ref[l] = dlt
            nmw_ref[l] = m2
            nvw_ref[l] = v2
        gb = dma_ref[0]
        for i in range(1, N_DEV):
            gb = gb + dma_ref[i]
        gb_ref[...] = gb
        dlt, m2, v2 = _adam(b_ref[...], gb, mb_ref[...], vb_ref[...])
        db_ref[...] = dlt
        nmb_ref[...] = m2
        nvb_ref[...] = v2

    wspec = _full((2, D, cols))
    bspec = _full((2, nb))
    wshape = jax.ShapeDtypeStruct((2, D, cols), F32)
    bshape = jax.ShapeDtypeStruct((2, nb), F32)
    return _pallas(
        body, name="ada_update", grid=(1,),
        in_specs=[_full((N_DEV, D)), _full((2, N_DEV, cols)), _full((N_DEV, 2, nb)),
                  wspec, wspec, wspec, bspec, bspec, bspec],
        out_specs=[wspec] * 4 + [bspec] * 4,
        out_shape=[wshape] * 4 + [bshape] * 4,
        compiler_params=_params(("arbitrary",)),
    )(c_all, dmod_cols, dmod_all, ada_w, m_w, v_w, ada_b, m_b, v_b)


def _ev_in(x, mod, w_in, rc, rs1, rs2, comm=None):
    T = x.shape[0]

    def body(x_ref, mod_ref, w_ref, c_ref, s1_ref, s2_ref, q_ref, kv_ref, su_ref, sv_ref, g_ref):
        h = x_ref[...] * (1.0 + mod_ref[1:2, :]) + mod_ref[0:1, :]
        p = _dot_nt(h, w_ref[...])
        c, s1, s2 = c_ref[...], s1_ref[...], s2_ref[...]
        for j in range(ATTN_W // LANE):
            qr = _rope_fwd(p[:, j * LANE:(j + 1) * LANE], c, s1, s2)
            q_ref[:, j * LANE:(j + 1) * LANE] = (qr * (HEAD_DIM ** -0.5)).astype(BF16)
        low = lax.broadcasted_iota(jnp.int32, (TMF, LANE), 1) < HEAD_DIM
        for j, val in enumerate((_rope_fwd(p[:, 512:640], c, s1, s2), p[:, 640:768])):
            swapped = pltpu.roll(val, HEAD_DIM, 1)
            tiles = (jnp.where(low, val, 0.0), jnp.where(low, 0.0, swapped),
                     jnp.where(low, swapped, 0.0), jnp.where(low, 0.0, val))
            for k, tile in enumerate(tiles):
                kv_ref[:, (4 * j + k) * LANE:(4 * j + k + 1) * LANE] = tile.astype(BF16)
        su_ref[...] = p[:, 768:1280].astype(BF16)
        sv_ref[...] = p[:, 1280:1792].astype(BF16)
        g_ref[...] = p[:, 1792:2816].astype(BF16)

    sh = lambda w: jax.ShapeDtypeStruct((T, w), BF16)
    return _fused_call(
        body, comm, (x, mod, w_in, rc, rs1, rs2), name="ev_in", grid=(T // TMF,),
        in_specs=[_tile(TMF, D), _full((3, D)), _full((EV_IN, D)), _tile(TMF, LANE), _tile(TMF, LANE),
                  _tile(TMF, LANE)],
        out_specs=[_tile(TMF, ATTN_W), _tile(TMF, KVX_W), _tile(TMF, SG_W), _tile(TMF, SG_W), _tile(TMF, D)],
        out_shape=[sh(ATTN_W), sh(KVX_W), sh(SG_W), sh(SG_W), sh(D)], semantics=("parallel",))


def _band_specs(width, nb):
    return [pl.BlockSpec((BLK, width), lambda n: (jnp.maximum(n - 1, 0), 0)),
            pl.BlockSpec((BLK, width), lambda n: (n, 0)),
            pl.BlockSpec((BLK, width), lambda n: (jnp.minimum(n + 1, nb - 1), 0))]


def _band_bias(bias_ref, n, nb):
    rows = lax.broadcasted_iota(jnp.int32, (3 * BLK, 1), 0)
    outside = ((rows < BLK) & (n == 0)) | ((rows >= 2 * BLK) & (n == nb - 1))
    return bias_ref[...] + jnp.where(outside, NEG_INF, 0.0)


def _lane_tile(ref, t):
    return ref[:, t * LANE:(t + 1) * LANE]


def _split_bf16(v):
    hi = v.astype(BF16)
    return hi, (v - hi.astype(F32)).astype(BF16)


def _group_mean(v, a_ref, exact_bf16=False):
    hi, lo = _split_bf16(v)
    a = a_ref[...]
    out = []
    for t in range(SG_W // (2 * LANE)):
        sl = slice(t * 2 * LANE, (t + 1) * 2 * LANE)
        r = jnp.dot(hi[:, sl], a, preferred_element_type=F32)
        if not exact_bf16:
            r = r + jnp.dot(lo[:, sl], a, preferred_element_type=F32)
        out.append(r)
    return jnp.concatenate(out, axis=-1)


def _sg_core(sv_ref, lng, lnb, a_ref, w_ref, bfull_ref):
    svf = sv_ref[...].astype(F32)
    xc = svf - _group_mean(svf, a_ref, exact_bf16=True)
    rstd = lax.rsqrt(_group_mean(xc * xc, a_ref) + LN_EPS)
    xhat = xc * rstd
    vb = (xhat * lng + lnb).astype(BF16)
    low = lax.broadcasted_iota(jnp.int32, (BLK, LANE), 1) < SG_DIM
    tiles = []
    for t in range(SG_W // LANE):
        v2 = vb[:, t * LANE:(t + 1) * LANE]
        r0 = jnp.dot(w_ref[2 * t], v2, preferred_element_type=F32)
        r1 = jnp.dot(w_ref[2 * t + 1], v2, preferred_element_type=F32)
        tiles.append(jnp.where(low, r0, r1))
    svm = jnp.concatenate(tiles, axis=-1) + bfull_ref[...]
    return xhat, rstd, vb, svm


def _mix0_fwd(q, kvx, su, sv, g0, sink_l, bias, a128, sg_lng, sg_lnb, sg_w, sg_bfull, comm=None):
    T = q.shape[0]
    nb = T // BLK

    def body(q_ref, kp_ref, kc_ref, kn_ref, su_ref, sv_ref, g_ref, sink_ref, bias_ref, a_ref, lng_ref, lnb_ref,
             w_ref, bfull_ref, ycat_ref, y0_ref, lse_ref):
        n = pl.program_id(0)
        bias = _band_bias(bias_ref, n, nb)
        kvx = jnp.concatenate([kp_ref[...], kc_ref[...], kn_ref[...]], axis=0)
        tiles = []
        for t in range(ATTN_W // LANE):
            qt = _lane_tile(q_ref, t)
            acc = None
            for par in range(2):
                h = 2 * t + par
                kt = 2 * (h // 4) + par
                ke = kvx[:, kt * LANE:(kt + 1) * LANE]
                ve = kvx[:, (4 + kt) * LANE:(5 + kt) * LANE]
                st = _dot_nt(ke, qt) + bias
                sk = _lane_tile(sink_ref, h)
                m = jnp.maximum(jnp.max(st, axis=0, keepdims=True), sk)
                p = jnp.exp(st - m)
                denom = jnp.sum(p, axis=0, keepdims=True) + jnp.exp(sk - m)
                contrib = _dot_tn(p * (1.0 / denom), ve)
                acc = contrib if acc is None else acc + contrib
                lse_ref[0, :, h * LANE:(h + 1) * LANE] = m + jnp.log(denom)
            tiles.append(acc)
        _, _, _, svm = _sg_core(sv_ref, lng_ref[...], lnb_ref[...], a_ref, w_ref, bfull_ref)
        tiles.append(su_ref[...].astype(F32) * svm)
        ycat = jnp.concatenate(tiles, axis=-1)
        gf = g_ref[...].astype(F32)
        ycat_ref[...] = ycat.astype(BF16)
        y0_ref[...] = (ycat * (gf * _sigmoid(gf))).astype(BF16)

    return _fused_call(
        body, comm, (q, kvx, kvx, kvx, su, sv, g0, sink_l, bias, a128, sg_lng, sg_lnb, sg_w, sg_bfull),
        name="mix0_fwd", grid=(nb,),
        in_specs=[_tile(BLK, ATTN_W)] + _band_specs(KVX_W, nb) + [
            _tile(BLK, SG_W), _tile(BLK, SG_W), _tile(BLK, D), _full((1, N_HEADS * LANE)), _full((3 * BLK, LANE)),
            _full((2 * LANE, 2 * LANE)),_full((1, SG_W)), _full((1, SG_W)), _full((SG_GROUPS, BLK, BLK)),
            _full((BLK, SG_W))],
        out_specs=[_tile(BLK, D), _tile(BLK, D), pl.BlockSpec((1, 1, N_HEADS * LANE), lambda n: (n, 0, 0))],
        out_shape=[jax.ShapeDtypeStruct((T, D), BF16), jax.ShapeDtypeStruct((T, D), BF16),
                   jax.ShapeDtypeStruct((nb, 1, N_HEADS * LANE), F32)], semantics=("parallel",))


def _ev_out(y0, w_out, x, mod, lnp):
    T = x.shape[0]

    def body(y_ref, w_ref, x_ref, mod_ref, ln_ref, out_ref, z_ref, x1_ref):
        out = _dot(y_ref[...], w_ref[...])
        z = ALPHA * x_ref[...] + mod_ref[2:3, :] * out
        x1, _, _ = _ln_fwd(z, ln_ref[0:1, :], ln_ref[1:2, :])
        out_ref[...] = out.astype(BF16)
        z_ref[...] = z
        x1_ref[...] = x1

    return _pallas(
        body, name="ev_out", grid=(T // TMF,),
        in_specs=[_tile(TMF, D), _full((D, D)), _tile(TMF, D), _full((3, D)), _full((2, D))],
        out_specs=[_tile(TMF, D)] * 3,
        out_shape=[jax.ShapeDtypeStruct((T, D), BF16), jax.ShapeDtypeStruct((T, D), F32),
                   jax.ShapeDtypeStruct((T, D), F32)],
        compiler_params=_params(("parallel",)),
    )(y0, w_out, x, mod, lnp)


def _od_in(x1, mod, w_in):
    T = x1.shape[0]

    def body(x_ref, mod_ref, w_ref, xr_ref, g_ref):
        h = x_ref[...] * (1.0 + mod_ref[1:2, :]) + mod_ref[0:1, :]
        p = _dot(h, w_ref[...])
        xr_ref[...] = p[:, :D]
        g_ref[...] = p[:, D:].astype(BF16)

    return _pallas(
        body, name="od_in", grid=(T // TMF,),
        in_specs=[_tile(TMF, D), _full((3, D)), _full((D, OD_IN))],
        out_specs=[_tile(TMF, D), _tile(TMF, D)],
        out_shape=[jax.ShapeDtypeStruct((T, D), F32), jax.ShapeDtypeStruct((T, D), BF16)],
        compiler_params=_params(("parallel",)),
    )(x1, mod, w_in)


def _ext_rows(prev_ref, cur, next_ref, j, n):
    prev = jnp.where(j > 0, prev_ref[...], 0.0)
    nxt = jnp.where(j < n - 1, next_ref[...], 0.0)
    return jnp.concatenate([prev, cur, nxt], axis=0)


def _shift_rows(ext, off, rows):
    total = ext.shape[0]
    if off == 0:
        return ext[SUBLANE:SUBLANE + rows, :]
    return pltpu.roll(ext, (-off) % total, 0)[SUBLANE:SUBLANE + rows, :]


def _conv_fwd(ext, cw, cb, rows):
    xc = cb
    for k in range(4):
        xc = xc + cw[k:k + 1, :] * _shift_rows(ext, k - 2, rows)
    return xc


def _gates(xc, wa_ref, wx_ref, ba, bx, lam):
    pr, pi = [], []
    for h in range(RNN_HEADS):
        xh = xc[:, h * RNN_HD:(h + 1) * RNN_HD].astype(BF16)
        pr.append(_dot(xh, wa_ref[h]))
        pi.append(_dot(xh, wx_ref[h]))
    r = _sigmoid(jnp.concatenate(pr, axis=-1) + ba)
    ig = _sigmoid(jnp.concatenate(pi, axis=-1) + bx)
    sp = jnp.maximum(-lam, 0.0) + jnp.log(1.0 + jnp.exp(-jnp.abs(lam)))
    neg_log_a = RG_C * r * sp
    a = jnp.exp(-neg_log_a)
    s2 = (1.0 + a * a) * jnp.tanh(neg_log_a)
    inv_s = lax.rsqrt(jnp.maximum(s2, 1e-30))
    return r, ig, sp, a, s2 * inv_s, inv_s


def _scan_tile(a_ref, b_ref, o_ref, carry_ref, rows, reverse):
    ridx = lax.broadcasted_iota(jnp.int32, (SUBLANE, D), 0)
    groups = rows // SUBLANE

    def group(gi, h):
        g = (groups - 1 - gi) if reverse else gi
        off = pl.multiple_of(g * SUBLANE, SUBLANE)
        a = a_ref[pl.ds(off, SUBLANE), :]
        b = b_ref[pl.ds(off, SUBLANE), :]
        for sh in (1, 2, 4):
            if reverse:
                keep = ridx < SUBLANE - sh
                a_p = jnp.where(keep, pltpu.roll(a, SUBLANE - sh, 0), 1.0)
                b_p = jnp.where(keep, pltpu.roll(b, SUBLANE - sh, 0), 0.0)
            else:
                keep = ridx >= sh
                a_p = jnp.where(keep, pltpu.roll(a, sh, 0), 1.0)
                b_p = jnp.where(keep, pltpu.roll(b, sh, 0), 0.0)
            b = b + a * b_p
            a = a * a_p
        hh = b + a * h
        o_ref[pl.ds(off, SUBLANE), :] = hh
        return hh[0:1, :] if reverse else hh[SUBLANE - 1:SUBLANE, :]

    carry_ref[...] = lax.fori_loop(0, groups, group, carry_ref[...])


def _rglru_fwd(xr, cw, cb, wa, wx, ba, bx, lam, reverse, name):
    T = xr.shape[0]
    n = T // TS
    prev_spec, next_spec = _halo_specs(TS, D, n, T, reverse)

    def body(prev_ref, cur_ref, next_ref, cw_ref, cb_ref, wa_ref, wx_ref, ba_ref, bx_ref, lam_ref,
             h_ref, a_ref, s_ref, r_ref, ig_ref, xc_ref, b_s, carry):
        i = pl.program_id(0)
        j = (n - 1 - i) if reverse else i

        @pl.when(i == 0)
        def _():
            carry[...] = jnp.zeros_like(carry)

        ext = _ext_rows(prev_ref, cur_ref[...], next_ref, j, n)
        xc = _conv_fwd(ext, cw_ref[...], cb_ref[...], TS)
        r, ig, _, a, s, _ = _gates(xc, wa_ref, wx_ref, ba_ref[...], bx_ref[...], lam_ref[...])
        s_ref[...] = s
        r_ref[...] = r.astype(BF16)
        ig_ref[...] = ig.astype(BF16)
        xc_ref[...] = xc.astype(BF16)
        a_ref[...] = a
        b_s[...] = s * ig * xc
        _scan_tile(a_ref, b_s, h_ref, carry, TS, reverse)

    wspec = _full((RNN_HEADS, RNN_HD, RNN_HD))
    cur = _rev_tile(TS, D, n, reverse)
    f32 = jax.ShapeDtypeStruct((T, D), F32)
    b16 = jax.ShapeDtypeStruct((T, D), BF16)
    return _pallas(
        body, name=name, grid=(n,),
        in_specs=[prev_spec, cur, next_spec, _full((4, D)), _full((1, D)),
                  wspec, wspec, _full((1, D)), _full((1, D)), _full((1, D))],
        out_specs=[cur] * 6,
        out_shape=[f32, f32, f32, b16, b16, b16],
        scratch_shapes=[pltpu.VMEM((TS, D), F32), pltpu.VMEM((1, D), F32)],
        compiler_params=_params(("arbitrary",)),
    )(xr, xr, xr, cw, cb, wa, wx, ba, bx, lam)


def _od_out(hf, hb, g1, w_out, x1, tgt, mod, lnp):
    T = x1.shape[0]

    def body(hf_ref, hb_ref, g_ref, w_ref, x_ref, t_ref, mod_ref, ln_ref,
             dh_ref, dg_ref, dx_ref, dwb_ref, vec_ref, dw_ref):
        i = pl.program_id(0)

        @pl.when(i == 0)
        def _():
            dw_ref[...] = jnp.zeros_like(dw_ref)
            vec_ref[...] = jnp.zeros_like(vec_ref)

        hs = hf_ref[...] + hb_ref[...]
        sg, dsg = _silu_and_grad(g_ref[...].astype(F32))
        yr = (hs * sg).astype(BF16)
        w = w_ref[...]
        out = _dot(yr, w)
        gate = mod_ref[2:3, :]
        z = ALPHA * x_ref[...] + gate * out
        lng = ln_ref[0:1, :]
        x2, xhat, rstd = _ln_fwd(z, lng, ln_ref[1:2, :])
        diff = x2 - t_ref[...]
        vec_ref[3:4, 0:LANE] += 0.5 * jnp.sum(diff * diff) * (1.0 / D)
        dx2 = diff * (1.0 / D)
        dz = _ln_bwd(dx2, xhat, rstd, lng)
        vec_ref[0:1, :] += _rowsum(dx2 * xhat)
        vec_ref[1:2, :] += _rowsum(dx2)
        vec_ref[2:3, :] += _rowsum(dz * out)
        dout = (dz * gate).astype(BF16)
        dyr = _dot_nt(dout, w)
        dw_ref[...] += _dot_tn(yr, dout)
        dh_ref[...] = dyr * sg
        dg_ref[...] = (dyr * hs * dsg).astype(BF16)
        dx_ref[...] = ALPHA * dz

        @pl.when(i == T // TMO - 1)
        def _():
            dwb_ref[...] = dw_ref[...].astype(BF16)

    return _pallas(
        body, name="od_out", grid=(T // TMO,),
        in_specs=[_tile(TMO, D), _tile(TMO, D), _tile(TMO, D), _full((D, D)), _tile(TMO, D), _tile(TMO, D),
                  _full((3, D)), _full((2, D))],
        out_specs=[_tile(TMO, D), _tile(TMO, D), _tile(TMO, D), _full((D, D)), _full((SUBLANE, D))],
        out_shape=[jax.ShapeDtypeStruct((T, D), F32), jax.ShapeDtypeStruct((T, D), BF16),
                   jax.ShapeDtypeStruct((T, D), F32), jax.ShapeDtypeStruct((D, D), BF16),
                   jax.ShapeDtypeStruct((SUBLANE, D), F32)],
        scratch_shapes=[pltpu.VMEM((D, D), F32)],
        compiler_params=_params(("arbitrary",)),
    )(hf, hb, g1, w_out, x1, tgt, mod, lnp)


def _rglru_bwd(fwd, dh, wa, wx, lam, reverse, name, comm=None):
    h, a_all, s_all, r_all, ig_all, xc_all = fwd
    T = h.shape[0]
    n = T // TS
    adj_rev = not reverse
    hprev_spec, hnext_spec = _halo_specs(TS, D, n, T, adj_rev)
    h_halo_spec = hnext_spec if reverse else hprev_spec

    def body(dh_ref, h_ref, hh_ref, a_ref, s_ref, r_ref, ig_ref, xc_ref, wa_ref, wx_ref, lam_ref,
             dxc_ref, dwa_ref, dwx_ref, vec_ref, a_s, l_s, carry, a_edge):
        i = pl.program_id(0)
        j = (n - 1 - i) if adj_rev else i

        @pl.when(i == 0)
        def _():
            carry[...] = jnp.zeros_like(carry)
            a_edge[...] = jnp.zeros_like(a_edge)
            dwa_ref[...] = jnp.zeros_like(dwa_ref)
            dwx_ref[...] = jnp.zeros_like(dwx_ref)
            vec_ref[...] = jnp.zeros_like(vec_ref)

        lam = lam_ref[...]
        sp = jnp.maximum(-lam, 0.0) + jnp.log(1.0 + jnp.exp(-jnp.abs(lam)))
        a, s = a_ref[...], s_ref[...]
        inv_s = lax.rsqrt(jnp.maximum(s * s, 1e-30))
        r, ig = r_ref[...].astype(F32), ig_ref[...].astype(F32)
        xcb = xc_ref[...]
        xc = xcb.astype(F32)

        rows = lax.broadcasted_iota(jnp.int32, (TS, D), 0)
        hcur = h_ref[...]
        if reverse:
            a_sh = jnp.where(rows == 0, a_edge[...], pltpu.roll(a, 1, 0))
            halo = jnp.where(j < n - 1, hh_ref[0:1, :], 0.0)
            h_nb = jnp.where(rows == TS - 1, halo, pltpu.roll(hcur, TS - 1, 0))
        else:
            a_sh = jnp.where(rows == TS - 1, a_edge[...], pltpu.roll(a, TS - 1, 0))
            halo = jnp.where(j > 0, hh_ref[SUBLANE - 1:SUBLANE, :], 0.0)
            h_nb = jnp.where(rows == 0, halo, pltpu.roll(hcur, 1, 0))
        a_s[...] = a_sh
        _scan_tile(a_s, dh_ref, l_s, carry, TS, adj_rev)
        a_edge[...] = a[TS - 1:TS, :] if reverse else a[0:1, :]

        lm = l_s[...]
        da = lm * h_nb
        di = lm * s * xc
        dxc = lm * s * ig
        ds = lm * ig * xc
        dlog_a = a * (da - ds * a * inv_s)
        dr = (-RG_C) * sp * dlog_a
        dsp = _rowsum((-RG_C) * r * dlog_a)
        dpr = dr * r * (1.0 - r)
        dpi = di * ig * (1.0 - ig)
        vec_ref[0:1, :] += _rowsum(dpr)
        vec_ref[1:2, :] += _rowsum(dpi)
        vec_ref[2:3, :] += dsp * (-_sigmoid(-lam))
        parts = []
        for hd in range(RNN_HEADS):
            sl = slice(hd * RNN_HD, (hd + 1) * RNN_HD)
            xh = xcb[:, sl]
            dprh = dpr[:, sl].astype(BF16)
            dpih = dpi[:, sl].astype(BF16)
            parts.append(_dot_nt(dprh, wa_ref[hd]) + _dot_nt(dpih, wx_ref[hd]))
            dwa_ref[hd] += _dot_tn(xh, dprh)
            dwx_ref[hd] += _dot_tn(xh, dpih)
        dxc_ref[...] = dxc + jnp.concatenate(parts, axis=-1)

    wspec = _full((RNN_HEADS, RNN_HD, RNN_HD))
    cur = _rev_tile(TS, D, n, adj_rev)
    return _fused_call(
        body, comm, (dh, h, h, a_all, s_all, r_all, ig_all, xc_all, wa, wx, lam), name=name, grid=(n,),
        in_specs=[cur, cur, h_halo_spec, cur, cur, cur, cur, cur, wspec, wspec, _full((1, D))],
        out_specs=[cur, wspec, wspec, _full((SUBLANE, D))],
        out_shape=[jax.ShapeDtypeStruct((T, D), F32),
                   jax.ShapeDtypeStruct((RNN_HEADS, RNN_HD, RNN_HD), F32),
                   jax.ShapeDtypeStruct((RNN_HEADS, RNN_HD, RNN_HD), F32),
                   jax.ShapeDtypeStruct((SUBLANE, D), F32)],
        scratch_shapes=[pltpu.VMEM((TS, D), F32)] * 2 + [pltpu.VMEM((1, D), F32)] * 2)


def _od_in_bwd(dxcf, dxcb, xr, dg1, x1, dx1p, mod, w_in, cw, comm=None):
    T = x1.shape[0]
    n = T // TMO
    slab = OD_IN // N_DEV
    prev_spec, next_spec = _halo_specs(TMO, D, n, T, False)

    def body(fp_ref, fc_ref, fn_ref, bp_ref, bc_ref, bn_ref, xr_ref, dg_ref, x1_ref, dxp_ref,
             mod_ref, w_ref, cw_ref, dx_ref, dwb_ref, vec_ref, dw_ref):
        i = pl.program_id(0)

        @pl.when(i == 0)
        def _():
            dw_ref[...] = jnp.zeros_like(dw_ref)
            vec_ref[...] = jnp.zeros_like(vec_ref)

        dcur = fc_ref[...] + bc_ref[...]
        dprev = jnp.where(i > 0, fp_ref[...] + bp_ref[...], 0.0)
        dnext = jnp.where(i < n - 1, fn_ref[...] + bn_ref[...], 0.0)
        dext = jnp.concatenate([dprev, dcur, dnext], axis=0)
        xr_v = xr_ref[...]
        cw_v = cw_ref[...]
        dxr = None
        for k in range(4):
            shifted = _shift_rows(dext, 2 - k, TMO)
            term = cw_v[k:k + 1, :] * shifted
            dxr = term if dxr is None else dxr + term
            vec_ref[k:k + 1, :] += _rowsum(shifted * xr_v)
        vec_ref[4:5, :] += _rowsum(dcur)
        dp = jnp.concatenate([dxr.astype(BF16), dg_ref[...]], axis=-1)
        x1v = x1_ref[...]
        scale1 = 1.0 + mod_ref[1:2, :]
        h1 = (x1v * scale1 + mod_ref[0:1, :]).astype(BF16)
        dh1 = _dot_nt(dp, w_ref[...])
        dw_ref[...] += _dot_tn(h1, dp)
        dx_ref[...] = dxp_ref[...] + dh1 * scale1
        vec_ref[5:6, :] += _rowsum(dh1)
        vec_ref[6:7, :] += _rowsum(dh1 * x1v)

        @pl.when(i == n - 1)
        def _():
            for j in range(N_DEV):
                dwb_ref[j] = dw_ref[:, j * slab:(j + 1) * slab].astype(BF16)

    t = _tile(TMO, D)
    return _fused_call(
        body, comm, (dxcf, dxcf, dxcf, dxcb, dxcb, dxcb, xr, dg1, x1, dx1p, mod, w_in, cw),
        name="od_in_bwd", grid=(n,),
        in_specs=[prev_spec, t, next_spec, prev_spec, t, next_spec, t, t, t, t,
                  _full((3, D)), _full((D, OD_IN)), _full((4, D))],
        out_specs=[t, _full((N_DEV, D, slab)), _full((SUBLANE, D))],
        out_shape=[jax.ShapeDtypeStruct((T, D), F32), jax.ShapeDtypeStruct((N_DEV, D, slab), BF16),
                   jax.ShapeDtypeStruct((SUBLANE, D), F32)],
        scratch_shapes=[pltpu.VMEM((D, OD_IN), F32)])


def _ev_out_bwd(dx1, z0, out0, y0, ycat, g0, w_out, mod, lnp):
    T = dx1.shape[0]

    def body(dx_ref, z_ref, out_ref, y0_ref, yc_ref, g_ref, w_ref, mod_ref, ln_ref,
             dxp_ref, dyc_ref, dg_ref, dwb_ref, vec_ref, dw_ref):
        i = pl.program_id(0)

        @pl.when(i == 0)
        def _():
            dw_ref[...] = jnp.zeros_like(dw_ref)
            vec_ref[...] = jnp.zeros_like(vec_ref)

        lng = ln_ref[0:1, :]
        _, xhat, rstd = _ln_fwd(z_ref[...], lng, ln_ref[1:2, :])
        dy = dx_ref[...]
        dz = _ln_bwd(dy, xhat, rstd, lng)
        vec_ref[0:1, :] += _rowsum(dy * xhat)
        vec_ref[1:2, :] += _rowsum(dy)
        vec_ref[2:3, :] += _rowsum(dz * out_ref[...].astype(F32))
        dout = (dz * mod_ref[2:3, :]).astype(BF16)
        dy0 = _dot_nt(dout, w_ref[...])
        dw_ref[...] += _dot_tn(y0_ref[...], dout)
        sg, dsg = _silu_and_grad(g_ref[...].astype(F32))
        dyc_ref[...] = (dy0 * sg).astype(BF16)
        dg_ref[...] = (dy0 * yc_ref[...].astype(F32) * dsg).astype(BF16)
        dxp_ref[...] = ALPHA * dz

        @pl.when(i == T // TMO - 1)
        def _():
            dwb_ref[...] = dw_ref[...].astype(BF16)

    t = _tile(TMO, D)
    return _pallas(
        body, name="ev_out_bwd", grid=(T // TMO,),
        in_specs=[t, t, t, t, t, t, _full((D, D)), _full((3, D)), _full((2, D))],
        out_specs=[t, t, t, _full((D, D)), _full((SUBLANE, D))],
        out_shape=[jax.ShapeDtypeStruct((T, D), F32), jax.ShapeDtypeStruct((T, D), BF16),
                   jax.ShapeDtypeStruct((T, D), BF16), jax.ShapeDtypeStruct((D, D), BF16),
                   jax.ShapeDtypeStruct((SUBLANE, D), F32)],
        scratch_shapes=[pltpu.VMEM((D, D), F32)],
        compiler_params=_params(("arbitrary",)),
    )(dx1, z0, out0, y0, ycat, g0, w_out, mod, lnp)


def _mix0_bwd(q, kvx, lse, dyc, ycat, su, sv, sink_l, bias, a128, gsum, sel, sg_lng, sg_lnb, sg_w, sg_bfull,
              rc, rs1, rs2, comm=None):
    T = q.shape[0]
    nb = T // BLK

    def body(q_ref, kp_ref, kc_ref, kn_ref, lse_ref, dyc_ref, yc_ref, su_ref, sv_ref, sink_ref, bias_ref, a_ref,
             gsum_ref, sel_ref, lng_ref, lnb_ref, w_ref, bfull_ref, c_ref, s1_ref, s2_ref,
             dq_ref, dkv_ref, dsu_ref, dsv_ref, dw_ref, dbt_ref, vec_ref, dsink_ref):
        n = pl.program_id(0)

        @pl.when(n == 0)
        def _():
            dkv_ref[...] = jnp.zeros_like(dkv_ref)
            dw_ref[...] = jnp.zeros_like(dw_ref)
            dbt_ref[...] = jnp.zeros_like(dbt_ref)
            vec_ref[...] = jnp.zeros_like(vec_ref)
            dsink_ref[...] = jnp.zeros_like(dsink_ref)

        band = pl.ds(pl.multiple_of(n * BLK + (TM - BLK), BLK), 3 * BLK)
        bias = _band_bias(bias_ref, n, nb)
        kvx = jnp.concatenate([kp_ref[...], kc_ref[...], kn_ref[...]], axis=0)
        bias2 = jnp.concatenate([bias, bias], axis=1)
        low = lax.broadcasted_iota(jnp.int32, (BLK, LANE), 1) < HEAD_DIM
        low2 = lax.broadcasted_iota(jnp.int32, (2 * BLK, LANE), 1) < HEAD_DIM
        sel = sel_ref[...]
        c, s1, s2 = c_ref[...], s1_ref[...], s2_ref[...]
        for kvh in range(2):
            t0, t1 = 2 * kvh, 2 * kvh + 1
            q2 = jnp.concatenate([_lane_tile(q_ref, t0), _lane_tile(q_ref, t1)], axis=0)
            do2 = jnp.concatenate([_lane_tile(dyc_ref, t0), _lane_tile(dyc_ref, t1)], axis=0)
            yc2 = jnp.concatenate([_lane_tile(yc_ref, t0), _lane_tile(yc_ref, t1)], axis=0)
            p_hi, p_lo = _split_bf16(do2.astype(F32) * yc2.astype(F32))
            deltas = _dot_nt(sel, p_hi) + _dot_nt(sel, p_lo)
            dkx = jnp.zeros((3 * BLK, LANE), F32)
            dvx = jnp.zeros((3 * BLK, LANE), F32)
            dq_acc = None
            for par in range(2):
                heads = (4 * kvh + par, 4 * kvh + 2 + par)
                kt = 2 * kvh + par
                ke = kvx[:, kt * LANE:(kt + 1) * LANE]
                ve = kvx[:, (4 + kt) * LANE:(5 + kt) * LANE]
                lse = jnp.concatenate([lse_ref[0, :, h * LANE:(h + 1) * LANE] for h in heads], axis=1)
                sk = jnp.concatenate([_lane_tile(sink_ref, h) for h in heads], axis=1)
                delta = deltas[par:par + 1, :]
                pt = jnp.exp(_dot_nt(ke, q2) + bias2 - lse)
                dst = (pt * (_dot_nt(ve, do2) - delta)).astype(BF16)
                sink_terms = jnp.exp(sk - lse) * delta
                for k, h in enumerate(heads):
                    dsink_ref[:, h * LANE:(h + 1) * LANE] += sink_terms[:, k * LANE:(k + 1) * LANE]
                part = _dot_tn(dst, ke)
                dq_acc = part if dq_acc is None else dq_acc + part
                mine = low2 if par == 0 else jnp.logical_not(low2)
                dkx = dkx + jnp.dot(dst, jnp.where(mine, q2, jnp.zeros_like(q2)), preferred_element_type=F32)
                dvx = dvx + jnp.dot(pt.astype(BF16), jnp.where(mine, do2, jnp.zeros_like(do2)),
                                    preferred_element_type=F32)
            for k, t in enumerate((t0, t1)):
                dq_t = dq_acc[k * BLK:(k + 1) * BLK] * (HEAD_DIM ** -0.5)
                dq_ref[:, t * LANE:(t + 1) * LANE] = _rope_bwd(dq_t, c, s1, s2).astype(BF16)
            dkv_ref[band, kvh * LANE:(kvh + 1) * LANE] += dkx
            dkv_ref[band, (2 + kvh) * LANE:(3 + kvh) * LANE] += dvx

        lng = lng_ref[...]
        xhat, rstd, vb, svm = _sg_core(sv_ref, lng, lnb_ref[...], a_ref, w_ref, bfull_ref)
        dy = dyc_ref[:, ATTN_W:].astype(F32)
        dsu_ref[...] = (dy * svm).astype(BF16)
        dsvm = dy * su_ref[...].astype(F32)
        d_hi, d_lo = _split_bf16(dsvm)
        gsum = gsum_ref[...]
        dbt_ref[...] += jnp.dot(d_hi, gsum, preferred_element_type=F32) + jnp.dot(d_lo, gsum,
                                                                                 preferred_element_type=F32)
        tiles = []
        for t in range(SG_W // LANE):
            tl = slice(t * LANE, (t + 1) * LANE)
            dt, v2 = d_hi[:, tl], vb[:, tl]
            dw_ref[2 * t] += _dot_nt(jnp.where(low, dt, jnp.zeros_like(dt)), v2)
            dw_ref[2 * t + 1] += _dot_nt(jnp.where(low, jnp.zeros_like(dt), dt), v2)
            tiles.append(jnp.where(low, _dot_tn(w_ref[2 * t], dt), _dot_tn(w_ref[2 * t + 1], dt)))
        dvgn = jnp.concatenate(tiles, axis=-1)
        vec_ref[0:1, :] += _rowsum(dvgn * xhat)
        vec_ref[1:2, :] += _rowsum(dvgn)
        dxh = dvgn * lng
        m1 = _group_mean(dxh, a_ref)
        m2 = _group_mean(dxh * xhat, a_ref)
        dsv_ref[...] = (rstd * (dxh - m1 - xhat * m2)).astype(BF16)

    return _fused_call(
        body, comm, (q, kvx, kvx, kvx, lse, dyc, ycat, su, sv, sink_l, bias, a128, gsum, sel, sg_lng, sg_lnb, sg_w,
                     sg_bfull, rc, rs1, rs2),
        name="mix0_bwd", grid=(nb,),
        in_specs=[_tile(BLK, ATTN_W)] + _band_specs(KVX_W, nb) + [
            pl.BlockSpec((1, 1, N_HEADS * LANE), lambda n: (n, 0, 0)), _tile(BLK, D), _tile(BLK, D),
            _tile(BLK, SG_W), _tile(BLK, SG_W), _full((1, N_HEADS * LANE)), _full((3 * BLK, LANE)),
            _full((2 * LANE, 2 * LANE)),_full((SG_W, LANE)), _full((SUBLANE, LANE)), _full((1, SG_W)), _full((1, SG_W)),
            _full((SG_GROUPS, BLK, BLK)), _full((BLK, SG_W)), _tile(BLK, LANE), _tile(BLK, LANE), _tile(BLK, LANE)],
        out_specs=[_tile(BLK, ATTN_W), _full((T + 2 * TM, 4 * LANE)), _tile(BLK, SG_W), _tile(BLK, SG_W),
                   _full((SG_GROUPS, BLK, BLK)), _full((BLK, LANE)), _full((SUBLANE, SG_W)),
                   _full((1, N_HEADS * LANE))],
        out_shape=[jax.ShapeDtypeStruct((T, ATTN_W), BF16), jax.ShapeDtypeStruct((T + 2 * TM, 4 * LANE), F32),
                   jax.ShapeDtypeStruct((T, SG_W), BF16), jax.ShapeDtypeStruct((T, SG_W), BF16),
                   jax.ShapeDtypeStruct((SG_GROUPS, BLK, BLK), F32), jax.ShapeDtypeStruct((BLK, LANE), F32),
                   jax.ShapeDtypeStruct((SUBLANE, SG_W), F32), jax.ShapeDtypeStruct((1, N_HEADS * LANE), F32)])


def _ev_in_bwd(dq, dkv, dsu, dsv, dg0, x, dxp, mod, w_in, rc, rs1, rs2, comm=None):
    T = x.shape[0]

    def body(dq_ref, dkv_ref, dsu_ref, dsv_ref, dg_ref, x_ref, dxp_ref, mod_ref, w_ref, c_ref, s1_ref, s2_ref,
             dx_ref, dwb_ref, vec_ref, dw_ref):
        i = pl.program_id(0)

        @pl.when(i == 0)
        def _():
            dw_ref[...] = jnp.zeros_like(dw_ref)
            vec_ref[...] = jnp.zeros_like(vec_ref)

        low = lax.broadcasted_iota(jnp.int32, (TM, LANE), 1) < HEAD_DIM

        def fold(j):
            t0 = dkv_ref[:, (2 * j) * LANE:(2 * j + 1) * LANE]
            t1 = dkv_ref[:, (2 * j + 1) * LANE:(2 * j + 2) * LANE]
            return jnp.where(low, t0 + pltpu.roll(t0, HEAD_DIM, 1), t1 + pltpu.roll(t1, HEAD_DIM, 1))

        dk = _rope_bwd(fold(0), c_ref[...], s1_ref[...], s2_ref[...]).astype(BF16)
        dp = jnp.concatenate([dq_ref[...], dk, fold(1).astype(BF16), dsu_ref[...], dsv_ref[...],
                              dg_ref[...]], axis=-1)
        xv = x_ref[...]
        scale0 = 1.0 + mod_ref[1:2, :]
        h0 = (xv * scale0 + mod_ref[0:1, :]).astype(BF16)
        dh0 = _dot(dp, w_ref[...])
        dw_ref[...] += _dot_tn(dp, h0)
        dx_ref[...] = dxp_ref[...] + dh0 * scale0
        vec_ref[0:1, :] += _rowsum(dh0)
        vec_ref[1:2, :] += _rowsum(dh0 * xv)

        @pl.when(i == T // TM - 1)
        def _():
            dwb_ref[...] = dw_ref[...].astype(BF16)

    t = _tile(TM, D)
    return _fused_call(
        body, comm, (dq, dkv, dsu, dsv, dg0, x, dxp, mod, w_in, rc, rs1, rs2), name="ev_in_bwd", grid=(T // TM,),
        in_specs=[_tile(TM, ATTN_W), pl.BlockSpec((TM, 4 * LANE), lambda i: (i + 1, 0)), _tile(TM, SG_W),
                  _tile(TM, SG_W), t, t, t,
                  _full((3, D)), _full((EV_IN, D)), _tile(TM, LANE), _tile(TM, LANE), _tile(TM, LANE)],
        out_specs=[t, _full((EV_IN, D)), _full((SUBLANE, D))],
        out_shape=[jax.ShapeDtypeStruct((T, D), F32), jax.ShapeDtypeStruct((EV_IN, D), BF16),
                   jax.ShapeDtypeStruct((SUBLANE, D), F32)],
        scratch_shapes=[pltpu.VMEM((EV_IN, D), F32)])


def _sum_slots(land_ref):
    g = land_ref[0].astype(F32)
    for i in range(1, land_ref.shape[0]):
        g = g + land_ref[i].astype(F32)
    return g


def _reduce_adam(items, name):
    R, C = items[0][1].shape
    rb = R
    if R > 512:
        for cand in (512, 256, 128, 64, 32, 16, 8):
            if R % cand == 0:
                rb = cand
                break
    n = len(items)

    def body(*refs):
        for k in range(n):
            l_ref, w_ref, m_ref, v_ref = refs[4 * k:4 * k + 4]
            g_ref, d_ref, nm_ref, nv_ref = refs[4 * n + 4 * k:4 * n + 4 * k + 4]
            g = _sum_slots(l_ref)
            g_ref[...] = g
            dlt, m2, v2 = _adam(w_ref[...], g, m_ref[...], v_ref[...])
            d_ref[...] = dlt
            nm_ref[...] = m2
            nv_ref[...] = v2

    t = pl.BlockSpec((rb, C), lambda i: (i, 0))
    shp = jax.ShapeDtypeStruct((R, C), F32)
    in_specs, operands = [], []
    for land, w, m, v in items:
        in_specs += [pl.BlockSpec((land.shape[0], rb, C), lambda i: (0, i, 0)), t, t, t]
        operands += [land, w, m, v]
    res = _pallas(
        body, name=name, grid=(R // rb,),
        in_specs=in_specs, out_specs=[t] * (4 * n), out_shape=[shp] * (4 * n),
        compiler_params=_params(("parallel",)),
    )(*operands)
    return [list(res[4 * k:4 * k + 4]) for k in range(n)]


def _tail_exchange(slabs, small):
    _, R, C = slabs.shape
    n_chips = N_DEV // 2
    gather = _GatherComm(small)
    ns = gather.n

    def body(*refs):
        slab_ref = refs[0]
        g_ins = refs[1:1 + ns]
        land_ref = refs[1 + ns]
        g_outs = refs[2 + ns:2 + 2 * ns]
        stage, part, s1_send, s1_recv, s2_send, s2_recv = refs[2 + 2 * ns:8 + 2 * ns]
        g_sems = refs[8 + 2 * ns:]
        x, y, c = _my_pos()
        chip = 2 * x + y
        gather.start(g_ins, g_outs, g_sems)

        swaps = [pltpu.make_async_remote_copy(
            src_ref=slab_ref.at[2 * k + (1 - c)], dst_ref=stage.at[k], send_sem=s1_send.at[k],
            recv_sem=s1_recv.at[k], device_id=(x, y, 1 - c), device_id_type=MESH) for k in range(n_chips)]
        for cp in swaps:
            cp.start()
        for cp in swaps:
            cp.wait()
        for k in range(n_chips):
            part[k] = (slab_ref[2 * k + c].astype(F32) + stage[k].astype(F32)).astype(BF16)

        gather.mid(g_ins, g_outs, g_sems)

        sends = []
        for r in range(1, n_chips):
            px = (1 - x) if (r & 2) else x
            py = (1 - y) if (r & 1) else y
            sends.append(pltpu.make_async_remote_copy(
                src_ref=part.at[2 * px + py], dst_ref=land_ref.at[chip], send_sem=s2_send.at[r - 1],
                recv_sem=s2_recv.at[r - 1], device_id=(px, py, c), device_id_type=MESH))
        for cp in sends:
            cp.start()
        land_ref[chip] = part[chip]
        for cp in sends:
            cp.wait()
        gather.finish(g_ins, g_outs, g_sems)

    any_spec = pl.BlockSpec(memory_space=pl.ANY)
    vmem_spec = pl.BlockSpec(memory_space=pltpu.VMEM)
    res = _pallas(
        body, name="tail_exchange",
        out_shape=[jax.ShapeDtypeStruct((n_chips, R, C), BF16)] + gather.out_shapes(),
        in_specs=[vmem_spec] + [any_spec] * ns, out_specs=[vmem_spec] + [any_spec] * ns,
        scratch_shapes=[pltpu.VMEM((n_chips, R, C), BF16), pltpu.VMEM((n_chips, R, C), BF16),
                        pltpu.SemaphoreType.DMA((n_chips,)), pltpu.SemaphoreType.DMA((n_chips,)),
                        pltpu.SemaphoreType.DMA((n_chips - 1,)), pltpu.SemaphoreType.DMA((n_chips - 1,))]
        + gather.sems(),
        compiler_params=pltpu.CompilerParams(vmem_limit_bytes=VMEM_LIMIT),
    )(slabs, *gather.arrs)
    return res[0], list(res[1:])


def _tail_stage1(slabs, small):
    _, R, C = slabs.shape
    n_chips = N_DEV // 2
    gather = _GatherComm(small)
    ns = gather.n

    def body(*refs):
        slab_ref = refs[0]
        g_ins = refs[1:1 + ns]
        part, land_ref = refs[1 + ns], refs[2 + ns]
        g_outs = refs[3 + ns:3 + 2 * ns]
        stage, s1_send, s1_recv = refs[3 + 2 * ns:6 + 2 * ns]
        g_sems = refs[6 + 2 * ns:]
        x, y, c = _my_pos()
        chip = 2 * x + y
        gather.start(g_ins, g_outs, g_sems)
        swaps = [pltpu.make_async_remote_copy(
            src_ref=slab_ref.at[2 * k + (1 - c)], dst_ref=stage.at[k], send_sem=s1_send.at[k],
            recv_sem=s1_recv.at[k], device_id=(x, y, 1 - c), device_id_type=MESH) for k in range(n_chips)]
        for cp in swaps:
            cp.start()
        for cp in swaps:
            cp.wait()
        for k in range(n_chips):
            part[k] = (slab_ref[2 * k + c].astype(F32) + stage[k].astype(F32)).astype(BF16)
        land_ref[chip] = part[chip]
        gather.mid(g_ins, g_outs, g_sems)
        gather.finish(g_ins, g_outs, g_sems)

    any_spec = pl.BlockSpec(memory_space=pl.ANY)
    vmem_spec = pl.BlockSpec(memory_space=pltpu.VMEM)
    slab4 = jax.ShapeDtypeStruct((n_chips, R, C), BF16)
    res = _pallas(
        body, name="tail_stage1",
        out_shape=[slab4, slab4] + gather.out_shapes(),
        in_specs=[vmem_spec] + [any_spec] * ns, out_specs=[vmem_spec, vmem_spec] + [any_spec] * ns,
        scratch_shapes=[pltpu.VMEM((n_chips, R, C), BF16),
                        pltpu.SemaphoreType.DMA((n_chips,)), pltpu.SemaphoreType.DMA((n_chips,))] + gather.sems(),
        compiler_params=pltpu.CompilerParams(vmem_limit_bytes=VMEM_LIMIT),
    )(slabs, *gather.arrs)
    return res[0], res[1], list(res[2:])


def _chip_copies(part_ref, land_ref, send_sems, recv_sems):
    x, y, c = _my_pos()
    chip = 2 * x + y
    copies = []
    for r in range(1, N_DEV // 2):
        px = (1 - x) if (r & 2) else x
        py = (1 - y) if (r & 1) else y
        copies.append(pltpu.make_async_remote_copy(
            src_ref=part_ref.at[2 * px + py], dst_ref=land_ref.at[chip], send_sem=send_sems[r - 1],
            recv_sem=recv_sems[r - 1], device_id=(px, py, c), device_id_type=MESH))
    return copies


def _tail_send(part, land):
    n = N_DEV // 2 - 1

    def body(part_ref, land_ref, *outs):
        send_sems, recv_sems = outs[:n], outs[n:2 * n]
        token = outs[2 * n + 2]
        for cp in _chip_copies(part_ref, land_ref, send_sems, recv_sems):
            cp.start()
        token[...] = jnp.zeros_like(token)

    hbm = pl.BlockSpec(memory_space=pltpu.HBM)
    sem = pl.BlockSpec(memory_space=pltpu.SEMAPHORE)
    res = _pallas(
        body, name="tail_send",
        out_shape=tuple([pltpu.SemaphoreType.DMA(())] * (2 * n)
                        + [pltpu.HBM(part.shape, part.dtype), pltpu.HBM(land.shape, land.dtype),
                           jax.ShapeDtypeStruct((SUBLANE, LANE), F32)]),
        in_specs=(hbm, hbm), out_specs=tuple([sem] * (2 * n) + [hbm, hbm, pl.BlockSpec(memory_space=pltpu.VMEM)]),
        input_output_aliases={0: 2 * n, 1: 2 * n + 1},
        compiler_params=pltpu.CompilerParams(has_side_effects=pltpu.SideEffectType.DATAFLOW_SIDE_EFFECTING),
    )(pltpu.with_memory_space_constraint(part, pltpu.HBM), pltpu.with_memory_space_constraint(land, pltpu.HBM))
    return list(res[:n]), list(res[n:2 * n]), res[2 * n], res[2 * n + 1], res[2 * n + 2]


def _tail_wait(send_sems, recv_sems, part, land, after):
    n = len(send_sems)

    def body(part_ref, land_ref, *rest):
        ss, rs = rest[:n], rest[n:2 * n]
        for cp in _chip_copies(part_ref, land_ref, ss, rs):
            cp.wait_send()
            cp.wait_recv()

    hbm = pl.BlockSpec(memory_space=pltpu.HBM)
    sem = pl.BlockSpec(memory_space=pltpu.SEMAPHORE)
    any_spec = pl.BlockSpec(memory_space=pl.ANY)
    res = _pallas(
        body, name="tail_wait",
        out_shape=(pltpu.HBM(part.shape, part.dtype), pltpu.HBM(land.shape, land.dtype)),
        in_specs=tuple([hbm, hbm] + [sem] * (2 * n) + [any_spec] * len(after)), out_specs=(hbm, hbm),
        input_output_aliases={0: 0, 1: 1},
        compiler_params=pltpu.CompilerParams(has_side_effects=pltpu.SideEffectType.DATAFLOW_SIDE_EFFECTING),
    )(part, land, *send_sems, *recv_sems, *after)
    return res[1]


def _slots_adam(items, name):
    zeros3 = (0, 0, 0)
    in_specs, out_specs, out_shape, operands = [], [], [], []
    for land, w, m, v in items:
        inner = w.shape[-3:]
        if w.ndim == 5:
            lspec = pl.BlockSpec((N_DEV, 1) + inner, lambda i: (0, i) + zeros3)
            wspec = pl.BlockSpec((1, 1) + inner, lambda i: (0, i) + zeros3)
        else:
            lspec = pl.BlockSpec((N_DEV,) + inner, lambda i: (0,) + zeros3)
            wspec = pl.BlockSpec((1,) + inner, lambda i: (0,) + zeros3)
        in_specs += [lspec, wspec, wspec, wspec]
        out_specs += [wspec] * 4
        out_shape += [jax.ShapeDtypeStruct(w.shape, F32)] * 4
        operands += [land, w, m, v]
    n = len(items)

    def body(*refs):
        for k, (_, w, _, _) in enumerate(items):
            l_ref, w_ref, m_ref, v_ref = refs[4 * k:4 * k + 4]
            outs = refs[4 * n + 4 * k:4 * n + 4 * k + 4]
            at = (0, 0) if w.ndim == 5 else (0,)

            def update(l_ref=l_ref, w_ref=w_ref, m_ref=m_ref, v_ref=v_ref, outs=outs, at=at):
                g = l_ref[(0,) + at[1:]].astype(F32)
                for i in range(1, N_DEV):
                    g = g + l_ref[(i,) + at[1:]].astype(F32)
                dlt, m2, v2 = _adam(w_ref[at], g, m_ref[at], v_ref[at])
                for o_ref, val in zip(outs, (g, dlt, m2, v2)):
                    o_ref[at] = val

            if w.ndim == 5:
                update()
            else:
                pl.when(pl.program_id(0) == 0)(update)

    res = _pallas(
        body, name=name, grid=(2,),
        in_specs=in_specs, out_specs=out_specs, out_shape=out_shape,
        compiler_params=_params(("arbitrary",)),
    )(*operands)
    return [list(res[4 * k:4 * k + 4]) for k in range(n)]


SMALL_PARAMS = ("ln_g", "ln_b", "ev_sg_ln_g", "ev_sg_ln_b", "ev_sink", "ev_sg_b",
                "od_conv_w", "od_conv_b", "od_b_a", "od_b_x", "od_lam")


def _small_update(ga, gc, gd, gf, gb, ge, gsink, gbt, params):
    names = list(SMALL_PARAMS)
    flat = [a for nm in names for a in params[nm]]
    n_g = 8

    def body(*refs):
        ga_ref, gc_ref, gd_ref, gf_ref, gb_ref, ge_ref, gs_ref, gbt_ref = refs[:n_g]
        prm = refs[n_g:n_g + 3 * len(names)]
        loss_ref = refs[n_g + 3 * len(names)]
        outs = refs[n_g + 3 * len(names) + 1:]

        def ssum(ref):
            acc = ref[0]
            for i in range(1, N_DEV):
                acc = acc + ref[i]
            return acc

        a, cc, dd, ff, bb, ee = ssum(ga_ref), ssum(gc_ref), ssum(gd_ref), ssum(gf_ref), ssum(gb_ref), ssum(ge_ref)
        loss_ref[...] = a[3:4, 0:LANE]
        me = _slot(*_my_pos())

        def mine(rows):
            acc = jnp.zeros((rows.shape[0], LANE), F32)
            for j in range(N_DEV):
                acc = acc + jnp.where(me == j, rows[:, j * LANE:(j + 1) * LANE], 0.0)
            return acc

        sink_terms = ssum(gs_ref)
        lane8 = lax.broadcasted_iota(jnp.int32, (1, N_HEADS), 1)
        g_sink = jnp.zeros((1, N_HEADS), F32)
        for h in range(N_HEADS):
            tot = -jnp.sum(sink_terms[:, h * LANE:(h + 1) * LANE], axis=1, keepdims=True)
            g_sink = jnp.where(lane8 == h, tot, g_sink)
        grads = dict(
            ln_g=jnp.concatenate([dd[0:1], a[0:1]], axis=0), ln_b=jnp.concatenate([dd[1:2], a[1:2]], axis=0),
            ev_sg_ln_g=ee[0:1], ev_sg_ln_b=ee[1:2], ev_sink=g_sink,
            ev_sg_b=jnp.transpose(ssum(gbt_ref))[0:SG_GROUPS, :],
            od_conv_w=mine(cc[0:4]), od_conv_b=mine(cc[4:5]),
            od_b_a=mine(jnp.concatenate([ff[0:1], bb[0:1]], axis=0)),
            od_b_x=mine(jnp.concatenate([ff[1:2], bb[1:2]], axis=0)),
            od_lam=mine(jnp.concatenate([ff[2:3], bb[2:3]], axis=0)))
        for k, nm in enumerate(names):
            w_ref, m_ref, v_ref = prm[3 * k:3 * k + 3]
            at = (0,) if len(w_ref.shape) == 3 else ()
            g = grads[nm]
            dlt, m2, v2 = _adam(w_ref[at] if at else w_ref[...], g, m_ref[at] if at else m_ref[...],
                                v_ref[at] if at else v_ref[...])
            for o_ref, val in zip(outs[4 * k:4 * k + 4], (g, dlt, m2, v2)):
                if at:
                    o_ref[at] = val
                else:
                    o_ref[...] = val

    gathered = [ga, gc, gd, gf, gb, ge, gsink, gbt]
    out_shape = [jax.ShapeDtypeStruct((1, LANE), F32)]
    for nm in names:
        out_shape += [jax.ShapeDtypeStruct(params[nm][0].shape, F32)] * 4
    return _pallas(
        body, name="small_update", grid=(1,),
        in_specs=[_full(a.shape) for a in gathered + flat],
        out_specs=[_full(s.shape) for s in out_shape], out_shape=out_shape,
        compiler_params=_params(("arbitrary",)),
    )(*gathered, *flat)


VEC_ROWS = 16
VEC_LAYOUT = (("od_conv_w", 4), ("od_conv_b", 1), ("od_b_a", 2), ("od_b_x", 2), ("od_lam", 2))


def _to_slabs(full, cols_per):
    R = full.shape[0]
    return full.reshape(R, N_DEV, cols_per).transpose(1, 0, 2)


def _from_slabs(slabs):
    n, R, cp = slabs.shape
    return slabs.transpose(1, 0, 2).reshape(R, n * cp)


def kernel(x, c, positions, ada_w, ada_b, ln_g, ln_b, ev_w_in, ev_w_out, ev_sink, ev_sg_ln_g, ev_sg_ln_b, ev_sg_w, ev_sg_b, od_w_in, od_conv_w, od_conv_b, od_w_a, od_b_a, od_w_x, od_b_x, od_lam, od_w_out, loss_target, m_ada_w, m_ada_b, m_ln_g, m_ln_b, m_ev_w_in, m_ev_w_out, m_ev_sink, m_ev_sg_ln_g, m_ev_sg_ln_b, m_ev_sg_w, m_ev_sg_b, m_od_w_in, m_od_conv_w, m_od_conv_b, m_od_w_a, m_od_b_a, m_od_w_x, m_od_b_x, m_od_lam, m_od_w_out, v_ada_w, v_ada_b, v_ln_g, v_ln_b, v_ev_w_in, v_ev_w_out, v_ev_sink, v_ev_sg_ln_g, v_ev_sg_ln_b, v_ev_sg_w, v_ev_sg_b, v_od_w_in, v_od_conv_w, v_od_conv_b, v_od_w_a, v_od_b_a, v_od_w_x, v_od_b_x, v_od_lam, v_od_w_out):
    T = x.shape[1]
    me = _slot(*_my_pos())
    xs = x.reshape(T, D)
    tgt = loss_target.reshape(T, D)

    c_all, mod_all, g_vec, (g_ev_in,), (s_ev_out, s_od_in, s_od_out, sg_w, wa, wx) = _head_gather(
        c, ada_w, [ev_w_in[0].T.astype(BF16)],
        [ev_w_out[0], od_w_in[0], od_w_out[0], ev_sg_w[0], od_w_a[0], od_w_x[0]],
        [od_conv_w, od_conv_b, od_b_a, od_b_x, od_lam])
    c_all = c_all.reshape(N_DEV, D)
    w_ev_in = g_ev_in.reshape(EV_IN, D)
    vec_full = _from_slabs(g_vec)
    cw, cb = vec_full[0:4], vec_full[4:5]
    ba, bx, lam = vec_full[5:7], vec_full[7:9], vec_full[9:11]
    mod_mine = lax.dynamic_index_in_dim(mod_all, me, axis=2, keepdims=False)
    mod = mod_mine.transpose(1, 0, 2).reshape(2, 3 * D) + ada_b
    mod0 = mod[0].reshape(3, D)
    mod1 = mod[1].reshape(3, D)

    half = 8
    inv_freq = jnp.power(jnp.float32(ROPE_THETA), -jnp.arange(half, dtype=F32) / half)
    ang = positions.reshape(T).astype(F32)[:, None] * inv_freq
    cos_t = jnp.tile(jnp.cos(ang), (1, LANE // half))
    sin_t = jnp.tile(jnp.sin(ang), (1, LANE // half))
    l64 = jnp.arange(LANE) % HEAD_DIM
    rc = jnp.where(l64 < 2 * half, cos_t, 1.0)
    rs1 = jnp.where(l64 < half, -sin_t, 0.0)
    rs2 = jnp.where((l64 >= half) & (l64 < 2 * half), sin_t, 0.0)

    ln0 = jnp.stack([ln_g[0], ln_b[0]])
    ln1 = jnp.stack([ln_g[1], ln_b[1]])
    sg_lng = ev_sg_ln_g
    sg_lnb = ev_sg_ln_b
    sg_bfull = jnp.repeat(ev_sg_b[0].T, SG_DIM, axis=1)
    sink_l = jnp.repeat(ev_sink, LANE, axis=1)
    kj = jnp.arange(3 * BLK)[:, None]
    qi = jnp.arange(BLK)[None, :]
    band_bias = jnp.where(jnp.abs(kj - BLK - qi) <= BLK, 0.0, NEG_INF).astype(F32)
    lanes = jnp.arange(LANE)
    lanes2 = jnp.arange(2 * LANE)
    a128 = jnp.where(lanes2[:, None] // SG_DIM == lanes2[None, :] // SG_DIM, 1.0 / SG_DIM, 0.0).astype(BF16)
    gsum = (jnp.arange(SG_W)[:, None] // SG_DIM == lanes[None, :]).astype(BF16)
    sel = (jnp.arange(SUBLANE)[:, None] == lanes[None, :] // HEAD_DIM).astype(BF16)

    (q, kvx, su, sv, g0), _ = _ev_in(xs, mod0, w_ev_in, rc, rs1, rs2)
    (ycat, y0, lse), (g_ev_out, g_od_in, g_od_out) = _mix0_fwd(
        q, kvx, su, sv, g0, sink_l, band_bias, a128, sg_lng, sg_lnb, sg_w, sg_bfull,
        _GatherComm([s_ev_out, s_od_in, s_od_out], mid_frac=0.75))
    w_ev_out = g_ev_out.reshape(D, D)
    w_od_in = _from_slabs(g_od_in)
    w_od_out = g_od_out.reshape(D, D)
    out0, z0, x1 = _ev_out(y0, w_ev_out, xs, mod0, ln0)
    xr, g1 = _od_in(x1, mod1, w_od_in)
    fwd_f = _rglru_fwd(xr, cw, cb, wa[0], wx[0], ba[0:1], bx[0:1], lam[0:1], False, "rglru_fwd_f")
    fwd_b = _rglru_fwd(xr, cw, cb, wa[1], wx[1], ba[1:2], bx[1:2], lam[1:2], True, "rglru_fwd_b")
    dh, dg1, dx1p, d_od_out, vec_a = _od_out(fwd_f[0], fwd_b[0], g1, w_od_out, x1, tgt, mod1, ln1)

    (dxcf, dwa_f, dwx_f, vec_f), (l_od_out,) = _rglru_bwd(
        fwd_f, dh, wa[0], wx[0], lam[0:1], False, "rglru_bwd_f",
        _ExchangeComm([d_od_out.reshape(N_DEV, D // N_DEV, D)]))
    (dxcb, dwa_b, dwx_b, vec_b), _ = _rglru_bwd(fwd_b, dh, wa[1], wx[1], lam[1:2], True, "rglru_bwd_b")
    (dx1, d_od_in, vec_c), (a_wa, a_wx) = _od_in_bwd(
        dxcf, dxcb, xr, dg1, x1, dx1p, mod1, w_od_in, cw,
        _GatherComm([jnp.stack([dwa_f, dwa_b]).astype(BF16), jnp.stack([dwx_f, dwx_b]).astype(BF16)],
                    mid_frac=0.75))
    dxp, dyc, dg0, d_ev_out, vec_d = _ev_out_bwd(dx1, z0, out0, y0, ycat, g0, w_ev_out, mod0, ln0)
    (dq, dkv, dsu, dsv, d_sg_w, d_sg_bt, vec_e, d_sink_l), (l_od_in, l_ev_out) = _mix0_bwd(
        q, kvx, lse, dyc, ycat, su, sv, sink_l, band_bias, a128, gsum, sel, sg_lng, sg_lnb, sg_w, sg_bfull,
        rc, rs1, rs2, _ExchangeComm([d_od_in, d_ev_out.reshape(N_DEV, D // N_DEV, D)]))
    (grad_x, d_ev_in, vec_g), _ = _ev_in_bwd(dq, dkv, dsu, dsv, dg0, xs, dxp, mod0, w_ev_in, rc, rs1, rs2)

    part, land, (ga, gc, gd, gf, gb, gg, ge, gsink, gbt, a_sgw) = _tail_stage1(
        d_ev_in.reshape(N_DEV, EV_IN // N_DEV, D),
        [vec_a, vec_c, vec_d, vec_f, vec_b, vec_g, vec_e, d_sink_l, d_sg_bt, d_sg_w.astype(BF16)])
    send_sems, recv_sems, part, land, token = _tail_send(part, land)
    (l_od_in, l_ev_out, l_od_out, a_wa, a_wx, ga), _ = lax.optimization_barrier(
        ((l_od_in, l_ev_out, l_od_out, a_wa, a_wx, ga), token))

    dmod_all = jnp.stack([jnp.concatenate([gg[:, 0], gg[:, 1], gd[:, 2]], axis=-1),
                          jnp.concatenate([gc[:, 5], gc[:, 6], ga[:, 2]], axis=-1)], axis=1)
    cols = ada_w.shape[2]
    dmod_cols = lax.dynamic_slice_in_dim(dmod_all, me * cols, cols, axis=2).transpose(1, 0, 2)
    (g_ada_w, d_ada_w, nm_ada_w, nv_ada_w, g_ada_b, d_ada_b, nm_ada_b, nv_ada_b) = _ada_update(
        c_all, dmod_cols, dmod_all, ada_w, m_ada_w, v_ada_w, ada_b, m_ada_b, v_ada_b)

    res = dict(ada_w=[g_ada_w, d_ada_w, nm_ada_w, nv_ada_w], ada_b=[g_ada_b, d_ada_b, nm_ada_b, nv_ada_b])
    (r_od_in,) = _reduce_adam([(l_od_in, od_w_in[0], m_od_w_in[0], v_od_w_in[0])], "adam_od_w_in")
    r_ev_out, r_od_out = _reduce_adam([(l_ev_out, ev_w_out[0], m_ev_w_out[0], v_ev_w_out[0]),
                                       (l_od_out, od_w_out[0], m_od_w_out[0], v_od_w_out[0])], "adam_w_out")
    for name, r in (("od_w_in", r_od_in), ("ev_w_out", r_ev_out), ("od_w_out", r_od_out)):
        res[name] = [a[None] for a in r]
    res["od_w_a"], res["od_w_x"], res["ev_sg_w"] = _slots_adam(
        [(a_wa, od_w_a, m_od_w_a, v_od_w_a), (a_wx, od_w_x, m_od_w_x, v_od_w_x),
         (a_sgw, ev_sg_w, m_ev_sg_w, v_ev_sg_w)], "adam_gates")
    small = dict(ln_g=(ln_g, m_ln_g, v_ln_g), ln_b=(ln_b, m_ln_b, v_ln_b),
                 ev_sg_ln_g=(ev_sg_ln_g, m_ev_sg_ln_g, v_ev_sg_ln_g),
                 ev_sg_ln_b=(ev_sg_ln_b, m_ev_sg_ln_b, v_ev_sg_ln_b),
                 ev_sink=(ev_sink, m_ev_sink, v_ev_sink), ev_sg_b=(ev_sg_b, m_ev_sg_b, v_ev_sg_b),
                 od_conv_w=(od_conv_w, m_od_conv_w, v_od_conv_w), od_conv_b=(od_conv_b, m_od_conv_b, v_od_conv_b),
                 od_b_a=(od_b_a, m_od_b_a, v_od_b_a), od_b_x=(od_b_x, m_od_b_x, v_od_b_x),
                 od_lam=(od_lam, m_od_lam, v_od_lam))
    small_out = _small_update(ga, gc, gd, gf, gb, ge, gsink, gbt, small)
    l_ev_in = _tail_wait(send_sems, recv_sems, part, land,
                         [r_od_in[0], r_od_out[0], res["od_w_x"][0], g_ada_w, small_out[0]])
    (r_ev_in,) = _reduce_adam([(l_ev_in, ev_w_in[0].T, m_ev_w_in[0].T, v_ev_w_in[0].T)], "adam_ev_w_in")
    res["ev_w_in"] = [a.T[None] for a in r_ev_in]
    loss = small_out[0][0, 0]
    for k, name in enumerate(SMALL_PARAMS):
        res[name] = small_out[1 + 4 * k:5 + 4 * k]

    order = ["ada_w", "ada_b", "ln_g", "ln_b", "ev_w_in", "ev_w_out", "ev_sink", "ev_sg_ln_g", "ev_sg_ln_b",
             "ev_sg_w", "ev_sg_b", "od_w_in", "od_conv_w", "od_conv_b", "od_w_a", "od_b_a", "od_w_x", "od_b_x",
             "od_lam", "od_w_out"]
    outs = [loss, grad_x.reshape(1, T, D)]
    for kind in range(4):
        outs += [res[name][kind] for name in order]
    return tuple(outs)
```

```python
import functools

import jax
import jax.numpy as jnp
from jax import lax
from jax.experimental import pallas as pl
from jax.experimental.pallas import tpu as pltpu

F32 = jnp.float32
BF16 = jnp.bfloat16

N_DEV = 8
D = 1024
N_HEADS = 8
HEAD_DIM = 64
KV_WIDTH = 128
ATTN_W = 512
SG_W = 512
SG_GROUPS = 8
SG_DIM = 64
BLK = 128
KVX_W = 1024
EV_IN = 2816
OD_IN = 2048
RNN_HEADS = 8
RNN_HD = 128
ALPHA = 4.0 ** 0.25
LN_EPS = 1e-5
NEG_INF = -1e30
RG_C = 8.0
ROPE_THETA = 500000.0
LR, B1, B2, EPS, WD, STEP = 0.001, 0.9, 0.999, 1e-08, 0.01, 10

LANE = 128
SUBLANE = 8
TM = 256
TMF = 512
TMO = 512
TS = 256
VMEM_LIMIT = 56 * 1024 * 1024

MESH = pl.DeviceIdType.MESH


def _pallas(body, **kw):
    return pl.pallas_call(body, **kw)


def _params(sem, vmem=VMEM_LIMIT):
    return pltpu.CompilerParams(dimension_semantics=sem, vmem_limit_bytes=vmem)


def _sigmoid(x):
    return 0.5 * jnp.tanh(0.5 * x) + 0.5


def _silu_and_grad(x):
    s = _sigmoid(x)
    return x * s, s * (1.0 + x * (1.0 - s))


def _dot(a, b):
    return jnp.dot(a.astype(BF16), b.astype(BF16), preferred_element_type=F32)


def _dot_nt(a, b):
    return lax.dot_general(a.astype(BF16), b.astype(BF16), (((1,), (1,)), ((), ())), preferred_element_type=F32)


def _dot_tn(a, b):
    return lax.dot_general(a.astype(BF16), b.astype(BF16), (((0,), (0,)), ((), ())), preferred_element_type=F32)


def _ln_fwd(z, g, b):
    mu = jnp.mean(z, axis=-1, keepdims=True)
    zc = z - mu
    var = jnp.mean(zc * zc, axis=-1, keepdims=True)
    rstd = lax.rsqrt(var + LN_EPS)
    xhat = zc * rstd
    return xhat * g + b, xhat, rstd


def _ln_bwd(dy, xhat, rstd, g):
    dxh = dy * g
    m1 = jnp.mean(dxh, axis=-1, keepdims=True)
    m2 = jnp.mean(dxh * xhat, axis=-1, keepdims=True)
    return rstd * (dxh - m1 - xhat * m2)


def _rowsum(v):
    return jnp.sum(v, axis=0, keepdims=True)


def _rope_fwd(t, c, s1, s2):
    return t * c + pltpu.roll(t, LANE - 8, 1) * s1 + pltpu.roll(t, 8, 1) * s2


def _rope_bwd(d, c, s1, s2):
    return d * c + pltpu.roll(d * s1, 8, 1) + pltpu.roll(d * s2, LANE - 8, 1)


def _adam(w, g, m, v):
    m2 = B1 * m + (1.0 - B1) * g
    v2 = B2 * v + (1.0 - B2) * (g * g)
    m_hat = m2 / (1.0 - B1 ** STEP)
    v_hat = v2 / (1.0 - B2 ** STEP)
    delta = -LR * (m_hat / (jnp.sqrt(v_hat) + EPS) + WD * w)
    return delta, m2, v2


def _tile(rows, width):
    return pl.BlockSpec((rows, width), lambda i: (i, 0))


def _full(shape):
    zeros = (0,) * len(shape)
    return pl.BlockSpec(shape, lambda i: zeros)


def _rev_tile(rows, width, n, reverse):
    if reverse:
        return pl.BlockSpec((rows, width), lambda i: (n - 1 - i, 0))
    return pl.BlockSpec((rows, width), lambda i: (i, 0))


def _halo_specs(rows, width, n, total_rows, reverse):
    per = rows // SUBLANE
    last = total_rows // SUBLANE - 1

    def tile_of(i):
        return (n - 1 - i) if reverse else i

    prev = pl.BlockSpec((SUBLANE, width), lambda i: (jnp.maximum(tile_of(i) * per - 1, 0), 0))
    nxt = pl.BlockSpec((SUBLANE, width), lambda i: (jnp.minimum((tile_of(i) + 1) * per, last), 0))
    return prev, nxt


def _my_pos():
    return lax.axis_index("x"), lax.axis_index("y"), lax.axis_index("c")


def _slot(px, py, pc):
    return 4 * px + 2 * py + pc


class _GatherComm:
    has_mid = True

    def __init__(self, arrs, mid_frac=0.5):
        self.arrs = list(arrs)
        self.n = len(self.arrs)
        self.mid_frac = mid_frac

    def out_shapes(self):
        return [jax.ShapeDtypeStruct((N_DEV,) + a.shape, a.dtype) for a in self.arrs]

    def sems(self):
        return [pltpu.SemaphoreType.DMA((7 * self.n,)), pltpu.SemaphoreType.DMA((7 * self.n,)),
                pltpu.SemaphoreType.DMA((self.n,))]

    def _parts(self, ins, outs, sems):
        send_sems, recv_sems, local_sems = sems
        x, y, c = _my_pos()
        me, sibling = (x, y, c), (x, y, 1 - c)
        chips = [(1 - x, y), (x, 1 - y), (1 - x, 1 - y)]

        def copy(a, k, block, to, src=None):
            dst = outs[a].at[_slot(*block)]
            return pltpu.make_async_remote_copy(
                src_ref=dst if src is None else src, dst_ref=dst,
                send_sem=send_sems.at[a * 7 + k], recv_sem=recv_sems.at[a * 7 + k],
                device_id=to, device_id_type=MESH)

        local = [pltpu.make_async_copy(ins[a], outs[a].at[_slot(*me)], local_sems.at[a]) for a in range(self.n)]
        first = []
        for a in range(self.n):
            first.append(copy(a, 0, me, sibling, src=ins[a]))
            first += [copy(a, 1 + j, me, (*chip, c), src=ins[a]) for j, chip in enumerate(chips)]
        ici_in = [copy(a, 1 + j, (*chip, c), me) for j, chip in enumerate(chips) for a in range(self.n)]
        passed = [copy(a, 4 + j, (*chip, c), sibling) for j, chip in enumerate(chips) for a in range(self.n)]
        d2d_in = []
        for a in range(self.n):
            d2d_in.append(copy(a, 0, sibling, me))
            d2d_in += [copy(a, 4 + j, (*chip, 1 - c), me) for j, chip in enumerate(chips)]
        return local, first, ici_in, passed, d2d_in

    def start(self, ins, outs, sems):
        local, first, _, _, _ = self._parts(ins, outs, sems)
        for cp in local + first:
            cp.start()

    def mid(self, ins, outs, sems):
        _, _, ici_in, passed, _ = self._parts(ins, outs, sems)
        for arrived, fw in zip(ici_in, passed):
            arrived.wait_recv()
            fw.start()

    def finish(self, ins, outs, sems):
        local, first, _, passed, d2d_in = self._parts(ins, outs, sems)
        for cp in d2d_in:
            cp.wait_recv()
        for cp in first + passed:
            cp.wait_send()
        for cp in local:
            cp.wait()


class _ExchangeComm:
    has_mid = False

    def __init__(self, arrs):
        self.arrs = list(arrs)
        self.n = len(self.arrs)

    def out_shapes(self):
        return [jax.ShapeDtypeStruct(a.shape, a.dtype) for a in self.arrs]

    def sems(self):
        return [pltpu.SemaphoreType.DMA((7 * self.n,)), pltpu.SemaphoreType.DMA((7 * self.n,)),
                pltpu.SemaphoreType.DMA((self.n,))]

    def _copies(self, ins, outs, sems):
        send_sems, recv_sems, local_sems = sems
        x, y, c = _my_pos()
        mine = _slot(x, y, c)
        copies = [pltpu.make_async_copy(ins[a].at[mine], outs[a].at[mine], local_sems.at[a]) for a in range(self.n)]
        for k in range(1, N_DEV):
            px = (1 - x) if (k & 4) else x
            py = (1 - y) if (k & 2) else y
            pc = (1 - c) if (k & 1) else c
            for a in range(self.n):
                copies.append(pltpu.make_async_remote_copy(
                    src_ref=ins[a].at[_slot(px, py, pc)], dst_ref=outs[a].at[mine],
                    send_sem=send_sems.at[a * 7 + k - 1], recv_sem=recv_sems.at[a * 7 + k - 1],
                    device_id=(px, py, pc), device_id_type=MESH))
        return copies

    def start(self, ins, outs, sems):
        for cp in self._copies(ins, outs, sems):
            cp.start()

    def finish(self, ins, outs, sems):
        for cp in self._copies(ins, outs, sems):
            cp.wait()


def _fused_call(body, comm, operands, *, name, grid, in_specs, out_specs, out_shape, scratch_shapes=(),
                semantics=("arbitrary",)):
    n_in, n_out, n_scr = len(in_specs), len(out_specs), len(scratch_shapes)
    if comm is None:
        res = _pallas(body, name=name, grid=grid, in_specs=list(in_specs), out_specs=list(out_specs),
                      out_shape=list(out_shape), scratch_shapes=list(scratch_shapes),
                      compiler_params=_params(semantics))(*operands)
        return list(res), []
    k = comm.n
    steps = grid[0]

    def wrapped(*refs):
        ins, cins = refs[:n_in], refs[n_in:n_in + k]
        outs = refs[n_in + k:n_in + k + n_out]
        couts = refs[n_in + k + n_out:n_in + 2 * k + n_out]
        rest = refs[n_in + 2 * k + n_out:]
        scratch, sems = rest[:n_scr], rest[n_scr:]
        i = pl.program_id(0)

        @pl.when(i == 0)
        def _():
            comm.start(cins, couts, sems)

        body(*ins, *outs, *scratch)

        if comm.has_mid:
            @pl.when(i == int(steps * comm.mid_frac))
            def _():
                comm.mid(cins, couts, sems)

        @pl.when(i == steps - 1)
        def _():
            comm.finish(cins, couts, sems)

    any_spec = pl.BlockSpec(memory_space=pl.ANY)
    res = _pallas(wrapped, name=name, grid=grid, in_specs=list(in_specs) + [any_spec] * k,
                  out_specs=list(out_specs) + [any_spec] * k, out_shape=list(out_shape) + comm.out_shapes(),
                  scratch_shapes=list(scratch_shapes) + comm.sems(),
                  compiler_params=_params(("arbitrary",)))(*operands, *comm.arrs)
    return list(res[:n_out]), list(res[n_out:])


def _head_gather(c, ada_w, big, to_cast, vec_parts):
    cols = ada_w.shape[2]
    g_c, g_big = _GatherComm([c]), _GatherComm(big)
    g_mod = _GatherComm([jax.ShapeDtypeStruct((2, N_DEV, cols), F32)])
    g_vec = _GatherComm([jax.ShapeDtypeStruct((VEC_ROWS, LANE), F32)])
    nb, nc, nv = g_big.n, len(to_cast), len(vec_parts)

    def body(*refs):
        c_ref, w_ref = refs[0], refs[1]
        vec_in = refs[2:2 + nv]
        cast_in = refs[2 + nv:2 + nv + nc]
        big_in = refs[2 + nv + nc:2 + nv + nc + nb]
        outs = refs[2 + nv + nc + nb:]
        c_all_ref, mod_all_ref, vec_all_ref = outs[0], outs[1], outs[2]
        cast_out = outs[3:3 + nc]
        big_out = outs[3 + nc:3 + nc + nb]
        part_ref, pack_ref = outs[3 + nc + nb], outs[4 + nc + nb]
        sems = outs[5 + nc + nb:]
        s_c, s_mod, s_big, s_vec = sems[0:3], sems[3:6], sems[6:9], sems[9:12]
        g_c.start([c_ref], [c_all_ref], s_c)
        g_big.start(big_in, big_out, s_big)
        pack_ref[...] = jnp.zeros_like(pack_ref)
        row = 0
        for ref, (_, nrows) in zip(vec_in, VEC_LAYOUT):
            pack_ref[row:row + nrows, :] = ref[0] if len(ref.shape) == 3 else ref[...]
            row += nrows
        g_vec.start([pack_ref], [vec_all_ref], s_vec)
        g_c.mid([c_ref], [c_all_ref], s_c)
        g_c.finish([c_ref], [c_all_ref], s_c)
        cv = c_all_ref[:, 0, :]
        cond = cv * _sigmoid(cv)
        for l in range(2):
            part_ref[l] = _dot(cond, w_ref[l])
        g_mod.start([part_ref], [mod_all_ref], s_mod)
        for src, dst in zip(cast_in, cast_out):
            dst[...] = src[...].astype(BF16)
        for g, ins, outs_, sm in ((g_vec, [pack_ref], [vec_all_ref], s_vec), (g_mod, [part_ref], [mod_all_ref], s_mod),
                                  (g_big, big_in, big_out, s_big)):
            g.mid(ins, outs_, sm)
            g.finish(ins, outs_, sm)

    any_spec = pl.BlockSpec(memory_space=pl.ANY)
    vmem_spec = pl.BlockSpec(memory_space=pltpu.VMEM)
    res = _pallas(
        body, name="head_gather",
        out_shape=(g_c.out_shapes() + g_mod.out_shapes() + g_vec.out_shapes()
                   + [jax.ShapeDtypeStruct(a.shape, BF16) for a in to_cast] + g_big.out_shapes()),
        in_specs=[vmem_spec] * (2 + nv + nc) + [any_spec] * nb,
        out_specs=[vmem_spec] * (3 + nc) + [any_spec] * nb,
        scratch_shapes=[pltpu.VMEM((2, N_DEV, cols), F32), pltpu.VMEM((VEC_ROWS, LANE), F32)]
        + g_c.sems() + g_mod.sems() + g_big.sems() + g_vec.sems(),
        compiler_params=pltpu.CompilerParams(vmem_limit_bytes=VMEM_LIMIT),
    )(c, ada_w, *vec_parts, *to_cast, *big)
    return res[0], res[1], res[2], list(res[3 + nc:]), list(res[3:3 + nc])


def _ada_update(c_all, dmod_cols, dmod_all, ada_w, m_w, v_w, ada_b, m_b, v_b):
    cols = ada_w.shape[2]
    nb = ada_b.shape[1]

    def body(c_ref, dmc_ref, dma_ref, w_ref, mw_ref, vw_ref, b_ref, mb_ref, vb_ref,
             gw_ref, dw_ref, nmw_ref, nvw_ref, gb_ref, db_ref, nmb_ref, nvb_ref):
        cv = c_ref[...]
        cond = cv * _sigmoid(cv)
        for l in range(2):
            g = _dot_tn(cond, dmc_ref[l])
            gw_ref[l] = g
            dlt, m2, v2 = _adam(w_ref[l], g, mw_ref[l], vw_ref[l])
            dw_ref[l] = dlt
            nmw_ref[l] = m2
            nvw_ref[l] = v2
        gb = dma_ref[0]
        for i in range(1, N_DEV):
            gb = gb + dma_ref[i]
        gb_ref[...] = gb
        dlt, m2, v2 = _adam(b_ref[...], gb, mb_ref[...], vb_ref[...])
        db_ref[...] = dlt
        nmb_ref[...] = m2
        nvb_ref[...] = v2

    wspec = _full((2, D, cols))
    bspec = _full((2, nb))
    wshape = jax.ShapeDtypeStruct((2, D, cols), F32)
    bshape = jax.ShapeDtypeStruct((2, nb), F32)
    return _pallas(
        body, name="ada_update", grid=(1,),
        in_specs=[_full((N_DEV, D)), _full((2, N_DEV, cols)), _full((N_DEV, 2, nb)),
                  wspec, wspec, wspec, bspec, bspec, bspec],
        out_specs=[wspec] * 4 + [bspec] * 4,
        out_shape=[wshape] * 4 + [bshape] * 4,
        compiler_params=_params(("arbitrary",)),
    )(c_all, dmod_cols, dmod_all, ada_w, m_w, v_w, ada_b, m_b, v_b)


def _ev_in(x, mod, w_in, rc, rs1, rs2, comm=None):
    T = x.shape[0]

    def body(x_ref, mod_ref, w_ref, c_ref, s1_ref, s2_ref, q_ref, kv_ref, su_ref, sv_ref, g_ref):
        h = x_ref[...] * (1.0 + mod_ref[1:2, :]) + mod_ref[0:1, :]
        p = _dot_nt(h, w_ref[...])
        c, s1, s2 = c_ref[...], s1_ref[...], s2_ref[...]
        for j in range(ATTN_W // LANE):
            qr = _rope_fwd(p[:, j * LANE:(j + 1) * LANE], c, s1, s2)
            q_ref[:, j * LANE:(j + 1) * LANE] = (qr * (HEAD_DIM ** -0.5)).astype(BF16)
        low = lax.broadcasted_iota(jnp.int32, (TMF, LANE), 1) < HEAD_DIM
        for j, val in enumerate((_rope_fwd(p[:, 512:640], c, s1, s2), p[:, 640:768])):
            swapped = pltpu.roll(val, HEAD_DIM, 1)
            tiles = (jnp.where(low, val, 0.0), jnp.where(low, 0.0, swapped),
                     jnp.where(low, swapped, 0.0), jnp.where(low, 0.0, val))
            for k, tile in enumerate(tiles):
                kv_ref[:, (4 * j + k) * LANE:(4 * j + k + 1) * LANE] = tile.astype(BF16)
        su_ref[...] = p[:, 768:1280].astype(BF16)
        sv_ref[...] = p[:, 1280:1792].astype(BF16)
        g_ref[...] = p[:, 1792:2816].astype(BF16)

    sh = lambda w: jax.ShapeDtypeStruct((T, w), BF16)
    return _fused_call(
        body, comm, (x, mod, w_in, rc, rs1, rs2), name="ev_in", grid=(T // TMF,),
        in_specs=[_tile(TMF, D), _full((3, D)), _full((EV_IN, D)), _tile(TMF, LANE), _tile(TMF, LANE),
                  _tile(TMF, LANE)],
        out_specs=[_tile(TMF, ATTN_W), _tile(TMF, KVX_W), _tile(TMF, SG_W), _tile(TMF, SG_W), _tile(TMF, D)],
        out_shape=[sh(ATTN_W), sh(KVX_W), sh(SG_W), sh(SG_W), sh(D)], semantics=("parallel",))


def _band_specs(width, nb):
    return [pl.BlockSpec((BLK, width), lambda n: (jnp.maximum(n - 1, 0), 0)),
            pl.BlockSpec((BLK, width), lambda n: (n, 0)),
            pl.BlockSpec((BLK, width), lambda n: (jnp.minimum(n + 1, nb - 1), 0))]


def _band_bias(bias_ref, n, nb):
    rows = lax.broadcasted_iota(jnp.int32, (3 * BLK, 1), 0)
    outside = ((rows < BLK) & (n == 0)) | ((rows >= 2 * BLK) & (n == nb - 1))
    return bias_ref[...] + jnp.where(outside, NEG_INF, 0.0)


def _lane_tile(ref, t):
    return ref[:, t * LANE:(t + 1) * LANE]


def _split_bf16(v):
    hi = v.astype(BF16)
    return hi, (v - hi.astype(F32)).astype(BF16)


def _group_mean(v, a_ref, exact_bf16=False):
    hi, lo = _split_bf16(v)
    a = a_ref[...]
    out = []
    for t in range(SG_W // (2 * LANE)):
        sl = slice(t * 2 * LANE, (t + 1) * 2 * LANE)
        r = jnp.dot(hi[:, sl], a, preferred_element_type=F32)
        if not exact_bf16:
            r = r + jnp.dot(lo[:, sl], a, preferred_element_type=F32)
        out.append(r)
    return jnp.concatenate(out, axis=-1)


def _sg_core(sv_ref, lng, lnb, a_ref, w_ref, bfull_ref):
    svf = sv_ref[...].astype(F32)
    xc = svf - _group_mean(svf, a_ref, exact_bf16=True)
    rstd = lax.rsqrt(_group_mean(xc * xc, a_ref) + LN_EPS)
    xhat = xc * rstd
    vb = (xhat * lng + lnb).astype(BF16)
    low = lax.broadcasted_iota(jnp.int32, (BLK, LANE), 1) < SG_DIM
    tiles = []
    for t in range(SG_W // LANE):
        v2 = vb[:, t * LANE:(t + 1) * LANE]
        r0 = jnp.dot(w_ref[2 * t], v2, preferred_element_type=F32)
        r1 = jnp.dot(w_ref[2 * t + 1], v2, preferred_element_type=F32)
        tiles.append(jnp.where(low, r0, r1))
    svm = jnp.concatenate(tiles, axis=-1) + bfull_ref[...]
    return xhat, rstd, vb, svm


def _mix0_fwd(q, kvx, su, sv, g0, sink_l, bias, a128, sg_lng, sg_lnb, sg_w, sg_bfull, comm=None):
    T = q.shape[0]
    nb = T // BLK

    def body(q_ref, kp_ref, kc_ref, kn_ref, su_ref, sv_ref, g_ref, sink_ref, bias_ref, a_ref, lng_ref, lnb_ref,
             w_ref, bfull_ref, ycat_ref, y0_ref, lse_ref):
        n = pl.program_id(0)
        bias = _band_bias(bias_ref, n, nb)
        kvx = jnp.concatenate([kp_ref[...], kc_ref[...], kn_ref[...]], axis=0)
        tiles = []
        for t in range(ATTN_W // LANE):
            qt = _lane_tile(q_ref, t)
            acc = None
            for par in range(2):
                h = 2 * t + par
                kt = 2 * (h // 4) + par
                ke = kvx[:, kt * LANE:(kt + 1) * LANE]
                ve = kvx[:, (4 + kt) * LANE:(5 + kt) * LANE]
                st = _dot_nt(ke, qt) + bias
                sk = _lane_tile(sink_ref, h)
                m = jnp.maximum(jnp.max(st, axis=0, keepdims=True), sk)
                p = jnp.exp(st - m)
                denom = jnp.sum(p, axis=0, keepdims=True) + jnp.exp(sk - m)
                contrib = _dot_tn(p * (1.0 / denom), ve)
                acc = contrib if acc is None else acc + contrib
                lse_ref[0, :, h * LANE:(h + 1) * LANE] = m + jnp.log(denom)
            tiles.append(acc)
        _, _, _, svm = _sg_core(sv_ref, lng_ref[...], lnb_ref[...], a_ref, w_ref, bfull_ref)
        tiles.append(su_ref[...].astype(F32) * svm)
        ycat = jnp.concatenate(tiles, axis=-1)
        gf = g_ref[...].astype(F32)
        ycat_ref[...] = ycat.astype(BF16)
        y0_ref[...] = (ycat * (gf * _sigmoid(gf))).astype(BF16)

    return _fused_call(
        body, comm, (q, kvx, kvx, kvx, su, sv, g0, sink_l, bias, a128, sg_lng, sg_lnb, sg_w, sg_bfull),
        name="mix0_fwd", grid=(nb,),
        in_specs=[_tile(BLK, ATTN_W)] + _band_specs(KVX_W, nb) + [
            _tile(BLK, SG_W), _tile(BLK, SG_W), _tile(BLK, D), _full((1, N_HEADS * LANE)), _full((3 * BLK, LANE)),
            _full((2 * LANE, 2 * LANE)),_full((1, SG_W)), _full((1, SG_W)), _full((SG_GROUPS, BLK, BLK)),
            _full((BLK, SG_W))],
        out_specs=[_tile(BLK, D), _tile(BLK, D), pl.BlockSpec((1, 1, N_HEADS * LANE), lambda n: (n, 0, 0))],
        out_shape=[jax.ShapeDtypeStruct((T, D), BF16), jax.ShapeDtypeStruct((T, D), BF16),
                   jax.ShapeDtypeStruct((nb, 1, N_HEADS * LANE), F32)], semantics=("parallel",))


def _ev_out(y0, w_out, x, mod, lnp):
    T = x.shape[0]

    def body(y_ref, w_ref, x_ref, mod_ref, ln_ref, out_ref, z_ref, x1_ref):
        out = _dot(y_ref[...], w_ref[...])
        z = ALPHA * x_ref[...] + mod_ref[2:3, :] * out
        x1, _, _ = _ln_fwd(z, ln_ref[0:1, :], ln_ref[1:2, :])
        out_ref[...] = out.astype(BF16)
        z_ref[...] = z
        x1_ref[...] = x1

    return _pallas(
        body, name="ev_out", grid=(T // TMF,),
        in_specs=[_tile(TMF, D), _full((D, D)), _tile(TMF, D), _full((3, D)), _full((2, D))],
        out_specs=[_tile(TMF, D)] * 3,
        out_shape=[jax.ShapeDtypeStruct((T, D), BF16), jax.ShapeDtypeStruct((T, D), F32),
                   jax.ShapeDtypeStruct((T, D), F32)],
        compiler_params=_params(("parallel",)),
    )(y0, w_out, x, mod, lnp)


def _od_in(x1, mod, w_in):
    T = x1.shape[0]

    def body(x_ref, mod_ref, w_ref, xr_ref, g_ref):
        h = x_ref[...] * (1.0 + mod_ref[1:2, :]) + mod_ref[0:1, :]
        p = _dot(h, w_ref[...])
        xr_ref[...] = p[:, :D]
        g_ref[...] = p[:, D:].astype(BF16)

    return _pallas(
        body, name="od_in", grid=(T // TMF,),
        in_specs=[_tile(TMF, D), _full((3, D)), _full((D, OD_IN))],
        out_specs=[_tile(TMF, D), _tile(TMF, D)],
        out_shape=[jax.ShapeDtypeStruct((T, D), F32), jax.ShapeDtypeStruct((T, D), BF16)],
        compiler_params=_params(("parallel",)),
    )(x1, mod, w_in)


def _ext_rows(prev_ref, cur, next_ref, j, n):
    prev = jnp.where(j > 0, prev_ref[...], 0.0)
    nxt = jnp.where(j < n - 1, next_ref[...], 0.0)
    return jnp.concatenate([prev, cur, nxt], axis=0)


def _shift_rows(ext, off, rows):
    total = ext.shape[0]
    if off == 0:
        return ext[SUBLANE:SUBLANE + rows, :]
    return pltpu.roll(ext, (-off) % total, 0)[SUBLANE:SUBLANE + rows, :]


def _conv_fwd(ext, cw, cb, rows):
    xc = cb
    for k in range(4):
        xc = xc + cw[k:k + 1, :] * _shift_rows(ext, k - 2, rows)
    return xc


def _gates(xc, wa_ref, wx_ref, ba, bx, lam):
    pr, pi = [], []
    for h in range(RNN_HEADS):
        xh = xc[:, h * RNN_HD:(h + 1) * RNN_HD].astype(BF16)
        pr.append(_dot(xh, wa_ref[h]))
        pi.append(_dot(xh, wx_ref[h]))
    r = _sigmoid(jnp.concatenate(pr, axis=-1) + ba)
    ig = _sigmoid(jnp.concatenate(pi, axis=-1) + bx)
    sp = jnp.maximum(-lam, 0.0) + jnp.log(1.0 + jnp.exp(-jnp.abs(lam)))
    neg_log_a = RG_C * r * sp
    a = jnp.exp(-neg_log_a)
    s2 = (1.0 + a * a) * jnp.tanh(neg_log_a)
    inv_s = lax.rsqrt(jnp.maximum(s2, 1e-30))
    return r, ig, sp, a, s2 * inv_s, inv_s


def _scan_tile(a_ref, b_ref, o_ref, carry_ref, rows, reverse):
    ridx = lax.broadcasted_iota(jnp.int32, (SUBLANE, D), 0)
    groups = rows // SUBLANE

    def group(gi, h):
        g = (groups - 1 - gi) if reverse else gi
        off = pl.multiple_of(g * SUBLANE, SUBLANE)
        a = a_ref[pl.ds(off, SUBLANE), :]
        b = b_ref[pl.ds(off, SUBLANE), :]
        for sh in (1, 2, 4):
            if reverse:
                keep = ridx < SUBLANE - sh
                a_p = jnp.where(keep, pltpu.roll(a, SUBLANE - sh, 0), 1.0)
                b_p = jnp.where(keep, pltpu.roll(b, SUBLANE - sh, 0), 0.0)
            else:
                keep = ridx >= sh
                a_p = jnp.where(keep, pltpu.roll(a, sh, 0), 1.0)
                b_p = jnp.where(keep, pltpu.roll(b, sh, 0), 0.0)
            b = b + a * b_p
            a = a * a_p
        hh = b + a * h
        o_ref[pl.ds(off, SUBLANE), :] = hh
        return hh[0:1, :] if reverse else hh[SUBLANE - 1:SUBLANE, :]

    carry_ref[...] = lax.fori_loop(0, groups, group, carry_ref[...])


def _rglru_fwd(xr, cw, cb, wa, wx, ba, bx, lam, reverse, name):
    T = xr.shape[0]
    n = T // TS
    prev_spec, next_spec = _halo_specs(TS, D, n, T, reverse)

    def body(prev_ref, cur_ref, next_ref, cw_ref, cb_ref, wa_ref, wx_ref, ba_ref, bx_ref, lam_ref,
             h_ref, a_ref, s_ref, r_ref, ig_ref, xc_ref, b_s, carry):
        i = pl.program_id(0)
        j = (n - 1 - i) if reverse else i

        @pl.when(i == 0)
        def _():
            carry[...] = jnp.zeros_like(carry)

        ext = _ext_rows(prev_ref, cur_ref[...], next_ref, j, n)
        xc = _conv_fwd(ext, cw_ref[...], cb_ref[...], TS)
        r, ig, _, a, s, _ = _gates(xc, wa_ref, wx_ref, ba_ref[...], bx_ref[...], lam_ref[...])
        s_ref[...] = s
        r_ref[...] = r.astype(BF16)
        ig_ref[...] = ig.astype(BF16)
        xc_ref[...] = xc.astype(BF16)
        a_ref[...] = a
        b_s[...] = s * ig * xc
        _scan_tile(a_ref, b_s, h_ref, carry, TS, reverse)

    wspec = _full((RNN_HEADS, RNN_HD, RNN_HD))
    cur = _rev_tile(TS, D, n, reverse)
    f32 = jax.ShapeDtypeStruct((T, D), F32)
    b16 = jax.ShapeDtypeStruct((T, D), BF16)
    return _pallas(
        body, name=name, grid=(n,),
        in_specs=[prev_spec, cur, next_spec, _full((4, D)), _full((1, D)),
                  wspec, wspec, _full((1, D)), _full((1, D)), _full((1, D))],
        out_specs=[cur] * 6,
        out_shape=[f32, f32, f32, b16, b16, b16],
        scratch_shapes=[pltpu.VMEM((TS, D), F32), pltpu.VMEM((1, D), F32)],
        compiler_params=_params(("arbitrary",)),
    )(xr, xr, xr, cw, cb, wa, wx, ba, bx, lam)


def _od_out(hf, hb, g1, w_out, x1, tgt, mod, lnp):
    T = x1.shape[0]

    def body(hf_ref, hb_ref, g_ref, w_ref, x_ref, t_ref, mod_ref, ln_ref,
             dh_ref, dg_ref, dx_ref, dwb_ref, vec_ref, dw_ref):
        i = pl.program_id(0)

        @pl.when(i == 0)
        def _():
            dw_ref[...] = jnp.zeros_like(dw_ref)
            vec_ref[...] = jnp.zeros_like(vec_ref)

        hs = hf_ref[...] + hb_ref[...]
        sg, dsg = _silu_and_grad(g_ref[...].astype(F32))
        yr = (hs * sg).astype(BF16)
        w = w_ref[...]
        out = _dot(yr, w)
        gate = mod_ref[2:3, :]
        z = ALPHA * x_ref[...] + gate * out
        lng = ln_ref[0:1, :]
        x2, xhat, rstd = _ln_fwd(z, lng, ln_ref[1:2, :])
        diff = x2 - t_ref[...]
        vec_ref[3:4, 0:LANE] += 0.5 * jnp.sum(diff * diff) * (1.0 / D)
        dx2 = diff * (1.0 / D)
        dz = _ln_bwd(dx2, xhat, rstd, lng)
        vec_ref[0:1, :] += _rowsum(dx2 * xhat)
        vec_ref[1:2, :] += _rowsum(dx2)
        vec_ref[2:3, :] += _rowsum(dz * out)
        dout = (dz * gate).astype(BF16)
        dyr = _dot_nt(dout, w)
        dw_ref[...] += _dot_tn(yr, dout)
        dh_ref[...] = dyr * sg
        dg_ref[...] = (dyr * hs * dsg).astype(BF16)
        dx_ref[...] = ALPHA * dz

        @pl.when(i == T // TMO - 1)
        def _():
            dwb_ref[...] = dw_ref[...].astype(BF16)

    return _pallas(
        body, name="od_out", grid=(T // TMO,),
        in_specs=[_tile(TMO, D), _tile(TMO, D), _tile(TMO, D), _full((D, D)), _tile(TMO, D), _tile(TMO, D),
                  _full((3, D)), _full((2, D))],
        out_specs=[_tile(TMO, D), _tile(TMO, D), _tile(TMO, D), _full((D, D)), _full((SUBLANE, D))],
        out_shape=[jax.ShapeDtypeStruct((T, D), F32), jax.ShapeDtypeStruct((T, D), BF16),
                   jax.ShapeDtypeStruct((T, D), F32), jax.ShapeDtypeStruct((D, D), BF16),
                   jax.ShapeDtypeStruct((SUBLANE, D), F32)],
        scratch_shapes=[pltpu.VMEM((D, D), F32)],
        compiler_params=_params(("arbitrary",)),
    )(hf, hb, g1, w_out, x1, tgt, mod, lnp)


def _rglru_bwd(fwd, dh, wa, wx, lam, reverse, name, comm=None):
    h, a_all, s_all, r_all, ig_all, xc_all = fwd
    T = h.shape[0]
    n = T // TS
    adj_rev = not reverse
    hprev_spec, hnext_spec = _halo_specs(TS, D, n, T, adj_rev)
    h_halo_spec = hnext_spec if reverse else hprev_spec

    def body(dh_ref, h_ref, hh_ref, a_ref, s_ref, r_ref, ig_ref, xc_ref, wa_ref, wx_ref, lam_ref,
             dxc_ref, dwa_ref, dwx_ref, vec_ref, a_s, l_s, carry, a_edge):
        i = pl.program_id(0)
        j = (n - 1 - i) if adj_rev else i

        @pl.when(i == 0)
        def _():
            carry[...] = jnp.zeros_like(carry)
            a_edge[...] = jnp.zeros_like(a_edge)
            dwa_ref[...] = jnp.zeros_like(dwa_ref)
            dwx_ref[...] = jnp.zeros_like(dwx_ref)
            vec_ref[...] = jnp.zeros_like(vec_ref)

        lam = lam_ref[...]
        sp = jnp.maximum(-lam, 0.0) + jnp.log(1.0 + jnp.exp(-jnp.abs(lam)))
        a, s = a_ref[...], s_ref[...]
        inv_s = lax.rsqrt(jnp.maximum(s * s, 1e-30))
        r, ig = r_ref[...].astype(F32), ig_ref[...].astype(F32)
        xcb = xc_ref[...]
        xc = xcb.astype(F32)

        rows = lax.broadcasted_iota(jnp.int32, (TS, D), 0)
        hcur = h_ref[...]
        if reverse:
            a_sh = jnp.where(rows == 0, a_edge[...], pltpu.roll(a, 1, 0))
            halo = jnp.where(j < n - 1, hh_ref[0:1, :], 0.0)
            h_nb = jnp.where(rows == TS - 1, halo, pltpu.roll(hcur, TS - 1, 0))
        else:
            a_sh = jnp.where(rows == TS - 1, a_edge[...], pltpu.roll(a, TS - 1, 0))
            halo = jnp.where(j > 0, hh_ref[SUBLANE - 1:SUBLANE, :], 0.0)
            h_nb = jnp.where(rows == 0, halo, pltpu.roll(hcur, 1, 0))
        a_s[...] = a_sh
        _scan_tile(a_s, dh_ref, l_s, carry, TS, adj_rev)
        a_edge[...] = a[TS - 1:TS, :] if reverse else a[0:1, :]

        lm = l_s[...]
        da = lm * h_nb
        di = lm * s * xc
        dxc = lm * s * ig
        ds = lm * ig * xc
        dlog_a = a * (da - ds * a * inv_s)
        dr = (-RG_C) * sp * dlog_a
        dsp = _rowsum((-RG_C) * r * dlog_a)
        dpr = dr * r * (1.0 - r)
        dpi = di * ig * (1.0 - ig)
        vec_ref[0:1, :] += _rowsum(dpr)
        vec_ref[1:2, :] += _rowsum(dpi)
        vec_ref[2:3, :] += dsp * (-_sigmoid(-lam))
        parts = []
        for hd in range(RNN_HEADS):
            sl = slice(hd * RNN_HD, (hd + 1) * RNN_HD)
            xh = xcb[:, sl]
            dprh = dpr[:, sl].astype(BF16)
            dpih = dpi[:, sl].astype(BF16)
            parts.append(_dot_nt(dprh, wa_ref[hd]) + _dot_nt(dpih, wx_ref[hd]))
            dwa_ref[hd] += _dot_tn(xh, dprh)
            dwx_ref[hd] += _dot_tn(xh, dpih)
        dxc_ref[...] = dxc + jnp.concatenate(parts, axis=-1)

    wspec = _full((RNN_HEADS, RNN_HD, RNN_HD))
    cur = _rev_tile(TS, D, n, adj_rev)
    return _fused_call(
        body, comm, (dh, h, h, a_all, s_all, r_all, ig_all, xc_all, wa, wx, lam), name=name, grid=(n,),
        in_specs=[cur, cur, h_halo_spec, cur, cur, cur, cur, cur, wspec, wspec, _full((1, D))],
        out_specs=[cur, wspec, wspec, _full((SUBLANE, D))],
        out_shape=[jax.ShapeDtypeStruct((T, D), F32),
                   jax.ShapeDtypeStruct((RNN_HEADS, RNN_HD, RNN_HD), F32),
                   jax.ShapeDtypeStruct((RNN_HEADS, RNN_HD, RNN_HD), F32),
                   jax.ShapeDtypeStruct((SUBLANE, D), F32)],
        scratch_shapes=[pltpu.VMEM((TS, D), F32)] * 2 + [pltpu.VMEM((1, D), F32)] * 2)


def _od_in_bwd(dxcf, dxcb, xr, dg1, x1, dx1p, mod, w_in, cw, comm=None):
    T = x1.shape[0]
    n = T // TMO
    slab = OD_IN // N_DEV
    prev_spec, next_spec = _halo_specs(TMO, D, n, T, False)

    def body(fp_ref, fc_ref, fn_ref, bp_ref, bc_ref, bn_ref, xr_ref, dg_ref, x1_ref, dxp_ref,
             mod_ref, w_ref, cw_ref, dx_ref, dwb_ref, vec_ref, dw_ref):
        i = pl.program_id(0)

        @pl.when(i == 0)
        def _():
            dw_ref[...] = jnp.zeros_like(dw_ref)
            vec_ref[...] = jnp.zeros_like(vec_ref)

        dcur = fc_ref[...] + bc_ref[...]
        dprev = jnp.where(i > 0, fp_ref[...] + bp_ref[...], 0.0)
        dnext = jnp.where(i < n - 1, fn_ref[...] + bn_ref[...], 0.0)
        dext = jnp.concatenate([dprev, dcur, dnext], axis=0)
        xr_v = xr_ref[...]
        cw_v = cw_ref[...]
        dxr = None
        for k in range(4):
            shifted = _shift_rows(dext, 2 - k, TMO)
            term = cw_v[k:k + 1, :] * shifted
            dxr = term if dxr is None else dxr + term
            vec_ref[k:k + 1, :] += _rowsum(shifted * xr_v)
        vec_ref[4:5, :] += _rowsum(dcur)
        dp = jnp.concatenate([dxr.astype(BF16), dg_ref[...]], axis=-1)
        x1v = x1_ref[...]
        scale1 = 1.0 + mod_ref[1:2, :]
        h1 = (x1v * scale1 + mod_ref[0:1, :]).astype(BF16)
        dh1 = _dot_nt(dp, w_ref[...])
        dw_ref[...] += _dot_tn(h1, dp)
        dx_ref[...] = dxp_ref[...] + dh1 * scale1
        vec_ref[5:6, :] += _rowsum(dh1)
        vec_ref[6:7, :] += _rowsum(dh1 * x1v)

        @pl.when(i == n - 1)
        def _():
            for j in range(N_DEV):
                dwb_ref[j] = dw_ref[:, j * slab:(j + 1) * slab].astype(BF16)

    t = _tile(TMO, D)
    return _fused_call(
        body, comm, (dxcf, dxcf, dxcf, dxcb, dxcb, dxcb, xr, dg1, x1, dx1p, mod, w_in, cw),
        name="od_in_bwd", grid=(n,),
        in_specs=[prev_spec, t, next_spec, prev_spec, t, next_spec, t, t, t, t,
                  _full((3, D)), _full((D, OD_IN)), _full((4, D))],
        out_specs=[t, _full((N_DEV, D, slab)), _full((SUBLANE, D))],
        out_shape=[jax.ShapeDtypeStruct((T, D), F32), jax.ShapeDtypeStruct((N_DEV, D, slab), BF16),
                   jax.ShapeDtypeStruct((SUBLANE, D), F32)],
        scratch_shapes=[pltpu.VMEM((D, OD_IN), F32)])


def _ev_out_bwd(dx1, z0, out0, y0, ycat, g0, w_out, mod, lnp):
    T = dx1.shape[0]

    def body(dx_ref, z_ref, out_ref, y0_ref, yc_ref, g_ref, w_ref, mod_ref, ln_ref,
             dxp_ref, dyc_ref, dg_ref, dwb_ref, vec_ref, dw_ref):
        i = pl.program_id(0)

        @pl.when(i == 0)
        def _():
            dw_ref[...] = jnp.zeros_like(dw_ref)
            vec_ref[...] = jnp.zeros_like(vec_ref)

        lng = ln_ref[0:1, :]
        _, xhat, rstd = _ln_fwd(z_ref[...], lng, ln_ref[1:2, :])
        dy = dx_ref[...]
        dz = _ln_bwd(dy, xhat, rstd, lng)
        vec_ref[0:1, :] += _rowsum(dy * xhat)
        vec_ref[1:2, :] += _rowsum(dy)
        vec_ref[2:3, :] += _rowsum(dz * out_ref[...].astype(F32))
        dout = (dz * mod_ref[2:3, :]).astype(BF16)
        dy0 = _dot_nt(dout, w_ref[...])
        dw_ref[...] += _dot_tn(y0_ref[...], dout)
        sg, dsg = _silu_and_grad(g_ref[...].astype(F32))
        dyc_ref[...] = (dy0 * sg).astype(BF16)
        dg_ref[...] = (dy0 * yc_ref[...].astype(F32) * dsg).astype(BF16)
        dxp_ref[...] = ALPHA * dz

        @pl.when(i == T // TMO - 1)
        def _():
            dwb_ref[...] = dw_ref[...].astype(BF16)

    t = _tile(TMO, D)
    return _pallas(
        body, name="ev_out_bwd", grid=(T // TMO,),
        in_specs=[t, t, t, t, t, t, _full((D, D)), _full((3, D)), _full((2, D))],
        out_specs=[t, t, t, _full((D, D)), _full((SUBLANE, D))],
        out_shape=[jax.ShapeDtypeStruct((T, D), F32), jax.ShapeDtypeStruct((T, D), BF16),
                   jax.ShapeDtypeStruct((T, D), BF16), jax.ShapeDtypeStruct((D, D), BF16),
                   jax.ShapeDtypeStruct((SUBLANE, D), F32)],
        scratch_shapes=[pltpu.VMEM((D, D), F32)],
        compiler_params=_params(("arbitrary",)),
    )(dx1, z0, out0, y0, ycat, g0, w_out, mod, lnp)


def _mix0_bwd(q, kvx, lse, dyc, ycat, su, sv, sink_l, bias, a128, gsum, sel, sg_lng, sg_lnb, sg_w, sg_bfull,
              rc, rs1, rs2, comm=None):
    T = q.shape[0]
    nb = T // BLK

    def body(q_ref, kp_ref, kc_ref, kn_ref, lse_ref, dyc_ref, yc_ref, su_ref, sv_ref, sink_ref, bias_ref, a_ref,
             gsum_ref, sel_ref, lng_ref, lnb_ref, w_ref, bfull_ref, c_ref, s1_ref, s2_ref,
             dq_ref, dkv_ref, dsu_ref, dsv_ref, dw_ref, dbt_ref, vec_ref, dsink_ref):
        n = pl.program_id(0)

        @pl.when(n == 0)
        def _():
            dkv_ref[...] = jnp.zeros_like(dkv_ref)
            dw_ref[...] = jnp.zeros_like(dw_ref)
            dbt_ref[...] = jnp.zeros_like(dbt_ref)
            vec_ref[...] = jnp.zeros_like(vec_ref)
            dsink_ref[...] = jnp.zeros_like(dsink_ref)

        band = pl.ds(pl.multiple_of(n * BLK + (TM - BLK), BLK), 3 * BLK)
        bias = _band_bias(bias_ref, n, nb)
        kvx = jnp.concatenate([kp_ref[...], kc_ref[...], kn_ref[...]], axis=0)
        bias2 = jnp.concatenate([bias, bias], axis=1)
        low = lax.broadcasted_iota(jnp.int32, (BLK, LANE), 1) < HEAD_DIM
        low2 = lax.broadcasted_iota(jnp.int32, (2 * BLK, LANE), 1) < HEAD_DIM
        sel = sel_ref[...]
        c, s1, s2 = c_ref[...], s1_ref[...], s2_ref[...]
        for kvh in range(2):
            t0, t1 = 2 * kvh, 2 * kvh + 1
            q2 = jnp.concatenate([_lane_tile(q_ref, t0), _lane_tile(q_ref, t1)], axis=0)
            do2 = jnp.concatenate([_lane_tile(dyc_ref, t0), _lane_tile(dyc_ref, t1)], axis=0)
            yc2 = jnp.concatenate([_lane_tile(yc_ref, t0), _lane_tile(yc_ref, t1)], axis=0)
            p_hi, p_lo = _split_bf16(do2.astype(F32) * yc2.astype(F32))
            deltas = _dot_nt(sel, p_hi) + _dot_nt(sel, p_lo)
            dkx = jnp.zeros((3 * BLK, LANE), F32)
            dvx = jnp.zeros((3 * BLK, LANE), F32)
            dq_acc = None
            for par in range(2):
                heads = (4 * kvh + par, 4 * kvh + 2 + par)
                kt = 2 * kvh + par
                ke = kvx[:, kt * LANE:(kt + 1) * LANE]
                ve = kvx[:, (4 + kt) * LANE:(5 + kt) * LANE]
                lse = jnp.concatenate([lse_ref[0, :, h * LANE:(h + 1) * LANE] for h in heads], axis=1)
                sk = jnp.concatenate([_lane_tile(sink_ref, h) for h in heads], axis=1)
                delta = deltas[par:par + 1, :]
                pt = jnp.exp(_dot_nt(ke, q2) + bias2 - lse)
                dst = (pt * (_dot_nt(ve, do2) - delta)).astype(BF16)
                sink_terms = jnp.exp(sk - lse) * delta
                for k, h in enumerate(heads):
                    dsink_ref[:, h * LANE:(h + 1) * LANE] += sink_terms[:, k * LANE:(k + 1) * LANE]
                part = _dot_tn(dst, ke)
                dq_acc = part if dq_acc is None else dq_acc + part
                mine = low2 if par == 0 else jnp.logical_not(low2)
                dkx = dkx + jnp.dot(dst, jnp.where(mine, q2, jnp.zeros_like(q2)), preferred_element_type=F32)
                dvx = dvx + jnp.dot(pt.astype(BF16), jnp.where(mine, do2, jnp.zeros_like(do2)),
                                    preferred_element_type=F32)
            for k, t in enumerate((t0, t1)):
                dq_t = dq_acc[k * BLK:(k + 1) * BLK] * (HEAD_DIM ** -0.5)
                dq_ref[:, t * LANE:(t + 1) * LANE] = _rope_bwd(dq_t, c, s1, s2).astype(BF16)
            dkv_ref[band, kvh * LANE:(kvh + 1) * LANE] += dkx
            dkv_ref[band, (2 + kvh) * LANE:(3 + kvh) * LANE] += dvx

        lng = lng_ref[...]
        xhat, rstd, vb, svm = _sg_core(sv_ref, lng, lnb_ref[...], a_ref, w_ref, bfull_ref)
        dy = dyc_ref[:, ATTN_W:].astype(F32)
        dsu_ref[...] = (dy * svm).astype(BF16)
        dsvm = dy * su_ref[...].astype(F32)
        d_hi, d_lo = _split_bf16(dsvm)
        gsum = gsum_ref[...]
        dbt_ref[...] += jnp.dot(d_hi, gsum, preferred_element_type=F32) + jnp.dot(d_lo, gsum,
                                                                                 preferred_element_type=F32)
        tiles = []
        for t in range(SG_W // LANE):
            tl = slice(t * LANE, (t + 1) * LANE)
            dt, v2 = d_hi[:, tl], vb[:, tl]
            dw_ref[2 * t] += _dot_nt(jnp.where(low, dt, jnp.zeros_like(dt)), v2)
            dw_ref[2 * t + 1] += _dot_nt(jnp.where(low, jnp.zeros_like(dt), dt), v2)
            tiles.append(jnp.where(low, _dot_tn(w_ref[2 * t], dt), _dot_tn(w_ref[2 * t + 1], dt)))
        dvgn = jnp.concatenate(tiles, axis=-1)
        vec_ref[0:1, :] += _rowsum(dvgn * xhat)
        vec_ref[1:2, :] += _rowsum(dvgn)
        dxh = dvgn * lng
        m1 = _group_mean(dxh, a_ref)
        m2 = _group_mean(dxh * xhat, a_ref)
        dsv_ref[...] = (rstd * (dxh - m1 - xhat * m2)).astype(BF16)

    return _fused_call(
        body, comm, (q, kvx, kvx, kvx, lse, dyc, ycat, su, sv, sink_l, bias, a128, gsum, sel, sg_lng, sg_lnb, sg_w,
                     sg_bfull, rc, rs1, rs2),
        name="mix0_bwd", grid=(nb,),
        in_specs=[_tile(BLK, ATTN_W)] + _band_specs(KVX_W, nb) + [
            pl.BlockSpec((1, 1, N_HEADS * LANE), lambda n: (n, 0, 0)), _tile(BLK, D), _tile(BLK, D),
            _tile(BLK, SG_W), _tile(BLK, SG_W), _full((1, N_HEADS * LANE)), _full((3 * BLK, LANE)),
            _full((2 * LANE, 2 * LANE)),_full((SG_W, LANE)), _full((SUBLANE, LANE)), _full((1, SG_W)), _full((1, SG_W)),
            _full((SG_GROUPS, BLK, BLK)), _full((BLK, SG_W)), _tile(BLK, LANE), _tile(BLK, LANE), _tile(BLK, LANE)],
        out_specs=[_tile(BLK, ATTN_W), _full((T + 2 * TM, 4 * LANE)), _tile(BLK, SG_W), _tile(BLK, SG_W),
                   _full((SG_GROUPS, BLK, BLK)), _full((BLK, LANE)), _full((SUBLANE, SG_W)),
                   _full((1, N_HEADS * LANE))],
        out_shape=[jax.ShapeDtypeStruct((T, ATTN_W), BF16), jax.ShapeDtypeStruct((T + 2 * TM, 4 * LANE), F32),
                   jax.ShapeDtypeStruct((T, SG_W), BF16), jax.ShapeDtypeStruct((T, SG_W), BF16),
                   jax.ShapeDtypeStruct((SG_GROUPS, BLK, BLK), F32), jax.ShapeDtypeStruct((BLK, LANE), F32),
                   jax.ShapeDtypeStruct((SUBLANE, SG_W), F32), jax.ShapeDtypeStruct((1, N_HEADS * LANE), F32)])


def _ev_in_bwd(dq, dkv, dsu, dsv, dg0, x, dxp, mod, w_in, rc, rs1, rs2, comm=None):
    T = x.shape[0]

    def body(dq_ref, dkv_ref, dsu_ref, dsv_ref, dg_ref, x_ref, dxp_ref, mod_ref, w_ref, c_ref, s1_ref, s2_ref,
             dx_ref, dwb_ref, vec_ref, dw_ref):
        i = pl.program_id(0)

        @pl.when(i == 0)
        def _():
            dw_ref[...] = jnp.zeros_like(dw_ref)
            vec_ref[...] = jnp.zeros_like(vec_ref)

        low = lax.broadcasted_iota(jnp.int32, (TM, LANE), 1) < HEAD_DIM

        def fold(j):
            t0 = dkv_ref[:, (2 * j) * LANE:(2 * j + 1) * LANE]
            t1 = dkv_ref[:, (2 * j + 1) * LANE:(2 * j + 2) * LANE]
            return jnp.where(low, t0 + pltpu.roll(t0, HEAD_DIM, 1), t1 + pltpu.roll(t1, HEAD_DIM, 1))

        dk = _rope_bwd(fold(0), c_ref[...], s1_ref[...], s2_ref[...]).astype(BF16)
        dp = jnp.concatenate([dq_ref[...], dk, fold(1).astype(BF16), dsu_ref[...], dsv_ref[...],
                              dg_ref[...]], axis=-1)
        xv = x_ref[...]
        scale0 = 1.0 + mod_ref[1:2, :]
        h0 = (xv * scale0 + mod_ref[0:1, :]).astype(BF16)
        dh0 = _dot(dp, w_ref[...])
        dw_ref[...] += _dot_tn(dp, h0)
        dx_ref[...] = dxp_ref[...] + dh0 * scale0
        vec_ref[0:1, :] += _rowsum(dh0)
        vec_ref[1:2, :] += _rowsum(dh0 * xv)

        @pl.when(i == T // TM - 1)
        def _():
            dwb_ref[...] = dw_ref[...].astype(BF16)

    t = _tile(TM, D)
    return _fused_call(
        body, comm, (dq, dkv, dsu, dsv, dg0, x, dxp, mod, w_in, rc, rs1, rs2), name="ev_in_bwd", grid=(T // TM,),
        in_specs=[_tile(TM, ATTN_W), pl.BlockSpec((TM, 4 * LANE), lambda i: (i + 1, 0)), _tile(TM, SG_W),
                  _tile(TM, SG_W), t, t, t,
                  _full((3, D)), _full((EV_IN, D)), _tile(TM, LANE), _tile(TM, LANE), _tile(TM, LANE)],
        out_specs=[t, _full((EV_IN, D)), _full((SUBLANE, D))],
        out_shape=[jax.ShapeDtypeStruct((T, D), F32), jax.ShapeDtypeStruct((EV_IN, D), BF16),
                   jax.ShapeDtypeStruct((SUBLANE, D), F32)],
        scratch_shapes=[pltpu.VMEM((EV_IN, D), F32)])


def _sum_slots(land_ref):
    g = land_ref[0].astype(F32)
    for i in range(1, land_ref.shape[0]):
        g = g + land_ref[i].astype(F32)
    return g


def _reduce_adam(items, name, after=()):
    R, C = items[0][1].shape
    rb = R
    if R > 512:
        for cand in (512, 256, 128, 64, 32, 16, 8):
            if R % cand == 0:
                rb = cand
                break
    n = len(items)

    def body(*refs):
        for k in range(n):
            l_ref, w_ref, m_ref, v_ref = refs[4 * k:4 * k + 4]
            first_out = 4 * n + len(after)
            g_ref, d_ref, nm_ref, nv_ref = refs[first_out + 4 * k:first_out + 4 * k + 4]
            g = _sum_slots(l_ref)
            g_ref[...] = g
            dlt, m2, v2 = _adam(w_ref[...], g, m_ref[...], v_ref[...])
            d_ref[...] = dlt
            nm_ref[...] = m2
            nv_ref[...] = v2

    t = pl.BlockSpec((rb, C), lambda i: (i, 0))
    shp = jax.ShapeDtypeStruct((R, C), F32)
    in_specs, operands = [], []
    for land, w, m, v in items:
        in_specs += [pl.BlockSpec((land.shape[0], rb, C), lambda i: (0, i, 0)), t, t, t]
        operands += [land, w, m, v]
    res = _pallas(
        body, name=name, grid=(R // rb,),
        in_specs=in_specs + [pl.BlockSpec(memory_space=pl.ANY)] * len(after),
        out_specs=[t] * (4 * n), out_shape=[shp] * (4 * n),
        compiler_params=_params(("parallel",)),
    )(*operands, *after)
    return [list(res[4 * k:4 * k + 4]) for k in range(n)]


def _tail_stage1(slabs, small):
    _, R, C = slabs.shape
    n_chips = N_DEV // 2
    gather = _GatherComm(small)
    ns = gather.n

    def body(*refs):
        slab_ref = refs[0]
        g_ins = refs[1:1 + ns]
        part, land_ref = refs[1 + ns], refs[2 + ns]
        g_outs = refs[3 + ns:3 + 2 * ns]
        stage, s1_send, s1_recv = refs[3 + 2 * ns:6 + 2 * ns]
        g_sems = refs[6 + 2 * ns:]
        x, y, c = _my_pos()
        chip = 2 * x + y
        gather.start(g_ins, g_outs, g_sems)
        swaps = [pltpu.make_async_remote_copy(
            src_ref=slab_ref.at[2 * k + (1 - c)], dst_ref=stage.at[k], send_sem=s1_send.at[k],
            recv_sem=s1_recv.at[k], device_id=(x, y, 1 - c), device_id_type=MESH) for k in range(n_chips)]
        for cp in swaps:
            cp.start()
        for cp in swaps:
            cp.wait()
        for k in range(n_chips):
            part[k] = (slab_ref[2 * k + c].astype(F32) + stage[k].astype(F32)).astype(BF16)
        land_ref[chip] = part[chip]
        gather.mid(g_ins, g_outs, g_sems)
        gather.finish(g_ins, g_outs, g_sems)

    any_spec = pl.BlockSpec(memory_space=pl.ANY)
    vmem_spec = pl.BlockSpec(memory_space=pltpu.VMEM)
    slab4 = jax.ShapeDtypeStruct((n_chips, R, C), BF16)
    res = _pallas(
        body, name="tail_stage1",
        out_shape=[slab4, slab4] + gather.out_shapes(),
        in_specs=[vmem_spec] + [any_spec] * ns, out_specs=[vmem_spec, vmem_spec] + [any_spec] * ns,
        scratch_shapes=[pltpu.VMEM((n_chips, R, C), BF16),
                        pltpu.SemaphoreType.DMA((n_chips,)), pltpu.SemaphoreType.DMA((n_chips,))] + gather.sems(),
        compiler_params=pltpu.CompilerParams(vmem_limit_bytes=VMEM_LIMIT),
    )(slabs, *gather.arrs)
    return res[0], res[1], list(res[2:])


def _chip_copies(part_ref, land_ref, send_sems, recv_sems):
    x, y, c = _my_pos()
    chip = 2 * x + y
    copies = []
    for r in range(1, N_DEV // 2):
        px = (1 - x) if (r & 2) else x
        py = (1 - y) if (r & 1) else y
        copies.append(pltpu.make_async_remote_copy(
            src_ref=part_ref.at[2 * px + py], dst_ref=land_ref.at[chip], send_sem=send_sems[r - 1],
            recv_sem=recv_sems[r - 1], device_id=(px, py, c), device_id_type=MESH))
    return copies


def _tail_send(part, land):
    n = N_DEV // 2 - 1

    def body(part_ref, land_ref, *outs):
        send_sems, recv_sems = outs[:n], outs[n:2 * n]
        token = outs[2 * n + 2]
        for cp in _chip_copies(part_ref, land_ref, send_sems, recv_sems):
            cp.start()
        token[...] = jnp.zeros_like(token)

    hbm = pl.BlockSpec(memory_space=pltpu.HBM)
    sem = pl.BlockSpec(memory_space=pltpu.SEMAPHORE)
    res = _pallas(
        body, name="tail_send",
        out_shape=tuple([pltpu.SemaphoreType.DMA(())] * (2 * n)
                        + [pltpu.HBM(part.shape, part.dtype), pltpu.HBM(land.shape, land.dtype),
                           jax.ShapeDtypeStruct((SUBLANE, LANE), F32)]),
        in_specs=(hbm, hbm), out_specs=tuple([sem] * (2 * n) + [hbm, hbm, pl.BlockSpec(memory_space=pltpu.VMEM)]),
        input_output_aliases={0: 2 * n, 1: 2 * n + 1},
        compiler_params=pltpu.CompilerParams(has_side_effects=pltpu.SideEffectType.DATAFLOW_SIDE_EFFECTING),
    )(pltpu.with_memory_space_constraint(part, pltpu.HBM), pltpu.with_memory_space_constraint(land, pltpu.HBM))
    return list(res[:n]), list(res[n:2 * n]), res[2 * n], res[2 * n + 1], res[2 * n + 2]


def _tail_wait(send_sems, recv_sems, part, land, after):
    n = len(send_sems)

    def body(part_ref, land_ref, *rest):
        ss, rs = rest[:n], rest[n:2 * n]
        for cp in _chip_copies(part_ref, land_ref, ss, rs):
            cp.wait_send()
            cp.wait_recv()

    hbm = pl.BlockSpec(memory_space=pltpu.HBM)
    sem = pl.BlockSpec(memory_space=pltpu.SEMAPHORE)
    any_spec = pl.BlockSpec(memory_space=pl.ANY)
    res = _pallas(
        body, name="tail_wait",
        out_shape=(pltpu.HBM(part.shape, part.dtype), pltpu.HBM(land.shape, land.dtype)),
        in_specs=tuple([hbm, hbm] + [sem] * (2 * n) + [any_spec] * len(after)), out_specs=(hbm, hbm),
        input_output_aliases={0: 0, 1: 1},
        compiler_params=pltpu.CompilerParams(has_side_effects=pltpu.SideEffectType.DATAFLOW_SIDE_EFFECTING),
    )(part, land, *send_sems, *recv_sems, *after)
    return res[1]


def _slots_adam(items, name, after=()):
    zeros3 = (0, 0, 0)
    in_specs, out_specs, out_shape, operands = [], [], [], []
    for land, w, m, v in items:
        inner = w.shape[-3:]
        if w.ndim == 5:
            lspec = pl.BlockSpec((N_DEV, 1) + inner, lambda i: (0, i) + zeros3)
            wspec = pl.BlockSpec((1, 1) + inner, lambda i: (0, i) + zeros3)
        else:
            lspec = pl.BlockSpec((N_DEV,) + inner, lambda i: (0,) + zeros3)
            wspec = pl.BlockSpec((1,) + inner, lambda i: (0,) + zeros3)
        in_specs += [lspec, wspec, wspec, wspec]
        out_specs += [wspec] * 4
        out_shape += [jax.ShapeDtypeStruct(w.shape, F32)] * 4
        operands += [land, w, m, v]
    n = len(items)

    def body(*refs):
        for k, (_, w, _, _) in enumerate(items):
            l_ref, w_ref, m_ref, v_ref = refs[4 * k:4 * k + 4]
            first_out = 4 * n + len(after)
            outs = refs[first_out + 4 * k:first_out + 4 * k + 4]
            at = (0, 0) if w.ndim == 5 else (0,)

            def update(l_ref=l_ref, w_ref=w_ref, m_ref=m_ref, v_ref=v_ref, outs=outs, at=at):
                g = l_ref[(0,) + at[1:]].astype(F32)
                for i in range(1, N_DEV):
                    g = g + l_ref[(i,) + at[1:]].astype(F32)
                dlt, m2, v2 = _adam(w_ref[at], g, m_ref[at], v_ref[at])
                for o_ref, val in zip(outs, (g, dlt, m2, v2)):
                    o_ref[at] = val

            if w.ndim == 5:
                update()
            else:
                pl.when(pl.program_id(0) == 0)(update)

    res = _pallas(
        body, name=name, grid=(2,),
        in_specs=in_specs + [pl.BlockSpec(memory_space=pl.ANY)] * len(after),
        out_specs=out_specs, out_shape=out_shape,
        compiler_params=_params(("arbitrary",)),
    )(*operands, *after)
    return [list(res[4 * k:4 * k + 4]) for k in range(n)]


SMALL_PARAMS = ("ln_g", "ln_b", "ev_sg_ln_g", "ev_sg_ln_b", "ev_sink", "ev_sg_b",
                "od_conv_w", "od_conv_b", "od_b_a", "od_b_x", "od_lam")


def _small_update(ga, gc, gd, gf, gb, ge, gsink, gbt, params):
    names = list(SMALL_PARAMS)
    flat = [a for nm in names for a in params[nm]]
    n_g = 8

    def body(*refs):
        ga_ref, gc_ref, gd_ref, gf_ref, gb_ref, ge_ref, gs_ref, gbt_ref = refs[:n_g]
        prm = refs[n_g:n_g + 3 * len(names)]
        loss_ref = refs[n_g + 3 * len(names)]
        outs = refs[n_g + 3 * len(names) + 1:]

        def ssum(ref):
            acc = ref[0]
            for i in range(1, N_DEV):
                acc = acc + ref[i]
            return acc

        a, cc, dd, ff, bb, ee = ssum(ga_ref), ssum(gc_ref), ssum(gd_ref), ssum(gf_ref), ssum(gb_ref), ssum(ge_ref)
        loss_ref[...] = a[3:4, 0:LANE]
        me = _slot(*_my_pos())

        def mine(rows):
            acc = jnp.zeros((rows.shape[0], LANE), F32)
            for j in range(N_DEV):
                acc = acc + jnp.where(me == j, rows[:, j * LANE:(j + 1) * LANE], 0.0)
            return acc

        sink_terms = ssum(gs_ref)
        lane8 = lax.broadcasted_iota(jnp.int32, (1, N_HEADS), 1)
        g_sink = jnp.zeros((1, N_HEADS), F32)
        for h in range(N_HEADS):
            tot = -jnp.sum(sink_terms[:, h * LANE:(h + 1) * LANE], axis=1, keepdims=True)
            g_sink = jnp.where(lane8 == h, tot, g_sink)
        grads = dict(
            ln_g=jnp.concatenate([dd[0:1], a[0:1]], axis=0), ln_b=jnp.concatenate([dd[1:2], a[1:2]], axis=0),
            ev_sg_ln_g=ee[0:1], ev_sg_ln_b=ee[1:2], ev_sink=g_sink,
            ev_sg_b=jnp.transpose(ssum(gbt_ref))[0:SG_GROUPS, :],
            od_conv_w=mine(cc[0:4]), od_conv_b=mine(cc[4:5]),
            od_b_a=mine(jnp.concatenate([ff[0:1], bb[0:1]], axis=0)),
            od_b_x=mine(jnp.concatenate([ff[1:2], bb[1:2]], axis=0)),
            od_lam=mine(jnp.concatenate([ff[2:3], bb[2:3]], axis=0)))
        for k, nm in enumerate(names):
            w_ref, m_ref, v_ref = prm[3 * k:3 * k + 3]
            at = (0,) if len(w_ref.shape) == 3 else ()
            g = grads[nm]
            dlt, m2, v2 = _adam(w_ref[at] if at else w_ref[...], g, m_ref[at] if at else m_ref[...],
                                v_ref[at] if at else v_ref[...])
            for o_ref, val in zip(outs[4 * k:4 * k + 4], (g, dlt, m2, v2)):
                if at:
                    o_ref[at] = val
                else:
                    o_ref[...] = val

    gathered = [ga, gc, gd, gf, gb, ge, gsink, gbt]
    out_shape = [jax.ShapeDtypeStruct((1, LANE), F32)]
    for nm in names:
        out_shape += [jax.ShapeDtypeStruct(params[nm][0].shape, F32)] * 4
    return _pallas(
        body, name="small_update", grid=(1,),
        in_specs=[_full(a.shape) for a in gathered + flat],
        out_specs=[_full(s.shape) for s in out_shape], out_shape=out_shape,
        compiler_params=_params(("arbitrary",)),
    )(*gathered, *flat)


VEC_ROWS = 16
VEC_LAYOUT = (("od_conv_w", 4), ("od_conv_b", 1), ("od_b_a", 2), ("od_b_x", 2), ("od_lam", 2))


def _to_slabs(full, cols_per):
    R = full.shape[0]
    return full.reshape(R, N_DEV, cols_per).transpose(1, 0, 2)


def _from_slabs(slabs):
    n, R, cp = slabs.shape
    return slabs.transpose(1, 0, 2).reshape(R, n * cp)


def kernel(x, c, positions, ada_w, ada_b, ln_g, ln_b, ev_w_in, ev_w_out, ev_sink, ev_sg_ln_g, ev_sg_ln_b, ev_sg_w, ev_sg_b, od_w_in, od_conv_w, od_conv_b, od_w_a, od_b_a, od_w_x, od_b_x, od_lam, od_w_out, loss_target, m_ada_w, m_ada_b, m_ln_g, m_ln_b, m_ev_w_in, m_ev_w_out, m_ev_sink, m_ev_sg_ln_g, m_ev_sg_ln_b, m_ev_sg_w, m_ev_sg_b, m_od_w_in, m_od_conv_w, m_od_conv_b, m_od_w_a, m_od_b_a, m_od_w_x, m_od_b_x, m_od_lam, m_od_w_out, v_ada_w, v_ada_b, v_ln_g, v_ln_b, v_ev_w_in, v_ev_w_out, v_ev_sink, v_ev_sg_ln_g, v_ev_sg_ln_b, v_ev_sg_w, v_ev_sg_b, v_od_w_in, v_od_conv_w, v_od_conv_b, v_od_w_a, v_od_b_a, v_od_w_x, v_od_b_x, v_od_lam, v_od_w_out):
    T = x.shape[1]
    me = _slot(*_my_pos())
    xs = x.reshape(T, D)
    tgt = loss_target.reshape(T, D)

    c_all, mod_all, g_vec, (g_ev_in,), (s_ev_out, s_od_in, s_od_out, sg_w, wa, wx) = _head_gather(
        c, ada_w, [ev_w_in[0].T.astype(BF16)],
        [ev_w_out[0], od_w_in[0], od_w_out[0], ev_sg_w[0], od_w_a[0], od_w_x[0]],
        [od_conv_w, od_conv_b, od_b_a, od_b_x, od_lam])
    c_all = c_all.reshape(N_DEV, D)
    w_ev_in = g_ev_in.reshape(EV_IN, D)
    vec_full = _from_slabs(g_vec)
    cw, cb = vec_full[0:4], vec_full[4:5]
    ba, bx, lam = vec_full[5:7], vec_full[7:9], vec_full[9:11]
    mod_mine = lax.dynamic_index_in_dim(mod_all, me, axis=2, keepdims=False)
    mod = mod_mine.transpose(1, 0, 2).reshape(2, 3 * D) + ada_b
    mod0 = mod[0].reshape(3, D)
    mod1 = mod[1].reshape(3, D)

    half = 8
    inv_freq = jnp.power(jnp.float32(ROPE_THETA), -jnp.arange(half, dtype=F32) / half)
    ang = positions.reshape(T).astype(F32)[:, None] * inv_freq
    cos_t = jnp.tile(jnp.cos(ang), (1, LANE // half))
    sin_t = jnp.tile(jnp.sin(ang), (1, LANE // half))
    l64 = jnp.arange(LANE) % HEAD_DIM
    rc = jnp.where(l64 < 2 * half, cos_t, 1.0)
    rs1 = jnp.where(l64 < half, -sin_t, 0.0)
    rs2 = jnp.where((l64 >= half) & (l64 < 2 * half), sin_t, 0.0)

    ln0 = jnp.stack([ln_g[0], ln_b[0]])
    ln1 = jnp.stack([ln_g[1], ln_b[1]])
    sg_lng = ev_sg_ln_g
    sg_lnb = ev_sg_ln_b
    sg_bfull = jnp.repeat(ev_sg_b[0].T, SG_DIM, axis=1)
    sink_l = jnp.repeat(ev_sink, LANE, axis=1)
    kj = jnp.arange(3 * BLK)[:, None]
    qi = jnp.arange(BLK)[None, :]
    band_bias = jnp.where(jnp.abs(kj - BLK - qi) <= BLK, 0.0, NEG_INF).astype(F32)
    lanes = jnp.arange(LANE)
    lanes2 = jnp.arange(2 * LANE)
    a128 = jnp.where(lanes2[:, None] // SG_DIM == lanes2[None, :] // SG_DIM, 1.0 / SG_DIM, 0.0).astype(BF16)
    gsum = (jnp.arange(SG_W)[:, None] // SG_DIM == lanes[None, :]).astype(BF16)
    sel = (jnp.arange(SUBLANE)[:, None] == lanes[None, :] // HEAD_DIM).astype(BF16)

    (q, kvx, su, sv, g0), _ = _ev_in(xs, mod0, w_ev_in, rc, rs1, rs2)
    (ycat, y0, lse), (g_ev_out, g_od_in, g_od_out) = _mix0_fwd(
        q, kvx, su, sv, g0, sink_l, band_bias, a128, sg_lng, sg_lnb, sg_w, sg_bfull,
        _GatherComm([s_ev_out, s_od_in, s_od_out], mid_frac=0.75))
    w_ev_out = g_ev_out.reshape(D, D)
    w_od_in = _from_slabs(g_od_in)
    w_od_out = g_od_out.reshape(D, D)
    out0, z0, x1 = _ev_out(y0, w_ev_out, xs, mod0, ln0)
    xr, g1 = _od_in(x1, mod1, w_od_in)
    fwd_f = _rglru_fwd(xr, cw, cb, wa[0], wx[0], ba[0:1], bx[0:1], lam[0:1], False, "rglru_fwd_f")
    fwd_b = _rglru_fwd(xr, cw, cb, wa[1], wx[1], ba[1:2], bx[1:2], lam[1:2], True, "rglru_fwd_b")
    dh, dg1, dx1p, d_od_out, vec_a = _od_out(fwd_f[0], fwd_b[0], g1, w_od_out, x1, tgt, mod1, ln1)

    (dxcf, dwa_f, dwx_f, vec_f), (l_od_out,) = _rglru_bwd(
        fwd_f, dh, wa[0], wx[0], lam[0:1], False, "rglru_bwd_f",
        _ExchangeComm([d_od_out.reshape(N_DEV, D // N_DEV, D)]))
    (dxcb, dwa_b, dwx_b, vec_b), _ = _rglru_bwd(fwd_b, dh, wa[1], wx[1], lam[1:2], True, "rglru_bwd_b")
    (dx1, d_od_in, vec_c), (a_wa, a_wx) = _od_in_bwd(
        dxcf, dxcb, xr, dg1, x1, dx1p, mod1, w_od_in, cw,
        _GatherComm([jnp.stack([dwa_f, dwa_b]).astype(BF16), jnp.stack([dwx_f, dwx_b]).astype(BF16)],
                    mid_frac=0.75))
    dxp, dyc, dg0, d_ev_out, vec_d = _ev_out_bwd(dx1, z0, out0, y0, ycat, g0, w_ev_out, mod0, ln0)
    (dq, dkv, dsu, dsv, d_sg_w, d_sg_bt, vec_e, d_sink_l), (l_od_in, l_ev_out) = _mix0_bwd(
        q, kvx, lse, dyc, ycat, su, sv, sink_l, band_bias, a128, gsum, sel, sg_lng, sg_lnb, sg_w, sg_bfull,
        rc, rs1, rs2, _ExchangeComm([d_od_in, d_ev_out.reshape(N_DEV, D // N_DEV, D)]))
    (grad_x, d_ev_in, vec_g), _ = _ev_in_bwd(dq, dkv, dsu, dsv, dg0, xs, dxp, mod0, w_ev_in, rc, rs1, rs2)

    part, land, (ga, gc, gd, gf, gb, gg, ge, gsink, gbt, a_sgw) = _tail_stage1(
        d_ev_in.reshape(N_DEV, EV_IN // N_DEV, D),
        [vec_a, vec_c, vec_d, vec_f, vec_b, vec_g, vec_e, d_sink_l, d_sg_bt, d_sg_w.astype(BF16)])
    send_sems, recv_sems, part, land, token = _tail_send(part, land)

    dmod_all = jnp.stack([jnp.concatenate([gg[:, 0], gg[:, 1], gd[:, 2]], axis=-1),
                          jnp.concatenate([gc[:, 5], gc[:, 6], ga[:, 2]], axis=-1)], axis=1)
    cols = ada_w.shape[2]
    dmod_cols = lax.dynamic_slice_in_dim(dmod_all, me * cols, cols, axis=2).transpose(1, 0, 2)
    (g_ada_w, d_ada_w, nm_ada_w, nv_ada_w, g_ada_b, d_ada_b, nm_ada_b, nv_ada_b) = _ada_update(
        c_all, dmod_cols, dmod_all, ada_w, m_ada_w, v_ada_w, ada_b, m_ada_b, v_ada_b)

    res = dict(ada_w=[g_ada_w, d_ada_w, nm_ada_w, nv_ada_w], ada_b=[g_ada_b, d_ada_b, nm_ada_b, nv_ada_b])
    (r_od_in,) = _reduce_adam([(l_od_in, od_w_in[0], m_od_w_in[0], v_od_w_in[0])], "adam_od_w_in", after=[token])
    r_ev_out, r_od_out = _reduce_adam([(l_ev_out, ev_w_out[0], m_ev_w_out[0], v_ev_w_out[0]),
                                       (l_od_out, od_w_out[0], m_od_w_out[0], v_od_w_out[0])], "adam_w_out",
                                      after=[token])
    for name, r in (("od_w_in", r_od_in), ("ev_w_out", r_ev_out), ("od_w_out", r_od_out)):
        res[name] = [a[None] for a in r]
    res["od_w_a"], res["od_w_x"], res["ev_sg_w"] = _slots_adam(
        [(a_wa, od_w_a, m_od_w_a, v_od_w_a), (a_wx, od_w_x, m_od_w_x, v_od_w_x),
         (a_sgw, ev_sg_w, m_ev_sg_w, v_ev_sg_w)], "adam_gates", after=[token])
    small = dict(ln_g=(ln_g, m_ln_g, v_ln_g), ln_b=(ln_b, m_ln_b, v_ln_b),
                 ev_sg_ln_g=(ev_sg_ln_g, m_ev_sg_ln_g, v_ev_sg_ln_g),
                 ev_sg_ln_b=(ev_sg_ln_b, m_ev_sg_ln_b, v_ev_sg_ln_b),
                 ev_sink=(ev_sink, m_ev_sink, v_ev_sink), ev_sg_b=(ev_sg_b, m_ev_sg_b, v_ev_sg_b),
                 od_conv_w=(od_conv_w, m_od_conv_w, v_od_conv_w), od_conv_b=(od_conv_b, m_od_conv_b, v_od_conv_b),
                 od_b_a=(od_b_a, m_od_b_a, v_od_b_a), od_b_x=(od_b_x, m_od_b_x, v_od_b_x),
                 od_lam=(od_lam, m_od_lam, v_od_lam))
    small_out = _small_update(ga, gc, gd, gf, gb, ge, gsink, gbt, small)
    l_ev_in = _tail_wait(send_sems, recv_sems, part, land,
                         [r_od_in[0], r_od_out[0], res["od_w_x"][0], g_ada_w, small_out[0]])
    (r_ev_in,) = _reduce_adam([(l_ev_in, ev_w_in[0].T, m_ev_w_in[0].T, v_ev_w_in[0].T)], "adam_ev_w_in")
    res["ev_w_in"] = [a.T[None] for a in r_ev_in]
    loss = small_out[0][0, 0]
    for k, name in enumerate(SMALL_PARAMS):
        res[name] = small_out[1 + 4 * k:5 + 4 * k]

    order = ["ada_w", "ada_b", "ln_g", "ln_b", "ev_w_in", "ev_w_out", "ev_sink", "ev_sg_ln_g", "ev_sg_ln_b",
             "ev_sg_w", "ev_sg_b", "od_w_in", "od_conv_w", "od_conv_b", "od_w_a", "od_b_a", "od_w_x", "od_b_x",
             "od_lam", "od_w_out"]
    outs = [loss, grad_x.reshape(1, T, D)]
    for kind in range(4):
        outs += [res[name][kind] for name in order]
    return tuple(outs)
```

```python
import functools

import jax
import jax.numpy as jnp
from jax import lax
from jax.experimental import pallas as pl
from jax.experimental.pallas import tpu as pltpu

F32 = jnp.float32
BF16 = jnp.bfloat16

N_DEV = 8
D = 1024
N_HEADS = 8
HEAD_DIM = 64
KV_WIDTH = 128
ATTN_W = 512
SG_W = 512
SG_GROUPS = 8
SG_DIM = 64
BLK = 128
KVX_W = 1024
EV_IN = 2816
OD_IN = 2048
RNN_HEADS = 8
RNN_HD = 128
ALPHA = 4.0 ** 0.25
LN_EPS = 1e-5
NEG_INF = -1e30
RG_C = 8.0
ROPE_THETA = 500000.0
LR, B1, B2, EPS, WD, STEP = 0.001, 0.9, 0.999, 1e-08, 0.01, 10

LANE = 128
SUBLANE = 8
TM = 256
TMF = 512
TMO = 512
TS = 256
VMEM_LIMIT = 56 * 1024 * 1024

MESH = pl.DeviceIdType.MESH


def _pallas(body, **kw):
    return pl.pallas_call(body, **kw)


def _params(sem, vmem=VMEM_LIMIT):
    return pltpu.CompilerParams(dimension_semantics=sem, vmem_limit_bytes=vmem)


def _sigmoid(x):
    return 0.5 * jnp.tanh(0.5 * x) + 0.5


def _silu_and_grad(x):
    s = _sigmoid(x)
    return x * s, s * (1.0 + x * (1.0 - s))


def _dot(a, b):
    return jnp.dot(a.astype(BF16), b.astype(BF16), preferred_element_type=F32)


def _dot_nt(a, b):
    return lax.dot_general(a.astype(BF16), b.astype(BF16), (((1,), (1,)), ((), ())), preferred_element_type=F32)


def _dot_tn(a, b):
    return lax.dot_general(a.astype(BF16), b.astype(BF16), (((0,), (0,)), ((), ())), preferred_element_type=F32)


def _ln_fwd(z, g, b):
    mu = jnp.mean(z, axis=-1, keepdims=True)
    zc = z - mu
    var = jnp.mean(zc * zc, axis=-1, keepdims=True)
    rstd = lax.rsqrt(var + LN_EPS)
    xhat = zc * rstd
    return xhat * g + b, xhat, rstd


def _ln_bwd(dy, xhat, rstd, g):
    dxh = dy * g
    m1 = jnp.mean(dxh, axis=-1, keepdims=True)
    m2 = jnp.mean(dxh * xhat, axis=-1, keepdims=True)
    return rstd * (dxh - m1 - xhat * m2)


def _rowsum(v):
    return jnp.sum(v, axis=0, keepdims=True)


def _rope_fwd(t, c, s1, s2):
    return t * c + pltpu.roll(t, LANE - 8, 1) * s1 + pltpu.roll(t, 8, 1) * s2


def _rope_bwd(d, c, s1, s2):
    return d * c + pltpu.roll(d * s1, 8, 1) + pltpu.roll(d * s2, LANE - 8, 1)


def _adam(w, g, m, v):
    m2 = B1 * m + (1.0 - B1) * g
    v2 = B2 * v + (1.0 - B2) * (g * g)
    m_hat = m2 / (1.0 - B1 ** STEP)
    v_hat = v2 / (1.0 - B2 ** STEP)
    delta = -LR * (m_hat / (jnp.sqrt(v_hat) + EPS) + WD * w)
    return delta, m2, v2


def _tile(rows, width):
    return pl.BlockSpec((rows, width), lambda i: (i, 0))


def _full(shape):
    zeros = (0,) * len(shape)
    return pl.BlockSpec(shape, lambda i: zeros)


def _rev_tile(rows, width, n, reverse):
    if reverse:
        return pl.BlockSpec((rows, width), lambda i: (n - 1 - i, 0))
    return pl.BlockSpec((rows, width), lambda i: (i, 0))


def _halo_specs(rows, width, n, total_rows, reverse):
    per = rows // SUBLANE
    last = total_rows // SUBLANE - 1

    def tile_of(i):
        return (n - 1 - i) if reverse else i

    prev = pl.BlockSpec((SUBLANE, width), lambda i: (jnp.maximum(tile_of(i) * per - 1, 0), 0))
    nxt = pl.BlockSpec((SUBLANE, width), lambda i: (jnp.minimum((tile_of(i) + 1) * per, last), 0))
    return prev, nxt


def _my_pos():
    return lax.axis_index("x"), lax.axis_index("y"), lax.axis_index("c")


def _slot(px, py, pc):
    return 4 * px + 2 * py + pc


class _GatherComm:
    has_mid = True

    def __init__(self, arrs, mid_frac=0.5):
        self.arrs = list(arrs)
        self.n = len(self.arrs)
        self.mid_frac = mid_frac

    def out_shapes(self):
        return [jax.ShapeDtypeStruct((N_DEV,) + a.shape, a.dtype) for a in self.arrs]

    def sems(self):
        return [pltpu.SemaphoreType.DMA((7 * self.n,)), pltpu.SemaphoreType.DMA((7 * self.n,)),
                pltpu.SemaphoreType.DMA((self.n,))]

    def _parts(self, ins, outs, sems):
        send_sems, recv_sems, local_sems = sems
        x, y, c = _my_pos()
        me, sibling = (x, y, c), (x, y, 1 - c)
        chips = [(1 - x, y), (x, 1 - y), (1 - x, 1 - y)]

        def copy(a, k, block, to, src=None):
            dst = outs[a].at[_slot(*block)]
            return pltpu.make_async_remote_copy(
                src_ref=dst if src is None else src, dst_ref=dst,
                send_sem=send_sems.at[a * 7 + k], recv_sem=recv_sems.at[a * 7 + k],
                device_id=to, device_id_type=MESH)

        local = [pltpu.make_async_copy(ins[a], outs[a].at[_slot(*me)], local_sems.at[a]) for a in range(self.n)]
        first = []
        for a in range(self.n):
            first.append(copy(a, 0, me, sibling, src=ins[a]))
            first += [copy(a, 1 + j, me, (*chip, c), src=ins[a]) for j, chip in enumerate(chips)]
        ici_in = [copy(a, 1 + j, (*chip, c), me) for j, chip in enumerate(chips) for a in range(self.n)]
        passed = [copy(a, 4 + j, (*chip, c), sibling) for j, chip in enumerate(chips) for a in range(self.n)]
        d2d_in = []
        for a in range(self.n):
            d2d_in.append(copy(a, 0, sibling, me))
            d2d_in += [copy(a, 4 + j, (*chip, 1 - c), me) for j, chip in enumerate(chips)]
        return local, first, ici_in, passed, d2d_in

    def start(self, ins, outs, sems):
        local, first, _, _, _ = self._parts(ins, outs, sems)
        for cp in local + first:
            cp.start()

    def mid(self, ins, outs, sems):
        _, _, ici_in, passed, _ = self._parts(ins, outs, sems)
        for arrived, fw in zip(ici_in, passed):
            arrived.wait_recv()
            fw.start()

    def finish(self, ins, outs, sems):
        local, first, _, passed, d2d_in = self._parts(ins, outs, sems)
        for cp in d2d_in:
            cp.wait_recv()
        for cp in first + passed:
            cp.wait_send()
        for cp in local:
            cp.wait()


class _ExchangeComm:
    has_mid = False

    def __init__(self, arrs):
        self.arrs = list(arrs)
        self.n = len(self.arrs)

    def out_shapes(self):
        return [jax.ShapeDtypeStruct(a.shape, a.dtype) for a in self.arrs]

    def sems(self):
        return [pltpu.SemaphoreType.DMA((7 * self.n,)), pltpu.SemaphoreType.DMA((7 * self.n,)),
                pltpu.SemaphoreType.DMA((self.n,))]

    def _copies(self, ins, outs, sems):
        send_sems, recv_sems, local_sems = sems
        x, y, c = _my_pos()
        mine = _slot(x, y, c)
        copies = [pltpu.make_async_copy(ins[a].at[mine], outs[a].at[mine], local_sems.at[a]) for a in range(self.n)]
        for k in range(1, N_DEV):
            px = (1 - x) if (k & 4) else x
            py = (1 - y) if (k & 2) else y
            pc = (1 - c) if (k & 1) else c
            for a in range(self.n):
                copies.append(pltpu.make_async_remote_copy(
                    src_ref=ins[a].at[_slot(px, py, pc)], dst_ref=outs[a].at[mine],
                    send_sem=send_sems.at[a * 7 + k - 1], recv_sem=recv_sems.at[a * 7 + k - 1],
                    device_id=(px, py, pc), device_id_type=MESH))
        return copies

    def start(self, ins, outs, sems):
        for cp in self._copies(ins, outs, sems):
            cp.start()

    def finish(self, ins, outs, sems):
        for cp in self._copies(ins, outs, sems):
            cp.wait()


class _BothComm:
    has_mid = True

    def __init__(self, first, second):
        self.parts = (first, second)
        self.arrs = first.arrs + second.arrs
        self.n = first.n + second.n
        self.mid_frac = second.mid_frac

    def out_shapes(self):
        return self.parts[0].out_shapes() + self.parts[1].out_shapes()

    def sems(self):
        return self.parts[0].sems() + self.parts[1].sems()

    def _each(self, ins, outs, sems):
        a, b = self.parts
        return ((a, ins[:a.n], outs[:a.n], sems[:3]), (b, ins[a.n:], outs[a.n:], sems[3:]))

    def start(self, ins, outs, sems):
        for cm, i_, o_, s_ in self._each(ins, outs, sems):
            cm.start(i_, o_, s_)

    def mid(self, ins, outs, sems):
        for cm, i_, o_, s_ in self._each(ins, outs, sems):
            if cm.has_mid:
                cm.mid(i_, o_, s_)

    def finish(self, ins, outs, sems):
        for cm, i_, o_, s_ in self._each(ins, outs, sems):
            cm.finish(i_, o_, s_)


def _fused_call(body, comm, operands, *, name, grid, in_specs, out_specs, out_shape, scratch_shapes=(),
                semantics=("arbitrary",)):
    n_in, n_out, n_scr = len(in_specs), len(out_specs), len(scratch_shapes)
    if comm is None:
        res = _pallas(body, name=name, grid=grid, in_specs=list(in_specs), out_specs=list(out_specs),
                      out_shape=list(out_shape), scratch_shapes=list(scratch_shapes),
                      compiler_params=_params(semantics))(*operands)
        return list(res), []
    k = comm.n
    steps = grid[0]

    def wrapped(*refs):
        ins, cins = refs[:n_in], refs[n_in:n_in + k]
        outs = refs[n_in + k:n_in + k + n_out]
        couts = refs[n_in + k + n_out:n_in + 2 * k + n_out]
        rest = refs[n_in + 2 * k + n_out:]
        scratch, sems = rest[:n_scr], rest[n_scr:]
        i = pl.program_id(0)

        @pl.when(i == 0)
        def _():
            comm.start(cins, couts, sems)

        body(*ins, *outs, *scratch)

        if comm.has_mid:
            @pl.when(i == int(steps * comm.mid_frac))
            def _():
                comm.mid(cins, couts, sems)

        @pl.when(i == steps - 1)
        def _():
            comm.finish(cins, couts, sems)

    any_spec = pl.BlockSpec(memory_space=pl.ANY)
    res = _pallas(wrapped, name=name, grid=grid, in_specs=list(in_specs) + [any_spec] * k,
                  out_specs=list(out_specs) + [any_spec] * k, out_shape=list(out_shape) + comm.out_shapes(),
                  scratch_shapes=list(scratch_shapes) + comm.sems(),
                  compiler_params=_params(("arbitrary",)))(*operands, *comm.arrs)
    return list(res[:n_out]), list(res[n_out:])


def _head_gather(c, ada_w, big, to_cast, vec_parts):
    cols = ada_w.shape[2]
    g_c, g_big = _GatherComm([c]), _GatherComm(big)
    g_mod = _GatherComm([jax.ShapeDtypeStruct((2, N_DEV, cols), F32)])
    g_vec = _GatherComm([jax.ShapeDtypeStruct((VEC_ROWS, LANE), F32)])
    nb, nc, nv = g_big.n, len(to_cast), len(vec_parts)

    def body(*refs):
        c_ref, w_ref = refs[0], refs[1]
        vec_in = refs[2:2 + nv]
        cast_in = refs[2 + nv:2 + nv + nc]
        big_in = refs[2 + nv + nc:2 + nv + nc + nb]
        outs = refs[2 + nv + nc + nb:]
        c_all_ref, mod_all_ref, vec_all_ref = outs[0], outs[1], outs[2]
        cast_out = outs[3:3 + nc]
        big_out = outs[3 + nc:3 + nc + nb]
        part_ref, pack_ref = outs[3 + nc + nb], outs[4 + nc + nb]
        sems = outs[5 + nc + nb:]
        s_c, s_mod, s_big, s_vec = sems[0:3], sems[3:6], sems[6:9], sems[9:12]
        g_c.start([c_ref], [c_all_ref], s_c)
        g_big.start(big_in, big_out, s_big)
        pack_ref[...] = jnp.zeros_like(pack_ref)
        row = 0
        for ref, (_, nrows) in zip(vec_in, VEC_LAYOUT):
            pack_ref[row:row + nrows, :] = ref[0] if len(ref.shape) == 3 else ref[...]
            row += nrows
        g_vec.start([pack_ref], [vec_all_ref], s_vec)
        g_c.mid([c_ref], [c_all_ref], s_c)
        g_c.finish([c_ref], [c_all_ref], s_c)
        cv = c_all_ref[:, 0, :]
        cond = cv * _sigmoid(cv)
        for l in range(2):
            part_ref[l] = _dot(cond, w_ref[l])
        g_mod.start([part_ref], [mod_all_ref], s_mod)
        for src, dst in zip(cast_in, cast_out):
            dst[...] = src[...].astype(BF16)
        for g, ins, outs_, sm in ((g_vec, [pack_ref], [vec_all_ref], s_vec), (g_mod, [part_ref], [mod_all_ref], s_mod),
                                  (g_big, big_in, big_out, s_big)):
            g.mid(ins, outs_, sm)
            g.finish(ins, outs_, sm)

    any_spec = pl.BlockSpec(memory_space=pl.ANY)
    vmem_spec = pl.BlockSpec(memory_space=pltpu.VMEM)
    res = _pallas(
        body, name="head_gather",
        out_shape=(g_c.out_shapes() + g_mod.out_shapes() + g_vec.out_shapes()
                   + [jax.ShapeDtypeStruct(a.shape, BF16) for a in to_cast] + g_big.out_shapes()),
        in_specs=[vmem_spec] * (2 + nv + nc) + [any_spec] * nb,
        out_specs=[vmem_spec] * (3 + nc) + [any_spec] * nb,
        scratch_shapes=[pltpu.VMEM((2, N_DEV, cols), F32), pltpu.VMEM((VEC_ROWS, LANE), F32)]
        + g_c.sems() + g_mod.sems() + g_big.sems() + g_vec.sems(),
        compiler_params=pltpu.CompilerParams(vmem_limit_bytes=VMEM_LIMIT),
    )(c, ada_w, *vec_parts, *to_cast, *big)
    return res[0], res[1], res[2], list(res[3 + nc:]), list(res[3:3 + nc])


def _ada_update(c_all, dmod_cols, dmod_all, ada_w, m_w, v_w, ada_b, m_b, v_b):
    cols = ada_w.shape[2]
    nb = ada_b.shape[1]

    def body(c_ref, dmc_ref, dma_ref, w_ref, mw_ref, vw_ref, b_ref, mb_ref, vb_ref,
             gw_ref, dw_ref, nmw_ref, nvw_ref, gb_ref, db_ref, nmb_ref, nvb_ref):
        cv = c_ref[...]
        cond = cv * _sigmoid(cv)
        for l in range(2):
            g = _dot_tn(cond, dmc_ref[l])
            gw_ref[l] = g
            dlt, m2, v2 = _adam(w_ref[l], g, mw_ref[l], vw_ref[l])
            dw_ref[l] = dlt
            nmw_ref[l] = m2
            nvw_ref[l] = v2
        gb = dma_ref[0]
        for i in range(1, N_DEV):
            gb = gb + dma_ref[i]
        gb_ref[...] = gb
        dlt, m2, v2 = _adam(b_ref[...], gb, mb_ref[...], vb_ref[...])
        db_ref[...] = dlt
        nmb_ref[...] = m2
        nvb_ref[...] = v2

    wspec = _full((2, D, cols))
    bspec = _full((2, nb))
    wshape = jax.ShapeDtypeStruct((2, D, cols), F32)
    bshape = jax.ShapeDtypeStruct((2, nb), F32)
    return _pallas(
        body, name="ada_update", grid=(1,),
        in_specs=[_full((N_DEV, D)), _full((2, N_DEV, cols)), _full((N_DEV, 2, nb)),
                  wspec, wspec, wspec, bspec, bspec, bspec],
        out_specs=[wspec] * 4 + [bspec] * 4,
        out_shape=[wshape] * 4 + [bshape] * 4,
        compiler_params=_params(("arbitrary",)),
    )(c_all, dmod_cols, dmod_all, ada_w, m_w, v_w, ada_b, m_b, v_b)


def _ev_in(x, mod, w_in, rc, rs1, rs2, comm=None):
    T = x.shape[0]

    def body(x_ref, mod_ref, w_ref, c_ref, s1_ref, s2_ref, q_ref, kv_ref, su_ref, sv_ref, g_ref):
        h = x_ref[...] * (1.0 + mod_ref[1:2, :]) + mod_ref[0:1, :]
        p = _dot_nt(h, w_ref[...])
        c, s1, s2 = c_ref[...], s1_ref[...], s2_ref[...]
        for j in range(ATTN_W // LANE):
            qr = _rope_fwd(p[:, j * LANE:(j + 1) * LANE], c, s1, s2)
            q_ref[:, j * LANE:(j + 1) * LANE] = (qr * (HEAD_DIM ** -0.5)).astype(BF16)
        low = lax.broadcasted_iota(jnp.int32, (TMF, LANE), 1) < HEAD_DIM
        for j, val in enumerate((_rope_fwd(p[:, 512:640], c, s1, s2), p[:, 640:768])):
            swapped = pltpu.roll(val, HEAD_DIM, 1)
            tiles = (jnp.where(low, val, 0.0), jnp.where(low, 0.0, swapped),
                     jnp.where(low, swapped, 0.0), jnp.where(low, 0.0, val))
            for k, tile in enumerate(tiles):
                kv_ref[:, (4 * j + k) * LANE:(4 * j + k + 1) * LANE] = tile.astype(BF16)
        su_ref[...] = p[:, 768:1280].astype(BF16)
        sv_ref[...] = p[:, 1280:1792].astype(BF16)
        g_ref[...] = p[:, 1792:2816].astype(BF16)

    sh = lambda w: jax.ShapeDtypeStruct((T, w), BF16)
    return _fused_call(
        body, comm, (x, mod, w_in, rc, rs1, rs2), name="ev_in", grid=(T // TMF,),
        in_specs=[_tile(TMF, D), _full((3, D)), _full((EV_IN, D)), _tile(TMF, LANE), _tile(TMF, LANE),
                  _tile(TMF, LANE)],
        out_specs=[_tile(TMF, ATTN_W), _tile(TMF, KVX_W), _tile(TMF, SG_W), _tile(TMF, SG_W), _tile(TMF, D)],
        out_shape=[sh(ATTN_W), sh(KVX_W), sh(SG_W), sh(SG_W), sh(D)], semantics=("parallel",))


def _band_specs(width, nb):
    return [pl.BlockSpec((BLK, width), lambda n: (jnp.maximum(n - 1, 0), 0)),
            pl.BlockSpec((BLK, width), lambda n: (n, 0)),
            pl.BlockSpec((BLK, width), lambda n: (jnp.minimum(n + 1, nb - 1), 0))]


def _band_bias(bias_ref, n, nb):
    rows = lax.broadcasted_iota(jnp.int32, (3 * BLK, 1), 0)
    outside = ((rows < BLK) & (n == 0)) | ((rows >= 2 * BLK) & (n == nb - 1))
    return bias_ref[...] + jnp.where(outside, NEG_INF, 0.0)


def _lane_tile(ref, t):
    return ref[:, t * LANE:(t + 1) * LANE]


def _split_bf16(v):
    hi = v.astype(BF16)
    return hi, (v - hi.astype(F32)).astype(BF16)


def _group_mean(v, a_ref, exact_bf16=False):
    hi, lo = _split_bf16(v)
    a = a_ref[...]
    out = []
    for t in range(SG_W // (2 * LANE)):
        sl = slice(t * 2 * LANE, (t + 1) * 2 * LANE)
        r = jnp.dot(hi[:, sl], a, preferred_element_type=F32)
        if not exact_bf16:
            r = r + jnp.dot(lo[:, sl], a, preferred_element_type=F32)
        out.append(r)
    return jnp.concatenate(out, axis=-1)


def _sg_core(sv_ref, lng, lnb, a_ref, w_ref, bfull_ref):
    svf = sv_ref[...].astype(F32)
    xc = svf - _group_mean(svf, a_ref, exact_bf16=True)
    rstd = lax.rsqrt(_group_mean(xc * xc, a_ref) + LN_EPS)
    xhat = xc * rstd
    vb = (xhat * lng + lnb).astype(BF16)
    low = lax.broadcasted_iota(jnp.int32, (BLK, LANE), 1) < SG_DIM
    tiles = []
    for t in range(SG_W // LANE):
        v2 = vb[:, t * LANE:(t + 1) * LANE]
        r0 = jnp.dot(w_ref[2 * t], v2, preferred_element_type=F32)
        r1 = jnp.dot(w_ref[2 * t + 1], v2, preferred_element_type=F32)
        tiles.append(jnp.where(low, r0, r1))
    svm = jnp.concatenate(tiles, axis=-1) + bfull_ref[...]
    return xhat, rstd, vb, svm


def _mix0_fwd(q, kvx, su, sv, g0, sink_l, bias, a128, sg_lng, sg_lnb, sg_w, sg_bfull, comm=None):
    T = q.shape[0]
    nb = T // BLK

    def body(q_ref, kp_ref, kc_ref, kn_ref, su_ref, sv_ref, g_ref, sink_ref, bias_ref, a_ref, lng_ref, lnb_ref,
             w_ref, bfull_ref, ycat_ref, y0_ref, lse_ref):
        n = pl.program_id(0)
        bias = _band_bias(bias_ref, n, nb)
        kvx = jnp.concatenate([kp_ref[...], kc_ref[...], kn_ref[...]], axis=0)
        tiles = []
        for t in range(ATTN_W // LANE):
            qt = _lane_tile(q_ref, t)
            acc = None
            for par in range(2):
                h = 2 * t + par
                kt = 2 * (h // 4) + par
                ke = kvx[:, kt * LANE:(kt + 1) * LANE]
                ve = kvx[:, (4 + kt) * LANE:(5 + kt) * LANE]
                st = _dot_nt(ke, qt) + bias
                sk = _lane_tile(sink_ref, h)
                m = jnp.maximum(jnp.max(st, axis=0, keepdims=True), sk)
                p = jnp.exp(st - m)
                denom = jnp.sum(p, axis=0, keepdims=True) + jnp.exp(sk - m)
                contrib = _dot_tn(p * (1.0 / denom), ve)
                acc = contrib if acc is None else acc + contrib
                lse_ref[0, :, h * LANE:(h + 1) * LANE] = m + jnp.log(denom)
            tiles.append(acc)
        _, _, _, svm = _sg_core(sv_ref, lng_ref[...], lnb_ref[...], a_ref, w_ref, bfull_ref)
        tiles.append(su_ref[...].astype(F32) * svm)
        ycat = jnp.concatenate(tiles, axis=-1)
        gf = g_ref[...].astype(F32)
        ycat_ref[...] = ycat.astype(BF16)
        y0_ref[...] = (ycat * (gf * _sigmoid(gf))).astype(BF16)

    return _fused_call(
        body, comm, (q, kvx, kvx, kvx, su, sv, g0, sink_l, bias, a128, sg_lng, sg_lnb, sg_w, sg_bfull),
        name="mix0_fwd", grid=(nb,),
        in_specs=[_tile(BLK, ATTN_W)] + _band_specs(KVX_W, nb) + [
            _tile(BLK, SG_W), _tile(BLK, SG_W), _tile(BLK, D), _full((1, N_HEADS * LANE)), _full((3 * BLK, LANE)),
            _full((2 * LANE, 2 * LANE)),_full((1, SG_W)), _full((1, SG_W)), _full((SG_GROUPS, BLK, BLK)),
            _full((BLK, SG_W))],
        out_specs=[_tile(BLK, D), _tile(BLK, D), pl.BlockSpec((1, 1, N_HEADS * LANE), lambda n: (n, 0, 0))],
        out_shape=[jax.ShapeDtypeStruct((T, D), BF16), jax.ShapeDtypeStruct((T, D), BF16),
                   jax.ShapeDtypeStruct((nb, 1, N_HEADS * LANE), F32)], semantics=("parallel",))


def _ev_out(y0, w_out, x, mod, lnp):
    T = x.shape[0]

    def body(y_ref, w_ref, x_ref, mod_ref, ln_ref, out_ref, z_ref, x1_ref):
        out = _dot(y_ref[...], w_ref[...])
        z = ALPHA * x_ref[...] + mod_ref[2:3, :] * out
        x1, _, _ = _ln_fwd(z, ln_ref[0:1, :], ln_ref[1:2, :])
        out_ref[...] = out.astype(BF16)
        z_ref[...] = z
        x1_ref[...] = x1

    return _pallas(
        body, name="ev_out", grid=(T // TMF,),
        in_specs=[_tile(TMF, D), _full((D, D)), _tile(TMF, D), _full((3, D)), _full((2, D))],
        out_specs=[_tile(TMF, D)] * 3,
        out_shape=[jax.ShapeDtypeStruct((T, D), BF16), jax.ShapeDtypeStruct((T, D), F32),
                   jax.ShapeDtypeStruct((T, D), F32)],
        compiler_params=_params(("parallel",)),
    )(y0, w_out, x, mod, lnp)


def _od_in(x1, mod, w_in):
    T = x1.shape[0]

    def body(x_ref, mod_ref, w_ref, xr_ref, g_ref):
        h = x_ref[...] * (1.0 + mod_ref[1:2, :]) + mod_ref[0:1, :]
        p = _dot(h, w_ref[...])
        xr_ref[...] = p[:, :D]
        g_ref[...] = p[:, D:].astype(BF16)

    return _pallas(
        body, name="od_in", grid=(T // TMF,),
        in_specs=[_tile(TMF, D), _full((3, D)), _full((D, OD_IN))],
        out_specs=[_tile(TMF, D), _tile(TMF, D)],
        out_shape=[jax.ShapeDtypeStruct((T, D), F32), jax.ShapeDtypeStruct((T, D), BF16)],
        compiler_params=_params(("parallel",)),
    )(x1, mod, w_in)


def _ext_rows(prev_ref, cur, next_ref, j, n):
    prev = jnp.where(j > 0, prev_ref[...], 0.0)
    nxt = jnp.where(j < n - 1, next_ref[...], 0.0)
    return jnp.concatenate([prev, cur, nxt], axis=0)


def _shift_rows(ext, off, rows):
    total = ext.shape[0]
    if off == 0:
        return ext[SUBLANE:SUBLANE + rows, :]
    return pltpu.roll(ext, (-off) % total, 0)[SUBLANE:SUBLANE + rows, :]


def _conv_fwd(ext, cw, cb, rows):
    xc = cb
    for k in range(4):
        xc = xc + cw[k:k + 1, :] * _shift_rows(ext, k - 2, rows)
    return xc


def _gates(xc, wa_ref, wx_ref, ba, bx, lam):
    pr, pi = [], []
    for h in range(RNN_HEADS):
        xh = xc[:, h * RNN_HD:(h + 1) * RNN_HD].astype(BF16)
        pr.append(_dot(xh, wa_ref[h]))
        pi.append(_dot(xh, wx_ref[h]))
    r = _sigmoid(jnp.concatenate(pr, axis=-1) + ba)
    ig = _sigmoid(jnp.concatenate(pi, axis=-1) + bx)
    sp = jnp.maximum(-lam, 0.0) + jnp.log(1.0 + jnp.exp(-jnp.abs(lam)))
    neg_log_a = RG_C * r * sp
    a = jnp.exp(-neg_log_a)
    s2 = (1.0 + a * a) * jnp.tanh(neg_log_a)
    inv_s = lax.rsqrt(jnp.maximum(s2, 1e-30))
    return r, ig, sp, a, s2 * inv_s, inv_s


def _scan_tile(a_ref, b_ref, o_ref, carry_ref, rows, reverse):
    ridx = lax.broadcasted_iota(jnp.int32, (SUBLANE, D), 0)
    groups = rows // SUBLANE

    def group(gi, h):
        g = (groups - 1 - gi) if reverse else gi
        off = pl.multiple_of(g * SUBLANE, SUBLANE)
        a = a_ref[pl.ds(off, SUBLANE), :]
        b = b_ref[pl.ds(off, SUBLANE), :]
        for sh in (1, 2, 4):
            if reverse:
                keep = ridx < SUBLANE - sh
                a_p = jnp.where(keep, pltpu.roll(a, SUBLANE - sh, 0), 1.0)
                b_p = jnp.where(keep, pltpu.roll(b, SUBLANE - sh, 0), 0.0)
            else:
                keep = ridx >= sh
                a_p = jnp.where(keep, pltpu.roll(a, sh, 0), 1.0)
                b_p = jnp.where(keep, pltpu.roll(b, sh, 0), 0.0)
            b = b + a * b_p
            a = a * a_p
        hh = b + a * h
        o_ref[pl.ds(off, SUBLANE), :] = hh
        return hh[0:1, :] if reverse else hh[SUBLANE - 1:SUBLANE, :]

    carry_ref[...] = lax.fori_loop(0, groups, group, carry_ref[...])


def _rglru_fwd(xr, cw, cb, wa, wx, ba, bx, lam, reverse, name):
    T = xr.shape[0]
    n = T // TS
    prev_spec, next_spec = _halo_specs(TS, D, n, T, reverse)

    def body(prev_ref, cur_ref, next_ref, cw_ref, cb_ref, wa_ref, wx_ref, ba_ref, bx_ref, lam_ref,
             h_ref, a_ref, s_ref, r_ref, ig_ref, xc_ref, b_s, carry):
        i = pl.program_id(0)
        j = (n - 1 - i) if reverse else i

        @pl.when(i == 0)
        def _():
            carry[...] = jnp.zeros_like(carry)

        ext = _ext_rows(prev_ref, cur_ref[...], next_ref, j, n)
        xc = _conv_fwd(ext, cw_ref[...], cb_ref[...], TS)
        r, ig, _, a, s, _ = _gates(xc, wa_ref, wx_ref, ba_ref[...], bx_ref[...], lam_ref[...])
        s_ref[...] = s
        r_ref[...] = r.astype(BF16)
        ig_ref[...] = ig.astype(BF16)
        xc_ref[...] = xc.astype(BF16)
        a_ref[...] = a
        b_s[...] = s * ig * xc
        _scan_tile(a_ref, b_s, h_ref, carry, TS, reverse)

    wspec = _full((RNN_HEADS, RNN_HD, RNN_HD))
    cur = _rev_tile(TS, D, n, reverse)
    f32 = jax.ShapeDtypeStruct((T, D), F32)
    b16 = jax.ShapeDtypeStruct((T, D), BF16)
    return _pallas(
        body, name=name, grid=(n,),
        in_specs=[prev_spec, cur, next_spec, _full((4, D)), _full((1, D)),
                  wspec, wspec, _full((1, D)), _full((1, D)), _full((1, D))],
        out_specs=[cur] * 6,
        out_shape=[f32, f32, f32, b16, b16, b16],
        scratch_shapes=[pltpu.VMEM((TS, D), F32), pltpu.VMEM((1, D), F32)],
        compiler_params=_params(("arbitrary",)),
    )(xr, xr, xr, cw, cb, wa, wx, ba, bx, lam)


def _od_out(hf, hb, g1, w_out, x1, tgt, mod, lnp):
    T = x1.shape[0]

    def body(hf_ref, hb_ref, g_ref, w_ref, x_ref, t_ref, mod_ref, ln_ref,
             dh_ref, dg_ref, dx_ref, dwb_ref, vec_ref, dw_ref):
        i = pl.program_id(0)

        @pl.when(i == 0)
        def _():
            dw_ref[...] = jnp.zeros_like(dw_ref)
            vec_ref[...] = jnp.zeros_like(vec_ref)

        hs = hf_ref[...] + hb_ref[...]
        sg, dsg = _silu_and_grad(g_ref[...].astype(F32))
        yr = (hs * sg).astype(BF16)
        w = w_ref[...]
        out = _dot(yr, w)
        gate = mod_ref[2:3, :]
        z = ALPHA * x_ref[...] + gate * out
        lng = ln_ref[0:1, :]
        x2, xhat, rstd = _ln_fwd(z, lng, ln_ref[1:2, :])
        diff = x2 - t_ref[...]
        vec_ref[3:4, 0:LANE] += 0.5 * jnp.sum(diff * diff) * (1.0 / D)
        dx2 = diff * (1.0 / D)
        dz = _ln_bwd(dx2, xhat, rstd, lng)
        vec_ref[0:1, :] += _rowsum(dx2 * xhat)
        vec_ref[1:2, :] += _rowsum(dx2)
        vec_ref[2:3, :] += _rowsum(dz * out)
        dout = (dz * gate).astype(BF16)
        dyr = _dot_nt(dout, w)
        dw_ref[...] += _dot_tn(yr, dout)
        dh_ref[...] = dyr * sg
        dg_ref[...] = (dyr * hs * dsg).astype(BF16)
        dx_ref[...] = ALPHA * dz

        @pl.when(i == T // TMO - 1)
        def _():
            dwb_ref[...] = dw_ref[...].astype(BF16)

    return _pallas(
        body, name="od_out", grid=(T // TMO,),
        in_specs=[_tile(TMO, D), _tile(TMO, D), _tile(TMO, D), _full((D, D)), _tile(TMO, D), _tile(TMO, D),
                  _full((3, D)), _full((2, D))],
        out_specs=[_tile(TMO, D), _tile(TMO, D), _tile(TMO, D), _full((D, D)), _full((SUBLANE, D))],
        out_shape=[jax.ShapeDtypeStruct((T, D), F32), jax.ShapeDtypeStruct((T, D), BF16),
                   jax.ShapeDtypeStruct((T, D), F32), jax.ShapeDtypeStruct((D, D), BF16),
                   jax.ShapeDtypeStruct((SUBLANE, D), F32)],
        scratch_shapes=[pltpu.VMEM((D, D), F32)],
        compiler_params=_params(("arbitrary",)),
    )(hf, hb, g1, w_out, x1, tgt, mod, lnp)


def _rglru_bwd(fwd, dh, wa, wx, lam, reverse, name, comm=None):
    h, a_all, s_all, r_all, ig_all, xc_all = fwd
    T = h.shape[0]
    n = T // TS
    adj_rev = not reverse
    hprev_spec, hnext_spec = _halo_specs(TS, D, n, T, adj_rev)
    h_halo_spec = hnext_spec if reverse else hprev_spec

    def body(dh_ref, h_ref, hh_ref, a_ref, s_ref, r_ref, ig_ref, xc_ref, wa_ref, wx_ref, lam_ref,
             dxc_ref, dwa_ref, dwx_ref, vec_ref, a_s, l_s, carry, a_edge):
        i = pl.program_id(0)
        j = (n - 1 - i) if adj_rev else i

        @pl.when(i == 0)
        def _():
            carry[...] = jnp.zeros_like(carry)
            a_edge[...] = jnp.zeros_like(a_edge)
            dwa_ref[...] = jnp.zeros_like(dwa_ref)
            dwx_ref[...] = jnp.zeros_like(dwx_ref)
            vec_ref[...] = jnp.zeros_like(vec_ref)

        lam = lam_ref[...]
        sp = jnp.maximum(-lam, 0.0) + jnp.log(1.0 + jnp.exp(-jnp.abs(lam)))
        a, s = a_ref[...], s_ref[...]
        inv_s = lax.rsqrt(jnp.maximum(s * s, 1e-30))
        r, ig = r_ref[...].astype(F32), ig_ref[...].astype(F32)
        xcb = xc_ref[...]
        xc = xcb.astype(F32)

        rows = lax.broadcasted_iota(jnp.int32, (TS, D), 0)
        hcur = h_ref[...]
        if reverse:
            a_sh = jnp.where(rows == 0, a_edge[...], pltpu.roll(a, 1, 0))
            halo = jnp.where(j < n - 1, hh_ref[0:1, :], 0.0)
            h_nb = jnp.where(rows == TS - 1, halo, pltpu.roll(hcur, TS - 1, 0))
        else:
            a_sh = jnp.where(rows == TS - 1, a_edge[...], pltpu.roll(a, TS - 1, 0))
            halo = jnp.where(j > 0, hh_ref[SUBLANE - 1:SUBLANE, :], 0.0)
            h_nb = jnp.where(rows == 0, halo, pltpu.roll(hcur, 1, 0))
        a_s[...] = a_sh
        _scan_tile(a_s, dh_ref, l_s, carry, TS, adj_rev)
        a_edge[...] = a[TS - 1:TS, :] if reverse else a[0:1, :]

        lm = l_s[...]
        da = lm * h_nb
        di = lm * s * xc
        dxc = lm * s * ig
        ds = lm * ig * xc
        dlog_a = a * (da - ds * a * inv_s)
        dr = (-RG_C) * sp * dlog_a
        dsp = _rowsum((-RG_C) * r * dlog_a)
        dpr = dr * r * (1.0 - r)
        dpi = di * ig * (1.0 - ig)
        vec_ref[0:1, :] += _rowsum(dpr)
        vec_ref[1:2, :] += _rowsum(dpi)
        vec_ref[2:3, :] += dsp * (-_sigmoid(-lam))
        parts = []
        for hd in range(RNN_HEADS):
            sl = slice(hd * RNN_HD, (hd + 1) * RNN_HD)
            xh = xcb[:, sl]
            dprh = dpr[:, sl].astype(BF16)
            dpih = dpi[:, sl].astype(BF16)
            parts.append(_dot_nt(dprh, wa_ref[hd]) + _dot_nt(dpih, wx_ref[hd]))
            dwa_ref[hd] += _dot_tn(xh, dprh)
            dwx_ref[hd] += _dot_tn(xh, dpih)
        dxc_ref[...] = dxc + jnp.concatenate(parts, axis=-1)

    wspec = _full((RNN_HEADS, RNN_HD, RNN_HD))
    cur = _rev_tile(TS, D, n, adj_rev)
    return _fused_call(
        body, comm, (dh, h, h, a_all, s_all, r_all, ig_all, xc_all, wa, wx, lam), name=name, grid=(n,),
        in_specs=[cur, cur, h_halo_spec, cur, cur, cur, cur, cur, wspec, wspec, _full((1, D))],
        out_specs=[cur, wspec, wspec, _full((SUBLANE, D))],
        out_shape=[jax.ShapeDtypeStruct((T, D), F32),
                   jax.ShapeDtypeStruct((RNN_HEADS, RNN_HD, RNN_HD), F32),
                   jax.ShapeDtypeStruct((RNN_HEADS, RNN_HD, RNN_HD), F32),
                   jax.ShapeDtypeStruct((SUBLANE, D), F32)],
        scratch_shapes=[pltpu.VMEM((TS, D), F32)] * 2 + [pltpu.VMEM((1, D), F32)] * 2)


def _od_in_bwd(dxcf, dxcb, xr, dg1, x1, dx1p, mod, w_in, cw, comm=None):
    T = x1.shape[0]
    n = T // TMO
    slab = OD_IN // N_DEV
    prev_spec, next_spec = _halo_specs(TMO, D, n, T, False)

    def body(fp_ref, fc_ref, fn_ref, bp_ref, bc_ref, bn_ref, xr_ref, dg_ref, x1_ref, dxp_ref,
             mod_ref, w_ref, cw_ref, dx_ref, dwb_ref, vec_ref, dw_ref):
        i = pl.program_id(0)

        @pl.when(i == 0)
        def _():
            dw_ref[...] = jnp.zeros_like(dw_ref)
            vec_ref[...] = jnp.zeros_like(vec_ref)

        dcur = fc_ref[...] + bc_ref[...]
        dprev = jnp.where(i > 0, fp_ref[...] + bp_ref[...], 0.0)
        dnext = jnp.where(i < n - 1, fn_ref[...] + bn_ref[...], 0.0)
        dext = jnp.concatenate([dprev, dcur, dnext], axis=0)
        xr_v = xr_ref[...]
        cw_v = cw_ref[...]
        dxr = None
        for k in range(4):
            shifted = _shift_rows(dext, 2 - k, TMO)
            term = cw_v[k:k + 1, :] * shifted
            dxr = term if dxr is None else dxr + term
            vec_ref[k:k + 1, :] += _rowsum(shifted * xr_v)
        vec_ref[4:5, :] += _rowsum(dcur)
        dp = jnp.concatenate([dxr.astype(BF16), dg_ref[...]], axis=-1)
        x1v = x1_ref[...]
        scale1 = 1.0 + mod_ref[1:2, :]
        h1 = (x1v * scale1 + mod_ref[0:1, :]).astype(BF16)
        dh1 = _dot_nt(dp, w_ref[...])
        dw_ref[...] += _dot_tn(h1, dp)
        dx_ref[...] = dxp_ref[...] + dh1 * scale1
        vec_ref[5:6, :] += _rowsum(dh1)
        vec_ref[6:7, :] += _rowsum(dh1 * x1v)

        @pl.when(i == n - 1)
        def _():
            for j in range(N_DEV):
                dwb_ref[j] = dw_ref[:, j * slab:(j + 1) * slab].astype(BF16)

    t = _tile(TMO, D)
    return _fused_call(
        body, comm, (dxcf, dxcf, dxcf, dxcb, dxcb, dxcb, xr, dg1, x1, dx1p, mod, w_in, cw),
        name="od_in_bwd", grid=(n,),
        in_specs=[prev_spec, t, next_spec, prev_spec, t, next_spec, t, t, t, t,
                  _full((3, D)), _full((D, OD_IN)), _full((4, D))],
        out_specs=[t, _full((N_DEV, D, slab)), _full((SUBLANE, D))],
        out_shape=[jax.ShapeDtypeStruct((T, D), F32), jax.ShapeDtypeStruct((N_DEV, D, slab), BF16),
                   jax.ShapeDtypeStruct((SUBLANE, D), F32)],
        scratch_shapes=[pltpu.VMEM((D, OD_IN), F32)])


def _ev_out_bwd(dx1, z0, out0, y0, ycat, g0, w_out, mod, lnp):
    T = dx1.shape[0]

    def body(dx_ref, z_ref, out_ref, y0_ref, yc_ref, g_ref, w_ref, mod_ref, ln_ref,
             dxp_ref, dyc_ref, dg_ref, dwb_ref, vec_ref, dw_ref):
        i = pl.program_id(0)

        @pl.when(i == 0)
        def _():
            dw_ref[...] = jnp.zeros_like(dw_ref)
            vec_ref[...] = jnp.zeros_like(vec_ref)

        lng = ln_ref[0:1, :]
        _, xhat, rstd = _ln_fwd(z_ref[...], lng, ln_ref[1:2, :])
        dy = dx_ref[...]
        dz = _ln_bwd(dy, xhat, rstd, lng)
        vec_ref[0:1, :] += _rowsum(dy * xhat)
        vec_ref[1:2, :] += _rowsum(dy)
        vec_ref[2:3, :] += _rowsum(dz * out_ref[...].astype(F32))
        dout = (dz * mod_ref[2:3, :]).astype(BF16)
        dy0 = _dot_nt(dout, w_ref[...])
        dw_ref[...] += _dot_tn(y0_ref[...], dout)
        sg, dsg = _silu_and_grad(g_ref[...].astype(F32))
        dyc_ref[...] = (dy0 * sg).astype(BF16)
        dg_ref[...] = (dy0 * yc_ref[...].astype(F32) * dsg).astype(BF16)
        dxp_ref[...] = ALPHA * dz

        @pl.when(i == T // TMO - 1)
        def _():
            dwb_ref[...] = dw_ref[...].astype(BF16)

    t = _tile(TMO, D)
    return _pallas(
        body, name="ev_out_bwd", grid=(T // TMO,),
        in_specs=[t, t, t, t, t, t, _full((D, D)), _full((3, D)), _full((2, D))],
        out_specs=[t, t, t, _full((D, D)), _full((SUBLANE, D))],
        out_shape=[jax.ShapeDtypeStruct((T, D), F32), jax.ShapeDtypeStruct((T, D), BF16),
                   jax.ShapeDtypeStruct((T, D), BF16), jax.ShapeDtypeStruct((D, D), BF16),
                   jax.ShapeDtypeStruct((SUBLANE, D), F32)],
        scratch_shapes=[pltpu.VMEM((D, D), F32)],
        compiler_params=_params(("arbitrary",)),
    )(dx1, z0, out0, y0, ycat, g0, w_out, mod, lnp)


def _mix0_bwd(q, kvx, lse, dyc, ycat, su, sv, sink_l, bias, a128, gsum, sel, sg_lng, sg_lnb, sg_w, sg_bfull,
              rc, rs1, rs2, comm=None):
    T = q.shape[0]
    nb = T // BLK

    def body(q_ref, kp_ref, kc_ref, kn_ref, lse_ref, dyc_ref, yc_ref, su_ref, sv_ref, sink_ref, bias_ref, a_ref,
             gsum_ref, sel_ref, lng_ref, lnb_ref, w_ref, bfull_ref, c_ref, s1_ref, s2_ref,
             dq_ref, dkv_ref, dsu_ref, dsv_ref, dw_ref, dbt_ref, vec_ref, dsink_ref):
        n = pl.program_id(0)

        @pl.when(n == 0)
        def _():
            dkv_ref[...] = jnp.zeros_like(dkv_ref)
            dw_ref[...] = jnp.zeros_like(dw_ref)
            dbt_ref[...] = jnp.zeros_like(dbt_ref)
            vec_ref[...] = jnp.zeros_like(vec_ref)
            dsink_ref[...] = jnp.zeros_like(dsink_ref)

        band = pl.ds(pl.multiple_of(n * BLK + (TM - BLK), BLK), 3 * BLK)
        bias = _band_bias(bias_ref, n, nb)
        kvx = jnp.concatenate([kp_ref[...], kc_ref[...], kn_ref[...]], axis=0)
        bias2 = jnp.concatenate([bias, bias], axis=1)
        low = lax.broadcasted_iota(jnp.int32, (BLK, LANE), 1) < HEAD_DIM
        low2 = lax.broadcasted_iota(jnp.int32, (2 * BLK, LANE), 1) < HEAD_DIM
        sel = sel_ref[...]
        c, s1, s2 = c_ref[...], s1_ref[...], s2_ref[...]
        for kvh in range(2):
            t0, t1 = 2 * kvh, 2 * kvh + 1
            q2 = jnp.concatenate([_lane_tile(q_ref, t0), _lane_tile(q_ref, t1)], axis=0)
            do2 = jnp.concatenate([_lane_tile(dyc_ref, t0), _lane_tile(dyc_ref, t1)], axis=0)
            yc2 = jnp.concatenate([_lane_tile(yc_ref, t0), _lane_tile(yc_ref, t1)], axis=0)
            p_hi, p_lo = _split_bf16(do2.astype(F32) * yc2.astype(F32))
            deltas = _dot_nt(sel, p_hi) + _dot_nt(sel, p_lo)
            dkx = jnp.zeros((3 * BLK, LANE), F32)
            dvx = jnp.zeros((3 * BLK, LANE), F32)
            dq_acc = None
            for par in range(2):
                heads = (4 * kvh + par, 4 * kvh + 2 + par)
                kt = 2 * kvh + par
                ke = kvx[:, kt * LANE:(kt + 1) * LANE]
                ve = kvx[:, (4 + kt) * LANE:(5 + kt) * LANE]
                lse = jnp.concatenate([lse_ref[0, :, h * LANE:(h + 1) * LANE] for h in heads], axis=1)
                sk = jnp.concatenate([_lane_tile(sink_ref, h) for h in heads], axis=1)
                delta = deltas[par:par + 1, :]
                pt = jnp.exp(_dot_nt(ke, q2) + bias2 - lse)
                dst = (pt * (_dot_nt(ve, do2) - delta)).astype(BF16)
                sink_terms = jnp.exp(sk - lse) * delta
                for k, h in enumerate(heads):
                    dsink_ref[:, h * LANE:(h + 1) * LANE] += sink_terms[:, k * LANE:(k + 1) * LANE]
                part = _dot_tn(dst, ke)
                dq_acc = part if dq_acc is None else dq_acc + part
                mine = low2 if par == 0 else jnp.logical_not(low2)
                dkx = dkx + jnp.dot(dst, jnp.where(mine, q2, jnp.zeros_like(q2)), preferred_element_type=F32)
                dvx = dvx + jnp.dot(pt.astype(BF16), jnp.where(mine, do2, jnp.zeros_like(do2)),
                                    preferred_element_type=F32)
            for k, t in enumerate((t0, t1)):
                dq_t = dq_acc[k * BLK:(k + 1) * BLK] * (HEAD_DIM ** -0.5)
                dq_ref[:, t * LANE:(t + 1) * LANE] = _rope_bwd(dq_t, c, s1, s2).astype(BF16)
            dkv_ref[band, kvh * LANE:(kvh + 1) * LANE] += dkx
            dkv_ref[band, (2 + kvh) * LANE:(3 + kvh) * LANE] += dvx

        lng = lng_ref[...]
        xhat, rstd, vb, svm = _sg_core(sv_ref, lng, lnb_ref[...], a_ref, w_ref, bfull_ref)
        dy = dyc_ref[:, ATTN_W:].astype(F32)
        dsu_ref[...] = (dy * svm).astype(BF16)
        dsvm = dy * su_ref[...].astype(F32)
        d_hi, d_lo = _split_bf16(dsvm)
        gsum = gsum_ref[...]
        dbt_ref[...] += jnp.dot(d_hi, gsum, preferred_element_type=F32) + jnp.dot(d_lo, gsum,
                                                                                 preferred_element_type=F32)
        tiles = []
        for t in range(SG_W // LANE):
            tl = slice(t * LANE, (t + 1) * LANE)
            dt, v2 = d_hi[:, tl], vb[:, tl]
            dw_ref[2 * t] += _dot_nt(jnp.where(low, dt, jnp.zeros_like(dt)), v2)
            dw_ref[2 * t + 1] += _dot_nt(jnp.where(low, jnp.zeros_like(dt), dt), v2)
            tiles.append(jnp.where(low, _dot_tn(w_ref[2 * t], dt), _dot_tn(w_ref[2 * t + 1], dt)))
        dvgn = jnp.concatenate(tiles, axis=-1)
        vec_ref[0:1, :] += _rowsum(dvgn * xhat)
        vec_ref[1:2, :] += _rowsum(dvgn)
        dxh = dvgn * lng
        m1 = _group_mean(dxh, a_ref)
        m2 = _group_mean(dxh * xhat, a_ref)
        dsv_ref[...] = (rstd * (dxh - m1 - xhat * m2)).astype(BF16)

    return _fused_call(
        body, comm, (q, kvx, kvx, kvx, lse, dyc, ycat, su, sv, sink_l, bias, a128, gsum, sel, sg_lng, sg_lnb, sg_w,
                     sg_bfull, rc, rs1, rs2),
        name="mix0_bwd", grid=(nb,),
        in_specs=[_tile(BLK, ATTN_W)] + _band_specs(KVX_W, nb) + [
            pl.BlockSpec((1, 1, N_HEADS * LANE), lambda n: (n, 0, 0)), _tile(BLK, D), _tile(BLK, D),
            _tile(BLK, SG_W), _tile(BLK, SG_W), _full((1, N_HEADS * LANE)), _full((3 * BLK, LANE)),
            _full((2 * LANE, 2 * LANE)),_full((SG_W, LANE)), _full((SUBLANE, LANE)), _full((1, SG_W)), _full((1, SG_W)),
            _full((SG_GROUPS, BLK, BLK)), _full((BLK, SG_W)), _tile(BLK, LANE), _tile(BLK, LANE), _tile(BLK, LANE)],
        out_specs=[_tile(BLK, ATTN_W), _full((T + 2 * TM, 4 * LANE)), _tile(BLK, SG_W), _tile(BLK, SG_W),
                   _full((SG_GROUPS, BLK, BLK)), _full((BLK, LANE)), _full((SUBLANE, SG_W)),
                   _full((1, N_HEADS * LANE))],
        out_shape=[jax.ShapeDtypeStruct((T, ATTN_W), BF16), jax.ShapeDtypeStruct((T + 2 * TM, 4 * LANE), F32),
                   jax.ShapeDtypeStruct((T, SG_W), BF16), jax.ShapeDtypeStruct((T, SG_W), BF16),
                   jax.ShapeDtypeStruct((SG_GROUPS, BLK, BLK), F32), jax.ShapeDtypeStruct((BLK, LANE), F32),
                   jax.ShapeDtypeStruct((SUBLANE, SG_W), F32), jax.ShapeDtypeStruct((1, N_HEADS * LANE), F32)])


def _ev_in_bwd(dq, dkv, dsu, dsv, dg0, x, dxp, mod, w_in, rc, rs1, rs2, comm=None):
    T = x.shape[0]

    def body(dq_ref, dkv_ref, dsu_ref, dsv_ref, dg_ref, x_ref, dxp_ref, mod_ref, w_ref, c_ref, s1_ref, s2_ref,
             dx_ref, dwb_ref, vec_ref, dw_ref):
        i = pl.program_id(0)

        @pl.when(i == 0)
        def _():
            dw_ref[...] = jnp.zeros_like(dw_ref)
            vec_ref[...] = jnp.zeros_like(vec_ref)

        low = lax.broadcasted_iota(jnp.int32, (TM, LANE), 1) < HEAD_DIM

        def fold(j):
            t0 = dkv_ref[:, (2 * j) * LANE:(2 * j + 1) * LANE]
            t1 = dkv_ref[:, (2 * j + 1) * LANE:(2 * j + 2) * LANE]
            return jnp.where(low, t0 + pltpu.roll(t0, HEAD_DIM, 1), t1 + pltpu.roll(t1, HEAD_DIM, 1))

        dk = _rope_bwd(fold(0), c_ref[...], s1_ref[...], s2_ref[...]).astype(BF16)
        dp = jnp.concatenate([dq_ref[...], dk, fold(1).astype(BF16), dsu_ref[...], dsv_ref[...],
                              dg_ref[...]], axis=-1)
        xv = x_ref[...]
        scale0 = 1.0 + mod_ref[1:2, :]
        h0 = (xv * scale0 + mod_ref[0:1, :]).astype(BF16)
        dh0 = _dot(dp, w_ref[...])
        dw_ref[...] += _dot_tn(dp, h0)
        dx_ref[...] = dxp_ref[...] + dh0 * scale0
        vec_ref[0:1, :] += _rowsum(dh0)
        vec_ref[1:2, :] += _rowsum(dh0 * xv)

        @pl.when(i == T // TM - 1)
        def _():
            dwb_ref[...] = dw_ref[...].astype(BF16)

    t = _tile(TM, D)
    return _fused_call(
        body, comm, (dq, dkv, dsu, dsv, dg0, x, dxp, mod, w_in, rc, rs1, rs2), name="ev_in_bwd", grid=(T // TM,),
        in_specs=[_tile(TM, ATTN_W), pl.BlockSpec((TM, 4 * LANE), lambda i: (i + 1, 0)), _tile(TM, SG_W),
                  _tile(TM, SG_W), t, t, t,
                  _full((3, D)), _full((EV_IN, D)), _tile(TM, LANE), _tile(TM, LANE), _tile(TM, LANE)],
        out_specs=[t, _full((EV_IN, D)), _full((SUBLANE, D))],
        out_shape=[jax.ShapeDtypeStruct((T, D), F32), jax.ShapeDtypeStruct((EV_IN, D), BF16),
                   jax.ShapeDtypeStruct((SUBLANE, D), F32)],
        scratch_shapes=[pltpu.VMEM((EV_IN, D), F32)])


def _sum_slots(land_ref):
    g = land_ref[0].astype(F32)
    for i in range(1, land_ref.shape[0]):
        g = g + land_ref[i].astype(F32)
    return g


def _reduce_adam(items, name, after=()):
    R, C = items[0][1].shape
    rb = R
    if R > 512:
        for cand in (512, 256, 128, 64, 32, 16, 8):
            if R % cand == 0:
                rb = cand
                break
    n = len(items)

    def body(*refs):
        for k in range(n):
            l_ref, w_ref, m_ref, v_ref = refs[4 * k:4 * k + 4]
            first_out = 4 * n + len(after)
            g_ref, d_ref, nm_ref, nv_ref = refs[first_out + 4 * k:first_out + 4 * k + 4]
            g = _sum_slots(l_ref)
            g_ref[...] = g
            dlt, m2, v2 = _adam(w_ref[...], g, m_ref[...], v_ref[...])
            d_ref[...] = dlt
            nm_ref[...] = m2
            nv_ref[...] = v2

    t = pl.BlockSpec((rb, C), lambda i: (i, 0))
    shp = jax.ShapeDtypeStruct((R, C), F32)
    in_specs, operands = [], []
    for land, w, m, v in items:
        in_specs += [pl.BlockSpec((land.shape[0], rb, C), lambda i: (0, i, 0)), t, t, t]
        operands += [land, w, m, v]
    res = _pallas(
        body, name=name, grid=(R // rb,),
        in_specs=in_specs + [pl.BlockSpec(memory_space=pl.ANY)] * len(after),
        out_specs=[t] * (4 * n), out_shape=[shp] * (4 * n),
        compiler_params=_params(("parallel",)),
    )(*operands, *after)
    return [list(res[4 * k:4 * k + 4]) for k in range(n)]


def _tail_stage1(slabs, small):
    _, R, C = slabs.shape
    n_chips = N_DEV // 2
    gather = _GatherComm(small)
    ns = gather.n

    def body(*refs):
        slab_ref = refs[0]
        g_ins = refs[1:1 + ns]
        part, land_ref = refs[1 + ns], refs[2 + ns]
        g_outs = refs[3 + ns:3 + 2 * ns]
        stage, s1_send, s1_recv = refs[3 + 2 * ns:6 + 2 * ns]
        g_sems = refs[6 + 2 * ns:]
        x, y, c = _my_pos()
        chip = 2 * x + y
        gather.start(g_ins, g_outs, g_sems)
        swaps = [pltpu.make_async_remote_copy(
            src_ref=slab_ref.at[2 * k + (1 - c)], dst_ref=stage.at[k], send_sem=s1_send.at[k],
            recv_sem=s1_recv.at[k], device_id=(x, y, 1 - c), device_id_type=MESH) for k in range(n_chips)]
        for cp in swaps:
            cp.start()
        for cp in swaps:
            cp.wait()
        for k in range(n_chips):
            part[k] = (slab_ref[2 * k + c].astype(F32) + stage[k].astype(F32)).astype(BF16)
        land_ref[chip] = part[chip]
        gather.mid(g_ins, g_outs, g_sems)
        gather.finish(g_ins, g_outs, g_sems)

    any_spec = pl.BlockSpec(memory_space=pl.ANY)
    vmem_spec = pl.BlockSpec(memory_space=pltpu.VMEM)
    slab4 = jax.ShapeDtypeStruct((n_chips, R, C), BF16)
    res = _pallas(
        body, name="tail_stage1",
        out_shape=[slab4, slab4] + gather.out_shapes(),
        in_specs=[vmem_spec] + [any_spec] * ns, out_specs=[vmem_spec, vmem_spec] + [any_spec] * ns,
        scratch_shapes=[pltpu.VMEM((n_chips, R, C), BF16),
                        pltpu.SemaphoreType.DMA((n_chips,)), pltpu.SemaphoreType.DMA((n_chips,))] + gather.sems(),
        compiler_params=pltpu.CompilerParams(vmem_limit_bytes=VMEM_LIMIT),
    )(slabs, *gather.arrs)
    return res[0], res[1], list(res[2:])


def _chip_copies(part_ref, land_ref, send_sems, recv_sems):
    x, y, c = _my_pos()
    chip = 2 * x + y
    copies = []
    for r in range(1, N_DEV // 2):
        px = (1 - x) if (r & 2) else x
        py = (1 - y) if (r & 1) else y
        copies.append(pltpu.make_async_remote_copy(
            src_ref=part_ref.at[2 * px + py], dst_ref=land_ref.at[chip], send_sem=send_sems[r - 1],
            recv_sem=recv_sems[r - 1], device_id=(px, py, c), device_id_type=MESH))
    return copies


def _tail_send(part, land):
    n = N_DEV // 2 - 1

    def body(part_ref, land_ref, *outs):
        send_sems, recv_sems = outs[:n], outs[n:2 * n]
        token = outs[2 * n + 2]
        for cp in _chip_copies(part_ref, land_ref, send_sems, recv_sems):
            cp.start()
        token[...] = jnp.zeros_like(token)

    hbm = pl.BlockSpec(memory_space=pltpu.HBM)
    sem = pl.BlockSpec(memory_space=pltpu.SEMAPHORE)
    res = _pallas(
        body, name="tail_send",
        out_shape=tuple([pltpu.SemaphoreType.DMA(())] * (2 * n)
                        + [pltpu.HBM(part.shape, part.dtype), pltpu.HBM(land.shape, land.dtype),
                           jax.ShapeDtypeStruct((SUBLANE, LANE), F32)]),
        in_specs=(hbm, hbm), out_specs=tuple([sem] * (2 * n) + [hbm, hbm, pl.BlockSpec(memory_space=pltpu.VMEM)]),
        input_output_aliases={0: 2 * n, 1: 2 * n + 1},
        compiler_params=pltpu.CompilerParams(has_side_effects=pltpu.SideEffectType.DATAFLOW_SIDE_EFFECTING),
    )(pltpu.with_memory_space_constraint(part, pltpu.HBM), pltpu.with_memory_space_constraint(land, pltpu.HBM))
    return list(res[:n]), list(res[n:2 * n]), res[2 * n], res[2 * n + 1], res[2 * n + 2]


def _tail_wait(send_sems, recv_sems, part, land, after):
    n = len(send_sems)

    def body(part_ref, land_ref, *rest):
        ss, rs = rest[:n], rest[n:2 * n]
        for cp in _chip_copies(part_ref, land_ref, ss, rs):
            cp.wait_send()
            cp.wait_recv()

    hbm = pl.BlockSpec(memory_space=pltpu.HBM)
    sem = pl.BlockSpec(memory_space=pltpu.SEMAPHORE)
    any_spec = pl.BlockSpec(memory_space=pl.ANY)
    res = _pallas(
        body, name="tail_wait",
        out_shape=(pltpu.HBM(part.shape, part.dtype), pltpu.HBM(land.shape, land.dtype)),
        in_specs=tuple([hbm, hbm] + [sem] * (2 * n) + [any_spec] * len(after)), out_specs=(hbm, hbm),
        input_output_aliases={0: 0, 1: 1},
        compiler_params=pltpu.CompilerParams(has_side_effects=pltpu.SideEffectType.DATAFLOW_SIDE_EFFECTING),
    )(part, land, *send_sems, *recv_sems, *after)
    return res[1]


def _slots_adam(items, name, after=()):
    zeros3 = (0, 0, 0)
    in_specs, out_specs, out_shape, operands = [], [], [], []
    for land, w, m, v in items:
        inner = w.shape[-3:]
        if w.ndim == 5:
            lspec = pl.BlockSpec((N_DEV, 1) + inner, lambda i: (0, i) + zeros3)
            wspec = pl.BlockSpec((1, 1) + inner, lambda i: (0, i) + zeros3)
        else:
            lspec = pl.BlockSpec((N_DEV,) + inner, lambda i: (0,) + zeros3)
            wspec = pl.BlockSpec((1,) + inner, lambda i: (0,) + zeros3)
        in_specs += [lspec, wspec, wspec, wspec]
        out_specs += [wspec] * 4
        out_shape += [jax.ShapeDtypeStruct(w.shape, F32)] * 4
        operands += [land, w, m, v]
    n = len(items)

    def body(*refs):
        for k, (_, w, _, _) in enumerate(items):
            l_ref, w_ref, m_ref, v_ref = refs[4 * k:4 * k + 4]
            first_out = 4 * n + len(after)
            outs = refs[first_out + 4 * k:first_out + 4 * k + 4]
            at = (0, 0) if w.ndim == 5 else (0,)

            def update(l_ref=l_ref, w_ref=w_ref, m_ref=m_ref, v_ref=v_ref, outs=outs, at=at):
                g = l_ref[(0,) + at[1:]].astype(F32)
                for i in range(1, N_DEV):
                    g = g + l_ref[(i,) + at[1:]].astype(F32)
                dlt, m2, v2 = _adam(w_ref[at], g, m_ref[at], v_ref[at])
                for o_ref, val in zip(outs, (g, dlt, m2, v2)):
                    o_ref[at] = val

            if w.ndim == 5:
                update()
            else:
                pl.when(pl.program_id(0) == 0)(update)

    res = _pallas(
        body, name=name, grid=(2,),
        in_specs=in_specs + [pl.BlockSpec(memory_space=pl.ANY)] * len(after),
        out_specs=out_specs, out_shape=out_shape,
        compiler_params=_params(("arbitrary",)),
    )(*operands, *after)
    return [list(res[4 * k:4 * k + 4]) for k in range(n)]


SMALL_PARAMS = ("ln_g", "ln_b", "ev_sg_ln_g", "ev_sg_ln_b", "ev_sink", "ev_sg_b",
                "od_conv_w", "od_conv_b", "od_b_a", "od_b_x", "od_lam")


def _small_update(ga, gc, gd, gf, gb, ge, gsink, gbt, params):
    names = list(SMALL_PARAMS)
    flat = [a for nm in names for a in params[nm]]
    n_g = 8

    def body(*refs):
        ga_ref, gc_ref, gd_ref, gf_ref, gb_ref, ge_ref, gs_ref, gbt_ref = refs[:n_g]
        prm = refs[n_g:n_g + 3 * len(names)]
        loss_ref = refs[n_g + 3 * len(names)]
        outs = refs[n_g + 3 * len(names) + 1:]

        def ssum(ref):
            acc = ref[0]
            for i in range(1, N_DEV):
                acc = acc + ref[i]
            return acc

        a, cc, dd, ff, bb, ee = ssum(ga_ref), ssum(gc_ref), ssum(gd_ref), ssum(gf_ref), ssum(gb_ref), ssum(ge_ref)
        loss_ref[...] = a[3:4, 0:LANE]
        me = _slot(*_my_pos())

        def mine(rows):
            acc = jnp.zeros((rows.shape[0], LANE), F32)
            for j in range(N_DEV):
                acc = acc + jnp.where(me == j, rows[:, j * LANE:(j + 1) * LANE], 0.0)
            return acc

        sink_terms = ssum(gs_ref)
        lane8 = lax.broadcasted_iota(jnp.int32, (1, N_HEADS), 1)
        g_sink = jnp.zeros((1, N_HEADS), F32)
        for h in range(N_HEADS):
            tot = -jnp.sum(sink_terms[:, h * LANE:(h + 1) * LANE], axis=1, keepdims=True)
            g_sink = jnp.where(lane8 == h, tot, g_sink)
        grads = dict(
            ln_g=jnp.concatenate([dd[0:1], a[0:1]], axis=0), ln_b=jnp.concatenate([dd[1:2], a[1:2]], axis=0),
            ev_sg_ln_g=ee[0:1], ev_sg_ln_b=ee[1:2], ev_sink=g_sink,
            ev_sg_b=jnp.transpose(ssum(gbt_ref))[0:SG_GROUPS, :],
            od_conv_w=mine(cc[0:4]), od_conv_b=mine(cc[4:5]),
            od_b_a=mine(jnp.concatenate([ff[0:1], bb[0:1]], axis=0)),
            od_b_x=mine(jnp.concatenate([ff[1:2], bb[1:2]], axis=0)),
            od_lam=mine(jnp.concatenate([ff[2:3], bb[2:3]], axis=0)))
        for k, nm in enumerate(names):
            w_ref, m_ref, v_ref = prm[3 * k:3 * k + 3]
            at = (0,) if len(w_ref.shape) == 3 else ()
            g = grads[nm]
            dlt, m2, v2 = _adam(w_ref[at] if at else w_ref[...], g, m_ref[at] if at else m_ref[...],
                                v_ref[at] if at else v_ref[...])
            for o_ref, val in zip(outs[4 * k:4 * k + 4], (g, dlt, m2, v2)):
                if at:
                    o_ref[at] = val
                else:
                    o_ref[...] = val

    gathered = [ga, gc, gd, gf, gb, ge, gsink, gbt]
    out_shape = [jax.ShapeDtypeStruct((1, LANE), F32)]
    for nm in names:
        out_shape += [jax.ShapeDtypeStruct(params[nm][0].shape, F32)] * 4
    return _pallas(
        body, name="small_update", grid=(1,),
        in_specs=[_full(a.shape) for a in gathered + flat],
        out_specs=[_full(s.shape) for s in out_shape], out_shape=out_shape,
        compiler_params=_params(("arbitrary",)),
    )(*gathered, *flat)


VEC_ROWS = 16
VEC_LAYOUT = (("od_conv_w", 4), ("od_conv_b", 1), ("od_b_a", 2), ("od_b_x", 2), ("od_lam", 2))


def _to_slabs(full, cols_per):
    R = full.shape[0]
    return full.reshape(R, N_DEV, cols_per).transpose(1, 0, 2)


def _from_slabs(slabs):
    n, R, cp = slabs.shape
    return slabs.transpose(1, 0, 2).reshape(R, n * cp)


def kernel(x, c, positions, ada_w, ada_b, ln_g, ln_b, ev_w_in, ev_w_out, ev_sink, ev_sg_ln_g, ev_sg_ln_b, ev_sg_w, ev_sg_b, od_w_in, od_conv_w, od_conv_b, od_w_a, od_b_a, od_w_x, od_b_x, od_lam, od_w_out, loss_target, m_ada_w, m_ada_b, m_ln_g, m_ln_b, m_ev_w_in, m_ev_w_out, m_ev_sink, m_ev_sg_ln_g, m_ev_sg_ln_b, m_ev_sg_w, m_ev_sg_b, m_od_w_in, m_od_conv_w, m_od_conv_b, m_od_w_a, m_od_b_a, m_od_w_x, m_od_b_x, m_od_lam, m_od_w_out, v_ada_w, v_ada_b, v_ln_g, v_ln_b, v_ev_w_in, v_ev_w_out, v_ev_sink, v_ev_sg_ln_g, v_ev_sg_ln_b, v_ev_sg_w, v_ev_sg_b, v_od_w_in, v_od_conv_w, v_od_conv_b, v_od_w_a, v_od_b_a, v_od_w_x, v_od_b_x, v_od_lam, v_od_w_out):
    T = x.shape[1]
    me = _slot(*_my_pos())
    xs = x.reshape(T, D)
    tgt = loss_target.reshape(T, D)

    c_all, mod_all, g_vec, (g_ev_in,), (s_ev_out, s_od_in, s_od_out, sg_w, wa, wx) = _head_gather(
        c, ada_w, [ev_w_in[0].T.astype(BF16)],
        [ev_w_out[0], od_w_in[0], od_w_out[0], ev_sg_w[0], od_w_a[0], od_w_x[0]],
        [od_conv_w, od_conv_b, od_b_a, od_b_x, od_lam])
    c_all = c_all.reshape(N_DEV, D)
    w_ev_in = g_ev_in.reshape(EV_IN, D)
    vec_full = _from_slabs(g_vec)
    cw, cb = vec_full[0:4], vec_full[4:5]
    ba, bx, lam = vec_full[5:7], vec_full[7:9], vec_full[9:11]
    mod_mine = lax.dynamic_index_in_dim(mod_all, me, axis=2, keepdims=False)
    mod = mod_mine.transpose(1, 0, 2).reshape(2, 3 * D) + ada_b
    mod0 = mod[0].reshape(3, D)
    mod1 = mod[1].reshape(3, D)

    half = 8
    inv_freq = jnp.power(jnp.float32(ROPE_THETA), -jnp.arange(half, dtype=F32) / half)
    ang = positions.reshape(T).astype(F32)[:, None] * inv_freq
    cos_t = jnp.tile(jnp.cos(ang), (1, LANE // half))
    sin_t = jnp.tile(jnp.sin(ang), (1, LANE // half))
    l64 = jnp.arange(LANE) % HEAD_DIM
    rc = jnp.where(l64 < 2 * half, cos_t, 1.0)
    rs1 = jnp.where(l64 < half, -sin_t, 0.0)
    rs2 = jnp.where((l64 >= half) & (l64 < 2 * half), sin_t, 0.0)

    ln0 = jnp.stack([ln_g[0], ln_b[0]])
    ln1 = jnp.stack([ln_g[1], ln_b[1]])
    sg_lng = ev_sg_ln_g
    sg_lnb = ev_sg_ln_b
    sg_bfull = jnp.repeat(ev_sg_b[0].T, SG_DIM, axis=1)
    sink_l = jnp.repeat(ev_sink, LANE, axis=1)
    kj = jnp.arange(3 * BLK)[:, None]
    qi = jnp.arange(BLK)[None, :]
    band_bias = jnp.where(jnp.abs(kj - BLK - qi) <= BLK, 0.0, NEG_INF).astype(F32)
    lanes = jnp.arange(LANE)
    lanes2 = jnp.arange(2 * LANE)
    a128 = jnp.where(lanes2[:, None] // SG_DIM == lanes2[None, :] // SG_DIM, 1.0 / SG_DIM, 0.0).astype(BF16)
    gsum = (jnp.arange(SG_W)[:, None] // SG_DIM == lanes[None, :]).astype(BF16)
    sel = (jnp.arange(SUBLANE)[:, None] == lanes[None, :] // HEAD_DIM).astype(BF16)

    (q, kvx, su, sv, g0), _ = _ev_in(xs, mod0, w_ev_in, rc, rs1, rs2)
    (ycat, y0, lse), (g_ev_out, g_od_in, g_od_out) = _mix0_fwd(
        q, kvx, su, sv, g0, sink_l, band_bias, a128, sg_lng, sg_lnb, sg_w, sg_bfull,
        _GatherComm([s_ev_out, s_od_in, s_od_out], mid_frac=0.75))
    w_ev_out = g_ev_out.reshape(D, D)
    w_od_in = _from_slabs(g_od_in)
    w_od_out = g_od_out.reshape(D, D)
    out0, z0, x1 = _ev_out(y0, w_ev_out, xs, mod0, ln0)
    xr, g1 = _od_in(x1, mod1, w_od_in)
    fwd_f = _rglru_fwd(xr, cw, cb, wa[0], wx[0], ba[0:1], bx[0:1], lam[0:1], False, "rglru_fwd_f")
    fwd_b = _rglru_fwd(xr, cw, cb, wa[1], wx[1], ba[1:2], bx[1:2], lam[1:2], True, "rglru_fwd_b")
    dh, dg1, dx1p, d_od_out, vec_a = _od_out(fwd_f[0], fwd_b[0], g1, w_od_out, x1, tgt, mod1, ln1)

    (dxcf, dwa_f, dwx_f, vec_f), (l_od_out,) = _rglru_bwd(
        fwd_f, dh, wa[0], wx[0], lam[0:1], False, "rglru_bwd_f",
        _ExchangeComm([d_od_out.reshape(N_DEV, D // N_DEV, D)]))
    (dxcb, dwa_b, dwx_b, vec_b), _ = _rglru_bwd(fwd_b, dh, wa[1], wx[1], lam[1:2], True, "rglru_bwd_b")
    (dx1, d_od_in, vec_c), (a_wa, a_wx) = _od_in_bwd(
        dxcf, dxcb, xr, dg1, x1, dx1p, mod1, w_od_in, cw,
        _GatherComm([jnp.stack([dwa_f, dwa_b]).astype(BF16), jnp.stack([dwx_f, dwx_b]).astype(BF16)],
                    mid_frac=0.75))
    dxp, dyc, dg0, d_ev_out, vec_d = _ev_out_bwd(dx1, z0, out0, y0, ycat, g0, w_ev_out, mod0, ln0)
    (dq, dkv, dsu, dsv, d_sg_w, d_sg_bt, vec_e, d_sink_l), (l_od_in, l_ev_out, ga, gc, gd, gf, gb) = _mix0_bwd(
        q, kvx, lse, dyc, ycat, su, sv, sink_l, band_bias, a128, gsum, sel, sg_lng, sg_lnb, sg_w, sg_bfull,
        rc, rs1, rs2, _BothComm(_ExchangeComm([d_od_in, d_ev_out.reshape(N_DEV, D // N_DEV, D)]),
                                _GatherComm([vec_a, vec_c, vec_d, vec_f, vec_b])))
    (grad_x, d_ev_in, vec_g), _ = _ev_in_bwd(dq, dkv, dsu, dsv, dg0, xs, dxp, mod0, w_ev_in, rc, rs1, rs2)

    part, land, (gg, ge, gsink, gbt, a_sgw) = _tail_stage1(
        d_ev_in.reshape(N_DEV, EV_IN // N_DEV, D), [vec_g, vec_e, d_sink_l, d_sg_bt, d_sg_w.astype(BF16)])
    send_sems, recv_sems, part, land, token = _tail_send(part, land)

    dmod_all = jnp.stack([jnp.concatenate([gg[:, 0], gg[:, 1], gd[:, 2]], axis=-1),
                          jnp.concatenate([gc[:, 5], gc[:, 6], ga[:, 2]], axis=-1)], axis=1)
    cols = ada_w.shape[2]
    dmod_cols = lax.dynamic_slice_in_dim(dmod_all, me * cols, cols, axis=2).transpose(1, 0, 2)
    (g_ada_w, d_ada_w, nm_ada_w, nv_ada_w, g_ada_b, d_ada_b, nm_ada_b, nv_ada_b) = _ada_update(
        c_all, dmod_cols, dmod_all, ada_w, m_ada_w, v_ada_w, ada_b, m_ada_b, v_ada_b)

    res = dict(ada_w=[g_ada_w, d_ada_w, nm_ada_w, nv_ada_w], ada_b=[g_ada_b, d_ada_b, nm_ada_b, nv_ada_b])
    (r_od_in,) = _reduce_adam([(l_od_in, od_w_in[0], m_od_w_in[0], v_od_w_in[0])], "adam_od_w_in", after=[token])
    r_ev_out, r_od_out = _reduce_adam([(l_ev_out, ev_w_out[0], m_ev_w_out[0], v_ev_w_out[0]),
                                       (l_od_out, od_w_out[0], m_od_w_out[0], v_od_w_out[0])], "adam_w_out",
                                      after=[token])
    for name, r in (("od_w_in", r_od_in), ("ev_w_out", r_ev_out), ("od_w_out", r_od_out)):
        res[name] = [a[None] for a in r]
    res["od_w_a"], res["od_w_x"], res["ev_sg_w"] = _slots_adam(
        [(a_wa, od_w_a, m_od_w_a, v_od_w_a), (a_wx, od_w_x, m_od_w_x, v_od_w_x),
         (a_sgw, ev_sg_w, m_ev_sg_w, v_ev_sg_w)], "adam_gates", after=[token])
    small = dict(ln_g=(ln_g, m_ln_g, v_ln_g), ln_b=(ln_b, m_ln_b, v_ln_b),
                 ev_sg_ln_g=(ev_sg_ln_g, m_ev_sg_ln_g, v_ev_sg_ln_g),
                 ev_sg_ln_b=(ev_sg_ln_b, m_ev_sg_ln_b, v_ev_sg_ln_b),
                 ev_sink=(ev_sink, m_ev_sink, v_ev_sink), ev_sg_b=(ev_sg_b, m_ev_sg_b, v_ev_sg_b),
                 od_conv_w=(od_conv_w, m_od_conv_w, v_od_conv_w), od_conv_b=(od_conv_b, m_od_conv_b, v_od_conv_b),
                 od_b_a=(od_b_a, m_od_b_a, v_od_b_a), od_b_x=(od_b_x, m_od_b_x, v_od_b_x),
                 od_lam=(od_lam, m_od_lam, v_od_lam))
    small_out = _small_update(ga, gc, gd, gf, gb, ge, gsink, gbt, small)
    l_ev_in = _tail_wait(send_sems, recv_sems, part, land,
                         [r_od_in[0], r_od_out[0], res["od_w_x"][0], g_ada_w, small_out[0]])
    (r_ev_in,) = _reduce_adam([(l_ev_in, ev_w_in[0].T, m_ev_w_in[0].T, v_ev_w_in[0].T)], "adam_ev_w_in")
    res["ev_w_in"] = [a.T[None] for a in r_ev_in]
    loss = small_out[0][0, 0]
    for k, name in enumerate(SMALL_PARAMS):
        res[name] = small_out[1 + 4 * k:5 + 4 * k]

    order = ["ada_w", "ada_b", "ln_g", "ln_b", "ev_w_in", "ev_w_out", "ev_sink", "ev_sg_ln_g", "ev_sg_ln_b",
             "ev_sg_w", "ev_sg_b", "od_w_in", "od_conv_w", "od_conv_b", "od_w_a", "od_b_a", "od_w_x", "od_b_x",
             "od_lam", "od_w_out"]
    outs = [loss, grad_x.reshape(1, T, D)]
    for kind in range(4):
        outs += [res[name][kind] for name in order]
    return tuple(outs)
```

```python
import functools

import jax
import jax.numpy as jnp
from jax import lax
from jax.experimental import pallas as pl
from jax.experimental.pallas import tpu as pltpu

F32 = jnp.float32
BF16 = jnp.bfloat16

N_DEV = 8
D = 1024
N_HEADS = 8
HEAD_DIM = 64
KV_WIDTH = 128
ATTN_W = 512
SG_W = 512
SG_GROUPS = 8
SG_DIM = 64
BLK = 128
KVX_W = 1024
EV_IN = 2816
OD_IN = 2048
RNN_HEADS = 8
RNN_HD = 128
ALPHA = 4.0 ** 0.25
LN_EPS = 1e-5
NEG_INF = -1e30
RG_C = 8.0
ROPE_THETA = 500000.0
LR, B1, B2, EPS, WD, STEP = 0.001, 0.9, 0.999, 1e-08, 0.01, 10

LANE = 128
SUBLANE = 8
TM = 256
TMF = 512
TMO = 512
TS = 256
VMEM_LIMIT = 56 * 1024 * 1024

MESH = pl.DeviceIdType.MESH


def _pallas(body, **kw):
    return pl.pallas_call(body, **kw)


def _params(sem, vmem=VMEM_LIMIT):
    return pltpu.CompilerParams(dimension_semantics=sem, vmem_limit_bytes=vmem)


def _sigmoid(x):
    return 0.5 * jnp.tanh(0.5 * x) + 0.5


def _silu_and_grad(x):
    s = _sigmoid(x)
    return x * s, s * (1.0 + x * (1.0 - s))


def _dot(a, b):
    return jnp.dot(a.astype(BF16), b.astype(BF16), preferred_element_type=F32)


def _dot_nt(a, b):
    return lax.dot_general(a.astype(BF16), b.astype(BF16), (((1,), (1,)), ((), ())), preferred_element_type=F32)


def _dot_tn(a, b):
    return lax.dot_general(a.astype(BF16), b.astype(BF16), (((0,), (0,)), ((), ())), preferred_element_type=F32)


def _ln_fwd(z, g, b):
    mu = jnp.mean(z, axis=-1, keepdims=True)
    zc = z - mu
    var = jnp.mean(zc * zc, axis=-1, keepdims=True)
    rstd = lax.rsqrt(var + LN_EPS)
    xhat = zc * rstd
    return xhat * g + b, xhat, rstd


def _ln_bwd(dy, xhat, rstd, g):
    dxh = dy * g
    m1 = jnp.mean(dxh, axis=-1, keepdims=True)
    m2 = jnp.mean(dxh * xhat, axis=-1, keepdims=True)
    return rstd * (dxh - m1 - xhat * m2)


def _rowsum(v):
    return jnp.sum(v, axis=0, keepdims=True)


def _rope_fwd(t, c, s1, s2):
    return t * c + pltpu.roll(t, LANE - 8, 1) * s1 + pltpu.roll(t, 8, 1) * s2


def _rope_bwd(d, c, s1, s2):
    return d * c + pltpu.roll(d * s1, 8, 1) + pltpu.roll(d * s2, LANE - 8, 1)


def _adam(w, g, m, v):
    m2 = B1 * m + (1.0 - B1) * g
    v2 = B2 * v + (1.0 - B2) * (g * g)
    m_hat = m2 / (1.0 - B1 ** STEP)
    v_hat = v2 / (1.0 - B2 ** STEP)
    delta = -LR * (m_hat / (jnp.sqrt(v_hat) + EPS) + WD * w)
    return delta, m2, v2


def _tile(rows, width):
    return pl.BlockSpec((rows, width), lambda i: (i, 0))


def _full(shape):
    zeros = (0,) * len(shape)
    return pl.BlockSpec(shape, lambda i: zeros)


def _rev_tile(rows, width, n, reverse):
    if reverse:
        return pl.BlockSpec((rows, width), lambda i: (n - 1 - i, 0))
    return pl.BlockSpec((rows, width), lambda i: (i, 0))


def _halo_specs(rows, width, n, total_rows, reverse):
    per = rows // SUBLANE
    last = total_rows // SUBLANE - 1

    def tile_of(i):
        return (n - 1 - i) if reverse else i

    prev = pl.BlockSpec((SUBLANE, width), lambda i: (jnp.maximum(tile_of(i) * per - 1, 0), 0))
    nxt = pl.BlockSpec((SUBLANE, width), lambda i: (jnp.minimum((tile_of(i) + 1) * per, last), 0))
    return prev, nxt


def _my_pos():
    return lax.axis_index("x"), lax.axis_index("y"), lax.axis_index("c")


def _slot(px, py, pc):
    return 4 * px + 2 * py + pc


class _GatherComm:
    has_mid = True

    def __init__(self, arrs, mid_frac=0.5):
        self.arrs = list(arrs)
        self.n = len(self.arrs)
        self.mid_frac = mid_frac

    def out_shapes(self):
        return [jax.ShapeDtypeStruct((N_DEV,) + a.shape, a.dtype) for a in self.arrs]

    def sems(self):
        return [pltpu.SemaphoreType.DMA((7 * self.n,)), pltpu.SemaphoreType.DMA((7 * self.n,)),
                pltpu.SemaphoreType.DMA((self.n,))]

    def _parts(self, ins, outs, sems):
        send_sems, recv_sems, local_sems = sems
        x, y, c = _my_pos()
        me, sibling = (x, y, c), (x, y, 1 - c)
        chips = [(1 - x, y), (x, 1 - y), (1 - x, 1 - y)]

        def copy(a, k, block, to, src=None):
            dst = outs[a].at[_slot(*block)]
            return pltpu.make_async_remote_copy(
                src_ref=dst if src is None else src, dst_ref=dst,
                send_sem=send_sems.at[a * 7 + k], recv_sem=recv_sems.at[a * 7 + k],
                device_id=to, device_id_type=MESH)

        local = [pltpu.make_async_copy(ins[a], outs[a].at[_slot(*me)], local_sems.at[a]) for a in range(self.n)]
        first = []
        for a in range(self.n):
            first.append(copy(a, 0, me, sibling, src=ins[a]))
            first += [copy(a, 1 + j, me, (*chip, c), src=ins[a]) for j, chip in enumerate(chips)]
        ici_in = [copy(a, 1 + j, (*chip, c), me) for j, chip in enumerate(chips) for a in range(self.n)]
        passed = [copy(a, 4 + j, (*chip, c), sibling) for j, chip in enumerate(chips) for a in range(self.n)]
        d2d_in = []
        for a in range(self.n):
            d2d_in.append(copy(a, 0, sibling, me))
            d2d_in += [copy(a, 4 + j, (*chip, 1 - c), me) for j, chip in enumerate(chips)]
        return local, first, ici_in, passed, d2d_in

    def start(self, ins, outs, sems):
        local, first, _, _, _ = self._parts(ins, outs, sems)
        for cp in local + first:
            cp.start()

    def mid(self, ins, outs, sems):
        _, _, ici_in, passed, _ = self._parts(ins, outs, sems)
        for arrived, fw in zip(ici_in, passed):
            arrived.wait_recv()
            fw.start()

    def finish(self, ins, outs, sems):
        local, first, _, passed, d2d_in = self._parts(ins, outs, sems)
        for cp in d2d_in:
            cp.wait_recv()
        for cp in first + passed:
            cp.wait_send()
        for cp in local:
            cp.wait()


class _ExchangeComm:
    has_mid = False

    def __init__(self, arrs):
        self.arrs = list(arrs)
        self.n = len(self.arrs)

    def out_shapes(self):
        return [jax.ShapeDtypeStruct(a.shape, a.dtype) for a in self.arrs]

    def sems(self):
        return [pltpu.SemaphoreType.DMA((7 * self.n,)), pltpu.SemaphoreType.DMA((7 * self.n,)),
                pltpu.SemaphoreType.DMA((self.n,))]

    def _copies(self, ins, outs, sems):
        send_sems, recv_sems, local_sems = sems
        x, y, c = _my_pos()
        mine = _slot(x, y, c)
        copies = [pltpu.make_async_copy(ins[a].at[mine], outs[a].at[mine], local_sems.at[a]) for a in range(self.n)]
        for k in range(1, N_DEV):
            px = (1 - x) if (k & 4) else x
            py = (1 - y) if (k & 2) else y
            pc = (1 - c) if (k & 1) else c
            for a in range(self.n):
                copies.append(pltpu.make_async_remote_copy(
                    src_ref=ins[a].at[_slot(px, py, pc)], dst_ref=outs[a].at[mine],
                    send_sem=send_sems.at[a * 7 + k - 1], recv_sem=recv_sems.at[a * 7 + k - 1],
                    device_id=(px, py, pc), device_id_type=MESH))
        return copies

    def start(self, ins, outs, sems):
        for cp in self._copies(ins, outs, sems):
            cp.start()

    def finish(self, ins, outs, sems):
        for cp in self._copies(ins, outs, sems):
            cp.wait()


class _BothComm:
    has_mid = True

    def __init__(self, first, second):
        self.parts = (first, second)
        self.arrs = first.arrs + second.arrs
        self.n = first.n + second.n
        self.mid_frac = second.mid_frac

    def out_shapes(self):
        return self.parts[0].out_shapes() + self.parts[1].out_shapes()

    def sems(self):
        return self.parts[0].sems() + self.parts[1].sems()

    def _each(self, ins, outs, sems):
        a, b = self.parts
        return ((a, ins[:a.n], outs[:a.n], sems[:3]), (b, ins[a.n:], outs[a.n:], sems[3:]))

    def start(self, ins, outs, sems):
        for cm, i_, o_, s_ in self._each(ins, outs, sems):
            cm.start(i_, o_, s_)

    def mid(self, ins, outs, sems):
        for cm, i_, o_, s_ in self._each(ins, outs, sems):
            if cm.has_mid:
                cm.mid(i_, o_, s_)

    def finish(self, ins, outs, sems):
        for cm, i_, o_, s_ in self._each(ins, outs, sems):
            cm.finish(i_, o_, s_)


def _fused_call(body, comm, operands, *, name, grid, in_specs, out_specs, out_shape, scratch_shapes=(),
                semantics=("arbitrary",)):
    n_in, n_out, n_scr = len(in_specs), len(out_specs), len(scratch_shapes)
    if comm is None:
        res = _pallas(body, name=name, grid=grid, in_specs=list(in_specs), out_specs=list(out_specs),
                      out_shape=list(out_shape), scratch_shapes=list(scratch_shapes),
                      compiler_params=_params(semantics))(*operands)
        return list(res), []
    k = comm.n
    steps = grid[0]

    def wrapped(*refs):
        ins, cins = refs[:n_in], refs[n_in:n_in + k]
        outs = refs[n_in + k:n_in + k + n_out]
        couts = refs[n_in + k + n_out:n_in + 2 * k + n_out]
        rest = refs[n_in + 2 * k + n_out:]
        scratch, sems = rest[:n_scr], rest[n_scr:]
        i = pl.program_id(0)

        @pl.when(i == 0)
        def _():
            comm.start(cins, couts, sems)

        body(*ins, *outs, *scratch)

        if comm.has_mid:
            @pl.when(i == int(steps * comm.mid_frac))
            def _():
                comm.mid(cins, couts, sems)

        @pl.when(i == steps - 1)
        def _():
            comm.finish(cins, couts, sems)

    any_spec = pl.BlockSpec(memory_space=pl.ANY)
    res = _pallas(wrapped, name=name, grid=grid, in_specs=list(in_specs) + [any_spec] * k,
                  out_specs=list(out_specs) + [any_spec] * k, out_shape=list(out_shape) + comm.out_shapes(),
                  scratch_shapes=list(scratch_shapes) + comm.sems(),
                  compiler_params=_params(("arbitrary",)))(*operands, *comm.arrs)
    return list(res[:n_out]), list(res[n_out:])


def _head_gather(c, ada_w, big, to_cast, vec_parts):
    cols = ada_w.shape[2]
    g_c, g_big = _GatherComm([c]), _GatherComm(big)
    g_mod = _GatherComm([jax.ShapeDtypeStruct((2, N_DEV, cols), F32)])
    g_vec = _GatherComm([jax.ShapeDtypeStruct((VEC_ROWS, LANE), F32)])
    nb, nc, nv = g_big.n, len(to_cast), len(vec_parts)

    def body(*refs):
        c_ref, w_ref = refs[0], refs[1]
        vec_in = refs[2:2 + nv]
        cast_in = refs[2 + nv:2 + nv + nc]
        big_in = refs[2 + nv + nc:2 + nv + nc + nb]
        outs = refs[2 + nv + nc + nb:]
        c_all_ref, mod_all_ref, vec_all_ref = outs[0], outs[1], outs[2]
        cast_out = outs[3:3 + nc]
        big_out = outs[3 + nc:3 + nc + nb]
        part_ref, pack_ref = outs[3 + nc + nb], outs[4 + nc + nb]
        sems = outs[5 + nc + nb:]
        s_c, s_mod, s_big, s_vec = sems[0:3], sems[3:6], sems[6:9], sems[9:12]
        g_c.start([c_ref], [c_all_ref], s_c)
        g_big.start(big_in, big_out, s_big)
        pack_ref[...] = jnp.zeros_like(pack_ref)
        row = 0
        for ref, (_, nrows) in zip(vec_in, VEC_LAYOUT):
            pack_ref[row:row + nrows, :] = ref[0] if len(ref.shape) == 3 else ref[...]
            row += nrows
        g_vec.start([pack_ref], [vec_all_ref], s_vec)
        g_c.mid([c_ref], [c_all_ref], s_c)
        g_c.finish([c_ref], [c_all_ref], s_c)
        cv = c_all_ref[:, 0, :]
        cond = cv * _sigmoid(cv)
        for l in range(2):
            part_ref[l] = _dot(cond, w_ref[l])
        g_mod.start([part_ref], [mod_all_ref], s_mod)
        for src, dst in zip(cast_in, cast_out):
            dst[...] = src[...].astype(BF16)
        for g, ins, outs_, sm in ((g_vec, [pack_ref], [vec_all_ref], s_vec), (g_mod, [part_ref], [mod_all_ref], s_mod),
                                  (g_big, big_in, big_out, s_big)):
            g.mid(ins, outs_, sm)
            g.finish(ins, outs_, sm)

    any_spec = pl.BlockSpec(memory_space=pl.ANY)
    vmem_spec = pl.BlockSpec(memory_space=pltpu.VMEM)
    res = _pallas(
        body, name="head_gather",
        out_shape=(g_c.out_shapes() + g_mod.out_shapes() + g_vec.out_shapes()
                   + [jax.ShapeDtypeStruct(a.shape, BF16) for a in to_cast] + g_big.out_shapes()),
        in_specs=[vmem_spec] * (2 + nv + nc) + [any_spec] * nb,
        out_specs=[vmem_spec] * (3 + nc) + [any_spec] * nb,
        scratch_shapes=[pltpu.VMEM((2, N_DEV, cols), F32), pltpu.VMEM((VEC_ROWS, LANE), F32)]
        + g_c.sems() + g_mod.sems() + g_big.sems() + g_vec.sems(),
        compiler_params=pltpu.CompilerParams(vmem_limit_bytes=VMEM_LIMIT),
    )(c, ada_w, *vec_parts, *to_cast, *big)
    return res[0], res[1], res[2], list(res[3 + nc:]), list(res[3:3 + nc])


def _ada_update(c_all, dmod_cols, dmod_all, ada_w, m_w, v_w, ada_b, m_b, v_b):
    cols = ada_w.shape[2]
    nb = ada_b.shape[1]

    def body(c_ref, dmc_ref, dma_ref, w_ref, mw_ref, vw_ref, b_ref, mb_ref, vb_ref,
             gw_ref, dw_ref, nmw_ref, nvw_ref, gb_ref, db_ref, nmb_ref, nvb_ref):
        cv = c_ref[...]
        cond = cv * _sigmoid(cv)
        for l in range(2):
            g = _dot_tn(cond, dmc_ref[l])
            gw_ref[l] = g
            dlt, m2, v2 = _adam(w_ref[l], g, mw_ref[l], vw_ref[l])
            dw_ref[l] = dlt
            nmw_ref[l] = m2
            nvw_ref[l] = v2
        gb = dma_ref[0]
        for i in range(1, N_DEV):
            gb = gb + dma_ref[i]
        gb_ref[...] = gb
        dlt, m2, v2 = _adam(b_ref[...], gb, mb_ref[...], vb_ref[...])
        db_ref[...] = dlt
        nmb_ref[...] = m2
        nvb_ref[...] = v2

    wspec = _full((2, D, cols))
    bspec = _full((2, nb))
    wshape = jax.ShapeDtypeStruct((2, D, cols), F32)
    bshape = jax.ShapeDtypeStruct((2, nb), F32)
    return _pallas(
        body, name="ada_update", grid=(1,),
        in_specs=[_full((N_DEV, D)), _full((2, N_DEV, cols)), _full((N_DEV, 2, nb)),
                  wspec, wspec, wspec, bspec, bspec, bspec],
        out_specs=[wspec] * 4 + [bspec] * 4,
        out_shape=[wshape] * 4 + [bshape] * 4,
        compiler_params=_params(("arbitrary",)),
    )(c_all, dmod_cols, dmod_all, ada_w, m_w, v_w, ada_b, m_b, v_b)


def _ev_in(x, mod, w_in, rc, rs1, rs2, comm=None):
    T = x.shape[0]

    def body(x_ref, mod_ref, w_ref, c_ref, s1_ref, s2_ref, q_ref, kv_ref, su_ref, sv_ref, g_ref):
        h = x_ref[...] * (1.0 + mod_ref[1:2, :]) + mod_ref[0:1, :]
        p = _dot_nt(h, w_ref[...])
        c, s1, s2 = c_ref[...], s1_ref[...], s2_ref[...]
        for j in range(ATTN_W // LANE):
            qr = _rope_fwd(p[:, j * LANE:(j + 1) * LANE], c, s1, s2)
            q_ref[:, j * LANE:(j + 1) * LANE] = (qr * (HEAD_DIM ** -0.5)).astype(BF16)
        low = lax.broadcasted_iota(jnp.int32, (TMF, LANE), 1) < HEAD_DIM
        for j, val in enumerate((_rope_fwd(p[:, 512:640], c, s1, s2), p[:, 640:768])):
            swapped = pltpu.roll(val, HEAD_DIM, 1)
            tiles = (jnp.where(low, val, 0.0), jnp.where(low, 0.0, swapped),
                     jnp.where(low, swapped, 0.0), jnp.where(low, 0.0, val))
            for k, tile in enumerate(tiles):
                kv_ref[:, (4 * j + k) * LANE:(4 * j + k + 1) * LANE] = tile.astype(BF16)
        su_ref[...] = p[:, 768:1280].astype(BF16)
        sv_ref[...] = p[:, 1280:1792].astype(BF16)
        g_ref[...] = p[:, 1792:2816].astype(BF16)

    sh = lambda w: jax.ShapeDtypeStruct((T, w), BF16)
    return _fused_call(
        body, comm, (x, mod, w_in, rc, rs1, rs2), name="ev_in", grid=(T // TMF,),
        in_specs=[_tile(TMF, D), _full((3, D)), _full((EV_IN, D)), _tile(TMF, LANE), _tile(TMF, LANE),
                  _tile(TMF, LANE)],
        out_specs=[_tile(TMF, ATTN_W), _tile(TMF, KVX_W), _tile(TMF, SG_W), _tile(TMF, SG_W), _tile(TMF, D)],
        out_shape=[sh(ATTN_W), sh(KVX_W), sh(SG_W), sh(SG_W), sh(D)], semantics=("parallel",))


def _band_specs(width, nb):
    return [pl.BlockSpec((BLK, width), lambda n: (jnp.maximum(n - 1, 0), 0)),
            pl.BlockSpec((BLK, width), lambda n: (n, 0)),
            pl.BlockSpec((BLK, width), lambda n: (jnp.minimum(n + 1, nb - 1), 0))]


def _band_bias(bias_ref, n, nb):
    rows = lax.broadcasted_iota(jnp.int32, (3 * BLK, 1), 0)
    outside = ((rows < BLK) & (n == 0)) | ((rows >= 2 * BLK) & (n == nb - 1))
    return bias_ref[...] + jnp.where(outside, NEG_INF, 0.0)


def _lane_tile(ref, t):
    return ref[:, t * LANE:(t + 1) * LANE]


def _split_bf16(v):
    hi = v.astype(BF16)
    return hi, (v - hi.astype(F32)).astype(BF16)


def _group_mean(v, a_ref, exact_bf16=False):
    hi, lo = _split_bf16(v)
    a = a_ref[...]
    out = []
    for t in range(SG_W // (2 * LANE)):
        sl = slice(t * 2 * LANE, (t + 1) * 2 * LANE)
        r = jnp.dot(hi[:, sl], a, preferred_element_type=F32)
        if not exact_bf16:
            r = r + jnp.dot(lo[:, sl], a, preferred_element_type=F32)
        out.append(r)
    return jnp.concatenate(out, axis=-1)


def _sg_core(sv_ref, lng, lnb, a_ref, w_ref, bfull_ref):
    svf = sv_ref[...].astype(F32)
    xc = svf - _group_mean(svf, a_ref, exact_bf16=True)
    rstd = lax.rsqrt(_group_mean(xc * xc, a_ref) + LN_EPS)
    xhat = xc * rstd
    vb = (xhat * lng + lnb).astype(BF16)
    low = lax.broadcasted_iota(jnp.int32, (BLK, LANE), 1) < SG_DIM
    tiles = []
    for t in range(SG_W // LANE):
        v2 = vb[:, t * LANE:(t + 1) * LANE]
        r0 = jnp.dot(w_ref[2 * t], v2, preferred_element_type=F32)
        r1 = jnp.dot(w_ref[2 * t + 1], v2, preferred_element_type=F32)
        tiles.append(jnp.where(low, r0, r1))
    svm = jnp.concatenate(tiles, axis=-1) + bfull_ref[...]
    return xhat, rstd, vb, svm


def _mix0_fwd(q, kvx, su, sv, g0, sink_l, bias, a128, sg_lng, sg_lnb, sg_w, sg_bfull, comm=None):
    T = q.shape[0]
    nb = T // BLK

    def body(q_ref, kp_ref, kc_ref, kn_ref, su_ref, sv_ref, g_ref, sink_ref, bias_ref, a_ref, lng_ref, lnb_ref,
             w_ref, bfull_ref, ycat_ref, y0_ref, lse_ref):
        n = pl.program_id(0)
        bias = _band_bias(bias_ref, n, nb)
        kvx = jnp.concatenate([kp_ref[...], kc_ref[...], kn_ref[...]], axis=0)
        tiles = []
        for t in range(ATTN_W // LANE):
            qt = _lane_tile(q_ref, t)
            acc = None
            for par in range(2):
                h = 2 * t + par
                kt = 2 * (h // 4) + par
                ke = kvx[:, kt * LANE:(kt + 1) * LANE]
                ve = kvx[:, (4 + kt) * LANE:(5 + kt) * LANE]
                st = _dot_nt(ke, qt) + bias
                sk = _lane_tile(sink_ref, h)
                m = jnp.maximum(jnp.max(st, axis=0, keepdims=True), sk)
                p = jnp.exp(st - m)
                denom = jnp.sum(p, axis=0, keepdims=True) + jnp.exp(sk - m)
                contrib = _dot_tn(p * (1.0 / denom), ve)
                acc = contrib if acc is None else acc + contrib
                lse_ref[0, :, h * LANE:(h + 1) * LANE] = m + jnp.log(denom)
            tiles.append(acc)
        _, _, _, svm = _sg_core(sv_ref, lng_ref[...], lnb_ref[...], a_ref, w_ref, bfull_ref)
        tiles.append(su_ref[...].astype(F32) * svm)
        ycat = jnp.concatenate(tiles, axis=-1)
        gf = g_ref[...].astype(F32)
        ycat_ref[...] = ycat.astype(BF16)
        y0_ref[...] = (ycat * (gf * _sigmoid(gf))).astype(BF16)

    return _fused_call(
        body, comm, (q, kvx, kvx, kvx, su, sv, g0, sink_l, bias, a128, sg_lng, sg_lnb, sg_w, sg_bfull),
        name="mix0_fwd", grid=(nb,),
        in_specs=[_tile(BLK, ATTN_W)] + _band_specs(KVX_W, nb) + [
            _tile(BLK, SG_W), _tile(BLK, SG_W), _tile(BLK, D), _full((1, N_HEADS * LANE)), _full((3 * BLK, LANE)),
            _full((2 * LANE, 2 * LANE)),_full((1, SG_W)), _full((1, SG_W)), _full((SG_GROUPS, BLK, BLK)),
            _full((BLK, SG_W))],
        out_specs=[_tile(BLK, D), _tile(BLK, D), pl.BlockSpec((1, 1, N_HEADS * LANE), lambda n: (n, 0, 0))],
        out_shape=[jax.ShapeDtypeStruct((T, D), BF16), jax.ShapeDtypeStruct((T, D), BF16),
                   jax.ShapeDtypeStruct((nb, 1, N_HEADS * LANE), F32)], semantics=("parallel",))


def _ev_out(y0, w_out, x, mod, lnp):
    T = x.shape[0]

    def body(y_ref, w_ref, x_ref, mod_ref, ln_ref, out_ref, z_ref, x1_ref):
        out = _dot(y_ref[...], w_ref[...])
        z = ALPHA * x_ref[...] + mod_ref[2:3, :] * out
        x1, _, _ = _ln_fwd(z, ln_ref[0:1, :], ln_ref[1:2, :])
        out_ref[...] = out.astype(BF16)
        z_ref[...] = z
        x1_ref[...] = x1

    return _pallas(
        body, name="ev_out", grid=(T // TMF,),
        in_specs=[_tile(TMF, D), _full((D, D)), _tile(TMF, D), _full((3, D)), _full((2, D))],
        out_specs=[_tile(TMF, D)] * 3,
        out_shape=[jax.ShapeDtypeStruct((T, D), BF16), jax.ShapeDtypeStruct((T, D), F32),
                   jax.ShapeDtypeStruct((T, D), F32)],
        compiler_params=_params(("parallel",)),
    )(y0, w_out, x, mod, lnp)


def _od_in(x1, mod, w_in):
    T = x1.shape[0]

    def body(x_ref, mod_ref, w_ref, xr_ref, g_ref):
        h = x_ref[...] * (1.0 + mod_ref[1:2, :]) + mod_ref[0:1, :]
        p = _dot(h, w_ref[...])
        xr_ref[...] = p[:, :D]
        g_ref[...] = p[:, D:].astype(BF16)

    return _pallas(
        body, name="od_in", grid=(T // TMF,),
        in_specs=[_tile(TMF, D), _full((3, D)), _full((D, OD_IN))],
        out_specs=[_tile(TMF, D), _tile(TMF, D)],
        out_shape=[jax.ShapeDtypeStruct((T, D), F32), jax.ShapeDtypeStruct((T, D), BF16)],
        compiler_params=_params(("parallel",)),
    )(x1, mod, w_in)


def _ext_rows(prev_ref, cur, next_ref, j, n):
    prev = jnp.where(j > 0, prev_ref[...], 0.0)
    nxt = jnp.where(j < n - 1, next_ref[...], 0.0)
    return jnp.concatenate([prev, cur, nxt], axis=0)


def _shift_rows(ext, off, rows):
    total = ext.shape[0]
    if off == 0:
        return ext[SUBLANE:SUBLANE + rows, :]
    return pltpu.roll(ext, (-off) % total, 0)[SUBLANE:SUBLANE + rows, :]


def _conv_fwd(ext, cw, cb, rows):
    xc = cb
    for k in range(4):
        xc = xc + cw[k:k + 1, :] * _shift_rows(ext, k - 2, rows)
    return xc


def _gates(xc, wa_ref, wx_ref, ba, bx, lam):
    pr, pi = [], []
    for h in range(RNN_HEADS):
        xh = xc[:, h * RNN_HD:(h + 1) * RNN_HD].astype(BF16)
        pr.append(_dot(xh, wa_ref[h]))
        pi.append(_dot(xh, wx_ref[h]))
    r = _sigmoid(jnp.concatenate(pr, axis=-1) + ba)
    ig = _sigmoid(jnp.concatenate(pi, axis=-1) + bx)
    sp = jnp.maximum(-lam, 0.0) + jnp.log(1.0 + jnp.exp(-jnp.abs(lam)))
    neg_log_a = RG_C * r * sp
    a = jnp.exp(-neg_log_a)
    s2 = (1.0 + a * a) * jnp.tanh(neg_log_a)
    inv_s = lax.rsqrt(jnp.maximum(s2, 1e-30))
    return r, ig, sp, a, s2 * inv_s, inv_s


def _scan_tile(a_ref, b_ref, o_ref, carry_ref, rows, reverse):
    ridx = lax.broadcasted_iota(jnp.int32, (SUBLANE, D), 0)
    groups = rows // SUBLANE

    def group(gi, h):
        g = (groups - 1 - gi) if reverse else gi
        off = pl.multiple_of(g * SUBLANE, SUBLANE)
        a = a_ref[pl.ds(off, SUBLANE), :]
        b = b_ref[pl.ds(off, SUBLANE), :]
        for sh in (1, 2, 4):
            if reverse:
                keep = ridx < SUBLANE - sh
                a_p = jnp.where(keep, pltpu.roll(a, SUBLANE - sh, 0), 1.0)
                b_p = jnp.where(keep, pltpu.roll(b, SUBLANE - sh, 0), 0.0)
            else:
                keep = ridx >= sh
                a_p = jnp.where(keep, pltpu.roll(a, sh, 0), 1.0)
                b_p = jnp.where(keep, pltpu.roll(b, sh, 0), 0.0)
            b = b + a * b_p
            a = a * a_p
        hh = b + a * h
        o_ref[pl.ds(off, SUBLANE), :] = hh
        return hh[0:1, :] if reverse else hh[SUBLANE - 1:SUBLANE, :]

    carry_ref[...] = lax.fori_loop(0, groups, group, carry_ref[...])


def _rglru_fwd(xr, cw, cb, wa, wx, ba, bx, lam, reverse, name):
    T = xr.shape[0]
    n = T // TS
    prev_spec, next_spec = _halo_specs(TS, D, n, T, reverse)

    def body(prev_ref, cur_ref, next_ref, cw_ref, cb_ref, wa_ref, wx_ref, ba_ref, bx_ref, lam_ref,
             h_ref, a_ref, s_ref, r_ref, ig_ref, xc_ref, b_s, carry):
        i = pl.program_id(0)
        j = (n - 1 - i) if reverse else i

        @pl.when(i == 0)
        def _():
            carry[...] = jnp.zeros_like(carry)

        ext = _ext_rows(prev_ref, cur_ref[...], next_ref, j, n)
        xc = _conv_fwd(ext, cw_ref[...], cb_ref[...], TS)
        r, ig, _, a, s, _ = _gates(xc, wa_ref, wx_ref, ba_ref[...], bx_ref[...], lam_ref[...])
        s_ref[...] = s
        r_ref[...] = r.astype(BF16)
        ig_ref[...] = ig.astype(BF16)
        xc_ref[...] = xc.astype(BF16)
        a_ref[...] = a
        b_s[...] = s * ig * xc
        _scan_tile(a_ref, b_s, h_ref, carry, TS, reverse)

    wspec = _full((RNN_HEADS, RNN_HD, RNN_HD))
    cur = _rev_tile(TS, D, n, reverse)
    f32 = jax.ShapeDtypeStruct((T, D), F32)
    b16 = jax.ShapeDtypeStruct((T, D), BF16)
    return _pallas(
        body, name=name, grid=(n,),
        in_specs=[prev_spec, cur, next_spec, _full((4, D)), _full((1, D)),
                  wspec, wspec, _full((1, D)), _full((1, D)), _full((1, D))],
        out_specs=[cur] * 6,
        out_shape=[f32, f32, f32, b16, b16, b16],
        scratch_shapes=[pltpu.VMEM((TS, D), F32), pltpu.VMEM((1, D), F32)],
        compiler_params=_params(("arbitrary",)),
    )(xr, xr, xr, cw, cb, wa, wx, ba, bx, lam)


def _od_out(hf, hb, g1, w_out, x1, tgt, mod, lnp):
    T = x1.shape[0]

    def body(hf_ref, hb_ref, g_ref, w_ref, x_ref, t_ref, mod_ref, ln_ref,
             dh_ref, dg_ref, dx_ref, dwb_ref, vec_ref, dw_ref):
        i = pl.program_id(0)

        @pl.when(i == 0)
        def _():
            dw_ref[...] = jnp.zeros_like(dw_ref)
            vec_ref[...] = jnp.zeros_like(vec_ref)

        hs = hf_ref[...] + hb_ref[...]
        sg, dsg = _silu_and_grad(g_ref[...].astype(F32))
        yr = (hs * sg).astype(BF16)
        w = w_ref[...]
        out = _dot(yr, w)
        gate = mod_ref[2:3, :]
        z = ALPHA * x_ref[...] + gate * out
        lng = ln_ref[0:1, :]
        x2, xhat, rstd = _ln_fwd(z, lng, ln_ref[1:2, :])
        diff = x2 - t_ref[...]
        vec_ref[3:4, 0:LANE] += 0.5 * jnp.sum(diff * diff) * (1.0 / D)
        dx2 = diff * (1.0 / D)
        dz = _ln_bwd(dx2, xhat, rstd, lng)
        vec_ref[0:1, :] += _rowsum(dx2 * xhat)
        vec_ref[1:2, :] += _rowsum(dx2)
        vec_ref[2:3, :] += _rowsum(dz * out)
        dout = (dz * gate).astype(BF16)
        dyr = _dot_nt(dout, w)
        dw_ref[...] += _dot_tn(yr, dout)
        dh_ref[...] = dyr * sg
        dg_ref[...] = (dyr * hs * dsg).astype(BF16)
        dx_ref[...] = ALPHA * dz

        @pl.when(i == T // TMO - 1)
        def _():
            dwb_ref[...] = dw_ref[...].astype(BF16)

    return _pallas(
        body, name="od_out", grid=(T // TMO,),
        in_specs=[_tile(TMO, D), _tile(TMO, D), _tile(TMO, D), _full((D, D)), _tile(TMO, D), _tile(TMO, D),
                  _full((3, D)), _full((2, D))],
        out_specs=[_tile(TMO, D), _tile(TMO, D), _tile(TMO, D), _full((D, D)), _full((SUBLANE, D))],
        out_shape=[jax.ShapeDtypeStruct((T, D), F32), jax.ShapeDtypeStruct((T, D), BF16),
                   jax.ShapeDtypeStruct((T, D), F32), jax.ShapeDtypeStruct((D, D), BF16),
                   jax.ShapeDtypeStruct((SUBLANE, D), F32)],
        scratch_shapes=[pltpu.VMEM((D, D), F32)],
        compiler_params=_params(("arbitrary",)),
    )(hf, hb, g1, w_out, x1, tgt, mod, lnp)


def _rglru_bwd(fwd, dh, wa, wx, lam, reverse, name, comm=None):
    h, a_all, s_all, r_all, ig_all, xc_all = fwd
    T = h.shape[0]
    n = T // TS
    adj_rev = not reverse
    hprev_spec, hnext_spec = _halo_specs(TS, D, n, T, adj_rev)
    h_halo_spec = hnext_spec if reverse else hprev_spec

    def body(dh_ref, h_ref, hh_ref, a_ref, s_ref, r_ref, ig_ref, xc_ref, wa_ref, wx_ref, lam_ref,
             dxc_ref, dwa_ref, dwx_ref, vec_ref, a_s, l_s, carry, a_edge):
        i = pl.program_id(0)
        j = (n - 1 - i) if adj_rev else i

        @pl.when(i == 0)
        def _():
            carry[...] = jnp.zeros_like(carry)
            a_edge[...] = jnp.zeros_like(a_edge)
            dwa_ref[...] = jnp.zeros_like(dwa_ref)
            dwx_ref[...] = jnp.zeros_like(dwx_ref)
            vec_ref[...] = jnp.zeros_like(vec_ref)

        lam = lam_ref[...]
        sp = jnp.maximum(-lam, 0.0) + jnp.log(1.0 + jnp.exp(-jnp.abs(lam)))
        a, s = a_ref[...], s_ref[...]
        inv_s = lax.rsqrt(jnp.maximum(s * s, 1e-30))
        r, ig = r_ref[...].astype(F32), ig_ref[...].astype(F32)
        xcb = xc_ref[...]
        xc = xcb.astype(F32)

        rows = lax.broadcasted_iota(jnp.int32, (TS, D), 0)
        hcur = h_ref[...]
        if reverse:
            a_sh = jnp.where(rows == 0, a_edge[...], pltpu.roll(a, 1, 0))
            halo = jnp.where(j < n - 1, hh_ref[0:1, :], 0.0)
            h_nb = jnp.where(rows == TS - 1, halo, pltpu.roll(hcur, TS - 1, 0))
        else:
            a_sh = jnp.where(rows == TS - 1, a_edge[...], pltpu.roll(a, TS - 1, 0))
            halo = jnp.where(j > 0, hh_ref[SUBLANE - 1:SUBLANE, :], 0.0)
            h_nb = jnp.where(rows == 0, halo, pltpu.roll(hcur, 1, 0))
        a_s[...] = a_sh
        _scan_tile(a_s, dh_ref, l_s, carry, TS, adj_rev)
        a_edge[...] = a[TS - 1:TS, :] if reverse else a[0:1, :]

        lm = l_s[...]
        da = lm * h_nb
        di = lm * s * xc
        dxc = lm * s * ig
        ds = lm * ig * xc
        dlog_a = a * (da - ds * a * inv_s)
        dr = (-RG_C) * sp * dlog_a
        dsp = _rowsum((-RG_C) * r * dlog_a)
        dpr = dr * r * (1.0 - r)
        dpi = di * ig * (1.0 - ig)
        vec_ref[0:1, :] += _rowsum(dpr)
        vec_ref[1:2, :] += _rowsum(dpi)
        vec_ref[2:3, :] += dsp * (-_sigmoid(-lam))
        parts = []
        for hd in range(RNN_HEADS):
            sl = slice(hd * RNN_HD, (hd + 1) * RNN_HD)
            xh = xcb[:, sl]
            dprh = dpr[:, sl].astype(BF16)
            dpih = dpi[:, sl].astype(BF16)
            parts.append(_dot_nt(dprh, wa_ref[hd]) + _dot_nt(dpih, wx_ref[hd]))
            dwa_ref[hd] += _dot_tn(xh, dprh)
            dwx_ref[hd] += _dot_tn(xh, dpih)
        dxc_ref[...] = dxc + jnp.concatenate(parts, axis=-1)

    wspec = _full((RNN_HEADS, RNN_HD, RNN_HD))
    cur = _rev_tile(TS, D, n, adj_rev)
    return _fused_call(
        body, comm, (dh, h, h, a_all, s_all, r_all, ig_all, xc_all, wa, wx, lam), name=name, grid=(n,),
        in_specs=[cur, cur, h_halo_spec, cur, cur, cur, cur, cur, wspec, wspec, _full((1, D))],
        out_specs=[cur, wspec, wspec, _full((SUBLANE, D))],
        out_shape=[jax.ShapeDtypeStruct((T, D), F32),
                   jax.ShapeDtypeStruct((RNN_HEADS, RNN_HD, RNN_HD), F32),
                   jax.ShapeDtypeStruct((RNN_HEADS, RNN_HD, RNN_HD), F32),
                   jax.ShapeDtypeStruct((SUBLANE, D), F32)],
        scratch_shapes=[pltpu.VMEM((TS, D), F32)] * 2 + [pltpu.VMEM((1, D), F32)] * 2)


def _od_in_bwd(dxcf, dxcb, xr, dg1, x1, dx1p, mod, w_in, cw, comm=None):
    T = x1.shape[0]
    n = T // TMO
    slab = OD_IN // N_DEV
    prev_spec, next_spec = _halo_specs(TMO, D, n, T, False)

    def body(fp_ref, fc_ref, fn_ref, bp_ref, bc_ref, bn_ref, xr_ref, dg_ref, x1_ref, dxp_ref,
             mod_ref, w_ref, cw_ref, dx_ref, dwb_ref, vec_ref, dw_ref):
        i = pl.program_id(0)

        @pl.when(i == 0)
        def _():
            dw_ref[...] = jnp.zeros_like(dw_ref)
            vec_ref[...] = jnp.zeros_like(vec_ref)

        dcur = fc_ref[...] + bc_ref[...]
        dprev = jnp.where(i > 0, fp_ref[...] + bp_ref[...], 0.0)
        dnext = jnp.where(i < n - 1, fn_ref[...] + bn_ref[...], 0.0)
        dext = jnp.concatenate([dprev, dcur, dnext], axis=0)
        xr_v = xr_ref[...]
        cw_v = cw_ref[...]
        dxr = None
        for k in range(4):
            shifted = _shift_rows(dext, 2 - k, TMO)
            term = cw_v[k:k + 1, :] * shifted
            dxr = term if dxr is None else dxr + term
            vec_ref[k:k + 1, :] += _rowsum(shifted * xr_v)
        vec_ref[4:5, :] += _rowsum(dcur)
        dp = jnp.concatenate([dxr.astype(BF16), dg_ref[...]], axis=-1)
        x1v = x1_ref[...]
        scale1 = 1.0 + mod_ref[1:2, :]
        h1 = (x1v * scale1 + mod_ref[0:1, :]).astype(BF16)
        dh1 = _dot_nt(dp, w_ref[...])
        dw_ref[...] += _dot_tn(h1, dp)
        dx_ref[...] = dxp_ref[...] + dh1 * scale1
        vec_ref[5:6, :] += _rowsum(dh1)
        vec_ref[6:7, :] += _rowsum(dh1 * x1v)

        @pl.when(i == n - 1)
        def _():
            for j in range(N_DEV):
                dwb_ref[j] = dw_ref[:, j * slab:(j + 1) * slab].astype(BF16)

    t = _tile(TMO, D)
    return _fused_call(
        body, comm, (dxcf, dxcf, dxcf, dxcb, dxcb, dxcb, xr, dg1, x1, dx1p, mod, w_in, cw),
        name="od_in_bwd", grid=(n,),
        in_specs=[prev_spec, t, next_spec, prev_spec, t, next_spec, t, t, t, t,
                  _full((3, D)), _full((D, OD_IN)), _full((4, D))],
        out_specs=[t, _full((N_DEV, D, slab)), _full((SUBLANE, D))],
        out_shape=[jax.ShapeDtypeStruct((T, D), F32), jax.ShapeDtypeStruct((N_DEV, D, slab), BF16),
                   jax.ShapeDtypeStruct((SUBLANE, D), F32)],
        scratch_shapes=[pltpu.VMEM((D, OD_IN), F32)])


def _ev_out_bwd(dx1, z0, out0, y0, ycat, g0, w_out, mod, lnp):
    T = dx1.shape[0]

    def body(dx_ref, z_ref, out_ref, y0_ref, yc_ref, g_ref, w_ref, mod_ref, ln_ref,
             dxp_ref, dyc_ref, dg_ref, dwb_ref, vec_ref, dw_ref):
        i = pl.program_id(0)

        @pl.when(i == 0)
        def _():
            dw_ref[...] = jnp.zeros_like(dw_ref)
            vec_ref[...] = jnp.zeros_like(vec_ref)

        lng = ln_ref[0:1, :]
        _, xhat, rstd = _ln_fwd(z_ref[...], lng, ln_ref[1:2, :])
        dy = dx_ref[...]
        dz = _ln_bwd(dy, xhat, rstd, lng)
        vec_ref[0:1, :] += _rowsum(dy * xhat)
        vec_ref[1:2, :] += _rowsum(dy)
        vec_ref[2:3, :] += _rowsum(dz * out_ref[...].astype(F32))
        dout = (dz * mod_ref[2:3, :]).astype(BF16)
        dy0 = _dot_nt(dout, w_ref[...])
        dw_ref[...] += _dot_tn(y0_ref[...], dout)
        sg, dsg = _silu_and_grad(g_ref[...].astype(F32))
        dyc_ref[...] = (dy0 * sg).astype(BF16)
        dg_ref[...] = (dy0 * yc_ref[...].astype(F32) * dsg).astype(BF16)
        dxp_ref[...] = ALPHA * dz

        @pl.when(i == T // TMO - 1)
        def _():
            dwb_ref[...] = dw_ref[...].astype(BF16)

    t = _tile(TMO, D)
    return _pallas(
        body, name="ev_out_bwd", grid=(T // TMO,),
        in_specs=[t, t, t, t, t, t, _full((D, D)), _full((3, D)), _full((2, D))],
        out_specs=[t, t, t, _full((D, D)), _full((SUBLANE, D))],
        out_shape=[jax.ShapeDtypeStruct((T, D), F32), jax.ShapeDtypeStruct((T, D), BF16),
                   jax.ShapeDtypeStruct((T, D), BF16), jax.ShapeDtypeStruct((D, D), BF16),
                   jax.ShapeDtypeStruct((SUBLANE, D), F32)],
        scratch_shapes=[pltpu.VMEM((D, D), F32)],
        compiler_params=_params(("arbitrary",)),
    )(dx1, z0, out0, y0, ycat, g0, w_out, mod, lnp)


def _mix0_bwd(q, kvx, lse, dyc, ycat, su, sv, sink_l, bias, a128, gsum, sel, sg_lng, sg_lnb, sg_w, sg_bfull,
              rc, rs1, rs2, comm=None):
    T = q.shape[0]
    nb = T // BLK

    def body(q_ref, kp_ref, kc_ref, kn_ref, lse_ref, dyc_ref, yc_ref, su_ref, sv_ref, sink_ref, bias_ref, a_ref,
             gsum_ref, sel_ref, lng_ref, lnb_ref, w_ref, bfull_ref, c_ref, s1_ref, s2_ref,
             dq_ref, dkv_ref, dsu_ref, dsv_ref, dw_ref, dbt_ref, vec_ref, dsink_ref):
        n = pl.program_id(0)

        @pl.when(n == 0)
        def _():
            dkv_ref[...] = jnp.zeros_like(dkv_ref)
            dw_ref[...] = jnp.zeros_like(dw_ref)
            dbt_ref[...] = jnp.zeros_like(dbt_ref)
            vec_ref[...] = jnp.zeros_like(vec_ref)
            dsink_ref[...] = jnp.zeros_like(dsink_ref)

        band = pl.ds(pl.multiple_of(n * BLK + (TM - BLK), BLK), 3 * BLK)
        bias = _band_bias(bias_ref, n, nb)
        kvx = jnp.concatenate([kp_ref[...], kc_ref[...], kn_ref[...]], axis=0)
        bias2 = jnp.concatenate([bias, bias], axis=1)
        low = lax.broadcasted_iota(jnp.int32, (BLK, LANE), 1) < HEAD_DIM
        low2 = lax.broadcasted_iota(jnp.int32, (2 * BLK, LANE), 1) < HEAD_DIM
        sel = sel_ref[...]
        c, s1, s2 = c_ref[...], s1_ref[...], s2_ref[...]
        for kvh in range(2):
            t0, t1 = 2 * kvh, 2 * kvh + 1
            q2 = jnp.concatenate([_lane_tile(q_ref, t0), _lane_tile(q_ref, t1)], axis=0)
            do2 = jnp.concatenate([_lane_tile(dyc_ref, t0), _lane_tile(dyc_ref, t1)], axis=0)
            yc2 = jnp.concatenate([_lane_tile(yc_ref, t0), _lane_tile(yc_ref, t1)], axis=0)
            p_hi, p_lo = _split_bf16(do2.astype(F32) * yc2.astype(F32))
            deltas = _dot_nt(sel, p_hi) + _dot_nt(sel, p_lo)
            dkx = jnp.zeros((3 * BLK, LANE), F32)
            dvx = jnp.zeros((3 * BLK, LANE), F32)
            dq_acc = None
            for par in range(2):
                heads = (4 * kvh + par, 4 * kvh + 2 + par)
                kt = 2 * kvh + par
                ke = kvx[:, kt * LANE:(kt + 1) * LANE]
                ve = kvx[:, (4 + kt) * LANE:(5 + kt) * LANE]
                lse = jnp.concatenate([lse_ref[0, :, h * LANE:(h + 1) * LANE] for h in heads], axis=1)
                sk = jnp.concatenate([_lane_tile(sink_ref, h) for h in heads], axis=1)
                delta = deltas[par:par + 1, :]
                pt = jnp.exp(_dot_nt(ke, q2) + bias2 - lse)
                dst = (pt * (_dot_nt(ve, do2) - delta)).astype(BF16)
                sink_terms = jnp.exp(sk - lse) * delta
                for k, h in enumerate(heads):
                    dsink_ref[:, h * LANE:(h + 1) * LANE] += sink_terms[:, k * LANE:(k + 1) * LANE]
                part = _dot_tn(dst, ke)
                dq_acc = part if dq_acc is None else dq_acc + part
                mine = low2 if par == 0 else jnp.logical_not(low2)
                dkx = dkx + jnp.dot(dst, jnp.where(mine, q2, jnp.zeros_like(q2)), preferred_element_type=F32)
                dvx = dvx + jnp.dot(pt.astype(BF16), jnp.where(mine, do2, jnp.zeros_like(do2)),
                                    preferred_element_type=F32)
            for k, t in enumerate((t0, t1)):
                dq_t = dq_acc[k * BLK:(k + 1) * BLK] * (HEAD_DIM ** -0.5)
                dq_ref[:, t * LANE:(t + 1) * LANE] = _rope_bwd(dq_t, c, s1, s2).astype(BF16)
            dkv_ref[band, kvh * LANE:(kvh + 1) * LANE] += dkx
            dkv_ref[band, (2 + kvh) * LANE:(3 + kvh) * LANE] += dvx

        lng = lng_ref[...]
        xhat, rstd, vb, svm = _sg_core(sv_ref, lng, lnb_ref[...], a_ref, w_ref, bfull_ref)
        dy = dyc_ref[:, ATTN_W:].astype(F32)
        dsu_ref[...] = (dy * svm).astype(BF16)
        dsvm = dy * su_ref[...].astype(F32)
        d_hi, d_lo = _split_bf16(dsvm)
        gsum = gsum_ref[...]
        dbt_ref[...] += jnp.dot(d_hi, gsum, preferred_element_type=F32) + jnp.dot(d_lo, gsum,
                                                                                 preferred_element_type=F32)
        tiles = []
        for t in range(SG_W // LANE):
            tl = slice(t * LANE, (t + 1) * LANE)
            dt, v2 = d_hi[:, tl], vb[:, tl]
            dw_ref[2 * t] += _dot_nt(jnp.where(low, dt, jnp.zeros_like(dt)), v2)
            dw_ref[2 * t + 1] += _dot_nt(jnp.where(low, jnp.zeros_like(dt), dt), v2)
            tiles.append(jnp.where(low, _dot_tn(w_ref[2 * t], dt), _dot_tn(w_ref[2 * t + 1], dt)))
        dvgn = jnp.concatenate(tiles, axis=-1)
        vec_ref[0:1, :] += _rowsum(dvgn * xhat)
        vec_ref[1:2, :] += _rowsum(dvgn)
        dxh = dvgn * lng
        m1 = _group_mean(dxh, a_ref)
        m2 = _group_mean(dxh * xhat, a_ref)
        dsv_ref[...] = (rstd * (dxh - m1 - xhat * m2)).astype(BF16)

    return _fused_call(
        body, comm, (q, kvx, kvx, kvx, lse, dyc, ycat, su, sv, sink_l, bias, a128, gsum, sel, sg_lng, sg_lnb, sg_w,
                     sg_bfull, rc, rs1, rs2),
        name="mix0_bwd", grid=(nb,),
        in_specs=[_tile(BLK, ATTN_W)] + _band_specs(KVX_W, nb) + [
            pl.BlockSpec((1, 1, N_HEADS * LANE), lambda n: (n, 0, 0)), _tile(BLK, D), _tile(BLK, D),
            _tile(BLK, SG_W), _tile(BLK, SG_W), _full((1, N_HEADS * LANE)), _full((3 * BLK, LANE)),
            _full((2 * LANE, 2 * LANE)),_full((SG_W, LANE)), _full((SUBLANE, LANE)), _full((1, SG_W)), _full((1, SG_W)),
            _full((SG_GROUPS, BLK, BLK)), _full((BLK, SG_W)), _tile(BLK, LANE), _tile(BLK, LANE), _tile(BLK, LANE)],
        out_specs=[_tile(BLK, ATTN_W), _full((T + 2 * TM, 4 * LANE)), _tile(BLK, SG_W), _tile(BLK, SG_W),
                   _full((SG_GROUPS, BLK, BLK)), _full((BLK, LANE)), _full((SUBLANE, SG_W)),
                   _full((1, N_HEADS * LANE))],
        out_shape=[jax.ShapeDtypeStruct((T, ATTN_W), BF16), jax.ShapeDtypeStruct((T + 2 * TM, 4 * LANE), F32),
                   jax.ShapeDtypeStruct((T, SG_W), BF16), jax.ShapeDtypeStruct((T, SG_W), BF16),
                   jax.ShapeDtypeStruct((SG_GROUPS, BLK, BLK), F32), jax.ShapeDtypeStruct((BLK, LANE), F32),
                   jax.ShapeDtypeStruct((SUBLANE, SG_W), F32), jax.ShapeDtypeStruct((1, N_HEADS * LANE), F32)])


def _ev_in_bwd(dq, dkv, dsu, dsv, dg0, x, dxp, mod, w_in, rc, rs1, rs2, comm=None):
    T = x.shape[0]

    def body(dq_ref, dkv_ref, dsu_ref, dsv_ref, dg_ref, x_ref, dxp_ref, mod_ref, w_ref, c_ref, s1_ref, s2_ref,
             dx_ref, dwb_ref, vec_ref, dw_ref):
        i = pl.program_id(0)

        @pl.when(i == 0)
        def _():
            dw_ref[...] = jnp.zeros_like(dw_ref)
            vec_ref[...] = jnp.zeros_like(vec_ref)

        low = lax.broadcasted_iota(jnp.int32, (TM, LANE), 1) < HEAD_DIM

        def fold(j):
            t0 = dkv_ref[:, (2 * j) * LANE:(2 * j + 1) * LANE]
            t1 = dkv_ref[:, (2 * j + 1) * LANE:(2 * j + 2) * LANE]
            return jnp.where(low, t0 + pltpu.roll(t0, HEAD_DIM, 1), t1 + pltpu.roll(t1, HEAD_DIM, 1))

        dk = _rope_bwd(fold(0), c_ref[...], s1_ref[...], s2_ref[...]).astype(BF16)
        dp = jnp.concatenate([dq_ref[...], dk, fold(1).astype(BF16), dsu_ref[...], dsv_ref[...],
                              dg_ref[...]], axis=-1)
        xv = x_ref[...]
        scale0 = 1.0 + mod_ref[1:2, :]
        h0 = (xv * scale0 + mod_ref[0:1, :]).astype(BF16)
        dh0 = _dot(dp, w_ref[...])
        dw_ref[...] += _dot_tn(dp, h0)
        dx_ref[...] = dxp_ref[...] + dh0 * scale0
        vec_ref[0:1, :] += _rowsum(dh0)
        vec_ref[1:2, :] += _rowsum(dh0 * xv)

        @pl.when(i == T // TM - 1)
        def _():
            dwb_ref[...] = dw_ref[...].astype(BF16)

    t = _tile(TM, D)
    return _fused_call(
        body, comm, (dq, dkv, dsu, dsv, dg0, x, dxp, mod, w_in, rc, rs1, rs2), name="ev_in_bwd", grid=(T // TM,),
        in_specs=[_tile(TM, ATTN_W), pl.BlockSpec((TM, 4 * LANE), lambda i: (i + 1, 0)), _tile(TM, SG_W),
                  _tile(TM, SG_W), t, t, t,
                  _full((3, D)), _full((EV_IN, D)), _tile(TM, LANE), _tile(TM, LANE), _tile(TM, LANE)],
        out_specs=[t, _full((EV_IN, D)), _full((SUBLANE, D))],
        out_shape=[jax.ShapeDtypeStruct((T, D), F32), jax.ShapeDtypeStruct((EV_IN, D), BF16),
                   jax.ShapeDtypeStruct((SUBLANE, D), F32)],
        scratch_shapes=[pltpu.VMEM((EV_IN, D), F32)])


def _sum_slots(land_ref):
    g = land_ref[0].astype(F32)
    for i in range(1, land_ref.shape[0]):
        g = g + land_ref[i].astype(F32)
    return g


def _reduce_adam(items, name, after=()):
    R, C = items[0][1].shape
    rb = R
    if R > 512:
        for cand in (512, 256, 128, 64, 32, 16, 8):
            if R % cand == 0:
                rb = cand
                break
    n = len(items)

    def body(*refs):
        for k in range(n):
            l_ref, w_ref, m_ref, v_ref = refs[4 * k:4 * k + 4]
            first_out = 4 * n + len(after)
            g_ref, d_ref, nm_ref, nv_ref = refs[first_out + 4 * k:first_out + 4 * k + 4]
            g = _sum_slots(l_ref)
            g_ref[...] = g
            dlt, m2, v2 = _adam(w_ref[...], g, m_ref[...], v_ref[...])
            d_ref[...] = dlt
            nm_ref[...] = m2
            nv_ref[...] = v2

    t = pl.BlockSpec((rb, C), lambda i: (i, 0))
    shp = jax.ShapeDtypeStruct((R, C), F32)
    in_specs, operands = [], []
    for land, w, m, v in items:
        in_specs += [pl.BlockSpec((land.shape[0], rb, C), lambda i: (0, i, 0)), t, t, t]
        operands += [land, w, m, v]
    res = _pallas(
        body, name=name, grid=(R // rb,),
        in_specs=in_specs + [pl.BlockSpec(memory_space=pl.ANY)] * len(after),
        out_specs=[t] * (4 * n), out_shape=[shp] * (4 * n),
        compiler_params=_params(("parallel",)),
    )(*operands, *after)
    return [list(res[4 * k:4 * k + 4]) for k in range(n)]


def _tail_stage1(slabs, small):
    _, R, C = slabs.shape
    n_chips = N_DEV // 2
    gather = _GatherComm(small)
    ns = gather.n

    def body(*refs):
        slab_ref = refs[0]
        g_ins = refs[1:1 + ns]
        part, land_ref = refs[1 + ns], refs[2 + ns]
        g_outs = refs[3 + ns:3 + 2 * ns]
        stage, s1_send, s1_recv = refs[3 + 2 * ns:6 + 2 * ns]
        g_sems = refs[6 + 2 * ns:]
        x, y, c = _my_pos()
        chip = 2 * x + y
        gather.start(g_ins, g_outs, g_sems)
        swaps = [pltpu.make_async_remote_copy(
            src_ref=slab_ref.at[2 * k + (1 - c)], dst_ref=stage.at[k], send_sem=s1_send.at[k],
            recv_sem=s1_recv.at[k], device_id=(x, y, 1 - c), device_id_type=MESH) for k in range(n_chips)]
        for cp in swaps:
            cp.start()
        for cp in swaps:
            cp.wait()
        for k in range(n_chips):
            part[k] = (slab_ref[2 * k + c].astype(F32) + stage[k].astype(F32)).astype(BF16)
        land_ref[chip] = part[chip]
        gather.mid(g_ins, g_outs, g_sems)
        gather.finish(g_ins, g_outs, g_sems)

    any_spec = pl.BlockSpec(memory_space=pl.ANY)
    vmem_spec = pl.BlockSpec(memory_space=pltpu.VMEM)
    slab4 = jax.ShapeDtypeStruct((n_chips, R, C), BF16)
    res = _pallas(
        body, name="tail_stage1",
        out_shape=[slab4, slab4] + gather.out_shapes(),
        in_specs=[vmem_spec] + [any_spec] * ns, out_specs=[vmem_spec, vmem_spec] + [any_spec] * ns,
        scratch_shapes=[pltpu.VMEM((n_chips, R, C), BF16),
                        pltpu.SemaphoreType.DMA((n_chips,)), pltpu.SemaphoreType.DMA((n_chips,))] + gather.sems(),
        compiler_params=pltpu.CompilerParams(vmem_limit_bytes=VMEM_LIMIT),
    )(slabs, *gather.arrs)
    return res[0], res[1], list(res[2:])


def _chip_copies(part_ref, land_ref, send_sems, recv_sems):
    x, y, c = _my_pos()
    chip = 2 * x + y
    copies = []
    for r in range(1, N_DEV // 2):
        px = (1 - x) if (r & 2) else x
        py = (1 - y) if (r & 1) else y
        copies.append(pltpu.make_async_remote_copy(
            src_ref=part_ref.at[2 * px + py], dst_ref=land_ref.at[chip], send_sem=send_sems[r - 1],
            recv_sem=recv_sems[r - 1], device_id=(px, py, c), device_id_type=MESH))
    return copies


def _tail_send(part, land):
    n = N_DEV // 2 - 1

    def body(part_ref, land_ref, *outs):
        send_sems, recv_sems = outs[:n], outs[n:2 * n]
        token = outs[2 * n + 2]
        for cp in _chip_copies(part_ref, land_ref, send_sems, recv_sems):
            cp.start()
        token[...] = jnp.zeros_like(token)

    hbm = pl.BlockSpec(memory_space=pltpu.HBM)
    sem = pl.BlockSpec(memory_space=pltpu.SEMAPHORE)
    res = _pallas(
        body, name="tail_send",
        out_shape=tuple([pltpu.SemaphoreType.DMA(())] * (2 * n)
                        + [pltpu.HBM(part.shape, part.dtype), pltpu.HBM(land.shape, land.dtype),
                           jax.ShapeDtypeStruct((SUBLANE, LANE), F32)]),
        in_specs=(hbm, hbm), out_specs=tuple([sem] * (2 * n) + [hbm, hbm, pl.BlockSpec(memory_space=pltpu.VMEM)]),
        input_output_aliases={0: 2 * n, 1: 2 * n + 1},
        compiler_params=pltpu.CompilerParams(has_side_effects=pltpu.SideEffectType.DATAFLOW_SIDE_EFFECTING),
    )(pltpu.with_memory_space_constraint(part, pltpu.HBM), pltpu.with_memory_space_constraint(land, pltpu.HBM))
    return list(res[:n]), list(res[n:2 * n]), res[2 * n], res[2 * n + 1], res[2 * n + 2]


def _tail_wait(send_sems, recv_sems, part, land, after):
    n = len(send_sems)

    def body(part_ref, land_ref, *rest):
        ss, rs = rest[:n], rest[n:2 * n]
        for cp in _chip_copies(part_ref, land_ref, ss, rs):
            cp.wait_send()
            cp.wait_recv()

    hbm = pl.BlockSpec(memory_space=pltpu.HBM)
    sem = pl.BlockSpec(memory_space=pltpu.SEMAPHORE)
    any_spec = pl.BlockSpec(memory_space=pl.ANY)
    res = _pallas(
        body, name="tail_wait",
        out_shape=(pltpu.HBM(part.shape, part.dtype), pltpu.HBM(land.shape, land.dtype)),
        in_specs=tuple([hbm, hbm] + [sem] * (2 * n) + [any_spec] * len(after)), out_specs=(hbm, hbm),
        input_output_aliases={0: 0, 1: 1},
        compiler_params=pltpu.CompilerParams(has_side_effects=pltpu.SideEffectType.DATAFLOW_SIDE_EFFECTING),
    )(part, land, *send_sems, *recv_sems, *after)
    return res[1]


def _slots_adam(items, name, after=()):
    zeros3 = (0, 0, 0)
    in_specs, out_specs, out_shape, operands = [], [], [], []
    for land, w, m, v in items:
        inner = w.shape[-3:]
        if w.ndim == 5:
            lspec = pl.BlockSpec((N_DEV, 1) + inner, lambda i: (0, i) + zeros3)
            wspec = pl.BlockSpec((1, 1) + inner, lambda i: (0, i) + zeros3)
        else:
            lspec = pl.BlockSpec((N_DEV,) + inner, lambda i: (0,) + zeros3)
            wspec = pl.BlockSpec((1,) + inner, lambda i: (0,) + zeros3)
        in_specs += [lspec, wspec, wspec, wspec]
        out_specs += [wspec] * 4
        out_shape += [jax.ShapeDtypeStruct(w.shape, F32)] * 4
        operands += [land, w, m, v]
    n = len(items)

    def body(*refs):
        for k, (_, w, _, _) in enumerate(items):
            l_ref, w_ref, m_ref, v_ref = refs[4 * k:4 * k + 4]
            first_out = 4 * n + len(after)
            outs = refs[first_out + 4 * k:first_out + 4 * k + 4]
            at = (0, 0) if w.ndim == 5 else (0,)

            def update(l_ref=l_ref, w_ref=w_ref, m_ref=m_ref, v_ref=v_ref, outs=outs, at=at):
                g = l_ref[(0,) + at[1:]].astype(F32)
                for i in range(1, N_DEV):
                    g = g + l_ref[(i,) + at[1:]].astype(F32)
                dlt, m2, v2 = _adam(w_ref[at], g, m_ref[at], v_ref[at])
                for o_ref, val in zip(outs, (g, dlt, m2, v2)):
                    o_ref[at] = val

            if w.ndim == 5:
                update()
            else:
                pl.when(pl.program_id(0) == 0)(update)

    res = _pallas(
        body, name=name, grid=(2,),
        in_specs=in_specs + [pl.BlockSpec(memory_space=pl.ANY)] * len(after),
        out_specs=out_specs, out_shape=out_shape,
        compiler_params=_params(("arbitrary",)),
    )(*operands, *after)
    return [list(res[4 * k:4 * k + 4]) for k in range(n)]


SMALL_PARAMS = ("ln_g", "ln_b", "ev_sg_ln_g", "ev_sg_ln_b", "ev_sink", "ev_sg_b",
                "od_conv_w", "od_conv_b", "od_b_a", "od_b_x", "od_lam")


def _small_update(ga, gc, gd, gf, gb, ge, gsink, gbt, params):
    names = list(SMALL_PARAMS)
    flat = [a for nm in names for a in params[nm]]
    n_g = 8

    def body(*refs):
        ga_ref, gc_ref, gd_ref, gf_ref, gb_ref, ge_ref, gs_ref, gbt_ref = refs[:n_g]
        prm = refs[n_g:n_g + 3 * len(names)]
        loss_ref = refs[n_g + 3 * len(names)]
        outs = refs[n_g + 3 * len(names) + 1:]

        def ssum(ref):
            acc = ref[0]
            for i in range(1, N_DEV):
                acc = acc + ref[i]
            return acc

        a, cc, dd, ff, bb, ee = ssum(ga_ref), ssum(gc_ref), ssum(gd_ref), ssum(gf_ref), ssum(gb_ref), ssum(ge_ref)
        loss_ref[...] = a[3:4, 0:LANE]
        me = _slot(*_my_pos())

        def mine(rows):
            acc = jnp.zeros((rows.shape[0], LANE), F32)
            for j in range(N_DEV):
                acc = acc + jnp.where(me == j, rows[:, j * LANE:(j + 1) * LANE], 0.0)
            return acc

        sink_terms = ssum(gs_ref)
        lane8 = lax.broadcasted_iota(jnp.int32, (1, N_HEADS), 1)
        g_sink = jnp.zeros((1, N_HEADS), F32)
        for h in range(N_HEADS):
            tot = -jnp.sum(sink_terms[:, h * LANE:(h + 1) * LANE], axis=1, keepdims=True)
            g_sink = jnp.where(lane8 == h, tot, g_sink)
        grads = dict(
            ln_g=jnp.concatenate([dd[0:1], a[0:1]], axis=0), ln_b=jnp.concatenate([dd[1:2], a[1:2]], axis=0),
            ev_sg_ln_g=ee[0:1], ev_sg_ln_b=ee[1:2], ev_sink=g_sink,
            ev_sg_b=jnp.transpose(ssum(gbt_ref))[0:SG_GROUPS, :],
            od_conv_w=mine(cc[0:4]), od_conv_b=mine(cc[4:5]),
            od_b_a=mine(jnp.concatenate([ff[0:1], bb[0:1]], axis=0)),
            od_b_x=mine(jnp.concatenate([ff[1:2], bb[1:2]], axis=0)),
            od_lam=mine(jnp.concatenate([ff[2:3], bb[2:3]], axis=0)))
        for k, nm in enumerate(names):
            w_ref, m_ref, v_ref = prm[3 * k:3 * k + 3]
            at = (0,) if len(w_ref.shape) == 3 else ()
            g = grads[nm]
            dlt, m2, v2 = _adam(w_ref[at] if at else w_ref[...], g, m_ref[at] if at else m_ref[...],
                                v_ref[at] if at else v_ref[...])
            for o_ref, val in zip(outs[4 * k:4 * k + 4], (g, dlt, m2, v2)):
                if at:
                    o_ref[at] = val
                else:
                    o_ref[...] = val

    gathered = [ga, gc, gd, gf, gb, ge, gsink, gbt]
    out_shape = [jax.ShapeDtypeStruct((1, LANE), F32)]
    for nm in names:
        out_shape += [jax.ShapeDtypeStruct(params[nm][0].shape, F32)] * 4
    return _pallas(
        body, name="small_update", grid=(1,),
        in_specs=[_full(a.shape) for a in gathered + flat],
        out_specs=[_full(s.shape) for s in out_shape], out_shape=out_shape,
        compiler_params=_params(("arbitrary",)),
    )(*gathered, *flat)


VEC_ROWS = 16
VEC_LAYOUT = (("od_conv_w", 4), ("od_conv_b", 1), ("od_b_a", 2), ("od_b_x", 2), ("od_lam", 2))


def _to_slabs(full, cols_per):
    R = full.shape[0]
    return full.reshape(R, N_DEV, cols_per).transpose(1, 0, 2)


def _from_slabs(slabs):
    n, R, cp = slabs.shape
    return slabs.transpose(1, 0, 2).reshape(R, n * cp)


def kernel(x, c, positions, ada_w, ada_b, ln_g, ln_b, ev_w_in, ev_w_out, ev_sink, ev_sg_ln_g, ev_sg_ln_b, ev_sg_w, ev_sg_b, od_w_in, od_conv_w, od_conv_b, od_w_a, od_b_a, od_w_x, od_b_x, od_lam, od_w_out, loss_target, m_ada_w, m_ada_b, m_ln_g, m_ln_b, m_ev_w_in, m_ev_w_out, m_ev_sink, m_ev_sg_ln_g, m_ev_sg_ln_b, m_ev_sg_w, m_ev_sg_b, m_od_w_in, m_od_conv_w, m_od_conv_b, m_od_w_a, m_od_b_a, m_od_w_x, m_od_b_x, m_od_lam, m_od_w_out, v_ada_w, v_ada_b, v_ln_g, v_ln_b, v_ev_w_in, v_ev_w_out, v_ev_sink, v_ev_sg_ln_g, v_ev_sg_ln_b, v_ev_sg_w, v_ev_sg_b, v_od_w_in, v_od_conv_w, v_od_conv_b, v_od_w_a, v_od_b_a, v_od_w_x, v_od_b_x, v_od_lam, v_od_w_out):
    T = x.shape[1]
    me = _slot(*_my_pos())
    xs = x.reshape(T, D)
    tgt = loss_target.reshape(T, D)

    c_all, mod_all, g_vec, (g_ev_in,), (s_ev_out, s_od_in, s_od_out, sg_w, wa, wx) = _head_gather(
        c, ada_w, [ev_w_in[0].T.astype(BF16)],
        [ev_w_out[0], od_w_in[0], od_w_out[0], ev_sg_w[0], od_w_a[0], od_w_x[0]],
        [od_conv_w, od_conv_b, od_b_a, od_b_x, od_lam])
    c_all = c_all.reshape(N_DEV, D)
    w_ev_in = g_ev_in.reshape(EV_IN, D)
    vec_full = _from_slabs(g_vec)
    cw, cb = vec_full[0:4], vec_full[4:5]
    ba, bx, lam = vec_full[5:7], vec_full[7:9], vec_full[9:11]
    mod_mine = lax.dynamic_index_in_dim(mod_all, me, axis=2, keepdims=False)
    mod = mod_mine.transpose(1, 0, 2).reshape(2, 3 * D) + ada_b
    mod0 = mod[0].reshape(3, D)
    mod1 = mod[1].reshape(3, D)

    half = 8
    inv_freq = jnp.power(jnp.float32(ROPE_THETA), -jnp.arange(half, dtype=F32) / half)
    ang = positions.reshape(T).astype(F32)[:, None] * inv_freq
    cos_t = jnp.tile(jnp.cos(ang), (1, LANE // half))
    sin_t = jnp.tile(jnp.sin(ang), (1, LANE // half))
    l64 = jnp.arange(LANE) % HEAD_DIM
    rc = jnp.where(l64 < 2 * half, cos_t, 1.0)
    rs1 = jnp.where(l64 < half, -sin_t, 0.0)
    rs2 = jnp.where((l64 >= half) & (l64 < 2 * half), sin_t, 0.0)

    ln0 = jnp.stack([ln_g[0], ln_b[0]])
    ln1 = jnp.stack([ln_g[1], ln_b[1]])
    sg_lng = ev_sg_ln_g
    sg_lnb = ev_sg_ln_b
    sg_bfull = jnp.repeat(ev_sg_b[0].T, SG_DIM, axis=1)
    sink_l = jnp.repeat(ev_sink, LANE, axis=1)
    kj = jnp.arange(3 * BLK)[:, None]
    qi = jnp.arange(BLK)[None, :]
    band_bias = jnp.where(jnp.abs(kj - BLK - qi) <= BLK, 0.0, NEG_INF).astype(F32)
    lanes = jnp.arange(LANE)
    lanes2 = jnp.arange(2 * LANE)
    a128 = jnp.where(lanes2[:, None] // SG_DIM == lanes2[None, :] // SG_DIM, 1.0 / SG_DIM, 0.0).astype(BF16)
    gsum = (jnp.arange(SG_W)[:, None] // SG_DIM == lanes[None, :]).astype(BF16)
    sel = (jnp.arange(SUBLANE)[:, None] == lanes[None, :] // HEAD_DIM).astype(BF16)

    (q, kvx, su, sv, g0), _ = _ev_in(xs, mod0, w_ev_in, rc, rs1, rs2)
    (ycat, y0, lse), (g_ev_out, g_od_in, g_od_out) = _mix0_fwd(
        q, kvx, su, sv, g0, sink_l, band_bias, a128, sg_lng, sg_lnb, sg_w, sg_bfull,
        _GatherComm([s_ev_out, s_od_in, s_od_out], mid_frac=0.75))
    w_ev_out = g_ev_out.reshape(D, D)
    w_od_in = _from_slabs(g_od_in)
    w_od_out = g_od_out.reshape(D, D)
    out0, z0, x1 = _ev_out(y0, w_ev_out, xs, mod0, ln0)
    xr, g1 = _od_in(x1, mod1, w_od_in)
    fwd_f = _rglru_fwd(xr, cw, cb, wa[0], wx[0], ba[0:1], bx[0:1], lam[0:1], False, "rglru_fwd_f")
    fwd_b = _rglru_fwd(xr, cw, cb, wa[1], wx[1], ba[1:2], bx[1:2], lam[1:2], True, "rglru_fwd_b")
    dh, dg1, dx1p, d_od_out, vec_a = _od_out(fwd_f[0], fwd_b[0], g1, w_od_out, x1, tgt, mod1, ln1)

    (dxcf, dwa_f, dwx_f, vec_f), (l_od_out,) = _rglru_bwd(
        fwd_f, dh, wa[0], wx[0], lam[0:1], False, "rglru_bwd_f",
        _ExchangeComm([d_od_out.reshape(N_DEV, D // N_DEV, D)]))
    (dxcb, dwa_b, dwx_b, vec_b), _ = _rglru_bwd(fwd_b, dh, wa[1], wx[1], lam[1:2], True, "rglru_bwd_b")
    (dx1, d_od_in, vec_c), (a_wa, a_wx) = _od_in_bwd(
        dxcf, dxcb, xr, dg1, x1, dx1p, mod1, w_od_in, cw,
        _GatherComm([jnp.stack([dwa_f, dwa_b]).astype(BF16), jnp.stack([dwx_f, dwx_b]).astype(BF16)],
                    mid_frac=0.75))
    dxp, dyc, dg0, d_ev_out, vec_d = _ev_out_bwd(dx1, z0, out0, y0, ycat, g0, w_ev_out, mod0, ln0)
    (dq, dkv, dsu, dsv, d_sg_w, d_sg_bt, vec_e, d_sink_l), (l_od_in, l_ev_out, ga, gc, gd, gf, gb) = _mix0_bwd(
        q, kvx, lse, dyc, ycat, su, sv, sink_l, band_bias, a128, gsum, sel, sg_lng, sg_lnb, sg_w, sg_bfull,
        rc, rs1, rs2, _BothComm(_ExchangeComm([d_od_in, d_ev_out.reshape(N_DEV, D // N_DEV, D)]),
                                _GatherComm([vec_a, vec_c, vec_d, vec_f, vec_b], mid_frac=0.9)))
    (grad_x, d_ev_in, vec_g), _ = _ev_in_bwd(dq, dkv, dsu, dsv, dg0, xs, dxp, mod0, w_ev_in, rc, rs1, rs2)

    part, land, (gg, ge, gsink, gbt, a_sgw) = _tail_stage1(
        d_ev_in.reshape(N_DEV, EV_IN // N_DEV, D), [vec_g, vec_e, d_sink_l, d_sg_bt, d_sg_w.astype(BF16)])
    send_sems, recv_sems, part, land, token = _tail_send(part, land)

    dmod_all = jnp.stack([jnp.concatenate([gg[:, 0], gg[:, 1], gd[:, 2]], axis=-1),
                          jnp.concatenate([gc[:, 5], gc[:, 6], ga[:, 2]], axis=-1)], axis=1)
    cols = ada_w.shape[2]
    dmod_cols = lax.dynamic_slice_in_dim(dmod_all, me * cols, cols, axis=2).transpose(1, 0, 2)
    (g_ada_w, d_ada_w, nm_ada_w, nv_ada_w, g_ada_b, d_ada_b, nm_ada_b, nv_ada_b) = _ada_update(
        c_all, dmod_cols, dmod_all, ada_w, m_ada_w, v_ada_w, ada_b, m_ada_b, v_ada_b)

    res = dict(ada_w=[g_ada_w, d_ada_w, nm_ada_w, nv_ada_w], ada_b=[g_ada_b, d_ada_b, nm_ada_b, nv_ada_b])
    (r_od_in,) = _reduce_adam([(l_od_in, od_w_in[0], m_od_w_in[0], v_od_w_in[0])], "adam_od_w_in", after=[token])
    r_ev_out, r_od_out = _reduce_adam([(l_ev_out, ev_w_out[0], m_ev_w_out[0], v_ev_w_out[0]),
                                       (l_od_out, od_w_out[0], m_od_w_out[0], v_od_w_out[0])], "adam_w_out",
                                      after=[token])
    for name, r in (("od_w_in", r_od_in), ("ev_w_out", r_ev_out), ("od_w_out", r_od_out)):
        res[name] = [a[None] for a in r]
    res["od_w_a"], res["od_w_x"], res["ev_sg_w"] = _slots_adam(
        [(a_wa, od_w_a, m_od_w_a, v_od_w_a), (a_wx, od_w_x, m_od_w_x, v_od_w_x),
         (a_sgw, ev_sg_w, m_ev_sg_w, v_ev_sg_w)], "adam_gates", after=[token])
    small = dict(ln_g=(ln_g, m_ln_g, v_ln_g), ln_b=(ln_b, m_ln_b, v_ln_b),
                 ev_sg_ln_g=(ev_sg_ln_g, m_ev_sg_ln_g, v_ev_sg_ln_g),
                 ev_sg_ln_b=(ev_sg_ln_b, m_ev_sg_ln_b, v_ev_sg_ln_b),
                 ev_sink=(ev_sink, m_ev_sink, v_ev_sink), ev_sg_b=(ev_sg_b, m_ev_sg_b, v_ev_sg_b),
                 od_conv_w=(od_conv_w, m_od_conv_w, v_od_conv_w), od_conv_b=(od_conv_b, m_od_conv_b, v_od_conv_b),
                 od_b_a=(od_b_a, m_od_b_a, v_od_b_a), od_b_x=(od_b_x, m_od_b_x, v_od_b_x),
                 od_lam=(od_lam, m_od_lam, v_od_lam))
    small_out = _small_update(ga, gc, gd, gf, gb, ge, gsink, gbt, small)
    l_ev_in = _tail_wait(send_sems, recv_sems, part, land,
                         [r_od_in[0], r_od_out[0], res["od_w_x"][0], g_ada_w, small_out[0]])
    (r_ev_in,) = _reduce_adam([(l_ev_in, ev_w_in[0].T, m_ev_w_in[0].T, v_ev_w_in[0].T)], "adam_ev_w_in")
    res["ev_w_in"] = [a.T[None] for a in r_ev_in]
    loss = small_out[0][0, 0]
    for k, name in enumerate(SMALL_PARAMS):
        res[name] = small_out[1 + 4 * k:5 + 4 * k]

    order = ["ada_w", "ada_b", "ln_g", "ln_b", "ev_w_in", "ev_w_out", "ev_sink", "ev_sg_ln_g", "ev_sg_ln_b",
             "ev_sg_w", "ev_sg_b", "od_w_in", "od_conv_w", "od_conv_b", "od_w_a", "od_b_a", "od_w_x", "od_b_x",
             "od_lam", "od_w_out"]
    outs = [loss, grad_x.reshape(1, T, D)]
    for kind in range(4):
        outs += [res[name][kind] for name in order]
    return tuple(outs)
```

```python
import jax
import jax.numpy as jnp
from jax import lax
from jax.experimental import pallas as pl
from jax.experimental.pallas import tpu as pltpu

F32 = jnp.float32
BF16 = jnp.bfloat16

N_DEV = 8
D = 1024
N_HEADS = 8
HEAD_DIM = 64
ATTN_W = 512
SG_W = 512
SG_GROUPS = 8
SG_DIM = 64
BLK = 128
KVX_W = 1024
EV_IN = 2816
OD_IN = 2048
RNN_HEADS = 8
RNN_HD = 128
ALPHA = 4.0 ** 0.25
LN_EPS = 1e-5
NEG_INF = -1e30
RG_C = 8.0
ROPE_THETA = 500000.0
LR, B1, B2, EPS, WD, STEP = 0.001, 0.9, 0.999, 1e-08, 0.01, 10

LANE = 128
SUBLANE = 8
TM = 256
TMF = 512
TMO = 512
TS = 256
VMEM_LIMIT = 56 * 1024 * 1024

MESH = pl.DeviceIdType.MESH


def _pallas(body, **kw):
    return pl.pallas_call(body, **kw)


def _params(sem, vmem=VMEM_LIMIT):
    return pltpu.CompilerParams(dimension_semantics=sem, vmem_limit_bytes=vmem)


def _sigmoid(x):
    return 0.5 * jnp.tanh(0.5 * x) + 0.5


def _silu_and_grad(x):
    s = _sigmoid(x)
    return x * s, s * (1.0 + x * (1.0 - s))


def _dot(a, b):
    return jnp.dot(a.astype(BF16), b.astype(BF16), preferred_element_type=F32)


def _dot_nt(a, b):
    return lax.dot_general(a.astype(BF16), b.astype(BF16), (((1,), (1,)), ((), ())), preferred_element_type=F32)


def _dot_tn(a, b):
    return lax.dot_general(a.astype(BF16), b.astype(BF16), (((0,), (0,)), ((), ())), preferred_element_type=F32)


def _ln_fwd(z, g, b):
    mu = jnp.mean(z, axis=-1, keepdims=True)
    zc = z - mu
    var = jnp.mean(zc * zc, axis=-1, keepdims=True)
    rstd = lax.rsqrt(var + LN_EPS)
    xhat = zc * rstd
    return xhat * g + b, xhat, rstd


def _ln_bwd(dy, xhat, rstd, g):
    dxh = dy * g
    m1 = jnp.mean(dxh, axis=-1, keepdims=True)
    m2 = jnp.mean(dxh * xhat, axis=-1, keepdims=True)
    return rstd * (dxh - m1 - xhat * m2)


def _rowsum(v):
    return jnp.sum(v, axis=0, keepdims=True)


def _rope_fwd(t, c, s1, s2):
    return t * c + pltpu.roll(t, LANE - 8, 1) * s1 + pltpu.roll(t, 8, 1) * s2


def _rope_bwd(d, c, s1, s2):
    return d * c + pltpu.roll(d * s1, 8, 1) + pltpu.roll(d * s2, LANE - 8, 1)


def _adam(w, g, m, v):
    m2 = B1 * m + (1.0 - B1) * g
    v2 = B2 * v + (1.0 - B2) * (g * g)
    m_hat = m2 / (1.0 - B1 ** STEP)
    v_hat = v2 / (1.0 - B2 ** STEP)
    delta = -LR * (m_hat / (jnp.sqrt(v_hat) + EPS) + WD * w)
    return delta, m2, v2


def _tile(rows, width):
    return pl.BlockSpec((rows, width), lambda i: (i, 0))


def _full(shape):
    zeros = (0,) * len(shape)
    return pl.BlockSpec(shape, lambda i: zeros)


def _rev_tile(rows, width, n, reverse):
    if reverse:
        return pl.BlockSpec((rows, width), lambda i: (n - 1 - i, 0))
    return pl.BlockSpec((rows, width), lambda i: (i, 0))


def _halo_specs(rows, width, n, total_rows, reverse):
    per = rows // SUBLANE
    last = total_rows // SUBLANE - 1

    def tile_of(i):
        return (n - 1 - i) if reverse else i

    prev = pl.BlockSpec((SUBLANE, width), lambda i: (jnp.maximum(tile_of(i) * per - 1, 0), 0))
    nxt = pl.BlockSpec((SUBLANE, width), lambda i: (jnp.minimum((tile_of(i) + 1) * per, last), 0))
    return prev, nxt


def _my_pos():
    return lax.axis_index("x"), lax.axis_index("y"), lax.axis_index("c")


def _slot(px, py, pc):
    return 4 * px + 2 * py + pc


class _GatherComm:
    has_mid = True

    def __init__(self, arrs, mid_frac=0.5):
        self.arrs = list(arrs)
        self.n = len(self.arrs)
        self.mid_frac = mid_frac

    def out_shapes(self):
        return [jax.ShapeDtypeStruct((N_DEV,) + a.shape, a.dtype) for a in self.arrs]

    def sems(self):
        return [pltpu.SemaphoreType.DMA((7 * self.n,)), pltpu.SemaphoreType.DMA((7 * self.n,)),
                pltpu.SemaphoreType.DMA((self.n,))]

    def _parts(self, ins, outs, sems):
        send_sems, recv_sems, local_sems = sems
        x, y, c = _my_pos()
        me, sibling = (x, y, c), (x, y, 1 - c)
        chips = [(1 - x, y), (x, 1 - y), (1 - x, 1 - y)]

        def copy(a, k, block, to, src=None):
            dst = outs[a].at[_slot(*block)]
            return pltpu.make_async_remote_copy(
                src_ref=dst if src is None else src, dst_ref=dst,
                send_sem=send_sems.at[a * 7 + k], recv_sem=recv_sems.at[a * 7 + k],
                device_id=to, device_id_type=MESH)

        local = [pltpu.make_async_copy(ins[a], outs[a].at[_slot(*me)], local_sems.at[a]) for a in range(self.n)]
        first = []
        for a in range(self.n):
            first.append(copy(a, 0, me, sibling, src=ins[a]))
            first += [copy(a, 1 + j, me, (*chip, c), src=ins[a]) for j, chip in enumerate(chips)]
        ici_in = [copy(a, 1 + j, (*chip, c), me) for j, chip in enumerate(chips) for a in range(self.n)]
        passed = [copy(a, 4 + j, (*chip, c), sibling) for j, chip in enumerate(chips) for a in range(self.n)]
        d2d_in = []
        for a in range(self.n):
            d2d_in.append(copy(a, 0, sibling, me))
            d2d_in += [copy(a, 4 + j, (*chip, 1 - c), me) for j, chip in enumerate(chips)]
        return local, first, ici_in, passed, d2d_in

    def start(self, ins, outs, sems):
        local, first, _, _, _ = self._parts(ins, outs, sems)
        for cp in local + first:
            cp.start()

    def mid(self, ins, outs, sems):
        _, _, ici_in, passed, _ = self._parts(ins, outs, sems)
        for arrived, fw in zip(ici_in, passed):
            arrived.wait_recv()
            fw.start()

    def finish(self, ins, outs, sems):
        local, first, _, passed, d2d_in = self._parts(ins, outs, sems)
        for cp in d2d_in:
            cp.wait_recv()
        for cp in first + passed:
            cp.wait_send()
        for cp in local:
            cp.wait()


class _ExchangeComm:
    has_mid = False

    def __init__(self, arrs):
        self.arrs = list(arrs)
        self.n = len(self.arrs)

    def out_shapes(self):
        return [jax.ShapeDtypeStruct(a.shape, a.dtype) for a in self.arrs]

    def sems(self):
        return [pltpu.SemaphoreType.DMA((7 * self.n,)), pltpu.SemaphoreType.DMA((7 * self.n,)),
                pltpu.SemaphoreType.DMA((self.n,))]

    def _copies(self, ins, outs, sems):
        send_sems, recv_sems, local_sems = sems
        x, y, c = _my_pos()
        mine = _slot(x, y, c)
        copies = [pltpu.make_async_copy(ins[a].at[mine], outs[a].at[mine], local_sems.at[a]) for a in range(self.n)]
        for k in range(1, N_DEV):
            px = (1 - x) if (k & 4) else x
            py = (1 - y) if (k & 2) else y
            pc = (1 - c) if (k & 1) else c
            for a in range(self.n):
                copies.append(pltpu.make_async_remote_copy(
                    src_ref=ins[a].at[_slot(px, py, pc)], dst_ref=outs[a].at[mine],
                    send_sem=send_sems.at[a * 7 + k - 1], recv_sem=recv_sems.at[a * 7 + k - 1],
                    device_id=(px, py, pc), device_id_type=MESH))
        return copies

    def start(self, ins, outs, sems):
        for cp in self._copies(ins, outs, sems):
            cp.start()

    def finish(self, ins, outs, sems):
        for cp in self._copies(ins, outs, sems):
            cp.wait()


class _BothComm:
    has_mid = True

    def __init__(self, first, second):
        self.parts = (first, second)
        self.arrs = first.arrs + second.arrs
        self.n = first.n + second.n
        self.mid_frac = second.mid_frac

    def out_shapes(self):
        return self.parts[0].out_shapes() + self.parts[1].out_shapes()

    def sems(self):
        return self.parts[0].sems() + self.parts[1].sems()

    def _each(self, ins, outs, sems):
        a, b = self.parts
        return ((a, ins[:a.n], outs[:a.n], sems[:3]), (b, ins[a.n:], outs[a.n:], sems[3:]))

    def start(self, ins, outs, sems):
        for cm, i_, o_, s_ in self._each(ins, outs, sems):
            cm.start(i_, o_, s_)

    def mid(self, ins, outs, sems):
        for cm, i_, o_, s_ in self._each(ins, outs, sems):
            if cm.has_mid:
                cm.mid(i_, o_, s_)

    def finish(self, ins, outs, sems):
        for cm, i_, o_, s_ in self._each(ins, outs, sems):
            cm.finish(i_, o_, s_)


def _fused_call(body, comm, operands, *, name, grid, in_specs, out_specs, out_shape, scratch_shapes=(),
                semantics=("arbitrary",)):
    n_in, n_out, n_scr = len(in_specs), len(out_specs), len(scratch_shapes)
    if comm is None:
        res = _pallas(body, name=name, grid=grid, in_specs=list(in_specs), out_specs=list(out_specs),
                      out_shape=list(out_shape), scratch_shapes=list(scratch_shapes),
                      compiler_params=_params(semantics))(*operands)
        return list(res), []
    k = comm.n
    steps = grid[0]

    def wrapped(*refs):
        ins, cins = refs[:n_in], refs[n_in:n_in + k]
        outs = refs[n_in + k:n_in + k + n_out]
        couts = refs[n_in + k + n_out:n_in + 2 * k + n_out]
        rest = refs[n_in + 2 * k + n_out:]
        scratch, sems = rest[:n_scr], rest[n_scr:]
        i = pl.program_id(0)

        @pl.when(i == 0)
        def _():
            comm.start(cins, couts, sems)

        body(*ins, *outs, *scratch)

        if comm.has_mid:
            @pl.when(i == int(steps * comm.mid_frac))
            def _():
                comm.mid(cins, couts, sems)

        @pl.when(i == steps - 1)
        def _():
            comm.finish(cins, couts, sems)

    any_spec = pl.BlockSpec(memory_space=pl.ANY)
    res = _pallas(wrapped, name=name, grid=grid, in_specs=list(in_specs) + [any_spec] * k,
                  out_specs=list(out_specs) + [any_spec] * k, out_shape=list(out_shape) + comm.out_shapes(),
                  scratch_shapes=list(scratch_shapes) + comm.sems(),
                  compiler_params=_params(("arbitrary",)))(*operands, *comm.arrs)
    return list(res[:n_out]), list(res[n_out:])


def _head_gather(c, ada_w, big, to_cast, vec_parts):
    cols = ada_w.shape[2]
    g_c, g_big = _GatherComm([c]), _GatherComm(big)
    g_mod = _GatherComm([jax.ShapeDtypeStruct((2, N_DEV, cols), F32)])
    g_vec = _GatherComm([jax.ShapeDtypeStruct((VEC_ROWS, LANE), F32)])
    nb, nc, nv = g_big.n, len(to_cast), len(vec_parts)

    def body(*refs):
        c_ref, w_ref = refs[0], refs[1]
        vec_in = refs[2:2 + nv]
        cast_in = refs[2 + nv:2 + nv + nc]
        big_in = refs[2 + nv + nc:2 + nv + nc + nb]
        outs = refs[2 + nv + nc + nb:]
        c_all_ref, mod_all_ref, vec_all_ref = outs[0], outs[1], outs[2]
        cast_out = outs[3:3 + nc]
        big_out = outs[3 + nc:3 + nc + nb]
        part_ref, pack_ref = outs[3 + nc + nb], outs[4 + nc + nb]
        sems = outs[5 + nc + nb:]
        s_c, s_mod, s_big, s_vec = sems[0:3], sems[3:6], sems[6:9], sems[9:12]
        g_c.start([c_ref], [c_all_ref], s_c)
        g_big.start(big_in, big_out, s_big)
        pack_ref[...] = jnp.zeros_like(pack_ref)
        row = 0
        for ref, (_, nrows) in zip(vec_in, VEC_LAYOUT):
            pack_ref[row:row + nrows, :] = ref[0] if len(ref.shape) == 3 else ref[...]
            row += nrows
        g_vec.start([pack_ref], [vec_all_ref], s_vec)
        g_c.mid([c_ref], [c_all_ref], s_c)
        g_c.finish([c_ref], [c_all_ref], s_c)
        cv = c_all_ref[:, 0, :]
        cond = cv * _sigmoid(cv)
        for l in range(2):
            part_ref[l] = _dot(cond, w_ref[l])
        g_mod.start([part_ref], [mod_all_ref], s_mod)
        for src, dst in zip(cast_in, cast_out):
            dst[...] = src[...].astype(BF16)
        for g, ins, outs_, sm in ((g_vec, [pack_ref], [vec_all_ref], s_vec), (g_mod, [part_ref], [mod_all_ref], s_mod),
                                  (g_big, big_in, big_out, s_big)):
            g.mid(ins, outs_, sm)
            g.finish(ins, outs_, sm)

    any_spec = pl.BlockSpec(memory_space=pl.ANY)
    vmem_spec = pl.BlockSpec(memory_space=pltpu.VMEM)
    res = _pallas(
        body, name="head_gather",
        out_shape=(g_c.out_shapes() + g_mod.out_shapes() + g_vec.out_shapes()
                   + [jax.ShapeDtypeStruct(a.shape, BF16) for a in to_cast] + g_big.out_shapes()),
        in_specs=[vmem_spec] * (2 + nv + nc) + [any_spec] * nb,
        out_specs=[vmem_spec] * (3 + nc) + [any_spec] * nb,
        scratch_shapes=[pltpu.VMEM((2, N_DEV, cols), F32), pltpu.VMEM((VEC_ROWS, LANE), F32)]
        + g_c.sems() + g_mod.sems() + g_big.sems() + g_vec.sems(),
        compiler_params=pltpu.CompilerParams(vmem_limit_bytes=VMEM_LIMIT),
    )(c, ada_w, *vec_parts, *to_cast, *big)
    return res[0], res[1], res[2], list(res[3 + nc:]), list(res[3:3 + nc])


def _ada_update(c_all, dmod_cols, dmod_all, ada_w, m_w, v_w, ada_b, m_b, v_b):
    cols = ada_w.shape[2]
    nb = ada_b.shape[1]

    def body(c_ref, dmc_ref, dma_ref, w_ref, mw_ref, vw_ref, b_ref, mb_ref, vb_ref,
             gw_ref, dw_ref, nmw_ref, nvw_ref, gb_ref, db_ref, nmb_ref, nvb_ref):
        cv = c_ref[...]
        cond = cv * _sigmoid(cv)
        for l in range(2):
            g = _dot_tn(cond, dmc_ref[l])
            gw_ref[l] = g
            dlt, m2, v2 = _adam(w_ref[l], g, mw_ref[l], vw_ref[l])
            dw_ref[l] = dlt
            nmw_ref[l] = m2
            nvw_ref[l] = v2
        gb = dma_ref[0]
        for i in range(1, N_DEV):
            gb = gb + dma_ref[i]
        gb_ref[...] = gb
        dlt, m2, v2 = _adam(b_ref[...], gb, mb_ref[...], vb_ref[...])
        db_ref[...] = dlt
        nmb_ref[...] = m2
        nvb_ref[...] = v2

    wspec = _full((2, D, cols))
    bspec = _full((2, nb))
    wshape = jax.ShapeDtypeStruct((2, D, cols), F32)
    bshape = jax.ShapeDtypeStruct((2, nb), F32)
    return _pallas(
        body, name="ada_update", grid=(1,),
        in_specs=[_full((N_DEV, D)), _full((2, N_DEV, cols)), _full((N_DEV, 2, nb)),
                  wspec, wspec, wspec, bspec, bspec, bspec],
        out_specs=[wspec] * 4 + [bspec] * 4,
        out_shape=[wshape] * 4 + [bshape] * 4,
        compiler_params=_params(("arbitrary",)),
    )(c_all, dmod_cols, dmod_all, ada_w, m_w, v_w, ada_b, m_b, v_b)


def _ev_in(x, mod, w_in, rc, rs1, rs2, comm=None):
    T = x.shape[0]

    def body(x_ref, mod_ref, w_ref, c_ref, s1_ref, s2_ref, q_ref, kv_ref, su_ref, sv_ref, g_ref):
        h = x_ref[...] * (1.0 + mod_ref[1:2, :]) + mod_ref[0:1, :]
        p = _dot_nt(h, w_ref[...])
        c, s1, s2 = c_ref[...], s1_ref[...], s2_ref[...]
        for j in range(ATTN_W // LANE):
            qr = _rope_fwd(p[:, j * LANE:(j + 1) * LANE], c, s1, s2)
            q_ref[:, j * LANE:(j + 1) * LANE] = (qr * (HEAD_DIM ** -0.5)).astype(BF16)
        low = lax.broadcasted_iota(jnp.int32, (TMF, LANE), 1) < HEAD_DIM
        for j, val in enumerate((_rope_fwd(p[:, 512:640], c, s1, s2), p[:, 640:768])):
            swapped = pltpu.roll(val, HEAD_DIM, 1)
            tiles = (jnp.where(low, val, 0.0), jnp.where(low, 0.0, swapped),
                     jnp.where(low, swapped, 0.0), jnp.where(low, 0.0, val))
            for k, tile in enumerate(tiles):
                kv_ref[:, (4 * j + k) * LANE:(4 * j + k + 1) * LANE] = tile.astype(BF16)
        su_ref[...] = p[:, 768:1280].astype(BF16)
        sv_ref[...] = p[:, 1280:1792].astype(BF16)
        g_ref[...] = p[:, 1792:2816].astype(BF16)

    sh = lambda w: jax.ShapeDtypeStruct((T, w), BF16)
    return _fused_call(
        body, comm, (x, mod, w_in, rc, rs1, rs2), name="ev_in", grid=(T // TMF,),
        in_specs=[_tile(TMF, D), _full((3, D)), _full((EV_IN, D)), _tile(TMF, LANE), _tile(TMF, LANE),
                  _tile(TMF, LANE)],
        out_specs=[_tile(TMF, ATTN_W), _tile(TMF, KVX_W), _tile(TMF, SG_W), _tile(TMF, SG_W), _tile(TMF, D)],
        out_shape=[sh(ATTN_W), sh(KVX_W), sh(SG_W), sh(SG_W), sh(D)], semantics=("parallel",))


def _band_specs(width, nb):
    return [pl.BlockSpec((BLK, width), lambda n: (jnp.maximum(n - 1, 0), 0)),
            pl.BlockSpec((BLK, width), lambda n: (n, 0)),
            pl.BlockSpec((BLK, width), lambda n: (jnp.minimum(n + 1, nb - 1), 0))]


def _band_bias(bias_ref, n, nb):
    rows = lax.broadcasted_iota(jnp.int32, (3 * BLK, 1), 0)
    outside = ((rows < BLK) & (n == 0)) | ((rows >= 2 * BLK) & (n == nb - 1))
    return bias_ref[...] + jnp.where(outside, NEG_INF, 0.0)


def _lane_tile(ref, t):
    return ref[:, t * LANE:(t + 1) * LANE]


def _split_bf16(v):
    hi = v.astype(BF16)
    return hi, (v - hi.astype(F32)).astype(BF16)


def _group_mean(v, a_ref, exact_bf16=False):
    hi, lo = _split_bf16(v)
    a = a_ref[...]
    out = []
    for t in range(SG_W // (2 * LANE)):
        sl = slice(t * 2 * LANE, (t + 1) * 2 * LANE)
        r = jnp.dot(hi[:, sl], a, preferred_element_type=F32)
        if not exact_bf16:
            r = r + jnp.dot(lo[:, sl], a, preferred_element_type=F32)
        out.append(r)
    return jnp.concatenate(out, axis=-1)


def _sg_core(sv_ref, lng, lnb, a_ref, w_ref, bfull_ref):
    svf = sv_ref[...].astype(F32)
    xc = svf - _group_mean(svf, a_ref, exact_bf16=True)
    rstd = lax.rsqrt(_group_mean(xc * xc, a_ref) + LN_EPS)
    xhat = xc * rstd
    vb = (xhat * lng + lnb).astype(BF16)
    low = lax.broadcasted_iota(jnp.int32, (BLK, LANE), 1) < SG_DIM
    tiles = []
    for t in range(SG_W // LANE):
        v2 = vb[:, t * LANE:(t + 1) * LANE]
        r0 = jnp.dot(w_ref[2 * t], v2, preferred_element_type=F32)
        r1 = jnp.dot(w_ref[2 * t + 1], v2, preferred_element_type=F32)
        tiles.append(jnp.where(low, r0, r1))
    svm = jnp.concatenate(tiles, axis=-1) + bfull_ref[...]
    return xhat, rstd, vb, svm


def _mix0_fwd(q, kvx, su, sv, g0, sink_l, bias, a128, sg_lng, sg_lnb, sg_w, sg_bfull, comm=None):
    T = q.shape[0]
    nb = T // BLK

    def body(q_ref, kp_ref, kc_ref, kn_ref, su_ref, sv_ref, g_ref, sink_ref, bias_ref, a_ref, lng_ref, lnb_ref,
             w_ref, bfull_ref, ycat_ref, y0_ref, lse_ref):
        n = pl.program_id(0)
        bias = _band_bias(bias_ref, n, nb)
        kvx = jnp.concatenate([kp_ref[...], kc_ref[...], kn_ref[...]], axis=0)
        tiles = []
        for t in range(ATTN_W // LANE):
            qt = _lane_tile(q_ref, t)
            acc = None
            for par in range(2):
                h = 2 * t + par
                kt = 2 * (h // 4) + par
                ke = kvx[:, kt * LANE:(kt + 1) * LANE]
                ve = kvx[:, (4 + kt) * LANE:(5 + kt) * LANE]
                st = _dot_nt(ke, qt) + bias
                sk = _lane_tile(sink_ref, h)
                m = jnp.maximum(jnp.max(st, axis=0, keepdims=True), sk)
                p = jnp.exp(st - m)
                denom = jnp.sum(p, axis=0, keepdims=True) + jnp.exp(sk - m)
                contrib = _dot_tn(p * (1.0 / denom), ve)
                acc = contrib if acc is None else acc + contrib
                lse_ref[0, :, h * LANE:(h + 1) * LANE] = m + jnp.log(denom)
            tiles.append(acc)
        _, _, _, svm = _sg_core(sv_ref, lng_ref[...], lnb_ref[...], a_ref, w_ref, bfull_ref)
        tiles.append(su_ref[...].astype(F32) * svm)
        ycat = jnp.concatenate(tiles, axis=-1)
        gf = g_ref[...].astype(F32)
        ycat_ref[...] = ycat.astype(BF16)
        y0_ref[...] = (ycat * (gf * _sigmoid(gf))).astype(BF16)

    return _fused_call(
        body, comm, (q, kvx, kvx, kvx, su, sv, g0, sink_l, bias, a128, sg_lng, sg_lnb, sg_w, sg_bfull),
        name="mix0_fwd", grid=(nb,),
        in_specs=[_tile(BLK, ATTN_W)] + _band_specs(KVX_W, nb) + [
            _tile(BLK, SG_W), _tile(BLK, SG_W), _tile(BLK, D), _full((1, N_HEADS * LANE)), _full((3 * BLK, LANE)),
            _full((2 * LANE, 2 * LANE)),_full((1, SG_W)), _full((1, SG_W)), _full((SG_GROUPS, BLK, BLK)),
            _full((BLK, SG_W))],
        out_specs=[_tile(BLK, D), _tile(BLK, D), pl.BlockSpec((1, 1, N_HEADS * LANE), lambda n: (n, 0, 0))],
        out_shape=[jax.ShapeDtypeStruct((T, D), BF16), jax.ShapeDtypeStruct((T, D), BF16),
                   jax.ShapeDtypeStruct((nb, 1, N_HEADS * LANE), F32)], semantics=("parallel",))


def _ev_out(y0, w_out, x, mod, lnp):
    T = x.shape[0]

    def body(y_ref, w_ref, x_ref, mod_ref, ln_ref, out_ref, z_ref, x1_ref):
        out = _dot(y_ref[...], w_ref[...])
        z = ALPHA * x_ref[...] + mod_ref[2:3, :] * out
        x1, _, _ = _ln_fwd(z, ln_ref[0:1, :], ln_ref[1:2, :])
        out_ref[...] = out.astype(BF16)
        z_ref[...] = z
        x1_ref[...] = x1

    return _pallas(
        body, name="ev_out", grid=(T // TMF,),
        in_specs=[_tile(TMF, D), _full((D, D)), _tile(TMF, D), _full((3, D)), _full((2, D))],
        out_specs=[_tile(TMF, D)] * 3,
        out_shape=[jax.ShapeDtypeStruct((T, D), BF16), jax.ShapeDtypeStruct((T, D), F32),
                   jax.ShapeDtypeStruct((T, D), F32)],
        compiler_params=_params(("parallel",)),
    )(y0, w_out, x, mod, lnp)


def _od_in(x1, mod, w_in):
    T = x1.shape[0]

    def body(x_ref, mod_ref, w_ref, xr_ref, g_ref):
        h = x_ref[...] * (1.0 + mod_ref[1:2, :]) + mod_ref[0:1, :]
        p = _dot(h, w_ref[...])
        xr_ref[...] = p[:, :D]
        g_ref[...] = p[:, D:].astype(BF16)

    return _pallas(
        body, name="od_in", grid=(T // TMF,),
        in_specs=[_tile(TMF, D), _full((3, D)), _full((D, OD_IN))],
        out_specs=[_tile(TMF, D), _tile(TMF, D)],
        out_shape=[jax.ShapeDtypeStruct((T, D), F32), jax.ShapeDtypeStruct((T, D), BF16)],
        compiler_params=_params(("parallel",)),
    )(x1, mod, w_in)


def _ext_rows(prev_ref, cur, next_ref, j, n):
    prev = jnp.where(j > 0, prev_ref[...], 0.0)
    nxt = jnp.where(j < n - 1, next_ref[...], 0.0)
    return jnp.concatenate([prev, cur, nxt], axis=0)


def _shift_rows(ext, off, rows):
    total = ext.shape[0]
    if off == 0:
        return ext[SUBLANE:SUBLANE + rows, :]
    return pltpu.roll(ext, (-off) % total, 0)[SUBLANE:SUBLANE + rows, :]


def _conv_fwd(ext, cw, cb, rows):
    xc = cb
    for k in range(4):
        xc = xc + cw[k:k + 1, :] * _shift_rows(ext, k - 2, rows)
    return xc


def _gates(xc, wa_ref, wx_ref, ba, bx, lam):
    pr, pi = [], []
    for h in range(RNN_HEADS):
        xh = xc[:, h * RNN_HD:(h + 1) * RNN_HD].astype(BF16)
        pr.append(_dot(xh, wa_ref[h]))
        pi.append(_dot(xh, wx_ref[h]))
    r = _sigmoid(jnp.concatenate(pr, axis=-1) + ba)
    ig = _sigmoid(jnp.concatenate(pi, axis=-1) + bx)
    sp = jnp.maximum(-lam, 0.0) + jnp.log(1.0 + jnp.exp(-jnp.abs(lam)))
    neg_log_a = RG_C * r * sp
    a = jnp.exp(-neg_log_a)
    s2 = (1.0 + a * a) * jnp.tanh(neg_log_a)
    inv_s = lax.rsqrt(jnp.maximum(s2, 1e-30))
    return r, ig, sp, a, s2 * inv_s, inv_s


def _scan_tile(a_ref, b_ref, o_ref, carry_ref, rows, reverse):
    ridx = lax.broadcasted_iota(jnp.int32, (SUBLANE, D), 0)
    groups = rows // SUBLANE

    def group(gi, h):
        g = (groups - 1 - gi) if reverse else gi
        off = pl.multiple_of(g * SUBLANE, SUBLANE)
        a = a_ref[pl.ds(off, SUBLANE), :]
        b = b_ref[pl.ds(off, SUBLANE), :]
        for sh in (1, 2, 4):
            if reverse:
                keep = ridx < SUBLANE - sh
                a_p = jnp.where(keep, pltpu.roll(a, SUBLANE - sh, 0), 1.0)
                b_p = jnp.where(keep, pltpu.roll(b, SUBLANE - sh, 0), 0.0)
            else:
                keep = ridx >= sh
                a_p = jnp.where(keep, pltpu.roll(a, sh, 0), 1.0)
                b_p = jnp.where(keep, pltpu.roll(b, sh, 0), 0.0)
            b = b + a * b_p
            a = a * a_p
        hh = b + a * h
        o_ref[pl.ds(off, SUBLANE), :] = hh
        return hh[0:1, :] if reverse else hh[SUBLANE - 1:SUBLANE, :]

    carry_ref[...] = lax.fori_loop(0, groups, group, carry_ref[...])


def _rglru_fwd(xr, cw, cb, wa, wx, ba, bx, lam, reverse, name):
    T = xr.shape[0]
    n = T // TS
    prev_spec, next_spec = _halo_specs(TS, D, n, T, reverse)

    def body(prev_ref, cur_ref, next_ref, cw_ref, cb_ref, wa_ref, wx_ref, ba_ref, bx_ref, lam_ref,
             h_ref, a_ref, s_ref, r_ref, ig_ref, xc_ref, b_s, carry):
        i = pl.program_id(0)
        j = (n - 1 - i) if reverse else i

        @pl.when(i == 0)
        def _():
            carry[...] = jnp.zeros_like(carry)

        ext = _ext_rows(prev_ref, cur_ref[...], next_ref, j, n)
        xc = _conv_fwd(ext, cw_ref[...], cb_ref[...], TS)
        r, ig, _, a, s, _ = _gates(xc, wa_ref, wx_ref, ba_ref[...], bx_ref[...], lam_ref[...])
        s_ref[...] = s
        r_ref[...] = r.astype(BF16)
        ig_ref[...] = ig.astype(BF16)
        xc_ref[...] = xc.astype(BF16)
        a_ref[...] = a
        b_s[...] = s * ig * xc
        _scan_tile(a_ref, b_s, h_ref, carry, TS, reverse)

    wspec = _full((RNN_HEADS, RNN_HD, RNN_HD))
    cur = _rev_tile(TS, D, n, reverse)
    f32 = jax.ShapeDtypeStruct((T, D), F32)
    b16 = jax.ShapeDtypeStruct((T, D), BF16)
    return _pallas(
        body, name=name, grid=(n,),
        in_specs=[prev_spec, cur, next_spec, _full((4, D)), _full((1, D)),
                  wspec, wspec, _full((1, D)), _full((1, D)), _full((1, D))],
        out_specs=[cur] * 6,
        out_shape=[f32, f32, f32, b16, b16, b16],
        scratch_shapes=[pltpu.VMEM((TS, D), F32), pltpu.VMEM((1, D), F32)],
        compiler_params=_params(("arbitrary",)),
    )(xr, xr, xr, cw, cb, wa, wx, ba, bx, lam)


def _od_out(hf, hb, g1, w_out, x1, tgt, mod, lnp):
    T = x1.shape[0]

    def body(hf_ref, hb_ref, g_ref, w_ref, x_ref, t_ref, mod_ref, ln_ref,
             dh_ref, dg_ref, dx_ref, dwb_ref, vec_ref, dw_ref):
        i = pl.program_id(0)

        @pl.when(i == 0)
        def _():
            dw_ref[...] = jnp.zeros_like(dw_ref)
            vec_ref[...] = jnp.zeros_like(vec_ref)

        hs = hf_ref[...] + hb_ref[...]
        sg, dsg = _silu_and_grad(g_ref[...].astype(F32))
        yr = (hs * sg).astype(BF16)
        w = w_ref[...]
        out = _dot(yr, w)
        gate = mod_ref[2:3, :]
        z = ALPHA * x_ref[...] + gate * out
        lng = ln_ref[0:1, :]
        x2, xhat, rstd = _ln_fwd(z, lng, ln_ref[1:2, :])
        diff = x2 - t_ref[...]
        vec_ref[3:4, 0:LANE] += 0.5 * jnp.sum(diff * diff) * (1.0 / D)
        dx2 = diff * (1.0 / D)
        dz = _ln_bwd(dx2, xhat, rstd, lng)
        vec_ref[0:1, :] += _rowsum(dx2 * xhat)
        vec_ref[1:2, :] += _rowsum(dx2)
        vec_ref[2:3, :] += _rowsum(dz * out)
        dout = (dz * gate).astype(BF16)
        dyr = _dot_nt(dout, w)
        dw_ref[...] += _dot_tn(yr, dout)
        dh_ref[...] = dyr * sg
        dg_ref[...] = (dyr * hs * dsg).astype(BF16)
        dx_ref[...] = ALPHA * dz

        @pl.when(i == T // TMO - 1)
        def _():
            dwb_ref[...] = dw_ref[...].astype(BF16)

    return _pallas(
        body, name="od_out", grid=(T // TMO,),
        in_specs=[_tile(TMO, D), _tile(TMO, D), _tile(TMO, D), _full((D, D)), _tile(TMO, D), _tile(TMO, D),
                  _full((3, D)), _full((2, D))],
        out_specs=[_tile(TMO, D), _tile(TMO, D), _tile(TMO, D), _full((D, D)), _full((SUBLANE, D))],
        out_shape=[jax.ShapeDtypeStruct((T, D), F32), jax.ShapeDtypeStruct((T, D), BF16),
                   jax.ShapeDtypeStruct((T, D), F32), jax.ShapeDtypeStruct((D, D), BF16),
                   jax.ShapeDtypeStruct((SUBLANE, D), F32)],
        scratch_shapes=[pltpu.VMEM((D, D), F32)],
        compiler_params=_params(("arbitrary",)),
    )(hf, hb, g1, w_out, x1, tgt, mod, lnp)


def _rglru_bwd(fwd, dh, wa, wx, lam, reverse, name, comm=None):
    h, a_all, s_all, r_all, ig_all, xc_all = fwd
    T = h.shape[0]
    n = T // TS
    adj_rev = not reverse
    hprev_spec, hnext_spec = _halo_specs(TS, D, n, T, adj_rev)
    h_halo_spec = hnext_spec if reverse else hprev_spec

    def body(dh_ref, h_ref, hh_ref, a_ref, s_ref, r_ref, ig_ref, xc_ref, wa_ref, wx_ref, lam_ref,
             dxc_ref, dwa_ref, dwx_ref, vec_ref, a_s, l_s, carry, a_edge):
        i = pl.program_id(0)
        j = (n - 1 - i) if adj_rev else i

        @pl.when(i == 0)
        def _():
            carry[...] = jnp.zeros_like(carry)
            a_edge[...] = jnp.zeros_like(a_edge)
            dwa_ref[...] = jnp.zeros_like(dwa_ref)
            dwx_ref[...] = jnp.zeros_like(dwx_ref)
            vec_ref[...] = jnp.zeros_like(vec_ref)

        lam = lam_ref[...]
        sp = jnp.maximum(-lam, 0.0) + jnp.log(1.0 + jnp.exp(-jnp.abs(lam)))
        a, s = a_ref[...], s_ref[...]
        inv_s = lax.rsqrt(jnp.maximum(s * s, 1e-30))
        r, ig = r_ref[...].astype(F32), ig_ref[...].astype(F32)
        xcb = xc_ref[...]
        xc = xcb.astype(F32)

        rows = lax.broadcasted_iota(jnp.int32, (TS, D), 0)
        hcur = h_ref[...]
        if reverse:
            a_sh = jnp.where(rows == 0, a_edge[...], pltpu.roll(a, 1, 0))
            halo = jnp.where(j < n - 1, hh_ref[0:1, :], 0.0)
            h_nb = jnp.where(rows == TS - 1, halo, pltpu.roll(hcur, TS - 1, 0))
        else:
            a_sh = jnp.where(rows == TS - 1, a_edge[...], pltpu.roll(a, TS - 1, 0))
            halo = jnp.where(j > 0, hh_ref[SUBLANE - 1:SUBLANE, :], 0.0)
            h_nb = jnp.where(rows == 0, halo, pltpu.roll(hcur, 1, 0))
        a_s[...] = a_sh
        _scan_tile(a_s, dh_ref, l_s, carry, TS, adj_rev)
        a_edge[...] = a[TS - 1:TS, :] if reverse else a[0:1, :]

        lm = l_s[...]
        da = lm * h_nb
        di = lm * s * xc
        dxc = lm * s * ig
        ds = lm * ig * xc
        dlog_a = a * (da - ds * a * inv_s)
        dr = (-RG_C) * sp * dlog_a
        dsp = _rowsum((-RG_C) * r * dlog_a)
        dpr = dr * r * (1.0 - r)
        dpi = di * ig * (1.0 - ig)
        vec_ref[0:1, :] += _rowsum(dpr)
        vec_ref[1:2, :] += _rowsum(dpi)
        vec_ref[2:3, :] += dsp * (-_sigmoid(-lam))
        parts = []
        for hd in range(RNN_HEADS):
            sl = slice(hd * RNN_HD, (hd + 1) * RNN_HD)
            xh = xcb[:, sl]
            dprh = dpr[:, sl].astype(BF16)
            dpih = dpi[:, sl].astype(BF16)
            parts.append(_dot_nt(dprh, wa_ref[hd]) + _dot_nt(dpih, wx_ref[hd]))
            dwa_ref[hd] += _dot_tn(xh, dprh)
            dwx_ref[hd] += _dot_tn(xh, dpih)
        dxc_ref[...] = dxc + jnp.concatenate(parts, axis=-1)

    wspec = _full((RNN_HEADS, RNN_HD, RNN_HD))
    cur = _rev_tile(TS, D, n, adj_rev)
    return _fused_call(
        body, comm, (dh, h, h, a_all, s_all, r_all, ig_all, xc_all, wa, wx, lam), name=name, grid=(n,),
        in_specs=[cur, cur, h_halo_spec, cur, cur, cur, cur, cur, wspec, wspec, _full((1, D))],
        out_specs=[cur, wspec, wspec, _full((SUBLANE, D))],
        out_shape=[jax.ShapeDtypeStruct((T, D), F32),
                   jax.ShapeDtypeStruct((RNN_HEADS, RNN_HD, RNN_HD), F32),
                   jax.ShapeDtypeStruct((RNN_HEADS, RNN_HD, RNN_HD), F32),
                   jax.ShapeDtypeStruct((SUBLANE, D), F32)],
        scratch_shapes=[pltpu.VMEM((TS, D), F32)] * 2 + [pltpu.VMEM((1, D), F32)] * 2)


def _od_in_bwd(dxcf, dxcb, xr, dg1, x1, dx1p, mod, w_in, cw, comm=None):
    T = x1.shape[0]
    n = T // TMO
    slab = OD_IN // N_DEV
    prev_spec, next_spec = _halo_specs(TMO, D, n, T, False)

    def body(fp_ref, fc_ref, fn_ref, bp_ref, bc_ref, bn_ref, xr_ref, dg_ref, x1_ref, dxp_ref,
             mod_ref, w_ref, cw_ref, dx_ref, dwb_ref, vec_ref, dw_ref):
        i = pl.program_id(0)

        @pl.when(i == 0)
        def _():
            dw_ref[...] = jnp.zeros_like(dw_ref)
            vec_ref[...] = jnp.zeros_like(vec_ref)

        dcur = fc_ref[...] + bc_ref[...]
        dprev = jnp.where(i > 0, fp_ref[...] + bp_ref[...], 0.0)
        dnext = jnp.where(i < n - 1, fn_ref[...] + bn_ref[...], 0.0)
        dext = jnp.concatenate([dprev, dcur, dnext], axis=0)
        xr_v = xr_ref[...]
        cw_v = cw_ref[...]
        dxr = None
        for k in range(4):
            shifted = _shift_rows(dext, 2 - k, TMO)
            term = cw_v[k:k + 1, :] * shifted
            dxr = term if dxr is None else dxr + term
            vec_ref[k:k + 1, :] += _rowsum(shifted * xr_v)
        vec_ref[4:5, :] += _rowsum(dcur)
        dp = jnp.concatenate([dxr.astype(BF16), dg_ref[...]], axis=-1)
        x1v = x1_ref[...]
        scale1 = 1.0 + mod_ref[1:2, :]
        h1 = (x1v * scale1 + mod_ref[0:1, :]).astype(BF16)
        dh1 = _dot_nt(dp, w_ref[...])
        dw_ref[...] += _dot_tn(h1, dp)
        dx_ref[...] = dxp_ref[...] + dh1 * scale1
        vec_ref[5:6, :] += _rowsum(dh1)
        vec_ref[6:7, :] += _rowsum(dh1 * x1v)

        @pl.when(i == n - 1)
        def _():
            for j in range(N_DEV):
                dwb_ref[j] = dw_ref[:, j * slab:(j + 1) * slab].astype(BF16)

    t = _tile(TMO, D)
    return _fused_call(
        body, comm, (dxcf, dxcf, dxcf, dxcb, dxcb, dxcb, xr, dg1, x1, dx1p, mod, w_in, cw),
        name="od_in_bwd", grid=(n,),
        in_specs=[prev_spec, t, next_spec, prev_spec, t, next_spec, t, t, t, t,
                  _full((3, D)), _full((D, OD_IN)), _full((4, D))],
        out_specs=[t, _full((N_DEV, D, slab)), _full((SUBLANE, D))],
        out_shape=[jax.ShapeDtypeStruct((T, D), F32), jax.ShapeDtypeStruct((N_DEV, D, slab), BF16),
                   jax.ShapeDtypeStruct((SUBLANE, D), F32)],
        scratch_shapes=[pltpu.VMEM((D, OD_IN), F32)])


def _ev_out_bwd(dx1, z0, out0, y0, ycat, g0, w_out, mod, lnp):
    T = dx1.shape[0]

    def body(dx_ref, z_ref, out_ref, y0_ref, yc_ref, g_ref, w_ref, mod_ref, ln_ref,
             dxp_ref, dyc_ref, dg_ref, dwb_ref, vec_ref, dw_ref):
        i = pl.program_id(0)

        @pl.when(i == 0)
        def _():
            dw_ref[...] = jnp.zeros_like(dw_ref)
            vec_ref[...] = jnp.zeros_like(vec_ref)

        lng = ln_ref[0:1, :]
        _, xhat, rstd = _ln_fwd(z_ref[...], lng, ln_ref[1:2, :])
        dy = dx_ref[...]
        dz = _ln_bwd(dy, xhat, rstd, lng)
        vec_ref[0:1, :] += _rowsum(dy * xhat)
        vec_ref[1:2, :] += _rowsum(dy)
        vec_ref[2:3, :] += _rowsum(dz * out_ref[...].astype(F32))
        dout = (dz * mod_ref[2:3, :]).astype(BF16)
        dy0 = _dot_nt(dout, w_ref[...])
        dw_ref[...] += _dot_tn(y0_ref[...], dout)
        sg, dsg = _silu_and_grad(g_ref[...].astype(F32))
        dyc_ref[...] = (dy0 * sg).astype(BF16)
        dg_ref[...] = (dy0 * yc_ref[...].astype(F32) * dsg).astype(BF16)
        dxp_ref[...] = ALPHA * dz

        @pl.when(i == T // TMO - 1)
        def _():
            dwb_ref[...] = dw_ref[...].astype(BF16)

    t = _tile(TMO, D)
    return _pallas(
        body, name="ev_out_bwd", grid=(T // TMO,),
        in_specs=[t, t, t, t, t, t, _full((D, D)), _full((3, D)), _full((2, D))],
        out_specs=[t, t, t, _full((D, D)), _full((SUBLANE, D))],
        out_shape=[jax.ShapeDtypeStruct((T, D), F32), jax.ShapeDtypeStruct((T, D), BF16),
                   jax.ShapeDtypeStruct((T, D), BF16), jax.ShapeDtypeStruct((D, D), BF16),
                   jax.ShapeDtypeStruct((SUBLANE, D), F32)],
        scratch_shapes=[pltpu.VMEM((D, D), F32)],
        compiler_params=_params(("arbitrary",)),
    )(dx1, z0, out0, y0, ycat, g0, w_out, mod, lnp)


def _sg_bwd(dyc, su, sv, a128, gsum, sg_lng, sg_lnb, sg_w, sg_bfull):
    T = su.shape[0]
    nb = T // BLK

    def body(dy_ref, su_ref, sv_ref, a_ref, gsum_ref, lng_ref, lnb_ref, w_ref, bfull_ref,
             dsu_ref, dsv_ref, dw_ref, dbt_ref, vec_ref):
        @pl.when(pl.program_id(0) == 0)
        def _():
            dw_ref[...] = jnp.zeros_like(dw_ref)
            dbt_ref[...] = jnp.zeros_like(dbt_ref)
            vec_ref[...] = jnp.zeros_like(vec_ref)

        low = lax.broadcasted_iota(jnp.int32, (BLK, LANE), 1) < HEAD_DIM
        lng = lng_ref[...]
        xhat, rstd, vb, svm = _sg_core(sv_ref, lng, lnb_ref[...], a_ref, w_ref, bfull_ref)
        dy = dy_ref[...].astype(F32)
        dsu_ref[...] = (dy * svm).astype(BF16)
        dsvm = dy * su_ref[...].astype(F32)
        d_hi, d_lo = _split_bf16(dsvm)
        gsum = gsum_ref[...]
        dbt_ref[...] += jnp.dot(d_hi, gsum, preferred_element_type=F32) + jnp.dot(d_lo, gsum,
                                                                                 preferred_element_type=F32)
        tiles = []
        for t in range(SG_W // LANE):
            tl = slice(t * LANE, (t + 1) * LANE)
            dt, v2 = d_hi[:, tl], vb[:, tl]
            dw_ref[2 * t] += _dot_nt(jnp.where(low, dt, jnp.zeros_like(dt)), v2)
            dw_ref[2 * t + 1] += _dot_nt(jnp.where(low, jnp.zeros_like(dt), dt), v2)
            tiles.append(jnp.where(low, _dot_tn(w_ref[2 * t], dt), _dot_tn(w_ref[2 * t + 1], dt)))
        dvgn = jnp.concatenate(tiles, axis=-1)
        vec_ref[0:1, :] += _rowsum(dvgn * xhat)
        vec_ref[1:2, :] += _rowsum(dvgn)
        dxh = dvgn * lng
        m1 = _group_mean(dxh, a_ref)
        m2 = _group_mean(dxh * xhat, a_ref)
        dsv_ref[...] = (rstd * (dxh - m1 - xhat * m2)).astype(BF16)

    t = _tile(BLK, SG_W)
    return _pallas(
        body, name="sg_bwd", grid=(nb,),
        in_specs=[pl.BlockSpec((BLK, SG_W), lambda n: (n, 1)), t, t, _full((2 * LANE, 2 * LANE)), _full((SG_W, LANE)),
                  _full((1, SG_W)), _full((1, SG_W)), _full((SG_GROUPS, BLK, BLK)), _full((BLK, SG_W))],
        out_specs=[t, t, _full((SG_GROUPS, BLK, BLK)), _full((BLK, LANE)), _full((SUBLANE, SG_W))],
        out_shape=[jax.ShapeDtypeStruct((T, SG_W), BF16), jax.ShapeDtypeStruct((T, SG_W), BF16),
                   jax.ShapeDtypeStruct((SG_GROUPS, BLK, BLK), F32), jax.ShapeDtypeStruct((BLK, LANE), F32),
                   jax.ShapeDtypeStruct((SUBLANE, SG_W), F32)],
        compiler_params=_params(("arbitrary",)),
    )(dyc, su, sv, a128, gsum, sg_lng, sg_lnb, sg_w, sg_bfull)


def _mix0_bwd(q, kvx, lse, dyc, ycat, sink_l, bias, sel, rc, rs1, rs2, comm=None):
    T = q.shape[0]
    nb = T // BLK

    def body(q_ref, kp_ref, kc_ref, kn_ref, lse_ref, dyc_ref, yc_ref, sink_ref, bias_ref, sel_ref,
             c_ref, s1_ref, s2_ref, dq_ref, dkv_ref, dsink_ref):
        n = pl.program_id(0)

        @pl.when(n == 0)
        def _():
            dkv_ref[...] = jnp.zeros_like(dkv_ref)
            dsink_ref[...] = jnp.zeros_like(dsink_ref)

        band = pl.ds(pl.multiple_of(n * BLK + (TM - BLK), BLK), 3 * BLK)
        bias = _band_bias(bias_ref, n, nb)
        kvx = jnp.concatenate([kp_ref[...], kc_ref[...], kn_ref[...]], axis=0)
        bias2 = jnp.concatenate([bias, bias], axis=1)
        low2 = lax.broadcasted_iota(jnp.int32, (2 * BLK, LANE), 1) < HEAD_DIM
        sel = sel_ref[...]
        c, s1, s2 = c_ref[...], s1_ref[...], s2_ref[...]
        for kvh in range(2):
            t0, t1 = 2 * kvh, 2 * kvh + 1
            q2 = jnp.concatenate([_lane_tile(q_ref, t0), _lane_tile(q_ref, t1)], axis=0)
            do2 = jnp.concatenate([_lane_tile(dyc_ref, t0), _lane_tile(dyc_ref, t1)], axis=0)
            yc2 = jnp.concatenate([_lane_tile(yc_ref, t0), _lane_tile(yc_ref, t1)], axis=0)
            p_hi, p_lo = _split_bf16(do2.astype(F32) * yc2.astype(F32))
            deltas = _dot_nt(sel, p_hi) + _dot_nt(sel, p_lo)
            dkx = jnp.zeros((3 * BLK, LANE), F32)
            dvx = jnp.zeros((3 * BLK, LANE), F32)
            dq_acc = None
            for par in range(2):
                heads = (4 * kvh + par, 4 * kvh + 2 + par)
                kt = 2 * kvh + par
                ke = kvx[:, kt * LANE:(kt + 1) * LANE]
                ve = kvx[:, (4 + kt) * LANE:(5 + kt) * LANE]
                lse = jnp.concatenate([lse_ref[0, :, h * LANE:(h + 1) * LANE] for h in heads], axis=1)
                sk = jnp.concatenate([_lane_tile(sink_ref, h) for h in heads], axis=1)
                delta = deltas[par:par + 1, :]
                pt = jnp.exp(_dot_nt(ke, q2) + bias2 - lse)
                dst = (pt * (_dot_nt(ve, do2) - delta)).astype(BF16)
                sink_terms = jnp.exp(sk - lse) * delta
                for k, h in enumerate(heads):
                    dsink_ref[:, h * LANE:(h + 1) * LANE] += sink_terms[:, k * LANE:(k + 1) * LANE]
                part = _dot_tn(dst, ke)
                dq_acc = part if dq_acc is None else dq_acc + part
                mine = low2 if par == 0 else jnp.logical_not(low2)
                dkx = dkx + jnp.dot(dst, jnp.where(mine, q2, jnp.zeros_like(q2)), preferred_element_type=F32)
                dvx = dvx + jnp.dot(pt.astype(BF16), jnp.where(mine, do2, jnp.zeros_like(do2)),
                                    preferred_element_type=F32)
            for k, t in enumerate((t0, t1)):
                dq_t = dq_acc[k * BLK:(k + 1) * BLK] * (HEAD_DIM ** -0.5)
                dq_ref[:, t * LANE:(t + 1) * LANE] = _rope_bwd(dq_t, c, s1, s2).astype(BF16)
            dkv_ref[band, kvh * LANE:(kvh + 1) * LANE] += dkx
            dkv_ref[band, (2 + kvh) * LANE:(3 + kvh) * LANE] += dvx

    return _fused_call(
        body, comm, (q, kvx, kvx, kvx, lse, dyc, ycat, sink_l, bias, sel, rc, rs1, rs2),
        name="mix0_bwd", grid=(nb,),
        in_specs=[_tile(BLK, ATTN_W)] + _band_specs(KVX_W, nb) + [
            pl.BlockSpec((1, 1, N_HEADS * LANE), lambda n: (n, 0, 0)), _tile(BLK, ATTN_W), _tile(BLK, ATTN_W),
            _full((1, N_HEADS * LANE)), _full((3 * BLK, LANE)), _full((SUBLANE, LANE)),
            _tile(BLK, LANE), _tile(BLK, LANE), _tile(BLK, LANE)],
        out_specs=[_tile(BLK, ATTN_W), _full((T + 2 * TM, 4 * LANE)), _full((1, N_HEADS * LANE))],
        out_shape=[jax.ShapeDtypeStruct((T, ATTN_W), BF16), jax.ShapeDtypeStruct((T + 2 * TM, 4 * LANE), F32),
                   jax.ShapeDtypeStruct((1, N_HEADS * LANE), F32)])


def _ev_in_bwd(dq, dkv, dsu, dsv, dg0, x, dxp, mod, w_in, rc, rs1, rs2, comm=None):
    T = x.shape[0]

    def body(dq_ref, dkv_ref, dsu_ref, dsv_ref, dg_ref, x_ref, dxp_ref, mod_ref, w_ref, c_ref, s1_ref, s2_ref,
             dx_ref, dwb_ref, vec_ref, dw_ref):
        i = pl.program_id(0)

        @pl.when(i == 0)
        def _():
            dw_ref[...] = jnp.zeros_like(dw_ref)
            vec_ref[...] = jnp.zeros_like(vec_ref)

        low = lax.broadcasted_iota(jnp.int32, (TM, LANE), 1) < HEAD_DIM

        def fold(j):
            t0 = dkv_ref[:, (2 * j) * LANE:(2 * j + 1) * LANE]
            t1 = dkv_ref[:, (2 * j + 1) * LANE:(2 * j + 2) * LANE]
            return jnp.where(low, t0 + pltpu.roll(t0, HEAD_DIM, 1), t1 + pltpu.roll(t1, HEAD_DIM, 1))

        dk = _rope_bwd(fold(0), c_ref[...], s1_ref[...], s2_ref[...]).astype(BF16)
        dp = jnp.concatenate([dq_ref[...], dk, fold(1).astype(BF16), dsu_ref[...], dsv_ref[...],
                              dg_ref[...]], axis=-1)
        xv = x_ref[...]
        scale0 = 1.0 + mod_ref[1:2, :]
        h0 = (xv * scale0 + mod_ref[0:1, :]).astype(BF16)
        dh0 = _dot(dp, w_ref[...])
        dw_ref[...] += _dot_tn(dp, h0)
        dx_ref[...] = dxp_ref[...] + dh0 * scale0
        vec_ref[0:1, :] += _rowsum(dh0)
        vec_ref[1:2, :] += _rowsum(dh0 * xv)

        @pl.when(i == T // TM - 1)
        def _():
            dwb_ref[...] = dw_ref[...].astype(BF16)

    t = _tile(TM, D)
    return _fused_call(
        body, comm, (dq, dkv, dsu, dsv, dg0, x, dxp, mod, w_in, rc, rs1, rs2), name="ev_in_bwd", grid=(T // TM,),
        in_specs=[_tile(TM, ATTN_W), pl.BlockSpec((TM, 4 * LANE), lambda i: (i + 1, 0)), _tile(TM, SG_W),
                  _tile(TM, SG_W), t, t, t,
                  _full((3, D)), _full((EV_IN, D)), _tile(TM, LANE), _tile(TM, LANE), _tile(TM, LANE)],
        out_specs=[t, _full((EV_IN, D)), _full((SUBLANE, D))],
        out_shape=[jax.ShapeDtypeStruct((T, D), F32), jax.ShapeDtypeStruct((EV_IN, D), BF16),
                   jax.ShapeDtypeStruct((SUBLANE, D), F32)],
        scratch_shapes=[pltpu.VMEM((EV_IN, D), F32)])


def _sum_slots(land_ref):
    g = land_ref[0].astype(F32)
    for i in range(1, land_ref.shape[0]):
        g = g + land_ref[i].astype(F32)
    return g


def _reduce_adam(items, name, after=()):
    R, C = items[0][1].shape
    rb = R
    if R > 512:
        for cand in (512, 256, 128, 64, 32, 16, 8):
            if R % cand == 0:
                rb = cand
                break
    n = len(items)

    def body(*refs):
        for k in range(n):
            l_ref, w_ref, m_ref, v_ref = refs[4 * k:4 * k + 4]
            first_out = 4 * n + len(after)
            g_ref, d_ref, nm_ref, nv_ref = refs[first_out + 4 * k:first_out + 4 * k + 4]
            g = _sum_slots(l_ref)
            g_ref[...] = g
            dlt, m2, v2 = _adam(w_ref[...], g, m_ref[...], v_ref[...])
            d_ref[...] = dlt
            nm_ref[...] = m2
            nv_ref[...] = v2

    t = pl.BlockSpec((rb, C), lambda i: (i, 0))
    shp = jax.ShapeDtypeStruct((R, C), F32)
    in_specs, operands = [], []
    for land, w, m, v in items:
        in_specs += [pl.BlockSpec((land.shape[0], rb, C), lambda i: (0, i, 0)), t, t, t]
        operands += [land, w, m, v]
    res = _pallas(
        body, name=name, grid=(R // rb,),
        in_specs=in_specs + [pl.BlockSpec(memory_space=pl.ANY)] * len(after),
        out_specs=[t] * (4 * n), out_shape=[shp] * (4 * n),
        compiler_params=_params(("parallel",)),
    )(*operands, *after)
    return [list(res[4 * k:4 * k + 4]) for k in range(n)]


def _tail_stage1(slabs, small):
    _, R, C = slabs.shape
    n_chips = N_DEV // 2
    gather = _GatherComm(small)
    ns = gather.n

    def body(*refs):
        slab_ref = refs[0]
        g_ins = refs[1:1 + ns]
        part, land_ref = refs[1 + ns], refs[2 + ns]
        g_outs = refs[3 + ns:3 + 2 * ns]
        stage, s1_send, s1_recv = refs[3 + 2 * ns:6 + 2 * ns]
        g_sems = refs[6 + 2 * ns:]
        x, y, c = _my_pos()
        chip = 2 * x + y
        gather.start(g_ins, g_outs, g_sems)
        swaps = [pltpu.make_async_remote_copy(
            src_ref=slab_ref.at[2 * k + (1 - c)], dst_ref=stage.at[k], send_sem=s1_send.at[k],
            recv_sem=s1_recv.at[k], device_id=(x, y, 1 - c), device_id_type=MESH) for k in range(n_chips)]
        for cp in swaps:
            cp.start()
        for cp in swaps:
            cp.wait()
        for k in range(n_chips):
            part[k] = (slab_ref[2 * k + c].astype(F32) + stage[k].astype(F32)).astype(BF16)
        land_ref[chip] = part[chip]
        gather.mid(g_ins, g_outs, g_sems)
        gather.finish(g_ins, g_outs, g_sems)

    any_spec = pl.BlockSpec(memory_space=pl.ANY)
    vmem_spec = pl.BlockSpec(memory_space=pltpu.VMEM)
    slab4 = jax.ShapeDtypeStruct((n_chips, R, C), BF16)
    res = _pallas(
        body, name="tail_stage1",
        out_shape=[slab4, slab4] + gather.out_shapes(),
        in_specs=[vmem_spec] + [any_spec] * ns, out_specs=[vmem_spec, vmem_spec] + [any_spec] * ns,
        scratch_shapes=[pltpu.VMEM((n_chips, R, C), BF16),
                        pltpu.SemaphoreType.DMA((n_chips,)), pltpu.SemaphoreType.DMA((n_chips,))] + gather.sems(),
        compiler_params=pltpu.CompilerParams(vmem_limit_bytes=VMEM_LIMIT),
    )(slabs, *gather.arrs)
    return res[0], res[1], list(res[2:])


def _chip_copies(part_ref, land_ref, send_sems, recv_sems):
    x, y, c = _my_pos()
    chip = 2 * x + y
    copies = []
    for r in range(1, N_DEV // 2):
        px = (1 - x) if (r & 2) else x
        py = (1 - y) if (r & 1) else y
        copies.append(pltpu.make_async_remote_copy(
            src_ref=part_ref.at[2 * px + py], dst_ref=land_ref.at[chip], send_sem=send_sems[r - 1],
            recv_sem=recv_sems[r - 1], device_id=(px, py, c), device_id_type=MESH))
    return copies


def _tail_send(part, land):
    n = N_DEV // 2 - 1

    def body(part_ref, land_ref, *outs):
        send_sems, recv_sems = outs[:n], outs[n:2 * n]
        token = outs[2 * n + 2]
        for cp in _chip_copies(part_ref, land_ref, send_sems, recv_sems):
            cp.start()
        token[...] = jnp.zeros_like(token)

    hbm = pl.BlockSpec(memory_space=pltpu.HBM)
    sem = pl.BlockSpec(memory_space=pltpu.SEMAPHORE)
    res = _pallas(
        body, name="tail_send",
        out_shape=tuple([pltpu.SemaphoreType.DMA(())] * (2 * n)
                        + [pltpu.HBM(part.shape, part.dtype), pltpu.HBM(land.shape, land.dtype),
                           jax.ShapeDtypeStruct((SUBLANE, LANE), F32)]),
        in_specs=(hbm, hbm), out_specs=tuple([sem] * (2 * n) + [hbm, hbm, pl.BlockSpec(memory_space=pltpu.VMEM)]),
        input_output_aliases={0: 2 * n, 1: 2 * n + 1},
        compiler_params=pltpu.CompilerParams(has_side_effects=pltpu.SideEffectType.DATAFLOW_SIDE_EFFECTING),
    )(pltpu.with_memory_space_constraint(part, pltpu.HBM), pltpu.with_memory_space_constraint(land, pltpu.HBM))
    return list(res[:n]), list(res[n:2 * n]), res[2 * n], res[2 * n + 1], res[2 * n + 2]


def _tail_wait(send_sems, recv_sems, part, land, after):
    n = len(send_sems)

    def body(part_ref, land_ref, *rest):
        ss, rs = rest[:n], rest[n:2 * n]
        for cp in _chip_copies(part_ref, land_ref, ss, rs):
            cp.wait_send()
            cp.wait_recv()

    hbm = pl.BlockSpec(memory_space=pltpu.HBM)
    sem = pl.BlockSpec(memory_space=pltpu.SEMAPHORE)
    any_spec = pl.BlockSpec(memory_space=pl.ANY)
    res = _pallas(
        body, name="tail_wait",
        out_shape=(pltpu.HBM(part.shape, part.dtype), pltpu.HBM(land.shape, land.dtype)),
        in_specs=tuple([hbm, hbm] + [sem] * (2 * n) + [any_spec] * len(after)), out_specs=(hbm, hbm),
        input_output_aliases={0: 0, 1: 1},
        compiler_params=pltpu.CompilerParams(has_side_effects=pltpu.SideEffectType.DATAFLOW_SIDE_EFFECTING),
    )(part, land, *send_sems, *recv_sems, *after)
    return res[1]


def _slots_adam(items, name, after=()):
    zeros3 = (0, 0, 0)
    in_specs, out_specs, out_shape, operands = [], [], [], []
    for land, w, m, v in items:
        inner = w.shape[-3:]
        if w.ndim == 5:
            lspec = pl.BlockSpec((N_DEV, 1) + inner, lambda i: (0, i) + zeros3)
            wspec = pl.BlockSpec((1, 1) + inner, lambda i: (0, i) + zeros3)
        else:
            lspec = pl.BlockSpec((N_DEV,) + inner, lambda i: (0,) + zeros3)
            wspec = pl.BlockSpec((1,) + inner, lambda i: (0,) + zeros3)
        in_specs += [lspec, wspec, wspec, wspec]
        out_specs += [wspec] * 4
        out_shape += [jax.ShapeDtypeStruct(w.shape, F32)] * 4
        operands += [land, w, m, v]
    n = len(items)

    def body(*refs):
        for k, (_, w, _, _) in enumerate(items):
            l_ref, w_ref, m_ref, v_ref = refs[4 * k:4 * k + 4]
            first_out = 4 * n + len(after)
            outs = refs[first_out + 4 * k:first_out + 4 * k + 4]
            at = (0, 0) if w.ndim == 5 else (0,)

            def update(l_ref=l_ref, w_ref=w_ref, m_ref=m_ref, v_ref=v_ref, outs=outs, at=at):
                g = l_ref[(0,) + at[1:]].astype(F32)
                for i in range(1, N_DEV):
                    g = g + l_ref[(i,) + at[1:]].astype(F32)
                dlt, m2, v2 = _adam(w_ref[at], g, m_ref[at], v_ref[at])
                for o_ref, val in zip(outs, (g, dlt, m2, v2)):
                    o_ref[at] = val

            if w.ndim == 5:
                update()
            else:
                pl.when(pl.program_id(0) == 0)(update)

    res = _pallas(
        body, name=name, grid=(2,),
        in_specs=in_specs + [pl.BlockSpec(memory_space=pl.ANY)] * len(after),
        out_specs=out_specs, out_shape=out_shape,
        compiler_params=_params(("arbitrary",)),
    )(*operands, *after)
    return [list(res[4 * k:4 * k + 4]) for k in range(n)]


SMALL_PARAMS = ("ln_g", "ln_b", "ev_sg_ln_g", "ev_sg_ln_b", "ev_sink", "ev_sg_b",
                "od_conv_w", "od_conv_b", "od_b_a", "od_b_x", "od_lam")


def _small_update(ga, gc, gd, gf, gb, ge, gsink, gbt, params):
    names = list(SMALL_PARAMS)
    flat = [a for nm in names for a in params[nm]]
    n_g = 8

    def body(*refs):
        ga_ref, gc_ref, gd_ref, gf_ref, gb_ref, ge_ref, gs_ref, gbt_ref = refs[:n_g]
        prm = refs[n_g:n_g + 3 * len(names)]
        loss_ref = refs[n_g + 3 * len(names)]
        outs = refs[n_g + 3 * len(names) + 1:]

        def ssum(ref):
            acc = ref[0]
            for i in range(1, N_DEV):
                acc = acc + ref[i]
            return acc

        a, cc, dd, ff, bb, ee = ssum(ga_ref), ssum(gc_ref), ssum(gd_ref), ssum(gf_ref), ssum(gb_ref), ssum(ge_ref)
        loss_ref[...] = a[3:4, 0:LANE]
        me = _slot(*_my_pos())

        def mine(rows):
            acc = jnp.zeros((rows.shape[0], LANE), F32)
            for j in range(N_DEV):
                acc = acc + jnp.where(me == j, rows[:, j * LANE:(j + 1) * LANE], 0.0)
            return acc

        sink_terms = ssum(gs_ref)
        lane8 = lax.broadcasted_iota(jnp.int32, (1, N_HEADS), 1)
        g_sink = jnp.zeros((1, N_HEADS), F32)
        for h in range(N_HEADS):
            tot = -jnp.sum(sink_terms[:, h * LANE:(h + 1) * LANE], axis=1, keepdims=True)
            g_sink = jnp.where(lane8 == h, tot, g_sink)
        grads = dict(
            ln_g=jnp.concatenate([dd[0:1], a[0:1]], axis=0), ln_b=jnp.concatenate([dd[1:2], a[1:2]], axis=0),
            ev_sg_ln_g=ee[0:1], ev_sg_ln_b=ee[1:2], ev_sink=g_sink,
            ev_sg_b=jnp.transpose(ssum(gbt_ref))[0:SG_GROUPS, :],
            od_conv_w=mine(cc[0:4]), od_conv_b=mine(cc[4:5]),
            od_b_a=mine(jnp.concatenate([ff[0:1], bb[0:1]], axis=0)),
            od_b_x=mine(jnp.concatenate([ff[1:2], bb[1:2]], axis=0)),
            od_lam=mine(jnp.concatenate([ff[2:3], bb[2:3]], axis=0)))
        for k, nm in enumerate(names):
            w_ref, m_ref, v_ref = prm[3 * k:3 * k + 3]
            at = (0,) if len(w_ref.shape) == 3 else ()
            g = grads[nm]
            dlt, m2, v2 = _adam(w_ref[at] if at else w_ref[...], g, m_ref[at] if at else m_ref[...],
                                v_ref[at] if at else v_ref[...])
            for o_ref, val in zip(outs[4 * k:4 * k + 4], (g, dlt, m2, v2)):
                if at:
                    o_ref[at] = val
                else:
                    o_ref[...] = val

    gathered = [ga, gc, gd, gf, gb, ge, gsink, gbt]
    out_shape = [jax.ShapeDtypeStruct((1, LANE), F32)]
    for nm in names:
        out_shape += [jax.ShapeDtypeStruct(params[nm][0].shape, F32)] * 4
    return _pallas(
        body, name="small_update", grid=(1,),
        in_specs=[_full(a.shape) for a in gathered + flat],
        out_specs=[_full(s.shape) for s in out_shape], out_shape=out_shape,
        compiler_params=_params(("arbitrary",)),
    )(*gathered, *flat)


VEC_ROWS = 16
VEC_LAYOUT = (("od_conv_w", 4), ("od_conv_b", 1), ("od_b_a", 2), ("od_b_x", 2), ("od_lam", 2))


def _from_slabs(slabs):
    n, R, cp = slabs.shape
    return slabs.transpose(1, 0, 2).reshape(R, n * cp)


def kernel(x, c, positions, ada_w, ada_b, ln_g, ln_b, ev_w_in, ev_w_out, ev_sink, ev_sg_ln_g, ev_sg_ln_b, ev_sg_w, ev_sg_b, od_w_in, od_conv_w, od_conv_b, od_w_a, od_b_a, od_w_x, od_b_x, od_lam, od_w_out, loss_target, m_ada_w, m_ada_b, m_ln_g, m_ln_b, m_ev_w_in, m_ev_w_out, m_ev_sink, m_ev_sg_ln_g, m_ev_sg_ln_b, m_ev_sg_w, m_ev_sg_b, m_od_w_in, m_od_conv_w, m_od_conv_b, m_od_w_a, m_od_b_a, m_od_w_x, m_od_b_x, m_od_lam, m_od_w_out, v_ada_w, v_ada_b, v_ln_g, v_ln_b, v_ev_w_in, v_ev_w_out, v_ev_sink, v_ev_sg_ln_g, v_ev_sg_ln_b, v_ev_sg_w, v_ev_sg_b, v_od_w_in, v_od_conv_w, v_od_conv_b, v_od_w_a, v_od_b_a, v_od_w_x, v_od_b_x, v_od_lam, v_od_w_out):
    T = x.shape[1]
    me = _slot(*_my_pos())
    xs = x.reshape(T, D)
    tgt = loss_target.reshape(T, D)

    c_all, mod_all, g_vec, (g_ev_in,), (s_ev_out, s_od_in, s_od_out, sg_w, wa, wx) = _head_gather(
        c, ada_w, [ev_w_in[0].T.astype(BF16)],
        [ev_w_out[0], od_w_in[0], od_w_out[0], ev_sg_w[0], od_w_a[0], od_w_x[0]],
        [od_conv_w, od_conv_b, od_b_a, od_b_x, od_lam])
    c_all = c_all.reshape(N_DEV, D)
    w_ev_in = g_ev_in.reshape(EV_IN, D)
    vec_full = _from_slabs(g_vec)
    cw, cb = vec_full[0:4], vec_full[4:5]
    ba, bx, lam = vec_full[5:7], vec_full[7:9], vec_full[9:11]
    mod_mine = lax.dynamic_index_in_dim(mod_all, me, axis=2, keepdims=False)
    mod = mod_mine.transpose(1, 0, 2).reshape(2, 3 * D) + ada_b
    mod0 = mod[0].reshape(3, D)
    mod1 = mod[1].reshape(3, D)

    half = 8
    inv_freq = jnp.power(jnp.float32(ROPE_THETA), -jnp.arange(half, dtype=F32) / half)
    ang = positions.reshape(T).astype(F32)[:, None] * inv_freq
    cos_t = jnp.tile(jnp.cos(ang), (1, LANE // half))
    sin_t = jnp.tile(jnp.sin(ang), (1, LANE // half))
    l64 = jnp.arange(LANE) % HEAD_DIM
    rc = jnp.where(l64 < 2 * half, cos_t, 1.0)
    rs1 = jnp.where(l64 < half, -sin_t, 0.0)
    rs2 = jnp.where((l64 >= half) & (l64 < 2 * half), sin_t, 0.0)

    ln0 = jnp.stack([ln_g[0], ln_b[0]])
    ln1 = jnp.stack([ln_g[1], ln_b[1]])
    sg_lng = ev_sg_ln_g
    sg_lnb = ev_sg_ln_b
    sg_bfull = jnp.repeat(ev_sg_b[0].T, SG_DIM, axis=1)
    sink_l = jnp.repeat(ev_sink, LANE, axis=1)
    kj = jnp.arange(3 * BLK)[:, None]
    qi = jnp.arange(BLK)[None, :]
    band_bias = jnp.where(jnp.abs(kj - BLK - qi) <= BLK, 0.0, NEG_INF).astype(F32)
    lanes = jnp.arange(LANE)
    lanes2 = jnp.arange(2 * LANE)
    a128 = jnp.where(lanes2[:, None] // SG_DIM == lanes2[None, :] // SG_DIM, 1.0 / SG_DIM, 0.0).astype(BF16)
    gsum = (jnp.arange(SG_W)[:, None] // SG_DIM == lanes[None, :]).astype(BF16)
    sel = (jnp.arange(SUBLANE)[:, None] == lanes[None, :] // HEAD_DIM).astype(BF16)

    (q, kvx, su, sv, g0), _ = _ev_in(xs, mod0, w_ev_in, rc, rs1, rs2)
    (ycat, y0, lse), (g_ev_out, g_od_in, g_od_out) = _mix0_fwd(
        q, kvx, su, sv, g0, sink_l, band_bias, a128, sg_lng, sg_lnb, sg_w, sg_bfull,
        _GatherComm([s_ev_out, s_od_in, s_od_out], mid_frac=0.75))
    w_ev_out = g_ev_out.reshape(D, D)
    w_od_in = _from_slabs(g_od_in)
    w_od_out = g_od_out.reshape(D, D)
    out0, z0, x1 = _ev_out(y0, w_ev_out, xs, mod0, ln0)
    xr, g1 = _od_in(x1, mod1, w_od_in)
    fwd_f = _rglru_fwd(xr, cw, cb, wa[0], wx[0], ba[0:1], bx[0:1], lam[0:1], False, "rglru_fwd_f")
    fwd_b = _rglru_fwd(xr, cw, cb, wa[1], wx[1], ba[1:2], bx[1:2], lam[1:2], True, "rglru_fwd_b")
    dh, dg1, dx1p, d_od_out, vec_a = _od_out(fwd_f[0], fwd_b[0], g1, w_od_out, x1, tgt, mod1, ln1)

    (dxcf, dwa_f, dwx_f, vec_f), (l_od_out,) = _rglru_bwd(
        fwd_f, dh, wa[0], wx[0], lam[0:1], False, "rglru_bwd_f",
        _ExchangeComm([d_od_out.reshape(N_DEV, D // N_DEV, D)]))
    (dxcb, dwa_b, dwx_b, vec_b), _ = _rglru_bwd(fwd_b, dh, wa[1], wx[1], lam[1:2], True, "rglru_bwd_b")
    (dx1, d_od_in, vec_c), (a_wa, a_wx) = _od_in_bwd(
        dxcf, dxcb, xr, dg1, x1, dx1p, mod1, w_od_in, cw,
        _GatherComm([jnp.stack([dwa_f, dwa_b]).astype(BF16), jnp.stack([dwx_f, dwx_b]).astype(BF16)],
                    mid_frac=0.75))
    dxp, dyc, dg0, d_ev_out, vec_d = _ev_out_bwd(dx1, z0, out0, y0, ycat, g0, w_ev_out, mod0, ln0)
    dsu, dsv, d_sg_w, d_sg_bt, vec_e = _sg_bwd(dyc, su, sv, a128, gsum, sg_lng, sg_lnb, sg_w, sg_bfull)
    (dq, dkv, d_sink_l), (l_od_in, l_ev_out, ga, gc, gd, gf, gb, ge, gbt, a_sgw) = _mix0_bwd(
        q, kvx, lse, dyc, ycat, sink_l, band_bias, sel, rc, rs1, rs2,
        _BothComm(_ExchangeComm([d_od_in, d_ev_out.reshape(N_DEV, D // N_DEV, D)]),
                  _GatherComm([vec_a, vec_c, vec_d, vec_f, vec_b, vec_e, d_sg_bt, d_sg_w.astype(BF16)], mid_frac=0.9)))
    (grad_x, d_ev_in, vec_g), _ = _ev_in_bwd(dq, dkv, dsu, dsv, dg0, xs, dxp, mod0, w_ev_in, rc, rs1, rs2)

    part, land, (gg, gsink) = _tail_stage1(d_ev_in.reshape(N_DEV, EV_IN // N_DEV, D), [vec_g, d_sink_l])
    send_sems, recv_sems, part, land, token = _tail_send(part, land)

    dmod_all = jnp.stack([jnp.concatenate([gg[:, 0], gg[:, 1], gd[:, 2]], axis=-1),
                          jnp.concatenate([gc[:, 5], gc[:, 6], ga[:, 2]], axis=-1)], axis=1)
    cols = ada_w.shape[2]
    dmod_cols = lax.dynamic_slice_in_dim(dmod_all, me * cols, cols, axis=2).transpose(1, 0, 2)
    (g_ada_w, d_ada_w, nm_ada_w, nv_ada_w, g_ada_b, d_ada_b, nm_ada_b, nv_ada_b) = _ada_update(
        c_all, dmod_cols, dmod_all, ada_w, m_ada_w, v_ada_w, ada_b, m_ada_b, v_ada_b)

    res = dict(ada_w=[g_ada_w, d_ada_w, nm_ada_w, nv_ada_w], ada_b=[g_ada_b, d_ada_b, nm_ada_b, nv_ada_b])
    (r_od_in,) = _reduce_adam([(l_od_in, od_w_in[0], m_od_w_in[0], v_od_w_in[0])], "adam_od_w_in", after=[token])
    r_ev_out, r_od_out = _reduce_adam([(l_ev_out, ev_w_out[0], m_ev_w_out[0], v_ev_w_out[0]),
                                       (l_od_out, od_w_out[0], m_od_w_out[0], v_od_w_out[0])], "adam_w_out",
                                      after=[token])
    for name, r in (("od_w_in", r_od_in), ("ev_w_out", r_ev_out), ("od_w_out", r_od_out)):
        res[name] = [a[None] for a in r]
    res["od_w_a"], res["od_w_x"], res["ev_sg_w"] = _slots_adam(
        [(a_wa, od_w_a, m_od_w_a, v_od_w_a), (a_wx, od_w_x, m_od_w_x, v_od_w_x),
         (a_sgw, ev_sg_w, m_ev_sg_w, v_ev_sg_w)], "adam_gates", after=[token])
    small = dict(ln_g=(ln_g, m_ln_g, v_ln_g), ln_b=(ln_b, m_ln_b, v_ln_b),
                 ev_sg_ln_g=(ev_sg_ln_g, m_ev_sg_ln_g, v_ev_sg_ln_g),
                 ev_sg_ln_b=(ev_sg_ln_b, m_ev_sg_ln_b, v_ev_sg_ln_b),
                 ev_sink=(ev_sink, m_ev_sink, v_ev_sink), ev_sg_b=(ev_sg_b, m_ev_sg_b, v_ev_sg_b),
                 od_conv_w=(od_conv_w, m_od_conv_w, v_od_conv_w), od_conv_b=(od_conv_b, m_od_conv_b, v_od_conv_b),
                 od_b_a=(od_b_a, m_od_b_a, v_od_b_a), od_b_x=(od_b_x, m_od_b_x, v_od_b_x),
                 od_lam=(od_lam, m_od_lam, v_od_lam))
    small_out = _small_update(ga, gc, gd, gf, gb, ge, gsink, gbt, small)
    l_ev_in = _tail_wait(send_sems, recv_sems, part, land,
                         [r_od_in[0], r_od_out[0], res["od_w_x"][0], g_ada_w, small_out[0]])
    (r_ev_in,) = _reduce_adam([(l_ev_in, ev_w_in[0].T, m_ev_w_in[0].T, v_ev_w_in[0].T)], "adam_ev_w_in")
    res["ev_w_in"] = [a.T[None] for a in r_ev_in]
    loss = small_out[0][0, 0]
    for k, name in enumerate(SMALL_PARAMS):
        res[name] = small_out[1 + 4 * k:5 + 4 * k]

    order = ["ada_w", "ada_b", "ln_g", "ln_b", "ev_w_in", "ev_w_out", "ev_sink", "ev_sg_ln_g", "ev_sg_ln_b",
             "ev_sg_w", "ev_sg_b", "od_w_in", "od_conv_w", "od_conv_b", "od_w_a", "od_b_a", "od_w_x", "od_b_x",
             "od_lam", "od_w_out"]
    outs = [loss, grad_x.reshape(1, T, D)]
    for kind in range(4):
        outs += [res[name][kind] for name in order]
    return tuple(outs)
```

```python
import jax
import jax.numpy as jnp
from jax import lax
from jax.experimental import pallas as pl
from jax.experimental.pallas import tpu as pltpu

F32 = jnp.float32
BF16 = jnp.bfloat16

N_DEV = 8
D = 1024
N_HEADS = 8
HEAD_DIM = 64
ATTN_W = 512
SG_W = 512
SG_GROUPS = 8
SG_DIM = 64
BLK = 128
KVX_W = 1024
EV_IN = 2816
OD_IN = 2048
RNN_HEADS = 8
RNN_HD = 128
ALPHA = 4.0 ** 0.25
LN_EPS = 1e-5
NEG_INF = -1e30
RG_C = 8.0
ROPE_THETA = 500000.0
LR, B1, B2, EPS, WD, STEP = 0.001, 0.9, 0.999, 1e-08, 0.01, 10

LANE = 128
SUBLANE = 8
TM = 256
TMF = 512
TMO = 512
TS = 256
VMEM_LIMIT = 56 * 1024 * 1024

MESH = pl.DeviceIdType.MESH


def _pallas(body, **kw):
    return pl.pallas_call(body, **kw)


def _params(sem, vmem=VMEM_LIMIT):
    return pltpu.CompilerParams(dimension_semantics=sem, vmem_limit_bytes=vmem)


def _sigmoid(x):
    return 0.5 * jnp.tanh(0.5 * x) + 0.5


def _silu_and_grad(x):
    s = _sigmoid(x)
    return x * s, s * (1.0 + x * (1.0 - s))


def _dot(a, b):
    return jnp.dot(a.astype(BF16), b.astype(BF16), preferred_element_type=F32)


def _dot_nt(a, b):
    return lax.dot_general(a.astype(BF16), b.astype(BF16), (((1,), (1,)), ((), ())), preferred_element_type=F32)


def _dot_tn(a, b):
    return lax.dot_general(a.astype(BF16), b.astype(BF16), (((0,), (0,)), ((), ())), preferred_element_type=F32)


def _ln_fwd(z, g, b):
    mu = jnp.mean(z, axis=-1, keepdims=True)
    zc = z - mu
    var = jnp.mean(zc * zc, axis=-1, keepdims=True)
    rstd = lax.rsqrt(var + LN_EPS)
    xhat = zc * rstd
    return xhat * g + b, xhat, rstd


def _ln_bwd(dy, xhat, rstd, g):
    dxh = dy * g
    m1 = jnp.mean(dxh, axis=-1, keepdims=True)
    m2 = jnp.mean(dxh * xhat, axis=-1, keepdims=True)
    return rstd * (dxh - m1 - xhat * m2)


def _rowsum(v):
    return jnp.sum(v, axis=0, keepdims=True)


def _rope_fwd(t, c, s1, s2):
    return t * c + pltpu.roll(t, LANE - 8, 1) * s1 + pltpu.roll(t, 8, 1) * s2


def _rope_bwd(d, c, s1, s2):
    return d * c + pltpu.roll(d * s1, 8, 1) + pltpu.roll(d * s2, LANE - 8, 1)


def _adam(w, g, m, v):
    m2 = B1 * m + (1.0 - B1) * g
    v2 = B2 * v + (1.0 - B2) * (g * g)
    m_hat = m2 / (1.0 - B1 ** STEP)
    v_hat = v2 / (1.0 - B2 ** STEP)
    delta = -LR * (m_hat / (jnp.sqrt(v_hat) + EPS) + WD * w)
    return delta, m2, v2


def _tile(rows, width):
    return pl.BlockSpec((rows, width), lambda i: (i, 0))


def _full(shape):
    zeros = (0,) * len(shape)
    return pl.BlockSpec(shape, lambda i: zeros)


def _rev_tile(rows, width, n, reverse):
    if reverse:
        return pl.BlockSpec((rows, width), lambda i: (n - 1 - i, 0))
    return pl.BlockSpec((rows, width), lambda i: (i, 0))


def _halo_specs(rows, width, n, total_rows, reverse):
    per = rows // SUBLANE
    last = total_rows // SUBLANE - 1

    def tile_of(i):
        return (n - 1 - i) if reverse else i

    prev = pl.BlockSpec((SUBLANE, width), lambda i: (jnp.maximum(tile_of(i) * per - 1, 0), 0))
    nxt = pl.BlockSpec((SUBLANE, width), lambda i: (jnp.minimum((tile_of(i) + 1) * per, last), 0))
    return prev, nxt


def _my_pos():
    return lax.axis_index("x"), lax.axis_index("y"), lax.axis_index("c")


def _slot(px, py, pc):
    return 4 * px + 2 * py + pc


class _GatherComm:
    has_mid = True

    def __init__(self, arrs, mid_frac=0.5):
        self.arrs = list(arrs)
        self.n = len(self.arrs)
        self.mid_frac = mid_frac

    def out_shapes(self):
        return [jax.ShapeDtypeStruct((N_DEV,) + a.shape, a.dtype) for a in self.arrs]

    def sems(self):
        return [pltpu.SemaphoreType.DMA((7 * self.n,)), pltpu.SemaphoreType.DMA((7 * self.n,)),
                pltpu.SemaphoreType.DMA((self.n,))]

    def _parts(self, ins, outs, sems):
        send_sems, recv_sems, local_sems = sems
        x, y, c = _my_pos()
        me, sibling = (x, y, c), (x, y, 1 - c)
        chips = [(1 - x, y), (x, 1 - y), (1 - x, 1 - y)]

        def copy(a, k, block, to, src=None):
            dst = outs[a].at[_slot(*block)]
            return pltpu.make_async_remote_copy(
                src_ref=dst if src is None else src, dst_ref=dst,
                send_sem=send_sems.at[a * 7 + k], recv_sem=recv_sems.at[a * 7 + k],
                device_id=to, device_id_type=MESH)

        local = [pltpu.make_async_copy(ins[a], outs[a].at[_slot(*me)], local_sems.at[a]) for a in range(self.n)]
        first = []
        for a in range(self.n):
            first.append(copy(a, 0, me, sibling, src=ins[a]))
            first += [copy(a, 1 + j, me, (*chip, c), src=ins[a]) for j, chip in enumerate(chips)]
        ici_in = [copy(a, 1 + j, (*chip, c), me) for j, chip in enumerate(chips) for a in range(self.n)]
        passed = [copy(a, 4 + j, (*chip, c), sibling) for j, chip in enumerate(chips) for a in range(self.n)]
        d2d_in = []
        for a in range(self.n):
            d2d_in.append(copy(a, 0, sibling, me))
            d2d_in += [copy(a, 4 + j, (*chip, 1 - c), me) for j, chip in enumerate(chips)]
        return local, first, ici_in, passed, d2d_in

    def start(self, ins, outs, sems):
        local, first, _, _, _ = self._parts(ins, outs, sems)
        for cp in local + first:
            cp.start()

    def mid(self, ins, outs, sems):
        _, _, ici_in, passed, _ = self._parts(ins, outs, sems)
        for arrived, fw in zip(ici_in, passed):
            arrived.wait_recv()
            fw.start()

    def finish(self, ins, outs, sems):
        local, first, _, passed, d2d_in = self._parts(ins, outs, sems)
        for cp in d2d_in:
            cp.wait_recv()
        for cp in first + passed:
            cp.wait_send()
        for cp in local:
            cp.wait()


class _ExchangeComm:
    has_mid = False

    def __init__(self, arrs):
        self.arrs = list(arrs)
        self.n = len(self.arrs)

    def out_shapes(self):
        return [jax.ShapeDtypeStruct(a.shape, a.dtype) for a in self.arrs]

    def sems(self):
        return [pltpu.SemaphoreType.DMA((7 * self.n,)), pltpu.SemaphoreType.DMA((7 * self.n,)),
                pltpu.SemaphoreType.DMA((self.n,))]

    def _copies(self, ins, outs, sems):
        send_sems, recv_sems, local_sems = sems
        x, y, c = _my_pos()
        mine = _slot(x, y, c)
        copies = [pltpu.make_async_copy(ins[a].at[mine], outs[a].at[mine], local_sems.at[a]) for a in range(self.n)]
        for k in range(1, N_DEV):
            px = (1 - x) if (k & 4) else x
            py = (1 - y) if (k & 2) else y
            pc = (1 - c) if (k & 1) else c
            for a in range(self.n):
                copies.append(pltpu.make_async_remote_copy(
                    src_ref=ins[a].at[_slot(px, py, pc)], dst_ref=outs[a].at[mine],
                    send_sem=send_sems.at[a * 7 + k - 1], recv_sem=recv_sems.at[a * 7 + k - 1],
                    device_id=(px, py, pc), device_id_type=MESH))
        return copies

    def start(self, ins, outs, sems):
        for cp in self._copies(ins, outs, sems):
            cp.start()

    def finish(self, ins, outs, sems):
        for cp in self._copies(ins, outs, sems):
            cp.wait()


class _BothComm:
    has_mid = True

    def __init__(self, first, second):
        self.parts = (first, second)
        self.arrs = first.arrs + second.arrs
        self.n = first.n + second.n
        self.mid_frac = second.mid_frac

    def out_shapes(self):
        return self.parts[0].out_shapes() + self.parts[1].out_shapes()

    def sems(self):
        return self.parts[0].sems() + self.parts[1].sems()

    def _each(self, ins, outs, sems):
        a, b = self.parts
        return ((a, ins[:a.n], outs[:a.n], sems[:3]), (b, ins[a.n:], outs[a.n:], sems[3:]))

    def start(self, ins, outs, sems):
        for cm, i_, o_, s_ in self._each(ins, outs, sems):
            cm.start(i_, o_, s_)

    def mid(self, ins, outs, sems):
        for cm, i_, o_, s_ in self._each(ins, outs, sems):
            if cm.has_mid:
                cm.mid(i_, o_, s_)

    def finish(self, ins, outs, sems):
        for cm, i_, o_, s_ in self._each(ins, outs, sems):
            cm.finish(i_, o_, s_)


def _fused_call(body, comm, operands, *, name, grid, in_specs, out_specs, out_shape, scratch_shapes=(),
                semantics=("arbitrary",)):
    n_in, n_out, n_scr = len(in_specs), len(out_specs), len(scratch_shapes)
    if comm is None:
        res = _pallas(body, name=name, grid=grid, in_specs=list(in_specs), out_specs=list(out_specs),
                      out_shape=list(out_shape), scratch_shapes=list(scratch_shapes),
                      compiler_params=_params(semantics))(*operands)
        return list(res), []
    k = comm.n
    steps = grid[0]

    def wrapped(*refs):
        ins, cins = refs[:n_in], refs[n_in:n_in + k]
        outs = refs[n_in + k:n_in + k + n_out]
        couts = refs[n_in + k + n_out:n_in + 2 * k + n_out]
        rest = refs[n_in + 2 * k + n_out:]
        scratch, sems = rest[:n_scr], rest[n_scr:]
        i = pl.program_id(0)

        @pl.when(i == 0)
        def _():
            comm.start(cins, couts, sems)

        body(*ins, *outs, *scratch)

        if comm.has_mid:
            @pl.when(i == int(steps * comm.mid_frac))
            def _():
                comm.mid(cins, couts, sems)

        @pl.when(i == steps - 1)
        def _():
            comm.finish(cins, couts, sems)

    any_spec = pl.BlockSpec(memory_space=pl.ANY)
    res = _pallas(wrapped, name=name, grid=grid, in_specs=list(in_specs) + [any_spec] * k,
                  out_specs=list(out_specs) + [any_spec] * k, out_shape=list(out_shape) + comm.out_shapes(),
                  scratch_shapes=list(scratch_shapes) + comm.sems(),
                  compiler_params=_params(("arbitrary",)))(*operands, *comm.arrs)
    return list(res[:n_out]), list(res[n_out:])


def _head_gather(c, ada_w, big, to_cast, vec_parts):
    cols = ada_w.shape[2]
    g_c, g_big = _GatherComm([c]), _GatherComm(big)
    g_mod = _GatherComm([jax.ShapeDtypeStruct((2, N_DEV, cols), F32)])
    g_vec = _GatherComm([jax.ShapeDtypeStruct((VEC_ROWS, LANE), F32)])
    nb, nc, nv = g_big.n, len(to_cast), len(vec_parts)

    def body(*refs):
        c_ref, w_ref = refs[0], refs[1]
        vec_in = refs[2:2 + nv]
        cast_in = refs[2 + nv:2 + nv + nc]
        big_in = refs[2 + nv + nc:2 + nv + nc + nb]
        outs = refs[2 + nv + nc + nb:]
        c_all_ref, mod_all_ref, vec_all_ref = outs[0], outs[1], outs[2]
        cast_out = outs[3:3 + nc]
        big_out = outs[3 + nc:3 + nc + nb]
        part_ref, pack_ref = outs[3 + nc + nb], outs[4 + nc + nb]
        sems = outs[5 + nc + nb:]
        s_c, s_mod, s_big, s_vec = sems[0:3], sems[3:6], sems[6:9], sems[9:12]
        g_c.start([c_ref], [c_all_ref], s_c)
        g_big.start(big_in, big_out, s_big)
        pack_ref[...] = jnp.zeros_like(pack_ref)
        row = 0
        for ref, (_, nrows) in zip(vec_in, VEC_LAYOUT):
            pack_ref[row:row + nrows, :] = ref[0] if len(ref.shape) == 3 else ref[...]
            row += nrows
        g_vec.start([pack_ref], [vec_all_ref], s_vec)
        g_c.mid([c_ref], [c_all_ref], s_c)
        g_c.finish([c_ref], [c_all_ref], s_c)
        cv = c_all_ref[:, 0, :]
        cond = cv * _sigmoid(cv)
        for l in range(2):
            part_ref[l] = _dot(cond, w_ref[l])
        g_mod.start([part_ref], [mod_all_ref], s_mod)
        for src, dst in zip(cast_in, cast_out):
            dst[...] = src[...].astype(BF16)
        for g, ins, outs_, sm in ((g_vec, [pack_ref], [vec_all_ref], s_vec), (g_mod, [part_ref], [mod_all_ref], s_mod),
                                  (g_big, big_in, big_out, s_big)):
            g.mid(ins, outs_, sm)
            g.finish(ins, outs_, sm)

    any_spec = pl.BlockSpec(memory_space=pl.ANY)
    vmem_spec = pl.BlockSpec(memory_space=pltpu.VMEM)
    res = _pallas(
        body, name="head_gather",
        out_shape=(g_c.out_shapes() + g_mod.out_shapes() + g_vec.out_shapes()
                   + [jax.ShapeDtypeStruct(a.shape, BF16) for a in to_cast] + g_big.out_shapes()),
        in_specs=[vmem_spec] * (2 + nv + nc) + [any_spec] * nb,
        out_specs=[vmem_spec] * (3 + nc) + [any_spec] * nb,
        scratch_shapes=[pltpu.VMEM((2, N_DEV, cols), F32), pltpu.VMEM((VEC_ROWS, LANE), F32)]
        + g_c.sems() + g_mod.sems() + g_big.sems() + g_vec.sems(),
        compiler_params=pltpu.CompilerParams(vmem_limit_bytes=VMEM_LIMIT),
    )(c, ada_w, *vec_parts, *to_cast, *big)
    return res[0], res[1], res[2], list(res[3 + nc:]), list(res[3:3 + nc])


def _ada_update(c_all, dmod_cols, dmod_all, ada_w, m_w, v_w, ada_b, m_b, v_b):
    cols = ada_w.shape[2]
    nb = ada_b.shape[1]

    def body(c_ref, dmc_ref, dma_ref, w_ref, mw_ref, vw_ref, b_ref, mb_ref, vb_ref,
             gw_ref, dw_ref, nmw_ref, nvw_ref, gb_ref, db_ref, nmb_ref, nvb_ref):
        cv = c_ref[...]
        cond = cv * _sigmoid(cv)
        for l in range(2):
            g = _dot_tn(cond, dmc_ref[l])
            gw_ref[l] = g
            dlt, m2, v2 = _adam(w_ref[l], g, mw_ref[l], vw_ref[l])
            dw_ref[l] = dlt
            nmw_ref[l] = m2
            nvw_ref[l] = v2
        gb = dma_ref[0]
        for i in range(1, N_DEV):
            gb = gb + dma_ref[i]
        gb_ref[...] = gb
        dlt, m2, v2 = _adam(b_ref[...], gb, mb_ref[...], vb_ref[...])
        db_ref[...] = dlt
        nmb_ref[...] = m2
        nvb_ref[...] = v2

    wspec = _full((2, D, cols))
    bspec = _full((2, nb))
    wshape = jax.ShapeDtypeStruct((2, D, cols), F32)
    bshape = jax.ShapeDtypeStruct((2, nb), F32)
    return _pallas(
        body, name="ada_update", grid=(1,),
        in_specs=[_full((N_DEV, D)), _full((2, N_DEV, cols)), _full((N_DEV, 2, nb)),
                  wspec, wspec, wspec, bspec, bspec, bspec],
        out_specs=[wspec] * 4 + [bspec] * 4,
        out_shape=[wshape] * 4 + [bshape] * 4,
        compiler_params=_params(("arbitrary",)),
    )(c_all, dmod_cols, dmod_all, ada_w, m_w, v_w, ada_b, m_b, v_b)


def _ev_in(x, mod, w_in, rc, rs1, rs2, comm=None):
    T = x.shape[0]

    def body(x_ref, mod_ref, w_ref, c_ref, s1_ref, s2_ref, q_ref, kv_ref, su_ref, sv_ref, g_ref):
        h = x_ref[...] * (1.0 + mod_ref[1:2, :]) + mod_ref[0:1, :]
        p = _dot_nt(h, w_ref[...])
        c, s1, s2 = c_ref[...], s1_ref[...], s2_ref[...]
        for j in range(ATTN_W // LANE):
            qr = _rope_fwd(p[:, j * LANE:(j + 1) * LANE], c, s1, s2)
            q_ref[:, j * LANE:(j + 1) * LANE] = (qr * (HEAD_DIM ** -0.5)).astype(BF16)
        low = lax.broadcasted_iota(jnp.int32, (TMF, LANE), 1) < HEAD_DIM
        for j, val in enumerate((_rope_fwd(p[:, 512:640], c, s1, s2), p[:, 640:768])):
            swapped = pltpu.roll(val, HEAD_DIM, 1)
            tiles = (jnp.where(low, val, 0.0), jnp.where(low, 0.0, swapped),
                     jnp.where(low, swapped, 0.0), jnp.where(low, 0.0, val))
            for k, tile in enumerate(tiles):
                kv_ref[:, (4 * j + k) * LANE:(4 * j + k + 1) * LANE] = tile.astype(BF16)
        su_ref[...] = p[:, 768:1280].astype(BF16)
        sv_ref[...] = p[:, 1280:1792].astype(BF16)
        g_ref[...] = p[:, 1792:2816].astype(BF16)

    sh = lambda w: jax.ShapeDtypeStruct((T, w), BF16)
    return _fused_call(
        body, comm, (x, mod, w_in, rc, rs1, rs2), name="ev_in", grid=(T // TMF,),
        in_specs=[_tile(TMF, D), _full((3, D)), _full((EV_IN, D)), _tile(TMF, LANE), _tile(TMF, LANE),
                  _tile(TMF, LANE)],
        out_specs=[_tile(TMF, ATTN_W), _tile(TMF, KVX_W), _tile(TMF, SG_W), _tile(TMF, SG_W), _tile(TMF, D)],
        out_shape=[sh(ATTN_W), sh(KVX_W), sh(SG_W), sh(SG_W), sh(D)], semantics=("parallel",))


def _band_specs2(width, nb):
    return [pl.BlockSpec((BLK, width), lambda n: (jnp.maximum(2 * n - 1, 0), 0)),
            pl.BlockSpec((2 * BLK, width), lambda n: (n, 0)),
            pl.BlockSpec((BLK, width), lambda n: (jnp.minimum(2 * n + 2, nb - 1), 0))]


def _band_bias(bias_ref, n, nb):
    rows = lax.broadcasted_iota(jnp.int32, (3 * BLK, 1), 0)
    outside = ((rows < BLK) & (n == 0)) | ((rows >= 2 * BLK) & (n == nb - 1))
    return bias_ref[...] + jnp.where(outside, NEG_INF, 0.0)


def _lane_tile(ref, t):
    return ref[:, t * LANE:(t + 1) * LANE]


def _split_bf16(v):
    hi = v.astype(BF16)
    return hi, (v - hi.astype(F32)).astype(BF16)


def _group_mean(v, a_ref, exact_bf16=False):
    hi, lo = _split_bf16(v)
    a = a_ref[...]
    out = []
    for t in range(SG_W // (2 * LANE)):
        sl = slice(t * 2 * LANE, (t + 1) * 2 * LANE)
        r = jnp.dot(hi[:, sl], a, preferred_element_type=F32)
        if not exact_bf16:
            r = r + jnp.dot(lo[:, sl], a, preferred_element_type=F32)
        out.append(r)
    return jnp.concatenate(out, axis=-1)


def _sg_core(sv_ref, lng, lnb, a_ref, w_ref, bfull_ref):
    svf = sv_ref[...].astype(F32)
    xc = svf - _group_mean(svf, a_ref, exact_bf16=True)
    rstd = lax.rsqrt(_group_mean(xc * xc, a_ref) + LN_EPS)
    xhat = xc * rstd
    vb = (xhat * lng + lnb).astype(BF16)
    low = lax.broadcasted_iota(jnp.int32, (BLK, LANE), 1) < SG_DIM
    tiles = []
    for t in range(SG_W // LANE):
        v2 = vb[:, t * LANE:(t + 1) * LANE]
        r0 = jnp.dot(w_ref[2 * t], v2, preferred_element_type=F32)
        r1 = jnp.dot(w_ref[2 * t + 1], v2, preferred_element_type=F32)
        tiles.append(jnp.where(low, r0, r1))
    svm = jnp.concatenate(tiles, axis=-1) + bfull_ref[...]
    return xhat, rstd, vb, svm


def _mix0_fwd(q, kvx, su, sv, g0, sink_l, bias, a128, sg_lng, sg_lnb, sg_w, sg_bfull, comm=None):
    T = q.shape[0]
    nb = T // BLK

    def body(q_ref, kp_ref, kc_ref, kn_ref, su_ref, sv_ref, g_ref, sink_ref, bias_ref, a_ref, lng_ref, lnb_ref,
             w_ref, bfull_ref, ycat_ref, y0_ref, lse_ref):
        n = pl.program_id(0)
        kvx4 = jnp.concatenate([kp_ref[...], kc_ref[...], kn_ref[...]], axis=0)
        for s in range(2):
            rows = slice(s * BLK, (s + 1) * BLK)
            bias = _band_bias(bias_ref, 2 * n + s, nb)
            kvx = kvx4[s * BLK:s * BLK + 3 * BLK]
            tiles = []
            for t in range(ATTN_W // LANE):
                qt = q_ref[rows, t * LANE:(t + 1) * LANE]
                acc = None
                for par in range(2):
                    h = 2 * t + par
                    kt = 2 * (h // 4) + par
                    ke = kvx[:, kt * LANE:(kt + 1) * LANE]
                    ve = kvx[:, (4 + kt) * LANE:(5 + kt) * LANE]
                    st = _dot_nt(ke, qt) + bias
                    sk = _lane_tile(sink_ref, h)
                    m = jnp.maximum(jnp.max(st, axis=0, keepdims=True), sk)
                    p = jnp.exp(st - m)
                    denom = jnp.sum(p, axis=0, keepdims=True) + jnp.exp(sk - m)
                    contrib = _dot_tn(p * (1.0 / denom), ve)
                    acc = contrib if acc is None else acc + contrib
                    lse_ref[s, :, h * LANE:(h + 1) * LANE] = m + jnp.log(denom)
                tiles.append(acc)
            _, _, _, svm = _sg_core(sv_ref.at[rows, :], lng_ref[...], lnb_ref[...], a_ref, w_ref, bfull_ref)
            tiles.append(su_ref[rows, :].astype(F32) * svm)
            ycat = jnp.concatenate(tiles, axis=-1)
            gf = g_ref[rows, :].astype(F32)
            ycat_ref[rows, :] = ycat.astype(BF16)
            y0_ref[rows, :] = (ycat * (gf * _sigmoid(gf))).astype(BF16)

    two = 2 * BLK
    return _fused_call(
        body, comm, (q, kvx, kvx, kvx, su, sv, g0, sink_l, bias, a128, sg_lng, sg_lnb, sg_w, sg_bfull),
        name="mix0_fwd", grid=(nb // 2,),
        in_specs=[_tile(two, ATTN_W)] + _band_specs2(KVX_W, nb) + [
                  _tile(two, SG_W), _tile(two, SG_W), _tile(two, D), _full((1, N_HEADS * LANE)),
                  _full((3 * BLK, LANE)), _full((2 * LANE, 2 * LANE)),_full((1, SG_W)), _full((1, SG_W)),
                  _full((SG_GROUPS, BLK, BLK)), _full((BLK, SG_W))],
        out_specs=[_tile(two, D), _tile(two, D), pl.BlockSpec((2, 1, N_HEADS * LANE), lambda n: (n, 0, 0))],
        out_shape=[jax.ShapeDtypeStruct((T, D), BF16), jax.ShapeDtypeStruct((T, D), BF16),
                   jax.ShapeDtypeStruct((nb, 1, N_HEADS * LANE), F32)], semantics=("parallel",))


def _ev_out(y0, w_out, x, mod, lnp):
    T = x.shape[0]

    def body(y_ref, w_ref, x_ref, mod_ref, ln_ref, out_ref, z_ref, x1_ref):
        out = _dot(y_ref[...], w_ref[...])
        z = ALPHA * x_ref[...] + mod_ref[2:3, :] * out
        x1, _, _ = _ln_fwd(z, ln_ref[0:1, :], ln_ref[1:2, :])
        out_ref[...] = out.astype(BF16)
        z_ref[...] = z
        x1_ref[...] = x1

    return _pallas(
        body, name="ev_out", grid=(T // TMF,),
        in_specs=[_tile(TMF, D), _full((D, D)), _tile(TMF, D), _full((3, D)), _full((2, D))],
        out_specs=[_tile(TMF, D)] * 3,
        out_shape=[jax.ShapeDtypeStruct((T, D), BF16), jax.ShapeDtypeStruct((T, D), F32),
                   jax.ShapeDtypeStruct((T, D), F32)],
        compiler_params=_params(("parallel",)),
    )(y0, w_out, x, mod, lnp)


def _od_in(x1, mod, w_in):
    T = x1.shape[0]

    def body(x_ref, mod_ref, w_ref, xr_ref, g_ref):
        h = x_ref[...] * (1.0 + mod_ref[1:2, :]) + mod_ref[0:1, :]
        p = _dot(h, w_ref[...])
        xr_ref[...] = p[:, :D]
        g_ref[...] = p[:, D:].astype(BF16)

    return _pallas(
        body, name="od_in", grid=(T // TMF,),
        in_specs=[_tile(TMF, D), _full((3, D)), _full((D, OD_IN))],
        out_specs=[_tile(TMF, D), _tile(TMF, D)],
        out_shape=[jax.ShapeDtypeStruct((T, D), F32), jax.ShapeDtypeStruct((T, D), BF16)],
        compiler_params=_params(("parallel",)),
    )(x1, mod, w_in)


def _ext_rows(prev_ref, cur, next_ref, j, n):
    prev = jnp.where(j > 0, prev_ref[...], 0.0)
    nxt = jnp.where(j < n - 1, next_ref[...], 0.0)
    return jnp.concatenate([prev, cur, nxt], axis=0)


def _shift_rows(ext, off, rows):
    total = ext.shape[0]
    if off == 0:
        return ext[SUBLANE:SUBLANE + rows, :]
    return pltpu.roll(ext, (-off) % total, 0)[SUBLANE:SUBLANE + rows, :]


def _conv_fwd(ext, cw, cb, rows):
    xc = cb
    for k in range(4):
        xc = xc + cw[k:k + 1, :] * _shift_rows(ext, k - 2, rows)
    return xc


def _gates(xc, wa_ref, wx_ref, ba, bx, lam):
    pr, pi = [], []
    for h in range(RNN_HEADS):
        xh = xc[:, h * RNN_HD:(h + 1) * RNN_HD].astype(BF16)
        pr.append(_dot(xh, wa_ref[h]))
        pi.append(_dot(xh, wx_ref[h]))
    r = _sigmoid(jnp.concatenate(pr, axis=-1) + ba)
    ig = _sigmoid(jnp.concatenate(pi, axis=-1) + bx)
    sp = jnp.maximum(-lam, 0.0) + jnp.log(1.0 + jnp.exp(-jnp.abs(lam)))
    neg_log_a = RG_C * r * sp
    a = jnp.exp(-neg_log_a)
    s2 = (1.0 + a * a) * jnp.tanh(neg_log_a)
    inv_s = lax.rsqrt(jnp.maximum(s2, 1e-30))
    return r, ig, sp, a, s2 * inv_s, inv_s


def _scan_tile(a_ref, b_ref, o_ref, carry_ref, rows, reverse):
    ridx = lax.broadcasted_iota(jnp.int32, (SUBLANE, D), 0)
    groups = rows // SUBLANE

    def group(gi, h):
        g = (groups - 1 - gi) if reverse else gi
        off = pl.multiple_of(g * SUBLANE, SUBLANE)
        a = a_ref[pl.ds(off, SUBLANE), :]
        b = b_ref[pl.ds(off, SUBLANE), :]
        for sh in (1, 2, 4):
            if reverse:
                keep = ridx < SUBLANE - sh
                a_p = jnp.where(keep, pltpu.roll(a, SUBLANE - sh, 0), 1.0)
                b_p = jnp.where(keep, pltpu.roll(b, SUBLANE - sh, 0), 0.0)
            else:
                keep = ridx >= sh
                a_p = jnp.where(keep, pltpu.roll(a, sh, 0), 1.0)
                b_p = jnp.where(keep, pltpu.roll(b, sh, 0), 0.0)
            b = b + a * b_p
            a = a * a_p
        hh = b + a * h
        o_ref[pl.ds(off, SUBLANE), :] = hh
        return hh[0:1, :] if reverse else hh[SUBLANE - 1:SUBLANE, :]

    carry_ref[...] = lax.fori_loop(0, groups, group, carry_ref[...])


def _rglru_fwd(xr, cw, cb, wa, wx, ba, bx, lam, reverse, name):
    T = xr.shape[0]
    n = T // TS
    prev_spec, next_spec = _halo_specs(TS, D, n, T, reverse)

    def body(prev_ref, cur_ref, next_ref, cw_ref, cb_ref, wa_ref, wx_ref, ba_ref, bx_ref, lam_ref,
             h_ref, a_ref, s_ref, r_ref, ig_ref, xc_ref, b_s, carry):
        i = pl.program_id(0)
        j = (n - 1 - i) if reverse else i

        @pl.when(i == 0)
        def _():
            carry[...] = jnp.zeros_like(carry)

        ext = _ext_rows(prev_ref, cur_ref[...], next_ref, j, n)
        xc = _conv_fwd(ext, cw_ref[...], cb_ref[...], TS)
        r, ig, _, a, s, _ = _gates(xc, wa_ref, wx_ref, ba_ref[...], bx_ref[...], lam_ref[...])
        s_ref[...] = s
        r_ref[...] = r.astype(BF16)
        ig_ref[...] = ig.astype(BF16)
        xc_ref[...] = xc.astype(BF16)
        a_ref[...] = a
        b_s[...] = s * ig * xc
        _scan_tile(a_ref, b_s, h_ref, carry, TS, reverse)

    wspec = _full((RNN_HEADS, RNN_HD, RNN_HD))
    cur = _rev_tile(TS, D, n, reverse)
    f32 = jax.ShapeDtypeStruct((T, D), F32)
    b16 = jax.ShapeDtypeStruct((T, D), BF16)
    return _pallas(
        body, name=name, grid=(n,),
        in_specs=[prev_spec, cur, next_spec, _full((4, D)), _full((1, D)),
                  wspec, wspec, _full((1, D)), _full((1, D)), _full((1, D))],
        out_specs=[cur] * 6,
        out_shape=[f32, f32, f32, b16, b16, b16],
        scratch_shapes=[pltpu.VMEM((TS, D), F32), pltpu.VMEM((1, D), F32)],
        compiler_params=_params(("arbitrary",)),
    )(xr, xr, xr, cw, cb, wa, wx, ba, bx, lam)


def _od_out(hf, hb, g1, w_out, x1, tgt, mod, lnp):
    T = x1.shape[0]

    def body(hf_ref, hb_ref, g_ref, w_ref, x_ref, t_ref, mod_ref, ln_ref,
             dh_ref, dg_ref, dx_ref, dwb_ref, vec_ref, dw_ref):
        i = pl.program_id(0)

        @pl.when(i == 0)
        def _():
            dw_ref[...] = jnp.zeros_like(dw_ref)
            vec_ref[...] = jnp.zeros_like(vec_ref)

        hs = hf_ref[...] + hb_ref[...]
        sg, dsg = _silu_and_grad(g_ref[...].astype(F32))
        yr = (hs * sg).astype(BF16)
        w = w_ref[...]
        out = _dot(yr, w)
        gate = mod_ref[2:3, :]
        z = ALPHA * x_ref[...] + gate * out
        lng = ln_ref[0:1, :]
        x2, xhat, rstd = _ln_fwd(z, lng, ln_ref[1:2, :])
        diff = x2 - t_ref[...]
        vec_ref[3:4, 0:LANE] += 0.5 * jnp.sum(diff * diff) * (1.0 / D)
        dx2 = diff * (1.0 / D)
        dz = _ln_bwd(dx2, xhat, rstd, lng)
        vec_ref[0:1, :] += _rowsum(dx2 * xhat)
        vec_ref[1:2, :] += _rowsum(dx2)
        vec_ref[2:3, :] += _rowsum(dz * out)
        dout = (dz * gate).astype(BF16)
        dyr = _dot_nt(dout, w)
        dw_ref[...] += _dot_tn(yr, dout)
        dh_ref[...] = dyr * sg
        dg_ref[...] = (dyr * hs * dsg).astype(BF16)
        dx_ref[...] = ALPHA * dz

        @pl.when(i == T // TMO - 1)
        def _():
            dwb_ref[...] = dw_ref[...].astype(BF16)

    return _pallas(
        body, name="od_out", grid=(T // TMO,),
        in_specs=[_tile(TMO, D), _tile(TMO, D), _tile(TMO, D), _full((D, D)), _tile(TMO, D), _tile(TMO, D),
                  _full((3, D)), _full((2, D))],
        out_specs=[_tile(TMO, D), _tile(TMO, D), _tile(TMO, D), _full((D, D)), _full((SUBLANE, D))],
        out_shape=[jax.ShapeDtypeStruct((T, D), F32), jax.ShapeDtypeStruct((T, D), BF16),
                   jax.ShapeDtypeStruct((T, D), F32), jax.ShapeDtypeStruct((D, D), BF16),
                   jax.ShapeDtypeStruct((SUBLANE, D), F32)],
        scratch_shapes=[pltpu.VMEM((D, D), F32)],
        compiler_params=_params(("arbitrary",)),
    )(hf, hb, g1, w_out, x1, tgt, mod, lnp)


def _rglru_bwd(fwd, dh, wa, wx, lam, reverse, name, comm=None):
    h, a_all, s_all, r_all, ig_all, xc_all = fwd
    T = h.shape[0]
    n = T // TS
    adj_rev = not reverse
    hprev_spec, hnext_spec = _halo_specs(TS, D, n, T, adj_rev)
    h_halo_spec = hnext_spec if reverse else hprev_spec

    def body(dh_ref, h_ref, hh_ref, a_ref, s_ref, r_ref, ig_ref, xc_ref, wa_ref, wx_ref, lam_ref,
             dxc_ref, dwa_ref, dwx_ref, vec_ref, a_s, l_s, carry, a_edge):
        i = pl.program_id(0)
        j = (n - 1 - i) if adj_rev else i

        @pl.when(i == 0)
        def _():
            carry[...] = jnp.zeros_like(carry)
            a_edge[...] = jnp.zeros_like(a_edge)
            dwa_ref[...] = jnp.zeros_like(dwa_ref)
            dwx_ref[...] = jnp.zeros_like(dwx_ref)
            vec_ref[...] = jnp.zeros_like(vec_ref)

        lam = lam_ref[...]
        sp = jnp.maximum(-lam, 0.0) + jnp.log(1.0 + jnp.exp(-jnp.abs(lam)))
        a, s = a_ref[...], s_ref[...]
        inv_s = lax.rsqrt(jnp.maximum(s * s, 1e-30))
        r, ig = r_ref[...].astype(F32), ig_ref[...].astype(F32)
        xcb = xc_ref[...]
        xc = xcb.astype(F32)

        rows = lax.broadcasted_iota(jnp.int32, (TS, D), 0)
        hcur = h_ref[...]
        if reverse:
            a_sh = jnp.where(rows == 0, a_edge[...], pltpu.roll(a, 1, 0))
            halo = jnp.where(j < n - 1, hh_ref[0:1, :], 0.0)
            h_nb = jnp.where(rows == TS - 1, halo, pltpu.roll(hcur, TS - 1, 0))
        else:
            a_sh = jnp.where(rows == TS - 1, a_edge[...], pltpu.roll(a, TS - 1, 0))
            halo = jnp.where(j > 0, hh_ref[SUBLANE - 1:SUBLANE, :], 0.0)
            h_nb = jnp.where(rows == 0, halo, pltpu.roll(hcur, 1, 0))
        a_s[...] = a_sh
        _scan_tile(a_s, dh_ref, l_s, carry, TS, adj_rev)
        a_edge[...] = a[TS - 1:TS, :] if reverse else a[0:1, :]

        lm = l_s[...]
        da = lm * h_nb
        di = lm * s * xc
        dxc = lm * s * ig
        ds = lm * ig * xc
        dlog_a = a * (da - ds * a * inv_s)
        dr = (-RG_C) * sp * dlog_a
        dsp = _rowsum((-RG_C) * r * dlog_a)
        dpr = dr * r * (1.0 - r)
        dpi = di * ig * (1.0 - ig)
        vec_ref[0:1, :] += _rowsum(dpr)
        vec_ref[1:2, :] += _rowsum(dpi)
        vec_ref[2:3, :] += dsp * (-_sigmoid(-lam))
        parts = []
        for hd in range(RNN_HEADS):
            sl = slice(hd * RNN_HD, (hd + 1) * RNN_HD)
            xh = xcb[:, sl]
            dprh = dpr[:, sl].astype(BF16)
            dpih = dpi[:, sl].astype(BF16)
            parts.append(_dot_nt(dprh, wa_ref[hd]) + _dot_nt(dpih, wx_ref[hd]))
            dwa_ref[hd] += _dot_tn(xh, dprh)
            dwx_ref[hd] += _dot_tn(xh, dpih)
        dxc_ref[...] = dxc + jnp.concatenate(parts, axis=-1)

    wspec = _full((RNN_HEADS, RNN_HD, RNN_HD))
    cur = _rev_tile(TS, D, n, adj_rev)
    return _fused_call(
        body, comm, (dh, h, h, a_all, s_all, r_all, ig_all, xc_all, wa, wx, lam), name=name, grid=(n,),
        in_specs=[cur, cur, h_halo_spec, cur, cur, cur, cur, cur, wspec, wspec, _full((1, D))],
        out_specs=[cur, wspec, wspec, _full((SUBLANE, D))],
        out_shape=[jax.ShapeDtypeStruct((T, D), F32),
                   jax.ShapeDtypeStruct((RNN_HEADS, RNN_HD, RNN_HD), F32),
                   jax.ShapeDtypeStruct((RNN_HEADS, RNN_HD, RNN_HD), F32),
                   jax.ShapeDtypeStruct((SUBLANE, D), F32)],
        scratch_shapes=[pltpu.VMEM((TS, D), F32)] * 2 + [pltpu.VMEM((1, D), F32)] * 2)


def _od_in_bwd(dxcf, dxcb, xr, dg1, x1, dx1p, mod, w_in, cw, comm=None):
    T = x1.shape[0]
    n = T // TMO
    slab = OD_IN // N_DEV
    prev_spec, next_spec = _halo_specs(TMO, D, n, T, False)

    def body(fp_ref, fc_ref, fn_ref, bp_ref, bc_ref, bn_ref, xr_ref, dg_ref, x1_ref, dxp_ref,
             mod_ref, w_ref, cw_ref, dx_ref, dwb_ref, vec_ref, dw_ref):
        i = pl.program_id(0)

        @pl.when(i == 0)
        def _():
            dw_ref[...] = jnp.zeros_like(dw_ref)
            vec_ref[...] = jnp.zeros_like(vec_ref)

        dcur = fc_ref[...] + bc_ref[...]
        dprev = jnp.where(i > 0, fp_ref[...] + bp_ref[...], 0.0)
        dnext = jnp.where(i < n - 1, fn_ref[...] + bn_ref[...], 0.0)
        dext = jnp.concatenate([dprev, dcur, dnext], axis=0)
        xr_v = xr_ref[...]
        cw_v = cw_ref[...]
        dxr = None
        for k in range(4):
            shifted = _shift_rows(dext, 2 - k, TMO)
            term = cw_v[k:k + 1, :] * shifted
            dxr = term if dxr is None else dxr + term
            vec_ref[k:k + 1, :] += _rowsum(shifted * xr_v)
        vec_ref[4:5, :] += _rowsum(dcur)
        dp = jnp.concatenate([dxr.astype(BF16), dg_ref[...]], axis=-1)
        x1v = x1_ref[...]
        scale1 = 1.0 + mod_ref[1:2, :]
        h1 = (x1v * scale1 + mod_ref[0:1, :]).astype(BF16)
        dh1 = _dot_nt(dp, w_ref[...])
        dw_ref[...] += _dot_tn(h1, dp)
        dx_ref[...] = dxp_ref[...] + dh1 * scale1
        vec_ref[5:6, :] += _rowsum(dh1)
        vec_ref[6:7, :] += _rowsum(dh1 * x1v)

        @pl.when(i == n - 1)
        def _():
            for j in range(N_DEV):
                dwb_ref[j] = dw_ref[:, j * slab:(j + 1) * slab].astype(BF16)

    t = _tile(TMO, D)
    return _fused_call(
        body, comm, (dxcf, dxcf, dxcf, dxcb, dxcb, dxcb, xr, dg1, x1, dx1p, mod, w_in, cw),
        name="od_in_bwd", grid=(n,),
        in_specs=[prev_spec, t, next_spec, prev_spec, t, next_spec, t, t, t, t,
                  _full((3, D)), _full((D, OD_IN)), _full((4, D))],
        out_specs=[t, _full((N_DEV, D, slab)), _full((SUBLANE, D))],
        out_shape=[jax.ShapeDtypeStruct((T, D), F32), jax.ShapeDtypeStruct((N_DEV, D, slab), BF16),
                   jax.ShapeDtypeStruct((SUBLANE, D), F32)],
        scratch_shapes=[pltpu.VMEM((D, OD_IN), F32)])


def _ev_out_bwd(dx1, z0, out0, y0, ycat, g0, w_out, mod, lnp):
    T = dx1.shape[0]

    def body(dx_ref, z_ref, out_ref, y0_ref, yc_ref, g_ref, w_ref, mod_ref, ln_ref,
             dxp_ref, dyc_ref, dg_ref, dwb_ref, vec_ref, dw_ref):
        i = pl.program_id(0)

        @pl.when(i == 0)
        def _():
            dw_ref[...] = jnp.zeros_like(dw_ref)
            vec_ref[...] = jnp.zeros_like(vec_ref)

        lng = ln_ref[0:1, :]
        _, xhat, rstd = _ln_fwd(z_ref[...], lng, ln_ref[1:2, :])
        dy = dx_ref[...]
        dz = _ln_bwd(dy, xhat, rstd, lng)
        vec_ref[0:1, :] += _rowsum(dy * xhat)
        vec_ref[1:2, :] += _rowsum(dy)
        vec_ref[2:3, :] += _rowsum(dz * out_ref[...].astype(F32))
        dout = (dz * mod_ref[2:3, :]).astype(BF16)
        dy0 = _dot_nt(dout, w_ref[...])
        dw_ref[...] += _dot_tn(y0_ref[...], dout)
        sg, dsg = _silu_and_grad(g_ref[...].astype(F32))
        dyc_ref[...] = (dy0 * sg).astype(BF16)
        dg_ref[...] = (dy0 * yc_ref[...].astype(F32) * dsg).astype(BF16)
        dxp_ref[...] = ALPHA * dz

        @pl.when(i == T // TMO - 1)
        def _():
            dwb_ref[...] = dw_ref[...].astype(BF16)

    t = _tile(TMO, D)
    return _pallas(
        body, name="ev_out_bwd", grid=(T // TMO,),
        in_specs=[t, t, t, t, t, t, _full((D, D)), _full((3, D)), _full((2, D))],
        out_specs=[t, t, t, _full((D, D)), _full((SUBLANE, D))],
        out_shape=[jax.ShapeDtypeStruct((T, D), F32), jax.ShapeDtypeStruct((T, D), BF16),
                   jax.ShapeDtypeStruct((T, D), BF16), jax.ShapeDtypeStruct((D, D), BF16),
                   jax.ShapeDtypeStruct((SUBLANE, D), F32)],
        scratch_shapes=[pltpu.VMEM((D, D), F32)],
        compiler_params=_params(("arbitrary",)),
    )(dx1, z0, out0, y0, ycat, g0, w_out, mod, lnp)


def _mix0_bwd(q, kvx, lse, dyc, ycat, su, sv, sink_l, bias, a128, gsum, sel, sg_lng, sg_lnb, sg_w, sg_bfull,
              rc, rs1, rs2, comm=None):
    T = q.shape[0]
    nb = T // BLK

    def body(q_ref, kp_ref, kc_ref, kn_ref, lse_ref, dyc_ref, yc_ref, su_ref, sv_ref, sink_ref, bias_ref, a_ref,
             gsum_ref, sel_ref, lng_ref, lnb_ref, w_ref, bfull_ref, c_ref, s1_ref, s2_ref,
             dq_ref, dkv_ref, dsu_ref, dsv_ref, dw_ref, dbt_ref, vec_ref, dsink_ref):
        n = pl.program_id(0)

        @pl.when(n == 0)
        def _():
            dkv_ref[...] = jnp.zeros_like(dkv_ref)
            dw_ref[...] = jnp.zeros_like(dw_ref)
            dbt_ref[...] = jnp.zeros_like(dbt_ref)
            vec_ref[...] = jnp.zeros_like(vec_ref)
            dsink_ref[...] = jnp.zeros_like(dsink_ref)

        kvx4 = jnp.concatenate([kp_ref[...], kc_ref[...], kn_ref[...]], axis=0)
        for s in range(2):
            _mix0_bwd_block(s, 2 * n + s, nb, kvx4[s * BLK:s * BLK + 3 * BLK], q_ref, lse_ref, dyc_ref, yc_ref, su_ref,
                            sv_ref, sink_ref, bias_ref, a_ref, gsum_ref, sel_ref, lng_ref, lnb_ref, w_ref, bfull_ref,
                            c_ref, s1_ref, s2_ref, dq_ref, dkv_ref, dsu_ref, dsv_ref, dw_ref, dbt_ref, vec_ref,
                            dsink_ref)

    def _mix0_bwd_block(s, b, nb, kvx, q_ref, lse_ref, dyc_ref, yc_ref, su_ref, sv_ref, sink_ref, bias_ref, a_ref,
                        gsum_ref, sel_ref, lng_ref, lnb_ref, w_ref, bfull_ref, c_ref, s1_ref, s2_ref,
                        dq_ref, dkv_ref, dsu_ref, dsv_ref, dw_ref, dbt_ref, vec_ref, dsink_ref):
        rows = slice(s * BLK, (s + 1) * BLK)

        def tile(ref, t):
            return ref[rows, t * LANE:(t + 1) * LANE]

        band = pl.ds(pl.multiple_of(b * BLK + (TM - BLK), BLK), 3 * BLK)
        bias = _band_bias(bias_ref, b, nb)
        bias2 = jnp.concatenate([bias, bias], axis=1)
        low = lax.broadcasted_iota(jnp.int32, (BLK, LANE), 1) < HEAD_DIM
        low2 = lax.broadcasted_iota(jnp.int32, (2 * BLK, LANE), 1) < HEAD_DIM
        sel = sel_ref[...]
        c, s1, s2 = c_ref[rows, :], s1_ref[rows, :], s2_ref[rows, :]
        for kvh in range(2):
            t0, t1 = 2 * kvh, 2 * kvh + 1
            q2 = jnp.concatenate([tile(q_ref, t0), tile(q_ref, t1)], axis=0)
            do2 = jnp.concatenate([tile(dyc_ref, t0), tile(dyc_ref, t1)], axis=0)
            yc2 = jnp.concatenate([tile(yc_ref, t0), tile(yc_ref, t1)], axis=0)
            p_hi, p_lo = _split_bf16(do2.astype(F32) * yc2.astype(F32))
            deltas = _dot_nt(sel, p_hi) + _dot_nt(sel, p_lo)
            dkx = jnp.zeros((3 * BLK, LANE), F32)
            dvx = jnp.zeros((3 * BLK, LANE), F32)
            dq_acc = None
            for par in range(2):
                heads = (4 * kvh + par, 4 * kvh + 2 + par)
                kt = 2 * kvh + par
                ke = kvx[:, kt * LANE:(kt + 1) * LANE]
                ve = kvx[:, (4 + kt) * LANE:(5 + kt) * LANE]
                lse = jnp.concatenate([lse_ref[s, :, h * LANE:(h + 1) * LANE] for h in heads], axis=1)
                sk = jnp.concatenate([_lane_tile(sink_ref, h) for h in heads], axis=1)
                delta = deltas[par:par + 1, :]
                pt = jnp.exp(_dot_nt(ke, q2) + bias2 - lse)
                dst = (pt * (_dot_nt(ve, do2) - delta)).astype(BF16)
                sink_terms = jnp.exp(sk - lse) * delta
                for k, h in enumerate(heads):
                    dsink_ref[:, h * LANE:(h + 1) * LANE] += sink_terms[:, k * LANE:(k + 1) * LANE]
                part = _dot_tn(dst, ke)
                dq_acc = part if dq_acc is None else dq_acc + part
                mine = low2 if par == 0 else jnp.logical_not(low2)
                dkx = dkx + jnp.dot(dst, jnp.where(mine, q2, jnp.zeros_like(q2)), preferred_element_type=F32)
                dvx = dvx + jnp.dot(pt.astype(BF16), jnp.where(mine, do2, jnp.zeros_like(do2)),
                                    preferred_element_type=F32)
            for k, t in enumerate((t0, t1)):
                dq_t = dq_acc[k * BLK:(k + 1) * BLK] * (HEAD_DIM ** -0.5)
                dq_ref[rows, t * LANE:(t + 1) * LANE] = _rope_bwd(dq_t, c, s1, s2).astype(BF16)
            dkv_ref[band, kvh * LANE:(kvh + 1) * LANE] += dkx
            dkv_ref[band, (2 + kvh) * LANE:(3 + kvh) * LANE] += dvx

        lng = lng_ref[...]
        xhat, rstd, vb, svm = _sg_core(sv_ref.at[rows, :], lng, lnb_ref[...], a_ref, w_ref, bfull_ref)
        dy = dyc_ref[rows, ATTN_W:].astype(F32)
        dsu_ref[rows, :] = (dy * svm).astype(BF16)
        dsvm = dy * su_ref[rows, :].astype(F32)
        d_hi, d_lo = _split_bf16(dsvm)
        gsum = gsum_ref[...]
        dbt_ref[...] += jnp.dot(d_hi, gsum, preferred_element_type=F32) + jnp.dot(d_lo, gsum,
                                                                                 preferred_element_type=F32)
        tiles = []
        for t in range(SG_W // LANE):
            tl = slice(t * LANE, (t + 1) * LANE)
            dt, v2 = d_hi[:, tl], vb[:, tl]
            dw_ref[2 * t] += _dot_nt(jnp.where(low, dt, jnp.zeros_like(dt)), v2)
            dw_ref[2 * t + 1] += _dot_nt(jnp.where(low, jnp.zeros_like(dt), dt), v2)
            tiles.append(jnp.where(low, _dot_tn(w_ref[2 * t], dt), _dot_tn(w_ref[2 * t + 1], dt)))
        dvgn = jnp.concatenate(tiles, axis=-1)
        vec_ref[0:1, :] += _rowsum(dvgn * xhat)
        vec_ref[1:2, :] += _rowsum(dvgn)
        dxh = dvgn * lng
        m1 = _group_mean(dxh, a_ref)
        m2 = _group_mean(dxh * xhat, a_ref)
        dsv_ref[rows, :] = (rstd * (dxh - m1 - xhat * m2)).astype(BF16)

    two = 2 * BLK
    return _fused_call(
        body, comm, (q, kvx, kvx, kvx, lse, dyc, ycat, su, sv, sink_l, bias, a128, gsum, sel, sg_lng, sg_lnb, sg_w,
                     sg_bfull, rc, rs1, rs2),
        name="mix0_bwd", grid=(nb // 2,),
        in_specs=[_tile(two, ATTN_W)] + _band_specs2(KVX_W, nb) + [
            pl.BlockSpec((2, 1, N_HEADS * LANE), lambda n: (n, 0, 0)), _tile(two, D), _tile(two, D),
            _tile(two, SG_W), _tile(two, SG_W), _full((1, N_HEADS * LANE)), _full((3 * BLK, LANE)),
            _full((2 * LANE, 2 * LANE)),_full((SG_W, LANE)), _full((SUBLANE, LANE)), _full((1, SG_W)), _full((1, SG_W)),
            _full((SG_GROUPS, BLK, BLK)), _full((BLK, SG_W)), _tile(two, LANE), _tile(two, LANE), _tile(two, LANE)],
        out_specs=[_tile(two, ATTN_W), _full((T + 2 * TM, 4 * LANE)), _tile(two, SG_W), _tile(two, SG_W),
                   _full((SG_GROUPS, BLK, BLK)), _full((BLK, LANE)), _full((SUBLANE, SG_W)),
                   _full((1, N_HEADS * LANE))],
        out_shape=[jax.ShapeDtypeStruct((T, ATTN_W), BF16), jax.ShapeDtypeStruct((T + 2 * TM, 4 * LANE), F32),
                   jax.ShapeDtypeStruct((T, SG_W), BF16), jax.ShapeDtypeStruct((T, SG_W), BF16),
                   jax.ShapeDtypeStruct((SG_GROUPS, BLK, BLK), F32), jax.ShapeDtypeStruct((BLK, LANE), F32),
                   jax.ShapeDtypeStruct((SUBLANE, SG_W), F32), jax.ShapeDtypeStruct((1, N_HEADS * LANE), F32)])


def _ev_in_bwd(dq, dkv, dsu, dsv, dg0, x, dxp, mod, w_in, rc, rs1, rs2, comm=None):
    T = x.shape[0]

    def body(dq_ref, dkv_ref, dsu_ref, dsv_ref, dg_ref, x_ref, dxp_ref, mod_ref, w_ref, c_ref, s1_ref, s2_ref,
             dx_ref, dwb_ref, vec_ref, dw_ref):
        i = pl.program_id(0)

        @pl.when(i == 0)
        def _():
            dw_ref[...] = jnp.zeros_like(dw_ref)
            vec_ref[...] = jnp.zeros_like(vec_ref)

        low = lax.broadcasted_iota(jnp.int32, (TM, LANE), 1) < HEAD_DIM

        def fold(j):
            t0 = dkv_ref[:, (2 * j) * LANE:(2 * j + 1) * LANE]
            t1 = dkv_ref[:, (2 * j + 1) * LANE:(2 * j + 2) * LANE]
            return jnp.where(low, t0 + pltpu.roll(t0, HEAD_DIM, 1), t1 + pltpu.roll(t1, HEAD_DIM, 1))

        dk = _rope_bwd(fold(0), c_ref[...], s1_ref[...], s2_ref[...]).astype(BF16)
        dp = jnp.concatenate([dq_ref[...], dk, fold(1).astype(BF16), dsu_ref[...], dsv_ref[...],
                              dg_ref[...]], axis=-1)
        xv = x_ref[...]
        scale0 = 1.0 + mod_ref[1:2, :]
        h0 = (xv * scale0 + mod_ref[0:1, :]).astype(BF16)
        dh0 = _dot(dp, w_ref[...])
        dw_ref[...] += _dot_tn(dp, h0)
        dx_ref[...] = dxp_ref[...] + dh0 * scale0
        vec_ref[0:1, :] += _rowsum(dh0)
        vec_ref[1:2, :] += _rowsum(dh0 * xv)

        @pl.when(i == T // TM - 1)
        def _():
            dwb_ref[...] = dw_ref[...].astype(BF16)

    t = _tile(TM, D)
    return _fused_call(
        body, comm, (dq, dkv, dsu, dsv, dg0, x, dxp, mod, w_in, rc, rs1, rs2), name="ev_in_bwd", grid=(T // TM,),
        in_specs=[_tile(TM, ATTN_W), pl.BlockSpec((TM, 4 * LANE), lambda i: (i + 1, 0)), _tile(TM, SG_W),
                  _tile(TM, SG_W), t, t, t,
                  _full((3, D)), _full((EV_IN, D)), _tile(TM, LANE), _tile(TM, LANE), _tile(TM, LANE)],
        out_specs=[t, _full((EV_IN, D)), _full((SUBLANE, D))],
        out_shape=[jax.ShapeDtypeStruct((T, D), F32), jax.ShapeDtypeStruct((EV_IN, D), BF16),
                   jax.ShapeDtypeStruct((SUBLANE, D), F32)],
        scratch_shapes=[pltpu.VMEM((EV_IN, D), F32)])


def _sum_slots(land_ref):
    g = land_ref[0].astype(F32)
    for i in range(1, land_ref.shape[0]):
        g = g + land_ref[i].astype(F32)
    return g


def _reduce_adam(items, name, after=()):
    R, C = items[0][1].shape
    rb = R
    if R > 512:
        for cand in (512, 256, 128, 64, 32, 16, 8):
            if R % cand == 0:
                rb = cand
                break
    n = len(items)

    def body(*refs):
        for k in range(n):
            l_ref, w_ref, m_ref, v_ref = refs[4 * k:4 * k + 4]
            first_out = 4 * n + len(after)
            g_ref, d_ref, nm_ref, nv_ref = refs[first_out + 4 * k:first_out + 4 * k + 4]
            g = _sum_slots(l_ref)
            g_ref[...] = g
            dlt, m2, v2 = _adam(w_ref[...], g, m_ref[...], v_ref[...])
            d_ref[...] = dlt
            nm_ref[...] = m2
            nv_ref[...] = v2

    t = pl.BlockSpec((rb, C), lambda i: (i, 0))
    shp = jax.ShapeDtypeStruct((R, C), F32)
    in_specs, operands = [], []
    for land, w, m, v in items:
        in_specs += [pl.BlockSpec((land.shape[0], rb, C), lambda i: (0, i, 0)), t, t, t]
        operands += [land, w, m, v]
    res = _pallas(
        body, name=name, grid=(R // rb,),
        in_specs=in_specs + [pl.BlockSpec(memory_space=pl.ANY)] * len(after),
        out_specs=[t] * (4 * n), out_shape=[shp] * (4 * n),
        compiler_params=_params(("parallel",)),
    )(*operands, *after)
    return [list(res[4 * k:4 * k + 4]) for k in range(n)]


def _tail_stage1(slabs, small):
    _, R, C = slabs.shape
    n_chips = N_DEV // 2
    gather = _GatherComm(small)
    ns = gather.n

    def body(*refs):
        slab_ref = refs[0]
        g_ins = refs[1:1 + ns]
        part, land_ref = refs[1 + ns], refs[2 + ns]
        g_outs = refs[3 + ns:3 + 2 * ns]
        stage, s1_send, s1_recv = refs[3 + 2 * ns:6 + 2 * ns]
        g_sems = refs[6 + 2 * ns:]
        x, y, c = _my_pos()
        chip = 2 * x + y
        gather.start(g_ins, g_outs, g_sems)
        swaps = [pltpu.make_async_remote_copy(
            src_ref=slab_ref.at[2 * k + (1 - c)], dst_ref=stage.at[k], send_sem=s1_send.at[k],
            recv_sem=s1_recv.at[k], device_id=(x, y, 1 - c), device_id_type=MESH) for k in range(n_chips)]
        for cp in swaps:
            cp.start()
        for cp in swaps:
            cp.wait()
        for k in range(n_chips):
            part[k] = (slab_ref[2 * k + c].astype(F32) + stage[k].astype(F32)).astype(BF16)
        land_ref[chip] = part[chip]
        gather.mid(g_ins, g_outs, g_sems)
        gather.finish(g_ins, g_outs, g_sems)

    any_spec = pl.BlockSpec(memory_space=pl.ANY)
    vmem_spec = pl.BlockSpec(memory_space=pltpu.VMEM)
    slab4 = jax.ShapeDtypeStruct((n_chips, R, C), BF16)
    res = _pallas(
        body, name="tail_stage1",
        out_shape=[slab4, slab4] + gather.out_shapes(),
        in_specs=[vmem_spec] + [any_spec] * ns, out_specs=[vmem_spec, vmem_spec] + [any_spec] * ns,
        scratch_shapes=[pltpu.VMEM((n_chips, R, C), BF16),
                        pltpu.SemaphoreType.DMA((n_chips,)), pltpu.SemaphoreType.DMA((n_chips,))] + gather.sems(),
        compiler_params=pltpu.CompilerParams(vmem_limit_bytes=VMEM_LIMIT),
    )(slabs, *gather.arrs)
    return res[0], res[1], list(res[2:])


def _chip_copies(part_ref, land_ref, send_sems, recv_sems):
    x, y, c = _my_pos()
    chip = 2 * x + y
    copies = []
    for r in range(1, N_DEV // 2):
        px = (1 - x) if (r & 2) else x
        py = (1 - y) if (r & 1) else y
        copies.append(pltpu.make_async_remote_copy(
            src_ref=part_ref.at[2 * px + py], dst_ref=land_ref.at[chip], send_sem=send_sems[r - 1],
            recv_sem=recv_sems[r - 1], device_id=(px, py, c), device_id_type=MESH))
    return copies


def _tail_send(part, land):
    n = N_DEV // 2 - 1

    def body(part_ref, land_ref, *outs):
        send_sems, recv_sems = outs[:n], outs[n:2 * n]
        token = outs[2 * n + 2]
        for cp in _chip_copies(part_ref, land_ref, send_sems, recv_sems):
            cp.start()
        token[...] = jnp.zeros_like(token)

    hbm = pl.BlockSpec(memory_space=pltpu.HBM)
    sem = pl.BlockSpec(memory_space=pltpu.SEMAPHORE)
    res = _pallas(
        body, name="tail_send",
        out_shape=tuple([pltpu.SemaphoreType.DMA(())] * (2 * n)
                        + [pltpu.HBM(part.shape, part.dtype), pltpu.HBM(land.shape, land.dtype),
                           jax.ShapeDtypeStruct((SUBLANE, LANE), F32)]),
        in_specs=(hbm, hbm), out_specs=tuple([sem] * (2 * n) + [hbm, hbm, pl.BlockSpec(memory_space=pltpu.VMEM)]),
        input_output_aliases={0: 2 * n, 1: 2 * n + 1},
        compiler_params=pltpu.CompilerParams(has_side_effects=pltpu.SideEffectType.DATAFLOW_SIDE_EFFECTING),
    )(pltpu.with_memory_space_constraint(part, pltpu.HBM), pltpu.with_memory_space_constraint(land, pltpu.HBM))
    return list(res[:n]), list(res[n:2 * n]), res[2 * n], res[2 * n + 1], res[2 * n + 2]


def _tail_wait(send_sems, recv_sems, part, land, after):
    n = len(send_sems)

    def body(part_ref, land_ref, *rest):
        ss, rs = rest[:n], rest[n:2 * n]
        for cp in _chip_copies(part_ref, land_ref, ss, rs):
            cp.wait_send()
            cp.wait_recv()

    hbm = pl.BlockSpec(memory_space=pltpu.HBM)
    sem = pl.BlockSpec(memory_space=pltpu.SEMAPHORE)
    any_spec = pl.BlockSpec(memory_space=pl.ANY)
    res = _pallas(
        body, name="tail_wait",
        out_shape=(pltpu.HBM(part.shape, part.dtype), pltpu.HBM(land.shape, land.dtype)),
        in_specs=tuple([hbm, hbm] + [sem] * (2 * n) + [any_spec] * len(after)), out_specs=(hbm, hbm),
        input_output_aliases={0: 0, 1: 1},
        compiler_params=pltpu.CompilerParams(has_side_effects=pltpu.SideEffectType.DATAFLOW_SIDE_EFFECTING),
    )(part, land, *send_sems, *recv_sems, *after)
    return res[1]


def _slots_adam(items, name, after=()):
    zeros3 = (0, 0, 0)
    in_specs, out_specs, out_shape, operands = [], [], [], []
    for land, w, m, v in items:
        inner = w.shape[-3:]
        if w.ndim == 5:
            lspec = pl.BlockSpec((N_DEV, 1) + inner, lambda i: (0, i) + zeros3)
            wspec = pl.BlockSpec((1, 1) + inner, lambda i: (0, i) + zeros3)
        else:
            lspec = pl.BlockSpec((N_DEV,) + inner, lambda i: (0,) + zeros3)
            wspec = pl.BlockSpec((1,) + inner, lambda i: (0,) + zeros3)
        in_specs += [lspec, wspec, wspec, wspec]
        out_specs += [wspec] * 4
        out_shape += [jax.ShapeDtypeStruct(w.shape, F32)] * 4
        operands += [land, w, m, v]
    n = len(items)

    def body(*refs):
        for k, (_, w, _, _) in enumerate(items):
            l_ref, w_ref, m_ref, v_ref = refs[4 * k:4 * k + 4]
            first_out = 4 * n + len(after)
            outs = refs[first_out + 4 * k:first_out + 4 * k + 4]
            at = (0, 0) if w.ndim == 5 else (0,)

            def update(l_ref=l_ref, w_ref=w_ref, m_ref=m_ref, v_ref=v_ref, outs=outs, at=at):
                g = l_ref[(0,) + at[1:]].astype(F32)
                for i in range(1, N_DEV):
                    g = g + l_ref[(i,) + at[1:]].astype(F32)
                dlt, m2, v2 = _adam(w_ref[at], g, m_ref[at], v_ref[at])
                for o_ref, val in zip(outs, (g, dlt, m2, v2)):
                    o_ref[at] = val

            if w.ndim == 5:
                update()
            else:
                pl.when(pl.program_id(0) == 0)(update)

    res = _pallas(
        body, name=name, grid=(2,),
        in_specs=in_specs + [pl.BlockSpec(memory_space=pl.ANY)] * len(after),
        out_specs=out_specs, out_shape=out_shape,
        compiler_params=_params(("arbitrary",)),
    )(*operands, *after)
    return [list(res[4 * k:4 * k + 4]) for k in range(n)]


SMALL_PARAMS = ("ln_g", "ln_b", "ev_sg_ln_g", "ev_sg_ln_b", "ev_sink", "ev_sg_b",
                "od_conv_w", "od_conv_b", "od_b_a", "od_b_x", "od_lam")


def _small_update(ga, gc, gd, gf, gb, ge, gsink, gbt, params):
    names = list(SMALL_PARAMS)
    flat = [a for nm in names for a in params[nm]]
    n_g = 8

    def body(*refs):
        ga_ref, gc_ref, gd_ref, gf_ref, gb_ref, ge_ref, gs_ref, gbt_ref = refs[:n_g]
        prm = refs[n_g:n_g + 3 * len(names)]
        loss_ref = refs[n_g + 3 * len(names)]
        outs = refs[n_g + 3 * len(names) + 1:]

        def ssum(ref):
            acc = ref[0]
            for i in range(1, N_DEV):
                acc = acc + ref[i]
            return acc

        a, cc, dd, ff, bb, ee = ssum(ga_ref), ssum(gc_ref), ssum(gd_ref), ssum(gf_ref), ssum(gb_ref), ssum(ge_ref)
        loss_ref[...] = a[3:4, 0:LANE]
        me = _slot(*_my_pos())

        def mine(rows):
            acc = jnp.zeros((rows.shape[0], LANE), F32)
            for j in range(N_DEV):
                acc = acc + jnp.where(me == j, rows[:, j * LANE:(j + 1) * LANE], 0.0)
            return acc

        sink_terms = ssum(gs_ref)
        lane8 = lax.broadcasted_iota(jnp.int32, (1, N_HEADS), 1)
        g_sink = jnp.zeros((1, N_HEADS), F32)
        for h in range(N_HEADS):
            tot = -jnp.sum(sink_terms[:, h * LANE:(h + 1) * LANE], axis=1, keepdims=True)
            g_sink = jnp.where(lane8 == h, tot, g_sink)
        grads = dict(
            ln_g=jnp.concatenate([dd[0:1], a[0:1]], axis=0), ln_b=jnp.concatenate([dd[1:2], a[1:2]], axis=0),
            ev_sg_ln_g=ee[0:1], ev_sg_ln_b=ee[1:2], ev_sink=g_sink,
            ev_sg_b=jnp.transpose(ssum(gbt_ref))[0:SG_GROUPS, :],
            od_conv_w=mine(cc[0:4]), od_conv_b=mine(cc[4:5]),
            od_b_a=mine(jnp.concatenate([ff[0:1], bb[0:1]], axis=0)),
            od_b_x=mine(jnp.concatenate([ff[1:2], bb[1:2]], axis=0)),
            od_lam=mine(jnp.concatenate([ff[2:3], bb[2:3]], axis=0)))
        for k, nm in enumerate(names):
            w_ref, m_ref, v_ref = prm[3 * k:3 * k + 3]
            at = (0,) if len(w_ref.shape) == 3 else ()
            g = grads[nm]
            dlt, m2, v2 = _adam(w_ref[at] if at else w_ref[...], g, m_ref[at] if at else m_ref[...],
                                v_ref[at] if at else v_ref[...])
            for o_ref, val in zip(outs[4 * k:4 * k + 4], (g, dlt, m2, v2)):
                if at:
                    o_ref[at] = val
                else:
                    o_ref[...] = val

    gathered = [ga, gc, gd, gf, gb, ge, gsink, gbt]
    out_shape = [jax.ShapeDtypeStruct((1, LANE), F32)]
    for nm in names:
        out_shape += [jax.ShapeDtypeStruct(params[nm][0].shape, F32)] * 4
    return _pallas(
        body, name="small_update", grid=(1,),
        in_specs=[_full(a.shape) for a in gathered + flat],
        out_specs=[_full(s.shape) for s in out_shape], out_shape=out_shape,
        compiler_params=_params(("arbitrary",)),
    )(*gathered, *flat)


VEC_ROWS = 16
VEC_LAYOUT = (("od_conv_w", 4), ("od_conv_b", 1), ("od_b_a", 2), ("od_b_x", 2), ("od_lam", 2))


def _from_slabs(slabs):
    n, R, cp = slabs.shape
    return slabs.transpose(1, 0, 2).reshape(R, n * cp)


def kernel(x, c, positions, ada_w, ada_b, ln_g, ln_b, ev_w_in, ev_w_out, ev_sink, ev_sg_ln_g, ev_sg_ln_b, ev_sg_w, ev_sg_b, od_w_in, od_conv_w, od_conv_b, od_w_a, od_b_a, od_w_x, od_b_x, od_lam, od_w_out, loss_target, m_ada_w, m_ada_b, m_ln_g, m_ln_b, m_ev_w_in, m_ev_w_out, m_ev_sink, m_ev_sg_ln_g, m_ev_sg_ln_b, m_ev_sg_w, m_ev_sg_b, m_od_w_in, m_od_conv_w, m_od_conv_b, m_od_w_a, m_od_b_a, m_od_w_x, m_od_b_x, m_od_lam, m_od_w_out, v_ada_w, v_ada_b, v_ln_g, v_ln_b, v_ev_w_in, v_ev_w_out, v_ev_sink, v_ev_sg_ln_g, v_ev_sg_ln_b, v_ev_sg_w, v_ev_sg_b, v_od_w_in, v_od_conv_w, v_od_conv_b, v_od_w_a, v_od_b_a, v_od_w_x, v_od_b_x, v_od_lam, v_od_w_out):
    T = x.shape[1]
    me = _slot(*_my_pos())
    xs = x.reshape(T, D)
    tgt = loss_target.reshape(T, D)

    c_all, mod_all, g_vec, (g_ev_in,), (s_ev_out, s_od_in, s_od_out, sg_w, wa, wx) = _head_gather(
        c, ada_w, [ev_w_in[0].T.astype(BF16)],
        [ev_w_out[0], od_w_in[0], od_w_out[0], ev_sg_w[0], od_w_a[0], od_w_x[0]],
        [od_conv_w, od_conv_b, od_b_a, od_b_x, od_lam])
    c_all = c_all.reshape(N_DEV, D)
    w_ev_in = g_ev_in.reshape(EV_IN, D)
    vec_full = _from_slabs(g_vec)
    cw, cb = vec_full[0:4], vec_full[4:5]
    ba, bx, lam = vec_full[5:7], vec_full[7:9], vec_full[9:11]
    mod_mine = lax.dynamic_index_in_dim(mod_all, me, axis=2, keepdims=False)
    mod = mod_mine.transpose(1, 0, 2).reshape(2, 3 * D) + ada_b
    mod0 = mod[0].reshape(3, D)
    mod1 = mod[1].reshape(3, D)

    half = 8
    inv_freq = jnp.power(jnp.float32(ROPE_THETA), -jnp.arange(half, dtype=F32) / half)
    ang = positions.reshape(T).astype(F32)[:, None] * inv_freq
    cos_t = jnp.tile(jnp.cos(ang), (1, LANE // half))
    sin_t = jnp.tile(jnp.sin(ang), (1, LANE // half))
    l64 = jnp.arange(LANE) % HEAD_DIM
    rc = jnp.where(l64 < 2 * half, cos_t, 1.0)
    rs1 = jnp.where(l64 < half, -sin_t, 0.0)
    rs2 = jnp.where((l64 >= half) & (l64 < 2 * half), sin_t, 0.0)

    ln0 = jnp.stack([ln_g[0], ln_b[0]])
    ln1 = jnp.stack([ln_g[1], ln_b[1]])
    sg_lng = ev_sg_ln_g
    sg_lnb = ev_sg_ln_b
    sg_bfull = jnp.repeat(ev_sg_b[0].T, SG_DIM, axis=1)
    sink_l = jnp.repeat(ev_sink, LANE, axis=1)
    kj = jnp.arange(3 * BLK)[:, None]
    qi = jnp.arange(BLK)[None, :]
    band_bias = jnp.where(jnp.abs(kj - BLK - qi) <= BLK, 0.0, NEG_INF).astype(F32)
    lanes = jnp.arange(LANE)
    lanes2 = jnp.arange(2 * LANE)
    a128 = jnp.where(lanes2[:, None] // SG_DIM == lanes2[None, :] // SG_DIM, 1.0 / SG_DIM, 0.0).astype(BF16)
    gsum = (jnp.arange(SG_W)[:, None] // SG_DIM == lanes[None, :]).astype(BF16)
    sel = (jnp.arange(SUBLANE)[:, None] == lanes[None, :] // HEAD_DIM).astype(BF16)

    (q, kvx, su, sv, g0), _ = _ev_in(xs, mod0, w_ev_in, rc, rs1, rs2)
    (ycat, y0, lse), (g_ev_out, g_od_in, g_od_out) = _mix0_fwd(
        q, kvx, su, sv, g0, sink_l, band_bias, a128, sg_lng, sg_lnb, sg_w, sg_bfull,
        _GatherComm([s_ev_out, s_od_in, s_od_out], mid_frac=0.75))
    w_ev_out = g_ev_out.reshape(D, D)
    w_od_in = _from_slabs(g_od_in)
    w_od_out = g_od_out.reshape(D, D)
    out0, z0, x1 = _ev_out(y0, w_ev_out, xs, mod0, ln0)
    xr, g1 = _od_in(x1, mod1, w_od_in)
    fwd_f = _rglru_fwd(xr, cw, cb, wa[0], wx[0], ba[0:1], bx[0:1], lam[0:1], False, "rglru_fwd_f")
    fwd_b = _rglru_fwd(xr, cw, cb, wa[1], wx[1], ba[1:2], bx[1:2], lam[1:2], True, "rglru_fwd_b")
    dh, dg1, dx1p, d_od_out, vec_a = _od_out(fwd_f[0], fwd_b[0], g1, w_od_out, x1, tgt, mod1, ln1)

    (dxcf, dwa_f, dwx_f, vec_f), (l_od_out,) = _rglru_bwd(
        fwd_f, dh, wa[0], wx[0], lam[0:1], False, "rglru_bwd_f",
        _ExchangeComm([d_od_out.reshape(N_DEV, D // N_DEV, D)]))
    (dxcb, dwa_b, dwx_b, vec_b), _ = _rglru_bwd(fwd_b, dh, wa[1], wx[1], lam[1:2], True, "rglru_bwd_b")
    (dx1, d_od_in, vec_c), (a_wa, a_wx) = _od_in_bwd(
        dxcf, dxcb, xr, dg1, x1, dx1p, mod1, w_od_in, cw,
        _GatherComm([jnp.stack([dwa_f, dwa_b]).astype(BF16), jnp.stack([dwx_f, dwx_b]).astype(BF16)],
                    mid_frac=0.75))
    dxp, dyc, dg0, d_ev_out, vec_d = _ev_out_bwd(dx1, z0, out0, y0, ycat, g0, w_ev_out, mod0, ln0)
    (dq, dkv, dsu, dsv, d_sg_w, d_sg_bt, vec_e, d_sink_l), (l_od_in, l_ev_out, ga, gc, gd, gf, gb) = _mix0_bwd(
        q, kvx, lse, dyc, ycat, su, sv, sink_l, band_bias, a128, gsum, sel, sg_lng, sg_lnb, sg_w, sg_bfull,
        rc, rs1, rs2, _BothComm(_ExchangeComm([d_od_in, d_ev_out.reshape(N_DEV, D // N_DEV, D)]),
                                _GatherComm([vec_a, vec_c, vec_d, vec_f, vec_b], mid_frac=0.9)))
    (grad_x, d_ev_in, vec_g), _ = _ev_in_bwd(dq, dkv, dsu, dsv, dg0, xs, dxp, mod0, w_ev_in, rc, rs1, rs2)

    part, land, (gg, ge, gsink, gbt, a_sgw) = _tail_stage1(
        d_ev_in.reshape(N_DEV, EV_IN // N_DEV, D), [vec_g, vec_e, d_sink_l, d_sg_bt, d_sg_w.astype(BF16)])
    send_sems, recv_sems, part, land, token = _tail_send(part, land)

    dmod_all = jnp.stack([jnp.concatenate([gg[:, 0], gg[:, 1], gd[:, 2]], axis=-1),
                          jnp.concatenate([gc[:, 5], gc[:, 6], ga[:, 2]], axis=-1)], axis=1)
    cols = ada_w.shape[2]
    dmod_cols = lax.dynamic_slice_in_dim(dmod_all, me * cols, cols, axis=2).transpose(1, 0, 2)
    (g_ada_w, d_ada_w, nm_ada_w, nv_ada_w, g_ada_b, d_ada_b, nm_ada_b, nv_ada_b) = _ada_update(
        c_all, dmod_cols, dmod_all, ada_w, m_ada_w, v_ada_w, ada_b, m_ada_b, v_ada_b)

    res = dict(ada_w=[g_ada_w, d_ada_w, nm_ada_w, nv_ada_w], ada_b=[g_ada_b, d_ada_b, nm_ada_b, nv_ada_b])
    (r_od_in,) = _reduce_adam([(l_od_in, od_w_in[0], m_od_w_in[0], v_od_w_in[0])], "adam_od_w_in", after=[token])
    r_ev_out, r_od_out = _reduce_adam([(l_ev_out, ev_w_out[0], m_ev_w_out[0], v_ev_w_out[0]),
                                       (l_od_out, od_w_out[0], m_od_w_out[0], v_od_w_out[0])], "adam_w_out",
                                      after=[token])
    for name, r in (("od_w_in", r_od_in), ("ev_w_out", r_ev_out), ("od_w_out", r_od_out)):
        res[name] = [a[None] for a in r]
    res["od_w_a"], res["od_w_x"], res["ev_sg_w"] = _slots_adam(
        [(a_wa, od_w_a, m_od_w_a, v_od_w_a), (a_wx, od_w_x, m_od_w_x, v_od_w_x),
         (a_sgw, ev_sg_w, m_ev_sg_w, v_ev_sg_w)], "adam_gates", after=[token])
    small = dict(ln_g=(ln_g, m_ln_g, v_ln_g), ln_b=(ln_b, m_ln_b, v_ln_b),
                 ev_sg_ln_g=(ev_sg_ln_g, m_ev_sg_ln_g, v_ev_sg_ln_g),
                 ev_sg_ln_b=(ev_sg_ln_b, m_ev_sg_ln_b, v_ev_sg_ln_b),
                 ev_sink=(ev_sink, m_ev_sink, v_ev_sink), ev_sg_b=(ev_sg_b, m_ev_sg_b, v_ev_sg_b),
                 od_conv_w=(od_conv_w, m_od_conv_w, v_od_conv_w), od_conv_b=(od_conv_b, m_od_conv_b, v_od_conv_b),
                 od_b_a=(od_b_a, m_od_b_a, v_od_b_a), od_b_x=(od_b_x, m_od_b_x, v_od_b_x),
                 od_lam=(od_lam, m_od_lam, v_od_lam))
    small_out = _small_update(ga, gc, gd, gf, gb, ge, gsink, gbt, small)
    l_ev_in = _tail_wait(send_sems, recv_sems, part, land,
                         [r_od_in[0], r_od_out[0], res["od_w_x"][0], g_ada_w, small_out[0]])
    (r_ev_in,) = _reduce_adam([(l_ev_in, ev_w_in[0].T, m_ev_w_in[0].T, v_ev_w_in[0].T)], "adam_ev_w_in")
    res["ev_w_in"] = [a.T[None] for a in r_ev_in]
    loss = small_out[0][0, 0]
    for k, name in enumerate(SMALL_PARAMS):
        res[name] = small_out[1 + 4 * k:5 + 4 * k]

    order = ["ada_w", "ada_b", "ln_g", "ln_b", "ev_w_in", "ev_w_out", "ev_sink", "ev_sg_ln_g", "ev_sg_ln_b",
             "ev_sg_w", "ev_sg_b", "od_w_in", "od_conv_w", "od_conv_b", "od_w_a", "od_b_a", "od_w_x", "od_b_x",
             "od_lam", "od_w_out"]
    outs = [loss, grad_x.reshape(1, T, D)]
    for kind in range(4):
        outs += [res[name][kind] for name in order]
    return tuple(outs)
```

```python
import jax
import jax.numpy as jnp
from jax import lax
from jax.experimental import pallas as pl
from jax.experimental.pallas import tpu as pltpu

F32 = jnp.float32
BF16 = jnp.bfloat16

N_DEV = 8
D = 1024
N_HEADS = 8
HEAD_DIM = 64
ATTN_W = 512
SG_W = 512
SG_GROUPS = 8
SG_DIM = 64
BLK = 128
KVX_W = 1024
EV_IN = 2816
OD_IN = 2048
RNN_HEADS = 8
RNN_HD = 128
ALPHA = 4.0 ** 0.25
LN_EPS = 1e-5
NEG_INF = -1e30
RG_C = 8.0
ROPE_THETA = 500000.0
LR, B1, B2, EPS, WD, STEP = 0.001, 0.9, 0.999, 1e-08, 0.01, 10

LANE = 128
SUBLANE = 8
TM = 256
TMF = 512
TMO = 512
TS = 256
FWD_BLOCKS = 4
VMEM_LIMIT = 56 * 1024 * 1024

MESH = pl.DeviceIdType.MESH


def _pallas(body, **kw):
    return pl.pallas_call(body, **kw)


def _params(sem, vmem=VMEM_LIMIT):
    return pltpu.CompilerParams(dimension_semantics=sem, vmem_limit_bytes=vmem)


def _sigmoid(x):
    return 0.5 * jnp.tanh(0.5 * x) + 0.5


def _silu_and_grad(x):
    s = _sigmoid(x)
    return x * s, s * (1.0 + x * (1.0 - s))


def _dot(a, b):
    return jnp.dot(a.astype(BF16), b.astype(BF16), preferred_element_type=F32)


def _dot_nt(a, b):
    return lax.dot_general(a.astype(BF16), b.astype(BF16), (((1,), (1,)), ((), ())), preferred_element_type=F32)


def _dot_tn(a, b):
    return lax.dot_general(a.astype(BF16), b.astype(BF16), (((0,), (0,)), ((), ())), preferred_element_type=F32)


def _ln_fwd(z, g, b):
    mu = jnp.mean(z, axis=-1, keepdims=True)
    zc = z - mu
    var = jnp.mean(zc * zc, axis=-1, keepdims=True)
    rstd = lax.rsqrt(var + LN_EPS)
    xhat = zc * rstd
    return xhat * g + b, xhat, rstd


def _ln_bwd(dy, xhat, rstd, g):
    dxh = dy * g
    m1 = jnp.mean(dxh, axis=-1, keepdims=True)
    m2 = jnp.mean(dxh * xhat, axis=-1, keepdims=True)
    return rstd * (dxh - m1 - xhat * m2)


def _rowsum(v):
    return jnp.sum(v, axis=0, keepdims=True)


def _rope_fwd(t, c, s1, s2):
    return t * c + pltpu.roll(t, LANE - 8, 1) * s1 + pltpu.roll(t, 8, 1) * s2


def _rope_bwd(d, c, s1, s2):
    return d * c + pltpu.roll(d * s1, 8, 1) + pltpu.roll(d * s2, LANE - 8, 1)


def _adam(w, g, m, v):
    m2 = B1 * m + (1.0 - B1) * g
    v2 = B2 * v + (1.0 - B2) * (g * g)
    m_hat = m2 / (1.0 - B1 ** STEP)
    v_hat = v2 / (1.0 - B2 ** STEP)
    delta = -LR * (m_hat / (jnp.sqrt(v_hat) + EPS) + WD * w)
    return delta, m2, v2


def _tile(rows, width):
    return pl.BlockSpec((rows, width), lambda i: (i, 0))


def _full(shape):
    zeros = (0,) * len(shape)
    return pl.BlockSpec(shape, lambda i: zeros)


def _rev_tile(rows, width, n, reverse):
    if reverse:
        return pl.BlockSpec((rows, width), lambda i: (n - 1 - i, 0))
    return pl.BlockSpec((rows, width), lambda i: (i, 0))


def _halo_specs(rows, width, n, total_rows, reverse):
    per = rows // SUBLANE
    last = total_rows // SUBLANE - 1

    def tile_of(i):
        return (n - 1 - i) if reverse else i

    prev = pl.BlockSpec((SUBLANE, width), lambda i: (jnp.maximum(tile_of(i) * per - 1, 0), 0))
    nxt = pl.BlockSpec((SUBLANE, width), lambda i: (jnp.minimum((tile_of(i) + 1) * per, last), 0))
    return prev, nxt


def _my_pos():
    return lax.axis_index("x"), lax.axis_index("y"), lax.axis_index("c")


def _slot(px, py, pc):
    return 4 * px + 2 * py + pc


class _GatherComm:
    has_mid = True

    def __init__(self, arrs, mid_frac=0.5):
        self.arrs = list(arrs)
        self.n = len(self.arrs)
        self.mid_frac = mid_frac

    def out_shapes(self):
        return [jax.ShapeDtypeStruct((N_DEV,) + a.shape, a.dtype) for a in self.arrs]

    def sems(self):
        return [pltpu.SemaphoreType.DMA((7 * self.n,)), pltpu.SemaphoreType.DMA((7 * self.n,)),
                pltpu.SemaphoreType.DMA((self.n,))]

    def _parts(self, ins, outs, sems):
        send_sems, recv_sems, local_sems = sems
        x, y, c = _my_pos()
        me, sibling = (x, y, c), (x, y, 1 - c)
        chips = [(1 - x, y), (x, 1 - y), (1 - x, 1 - y)]

        def copy(a, k, block, to, src=None):
            dst = outs[a].at[_slot(*block)]
            return pltpu.make_async_remote_copy(
                src_ref=dst if src is None else src, dst_ref=dst,
                send_sem=send_sems.at[a * 7 + k], recv_sem=recv_sems.at[a * 7 + k],
                device_id=to, device_id_type=MESH)

        local = [pltpu.make_async_copy(ins[a], outs[a].at[_slot(*me)], local_sems.at[a]) for a in range(self.n)]
        first = []
        for a in range(self.n):
            first.append(copy(a, 0, me, sibling, src=ins[a]))
            first += [copy(a, 1 + j, me, (*chip, c), src=ins[a]) for j, chip in enumerate(chips)]
        ici_in = [copy(a, 1 + j, (*chip, c), me) for j, chip in enumerate(chips) for a in range(self.n)]
        passed = [copy(a, 4 + j, (*chip, c), sibling) for j, chip in enumerate(chips) for a in range(self.n)]
        d2d_in = []
        for a in range(self.n):
            d2d_in.append(copy(a, 0, sibling, me))
            d2d_in += [copy(a, 4 + j, (*chip, 1 - c), me) for j, chip in enumerate(chips)]
        return local, first, ici_in, passed, d2d_in

    def start(self, ins, outs, sems):
        local, first, _, _, _ = self._parts(ins, outs, sems)
        for cp in local + first:
            cp.start()

    def mid(self, ins, outs, sems):
        _, _, ici_in, passed, _ = self._parts(ins, outs, sems)
        for arrived, fw in zip(ici_in, passed):
            arrived.wait_recv()
            fw.start()

    def finish(self, ins, outs, sems):
        local, first, _, passed, d2d_in = self._parts(ins, outs, sems)
        for cp in d2d_in:
            cp.wait_recv()
        for cp in first + passed:
            cp.wait_send()
        for cp in local:
            cp.wait()


class _ExchangeComm:
    has_mid = False

    def __init__(self, arrs):
        self.arrs = list(arrs)
        self.n = len(self.arrs)

    def out_shapes(self):
        return [jax.ShapeDtypeStruct(a.shape, a.dtype) for a in self.arrs]

    def sems(self):
        return [pltpu.SemaphoreType.DMA((7 * self.n,)), pltpu.SemaphoreType.DMA((7 * self.n,)),
                pltpu.SemaphoreType.DMA((self.n,))]

    def _copies(self, ins, outs, sems):
        send_sems, recv_sems, local_sems = sems
        x, y, c = _my_pos()
        mine = _slot(x, y, c)
        copies = [pltpu.make_async_copy(ins[a].at[mine], outs[a].at[mine], local_sems.at[a]) for a in range(self.n)]
        for k in range(1, N_DEV):
            px = (1 - x) if (k & 4) else x
            py = (1 - y) if (k & 2) else y
            pc = (1 - c) if (k & 1) else c
            for a in range(self.n):
                copies.append(pltpu.make_async_remote_copy(
                    src_ref=ins[a].at[_slot(px, py, pc)], dst_ref=outs[a].at[mine],
                    send_sem=send_sems.at[a * 7 + k - 1], recv_sem=recv_sems.at[a * 7 + k - 1],
                    device_id=(px, py, pc), device_id_type=MESH))
        return copies

    def start(self, ins, outs, sems):
        for cp in self._copies(ins, outs, sems):
            cp.start()

    def finish(self, ins, outs, sems):
        for cp in self._copies(ins, outs, sems):
            cp.wait()


class _BothComm:
    has_mid = True

    def __init__(self, first, second):
        self.parts = (first, second)
        self.arrs = first.arrs + second.arrs
        self.n = first.n + second.n
        self.mid_frac = second.mid_frac

    def out_shapes(self):
        return self.parts[0].out_shapes() + self.parts[1].out_shapes()

    def sems(self):
        return self.parts[0].sems() + self.parts[1].sems()

    def _each(self, ins, outs, sems):
        a, b = self.parts
        return ((a, ins[:a.n], outs[:a.n], sems[:3]), (b, ins[a.n:], outs[a.n:], sems[3:]))

    def start(self, ins, outs, sems):
        for cm, i_, o_, s_ in self._each(ins, outs, sems):
            cm.start(i_, o_, s_)

    def mid(self, ins, outs, sems):
        for cm, i_, o_, s_ in self._each(ins, outs, sems):
            if cm.has_mid:
                cm.mid(i_, o_, s_)

    def finish(self, ins, outs, sems):
        for cm, i_, o_, s_ in self._each(ins, outs, sems):
            cm.finish(i_, o_, s_)


def _fused_call(body, comm, operands, *, name, grid, in_specs, out_specs, out_shape, scratch_shapes=(),
                semantics=("arbitrary",)):
    n_in, n_out, n_scr = len(in_specs), len(out_specs), len(scratch_shapes)
    if comm is None:
        res = _pallas(body, name=name, grid=grid, in_specs=list(in_specs), out_specs=list(out_specs),
                      out_shape=list(out_shape), scratch_shapes=list(scratch_shapes),
                      compiler_params=_params(semantics))(*operands)
        return list(res), []
    k = comm.n
    steps = grid[0]

    def wrapped(*refs):
        ins, cins = refs[:n_in], refs[n_in:n_in + k]
        outs = refs[n_in + k:n_in + k + n_out]
        couts = refs[n_in + k + n_out:n_in + 2 * k + n_out]
        rest = refs[n_in + 2 * k + n_out:]
        scratch, sems = rest[:n_scr], rest[n_scr:]
        i = pl.program_id(0)

        @pl.when(i == 0)
        def _():
            comm.start(cins, couts, sems)

        body(*ins, *outs, *scratch)

        if comm.has_mid:
            @pl.when(i == int(steps * comm.mid_frac))
            def _():
                comm.mid(cins, couts, sems)

        @pl.when(i == steps - 1)
        def _():
            comm.finish(cins, couts, sems)

    any_spec = pl.BlockSpec(memory_space=pl.ANY)
    res = _pallas(wrapped, name=name, grid=grid, in_specs=list(in_specs) + [any_spec] * k,
                  out_specs=list(out_specs) + [any_spec] * k, out_shape=list(out_shape) + comm.out_shapes(),
                  scratch_shapes=list(scratch_shapes) + comm.sems(),
                  compiler_params=_params(("arbitrary",)))(*operands, *comm.arrs)
    return list(res[:n_out]), list(res[n_out:])


def _head_gather(c, ada_w, big, to_cast, vec_parts):
    cols = ada_w.shape[2]
    g_c, g_big = _GatherComm([c]), _GatherComm(big)
    g_mod = _GatherComm([jax.ShapeDtypeStruct((2, N_DEV, cols), F32)])
    g_vec = _GatherComm([jax.ShapeDtypeStruct((VEC_ROWS, LANE), F32)])
    nb, nc, nv = g_big.n, len(to_cast), len(vec_parts)

    def body(*refs):
        c_ref, w_ref = refs[0], refs[1]
        vec_in = refs[2:2 + nv]
        cast_in = refs[2 + nv:2 + nv + nc]
        big_in = refs[2 + nv + nc:2 + nv + nc + nb]
        outs = refs[2 + nv + nc + nb:]
        c_all_ref, mod_all_ref, vec_all_ref = outs[0], outs[1], outs[2]
        cast_out = outs[3:3 + nc]
        big_out = outs[3 + nc:3 + nc + nb]
        part_ref, pack_ref = outs[3 + nc + nb], outs[4 + nc + nb]
        sems = outs[5 + nc + nb:]
        s_c, s_mod, s_big, s_vec = sems[0:3], sems[3:6], sems[6:9], sems[9:12]
        g_c.start([c_ref], [c_all_ref], s_c)
        g_big.start(big_in, big_out, s_big)
        pack_ref[...] = jnp.zeros_like(pack_ref)
        row = 0
        for ref, (_, nrows) in zip(vec_in, VEC_LAYOUT):
            pack_ref[row:row + nrows, :] = ref[0] if len(ref.shape) == 3 else ref[...]
            row += nrows
        g_vec.start([pack_ref], [vec_all_ref], s_vec)
        g_c.mid([c_ref], [c_all_ref], s_c)
        g_c.finish([c_ref], [c_all_ref], s_c)
        cv = c_all_ref[:, 0, :]
        cond = cv * _sigmoid(cv)
        for l in range(2):
            part_ref[l] = _dot(cond, w_ref[l])
        g_mod.start([part_ref], [mod_all_ref], s_mod)
        for src, dst in zip(cast_in, cast_out):
            dst[...] = src[...].astype(BF16)
        for g, ins, outs_, sm in ((g_vec, [pack_ref], [vec_all_ref], s_vec), (g_mod, [part_ref], [mod_all_ref], s_mod),
                                  (g_big, big_in, big_out, s_big)):
            g.mid(ins, outs_, sm)
            g.finish(ins, outs_, sm)

    any_spec = pl.BlockSpec(memory_space=pl.ANY)
    vmem_spec = pl.BlockSpec(memory_space=pltpu.VMEM)
    res = _pallas(
        body, name="head_gather",
        out_shape=(g_c.out_shapes() + g_mod.out_shapes() + g_vec.out_shapes()
                   + [jax.ShapeDtypeStruct(a.shape, BF16) for a in to_cast] + g_big.out_shapes()),
        in_specs=[vmem_spec] * (2 + nv + nc) + [any_spec] * nb,
        out_specs=[vmem_spec] * (3 + nc) + [any_spec] * nb,
        scratch_shapes=[pltpu.VMEM((2, N_DEV, cols), F32), pltpu.VMEM((VEC_ROWS, LANE), F32)]
        + g_c.sems() + g_mod.sems() + g_big.sems() + g_vec.sems(),
        compiler_params=pltpu.CompilerParams(vmem_limit_bytes=VMEM_LIMIT),
    )(c, ada_w, *vec_parts, *to_cast, *big)
    return res[0], res[1], res[2], list(res[3 + nc:]), list(res[3:3 + nc])


def _ada_update(c_all, dmod_cols, dmod_all, ada_w, m_w, v_w, ada_b, m_b, v_b):
    cols = ada_w.shape[2]
    nb = ada_b.shape[1]

    def body(c_ref, dmc_ref, dma_ref, w_ref, mw_ref, vw_ref, b_ref, mb_ref, vb_ref,
             gw_ref, dw_ref, nmw_ref, nvw_ref, gb_ref, db_ref, nmb_ref, nvb_ref):
        cv = c_ref[...]
        cond = cv * _sigmoid(cv)
        for l in range(2):
            g = _dot_tn(cond, dmc_ref[l])
            gw_ref[l] = g
            dlt, m2, v2 = _adam(w_ref[l], g, mw_ref[l], vw_ref[l])
            dw_ref[l] = dlt
            nmw_ref[l] = m2
            nvw_ref[l] = v2
        gb = dma_ref[0]
        for i in range(1, N_DEV):
            gb = gb + dma_ref[i]
        gb_ref[...] = gb
        dlt, m2, v2 = _adam(b_ref[...], gb, mb_ref[...], vb_ref[...])
        db_ref[...] = dlt
        nmb_ref[...] = m2
        nvb_ref[...] = v2

    wspec = _full((2, D, cols))
    bspec = _full((2, nb))
    wshape = jax.ShapeDtypeStruct((2, D, cols), F32)
    bshape = jax.ShapeDtypeStruct((2, nb), F32)
    return _pallas(
        body, name="ada_update", grid=(1,),
        in_specs=[_full((N_DEV, D)), _full((2, N_DEV, cols)), _full((N_DEV, 2, nb)),
                  wspec, wspec, wspec, bspec, bspec, bspec],
        out_specs=[wspec] * 4 + [bspec] * 4,
        out_shape=[wshape] * 4 + [bshape] * 4,
        compiler_params=_params(("arbitrary",)),
    )(c_all, dmod_cols, dmod_all, ada_w, m_w, v_w, ada_b, m_b, v_b)


def _ev_in(x, mod, w_in, rc, rs1, rs2, comm=None):
    T = x.shape[0]

    def body(x_ref, mod_ref, w_ref, c_ref, s1_ref, s2_ref, q_ref, kv_ref, su_ref, sv_ref, g_ref):
        h = x_ref[...] * (1.0 + mod_ref[1:2, :]) + mod_ref[0:1, :]
        p = _dot_nt(h, w_ref[...])
        c, s1, s2 = c_ref[...], s1_ref[...], s2_ref[...]
        for j in range(ATTN_W // LANE):
            qr = _rope_fwd(p[:, j * LANE:(j + 1) * LANE], c, s1, s2)
            q_ref[:, j * LANE:(j + 1) * LANE] = (qr * (HEAD_DIM ** -0.5)).astype(BF16)
        low = lax.broadcasted_iota(jnp.int32, (TMF, LANE), 1) < HEAD_DIM
        for j, val in enumerate((_rope_fwd(p[:, 512:640], c, s1, s2), p[:, 640:768])):
            swapped = pltpu.roll(val, HEAD_DIM, 1)
            tiles = (jnp.where(low, val, 0.0), jnp.where(low, 0.0, swapped),
                     jnp.where(low, swapped, 0.0), jnp.where(low, 0.0, val))
            for k, tile in enumerate(tiles):
                kv_ref[:, (4 * j + k) * LANE:(4 * j + k + 1) * LANE] = tile.astype(BF16)
        su_ref[...] = p[:, 768:1280].astype(BF16)
        sv_ref[...] = p[:, 1280:1792].astype(BF16)
        g_ref[...] = p[:, 1792:2816].astype(BF16)

    sh = lambda w: jax.ShapeDtypeStruct((T, w), BF16)
    return _fused_call(
        body, comm, (x, mod, w_in, rc, rs1, rs2), name="ev_in", grid=(T // TMF,),
        in_specs=[_tile(TMF, D), _full((3, D)), _full((EV_IN, D)), _tile(TMF, LANE), _tile(TMF, LANE),
                  _tile(TMF, LANE)],
        out_specs=[_tile(TMF, ATTN_W), _tile(TMF, KVX_W), _tile(TMF, SG_W), _tile(TMF, SG_W), _tile(TMF, D)],
        out_shape=[sh(ATTN_W), sh(KVX_W), sh(SG_W), sh(SG_W), sh(D)], semantics=("parallel",))


def _band_specs(width, nb, k):
    return [pl.BlockSpec((BLK, width), lambda n: (jnp.maximum(k * n - 1, 0), 0)),
            pl.BlockSpec((k * BLK, width), lambda n: (n, 0)),
            pl.BlockSpec((BLK, width), lambda n: (jnp.minimum(k * n + k, nb - 1), 0))]


def _band_bias(bias_ref, n, nb):
    rows = lax.broadcasted_iota(jnp.int32, (3 * BLK, 1), 0)
    outside = ((rows < BLK) & (n == 0)) | ((rows >= 2 * BLK) & (n == nb - 1))
    return bias_ref[...] + jnp.where(outside, NEG_INF, 0.0)


def _lane_tile(ref, t):
    return ref[:, t * LANE:(t + 1) * LANE]


def _split_bf16(v):
    hi = v.astype(BF16)
    return hi, (v - hi.astype(F32)).astype(BF16)


def _group_mean(v, a_ref, exact_bf16=False):
    hi, lo = _split_bf16(v)
    a = a_ref[...]
    out = []
    for t in range(SG_W // (2 * LANE)):
        sl = slice(t * 2 * LANE, (t + 1) * 2 * LANE)
        r = jnp.dot(hi[:, sl], a, preferred_element_type=F32)
        if not exact_bf16:
            r = r + jnp.dot(lo[:, sl], a, preferred_element_type=F32)
        out.append(r)
    return jnp.concatenate(out, axis=-1)


def _sg_core(sv_ref, lng, lnb, a_ref, w_ref, bfull_ref):
    svf = sv_ref[...].astype(F32)
    xc = svf - _group_mean(svf, a_ref, exact_bf16=True)
    rstd = lax.rsqrt(_group_mean(xc * xc, a_ref) + LN_EPS)
    xhat = xc * rstd
    vb = (xhat * lng + lnb).astype(BF16)
    low = lax.broadcasted_iota(jnp.int32, (BLK, LANE), 1) < SG_DIM
    tiles = []
    for t in range(SG_W // LANE):
        v2 = vb[:, t * LANE:(t + 1) * LANE]
        r0 = jnp.dot(w_ref[2 * t], v2, preferred_element_type=F32)
        r1 = jnp.dot(w_ref[2 * t + 1], v2, preferred_element_type=F32)
        tiles.append(jnp.where(low, r0, r1))
    svm = jnp.concatenate(tiles, axis=-1) + bfull_ref[...]
    return xhat, rstd, vb, svm


def _mix0_fwd(q, kvx, su, sv, g0, sink_l, bias, a128, sg_lng, sg_lnb, sg_w, sg_bfull, comm=None):
    T = q.shape[0]
    nb = T // BLK

    def body(q_ref, kp_ref, kc_ref, kn_ref, su_ref, sv_ref, g_ref, sink_ref, bias_ref, a_ref, lng_ref, lnb_ref,
             w_ref, bfull_ref, ycat_ref, y0_ref, lse_ref):
        n = pl.program_id(0)
        kvx4 = jnp.concatenate([kp_ref[...], kc_ref[...], kn_ref[...]], axis=0)
        for s in range(FWD_BLOCKS):
            rows = slice(s * BLK, (s + 1) * BLK)
            bias = _band_bias(bias_ref, FWD_BLOCKS * n + s, nb)
            kvx = kvx4[s * BLK:s * BLK + 3 * BLK]
            tiles = []
            for t in range(ATTN_W // LANE):
                qt = q_ref[rows, t * LANE:(t + 1) * LANE]
                acc = None
                for par in range(2):
                    h = 2 * t + par
                    kt = 2 * (h // 4) + par
                    ke = kvx[:, kt * LANE:(kt + 1) * LANE]
                    ve = kvx[:, (4 + kt) * LANE:(5 + kt) * LANE]
                    st = _dot_nt(ke, qt) + bias
                    sk = _lane_tile(sink_ref, h)
                    m = jnp.maximum(jnp.max(st, axis=0, keepdims=True), sk)
                    p = jnp.exp(st - m)
                    denom = jnp.sum(p, axis=0, keepdims=True) + jnp.exp(sk - m)
                    contrib = _dot_tn(p * (1.0 / denom), ve)
                    acc = contrib if acc is None else acc + contrib
                    lse_ref[s, :, h * LANE:(h + 1) * LANE] = m + jnp.log(denom)
                tiles.append(acc)
            _, _, _, svm = _sg_core(sv_ref.at[rows, :], lng_ref[...], lnb_ref[...], a_ref, w_ref, bfull_ref)
            tiles.append(su_ref[rows, :].astype(F32) * svm)
            ycat = jnp.concatenate(tiles, axis=-1)
            gf = g_ref[rows, :].astype(F32)
            ycat_ref[rows, :] = ycat.astype(BF16)
            y0_ref[rows, :] = (ycat * (gf * _sigmoid(gf))).astype(BF16)

    two = FWD_BLOCKS * BLK
    return _fused_call(
        body, comm, (q, kvx, kvx, kvx, su, sv, g0, sink_l, bias, a128, sg_lng, sg_lnb, sg_w, sg_bfull),
        name="mix0_fwd", grid=(nb // FWD_BLOCKS,),
        in_specs=[_tile(two, ATTN_W)] + _band_specs(KVX_W, nb, FWD_BLOCKS) + [
                  _tile(two, SG_W), _tile(two, SG_W), _tile(two, D), _full((1, N_HEADS * LANE)),
                  _full((3 * BLK, LANE)), _full((2 * LANE, 2 * LANE)),_full((1, SG_W)), _full((1, SG_W)),
                  _full((SG_GROUPS, BLK, BLK)), _full((BLK, SG_W))],
        out_specs=[_tile(two, D), _tile(two, D),
                   pl.BlockSpec((FWD_BLOCKS, 1, N_HEADS * LANE), lambda n: (n, 0, 0))],
        out_shape=[jax.ShapeDtypeStruct((T, D), BF16), jax.ShapeDtypeStruct((T, D), BF16),
                   jax.ShapeDtypeStruct((nb, 1, N_HEADS * LANE), F32)], semantics=("parallel",))


def _ev_out(y0, w_out, x, mod, lnp):
    T = x.shape[0]

    def body(y_ref, w_ref, x_ref, mod_ref, ln_ref, out_ref, z_ref, x1_ref):
        out = _dot(y_ref[...], w_ref[...])
        z = ALPHA * x_ref[...] + mod_ref[2:3, :] * out
        x1, _, _ = _ln_fwd(z, ln_ref[0:1, :], ln_ref[1:2, :])
        out_ref[...] = out.astype(BF16)
        z_ref[...] = z
        x1_ref[...] = x1

    return _pallas(
        body, name="ev_out", grid=(T // TMF,),
        in_specs=[_tile(TMF, D), _full((D, D)), _tile(TMF, D), _full((3, D)), _full((2, D))],
        out_specs=[_tile(TMF, D)] * 3,
        out_shape=[jax.ShapeDtypeStruct((T, D), BF16), jax.ShapeDtypeStruct((T, D), F32),
                   jax.ShapeDtypeStruct((T, D), F32)],
        compiler_params=_params(("parallel",)),
    )(y0, w_out, x, mod, lnp)


def _od_in(x1, mod, w_in):
    T = x1.shape[0]

    def body(x_ref, mod_ref, w_ref, xr_ref, g_ref):
        h = x_ref[...] * (1.0 + mod_ref[1:2, :]) + mod_ref[0:1, :]
        p = _dot(h, w_ref[...])
        xr_ref[...] = p[:, :D]
        g_ref[...] = p[:, D:].astype(BF16)

    return _pallas(
        body, name="od_in", grid=(T // TMF,),
        in_specs=[_tile(TMF, D), _full((3, D)), _full((D, OD_IN))],
        out_specs=[_tile(TMF, D), _tile(TMF, D)],
        out_shape=[jax.ShapeDtypeStruct((T, D), F32), jax.ShapeDtypeStruct((T, D), BF16)],
        compiler_params=_params(("parallel",)),
    )(x1, mod, w_in)


def _ext_rows(prev_ref, cur, next_ref, j, n):
    prev = jnp.where(j > 0, prev_ref[...], 0.0)
    nxt = jnp.where(j < n - 1, next_ref[...], 0.0)
    return jnp.concatenate([prev, cur, nxt], axis=0)


def _shift_rows(ext, off, rows):
    total = ext.shape[0]
    if off == 0:
        return ext[SUBLANE:SUBLANE + rows, :]
    return pltpu.roll(ext, (-off) % total, 0)[SUBLANE:SUBLANE + rows, :]


def _conv_fwd(ext, cw, cb, rows):
    xc = cb
    for k in range(4):
        xc = xc + cw[k:k + 1, :] * _shift_rows(ext, k - 2, rows)
    return xc


def _gates(xc, wa_ref, wx_ref, ba, bx, lam):
    pr, pi = [], []
    for h in range(RNN_HEADS):
        xh = xc[:, h * RNN_HD:(h + 1) * RNN_HD].astype(BF16)
        pr.append(_dot(xh, wa_ref[h]))
        pi.append(_dot(xh, wx_ref[h]))
    r = _sigmoid(jnp.concatenate(pr, axis=-1) + ba)
    ig = _sigmoid(jnp.concatenate(pi, axis=-1) + bx)
    sp = jnp.maximum(-lam, 0.0) + jnp.log(1.0 + jnp.exp(-jnp.abs(lam)))
    neg_log_a = RG_C * r * sp
    a = jnp.exp(-neg_log_a)
    s2 = (1.0 + a * a) * jnp.tanh(neg_log_a)
    inv_s = lax.rsqrt(jnp.maximum(s2, 1e-30))
    return r, ig, sp, a, s2 * inv_s, inv_s


def _scan_tile(a_ref, b_ref, o_ref, carry_ref, rows, reverse):
    ridx = lax.broadcasted_iota(jnp.int32, (SUBLANE, D), 0)
    groups = rows // SUBLANE

    def group(gi, h):
        g = (groups - 1 - gi) if reverse else gi
        off = pl.multiple_of(g * SUBLANE, SUBLANE)
        a = a_ref[pl.ds(off, SUBLANE), :]
        b = b_ref[pl.ds(off, SUBLANE), :]
        for sh in (1, 2, 4):
            if reverse:
                keep = ridx < SUBLANE - sh
                a_p = jnp.where(keep, pltpu.roll(a, SUBLANE - sh, 0), 1.0)
                b_p = jnp.where(keep, pltpu.roll(b, SUBLANE - sh, 0), 0.0)
            else:
                keep = ridx >= sh
                a_p = jnp.where(keep, pltpu.roll(a, sh, 0), 1.0)
                b_p = jnp.where(keep, pltpu.roll(b, sh, 0), 0.0)
            b = b + a * b_p
            a = a * a_p
        hh = b + a * h
        o_ref[pl.ds(off, SUBLANE), :] = hh
        return hh[0:1, :] if reverse else hh[SUBLANE - 1:SUBLANE, :]

    carry_ref[...] = lax.fori_loop(0, groups, group, carry_ref[...])


def _rglru_fwd(xr, cw, cb, wa, wx, ba, bx, lam, reverse, name):
    T = xr.shape[0]
    n = T // TS
    prev_spec, next_spec = _halo_specs(TS, D, n, T, reverse)

    def body(prev_ref, cur_ref, next_ref, cw_ref, cb_ref, wa_ref, wx_ref, ba_ref, bx_ref, lam_ref,
             h_ref, a_ref, s_ref, r_ref, ig_ref, xc_ref, b_s, carry):
        i = pl.program_id(0)
        j = (n - 1 - i) if reverse else i

        @pl.when(i == 0)
        def _():
            carry[...] = jnp.zeros_like(carry)

        ext = _ext_rows(prev_ref, cur_ref[...], next_ref, j, n)
        xc = _conv_fwd(ext, cw_ref[...], cb_ref[...], TS)
        r, ig, _, a, s, _ = _gates(xc, wa_ref, wx_ref, ba_ref[...], bx_ref[...], lam_ref[...])
        s_ref[...] = s
        r_ref[...] = r.astype(BF16)
        ig_ref[...] = ig.astype(BF16)
        xc_ref[...] = xc.astype(BF16)
        a_ref[...] = a
        b_s[...] = s * ig * xc
        _scan_tile(a_ref, b_s, h_ref, carry, TS, reverse)

    wspec = _full((RNN_HEADS, RNN_HD, RNN_HD))
    cur = _rev_tile(TS, D, n, reverse)
    f32 = jax.ShapeDtypeStruct((T, D), F32)
    b16 = jax.ShapeDtypeStruct((T, D), BF16)
    return _pallas(
        body, name=name, grid=(n,),
        in_specs=[prev_spec, cur, next_spec, _full((4, D)), _full((1, D)),
                  wspec, wspec, _full((1, D)), _full((1, D)), _full((1, D))],
        out_specs=[cur] * 6,
        out_shape=[f32, f32, f32, b16, b16, b16],
        scratch_shapes=[pltpu.VMEM((TS, D), F32), pltpu.VMEM((1, D), F32)],
        compiler_params=_params(("arbitrary",)),
    )(xr, xr, xr, cw, cb, wa, wx, ba, bx, lam)


def _od_out(hf, hb, g1, w_out, x1, tgt, mod, lnp):
    T = x1.shape[0]

    def body(hf_ref, hb_ref, g_ref, w_ref, x_ref, t_ref, mod_ref, ln_ref,
             dh_ref, dg_ref, dx_ref, dwb_ref, vec_ref, dw_ref):
        i = pl.program_id(0)

        @pl.when(i == 0)
        def _():
            dw_ref[...] = jnp.zeros_like(dw_ref)
            vec_ref[...] = jnp.zeros_like(vec_ref)

        hs = hf_ref[...] + hb_ref[...]
        sg, dsg = _silu_and_grad(g_ref[...].astype(F32))
        yr = (hs * sg).astype(BF16)
        w = w_ref[...]
        out = _dot(yr, w)
        gate = mod_ref[2:3, :]
        z = ALPHA * x_ref[...] + gate * out
        lng = ln_ref[0:1, :]
        x2, xhat, rstd = _ln_fwd(z, lng, ln_ref[1:2, :])
        diff = x2 - t_ref[...]
        vec_ref[3:4, 0:LANE] += 0.5 * jnp.sum(diff * diff) * (1.0 / D)
        dx2 = diff * (1.0 / D)
        dz = _ln_bwd(dx2, xhat, rstd, lng)
        vec_ref[0:1, :] += _rowsum(dx2 * xhat)
        vec_ref[1:2, :] += _rowsum(dx2)
        vec_ref[2:3, :] += _rowsum(dz * out)
        dout = (dz * gate).astype(BF16)
        dyr = _dot_nt(dout, w)
        dw_ref[...] += _dot_tn(yr, dout)
        dh_ref[...] = dyr * sg
        dg_ref[...] = (dyr * hs * dsg).astype(BF16)
        dx_ref[...] = ALPHA * dz

        @pl.when(i == T // TMO - 1)
        def _():
            dwb_ref[...] = dw_ref[...].astype(BF16)

    return _pallas(
        body, name="od_out", grid=(T // TMO,),
        in_specs=[_tile(TMO, D), _tile(TMO, D), _tile(TMO, D), _full((D, D)), _tile(TMO, D), _tile(TMO, D),
                  _full((3, D)), _full((2, D))],
        out_specs=[_tile(TMO, D), _tile(TMO, D), _tile(TMO, D), _full((D, D)), _full((SUBLANE, D))],
        out_shape=[jax.ShapeDtypeStruct((T, D), F32), jax.ShapeDtypeStruct((T, D), BF16),
                   jax.ShapeDtypeStruct((T, D), F32), jax.ShapeDtypeStruct((D, D), BF16),
                   jax.ShapeDtypeStruct((SUBLANE, D), F32)],
        scratch_shapes=[pltpu.VMEM((D, D), F32)],
        compiler_params=_params(("arbitrary",)),
    )(hf, hb, g1, w_out, x1, tgt, mod, lnp)


def _rglru_bwd(fwd, dh, wa, wx, lam, reverse, name, comm=None):
    h, a_all, s_all, r_all, ig_all, xc_all = fwd
    T = h.shape[0]
    n = T // TS
    adj_rev = not reverse
    hprev_spec, hnext_spec = _halo_specs(TS, D, n, T, adj_rev)
    h_halo_spec = hnext_spec if reverse else hprev_spec

    def body(dh_ref, h_ref, hh_ref, a_ref, s_ref, r_ref, ig_ref, xc_ref, wa_ref, wx_ref, lam_ref,
             dxc_ref, dwa_ref, dwx_ref, vec_ref, a_s, l_s, carry, a_edge):
        i = pl.program_id(0)
        j = (n - 1 - i) if adj_rev else i

        @pl.when(i == 0)
        def _():
            carry[...] = jnp.zeros_like(carry)
            a_edge[...] = jnp.zeros_like(a_edge)
            dwa_ref[...] = jnp.zeros_like(dwa_ref)
            dwx_ref[...] = jnp.zeros_like(dwx_ref)
            vec_ref[...] = jnp.zeros_like(vec_ref)

        lam = lam_ref[...]
        sp = jnp.maximum(-lam, 0.0) + jnp.log(1.0 + jnp.exp(-jnp.abs(lam)))
        a, s = a_ref[...], s_ref[...]
        inv_s = lax.rsqrt(jnp.maximum(s * s, 1e-30))
        r, ig = r_ref[...].astype(F32), ig_ref[...].astype(F32)
        xcb = xc_ref[...]
        xc = xcb.astype(F32)

        rows = lax.broadcasted_iota(jnp.int32, (TS, D), 0)
        hcur = h_ref[...]
        if reverse:
            a_sh = jnp.where(rows == 0, a_edge[...], pltpu.roll(a, 1, 0))
            halo = jnp.where(j < n - 1, hh_ref[0:1, :], 0.0)
            h_nb = jnp.where(rows == TS - 1, halo, pltpu.roll(hcur, TS - 1, 0))
        else:
            a_sh = jnp.where(rows == TS - 1, a_edge[...], pltpu.roll(a, TS - 1, 0))
            halo = jnp.where(j > 0, hh_ref[SUBLANE - 1:SUBLANE, :], 0.0)
            h_nb = jnp.where(rows == 0, halo, pltpu.roll(hcur, 1, 0))
        a_s[...] = a_sh
        _scan_tile(a_s, dh_ref, l_s, carry, TS, adj_rev)
        a_edge[...] = a[TS - 1:TS, :] if reverse else a[0:1, :]

        lm = l_s[...]
        da = lm * h_nb
        di = lm * s * xc
        dxc = lm * s * ig
        ds = lm * ig * xc
        dlog_a = a * (da - ds * a * inv_s)
        dr = (-RG_C) * sp * dlog_a
        dsp = _rowsum((-RG_C) * r * dlog_a)
        dpr = dr * r * (1.0 - r)
        dpi = di * ig * (1.0 - ig)
        vec_ref[0:1, :] += _rowsum(dpr)
        vec_ref[1:2, :] += _rowsum(dpi)
        vec_ref[2:3, :] += dsp * (-_sigmoid(-lam))
        parts = []
        for hd in range(RNN_HEADS):
            sl = slice(hd * RNN_HD, (hd + 1) * RNN_HD)
            xh = xcb[:, sl]
            dprh = dpr[:, sl].astype(BF16)
            dpih = dpi[:, sl].astype(BF16)
            parts.append(_dot_nt(dprh, wa_ref[hd]) + _dot_nt(dpih, wx_ref[hd]))
            dwa_ref[hd] += _dot_tn(xh, dprh)
            dwx_ref[hd] += _dot_tn(xh, dpih)
        dxc_ref[...] = dxc + jnp.concatenate(parts, axis=-1)

    wspec = _full((RNN_HEADS, RNN_HD, RNN_HD))
    cur = _rev_tile(TS, D, n, adj_rev)
    return _fused_call(
        body, comm, (dh, h, h, a_all, s_all, r_all, ig_all, xc_all, wa, wx, lam), name=name, grid=(n,),
        in_specs=[cur, cur, h_halo_spec, cur, cur, cur, cur, cur, wspec, wspec, _full((1, D))],
        out_specs=[cur, wspec, wspec, _full((SUBLANE, D))],
        out_shape=[jax.ShapeDtypeStruct((T, D), F32),
                   jax.ShapeDtypeStruct((RNN_HEADS, RNN_HD, RNN_HD), F32),
                   jax.ShapeDtypeStruct((RNN_HEADS, RNN_HD, RNN_HD), F32),
                   jax.ShapeDtypeStruct((SUBLANE, D), F32)],
        scratch_shapes=[pltpu.VMEM((TS, D), F32)] * 2 + [pltpu.VMEM((1, D), F32)] * 2)


def _od_in_bwd(dxcf, dxcb, xr, dg1, x1, dx1p, mod, w_in, cw, comm=None):
    T = x1.shape[0]
    n = T // TMO
    slab = OD_IN // N_DEV
    prev_spec, next_spec = _halo_specs(TMO, D, n, T, False)

    def body(fp_ref, fc_ref, fn_ref, bp_ref, bc_ref, bn_ref, xr_ref, dg_ref, x1_ref, dxp_ref,
             mod_ref, w_ref, cw_ref, dx_ref, dwb_ref, vec_ref, dw_ref):
        i = pl.program_id(0)

        @pl.when(i == 0)
        def _():
            dw_ref[...] = jnp.zeros_like(dw_ref)
            vec_ref[...] = jnp.zeros_like(vec_ref)

        dcur = fc_ref[...] + bc_ref[...]
        dprev = jnp.where(i > 0, fp_ref[...] + bp_ref[...], 0.0)
        dnext = jnp.where(i < n - 1, fn_ref[...] + bn_ref[...], 0.0)
        dext = jnp.concatenate([dprev, dcur, dnext], axis=0)
        xr_v = xr_ref[...]
        cw_v = cw_ref[...]
        dxr = None
        for k in range(4):
            shifted = _shift_rows(dext, 2 - k, TMO)
            term = cw_v[k:k + 1, :] * shifted
            dxr = term if dxr is None else dxr + term
            vec_ref[k:k + 1, :] += _rowsum(shifted * xr_v)
        vec_ref[4:5, :] += _rowsum(dcur)
        dp = jnp.concatenate([dxr.astype(BF16), dg_ref[...]], axis=-1)
        x1v = x1_ref[...]
        scale1 = 1.0 + mod_ref[1:2, :]
        h1 = (x1v * scale1 + mod_ref[0:1, :]).astype(BF16)
        dh1 = _dot_nt(dp, w_ref[...])
        dw_ref[...] += _dot_tn(h1, dp)
        dx_ref[...] = dxp_ref[...] + dh1 * scale1
        vec_ref[5:6, :] += _rowsum(dh1)
        vec_ref[6:7, :] += _rowsum(dh1 * x1v)

        @pl.when(i == n - 1)
        def _():
            for j in range(N_DEV):
                dwb_ref[j] = dw_ref[:, j * slab:(j + 1) * slab].astype(BF16)

    t = _tile(TMO, D)
    return _fused_call(
        body, comm, (dxcf, dxcf, dxcf, dxcb, dxcb, dxcb, xr, dg1, x1, dx1p, mod, w_in, cw),
        name="od_in_bwd", grid=(n,),
        in_specs=[prev_spec, t, next_spec, prev_spec, t, next_spec, t, t, t, t,
                  _full((3, D)), _full((D, OD_IN)), _full((4, D))],
        out_specs=[t, _full((N_DEV, D, slab)), _full((SUBLANE, D))],
        out_shape=[jax.ShapeDtypeStruct((T, D), F32), jax.ShapeDtypeStruct((N_DEV, D, slab), BF16),
                   jax.ShapeDtypeStruct((SUBLANE, D), F32)],
        scratch_shapes=[pltpu.VMEM((D, OD_IN), F32)])


def _ev_out_bwd(dx1, z0, out0, y0, ycat, g0, w_out, mod, lnp):
    T = dx1.shape[0]

    def body(dx_ref, z_ref, out_ref, y0_ref, yc_ref, g_ref, w_ref, mod_ref, ln_ref,
             dxp_ref, dyc_ref, dg_ref, dwb_ref, vec_ref, dw_ref):
        i = pl.program_id(0)

        @pl.when(i == 0)
        def _():
            dw_ref[...] = jnp.zeros_like(dw_ref)
            vec_ref[...] = jnp.zeros_like(vec_ref)

        lng = ln_ref[0:1, :]
        _, xhat, rstd = _ln_fwd(z_ref[...], lng, ln_ref[1:2, :])
        dy = dx_ref[...]
        dz = _ln_bwd(dy, xhat, rstd, lng)
        vec_ref[0:1, :] += _rowsum(dy * xhat)
        vec_ref[1:2, :] += _rowsum(dy)
        vec_ref[2:3, :] += _rowsum(dz * out_ref[...].astype(F32))
        dout = (dz * mod_ref[2:3, :]).astype(BF16)
        dy0 = _dot_nt(dout, w_ref[...])
        dw_ref[...] += _dot_tn(y0_ref[...], dout)
        sg, dsg = _silu_and_grad(g_ref[...].astype(F32))
        dyc_ref[...] = (dy0 * sg).astype(BF16)
        dg_ref[...] = (dy0 * yc_ref[...].astype(F32) * dsg).astype(BF16)
        dxp_ref[...] = ALPHA * dz

        @pl.when(i == T // TMO - 1)
        def _():
            dwb_ref[...] = dw_ref[...].astype(BF16)

    t = _tile(TMO, D)
    return _pallas(
        body, name="ev_out_bwd", grid=(T // TMO,),
        in_specs=[t, t, t, t, t, t, _full((D, D)), _full((3, D)), _full((2, D))],
        out_specs=[t, t, t, _full((D, D)), _full((SUBLANE, D))],
        out_shape=[jax.ShapeDtypeStruct((T, D), F32), jax.ShapeDtypeStruct((T, D), BF16),
                   jax.ShapeDtypeStruct((T, D), BF16), jax.ShapeDtypeStruct((D, D), BF16),
                   jax.ShapeDtypeStruct((SUBLANE, D), F32)],
        scratch_shapes=[pltpu.VMEM((D, D), F32)],
        compiler_params=_params(("arbitrary",)),
    )(dx1, z0, out0, y0, ycat, g0, w_out, mod, lnp)


def _mix0_bwd(q, kvx, lse, dyc, ycat, su, sv, sink_l, bias, a128, gsum, sel, sg_lng, sg_lnb, sg_w, sg_bfull,
              rc, rs1, rs2, comm=None):
    T = q.shape[0]
    nb = T // BLK

    def body(q_ref, kp_ref, kc_ref, kn_ref, lse_ref, dyc_ref, yc_ref, su_ref, sv_ref, sink_ref, bias_ref, a_ref,
             gsum_ref, sel_ref, lng_ref, lnb_ref, w_ref, bfull_ref, c_ref, s1_ref, s2_ref,
             dq_ref, dkv_ref, dsu_ref, dsv_ref, dw_ref, dbt_ref, vec_ref, dsink_ref):
        n = pl.program_id(0)

        @pl.when(n == 0)
        def _():
            dkv_ref[...] = jnp.zeros_like(dkv_ref)
            dw_ref[...] = jnp.zeros_like(dw_ref)
            dbt_ref[...] = jnp.zeros_like(dbt_ref)
            vec_ref[...] = jnp.zeros_like(vec_ref)
            dsink_ref[...] = jnp.zeros_like(dsink_ref)

        kvx4 = jnp.concatenate([kp_ref[...], kc_ref[...], kn_ref[...]], axis=0)
        for s in range(2):
            _mix0_bwd_block(s, 2 * n + s, nb, kvx4[s * BLK:s * BLK + 3 * BLK], q_ref, lse_ref, dyc_ref, yc_ref, su_ref,
                            sv_ref, sink_ref, bias_ref, a_ref, gsum_ref, sel_ref, lng_ref, lnb_ref, w_ref, bfull_ref,
                            c_ref, s1_ref, s2_ref, dq_ref, dkv_ref, dsu_ref, dsv_ref, dw_ref, dbt_ref, vec_ref,
                            dsink_ref)

    def _mix0_bwd_block(s, b, nb, kvx, q_ref, lse_ref, dyc_ref, yc_ref, su_ref, sv_ref, sink_ref, bias_ref, a_ref,
                        gsum_ref, sel_ref, lng_ref, lnb_ref, w_ref, bfull_ref, c_ref, s1_ref, s2_ref,
                        dq_ref, dkv_ref, dsu_ref, dsv_ref, dw_ref, dbt_ref, vec_ref, dsink_ref):
        rows = slice(s * BLK, (s + 1) * BLK)

        def tile(ref, t):
            return ref[rows, t * LANE:(t + 1) * LANE]

        band = pl.ds(pl.multiple_of(b * BLK + (TM - BLK), BLK), 3 * BLK)
        bias = _band_bias(bias_ref, b, nb)
        bias2 = jnp.concatenate([bias, bias], axis=1)
        low = lax.broadcasted_iota(jnp.int32, (BLK, LANE), 1) < HEAD_DIM
        low2 = lax.broadcasted_iota(jnp.int32, (2 * BLK, LANE), 1) < HEAD_DIM
        sel = sel_ref[...]
        c, s1, s2 = c_ref[rows, :], s1_ref[rows, :], s2_ref[rows, :]
        for kvh in range(2):
            t0, t1 = 2 * kvh, 2 * kvh + 1
            q2 = jnp.concatenate([tile(q_ref, t0), tile(q_ref, t1)], axis=0)
            do2 = jnp.concatenate([tile(dyc_ref, t0), tile(dyc_ref, t1)], axis=0)
            yc2 = jnp.concatenate([tile(yc_ref, t0), tile(yc_ref, t1)], axis=0)
            p_hi, p_lo = _split_bf16(do2.astype(F32) * yc2.astype(F32))
            deltas = _dot_nt(sel, p_hi) + _dot_nt(sel, p_lo)
            dkx = jnp.zeros((3 * BLK, LANE), F32)
            dvx = jnp.zeros((3 * BLK, LANE), F32)
            dq_acc = None
            for par in range(2):
                heads = (4 * kvh + par, 4 * kvh + 2 + par)
                kt = 2 * kvh + par
                ke = kvx[:, kt * LANE:(kt + 1) * LANE]
                ve = kvx[:, (4 + kt) * LANE:(5 + kt) * LANE]
                lse = jnp.concatenate([lse_ref[s, :, h * LANE:(h + 1) * LANE] for h in heads], axis=1)
                sk = jnp.concatenate([_lane_tile(sink_ref, h) for h in heads], axis=1)
                delta = deltas[par:par + 1, :]
                pt = jnp.exp(_dot_nt(ke, q2) + bias2 - lse)
                dst = (pt * (_dot_nt(ve, do2) - delta)).astype(BF16)
                sink_terms = jnp.exp(sk - lse) * delta
                for k, h in enumerate(heads):
                    dsink_ref[:, h * LANE:(h + 1) * LANE] += sink_terms[:, k * LANE:(k + 1) * LANE]
                part = _dot_tn(dst, ke)
                dq_acc = part if dq_acc is None else dq_acc + part
                mine = low2 if par == 0 else jnp.logical_not(low2)
                dkx = dkx + jnp.dot(dst, jnp.where(mine, q2, jnp.zeros_like(q2)), preferred_element_type=F32)
                dvx = dvx + jnp.dot(pt.astype(BF16), jnp.where(mine, do2, jnp.zeros_like(do2)),
                                    preferred_element_type=F32)
            for k, t in enumerate((t0, t1)):
                dq_t = dq_acc[k * BLK:(k + 1) * BLK] * (HEAD_DIM ** -0.5)
                dq_ref[rows, t * LANE:(t + 1) * LANE] = _rope_bwd(dq_t, c, s1, s2).astype(BF16)
            dkv_ref[band, kvh * LANE:(kvh + 1) * LANE] += dkx
            dkv_ref[band, (2 + kvh) * LANE:(3 + kvh) * LANE] += dvx

        lng = lng_ref[...]
        xhat, rstd, vb, svm = _sg_core(sv_ref.at[rows, :], lng, lnb_ref[...], a_ref, w_ref, bfull_ref)
        dy = dyc_ref[rows, ATTN_W:].astype(F32)
        dsu_ref[rows, :] = (dy * svm).astype(BF16)
        dsvm = dy * su_ref[rows, :].astype(F32)
        d_hi, d_lo = _split_bf16(dsvm)
        gsum = gsum_ref[...]
        dbt_ref[...] += jnp.dot(d_hi, gsum, preferred_element_type=F32) + jnp.dot(d_lo, gsum,
                                                                                 preferred_element_type=F32)
        tiles = []
        for t in range(SG_W // LANE):
            tl = slice(t * LANE, (t + 1) * LANE)
            dt, v2 = d_hi[:, tl], vb[:, tl]
            dw_ref[2 * t] += _dot_nt(jnp.where(low, dt, jnp.zeros_like(dt)), v2)
            dw_ref[2 * t + 1] += _dot_nt(jnp.where(low, jnp.zeros_like(dt), dt), v2)
            tiles.append(jnp.where(low, _dot_tn(w_ref[2 * t], dt), _dot_tn(w_ref[2 * t + 1], dt)))
        dvgn = jnp.concatenate(tiles, axis=-1)
        vec_ref[0:1, :] += _rowsum(dvgn * xhat)
        vec_ref[1:2, :] += _rowsum(dvgn)
        dxh = dvgn * lng
        m1 = _group_mean(dxh, a_ref)
        m2 = _group_mean(dxh * xhat, a_ref)
        dsv_ref[rows, :] = (rstd * (dxh - m1 - xhat * m2)).astype(BF16)

    two = 2 * BLK
    return _fused_call(
        body, comm, (q, kvx, kvx, kvx, lse, dyc, ycat, su, sv, sink_l, bias, a128, gsum, sel, sg_lng, sg_lnb, sg_w,
                     sg_bfull, rc, rs1, rs2),
        name="mix0_bwd", grid=(nb // 2,),
        in_specs=[_tile(two, ATTN_W)] + _band_specs(KVX_W, nb, 2) + [
            pl.BlockSpec((2, 1, N_HEADS * LANE), lambda n: (n, 0, 0)), _tile(two, D), _tile(two, D),
            _tile(two, SG_W), _tile(two, SG_W), _full((1, N_HEADS * LANE)), _full((3 * BLK, LANE)),
            _full((2 * LANE, 2 * LANE)),_full((SG_W, LANE)), _full((SUBLANE, LANE)), _full((1, SG_W)), _full((1, SG_W)),
            _full((SG_GROUPS, BLK, BLK)), _full((BLK, SG_W)), _tile(two, LANE), _tile(two, LANE), _tile(two, LANE)],
        out_specs=[_tile(two, ATTN_W), _full((T + 2 * TM, 4 * LANE)), _tile(two, SG_W), _tile(two, SG_W),
                   _full((SG_GROUPS, BLK, BLK)), _full((BLK, LANE)), _full((SUBLANE, SG_W)),
                   _full((1, N_HEADS * LANE))],
        out_shape=[jax.ShapeDtypeStruct((T, ATTN_W), BF16), jax.ShapeDtypeStruct((T + 2 * TM, 4 * LANE), F32),
                   jax.ShapeDtypeStruct((T, SG_W), BF16), jax.ShapeDtypeStruct((T, SG_W), BF16),
                   jax.ShapeDtypeStruct((SG_GROUPS, BLK, BLK), F32), jax.ShapeDtypeStruct((BLK, LANE), F32),
                   jax.ShapeDtypeStruct((SUBLANE, SG_W), F32), jax.ShapeDtypeStruct((1, N_HEADS * LANE), F32)])


def _ev_in_bwd(dq, dkv, dsu, dsv, dg0, x, dxp, mod, w_in, rc, rs1, rs2, comm=None):
    T = x.shape[0]

    def body(dq_ref, dkv_ref, dsu_ref, dsv_ref, dg_ref, x_ref, dxp_ref, mod_ref, w_ref, c_ref, s1_ref, s2_ref,
             dx_ref, dwb_ref, vec_ref, dw_ref):
        i = pl.program_id(0)

        @pl.when(i == 0)
        def _():
            dw_ref[...] = jnp.zeros_like(dw_ref)
            vec_ref[...] = jnp.zeros_like(vec_ref)

        low = lax.broadcasted_iota(jnp.int32, (TM, LANE), 1) < HEAD_DIM

        def fold(j):
            t0 = dkv_ref[:, (2 * j) * LANE:(2 * j + 1) * LANE]
            t1 = dkv_ref[:, (2 * j + 1) * LANE:(2 * j + 2) * LANE]
            return jnp.where(low, t0 + pltpu.roll(t0, HEAD_DIM, 1), t1 + pltpu.roll(t1, HEAD_DIM, 1))

        dk = _rope_bwd(fold(0), c_ref[...], s1_ref[...], s2_ref[...]).astype(BF16)
        dp = jnp.concatenate([dq_ref[...], dk, fold(1).astype(BF16), dsu_ref[...], dsv_ref[...],
                              dg_ref[...]], axis=-1)
        xv = x_ref[...]
        scale0 = 1.0 + mod_ref[1:2, :]
        h0 = (xv * scale0 + mod_ref[0:1, :]).astype(BF16)
        dh0 = _dot(dp, w_ref[...])
        dw_ref[...] += _dot_tn(dp, h0)
        dx_ref[...] = dxp_ref[...] + dh0 * scale0
        vec_ref[0:1, :] += _rowsum(dh0)
        vec_ref[1:2, :] += _rowsum(dh0 * xv)

        @pl.when(i == T // TM - 1)
        def _():
            dwb_ref[...] = dw_ref[...].astype(BF16)

    t = _tile(TM, D)
    return _fused_call(
        body, comm, (dq, dkv, dsu, dsv, dg0, x, dxp, mod, w_in, rc, rs1, rs2), name="ev_in_bwd", grid=(T // TM,),
        in_specs=[_tile(TM, ATTN_W), pl.BlockSpec((TM, 4 * LANE), lambda i: (i + 1, 0)), _tile(TM, SG_W),
                  _tile(TM, SG_W), t, t, t,
                  _full((3, D)), _full((EV_IN, D)), _tile(TM, LANE), _tile(TM, LANE), _tile(TM, LANE)],
        out_specs=[t, _full((EV_IN, D)), _full((SUBLANE, D))],
        out_shape=[jax.ShapeDtypeStruct((T, D), F32), jax.ShapeDtypeStruct((EV_IN, D), BF16),
                   jax.ShapeDtypeStruct((SUBLANE, D), F32)],
        scratch_shapes=[pltpu.VMEM((EV_IN, D), F32)])


def _sum_slots(land_ref):
    g = land_ref[0].astype(F32)
    for i in range(1, land_ref.shape[0]):
        g = g + land_ref[i].astype(F32)
    return g


def _reduce_adam(items, name, after=()):
    R, C = items[0][1].shape
    rb = R
    if R > 512:
        for cand in (512, 256, 128, 64, 32, 16, 8):
            if R % cand == 0:
                rb = cand
                break
    n = len(items)

    def body(*refs):
        for k in range(n):
            l_ref, w_ref, m_ref, v_ref = refs[4 * k:4 * k + 4]
            first_out = 4 * n + len(after)
            g_ref, d_ref, nm_ref, nv_ref = refs[first_out + 4 * k:first_out + 4 * k + 4]
            g = _sum_slots(l_ref)
            g_ref[...] = g
            dlt, m2, v2 = _adam(w_ref[...], g, m_ref[...], v_ref[...])
            d_ref[...] = dlt
            nm_ref[...] = m2
            nv_ref[...] = v2

    t = pl.BlockSpec((rb, C), lambda i: (i, 0))
    shp = jax.ShapeDtypeStruct((R, C), F32)
    in_specs, operands = [], []
    for land, w, m, v in items:
        in_specs += [pl.BlockSpec((land.shape[0], rb, C), lambda i: (0, i, 0)), t, t, t]
        operands += [land, w, m, v]
    res = _pallas(
        body, name=name, grid=(R // rb,),
        in_specs=in_specs + [pl.BlockSpec(memory_space=pl.ANY)] * len(after),
        out_specs=[t] * (4 * n), out_shape=[shp] * (4 * n),
        compiler_params=_params(("parallel",)),
    )(*operands, *after)
    return [list(res[4 * k:4 * k + 4]) for k in range(n)]


def _tail_stage1(slabs, small):
    _, R, C = slabs.shape
    n_chips = N_DEV // 2
    gather = _GatherComm(small)
    ns = gather.n

    def body(*refs):
        slab_ref = refs[0]
        g_ins = refs[1:1 + ns]
        part, land_ref = refs[1 + ns], refs[2 + ns]
        g_outs = refs[3 + ns:3 + 2 * ns]
        stage, s1_send, s1_recv = refs[3 + 2 * ns:6 + 2 * ns]
        g_sems = refs[6 + 2 * ns:]
        x, y, c = _my_pos()
        chip = 2 * x + y
        gather.start(g_ins, g_outs, g_sems)
        swaps = [pltpu.make_async_remote_copy(
            src_ref=slab_ref.at[2 * k + (1 - c)], dst_ref=stage.at[k], send_sem=s1_send.at[k],
            recv_sem=s1_recv.at[k], device_id=(x, y, 1 - c), device_id_type=MESH) for k in range(n_chips)]
        for cp in swaps:
            cp.start()
        for cp in swaps:
            cp.wait()
        for k in range(n_chips):
            part[k] = (slab_ref[2 * k + c].astype(F32) + stage[k].astype(F32)).astype(BF16)
        land_ref[chip] = part[chip]
        gather.mid(g_ins, g_outs, g_sems)
        gather.finish(g_ins, g_outs, g_sems)

    any_spec = pl.BlockSpec(memory_space=pl.ANY)
    vmem_spec = pl.BlockSpec(memory_space=pltpu.VMEM)
    slab4 = jax.ShapeDtypeStruct((n_chips, R, C), BF16)
    res = _pallas(
        body, name="tail_stage1",
        out_shape=[slab4, slab4] + gather.out_shapes(),
        in_specs=[vmem_spec] + [any_spec] * ns, out_specs=[vmem_spec, vmem_spec] + [any_spec] * ns,
        scratch_shapes=[pltpu.VMEM((n_chips, R, C), BF16),
                        pltpu.SemaphoreType.DMA((n_chips,)), pltpu.SemaphoreType.DMA((n_chips,))] + gather.sems(),
        compiler_params=pltpu.CompilerParams(vmem_limit_bytes=VMEM_LIMIT),
    )(slabs, *gather.arrs)
    return res[0], res[1], list(res[2:])


def _chip_copies(part_ref, land_ref, send_sems, recv_sems):
    x, y, c = _my_pos()
    chip = 2 * x + y
    copies = []
    for r in range(1, N_DEV // 2):
        px = (1 - x) if (r & 2) else x
        py = (1 - y) if (r & 1) else y
        copies.append(pltpu.make_async_remote_copy(
            src_ref=part_ref.at[2 * px + py], dst_ref=land_ref.at[chip], send_sem=send_sems[r - 1],
            recv_sem=recv_sems[r - 1], device_id=(px, py, c), device_id_type=MESH))
    return copies


def _tail_send(part, land):
    n = N_DEV // 2 - 1

    def body(part_ref, land_ref, *outs):
        send_sems, recv_sems = outs[:n], outs[n:2 * n]
        token = outs[2 * n + 2]
        for cp in _chip_copies(part_ref, land_ref, send_sems, recv_sems):
            cp.start()
        token[...] = jnp.zeros_like(token)

    hbm = pl.BlockSpec(memory_space=pltpu.HBM)
    sem = pl.BlockSpec(memory_space=pltpu.SEMAPHORE)
    res = _pallas(
        body, name="tail_send",
        out_shape=tuple([pltpu.SemaphoreType.DMA(())] * (2 * n)
                        + [pltpu.HBM(part.shape, part.dtype), pltpu.HBM(land.shape, land.dtype),
                           jax.ShapeDtypeStruct((SUBLANE, LANE), F32)]),
        in_specs=(hbm, hbm), out_specs=tuple([sem] * (2 * n) + [hbm, hbm, pl.BlockSpec(memory_space=pltpu.VMEM)]),
        input_output_aliases={0: 2 * n, 1: 2 * n + 1},
        compiler_params=pltpu.CompilerParams(has_side_effects=pltpu.SideEffectType.DATAFLOW_SIDE_EFFECTING),
    )(pltpu.with_memory_space_constraint(part, pltpu.HBM), pltpu.with_memory_space_constraint(land, pltpu.HBM))
    return list(res[:n]), list(res[n:2 * n]), res[2 * n], res[2 * n + 1], res[2 * n + 2]


def _tail_wait(send_sems, recv_sems, part, land, after):
    n = len(send_sems)

    def body(part_ref, land_ref, *rest):
        ss, rs = rest[:n], rest[n:2 * n]
        for cp in _chip_copies(part_ref, land_ref, ss, rs):
            cp.wait_send()
            cp.wait_recv()

    hbm = pl.BlockSpec(memory_space=pltpu.HBM)
    sem = pl.BlockSpec(memory_space=pltpu.SEMAPHORE)
    any_spec = pl.BlockSpec(memory_space=pl.ANY)
    res = _pallas(
        body, name="tail_wait",
        out_shape=(pltpu.HBM(part.shape, part.dtype), pltpu.HBM(land.shape, land.dtype)),
        in_specs=tuple([hbm, hbm] + [sem] * (2 * n) + [any_spec] * len(after)), out_specs=(hbm, hbm),
        input_output_aliases={0: 0, 1: 1},
        compiler_params=pltpu.CompilerParams(has_side_effects=pltpu.SideEffectType.DATAFLOW_SIDE_EFFECTING),
    )(part, land, *send_sems, *recv_sems, *after)
    return res[1]


def _slots_adam(items, name, after=()):
    zeros3 = (0, 0, 0)
    in_specs, out_specs, out_shape, operands = [], [], [], []
    for land, w, m, v in items:
        inner = w.shape[-3:]
        if w.ndim == 5:
            lspec = pl.BlockSpec((N_DEV, 1) + inner, lambda i: (0, i) + zeros3)
            wspec = pl.BlockSpec((1, 1) + inner, lambda i: (0, i) + zeros3)
        else:
            lspec = pl.BlockSpec((N_DEV,) + inner, lambda i: (0,) + zeros3)
            wspec = pl.BlockSpec((1,) + inner, lambda i: (0,) + zeros3)
        in_specs += [lspec, wspec, wspec, wspec]
        out_specs += [wspec] * 4
        out_shape += [jax.ShapeDtypeStruct(w.shape, F32)] * 4
        operands += [land, w, m, v]
    n = len(items)

    def body(*refs):
        for k, (_, w, _, _) in enumerate(items):
            l_ref, w_ref, m_ref, v_ref = refs[4 * k:4 * k + 4]
            first_out = 4 * n + len(after)
            outs = refs[first_out + 4 * k:first_out + 4 * k + 4]
            at = (0, 0) if w.ndim == 5 else (0,)

            def update(l_ref=l_ref, w_ref=w_ref, m_ref=m_ref, v_ref=v_ref, outs=outs, at=at):
                g = l_ref[(0,) + at[1:]].astype(F32)
                for i in range(1, N_DEV):
                    g = g + l_ref[(i,) + at[1:]].astype(F32)
                dlt, m2, v2 = _adam(w_ref[at], g, m_ref[at], v_ref[at])
                for o_ref, val in zip(outs, (g, dlt, m2, v2)):
                    o_ref[at] = val

            if w.ndim == 5:
                update()
            else:
                pl.when(pl.program_id(0) == 0)(update)

    res = _pallas(
        body, name=name, grid=(2,),
        in_specs=in_specs + [pl.BlockSpec(memory_space=pl.ANY)] * len(after),
        out_specs=out_specs, out_shape=out_shape,
        compiler_params=_params(("arbitrary",)),
    )(*operands, *after)
    return [list(res[4 * k:4 * k + 4]) for k in range(n)]


SMALL_PARAMS = ("ln_g", "ln_b", "ev_sg_ln_g", "ev_sg_ln_b", "ev_sink", "ev_sg_b",
                "od_conv_w", "od_conv_b", "od_b_a", "od_b_x", "od_lam")


def _small_update(ga, gc, gd, gf, gb, ge, gsink, gbt, params):
    names = list(SMALL_PARAMS)
    flat = [a for nm in names for a in params[nm]]
    n_g = 8

    def body(*refs):
        ga_ref, gc_ref, gd_ref, gf_ref, gb_ref, ge_ref, gs_ref, gbt_ref = refs[:n_g]
        prm = refs[n_g:n_g + 3 * len(names)]
        loss_ref = refs[n_g + 3 * len(names)]
        outs = refs[n_g + 3 * len(names) + 1:]

        def ssum(ref):
            acc = ref[0]
            for i in range(1, N_DEV):
                acc = acc + ref[i]
            return acc

        a, cc, dd, ff, bb, ee = ssum(ga_ref), ssum(gc_ref), ssum(gd_ref), ssum(gf_ref), ssum(gb_ref), ssum(ge_ref)
        loss_ref[...] = a[3:4, 0:LANE]
        me = _slot(*_my_pos())

        def mine(rows):
            acc = jnp.zeros((rows.shape[0], LANE), F32)
            for j in range(N_DEV):
                acc = acc + jnp.where(me == j, rows[:, j * LANE:(j + 1) * LANE], 0.0)
            return acc

        sink_terms = ssum(gs_ref)
        lane8 = lax.broadcasted_iota(jnp.int32, (1, N_HEADS), 1)
        g_sink = jnp.zeros((1, N_HEADS), F32)
        for h in range(N_HEADS):
            tot = -jnp.sum(sink_terms[:, h * LANE:(h + 1) * LANE], axis=1, keepdims=True)
            g_sink = jnp.where(lane8 == h, tot, g_sink)
        grads = dict(
            ln_g=jnp.concatenate([dd[0:1], a[0:1]], axis=0), ln_b=jnp.concatenate([dd[1:2], a[1:2]], axis=0),
            ev_sg_ln_g=ee[0:1], ev_sg_ln_b=ee[1:2], ev_sink=g_sink,
            ev_sg_b=jnp.transpose(ssum(gbt_ref))[0:SG_GROUPS, :],
            od_conv_w=mine(cc[0:4]), od_conv_b=mine(cc[4:5]),
            od_b_a=mine(jnp.concatenate([ff[0:1], bb[0:1]], axis=0)),
            od_b_x=mine(jnp.concatenate([ff[1:2], bb[1:2]], axis=0)),
            od_lam=mine(jnp.concatenate([ff[2:3], bb[2:3]], axis=0)))
        for k, nm in enumerate(names):
            w_ref, m_ref, v_ref = prm[3 * k:3 * k + 3]
            at = (0,) if len(w_ref.shape) == 3 else ()
            g = grads[nm]
            dlt, m2, v2 = _adam(w_ref[at] if at else w_ref[...], g, m_ref[at] if at else m_ref[...],
                                v_ref[at] if at else v_ref[...])
            for o_ref, val in zip(outs[4 * k:4 * k + 4], (g, dlt, m2, v2)):
                if at:
                    o_ref[at] = val
                else:
                    o_ref[...] = val

    gathered = [ga, gc, gd, gf, gb, ge, gsink, gbt]
    out_shape = [jax.ShapeDtypeStruct((1, LANE), F32)]
    for nm in names:
        out_shape += [jax.ShapeDtypeStruct(params[nm][0].shape, F32)] * 4
    return _pallas(
        body, name="small_update", grid=(1,),
        in_specs=[_full(a.shape) for a in gathered + flat],
        out_specs=[_full(s.shape) for s in out_shape], out_shape=out_shape,
        compiler_params=_params(("arbitrary",)),
    )(*gathered, *flat)


VEC_ROWS = 16
VEC_LAYOUT = (("od_conv_w", 4), ("od_conv_b", 1), ("od_b_a", 2), ("od_b_x", 2), ("od_lam", 2))


def _from_slabs(slabs):
    n, R, cp = slabs.shape
    return slabs.transpose(1, 0, 2).reshape(R, n * cp)


def kernel(x, c, positions, ada_w, ada_b, ln_g, ln_b, ev_w_in, ev_w_out, ev_sink, ev_sg_ln_g, ev_sg_ln_b, ev_sg_w, ev_sg_b, od_w_in, od_conv_w, od_conv_b, od_w_a, od_b_a, od_w_x, od_b_x, od_lam, od_w_out, loss_target, m_ada_w, m_ada_b, m_ln_g, m_ln_b, m_ev_w_in, m_ev_w_out, m_ev_sink, m_ev_sg_ln_g, m_ev_sg_ln_b, m_ev_sg_w, m_ev_sg_b, m_od_w_in, m_od_conv_w, m_od_conv_b, m_od_w_a, m_od_b_a, m_od_w_x, m_od_b_x, m_od_lam, m_od_w_out, v_ada_w, v_ada_b, v_ln_g, v_ln_b, v_ev_w_in, v_ev_w_out, v_ev_sink, v_ev_sg_ln_g, v_ev_sg_ln_b, v_ev_sg_w, v_ev_sg_b, v_od_w_in, v_od_conv_w, v_od_conv_b, v_od_w_a, v_od_b_a, v_od_w_x, v_od_b_x, v_od_lam, v_od_w_out):
    T = x.shape[1]
    me = _slot(*_my_pos())
    xs = x.reshape(T, D)
    tgt = loss_target.reshape(T, D)

    c_all, mod_all, g_vec, (g_ev_in,), (s_ev_out, s_od_in, s_od_out, sg_w, wa, wx) = _head_gather(
        c, ada_w, [ev_w_in[0].T.astype(BF16)],
        [ev_w_out[0], od_w_in[0], od_w_out[0], ev_sg_w[0], od_w_a[0], od_w_x[0]],
        [od_conv_w, od_conv_b, od_b_a, od_b_x, od_lam])
    c_all = c_all.reshape(N_DEV, D)
    w_ev_in = g_ev_in.reshape(EV_IN, D)
    vec_full = _from_slabs(g_vec)
    cw, cb = vec_full[0:4], vec_full[4:5]
    ba, bx, lam = vec_full[5:7], vec_full[7:9], vec_full[9:11]
    mod_mine = lax.dynamic_index_in_dim(mod_all, me, axis=2, keepdims=False)
    mod = mod_mine.transpose(1, 0, 2).reshape(2, 3 * D) + ada_b
    mod0 = mod[0].reshape(3, D)
    mod1 = mod[1].reshape(3, D)

    half = 8
    inv_freq = jnp.power(jnp.float32(ROPE_THETA), -jnp.arange(half, dtype=F32) / half)
    ang = positions.reshape(T).astype(F32)[:, None] * inv_freq
    cos_t = jnp.tile(jnp.cos(ang), (1, LANE // half))
    sin_t = jnp.tile(jnp.sin(ang), (1, LANE // half))
    l64 = jnp.arange(LANE) % HEAD_DIM
    rc = jnp.where(l64 < 2 * half, cos_t, 1.0)
    rs1 = jnp.where(l64 < half, -sin_t, 0.0)
    rs2 = jnp.where((l64 >= half) & (l64 < 2 * half), sin_t, 0.0)

    ln0 = jnp.stack([ln_g[0], ln_b[0]])
    ln1 = jnp.stack([ln_g[1], ln_b[1]])
    sg_lng = ev_sg_ln_g
    sg_lnb = ev_sg_ln_b
    sg_bfull = jnp.repeat(ev_sg_b[0].T, SG_DIM, axis=1)
    sink_l = jnp.repeat(ev_sink, LANE, axis=1)
    kj = jnp.arange(3 * BLK)[:, None]
    qi = jnp.arange(BLK)[None, :]
    band_bias = jnp.where(jnp.abs(kj - BLK - qi) <= BLK, 0.0, NEG_INF).astype(F32)
    lanes = jnp.arange(LANE)
    lanes2 = jnp.arange(2 * LANE)
    a128 = jnp.where(lanes2[:, None] // SG_DIM == lanes2[None, :] // SG_DIM, 1.0 / SG_DIM, 0.0).astype(BF16)
    gsum = (jnp.arange(SG_W)[:, None] // SG_DIM == lanes[None, :]).astype(BF16)
    sel = (jnp.arange(SUBLANE)[:, None] == lanes[None, :] // HEAD_DIM).astype(BF16)

    (q, kvx, su, sv, g0), _ = _ev_in(xs, mod0, w_ev_in, rc, rs1, rs2)
    (ycat, y0, lse), (g_ev_out, g_od_in, g_od_out) = _mix0_fwd(
        q, kvx, su, sv, g0, sink_l, band_bias, a128, sg_lng, sg_lnb, sg_w, sg_bfull,
        _GatherComm([s_ev_out, s_od_in, s_od_out], mid_frac=0.75))
    w_ev_out = g_ev_out.reshape(D, D)
    w_od_in = _from_slabs(g_od_in)
    w_od_out = g_od_out.reshape(D, D)
    out0, z0, x1 = _ev_out(y0, w_ev_out, xs, mod0, ln0)
    xr, g1 = _od_in(x1, mod1, w_od_in)
    fwd_f = _rglru_fwd(xr, cw, cb, wa[0], wx[0], ba[0:1], bx[0:1], lam[0:1], False, "rglru_fwd_f")
    fwd_b = _rglru_fwd(xr, cw, cb, wa[1], wx[1], ba[1:2], bx[1:2], lam[1:2], True, "rglru_fwd_b")
    dh, dg1, dx1p, d_od_out, vec_a = _od_out(fwd_f[0], fwd_b[0], g1, w_od_out, x1, tgt, mod1, ln1)

    (dxcf, dwa_f, dwx_f, vec_f), (l_od_out,) = _rglru_bwd(
        fwd_f, dh, wa[0], wx[0], lam[0:1], False, "rglru_bwd_f",
        _ExchangeComm([d_od_out.reshape(N_DEV, D // N_DEV, D)]))
    (dxcb, dwa_b, dwx_b, vec_b), _ = _rglru_bwd(fwd_b, dh, wa[1], wx[1], lam[1:2], True, "rglru_bwd_b")
    (dx1, d_od_in, vec_c), (a_wa, a_wx) = _od_in_bwd(
        dxcf, dxcb, xr, dg1, x1, dx1p, mod1, w_od_in, cw,
        _GatherComm([jnp.stack([dwa_f, dwa_b]).astype(BF16), jnp.stack([dwx_f, dwx_b]).astype(BF16)],
                    mid_frac=0.75))
    dxp, dyc, dg0, d_ev_out, vec_d = _ev_out_bwd(dx1, z0, out0, y0, ycat, g0, w_ev_out, mod0, ln0)
    (dq, dkv, dsu, dsv, d_sg_w, d_sg_bt, vec_e, d_sink_l), (l_od_in, l_ev_out, ga, gc, gd, gf, gb) = _mix0_bwd(
        q, kvx, lse, dyc, ycat, su, sv, sink_l, band_bias, a128, gsum, sel, sg_lng, sg_lnb, sg_w, sg_bfull,
        rc, rs1, rs2, _BothComm(_ExchangeComm([d_od_in, d_ev_out.reshape(N_DEV, D // N_DEV, D)]),
                                _GatherComm([vec_a, vec_c, vec_d, vec_f, vec_b], mid_frac=0.9)))
    (grad_x, d_ev_in, vec_g), _ = _ev_in_bwd(dq, dkv, dsu, dsv, dg0, xs, dxp, mod0, w_ev_in, rc, rs1, rs2)

    part, land, (gg, ge, gsink, gbt, a_sgw) = _tail_stage1(
        d_ev_in.reshape(N_DEV, EV_IN // N_DEV, D), [vec_g, vec_e, d_sink_l, d_sg_bt, d_sg_w.astype(BF16)])
    send_sems, recv_sems, part, land, token = _tail_send(part, land)

    dmod_all = jnp.stack([jnp.concatenate([gg[:, 0], gg[:, 1], gd[:, 2]], axis=-1),
                          jnp.concatenate([gc[:, 5], gc[:, 6], ga[:, 2]], axis=-1)], axis=1)
    cols = ada_w.shape[2]
    dmod_cols = lax.dynamic_slice_in_dim(dmod_all, me * cols, cols, axis=2).transpose(1, 0, 2)
    (g_ada_w, d_ada_w, nm_ada_w, nv_ada_w, g_ada_b, d_ada_b, nm_ada_b, nv_ada_b) = _ada_update(
        c_all, dmod_cols, dmod_all, ada_w, m_ada_w, v_ada_w, ada_b, m_ada_b, v_ada_b)

    res = dict(ada_w=[g_ada_w, d_ada_w, nm_ada_w, nv_ada_w], ada_b=[g_ada_b, d_ada_b, nm_ada_b, nv_ada_b])
    (r_od_in,) = _reduce_adam([(l_od_in, od_w_in[0], m_od_w_in[0], v_od_w_in[0])], "adam_od_w_in", after=[token])
    r_ev_out, r_od_out = _reduce_adam([(l_ev_out, ev_w_out[0], m_ev_w_out[0], v_ev_w_out[0]),
                                       (l_od_out, od_w_out[0], m_od_w_out[0], v_od_w_out[0])], "adam_w_out",
                                      after=[token])
    for name, r in (("od_w_in", r_od_in), ("ev_w_out", r_ev_out), ("od_w_out", r_od_out)):
        res[name] = [a[None] for a in r]
    res["od_w_a"], res["od_w_x"], res["ev_sg_w"] = _slots_adam(
        [(a_wa, od_w_a, m_od_w_a, v_od_w_a), (a_wx, od_w_x, m_od_w_x, v_od_w_x),
         (a_sgw, ev_sg_w, m_ev_sg_w, v_ev_sg_w)], "adam_gates", after=[token])
    small = dict(ln_g=(ln_g, m_ln_g, v_ln_g), ln_b=(ln_b, m_ln_b, v_ln_b),
                 ev_sg_ln_g=(ev_sg_ln_g, m_ev_sg_ln_g, v_ev_sg_ln_g),
                 ev_sg_ln_b=(ev_sg_ln_b, m_ev_sg_ln_b, v_ev_sg_ln_b),
                 ev_sink=(ev_sink, m_ev_sink, v_ev_sink), ev_sg_b=(ev_sg_b, m_ev_sg_b, v_ev_sg_b),
                 od_conv_w=(od_conv_w, m_od_conv_w, v_od_conv_w), od_conv_b=(od_conv_b, m_od_conv_b, v_od_conv_b),
                 od_b_a=(od_b_a, m_od_b_a, v_od_b_a), od_b_x=(od_b_x, m_od_b_x, v_od_b_x),
                 od_lam=(od_lam, m_od_lam, v_od_lam))
    small_out = _small_update(ga, gc, gd, gf, gb, ge, gsink, gbt, small)
    l_ev_in = _tail_wait(send_sems, recv_sems, part, land,
                         [r_od_in[0], r_od_out[0], res["od_w_x"][0], g_ada_w, small_out[0]])
    (r_ev_in,) = _reduce_adam([(l_ev_in, ev_w_in[0].T, m_ev_w_in[0].T, v_ev_w_in[0].T)], "adam_ev_w_in")
    res["ev_w_in"] = [a.T[None] for a in r_ev_in]
    loss = small_out[0][0, 0]
    for k, name in enumerate(SMALL_PARAMS):
        res[name] = small_out[1 + 4 * k:5 + 4 * k]

    order = ["ada_w", "ada_b", "ln_g", "ln_b", "ev_w_in", "ev_w_out", "ev_sink", "ev_sg_ln_g", "ev_sg_ln_b",
             "ev_sg_w", "ev_sg_b", "od_w_in", "od_conv_w", "od_conv_b", "od_w_a", "od_b_a", "od_w_x", "od_b_x",
             "od_lam", "od_w_out"]
    outs = [loss, grad_x.reshape(1, T, D)]
    for kind in range(4):
        outs += [res[name][kind] for name in order]
    return tuple(outs)
```

```python
import jax
import jax.numpy as jnp
from jax import lax
from jax.experimental import pallas as pl
from jax.experimental.pallas import tpu as pltpu

F32 = jnp.float32
BF16 = jnp.bfloat16

N_DEV = 8
D = 1024
N_HEADS = 8
HEAD_DIM = 64
ATTN_W = 512
SG_W = 512
SG_GROUPS = 8
SG_DIM = 64
BLK = 128
KVX_W = 1024
EV_IN = 2816
OD_IN = 2048
RNN_HEADS = 8
RNN_HD = 128
ALPHA = 4.0 ** 0.25
LN_EPS = 1e-5
NEG_INF = -1e30
RG_C = 8.0
ROPE_THETA = 500000.0
LR, B1, B2, EPS, WD, STEP = 0.001, 0.9, 0.999, 1e-08, 0.01, 10

LANE = 128
SUBLANE = 8
TM = 256
TMF = 512
TMO = 512
TS = 256
FWD_BLOCKS = 4
VMEM_LIMIT = 56 * 1024 * 1024

MESH = pl.DeviceIdType.MESH


def _pallas(body, **kw):
    return pl.pallas_call(body, **kw)


def _params(sem, vmem=VMEM_LIMIT):
    return pltpu.CompilerParams(dimension_semantics=sem, vmem_limit_bytes=vmem)


def _sigmoid(x):
    return 0.5 * jnp.tanh(0.5 * x) + 0.5


def _silu_and_grad(x):
    s = _sigmoid(x)
    return x * s, s * (1.0 + x * (1.0 - s))


def _dot(a, b):
    return jnp.dot(a.astype(BF16), b.astype(BF16), preferred_element_type=F32)


def _dot_nt(a, b):
    return lax.dot_general(a.astype(BF16), b.astype(BF16), (((1,), (1,)), ((), ())), preferred_element_type=F32)


def _dot_tn(a, b):
    return lax.dot_general(a.astype(BF16), b.astype(BF16), (((0,), (0,)), ((), ())), preferred_element_type=F32)


def _ln_fwd(z, g, b):
    mu = jnp.mean(z, axis=-1, keepdims=True)
    zc = z - mu
    var = jnp.mean(zc * zc, axis=-1, keepdims=True)
    rstd = lax.rsqrt(var + LN_EPS)
    xhat = zc * rstd
    return xhat * g + b, xhat, rstd


def _ln_bwd(dy, xhat, rstd, g):
    dxh = dy * g
    m1 = jnp.mean(dxh, axis=-1, keepdims=True)
    m2 = jnp.mean(dxh * xhat, axis=-1, keepdims=True)
    return rstd * (dxh - m1 - xhat * m2)


def _rowsum(v):
    return jnp.sum(v, axis=0, keepdims=True)


def _rope_fwd(t, c, s1, s2):
    return t * c + pltpu.roll(t, LANE - 8, 1) * s1 + pltpu.roll(t, 8, 1) * s2


def _rope_bwd(d, c, s1, s2):
    return d * c + pltpu.roll(d * s1, 8, 1) + pltpu.roll(d * s2, LANE - 8, 1)


def _adam(w, g, m, v):
    m2 = B1 * m + (1.0 - B1) * g
    v2 = B2 * v + (1.0 - B2) * (g * g)
    m_hat = m2 / (1.0 - B1 ** STEP)
    v_hat = v2 / (1.0 - B2 ** STEP)
    delta = -LR * (m_hat / (jnp.sqrt(v_hat) + EPS) + WD * w)
    return delta, m2, v2


def _tile(rows, width):
    return pl.BlockSpec((rows, width), lambda i: (i, 0))


def _full(shape):
    zeros = (0,) * len(shape)
    return pl.BlockSpec(shape, lambda i: zeros)


def _rev_tile(rows, width, n, reverse):
    if reverse:
        return pl.BlockSpec((rows, width), lambda i: (n - 1 - i, 0))
    return pl.BlockSpec((rows, width), lambda i: (i, 0))


def _halo_specs(rows, width, n, total_rows, reverse):
    per = rows // SUBLANE
    last = total_rows // SUBLANE - 1

    def tile_of(i):
        return (n - 1 - i) if reverse else i

    prev = pl.BlockSpec((SUBLANE, width), lambda i: (jnp.maximum(tile_of(i) * per - 1, 0), 0))
    nxt = pl.BlockSpec((SUBLANE, width), lambda i: (jnp.minimum((tile_of(i) + 1) * per, last), 0))
    return prev, nxt


def _my_pos():
    return lax.axis_index("x"), lax.axis_index("y"), lax.axis_index("c")


def _slot(px, py, pc):
    return 4 * px + 2 * py + pc


class _GatherComm:
    has_mid = True

    def __init__(self, arrs, mid_frac=0.5):
        self.arrs = list(arrs)
        self.n = len(self.arrs)
        self.mid_frac = mid_frac

    def out_shapes(self):
        return [jax.ShapeDtypeStruct((N_DEV,) + a.shape, a.dtype) for a in self.arrs]

    def sems(self):
        return [pltpu.SemaphoreType.DMA((7 * self.n,)), pltpu.SemaphoreType.DMA((7 * self.n,)),
                pltpu.SemaphoreType.DMA((self.n,))]

    def _parts(self, ins, outs, sems):
        send_sems, recv_sems, local_sems = sems
        x, y, c = _my_pos()
        me, sibling = (x, y, c), (x, y, 1 - c)
        chips = [(1 - x, y), (x, 1 - y), (1 - x, 1 - y)]

        def copy(a, k, block, to, src=None):
            dst = outs[a].at[_slot(*block)]
            return pltpu.make_async_remote_copy(
                src_ref=dst if src is None else src, dst_ref=dst,
                send_sem=send_sems.at[a * 7 + k], recv_sem=recv_sems.at[a * 7 + k],
                device_id=to, device_id_type=MESH)

        local = [pltpu.make_async_copy(ins[a], outs[a].at[_slot(*me)], local_sems.at[a]) for a in range(self.n)]
        first = []
        for a in range(self.n):
            first.append(copy(a, 0, me, sibling, src=ins[a]))
            first += [copy(a, 1 + j, me, (*chip, c), src=ins[a]) for j, chip in enumerate(chips)]
        ici_in = [copy(a, 1 + j, (*chip, c), me) for j, chip in enumerate(chips) for a in range(self.n)]
        passed = [copy(a, 4 + j, (*chip, c), sibling) for j, chip in enumerate(chips) for a in range(self.n)]
        d2d_in = []
        for a in range(self.n):
            d2d_in.append(copy(a, 0, sibling, me))
            d2d_in += [copy(a, 4 + j, (*chip, 1 - c), me) for j, chip in enumerate(chips)]
        return local, first, ici_in, passed, d2d_in

    def start(self, ins, outs, sems):
        local, first, _, _, _ = self._parts(ins, outs, sems)
        for cp in local + first:
            cp.start()

    def mid(self, ins, outs, sems):
        _, _, ici_in, passed, _ = self._parts(ins, outs, sems)
        for arrived, fw in zip(ici_in, passed):
            arrived.wait_recv()
            fw.start()

    def finish(self, ins, outs, sems):
        local, first, _, passed, d2d_in = self._parts(ins, outs, sems)
        for cp in d2d_in:
            cp.wait_recv()
        for cp in first + passed:
            cp.wait_send()
        for cp in local:
            cp.wait()


class _ExchangeComm:
    has_mid = False

    def __init__(self, arrs):
        self.arrs = list(arrs)
        self.n = len(self.arrs)

    def out_shapes(self):
        return [jax.ShapeDtypeStruct(a.shape, a.dtype) for a in self.arrs]

    def sems(self):
        return [pltpu.SemaphoreType.DMA((7 * self.n,)), pltpu.SemaphoreType.DMA((7 * self.n,)),
                pltpu.SemaphoreType.DMA((self.n,))]

    def _copies(self, ins, outs, sems):
        send_sems, recv_sems, local_sems = sems
        x, y, c = _my_pos()
        mine = _slot(x, y, c)
        copies = [pltpu.make_async_copy(ins[a].at[mine], outs[a].at[mine], local_sems.at[a]) for a in range(self.n)]
        for k in range(1, N_DEV):
            px = (1 - x) if (k & 4) else x
            py = (1 - y) if (k & 2) else y
            pc = (1 - c) if (k & 1) else c
            for a in range(self.n):
                copies.append(pltpu.make_async_remote_copy(
                    src_ref=ins[a].at[_slot(px, py, pc)], dst_ref=outs[a].at[mine],
                    send_sem=send_sems.at[a * 7 + k - 1], recv_sem=recv_sems.at[a * 7 + k - 1],
                    device_id=(px, py, pc), device_id_type=MESH))
        return copies

    def start(self, ins, outs, sems):
        for cp in self._copies(ins, outs, sems):
            cp.start()

    def finish(self, ins, outs, sems):
        for cp in self._copies(ins, outs, sems):
            cp.wait()


class _BothComm:
    has_mid = True

    def __init__(self, first, second):
        self.parts = (first, second)
        self.arrs = first.arrs + second.arrs
        self.n = first.n + second.n
        self.mid_frac = second.mid_frac

    def out_shapes(self):
        return self.parts[0].out_shapes() + self.parts[1].out_shapes()

    def sems(self):
        return self.parts[0].sems() + self.parts[1].sems()

    def _each(self, ins, outs, sems):
        a, b = self.parts
        return ((a, ins[:a.n], outs[:a.n], sems[:3]), (b, ins[a.n:], outs[a.n:], sems[3:]))

    def start(self, ins, outs, sems):
        for cm, i_, o_, s_ in self._each(ins, outs, sems):
            cm.start(i_, o_, s_)

    def mid(self, ins, outs, sems):
        for cm, i_, o_, s_ in self._each(ins, outs, sems):
            if cm.has_mid:
                cm.mid(i_, o_, s_)

    def finish(self, ins, outs, sems):
        for cm, i_, o_, s_ in self._each(ins, outs, sems):
            cm.finish(i_, o_, s_)


def _fused_call(body, comm, operands, *, name, grid, in_specs, out_specs, out_shape, scratch_shapes=(),
                semantics=("arbitrary",)):
    n_in, n_out, n_scr = len(in_specs), len(out_specs), len(scratch_shapes)
    if comm is None:
        res = _pallas(body, name=name, grid=grid, in_specs=list(in_specs), out_specs=list(out_specs),
                      out_shape=list(out_shape), scratch_shapes=list(scratch_shapes),
                      compiler_params=_params(semantics))(*operands)
        return list(res), []
    k = comm.n
    steps = grid[0]

    def wrapped(*refs):
        ins, cins = refs[:n_in], refs[n_in:n_in + k]
        outs = refs[n_in + k:n_in + k + n_out]
        couts = refs[n_in + k + n_out:n_in + 2 * k + n_out]
        rest = refs[n_in + 2 * k + n_out:]
        scratch, sems = rest[:n_scr], rest[n_scr:]
        i = pl.program_id(0)

        @pl.when(i == 0)
        def _():
            comm.start(cins, couts, sems)

        body(*ins, *outs, *scratch)

        if comm.has_mid:
            @pl.when(i == int(steps * comm.mid_frac))
            def _():
                comm.mid(cins, couts, sems)

        @pl.when(i == steps - 1)
        def _():
            comm.finish(cins, couts, sems)

    any_spec = pl.BlockSpec(memory_space=pl.ANY)
    res = _pallas(wrapped, name=name, grid=grid, in_specs=list(in_specs) + [any_spec] * k,
                  out_specs=list(out_specs) + [any_spec] * k, out_shape=list(out_shape) + comm.out_shapes(),
                  scratch_shapes=list(scratch_shapes) + comm.sems(),
                  compiler_params=_params(("arbitrary",)))(*operands, *comm.arrs)
    return list(res[:n_out]), list(res[n_out:])


def _head_gather(c, ada_w, big, to_cast, vec_parts):
    cols = ada_w.shape[2]
    g_c, g_big = _GatherComm([c]), _GatherComm(big)
    g_mod = _GatherComm([jax.ShapeDtypeStruct((2, N_DEV, cols), F32)])
    g_vec = _GatherComm([jax.ShapeDtypeStruct((VEC_ROWS, LANE), F32)])
    nb, nc, nv = g_big.n, len(to_cast), len(vec_parts)

    def body(*refs):
        c_ref, w_ref = refs[0], refs[1]
        vec_in = refs[2:2 + nv]
        cast_in = refs[2 + nv:2 + nv + nc]
        big_in = refs[2 + nv + nc:2 + nv + nc + nb]
        outs = refs[2 + nv + nc + nb:]
        c_all_ref, mod_all_ref, vec_all_ref = outs[0], outs[1], outs[2]
        cast_out = outs[3:3 + nc]
        big_out = outs[3 + nc:3 + nc + nb]
        part_ref, pack_ref = outs[3 + nc + nb], outs[4 + nc + nb]
        sems = outs[5 + nc + nb:]
        s_c, s_mod, s_big, s_vec = sems[0:3], sems[3:6], sems[6:9], sems[9:12]
        g_c.start([c_ref], [c_all_ref], s_c)
        g_big.start(big_in, big_out, s_big)
        pack_ref[...] = jnp.zeros_like(pack_ref)
        row = 0
        for ref, (_, nrows) in zip(vec_in, VEC_LAYOUT):
            pack_ref[row:row + nrows, :] = ref[0] if len(ref.shape) == 3 else ref[...]
            row += nrows
        g_vec.start([pack_ref], [vec_all_ref], s_vec)
        g_c.mid([c_ref], [c_all_ref], s_c)
        g_c.finish([c_ref], [c_all_ref], s_c)
        cv = c_all_ref[:, 0, :]
        cond = cv * _sigmoid(cv)
        for l in range(2):
            part_ref[l] = _dot(cond, w_ref[l])
        g_mod.start([part_ref], [mod_all_ref], s_mod)
        for src, dst in zip(cast_in, cast_out):
            dst[...] = src[...].astype(BF16)
        for g, ins, outs_, sm in ((g_vec, [pack_ref], [vec_all_ref], s_vec), (g_mod, [part_ref], [mod_all_ref], s_mod),
                                  (g_big, big_in, big_out, s_big)):
            g.mid(ins, outs_, sm)
            g.finish(ins, outs_, sm)

    any_spec = pl.BlockSpec(memory_space=pl.ANY)
    vmem_spec = pl.BlockSpec(memory_space=pltpu.VMEM)
    res = _pallas(
        body, name="head_gather",
        out_shape=(g_c.out_shapes() + g_mod.out_shapes() + g_vec.out_shapes()
                   + [jax.ShapeDtypeStruct(a.shape, BF16) for a in to_cast] + g_big.out_shapes()),
        in_specs=[vmem_spec] * (2 + nv + nc) + [any_spec] * nb,
        out_specs=[vmem_spec] * (3 + nc) + [any_spec] * nb,
        scratch_shapes=[pltpu.VMEM((2, N_DEV, cols), F32), pltpu.VMEM((VEC_ROWS, LANE), F32)]
        + g_c.sems() + g_mod.sems() + g_big.sems() + g_vec.sems(),
        compiler_params=pltpu.CompilerParams(vmem_limit_bytes=VMEM_LIMIT),
    )(c, ada_w, *vec_parts, *to_cast, *big)
    return res[0], res[1], res[2], list(res[3 + nc:]), list(res[3:3 + nc])


def _ada_update(c_all, dmod_cols, dmod_all, ada_w, m_w, v_w, ada_b, m_b, v_b):
    cols = ada_w.shape[2]
    nb = ada_b.shape[1]

    def body(c_ref, dmc_ref, dma_ref, w_ref, mw_ref, vw_ref, b_ref, mb_ref, vb_ref,
             gw_ref, dw_ref, nmw_ref, nvw_ref, gb_ref, db_ref, nmb_ref, nvb_ref):
        cv = c_ref[...]
        cond = cv * _sigmoid(cv)
        for l in range(2):
            g = _dot_tn(cond, dmc_ref[l])
            gw_ref[l] = g
            dlt, m2, v2 = _adam(w_ref[l], g, mw_ref[l], vw_ref[l])
            dw_ref[l] = dlt
            nmw_ref[l] = m2
            nvw_ref[l] = v2
        gb = dma_ref[0]
        for i in range(1, N_DEV):
            gb = gb + dma_ref[i]
        gb_ref[...] = gb
        dlt, m2, v2 = _adam(b_ref[...], gb, mb_ref[...], vb_ref[...])
        db_ref[...] = dlt
        nmb_ref[...] = m2
        nvb_ref[...] = v2

    wspec = _full((2, D, cols))
    bspec = _full((2, nb))
    wshape = jax.ShapeDtypeStruct((2, D, cols), F32)
    bshape = jax.ShapeDtypeStruct((2, nb), F32)
    return _pallas(
        body, name="ada_update", grid=(1,),
        in_specs=[_full((N_DEV, D)), _full((2, N_DEV, cols)), _full((N_DEV, 2, nb)),
                  wspec, wspec, wspec, bspec, bspec, bspec],
        out_specs=[wspec] * 4 + [bspec] * 4,
        out_shape=[wshape] * 4 + [bshape] * 4,
        compiler_params=_params(("arbitrary",)),
    )(c_all, dmod_cols, dmod_all, ada_w, m_w, v_w, ada_b, m_b, v_b)


def _ev_in(x, mod, w_in, rc, rs1, rs2, comm=None):
    T = x.shape[0]

    def body(x_ref, mod_ref, w_ref, c_ref, s1_ref, s2_ref, q_ref, kv_ref, su_ref, sv_ref, g_ref):
        h = x_ref[...] * (1.0 + mod_ref[1:2, :]) + mod_ref[0:1, :]
        p = _dot_nt(h, w_ref[...])
        c, s1, s2 = c_ref[...], s1_ref[...], s2_ref[...]
        for j in range(ATTN_W // LANE):
            qr = _rope_fwd(p[:, j * LANE:(j + 1) * LANE], c, s1, s2)
            q_ref[:, j * LANE:(j + 1) * LANE] = (qr * (HEAD_DIM ** -0.5)).astype(BF16)
        low = lax.broadcasted_iota(jnp.int32, (TMF, LANE), 1) < HEAD_DIM
        for j, val in enumerate((_rope_fwd(p[:, 512:640], c, s1, s2), p[:, 640:768])):
            swapped = pltpu.roll(val, HEAD_DIM, 1)
            tiles = (jnp.where(low, val, 0.0), jnp.where(low, 0.0, swapped),
                     jnp.where(low, swapped, 0.0), jnp.where(low, 0.0, val))
            for k, tile in enumerate(tiles):
                kv_ref[:, (4 * j + k) * LANE:(4 * j + k + 1) * LANE] = tile.astype(BF16)
        su_ref[...] = p[:, 768:1280].astype(BF16)
        sv_ref[...] = p[:, 1280:1792].astype(BF16)
        g_ref[...] = p[:, 1792:2816].astype(BF16)

    sh = lambda w: jax.ShapeDtypeStruct((T, w), BF16)
    return _fused_call(
        body, comm, (x, mod, w_in, rc, rs1, rs2), name="ev_in", grid=(T // TMF,),
        in_specs=[_tile(TMF, D), _full((3, D)), _full((EV_IN, D)), _tile(TMF, LANE), _tile(TMF, LANE),
                  _tile(TMF, LANE)],
        out_specs=[_tile(TMF, ATTN_W), _tile(TMF, KVX_W), _tile(TMF, SG_W), _tile(TMF, SG_W), _tile(TMF, D)],
        out_shape=[sh(ATTN_W), sh(KVX_W), sh(SG_W), sh(SG_W), sh(D)], semantics=("parallel",))


def _band_specs(width, nb, k):
    return [pl.BlockSpec((BLK, width), lambda n: (jnp.maximum(k * n - 1, 0), 0)),
            pl.BlockSpec((k * BLK, width), lambda n: (n, 0)),
            pl.BlockSpec((BLK, width), lambda n: (jnp.minimum(k * n + k, nb - 1), 0))]


def _band_bias(bias_ref, n, nb):
    rows = lax.broadcasted_iota(jnp.int32, (3 * BLK, 1), 0)
    outside = ((rows < BLK) & (n == 0)) | ((rows >= 2 * BLK) & (n == nb - 1))
    return bias_ref[...] + jnp.where(outside, NEG_INF, 0.0)


def _lane_tile(ref, t):
    return ref[:, t * LANE:(t + 1) * LANE]


def _split_bf16(v):
    hi = v.astype(BF16)
    return hi, (v - hi.astype(F32)).astype(BF16)


def _group_mean(v, a_ref, exact_bf16=False):
    hi, lo = _split_bf16(v)
    a = a_ref[...]
    out = []
    for t in range(SG_W // (2 * LANE)):
        sl = slice(t * 2 * LANE, (t + 1) * 2 * LANE)
        r = jnp.dot(hi[:, sl], a, preferred_element_type=F32)
        if not exact_bf16:
            r = r + jnp.dot(lo[:, sl], a, preferred_element_type=F32)
        out.append(r)
    return jnp.concatenate(out, axis=-1)


def _sg_core(sv_ref, lng, lnb, a_ref, w_ref, bfull_ref):
    svf = sv_ref[...].astype(F32)
    xc = svf - _group_mean(svf, a_ref, exact_bf16=True)
    rstd = lax.rsqrt(_group_mean(xc * xc, a_ref) + LN_EPS)
    xhat = xc * rstd
    vb = (xhat * lng + lnb).astype(BF16)
    low = lax.broadcasted_iota(jnp.int32, (BLK, LANE), 1) < SG_DIM
    tiles = []
    for t in range(SG_W // LANE):
        v2 = vb[:, t * LANE:(t + 1) * LANE]
        r0 = jnp.dot(w_ref[2 * t], v2, preferred_element_type=F32)
        r1 = jnp.dot(w_ref[2 * t + 1], v2, preferred_element_type=F32)
        tiles.append(jnp.where(low, r0, r1))
    svm = jnp.concatenate(tiles, axis=-1) + bfull_ref[...]
    return xhat, rstd, vb, svm


def _mix0_fwd(q, kvx, su, sv, g0, sink_l, bias, a128, sg_lng, sg_lnb, sg_w, sg_bfull, comm=None):
    T = q.shape[0]
    nb = T // BLK

    def body(q_ref, kp_ref, kc_ref, kn_ref, su_ref, sv_ref, g_ref, sink_ref, bias_ref, a_ref, lng_ref, lnb_ref,
             w_ref, bfull_ref, ycat_ref, y0_ref, lse_ref):
        n = pl.program_id(0)
        kvx4 = jnp.concatenate([kp_ref[...], kc_ref[...], kn_ref[...]], axis=0)
        for s in range(FWD_BLOCKS):
            rows = slice(s * BLK, (s + 1) * BLK)
            bias = _band_bias(bias_ref, FWD_BLOCKS * n + s, nb)
            kvx = kvx4[s * BLK:s * BLK + 3 * BLK]
            tiles = []
            for t in range(ATTN_W // LANE):
                qt = q_ref[rows, t * LANE:(t + 1) * LANE]
                acc = None
                for par in range(2):
                    h = 2 * t + par
                    kt = 2 * (h // 4) + par
                    ke = kvx[:, kt * LANE:(kt + 1) * LANE]
                    ve = kvx[:, (4 + kt) * LANE:(5 + kt) * LANE]
                    st = _dot_nt(ke, qt) + bias
                    sk = _lane_tile(sink_ref, h)
                    m = jnp.maximum(jnp.max(st, axis=0, keepdims=True), sk)
                    p = jnp.exp(st - m)
                    denom = jnp.sum(p, axis=0, keepdims=True) + jnp.exp(sk - m)
                    contrib = _dot_tn(p * (1.0 / denom), ve)
                    acc = contrib if acc is None else acc + contrib
                    lse_ref[s, :, h * LANE:(h + 1) * LANE] = m + jnp.log(denom)
                tiles.append(acc)
            _, _, _, svm = _sg_core(sv_ref.at[rows, :], lng_ref[...], lnb_ref[...], a_ref, w_ref, bfull_ref)
            tiles.append(su_ref[rows, :].astype(F32) * svm)
            ycat = jnp.concatenate(tiles, axis=-1)
            gf = g_ref[rows, :].astype(F32)
            ycat_ref[rows, :] = ycat.astype(BF16)
            y0_ref[rows, :] = (ycat * (gf * _sigmoid(gf))).astype(BF16)

    two = FWD_BLOCKS * BLK
    return _fused_call(
        body, comm, (q, kvx, kvx, kvx, su, sv, g0, sink_l, bias, a128, sg_lng, sg_lnb, sg_w, sg_bfull),
        name="mix0_fwd", grid=(nb // FWD_BLOCKS,),
        in_specs=[_tile(two, ATTN_W)] + _band_specs(KVX_W, nb, FWD_BLOCKS) + [
                  _tile(two, SG_W), _tile(two, SG_W), _tile(two, D), _full((1, N_HEADS * LANE)),
                  _full((3 * BLK, LANE)), _full((2 * LANE, 2 * LANE)),_full((1, SG_W)), _full((1, SG_W)),
                  _full((SG_GROUPS, BLK, BLK)), _full((BLK, SG_W))],
        out_specs=[_tile(two, D), _tile(two, D),
                   pl.BlockSpec((FWD_BLOCKS, 1, N_HEADS * LANE), lambda n: (n, 0, 0))],
        out_shape=[jax.ShapeDtypeStruct((T, D), BF16), jax.ShapeDtypeStruct((T, D), BF16),
                   jax.ShapeDtypeStruct((nb, 1, N_HEADS * LANE), F32)], semantics=("parallel",))


def _ev_out(y0, w_out, x, mod, lnp):
    T = x.shape[0]

    def body(y_ref, w_ref, x_ref, mod_ref, ln_ref, out_ref, z_ref, x1_ref):
        out = _dot(y_ref[...], w_ref[...])
        z = ALPHA * x_ref[...] + mod_ref[2:3, :] * out
        x1, _, _ = _ln_fwd(z, ln_ref[0:1, :], ln_ref[1:2, :])
        out_ref[...] = out.astype(BF16)
        z_ref[...] = z
        x1_ref[...] = x1

    return _pallas(
        body, name="ev_out", grid=(T // TMF,),
        in_specs=[_tile(TMF, D), _full((D, D)), _tile(TMF, D), _full((3, D)), _full((2, D))],
        out_specs=[_tile(TMF, D)] * 3,
        out_shape=[jax.ShapeDtypeStruct((T, D), BF16), jax.ShapeDtypeStruct((T, D), F32),
                   jax.ShapeDtypeStruct((T, D), F32)],
        compiler_params=_params(("parallel",)),
    )(y0, w_out, x, mod, lnp)


def _od_in(x1, mod, w_in):
    T = x1.shape[0]

    def body(x_ref, mod_ref, w_ref, xr_ref, g_ref):
        h = x_ref[...] * (1.0 + mod_ref[1:2, :]) + mod_ref[0:1, :]
        p = _dot(h, w_ref[...])
        xr_ref[...] = p[:, :D]
        g_ref[...] = p[:, D:].astype(BF16)

    return _pallas(
        body, name="od_in", grid=(T // TMF,),
        in_specs=[_tile(TMF, D), _full((3, D)), _full((D, OD_IN))],
        out_specs=[_tile(TMF, D), _tile(TMF, D)],
        out_shape=[jax.ShapeDtypeStruct((T, D), F32), jax.ShapeDtypeStruct((T, D), BF16)],
        compiler_params=_params(("parallel",)),
    )(x1, mod, w_in)


def _ext_rows(prev_ref, cur, next_ref, j, n):
    prev = jnp.where(j > 0, prev_ref[...], 0.0)
    nxt = jnp.where(j < n - 1, next_ref[...], 0.0)
    return jnp.concatenate([prev, cur, nxt], axis=0)


def _shift_rows(ext, off, rows):
    total = ext.shape[0]
    if off == 0:
        return ext[SUBLANE:SUBLANE + rows, :]
    return pltpu.roll(ext, (-off) % total, 0)[SUBLANE:SUBLANE + rows, :]


def _conv_fwd(ext, cw, cb, rows):
    xc = cb
    for k in range(4):
        xc = xc + cw[k:k + 1, :] * _shift_rows(ext, k - 2, rows)
    return xc


def _gates(xc, wa_ref, wx_ref, ba, bx, lam):
    pr, pi = [], []
    for h in range(RNN_HEADS):
        xh = xc[:, h * RNN_HD:(h + 1) * RNN_HD].astype(BF16)
        pr.append(_dot(xh, wa_ref[h]))
        pi.append(_dot(xh, wx_ref[h]))
    r = _sigmoid(jnp.concatenate(pr, axis=-1) + ba)
    ig = _sigmoid(jnp.concatenate(pi, axis=-1) + bx)
    sp = jnp.maximum(-lam, 0.0) + jnp.log(1.0 + jnp.exp(-jnp.abs(lam)))
    neg_log_a = RG_C * r * sp
    a = jnp.exp(-neg_log_a)
    s2 = (1.0 + a * a) * jnp.tanh(neg_log_a)
    inv_s = lax.rsqrt(jnp.maximum(s2, 1e-30))
    return r, ig, sp, a, s2 * inv_s, inv_s


def _scan_tile(a_ref, b_ref, o_ref, carry_ref, rows, reverse):
    ridx = lax.broadcasted_iota(jnp.int32, (SUBLANE, D), 0)
    groups = rows // SUBLANE

    def group(gi, h):
        g = (groups - 1 - gi) if reverse else gi
        off = pl.multiple_of(g * SUBLANE, SUBLANE)
        a = a_ref[pl.ds(off, SUBLANE), :]
        b = b_ref[pl.ds(off, SUBLANE), :]
        for sh in (1, 2, 4):
            if reverse:
                keep = ridx < SUBLANE - sh
                a_p = jnp.where(keep, pltpu.roll(a, SUBLANE - sh, 0), 1.0)
                b_p = jnp.where(keep, pltpu.roll(b, SUBLANE - sh, 0), 0.0)
            else:
                keep = ridx >= sh
                a_p = jnp.where(keep, pltpu.roll(a, sh, 0), 1.0)
                b_p = jnp.where(keep, pltpu.roll(b, sh, 0), 0.0)
            b = b + a * b_p
            a = a * a_p
        hh = b + a * h
        o_ref[pl.ds(off, SUBLANE), :] = hh
        return hh[0:1, :] if reverse else hh[SUBLANE - 1:SUBLANE, :]

    carry_ref[...] = lax.fori_loop(0, groups, group, carry_ref[...])


def _rglru_fwd(xr, cw, cb, wa, wx, ba, bx, lam, reverse, name):
    T = xr.shape[0]
    n = T // TS
    prev_spec, next_spec = _halo_specs(TS, D, n, T, reverse)

    def body(prev_ref, cur_ref, next_ref, cw_ref, cb_ref, wa_ref, wx_ref, ba_ref, bx_ref, lam_ref,
             h_ref, a_ref, s_ref, r_ref, ig_ref, xc_ref, b_s, carry):
        i = pl.program_id(0)
        j = (n - 1 - i) if reverse else i

        @pl.when(i == 0)
        def _():
            carry[...] = jnp.zeros_like(carry)

        ext = _ext_rows(prev_ref, cur_ref[...], next_ref, j, n)
        xc = _conv_fwd(ext, cw_ref[...], cb_ref[...], TS)
        r, ig, _, a, s, _ = _gates(xc, wa_ref, wx_ref, ba_ref[...], bx_ref[...], lam_ref[...])
        s_ref[...] = s
        r_ref[...] = r.astype(BF16)
        ig_ref[...] = ig.astype(BF16)
        xc_ref[...] = xc.astype(BF16)
        a_ref[...] = a
        b_s[...] = s * ig * xc
        _scan_tile(a_ref, b_s, h_ref, carry, TS, reverse)

    wspec = _full((RNN_HEADS, RNN_HD, RNN_HD))
    cur = _rev_tile(TS, D, n, reverse)
    f32 = jax.ShapeDtypeStruct((T, D), F32)
    b16 = jax.ShapeDtypeStruct((T, D), BF16)
    return _pallas(
        body, name=name, grid=(n,),
        in_specs=[prev_spec, cur, next_spec, _full((4, D)), _full((1, D)),
                  wspec, wspec, _full((1, D)), _full((1, D)), _full((1, D))],
        out_specs=[cur] * 6,
        out_shape=[f32, f32, f32, b16, b16, b16],
        scratch_shapes=[pltpu.VMEM((TS, D), F32), pltpu.VMEM((1, D), F32)],
        compiler_params=_params(("arbitrary",)),
    )(xr, xr, xr, cw, cb, wa, wx, ba, bx, lam)


def _od_out(hf, hb, g1, w_out, x1, tgt, mod, lnp):
    T = x1.shape[0]

    def body(hf_ref, hb_ref, g_ref, w_ref, x_ref, t_ref, mod_ref, ln_ref,
             dh_ref, dg_ref, dx_ref, dwb_ref, vec_ref, dw_ref):
        i = pl.program_id(0)

        @pl.when(i == 0)
        def _():
            dw_ref[...] = jnp.zeros_like(dw_ref)
            vec_ref[...] = jnp.zeros_like(vec_ref)

        hs = hf_ref[...] + hb_ref[...]
        sg, dsg = _silu_and_grad(g_ref[...].astype(F32))
        yr = (hs * sg).astype(BF16)
        w = w_ref[...]
        out = _dot(yr, w)
        gate = mod_ref[2:3, :]
        z = ALPHA * x_ref[...] + gate * out
        lng = ln_ref[0:1, :]
        x2, xhat, rstd = _ln_fwd(z, lng, ln_ref[1:2, :])
        diff = x2 - t_ref[...]
        vec_ref[3:4, 0:LANE] += 0.5 * jnp.sum(diff * diff) * (1.0 / D)
        dx2 = diff * (1.0 / D)
        dz = _ln_bwd(dx2, xhat, rstd, lng)
        vec_ref[0:1, :] += _rowsum(dx2 * xhat)
        vec_ref[1:2, :] += _rowsum(dx2)
        vec_ref[2:3, :] += _rowsum(dz * out)
        dout = (dz * gate).astype(BF16)
        dyr = _dot_nt(dout, w)
        dw_ref[...] += _dot_tn(yr, dout)
        dh_ref[...] = dyr * sg
        dg_ref[...] = (dyr * hs * dsg).astype(BF16)
        dx_ref[...] = ALPHA * dz

        @pl.when(i == T // TMO - 1)
        def _():
            dwb_ref[...] = dw_ref[...].astype(BF16)

    return _pallas(
        body, name="od_out", grid=(T // TMO,),
        in_specs=[_tile(TMO, D), _tile(TMO, D), _tile(TMO, D), _full((D, D)), _tile(TMO, D), _tile(TMO, D),
                  _full((3, D)), _full((2, D))],
        out_specs=[_tile(TMO, D), _tile(TMO, D), _tile(TMO, D), _full((D, D)), _full((SUBLANE, D))],
        out_shape=[jax.ShapeDtypeStruct((T, D), F32), jax.ShapeDtypeStruct((T, D), BF16),
                   jax.ShapeDtypeStruct((T, D), F32), jax.ShapeDtypeStruct((D, D), BF16),
                   jax.ShapeDtypeStruct((SUBLANE, D), F32)],
        scratch_shapes=[pltpu.VMEM((D, D), F32)],
        compiler_params=_params(("arbitrary",)),
    )(hf, hb, g1, w_out, x1, tgt, mod, lnp)


def _rglru_bwd(fwd, dh, wa, wx, lam, reverse, name, comm=None):
    h, a_all, s_all, r_all, ig_all, xc_all = fwd
    T = h.shape[0]
    n = T // TS
    adj_rev = not reverse
    hprev_spec, hnext_spec = _halo_specs(TS, D, n, T, adj_rev)
    h_halo_spec = hnext_spec if reverse else hprev_spec

    def body(dh_ref, h_ref, hh_ref, a_ref, s_ref, r_ref, ig_ref, xc_ref, wa_ref, wx_ref, lam_ref,
             dxc_ref, dwa_ref, dwx_ref, vec_ref, a_s, l_s, carry, a_edge):
        i = pl.program_id(0)
        j = (n - 1 - i) if adj_rev else i

        @pl.when(i == 0)
        def _():
            carry[...] = jnp.zeros_like(carry)
            a_edge[...] = jnp.zeros_like(a_edge)
            dwa_ref[...] = jnp.zeros_like(dwa_ref)
            dwx_ref[...] = jnp.zeros_like(dwx_ref)
            vec_ref[...] = jnp.zeros_like(vec_ref)

        lam = lam_ref[...]
        sp = jnp.maximum(-lam, 0.0) + jnp.log(1.0 + jnp.exp(-jnp.abs(lam)))
        a, s = a_ref[...], s_ref[...]
        inv_s = lax.rsqrt(jnp.maximum(s * s, 1e-30))
        r, ig = r_ref[...].astype(F32), ig_ref[...].astype(F32)
        xcb = xc_ref[...]
        xc = xcb.astype(F32)

        rows = lax.broadcasted_iota(jnp.int32, (TS, D), 0)
        hcur = h_ref[...]
        if reverse:
            a_sh = jnp.where(rows == 0, a_edge[...], pltpu.roll(a, 1, 0))
            halo = jnp.where(j < n - 1, hh_ref[0:1, :], 0.0)
            h_nb = jnp.where(rows == TS - 1, halo, pltpu.roll(hcur, TS - 1, 0))
        else:
            a_sh = jnp.where(rows == TS - 1, a_edge[...], pltpu.roll(a, TS - 1, 0))
            halo = jnp.where(j > 0, hh_ref[SUBLANE - 1:SUBLANE, :], 0.0)
            h_nb = jnp.where(rows == 0, halo, pltpu.roll(hcur, 1, 0))
        a_s[...] = a_sh
        _scan_tile(a_s, dh_ref, l_s, carry, TS, adj_rev)
        a_edge[...] = a[TS - 1:TS, :] if reverse else a[0:1, :]

        lm = l_s[...]
        da = lm * h_nb
        di = lm * s * xc
        dxc = lm * s * ig
        ds = lm * ig * xc
        dlog_a = a * (da - ds * a * inv_s)
        dr = (-RG_C) * sp * dlog_a
        dsp = _rowsum((-RG_C) * r * dlog_a)
        dpr = dr * r * (1.0 - r)
        dpi = di * ig * (1.0 - ig)
        vec_ref[0:1, :] += _rowsum(dpr)
        vec_ref[1:2, :] += _rowsum(dpi)
        vec_ref[2:3, :] += dsp * (-_sigmoid(-lam))
        parts = []
        for hd in range(RNN_HEADS):
            sl = slice(hd * RNN_HD, (hd + 1) * RNN_HD)
            xh = xcb[:, sl]
            dprh = dpr[:, sl].astype(BF16)
            dpih = dpi[:, sl].astype(BF16)
            parts.append(_dot_nt(dprh, wa_ref[hd]) + _dot_nt(dpih, wx_ref[hd]))
            dwa_ref[hd] += _dot_tn(xh, dprh)
            dwx_ref[hd] += _dot_tn(xh, dpih)
        dxc_ref[...] = dxc + jnp.concatenate(parts, axis=-1)

    wspec = _full((RNN_HEADS, RNN_HD, RNN_HD))
    cur = _rev_tile(TS, D, n, adj_rev)
    return _fused_call(
        body, comm, (dh, h, h, a_all, s_all, r_all, ig_all, xc_all, wa, wx, lam), name=name, grid=(n,),
        in_specs=[cur, cur, h_halo_spec, cur, cur, cur, cur, cur, wspec, wspec, _full((1, D))],
        out_specs=[cur, wspec, wspec, _full((SUBLANE, D))],
        out_shape=[jax.ShapeDtypeStruct((T, D), F32),
                   jax.ShapeDtypeStruct((RNN_HEADS, RNN_HD, RNN_HD), F32),
                   jax.ShapeDtypeStruct((RNN_HEADS, RNN_HD, RNN_HD), F32),
                   jax.ShapeDtypeStruct((SUBLANE, D), F32)],
        scratch_shapes=[pltpu.VMEM((TS, D), F32)] * 2 + [pltpu.VMEM((1, D), F32)] * 2)


def _od_in_bwd(dxcf, dxcb, xr, dg1, x1, dx1p, mod, w_in, cw, comm=None):
    T = x1.shape[0]
    n = T // TMO
    slab = OD_IN // N_DEV
    prev_spec, next_spec = _halo_specs(TMO, D, n, T, False)

    def body(fp_ref, fc_ref, fn_ref, bp_ref, bc_ref, bn_ref, xr_ref, dg_ref, x1_ref, dxp_ref,
             mod_ref, w_ref, cw_ref, dx_ref, dwb_ref, vec_ref, dw_ref):
        i = pl.program_id(0)

        @pl.when(i == 0)
        def _():
            dw_ref[...] = jnp.zeros_like(dw_ref)
            vec_ref[...] = jnp.zeros_like(vec_ref)

        dcur = fc_ref[...] + bc_ref[...]
        dprev = jnp.where(i > 0, fp_ref[...] + bp_ref[...], 0.0)
        dnext = jnp.where(i < n - 1, fn_ref[...] + bn_ref[...], 0.0)
        dext = jnp.concatenate([dprev, dcur, dnext], axis=0)
        xr_v = xr_ref[...]
        cw_v = cw_ref[...]
        dxr = None
        for k in range(4):
            shifted = _shift_rows(dext, 2 - k, TMO)
            term = cw_v[k:k + 1, :] * shifted
            dxr = term if dxr is None else dxr + term
            vec_ref[k:k + 1, :] += _rowsum(shifted * xr_v)
        vec_ref[4:5, :] += _rowsum(dcur)
        dp = jnp.concatenate([dxr.astype(BF16), dg_ref[...]], axis=-1)
        x1v = x1_ref[...]
        scale1 = 1.0 + mod_ref[1:2, :]
        h1 = (x1v * scale1 + mod_ref[0:1, :]).astype(BF16)
        dh1 = _dot_nt(dp, w_ref[...])
        dw_ref[...] += _dot_tn(h1, dp)
        dx_ref[...] = dxp_ref[...] + dh1 * scale1
        vec_ref[5:6, :] += _rowsum(dh1)
        vec_ref[6:7, :] += _rowsum(dh1 * x1v)

        @pl.when(i == n - 1)
        def _():
            for j in range(N_DEV):
                dwb_ref[j] = dw_ref[:, j * slab:(j + 1) * slab].astype(BF16)

    t = _tile(TMO, D)
    return _fused_call(
        body, comm, (dxcf, dxcf, dxcf, dxcb, dxcb, dxcb, xr, dg1, x1, dx1p, mod, w_in, cw),
        name="od_in_bwd", grid=(n,),
        in_specs=[prev_spec, t, next_spec, prev_spec, t, next_spec, t, t, t, t,
                  _full((3, D)), _full((D, OD_IN)), _full((4, D))],
        out_specs=[t, _full((N_DEV, D, slab)), _full((SUBLANE, D))],
        out_shape=[jax.ShapeDtypeStruct((T, D), F32), jax.ShapeDtypeStruct((N_DEV, D, slab), BF16),
                   jax.ShapeDtypeStruct((SUBLANE, D), F32)],
        scratch_shapes=[pltpu.VMEM((D, OD_IN), F32)])


def _ev_out_bwd(dx1, z0, out0, y0, ycat, g0, w_out, mod, lnp):
    T = dx1.shape[0]

    def body(dx_ref, z_ref, out_ref, y0_ref, yc_ref, g_ref, w_ref, mod_ref, ln_ref,
             dxp_ref, dyc_ref, dg_ref, dwb_ref, vec_ref, dw_ref):
        i = pl.program_id(0)

        @pl.when(i == 0)
        def _():
            dw_ref[...] = jnp.zeros_like(dw_ref)
            vec_ref[...] = jnp.zeros_like(vec_ref)

        lng = ln_ref[0:1, :]
        _, xhat, rstd = _ln_fwd(z_ref[...], lng, ln_ref[1:2, :])
        dy = dx_ref[...]
        dz = _ln_bwd(dy, xhat, rstd, lng)
        vec_ref[0:1, :] += _rowsum(dy * xhat)
        vec_ref[1:2, :] += _rowsum(dy)
        vec_ref[2:3, :] += _rowsum(dz * out_ref[...].astype(F32))
        dout = (dz * mod_ref[2:3, :]).astype(BF16)
        dy0 = _dot_nt(dout, w_ref[...])
        dw_ref[...] += _dot_tn(y0_ref[...], dout)
        sg, dsg = _silu_and_grad(g_ref[...].astype(F32))
        dyc_ref[...] = (dy0 * sg).astype(BF16)
        dg_ref[...] = (dy0 * yc_ref[...].astype(F32) * dsg).astype(BF16)
        dxp_ref[...] = ALPHA * dz

        @pl.when(i == T // TMO - 1)
        def _():
            dwb_ref[...] = dw_ref[...].astype(BF16)

    t = _tile(TMO, D)
    return _pallas(
        body, name="ev_out_bwd", grid=(T // TMO,),
        in_specs=[t, t, t, t, t, t, _full((D, D)), _full((3, D)), _full((2, D))],
        out_specs=[t, t, t, _full((D, D)), _full((SUBLANE, D))],
        out_shape=[jax.ShapeDtypeStruct((T, D), F32), jax.ShapeDtypeStruct((T, D), BF16),
                   jax.ShapeDtypeStruct((T, D), BF16), jax.ShapeDtypeStruct((D, D), BF16),
                   jax.ShapeDtypeStruct((SUBLANE, D), F32)],
        scratch_shapes=[pltpu.VMEM((D, D), F32)],
        compiler_params=_params(("arbitrary",)),
    )(dx1, z0, out0, y0, ycat, g0, w_out, mod, lnp)


def _mix0_bwd(q, kvx, lse, dyc, ycat, su, sv, sink_l, bias, a128, gsum, sel, sg_lng, sg_lnb, sg_w, sg_bfull,
              rc, rs1, rs2, comm=None):
    T = q.shape[0]
    nb = T // BLK

    def body(q_ref, kp_ref, kc_ref, kn_ref, lse_ref, dyc_ref, yc_ref, su_ref, sv_ref, sink_ref, bias_ref, a_ref,
             gsum_ref, sel_ref, lng_ref, lnb_ref, w_ref, bfull_ref, c_ref, s1_ref, s2_ref,
             dq_ref, dkv_ref, dsu_ref, dsv_ref, dw_ref, dbt_ref, vec_ref, dsink_ref):
        n = pl.program_id(0)

        @pl.when(n == 0)
        def _():
            dkv_ref[...] = jnp.zeros_like(dkv_ref)
            dw_ref[...] = jnp.zeros_like(dw_ref)
            dbt_ref[...] = jnp.zeros_like(dbt_ref)
            vec_ref[...] = jnp.zeros_like(vec_ref)
            dsink_ref[...] = jnp.zeros_like(dsink_ref)

        kvx4 = jnp.concatenate([kp_ref[...], kc_ref[...], kn_ref[...]], axis=0)
        for s in range(2):
            _mix0_bwd_block(s, 2 * n + s, nb, kvx4[s * BLK:s * BLK + 3 * BLK], q_ref, lse_ref, dyc_ref, yc_ref, su_ref,
                            sv_ref, sink_ref, bias_ref, a_ref, gsum_ref, sel_ref, lng_ref, lnb_ref, w_ref, bfull_ref,
                            c_ref, s1_ref, s2_ref, dq_ref, dkv_ref, dsu_ref, dsv_ref, dw_ref, dbt_ref, vec_ref,
                            dsink_ref)

    def _mix0_bwd_block(s, b, nb, kvx, q_ref, lse_ref, dyc_ref, yc_ref, su_ref, sv_ref, sink_ref, bias_ref, a_ref,
                        gsum_ref, sel_ref, lng_ref, lnb_ref, w_ref, bfull_ref, c_ref, s1_ref, s2_ref,
                        dq_ref, dkv_ref, dsu_ref, dsv_ref, dw_ref, dbt_ref, vec_ref, dsink_ref):
        rows = slice(s * BLK, (s + 1) * BLK)

        def tile(ref, t):
            return ref[rows, t * LANE:(t + 1) * LANE]

        band = pl.ds(pl.multiple_of(b * BLK + (TM - BLK), BLK), 3 * BLK)
        bias = _band_bias(bias_ref, b, nb)
        bias2 = jnp.concatenate([bias, bias], axis=1)
        low = lax.broadcasted_iota(jnp.int32, (BLK, LANE), 1) < HEAD_DIM
        low2 = lax.broadcasted_iota(jnp.int32, (2 * BLK, LANE), 1) < HEAD_DIM
        sel = sel_ref[...]
        c, s1, s2 = c_ref[rows, :], s1_ref[rows, :], s2_ref[rows, :]
        for kvh in range(2):
            t0, t1 = 2 * kvh, 2 * kvh + 1
            q2 = jnp.concatenate([tile(q_ref, t0), tile(q_ref, t1)], axis=0)
            do2 = jnp.concatenate([tile(dyc_ref, t0), tile(dyc_ref, t1)], axis=0)
            yc2 = jnp.concatenate([tile(yc_ref, t0), tile(yc_ref, t1)], axis=0)
            p_hi, p_lo = _split_bf16(do2.astype(F32) * yc2.astype(F32))
            deltas = _dot_nt(sel, p_hi) + _dot_nt(sel, p_lo)
            dkx = jnp.zeros((3 * BLK, LANE), F32)
            dvx = jnp.zeros((3 * BLK, LANE), F32)
            dq_acc = None
            for par in range(2):
                heads = (4 * kvh + par, 4 * kvh + 2 + par)
                kt = 2 * kvh + par
                ke = kvx[:, kt * LANE:(kt + 1) * LANE]
                ve = kvx[:, (4 + kt) * LANE:(5 + kt) * LANE]
                lse = jnp.concatenate([lse_ref[s, :, h * LANE:(h + 1) * LANE] for h in heads], axis=1)
                sk = jnp.concatenate([_lane_tile(sink_ref, h) for h in heads], axis=1)
                delta = deltas[par:par + 1, :]
                pt = jnp.exp(_dot_nt(ke, q2) + bias2 - lse)
                dst = (pt * (_dot_nt(ve, do2) - delta)).astype(BF16)
                sink_terms = jnp.exp(sk - lse) * delta
                for k, h in enumerate(heads):
                    dsink_ref[:, h * LANE:(h + 1) * LANE] += sink_terms[:, k * LANE:(k + 1) * LANE]
                part = _dot_tn(dst, ke)
                dq_acc = part if dq_acc is None else dq_acc + part
                mine = low2 if par == 0 else jnp.logical_not(low2)
                dkx = dkx + jnp.dot(dst, jnp.where(mine, q2, jnp.zeros_like(q2)), preferred_element_type=F32)
                dvx = dvx + jnp.dot(pt.astype(BF16), jnp.where(mine, do2, jnp.zeros_like(do2)),
                                    preferred_element_type=F32)
            for k, t in enumerate((t0, t1)):
                dq_t = dq_acc[k * BLK:(k + 1) * BLK] * (HEAD_DIM ** -0.5)
                dq_ref[rows, t * LANE:(t + 1) * LANE] = _rope_bwd(dq_t, c, s1, s2).astype(BF16)
            dkv_ref[band, kvh * LANE:(kvh + 1) * LANE] += dkx
            dkv_ref[band, (2 + kvh) * LANE:(3 + kvh) * LANE] += dvx

        lng = lng_ref[...]
        xhat, rstd, vb, svm = _sg_core(sv_ref.at[rows, :], lng, lnb_ref[...], a_ref, w_ref, bfull_ref)
        dy = dyc_ref[rows, ATTN_W:].astype(F32)
        dsu_ref[rows, :] = (dy * svm).astype(BF16)
        dsvm = dy * su_ref[rows, :].astype(F32)
        d_hi, d_lo = _split_bf16(dsvm)
        gsum = gsum_ref[...]
        dbt_ref[...] += jnp.dot(d_hi, gsum, preferred_element_type=F32) + jnp.dot(d_lo, gsum,
                                                                                 preferred_element_type=F32)
        tiles = []
        for t in range(SG_W // LANE):
            tl = slice(t * LANE, (t + 1) * LANE)
            dt, v2 = d_hi[:, tl], vb[:, tl]
            dw_ref[2 * t] += _dot_nt(jnp.where(low, dt, jnp.zeros_like(dt)), v2)
            dw_ref[2 * t + 1] += _dot_nt(jnp.where(low, jnp.zeros_like(dt), dt), v2)
            tiles.append(jnp.where(low, _dot_tn(w_ref[2 * t], dt), _dot_tn(w_ref[2 * t + 1], dt)))
        dvgn = jnp.concatenate(tiles, axis=-1)
        vec_ref[0:1, :] += _rowsum(dvgn * xhat)
        vec_ref[1:2, :] += _rowsum(dvgn)
        dxh = dvgn * lng
        m1 = _group_mean(dxh, a_ref)
        m2 = _group_mean(dxh * xhat, a_ref)
        dsv_ref[rows, :] = (rstd * (dxh - m1 - xhat * m2)).astype(BF16)

    two = 2 * BLK
    return _fused_call(
        body, comm, (q, kvx, kvx, kvx, lse, dyc, ycat, su, sv, sink_l, bias, a128, gsum, sel, sg_lng, sg_lnb, sg_w,
                     sg_bfull, rc, rs1, rs2),
        name="mix0_bwd", grid=(nb // 2,),
        in_specs=[_tile(two, ATTN_W)] + _band_specs(KVX_W, nb, 2) + [
            pl.BlockSpec((2, 1, N_HEADS * LANE), lambda n: (n, 0, 0)), _tile(two, D), _tile(two, D),
            _tile(two, SG_W), _tile(two, SG_W), _full((1, N_HEADS * LANE)), _full((3 * BLK, LANE)),
            _full((2 * LANE, 2 * LANE)),_full((SG_W, LANE)), _full((SUBLANE, LANE)), _full((1, SG_W)), _full((1, SG_W)),
            _full((SG_GROUPS, BLK, BLK)), _full((BLK, SG_W)), _tile(two, LANE), _tile(two, LANE), _tile(two, LANE)],
        out_specs=[_tile(two, ATTN_W), _full((T + 2 * TM, 4 * LANE)), _tile(two, SG_W), _tile(two, SG_W),
                   _full((SG_GROUPS, BLK, BLK)), _full((BLK, LANE)), _full((SUBLANE, SG_W)),
                   _full((1, N_HEADS * LANE))],
        out_shape=[jax.ShapeDtypeStruct((T, ATTN_W), BF16), jax.ShapeDtypeStruct((T + 2 * TM, 4 * LANE), F32),
                   jax.ShapeDtypeStruct((T, SG_W), BF16), jax.ShapeDtypeStruct((T, SG_W), BF16),
                   jax.ShapeDtypeStruct((SG_GROUPS, BLK, BLK), F32), jax.ShapeDtypeStruct((BLK, LANE), F32),
                   jax.ShapeDtypeStruct((SUBLANE, SG_W), F32), jax.ShapeDtypeStruct((1, N_HEADS * LANE), F32)])


def _ev_in_bwd(dq, dkv, dsu, dsv, dg0, x, dxp, mod, w_in, rc, rs1, rs2, comm=None):
    T = x.shape[0]

    def body(dq_ref, dkv_ref, dsu_ref, dsv_ref, dg_ref, x_ref, dxp_ref, mod_ref, w_ref, c_ref, s1_ref, s2_ref,
             dx_ref, dwb_ref, vec_ref, dw_ref):
        i = pl.program_id(0)

        @pl.when(i == 0)
        def _():
            dw_ref[...] = jnp.zeros_like(dw_ref)
            vec_ref[...] = jnp.zeros_like(vec_ref)

        low = lax.broadcasted_iota(jnp.int32, (TM, LANE), 1) < HEAD_DIM

        def fold(j):
            t0 = dkv_ref[:, (2 * j) * LANE:(2 * j + 1) * LANE]
            t1 = dkv_ref[:, (2 * j + 1) * LANE:(2 * j + 2) * LANE]
            return jnp.where(low, t0 + pltpu.roll(t0, HEAD_DIM, 1), t1 + pltpu.roll(t1, HEAD_DIM, 1))

        dk = _rope_bwd(fold(0), c_ref[...], s1_ref[...], s2_ref[...]).astype(BF16)
        dp = jnp.concatenate([dq_ref[...], dk, fold(1).astype(BF16), dsu_ref[...], dsv_ref[...],
                              dg_ref[...]], axis=-1)
        xv = x_ref[...]
        scale0 = 1.0 + mod_ref[1:2, :]
        h0 = (xv * scale0 + mod_ref[0:1, :]).astype(BF16)
        dh0 = _dot(dp, w_ref[...])
        dw_ref[...] += _dot_tn(dp, h0)
        dx_ref[...] = dxp_ref[...] + dh0 * scale0
        vec_ref[0:1, :] += _rowsum(dh0)
        vec_ref[1:2, :] += _rowsum(dh0 * xv)

        @pl.when(i == T // TM - 1)
        def _():
            dwb_ref[...] = dw_ref[...].astype(BF16)

    t = _tile(TM, D)
    return _fused_call(
        body, comm, (dq, dkv, dsu, dsv, dg0, x, dxp, mod, w_in, rc, rs1, rs2), name="ev_in_bwd", grid=(T // TM,),
        in_specs=[_tile(TM, ATTN_W), pl.BlockSpec((TM, 4 * LANE), lambda i: (i + 1, 0)), _tile(TM, SG_W),
                  _tile(TM, SG_W), t, t, t,
                  _full((3, D)), _full((EV_IN, D)), _tile(TM, LANE), _tile(TM, LANE), _tile(TM, LANE)],
        out_specs=[t, _full((EV_IN, D)), _full((SUBLANE, D))],
        out_shape=[jax.ShapeDtypeStruct((T, D), F32), jax.ShapeDtypeStruct((EV_IN, D), BF16),
                   jax.ShapeDtypeStruct((SUBLANE, D), F32)],
        scratch_shapes=[pltpu.VMEM((EV_IN, D), F32)])


def _sum_slots(land_ref):
    g = land_ref[0].astype(F32)
    for i in range(1, land_ref.shape[0]):
        g = g + land_ref[i].astype(F32)
    return g


def _reduce_adam(items, name, after=()):
    R, C = items[0][1].shape
    rb = R
    if R > 512:
        for cand in (512, 256, 128, 64, 32, 16, 8):
            if R % cand == 0:
                rb = cand
                break
    n = len(items)

    def body(*refs):
        for k in range(n):
            l_ref, w_ref, m_ref, v_ref = refs[4 * k:4 * k + 4]
            first_out = 4 * n + len(after)
            g_ref, d_ref, nm_ref, nv_ref = refs[first_out + 4 * k:first_out + 4 * k + 4]
            g = _sum_slots(l_ref)
            g_ref[...] = g
            dlt, m2, v2 = _adam(w_ref[...], g, m_ref[...], v_ref[...])
            d_ref[...] = dlt
            nm_ref[...] = m2
            nv_ref[...] = v2

    t = pl.BlockSpec((rb, C), lambda i: (i, 0))
    shp = jax.ShapeDtypeStruct((R, C), F32)
    in_specs, operands = [], []
    for land, w, m, v in items:
        in_specs += [pl.BlockSpec((land.shape[0], rb, C), lambda i: (0, i, 0)), t, t, t]
        operands += [land, w, m, v]
    res = _pallas(
        body, name=name, grid=(R // rb,),
        in_specs=in_specs + [pl.BlockSpec(memory_space=pl.ANY)] * len(after),
        out_specs=[t] * (4 * n), out_shape=[shp] * (4 * n),
        compiler_params=_params(("parallel",)),
    )(*operands, *after)
    return [list(res[4 * k:4 * k + 4]) for k in range(n)]


def _tail_stage1(slabs, small):
    _, R, C = slabs.shape
    n_chips = N_DEV // 2
    gather = _GatherComm(small)
    ns = gather.n

    def body(*refs):
        slab_ref = refs[0]
        g_ins = refs[1:1 + ns]
        part, land_ref = refs[1 + ns], refs[2 + ns]
        g_outs = refs[3 + ns:3 + 2 * ns]
        stage, s1_send, s1_recv = refs[3 + 2 * ns:6 + 2 * ns]
        g_sems = refs[6 + 2 * ns:]
        x, y, c = _my_pos()
        chip = 2 * x + y
        gather.start(g_ins, g_outs, g_sems)
        swaps = [pltpu.make_async_remote_copy(
            src_ref=slab_ref.at[2 * k + (1 - c)], dst_ref=stage.at[k], send_sem=s1_send.at[k],
            recv_sem=s1_recv.at[k], device_id=(x, y, 1 - c), device_id_type=MESH) for k in range(n_chips)]
        for cp in swaps:
            cp.start()
        for cp in swaps:
            cp.wait()
        for k in range(n_chips):
            part[k] = (slab_ref[2 * k + c].astype(F32) + stage[k].astype(F32)).astype(BF16)
        land_ref[chip] = part[chip]
        gather.mid(g_ins, g_outs, g_sems)
        gather.finish(g_ins, g_outs, g_sems)

    any_spec = pl.BlockSpec(memory_space=pl.ANY)
    vmem_spec = pl.BlockSpec(memory_space=pltpu.VMEM)
    slab4 = jax.ShapeDtypeStruct((n_chips, R, C), BF16)
    res = _pallas(
        body, name="tail_stage1",
        out_shape=[slab4, slab4] + gather.out_shapes(),
        in_specs=[vmem_spec] + [any_spec] * ns, out_specs=[vmem_spec, vmem_spec] + [any_spec] * ns,
        scratch_shapes=[pltpu.VMEM((n_chips, R, C), BF16),
                        pltpu.SemaphoreType.DMA((n_chips,)), pltpu.SemaphoreType.DMA((n_chips,))] + gather.sems(),
        compiler_params=pltpu.CompilerParams(vmem_limit_bytes=VMEM_LIMIT),
    )(slabs, *gather.arrs)
    return res[0], res[1], list(res[2:])


def _chip_copies(part_ref, land_ref, send_sems, recv_sems):
    x, y, c = _my_pos()
    chip = 2 * x + y
    copies = []
    for r in range(1, N_DEV // 2):
        px = (1 - x) if (r & 2) else x
        py = (1 - y) if (r & 1) else y
        copies.append(pltpu.make_async_remote_copy(
            src_ref=part_ref.at[2 * px + py], dst_ref=land_ref.at[chip], send_sem=send_sems[r - 1],
            recv_sem=recv_sems[r - 1], device_id=(px, py, c), device_id_type=MESH))
    return copies


def _tail_send(part, land):
    n = N_DEV // 2 - 1

    def body(part_ref, land_ref, *outs):
        send_sems, recv_sems = outs[:n], outs[n:2 * n]
        token = outs[2 * n + 2]
        for cp in _chip_copies(part_ref, land_ref, send_sems, recv_sems):
            cp.start()
        token[...] = jnp.zeros_like(token)

    hbm = pl.BlockSpec(memory_space=pltpu.HBM)
    sem = pl.BlockSpec(memory_space=pltpu.SEMAPHORE)
    res = _pallas(
        body, name="tail_send",
        out_shape=tuple([pltpu.SemaphoreType.DMA(())] * (2 * n)
                        + [pltpu.HBM(part.shape, part.dtype), pltpu.HBM(land.shape, land.dtype),
                           jax.ShapeDtypeStruct((SUBLANE, LANE), F32)]),
        in_specs=(hbm, hbm), out_specs=tuple([sem] * (2 * n) + [hbm, hbm, pl.BlockSpec(memory_space=pltpu.VMEM)]),
        input_output_aliases={0: 2 * n, 1: 2 * n + 1},
        compiler_params=pltpu.CompilerParams(has_side_effects=pltpu.SideEffectType.DATAFLOW_SIDE_EFFECTING),
    )(pltpu.with_memory_space_constraint(part, pltpu.HBM), pltpu.with_memory_space_constraint(land, pltpu.HBM))
    return list(res[:n]), list(res[n:2 * n]), res[2 * n], res[2 * n + 1], res[2 * n + 2]


def _tail_wait(send_sems, recv_sems, part, land, after):
    n = len(send_sems)

    def body(part_ref, land_ref, *rest):
        ss, rs = rest[:n], rest[n:2 * n]
        for cp in _chip_copies(part_ref, land_ref, ss, rs):
            cp.wait_send()
            cp.wait_recv()

    hbm = pl.BlockSpec(memory_space=pltpu.HBM)
    sem = pl.BlockSpec(memory_space=pltpu.SEMAPHORE)
    any_spec = pl.BlockSpec(memory_space=pl.ANY)
    res = _pallas(
        body, name="tail_wait",
        out_shape=(pltpu.HBM(part.shape, part.dtype), pltpu.HBM(land.shape, land.dtype)),
        in_specs=tuple([hbm, hbm] + [sem] * (2 * n) + [any_spec] * len(after)), out_specs=(hbm, hbm),
        input_output_aliases={0: 0, 1: 1},
        compiler_params=pltpu.CompilerParams(has_side_effects=pltpu.SideEffectType.DATAFLOW_SIDE_EFFECTING),
    )(part, land, *send_sems, *recv_sems, *after)
    return res[1]


def _slots_adam(items, name, after=()):
    zeros3 = (0, 0, 0)
    in_specs, out_specs, out_shape, operands = [], [], [], []
    for land, w, m, v in items:
        inner = w.shape[-3:]
        if w.ndim == 5:
            lspec = pl.BlockSpec((N_DEV, 1) + inner, lambda i: (0, i) + zeros3)
            wspec = pl.BlockSpec((1, 1) + inner, lambda i: (0, i) + zeros3)
        else:
            lspec = pl.BlockSpec((N_DEV,) + inner, lambda i: (0,) + zeros3)
            wspec = pl.BlockSpec((1,) + inner, lambda i: (0,) + zeros3)
        in_specs += [lspec, wspec, wspec, wspec]
        out_specs += [wspec] * 4
        out_shape += [jax.ShapeDtypeStruct(w.shape, F32)] * 4
        operands += [land, w, m, v]
    n = len(items)

    def body(*refs):
        for k, (_, w, _, _) in enumerate(items):
            l_ref, w_ref, m_ref, v_ref = refs[4 * k:4 * k + 4]
            first_out = 4 * n + len(after)
            outs = refs[first_out + 4 * k:first_out + 4 * k + 4]
            at = (0, 0) if w.ndim == 5 else (0,)

            def update(l_ref=l_ref, w_ref=w_ref, m_ref=m_ref, v_ref=v_ref, outs=outs, at=at):
                g = l_ref[(0,) + at[1:]].astype(F32)
                for i in range(1, N_DEV):
                    g = g + l_ref[(i,) + at[1:]].astype(F32)
                dlt, m2, v2 = _adam(w_ref[at], g, m_ref[at], v_ref[at])
                for o_ref, val in zip(outs, (g, dlt, m2, v2)):
                    o_ref[at] = val

            if w.ndim == 5:
                update()
            else:
                pl.when(pl.program_id(0) == 0)(update)

    res = _pallas(
        body, name=name, grid=(2,),
        in_specs=in_specs + [pl.BlockSpec(memory_space=pl.ANY)] * len(after),
        out_specs=out_specs, out_shape=out_shape,
        compiler_params=_params(("arbitrary",)),
    )(*operands, *after)
    return [list(res[4 * k:4 * k + 4]) for k in range(n)]


SMALL_PARAMS = ("ln_g", "ln_b", "ev_sg_ln_g", "ev_sg_ln_b", "ev_sink", "ev_sg_b",
                "od_conv_w", "od_conv_b", "od_b_a", "od_b_x", "od_lam")


def _small_update(ga, gc, gd, gf, gb, ge, gsink, gbt, params):
    names = list(SMALL_PARAMS)
    flat = [a for nm in names for a in params[nm]]
    n_g = 8

    def body(*refs):
        ga_ref, gc_ref, gd_ref, gf_ref, gb_ref, ge_ref, gs_ref, gbt_ref = refs[:n_g]
        prm = refs[n_g:n_g + 3 * len(names)]
        loss_ref = refs[n_g + 3 * len(names)]
        outs = refs[n_g + 3 * len(names) + 1:]

        def ssum(ref):
            acc = ref[0]
            for i in range(1, N_DEV):
                acc = acc + ref[i]
            return acc

        a, cc, dd, ff, bb, ee = ssum(ga_ref), ssum(gc_ref), ssum(gd_ref), ssum(gf_ref), ssum(gb_ref), ssum(ge_ref)
        loss_ref[...] = a[3:4, 0:LANE]
        me = _slot(*_my_pos())

        def mine(rows):
            acc = jnp.zeros((rows.shape[0], LANE), F32)
            for j in range(N_DEV):
                acc = acc + jnp.where(me == j, rows[:, j * LANE:(j + 1) * LANE], 0.0)
            return acc

        sink_terms = ssum(gs_ref)
        lane8 = lax.broadcasted_iota(jnp.int32, (1, N_HEADS), 1)
        g_sink = jnp.zeros((1, N_HEADS), F32)
        for h in range(N_HEADS):
            tot = -jnp.sum(sink_terms[:, h * LANE:(h + 1) * LANE], axis=1, keepdims=True)
            g_sink = jnp.where(lane8 == h, tot, g_sink)
        grads = dict(
            ln_g=jnp.concatenate([dd[0:1], a[0:1]], axis=0), ln_b=jnp.concatenate([dd[1:2], a[1:2]], axis=0),
            ev_sg_ln_g=ee[0:1], ev_sg_ln_b=ee[1:2], ev_sink=g_sink,
            ev_sg_b=jnp.transpose(ssum(gbt_ref))[0:SG_GROUPS, :],
            od_conv_w=mine(cc[0:4]), od_conv_b=mine(cc[4:5]),
            od_b_a=mine(jnp.concatenate([ff[0:1], bb[0:1]], axis=0)),
            od_b_x=mine(jnp.concatenate([ff[1:2], bb[1:2]], axis=0)),
            od_lam=mine(jnp.concatenate([ff[2:3], bb[2:3]], axis=0)))
        for k, nm in enumerate(names):
            w_ref, m_ref, v_ref = prm[3 * k:3 * k + 3]
            at = (0,) if len(w_ref.shape) == 3 else ()
            g = grads[nm]
            dlt, m2, v2 = _adam(w_ref[at] if at else w_ref[...], g, m_ref[at] if at else m_ref[...],
                                v_ref[at] if at else v_ref[...])
            for o_ref, val in zip(outs[4 * k:4 * k + 4], (g, dlt, m2, v2)):
                if at:
                    o_ref[at] = val
                else:
                    o_ref[...] = val

    gathered = [ga, gc, gd, gf, gb, ge, gsink, gbt]
    out_shape = [jax.ShapeDtypeStruct((1, LANE), F32)]
    for nm in names:
        out_shape += [jax.ShapeDtypeStruct(params[nm][0].shape, F32)] * 4
    return _pallas(
        body, name="small_update", grid=(1,),
        in_specs=[_full(a.shape) for a in gathered + flat],
        out_specs=[_full(s.shape) for s in out_shape], out_shape=out_shape,
        compiler_params=_params(("arbitrary",)),
    )(*gathered, *flat)


VEC_ROWS = 16
VEC_LAYOUT = (("od_conv_w", 4), ("od_conv_b", 1), ("od_b_a", 2), ("od_b_x", 2), ("od_lam", 2))


def _from_slabs(slabs):
    n, R, cp = slabs.shape
    return slabs.transpose(1, 0, 2).reshape(R, n * cp)


def kernel(x, c, positions, ada_w, ada_b, ln_g, ln_b, ev_w_in, ev_w_out, ev_sink, ev_sg_ln_g, ev_sg_ln_b, ev_sg_w, ev_sg_b, od_w_in, od_conv_w, od_conv_b, od_w_a, od_b_a, od_w_x, od_b_x, od_lam, od_w_out, loss_target, m_ada_w, m_ada_b, m_ln_g, m_ln_b, m_ev_w_in, m_ev_w_out, m_ev_sink, m_ev_sg_ln_g, m_ev_sg_ln_b, m_ev_sg_w, m_ev_sg_b, m_od_w_in, m_od_conv_w, m_od_conv_b, m_od_w_a, m_od_b_a, m_od_w_x, m_od_b_x, m_od_lam, m_od_w_out, v_ada_w, v_ada_b, v_ln_g, v_ln_b, v_ev_w_in, v_ev_w_out, v_ev_sink, v_ev_sg_ln_g, v_ev_sg_ln_b, v_ev_sg_w, v_ev_sg_b, v_od_w_in, v_od_conv_w, v_od_conv_b, v_od_w_a, v_od_b_a, v_od_w_x, v_od_b_x, v_od_lam, v_od_w_out):
    T = x.shape[1]
    me = _slot(*_my_pos())
    xs = x.reshape(T, D)
    tgt = loss_target.reshape(T, D)

    c_all, mod_all, g_vec, (g_ev_in,), (s_ev_out, s_od_in, s_od_out, sg_w, wa, wx) = _head_gather(
        c, ada_w, [ev_w_in[0].T.astype(BF16)],
        [ev_w_out[0], od_w_in[0], od_w_out[0], ev_sg_w[0], od_w_a[0], od_w_x[0]],
        [od_conv_w, od_conv_b, od_b_a, od_b_x, od_lam])
    c_all = c_all.reshape(N_DEV, D)
    w_ev_in = g_ev_in.reshape(EV_IN, D)
    vec_full = _from_slabs(g_vec)
    cw, cb = vec_full[0:4], vec_full[4:5]
    ba, bx, lam = vec_full[5:7], vec_full[7:9], vec_full[9:11]
    mod_mine = lax.dynamic_index_in_dim(mod_all, me, axis=2, keepdims=False)
    mod = mod_mine.transpose(1, 0, 2).reshape(2, 3 * D) + ada_b
    mod0 = mod[0].reshape(3, D)
    mod1 = mod[1].reshape(3, D)

    half = 8
    inv_freq = jnp.power(jnp.float32(ROPE_THETA), -jnp.arange(half, dtype=F32) / half)
    ang = positions.reshape(T).astype(F32)[:, None] * inv_freq
    cos_t = jnp.tile(jnp.cos(ang), (1, LANE // half))
    sin_t = jnp.tile(jnp.sin(ang), (1, LANE // half))
    l64 = jnp.arange(LANE) % HEAD_DIM
    rc = jnp.where(l64 < 2 * half, cos_t, 1.0)
    rs1 = jnp.where(l64 < half, -sin_t, 0.0)
    rs2 = jnp.where((l64 >= half) & (l64 < 2 * half), sin_t, 0.0)

    ln0 = jnp.stack([ln_g[0], ln_b[0]])
    ln1 = jnp.stack([ln_g[1], ln_b[1]])
    sg_lng = ev_sg_ln_g
    sg_lnb = ev_sg_ln_b
    sg_bfull = jnp.repeat(ev_sg_b[0].T, SG_DIM, axis=1)
    sink_l = jnp.repeat(ev_sink, LANE, axis=1)
    kj = jnp.arange(3 * BLK)[:, None]
    qi = jnp.arange(BLK)[None, :]
    band_bias = jnp.where(jnp.abs(kj - BLK - qi) <= BLK, 0.0, NEG_INF).astype(F32)
    lanes = jnp.arange(LANE)
    lanes2 = jnp.arange(2 * LANE)
    a128 = jnp.where(lanes2[:, None] // SG_DIM == lanes2[None, :] // SG_DIM, 1.0 / SG_DIM, 0.0).astype(BF16)
    gsum = (jnp.arange(SG_W)[:, None] // SG_DIM == lanes[None, :]).astype(BF16)
    sel = (jnp.arange(SUBLANE)[:, None] == lanes[None, :] // HEAD_DIM).astype(BF16)

    (q, kvx, su, sv, g0), (g_ev_out, g_od_out) = _ev_in(
        xs, mod0, w_ev_in, rc, rs1, rs2, _GatherComm([s_ev_out, s_od_out], mid_frac=0.75))
    (ycat, y0, lse), (g_od_in,) = _mix0_fwd(
        q, kvx, su, sv, g0, sink_l, band_bias, a128, sg_lng, sg_lnb, sg_w, sg_bfull,
        _GatherComm([s_od_in], mid_frac=0.75))
    w_ev_out = g_ev_out.reshape(D, D)
    w_od_in = _from_slabs(g_od_in)
    w_od_out = g_od_out.reshape(D, D)
    out0, z0, x1 = _ev_out(y0, w_ev_out, xs, mod0, ln0)
    xr, g1 = _od_in(x1, mod1, w_od_in)
    fwd_f = _rglru_fwd(xr, cw, cb, wa[0], wx[0], ba[0:1], bx[0:1], lam[0:1], False, "rglru_fwd_f")
    fwd_b = _rglru_fwd(xr, cw, cb, wa[1], wx[1], ba[1:2], bx[1:2], lam[1:2], True, "rglru_fwd_b")
    dh, dg1, dx1p, d_od_out, vec_a = _od_out(fwd_f[0], fwd_b[0], g1, w_od_out, x1, tgt, mod1, ln1)

    (dxcf, dwa_f, dwx_f, vec_f), (l_od_out,) = _rglru_bwd(
        fwd_f, dh, wa[0], wx[0], lam[0:1], False, "rglru_bwd_f",
        _ExchangeComm([d_od_out.reshape(N_DEV, D // N_DEV, D)]))
    (dxcb, dwa_b, dwx_b, vec_b), _ = _rglru_bwd(fwd_b, dh, wa[1], wx[1], lam[1:2], True, "rglru_bwd_b")
    (dx1, d_od_in, vec_c), (a_wa, a_wx) = _od_in_bwd(
        dxcf, dxcb, xr, dg1, x1, dx1p, mod1, w_od_in, cw,
        _GatherComm([jnp.stack([dwa_f, dwa_b]).astype(BF16), jnp.stack([dwx_f, dwx_b]).astype(BF16)],
                    mid_frac=0.75))
    dxp, dyc, dg0, d_ev_out, vec_d = _ev_out_bwd(dx1, z0, out0, y0, ycat, g0, w_ev_out, mod0, ln0)
    (dq, dkv, dsu, dsv, d_sg_w, d_sg_bt, vec_e, d_sink_l), (l_od_in, l_ev_out, ga, gc, gd, gf, gb) = _mix0_bwd(
        q, kvx, lse, dyc, ycat, su, sv, sink_l, band_bias, a128, gsum, sel, sg_lng, sg_lnb, sg_w, sg_bfull,
        rc, rs1, rs2, _BothComm(_ExchangeComm([d_od_in, d_ev_out.reshape(N_DEV, D // N_DEV, D)]),
                                _GatherComm([vec_a, vec_c, vec_d, vec_f, vec_b], mid_frac=0.9)))
    (grad_x, d_ev_in, vec_g), _ = _ev_in_bwd(dq, dkv, dsu, dsv, dg0, xs, dxp, mod0, w_ev_in, rc, rs1, rs2)

    part, land, (gg, ge, gsink, gbt, a_sgw) = _tail_stage1(
        d_ev_in.reshape(N_DEV, EV_IN // N_DEV, D), [vec_g, vec_e, d_sink_l, d_sg_bt, d_sg_w.astype(BF16)])
    send_sems, recv_sems, part, land, token = _tail_send(part, land)

    dmod_all = jnp.stack([jnp.concatenate([gg[:, 0], gg[:, 1], gd[:, 2]], axis=-1),
                          jnp.concatenate([gc[:, 5], gc[:, 6], ga[:, 2]], axis=-1)], axis=1)
    cols = ada_w.shape[2]
    dmod_cols = lax.dynamic_slice_in_dim(dmod_all, me * cols, cols, axis=2).transpose(1, 0, 2)
    (g_ada_w, d_ada_w, nm_ada_w, nv_ada_w, g_ada_b, d_ada_b, nm_ada_b, nv_ada_b) = _ada_update(
        c_all, dmod_cols, dmod_all, ada_w, m_ada_w, v_ada_w, ada_b, m_ada_b, v_ada_b)

    res = dict(ada_w=[g_ada_w, d_ada_w, nm_ada_w, nv_ada_w], ada_b=[g_ada_b, d_ada_b, nm_ada_b, nv_ada_b])
    (r_od_in,) = _reduce_adam([(l_od_in, od_w_in[0], m_od_w_in[0], v_od_w_in[0])], "adam_od_w_in", after=[token])
    r_ev_out, r_od_out = _reduce_adam([(l_ev_out, ev_w_out[0], m_ev_w_out[0], v_ev_w_out[0]),
                                       (l_od_out, od_w_out[0], m_od_w_out[0], v_od_w_out[0])], "adam_w_out",
                                      after=[token])
    for name, r in (("od_w_in", r_od_in), ("ev_w_out", r_ev_out), ("od_w_out", r_od_out)):
        res[name] = [a[None] for a in r]
    res["od_w_a"], res["od_w_x"], res["ev_sg_w"] = _slots_adam(
        [(a_wa, od_w_a, m_od_w_a, v_od_w_a), (a_wx, od_w_x, m_od_w_x, v_od_w_x),
         (a_sgw, ev_sg_w, m_ev_sg_w, v_ev_sg_w)], "adam_gates", after=[token])
    small = dict(ln_g=(ln_g, m_ln_g, v_ln_g), ln_b=(ln_b, m_ln_b, v_ln_b),
                 ev_sg_ln_g=(ev_sg_ln_g, m_ev_sg_ln_g, v_ev_sg_ln_g),
                 ev_sg_ln_b=(ev_sg_ln_b, m_ev_sg_ln_b, v_ev_sg_ln_b),
                 ev_sink=(ev_sink, m_ev_sink, v_ev_sink), ev_sg_b=(ev_sg_b, m_ev_sg_b, v_ev_sg_b),
                 od_conv_w=(od_conv_w, m_od_conv_w, v_od_conv_w), od_conv_b=(od_conv_b, m_od_conv_b, v_od_conv_b),
                 od_b_a=(od_b_a, m_od_b_a, v_od_b_a), od_b_x=(od_b_x, m_od_b_x, v_od_b_x),
                 od_lam=(od_lam, m_od_lam, v_od_lam))
    small_out = _small_update(ga, gc, gd, gf, gb, ge, gsink, gbt, small)
    l_ev_in = _tail_wait(send_sems, recv_sems, part, land,
                         [r_od_in[0], r_od_out[0], res["od_w_x"][0], g_ada_w, small_out[0]])
    (r_ev_in,) = _reduce_adam([(l_ev_in, ev_w_in[0].T, m_ev_w_in[0].T, v_ev_w_in[0].T)], "adam_ev_w_in")
    res["ev_w_in"] = [a.T[None] for a in r_ev_in]
    loss = small_out[0][0, 0]
    for k, name in enumerate(SMALL_PARAMS):
        res[name] = small_out[1 + 4 * k:5 + 4 * k]

    order = ["ada_w", "ada_b", "ln_g", "ln_b", "ev_w_in", "ev_w_out", "ev_sink", "ev_sg_ln_g", "ev_sg_ln_b",
             "ev_sg_w", "ev_sg_b", "od_w_in", "od_conv_w", "od_conv_b", "od_w_a", "od_b_a", "od_w_x", "od_b_x",
             "od_lam", "od_w_out"]
    outs = [loss, grad_x.reshape(1, T, D)]
    for kind in range(4):
        outs += [res[name][kind] for name in order]
    return tuple(outs)
```

```python
import jax
import jax.numpy as jnp
from jax import lax
from jax.experimental import pallas as pl
from jax.experimental.pallas import tpu as pltpu

F32 = jnp.float32
BF16 = jnp.bfloat16

N_DEV = 8
D = 1024
N_HEADS = 8
HEAD_DIM = 64
ATTN_W = 512
SG_W = 512
SG_GROUPS = 8
SG_DIM = 64
BLK = 128
KVX_W = 1024
EV_IN = 2816
OD_IN = 2048
RNN_HEADS = 8
RNN_HD = 128
ALPHA = 4.0 ** 0.25
LN_EPS = 1e-5
NEG_INF = -1e30
RG_C = 8.0
ROPE_THETA = 500000.0
LR, B1, B2, EPS, WD, STEP = 0.001, 0.9, 0.999, 1e-08, 0.01, 10

LANE = 128
SUBLANE = 8
TM = 256
TMF = 512
TMO = 512
TS = 256
FWD_BLOCKS = 4
BWD_BLOCKS = 4
VMEM_LIMIT = 56 * 1024 * 1024

MESH = pl.DeviceIdType.MESH


def _pallas(body, **kw):
    return pl.pallas_call(body, **kw)


def _params(sem, vmem=VMEM_LIMIT):
    return pltpu.CompilerParams(dimension_semantics=sem, vmem_limit_bytes=vmem)


def _sigmoid(x):
    return 0.5 * jnp.tanh(0.5 * x) + 0.5


def _silu_and_grad(x):
    s = _sigmoid(x)
    return x * s, s * (1.0 + x * (1.0 - s))


def _dot(a, b):
    return jnp.dot(a.astype(BF16), b.astype(BF16), preferred_element_type=F32)


def _dot_nt(a, b):
    return lax.dot_general(a.astype(BF16), b.astype(BF16), (((1,), (1,)), ((), ())), preferred_element_type=F32)


def _dot_tn(a, b):
    return lax.dot_general(a.astype(BF16), b.astype(BF16), (((0,), (0,)), ((), ())), preferred_element_type=F32)


def _ln_fwd(z, g, b):
    mu = jnp.mean(z, axis=-1, keepdims=True)
    zc = z - mu
    var = jnp.mean(zc * zc, axis=-1, keepdims=True)
    rstd = lax.rsqrt(var + LN_EPS)
    xhat = zc * rstd
    return xhat * g + b, xhat, rstd


def _ln_bwd(dy, xhat, rstd, g):
    dxh = dy * g
    m1 = jnp.mean(dxh, axis=-1, keepdims=True)
    m2 = jnp.mean(dxh * xhat, axis=-1, keepdims=True)
    return rstd * (dxh - m1 - xhat * m2)


def _rowsum(v):
    return jnp.sum(v, axis=0, keepdims=True)


def _rope_fwd(t, c, s1, s2):
    return t * c + pltpu.roll(t, LANE - 8, 1) * s1 + pltpu.roll(t, 8, 1) * s2


def _rope_bwd(d, c, s1, s2):
    return d * c + pltpu.roll(d * s1, 8, 1) + pltpu.roll(d * s2, LANE - 8, 1)


def _adam(w, g, m, v):
    m2 = B1 * m + (1.0 - B1) * g
    v2 = B2 * v + (1.0 - B2) * (g * g)
    m_hat = m2 / (1.0 - B1 ** STEP)
    v_hat = v2 / (1.0 - B2 ** STEP)
    delta = -LR * (m_hat / (jnp.sqrt(v_hat) + EPS) + WD * w)
    return delta, m2, v2


def _tile(rows, width):
    return pl.BlockSpec((rows, width), lambda i: (i, 0))


def _full(shape):
    zeros = (0,) * len(shape)
    return pl.BlockSpec(shape, lambda i: zeros)


def _rev_tile(rows, width, n, reverse):
    if reverse:
        return pl.BlockSpec((rows, width), lambda i: (n - 1 - i, 0))
    return pl.BlockSpec((rows, width), lambda i: (i, 0))


def _halo_specs(rows, width, n, total_rows, reverse):
    per = rows // SUBLANE
    last = total_rows // SUBLANE - 1

    def tile_of(i):
        return (n - 1 - i) if reverse else i

    prev = pl.BlockSpec((SUBLANE, width), lambda i: (jnp.maximum(tile_of(i) * per - 1, 0), 0))
    nxt = pl.BlockSpec((SUBLANE, width), lambda i: (jnp.minimum((tile_of(i) + 1) * per, last), 0))
    return prev, nxt


def _my_pos():
    return lax.axis_index("x"), lax.axis_index("y"), lax.axis_index("c")


def _slot(px, py, pc):
    return 4 * px + 2 * py + pc


class _GatherComm:
    has_mid = True

    def __init__(self, arrs, mid_frac=0.5):
        self.arrs = list(arrs)
        self.n = len(self.arrs)
        self.mid_frac = mid_frac

    def out_shapes(self):
        return [jax.ShapeDtypeStruct((N_DEV,) + a.shape, a.dtype) for a in self.arrs]

    def sems(self):
        return [pltpu.SemaphoreType.DMA((7 * self.n,)), pltpu.SemaphoreType.DMA((7 * self.n,)),
                pltpu.SemaphoreType.DMA((self.n,))]

    def _parts(self, ins, outs, sems):
        send_sems, recv_sems, local_sems = sems
        x, y, c = _my_pos()
        me, sibling = (x, y, c), (x, y, 1 - c)
        chips = [(1 - x, y), (x, 1 - y), (1 - x, 1 - y)]

        def copy(a, k, block, to, src=None):
            dst = outs[a].at[_slot(*block)]
            return pltpu.make_async_remote_copy(
                src_ref=dst if src is None else src, dst_ref=dst,
                send_sem=send_sems.at[a * 7 + k], recv_sem=recv_sems.at[a * 7 + k],
                device_id=to, device_id_type=MESH)

        local = [pltpu.make_async_copy(ins[a], outs[a].at[_slot(*me)], local_sems.at[a]) for a in range(self.n)]
        first = []
        for a in range(self.n):
            first.append(copy(a, 0, me, sibling, src=ins[a]))
            first += [copy(a, 1 + j, me, (*chip, c), src=ins[a]) for j, chip in enumerate(chips)]
        ici_in = [copy(a, 1 + j, (*chip, c), me) for j, chip in enumerate(chips) for a in range(self.n)]
        passed = [copy(a, 4 + j, (*chip, c), sibling) for j, chip in enumerate(chips) for a in range(self.n)]
        d2d_in = []
        for a in range(self.n):
            d2d_in.append(copy(a, 0, sibling, me))
            d2d_in += [copy(a, 4 + j, (*chip, 1 - c), me) for j, chip in enumerate(chips)]
        return local, first, ici_in, passed, d2d_in

    def start(self, ins, outs, sems):
        local, first, _, _, _ = self._parts(ins, outs, sems)
        for cp in local + first:
            cp.start()

    def mid(self, ins, outs, sems):
        _, _, ici_in, passed, _ = self._parts(ins, outs, sems)
        for arrived, fw in zip(ici_in, passed):
            arrived.wait_recv()
            fw.start()

    def finish(self, ins, outs, sems):
        local, first, _, passed, d2d_in = self._parts(ins, outs, sems)
        for cp in d2d_in:
            cp.wait_recv()
        for cp in first + passed:
            cp.wait_send()
        for cp in local:
            cp.wait()


class _ExchangeComm:
    has_mid = False

    def __init__(self, arrs):
        self.arrs = list(arrs)
        self.n = len(self.arrs)

    def out_shapes(self):
        return [jax.ShapeDtypeStruct(a.shape, a.dtype) for a in self.arrs]

    def sems(self):
        return [pltpu.SemaphoreType.DMA((7 * self.n,)), pltpu.SemaphoreType.DMA((7 * self.n,)),
                pltpu.SemaphoreType.DMA((self.n,))]

    def _copies(self, ins, outs, sems):
        send_sems, recv_sems, local_sems = sems
        x, y, c = _my_pos()
        mine = _slot(x, y, c)
        copies = [pltpu.make_async_copy(ins[a].at[mine], outs[a].at[mine], local_sems.at[a]) for a in range(self.n)]
        for k in range(1, N_DEV):
            px = (1 - x) if (k & 4) else x
            py = (1 - y) if (k & 2) else y
            pc = (1 - c) if (k & 1) else c
            for a in range(self.n):
                copies.append(pltpu.make_async_remote_copy(
                    src_ref=ins[a].at[_slot(px, py, pc)], dst_ref=outs[a].at[mine],
                    send_sem=send_sems.at[a * 7 + k - 1], recv_sem=recv_sems.at[a * 7 + k - 1],
                    device_id=(px, py, pc), device_id_type=MESH))
        return copies

    def start(self, ins, outs, sems):
        for cp in self._copies(ins, outs, sems):
            cp.start()

    def finish(self, ins, outs, sems):
        for cp in self._copies(ins, outs, sems):
            cp.wait()


class _BothComm:
    has_mid = True

    def __init__(self, first, second):
        self.parts = (first, second)
        self.arrs = first.arrs + second.arrs
        self.n = first.n + second.n
        self.mid_frac = second.mid_frac

    def out_shapes(self):
        return self.parts[0].out_shapes() + self.parts[1].out_shapes()

    def sems(self):
        return self.parts[0].sems() + self.parts[1].sems()

    def _each(self, ins, outs, sems):
        a, b = self.parts
        return ((a, ins[:a.n], outs[:a.n], sems[:3]), (b, ins[a.n:], outs[a.n:], sems[3:]))

    def start(self, ins, outs, sems):
        for cm, i_, o_, s_ in self._each(ins, outs, sems):
            cm.start(i_, o_, s_)

    def mid(self, ins, outs, sems):
        for cm, i_, o_, s_ in self._each(ins, outs, sems):
            if cm.has_mid:
                cm.mid(i_, o_, s_)

    def finish(self, ins, outs, sems):
        for cm, i_, o_, s_ in self._each(ins, outs, sems):
            cm.finish(i_, o_, s_)


def _fused_call(body, comm, operands, *, name, grid, in_specs, out_specs, out_shape, scratch_shapes=(),
                semantics=("arbitrary",)):
    n_in, n_out, n_scr = len(in_specs), len(out_specs), len(scratch_shapes)
    if comm is None:
        res = _pallas(body, name=name, grid=grid, in_specs=list(in_specs), out_specs=list(out_specs),
                      out_shape=list(out_shape), scratch_shapes=list(scratch_shapes),
                      compiler_params=_params(semantics))(*operands)
        return list(res), []
    k = comm.n
    steps = grid[0]

    def wrapped(*refs):
        ins, cins = refs[:n_in], refs[n_in:n_in + k]
        outs = refs[n_in + k:n_in + k + n_out]
        couts = refs[n_in + k + n_out:n_in + 2 * k + n_out]
        rest = refs[n_in + 2 * k + n_out:]
        scratch, sems = rest[:n_scr], rest[n_scr:]
        i = pl.program_id(0)

        @pl.when(i == 0)
        def _():
            comm.start(cins, couts, sems)

        body(*ins, *outs, *scratch)

        if comm.has_mid:
            @pl.when(i == int(steps * comm.mid_frac))
            def _():
                comm.mid(cins, couts, sems)

        @pl.when(i == steps - 1)
        def _():
            comm.finish(cins, couts, sems)

    any_spec = pl.BlockSpec(memory_space=pl.ANY)
    res = _pallas(wrapped, name=name, grid=grid, in_specs=list(in_specs) + [any_spec] * k,
                  out_specs=list(out_specs) + [any_spec] * k, out_shape=list(out_shape) + comm.out_shapes(),
                  scratch_shapes=list(scratch_shapes) + comm.sems(),
                  compiler_params=_params(("arbitrary",)))(*operands, *comm.arrs)
    return list(res[:n_out]), list(res[n_out:])


def _head_gather(c, ada_w, big, to_cast, vec_parts):
    cols = ada_w.shape[2]
    g_c, g_big = _GatherComm([c]), _GatherComm(big)
    g_mod = _GatherComm([jax.ShapeDtypeStruct((2, N_DEV, cols), F32)])
    g_vec = _GatherComm([jax.ShapeDtypeStruct((VEC_ROWS, LANE), F32)])
    nb, nc, nv = g_big.n, len(to_cast), len(vec_parts)

    def body(*refs):
        c_ref, w_ref = refs[0], refs[1]
        vec_in = refs[2:2 + nv]
        cast_in = refs[2 + nv:2 + nv + nc]
        big_in = refs[2 + nv + nc:2 + nv + nc + nb]
        outs = refs[2 + nv + nc + nb:]
        c_all_ref, mod_all_ref, vec_all_ref = outs[0], outs[1], outs[2]
        cast_out = outs[3:3 + nc]
        big_out = outs[3 + nc:3 + nc + nb]
        part_ref, pack_ref = outs[3 + nc + nb], outs[4 + nc + nb]
        sems = outs[5 + nc + nb:]
        s_c, s_mod, s_big, s_vec = sems[0:3], sems[3:6], sems[6:9], sems[9:12]
        g_c.start([c_ref], [c_all_ref], s_c)
        g_big.start(big_in, big_out, s_big)
        pack_ref[...] = jnp.zeros_like(pack_ref)
        row = 0
        for ref, (_, nrows) in zip(vec_in, VEC_LAYOUT):
            pack_ref[row:row + nrows, :] = ref[0] if len(ref.shape) == 3 else ref[...]
            row += nrows
        g_vec.start([pack_ref], [vec_all_ref], s_vec)
        g_c.mid([c_ref], [c_all_ref], s_c)
        g_c.finish([c_ref], [c_all_ref], s_c)
        cv = c_all_ref[:, 0, :]
        cond = cv * _sigmoid(cv)
        for l in range(2):
            part_ref[l] = _dot(cond, w_ref[l])
        g_mod.start([part_ref], [mod_all_ref], s_mod)
        for src, dst in zip(cast_in, cast_out):
            dst[...] = src[...].astype(BF16)
        for g, ins, outs_, sm in ((g_vec, [pack_ref], [vec_all_ref], s_vec), (g_mod, [part_ref], [mod_all_ref], s_mod),
                                  (g_big, big_in, big_out, s_big)):
            g.mid(ins, outs_, sm)
            g.finish(ins, outs_, sm)

    any_spec = pl.BlockSpec(memory_space=pl.ANY)
    vmem_spec = pl.BlockSpec(memory_space=pltpu.VMEM)
    res = _pallas(
        body, name="head_gather",
        out_shape=(g_c.out_shapes() + g_mod.out_shapes() + g_vec.out_shapes()
                   + [jax.ShapeDtypeStruct(a.shape, BF16) for a in to_cast] + g_big.out_shapes()),
        in_specs=[vmem_spec] * (2 + nv + nc) + [any_spec] * nb,
        out_specs=[vmem_spec] * (3 + nc) + [any_spec] * nb,
        scratch_shapes=[pltpu.VMEM((2, N_DEV, cols), F32), pltpu.VMEM((VEC_ROWS, LANE), F32)]
        + g_c.sems() + g_mod.sems() + g_big.sems() + g_vec.sems(),
        compiler_params=pltpu.CompilerParams(vmem_limit_bytes=VMEM_LIMIT),
    )(c, ada_w, *vec_parts, *to_cast, *big)
    return res[0], res[1], res[2], list(res[3 + nc:]), list(res[3:3 + nc])


def _ada_update(c_all, dmod_cols, dmod_all, ada_w, m_w, v_w, ada_b, m_b, v_b):
    cols = ada_w.shape[2]
    nb = ada_b.shape[1]

    def body(c_ref, dmc_ref, dma_ref, w_ref, mw_ref, vw_ref, b_ref, mb_ref, vb_ref,
             gw_ref, dw_ref, nmw_ref, nvw_ref, gb_ref, db_ref, nmb_ref, nvb_ref):
        cv = c_ref[...]
        cond = cv * _sigmoid(cv)
        for l in range(2):
            g = _dot_tn(cond, dmc_ref[l])
            gw_ref[l] = g
            dlt, m2, v2 = _adam(w_ref[l], g, mw_ref[l], vw_ref[l])
            dw_ref[l] = dlt
            nmw_ref[l] = m2
            nvw_ref[l] = v2
        gb = dma_ref[0]
        for i in range(1, N_DEV):
            gb = gb + dma_ref[i]
        gb_ref[...] = gb
        dlt, m2, v2 = _adam(b_ref[...], gb, mb_ref[...], vb_ref[...])
        db_ref[...] = dlt
        nmb_ref[...] = m2
        nvb_ref[...] = v2

    wspec = _full((2, D, cols))
    bspec = _full((2, nb))
    wshape = jax.ShapeDtypeStruct((2, D, cols), F32)
    bshape = jax.ShapeDtypeStruct((2, nb), F32)
    return _pallas(
        body, name="ada_update", grid=(1,),
        in_specs=[_full((N_DEV, D)), _full((2, N_DEV, cols)), _full((N_DEV, 2, nb)),
                  wspec, wspec, wspec, bspec, bspec, bspec],
        out_specs=[wspec] * 4 + [bspec] * 4,
        out_shape=[wshape] * 4 + [bshape] * 4,
        compiler_params=_params(("arbitrary",)),
    )(c_all, dmod_cols, dmod_all, ada_w, m_w, v_w, ada_b, m_b, v_b)


def _ev_in(x, mod, w_in, rc, rs1, rs2, comm=None):
    T = x.shape[0]

    def body(x_ref, mod_ref, w_ref, c_ref, s1_ref, s2_ref, q_ref, kv_ref, su_ref, sv_ref, g_ref):
        h = x_ref[...] * (1.0 + mod_ref[1:2, :]) + mod_ref[0:1, :]
        p = _dot_nt(h, w_ref[...])
        c, s1, s2 = c_ref[...], s1_ref[...], s2_ref[...]
        for j in range(ATTN_W // LANE):
            qr = _rope_fwd(p[:, j * LANE:(j + 1) * LANE], c, s1, s2)
            q_ref[:, j * LANE:(j + 1) * LANE] = (qr * (HEAD_DIM ** -0.5)).astype(BF16)
        low = lax.broadcasted_iota(jnp.int32, (TMF, LANE), 1) < HEAD_DIM
        for j, val in enumerate((_rope_fwd(p[:, 512:640], c, s1, s2), p[:, 640:768])):
            swapped = pltpu.roll(val, HEAD_DIM, 1)
            tiles = (jnp.where(low, val, 0.0), jnp.where(low, 0.0, swapped),
                     jnp.where(low, swapped, 0.0), jnp.where(low, 0.0, val))
            for k, tile in enumerate(tiles):
                kv_ref[:, (4 * j + k) * LANE:(4 * j + k + 1) * LANE] = tile.astype(BF16)
        su_ref[...] = p[:, 768:1280].astype(BF16)
        sv_ref[...] = p[:, 1280:1792].astype(BF16)
        g_ref[...] = p[:, 1792:2816].astype(BF16)

    sh = lambda w: jax.ShapeDtypeStruct((T, w), BF16)
    return _fused_call(
        body, comm, (x, mod, w_in, rc, rs1, rs2), name="ev_in", grid=(T // TMF,),
        in_specs=[_tile(TMF, D), _full((3, D)), _full((EV_IN, D)), _tile(TMF, LANE), _tile(TMF, LANE),
                  _tile(TMF, LANE)],
        out_specs=[_tile(TMF, ATTN_W), _tile(TMF, KVX_W), _tile(TMF, SG_W), _tile(TMF, SG_W), _tile(TMF, D)],
        out_shape=[sh(ATTN_W), sh(KVX_W), sh(SG_W), sh(SG_W), sh(D)], semantics=("parallel",))


def _band_specs(width, nb, k):
    return [pl.BlockSpec((BLK, width), lambda n: (jnp.maximum(k * n - 1, 0), 0)),
            pl.BlockSpec((k * BLK, width), lambda n: (n, 0)),
            pl.BlockSpec((BLK, width), lambda n: (jnp.minimum(k * n + k, nb - 1), 0))]


def _band_bias(bias_ref, n, nb):
    rows = lax.broadcasted_iota(jnp.int32, (3 * BLK, 1), 0)
    outside = ((rows < BLK) & (n == 0)) | ((rows >= 2 * BLK) & (n == nb - 1))
    return bias_ref[...] + jnp.where(outside, NEG_INF, 0.0)


def _lane_tile(ref, t):
    return ref[:, t * LANE:(t + 1) * LANE]


def _split_bf16(v):
    hi = v.astype(BF16)
    return hi, (v - hi.astype(F32)).astype(BF16)


def _group_mean(v, a_ref, exact_bf16=False):
    hi, lo = _split_bf16(v)
    a = a_ref[...]
    out = []
    for t in range(SG_W // (2 * LANE)):
        sl = slice(t * 2 * LANE, (t + 1) * 2 * LANE)
        r = jnp.dot(hi[:, sl], a, preferred_element_type=F32)
        if not exact_bf16:
            r = r + jnp.dot(lo[:, sl], a, preferred_element_type=F32)
        out.append(r)
    return jnp.concatenate(out, axis=-1)


def _sg_core(sv_ref, lng, lnb, a_ref, w_ref, bfull_ref):
    svf = sv_ref[...].astype(F32)
    xc = svf - _group_mean(svf, a_ref, exact_bf16=True)
    rstd = lax.rsqrt(_group_mean(xc * xc, a_ref) + LN_EPS)
    xhat = xc * rstd
    vb = (xhat * lng + lnb).astype(BF16)
    low = lax.broadcasted_iota(jnp.int32, (BLK, LANE), 1) < SG_DIM
    tiles = []
    for t in range(SG_W // LANE):
        v2 = vb[:, t * LANE:(t + 1) * LANE]
        r0 = jnp.dot(w_ref[2 * t], v2, preferred_element_type=F32)
        r1 = jnp.dot(w_ref[2 * t + 1], v2, preferred_element_type=F32)
        tiles.append(jnp.where(low, r0, r1))
    svm = jnp.concatenate(tiles, axis=-1) + bfull_ref[...]
    return xhat, rstd, vb, svm


def _mix0_fwd(q, kvx, su, sv, g0, sink_l, bias, a128, sg_lng, sg_lnb, sg_w, sg_bfull, comm=None):
    T = q.shape[0]
    nb = T // BLK

    def body(q_ref, kp_ref, kc_ref, kn_ref, su_ref, sv_ref, g_ref, sink_ref, bias_ref, a_ref, lng_ref, lnb_ref,
             w_ref, bfull_ref, ycat_ref, y0_ref, lse_ref):
        n = pl.program_id(0)
        kvx4 = jnp.concatenate([kp_ref[...], kc_ref[...], kn_ref[...]], axis=0)
        for s in range(FWD_BLOCKS):
            rows = slice(s * BLK, (s + 1) * BLK)
            bias = _band_bias(bias_ref, FWD_BLOCKS * n + s, nb)
            kvx = kvx4[s * BLK:s * BLK + 3 * BLK]
            tiles = []
            for t in range(ATTN_W // LANE):
                qt = q_ref[rows, t * LANE:(t + 1) * LANE]
                acc = None
                for par in range(2):
                    h = 2 * t + par
                    kt = 2 * (h // 4) + par
                    ke = kvx[:, kt * LANE:(kt + 1) * LANE]
                    ve = kvx[:, (4 + kt) * LANE:(5 + kt) * LANE]
                    st = _dot_nt(ke, qt) + bias
                    sk = _lane_tile(sink_ref, h)
                    m = jnp.maximum(jnp.max(st, axis=0, keepdims=True), sk)
                    p = jnp.exp(st - m)
                    denom = jnp.sum(p, axis=0, keepdims=True) + jnp.exp(sk - m)
                    contrib = _dot_tn(p * (1.0 / denom), ve)
                    acc = contrib if acc is None else acc + contrib
                    lse_ref[s, :, h * LANE:(h + 1) * LANE] = m + jnp.log(denom)
                tiles.append(acc)
            _, _, _, svm = _sg_core(sv_ref.at[rows, :], lng_ref[...], lnb_ref[...], a_ref, w_ref, bfull_ref)
            tiles.append(su_ref[rows, :].astype(F32) * svm)
            ycat = jnp.concatenate(tiles, axis=-1)
            gf = g_ref[rows, :].astype(F32)
            ycat_ref[rows, :] = ycat.astype(BF16)
            y0_ref[rows, :] = (ycat * (gf * _sigmoid(gf))).astype(BF16)

    two = FWD_BLOCKS * BLK
    return _fused_call(
        body, comm, (q, kvx, kvx, kvx, su, sv, g0, sink_l, bias, a128, sg_lng, sg_lnb, sg_w, sg_bfull),
        name="mix0_fwd", grid=(nb // FWD_BLOCKS,),
        in_specs=[_tile(two, ATTN_W)] + _band_specs(KVX_W, nb, FWD_BLOCKS) + [
                  _tile(two, SG_W), _tile(two, SG_W), _tile(two, D), _full((1, N_HEADS * LANE)),
                  _full((3 * BLK, LANE)), _full((2 * LANE, 2 * LANE)),_full((1, SG_W)), _full((1, SG_W)),
                  _full((SG_GROUPS, BLK, BLK)), _full((BLK, SG_W))],
        out_specs=[_tile(two, D), _tile(two, D),
                   pl.BlockSpec((FWD_BLOCKS, 1, N_HEADS * LANE), lambda n: (n, 0, 0))],
        out_shape=[jax.ShapeDtypeStruct((T, D), BF16), jax.ShapeDtypeStruct((T, D), BF16),
                   jax.ShapeDtypeStruct((nb, 1, N_HEADS * LANE), F32)], semantics=("parallel",))


def _ev_out(y0, w_out, x, mod, lnp):
    T = x.shape[0]

    def body(y_ref, w_ref, x_ref, mod_ref, ln_ref, out_ref, z_ref, x1_ref):
        out = _dot(y_ref[...], w_ref[...])
        z = ALPHA * x_ref[...] + mod_ref[2:3, :] * out
        x1, _, _ = _ln_fwd(z, ln_ref[0:1, :], ln_ref[1:2, :])
        out_ref[...] = out.astype(BF16)
        z_ref[...] = z
        x1_ref[...] = x1

    return _pallas(
        body, name="ev_out", grid=(T // TMF,),
        in_specs=[_tile(TMF, D), _full((D, D)), _tile(TMF, D), _full((3, D)), _full((2, D))],
        out_specs=[_tile(TMF, D)] * 3,
        out_shape=[jax.ShapeDtypeStruct((T, D), BF16), jax.ShapeDtypeStruct((T, D), F32),
                   jax.ShapeDtypeStruct((T, D), F32)],
        compiler_params=_params(("parallel",)),
    )(y0, w_out, x, mod, lnp)


def _od_in(x1, mod, w_in):
    T = x1.shape[0]

    def body(x_ref, mod_ref, w_ref, xr_ref, g_ref):
        h = x_ref[...] * (1.0 + mod_ref[1:2, :]) + mod_ref[0:1, :]
        p = _dot(h, w_ref[...])
        xr_ref[...] = p[:, :D]
        g_ref[...] = p[:, D:].astype(BF16)

    return _pallas(
        body, name="od_in", grid=(T // TMF,),
        in_specs=[_tile(TMF, D), _full((3, D)), _full((D, OD_IN))],
        out_specs=[_tile(TMF, D), _tile(TMF, D)],
        out_shape=[jax.ShapeDtypeStruct((T, D), F32), jax.ShapeDtypeStruct((T, D), BF16)],
        compiler_params=_params(("parallel",)),
    )(x1, mod, w_in)


def _ext_rows(prev_ref, cur, next_ref, j, n):
    prev = jnp.where(j > 0, prev_ref[...], 0.0)
    nxt = jnp.where(j < n - 1, next_ref[...], 0.0)
    return jnp.concatenate([prev, cur, nxt], axis=0)


def _shift_rows(ext, off, rows):
    total = ext.shape[0]
    if off == 0:
        return ext[SUBLANE:SUBLANE + rows, :]
    return pltpu.roll(ext, (-off) % total, 0)[SUBLANE:SUBLANE + rows, :]


def _conv_fwd(ext, cw, cb, rows):
    xc = cb
    for k in range(4):
        xc = xc + cw[k:k + 1, :] * _shift_rows(ext, k - 2, rows)
    return xc


def _gates(xc, wa_ref, wx_ref, ba, bx, lam):
    pr, pi = [], []
    for h in range(RNN_HEADS):
        xh = xc[:, h * RNN_HD:(h + 1) * RNN_HD].astype(BF16)
        pr.append(_dot(xh, wa_ref[h]))
        pi.append(_dot(xh, wx_ref[h]))
    r = _sigmoid(jnp.concatenate(pr, axis=-1) + ba)
    ig = _sigmoid(jnp.concatenate(pi, axis=-1) + bx)
    sp = jnp.maximum(-lam, 0.0) + jnp.log(1.0 + jnp.exp(-jnp.abs(lam)))
    neg_log_a = RG_C * r * sp
    a = jnp.exp(-neg_log_a)
    s2 = (1.0 + a * a) * jnp.tanh(neg_log_a)
    inv_s = lax.rsqrt(jnp.maximum(s2, 1e-30))
    return r, ig, sp, a, s2 * inv_s, inv_s


def _scan_tile(a_ref, b_ref, o_ref, carry_ref, rows, reverse):
    ridx = lax.broadcasted_iota(jnp.int32, (SUBLANE, D), 0)
    groups = rows // SUBLANE

    def group(gi, h):
        g = (groups - 1 - gi) if reverse else gi
        off = pl.multiple_of(g * SUBLANE, SUBLANE)
        a = a_ref[pl.ds(off, SUBLANE), :]
        b = b_ref[pl.ds(off, SUBLANE), :]
        for sh in (1, 2, 4):
            if reverse:
                keep = ridx < SUBLANE - sh
                a_p = jnp.where(keep, pltpu.roll(a, SUBLANE - sh, 0), 1.0)
                b_p = jnp.where(keep, pltpu.roll(b, SUBLANE - sh, 0), 0.0)
            else:
                keep = ridx >= sh
                a_p = jnp.where(keep, pltpu.roll(a, sh, 0), 1.0)
                b_p = jnp.where(keep, pltpu.roll(b, sh, 0), 0.0)
            b = b + a * b_p
            a = a * a_p
        hh = b + a * h
        o_ref[pl.ds(off, SUBLANE), :] = hh
        return hh[0:1, :] if reverse else hh[SUBLANE - 1:SUBLANE, :]

    carry_ref[...] = lax.fori_loop(0, groups, group, carry_ref[...])


def _rglru_fwd(xr, cw, cb, wa, wx, ba, bx, lam, reverse, name):
    T = xr.shape[0]
    n = T // TS
    prev_spec, next_spec = _halo_specs(TS, D, n, T, reverse)

    def body(prev_ref, cur_ref, next_ref, cw_ref, cb_ref, wa_ref, wx_ref, ba_ref, bx_ref, lam_ref,
             h_ref, a_ref, s_ref, r_ref, ig_ref, xc_ref, b_s, carry):
        i = pl.program_id(0)
        j = (n - 1 - i) if reverse else i

        @pl.when(i == 0)
        def _():
            carry[...] = jnp.zeros_like(carry)

        ext = _ext_rows(prev_ref, cur_ref[...], next_ref, j, n)
        xc = _conv_fwd(ext, cw_ref[...], cb_ref[...], TS)
        r, ig, _, a, s, _ = _gates(xc, wa_ref, wx_ref, ba_ref[...], bx_ref[...], lam_ref[...])
        s_ref[...] = s
        r_ref[...] = r.astype(BF16)
        ig_ref[...] = ig.astype(BF16)
        xc_ref[...] = xc.astype(BF16)
        a_ref[...] = a
        b_s[...] = s * ig * xc
        _scan_tile(a_ref, b_s, h_ref, carry, TS, reverse)

    wspec = _full((RNN_HEADS, RNN_HD, RNN_HD))
    cur = _rev_tile(TS, D, n, reverse)
    f32 = jax.ShapeDtypeStruct((T, D), F32)
    b16 = jax.ShapeDtypeStruct((T, D), BF16)
    return _pallas(
        body, name=name, grid=(n,),
        in_specs=[prev_spec, cur, next_spec, _full((4, D)), _full((1, D)),
                  wspec, wspec, _full((1, D)), _full((1, D)), _full((1, D))],
        out_specs=[cur] * 6,
        out_shape=[f32, f32, f32, b16, b16, b16],
        scratch_shapes=[pltpu.VMEM((TS, D), F32), pltpu.VMEM((1, D), F32)],
        compiler_params=_params(("arbitrary",)),
    )(xr, xr, xr, cw, cb, wa, wx, ba, bx, lam)


def _od_out(hf, hb, g1, w_out, x1, tgt, mod, lnp):
    T = x1.shape[0]

    def body(hf_ref, hb_ref, g_ref, w_ref, x_ref, t_ref, mod_ref, ln_ref,
             dh_ref, dg_ref, dx_ref, dwb_ref, vec_ref, dw_ref):
        i = pl.program_id(0)

        @pl.when(i == 0)
        def _():
            dw_ref[...] = jnp.zeros_like(dw_ref)
            vec_ref[...] = jnp.zeros_like(vec_ref)

        hs = hf_ref[...] + hb_ref[...]
        sg, dsg = _silu_and_grad(g_ref[...].astype(F32))
        yr = (hs * sg).astype(BF16)
        w = w_ref[...]
        out = _dot(yr, w)
        gate = mod_ref[2:3, :]
        z = ALPHA * x_ref[...] + gate * out
        lng = ln_ref[0:1, :]
        x2, xhat, rstd = _ln_fwd(z, lng, ln_ref[1:2, :])
        diff = x2 - t_ref[...]
        vec_ref[3:4, 0:LANE] += 0.5 * jnp.sum(diff * diff) * (1.0 / D)
        dx2 = diff * (1.0 / D)
        dz = _ln_bwd(dx2, xhat, rstd, lng)
        vec_ref[0:1, :] += _rowsum(dx2 * xhat)
        vec_ref[1:2, :] += _rowsum(dx2)
        vec_ref[2:3, :] += _rowsum(dz * out)
        dout = (dz * gate).astype(BF16)
        dyr = _dot_nt(dout, w)
        dw_ref[...] += _dot_tn(yr, dout)
        dh_ref[...] = dyr * sg
        dg_ref[...] = (dyr * hs * dsg).astype(BF16)
        dx_ref[...] = ALPHA * dz

        @pl.when(i == T // TMO - 1)
        def _():
            dwb_ref[...] = dw_ref[...].astype(BF16)

    return _pallas(
        body, name="od_out", grid=(T // TMO,),
        in_specs=[_tile(TMO, D), _tile(TMO, D), _tile(TMO, D), _full((D, D)), _tile(TMO, D), _tile(TMO, D),
                  _full((3, D)), _full((2, D))],
        out_specs=[_tile(TMO, D), _tile(TMO, D), _tile(TMO, D), _full((D, D)), _full((SUBLANE, D))],
        out_shape=[jax.ShapeDtypeStruct((T, D), F32), jax.ShapeDtypeStruct((T, D), BF16),
                   jax.ShapeDtypeStruct((T, D), F32), jax.ShapeDtypeStruct((D, D), BF16),
                   jax.ShapeDtypeStruct((SUBLANE, D), F32)],
        scratch_shapes=[pltpu.VMEM((D, D), F32)],
        compiler_params=_params(("arbitrary",)),
    )(hf, hb, g1, w_out, x1, tgt, mod, lnp)


def _rglru_bwd(fwd, dh, wa, wx, lam, reverse, name, comm=None):
    h, a_all, s_all, r_all, ig_all, xc_all = fwd
    T = h.shape[0]
    n = T // TS
    adj_rev = not reverse
    hprev_spec, hnext_spec = _halo_specs(TS, D, n, T, adj_rev)
    h_halo_spec = hnext_spec if reverse else hprev_spec

    def body(dh_ref, h_ref, hh_ref, a_ref, s_ref, r_ref, ig_ref, xc_ref, wa_ref, wx_ref, lam_ref,
             dxc_ref, dwa_ref, dwx_ref, vec_ref, a_s, l_s, carry, a_edge):
        i = pl.program_id(0)
        j = (n - 1 - i) if adj_rev else i

        @pl.when(i == 0)
        def _():
            carry[...] = jnp.zeros_like(carry)
            a_edge[...] = jnp.zeros_like(a_edge)
            dwa_ref[...] = jnp.zeros_like(dwa_ref)
            dwx_ref[...] = jnp.zeros_like(dwx_ref)
            vec_ref[...] = jnp.zeros_like(vec_ref)

        lam = lam_ref[...]
        sp = jnp.maximum(-lam, 0.0) + jnp.log(1.0 + jnp.exp(-jnp.abs(lam)))
        a, s = a_ref[...], s_ref[...]
        inv_s = lax.rsqrt(jnp.maximum(s * s, 1e-30))
        r, ig = r_ref[...].astype(F32), ig_ref[...].astype(F32)
        xcb = xc_ref[...]
        xc = xcb.astype(F32)

        rows = lax.broadcasted_iota(jnp.int32, (TS, D), 0)
        hcur = h_ref[...]
        if reverse:
            a_sh = jnp.where(rows == 0, a_edge[...], pltpu.roll(a, 1, 0))
            halo = jnp.where(j < n - 1, hh_ref[0:1, :], 0.0)
            h_nb = jnp.where(rows == TS - 1, halo, pltpu.roll(hcur, TS - 1, 0))
        else:
            a_sh = jnp.where(rows == TS - 1, a_edge[...], pltpu.roll(a, TS - 1, 0))
            halo = jnp.where(j > 0, hh_ref[SUBLANE - 1:SUBLANE, :], 0.0)
            h_nb = jnp.where(rows == 0, halo, pltpu.roll(hcur, 1, 0))
        a_s[...] = a_sh
        _scan_tile(a_s, dh_ref, l_s, carry, TS, adj_rev)
        a_edge[...] = a[TS - 1:TS, :] if reverse else a[0:1, :]

        lm = l_s[...]
        da = lm * h_nb
        di = lm * s * xc
        dxc = lm * s * ig
        ds = lm * ig * xc
        dlog_a = a * (da - ds * a * inv_s)
        dr = (-RG_C) * sp * dlog_a
        dsp = _rowsum((-RG_C) * r * dlog_a)
        dpr = dr * r * (1.0 - r)
        dpi = di * ig * (1.0 - ig)
        vec_ref[0:1, :] += _rowsum(dpr)
        vec_ref[1:2, :] += _rowsum(dpi)
        vec_ref[2:3, :] += dsp * (-_sigmoid(-lam))
        parts = []
        for hd in range(RNN_HEADS):
            sl = slice(hd * RNN_HD, (hd + 1) * RNN_HD)
            xh = xcb[:, sl]
            dprh = dpr[:, sl].astype(BF16)
            dpih = dpi[:, sl].astype(BF16)
            parts.append(_dot_nt(dprh, wa_ref[hd]) + _dot_nt(dpih, wx_ref[hd]))
            dwa_ref[hd] += _dot_tn(xh, dprh)
            dwx_ref[hd] += _dot_tn(xh, dpih)
        dxc_ref[...] = dxc + jnp.concatenate(parts, axis=-1)

    wspec = _full((RNN_HEADS, RNN_HD, RNN_HD))
    cur = _rev_tile(TS, D, n, adj_rev)
    return _fused_call(
        body, comm, (dh, h, h, a_all, s_all, r_all, ig_all, xc_all, wa, wx, lam), name=name, grid=(n,),
        in_specs=[cur, cur, h_halo_spec, cur, cur, cur, cur, cur, wspec, wspec, _full((1, D))],
        out_specs=[cur, wspec, wspec, _full((SUBLANE, D))],
        out_shape=[jax.ShapeDtypeStruct((T, D), F32),
                   jax.ShapeDtypeStruct((RNN_HEADS, RNN_HD, RNN_HD), F32),
                   jax.ShapeDtypeStruct((RNN_HEADS, RNN_HD, RNN_HD), F32),
                   jax.ShapeDtypeStruct((SUBLANE, D), F32)],
        scratch_shapes=[pltpu.VMEM((TS, D), F32)] * 2 + [pltpu.VMEM((1, D), F32)] * 2)


def _od_in_bwd(dxcf, dxcb, xr, dg1, x1, dx1p, mod, w_in, cw, comm=None):
    T = x1.shape[0]
    n = T // TMO
    slab = OD_IN // N_DEV
    prev_spec, next_spec = _halo_specs(TMO, D, n, T, False)

    def body(fp_ref, fc_ref, fn_ref, bp_ref, bc_ref, bn_ref, xr_ref, dg_ref, x1_ref, dxp_ref,
             mod_ref, w_ref, cw_ref, dx_ref, dwb_ref, vec_ref, dw_ref):
        i = pl.program_id(0)

        @pl.when(i == 0)
        def _():
            dw_ref[...] = jnp.zeros_like(dw_ref)
            vec_ref[...] = jnp.zeros_like(vec_ref)

        dcur = fc_ref[...] + bc_ref[...]
        dprev = jnp.where(i > 0, fp_ref[...] + bp_ref[...], 0.0)
        dnext = jnp.where(i < n - 1, fn_ref[...] + bn_ref[...], 0.0)
        dext = jnp.concatenate([dprev, dcur, dnext], axis=0)
        xr_v = xr_ref[...]
        cw_v = cw_ref[...]
        dxr = None
        for k in range(4):
            shifted = _shift_rows(dext, 2 - k, TMO)
            term = cw_v[k:k + 1, :] * shifted
            dxr = term if dxr is None else dxr + term
            vec_ref[k:k + 1, :] += _rowsum(shifted * xr_v)
        vec_ref[4:5, :] += _rowsum(dcur)
        dp = jnp.concatenate([dxr.astype(BF16), dg_ref[...]], axis=-1)
        x1v = x1_ref[...]
        scale1 = 1.0 + mod_ref[1:2, :]
        h1 = (x1v * scale1 + mod_ref[0:1, :]).astype(BF16)
        dh1 = _dot_nt(dp, w_ref[...])
        dw_ref[...] += _dot_tn(h1, dp)
        dx_ref[...] = dxp_ref[...] + dh1 * scale1
        vec_ref[5:6, :] += _rowsum(dh1)
        vec_ref[6:7, :] += _rowsum(dh1 * x1v)

        @pl.when(i == n - 1)
        def _():
            for j in range(N_DEV):
                dwb_ref[j] = dw_ref[:, j * slab:(j + 1) * slab].astype(BF16)

    t = _tile(TMO, D)
    return _fused_call(
        body, comm, (dxcf, dxcf, dxcf, dxcb, dxcb, dxcb, xr, dg1, x1, dx1p, mod, w_in, cw),
        name="od_in_bwd", grid=(n,),
        in_specs=[prev_spec, t, next_spec, prev_spec, t, next_spec, t, t, t, t,
                  _full((3, D)), _full((D, OD_IN)), _full((4, D))],
        out_specs=[t, _full((N_DEV, D, slab)), _full((SUBLANE, D))],
        out_shape=[jax.ShapeDtypeStruct((T, D), F32), jax.ShapeDtypeStruct((N_DEV, D, slab), BF16),
                   jax.ShapeDtypeStruct((SUBLANE, D), F32)],
        scratch_shapes=[pltpu.VMEM((D, OD_IN), F32)])


def _ev_out_bwd(dx1, z0, out0, y0, ycat, g0, w_out, mod, lnp):
    T = dx1.shape[0]

    def body(dx_ref, z_ref, out_ref, y0_ref, yc_ref, g_ref, w_ref, mod_ref, ln_ref,
             dxp_ref, dyc_ref, dg_ref, dwb_ref, vec_ref, dw_ref):
        i = pl.program_id(0)

        @pl.when(i == 0)
        def _():
            dw_ref[...] = jnp.zeros_like(dw_ref)
            vec_ref[...] = jnp.zeros_like(vec_ref)

        lng = ln_ref[0:1, :]
        _, xhat, rstd = _ln_fwd(z_ref[...], lng, ln_ref[1:2, :])
        dy = dx_ref[...]
        dz = _ln_bwd(dy, xhat, rstd, lng)
        vec_ref[0:1, :] += _rowsum(dy * xhat)
        vec_ref[1:2, :] += _rowsum(dy)
        vec_ref[2:3, :] += _rowsum(dz * out_ref[...].astype(F32))
        dout = (dz * mod_ref[2:3, :]).astype(BF16)
        dy0 = _dot_nt(dout, w_ref[...])
        dw_ref[...] += _dot_tn(y0_ref[...], dout)
        sg, dsg = _silu_and_grad(g_ref[...].astype(F32))
        dyc_ref[...] = (dy0 * sg).astype(BF16)
        dg_ref[...] = (dy0 * yc_ref[...].astype(F32) * dsg).astype(BF16)
        dxp_ref[...] = ALPHA * dz

        @pl.when(i == T // TMO - 1)
        def _():
            dwb_ref[...] = dw_ref[...].astype(BF16)

    t = _tile(TMO, D)
    return _pallas(
        body, name="ev_out_bwd", grid=(T // TMO,),
        in_specs=[t, t, t, t, t, t, _full((D, D)), _full((3, D)), _full((2, D))],
        out_specs=[t, t, t, _full((D, D)), _full((SUBLANE, D))],
        out_shape=[jax.ShapeDtypeStruct((T, D), F32), jax.ShapeDtypeStruct((T, D), BF16),
                   jax.ShapeDtypeStruct((T, D), BF16), jax.ShapeDtypeStruct((D, D), BF16),
                   jax.ShapeDtypeStruct((SUBLANE, D), F32)],
        scratch_shapes=[pltpu.VMEM((D, D), F32)],
        compiler_params=_params(("arbitrary",)),
    )(dx1, z0, out0, y0, ycat, g0, w_out, mod, lnp)


def _mix0_bwd(q, kvx, lse, dyc, ycat, su, sv, sink_l, bias, a128, gsum, sel, sg_lng, sg_lnb, sg_w, sg_bfull,
              rc, rs1, rs2, comm=None):
    T = q.shape[0]
    nb = T // BLK

    def body(q_ref, kp_ref, kc_ref, kn_ref, lse_ref, dyc_ref, yc_ref, su_ref, sv_ref, sink_ref, bias_ref, a_ref,
             gsum_ref, sel_ref, lng_ref, lnb_ref, w_ref, bfull_ref, c_ref, s1_ref, s2_ref,
             dq_ref, dkv_ref, dsu_ref, dsv_ref, dw_ref, dbt_ref, vec_ref, dsink_ref):
        n = pl.program_id(0)

        @pl.when(n == 0)
        def _():
            dkv_ref[...] = jnp.zeros_like(dkv_ref)
            dw_ref[...] = jnp.zeros_like(dw_ref)
            dbt_ref[...] = jnp.zeros_like(dbt_ref)
            vec_ref[...] = jnp.zeros_like(vec_ref)
            dsink_ref[...] = jnp.zeros_like(dsink_ref)

        kvx4 = jnp.concatenate([kp_ref[...], kc_ref[...], kn_ref[...]], axis=0)
        for s in range(BWD_BLOCKS):
            _mix0_bwd_block(s, BWD_BLOCKS * n + s, nb, kvx4[s * BLK:s * BLK + 3 * BLK], q_ref, lse_ref, dyc_ref, yc_ref, su_ref,
                            sv_ref, sink_ref, bias_ref, a_ref, gsum_ref, sel_ref, lng_ref, lnb_ref, w_ref, bfull_ref,
                            c_ref, s1_ref, s2_ref, dq_ref, dkv_ref, dsu_ref, dsv_ref, dw_ref, dbt_ref, vec_ref,
                            dsink_ref)

    def _mix0_bwd_block(s, b, nb, kvx, q_ref, lse_ref, dyc_ref, yc_ref, su_ref, sv_ref, sink_ref, bias_ref, a_ref,
                        gsum_ref, sel_ref, lng_ref, lnb_ref, w_ref, bfull_ref, c_ref, s1_ref, s2_ref,
                        dq_ref, dkv_ref, dsu_ref, dsv_ref, dw_ref, dbt_ref, vec_ref, dsink_ref):
        rows = slice(s * BLK, (s + 1) * BLK)

        def tile(ref, t):
            return ref[rows, t * LANE:(t + 1) * LANE]

        band = pl.ds(pl.multiple_of(b * BLK + (TM - BLK), BLK), 3 * BLK)
        bias = _band_bias(bias_ref, b, nb)
        bias2 = jnp.concatenate([bias, bias], axis=1)
        low = lax.broadcasted_iota(jnp.int32, (BLK, LANE), 1) < HEAD_DIM
        low2 = lax.broadcasted_iota(jnp.int32, (2 * BLK, LANE), 1) < HEAD_DIM
        sel = sel_ref[...]
        c, s1, s2 = c_ref[rows, :], s1_ref[rows, :], s2_ref[rows, :]
        for kvh in range(2):
            t0, t1 = 2 * kvh, 2 * kvh + 1
            q2 = jnp.concatenate([tile(q_ref, t0), tile(q_ref, t1)], axis=0)
            do2 = jnp.concatenate([tile(dyc_ref, t0), tile(dyc_ref, t1)], axis=0)
            yc2 = jnp.concatenate([tile(yc_ref, t0), tile(yc_ref, t1)], axis=0)
            p_hi, p_lo = _split_bf16(do2.astype(F32) * yc2.astype(F32))
            deltas = _dot_nt(sel, p_hi) + _dot_nt(sel, p_lo)
            dkx = jnp.zeros((3 * BLK, LANE), F32)
            dvx = jnp.zeros((3 * BLK, LANE), F32)
            dq_acc = None
            for par in range(2):
                heads = (4 * kvh + par, 4 * kvh + 2 + par)
                kt = 2 * kvh + par
                ke = kvx[:, kt * LANE:(kt + 1) * LANE]
                ve = kvx[:, (4 + kt) * LANE:(5 + kt) * LANE]
                lse = jnp.concatenate([lse_ref[s, :, h * LANE:(h + 1) * LANE] for h in heads], axis=1)
                sk = jnp.concatenate([_lane_tile(sink_ref, h) for h in heads], axis=1)
                delta = deltas[par:par + 1, :]
                pt = jnp.exp(_dot_nt(ke, q2) + bias2 - lse)
                dst = (pt * (_dot_nt(ve, do2) - delta)).astype(BF16)
                sink_terms = jnp.exp(sk - lse) * delta
                for k, h in enumerate(heads):
                    dsink_ref[:, h * LANE:(h + 1) * LANE] += sink_terms[:, k * LANE:(k + 1) * LANE]
                part = _dot_tn(dst, ke)
                dq_acc = part if dq_acc is None else dq_acc + part
                mine = low2 if par == 0 else jnp.logical_not(low2)
                dkx = dkx + jnp.dot(dst, jnp.where(mine, q2, jnp.zeros_like(q2)), preferred_element_type=F32)
                dvx = dvx + jnp.dot(pt.astype(BF16), jnp.where(mine, do2, jnp.zeros_like(do2)),
                                    preferred_element_type=F32)
            for k, t in enumerate((t0, t1)):
                dq_t = dq_acc[k * BLK:(k + 1) * BLK] * (HEAD_DIM ** -0.5)
                dq_ref[rows, t * LANE:(t + 1) * LANE] = _rope_bwd(dq_t, c, s1, s2).astype(BF16)
            dkv_ref[band, kvh * LANE:(kvh + 1) * LANE] += dkx
            dkv_ref[band, (2 + kvh) * LANE:(3 + kvh) * LANE] += dvx

        lng = lng_ref[...]
        xhat, rstd, vb, svm = _sg_core(sv_ref.at[rows, :], lng, lnb_ref[...], a_ref, w_ref, bfull_ref)
        dy = dyc_ref[rows, ATTN_W:].astype(F32)
        dsu_ref[rows, :] = (dy * svm).astype(BF16)
        dsvm = dy * su_ref[rows, :].astype(F32)
        d_hi, d_lo = _split_bf16(dsvm)
        gsum = gsum_ref[...]
        dbt_ref[...] += jnp.dot(d_hi, gsum, preferred_element_type=F32) + jnp.dot(d_lo, gsum,
                                                                                 preferred_element_type=F32)
        tiles = []
        for t in range(SG_W // LANE):
            tl = slice(t * LANE, (t + 1) * LANE)
            dt, v2 = d_hi[:, tl], vb[:, tl]
            dw_ref[2 * t] += _dot_nt(jnp.where(low, dt, jnp.zeros_like(dt)), v2)
            dw_ref[2 * t + 1] += _dot_nt(jnp.where(low, jnp.zeros_like(dt), dt), v2)
            tiles.append(jnp.where(low, _dot_tn(w_ref[2 * t], dt), _dot_tn(w_ref[2 * t + 1], dt)))
        dvgn = jnp.concatenate(tiles, axis=-1)
        vec_ref[0:1, :] += _rowsum(dvgn * xhat)
        vec_ref[1:2, :] += _rowsum(dvgn)
        dxh = dvgn * lng
        m1 = _group_mean(dxh, a_ref)
        m2 = _group_mean(dxh * xhat, a_ref)
        dsv_ref[rows, :] = (rstd * (dxh - m1 - xhat * m2)).astype(BF16)

    two = BWD_BLOCKS * BLK
    return _fused_call(
        body, comm, (q, kvx, kvx, kvx, lse, dyc, ycat, su, sv, sink_l, bias, a128, gsum, sel, sg_lng, sg_lnb, sg_w,
                     sg_bfull, rc, rs1, rs2),
        name="mix0_bwd", grid=(nb // BWD_BLOCKS,),
        in_specs=[_tile(two, ATTN_W)] + _band_specs(KVX_W, nb, BWD_BLOCKS) + [
            pl.BlockSpec((BWD_BLOCKS, 1, N_HEADS * LANE), lambda n: (n, 0, 0)), _tile(two, D), _tile(two, D),
            _tile(two, SG_W), _tile(two, SG_W), _full((1, N_HEADS * LANE)), _full((3 * BLK, LANE)),
            _full((2 * LANE, 2 * LANE)),_full((SG_W, LANE)), _full((SUBLANE, LANE)), _full((1, SG_W)), _full((1, SG_W)),
            _full((SG_GROUPS, BLK, BLK)), _full((BLK, SG_W)), _tile(two, LANE), _tile(two, LANE), _tile(two, LANE)],
        out_specs=[_tile(two, ATTN_W), _full((T + 2 * TM, 4 * LANE)), _tile(two, SG_W), _tile(two, SG_W),
                   _full((SG_GROUPS, BLK, BLK)), _full((BLK, LANE)), _full((SUBLANE, SG_W)),
                   _full((1, N_HEADS * LANE))],
        out_shape=[jax.ShapeDtypeStruct((T, ATTN_W), BF16), jax.ShapeDtypeStruct((T + 2 * TM, 4 * LANE), F32),
                   jax.ShapeDtypeStruct((T, SG_W), BF16), jax.ShapeDtypeStruct((T, SG_W), BF16),
                   jax.ShapeDtypeStruct((SG_GROUPS, BLK, BLK), F32), jax.ShapeDtypeStruct((BLK, LANE), F32),
                   jax.ShapeDtypeStruct((SUBLANE, SG_W), F32), jax.ShapeDtypeStruct((1, N_HEADS * LANE), F32)])


def _ev_in_bwd(dq, dkv, dsu, dsv, dg0, x, dxp, mod, w_in, rc, rs1, rs2, comm=None):
    T = x.shape[0]

    def body(dq_ref, dkv_ref, dsu_ref, dsv_ref, dg_ref, x_ref, dxp_ref, mod_ref, w_ref, c_ref, s1_ref, s2_ref,
             dx_ref, dwb_ref, vec_ref, dw_ref):
        i = pl.program_id(0)

        @pl.when(i == 0)
        def _():
            dw_ref[...] = jnp.zeros_like(dw_ref)
            vec_ref[...] = jnp.zeros_like(vec_ref)

        low = lax.broadcasted_iota(jnp.int32, (TM, LANE), 1) < HEAD_DIM

        def fold(j):
            t0 = dkv_ref[:, (2 * j) * LANE:(2 * j + 1) * LANE]
            t1 = dkv_ref[:, (2 * j + 1) * LANE:(2 * j + 2) * LANE]
            return jnp.where(low, t0 + pltpu.roll(t0, HEAD_DIM, 1), t1 + pltpu.roll(t1, HEAD_DIM, 1))

        dk = _rope_bwd(fold(0), c_ref[...], s1_ref[...], s2_ref[...]).astype(BF16)
        dp = jnp.concatenate([dq_ref[...], dk, fold(1).astype(BF16), dsu_ref[...], dsv_ref[...],
                              dg_ref[...]], axis=-1)
        xv = x_ref[...]
        scale0 = 1.0 + mod_ref[1:2, :]
        h0 = (xv * scale0 + mod_ref[0:1, :]).astype(BF16)
        dh0 = _dot(dp, w_ref[...])
        dw_ref[...] += _dot_tn(dp, h0)
        dx_ref[...] = dxp_ref[...] + dh0 * scale0
        vec_ref[0:1, :] += _rowsum(dh0)
        vec_ref[1:2, :] += _rowsum(dh0 * xv)

        @pl.when(i == T // TM - 1)
        def _():
            dwb_ref[...] = dw_ref[...].astype(BF16)

    t = _tile(TM, D)
    return _fused_call(
        body, comm, (dq, dkv, dsu, dsv, dg0, x, dxp, mod, w_in, rc, rs1, rs2), name="ev_in_bwd", grid=(T // TM,),
        in_specs=[_tile(TM, ATTN_W), pl.BlockSpec((TM, 4 * LANE), lambda i: (i + 1, 0)), _tile(TM, SG_W),
                  _tile(TM, SG_W), t, t, t,
                  _full((3, D)), _full((EV_IN, D)), _tile(TM, LANE), _tile(TM, LANE), _tile(TM, LANE)],
        out_specs=[t, _full((EV_IN, D)), _full((SUBLANE, D))],
        out_shape=[jax.ShapeDtypeStruct((T, D), F32), jax.ShapeDtypeStruct((EV_IN, D), BF16),
                   jax.ShapeDtypeStruct((SUBLANE, D), F32)],
        scratch_shapes=[pltpu.VMEM((EV_IN, D), F32)])


def _sum_slots(land_ref):
    g = land_ref[0].astype(F32)
    for i in range(1, land_ref.shape[0]):
        g = g + land_ref[i].astype(F32)
    return g


def _reduce_adam(items, name, after=()):
    R, C = items[0][1].shape
    rb = R
    if R > 512:
        for cand in (512, 256, 128, 64, 32, 16, 8):
            if R % cand == 0:
                rb = cand
                break
    n = len(items)

    def body(*refs):
        for k in range(n):
            l_ref, w_ref, m_ref, v_ref = refs[4 * k:4 * k + 4]
            first_out = 4 * n + len(after)
            g_ref, d_ref, nm_ref, nv_ref = refs[first_out + 4 * k:first_out + 4 * k + 4]
            g = _sum_slots(l_ref)
            g_ref[...] = g
            dlt, m2, v2 = _adam(w_ref[...], g, m_ref[...], v_ref[...])
            d_ref[...] = dlt
            nm_ref[...] = m2
            nv_ref[...] = v2

    t = pl.BlockSpec((rb, C), lambda i: (i, 0))
    shp = jax.ShapeDtypeStruct((R, C), F32)
    in_specs, operands = [], []
    for land, w, m, v in items:
        in_specs += [pl.BlockSpec((land.shape[0], rb, C), lambda i: (0, i, 0)), t, t, t]
        operands += [land, w, m, v]
    res = _pallas(
        body, name=name, grid=(R // rb,),
        in_specs=in_specs + [pl.BlockSpec(memory_space=pl.ANY)] * len(after),
        out_specs=[t] * (4 * n), out_shape=[shp] * (4 * n),
        compiler_params=_params(("parallel",)),
    )(*operands, *after)
    return [list(res[4 * k:4 * k + 4]) for k in range(n)]


def _tail_stage1(slabs, small):
    _, R, C = slabs.shape
    n_chips = N_DEV // 2
    gather = _GatherComm(small)
    ns = gather.n

    def body(*refs):
        slab_ref = refs[0]
        g_ins = refs[1:1 + ns]
        part, land_ref = refs[1 + ns], refs[2 + ns]
        g_outs = refs[3 + ns:3 + 2 * ns]
        stage, s1_send, s1_recv = refs[3 + 2 * ns:6 + 2 * ns]
        g_sems = refs[6 + 2 * ns:]
        x, y, c = _my_pos()
        chip = 2 * x + y
        gather.start(g_ins, g_outs, g_sems)
        swaps = [pltpu.make_async_remote_copy(
            src_ref=slab_ref.at[2 * k + (1 - c)], dst_ref=stage.at[k], send_sem=s1_send.at[k],
            recv_sem=s1_recv.at[k], device_id=(x, y, 1 - c), device_id_type=MESH) for k in range(n_chips)]
        for cp in swaps:
            cp.start()
        for cp in swaps:
            cp.wait()
        for k in range(n_chips):
            part[k] = (slab_ref[2 * k + c].astype(F32) + stage[k].astype(F32)).astype(BF16)
        land_ref[chip] = part[chip]
        gather.mid(g_ins, g_outs, g_sems)
        gather.finish(g_ins, g_outs, g_sems)

    any_spec = pl.BlockSpec(memory_space=pl.ANY)
    vmem_spec = pl.BlockSpec(memory_space=pltpu.VMEM)
    slab4 = jax.ShapeDtypeStruct((n_chips, R, C), BF16)
    res = _pallas(
        body, name="tail_stage1",
        out_shape=[slab4, slab4] + gather.out_shapes(),
        in_specs=[vmem_spec] + [any_spec] * ns, out_specs=[vmem_spec, vmem_spec] + [any_spec] * ns,
        scratch_shapes=[pltpu.VMEM((n_chips, R, C), BF16),
                        pltpu.SemaphoreType.DMA((n_chips,)), pltpu.SemaphoreType.DMA((n_chips,))] + gather.sems(),
        compiler_params=pltpu.CompilerParams(vmem_limit_bytes=VMEM_LIMIT),
    )(slabs, *gather.arrs)
    return res[0], res[1], list(res[2:])


def _chip_copies(part_ref, land_ref, send_sems, recv_sems):
    x, y, c = _my_pos()
    chip = 2 * x + y
    copies = []
    for r in range(1, N_DEV // 2):
        px = (1 - x) if (r & 2) else x
        py = (1 - y) if (r & 1) else y
        copies.append(pltpu.make_async_remote_copy(
            src_ref=part_ref.at[2 * px + py], dst_ref=land_ref.at[chip], send_sem=send_sems[r - 1],
            recv_sem=recv_sems[r - 1], device_id=(px, py, c), device_id_type=MESH))
    return copies


def _tail_send(part, land):
    n = N_DEV // 2 - 1

    def body(part_ref, land_ref, *outs):
        send_sems, recv_sems = outs[:n], outs[n:2 * n]
        token = outs[2 * n + 2]
        for cp in _chip_copies(part_ref, land_ref, send_sems, recv_sems):
            cp.start()
        token[...] = jnp.zeros_like(token)

    hbm = pl.BlockSpec(memory_space=pltpu.HBM)
    sem = pl.BlockSpec(memory_space=pltpu.SEMAPHORE)
    res = _pallas(
        body, name="tail_send",
        out_shape=tuple([pltpu.SemaphoreType.DMA(())] * (2 * n)
                        + [pltpu.HBM(part.shape, part.dtype), pltpu.HBM(land.shape, land.dtype),
                           jax.ShapeDtypeStruct((SUBLANE, LANE), F32)]),
        in_specs=(hbm, hbm), out_specs=tuple([sem] * (2 * n) + [hbm, hbm, pl.BlockSpec(memory_space=pltpu.VMEM)]),
        input_output_aliases={0: 2 * n, 1: 2 * n + 1},
        compiler_params=pltpu.CompilerParams(has_side_effects=pltpu.SideEffectType.DATAFLOW_SIDE_EFFECTING),
    )(pltpu.with_memory_space_constraint(part, pltpu.HBM), pltpu.with_memory_space_constraint(land, pltpu.HBM))
    return list(res[:n]), list(res[n:2 * n]), res[2 * n], res[2 * n + 1], res[2 * n + 2]


def _tail_wait(send_sems, recv_sems, part, land, after):
    n = len(send_sems)

    def body(part_ref, land_ref, *rest):
        ss, rs = rest[:n], rest[n:2 * n]
        for cp in _chip_copies(part_ref, land_ref, ss, rs):
            cp.wait_send()
            cp.wait_recv()

    hbm = pl.BlockSpec(memory_space=pltpu.HBM)
    sem = pl.BlockSpec(memory_space=pltpu.SEMAPHORE)
    any_spec = pl.BlockSpec(memory_space=pl.ANY)
    res = _pallas(
        body, name="tail_wait",
        out_shape=(pltpu.HBM(part.shape, part.dtype), pltpu.HBM(land.shape, land.dtype)),
        in_specs=tuple([hbm, hbm] + [sem] * (2 * n) + [any_spec] * len(after)), out_specs=(hbm, hbm),
        input_output_aliases={0: 0, 1: 1},
        compiler_params=pltpu.CompilerParams(has_side_effects=pltpu.SideEffectType.DATAFLOW_SIDE_EFFECTING),
    )(part, land, *send_sems, *recv_sems, *after)
    return res[1]


def _slots_adam(items, name, after=()):
    zeros3 = (0, 0, 0)
    in_specs, out_specs, out_shape, operands = [], [], [], []
    for land, w, m, v in items:
        inner = w.shape[-3:]
        if w.ndim == 5:
            lspec = pl.BlockSpec((N_DEV, 1) + inner, lambda i: (0, i) + zeros3)
            wspec = pl.BlockSpec((1, 1) + inner, lambda i: (0, i) + zeros3)
        else:
            lspec = pl.BlockSpec((N_DEV,) + inner, lambda i: (0,) + zeros3)
            wspec = pl.BlockSpec((1,) + inner, lambda i: (0,) + zeros3)
        in_specs += [lspec, wspec, wspec, wspec]
        out_specs += [wspec] * 4
        out_shape += [jax.ShapeDtypeStruct(w.shape, F32)] * 4
        operands += [land, w, m, v]
    n = len(items)

    def body(*refs):
        for k, (_, w, _, _) in enumerate(items):
            l_ref, w_ref, m_ref, v_ref = refs[4 * k:4 * k + 4]
            first_out = 4 * n + len(after)
            outs = refs[first_out + 4 * k:first_out + 4 * k + 4]
            at = (0, 0) if w.ndim == 5 else (0,)

            def update(l_ref=l_ref, w_ref=w_ref, m_ref=m_ref, v_ref=v_ref, outs=outs, at=at):
                g = l_ref[(0,) + at[1:]].astype(F32)
                for i in range(1, N_DEV):
                    g = g + l_ref[(i,) + at[1:]].astype(F32)
                dlt, m2, v2 = _adam(w_ref[at], g, m_ref[at], v_ref[at])
                for o_ref, val in zip(outs, (g, dlt, m2, v2)):
                    o_ref[at] = val

            if w.ndim == 5:
                update()
            else:
                pl.when(pl.program_id(0) == 0)(update)

    res = _pallas(
        body, name=name, grid=(2,),
        in_specs=in_specs + [pl.BlockSpec(memory_space=pl.ANY)] * len(after),
        out_specs=out_specs, out_shape=out_shape,
        compiler_params=_params(("arbitrary",)),
    )(*operands, *after)
    return [list(res[4 * k:4 * k + 4]) for k in range(n)]


SMALL_PARAMS = ("ln_g", "ln_b", "ev_sg_ln_g", "ev_sg_ln_b", "ev_sink", "ev_sg_b",
                "od_conv_w", "od_conv_b", "od_b_a", "od_b_x", "od_lam")


def _small_update(ga, gc, gd, gf, gb, ge, gsink, gbt, params):
    names = list(SMALL_PARAMS)
    flat = [a for nm in names for a in params[nm]]
    n_g = 8

    def body(*refs):
        ga_ref, gc_ref, gd_ref, gf_ref, gb_ref, ge_ref, gs_ref, gbt_ref = refs[:n_g]
        prm = refs[n_g:n_g + 3 * len(names)]
        loss_ref = refs[n_g + 3 * len(names)]
        outs = refs[n_g + 3 * len(names) + 1:]

        def ssum(ref):
            acc = ref[0]
            for i in range(1, N_DEV):
                acc = acc + ref[i]
            return acc

        a, cc, dd, ff, bb, ee = ssum(ga_ref), ssum(gc_ref), ssum(gd_ref), ssum(gf_ref), ssum(gb_ref), ssum(ge_ref)
        loss_ref[...] = a[3:4, 0:LANE]
        me = _slot(*_my_pos())

        def mine(rows):
            acc = jnp.zeros((rows.shape[0], LANE), F32)
            for j in range(N_DEV):
                acc = acc + jnp.where(me == j, rows[:, j * LANE:(j + 1) * LANE], 0.0)
            return acc

        sink_terms = ssum(gs_ref)
        lane8 = lax.broadcasted_iota(jnp.int32, (1, N_HEADS), 1)
        g_sink = jnp.zeros((1, N_HEADS), F32)
        for h in range(N_HEADS):
            tot = -jnp.sum(sink_terms[:, h * LANE:(h + 1) * LANE], axis=1, keepdims=True)
            g_sink = jnp.where(lane8 == h, tot, g_sink)
        grads = dict(
            ln_g=jnp.concatenate([dd[0:1], a[0:1]], axis=0), ln_b=jnp.concatenate([dd[1:2], a[1:2]], axis=0),
            ev_sg_ln_g=ee[0:1], ev_sg_ln_b=ee[1:2], ev_sink=g_sink,
            ev_sg_b=jnp.transpose(ssum(gbt_ref))[0:SG_GROUPS, :],
            od_conv_w=mine(cc[0:4]), od_conv_b=mine(cc[4:5]),
            od_b_a=mine(jnp.concatenate([ff[0:1], bb[0:1]], axis=0)),
            od_b_x=mine(jnp.concatenate([ff[1:2], bb[1:2]], axis=0)),
            od_lam=mine(jnp.concatenate([ff[2:3], bb[2:3]], axis=0)))
        for k, nm in enumerate(names):
            w_ref, m_ref, v_ref = prm[3 * k:3 * k + 3]
            at = (0,) if len(w_ref.shape) == 3 else ()
            g = grads[nm]
            dlt, m2, v2 = _adam(w_ref[at] if at else w_ref[...], g, m_ref[at] if at else m_ref[...],
                                v_ref[at] if at else v_ref[...])
            for o_ref, val in zip(outs[4 * k:4 * k + 4], (g, dlt, m2, v2)):
                if at:
                    o_ref[at] = val
                else:
                    o_ref[...] = val

    gathered = [ga, gc, gd, gf, gb, ge, gsink, gbt]
    out_shape = [jax.ShapeDtypeStruct((1, LANE), F32)]
    for nm in names:
        out_shape += [jax.ShapeDtypeStruct(params[nm][0].shape, F32)] * 4
    return _pallas(
        body, name="small_update", grid=(1,),
        in_specs=[_full(a.shape) for a in gathered + flat],
        out_specs=[_full(s.shape) for s in out_shape], out_shape=out_shape,
        compiler_params=_params(("arbitrary",)),
    )(*gathered, *flat)


VEC_ROWS = 16
VEC_LAYOUT = (("od_conv_w", 4), ("od_conv_b", 1), ("od_b_a", 2), ("od_b_x", 2), ("od_lam", 2))


def _from_slabs(slabs):
    n, R, cp = slabs.shape
    return slabs.transpose(1, 0, 2).reshape(R, n * cp)


def kernel(x, c, positions, ada_w, ada_b, ln_g, ln_b, ev_w_in, ev_w_out, ev_sink, ev_sg_ln_g, ev_sg_ln_b, ev_sg_w, ev_sg_b, od_w_in, od_conv_w, od_conv_b, od_w_a, od_b_a, od_w_x, od_b_x, od_lam, od_w_out, loss_target, m_ada_w, m_ada_b, m_ln_g, m_ln_b, m_ev_w_in, m_ev_w_out, m_ev_sink, m_ev_sg_ln_g, m_ev_sg_ln_b, m_ev_sg_w, m_ev_sg_b, m_od_w_in, m_od_conv_w, m_od_conv_b, m_od_w_a, m_od_b_a, m_od_w_x, m_od_b_x, m_od_lam, m_od_w_out, v_ada_w, v_ada_b, v_ln_g, v_ln_b, v_ev_w_in, v_ev_w_out, v_ev_sink, v_ev_sg_ln_g, v_ev_sg_ln_b, v_ev_sg_w, v_ev_sg_b, v_od_w_in, v_od_conv_w, v_od_conv_b, v_od_w_a, v_od_b_a, v_od_w_x, v_od_b_x, v_od_lam, v_od_w_out):
    T = x.shape[1]
    me = _slot(*_my_pos())
    xs = x.reshape(T, D)
    tgt = loss_target.reshape(T, D)

    c_all, mod_all, g_vec, (g_ev_in,), (s_ev_out, s_od_in, s_od_out, sg_w, wa, wx) = _head_gather(
        c, ada_w, [ev_w_in[0].T.astype(BF16)],
        [ev_w_out[0], od_w_in[0], od_w_out[0], ev_sg_w[0], od_w_a[0], od_w_x[0]],
        [od_conv_w, od_conv_b, od_b_a, od_b_x, od_lam])
    c_all = c_all.reshape(N_DEV, D)
    w_ev_in = g_ev_in.reshape(EV_IN, D)
    vec_full = _from_slabs(g_vec)
    cw, cb = vec_full[0:4], vec_full[4:5]
    ba, bx, lam = vec_full[5:7], vec_full[7:9], vec_full[9:11]
    mod_mine = lax.dynamic_index_in_dim(mod_all, me, axis=2, keepdims=False)
    mod = mod_mine.transpose(1, 0, 2).reshape(2, 3 * D) + ada_b
    mod0 = mod[0].reshape(3, D)
    mod1 = mod[1].reshape(3, D)

    half = 8
    inv_freq = jnp.power(jnp.float32(ROPE_THETA), -jnp.arange(half, dtype=F32) / half)
    ang = positions.reshape(T).astype(F32)[:, None] * inv_freq
    cos_t = jnp.tile(jnp.cos(ang), (1, LANE // half))
    sin_t = jnp.tile(jnp.sin(ang), (1, LANE // half))
    l64 = jnp.arange(LANE) % HEAD_DIM
    rc = jnp.where(l64 < 2 * half, cos_t, 1.0)
    rs1 = jnp.where(l64 < half, -sin_t, 0.0)
    rs2 = jnp.where((l64 >= half) & (l64 < 2 * half), sin_t, 0.0)

    ln0 = jnp.stack([ln_g[0], ln_b[0]])
    ln1 = jnp.stack([ln_g[1], ln_b[1]])
    sg_lng = ev_sg_ln_g
    sg_lnb = ev_sg_ln_b
    sg_bfull = jnp.repeat(ev_sg_b[0].T, SG_DIM, axis=1)
    sink_l = jnp.repeat(ev_sink, LANE, axis=1)
    kj = jnp.arange(3 * BLK)[:, None]
    qi = jnp.arange(BLK)[None, :]
    band_bias = jnp.where(jnp.abs(kj - BLK - qi) <= BLK, 0.0, NEG_INF).astype(F32)
    lanes = jnp.arange(LANE)
    lanes2 = jnp.arange(2 * LANE)
    a128 = jnp.where(lanes2[:, None] // SG_DIM == lanes2[None, :] // SG_DIM, 1.0 / SG_DIM, 0.0).astype(BF16)
    gsum = (jnp.arange(SG_W)[:, None] // SG_DIM == lanes[None, :]).astype(BF16)
    sel = (jnp.arange(SUBLANE)[:, None] == lanes[None, :] // HEAD_DIM).astype(BF16)

    (q, kvx, su, sv, g0), _ = _ev_in(xs, mod0, w_ev_in, rc, rs1, rs2)
    (ycat, y0, lse), (g_ev_out, g_od_in, g_od_out) = _mix0_fwd(
        q, kvx, su, sv, g0, sink_l, band_bias, a128, sg_lng, sg_lnb, sg_w, sg_bfull,
        _GatherComm([s_ev_out, s_od_in, s_od_out], mid_frac=0.75))
    w_ev_out = g_ev_out.reshape(D, D)
    w_od_in = _from_slabs(g_od_in)
    w_od_out = g_od_out.reshape(D, D)
    out0, z0, x1 = _ev_out(y0, w_ev_out, xs, mod0, ln0)
    xr, g1 = _od_in(x1, mod1, w_od_in)
    fwd_f = _rglru_fwd(xr, cw, cb, wa[0], wx[0], ba[0:1], bx[0:1], lam[0:1], False, "rglru_fwd_f")
    fwd_b = _rglru_fwd(xr, cw, cb, wa[1], wx[1], ba[1:2], bx[1:2], lam[1:2], True, "rglru_fwd_b")
    dh, dg1, dx1p, d_od_out, vec_a = _od_out(fwd_f[0], fwd_b[0], g1, w_od_out, x1, tgt, mod1, ln1)

    (dxcf, dwa_f, dwx_f, vec_f), (l_od_out,) = _rglru_bwd(
        fwd_f, dh, wa[0], wx[0], lam[0:1], False, "rglru_bwd_f",
        _ExchangeComm([d_od_out.reshape(N_DEV, D // N_DEV, D)]))
    (dxcb, dwa_b, dwx_b, vec_b), _ = _rglru_bwd(fwd_b, dh, wa[1], wx[1], lam[1:2], True, "rglru_bwd_b")
    (dx1, d_od_in, vec_c), (a_wa, a_wx) = _od_in_bwd(
        dxcf, dxcb, xr, dg1, x1, dx1p, mod1, w_od_in, cw,
        _GatherComm([jnp.stack([dwa_f, dwa_b]).astype(BF16), jnp.stack([dwx_f, dwx_b]).astype(BF16)],
                    mid_frac=0.75))
    dxp, dyc, dg0, d_ev_out, vec_d = _ev_out_bwd(dx1, z0, out0, y0, ycat, g0, w_ev_out, mod0, ln0)
    (dq, dkv, dsu, dsv, d_sg_w, d_sg_bt, vec_e, d_sink_l), (l_od_in, l_ev_out, ga, gc, gd, gf, gb) = _mix0_bwd(
        q, kvx, lse, dyc, ycat, su, sv, sink_l, band_bias, a128, gsum, sel, sg_lng, sg_lnb, sg_w, sg_bfull,
        rc, rs1, rs2, _BothComm(_ExchangeComm([d_od_in, d_ev_out.reshape(N_DEV, D // N_DEV, D)]),
                                _GatherComm([vec_a, vec_c, vec_d, vec_f, vec_b], mid_frac=0.9)))
    (grad_x, d_ev_in, vec_g), _ = _ev_in_bwd(dq, dkv, dsu, dsv, dg0, xs, dxp, mod0, w_ev_in, rc, rs1, rs2)

    part, land, (gg, ge, gsink, gbt, a_sgw) = _tail_stage1(
        d_ev_in.reshape(N_DEV, EV_IN // N_DEV, D), [vec_g, vec_e, d_sink_l, d_sg_bt, d_sg_w.astype(BF16)])
    send_sems, recv_sems, part, land, token = _tail_send(part, land)

    dmod_all = jnp.stack([jnp.concatenate([gg[:, 0], gg[:, 1], gd[:, 2]], axis=-1),
                          jnp.concatenate([gc[:, 5], gc[:, 6], ga[:, 2]], axis=-1)], axis=1)
    cols = ada_w.shape[2]
    dmod_cols = lax.dynamic_slice_in_dim(dmod_all, me * cols, cols, axis=2).transpose(1, 0, 2)
    (g_ada_w, d_ada_w, nm_ada_w, nv_ada_w, g_ada_b, d_ada_b, nm_ada_b, nv_ada_b) = _ada_update(
        c_all, dmod_cols, dmod_all, ada_w, m_ada_w, v_ada_w, ada_b, m_ada_b, v_ada_b)

    res = dict(ada_w=[g_ada_w, d_ada_w, nm_ada_w, nv_ada_w], ada_b=[g_ada_b, d_ada_b, nm_ada_b, nv_ada_b])
    (r_od_in,) = _reduce_adam([(l_od_in, od_w_in[0], m_od_w_in[0], v_od_w_in[0])], "adam_od_w_in", after=[token])
    r_ev_out, r_od_out = _reduce_adam([(l_ev_out, ev_w_out[0], m_ev_w_out[0], v_ev_w_out[0]),
                                       (l_od_out, od_w_out[0], m_od_w_out[0], v_od_w_out[0])], "adam_w_out",
                                      after=[token])
    for name, r in (("od_w_in", r_od_in), ("ev_w_out", r_ev_out), ("od_w_out", r_od_out)):
        res[name] = [a[None] for a in r]
    res["od_w_a"], res["od_w_x"], res["ev_sg_w"] = _slots_adam(
        [(a_wa, od_w_a, m_od_w_a, v_od_w_a), (a_wx, od_w_x, m_od_w_x, v_od_w_x),
         (a_sgw, ev_sg_w, m_ev_sg_w, v_ev_sg_w)], "adam_gates", after=[token])
    small = dict(ln_g=(ln_g, m_ln_g, v_ln_g), ln_b=(ln_b, m_ln_b, v_ln_b),
                 ev_sg_ln_g=(ev_sg_ln_g, m_ev_sg_ln_g, v_ev_sg_ln_g),
                 ev_sg_ln_b=(ev_sg_ln_b, m_ev_sg_ln_b, v_ev_sg_ln_b),
                 ev_sink=(ev_sink, m_ev_sink, v_ev_sink), ev_sg_b=(ev_sg_b, m_ev_sg_b, v_ev_sg_b),
                 od_conv_w=(od_conv_w, m_od_conv_w, v_od_conv_w), od_conv_b=(od_conv_b, m_od_conv_b, v_od_conv_b),
                 od_b_a=(od_b_a, m_od_b_a, v_od_b_a), od_b_x=(od_b_x, m_od_b_x, v_od_b_x),
                 od_lam=(od_lam, m_od_lam, v_od_lam))
    small_out = _small_update(ga, gc, gd, gf, gb, ge, gsink, gbt, small)
    l_ev_in = _tail_wait(send_sems, recv_sems, part, land,
                         [r_od_in[0], r_od_out[0], res["od_w_x"][0], g_ada_w, small_out[0]])
    (r_ev_in,) = _reduce_adam([(l_ev_in, ev_w_in[0].T, m_ev_w_in[0].T, v_ev_w_in[0].T)], "adam_ev_w_in")
    res["ev_w_in"] = [a.T[None] for a in r_ev_in]
    loss = small_out[0][0, 0]
    for k, name in enumerate(SMALL_PARAMS):
        res[name] = small_out[1 + 4 * k:5 + 4 * k]

    order = ["ada_w", "ada_b", "ln_g", "ln_b", "ev_w_in", "ev_w_out", "ev_sink", "ev_sg_ln_g", "ev_sg_ln_b",
             "ev_sg_w", "ev_sg_b", "od_w_in", "od_conv_w", "od_conv_b", "od_w_a", "od_b_a", "od_w_x", "od_b_x",
             "od_lam", "od_w_out"]
    outs = [loss, grad_x.reshape(1, T, D)]
    for kind in range(4):
        outs += [res[name][kind] for name in order]
    return tuple(outs)
```

```python
import jax
import jax.numpy as jnp
from jax import lax
from jax.experimental import pallas as pl
from jax.experimental.pallas import tpu as pltpu

F32 = jnp.float32
BF16 = jnp.bfloat16

N_DEV = 8
D = 1024
N_HEADS = 8
HEAD_DIM = 64
ATTN_W = 512
SG_W = 512
SG_GROUPS = 8
SG_DIM = 64
BLK = 128
KVX_W = 1024
EV_IN = 2816
OD_IN = 2048
RNN_HEADS = 8
RNN_HD = 128
ALPHA = 4.0 ** 0.25
LN_EPS = 1e-5
NEG_INF = -1e30
RG_C = 8.0
ROPE_THETA = 500000.0
LR, B1, B2, EPS, WD, STEP = 0.001, 0.9, 0.999, 1e-08, 0.01, 10

LANE = 128
SUBLANE = 8
TM = 256
TMF = 512
TMO = 512
TS = 256
FWD_BLOCKS = 4
BWD_BLOCKS = 4
VMEM_LIMIT = 56 * 1024 * 1024

MESH = pl.DeviceIdType.MESH


def _pallas(body, **kw):
    return pl.pallas_call(body, **kw)


def _params(sem, vmem=VMEM_LIMIT):
    return pltpu.CompilerParams(dimension_semantics=sem, vmem_limit_bytes=vmem)


def _sigmoid(x):
    return 0.5 * jnp.tanh(0.5 * x) + 0.5


def _silu_and_grad(x):
    s = _sigmoid(x)
    return x * s, s * (1.0 + x * (1.0 - s))


def _dot(a, b):
    return jnp.dot(a.astype(BF16), b.astype(BF16), preferred_element_type=F32)


def _dot_nt(a, b):
    return lax.dot_general(a.astype(BF16), b.astype(BF16), (((1,), (1,)), ((), ())), preferred_element_type=F32)


def _dot_tn(a, b):
    return lax.dot_general(a.astype(BF16), b.astype(BF16), (((0,), (0,)), ((), ())), preferred_element_type=F32)


def _ln_fwd(z, g, b):
    mu = jnp.mean(z, axis=-1, keepdims=True)
    zc = z - mu
    var = jnp.mean(zc * zc, axis=-1, keepdims=True)
    rstd = lax.rsqrt(var + LN_EPS)
    xhat = zc * rstd
    return xhat * g + b, xhat, rstd


def _ln_bwd(dy, xhat, rstd, g):
    dxh = dy * g
    m1 = jnp.mean(dxh, axis=-1, keepdims=True)
    m2 = jnp.mean(dxh * xhat, axis=-1, keepdims=True)
    return rstd * (dxh - m1 - xhat * m2)


def _rowsum(v):
    return jnp.sum(v, axis=0, keepdims=True)


def _rope_fwd(t, c, s1, s2):
    return t * c + pltpu.roll(t, LANE - 8, 1) * s1 + pltpu.roll(t, 8, 1) * s2


def _rope_bwd(d, c, s1, s2):
    return d * c + pltpu.roll(d * s1, 8, 1) + pltpu.roll(d * s2, LANE - 8, 1)


def _adam(w, g, m, v):
    m2 = B1 * m + (1.0 - B1) * g
    v2 = B2 * v + (1.0 - B2) * (g * g)
    m_hat = m2 / (1.0 - B1 ** STEP)
    v_hat = v2 / (1.0 - B2 ** STEP)
    delta = -LR * (m_hat / (jnp.sqrt(v_hat) + EPS) + WD * w)
    return delta, m2, v2


def _tile(rows, width):
    return pl.BlockSpec((rows, width), lambda i: (i, 0))


def _full(shape):
    zeros = (0,) * len(shape)
    return pl.BlockSpec(shape, lambda i: zeros)


def _rev_tile(rows, width, n, reverse):
    if reverse:
        return pl.BlockSpec((rows, width), lambda i: (n - 1 - i, 0))
    return pl.BlockSpec((rows, width), lambda i: (i, 0))


def _halo_specs(rows, width, n, total_rows, reverse):
    per = rows // SUBLANE
    last = total_rows // SUBLANE - 1

    def tile_of(i):
        return (n - 1 - i) if reverse else i

    prev = pl.BlockSpec((SUBLANE, width), lambda i: (jnp.maximum(tile_of(i) * per - 1, 0), 0))
    nxt = pl.BlockSpec((SUBLANE, width), lambda i: (jnp.minimum((tile_of(i) + 1) * per, last), 0))
    return prev, nxt


def _my_pos():
    return lax.axis_index("x"), lax.axis_index("y"), lax.axis_index("c")


def _slot(px, py, pc):
    return 4 * px + 2 * py + pc


class _GatherComm:
    has_mid = True

    def __init__(self, arrs, mid_frac=0.5):
        self.arrs = list(arrs)
        self.n = len(self.arrs)
        self.mid_frac = mid_frac

    def out_shapes(self):
        return [jax.ShapeDtypeStruct((N_DEV,) + a.shape, a.dtype) for a in self.arrs]

    def sems(self):
        return [pltpu.SemaphoreType.DMA((7 * self.n,)), pltpu.SemaphoreType.DMA((7 * self.n,)),
                pltpu.SemaphoreType.DMA((self.n,))]

    def _parts(self, ins, outs, sems):
        send_sems, recv_sems, local_sems = sems
        x, y, c = _my_pos()
        me, sibling = (x, y, c), (x, y, 1 - c)
        chips = [(1 - x, y), (x, 1 - y), (1 - x, 1 - y)]

        def copy(a, k, block, to, src=None):
            dst = outs[a].at[_slot(*block)]
            return pltpu.make_async_remote_copy(
                src_ref=dst if src is None else src, dst_ref=dst,
                send_sem=send_sems.at[a * 7 + k], recv_sem=recv_sems.at[a * 7 + k],
                device_id=to, device_id_type=MESH)

        local = [pltpu.make_async_copy(ins[a], outs[a].at[_slot(*me)], local_sems.at[a]) for a in range(self.n)]
        first = []
        for a in range(self.n):
            first.append(copy(a, 0, me, sibling, src=ins[a]))
            first += [copy(a, 1 + j, me, (*chip, c), src=ins[a]) for j, chip in enumerate(chips)]
        ici_in = [copy(a, 1 + j, (*chip, c), me) for j, chip in enumerate(chips) for a in range(self.n)]
        passed = [copy(a, 4 + j, (*chip, c), sibling) for j, chip in enumerate(chips) for a in range(self.n)]
        d2d_in = []
        for a in range(self.n):
            d2d_in.append(copy(a, 0, sibling, me))
            d2d_in += [copy(a, 4 + j, (*chip, 1 - c), me) for j, chip in enumerate(chips)]
        return local, first, ici_in, passed, d2d_in

    def start(self, ins, outs, sems):
        local, first, _, _, _ = self._parts(ins, outs, sems)
        for cp in local + first:
            cp.start()

    def mid(self, ins, outs, sems):
        _, _, ici_in, passed, _ = self._parts(ins, outs, sems)
        for arrived, fw in zip(ici_in, passed):
            arrived.wait_recv()
            fw.start()

    def finish(self, ins, outs, sems):
        local, first, _, passed, d2d_in = self._parts(ins, outs, sems)
        for cp in d2d_in:
            cp.wait_recv()
        for cp in first + passed:
            cp.wait_send()
        for cp in local:
            cp.wait()


class _ExchangeComm:
    has_mid = False

    def __init__(self, arrs):
        self.arrs = list(arrs)
        self.n = len(self.arrs)

    def out_shapes(self):
        return [jax.ShapeDtypeStruct(a.shape, a.dtype) for a in self.arrs]

    def sems(self):
        return [pltpu.SemaphoreType.DMA((7 * self.n,)), pltpu.SemaphoreType.DMA((7 * self.n,)),
                pltpu.SemaphoreType.DMA((self.n,))]

    def _copies(self, ins, outs, sems):
        send_sems, recv_sems, local_sems = sems
        x, y, c = _my_pos()
        mine = _slot(x, y, c)
        copies = [pltpu.make_async_copy(ins[a].at[mine], outs[a].at[mine], local_sems.at[a]) for a in range(self.n)]
        for k in range(1, N_DEV):
            px = (1 - x) if (k & 4) else x
            py = (1 - y) if (k & 2) else y
            pc = (1 - c) if (k & 1) else c
            for a in range(self.n):
                copies.append(pltpu.make_async_remote_copy(
                    src_ref=ins[a].at[_slot(px, py, pc)], dst_ref=outs[a].at[mine],
                    send_sem=send_sems.at[a * 7 + k - 1], recv_sem=recv_sems.at[a * 7 + k - 1],
                    device_id=(px, py, pc), device_id_type=MESH))
        return copies

    def start(self, ins, outs, sems):
        for cp in self._copies(ins, outs, sems):
            cp.start()

    def finish(self, ins, outs, sems):
        for cp in self._copies(ins, outs, sems):
            cp.wait()


class _BothComm:
    has_mid = True

    def __init__(self, first, second):
        self.parts = (first, second)
        self.arrs = first.arrs + second.arrs
        self.n = first.n + second.n
        self.mid_frac = second.mid_frac

    def out_shapes(self):
        return self.parts[0].out_shapes() + self.parts[1].out_shapes()

    def sems(self):
        return self.parts[0].sems() + self.parts[1].sems()

    def _each(self, ins, outs, sems):
        a, b = self.parts
        return ((a, ins[:a.n], outs[:a.n], sems[:3]), (b, ins[a.n:], outs[a.n:], sems[3:]))

    def start(self, ins, outs, sems):
        for cm, i_, o_, s_ in self._each(ins, outs, sems):
            cm.start(i_, o_, s_)

    def mid(self, ins, outs, sems):
        for cm, i_, o_, s_ in self._each(ins, outs, sems):
            if cm.has_mid:
                cm.mid(i_, o_, s_)

    def finish(self, ins, outs, sems):
        for cm, i_, o_, s_ in self._each(ins, outs, sems):
            cm.finish(i_, o_, s_)


def _fused_call(body, comm, operands, *, name, grid, in_specs, out_specs, out_shape, scratch_shapes=(),
                semantics=("arbitrary",)):
    n_in, n_out, n_scr = len(in_specs), len(out_specs), len(scratch_shapes)
    if comm is None:
        res = _pallas(body, name=name, grid=grid, in_specs=list(in_specs), out_specs=list(out_specs),
                      out_shape=list(out_shape), scratch_shapes=list(scratch_shapes),
                      compiler_params=_params(semantics))(*operands)
        return list(res), []
    k = comm.n
    steps = grid[0]

    def wrapped(*refs):
        ins, cins = refs[:n_in], refs[n_in:n_in + k]
        outs = refs[n_in + k:n_in + k + n_out]
        couts = refs[n_in + k + n_out:n_in + 2 * k + n_out]
        rest = refs[n_in + 2 * k + n_out:]
        scratch, sems = rest[:n_scr], rest[n_scr:]
        i = pl.program_id(0)

        @pl.when(i == 0)
        def _():
            comm.start(cins, couts, sems)

        body(*ins, *outs, *scratch)

        if comm.has_mid:
            @pl.when(i == int(steps * comm.mid_frac))
            def _():
                comm.mid(cins, couts, sems)

        @pl.when(i == steps - 1)
        def _():
            comm.finish(cins, couts, sems)

    any_spec = pl.BlockSpec(memory_space=pl.ANY)
    res = _pallas(wrapped, name=name, grid=grid, in_specs=list(in_specs) + [any_spec] * k,
                  out_specs=list(out_specs) + [any_spec] * k, out_shape=list(out_shape) + comm.out_shapes(),
                  scratch_shapes=list(scratch_shapes) + comm.sems(),
                  compiler_params=_params(("arbitrary",)))(*operands, *comm.arrs)
    return list(res[:n_out]), list(res[n_out:])


def _head_gather(c, ada_w, big, to_cast, vec_parts):
    cols = ada_w.shape[2]
    g_c, g_big = _GatherComm([c]), _GatherComm(big)
    g_mod = _GatherComm([jax.ShapeDtypeStruct((2, N_DEV, cols), F32)])
    g_vec = _GatherComm([jax.ShapeDtypeStruct((VEC_ROWS, LANE), F32)])
    nb, nc, nv = g_big.n, len(to_cast), len(vec_parts)

    def body(*refs):
        c_ref, w_ref = refs[0], refs[1]
        vec_in = refs[2:2 + nv]
        cast_in = refs[2 + nv:2 + nv + nc]
        big_in = refs[2 + nv + nc:2 + nv + nc + nb]
        outs = refs[2 + nv + nc + nb:]
        c_all_ref, mod_all_ref, vec_all_ref = outs[0], outs[1], outs[2]
        cast_out = outs[3:3 + nc]
        big_out = outs[3 + nc:3 + nc + nb]
        part_ref, pack_ref = outs[3 + nc + nb], outs[4 + nc + nb]
        sems = outs[5 + nc + nb:]
        s_c, s_mod, s_big, s_vec = sems[0:3], sems[3:6], sems[6:9], sems[9:12]
        g_c.start([c_ref], [c_all_ref], s_c)
        g_big.start(big_in, big_out, s_big)
        pack_ref[...] = jnp.zeros_like(pack_ref)
        row = 0
        for ref, (_, nrows) in zip(vec_in, VEC_LAYOUT):
            pack_ref[row:row + nrows, :] = ref[0] if len(ref.shape) == 3 else ref[...]
            row += nrows
        g_vec.start([pack_ref], [vec_all_ref], s_vec)
        g_c.mid([c_ref], [c_all_ref], s_c)
        g_c.finish([c_ref], [c_all_ref], s_c)
        cv = c_all_ref[:, 0, :]
        cond = cv * _sigmoid(cv)
        for l in range(2):
            part_ref[l] = _dot(cond, w_ref[l])
        g_mod.start([part_ref], [mod_all_ref], s_mod)
        for src, dst in zip(cast_in, cast_out):
            dst[...] = src[...].astype(BF16)
        for g, ins, outs_, sm in ((g_vec, [pack_ref], [vec_all_ref], s_vec), (g_mod, [part_ref], [mod_all_ref], s_mod),
                                  (g_big, big_in, big_out, s_big)):
            g.mid(ins, outs_, sm)
            g.finish(ins, outs_, sm)

    any_spec = pl.BlockSpec(memory_space=pl.ANY)
    vmem_spec = pl.BlockSpec(memory_space=pltpu.VMEM)
    res = _pallas(
        body, name="head_gather",
        out_shape=(g_c.out_shapes() + g_mod.out_shapes() + g_vec.out_shapes()
                   + [jax.ShapeDtypeStruct(a.shape, BF16) for a in to_cast] + g_big.out_shapes()),
        in_specs=[vmem_spec] * (2 + nv + nc) + [any_spec] * nb,
        out_specs=[vmem_spec] * (3 + nc) + [any_spec] * nb,
        scratch_shapes=[pltpu.VMEM((2, N_DEV, cols), F32), pltpu.VMEM((VEC_ROWS, LANE), F32)]
        + g_c.sems() + g_mod.sems() + g_big.sems() + g_vec.sems(),
        compiler_params=pltpu.CompilerParams(vmem_limit_bytes=VMEM_LIMIT),
    )(c, ada_w, *vec_parts, *to_cast, *big)
    return res[0], res[1], res[2], list(res[3 + nc:]), list(res[3:3 + nc])


def _ada_update(c_all, dmod_cols, dmod_all, ada_w, m_w, v_w, ada_b, m_b, v_b):
    cols = ada_w.shape[2]
    nb = ada_b.shape[1]

    def body(c_ref, dmc_ref, dma_ref, w_ref, mw_ref, vw_ref, b_ref, mb_ref, vb_ref,
             gw_ref, dw_ref, nmw_ref, nvw_ref, gb_ref, db_ref, nmb_ref, nvb_ref):
        cv = c_ref[...]
        cond = cv * _sigmoid(cv)
        for l in range(2):
            g = _dot_tn(cond, dmc_ref[l])
            gw_ref[l] = g
            dlt, m2, v2 = _adam(w_ref[l], g, mw_ref[l], vw_ref[l])
            dw_ref[l] = dlt
            nmw_ref[l] = m2
            nvw_ref[l] = v2
        gb = dma_ref[0]
        for i in range(1, N_DEV):
            gb = gb + dma_ref[i]
        gb_ref[...] = gb
        dlt, m2, v2 = _adam(b_ref[...], gb, mb_ref[...], vb_ref[...])
        db_ref[...] = dlt
        nmb_ref[...] = m2
        nvb_ref[...] = v2

    wspec = _full((2, D, cols))
    bspec = _full((2, nb))
    wshape = jax.ShapeDtypeStruct((2, D, cols), F32)
    bshape = jax.ShapeDtypeStruct((2, nb), F32)
    return _pallas(
        body, name="ada_update", grid=(1,),
        in_specs=[_full((N_DEV, D)), _full((2, N_DEV, cols)), _full((N_DEV, 2, nb)),
                  wspec, wspec, wspec, bspec, bspec, bspec],
        out_specs=[wspec] * 4 + [bspec] * 4,
        out_shape=[wshape] * 4 + [bshape] * 4,
        compiler_params=_params(("arbitrary",)),
    )(c_all, dmod_cols, dmod_all, ada_w, m_w, v_w, ada_b, m_b, v_b)


def _ev_in(x, mod, w_in, rc, rs1, rs2, comm=None):
    T = x.shape[0]

    def body(x_ref, mod_ref, w_ref, c_ref, s1_ref, s2_ref, q_ref, kv_ref, su_ref, sv_ref, g_ref):
        h = x_ref[...] * (1.0 + mod_ref[1:2, :]) + mod_ref[0:1, :]
        p = _dot_nt(h, w_ref[...])
        c, s1, s2 = c_ref[...], s1_ref[...], s2_ref[...]
        for j in range(ATTN_W // LANE):
            qr = _rope_fwd(p[:, j * LANE:(j + 1) * LANE], c, s1, s2)
            q_ref[:, j * LANE:(j + 1) * LANE] = (qr * (HEAD_DIM ** -0.5)).astype(BF16)
        low = lax.broadcasted_iota(jnp.int32, (TMF, LANE), 1) < HEAD_DIM
        for j, val in enumerate((_rope_fwd(p[:, 512:640], c, s1, s2), p[:, 640:768])):
            swapped = pltpu.roll(val, HEAD_DIM, 1)
            tiles = (jnp.where(low, val, 0.0), jnp.where(low, 0.0, swapped),
                     jnp.where(low, swapped, 0.0), jnp.where(low, 0.0, val))
            for k, tile in enumerate(tiles):
                kv_ref[:, (4 * j + k) * LANE:(4 * j + k + 1) * LANE] = tile.astype(BF16)
        su_ref[...] = p[:, 768:1280].astype(BF16)
        sv_ref[...] = p[:, 1280:1792].astype(BF16)
        g_ref[...] = p[:, 1792:2816].astype(BF16)

    sh = lambda w: jax.ShapeDtypeStruct((T, w), BF16)
    return _fused_call(
        body, comm, (x, mod, w_in, rc, rs1, rs2), name="ev_in", grid=(T // TMF,),
        in_specs=[_tile(TMF, D), _full((3, D)), _full((EV_IN, D)), _tile(TMF, LANE), _tile(TMF, LANE),
                  _tile(TMF, LANE)],
        out_specs=[_tile(TMF, ATTN_W), _tile(TMF, KVX_W), _tile(TMF, SG_W), _tile(TMF, SG_W), _tile(TMF, D)],
        out_shape=[sh(ATTN_W), sh(KVX_W), sh(SG_W), sh(SG_W), sh(D)], semantics=("parallel",))


def _band_specs(width, nb, k):
    return [pl.BlockSpec((BLK, width), lambda n: (jnp.maximum(k * n - 1, 0), 0)),
            pl.BlockSpec((k * BLK, width), lambda n: (n, 0)),
            pl.BlockSpec((BLK, width), lambda n: (jnp.minimum(k * n + k, nb - 1), 0))]


def _band_bias(bias_ref, n, nb):
    rows = lax.broadcasted_iota(jnp.int32, (3 * BLK, 1), 0)
    outside = ((rows < BLK) & (n == 0)) | ((rows >= 2 * BLK) & (n == nb - 1))
    return bias_ref[...] + jnp.where(outside, NEG_INF, 0.0)


def _lane_tile(ref, t):
    return ref[:, t * LANE:(t + 1) * LANE]


def _split_bf16(v):
    hi = v.astype(BF16)
    return hi, (v - hi.astype(F32)).astype(BF16)


def _group_mean(v, a_ref, exact_bf16=False):
    hi, lo = _split_bf16(v)
    a = a_ref[...]
    out = []
    for t in range(SG_W // (2 * LANE)):
        sl = slice(t * 2 * LANE, (t + 1) * 2 * LANE)
        r = jnp.dot(hi[:, sl], a, preferred_element_type=F32)
        if not exact_bf16:
            r = r + jnp.dot(lo[:, sl], a, preferred_element_type=F32)
        out.append(r)
    return jnp.concatenate(out, axis=-1)


def _sg_core(sv_ref, lng, lnb, a_ref, w_ref, bfull_ref):
    svf = sv_ref[...].astype(F32)
    xc = svf - _group_mean(svf, a_ref, exact_bf16=True)
    rstd = lax.rsqrt(_group_mean(xc * xc, a_ref) + LN_EPS)
    xhat = xc * rstd
    vb = (xhat * lng + lnb).astype(BF16)
    low = lax.broadcasted_iota(jnp.int32, (BLK, LANE), 1) < SG_DIM
    tiles = []
    for t in range(SG_W // LANE):
        v2 = vb[:, t * LANE:(t + 1) * LANE]
        r0 = jnp.dot(w_ref[2 * t], v2, preferred_element_type=F32)
        r1 = jnp.dot(w_ref[2 * t + 1], v2, preferred_element_type=F32)
        tiles.append(jnp.where(low, r0, r1))
    svm = jnp.concatenate(tiles, axis=-1) + bfull_ref[...]
    return xhat, rstd, vb, svm


def _mix0_fwd(q, kvx, su, sv, g0, sink_l, bias, a128, sg_lng, sg_lnb, sg_w, sg_bfull, comm=None):
    T = q.shape[0]
    nb = T // BLK

    def body(q_ref, kp_ref, kc_ref, kn_ref, su_ref, sv_ref, g_ref, sink_ref, bias_ref, a_ref, lng_ref, lnb_ref,
             w_ref, bfull_ref, ycat_ref, y0_ref, lse_ref):
        n = pl.program_id(0)
        kvx4 = jnp.concatenate([kp_ref[...], kc_ref[...], kn_ref[...]], axis=0)
        for s in range(FWD_BLOCKS):
            rows = slice(s * BLK, (s + 1) * BLK)
            bias = _band_bias(bias_ref, FWD_BLOCKS * n + s, nb)
            kvx = kvx4[s * BLK:s * BLK + 3 * BLK]
            tiles = []
            for t in range(ATTN_W // LANE):
                qt = q_ref[rows, t * LANE:(t + 1) * LANE]
                acc = None
                for par in range(2):
                    h = 2 * t + par
                    kt = 2 * (h // 4) + par
                    ke = kvx[:, kt * LANE:(kt + 1) * LANE]
                    ve = kvx[:, (4 + kt) * LANE:(5 + kt) * LANE]
                    st = _dot_nt(ke, qt) + bias
                    sk = _lane_tile(sink_ref, h)
                    m = jnp.maximum(jnp.max(st, axis=0, keepdims=True), sk)
                    p = jnp.exp(st - m)
                    denom = jnp.sum(p, axis=0, keepdims=True) + jnp.exp(sk - m)
                    contrib = _dot_tn(p * (1.0 / denom), ve)
                    acc = contrib if acc is None else acc + contrib
                    lse_ref[s, :, h * LANE:(h + 1) * LANE] = m + jnp.log(denom)
                tiles.append(acc)
            _, _, _, svm = _sg_core(sv_ref.at[rows, :], lng_ref[...], lnb_ref[...], a_ref, w_ref, bfull_ref)
            tiles.append(su_ref[rows, :].astype(F32) * svm)
            ycat = jnp.concatenate(tiles, axis=-1)
            gf = g_ref[rows, :].astype(F32)
            ycat_ref[rows, :] = ycat.astype(BF16)
            y0_ref[rows, :] = (ycat * (gf * _sigmoid(gf))).astype(BF16)

    two = FWD_BLOCKS * BLK
    return _fused_call(
        body, comm, (q, kvx, kvx, kvx, su, sv, g0, sink_l, bias, a128, sg_lng, sg_lnb, sg_w, sg_bfull),
        name="mix0_fwd", grid=(nb // FWD_BLOCKS,),
        in_specs=[_tile(two, ATTN_W)] + _band_specs(KVX_W, nb, FWD_BLOCKS) + [
                  _tile(two, SG_W), _tile(two, SG_W), _tile(two, D), _full((1, N_HEADS * LANE)),
                  _full((3 * BLK, LANE)), _full((2 * LANE, 2 * LANE)),_full((1, SG_W)), _full((1, SG_W)),
                  _full((SG_GROUPS, BLK, BLK)), _full((BLK, SG_W))],
        out_specs=[_tile(two, D), _tile(two, D),
                   pl.BlockSpec((FWD_BLOCKS, 1, N_HEADS * LANE), lambda n: (n, 0, 0))],
        out_shape=[jax.ShapeDtypeStruct((T, D), BF16), jax.ShapeDtypeStruct((T, D), BF16),
                   jax.ShapeDtypeStruct((nb, 1, N_HEADS * LANE), F32)], semantics=("parallel",))


def _ev_out(y0, w_out, x, mod, lnp):
    T = x.shape[0]

    def body(y_ref, w_ref, x_ref, mod_ref, ln_ref, out_ref, z_ref, x1_ref):
        out = _dot(y_ref[...], w_ref[...])
        z = ALPHA * x_ref[...] + mod_ref[2:3, :] * out
        x1, _, _ = _ln_fwd(z, ln_ref[0:1, :], ln_ref[1:2, :])
        out_ref[...] = out.astype(BF16)
        z_ref[...] = z
        x1_ref[...] = x1

    return _pallas(
        body, name="ev_out", grid=(T // TMF,),
        in_specs=[_tile(TMF, D), _full((D, D)), _tile(TMF, D), _full((3, D)), _full((2, D))],
        out_specs=[_tile(TMF, D)] * 3,
        out_shape=[jax.ShapeDtypeStruct((T, D), BF16), jax.ShapeDtypeStruct((T, D), F32),
                   jax.ShapeDtypeStruct((T, D), F32)],
        compiler_params=_params(("parallel",)),
    )(y0, w_out, x, mod, lnp)


def _od_in(x1, mod, w_in):
    T = x1.shape[0]

    def body(x_ref, mod_ref, w_ref, xr_ref, g_ref):
        h = x_ref[...] * (1.0 + mod_ref[1:2, :]) + mod_ref[0:1, :]
        p = _dot(h, w_ref[...])
        xr_ref[...] = p[:, :D]
        g_ref[...] = p[:, D:].astype(BF16)

    return _pallas(
        body, name="od_in", grid=(T // TMF,),
        in_specs=[_tile(TMF, D), _full((3, D)), _full((D, OD_IN))],
        out_specs=[_tile(TMF, D), _tile(TMF, D)],
        out_shape=[jax.ShapeDtypeStruct((T, D), F32), jax.ShapeDtypeStruct((T, D), BF16)],
        compiler_params=_params(("parallel",)),
    )(x1, mod, w_in)


def _ext_rows(prev_ref, cur, next_ref, j, n):
    prev = jnp.where(j > 0, prev_ref[...], 0.0)
    nxt = jnp.where(j < n - 1, next_ref[...], 0.0)
    return jnp.concatenate([prev, cur, nxt], axis=0)


def _shift_rows(ext, off, rows):
    total = ext.shape[0]
    if off == 0:
        return ext[SUBLANE:SUBLANE + rows, :]
    return pltpu.roll(ext, (-off) % total, 0)[SUBLANE:SUBLANE + rows, :]


def _conv_fwd(ext, cw, cb, rows):
    xc = cb
    for k in range(4):
        xc = xc + cw[k:k + 1, :] * _shift_rows(ext, k - 2, rows)
    return xc


def _gates(xc, wa_ref, wx_ref, ba, bx, lam):
    pr, pi = [], []
    for h in range(RNN_HEADS):
        xh = xc[:, h * RNN_HD:(h + 1) * RNN_HD].astype(BF16)
        pr.append(_dot(xh, wa_ref[h]))
        pi.append(_dot(xh, wx_ref[h]))
    r = _sigmoid(jnp.concatenate(pr, axis=-1) + ba)
    ig = _sigmoid(jnp.concatenate(pi, axis=-1) + bx)
    sp = jnp.maximum(-lam, 0.0) + jnp.log(1.0 + jnp.exp(-jnp.abs(lam)))
    neg_log_a = RG_C * r * sp
    a = jnp.exp(-neg_log_a)
    s2 = (1.0 + a * a) * jnp.tanh(neg_log_a)
    inv_s = lax.rsqrt(jnp.maximum(s2, 1e-30))
    return r, ig, sp, a, s2 * inv_s, inv_s


def _scan_tile(a_ref, b_ref, o_ref, carry_ref, rows, reverse):
    ridx = lax.broadcasted_iota(jnp.int32, (SUBLANE, D), 0)
    groups = rows // SUBLANE

    def group(gi, h):
        g = (groups - 1 - gi) if reverse else gi
        off = pl.multiple_of(g * SUBLANE, SUBLANE)
        a = a_ref[pl.ds(off, SUBLANE), :]
        b = b_ref[pl.ds(off, SUBLANE), :]
        for sh in (1, 2, 4):
            if reverse:
                keep = ridx < SUBLANE - sh
                a_p = jnp.where(keep, pltpu.roll(a, SUBLANE - sh, 0), 1.0)
                b_p = jnp.where(keep, pltpu.roll(b, SUBLANE - sh, 0), 0.0)
            else:
                keep = ridx >= sh
                a_p = jnp.where(keep, pltpu.roll(a, sh, 0), 1.0)
                b_p = jnp.where(keep, pltpu.roll(b, sh, 0), 0.0)
            b = b + a * b_p
            a = a * a_p
        hh = b + a * h
        o_ref[pl.ds(off, SUBLANE), :] = hh
        return hh[0:1, :] if reverse else hh[SUBLANE - 1:SUBLANE, :]

    carry_ref[...] = lax.fori_loop(0, groups, group, carry_ref[...])


def _rglru_fwd(xr, cw, cb, wa, wx, ba, bx, lam, reverse, name):
    T = xr.shape[0]
    n = T // TS
    prev_spec, next_spec = _halo_specs(TS, D, n, T, reverse)

    def body(prev_ref, cur_ref, next_ref, cw_ref, cb_ref, wa_ref, wx_ref, ba_ref, bx_ref, lam_ref,
             h_ref, a_ref, s_ref, r_ref, ig_ref, xc_ref, b_s, carry):
        i = pl.program_id(0)
        j = (n - 1 - i) if reverse else i

        @pl.when(i == 0)
        def _():
            carry[...] = jnp.zeros_like(carry)

        ext = _ext_rows(prev_ref, cur_ref[...], next_ref, j, n)
        xc = _conv_fwd(ext, cw_ref[...], cb_ref[...], TS)
        r, ig, _, a, s, _ = _gates(xc, wa_ref, wx_ref, ba_ref[...], bx_ref[...], lam_ref[...])
        s_ref[...] = s
        r_ref[...] = r.astype(BF16)
        ig_ref[...] = ig.astype(BF16)
        xc_ref[...] = xc.astype(BF16)
        a_ref[...] = a
        b_s[...] = s * ig * xc
        _scan_tile(a_ref, b_s, h_ref, carry, TS, reverse)

    wspec = _full((RNN_HEADS, RNN_HD, RNN_HD))
    cur = _rev_tile(TS, D, n, reverse)
    f32 = jax.ShapeDtypeStruct((T, D), F32)
    b16 = jax.ShapeDtypeStruct((T, D), BF16)
    return _pallas(
        body, name=name, grid=(n,),
        in_specs=[prev_spec, cur, next_spec, _full((4, D)), _full((1, D)),
                  wspec, wspec, _full((1, D)), _full((1, D)), _full((1, D))],
        out_specs=[cur] * 6,
        out_shape=[f32, f32, f32, b16, b16, b16],
        scratch_shapes=[pltpu.VMEM((TS, D), F32), pltpu.VMEM((1, D), F32)],
        compiler_params=_params(("arbitrary",)),
    )(xr, xr, xr, cw, cb, wa, wx, ba, bx, lam)


def _od_out(hf, hb, g1, w_out, x1, tgt, mod, lnp):
    T = x1.shape[0]

    def body(hf_ref, hb_ref, g_ref, w_ref, x_ref, t_ref, mod_ref, ln_ref,
             dh_ref, dg_ref, dx_ref, dwb_ref, vec_ref, dw_ref):
        i = pl.program_id(0)

        @pl.when(i == 0)
        def _():
            dw_ref[...] = jnp.zeros_like(dw_ref)
            vec_ref[...] = jnp.zeros_like(vec_ref)

        hs = hf_ref[...] + hb_ref[...]
        sg, dsg = _silu_and_grad(g_ref[...].astype(F32))
        yr = (hs * sg).astype(BF16)
        w = w_ref[...]
        out = _dot(yr, w)
        gate = mod_ref[2:3, :]
        z = ALPHA * x_ref[...] + gate * out
        lng = ln_ref[0:1, :]
        x2, xhat, rstd = _ln_fwd(z, lng, ln_ref[1:2, :])
        diff = x2 - t_ref[...]
        vec_ref[3:4, 0:LANE] += 0.5 * jnp.sum(diff * diff) * (1.0 / D)
        dx2 = diff * (1.0 / D)
        dz = _ln_bwd(dx2, xhat, rstd, lng)
        vec_ref[0:1, :] += _rowsum(dx2 * xhat)
        vec_ref[1:2, :] += _rowsum(dx2)
        vec_ref[2:3, :] += _rowsum(dz * out)
        dout = (dz * gate).astype(BF16)
        dyr = _dot_nt(dout, w)
        dw_ref[...] += _dot_tn(yr, dout)
        dh_ref[...] = dyr * sg
        dg_ref[...] = (dyr * hs * dsg).astype(BF16)
        dx_ref[...] = ALPHA * dz

        @pl.when(i == T // TMO - 1)
        def _():
            dwb_ref[...] = dw_ref[...].astype(BF16)

    return _pallas(
        body, name="od_out", grid=(T // TMO,),
        in_specs=[_tile(TMO, D), _tile(TMO, D), _tile(TMO, D), _full((D, D)), _tile(TMO, D), _tile(TMO, D),
                  _full((3, D)), _full((2, D))],
        out_specs=[_tile(TMO, D), _tile(TMO, D), _tile(TMO, D), _full((D, D)), _full((SUBLANE, D))],
        out_shape=[jax.ShapeDtypeStruct((T, D), F32), jax.ShapeDtypeStruct((T, D), BF16),
                   jax.ShapeDtypeStruct((T, D), F32), jax.ShapeDtypeStruct((D, D), BF16),
                   jax.ShapeDtypeStruct((SUBLANE, D), F32)],
        scratch_shapes=[pltpu.VMEM((D, D), F32)],
        compiler_params=_params(("arbitrary",)),
    )(hf, hb, g1, w_out, x1, tgt, mod, lnp)


def _rglru_bwd(fwd, dh, wa, wx, lam, reverse, name, comm=None):
    h, a_all, s_all, r_all, ig_all, xc_all = fwd
    T = h.shape[0]
    n = T // TS
    adj_rev = not reverse
    hprev_spec, hnext_spec = _halo_specs(TS, D, n, T, adj_rev)
    h_halo_spec = hnext_spec if reverse else hprev_spec

    def body(dh_ref, h_ref, hh_ref, a_ref, s_ref, r_ref, ig_ref, xc_ref, wa_ref, wx_ref, lam_ref,
             dxc_ref, dwa_ref, dwx_ref, vec_ref, a_s, l_s, carry, a_edge):
        i = pl.program_id(0)
        j = (n - 1 - i) if adj_rev else i

        @pl.when(i == 0)
        def _():
            carry[...] = jnp.zeros_like(carry)
            a_edge[...] = jnp.zeros_like(a_edge)
            dwa_ref[...] = jnp.zeros_like(dwa_ref)
            dwx_ref[...] = jnp.zeros_like(dwx_ref)
            vec_ref[...] = jnp.zeros_like(vec_ref)

        lam = lam_ref[...]
        sp = jnp.maximum(-lam, 0.0) + jnp.log(1.0 + jnp.exp(-jnp.abs(lam)))
        a, s = a_ref[...], s_ref[...]
        inv_s = lax.rsqrt(jnp.maximum(s * s, 1e-30))
        r, ig = r_ref[...].astype(F32), ig_ref[...].astype(F32)
        xcb = xc_ref[...]
        xc = xcb.astype(F32)

        rows = lax.broadcasted_iota(jnp.int32, (TS, D), 0)
        hcur = h_ref[...]
        if reverse:
            a_sh = jnp.where(rows == 0, a_edge[...], pltpu.roll(a, 1, 0))
            halo = jnp.where(j < n - 1, hh_ref[0:1, :], 0.0)
            h_nb = jnp.where(rows == TS - 1, halo, pltpu.roll(hcur, TS - 1, 0))
        else:
            a_sh = jnp.where(rows == TS - 1, a_edge[...], pltpu.roll(a, TS - 1, 0))
            halo = jnp.where(j > 0, hh_ref[SUBLANE - 1:SUBLANE, :], 0.0)
            h_nb = jnp.where(rows == 0, halo, pltpu.roll(hcur, 1, 0))
        a_s[...] = a_sh
        _scan_tile(a_s, dh_ref, l_s, carry, TS, adj_rev)
        a_edge[...] = a[TS - 1:TS, :] if reverse else a[0:1, :]

        lm = l_s[...]
        da = lm * h_nb
        di = lm * s * xc
        dxc = lm * s * ig
        ds = lm * ig * xc
        dlog_a = a * (da - ds * a * inv_s)
        dr = (-RG_C) * sp * dlog_a
        dsp = _rowsum((-RG_C) * r * dlog_a)
        dpr = dr * r * (1.0 - r)
        dpi = di * ig * (1.0 - ig)
        vec_ref[0:1, :] += _rowsum(dpr)
        vec_ref[1:2, :] += _rowsum(dpi)
        vec_ref[2:3, :] += dsp * (-_sigmoid(-lam))
        parts = []
        for hd in range(RNN_HEADS):
            sl = slice(hd * RNN_HD, (hd + 1) * RNN_HD)
            xh = xcb[:, sl]
            dprh = dpr[:, sl].astype(BF16)
            dpih = dpi[:, sl].astype(BF16)
            parts.append(_dot_nt(dprh, wa_ref[hd]) + _dot_nt(dpih, wx_ref[hd]))
            dwa_ref[hd] += _dot_tn(xh, dprh)
            dwx_ref[hd] += _dot_tn(xh, dpih)
        dxc_ref[...] = dxc + jnp.concatenate(parts, axis=-1)

    wspec = _full((RNN_HEADS, RNN_HD, RNN_HD))
    cur = _rev_tile(TS, D, n, adj_rev)
    return _fused_call(
        body, comm, (dh, h, h, a_all, s_all, r_all, ig_all, xc_all, wa, wx, lam), name=name, grid=(n,),
        in_specs=[cur, cur, h_halo_spec, cur, cur, cur, cur, cur, wspec, wspec, _full((1, D))],
        out_specs=[cur, wspec, wspec, _full((SUBLANE, D))],
        out_shape=[jax.ShapeDtypeStruct((T, D), F32),
                   jax.ShapeDtypeStruct((RNN_HEADS, RNN_HD, RNN_HD), F32),
                   jax.ShapeDtypeStruct((RNN_HEADS, RNN_HD, RNN_HD), F32),
                   jax.ShapeDtypeStruct((SUBLANE, D), F32)],
        scratch_shapes=[pltpu.VMEM((TS, D), F32)] * 2 + [pltpu.VMEM((1, D), F32)] * 2)


def _od_in_bwd(dxcf, dxcb, xr, dg1, x1, dx1p, mod, w_in, cw, comm=None):
    T = x1.shape[0]
    n = T // TMO
    slab = OD_IN // N_DEV
    prev_spec, next_spec = _halo_specs(TMO, D, n, T, False)

    def body(fp_ref, fc_ref, fn_ref, bp_ref, bc_ref, bn_ref, xr_ref, dg_ref, x1_ref, dxp_ref,
             mod_ref, w_ref, cw_ref, dx_ref, dwb_ref, vec_ref, dw_ref):
        i = pl.program_id(0)

        @pl.when(i == 0)
        def _():
            dw_ref[...] = jnp.zeros_like(dw_ref)
            vec_ref[...] = jnp.zeros_like(vec_ref)

        dcur = fc_ref[...] + bc_ref[...]
        dprev = jnp.where(i > 0, fp_ref[...] + bp_ref[...], 0.0)
        dnext = jnp.where(i < n - 1, fn_ref[...] + bn_ref[...], 0.0)
        dext = jnp.concatenate([dprev, dcur, dnext], axis=0)
        xr_v = xr_ref[...]
        cw_v = cw_ref[...]
        dxr = None
        for k in range(4):
            shifted = _shift_rows(dext, 2 - k, TMO)
            term = cw_v[k:k + 1, :] * shifted
            dxr = term if dxr is None else dxr + term
            vec_ref[k:k + 1, :] += _rowsum(shifted * xr_v)
        vec_ref[4:5, :] += _rowsum(dcur)
        dp = jnp.concatenate([dxr.astype(BF16), dg_ref[...]], axis=-1)
        x1v = x1_ref[...]
        scale1 = 1.0 + mod_ref[1:2, :]
        h1 = (x1v * scale1 + mod_ref[0:1, :]).astype(BF16)
        dh1 = _dot_nt(dp, w_ref[...])
        dw_ref[...] += _dot_tn(h1, dp)
        dx_ref[...] = dxp_ref[...] + dh1 * scale1
        vec_ref[5:6, :] += _rowsum(dh1)
        vec_ref[6:7, :] += _rowsum(dh1 * x1v)

        @pl.when(i == n - 1)
        def _():
            for j in range(N_DEV):
                dwb_ref[j] = dw_ref[:, j * slab:(j + 1) * slab].astype(BF16)

    t = _tile(TMO, D)
    return _fused_call(
        body, comm, (dxcf, dxcf, dxcf, dxcb, dxcb, dxcb, xr, dg1, x1, dx1p, mod, w_in, cw),
        name="od_in_bwd", grid=(n,),
        in_specs=[prev_spec, t, next_spec, prev_spec, t, next_spec, t, t, t, t,
                  _full((3, D)), _full((D, OD_IN)), _full((4, D))],
        out_specs=[t, _full((N_DEV, D, slab)), _full((SUBLANE, D))],
        out_shape=[jax.ShapeDtypeStruct((T, D), F32), jax.ShapeDtypeStruct((N_DEV, D, slab), BF16),
                   jax.ShapeDtypeStruct((SUBLANE, D), F32)],
        scratch_shapes=[pltpu.VMEM((D, OD_IN), F32)])


def _ev_out_bwd(dx1, z0, out0, y0, ycat, g0, w_out, mod, lnp, comm=None):
    T = dx1.shape[0]

    def body(dx_ref, z_ref, out_ref, y0_ref, yc_ref, g_ref, w_ref, mod_ref, ln_ref,
             dxp_ref, dyc_ref, dg_ref, dwb_ref, vec_ref, dw_ref):
        i = pl.program_id(0)

        @pl.when(i == 0)
        def _():
            dw_ref[...] = jnp.zeros_like(dw_ref)
            vec_ref[...] = jnp.zeros_like(vec_ref)

        lng = ln_ref[0:1, :]
        _, xhat, rstd = _ln_fwd(z_ref[...], lng, ln_ref[1:2, :])
        dy = dx_ref[...]
        dz = _ln_bwd(dy, xhat, rstd, lng)
        vec_ref[0:1, :] += _rowsum(dy * xhat)
        vec_ref[1:2, :] += _rowsum(dy)
        vec_ref[2:3, :] += _rowsum(dz * out_ref[...].astype(F32))
        dout = (dz * mod_ref[2:3, :]).astype(BF16)
        dy0 = _dot_nt(dout, w_ref[...])
        dw_ref[...] += _dot_tn(y0_ref[...], dout)
        sg, dsg = _silu_and_grad(g_ref[...].astype(F32))
        dyc_ref[...] = (dy0 * sg).astype(BF16)
        dg_ref[...] = (dy0 * yc_ref[...].astype(F32) * dsg).astype(BF16)
        dxp_ref[...] = ALPHA * dz

        @pl.when(i == T // TMO - 1)
        def _():
            dwb_ref[...] = dw_ref[...].astype(BF16)

    t = _tile(TMO, D)
    return _fused_call(
        body, comm, (dx1, z0, out0, y0, ycat, g0, w_out, mod, lnp), name="ev_out_bwd", grid=(T // TMO,),
        in_specs=[t, t, t, t, t, t, _full((D, D)), _full((3, D)), _full((2, D))],
        out_specs=[t, t, t, _full((D, D)), _full((SUBLANE, D))],
        out_shape=[jax.ShapeDtypeStruct((T, D), F32), jax.ShapeDtypeStruct((T, D), BF16),
                   jax.ShapeDtypeStruct((T, D), BF16), jax.ShapeDtypeStruct((D, D), BF16),
                   jax.ShapeDtypeStruct((SUBLANE, D), F32)],
        scratch_shapes=[pltpu.VMEM((D, D), F32)])


def _mix0_bwd(q, kvx, lse, dyc, ycat, su, sv, sink_l, bias, a128, gsum, sel, sg_lng, sg_lnb, sg_w, sg_bfull,
              rc, rs1, rs2, comm=None):
    T = q.shape[0]
    nb = T // BLK

    def body(q_ref, kp_ref, kc_ref, kn_ref, lse_ref, dyc_ref, yc_ref, su_ref, sv_ref, sink_ref, bias_ref, a_ref,
             gsum_ref, sel_ref, lng_ref, lnb_ref, w_ref, bfull_ref, c_ref, s1_ref, s2_ref,
             dq_ref, dkv_ref, dsu_ref, dsv_ref, dw_ref, dbt_ref, vec_ref, dsink_ref):
        n = pl.program_id(0)

        @pl.when(n == 0)
        def _():
            dkv_ref[...] = jnp.zeros_like(dkv_ref)
            dw_ref[...] = jnp.zeros_like(dw_ref)
            dbt_ref[...] = jnp.zeros_like(dbt_ref)
            vec_ref[...] = jnp.zeros_like(vec_ref)
            dsink_ref[...] = jnp.zeros_like(dsink_ref)

        kvx4 = jnp.concatenate([kp_ref[...], kc_ref[...], kn_ref[...]], axis=0)
        for s in range(BWD_BLOCKS):
            _mix0_bwd_block(s, BWD_BLOCKS * n + s, nb, kvx4[s * BLK:s * BLK + 3 * BLK], q_ref, lse_ref, dyc_ref, yc_ref, su_ref,
                            sv_ref, sink_ref, bias_ref, a_ref, gsum_ref, sel_ref, lng_ref, lnb_ref, w_ref, bfull_ref,
                            c_ref, s1_ref, s2_ref, dq_ref, dkv_ref, dsu_ref, dsv_ref, dw_ref, dbt_ref, vec_ref,
                            dsink_ref)

    def _mix0_bwd_block(s, b, nb, kvx, q_ref, lse_ref, dyc_ref, yc_ref, su_ref, sv_ref, sink_ref, bias_ref, a_ref,
                        gsum_ref, sel_ref, lng_ref, lnb_ref, w_ref, bfull_ref, c_ref, s1_ref, s2_ref,
                        dq_ref, dkv_ref, dsu_ref, dsv_ref, dw_ref, dbt_ref, vec_ref, dsink_ref):
        rows = slice(s * BLK, (s + 1) * BLK)

        def tile(ref, t):
            return ref[rows, t * LANE:(t + 1) * LANE]

        band = pl.ds(pl.multiple_of(b * BLK + (TM - BLK), BLK), 3 * BLK)
        bias = _band_bias(bias_ref, b, nb)
        bias2 = jnp.concatenate([bias, bias], axis=1)
        low = lax.broadcasted_iota(jnp.int32, (BLK, LANE), 1) < HEAD_DIM
        low2 = lax.broadcasted_iota(jnp.int32, (2 * BLK, LANE), 1) < HEAD_DIM
        sel = sel_ref[...]
        c, s1, s2 = c_ref[rows, :], s1_ref[rows, :], s2_ref[rows, :]
        for kvh in range(2):
            t0, t1 = 2 * kvh, 2 * kvh + 1
            q2 = jnp.concatenate([tile(q_ref, t0), tile(q_ref, t1)], axis=0)
            do2 = jnp.concatenate([tile(dyc_ref, t0), tile(dyc_ref, t1)], axis=0)
            yc2 = jnp.concatenate([tile(yc_ref, t0), tile(yc_ref, t1)], axis=0)
            p_hi, p_lo = _split_bf16(do2.astype(F32) * yc2.astype(F32))
            deltas = _dot_nt(sel, p_hi) + _dot_nt(sel, p_lo)
            dkx = jnp.zeros((3 * BLK, LANE), F32)
            dvx = jnp.zeros((3 * BLK, LANE), F32)
            dq_acc = None
            for par in range(2):
                heads = (4 * kvh + par, 4 * kvh + 2 + par)
                kt = 2 * kvh + par
                ke = kvx[:, kt * LANE:(kt + 1) * LANE]
                ve = kvx[:, (4 + kt) * LANE:(5 + kt) * LANE]
                lse = jnp.concatenate([lse_ref[s, :, h * LANE:(h + 1) * LANE] for h in heads], axis=1)
                sk = jnp.concatenate([_lane_tile(sink_ref, h) for h in heads], axis=1)
                delta = deltas[par:par + 1, :]
                pt = jnp.exp(_dot_nt(ke, q2) + bias2 - lse)
                dst = (pt * (_dot_nt(ve, do2) - delta)).astype(BF16)
                sink_terms = jnp.exp(sk - lse) * delta
                for k, h in enumerate(heads):
                    dsink_ref[:, h * LANE:(h + 1) * LANE] += sink_terms[:, k * LANE:(k + 1) * LANE]
                part = _dot_tn(dst, ke)
                dq_acc = part if dq_acc is None else dq_acc + part
                mine = low2 if par == 0 else jnp.logical_not(low2)
                dkx = dkx + jnp.dot(dst, jnp.where(mine, q2, jnp.zeros_like(q2)), preferred_element_type=F32)
                dvx = dvx + jnp.dot(pt.astype(BF16), jnp.where(mine, do2, jnp.zeros_like(do2)),
                                    preferred_element_type=F32)
            for k, t in enumerate((t0, t1)):
                dq_t = dq_acc[k * BLK:(k + 1) * BLK] * (HEAD_DIM ** -0.5)
                dq_ref[rows, t * LANE:(t + 1) * LANE] = _rope_bwd(dq_t, c, s1, s2).astype(BF16)
            dkv_ref[band, kvh * LANE:(kvh + 1) * LANE] += dkx
            dkv_ref[band, (2 + kvh) * LANE:(3 + kvh) * LANE] += dvx

        lng = lng_ref[...]
        xhat, rstd, vb, svm = _sg_core(sv_ref.at[rows, :], lng, lnb_ref[...], a_ref, w_ref, bfull_ref)
        dy = dyc_ref[rows, ATTN_W:].astype(F32)
        dsu_ref[rows, :] = (dy * svm).astype(BF16)
        dsvm = dy * su_ref[rows, :].astype(F32)
        d_hi, d_lo = _split_bf16(dsvm)
        gsum = gsum_ref[...]
        dbt_ref[...] += jnp.dot(d_hi, gsum, preferred_element_type=F32) + jnp.dot(d_lo, gsum,
                                                                                 preferred_element_type=F32)
        tiles = []
        for t in range(SG_W // LANE):
            tl = slice(t * LANE, (t + 1) * LANE)
            dt, v2 = d_hi[:, tl], vb[:, tl]
            dw_ref[2 * t] += _dot_nt(jnp.where(low, dt, jnp.zeros_like(dt)), v2)
            dw_ref[2 * t + 1] += _dot_nt(jnp.where(low, jnp.zeros_like(dt), dt), v2)
            tiles.append(jnp.where(low, _dot_tn(w_ref[2 * t], dt), _dot_tn(w_ref[2 * t + 1], dt)))
        dvgn = jnp.concatenate(tiles, axis=-1)
        vec_ref[0:1, :] += _rowsum(dvgn * xhat)
        vec_ref[1:2, :] += _rowsum(dvgn)
        dxh = dvgn * lng
        m1 = _group_mean(dxh, a_ref)
        m2 = _group_mean(dxh * xhat, a_ref)
        dsv_ref[rows, :] = (rstd * (dxh - m1 - xhat * m2)).astype(BF16)

    two = BWD_BLOCKS * BLK
    return _fused_call(
        body, comm, (q, kvx, kvx, kvx, lse, dyc, ycat, su, sv, sink_l, bias, a128, gsum, sel, sg_lng, sg_lnb, sg_w,
                     sg_bfull, rc, rs1, rs2),
        name="mix0_bwd", grid=(nb // BWD_BLOCKS,),
        in_specs=[_tile(two, ATTN_W)] + _band_specs(KVX_W, nb, BWD_BLOCKS) + [
            pl.BlockSpec((BWD_BLOCKS, 1, N_HEADS * LANE), lambda n: (n, 0, 0)), _tile(two, D), _tile(two, D),
            _tile(two, SG_W), _tile(two, SG_W), _full((1, N_HEADS * LANE)), _full((3 * BLK, LANE)),
            _full((2 * LANE, 2 * LANE)),_full((SG_W, LANE)), _full((SUBLANE, LANE)), _full((1, SG_W)), _full((1, SG_W)),
            _full((SG_GROUPS, BLK, BLK)), _full((BLK, SG_W)), _tile(two, LANE), _tile(two, LANE), _tile(two, LANE)],
        out_specs=[_tile(two, ATTN_W), _full((T + 2 * TM, 4 * LANE)), _tile(two, SG_W), _tile(two, SG_W),
                   _full((SG_GROUPS, BLK, BLK)), _full((BLK, LANE)), _full((SUBLANE, SG_W)),
                   _full((1, N_HEADS * LANE))],
        out_shape=[jax.ShapeDtypeStruct((T, ATTN_W), BF16), jax.ShapeDtypeStruct((T + 2 * TM, 4 * LANE), F32),
                   jax.ShapeDtypeStruct((T, SG_W), BF16), jax.ShapeDtypeStruct((T, SG_W), BF16),
                   jax.ShapeDtypeStruct((SG_GROUPS, BLK, BLK), F32), jax.ShapeDtypeStruct((BLK, LANE), F32),
                   jax.ShapeDtypeStruct((SUBLANE, SG_W), F32), jax.ShapeDtypeStruct((1, N_HEADS * LANE), F32)])


def _ev_in_bwd(dq, dkv, dsu, dsv, dg0, x, dxp, mod, w_in, rc, rs1, rs2, comm=None):
    T = x.shape[0]

    def body(dq_ref, dkv_ref, dsu_ref, dsv_ref, dg_ref, x_ref, dxp_ref, mod_ref, w_ref, c_ref, s1_ref, s2_ref,
             dx_ref, dwb_ref, vec_ref, dw_ref):
        i = pl.program_id(0)

        @pl.when(i == 0)
        def _():
            dw_ref[...] = jnp.zeros_like(dw_ref)
            vec_ref[...] = jnp.zeros_like(vec_ref)

        low = lax.broadcasted_iota(jnp.int32, (TM, LANE), 1) < HEAD_DIM

        def fold(j):
            t0 = dkv_ref[:, (2 * j) * LANE:(2 * j + 1) * LANE]
            t1 = dkv_ref[:, (2 * j + 1) * LANE:(2 * j + 2) * LANE]
            return jnp.where(low, t0 + pltpu.roll(t0, HEAD_DIM, 1), t1 + pltpu.roll(t1, HEAD_DIM, 1))

        dk = _rope_bwd(fold(0), c_ref[...], s1_ref[...], s2_ref[...]).astype(BF16)
        dp = jnp.concatenate([dq_ref[...], dk, fold(1).astype(BF16), dsu_ref[...], dsv_ref[...],
                              dg_ref[...]], axis=-1)
        xv = x_ref[...]
        scale0 = 1.0 + mod_ref[1:2, :]
        h0 = (xv * scale0 + mod_ref[0:1, :]).astype(BF16)
        dh0 = _dot(dp, w_ref[...])
        dw_ref[...] += _dot_tn(dp, h0)
        dx_ref[...] = dxp_ref[...] + dh0 * scale0
        vec_ref[0:1, :] += _rowsum(dh0)
        vec_ref[1:2, :] += _rowsum(dh0 * xv)

        @pl.when(i == T // TM - 1)
        def _():
            dwb_ref[...] = dw_ref[...].astype(BF16)

    t = _tile(TM, D)
    return _fused_call(
        body, comm, (dq, dkv, dsu, dsv, dg0, x, dxp, mod, w_in, rc, rs1, rs2), name="ev_in_bwd", grid=(T // TM,),
        in_specs=[_tile(TM, ATTN_W), pl.BlockSpec((TM, 4 * LANE), lambda i: (i + 1, 0)), _tile(TM, SG_W),
                  _tile(TM, SG_W), t, t, t,
                  _full((3, D)), _full((EV_IN, D)), _tile(TM, LANE), _tile(TM, LANE), _tile(TM, LANE)],
        out_specs=[t, _full((EV_IN, D)), _full((SUBLANE, D))],
        out_shape=[jax.ShapeDtypeStruct((T, D), F32), jax.ShapeDtypeStruct((EV_IN, D), BF16),
                   jax.ShapeDtypeStruct((SUBLANE, D), F32)],
        scratch_shapes=[pltpu.VMEM((EV_IN, D), F32)])


def _sum_slots(land_ref):
    g = land_ref[0].astype(F32)
    for i in range(1, land_ref.shape[0]):
        g = g + land_ref[i].astype(F32)
    return g


def _reduce_adam(items, name, after=()):
    R, C = items[0][1].shape
    rb = R
    if R > 512:
        for cand in (512, 256, 128, 64, 32, 16, 8):
            if R % cand == 0:
                rb = cand
                break
    n = len(items)

    def body(*refs):
        for k in range(n):
            l_ref, w_ref, m_ref, v_ref = refs[4 * k:4 * k + 4]
            first_out = 4 * n + len(after)
            g_ref, d_ref, nm_ref, nv_ref = refs[first_out + 4 * k:first_out + 4 * k + 4]
            g = _sum_slots(l_ref)
            g_ref[...] = g
            dlt, m2, v2 = _adam(w_ref[...], g, m_ref[...], v_ref[...])
            d_ref[...] = dlt
            nm_ref[...] = m2
            nv_ref[...] = v2

    t = pl.BlockSpec((rb, C), lambda i: (i, 0))
    shp = jax.ShapeDtypeStruct((R, C), F32)
    in_specs, operands = [], []
    for land, w, m, v in items:
        in_specs += [pl.BlockSpec((land.shape[0], rb, C), lambda i: (0, i, 0)), t, t, t]
        operands += [land, w, m, v]
    res = _pallas(
        body, name=name, grid=(R // rb,),
        in_specs=in_specs + [pl.BlockSpec(memory_space=pl.ANY)] * len(after),
        out_specs=[t] * (4 * n), out_shape=[shp] * (4 * n),
        compiler_params=_params(("parallel",)),
    )(*operands, *after)
    return [list(res[4 * k:4 * k + 4]) for k in range(n)]


def _tail_stage1(slabs, small):
    _, R, C = slabs.shape
    n_chips = N_DEV // 2
    gather = _GatherComm(small)
    ns = gather.n

    def body(*refs):
        slab_ref = refs[0]
        g_ins = refs[1:1 + ns]
        part, land_ref = refs[1 + ns], refs[2 + ns]
        g_outs = refs[3 + ns:3 + 2 * ns]
        stage, s1_send, s1_recv = refs[3 + 2 * ns:6 + 2 * ns]
        g_sems = refs[6 + 2 * ns:]
        x, y, c = _my_pos()
        chip = 2 * x + y
        gather.start(g_ins, g_outs, g_sems)
        swaps = [pltpu.make_async_remote_copy(
            src_ref=slab_ref.at[2 * k + (1 - c)], dst_ref=stage.at[k], send_sem=s1_send.at[k],
            recv_sem=s1_recv.at[k], device_id=(x, y, 1 - c), device_id_type=MESH) for k in range(n_chips)]
        for cp in swaps:
            cp.start()
        for cp in swaps:
            cp.wait()
        for k in range(n_chips):
            part[k] = (slab_ref[2 * k + c].astype(F32) + stage[k].astype(F32)).astype(BF16)
        land_ref[chip] = part[chip]
        gather.mid(g_ins, g_outs, g_sems)
        gather.finish(g_ins, g_outs, g_sems)

    any_spec = pl.BlockSpec(memory_space=pl.ANY)
    vmem_spec = pl.BlockSpec(memory_space=pltpu.VMEM)
    slab4 = jax.ShapeDtypeStruct((n_chips, R, C), BF16)
    res = _pallas(
        body, name="tail_stage1",
        out_shape=[slab4, slab4] + gather.out_shapes(),
        in_specs=[vmem_spec] + [any_spec] * ns, out_specs=[vmem_spec, vmem_spec] + [any_spec] * ns,
        scratch_shapes=[pltpu.VMEM((n_chips, R, C), BF16),
                        pltpu.SemaphoreType.DMA((n_chips,)), pltpu.SemaphoreType.DMA((n_chips,))] + gather.sems(),
        compiler_params=pltpu.CompilerParams(vmem_limit_bytes=VMEM_LIMIT),
    )(slabs, *gather.arrs)
    return res[0], res[1], list(res[2:])


def _chip_copies(part_ref, land_ref, send_sems, recv_sems):
    x, y, c = _my_pos()
    chip = 2 * x + y
    copies = []
    for r in range(1, N_DEV // 2):
        px = (1 - x) if (r & 2) else x
        py = (1 - y) if (r & 1) else y
        copies.append(pltpu.make_async_remote_copy(
            src_ref=part_ref.at[2 * px + py], dst_ref=land_ref.at[chip], send_sem=send_sems[r - 1],
            recv_sem=recv_sems[r - 1], device_id=(px, py, c), device_id_type=MESH))
    return copies


def _tail_send(part, land):
    n = N_DEV // 2 - 1

    def body(part_ref, land_ref, *outs):
        send_sems, recv_sems = outs[:n], outs[n:2 * n]
        token = outs[2 * n + 2]
        for cp in _chip_copies(part_ref, land_ref, send_sems, recv_sems):
            cp.start()
        token[...] = jnp.zeros_like(token)

    hbm = pl.BlockSpec(memory_space=pltpu.HBM)
    sem = pl.BlockSpec(memory_space=pltpu.SEMAPHORE)
    res = _pallas(
        body, name="tail_send",
        out_shape=tuple([pltpu.SemaphoreType.DMA(())] * (2 * n)
                        + [pltpu.HBM(part.shape, part.dtype), pltpu.HBM(land.shape, land.dtype),
                           jax.ShapeDtypeStruct((SUBLANE, LANE), F32)]),
        in_specs=(hbm, hbm), out_specs=tuple([sem] * (2 * n) + [hbm, hbm, pl.BlockSpec(memory_space=pltpu.VMEM)]),
        input_output_aliases={0: 2 * n, 1: 2 * n + 1},
        compiler_params=pltpu.CompilerParams(has_side_effects=pltpu.SideEffectType.DATAFLOW_SIDE_EFFECTING),
    )(pltpu.with_memory_space_constraint(part, pltpu.HBM), pltpu.with_memory_space_constraint(land, pltpu.HBM))
    return list(res[:n]), list(res[n:2 * n]), res[2 * n], res[2 * n + 1], res[2 * n + 2]


def _tail_wait(send_sems, recv_sems, part, land, after):
    n = len(send_sems)

    def body(part_ref, land_ref, *rest):
        ss, rs = rest[:n], rest[n:2 * n]
        for cp in _chip_copies(part_ref, land_ref, ss, rs):
            cp.wait_send()
            cp.wait_recv()

    hbm = pl.BlockSpec(memory_space=pltpu.HBM)
    sem = pl.BlockSpec(memory_space=pltpu.SEMAPHORE)
    any_spec = pl.BlockSpec(memory_space=pl.ANY)
    res = _pallas(
        body, name="tail_wait",
        out_shape=(pltpu.HBM(part.shape, part.dtype), pltpu.HBM(land.shape, land.dtype)),
        in_specs=tuple([hbm, hbm] + [sem] * (2 * n) + [any_spec] * len(after)), out_specs=(hbm, hbm),
        input_output_aliases={0: 0, 1: 1},
        compiler_params=pltpu.CompilerParams(has_side_effects=pltpu.SideEffectType.DATAFLOW_SIDE_EFFECTING),
    )(part, land, *send_sems, *recv_sems, *after)
    return res[1]


def _slots_adam(items, name, after=()):
    zeros3 = (0, 0, 0)
    in_specs, out_specs, out_shape, operands = [], [], [], []
    for land, w, m, v in items:
        inner = w.shape[-3:]
        if w.ndim == 5:
            lspec = pl.BlockSpec((N_DEV, 1) + inner, lambda i: (0, i) + zeros3)
            wspec = pl.BlockSpec((1, 1) + inner, lambda i: (0, i) + zeros3)
        else:
            lspec = pl.BlockSpec((N_DEV,) + inner, lambda i: (0,) + zeros3)
            wspec = pl.BlockSpec((1,) + inner, lambda i: (0,) + zeros3)
        in_specs += [lspec, wspec, wspec, wspec]
        out_specs += [wspec] * 4
        out_shape += [jax.ShapeDtypeStruct(w.shape, F32)] * 4
        operands += [land, w, m, v]
    n = len(items)

    def body(*refs):
        for k, (_, w, _, _) in enumerate(items):
            l_ref, w_ref, m_ref, v_ref = refs[4 * k:4 * k + 4]
            first_out = 4 * n + len(after)
            outs = refs[first_out + 4 * k:first_out + 4 * k + 4]
            at = (0, 0) if w.ndim == 5 else (0,)

            def update(l_ref=l_ref, w_ref=w_ref, m_ref=m_ref, v_ref=v_ref, outs=outs, at=at):
                g = l_ref[(0,) + at[1:]].astype(F32)
                for i in range(1, N_DEV):
                    g = g + l_ref[(i,) + at[1:]].astype(F32)
                dlt, m2, v2 = _adam(w_ref[at], g, m_ref[at], v_ref[at])
                for o_ref, val in zip(outs, (g, dlt, m2, v2)):
                    o_ref[at] = val

            if w.ndim == 5:
                update()
            else:
                pl.when(pl.program_id(0) == 0)(update)

    res = _pallas(
        body, name=name, grid=(2,),
        in_specs=in_specs + [pl.BlockSpec(memory_space=pl.ANY)] * len(after),
        out_specs=out_specs, out_shape=out_shape,
        compiler_params=_params(("arbitrary",)),
    )(*operands, *after)
    return [list(res[4 * k:4 * k + 4]) for k in range(n)]


SMALL_PARAMS = ("ln_g", "ln_b", "ev_sg_ln_g", "ev_sg_ln_b", "ev_sink", "ev_sg_b",
                "od_conv_w", "od_conv_b", "od_b_a", "od_b_x", "od_lam")


def _small_update(ga, gc, gd, gf, gb, ge, gsink, gbt, params):
    names = list(SMALL_PARAMS)
    flat = [a for nm in names for a in params[nm]]
    n_g = 8

    def body(*refs):
        ga_ref, gc_ref, gd_ref, gf_ref, gb_ref, ge_ref, gs_ref, gbt_ref = refs[:n_g]
        prm = refs[n_g:n_g + 3 * len(names)]
        loss_ref = refs[n_g + 3 * len(names)]
        outs = refs[n_g + 3 * len(names) + 1:]

        def ssum(ref):
            acc = ref[0]
            for i in range(1, N_DEV):
                acc = acc + ref[i]
            return acc

        a, cc, dd, ff, bb, ee = ssum(ga_ref), ssum(gc_ref), ssum(gd_ref), ssum(gf_ref), ssum(gb_ref), ssum(ge_ref)
        loss_ref[...] = a[3:4, 0:LANE]
        me = _slot(*_my_pos())

        def mine(rows):
            acc = jnp.zeros((rows.shape[0], LANE), F32)
            for j in range(N_DEV):
                acc = acc + jnp.where(me == j, rows[:, j * LANE:(j + 1) * LANE], 0.0)
            return acc

        sink_terms = ssum(gs_ref)
        lane8 = lax.broadcasted_iota(jnp.int32, (1, N_HEADS), 1)
        g_sink = jnp.zeros((1, N_HEADS), F32)
        for h in range(N_HEADS):
            tot = -jnp.sum(sink_terms[:, h * LANE:(h + 1) * LANE], axis=1, keepdims=True)
            g_sink = jnp.where(lane8 == h, tot, g_sink)
        grads = dict(
            ln_g=jnp.concatenate([dd[0:1], a[0:1]], axis=0), ln_b=jnp.concatenate([dd[1:2], a[1:2]], axis=0),
            ev_sg_ln_g=ee[0:1], ev_sg_ln_b=ee[1:2], ev_sink=g_sink,
            ev_sg_b=jnp.transpose(ssum(gbt_ref))[0:SG_GROUPS, :],
            od_conv_w=mine(cc[0:4]), od_conv_b=mine(cc[4:5]),
            od_b_a=mine(jnp.concatenate([ff[0:1], bb[0:1]], axis=0)),
            od_b_x=mine(jnp.concatenate([ff[1:2], bb[1:2]], axis=0)),
            od_lam=mine(jnp.concatenate([ff[2:3], bb[2:3]], axis=0)))
        for k, nm in enumerate(names):
            w_ref, m_ref, v_ref = prm[3 * k:3 * k + 3]
            at = (0,) if len(w_ref.shape) == 3 else ()
            g = grads[nm]
            dlt, m2, v2 = _adam(w_ref[at] if at else w_ref[...], g, m_ref[at] if at else m_ref[...],
                                v_ref[at] if at else v_ref[...])
            for o_ref, val in zip(outs[4 * k:4 * k + 4], (g, dlt, m2, v2)):
                if at:
                    o_ref[at] = val
                else:
                    o_ref[...] = val

    gathered = [ga, gc, gd, gf, gb, ge, gsink, gbt]
    out_shape = [jax.ShapeDtypeStruct((1, LANE), F32)]
    for nm in names:
        out_shape += [jax.ShapeDtypeStruct(params[nm][0].shape, F32)] * 4
    return _pallas(
        body, name="small_update", grid=(1,),
        in_specs=[_full(a.shape) for a in gathered + flat],
        out_specs=[_full(s.shape) for s in out_shape], out_shape=out_shape,
        compiler_params=_params(("arbitrary",)),
    )(*gathered, *flat)


VEC_ROWS = 16
VEC_LAYOUT = (("od_conv_w", 4), ("od_conv_b", 1), ("od_b_a", 2), ("od_b_x", 2), ("od_lam", 2))


def _from_slabs(slabs):
    n, R, cp = slabs.shape
    return slabs.transpose(1, 0, 2).reshape(R, n * cp)


def kernel(x, c, positions, ada_w, ada_b, ln_g, ln_b, ev_w_in, ev_w_out, ev_sink, ev_sg_ln_g, ev_sg_ln_b, ev_sg_w, ev_sg_b, od_w_in, od_conv_w, od_conv_b, od_w_a, od_b_a, od_w_x, od_b_x, od_lam, od_w_out, loss_target, m_ada_w, m_ada_b, m_ln_g, m_ln_b, m_ev_w_in, m_ev_w_out, m_ev_sink, m_ev_sg_ln_g, m_ev_sg_ln_b, m_ev_sg_w, m_ev_sg_b, m_od_w_in, m_od_conv_w, m_od_conv_b, m_od_w_a, m_od_b_a, m_od_w_x, m_od_b_x, m_od_lam, m_od_w_out, v_ada_w, v_ada_b, v_ln_g, v_ln_b, v_ev_w_in, v_ev_w_out, v_ev_sink, v_ev_sg_ln_g, v_ev_sg_ln_b, v_ev_sg_w, v_ev_sg_b, v_od_w_in, v_od_conv_w, v_od_conv_b, v_od_w_a, v_od_b_a, v_od_w_x, v_od_b_x, v_od_lam, v_od_w_out):
    T = x.shape[1]
    me = _slot(*_my_pos())
    xs = x.reshape(T, D)
    tgt = loss_target.reshape(T, D)

    c_all, mod_all, g_vec, (g_ev_in,), (s_ev_out, s_od_in, s_od_out, sg_w, wa, wx) = _head_gather(
        c, ada_w, [ev_w_in[0].T.astype(BF16)],
        [ev_w_out[0], od_w_in[0], od_w_out[0], ev_sg_w[0], od_w_a[0], od_w_x[0]],
        [od_conv_w, od_conv_b, od_b_a, od_b_x, od_lam])
    c_all = c_all.reshape(N_DEV, D)
    w_ev_in = g_ev_in.reshape(EV_IN, D)
    vec_full = _from_slabs(g_vec)
    cw, cb = vec_full[0:4], vec_full[4:5]
    ba, bx, lam = vec_full[5:7], vec_full[7:9], vec_full[9:11]
    mod_mine = lax.dynamic_index_in_dim(mod_all, me, axis=2, keepdims=False)
    mod = mod_mine.transpose(1, 0, 2).reshape(2, 3 * D) + ada_b
    mod0 = mod[0].reshape(3, D)
    mod1 = mod[1].reshape(3, D)

    half = 8
    inv_freq = jnp.power(jnp.float32(ROPE_THETA), -jnp.arange(half, dtype=F32) / half)
    ang = positions.reshape(T).astype(F32)[:, None] * inv_freq
    cos_t = jnp.tile(jnp.cos(ang), (1, LANE // half))
    sin_t = jnp.tile(jnp.sin(ang), (1, LANE // half))
    l64 = jnp.arange(LANE) % HEAD_DIM
    rc = jnp.where(l64 < 2 * half, cos_t, 1.0)
    rs1 = jnp.where(l64 < half, -sin_t, 0.0)
    rs2 = jnp.where((l64 >= half) & (l64 < 2 * half), sin_t, 0.0)

    ln0 = jnp.stack([ln_g[0], ln_b[0]])
    ln1 = jnp.stack([ln_g[1], ln_b[1]])
    sg_lng = ev_sg_ln_g
    sg_lnb = ev_sg_ln_b
    sg_bfull = jnp.repeat(ev_sg_b[0].T, SG_DIM, axis=1)
    sink_l = jnp.repeat(ev_sink, LANE, axis=1)
    kj = jnp.arange(3 * BLK)[:, None]
    qi = jnp.arange(BLK)[None, :]
    band_bias = jnp.where(jnp.abs(kj - BLK - qi) <= BLK, 0.0, NEG_INF).astype(F32)
    lanes = jnp.arange(LANE)
    lanes2 = jnp.arange(2 * LANE)
    a128 = jnp.where(lanes2[:, None] // SG_DIM == lanes2[None, :] // SG_DIM, 1.0 / SG_DIM, 0.0).astype(BF16)
    gsum = (jnp.arange(SG_W)[:, None] // SG_DIM == lanes[None, :]).astype(BF16)
    sel = (jnp.arange(SUBLANE)[:, None] == lanes[None, :] // HEAD_DIM).astype(BF16)

    (q, kvx, su, sv, g0), _ = _ev_in(xs, mod0, w_ev_in, rc, rs1, rs2)
    (ycat, y0, lse), (g_ev_out, g_od_in, g_od_out) = _mix0_fwd(
        q, kvx, su, sv, g0, sink_l, band_bias, a128, sg_lng, sg_lnb, sg_w, sg_bfull,
        _GatherComm([s_ev_out, s_od_in, s_od_out], mid_frac=0.75))
    w_ev_out = g_ev_out.reshape(D, D)
    w_od_in = _from_slabs(g_od_in)
    w_od_out = g_od_out.reshape(D, D)
    out0, z0, x1 = _ev_out(y0, w_ev_out, xs, mod0, ln0)
    xr, g1 = _od_in(x1, mod1, w_od_in)
    fwd_f = _rglru_fwd(xr, cw, cb, wa[0], wx[0], ba[0:1], bx[0:1], lam[0:1], False, "rglru_fwd_f")
    fwd_b = _rglru_fwd(xr, cw, cb, wa[1], wx[1], ba[1:2], bx[1:2], lam[1:2], True, "rglru_fwd_b")
    dh, dg1, dx1p, d_od_out, vec_a = _od_out(fwd_f[0], fwd_b[0], g1, w_od_out, x1, tgt, mod1, ln1)

    (dxcf, dwa_f, dwx_f, vec_f), (l_od_out,) = _rglru_bwd(
        fwd_f, dh, wa[0], wx[0], lam[0:1], False, "rglru_bwd_f",
        _ExchangeComm([d_od_out.reshape(N_DEV, D // N_DEV, D)]))
    (dxcb, dwa_b, dwx_b, vec_b), _ = _rglru_bwd(fwd_b, dh, wa[1], wx[1], lam[1:2], True, "rglru_bwd_b")
    (dx1, d_od_in, vec_c), (a_wa, a_wx) = _od_in_bwd(
        dxcf, dxcb, xr, dg1, x1, dx1p, mod1, w_od_in, cw,
        _GatherComm([jnp.stack([dwa_f, dwa_b]).astype(BF16), jnp.stack([dwx_f, dwx_b]).astype(BF16)],
                    mid_frac=0.75))
    (dxp, dyc, dg0, d_ev_out, vec_d), (l_od_in,) = _ev_out_bwd(
        dx1, z0, out0, y0, ycat, g0, w_ev_out, mod0, ln0, _ExchangeComm([d_od_in]))
    (dq, dkv, dsu, dsv, d_sg_w, d_sg_bt, vec_e, d_sink_l), (l_ev_out, ga, gc, gd, gf, gb) = _mix0_bwd(
        q, kvx, lse, dyc, ycat, su, sv, sink_l, band_bias, a128, gsum, sel, sg_lng, sg_lnb, sg_w, sg_bfull,
        rc, rs1, rs2, _BothComm(_ExchangeComm([d_ev_out.reshape(N_DEV, D // N_DEV, D)]),
                                _GatherComm([vec_a, vec_c, vec_d, vec_f, vec_b], mid_frac=0.75)))
    (grad_x, d_ev_in, vec_g), _ = _ev_in_bwd(dq, dkv, dsu, dsv, dg0, xs, dxp, mod0, w_ev_in, rc, rs1, rs2)

    part, land, (gg, ge, gsink, gbt, a_sgw) = _tail_stage1(
        d_ev_in.reshape(N_DEV, EV_IN // N_DEV, D), [vec_g, vec_e, d_sink_l, d_sg_bt, d_sg_w.astype(BF16)])
    send_sems, recv_sems, part, land, token = _tail_send(part, land)

    dmod_all = jnp.stack([jnp.concatenate([gg[:, 0], gg[:, 1], gd[:, 2]], axis=-1),
                          jnp.concatenate([gc[:, 5], gc[:, 6], ga[:, 2]], axis=-1)], axis=1)
    cols = ada_w.shape[2]
    dmod_cols = lax.dynamic_slice_in_dim(dmod_all, me * cols, cols, axis=2).transpose(1, 0, 2)
    (g_ada_w, d_ada_w, nm_ada_w, nv_ada_w, g_ada_b, d_ada_b, nm_ada_b, nv_ada_b) = _ada_update(
        c_all, dmod_cols, dmod_all, ada_w, m_ada_w, v_ada_w, ada_b, m_ada_b, v_ada_b)

    res = dict(ada_w=[g_ada_w, d_ada_w, nm_ada_w, nv_ada_w], ada_b=[g_ada_b, d_ada_b, nm_ada_b, nv_ada_b])
    (r_od_in,) = _reduce_adam([(l_od_in, od_w_in[0], m_od_w_in[0], v_od_w_in[0])], "adam_od_w_in", after=[token])
    r_ev_out, r_od_out = _reduce_adam([(l_ev_out, ev_w_out[0], m_ev_w_out[0], v_ev_w_out[0]),
                                       (l_od_out, od_w_out[0], m_od_w_out[0], v_od_w_out[0])], "adam_w_out",
                                      after=[token])
    for name, r in (("od_w_in", r_od_in), ("ev_w_out", r_ev_out), ("od_w_out", r_od_out)):
        res[name] = [a[None] for a in r]
    res["od_w_a"], res["od_w_x"], res["ev_sg_w"] = _slots_adam(
        [(a_wa, od_w_a, m_od_w_a, v_od_w_a), (a_wx, od_w_x, m_od_w_x, v_od_w_x),
         (a_sgw, ev_sg_w, m_ev_sg_w, v_ev_sg_w)], "adam_gates", after=[token])
    small = dict(ln_g=(ln_g, m_ln_g, v_ln_g), ln_b=(ln_b, m_ln_b, v_ln_b),
                 ev_sg_ln_g=(ev_sg_ln_g, m_ev_sg_ln_g, v_ev_sg_ln_g),
                 ev_sg_ln_b=(ev_sg_ln_b, m_ev_sg_ln_b, v_ev_sg_ln_b),
                 ev_sink=(ev_sink, m_ev_sink, v_ev_sink), ev_sg_b=(ev_sg_b, m_ev_sg_b, v_ev_sg_b),
                 od_conv_w=(od_conv_w, m_od_conv_w, v_od_conv_w), od_conv_b=(od_conv_b, m_od_conv_b, v_od_conv_b),
                 od_b_a=(od_b_a, m_od_b_a, v_od_b_a), od_b_x=(od_b_x, m_od_b_x, v_od_b_x),
                 od_lam=(od_lam, m_od_lam, v_od_lam))
    small_out = _small_update(ga, gc, gd, gf, gb, ge, gsink, gbt, small)
    l_ev_in = _tail_wait(send_sems, recv_sems, part, land,
                         [r_od_in[0], r_od_out[0], res["od_w_x"][0], g_ada_w, small_out[0]])
    (r_ev_in,) = _reduce_adam([(l_ev_in, ev_w_in[0].T, m_ev_w_in[0].T, v_ev_w_in[0].T)], "adam_ev_w_in")
    res["ev_w_in"] = [a.T[None] for a in r_ev_in]
    loss = small_out[0][0, 0]
    for k, name in enumerate(SMALL_PARAMS):
        res[name] = small_out[1 + 4 * k:5 + 4 * k]

    order = ["ada_w", "ada_b", "ln_g", "ln_b", "ev_w_in", "ev_w_out", "ev_sink", "ev_sg_ln_g", "ev_sg_ln_b",
             "ev_sg_w", "ev_sg_b", "od_w_in", "od_conv_w", "od_conv_b", "od_w_a", "od_b_a", "od_w_x", "od_b_x",
             "od_lam", "od_w_out"]
    outs = [loss, grad_x.reshape(1, T, D)]
    for kind in range(4):
        outs += [res[name][kind] for name in order]
    return tuple(outs)
```

```python
import jax
import jax.numpy as jnp
from jax import lax
from jax.experimental import pallas as pl
from jax.experimental.pallas import tpu as pltpu

F32 = jnp.float32
BF16 = jnp.bfloat16

N_DEV = 8
D = 1024
N_HEADS = 8
HEAD_DIM = 64
ATTN_W = 512
SG_W = 512
SG_GROUPS = 8
SG_DIM = 64
BLK = 128
KVX_W = 1024
EV_IN = 2816
OD_IN = 2048
RNN_HEADS = 8
RNN_HD = 128
ALPHA = 4.0 ** 0.25
LN_EPS = 1e-5
NEG_INF = -1e30
RG_C = 8.0
ROPE_THETA = 500000.0
LR, B1, B2, EPS, WD, STEP = 0.001, 0.9, 0.999, 1e-08, 0.01, 10

LANE = 128
SUBLANE = 8
TM = 256
TMF = 512
TMO = 512
TS = 256
FWD_BLOCKS = 4
VMEM_LIMIT = 56 * 1024 * 1024

MESH = pl.DeviceIdType.MESH


def _pallas(body, **kw):
    return pl.pallas_call(body, **kw)


def _params(sem, vmem=VMEM_LIMIT):
    return pltpu.CompilerParams(dimension_semantics=sem, vmem_limit_bytes=vmem)


def _sigmoid(x):
    return 0.5 * jnp.tanh(0.5 * x) + 0.5


def _silu_and_grad(x):
    s = _sigmoid(x)
    return x * s, s * (1.0 + x * (1.0 - s))


def _dot(a, b):
    return jnp.dot(a.astype(BF16), b.astype(BF16), preferred_element_type=F32)


def _dot_nt(a, b):
    return lax.dot_general(a.astype(BF16), b.astype(BF16), (((1,), (1,)), ((), ())), preferred_element_type=F32)


def _dot_tn(a, b):
    return lax.dot_general(a.astype(BF16), b.astype(BF16), (((0,), (0,)), ((), ())), preferred_element_type=F32)


def _ln_fwd(z, g, b):
    mu = jnp.mean(z, axis=-1, keepdims=True)
    zc = z - mu
    var = jnp.mean(zc * zc, axis=-1, keepdims=True)
    rstd = lax.rsqrt(var + LN_EPS)
    xhat = zc * rstd
    return xhat * g + b, xhat, rstd


def _ln_bwd(dy, xhat, rstd, g):
    dxh = dy * g
    m1 = jnp.mean(dxh, axis=-1, keepdims=True)
    m2 = jnp.mean(dxh * xhat, axis=-1, keepdims=True)
    return rstd * (dxh - m1 - xhat * m2)


def _rowsum(v):
    return jnp.sum(v, axis=0, keepdims=True)


def _rope_fwd(t, c, s1, s2):
    return t * c + pltpu.roll(t, LANE - 8, 1) * s1 + pltpu.roll(t, 8, 1) * s2


def _rope_bwd(d, c, s1, s2):
    return d * c + pltpu.roll(d * s1, 8, 1) + pltpu.roll(d * s2, LANE - 8, 1)


def _adam(w, g, m, v):
    m2 = B1 * m + (1.0 - B1) * g
    v2 = B2 * v + (1.0 - B2) * (g * g)
    m_hat = m2 / (1.0 - B1 ** STEP)
    v_hat = v2 / (1.0 - B2 ** STEP)
    delta = -LR * (m_hat / (jnp.sqrt(v_hat) + EPS) + WD * w)
    return delta, m2, v2


def _tile(rows, width):
    return pl.BlockSpec((rows, width), lambda i: (i, 0))


def _full(shape):
    zeros = (0,) * len(shape)
    return pl.BlockSpec(shape, lambda i: zeros)


def _rev_tile(rows, width, n, reverse):
    if reverse:
        return pl.BlockSpec((rows, width), lambda i: (n - 1 - i, 0))
    return pl.BlockSpec((rows, width), lambda i: (i, 0))


def _halo_specs(rows, width, n, total_rows, reverse):
    per = rows // SUBLANE
    last = total_rows // SUBLANE - 1

    def tile_of(i):
        return (n - 1 - i) if reverse else i

    prev = pl.BlockSpec((SUBLANE, width), lambda i: (jnp.maximum(tile_of(i) * per - 1, 0), 0))
    nxt = pl.BlockSpec((SUBLANE, width), lambda i: (jnp.minimum((tile_of(i) + 1) * per, last), 0))
    return prev, nxt


def _my_pos():
    return lax.axis_index("x"), lax.axis_index("y"), lax.axis_index("c")


def _slot(px, py, pc):
    return 4 * px + 2 * py + pc


class _GatherComm:
    has_mid = True

    def __init__(self, arrs, mid_frac=0.5):
        self.arrs = list(arrs)
        self.n = len(self.arrs)
        self.mid_frac = mid_frac

    def out_shapes(self):
        return [jax.ShapeDtypeStruct((N_DEV,) + a.shape, a.dtype) for a in self.arrs]

    def sems(self):
        return [pltpu.SemaphoreType.DMA((7 * self.n,)), pltpu.SemaphoreType.DMA((7 * self.n,)),
                pltpu.SemaphoreType.DMA((self.n,))]

    def _parts(self, ins, outs, sems):
        send_sems, recv_sems, local_sems = sems
        x, y, c = _my_pos()
        me, sibling = (x, y, c), (x, y, 1 - c)
        chips = [(1 - x, y), (x, 1 - y), (1 - x, 1 - y)]

        def copy(a, k, block, to, src=None):
            dst = outs[a].at[_slot(*block)]
            return pltpu.make_async_remote_copy(
                src_ref=dst if src is None else src, dst_ref=dst,
                send_sem=send_sems.at[a * 7 + k], recv_sem=recv_sems.at[a * 7 + k],
                device_id=to, device_id_type=MESH)

        local = [pltpu.make_async_copy(ins[a], outs[a].at[_slot(*me)], local_sems.at[a]) for a in range(self.n)]
        first = []
        for a in range(self.n):
            first.append(copy(a, 0, me, sibling, src=ins[a]))
            first += [copy(a, 1 + j, me, (*chip, c), src=ins[a]) for j, chip in enumerate(chips)]
        ici_in = [copy(a, 1 + j, (*chip, c), me) for j, chip in enumerate(chips) for a in range(self.n)]
        passed = [copy(a, 4 + j, (*chip, c), sibling) for j, chip in enumerate(chips) for a in range(self.n)]
        d2d_in = []
        for a in range(self.n):
            d2d_in.append(copy(a, 0, sibling, me))
            d2d_in += [copy(a, 4 + j, (*chip, 1 - c), me) for j, chip in enumerate(chips)]
        return local, first, ici_in, passed, d2d_in

    def start(self, ins, outs, sems):
        local, first, _, _, _ = self._parts(ins, outs, sems)
        for cp in local + first:
            cp.start()

    def mid(self, ins, outs, sems):
        _, _, ici_in, passed, _ = self._parts(ins, outs, sems)
        for arrived, fw in zip(ici_in, passed):
            arrived.wait_recv()
            fw.start()

    def finish(self, ins, outs, sems):
        local, first, _, passed, d2d_in = self._parts(ins, outs, sems)
        for cp in d2d_in:
            cp.wait_recv()
        for cp in first + passed:
            cp.wait_send()
        for cp in local:
            cp.wait()


class _ExchangeComm:
    has_mid = False

    def __init__(self, arrs):
        self.arrs = list(arrs)
        self.n = len(self.arrs)

    def out_shapes(self):
        return [jax.ShapeDtypeStruct(a.shape, a.dtype) for a in self.arrs]

    def sems(self):
        return [pltpu.SemaphoreType.DMA((7 * self.n,)), pltpu.SemaphoreType.DMA((7 * self.n,)),
                pltpu.SemaphoreType.DMA((self.n,))]

    def _copies(self, ins, outs, sems):
        send_sems, recv_sems, local_sems = sems
        x, y, c = _my_pos()
        mine = _slot(x, y, c)
        copies = [pltpu.make_async_copy(ins[a].at[mine], outs[a].at[mine], local_sems.at[a]) for a in range(self.n)]
        for k in range(1, N_DEV):
            px = (1 - x) if (k & 4) else x
            py = (1 - y) if (k & 2) else y
            pc = (1 - c) if (k & 1) else c
            for a in range(self.n):
                copies.append(pltpu.make_async_remote_copy(
                    src_ref=ins[a].at[_slot(px, py, pc)], dst_ref=outs[a].at[mine],
                    send_sem=send_sems.at[a * 7 + k - 1], recv_sem=recv_sems.at[a * 7 + k - 1],
                    device_id=(px, py, pc), device_id_type=MESH))
        return copies

    def start(self, ins, outs, sems):
        for cp in self._copies(ins, outs, sems):
            cp.start()

    def finish(self, ins, outs, sems):
        for cp in self._copies(ins, outs, sems):
            cp.wait()


class _BothComm:
    has_mid = True

    def __init__(self, first, second):
        self.parts = (first, second)
        self.arrs = first.arrs + second.arrs
        self.n = first.n + second.n
        self.mid_frac = second.mid_frac

    def out_shapes(self):
        return self.parts[0].out_shapes() + self.parts[1].out_shapes()

    def sems(self):
        return self.parts[0].sems() + self.parts[1].sems()

    def _each(self, ins, outs, sems):
        a, b = self.parts
        return ((a, ins[:a.n], outs[:a.n], sems[:3]), (b, ins[a.n:], outs[a.n:], sems[3:]))

    def start(self, ins, outs, sems):
        for cm, i_, o_, s_ in self._each(ins, outs, sems):
            cm.start(i_, o_, s_)

    def mid(self, ins, outs, sems):
        for cm, i_, o_, s_ in self._each(ins, outs, sems):
            if cm.has_mid:
                cm.mid(i_, o_, s_)

    def finish(self, ins, outs, sems):
        for cm, i_, o_, s_ in self._each(ins, outs, sems):
            cm.finish(i_, o_, s_)


def _fused_call(body, comm, operands, *, name, grid, in_specs, out_specs, out_shape, scratch_shapes=(),
                semantics=("arbitrary",)):
    n_in, n_out, n_scr = len(in_specs), len(out_specs), len(scratch_shapes)
    if comm is None:
        res = _pallas(body, name=name, grid=grid, in_specs=list(in_specs), out_specs=list(out_specs),
                      out_shape=list(out_shape), scratch_shapes=list(scratch_shapes),
                      compiler_params=_params(semantics))(*operands)
        return list(res), []
    k = comm.n
    steps = grid[0]

    def wrapped(*refs):
        ins, cins = refs[:n_in], refs[n_in:n_in + k]
        outs = refs[n_in + k:n_in + k + n_out]
        couts = refs[n_in + k + n_out:n_in + 2 * k + n_out]
        rest = refs[n_in + 2 * k + n_out:]
        scratch, sems = rest[:n_scr], rest[n_scr:]
        i = pl.program_id(0)

        @pl.when(i == 0)
        def _():
            comm.start(cins, couts, sems)

        body(*ins, *outs, *scratch)

        if comm.has_mid:
            @pl.when(i == int(steps * comm.mid_frac))
            def _():
                comm.mid(cins, couts, sems)

        @pl.when(i == steps - 1)
        def _():
            comm.finish(cins, couts, sems)

    any_spec = pl.BlockSpec(memory_space=pl.ANY)
    res = _pallas(wrapped, name=name, grid=grid, in_specs=list(in_specs) + [any_spec] * k,
                  out_specs=list(out_specs) + [any_spec] * k, out_shape=list(out_shape) + comm.out_shapes(),
                  scratch_shapes=list(scratch_shapes) + comm.sems(),
                  compiler_params=_params(("arbitrary",)))(*operands, *comm.arrs)
    return list(res[:n_out]), list(res[n_out:])


def _head_gather(c, ada_w, big, to_cast, vec_parts):
    cols = ada_w.shape[2]
    g_c, g_big = _GatherComm([c]), _GatherComm(big)
    g_mod = _GatherComm([jax.ShapeDtypeStruct((2, N_DEV, cols), F32)])
    g_vec = _GatherComm([jax.ShapeDtypeStruct((VEC_ROWS, LANE), F32)])
    nb, nc, nv = g_big.n, len(to_cast), len(vec_parts)

    def body(*refs):
        c_ref, w_ref = refs[0], refs[1]
        vec_in = refs[2:2 + nv]
        cast_in = refs[2 + nv:2 + nv + nc]
        big_in = refs[2 + nv + nc:2 + nv + nc + nb]
        outs = refs[2 + nv + nc + nb:]
        c_all_ref, mod_all_ref, vec_all_ref = outs[0], outs[1], outs[2]
        cast_out = outs[3:3 + nc]
        big_out = outs[3 + nc:3 + nc + nb]
        part_ref, pack_ref = outs[3 + nc + nb], outs[4 + nc + nb]
        sems = outs[5 + nc + nb:]
        s_c, s_mod, s_big, s_vec = sems[0:3], sems[3:6], sems[6:9], sems[9:12]
        g_c.start([c_ref], [c_all_ref], s_c)
        g_big.start(big_in, big_out, s_big)
        pack_ref[...] = jnp.zeros_like(pack_ref)
        row = 0
        for ref, (_, nrows) in zip(vec_in, VEC_LAYOUT):
            pack_ref[row:row + nrows, :] = ref[0] if len(ref.shape) == 3 else ref[...]
            row += nrows
        g_vec.start([pack_ref], [vec_all_ref], s_vec)
        g_c.mid([c_ref], [c_all_ref], s_c)
        g_c.finish([c_ref], [c_all_ref], s_c)
        cv = c_all_ref[:, 0, :]
        cond = cv * _sigmoid(cv)
        for l in range(2):
            part_ref[l] = _dot(cond, w_ref[l])
        g_mod.start([part_ref], [mod_all_ref], s_mod)
        for src, dst in zip(cast_in, cast_out):
            dst[...] = src[...].astype(BF16)
        for g, ins, outs_, sm in ((g_vec, [pack_ref], [vec_all_ref], s_vec), (g_mod, [part_ref], [mod_all_ref], s_mod),
                                  (g_big, big_in, big_out, s_big)):
            g.mid(ins, outs_, sm)
            g.finish(ins, outs_, sm)

    any_spec = pl.BlockSpec(memory_space=pl.ANY)
    vmem_spec = pl.BlockSpec(memory_space=pltpu.VMEM)
    res = _pallas(
        body, name="head_gather",
        out_shape=(g_c.out_shapes() + g_mod.out_shapes() + g_vec.out_shapes()
                   + [jax.ShapeDtypeStruct(a.shape, BF16) for a in to_cast] + g_big.out_shapes()),
        in_specs=[vmem_spec] * (2 + nv + nc) + [any_spec] * nb,
        out_specs=[vmem_spec] * (3 + nc) + [any_spec] * nb,
        scratch_shapes=[pltpu.VMEM((2, N_DEV, cols), F32), pltpu.VMEM((VEC_ROWS, LANE), F32)]
        + g_c.sems() + g_mod.sems() + g_big.sems() + g_vec.sems(),
        compiler_params=pltpu.CompilerParams(vmem_limit_bytes=VMEM_LIMIT),
    )(c, ada_w, *vec_parts, *to_cast, *big)
    return res[0], res[1], res[2], list(res[3 + nc:]), list(res[3:3 + nc])


def _ada_update(c_all, dmod_cols, dmod_all, ada_w, m_w, v_w, ada_b, m_b, v_b):
    cols = ada_w.shape[2]
    nb = ada_b.shape[1]

    def body(c_ref, dmc_ref, dma_ref, w_ref, mw_ref, vw_ref, b_ref, mb_ref, vb_ref,
             gw_ref, dw_ref, nmw_ref, nvw_ref, gb_ref, db_ref, nmb_ref, nvb_ref):
        cv = c_ref[...]
        cond = cv * _sigmoid(cv)
        for l in range(2):
            g = _dot_tn(cond, dmc_ref[l])
            gw_ref[l] = g
            dlt, m2, v2 = _adam(w_ref[l], g, mw_ref[l], vw_ref[l])
            dw_ref[l] = dlt
            nmw_ref[l] = m2
            nvw_ref[l] = v2
        gb = dma_ref[0]
        for i in range(1, N_DEV):
            gb = gb + dma_ref[i]
        gb_ref[...] = gb
        dlt, m2, v2 = _adam(b_ref[...], gb, mb_ref[...], vb_ref[...])
        db_ref[...] = dlt
        nmb_ref[...] = m2
        nvb_ref[...] = v2

    wspec = _full((2, D, cols))
    bspec = _full((2, nb))
    wshape = jax.ShapeDtypeStruct((2, D, cols), F32)
    bshape = jax.ShapeDtypeStruct((2, nb), F32)
    return _pallas(
        body, name="ada_update", grid=(1,),
        in_specs=[_full((N_DEV, D)), _full((2, N_DEV, cols)), _full((N_DEV, 2, nb)),
                  wspec, wspec, wspec, bspec, bspec, bspec],
        out_specs=[wspec] * 4 + [bspec] * 4,
        out_shape=[wshape] * 4 + [bshape] * 4,
        compiler_params=_params(("arbitrary",)),
    )(c_all, dmod_cols, dmod_all, ada_w, m_w, v_w, ada_b, m_b, v_b)


def _ev_in(x, mod, w_in, rc, rs1, rs2, comm=None):
    T = x.shape[0]

    def body(x_ref, mod_ref, w_ref, c_ref, s1_ref, s2_ref, q_ref, kv_ref, su_ref, sv_ref, g_ref):
        h = x_ref[...] * (1.0 + mod_ref[1:2, :]) + mod_ref[0:1, :]
        p = _dot_nt(h, w_ref[...])
        c, s1, s2 = c_ref[...], s1_ref[...], s2_ref[...]
        for j in range(ATTN_W // LANE):
            qr = _rope_fwd(p[:, j * LANE:(j + 1) * LANE], c, s1, s2)
            q_ref[:, j * LANE:(j + 1) * LANE] = (qr * (HEAD_DIM ** -0.5)).astype(BF16)
        low = lax.broadcasted_iota(jnp.int32, (TMF, LANE), 1) < HEAD_DIM
        for j, val in enumerate((_rope_fwd(p[:, 512:640], c, s1, s2), p[:, 640:768])):
            swapped = pltpu.roll(val, HEAD_DIM, 1)
            tiles = (jnp.where(low, val, 0.0), jnp.where(low, 0.0, swapped),
                     jnp.where(low, swapped, 0.0), jnp.where(low, 0.0, val))
            for k, tile in enumerate(tiles):
                kv_ref[:, (4 * j + k) * LANE:(4 * j + k + 1) * LANE] = tile.astype(BF16)
        su_ref[...] = p[:, 768:1280].astype(BF16)
        sv_ref[...] = p[:, 1280:1792].astype(BF16)
        g_ref[...] = p[:, 1792:2816].astype(BF16)

    sh = lambda w: jax.ShapeDtypeStruct((T, w), BF16)
    return _fused_call(
        body, comm, (x, mod, w_in, rc, rs1, rs2), name="ev_in", grid=(T // TMF,),
        in_specs=[_tile(TMF, D), _full((3, D)), _full((EV_IN, D)), _tile(TMF, LANE), _tile(TMF, LANE),
                  _tile(TMF, LANE)],
        out_specs=[_tile(TMF, ATTN_W), _tile(TMF, KVX_W), _tile(TMF, SG_W), _tile(TMF, SG_W), _tile(TMF, D)],
        out_shape=[sh(ATTN_W), sh(KVX_W), sh(SG_W), sh(SG_W), sh(D)], semantics=("parallel",))


def _band_specs(width, nb, k):
    return [pl.BlockSpec((BLK, width), lambda n: (jnp.maximum(k * n - 1, 0), 0)),
            pl.BlockSpec((k * BLK, width), lambda n: (n, 0)),
            pl.BlockSpec((BLK, width), lambda n: (jnp.minimum(k * n + k, nb - 1), 0))]


def _band_bias(bias_ref, n, nb):
    rows = lax.broadcasted_iota(jnp.int32, (3 * BLK, 1), 0)
    outside = ((rows < BLK) & (n == 0)) | ((rows >= 2 * BLK) & (n == nb - 1))
    return bias_ref[...] + jnp.where(outside, NEG_INF, 0.0)


def _lane_tile(ref, t):
    return ref[:, t * LANE:(t + 1) * LANE]


def _split_bf16(v):
    hi = v.astype(BF16)
    return hi, (v - hi.astype(F32)).astype(BF16)


def _group_mean(v, a_ref, exact_bf16=False):
    hi, lo = _split_bf16(v)
    a = a_ref[...]
    out = []
    for t in range(SG_W // (2 * LANE)):
        sl = slice(t * 2 * LANE, (t + 1) * 2 * LANE)
        r = jnp.dot(hi[:, sl], a, preferred_element_type=F32)
        if not exact_bf16:
            r = r + jnp.dot(lo[:, sl], a, preferred_element_type=F32)
        out.append(r)
    return jnp.concatenate(out, axis=-1)


def _sg_core(sv_ref, lng, lnb, a_ref, w_ref, bfull_ref):
    svf = sv_ref[...].astype(F32)
    xc = svf - _group_mean(svf, a_ref, exact_bf16=True)
    rstd = lax.rsqrt(_group_mean(xc * xc, a_ref) + LN_EPS)
    xhat = xc * rstd
    vb = (xhat * lng + lnb).astype(BF16)
    low = lax.broadcasted_iota(jnp.int32, (BLK, LANE), 1) < SG_DIM
    tiles = []
    for t in range(SG_W // LANE):
        v2 = vb[:, t * LANE:(t + 1) * LANE]
        r0 = jnp.dot(w_ref[2 * t], v2, preferred_element_type=F32)
        r1 = jnp.dot(w_ref[2 * t + 1], v2, preferred_element_type=F32)
        tiles.append(jnp.where(low, r0, r1))
    svm = jnp.concatenate(tiles, axis=-1) + bfull_ref[...]
    return xhat, rstd, vb, svm


def _mix0_fwd(q, kvx, su, sv, g0, sink_l, bias, a128, sg_lng, sg_lnb, sg_w, sg_bfull, comm=None):
    T = q.shape[0]
    nb = T // BLK

    def body(q_ref, kp_ref, kc_ref, kn_ref, su_ref, sv_ref, g_ref, sink_ref, bias_ref, a_ref, lng_ref, lnb_ref,
             w_ref, bfull_ref, ycat_ref, y0_ref, lse_ref):
        n = pl.program_id(0)
        kvx4 = jnp.concatenate([kp_ref[...], kc_ref[...], kn_ref[...]], axis=0)
        for s in range(FWD_BLOCKS):
            rows = slice(s * BLK, (s + 1) * BLK)
            bias = _band_bias(bias_ref, FWD_BLOCKS * n + s, nb)
            kvx = kvx4[s * BLK:s * BLK + 3 * BLK]
            tiles = []
            for t in range(ATTN_W // LANE):
                qt = q_ref[rows, t * LANE:(t + 1) * LANE]
                acc = None
                for par in range(2):
                    h = 2 * t + par
                    kt = 2 * (h // 4) + par
                    ke = kvx[:, kt * LANE:(kt + 1) * LANE]
                    ve = kvx[:, (4 + kt) * LANE:(5 + kt) * LANE]
                    st = _dot_nt(ke, qt) + bias
                    sk = _lane_tile(sink_ref, h)
                    m = jnp.maximum(jnp.max(st, axis=0, keepdims=True), sk)
                    p = jnp.exp(st - m)
                    denom = jnp.sum(p, axis=0, keepdims=True) + jnp.exp(sk - m)
                    contrib = _dot_tn(p * (1.0 / denom), ve)
                    acc = contrib if acc is None else acc + contrib
                    lse_ref[s, :, h * LANE:(h + 1) * LANE] = m + jnp.log(denom)
                tiles.append(acc)
            _, _, _, svm = _sg_core(sv_ref.at[rows, :], lng_ref[...], lnb_ref[...], a_ref, w_ref, bfull_ref)
            tiles.append(su_ref[rows, :].astype(F32) * svm)
            ycat = jnp.concatenate(tiles, axis=-1)
            gf = g_ref[rows, :].astype(F32)
            ycat_ref[rows, :] = ycat.astype(BF16)
            y0_ref[rows, :] = (ycat * (gf * _sigmoid(gf))).astype(BF16)

    two = FWD_BLOCKS * BLK
    return _fused_call(
        body, comm, (q, kvx, kvx, kvx, su, sv, g0, sink_l, bias, a128, sg_lng, sg_lnb, sg_w, sg_bfull),
        name="mix0_fwd", grid=(nb // FWD_BLOCKS,),
        in_specs=[_tile(two, ATTN_W)] + _band_specs(KVX_W, nb, FWD_BLOCKS) + [
                  _tile(two, SG_W), _tile(two, SG_W), _tile(two, D), _full((1, N_HEADS * LANE)),
                  _full((3 * BLK, LANE)), _full((2 * LANE, 2 * LANE)),_full((1, SG_W)), _full((1, SG_W)),
                  _full((SG_GROUPS, BLK, BLK)), _full((BLK, SG_W))],
        out_specs=[_tile(two, D), _tile(two, D),
                   pl.BlockSpec((FWD_BLOCKS, 1, N_HEADS * LANE), lambda n: (n, 0, 0))],
        out_shape=[jax.ShapeDtypeStruct((T, D), BF16), jax.ShapeDtypeStruct((T, D), BF16),
                   jax.ShapeDtypeStruct((nb, 1, N_HEADS * LANE), F32)], semantics=("parallel",))


def _ev_out(y0, w_out, x, mod, lnp):
    T = x.shape[0]

    def body(y_ref, w_ref, x_ref, mod_ref, ln_ref, out_ref, z_ref, x1_ref):
        out = _dot(y_ref[...], w_ref[...])
        z = ALPHA * x_ref[...] + mod_ref[2:3, :] * out
        x1, _, _ = _ln_fwd(z, ln_ref[0:1, :], ln_ref[1:2, :])
        out_ref[...] = out.astype(BF16)
        z_ref[...] = z
        x1_ref[...] = x1

    return _pallas(
        body, name="ev_out", grid=(T // TMF,),
        in_specs=[_tile(TMF, D), _full((D, D)), _tile(TMF, D), _full((3, D)), _full((2, D))],
        out_specs=[_tile(TMF, D)] * 3,
        out_shape=[jax.ShapeDtypeStruct((T, D), BF16), jax.ShapeDtypeStruct((T, D), F32),
                   jax.ShapeDtypeStruct((T, D), F32)],
        compiler_params=_params(("parallel",)),
    )(y0, w_out, x, mod, lnp)


def _od_in(x1, mod, w_in):
    T = x1.shape[0]

    def body(x_ref, mod_ref, w_ref, xr_ref, g_ref):
        h = x_ref[...] * (1.0 + mod_ref[1:2, :]) + mod_ref[0:1, :]
        p = _dot(h, w_ref[...])
        xr_ref[...] = p[:, :D]
        g_ref[...] = p[:, D:].astype(BF16)

    return _pallas(
        body, name="od_in", grid=(T // TMF,),
        in_specs=[_tile(TMF, D), _full((3, D)), _full((D, OD_IN))],
        out_specs=[_tile(TMF, D), _tile(TMF, D)],
        out_shape=[jax.ShapeDtypeStruct((T, D), F32), jax.ShapeDtypeStruct((T, D), BF16)],
        compiler_params=_params(("parallel",)),
    )(x1, mod, w_in)


def _ext_rows(prev_ref, cur, next_ref, j, n):
    prev = jnp.where(j > 0, prev_ref[...], 0.0)
    nxt = jnp.where(j < n - 1, next_ref[...], 0.0)
    return jnp.concatenate([prev, cur, nxt], axis=0)


def _shift_rows(ext, off, rows):
    total = ext.shape[0]
    if off == 0:
        return ext[SUBLANE:SUBLANE + rows, :]
    return pltpu.roll(ext, (-off) % total, 0)[SUBLANE:SUBLANE + rows, :]


def _conv_fwd(ext, cw, cb, rows):
    xc = cb
    for k in range(4):
        xc = xc + cw[k:k + 1, :] * _shift_rows(ext, k - 2, rows)
    return xc


def _gates(xc, wa_ref, wx_ref, ba, bx, lam):
    pr, pi = [], []
    for h in range(RNN_HEADS):
        xh = xc[:, h * RNN_HD:(h + 1) * RNN_HD].astype(BF16)
        pr.append(_dot(xh, wa_ref[h]))
        pi.append(_dot(xh, wx_ref[h]))
    r = _sigmoid(jnp.concatenate(pr, axis=-1) + ba)
    ig = _sigmoid(jnp.concatenate(pi, axis=-1) + bx)
    sp = jnp.maximum(-lam, 0.0) + jnp.log(1.0 + jnp.exp(-jnp.abs(lam)))
    neg_log_a = RG_C * r * sp
    a = jnp.exp(-neg_log_a)
    s2 = (1.0 + a * a) * jnp.tanh(neg_log_a)
    inv_s = lax.rsqrt(jnp.maximum(s2, 1e-30))
    return r, ig, sp, a, s2 * inv_s, inv_s


def _scan_tile(a_ref, b_ref, o_ref, carry_ref, rows, reverse):
    ridx = lax.broadcasted_iota(jnp.int32, (SUBLANE, D), 0)
    groups = rows // SUBLANE

    def group(gi, h):
        g = (groups - 1 - gi) if reverse else gi
        off = pl.multiple_of(g * SUBLANE, SUBLANE)
        a = a_ref[pl.ds(off, SUBLANE), :]
        b = b_ref[pl.ds(off, SUBLANE), :]
        for sh in (1, 2, 4):
            if reverse:
                keep = ridx < SUBLANE - sh
                a_p = jnp.where(keep, pltpu.roll(a, SUBLANE - sh, 0), 1.0)
                b_p = jnp.where(keep, pltpu.roll(b, SUBLANE - sh, 0), 0.0)
            else:
                keep = ridx >= sh
                a_p = jnp.where(keep, pltpu.roll(a, sh, 0), 1.0)
                b_p = jnp.where(keep, pltpu.roll(b, sh, 0), 0.0)
            b = b + a * b_p
            a = a * a_p
        hh = b + a * h
        o_ref[pl.ds(off, SUBLANE), :] = hh
        return hh[0:1, :] if reverse else hh[SUBLANE - 1:SUBLANE, :]

    carry_ref[...] = lax.fori_loop(0, groups, group, carry_ref[...])


def _rglru_fwd(xr, cw, cb, wa, wx, ba, bx, lam, reverse, name):
    T = xr.shape[0]
    n = T // TS
    prev_spec, next_spec = _halo_specs(TS, D, n, T, reverse)

    def body(prev_ref, cur_ref, next_ref, cw_ref, cb_ref, wa_ref, wx_ref, ba_ref, bx_ref, lam_ref,
             h_ref, a_ref, s_ref, r_ref, ig_ref, xc_ref, b_s, carry):
        i = pl.program_id(0)
        j = (n - 1 - i) if reverse else i

        @pl.when(i == 0)
        def _():
            carry[...] = jnp.zeros_like(carry)

        ext = _ext_rows(prev_ref, cur_ref[...], next_ref, j, n)
        xc = _conv_fwd(ext, cw_ref[...], cb_ref[...], TS)
        r, ig, _, a, s, _ = _gates(xc, wa_ref, wx_ref, ba_ref[...], bx_ref[...], lam_ref[...])
        s_ref[...] = s
        r_ref[...] = r.astype(BF16)
        ig_ref[...] = ig.astype(BF16)
        xc_ref[...] = xc.astype(BF16)
        a_ref[...] = a
        b_s[...] = s * ig * xc
        _scan_tile(a_ref, b_s, h_ref, carry, TS, reverse)

    wspec = _full((RNN_HEADS, RNN_HD, RNN_HD))
    cur = _rev_tile(TS, D, n, reverse)
    f32 = jax.ShapeDtypeStruct((T, D), F32)
    b16 = jax.ShapeDtypeStruct((T, D), BF16)
    return _pallas(
        body, name=name, grid=(n,),
        in_specs=[prev_spec, cur, next_spec, _full((4, D)), _full((1, D)),
                  wspec, wspec, _full((1, D)), _full((1, D)), _full((1, D))],
        out_specs=[cur] * 6,
        out_shape=[f32, f32, f32, b16, b16, b16],
        scratch_shapes=[pltpu.VMEM((TS, D), F32), pltpu.VMEM((1, D), F32)],
        compiler_params=_params(("arbitrary",)),
    )(xr, xr, xr, cw, cb, wa, wx, ba, bx, lam)


def _od_out(hf, hb, g1, w_out, x1, tgt, mod, lnp):
    T = x1.shape[0]

    def body(hf_ref, hb_ref, g_ref, w_ref, x_ref, t_ref, mod_ref, ln_ref,
             dh_ref, dg_ref, dx_ref, dwb_ref, vec_ref, dw_ref):
        i = pl.program_id(0)

        @pl.when(i == 0)
        def _():
            dw_ref[...] = jnp.zeros_like(dw_ref)
            vec_ref[...] = jnp.zeros_like(vec_ref)

        hs = hf_ref[...] + hb_ref[...]
        sg, dsg = _silu_and_grad(g_ref[...].astype(F32))
        yr = (hs * sg).astype(BF16)
        w = w_ref[...]
        out = _dot(yr, w)
        gate = mod_ref[2:3, :]
        z = ALPHA * x_ref[...] + gate * out
        lng = ln_ref[0:1, :]
        x2, xhat, rstd = _ln_fwd(z, lng, ln_ref[1:2, :])
        diff = x2 - t_ref[...]
        vec_ref[3:4, 0:LANE] += 0.5 * jnp.sum(diff * diff) * (1.0 / D)
        dx2 = diff * (1.0 / D)
        dz = _ln_bwd(dx2, xhat, rstd, lng)
        vec_ref[0:1, :] += _rowsum(dx2 * xhat)
        vec_ref[1:2, :] += _rowsum(dx2)
        vec_ref[2:3, :] += _rowsum(dz * out)
        dout = (dz * gate).astype(BF16)
        dyr = _dot_nt(dout, w)
        dw_ref[...] += _dot_tn(yr, dout)
        dh_ref[...] = dyr * sg
        dg_ref[...] = (dyr * hs * dsg).astype(BF16)
        dx_ref[...] = ALPHA * dz

        @pl.when(i == T // TMO - 1)
        def _():
            dwb_ref[...] = dw_ref[...].astype(BF16)

    return _pallas(
        body, name="od_out", grid=(T // TMO,),
        in_specs=[_tile(TMO, D), _tile(TMO, D), _tile(TMO, D), _full((D, D)), _tile(TMO, D), _tile(TMO, D),
                  _full((3, D)), _full((2, D))],
        out_specs=[_tile(TMO, D), _tile(TMO, D), _tile(TMO, D), _full((D, D)), _full((SUBLANE, D))],
        out_shape=[jax.ShapeDtypeStruct((T, D), F32), jax.ShapeDtypeStruct((T, D), BF16),
                   jax.ShapeDtypeStruct((T, D), F32), jax.ShapeDtypeStruct((D, D), BF16),
                   jax.ShapeDtypeStruct((SUBLANE, D), F32)],
        scratch_shapes=[pltpu.VMEM((D, D), F32)],
        compiler_params=_params(("arbitrary",)),
    )(hf, hb, g1, w_out, x1, tgt, mod, lnp)


def _rglru_bwd(fwd, dh, wa, wx, lam, reverse, name, comm=None):
    h, a_all, s_all, r_all, ig_all, xc_all = fwd
    T = h.shape[0]
    n = T // TS
    adj_rev = not reverse
    hprev_spec, hnext_spec = _halo_specs(TS, D, n, T, adj_rev)
    h_halo_spec = hnext_spec if reverse else hprev_spec

    def body(dh_ref, h_ref, hh_ref, a_ref, s_ref, r_ref, ig_ref, xc_ref, wa_ref, wx_ref, lam_ref,
             dxc_ref, dwa_ref, dwx_ref, vec_ref, a_s, l_s, carry, a_edge):
        i = pl.program_id(0)
        j = (n - 1 - i) if adj_rev else i

        @pl.when(i == 0)
        def _():
            carry[...] = jnp.zeros_like(carry)
            a_edge[...] = jnp.zeros_like(a_edge)
            dwa_ref[...] = jnp.zeros_like(dwa_ref)
            dwx_ref[...] = jnp.zeros_like(dwx_ref)
            vec_ref[...] = jnp.zeros_like(vec_ref)

        lam = lam_ref[...]
        sp = jnp.maximum(-lam, 0.0) + jnp.log(1.0 + jnp.exp(-jnp.abs(lam)))
        a, s = a_ref[...], s_ref[...]
        inv_s = lax.rsqrt(jnp.maximum(s * s, 1e-30))
        r, ig = r_ref[...].astype(F32), ig_ref[...].astype(F32)
        xcb = xc_ref[...]
        xc = xcb.astype(F32)

        rows = lax.broadcasted_iota(jnp.int32, (TS, D), 0)
        hcur = h_ref[...]
        if reverse:
            a_sh = jnp.where(rows == 0, a_edge[...], pltpu.roll(a, 1, 0))
            halo = jnp.where(j < n - 1, hh_ref[0:1, :], 0.0)
            h_nb = jnp.where(rows == TS - 1, halo, pltpu.roll(hcur, TS - 1, 0))
        else:
            a_sh = jnp.where(rows == TS - 1, a_edge[...], pltpu.roll(a, TS - 1, 0))
            halo = jnp.where(j > 0, hh_ref[SUBLANE - 1:SUBLANE, :], 0.0)
            h_nb = jnp.where(rows == 0, halo, pltpu.roll(hcur, 1, 0))
        a_s[...] = a_sh
        _scan_tile(a_s, dh_ref, l_s, carry, TS, adj_rev)
        a_edge[...] = a[TS - 1:TS, :] if reverse else a[0:1, :]

        lm = l_s[...]
        da = lm * h_nb
        di = lm * s * xc
        dxc = lm * s * ig
        ds = lm * ig * xc
        dlog_a = a * (da - ds * a * inv_s)
        dr = (-RG_C) * sp * dlog_a
        dsp = _rowsum((-RG_C) * r * dlog_a)
        dpr = dr * r * (1.0 - r)
        dpi = di * ig * (1.0 - ig)
        vec_ref[0:1, :] += _rowsum(dpr)
        vec_ref[1:2, :] += _rowsum(dpi)
        vec_ref[2:3, :] += dsp * (-_sigmoid(-lam))
        parts = []
        for hd in range(RNN_HEADS):
            sl = slice(hd * RNN_HD, (hd + 1) * RNN_HD)
            xh = xcb[:, sl]
            dprh = dpr[:, sl].astype(BF16)
            dpih = dpi[:, sl].astype(BF16)
            parts.append(_dot_nt(dprh, wa_ref[hd]) + _dot_nt(dpih, wx_ref[hd]))
            dwa_ref[hd] += _dot_tn(xh, dprh)
            dwx_ref[hd] += _dot_tn(xh, dpih)
        dxc_ref[...] = dxc + jnp.concatenate(parts, axis=-1)

    wspec = _full((RNN_HEADS, RNN_HD, RNN_HD))
    cur = _rev_tile(TS, D, n, adj_rev)
    return _fused_call(
        body, comm, (dh, h, h, a_all, s_all, r_all, ig_all, xc_all, wa, wx, lam), name=name, grid=(n,),
        in_specs=[cur, cur, h_halo_spec, cur, cur, cur, cur, cur, wspec, wspec, _full((1, D))],
        out_specs=[cur, wspec, wspec, _full((SUBLANE, D))],
        out_shape=[jax.ShapeDtypeStruct((T, D), F32),
                   jax.ShapeDtypeStruct((RNN_HEADS, RNN_HD, RNN_HD), F32),
                   jax.ShapeDtypeStruct((RNN_HEADS, RNN_HD, RNN_HD), F32),
                   jax.ShapeDtypeStruct((SUBLANE, D), F32)],
        scratch_shapes=[pltpu.VMEM((TS, D), F32)] * 2 + [pltpu.VMEM((1, D), F32)] * 2)


def _od_in_bwd(dxcf, dxcb, xr, dg1, x1, dx1p, mod, w_in, cw, comm=None):
    T = x1.shape[0]
    n = T // TMO
    slab = OD_IN // N_DEV
    prev_spec, next_spec = _halo_specs(TMO, D, n, T, False)

    def body(fp_ref, fc_ref, fn_ref, bp_ref, bc_ref, bn_ref, xr_ref, dg_ref, x1_ref, dxp_ref,
             mod_ref, w_ref, cw_ref, dx_ref, dwb_ref, vec_ref, dw_ref):
        i = pl.program_id(0)

        @pl.when(i == 0)
        def _():
            dw_ref[...] = jnp.zeros_like(dw_ref)
            vec_ref[...] = jnp.zeros_like(vec_ref)

        dcur = fc_ref[...] + bc_ref[...]
        dprev = jnp.where(i > 0, fp_ref[...] + bp_ref[...], 0.0)
        dnext = jnp.where(i < n - 1, fn_ref[...] + bn_ref[...], 0.0)
        dext = jnp.concatenate([dprev, dcur, dnext], axis=0)
        xr_v = xr_ref[...]
        cw_v = cw_ref[...]
        dxr = None
        for k in range(4):
            shifted = _shift_rows(dext, 2 - k, TMO)
            term = cw_v[k:k + 1, :] * shifted
            dxr = term if dxr is None else dxr + term
            vec_ref[k:k + 1, :] += _rowsum(shifted * xr_v)
        vec_ref[4:5, :] += _rowsum(dcur)
        dp = jnp.concatenate([dxr.astype(BF16), dg_ref[...]], axis=-1)
        x1v = x1_ref[...]
        scale1 = 1.0 + mod_ref[1:2, :]
        h1 = (x1v * scale1 + mod_ref[0:1, :]).astype(BF16)
        dh1 = _dot_nt(dp, w_ref[...])
        dw_ref[...] += _dot_tn(h1, dp)
        dx_ref[...] = dxp_ref[...] + dh1 * scale1
        vec_ref[5:6, :] += _rowsum(dh1)
        vec_ref[6:7, :] += _rowsum(dh1 * x1v)

        @pl.when(i == n - 1)
        def _():
            for j in range(N_DEV):
                dwb_ref[j] = dw_ref[:, j * slab:(j + 1) * slab].astype(BF16)

    t = _tile(TMO, D)
    return _fused_call(
        body, comm, (dxcf, dxcf, dxcf, dxcb, dxcb, dxcb, xr, dg1, x1, dx1p, mod, w_in, cw),
        name="od_in_bwd", grid=(n,),
        in_specs=[prev_spec, t, next_spec, prev_spec, t, next_spec, t, t, t, t,
                  _full((3, D)), _full((D, OD_IN)), _full((4, D))],
        out_specs=[t, _full((N_DEV, D, slab)), _full((SUBLANE, D))],
        out_shape=[jax.ShapeDtypeStruct((T, D), F32), jax.ShapeDtypeStruct((N_DEV, D, slab), BF16),
                   jax.ShapeDtypeStruct((SUBLANE, D), F32)],
        scratch_shapes=[pltpu.VMEM((D, OD_IN), F32)])


def _ev_out_bwd(dx1, z0, out0, y0, ycat, g0, w_out, mod, lnp):
    T = dx1.shape[0]

    def body(dx_ref, z_ref, out_ref, y0_ref, yc_ref, g_ref, w_ref, mod_ref, ln_ref,
             dxp_ref, dyc_ref, dg_ref, dwb_ref, vec_ref, dw_ref):
        i = pl.program_id(0)

        @pl.when(i == 0)
        def _():
            dw_ref[...] = jnp.zeros_like(dw_ref)
            vec_ref[...] = jnp.zeros_like(vec_ref)

        lng = ln_ref[0:1, :]
        _, xhat, rstd = _ln_fwd(z_ref[...], lng, ln_ref[1:2, :])
        dy = dx_ref[...]
        dz = _ln_bwd(dy, xhat, rstd, lng)
        vec_ref[0:1, :] += _rowsum(dy * xhat)
        vec_ref[1:2, :] += _rowsum(dy)
        vec_ref[2:3, :] += _rowsum(dz * out_ref[...].astype(F32))
        dout = (dz * mod_ref[2:3, :]).astype(BF16)
        dy0 = _dot_nt(dout, w_ref[...])
        dw_ref[...] += _dot_tn(y0_ref[...], dout)
        sg, dsg = _silu_and_grad(g_ref[...].astype(F32))
        dyc_ref[...] = (dy0 * sg).astype(BF16)
        dg_ref[...] = (dy0 * yc_ref[...].astype(F32) * dsg).astype(BF16)
        dxp_ref[...] = ALPHA * dz

        @pl.when(i == T // TMO - 1)
        def _():
            dwb_ref[...] = dw_ref[...].astype(BF16)

    t = _tile(TMO, D)
    return _pallas(
        body, name="ev_out_bwd", grid=(T // TMO,),
        in_specs=[t, t, t, t, t, t, _full((D, D)), _full((3, D)), _full((2, D))],
        out_specs=[t, t, t, _full((D, D)), _full((SUBLANE, D))],
        out_shape=[jax.ShapeDtypeStruct((T, D), F32), jax.ShapeDtypeStruct((T, D), BF16),
                   jax.ShapeDtypeStruct((T, D), BF16), jax.ShapeDtypeStruct((D, D), BF16),
                   jax.ShapeDtypeStruct((SUBLANE, D), F32)],
        scratch_shapes=[pltpu.VMEM((D, D), F32)],
        compiler_params=_params(("arbitrary",)),
    )(dx1, z0, out0, y0, ycat, g0, w_out, mod, lnp)


def _mix0_bwd(q, kvx, lse, dyc, ycat, su, sv, sink_l, bias, a128, gsum, sel, sg_lng, sg_lnb, sg_w, sg_bfull,
              rc, rs1, rs2, comm=None):
    T = q.shape[0]
    nb = T // BLK

    def body(q_ref, kp_ref, kc_ref, kn_ref, lse_ref, dyc_ref, yc_ref, su_ref, sv_ref, sink_ref, bias_ref, a_ref,
             gsum_ref, sel_ref, lng_ref, lnb_ref, w_ref, bfull_ref, c_ref, s1_ref, s2_ref,
             dq_ref, dkv_ref, dsu_ref, dsv_ref, dw_ref, dbt_ref, vec_ref, dsink_ref):
        n = pl.program_id(0)

        @pl.when(n == 0)
        def _():
            dkv_ref[...] = jnp.zeros_like(dkv_ref)
            dw_ref[...] = jnp.zeros_like(dw_ref)
            dbt_ref[...] = jnp.zeros_like(dbt_ref)
            vec_ref[...] = jnp.zeros_like(vec_ref)
            dsink_ref[...] = jnp.zeros_like(dsink_ref)

        kvx4 = jnp.concatenate([kp_ref[...], kc_ref[...], kn_ref[...]], axis=0)
        for s in range(2):
            _mix0_bwd_block(s, 2 * n + s, nb, kvx4[s * BLK:s * BLK + 3 * BLK], q_ref, lse_ref, dyc_ref, yc_ref, su_ref,
                            sv_ref, sink_ref, bias_ref, a_ref, gsum_ref, sel_ref, lng_ref, lnb_ref, w_ref, bfull_ref,
                            c_ref, s1_ref, s2_ref, dq_ref, dkv_ref, dsu_ref, dsv_ref, dw_ref, dbt_ref, vec_ref,
                            dsink_ref)

    def _mix0_bwd_block(s, b, nb, kvx, q_ref, lse_ref, dyc_ref, yc_ref, su_ref, sv_ref, sink_ref, bias_ref, a_ref,
                        gsum_ref, sel_ref, lng_ref, lnb_ref, w_ref, bfull_ref, c_ref, s1_ref, s2_ref,
                        dq_ref, dkv_ref, dsu_ref, dsv_ref, dw_ref, dbt_ref, vec_ref, dsink_ref):
        rows = slice(s * BLK, (s + 1) * BLK)

        def tile(ref, t):
            return ref[rows, t * LANE:(t + 1) * LANE]

        band = pl.ds(pl.multiple_of(b * BLK + (TM - BLK), BLK), 3 * BLK)
        bias = _band_bias(bias_ref, b, nb)
        bias2 = jnp.concatenate([bias, bias], axis=1)
        low = lax.broadcasted_iota(jnp.int32, (BLK, LANE), 1) < HEAD_DIM
        low2 = lax.broadcasted_iota(jnp.int32, (2 * BLK, LANE), 1) < HEAD_DIM
        sel = sel_ref[...]
        c, s1, s2 = c_ref[rows, :], s1_ref[rows, :], s2_ref[rows, :]
        for kvh in range(2):
            t0, t1 = 2 * kvh, 2 * kvh + 1
            q2 = jnp.concatenate([tile(q_ref, t0), tile(q_ref, t1)], axis=0)
            do2 = jnp.concatenate([tile(dyc_ref, t0), tile(dyc_ref, t1)], axis=0)
            yc2 = jnp.concatenate([tile(yc_ref, t0), tile(yc_ref, t1)], axis=0)
            p_hi, p_lo = _split_bf16(do2.astype(F32) * yc2.astype(F32))
            deltas = _dot_nt(sel, p_hi) + _dot_nt(sel, p_lo)
            dkx = jnp.zeros((3 * BLK, LANE), F32)
            dvx = jnp.zeros((3 * BLK, LANE), F32)
            dq_acc = None
            for par in range(2):
                heads = (4 * kvh + par, 4 * kvh + 2 + par)
                kt = 2 * kvh + par
                ke = kvx[:, kt * LANE:(kt + 1) * LANE]
                ve = kvx[:, (4 + kt) * LANE:(5 + kt) * LANE]
                lse = jnp.concatenate([lse_ref[s, :, h * LANE:(h + 1) * LANE] for h in heads], axis=1)
                sk = jnp.concatenate([_lane_tile(sink_ref, h) for h in heads], axis=1)
                delta = deltas[par:par + 1, :]
                pt = jnp.exp(_dot_nt(ke, q2) + bias2 - lse)
                dst = (pt * (_dot_nt(ve, do2) - delta)).astype(BF16)
                sink_terms = jnp.exp(sk - lse) * delta
                for k, h in enumerate(heads):
                    dsink_ref[:, h * LANE:(h + 1) * LANE] += sink_terms[:, k * LANE:(k + 1) * LANE]
                part = _dot_tn(dst, ke)
                dq_acc = part if dq_acc is None else dq_acc + part
                mine = low2 if par == 0 else jnp.logical_not(low2)
                dkx = dkx + jnp.dot(dst, jnp.where(mine, q2, jnp.zeros_like(q2)), preferred_element_type=F32)
                dvx = dvx + jnp.dot(pt.astype(BF16), jnp.where(mine, do2, jnp.zeros_like(do2)),
                                    preferred_element_type=F32)
            for k, t in enumerate((t0, t1)):
                dq_t = dq_acc[k * BLK:(k + 1) * BLK] * (HEAD_DIM ** -0.5)
                dq_ref[rows, t * LANE:(t + 1) * LANE] = _rope_bwd(dq_t, c, s1, s2).astype(BF16)
            dkv_ref[band, kvh * LANE:(kvh + 1) * LANE] += dkx
            dkv_ref[band, (2 + kvh) * LANE:(3 + kvh) * LANE] += dvx

        lng = lng_ref[...]
        xhat, rstd, vb, svm = _sg_core(sv_ref.at[rows, :], lng, lnb_ref[...], a_ref, w_ref, bfull_ref)
        dy = dyc_ref[rows, ATTN_W:].astype(F32)
        dsu_ref[rows, :] = (dy * svm).astype(BF16)
        dsvm = dy * su_ref[rows, :].astype(F32)
        d_hi, d_lo = _split_bf16(dsvm)
        gsum = gsum_ref[...]
        dbt_ref[...] += jnp.dot(d_hi, gsum, preferred_element_type=F32) + jnp.dot(d_lo, gsum,
                                                                                 preferred_element_type=F32)
        tiles = []
        for t in range(SG_W // LANE):
            tl = slice(t * LANE, (t + 1) * LANE)
            dt, v2 = d_hi[:, tl], vb[:, tl]
            dw_ref[2 * t] += _dot_nt(jnp.where(low, dt, jnp.zeros_like(dt)), v2)
            dw_ref[2 * t + 1] += _dot_nt(jnp.where(low, jnp.zeros_like(dt), dt), v2)
            tiles.append(jnp.where(low, _dot_tn(w_ref[2 * t], dt), _dot_tn(w_ref[2 * t + 1], dt)))
        dvgn = jnp.concatenate(tiles, axis=-1)
        vec_ref[0:1, :] += _rowsum(dvgn * xhat)
        vec_ref[1:2, :] += _rowsum(dvgn)
        dxh = dvgn * lng
        m1 = _group_mean(dxh, a_ref)
        m2 = _group_mean(dxh * xhat, a_ref)
        dsv_ref[rows, :] = (rstd * (dxh - m1 - xhat * m2)).astype(BF16)

    two = 2 * BLK
    return _fused_call(
        body, comm, (q, kvx, kvx, kvx, lse, dyc, ycat, su, sv, sink_l, bias, a128, gsum, sel, sg_lng, sg_lnb, sg_w,
                     sg_bfull, rc, rs1, rs2),
        name="mix0_bwd", grid=(nb // 2,),
        in_specs=[_tile(two, ATTN_W)] + _band_specs(KVX_W, nb, 2) + [
            pl.BlockSpec((2, 1, N_HEADS * LANE), lambda n: (n, 0, 0)), _tile(two, D), _tile(two, D),
            _tile(two, SG_W), _tile(two, SG_W), _full((1, N_HEADS * LANE)), _full((3 * BLK, LANE)),
            _full((2 * LANE, 2 * LANE)),_full((SG_W, LANE)), _full((SUBLANE, LANE)), _full((1, SG_W)), _full((1, SG_W)),
            _full((SG_GROUPS, BLK, BLK)), _full((BLK, SG_W)), _tile(two, LANE), _tile(two, LANE), _tile(two, LANE)],
        out_specs=[_tile(two, ATTN_W), _full((T + 2 * TM, 4 * LANE)), _tile(two, SG_W), _tile(two, SG_W),
                   _full((SG_GROUPS, BLK, BLK)), _full((BLK, LANE)), _full((SUBLANE, SG_W)),
                   _full((1, N_HEADS * LANE))],
        out_shape=[jax.ShapeDtypeStruct((T, ATTN_W), BF16), jax.ShapeDtypeStruct((T + 2 * TM, 4 * LANE), F32),
                   jax.ShapeDtypeStruct((T, SG_W), BF16), jax.ShapeDtypeStruct((T, SG_W), BF16),
                   jax.ShapeDtypeStruct((SG_GROUPS, BLK, BLK), F32), jax.ShapeDtypeStruct((BLK, LANE), F32),
                   jax.ShapeDtypeStruct((SUBLANE, SG_W), F32), jax.ShapeDtypeStruct((1, N_HEADS * LANE), F32)])


def _ev_in_bwd(dq, dkv, dsu, dsv, dg0, x, dxp, mod, w_in, rc, rs1, rs2, comm=None):
    T = x.shape[0]

    def body(dq_ref, dkv_ref, dsu_ref, dsv_ref, dg_ref, x_ref, dxp_ref, mod_ref, w_ref, c_ref, s1_ref, s2_ref,
             dx_ref, dwb_ref, vec_ref, dw_ref):
        i = pl.program_id(0)

        @pl.when(i == 0)
        def _():
            dw_ref[...] = jnp.zeros_like(dw_ref)
            vec_ref[...] = jnp.zeros_like(vec_ref)

        low = lax.broadcasted_iota(jnp.int32, (TM, LANE), 1) < HEAD_DIM

        def fold(j):
            t0 = dkv_ref[:, (2 * j) * LANE:(2 * j + 1) * LANE]
            t1 = dkv_ref[:, (2 * j + 1) * LANE:(2 * j + 2) * LANE]
            return jnp.where(low, t0 + pltpu.roll(t0, HEAD_DIM, 1), t1 + pltpu.roll(t1, HEAD_DIM, 1))

        dk = _rope_bwd(fold(0), c_ref[...], s1_ref[...], s2_ref[...]).astype(BF16)
        dp = jnp.concatenate([dq_ref[...], dk, fold(1).astype(BF16), dsu_ref[...], dsv_ref[...],
                              dg_ref[...]], axis=-1)
        xv = x_ref[...]
        scale0 = 1.0 + mod_ref[1:2, :]
        h0 = (xv * scale0 + mod_ref[0:1, :]).astype(BF16)
        dh0 = _dot(dp, w_ref[...])
        dw_ref[...] += _dot_tn(dp, h0)
        dx_ref[...] = dxp_ref[...] + dh0 * scale0
        vec_ref[0:1, :] += _rowsum(dh0)
        vec_ref[1:2, :] += _rowsum(dh0 * xv)

        @pl.when(i == T // TM - 1)
        def _():
            dwb_ref[...] = dw_ref[...].astype(BF16)

    t = _tile(TM, D)
    return _fused_call(
        body, comm, (dq, dkv, dsu, dsv, dg0, x, dxp, mod, w_in, rc, rs1, rs2), name="ev_in_bwd", grid=(T // TM,),
        in_specs=[_tile(TM, ATTN_W), pl.BlockSpec((TM, 4 * LANE), lambda i: (i + 1, 0)), _tile(TM, SG_W),
                  _tile(TM, SG_W), t, t, t,
                  _full((3, D)), _full((EV_IN, D)), _tile(TM, LANE), _tile(TM, LANE), _tile(TM, LANE)],
        out_specs=[t, _full((EV_IN, D)), _full((SUBLANE, D))],
        out_shape=[jax.ShapeDtypeStruct((T, D), F32), jax.ShapeDtypeStruct((EV_IN, D), BF16),
                   jax.ShapeDtypeStruct((SUBLANE, D), F32)],
        scratch_shapes=[pltpu.VMEM((EV_IN, D), F32)])


def _sum_slots(land_ref):
    g = land_ref[0].astype(F32)
    for i in range(1, land_ref.shape[0]):
        g = g + land_ref[i].astype(F32)
    return g


def _reduce_adam(items, name, after=()):
    R, C = items[0][1].shape
    rb = R
    if R > 512:
        for cand in (512, 256, 128, 64, 32, 16, 8):
            if R % cand == 0:
                rb = cand
                break
    n = len(items)

    def body(*refs):
        for k in range(n):
            l_ref, w_ref, m_ref, v_ref = refs[4 * k:4 * k + 4]
            first_out = 4 * n + len(after)
            g_ref, d_ref, nm_ref, nv_ref = refs[first_out + 4 * k:first_out + 4 * k + 4]
            g = _sum_slots(l_ref)
            g_ref[...] = g
            dlt, m2, v2 = _adam(w_ref[...], g, m_ref[...], v_ref[...])
            d_ref[...] = dlt
            nm_ref[...] = m2
            nv_ref[...] = v2

    t = pl.BlockSpec((rb, C), lambda i: (i, 0))
    shp = jax.ShapeDtypeStruct((R, C), F32)
    in_specs, operands = [], []
    for land, w, m, v in items:
        in_specs += [pl.BlockSpec((land.shape[0], rb, C), lambda i: (0, i, 0)), t, t, t]
        operands += [land, w, m, v]
    res = _pallas(
        body, name=name, grid=(R // rb,),
        in_specs=in_specs + [pl.BlockSpec(memory_space=pl.ANY)] * len(after),
        out_specs=[t] * (4 * n), out_shape=[shp] * (4 * n),
        compiler_params=_params(("parallel",)),
    )(*operands, *after)
    return [list(res[4 * k:4 * k + 4]) for k in range(n)]


def _tail_stage1(slabs, small):
    _, R, C = slabs.shape
    n_chips = N_DEV // 2
    gather = _GatherComm(small)
    ns = gather.n

    def body(*refs):
        slab_ref = refs[0]
        g_ins = refs[1:1 + ns]
        part, land_ref = refs[1 + ns], refs[2 + ns]
        g_outs = refs[3 + ns:3 + 2 * ns]
        stage, s1_send, s1_recv = refs[3 + 2 * ns:6 + 2 * ns]
        g_sems = refs[6 + 2 * ns:]
        x, y, c = _my_pos()
        chip = 2 * x + y
        gather.start(g_ins, g_outs, g_sems)
        swaps = [pltpu.make_async_remote_copy(
            src_ref=slab_ref.at[2 * k + (1 - c)], dst_ref=stage.at[k], send_sem=s1_send.at[k],
            recv_sem=s1_recv.at[k], device_id=(x, y, 1 - c), device_id_type=MESH) for k in range(n_chips)]
        for cp in swaps:
            cp.start()
        for cp in swaps:
            cp.wait()
        for k in range(n_chips):
            part[k] = (slab_ref[2 * k + c].astype(F32) + stage[k].astype(F32)).astype(BF16)
        land_ref[chip] = part[chip]
        gather.mid(g_ins, g_outs, g_sems)
        gather.finish(g_ins, g_outs, g_sems)

    any_spec = pl.BlockSpec(memory_space=pl.ANY)
    vmem_spec = pl.BlockSpec(memory_space=pltpu.VMEM)
    slab4 = jax.ShapeDtypeStruct((n_chips, R, C), BF16)
    res = _pallas(
        body, name="tail_stage1",
        out_shape=[slab4, slab4] + gather.out_shapes(),
        in_specs=[vmem_spec] + [any_spec] * ns, out_specs=[vmem_spec, vmem_spec] + [any_spec] * ns,
        scratch_shapes=[pltpu.VMEM((n_chips, R, C), BF16),
                        pltpu.SemaphoreType.DMA((n_chips,)), pltpu.SemaphoreType.DMA((n_chips,))] + gather.sems(),
        compiler_params=pltpu.CompilerParams(vmem_limit_bytes=VMEM_LIMIT),
    )(slabs, *gather.arrs)
    return res[0], res[1], list(res[2:])


def _chip_copies(part_ref, land_ref, send_sems, recv_sems):
    x, y, c = _my_pos()
    chip = 2 * x + y
    copies = []
    for r in range(1, N_DEV // 2):
        px = (1 - x) if (r & 2) else x
        py = (1 - y) if (r & 1) else y
        copies.append(pltpu.make_async_remote_copy(
            src_ref=part_ref.at[2 * px + py], dst_ref=land_ref.at[chip], send_sem=send_sems[r - 1],
            recv_sem=recv_sems[r - 1], device_id=(px, py, c), device_id_type=MESH))
    return copies


def _tail_send(part, land):
    n = N_DEV // 2 - 1

    def body(part_ref, land_ref, *outs):
        send_sems, recv_sems = outs[:n], outs[n:2 * n]
        token = outs[2 * n + 2]
        for cp in _chip_copies(part_ref, land_ref, send_sems, recv_sems):
            cp.start()
        token[...] = jnp.zeros_like(token)

    hbm = pl.BlockSpec(memory_space=pltpu.HBM)
    sem = pl.BlockSpec(memory_space=pltpu.SEMAPHORE)
    res = _pallas(
        body, name="tail_send",
        out_shape=tuple([pltpu.SemaphoreType.DMA(())] * (2 * n)
                        + [pltpu.HBM(part.shape, part.dtype), pltpu.HBM(land.shape, land.dtype),
                           jax.ShapeDtypeStruct((SUBLANE, LANE), F32)]),
        in_specs=(hbm, hbm), out_specs=tuple([sem] * (2 * n) + [hbm, hbm, pl.BlockSpec(memory_space=pltpu.VMEM)]),
        input_output_aliases={0: 2 * n, 1: 2 * n + 1},
        compiler_params=pltpu.CompilerParams(has_side_effects=pltpu.SideEffectType.DATAFLOW_SIDE_EFFECTING),
    )(pltpu.with_memory_space_constraint(part, pltpu.HBM), pltpu.with_memory_space_constraint(land, pltpu.HBM))
    return list(res[:n]), list(res[n:2 * n]), res[2 * n], res[2 * n + 1], res[2 * n + 2]


def _tail_wait(send_sems, recv_sems, part, land, after):
    n = len(send_sems)

    def body(part_ref, land_ref, *rest):
        ss, rs = rest[:n], rest[n:2 * n]
        for cp in _chip_copies(part_ref, land_ref, ss, rs):
            cp.wait_send()
            cp.wait_recv()

    hbm = pl.BlockSpec(memory_space=pltpu.HBM)
    sem = pl.BlockSpec(memory_space=pltpu.SEMAPHORE)
    any_spec = pl.BlockSpec(memory_space=pl.ANY)
    res = _pallas(
        body, name="tail_wait",
        out_shape=(pltpu.HBM(part.shape, part.dtype), pltpu.HBM(land.shape, land.dtype)),
        in_specs=tuple([hbm, hbm] + [sem] * (2 * n) + [any_spec] * len(after)), out_specs=(hbm, hbm),
        input_output_aliases={0: 0, 1: 1},
        compiler_params=pltpu.CompilerParams(has_side_effects=pltpu.SideEffectType.DATAFLOW_SIDE_EFFECTING),
    )(part, land, *send_sems, *recv_sems, *after)
    return res[1]


def _slots_adam(items, name, after=()):
    zeros3 = (0, 0, 0)
    in_specs, out_specs, out_shape, operands = [], [], [], []
    for land, w, m, v in items:
        inner = w.shape[-3:]
        if w.ndim == 5:
            lspec = pl.BlockSpec((N_DEV, 1) + inner, lambda i: (0, i) + zeros3)
            wspec = pl.BlockSpec((1, 1) + inner, lambda i: (0, i) + zeros3)
        else:
            lspec = pl.BlockSpec((N_DEV,) + inner, lambda i: (0,) + zeros3)
            wspec = pl.BlockSpec((1,) + inner, lambda i: (0,) + zeros3)
        in_specs += [lspec, wspec, wspec, wspec]
        out_specs += [wspec] * 4
        out_shape += [jax.ShapeDtypeStruct(w.shape, F32)] * 4
        operands += [land, w, m, v]
    n = len(items)

    def body(*refs):
        for k, (_, w, _, _) in enumerate(items):
            l_ref, w_ref, m_ref, v_ref = refs[4 * k:4 * k + 4]
            first_out = 4 * n + len(after)
            outs = refs[first_out + 4 * k:first_out + 4 * k + 4]
            at = (0, 0) if w.ndim == 5 else (0,)

            def update(l_ref=l_ref, w_ref=w_ref, m_ref=m_ref, v_ref=v_ref, outs=outs, at=at):
                g = l_ref[(0,) + at[1:]].astype(F32)
                for i in range(1, N_DEV):
                    g = g + l_ref[(i,) + at[1:]].astype(F32)
                dlt, m2, v2 = _adam(w_ref[at], g, m_ref[at], v_ref[at])
                for o_ref, val in zip(outs, (g, dlt, m2, v2)):
                    o_ref[at] = val

            if w.ndim == 5:
                update()
            else:
                pl.when(pl.program_id(0) == 0)(update)

    res = _pallas(
        body, name=name, grid=(2,),
        in_specs=in_specs + [pl.BlockSpec(memory_space=pl.ANY)] * len(after),
        out_specs=out_specs, out_shape=out_shape,
        compiler_params=_params(("arbitrary",)),
    )(*operands, *after)
    return [list(res[4 * k:4 * k + 4]) for k in range(n)]


SMALL_PARAMS = ("ln_g", "ln_b", "ev_sg_ln_g", "ev_sg_ln_b", "ev_sink", "ev_sg_b",
                "od_conv_w", "od_conv_b", "od_b_a", "od_b_x", "od_lam")


def _small_update(ga, gc, gd, gf, gb, ge, gsink, gbt, params):
    names = list(SMALL_PARAMS)
    flat = [a for nm in names for a in params[nm]]
    n_g = 8

    def body(*refs):
        ga_ref, gc_ref, gd_ref, gf_ref, gb_ref, ge_ref, gs_ref, gbt_ref = refs[:n_g]
        prm = refs[n_g:n_g + 3 * len(names)]
        loss_ref = refs[n_g + 3 * len(names)]
        outs = refs[n_g + 3 * len(names) + 1:]

        def ssum(ref):
            acc = ref[0]
            for i in range(1, N_DEV):
                acc = acc + ref[i]
            return acc

        a, cc, dd, ff, bb, ee = ssum(ga_ref), ssum(gc_ref), ssum(gd_ref), ssum(gf_ref), ssum(gb_ref), ssum(ge_ref)
        loss_ref[...] = a[3:4, 0:LANE]
        me = _slot(*_my_pos())

        def mine(rows):
            acc = jnp.zeros((rows.shape[0], LANE), F32)
            for j in range(N_DEV):
                acc = acc + jnp.where(me == j, rows[:, j * LANE:(j + 1) * LANE], 0.0)
            return acc

        sink_terms = ssum(gs_ref)
        lane8 = lax.broadcasted_iota(jnp.int32, (1, N_HEADS), 1)
        g_sink = jnp.zeros((1, N_HEADS), F32)
        for h in range(N_HEADS):
            tot = -jnp.sum(sink_terms[:, h * LANE:(h + 1) * LANE], axis=1, keepdims=True)
            g_sink = jnp.where(lane8 == h, tot, g_sink)
        grads = dict(
            ln_g=jnp.concatenate([dd[0:1], a[0:1]], axis=0), ln_b=jnp.concatenate([dd[1:2], a[1:2]], axis=0),
            ev_sg_ln_g=ee[0:1], ev_sg_ln_b=ee[1:2], ev_sink=g_sink,
            ev_sg_b=jnp.transpose(ssum(gbt_ref))[0:SG_GROUPS, :],
            od_conv_w=mine(cc[0:4]), od_conv_b=mine(cc[4:5]),
            od_b_a=mine(jnp.concatenate([ff[0:1], bb[0:1]], axis=0)),
            od_b_x=mine(jnp.concatenate([ff[1:2], bb[1:2]], axis=0)),
            od_lam=mine(jnp.concatenate([ff[2:3], bb[2:3]], axis=0)))
        for k, nm in enumerate(names):
            w_ref, m_ref, v_ref = prm[3 * k:3 * k + 3]
            at = (0,) if len(w_ref.shape) == 3 else ()
            g = grads[nm]
            dlt, m2, v2 = _adam(w_ref[at] if at else w_ref[...], g, m_ref[at] if at else m_ref[...],
                                v_ref[at] if at else v_ref[...])
            for o_ref, val in zip(outs[4 * k:4 * k + 4], (g, dlt, m2, v2)):
                if at:
                    o_ref[at] = val
                else:
                    o_ref[...] = val

    gathered = [ga, gc, gd, gf, gb, ge, gsink, gbt]
    out_shape = [jax.ShapeDtypeStruct((1, LANE), F32)]
    for nm in names:
        out_shape += [jax.ShapeDtypeStruct(params[nm][0].shape, F32)] * 4
    return _pallas(
        body, name="small_update", grid=(1,),
        in_specs=[_full(a.shape) for a in gathered + flat],
        out_specs=[_full(s.shape) for s in out_shape], out_shape=out_shape,
        compiler_params=_params(("arbitrary",)),
    )(*gathered, *flat)


VEC_ROWS = 16
VEC_LAYOUT = (("od_conv_w", 4), ("od_conv_b", 1), ("od_b_a", 2), ("od_b_x", 2), ("od_lam", 2))


def _from_slabs(slabs):
    n, R, cp = slabs.shape
    return slabs.transpose(1, 0, 2).reshape(R, n * cp)


def kernel(x, c, positions, ada_w, ada_b, ln_g, ln_b, ev_w_in, ev_w_out, ev_sink, ev_sg_ln_g, ev_sg_ln_b, ev_sg_w, ev_sg_b, od_w_in, od_conv_w, od_conv_b, od_w_a, od_b_a, od_w_x, od_b_x, od_lam, od_w_out, loss_target, m_ada_w, m_ada_b, m_ln_g, m_ln_b, m_ev_w_in, m_ev_w_out, m_ev_sink, m_ev_sg_ln_g, m_ev_sg_ln_b, m_ev_sg_w, m_ev_sg_b, m_od_w_in, m_od_conv_w, m_od_conv_b, m_od_w_a, m_od_b_a, m_od_w_x, m_od_b_x, m_od_lam, m_od_w_out, v_ada_w, v_ada_b, v_ln_g, v_ln_b, v_ev_w_in, v_ev_w_out, v_ev_sink, v_ev_sg_ln_g, v_ev_sg_ln_b, v_ev_sg_w, v_ev_sg_b, v_od_w_in, v_od_conv_w, v_od_conv_b, v_od_w_a, v_od_b_a, v_od_w_x, v_od_b_x, v_od_lam, v_od_w_out):
    T = x.shape[1]
    me = _slot(*_my_pos())
    xs = x.reshape(T, D)
    tgt = loss_target.reshape(T, D)

    c_all, mod_all, g_vec, (g_ev_in,), (s_ev_out, s_od_in, s_od_out, sg_w, wa, wx) = _head_gather(
        c, ada_w, [ev_w_in[0].T.astype(BF16)],
        [ev_w_out[0], od_w_in[0], od_w_out[0], ev_sg_w[0], od_w_a[0], od_w_x[0]],
        [od_conv_w, od_conv_b, od_b_a, od_b_x, od_lam])
    c_all = c_all.reshape(N_DEV, D)
    w_ev_in = g_ev_in.reshape(EV_IN, D)
    vec_full = _from_slabs(g_vec)
    cw, cb = vec_full[0:4], vec_full[4:5]
    ba, bx, lam = vec_full[5:7], vec_full[7:9], vec_full[9:11]
    mod_mine = lax.dynamic_index_in_dim(mod_all, me, axis=2, keepdims=False)
    mod = mod_mine.transpose(1, 0, 2).reshape(2, 3 * D) + ada_b
    mod0 = mod[0].reshape(3, D)
    mod1 = mod[1].reshape(3, D)

    half = 8
    inv_freq = jnp.power(jnp.float32(ROPE_THETA), -jnp.arange(half, dtype=F32) / half)
    ang = positions.reshape(T).astype(F32)[:, None] * inv_freq
    cos_t = jnp.tile(jnp.cos(ang), (1, LANE // half))
    sin_t = jnp.tile(jnp.sin(ang), (1, LANE // half))
    l64 = jnp.arange(LANE) % HEAD_DIM
    rc = jnp.where(l64 < 2 * half, cos_t, 1.0)
    rs1 = jnp.where(l64 < half, -sin_t, 0.0)
    rs2 = jnp.where((l64 >= half) & (l64 < 2 * half), sin_t, 0.0)

    ln0 = jnp.stack([ln_g[0], ln_b[0]])
    ln1 = jnp.stack([ln_g[1], ln_b[1]])
    sg_lng = ev_sg_ln_g
    sg_lnb = ev_sg_ln_b
    sg_bfull = jnp.repeat(ev_sg_b[0].T, SG_DIM, axis=1)
    sink_l = jnp.repeat(ev_sink, LANE, axis=1)
    kj = jnp.arange(3 * BLK)[:, None]
    qi = jnp.arange(BLK)[None, :]
    band_bias = jnp.where(jnp.abs(kj - BLK - qi) <= BLK, 0.0, NEG_INF).astype(F32)
    lanes = jnp.arange(LANE)
    lanes2 = jnp.arange(2 * LANE)
    a128 = jnp.where(lanes2[:, None] // SG_DIM == lanes2[None, :] // SG_DIM, 1.0 / SG_DIM, 0.0).astype(BF16)
    gsum = (jnp.arange(SG_W)[:, None] // SG_DIM == lanes[None, :]).astype(BF16)
    sel = (jnp.arange(SUBLANE)[:, None] == lanes[None, :] // HEAD_DIM).astype(BF16)

    (q, kvx, su, sv, g0), _ = _ev_in(xs, mod0, w_ev_in, rc, rs1, rs2)
    (ycat, y0, lse), (g_ev_out, g_od_in, g_od_out) = _mix0_fwd(
        q, kvx, su, sv, g0, sink_l, band_bias, a128, sg_lng, sg_lnb, sg_w, sg_bfull,
        _GatherComm([s_ev_out, s_od_in, s_od_out], mid_frac=0.75))
    w_ev_out = g_ev_out.reshape(D, D)
    w_od_in = _from_slabs(g_od_in)
    w_od_out = g_od_out.reshape(D, D)
    out0, z0, x1 = _ev_out(y0, w_ev_out, xs, mod0, ln0)
    xr, g1 = _od_in(x1, mod1, w_od_in)
    fwd_f = _rglru_fwd(xr, cw, cb, wa[0], wx[0], ba[0:1], bx[0:1], lam[0:1], False, "rglru_fwd_f")
    fwd_b = _rglru_fwd(xr, cw, cb, wa[1], wx[1], ba[1:2], bx[1:2], lam[1:2], True, "rglru_fwd_b")
    dh, dg1, dx1p, d_od_out, vec_a = _od_out(fwd_f[0], fwd_b[0], g1, w_od_out, x1, tgt, mod1, ln1)

    (dxcf, dwa_f, dwx_f, vec_f), (l_od_out,) = _rglru_bwd(
        fwd_f, dh, wa[0], wx[0], lam[0:1], False, "rglru_bwd_f",
        _ExchangeComm([d_od_out.reshape(N_DEV, D // N_DEV, D)]))
    (dxcb, dwa_b, dwx_b, vec_b), _ = _rglru_bwd(fwd_b, dh, wa[1], wx[1], lam[1:2], True, "rglru_bwd_b")
    (dx1, d_od_in, vec_c), (a_wa, a_wx) = _od_in_bwd(
        dxcf, dxcb, xr, dg1, x1, dx1p, mod1, w_od_in, cw,
        _GatherComm([jnp.stack([dwa_f, dwa_b]).astype(BF16), jnp.stack([dwx_f, dwx_b]).astype(BF16)],
                    mid_frac=0.75))
    dxp, dyc, dg0, d_ev_out, vec_d = _ev_out_bwd(dx1, z0, out0, y0, ycat, g0, w_ev_out, mod0, ln0)
    (dq, dkv, dsu, dsv, d_sg_w, d_sg_bt, vec_e, d_sink_l), (l_od_in, l_ev_out, ga, gc, gd, gf, gb) = _mix0_bwd(
        q, kvx, lse, dyc, ycat, su, sv, sink_l, band_bias, a128, gsum, sel, sg_lng, sg_lnb, sg_w, sg_bfull,
        rc, rs1, rs2, _BothComm(_ExchangeComm([d_od_in, d_ev_out.reshape(N_DEV, D // N_DEV, D)]),
                                _GatherComm([vec_a, vec_c, vec_d, vec_f, vec_b], mid_frac=0.9)))
    (grad_x, d_ev_in, vec_g), (ge, gsink, gbt, a_sgw) = _ev_in_bwd(
        dq, dkv, dsu, dsv, dg0, xs, dxp, mod0, w_ev_in, rc, rs1, rs2,
        _GatherComm([vec_e, d_sink_l, d_sg_bt, d_sg_w.astype(BF16)], mid_frac=0.75))

    part, land, (gg,) = _tail_stage1(d_ev_in.reshape(N_DEV, EV_IN // N_DEV, D), [vec_g])
    send_sems, recv_sems, part, land, token = _tail_send(part, land)

    dmod_all = jnp.stack([jnp.concatenate([gg[:, 0], gg[:, 1], gd[:, 2]], axis=-1),
                          jnp.concatenate([gc[:, 5], gc[:, 6], ga[:, 2]], axis=-1)], axis=1)
    cols = ada_w.shape[2]
    dmod_cols = lax.dynamic_slice_in_dim(dmod_all, me * cols, cols, axis=2).transpose(1, 0, 2)
    (g_ada_w, d_ada_w, nm_ada_w, nv_ada_w, g_ada_b, d_ada_b, nm_ada_b, nv_ada_b) = _ada_update(
        c_all, dmod_cols, dmod_all, ada_w, m_ada_w, v_ada_w, ada_b, m_ada_b, v_ada_b)

    res = dict(ada_w=[g_ada_w, d_ada_w, nm_ada_w, nv_ada_w], ada_b=[g_ada_b, d_ada_b, nm_ada_b, nv_ada_b])
    (r_od_in,) = _reduce_adam([(l_od_in, od_w_in[0], m_od_w_in[0], v_od_w_in[0])], "adam_od_w_in", after=[token])
    r_ev_out, r_od_out = _reduce_adam([(l_ev_out, ev_w_out[0], m_ev_w_out[0], v_ev_w_out[0]),
                                       (l_od_out, od_w_out[0], m_od_w_out[0], v_od_w_out[0])], "adam_w_out",
                                      after=[token])
    for name, r in (("od_w_in", r_od_in), ("ev_w_out", r_ev_out), ("od_w_out", r_od_out)):
        res[name] = [a[None] for a in r]
    res["od_w_a"], res["od_w_x"], res["ev_sg_w"] = _slots_adam(
        [(a_wa, od_w_a, m_od_w_a, v_od_w_a), (a_wx, od_w_x, m_od_w_x, v_od_w_x),
         (a_sgw, ev_sg_w, m_ev_sg_w, v_ev_sg_w)], "adam_gates", after=[token])
    small = dict(ln_g=(ln_g, m_ln_g, v_ln_g), ln_b=(ln_b, m_ln_b, v_ln_b),
                 ev_sg_ln_g=(ev_sg_ln_g, m_ev_sg_ln_g, v_ev_sg_ln_g),
                 ev_sg_ln_b=(ev_sg_ln_b, m_ev_sg_ln_b, v_ev_sg_ln_b),
                 ev_sink=(ev_sink, m_ev_sink, v_ev_sink), ev_sg_b=(ev_sg_b, m_ev_sg_b, v_ev_sg_b),
                 od_conv_w=(od_conv_w, m_od_conv_w, v_od_conv_w), od_conv_b=(od_conv_b, m_od_conv_b, v_od_conv_b),
                 od_b_a=(od_b_a, m_od_b_a, v_od_b_a), od_b_x=(od_b_x, m_od_b_x, v_od_b_x),
                 od_lam=(od_lam, m_od_lam, v_od_lam))
    small_out = _small_update(ga, gc, gd, gf, gb, ge, gsink, gbt, small)
    l_ev_in = _tail_wait(send_sems, recv_sems, part, land,
                         [r_od_in[0], r_od_out[0], res["od_w_x"][0], g_ada_w, small_out[0]])
    (r_ev_in,) = _reduce_adam([(l_ev_in, ev_w_in[0].T, m_ev_w_in[0].T, v_ev_w_in[0].T)], "adam_ev_w_in")
    res["ev_w_in"] = [a.T[None] for a in r_ev_in]
    loss = small_out[0][0, 0]
    for k, name in enumerate(SMALL_PARAMS):
        res[name] = small_out[1 + 4 * k:5 + 4 * k]

    order = ["ada_w", "ada_b", "ln_g", "ln_b", "ev_w_in", "ev_w_out", "ev_sink", "ev_sg_ln_g", "ev_sg_ln_b",
             "ev_sg_w", "ev_sg_b", "od_w_in", "od_conv_w", "od_conv_b", "od_w_a", "od_b_a", "od_w_x", "od_b_x",
             "od_lam", "od_w_out"]
    outs = [loss, grad_x.reshape(1, T, D)]
    for kind in range(4):
        outs += [res[name][kind] for name in order]
    return tuple(outs)
```
